```python
import math
import jax, jax.numpy as jnp
from jax import lax
import numpy as np

D_MODEL = 1024
BATCH = 8
SEQ = 4096
DEPTH = 2

N_HEADS = 8
QK_NOPE_DIM = 64
QK_ROPE_DIM = 32
V_HEAD_DIM = 64
Q_LORA_RANK = 384
KV_LORA_RANK = 256
ROPE_THETA = 10000.0
Q_BLOCK = 128
CONV_CHANNELS = 512
CONV_WIDTH = 31
POOL_WINDOWS = (2, 4, 8, 16)
POOL_GROUPS = 4
POOL_CHANNELS = 512
POOL_GROUP_DIM = POOL_CHANNELS // POOL_GROUPS
N_BRANCHES = 3
D_FF = -(-8 * D_MODEL // (3 * 256)) * 256
EPS = 1e-6
IN_WIDTHS = (Q_LORA_RANK, KV_LORA_RANK, QK_ROPE_DIM, 2 * CONV_CHANNELS, POOL_CHANNELS, N_BRANCHES * D_MODEL)
D_IN = sum(IN_WIDTHS)

kernel_name = "hybrid_mla_conformer_pool_gated_block"


def rms_norm(x, g):
    x32 = x.astype(jnp.float32)
    y = x32 * lax.rsqrt(jnp.mean(x32 * x32, axis=-1, keepdims=True) + EPS)
    return (y * g.astype(jnp.float32)).astype(x.dtype)


def layer_norm(x, g, b):
    x32 = x.astype(jnp.float32)
    mu = jnp.mean(x32, axis=-1, keepdims=True)
    xc = x32 - mu
    y = xc * lax.rsqrt(jnp.mean(xc * xc, axis=-1, keepdims=True) + EPS)
    return (y * g.astype(jnp.float32) + b.astype(jnp.float32)).astype(x.dtype)


def rope_tables(positions):
    inv_freq = ROPE_THETA ** (-jnp.arange(0, QK_ROPE_DIM, 2, dtype=jnp.float32) / QK_ROPE_DIM)
    ang = positions.astype(jnp.float32)[..., None] * inv_freq
    return jnp.cos(ang), jnp.sin(ang)


def apply_rope(x, cos, sin):
    x32 = x.astype(jnp.float32)
    half = x32.shape[-1] // 2
    x1, x2 = x32[..., :half], x32[..., half:]
    out = jnp.concatenate([x1 * cos - x2 * sin, x2 * cos + x1 * sin], axis=-1)
    return out.astype(x.dtype)


def mla_branch(c_q, c_kv, k_rope_raw, cos, sin, q_norm, w_uq, kv_norm, w_uk, w_uv, w_o):
    B, S, _ = c_q.shape
    q = (rms_norm(c_q, q_norm) @ w_uq).reshape(B, S, N_HEADS, QK_NOPE_DIM + QK_ROPE_DIM)
    q_nope = q[..., :QK_NOPE_DIM]
    q_rope = apply_rope(q[..., QK_NOPE_DIM:], cos[:, :, None, :], sin[:, :, None, :])
    c_kv_n = rms_norm(c_kv, kv_norm)
    k_nope = (c_kv_n @ w_uk).reshape(B, S, N_HEADS, QK_NOPE_DIM)
    v = (c_kv_n @ w_uv).reshape(B, S, N_HEADS, V_HEAD_DIM)
    k_rope = apply_rope(k_rope_raw, cos, sin)
    nb = S // Q_BLOCK
    qn_blocks = q_nope.reshape(B, nb, Q_BLOCK, N_HEADS, QK_NOPE_DIM).transpose(1, 0, 2, 3, 4)
    qr_blocks = q_rope.reshape(B, nb, Q_BLOCK, N_HEADS, QK_ROPE_DIM).transpose(1, 0, 2, 3, 4)
    starts = jnp.arange(nb, dtype=jnp.int32) * Q_BLOCK
    key_idx = jnp.arange(S, dtype=jnp.int32)
    scale = 1.0 / math.sqrt(QK_NOPE_DIM + QK_ROPE_DIM)

    def attend(args):
        qn, qr, start = args
        s = jnp.einsum('bqhd,bkhd->bhqk', qn, k_nope) + jnp.einsum('bqhr,bkr->bhqk', qr, k_rope)
        s = s.astype(jnp.float32) * scale
        q_idx = start + jnp.arange(Q_BLOCK, dtype=jnp.int32)
        causal = key_idx[None, :] <= q_idx[:, None]
        s = jnp.where(causal[None, None], s, -jnp.inf)
        p = jax.nn.softmax(s, axis=-1).astype(v.dtype)
        return jnp.einsum('bhqk,bkhd->bqhd', p, v)

    o = lax.map(attend, (qn_blocks, qr_blocks, starts))
    o = o.transpose(1, 0, 2, 3, 4).reshape(B, S, N_HEADS * V_HEAD_DIM)
    return o @ w_o


def conv_branch(u, conv_w, conv_b, ln_g, ln_b, w_out):
    a, g = jnp.split(u, 2, axis=-1)
    h = a * jax.nn.sigmoid(g)
    h = lax.conv_general_dilated(h, conv_w[:, None, :], window_strides=(1,),
                                 padding=[(CONV_WIDTH - 1, 0)],
                                 dimension_numbers=('NWC', 'WIO', 'NWC'),
                                 feature_group_count=CONV_CHANNELS) + conv_b
    h = jax.nn.silu(layer_norm(h, ln_g, ln_b))
    return h @ w_out


def pool_branch(u, pool_w, pool_scale, w_out):
    B, S, _ = u.shape
    u32 = u.astype(jnp.float32).reshape(B, S, POOL_GROUPS, POOL_GROUP_DIM)
    cs = jnp.cumsum(u32, axis=1)
    t = jnp.arange(S, dtype=jnp.int32)
    outs = []
    for gi, w in enumerate(POOL_WINDOWS):
        c = cs[:, :, gi]
        lag = jnp.pad(c, ((0, 0), (w, 0), (0, 0)))[:, :S]
        count = jnp.minimum(t + 1, w).astype(jnp.float32)[None, :, None]
        outs.append((c - lag) / count)
    pooled = jnp.stack(outs, axis=2)
    d = (pooled - u32).astype(u.dtype)
    m = jnp.einsum('bsgc,gcd->bsgd', d, pool_w).reshape(B, S, POOL_CHANNELS) * pool_scale
    return m @ w_out


def mixer_sublayer(x, cos, sin, norm_pre, w_in, q_norm, w_uq, kv_norm, w_uk, w_uv, w_attn_o,
                   conv_w, conv_b, conv_ln_g, conv_ln_b, w_conv_o, pool_w, pool_scale, w_pool_o,
                   w_mix_o, norm_post):
    B, S, D = x.shape
    h = rms_norm(x, norm_pre)
    z = h @ w_in
    split_at = [int(v) for v in np.cumsum(IN_WIDTHS)[:-1]]
    c_q, c_kv, k_r, u_conv, u_pool, gate_logits = jnp.split(z, split_at, axis=-1)
    y_attn = mla_branch(c_q, c_kv, k_r, cos, sin, q_norm, w_uq, kv_norm, w_uk, w_uv, w_attn_o)
    y_conv = conv_branch(u_conv, conv_w, conv_b, conv_ln_g, conv_ln_b, w_conv_o)
    y_pool = pool_branch(u_pool, pool_w, pool_scale, w_pool_o)
    gates = jax.nn.sigmoid(gate_logits.astype(jnp.float32)).astype(x.dtype).reshape(B, S, N_BRANCHES, D)
    merged = gates[:, :, 0] * y_attn + gates[:, :, 1] * y_conv + gates[:, :, 2] * y_pool
    return rms_norm(merged @ w_mix_o, norm_post)


def ffn_sublayer(x, norm_pre, w_gate, w_up, w_down, norm_post):
    h = rms_norm(x, norm_pre)
    y = (jax.nn.silu(h @ w_gate) * (h @ w_up)) @ w_down
    return rms_norm(y, norm_post)


def _fwd_setup_inputs(seed: int = 0) -> dict:
    key = jax.random.key(seed)
    ks = jax.random.split(key, 32)
    L, D = DEPTH, D_MODEL

    def dense(k, shape, fan_in):
        return jax.random.normal(k, shape, jnp.float32) * fan_in ** -0.5

    def gain(k, shape):
        return 1.0 + 0.02 * jax.random.normal(k, shape, jnp.float32)

    x = jax.random.normal(ks[0], (BATCH, SEQ, D), jnp.float32)
    offsets = jax.random.randint(ks[1], (BATCH, 1), 0, 4096, dtype=jnp.int32)
    positions = offsets + jnp.arange(SEQ, dtype=jnp.int32)[None, :]
    return {
        "x": x,
        "positions": positions,
        "mix_norm_pre": gain(ks[2], (L, D)),
        "w_in": dense(ks[3], (L, D, D_IN), D),
        "q_norm": gain(ks[4], (L, Q_LORA_RANK)),
        "w_uq": dense(ks[5], (L, Q_LORA_RANK, N_HEADS * (QK_NOPE_DIM + QK_ROPE_DIM)), Q_LORA_RANK),
        "kv_norm": gain(ks[6], (L, KV_LORA_RANK)),
        "w_uk": dense(ks[7], (L, KV_LORA_RANK, N_HEADS * QK_NOPE_DIM), KV_LORA_RANK),
        "w_uv": dense(ks[8], (L, KV_LORA_RANK, N_HEADS * V_HEAD_DIM), KV_LORA_RANK),
        "w_attn_o": dense(ks[9], (L, N_HEADS * V_HEAD_DIM, D), N_HEADS * V_HEAD_DIM),
        "conv_w": dense(ks[10], (L, CONV_WIDTH, CONV_CHANNELS), CONV_WIDTH),
        "conv_b": 0.02 * jax.random.normal(ks[11], (L, CONV_CHANNELS), jnp.float32),
        "conv_ln_g": gain(ks[12], (L, CONV_CHANNELS)),
        "conv_ln_b": 0.02 * jax.random.normal(ks[13], (L, CONV_CHANNELS), jnp.float32),
        "w_conv_o": dense(ks[14], (L, CONV_CHANNELS, D), CONV_CHANNELS),
        "pool_w": dense(ks[15], (L, POOL_GROUPS, POOL_GROUP_DIM, POOL_GROUP_DIM), POOL_GROUP_DIM),
        "pool_scale": 1.0 + 0.1 * jax.random.normal(ks[16], (L, POOL_CHANNELS), jnp.float32),
        "w_pool_o": dense(ks[17], (L, POOL_CHANNELS, D), POOL_CHANNELS),
        "w_mix_o": dense(ks[18], (L, D, D), D),
        "mix_norm_post": gain(ks[19], (L, D)),
        "ffn_norm_pre": gain(ks[20], (L, D)),
        "w_gate": dense(ks[21], (L, D, D_FF), D),
        "w_up": dense(ks[22], (L, D, D_FF), D),
        "w_down": dense(ks[23], (L, D_FF, D), D_FF),
        "ffn_norm_post": gain(ks[24], (L, D)),
    }


def _fwd_reference(x, positions, mix_norm_pre, w_in, q_norm, w_uq, kv_norm, w_uk, w_uv, w_attn_o,
              conv_w, conv_b, conv_ln_g, conv_ln_b, w_conv_o, pool_w, pool_scale, w_pool_o,
              w_mix_o, mix_norm_post, ffn_norm_pre, w_gate, w_up, w_down, ffn_norm_post):
    cos, sin = rope_tables(positions)
    h = x
    for l in range(DEPTH):
        h = h + mixer_sublayer(h, cos, sin, mix_norm_pre[l], w_in[l], q_norm[l], w_uq[l], kv_norm[l],
                               w_uk[l], w_uv[l], w_attn_o[l], conv_w[l], conv_b[l], conv_ln_g[l],
                               conv_ln_b[l], w_conv_o[l], pool_w[l], pool_scale[l], w_pool_o[l],
                               w_mix_o[l], mix_norm_post[l])
        h = h + ffn_sublayer(h, ffn_norm_pre[l], w_gate[l], w_up[l], w_down[l], ffn_norm_post[l])
    return h


import jax as _jax
import jax.numpy as _jnp

TWIN_FORMAT = 'train_step'
FWD_PARAMS = ['x', 'positions', 'mix_norm_pre', 'w_in', 'q_norm', 'w_uq', 'kv_norm', 'w_uk', 'w_uv', 'w_attn_o', 'conv_w', 'conv_b', 'conv_ln_g', 'conv_ln_b', 'w_conv_o', 'pool_w', 'pool_scale', 'w_pool_o', 'w_mix_o', 'mix_norm_post', 'ffn_norm_pre', 'w_gate', 'w_up', 'w_down', 'ffn_norm_post']
TWIN_WEIGHTS = ['mix_norm_pre', 'w_in', 'q_norm', 'w_uq', 'kv_norm', 'w_uk', 'w_uv', 'w_attn_o', 'conv_w', 'conv_b', 'conv_ln_g', 'conv_ln_b', 'w_conv_o', 'pool_w', 'pool_scale', 'w_pool_o', 'w_mix_o', 'mix_norm_post', 'ffn_norm_pre', 'w_gate', 'w_up', 'w_down', 'ffn_norm_post']
TWIN_DIFF_INPUT = 'x'
TWIN_INPUTS = ['x', 'positions', 'mix_norm_pre', 'w_in', 'q_norm', 'w_uq', 'kv_norm', 'w_uk', 'w_uv', 'w_attn_o', 'conv_w', 'conv_b', 'conv_ln_g', 'conv_ln_b', 'w_conv_o', 'pool_w', 'pool_scale', 'w_pool_o', 'w_mix_o', 'mix_norm_post', 'ffn_norm_pre', 'w_gate', 'w_up', 'w_down', 'ffn_norm_post', 'loss_target', 'm_mix_norm_pre', 'm_w_in', 'm_q_norm', 'm_w_uq', 'm_kv_norm', 'm_w_uk', 'm_w_uv', 'm_w_attn_o', 'm_conv_w', 'm_conv_b', 'm_conv_ln_g', 'm_conv_ln_b', 'm_w_conv_o', 'm_pool_w', 'm_pool_scale', 'm_w_pool_o', 'm_w_mix_o', 'm_mix_norm_post', 'm_ffn_norm_pre', 'm_w_gate', 'm_w_up', 'm_w_down', 'm_ffn_norm_post', 'v_mix_norm_pre', 'v_w_in', 'v_q_norm', 'v_w_uq', 'v_kv_norm', 'v_w_uk', 'v_w_uv', 'v_w_attn_o', 'v_conv_w', 'v_conv_b', 'v_conv_ln_g', 'v_conv_ln_b', 'v_w_conv_o', 'v_pool_w', 'v_pool_scale', 'v_w_pool_o', 'v_w_mix_o', 'v_mix_norm_post', 'v_ffn_norm_pre', 'v_w_gate', 'v_w_up', 'v_w_down', 'v_ffn_norm_post']
TWIN_OUTPUTS = ['loss', 'grad_x', 'grad_mix_norm_pre', 'grad_w_in', 'grad_q_norm', 'grad_w_uq', 'grad_kv_norm', 'grad_w_uk', 'grad_w_uv', 'grad_w_attn_o', 'grad_conv_w', 'grad_conv_b', 'grad_conv_ln_g', 'grad_conv_ln_b', 'grad_w_conv_o', 'grad_pool_w', 'grad_pool_scale', 'grad_w_pool_o', 'grad_w_mix_o', 'grad_mix_norm_post', 'grad_ffn_norm_pre', 'grad_w_gate', 'grad_w_up', 'grad_w_down', 'grad_ffn_norm_post', 'delta_mix_norm_pre', 'delta_w_in', 'delta_q_norm', 'delta_w_uq', 'delta_kv_norm', 'delta_w_uk', 'delta_w_uv', 'delta_w_attn_o', 'delta_conv_w', 'delta_conv_b', 'delta_conv_ln_g', 'delta_conv_ln_b', 'delta_w_conv_o', 'delta_pool_w', 'delta_pool_scale', 'delta_w_pool_o', 'delta_w_mix_o', 'delta_mix_norm_post', 'delta_ffn_norm_pre', 'delta_w_gate', 'delta_w_up', 'delta_w_down', 'delta_ffn_norm_post', 'new_m_mix_norm_pre', 'new_m_w_in', 'new_m_q_norm', 'new_m_w_uq', 'new_m_kv_norm', 'new_m_w_uk', 'new_m_w_uv', 'new_m_w_attn_o', 'new_m_conv_w', 'new_m_conv_b', 'new_m_conv_ln_g', 'new_m_conv_ln_b', 'new_m_w_conv_o', 'new_m_pool_w', 'new_m_pool_scale', 'new_m_w_pool_o', 'new_m_w_mix_o', 'new_m_mix_norm_post', 'new_m_ffn_norm_pre', 'new_m_w_gate', 'new_m_w_up', 'new_m_w_down', 'new_m_ffn_norm_post', 'new_v_mix_norm_pre', 'new_v_w_in', 'new_v_q_norm', 'new_v_w_uq', 'new_v_kv_norm', 'new_v_w_uk', 'new_v_w_uv', 'new_v_w_attn_o', 'new_v_conv_w', 'new_v_conv_b', 'new_v_conv_ln_g', 'new_v_conv_ln_b', 'new_v_w_conv_o', 'new_v_pool_w', 'new_v_pool_scale', 'new_v_w_pool_o', 'new_v_w_mix_o', 'new_v_mix_norm_post', 'new_v_ffn_norm_pre', 'new_v_w_gate', 'new_v_w_up', 'new_v_w_down', 'new_v_ffn_norm_post']
TWIN_LEAF_KINDS = {'loss': 'loss', 'grad_x': 'grad_x', 'grad_mix_norm_pre': 'grad_w', 'grad_w_in': 'grad_w', 'grad_q_norm': 'grad_w', 'grad_w_uq': 'grad_w', 'grad_kv_norm': 'grad_w', 'grad_w_uk': 'grad_w', 'grad_w_uv': 'grad_w', 'grad_w_attn_o': 'grad_w', 'grad_conv_w': 'grad_w', 'grad_conv_b': 'grad_w', 'grad_conv_ln_g': 'grad_w', 'grad_conv_ln_b': 'grad_w', 'grad_w_conv_o': 'grad_w', 'grad_pool_w': 'grad_w', 'grad_pool_scale': 'grad_w', 'grad_w_pool_o': 'grad_w', 'grad_w_mix_o': 'grad_w', 'grad_mix_norm_post': 'grad_w', 'grad_ffn_norm_pre': 'grad_w', 'grad_w_gate': 'grad_w', 'grad_w_up': 'grad_w', 'grad_w_down': 'grad_w', 'grad_ffn_norm_post': 'grad_w', 'delta_mix_norm_pre': 'delta_w', 'delta_w_in': 'delta_w', 'delta_q_norm': 'delta_w', 'delta_w_uq': 'delta_w', 'delta_kv_norm': 'delta_w', 'delta_w_uk': 'delta_w', 'delta_w_uv': 'delta_w', 'delta_w_attn_o': 'delta_w', 'delta_conv_w': 'delta_w', 'delta_conv_b': 'delta_w', 'delta_conv_ln_g': 'delta_w', 'delta_conv_ln_b': 'delta_w', 'delta_w_conv_o': 'delta_w', 'delta_pool_w': 'delta_w', 'delta_pool_scale': 'delta_w', 'delta_w_pool_o': 'delta_w', 'delta_w_mix_o': 'delta_w', 'delta_mix_norm_post': 'delta_w', 'delta_ffn_norm_pre': 'delta_w', 'delta_w_gate': 'delta_w', 'delta_w_up': 'delta_w', 'delta_w_down': 'delta_w', 'delta_ffn_norm_post': 'delta_w', 'new_m_mix_norm_pre': 'new_m', 'new_m_w_in': 'new_m', 'new_m_q_norm': 'new_m', 'new_m_w_uq': 'new_m', 'new_m_kv_norm': 'new_m', 'new_m_w_uk': 'new_m', 'new_m_w_uv': 'new_m', 'new_m_w_attn_o': 'new_m', 'new_m_conv_w': 'new_m', 'new_m_conv_b': 'new_m', 'new_m_conv_ln_g': 'new_m', 'new_m_conv_ln_b': 'new_m', 'new_m_w_conv_o': 'new_m', 'new_m_pool_w': 'new_m', 'new_m_pool_scale': 'new_m', 'new_m_w_pool_o': 'new_m', 'new_m_w_mix_o': 'new_m', 'new_m_mix_norm_post': 'new_m', 'new_m_ffn_norm_pre': 'new_m', 'new_m_w_gate': 'new_m', 'new_m_w_up': 'new_m', 'new_m_w_down': 'new_m', 'new_m_ffn_norm_post': 'new_m', 'new_v_mix_norm_pre': 'new_v', 'new_v_w_in': 'new_v', 'new_v_q_norm': 'new_v', 'new_v_w_uq': 'new_v', 'new_v_kv_norm': 'new_v', 'new_v_w_uk': 'new_v', 'new_v_w_uv': 'new_v', 'new_v_w_attn_o': 'new_v', 'new_v_conv_w': 'new_v', 'new_v_conv_b': 'new_v', 'new_v_conv_ln_g': 'new_v', 'new_v_conv_ln_b': 'new_v', 'new_v_w_conv_o': 'new_v', 'new_v_pool_w': 'new_v', 'new_v_pool_scale': 'new_v', 'new_v_w_pool_o': 'new_v', 'new_v_w_mix_o': 'new_v', 'new_v_mix_norm_post': 'new_v', 'new_v_ffn_norm_pre': 'new_v', 'new_v_w_gate': 'new_v', 'new_v_w_up': 'new_v', 'new_v_w_down': 'new_v', 'new_v_ffn_norm_post': 'new_v'}


def _forward(args):
    return _fwd_reference(*[args[k] for k in FWD_PARAMS])


def _output_shape():
    out = _jax.eval_shape(lambda: _forward(_fwd_setup_inputs(0)))
    return out.shape, out.dtype

N_MICROBATCH = 1
ADAM_LR = 0.001
ADAM_B1 = 0.9
ADAM_B2 = 0.999
ADAM_EPS = 1e-08
ADAM_WD = 0.01
ADAM_STEP = 10
PER_EXAMPLE_BATCH_AXIS = {'x': 0, 'positions': 0, 'loss_target': 0}
SHARED_INPUTS = []
_WEIGHT_DTYPES = {'mix_norm_pre': _jnp.float32, 'w_in': _jnp.float32, 'q_norm': _jnp.float32, 'w_uq': _jnp.float32, 'kv_norm': _jnp.float32, 'w_uk': _jnp.float32, 'w_uv': _jnp.float32, 'w_attn_o': _jnp.float32, 'conv_w': _jnp.float32, 'conv_b': _jnp.float32, 'conv_ln_g': _jnp.float32, 'conv_ln_b': _jnp.float32, 'w_conv_o': _jnp.float32, 'pool_w': _jnp.float32, 'pool_scale': _jnp.float32, 'w_pool_o': _jnp.float32, 'w_mix_o': _jnp.float32, 'mix_norm_post': _jnp.float32, 'ffn_norm_pre': _jnp.float32, 'w_gate': _jnp.float32, 'w_up': _jnp.float32, 'w_down': _jnp.float32, 'ffn_norm_post': _jnp.float32}
MOMENT_SCALE = {'mix_norm_pre': 1.202925e+00, 'w_in': 4.880830e-01, 'q_norm': 1.832234e-01, 'w_uq': 1.323151e-01, 'kv_norm': 7.062357e-01, 'w_uk': 1.348637e-01, 'w_uv': 4.233443e-01, 'w_attn_o': 2.916403e-01, 'conv_w': 7.202153e-01, 'conv_b': 8.425937e+00, 'conv_ln_g': 3.210581e+00, 'conv_ln_b': 4.907629e+00, 'w_conv_o': 1.203656e+00, 'pool_w': 1.488207e+00, 'pool_scale': 1.408444e+00, 'w_pool_o': 1.108106e+00, 'w_mix_o': 1.683859e+00, 'mix_norm_post': 3.226061e+01, 'ffn_norm_pre': 1.086560e+00, 'w_gate': 3.678851e-01, 'w_up': 5.135599e-01, 'w_down': 8.784748e-01, 'ffn_norm_post': 3.195626e+01}


def _to_microbatches(a, axis):
    t = _jnp.moveaxis(a, axis, 0)
    t = t.reshape((N_MICROBATCH, t.shape[0] // N_MICROBATCH) + t.shape[1:])
    return _jnp.moveaxis(t, 1, axis + 1)


def setup_inputs(seed: int = 0) -> dict:
    inp = _fwd_setup_inputs(seed)
    key = _jax.random.fold_in(_jax.random.key(seed), 7919)
    shape, _ = _output_shape()
    out = dict(inp)
    out["loss_target"] = _jax.random.normal(_jax.random.fold_in(key, 0), shape, _jnp.float32)
    for i, name in enumerate(TWIN_WEIGHTS):
        w = inp[name].astype(_jnp.float32)
        if MOMENT_SCALE is None:
            s = _jnp.sqrt(_jnp.mean(_jnp.square(w)) + 1e-30)
        else:
            s = MOMENT_SCALE[name]
        km, kv = _jax.random.split(_jax.random.fold_in(key, i + 1))
        out[name] = w
        out["m_" + name] = s * _jax.random.normal(km, w.shape, _jnp.float32)
        out["v_" + name] = (s * s) * _jax.random.uniform(kv, w.shape, _jnp.float32, 0.5, 1.5)
    if N_MICROBATCH > 1:
        for name, axis in PER_EXAMPLE_BATCH_AXIS.items():
            out[name] = _to_microbatches(out[name], axis)
    return {'x': out['x'], 'positions': out['positions'], 'mix_norm_pre': out['mix_norm_pre'], 'w_in': out['w_in'], 'q_norm': out['q_norm'], 'w_uq': out['w_uq'], 'kv_norm': out['kv_norm'], 'w_uk': out['w_uk'], 'w_uv': out['w_uv'], 'w_attn_o': out['w_attn_o'], 'conv_w': out['conv_w'], 'conv_b': out['conv_b'], 'conv_ln_g': out['conv_ln_g'], 'conv_ln_b': out['conv_ln_b'], 'w_conv_o': out['w_conv_o'], 'pool_w': out['pool_w'], 'pool_scale': out['pool_scale'], 'w_pool_o': out['w_pool_o'], 'w_mix_o': out['w_mix_o'], 'mix_norm_post': out['mix_norm_post'], 'ffn_norm_pre': out['ffn_norm_pre'], 'w_gate': out['w_gate'], 'w_up': out['w_up'], 'w_down': out['w_down'], 'ffn_norm_post': out['ffn_norm_post'], 'loss_target': out['loss_target'], 'm_mix_norm_pre': out['m_mix_norm_pre'], 'm_w_in': out['m_w_in'], 'm_q_norm': out['m_q_norm'], 'm_w_uq': out['m_w_uq'], 'm_kv_norm': out['m_kv_norm'], 'm_w_uk': out['m_w_uk'], 'm_w_uv': out['m_w_uv'], 'm_w_attn_o': out['m_w_attn_o'], 'm_conv_w': out['m_conv_w'], 'm_conv_b': out['m_conv_b'], 'm_conv_ln_g': out['m_conv_ln_g'], 'm_conv_ln_b': out['m_conv_ln_b'], 'm_w_conv_o': out['m_w_conv_o'], 'm_pool_w': out['m_pool_w'], 'm_pool_scale': out['m_pool_scale'], 'm_w_pool_o': out['m_w_pool_o'], 'm_w_mix_o': out['m_w_mix_o'], 'm_mix_norm_post': out['m_mix_norm_post'], 'm_ffn_norm_pre': out['m_ffn_norm_pre'], 'm_w_gate': out['m_w_gate'], 'm_w_up': out['m_w_up'], 'm_w_down': out['m_w_down'], 'm_ffn_norm_post': out['m_ffn_norm_post'], 'v_mix_norm_pre': out['v_mix_norm_pre'], 'v_w_in': out['v_w_in'], 'v_q_norm': out['v_q_norm'], 'v_w_uq': out['v_w_uq'], 'v_kv_norm': out['v_kv_norm'], 'v_w_uk': out['v_w_uk'], 'v_w_uv': out['v_w_uv'], 'v_w_attn_o': out['v_w_attn_o'], 'v_conv_w': out['v_conv_w'], 'v_conv_b': out['v_conv_b'], 'v_conv_ln_g': out['v_conv_ln_g'], 'v_conv_ln_b': out['v_conv_ln_b'], 'v_w_conv_o': out['v_w_conv_o'], 'v_pool_w': out['v_pool_w'], 'v_pool_scale': out['v_pool_scale'], 'v_w_pool_o': out['v_w_pool_o'], 'v_w_mix_o': out['v_w_mix_o'], 'v_mix_norm_post': out['v_mix_norm_post'], 'v_ffn_norm_pre': out['v_ffn_norm_pre'], 'v_w_gate': out['v_w_gate'], 'v_w_up': out['v_w_up'], 'v_w_down': out['v_w_down'], 'v_ffn_norm_post': out['v_ffn_norm_post']}


def _loss(weights, diff, rest, loss_target):
    with _jax.named_scope("forward"):
        args = {**rest, TWIN_DIFF_INPUT: diff, **{k: w.astype(_WEIGHT_DTYPES[k]) for k, w in weights.items()}}
        y = _forward(args)
    with _jax.named_scope("loss_head"):
        err = _jnp.square(y.astype(_jnp.float32) - loss_target)
        return 0.5 * _jnp.sum(_jnp.mean(err, axis=-1)) if err.ndim else 0.5 * err


def _adamw(w, g, m, v):
    m = ADAM_B1 * m + (1.0 - ADAM_B1) * g
    v = ADAM_B2 * v + (1.0 - ADAM_B2) * _jnp.square(g)
    m_hat = m / (1.0 - ADAM_B1 ** ADAM_STEP)
    v_hat = v / (1.0 - ADAM_B2 ** ADAM_STEP)
    delta = -ADAM_LR * (m_hat / (_jnp.sqrt(v_hat) + ADAM_EPS) + ADAM_WD * w)
    return delta, m, v


def reference(x, positions, mix_norm_pre, w_in, q_norm, w_uq, kv_norm, w_uk, w_uv, w_attn_o, conv_w, conv_b, conv_ln_g, conv_ln_b, w_conv_o, pool_w, pool_scale, w_pool_o, w_mix_o, mix_norm_post, ffn_norm_pre, w_gate, w_up, w_down, ffn_norm_post, loss_target, m_mix_norm_pre, m_w_in, m_q_norm, m_w_uq, m_kv_norm, m_w_uk, m_w_uv, m_w_attn_o, m_conv_w, m_conv_b, m_conv_ln_g, m_conv_ln_b, m_w_conv_o, m_pool_w, m_pool_scale, m_w_pool_o, m_w_mix_o, m_mix_norm_post, m_ffn_norm_pre, m_w_gate, m_w_up, m_w_down, m_ffn_norm_post, v_mix_norm_pre, v_w_in, v_q_norm, v_w_uq, v_kv_norm, v_w_uk, v_w_uv, v_w_attn_o, v_conv_w, v_conv_b, v_conv_ln_g, v_conv_ln_b, v_w_conv_o, v_pool_w, v_pool_scale, v_w_pool_o, v_w_mix_o, v_mix_norm_post, v_ffn_norm_pre, v_w_gate, v_w_up, v_w_down, v_ffn_norm_post):
    given = dict(x=x, positions=positions, mix_norm_pre=mix_norm_pre, w_in=w_in, q_norm=q_norm, w_uq=w_uq, kv_norm=kv_norm, w_uk=w_uk, w_uv=w_uv, w_attn_o=w_attn_o, conv_w=conv_w, conv_b=conv_b, conv_ln_g=conv_ln_g, conv_ln_b=conv_ln_b, w_conv_o=w_conv_o, pool_w=pool_w, pool_scale=pool_scale, w_pool_o=w_pool_o, w_mix_o=w_mix_o, mix_norm_post=mix_norm_post, ffn_norm_pre=ffn_norm_pre, w_gate=w_gate, w_up=w_up, w_down=w_down, ffn_norm_post=ffn_norm_post, loss_target=loss_target, m_mix_norm_pre=m_mix_norm_pre, m_w_in=m_w_in, m_q_norm=m_q_norm, m_w_uq=m_w_uq, m_kv_norm=m_kv_norm, m_w_uk=m_w_uk, m_w_uv=m_w_uv, m_w_attn_o=m_w_attn_o, m_conv_w=m_conv_w, m_conv_b=m_conv_b, m_conv_ln_g=m_conv_ln_g, m_conv_ln_b=m_conv_ln_b, m_w_conv_o=m_w_conv_o, m_pool_w=m_pool_w, m_pool_scale=m_pool_scale, m_w_pool_o=m_w_pool_o, m_w_mix_o=m_w_mix_o, m_mix_norm_post=m_mix_norm_post, m_ffn_norm_pre=m_ffn_norm_pre, m_w_gate=m_w_gate, m_w_up=m_w_up, m_w_down=m_w_down, m_ffn_norm_post=m_ffn_norm_post, v_mix_norm_pre=v_mix_norm_pre, v_w_in=v_w_in, v_q_norm=v_q_norm, v_w_uq=v_w_uq, v_kv_norm=v_kv_norm, v_w_uk=v_w_uk, v_w_uv=v_w_uv, v_w_attn_o=v_w_attn_o, v_conv_w=v_conv_w, v_conv_b=v_conv_b, v_conv_ln_g=v_conv_ln_g, v_conv_ln_b=v_conv_ln_b, v_w_conv_o=v_w_conv_o, v_pool_w=v_pool_w, v_pool_scale=v_pool_scale, v_w_pool_o=v_w_pool_o, v_w_mix_o=v_w_mix_o, v_mix_norm_post=v_mix_norm_post, v_ffn_norm_pre=v_ffn_norm_pre, v_w_gate=v_w_gate, v_w_up=v_w_up, v_w_down=v_w_down, v_ffn_norm_post=v_ffn_norm_post)
    weights = {n: given[n] for n in TWIN_WEIGHTS}
    shared = {n: given[n] for n in SHARED_INPUTS}
    per_example = {n: given[n] for n in ['x', 'positions']}
    grad_fn = _jax.value_and_grad(_loss, argnums=(0, 1))

    def one_microbatch(ex, loss_target):
        ex = dict(ex)
        diff = ex.pop(TWIN_DIFF_INPUT)
        return grad_fn(weights, diff, {**shared, **ex}, loss_target)

    if N_MICROBATCH == 1:
        loss, (grad_w, grad_x) = one_microbatch(per_example, given["loss_target"])
    else:
        def body(carry, xs):
            loss_sum, grad_sum = carry
            l_k, (gw_k, gx_k) = one_microbatch(xs[0], xs[1])
            with _jax.named_scope("update"):
                return (loss_sum + l_k, _jax.tree.map(_jnp.add, grad_sum, gw_k)), gx_k

        init = (_jnp.zeros((), _jnp.float32), _jax.tree.map(_jnp.zeros_like, weights))
        (loss, grad_w), grad_x = _jax.lax.scan(body, init, (per_example, given["loss_target"]))
    with _jax.named_scope("update"):
        delta_w, new_m, new_v = {}, {}, {}
        for n in TWIN_WEIGHTS:
            delta_w[n], new_m[n], new_v[n] = _adamw(weights[n], grad_w[n], given["m_" + n], given["v_" + n])
    return (loss, grad_x, *[grad_w[n] for n in TWIN_WEIGHTS], *[delta_w[n] for n in TWIN_WEIGHTS],
            *[new_m[n] for n in TWIN_WEIGHTS], *[new_v[n] for n in TWIN_WEIGHTS])
```

```python
import functools
import math

import jax
import jax.numpy as jnp
from jax import lax
from jax.experimental import pallas as pl
from jax.experimental.pallas import tpu as pltpu

F32, BF16 = jnp.float32, jnp.bfloat16
MESH = pl.DeviceIdType.MESH

LANES = 128
SUBLANES = 8
VMEM_LIMIT_BYTES = 56 * 1024 * 1024

N_DEV = 8
D_MODEL = 1024
DEPTH = 2
N_HEADS = 8
QK_NOPE, QK_ROPE, V_HEAD = 64, 32, 64
HEAD_PAD = LANES
Q_RANK, KV_RANK = 384, 256
ROPE_THETA = 10000.0
CONV_C, CONV_W = 512, 31
CONV_HALO = 32
POOL_WINDOWS = (2, 4, 8, 16)
POOL_C, POOL_G = 512, 4
POOL_GD = POOL_C // POOL_G
D_FF = 2816
EPS = 1e-6
ATTN_SCALE = 1.0 / math.sqrt(QK_NOPE + QK_ROPE)
LR, B1, B2, ADAM_EPS, WD, STEP = 0.001, 0.9, 0.999, 1e-08, 0.01, 10

Z_W = 5376
ZC_Q = (384, 0)
ZC_KR = (128, 3)
ZC_POOL = (512, 1)
ZC_CONV_A = (512, 2)
ZC_CONV_G = (512, 3)
ZC_GATE = (1024, 2)
ZC_KV = (256, 20)

BIG = ("w_in", "w_uq", "w_uk", "w_uv", "w_attn_o", "w_conv_o", "w_pool_o", "w_mix_o", "w_gate", "w_up", "w_down")
ROW_SHARDED = ("w_mix_o", "w_down")
SMALL = ("mix_norm_pre", "q_norm", "kv_norm", "conv_w", "conv_b", "conv_ln_g", "conv_ln_b", "pool_w", "pool_scale",
         "mix_norm_post", "ffn_norm_pre", "ffn_norm_post")
WEIGHTS = ("mix_norm_pre", "w_in", "q_norm", "w_uq", "kv_norm", "w_uk", "w_uv", "w_attn_o", "conv_w", "conv_b",
           "conv_ln_g", "conv_ln_b", "w_conv_o", "pool_w", "pool_scale", "w_pool_o", "w_mix_o", "mix_norm_post",
           "ffn_norm_pre", "w_gate", "w_up", "w_down", "ffn_norm_post")


def _params(*semantics):
    return pltpu.CompilerParams(dimension_semantics=semantics, vmem_limit_bytes=VMEM_LIMIT_BYTES)


def _tile(dim, cap):
    if dim <= cap:
        return dim
    for t in range(cap - cap % LANES, 0, -LANES):
        if dim % t == 0:
            return t
    raise ValueError(f"no tile for {dim} under {cap}")


def _rows(ts, width, cidx=0):
    return pl.BlockSpec((ts, width), lambda i: (i, cidx))


def _fixed(shape):
    return pl.BlockSpec(shape, lambda *_: (0,) * len(shape))


def _sigmoid(x):
    return 1.0 / (1.0 + jnp.exp(-x))


def _matmul(a, b, mode, out_dtype, name, add=None):
    if mode == "nn":
        (m, k), n = a.shape, b.shape[1]
    elif mode == "nt":
        (m, k), n = a.shape, b.shape[0]
    else:
        (k, m), n = a.shape, b.shape[1]
    tm, tn, tk = _tile(m, 1024), _tile(n, 1408), _tile(k, 1408 if mode != "tn" else 1024)
    nk = k // tk
    dims = {"nn": ((1,), (0,)), "nt": ((1,), (1,)), "tn": ((0,), (0,))}[mode]
    a_spec = {"nn": pl.BlockSpec((tm, tk), lambda i, j, s: (i, s)), "nt": pl.BlockSpec((tm, tk), lambda i, j, s: (i, s)),
              "tn": pl.BlockSpec((tk, tm), lambda i, j, s: (s, i))}[mode]
    b_spec = {"nn": pl.BlockSpec((tk, tn), lambda i, j, s: (s, j)), "nt": pl.BlockSpec((tn, tk), lambda i, j, s: (j, s)),
              "tn": pl.BlockSpec((tk, tn), lambda i, j, s: (s, j))}[mode]
    o_spec = pl.BlockSpec((tm, tn), lambda i, j, s: (i, j))
    has_add = add is not None

    def body(a_ref, b_ref, *rest):
        add_ref = rest[0] if has_add else None
        o_ref = rest[1] if has_add else rest[0]
        part = lax.dot_general(a_ref[...], b_ref[...], (dims, ((), ())), preferred_element_type=F32)

        def finish(total):
            if has_add:
                total = total + add_ref[...]
            o_ref[...] = total.astype(o_ref.dtype)

        if nk == 1:
            finish(part)
        else:
            acc = rest[-1]
            step = pl.program_id(2)

            @pl.when(step == 0)
            def _():
                acc[...] = part

            @pl.when(step > 0)
            def _():
                acc[...] += part

            @pl.when(step == nk - 1)
            def _():
                finish(acc[...])

    operands = (a, b, add) if has_add else (a, b)
    return pl.pallas_call(
        body, name=name, out_shape=jax.ShapeDtypeStruct((m, n), out_dtype), grid=(m // tm, n // tn, nk),
        in_specs=[a_spec, b_spec] + ([o_spec] if has_add else []), out_specs=o_spec,
        scratch_shapes=[pltpu.VMEM((tm, tn), F32)] if nk > 1 else [],
        compiler_params=_params("parallel", "parallel", "arbitrary"))(*operands)


def _rms_fwd(x, win, gain, out_dtype, name, res=None):
    width, cidx = win
    s = x.shape[0]
    ts = min(s, 512)
    has_res = res is not None

    def body(x_ref, g_ref, *rest):
        o_ref = rest[-1]
        xv = x_ref[...]
        r = lax.rsqrt(jnp.mean(xv * xv, axis=-1, keepdims=True) + EPS)
        y = (xv * r) * g_ref[...]
        if has_res:
            y = rest[0][...] + y
        o_ref[...] = y.astype(o_ref.dtype)

    ops = (x, gain.reshape(1, width)) + ((res,) if has_res else ())
    return pl.pallas_call(
        body, name=name, out_shape=jax.ShapeDtypeStruct((s, width), out_dtype), grid=(s // ts,),
        in_specs=[_rows(ts, width, cidx), _fixed((1, width))] + ([_rows(ts, width)] if has_res else []),
        out_specs=_rows(ts, width), compiler_params=_params("parallel"))(*ops)


def _rms_bwd(x, win, gain, dy, out_dtype, name, add=None):
    width, cidx = win
    s = x.shape[0]
    ts = min(s, 512)
    has_add = add is not None

    def body(x_ref, g_ref, dy_ref, *rest):
        dx_ref, dg_ref = rest[-2], rest[-1]
        xv = x_ref[...]
        r = lax.rsqrt(jnp.mean(xv * xv, axis=-1, keepdims=True) + EPS)
        xh = xv * r
        dyv = dy_ref[...].astype(F32)
        dyg = dyv * g_ref[...]
        dx = r * (dyg - xh * jnp.mean(dyg * xh, axis=-1, keepdims=True))
        if has_add:
            dx = dx + rest[0][...]
        dx_ref[...] = dx.astype(dx_ref.dtype)

        @pl.when(pl.program_id(0) == 0)
        def _():
            dg_ref[...] = jnp.zeros_like(dg_ref)

        dg_ref[...] += jnp.sum(dyv * xh, axis=0, keepdims=True)

    ops = (x, gain.reshape(1, width), dy) + ((add,) if has_add else ())
    dx, dg = pl.pallas_call(
        body, name=name,
        out_shape=(jax.ShapeDtypeStruct((s, width), out_dtype), jax.ShapeDtypeStruct((1, width), F32)), grid=(s // ts,),
        in_specs=[_rows(ts, width, cidx), _fixed((1, width)), _rows(ts, width)] + ([_rows(ts, width)] if has_add else []),
        out_specs=(_rows(ts, width), _fixed((1, width))), compiler_params=_params("arbitrary"))(*ops)
    return dx, dg.reshape(width)


def _rope(x, c, s1, s2):
    return x * c + pltpu.roll(x, 16, 1) * s1 + pltpu.roll(x, LANES - 16, 1) * s2


def _rope_t(g, c, s1, s2):
    return g * c + pltpu.roll(g * s1, LANES - 16, 1) + pltpu.roll(g * s2, 16, 1)


def _rope_tables(positions):
    inv_freq = ROPE_THETA ** (-jnp.arange(0, QK_ROPE, 2, dtype=F32) / QK_ROPE)
    ang = positions.astype(F32)[:, None] * inv_freq
    cos, sin = jnp.cos(ang), jnp.sin(ang)
    n = positions.shape[0]
    one, zero = jnp.ones((n, 1), F32), jnp.zeros((n, 1), F32)
    c = jnp.concatenate([jnp.tile(one, (1, QK_NOPE)), cos, cos, jnp.tile(one, (1, 32))], axis=1)
    s1 = jnp.concatenate([jnp.tile(zero, (1, QK_NOPE + 16)), sin, jnp.tile(zero, (1, 32))], axis=1)
    s2 = jnp.concatenate([jnp.tile(zero, (1, QK_NOPE)), -sin, jnp.tile(zero, (1, 48))], axis=1)
    return c, s1, s2


def _rope_qk_fwd(qf, kf, z, tables, name):
    s = qf.shape[0]
    ts = min(s, 256)
    hw = N_HEADS * HEAD_PAD

    def body(qf_ref, kf_ref, kr_ref, c_ref, s1_ref, s2_ref, q_ref, k_ref):
        c, s1, s2 = c_ref[...], s1_ref[...], s2_ref[...]
        kr = _rope(kr_ref[...], c, s1, s2)
        for h in range(N_HEADS):
            sl = slice(h * HEAD_PAD, (h + 1) * HEAD_PAD)
            q_ref[:, sl] = _rope(qf_ref[:, sl], c, s1, s2).astype(BF16)
            k_ref[:, sl] = (kf_ref[:, sl] + kr).astype(BF16)

    tab = _rows(ts, LANES)
    return pl.pallas_call(
        body, name=name, out_shape=(jax.ShapeDtypeStruct((s, hw), BF16),) * 2, grid=(s // ts,),
        in_specs=[_rows(ts, hw), _rows(ts, hw), _rows(ts, *ZC_KR), tab, tab, tab],
        out_specs=(_rows(ts, hw), _rows(ts, hw)), compiler_params=_params("parallel"))(qf, kf, z, *tables)


def _rope_qk_bwd(dq, dk, tables, name):
    s = dq.shape[0]
    ts = min(s, 256)
    hw = N_HEADS * HEAD_PAD

    def body(dq_ref, dk_ref, c_ref, s1_ref, s2_ref, dqf_ref, dkf_ref, dkr_ref):
        c, s1, s2 = c_ref[...], s1_ref[...], s2_ref[...]
        ksum = jnp.zeros((ts, HEAD_PAD), F32)
        for h in range(N_HEADS):
            sl = slice(h * HEAD_PAD, (h + 1) * HEAD_PAD)
            dqf_ref[:, sl] = _rope_t(dq_ref[:, sl], c, s1, s2).astype(BF16)
            dkh = dk_ref[:, sl]
            dkf_ref[:, sl] = dkh.astype(BF16)
            ksum = ksum + dkh
        lane = lax.broadcasted_iota(jnp.int32, (ts, HEAD_PAD), 1)
        in_rope = (lane >= QK_NOPE) & (lane < QK_NOPE + QK_ROPE)
        dkr_ref[...] = jnp.where(in_rope, _rope_t(ksum, c, s1, s2), 0.0).astype(BF16)

    tab = _rows(ts, LANES)
    return pl.pallas_call(
        body, name=name,
        out_shape=(jax.ShapeDtypeStruct((s, hw), BF16), jax.ShapeDtypeStruct((s, hw), BF16),
                   jax.ShapeDtypeStruct((s, LANES), BF16)), grid=(s // ts,),
        in_specs=[_rows(ts, hw), _rows(ts, hw), tab, tab, tab],
        out_specs=(_rows(ts, hw), _rows(ts, hw), _rows(ts, LANES)), compiler_params=_params("parallel"))(dq, dk, *tables)


def _attn_tile(s):
    return min(s, 512)


def _scores(q, k, row0, col0, masked):
    sc = lax.dot_general(q, k, (((1,), (1,)), ((), ())), preferred_element_type=F32) * ATTN_SCALE
    if masked:
        rows = row0 + lax.broadcasted_iota(jnp.int32, sc.shape, 0)
        cols = col0 + lax.broadcasted_iota(jnp.int32, sc.shape, 1)
        sc = jnp.where(cols <= rows, sc, -jnp.inf)
    return sc


def _flash_fwd(q, k, v, name):
    s = q.shape[0]
    t = _attn_tile(s)
    nt = s // t

    def body(q_ref, k_ref, v_ref, o_ref, lse_ref, m_scr, l_scr, acc_scr):
        i, j = pl.program_id(1), pl.program_id(2)

        @pl.when(j == 0)
        def _():
            m_scr[...] = jnp.full_like(m_scr, -jnp.inf)
            l_scr[...] = jnp.zeros_like(l_scr)
            acc_scr[...] = jnp.zeros_like(acc_scr)

        def step(masked):
            sc = _scores(q_ref[...], k_ref[...], i * t, j * t, masked)
            m_old = m_scr[...]
            m_new = jnp.maximum(m_old, jnp.max(sc, axis=-1, keepdims=True))
            p = jnp.exp(sc - m_new)
            alpha = jnp.exp(m_old - m_new)
            l_scr[...] = alpha * l_scr[...] + jnp.sum(p, axis=-1, keepdims=True)
            acc_scr[...] = alpha * acc_scr[...] + jnp.dot(p.astype(BF16), v_ref[...], preferred_element_type=F32)
            m_scr[...] = m_new

        @pl.when(j < i)
        def _():
            step(False)

        @pl.when(j == i)
        def _():
            step(True)
            o_ref[...] = (acc_scr[...] / l_scr[...]).astype(o_ref.dtype)
            lse_ref[0] = m_scr[...] + jnp.log(l_scr[...])

    qo = pl.BlockSpec((t, HEAD_PAD), lambda h, i, j: (i, h))
    kv = pl.BlockSpec((t, HEAD_PAD), lambda h, i, j: (jnp.minimum(j, i), h))
    return pl.pallas_call(
        body, name=name,
        out_shape=(jax.ShapeDtypeStruct(q.shape, BF16), jax.ShapeDtypeStruct((N_HEADS, s, 1), F32)),
        grid=(N_HEADS, nt, nt), in_specs=[qo, kv, kv],
        out_specs=(qo, pl.BlockSpec((1, t, 1), lambda h, i, j: (h, i, 0))),
        scratch_shapes=[pltpu.VMEM((t, 1), F32), pltpu.VMEM((t, 1), F32), pltpu.VMEM((t, HEAD_PAD), F32)],
        compiler_params=_params("parallel", "parallel", "arbitrary"))(q, k, v)


def _attn_delta(do, o, name):
    s = o.shape[0]
    t = _attn_tile(s)

    def body(do_ref, o_ref, delta_ref, dob_ref):
        dov = do_ref[...]
        delta_ref[0] = jnp.sum(dov * o_ref[...].astype(F32), axis=-1, keepdims=True)
        dob_ref[...] = dov.astype(BF16)

    blk = pl.BlockSpec((t, HEAD_PAD), lambda i, h: (i, h))
    return pl.pallas_call(
        body, name=name,
        out_shape=(jax.ShapeDtypeStruct((N_HEADS, s, 1), F32), jax.ShapeDtypeStruct(o.shape, BF16)),
        grid=(s // t, N_HEADS), in_specs=[blk, blk],
        out_specs=(pl.BlockSpec((1, t, 1), lambda i, h: (h, i, 0)), blk),
        compiler_params=_params("parallel", "parallel"))(do, o)


def _probs_and_ds(q, k, v, do, lse, delta, row0, col0, masked):
    sc = _scores(q, k, row0, col0, masked)
    p = jnp.exp(sc - lse)
    dp = lax.dot_general(do, v, (((1,), (1,)), ((), ())), preferred_element_type=F32)
    ds = p * (dp - delta) * ATTN_SCALE
    return p.astype(BF16), ds.astype(BF16)


def _flash_bwd_kv(q, k, v, do, lse, delta, name):
    s = q.shape[0]
    t = _attn_tile(s)
    nt = s // t

    def body(q_ref, k_ref, v_ref, do_ref, lse_ref, delta_ref, dk_ref, dv_ref, dk_scr, dv_scr):
        j, i = pl.program_id(1), pl.program_id(2)

        @pl.when(i == 0)
        def _():
            dk_scr[...] = jnp.zeros_like(dk_scr)
            dv_scr[...] = jnp.zeros_like(dv_scr)

        def step(masked):
            p, ds = _probs_and_ds(q_ref[...], k_ref[...], v_ref[...], do_ref[...], lse_ref[0], delta_ref[0],
                                  i * t, j * t, masked)
            dv_scr[...] += lax.dot_general(p, do_ref[...], (((0,), (0,)), ((), ())), preferred_element_type=F32)
            dk_scr[...] += lax.dot_general(ds, q_ref[...], (((0,), (0,)), ((), ())), preferred_element_type=F32)

        @pl.when(i == j)
        def _():
            step(True)

        @pl.when(i > j)
        def _():
            step(False)

        @pl.when(i == nt - 1)
        def _():
            dk_ref[...] = dk_scr[...]
            dv_ref[...] = dv_scr[...].astype(BF16)

    kv = pl.BlockSpec((t, HEAD_PAD), lambda h, j, i: (j, h))
    qd = pl.BlockSpec((t, HEAD_PAD), lambda h, j, i: (jnp.maximum(i, j), h))
    st = pl.BlockSpec((1, t, 1), lambda h, j, i: (h, jnp.maximum(i, j), 0))
    return pl.pallas_call(
        body, name=name, out_shape=(jax.ShapeDtypeStruct(q.shape, F32), jax.ShapeDtypeStruct(q.shape, BF16)),
        grid=(N_HEADS, nt, nt), in_specs=[qd, kv, kv, qd, st, st], out_specs=(kv, kv),
        scratch_shapes=[pltpu.VMEM((t, HEAD_PAD), F32), pltpu.VMEM((t, HEAD_PAD), F32)],
        compiler_params=_params("parallel", "parallel", "arbitrary"))(q, k, v, do, lse, delta)


def _flash_bwd_q(q, k, v, do, lse, delta, name):
    s = q.shape[0]
    t = _attn_tile(s)
    nt = s // t

    def body(q_ref, k_ref, v_ref, do_ref, lse_ref, delta_ref, dq_ref, dq_scr):
        i, j = pl.program_id(1), pl.program_id(2)

        @pl.when(j == 0)
        def _():
            dq_scr[...] = jnp.zeros_like(dq_scr)

        def step(masked):
            _, ds = _probs_and_ds(q_ref[...], k_ref[...], v_ref[...], do_ref[...], lse_ref[0], delta_ref[0],
                                  i * t, j * t, masked)
            dq_scr[...] += jnp.dot(ds, k_ref[...], preferred_element_type=F32)

        @pl.when(j < i)
        def _():
            step(False)

        @pl.when(j == i)
        def _():
            step(True)
            dq_ref[...] = dq_scr[...]

    qd = pl.BlockSpec((t, HEAD_PAD), lambda h, i, j: (i, h))
    kv = pl.BlockSpec((t, HEAD_PAD), lambda h, i, j: (jnp.minimum(j, i), h))
    st = pl.BlockSpec((1, t, 1), lambda h, i, j: (h, i, 0))
    return pl.pallas_call(
        body, name=name, out_shape=jax.ShapeDtypeStruct(q.shape, F32), grid=(N_HEADS, nt, nt),
        in_specs=[qd, kv, kv, qd, st, st], out_specs=qd, scratch_shapes=[pltpu.VMEM((t, HEAD_PAD), F32)],
        compiler_params=_params("parallel", "parallel", "arbitrary"))(q, k, v, do, lse, delta)


def _conv_tile(s):
    return min(s, 256)


def _halo_before(t, width, cidx):
    per = t // CONV_HALO
    return pl.BlockSpec((CONV_HALO, width), lambda i: (jnp.maximum(i * per - 1, 0), cidx))


def _halo_after(t, width, cidx, n_tiles):
    per = t // CONV_HALO
    last = n_tiles * per - 1
    return pl.BlockSpec((CONV_HALO, width), lambda i: (jnp.minimum((i + 1) * per, last), cidx))


def _fill_glu(hbuf, ap_ref, gp_ref, a_ref, g_ref, t):
    first = pl.program_id(0) == 0
    hbuf[pl.ds(0, CONV_HALO), :] = jnp.where(first, 0.0, ap_ref[...] * _sigmoid(gp_ref[...]))
    hbuf[pl.ds(CONV_HALO, t), :] = a_ref[...] * _sigmoid(g_ref[...])


def _layer_norm_parts(co):
    mu = jnp.mean(co, axis=-1, keepdims=True)
    xc = co - mu
    rstd = lax.rsqrt(jnp.mean(xc * xc, axis=-1, keepdims=True) + EPS)
    return xc * rstd, rstd


def _conv_fwd(z, conv_w, conv_b, ln_g, ln_b, name):
    s = z.shape[0]
    t = _conv_tile(s)
    off = CONV_HALO - (CONV_W - 1)

    def body(ap_ref, gp_ref, a_ref, g_ref, w_ref, b_ref, lg_ref, lb_ref, hc_ref, co_ref, hbuf):
        _fill_glu(hbuf, ap_ref, gp_ref, a_ref, g_ref, t)
        acc = jnp.zeros((t, CONV_C), F32) + b_ref[...]
        for j in range(CONV_W):
            acc = acc + hbuf[pl.ds(off + j, t), :] * w_ref[pl.ds(j, 1), :]
        co_ref[...] = acc
        xh, _ = _layer_norm_parts(acc)
        y = xh * lg_ref[...] + lb_ref[...]
        hc_ref[...] = (y * _sigmoid(y)).astype(BF16)

    vec = _fixed((1, CONV_C))
    return pl.pallas_call(
        body, name=name, out_shape=(jax.ShapeDtypeStruct((s, CONV_C), BF16), jax.ShapeDtypeStruct((s, CONV_C), F32)),
        grid=(s // t,),
        in_specs=[_halo_before(t, *ZC_CONV_A), _halo_before(t, *ZC_CONV_G), _rows(t, *ZC_CONV_A), _rows(t, *ZC_CONV_G),
                  _fixed((CONV_HALO, CONV_C)), vec, vec, vec],
        out_specs=(_rows(t, CONV_C), _rows(t, CONV_C)), scratch_shapes=[pltpu.VMEM((t + CONV_HALO, CONV_C), F32)],
        compiler_params=_params("parallel"))(z, z, z, z, conv_w, conv_b.reshape(1, -1), ln_g.reshape(1, -1),
                                             ln_b.reshape(1, -1))


def _conv_bwd_norm(dhc, co, ln_g, ln_b, name):
    s = co.shape[0]
    t = min(s, 512)

    def body(dhc_ref, co_ref, lg_ref, lb_ref, dco_ref, dg_ref, db_ref, dcb_ref):
        xh, rstd = _layer_norm_parts(co_ref[...])
        y = xh * lg_ref[...] + lb_ref[...]
        sg = _sigmoid(y)
        dy = dhc_ref[...] * (sg * (1.0 + y * (1.0 - sg)))
        dxh = dy * lg_ref[...]
        dco = rstd * (dxh - jnp.mean(dxh, axis=-1, keepdims=True) - xh * jnp.mean(dxh * xh, axis=-1, keepdims=True))
        dco_ref[...] = dco

        @pl.when(pl.program_id(0) == 0)
        def _():
            dg_ref[...] = jnp.zeros_like(dg_ref)
            db_ref[...] = jnp.zeros_like(db_ref)
            dcb_ref[...] = jnp.zeros_like(dcb_ref)

        dg_ref[...] += jnp.sum(dy * xh, axis=0, keepdims=True)
        db_ref[...] += jnp.sum(dy, axis=0, keepdims=True)
        dcb_ref[...] += jnp.sum(dco, axis=0, keepdims=True)

    vec = _fixed((1, CONV_C))
    one = jax.ShapeDtypeStruct((1, CONV_C), F32)
    dco, dg, db, dcb = pl.pallas_call(
        body, name=name, out_shape=(jax.ShapeDtypeStruct((s, CONV_C), F32), one, one, one), grid=(s // t,),
        in_specs=[_rows(t, CONV_C), _rows(t, CONV_C), vec, vec], out_specs=(_rows(t, CONV_C), vec, vec, vec),
        compiler_params=_params("arbitrary"))(dhc, co, ln_g.reshape(1, -1), ln_b.reshape(1, -1))
    return dco, dg.reshape(-1), db.reshape(-1), dcb.reshape(-1)


def _conv_bwd_taps(dco, z, conv_w, name):
    s = z.shape[0]
    t = _conv_tile(s)
    nt = s // t
    off = CONV_HALO - (CONV_W - 1)

    def body(ap_ref, gp_ref, a_ref, g_ref, d_ref, dn_ref, w_ref, du_ref, dw_ref, hbuf, dbuf):
        i = pl.program_id(0)
        _fill_glu(hbuf, ap_ref, gp_ref, a_ref, g_ref, t)
        dbuf[pl.ds(0, t), :] = d_ref[...]
        dbuf[pl.ds(t, CONV_HALO), :] = jnp.where(i == nt - 1, 0.0, dn_ref[...])

        @pl.when(i == 0)
        def _():
            dw_ref[...] = jnp.zeros_like(dw_ref)

        dcur = d_ref[...]
        dh = jnp.zeros((t, CONV_C), F32)
        for j in range(CONV_W):
            dh = dh + dbuf[pl.ds(CONV_W - 1 - j, t), :] * w_ref[pl.ds(j, 1), :]
            dw_ref[pl.ds(j, 1), :] += jnp.sum(dcur * hbuf[pl.ds(off + j, t), :], axis=0, keepdims=True)
        a, sg = a_ref[...], _sigmoid(g_ref[...])
        du_ref[:, pl.ds(0, CONV_C)] = (dh * sg).astype(BF16)
        du_ref[:, pl.ds(CONV_C, CONV_C)] = (dh * a * sg * (1.0 - sg)).astype(BF16)

    return pl.pallas_call(
        body, name=name,
        out_shape=(jax.ShapeDtypeStruct((s, 2 * CONV_C), BF16), jax.ShapeDtypeStruct((CONV_HALO, CONV_C), F32)),
        grid=(nt,),
        in_specs=[_halo_before(t, *ZC_CONV_A), _halo_before(t, *ZC_CONV_G), _rows(t, *ZC_CONV_A), _rows(t, *ZC_CONV_G),
                  _rows(t, CONV_C), _halo_after(t, CONV_C, 0, nt), _fixed((CONV_HALO, CONV_C))],
        out_specs=(_rows(t, 2 * CONV_C), _fixed((CONV_HALO, CONV_C))),
        scratch_shapes=[pltpu.VMEM((t + CONV_HALO, CONV_C), F32), pltpu.VMEM((t + CONV_HALO, CONV_C), F32)],
        compiler_params=_params("arbitrary"))(z, z, z, z, dco, dco, conv_w)


def _pool_tile(s):
    return min(s, 512)


def _pool_counts(row0, n, window):
    rows = row0 + lax.broadcasted_iota(jnp.int32, (n, POOL_GD), 0)
    return jnp.minimum(rows + 1, window).astype(F32)


def _pool_diff(ubuf, gi, window, row0, t):
    lanes = pl.ds(gi * POOL_GD, POOL_GD)
    tot = ubuf[pl.ds(CONV_HALO, t), lanes]
    cur = tot
    for back in range(1, window):
        tot = tot + ubuf[pl.ds(CONV_HALO - back, t), lanes]
    return tot / _pool_counts(row0, t, window) - cur


def _pool_fwd(z, pool_w, pool_scale, name):
    s = z.shape[0]
    t = _pool_tile(s)

    def body(up_ref, u_ref, w_ref, sc_ref, m_ref, ubuf):
        i = pl.program_id(0)
        ubuf[pl.ds(0, CONV_HALO), :] = jnp.where(i == 0, 0.0, up_ref[...])
        ubuf[pl.ds(CONV_HALO, t), :] = u_ref[...]
        for gi, window in enumerate(POOL_WINDOWS):
            d = _pool_diff(ubuf, gi, window, i * t, t)
            mm = jnp.dot(d.astype(BF16), w_ref[gi].astype(BF16), preferred_element_type=F32)
            lanes = pl.ds(gi * POOL_GD, POOL_GD)
            m_ref[:, lanes] = (mm * sc_ref[:, lanes]).astype(BF16)

    return pl.pallas_call(
        body, name=name, out_shape=jax.ShapeDtypeStruct((s, POOL_C), BF16), grid=(s // t,),
        in_specs=[_halo_before(t, *ZC_POOL), _rows(t, *ZC_POOL), _fixed((POOL_G, POOL_GD, POOL_GD)), _fixed((1, POOL_C))],
        out_specs=_rows(t, POOL_C), scratch_shapes=[pltpu.VMEM((t + CONV_HALO, POOL_C), F32)],
        compiler_params=_params("parallel"))(z, z, pool_w, pool_scale.reshape(1, -1))


def _pool_bwd(dm, z, pool_w, pool_scale, name):
    s = z.shape[0]
    t = _pool_tile(s)
    nt = s // t

    def body(up_ref, u_ref, dm_ref, dmn_ref, w_ref, sc_ref, du_ref, dw_ref, dsc_ref, ubuf, ebuf):
        i = pl.program_id(0)
        ubuf[pl.ds(0, CONV_HALO), :] = jnp.where(i == 0, 0.0, up_ref[...])
        ubuf[pl.ds(CONV_HALO, t), :] = u_ref[...]

        @pl.when(i == 0)
        def _():
            dw_ref[...] = jnp.zeros_like(dw_ref)
            dsc_ref[...] = jnp.zeros_like(dsc_ref)

        dm_next = jnp.where(i == nt - 1, 0.0, dmn_ref[...])
        for gi, window in enumerate(POOL_WINDOWS):
            lanes = pl.ds(gi * POOL_GD, POOL_GD)
            wb = w_ref[gi].astype(BF16)
            scale = sc_ref[:, lanes]
            d = _pool_diff(ubuf, gi, window, i * t, t).astype(BF16)
            mm = jnp.dot(d, wb, preferred_element_type=F32)
            dmv = dm_ref[:, lanes]
            dsc_ref[:, lanes] += jnp.sum(dmv * mm, axis=0, keepdims=True)
            dmm = (dmv * scale).astype(BF16)
            dw_ref[gi] += lax.dot_general(d, dmm, (((0,), (0,)), ((), ())), preferred_element_type=F32)
            dd = lax.dot_general(dmm, wb, (((1,), (1,)), ((), ())), preferred_element_type=F32)
            dd_next = lax.dot_general((dm_next[:, gi * POOL_GD:(gi + 1) * POOL_GD] * scale).astype(BF16), wb,
                                      (((1,), (1,)), ((), ())), preferred_element_type=F32)
            ebuf[pl.ds(0, t), lanes] = dd / _pool_counts(i * t, t, window)
            ebuf[pl.ds(t, CONV_HALO), lanes] = dd_next / _pool_counts((i + 1) * t, CONV_HALO, window)
            du = -dd
            for ahead in range(window):
                du = du + ebuf[pl.ds(ahead, t), lanes]
            du_ref[:, lanes] = du.astype(BF16)

    du, dw, dsc = pl.pallas_call(
        body, name=name,
        out_shape=(jax.ShapeDtypeStruct((s, POOL_C), BF16), jax.ShapeDtypeStruct((POOL_G, POOL_GD, POOL_GD), F32),
                   jax.ShapeDtypeStruct((1, POOL_C), F32)), grid=(nt,),
        in_specs=[_halo_before(t, *ZC_POOL), _rows(t, *ZC_POOL), _rows(t, POOL_C), _halo_after(t, POOL_C, 0, nt),
                  _fixed((POOL_G, POOL_GD, POOL_GD)), _fixed((1, POOL_C))],
        out_specs=(_rows(t, POOL_C), _fixed((POOL_G, POOL_GD, POOL_GD)), _fixed((1, POOL_C))),
        scratch_shapes=[pltpu.VMEM((t + CONV_HALO, POOL_C), F32), pltpu.VMEM((t + CONV_HALO, POOL_C), F32)],
        compiler_params=_params("arbitrary"))(z, z, dm, dm, pool_w, pool_scale.reshape(1, -1))
    return du, dw, dsc.reshape(-1)


def _gate_specs(ts):
    width, first = ZC_GATE
    return [_rows(ts, width, first + b) for b in range(3)]


def _merge_fwd(z, ys, name):
    s = z.shape[0]
    ts = min(s, 256)

    def body(g0, g1, g2, y0, y1, y2, o_ref):
        o_ref[...] = (_sigmoid(g0[...]) * y0[...] + _sigmoid(g1[...]) * y1[...]
                      + _sigmoid(g2[...]) * y2[...]).astype(BF16)

    return pl.pallas_call(
        body, name=name, out_shape=jax.ShapeDtypeStruct((s, D_MODEL), BF16), grid=(s // ts,),
        in_specs=_gate_specs(ts) + [_rows(ts, D_MODEL)] * 3, out_specs=_rows(ts, D_MODEL),
        compiler_params=_params("parallel"))(z, z, z, *ys)


def _merge_bwd(z, ys, dmerged, name):
    s = z.shape[0]
    ts = min(s, 256)

    def body(g0, g1, g2, y0, y1, y2, dm_ref, dy0, dy1, dy2, dg0, dg1, dg2):
        dmv = dm_ref[...]
        for g_ref, y_ref, dy_ref, dg_ref in ((g0, y0, dy0, dg0), (g1, y1, dy1, dg1), (g2, y2, dy2, dg2)):
            sg = _sigmoid(g_ref[...])
            dy_ref[...] = (dmv * sg).astype(BF16)
            dg_ref[...] = (dmv * y_ref[...] * sg * (1.0 - sg)).astype(BF16)

    out = jax.ShapeDtypeStruct((s, D_MODEL), BF16)
    return pl.pallas_call(
        body, name=name, out_shape=(out,) * 6, grid=(s // ts,),
        in_specs=_gate_specs(ts) + [_rows(ts, D_MODEL)] * 4, out_specs=(_rows(ts, D_MODEL),) * 6,
        compiler_params=_params("parallel"))(z, z, z, *ys, dmerged)


def _swiglu_fwd(gu, name):
    s = gu.shape[0]
    ts, tc = min(s, 512), D_FF // 2
    nc = D_FF // tc

    def body(g_ref, u_ref, o_ref):
        g = g_ref[...]
        o_ref[...] = (g * _sigmoid(g) * u_ref[...]).astype(BF16)

    return pl.pallas_call(
        body, name=name, out_shape=jax.ShapeDtypeStruct((s, D_FF), BF16), grid=(s // ts, nc),
        in_specs=[pl.BlockSpec((ts, tc), lambda i, j: (i, j)), pl.BlockSpec((ts, tc), lambda i, j: (i, j + nc))],
        out_specs=pl.BlockSpec((ts, tc), lambda i, j: (i, j)), compiler_params=_params("parallel", "parallel"))(gu, gu)


def _swiglu_bwd(gu, dact, name):
    s = gu.shape[0]
    ts, tc = min(s, 512), D_FF // 2
    nc = D_FF // tc

    def body(g_ref, u_ref, d_ref, o_ref):
        g, d = g_ref[...], d_ref[...]
        sg = _sigmoid(g)

        @pl.when(pl.program_id(1) < nc)
        def _():
            o_ref[...] = (d * u_ref[...] * (sg * (1.0 + g * (1.0 - sg)))).astype(BF16)

        @pl.when(pl.program_id(1) >= nc)
        def _():
            o_ref[...] = (d * g * sg).astype(BF16)

    lo = pl.BlockSpec((ts, tc), lambda i, j: (i, j % nc))
    hi = pl.BlockSpec((ts, tc), lambda i, j: (i, j % nc + nc))
    return pl.pallas_call(
        body, name=name, out_shape=jax.ShapeDtypeStruct((s, 2 * D_FF), BF16), grid=(s // ts, 2 * nc),
        in_specs=[lo, hi, lo], out_specs=pl.BlockSpec((ts, tc), lambda i, j: (i, j)),
        compiler_params=_params("parallel", "parallel"))(gu, gu, dact)


def _loss_grad(y, target, name):
    s, d = y.shape
    ts = min(s, 512)

    def body(y_ref, t_ref, dy_ref, sq_ref):
        e = y_ref[...] - t_ref[...]
        dy_ref[...] = e / d

        @pl.when(pl.program_id(0) == 0)
        def _():
            sq_ref[...] = jnp.zeros_like(sq_ref)

        sq_ref[...] += jnp.sum(e * e, axis=0, keepdims=True)

    return pl.pallas_call(
        body, name=name, out_shape=(jax.ShapeDtypeStruct((s, d), F32), jax.ShapeDtypeStruct((1, d), F32)),
        grid=(s // ts,), in_specs=[_rows(ts, d), _rows(ts, d)], out_specs=(_rows(ts, d), _fixed((1, d))),
        compiler_params=_params("arbitrary"))(y, target)


def _adamw(w, g, m, v, name):
    shape = w.shape
    cols = shape[-1]
    rows = math.prod(shape[:-1])
    tr = rows
    if rows * cols * 4 > (1 << 20):
        for cand in range(min(rows, 512) - min(rows, 512) % SUBLANES, 0, -SUBLANES):
            if rows % cand == 0 and cand * cols * 4 <= (1 << 20):
                tr = cand
                break

    def body(w_ref, g_ref, m_ref, v_ref, d_ref, mo_ref, vo_ref):
        gv = g_ref[...]
        mn = B1 * m_ref[...] + (1.0 - B1) * gv
        vn = B2 * v_ref[...] + (1.0 - B2) * (gv * gv)
        m_hat = mn / (1.0 - B1 ** STEP)
        v_hat = vn / (1.0 - B2 ** STEP)
        d_ref[...] = -LR * (m_hat / (jnp.sqrt(v_hat) + ADAM_EPS) + WD * w_ref[...])
        mo_ref[...] = mn
        vo_ref[...] = vn

    spec = _rows(tr, cols)
    out = jax.ShapeDtypeStruct((rows, cols), F32)
    res = pl.pallas_call(
        body, name=name, out_shape=(out,) * 3, grid=(rows // tr,), in_specs=[spec] * 4, out_specs=(spec,) * 3,
        compiler_params=_params("parallel"))(*[t.reshape(rows, cols) for t in (w, g, m, v)])
    return tuple(r.reshape(shape) for r in res)


ANY = pl.BlockSpec(memory_space=pl.ANY)


def _all_gather(slab, name):
    def body(x_ref, out_ref, send_sems, recv_sems, local_sem):
        x, y, c = lax.axis_index("x"), lax.axis_index("y"), lax.axis_index("c")
        me, sibling = (x, y, c), (x, y, 1 - c)
        chips = [(1 - x, y), (x, 1 - y), (1 - x, 1 - y)]

        def rows(px, py, pc):
            return out_ref.at[4 * px + 2 * py + pc]

        def copy(k, block, to, src=None):
            return pltpu.make_async_remote_copy(
                src_ref=rows(*block) if src is None else src, dst_ref=rows(*block), send_sem=send_sems.at[k],
                recv_sem=recv_sems.at[k], device_id=to, device_id_type=MESH)

        mine = pltpu.make_async_copy(x_ref, rows(*me), local_sem)
        mine.start()
        first = [copy(0, me, sibling, src=x_ref)]
        first += [copy(1 + j, me, (*chip, c), src=x_ref) for j, chip in enumerate(chips)]
        for cp in first:
            cp.start()
        passed = [copy(4 + j, (*chip, c), sibling) for j, chip in enumerate(chips)]
        for j, chip in enumerate(chips):
            copy(1 + j, (*chip, c), me).wait_recv()
            passed[j].start()
        copy(0, sibling, me).wait_recv()
        for j, chip in enumerate(chips):
            copy(4 + j, (*chip, 1 - c), me).wait_recv()
        for cp in first + passed:
            cp.wait_send()
        mine.wait()

    return pl.pallas_call(
        body, name=name, out_shape=jax.ShapeDtypeStruct((N_DEV,) + slab.shape, slab.dtype), in_specs=[ANY], out_specs=ANY,
        scratch_shapes=[pltpu.SemaphoreType.DMA((7,)), pltpu.SemaphoreType.DMA((7,)), pltpu.SemaphoreType.DMA])(slab)


def _swap_with_sibling(p, name):
    def body(p_ref, out_ref, send_sem, recv_sem):
        x, y, c = lax.axis_index("x"), lax.axis_index("y"), lax.axis_index("c")
        cp = pltpu.make_async_remote_copy(src_ref=p_ref.at[1 - c], dst_ref=out_ref, send_sem=send_sem, recv_sem=recv_sem,
                                          device_id=(x, y, 1 - c), device_id_type=MESH)
        cp.start()
        cp.wait()

    return pl.pallas_call(
        body, name=name, out_shape=jax.ShapeDtypeStruct(p.shape[1:], p.dtype), in_specs=[ANY], out_specs=ANY,
        scratch_shapes=[pltpu.SemaphoreType.DMA, pltpu.SemaphoreType.DMA])(p)


def _exchange_chips(q, name):
    def body(q_ref, out_ref, send_sems, recv_sems, local_sem):
        x, y, c = lax.axis_index("x"), lax.axis_index("y"), lax.axis_index("c")
        mine = pltpu.make_async_copy(q_ref.at[2 * x + y], out_ref.at[0], local_sem)
        mine.start()
        partners = [(x, 1 - y), (1 - x, y), (1 - x, 1 - y)]
        copies = [pltpu.make_async_remote_copy(
            src_ref=q_ref.at[2 * px + py], dst_ref=out_ref.at[1 + k], send_sem=send_sems.at[k], recv_sem=recv_sems.at[k],
            device_id=(px, py, c), device_id_type=MESH) for k, (px, py) in enumerate(partners)]
        for cp in copies:
            cp.start()
        for cp in copies:
            cp.wait()
        mine.wait()

    return pl.pallas_call(
        body, name=name, out_shape=jax.ShapeDtypeStruct(q.shape, q.dtype), in_specs=[ANY], out_specs=ANY,
        scratch_shapes=[pltpu.SemaphoreType.DMA((3,)), pltpu.SemaphoreType.DMA((3,)), pltpu.SemaphoreType.DMA])(q)


def _add_pairs(a, b, name):
    n, r, _ = a.shape
    tr = _tile_rows(r)

    def body(a_ref, b_ref, o_ref):
        o_ref[...] = (a_ref[...].astype(F32) + b_ref[...].astype(F32)).astype(o_ref.dtype)

    spec = pl.BlockSpec((1, tr, LANES), lambda i, j: (i, j, 0))
    return pl.pallas_call(body, name=name, out_shape=jax.ShapeDtypeStruct(a.shape, a.dtype), grid=(n, r // tr),
                          in_specs=[spec, spec], out_specs=spec, compiler_params=_params("parallel", "parallel"))(a, b)


def _tile_rows(r):
    for cand in (4208, 2104, 1052, 1384, 692, 512, 256, 128, 64, 32, 16, 8):
        if r % cand == 0 and cand % 16 == 0:
            return cand
    return r


def _sum_blocks(a, name):
    n, r, _ = a.shape
    tr = _tile_rows(r)

    def body(a_ref, o_ref):
        tot = a_ref[0].astype(F32)
        for k in range(1, n):
            tot = tot + a_ref[k].astype(F32)
        o_ref[...] = tot

    return pl.pallas_call(body, name=name, out_shape=jax.ShapeDtypeStruct((r, LANES), F32), grid=(r // tr,),
                          in_specs=[pl.BlockSpec((n, tr, LANES), lambda j: (0, j, 0))],
                          out_specs=pl.BlockSpec((tr, LANES), lambda j: (j, 0)), compiler_params=_params("parallel"))(a)


def _pad_heads_cols(w, per_head):
    k = w.shape[0]
    return jnp.pad(w.reshape(k, N_HEADS, per_head), ((0, 0), (0, 0), (0, HEAD_PAD - per_head))).reshape(k, N_HEADS * HEAD_PAD)


def _unpad_heads_cols(w, per_head):
    k = w.shape[0]
    return w.reshape(k, N_HEADS, HEAD_PAD)[:, :, :per_head].reshape(k, N_HEADS * per_head)


def _arrange_w_in(w):
    zeros = lambda n: jnp.zeros((w.shape[0], n), w.dtype)
    return jnp.concatenate([w[:, 0:384], zeros(64), w[:, 640:672], zeros(32), w[:, 1696:2208], w[:, 672:1696],
                            w[:, 2208:5280], w[:, 384:640]], axis=1)


def _unarrange_w_in(d):
    return jnp.concatenate([d[:, 0:384], d[:, 5120:5376], d[:, 448:480], d[:, 1024:2048], d[:, 512:1024],
                            d[:, 2048:5120]], axis=1)


def _layer_weights(full):
    v = full["w_attn_o"].reshape(N_HEADS, V_HEAD, D_MODEL)
    return {
        "w_in": _arrange_w_in(full["w_in"]),
        "w_uq": _pad_heads_cols(full["w_uq"], QK_NOPE + QK_ROPE),
        "w_uk": _pad_heads_cols(full["w_uk"], QK_NOPE),
        "w_uv": _pad_heads_cols(full["w_uv"], V_HEAD),
        "w_attn_o": jnp.pad(v, ((0, 0), (0, HEAD_PAD - V_HEAD), (0, 0))).reshape(N_HEADS * HEAD_PAD, D_MODEL),
        "w_conv_o": full["w_conv_o"], "w_pool_o": full["w_pool_o"], "w_mix_o": full["w_mix_o"],
        "w_gu": jnp.concatenate([full["w_gate"], full["w_up"]], axis=1), "w_down": full["w_down"],
    }


def _layer_grads_to_reference_layout(g):
    return {
        "w_in": _unarrange_w_in(g["w_in"]),
        "w_uq": _unpad_heads_cols(g["w_uq"], QK_NOPE + QK_ROPE),
        "w_uk": _unpad_heads_cols(g["w_uk"], QK_NOPE),
        "w_uv": _unpad_heads_cols(g["w_uv"], V_HEAD),
        "w_attn_o": g["w_attn_o"].reshape(N_HEADS, HEAD_PAD, D_MODEL)[:, :V_HEAD].reshape(N_HEADS * V_HEAD, D_MODEL),
        "w_conv_o": g["w_conv_o"], "w_pool_o": g["w_pool_o"], "w_mix_o": g["w_mix_o"],
        "w_gate": g["w_gu"][:, :D_FF], "w_up": g["w_gu"][:, D_FF:], "w_down": g["w_down"],
    }


def _layer_fwd(x, tables, w, sm, tag):
    nm = lambda n: f"{n}_{tag}"
    h = _rms_fwd(x, (D_MODEL, 0), sm["mix_norm_pre"], BF16, nm("mix_pre_norm"))
    z = _matmul(h, w["w_in"], "nn", F32, nm("in_proj"))
    cq = _rms_fwd(z, ZC_Q, sm["q_norm"], BF16, nm("q_norm"))
    ckv = _rms_fwd(z, ZC_KV, sm["kv_norm"], BF16, nm("kv_norm"))
    qf = _matmul(cq, w["w_uq"], "nn", F32, nm("q_up"))
    kf = _matmul(ckv, w["w_uk"], "nn", F32, nm("k_up"))
    v = _matmul(ckv, w["w_uv"], "nn", BF16, nm("v_up"))
    q, k = _rope_qk_fwd(qf, kf, z, tables, nm("rope_qk"))
    o, lse = _flash_fwd(q, k, v, nm("flash_fwd"))
    y_attn = _matmul(o, w["w_attn_o"], "nn", F32, nm("attn_out"))
    hc, co = _conv_fwd(z, sm["conv_w"], sm["conv_b"], sm["conv_ln_g"], sm["conv_ln_b"], nm("conv_fwd"))
    y_conv = _matmul(hc, w["w_conv_o"], "nn", F32, nm("conv_out"))
    pm = _pool_fwd(z, sm["pool_w"], sm["pool_scale"], nm("pool_fwd"))
    y_pool = _matmul(pm, w["w_pool_o"], "nn", F32, nm("pool_out"))
    ys = (y_attn, y_conv, y_pool)
    merged = _merge_fwd(z, ys, nm("merge_fwd"))
    mo = _matmul(merged, w["w_mix_o"], "nn", F32, nm("mix_out"))
    x_mid = _rms_fwd(mo, (D_MODEL, 0), sm["mix_norm_post"], F32, nm("mix_post_norm"), res=x)
    h2 = _rms_fwd(x_mid, (D_MODEL, 0), sm["ffn_norm_pre"], BF16, nm("ffn_pre_norm"))
    gu = _matmul(h2, w["w_gu"], "nn", F32, nm("ffn_gate_up"))
    act = _swiglu_fwd(gu, nm("swiglu_fwd"))
    fo = _matmul(act, w["w_down"], "nn", F32, nm("ffn_down"))
    out = _rms_fwd(fo, (D_MODEL, 0), sm["ffn_norm_post"], F32, nm("ffn_post_norm"), res=x_mid)
    saved = dict(x=x, h=h, z=z, cq=cq, ckv=ckv, q=q, k=k, v=v, o=o, lse=lse, hc=hc, co=co, pm=pm, ys=ys, merged=merged,
                 mo=mo, x_mid=x_mid, h2=h2, gu=gu, act=act, fo=fo)
    return out, saved


def _layer_bwd(dout, sv, tables, w, sm, tag):
    nm = lambda n: f"{n}_{tag}"
    gb, gs = {}, {}
    dfo, gs["ffn_norm_post"] = _rms_bwd(sv["fo"], (D_MODEL, 0), sm["ffn_norm_post"], dout, BF16, nm("ffn_post_norm_bwd"))
    dact = _matmul(dfo, w["w_down"], "nt", F32, nm("ffn_down_dx"))
    gb["w_down"] = _matmul(sv["act"], dfo, "tn", F32, nm("ffn_down_dw"))
    dgu = _swiglu_bwd(sv["gu"], dact, nm("swiglu_bwd"))
    dh2 = _matmul(dgu, w["w_gu"], "nt", F32, nm("ffn_gate_up_dx"))
    gb["w_gu"] = _matmul(sv["h2"], dgu, "tn", F32, nm("ffn_gate_up_dw"))
    dmid, gs["ffn_norm_pre"] = _rms_bwd(sv["x_mid"], (D_MODEL, 0), sm["ffn_norm_pre"], dh2, F32, nm("ffn_pre_norm_bwd"),
                                        add=dout)
    dmo, gs["mix_norm_post"] = _rms_bwd(sv["mo"], (D_MODEL, 0), sm["mix_norm_post"], dmid, BF16, nm("mix_post_norm_bwd"))
    dmerged = _matmul(dmo, w["w_mix_o"], "nt", F32, nm("mix_out_dx"))
    gb["w_mix_o"] = _matmul(sv["merged"], dmo, "tn", F32, nm("mix_out_dw"))
    dya, dyc, dyp, dg0, dg1, dg2 = _merge_bwd(sv["z"], sv["ys"], dmerged, nm("merge_bwd"))
    dpm = _matmul(dyp, w["w_pool_o"], "nt", F32, nm("pool_out_dx"))
    gb["w_pool_o"] = _matmul(sv["pm"], dyp, "tn", F32, nm("pool_out_dw"))
    du_pool, gs["pool_w"], gs["pool_scale"] = _pool_bwd(dpm, sv["z"], sm["pool_w"], sm["pool_scale"], nm("pool_bwd"))
    dhc = _matmul(dyc, w["w_conv_o"], "nt", F32, nm("conv_out_dx"))
    gb["w_conv_o"] = _matmul(sv["hc"], dyc, "tn", F32, nm("conv_out_dw"))
    dco, gs["conv_ln_g"], gs["conv_ln_b"], gs["conv_b"] = _conv_bwd_norm(dhc, sv["co"], sm["conv_ln_g"], sm["conv_ln_b"],
                                                                        nm("conv_bwd_norm"))
    du_conv, dcw = _conv_bwd_taps(dco, sv["z"], sm["conv_w"], nm("conv_bwd_taps"))
    gs["conv_w"] = dcw[:CONV_W]
    do = _matmul(dya, w["w_attn_o"], "nt", F32, nm("attn_out_dx"))
    gb["w_attn_o"] = _matmul(sv["o"], dya, "tn", F32, nm("attn_out_dw"))
    delta, dob = _attn_delta(do, sv["o"], nm("attn_delta"))
    dk, dv = _flash_bwd_kv(sv["q"], sv["k"], sv["v"], dob, sv["lse"], delta, nm("flash_bwd_kv"))
    dq = _flash_bwd_q(sv["q"], sv["k"], sv["v"], dob, sv["lse"], delta, nm("flash_bwd_q"))
    dqf, dkf, dkr = _rope_qk_bwd(dq, dk, tables, nm("rope_qk_bwd"))
    dcq_n = _matmul(dqf, w["w_uq"], "nt", F32, nm("q_up_dx"))
    gb["w_uq"] = _matmul(sv["cq"], dqf, "tn", F32, nm("q_up_dw"))
    dckv_k = _matmul(dkf, w["w_uk"], "nt", F32, nm("k_up_dx"))
    dckv_n = _matmul(dv, w["w_uv"], "nt", F32, nm("v_up_dx"), add=dckv_k)
    gb["w_uk"] = _matmul(sv["ckv"], dkf, "tn", F32, nm("k_up_dw"))
    gb["w_uv"] = _matmul(sv["ckv"], dv, "tn", F32, nm("v_up_dw"))
    dcq, gs["q_norm"] = _rms_bwd(sv["z"], ZC_Q, sm["q_norm"], dcq_n, BF16, nm("q_norm_bwd"))
    dckv, gs["kv_norm"] = _rms_bwd(sv["z"], ZC_KV, sm["kv_norm"], dckv_n, BF16, nm("kv_norm_bwd"))
    dz = jnp.concatenate([dcq, dkr, du_pool, du_conv, dg0, dg1, dg2, dckv], axis=1)
    dh = _matmul(dz, w["w_in"], "nt", F32, nm("in_proj_dx"))
    gb["w_in"] = _matmul(sv["h"], dz, "tn", F32, nm("in_proj_dw"))
    dx, gs["mix_norm_pre"] = _rms_bwd(sv["x"], (D_MODEL, 0), sm["mix_norm_pre"], dh, F32, nm("mix_pre_norm_bwd"), add=dmid)
    return dx, gb, gs


def _local_step(x, positions, target, weights, smalls):
    tables = _rope_tables(positions)
    saved = []
    h = x
    for l in range(DEPTH):
        h, sv = _layer_fwd(h, tables, weights[l], smalls[l], f"l{l}")
        saved.append(sv)
    dy, sq = _loss_grad(h, target, "loss_grad")
    big, small = [None] * DEPTH, [None] * DEPTH
    for l in reversed(range(DEPTH)):
        dy, big[l], small[l] = _layer_bwd(dy, saved[l], tables, weights[l], smalls[l], f"l{l}")
    return sq, dy, big, small


def _pad_rows(flat, multiple):
    n = flat.shape[-1]
    pad = (-n) % multiple
    if pad:
        flat = jnp.pad(flat, [(0, 0)] * (flat.ndim - 1) + [(0, pad)])
    return flat


def _gather_weights(shards):
    flat = jnp.concatenate([shards[n][l].astype(BF16).reshape(-1) for l in range(DEPTH) for n in BIG])
    slab = _pad_rows(flat, 16 * LANES).reshape(-1, LANES)
    g = _all_gather(slab, "gather_weights").reshape(N_DEV, -1)
    out, off = [], 0
    for l in range(DEPTH):
        layer = {}
        for n in BIG:
            r, c = shards[n].shape[1:]
            piece = g[:, off:off + r * c].reshape(N_DEV, r, c)
            off += r * c
            layer[n] = piece.reshape(N_DEV * r, c) if n in ROW_SHARDED else piece.transpose(1, 0, 2).reshape(r, N_DEV * c)
        out.append(layer)
    return out


def _scatter_grads(grads, shards):
    parts = []
    for l in range(DEPTH):
        for n in BIG:
            r, c = shards[n].shape[1:]
            gfull = grads[l][n].astype(BF16)
            if n in ROW_SHARDED:
                parts.append(gfull.reshape(N_DEV, r * c))
            else:
                parts.append(gfull.reshape(r, N_DEV, c).transpose(1, 0, 2).reshape(N_DEV, r * c))
    flat = _pad_rows(jnp.concatenate(parts, axis=1), 16 * LANES)
    rows = flat.shape[1] // LANES
    p = flat.reshape(4, 2, rows, LANES).transpose(1, 0, 2, 3)
    own_half = lax.dynamic_index_in_dim(p, lax.axis_index("c"), axis=0, keepdims=False)
    pair = _add_pairs(own_half, _swap_with_sibling(p, "reduce_d2d"), "reduce_pair_add")
    total = _sum_blocks(_exchange_chips(pair, "reduce_ici"), "reduce_chip_add").reshape(-1)
    out, off = {n: [] for n in BIG}, 0
    for l in range(DEPTH):
        for n in BIG:
            r, c = shards[n].shape[1:]
            out[n].append(total[off:off + r * c].reshape(r, c))
            off += r * c
    return {n: jnp.stack(v) for n, v in out.items()}


def _all_reduce_small(small):
    flat = jnp.concatenate([small[l][n].reshape(-1) for l in range(DEPTH) for n in SMALL])
    slab = _pad_rows(flat, SUBLANES * LANES).reshape(-1, LANES)
    total = _sum_blocks(_all_gather(slab, "gather_small_grads"), "sum_small_grads").reshape(-1)
    out, off = {n: [] for n in SMALL}, 0
    for l in range(DEPTH):
        for n in SMALL:
            shape = small[l][n].shape
            size = math.prod(shape)
            out[n].append(total[off:off + size].reshape(shape))
            off += size
    return {n: jnp.stack(v) for n, v in out.items()}


def kernel(x, positions, mix_norm_pre, w_in, q_norm, w_uq, kv_norm, w_uk, w_uv, w_attn_o, conv_w, conv_b, conv_ln_g, conv_ln_b, w_conv_o, pool_w, pool_scale, w_pool_o, w_mix_o, mix_norm_post, ffn_norm_pre, w_gate, w_up, w_down, ffn_norm_post, loss_target, m_mix_norm_pre, m_w_in, m_q_norm, m_w_uq, m_kv_norm, m_w_uk, m_w_uv, m_w_attn_o, m_conv_w, m_conv_b, m_conv_ln_g, m_conv_ln_b, m_w_conv_o, m_pool_w, m_pool_scale, m_w_pool_o, m_w_mix_o, m_mix_norm_post, m_ffn_norm_pre, m_w_gate, m_w_up, m_w_down, m_ffn_norm_post, v_mix_norm_pre, v_w_in, v_q_norm, v_w_uq, v_kv_norm, v_w_uk, v_w_uv, v_w_attn_o, v_conv_w, v_conv_b, v_conv_ln_g, v_conv_ln_b, v_w_conv_o, v_pool_w, v_pool_scale, v_w_pool_o, v_w_mix_o, v_mix_norm_post, v_ffn_norm_pre, v_w_gate, v_w_up, v_w_down, v_ffn_norm_post):
    given = dict(locals())
    dev = 4 * lax.axis_index("x") + 2 * lax.axis_index("y") + lax.axis_index("c")
    shards = {n: given[n] for n in BIG}

    full = _gather_weights(shards)
    weights = [_layer_weights(full[l]) for l in range(DEPTH)]
    conv_w_full = _all_gather(_pad_rows(conv_w.reshape(-1), SUBLANES * LANES).reshape(-1, LANES), "gather_conv_w")
    conv_w_full = conv_w_full.reshape(N_DEV, -1)[:, :DEPTH * CONV_W * (CONV_C // N_DEV)]
    conv_w_full = conv_w_full.reshape(N_DEV, DEPTH, CONV_W, CONV_C // N_DEV).transpose(1, 2, 0, 3).reshape(DEPTH, CONV_W, CONV_C)
    smalls = []
    for l in range(DEPTH):
        sm = {n: given[n][l] for n in SMALL if n != "conv_w"}
        sm["conv_w"] = jnp.pad(conv_w_full[l], ((0, CONV_HALO - CONV_W), (0, 0)))
        smalls.append(sm)

    sq, grad_x, big, small = _local_step(x[0], positions[0], loss_target[0], weights, smalls)
    loss = lax.psum(0.5 / D_MODEL * jnp.sum(sq), ("x", "y", "c"))

    grads = _scatter_grads([_layer_grads_to_reference_layout(big[l]) for l in range(DEPTH)], shards)
    small_sum = _all_reduce_small(small)
    for n in SMALL:
        grads[n] = small_sum[n]
    cw = CONV_C // N_DEV
    grads["conv_w"] = lax.dynamic_slice_in_dim(small_sum["conv_w"], dev * cw, cw, axis=2)

    delta, new_m, new_v = {}, {}, {}
    for n in WEIGHTS:
        delta[n], new_m[n], new_v[n] = _adamw(given[n], grads[n], given["m_" + n], given["v_" + n], f"adamw_{n}")
    return (loss, grad_x[None], *[grads[n] for n in WEIGHTS], *[delta[n] for n in WEIGHTS],
            *[new_m[n] for n in WEIGHTS], *[new_v[n] for n in WEIGHTS])
```

```python
import math

import jax
import jax.numpy as jnp
from jax import lax
from jax.experimental import pallas as pl
from jax.experimental.pallas import tpu as pltpu

F32, BF16 = jnp.float32, jnp.bfloat16
MESH = pl.DeviceIdType.MESH

LANES = 128
SUBLANES = 8
VMEM_LIMIT_BYTES = 56 * 1024 * 1024

N_DEV = 8
D_MODEL = 1024
DEPTH = 2
N_HEADS = 8
QK_NOPE, QK_ROPE, V_HEAD = 64, 32, 64
HEAD_PAD = LANES
Q_RANK, KV_RANK = 384, 256
ROPE_THETA = 10000.0
CONV_C, CONV_W = 512, 31
CONV_HALO = 32
POOL_WINDOWS = (2, 4, 8, 16)
POOL_C, POOL_G = 512, 4
POOL_GD = POOL_C // POOL_G
D_FF = 2816
FF_SHARD = D_FF // N_DEV
FF_SHARD_PAD = 3 * LANES
D_FF_PAD = N_DEV * FF_SHARD_PAD
W_IN_SHARD = 660
EPS = 1e-6
ATTN_SCALE = 1.0 / math.sqrt(QK_NOPE + QK_ROPE)
LOG2E = 1.4426950408889634
LR, B1, B2, ADAM_EPS, WD, STEP = 0.001, 0.9, 0.999, 1e-08, 0.01, 10

Z_W = 5376
ZC_Q = (384, 0)
ZC_KR = (128, 3)
ZC_POOL = (512, 1)
ZC_CONV_A = (512, 2)
ZC_CONV_G = (512, 3)
ZC_GATE = (1024, 2)
ZC_KV = (256, 20)
W_IN_PIECES = ((0, 384, 0), (384, 640, 5120), (640, 672, 448), (672, 1696, 1024), (1696, 2208, 512), (2208, 5280, 2048))

BIG = ("w_in", "w_uq", "w_uk", "w_uv", "w_attn_o", "w_conv_o", "w_pool_o", "w_mix_o", "w_gate", "w_up", "w_down")
SMALL = ("mix_norm_pre", "q_norm", "kv_norm", "conv_w", "conv_b", "conv_ln_g", "conv_ln_b", "pool_w", "pool_scale",
         "mix_norm_post", "ffn_norm_pre", "ffn_norm_post")
WEIGHTS = ("mix_norm_pre", "w_in", "q_norm", "w_uq", "kv_norm", "w_uk", "w_uv", "w_attn_o", "conv_w", "conv_b",
           "conv_ln_g", "conv_ln_b", "w_conv_o", "pool_w", "pool_scale", "w_pool_o", "w_mix_o", "mix_norm_post",
           "ffn_norm_pre", "w_gate", "w_up", "w_down", "ffn_norm_post")


def _params(*semantics):
    return pltpu.CompilerParams(dimension_semantics=semantics, vmem_limit_bytes=VMEM_LIMIT_BYTES)


def _tile(dim, cap):
    if dim <= cap:
        return dim
    for t in range(cap - cap % LANES, 0, -LANES):
        if dim % t == 0:
            return t
    raise ValueError(f"no tile for {dim} under {cap}")


def _row_tile(rows, row_bytes, budget=1 << 20):
    if rows * row_bytes <= budget:
        return rows
    cap = max(16, budget // row_bytes)
    for t in range(cap - cap % 16, 0, -16):
        if rows % t == 0:
            return t
    return rows


def _rows(ts, width, cidx=0):
    return pl.BlockSpec((ts, width), lambda i: (i, cidx))


def _fixed(shape):
    return pl.BlockSpec(shape, lambda *_: (0,) * len(shape))


def _sigmoid(x):
    return 1.0 / (1.0 + jnp.exp(-x))


def _matmul(a, b, mode, out_dtype, name, add=None):
    if mode == "nn":
        (m, k), n = a.shape, b.shape[1]
    elif mode == "nt":
        (m, k), n = a.shape, b.shape[0]
    else:
        (k, m), n = a.shape, b.shape[1]
    tm, tn, tk = _tile(m, 1024), _tile(n, 1408), _tile(k, 1408 if mode != "tn" else 1024)
    nk = k // tk
    dims = {"nn": ((1,), (0,)), "nt": ((1,), (1,)), "tn": ((0,), (0,))}[mode]
    a_spec = {"nn": pl.BlockSpec((tm, tk), lambda i, j, s: (i, s)), "nt": pl.BlockSpec((tm, tk), lambda i, j, s: (i, s)),
              "tn": pl.BlockSpec((tk, tm), lambda i, j, s: (s, i))}[mode]
    b_spec = {"nn": pl.BlockSpec((tk, tn), lambda i, j, s: (s, j)), "nt": pl.BlockSpec((tn, tk), lambda i, j, s: (j, s)),
              "tn": pl.BlockSpec((tk, tn), lambda i, j, s: (s, j))}[mode]
    o_spec = pl.BlockSpec((tm, tn), lambda i, j, s: (i, j))
    has_add = add is not None

    def body(a_ref, b_ref, *rest):
        add_ref = rest[0] if has_add else None
        o_ref = rest[1] if has_add else rest[0]
        part = lax.dot_general(a_ref[...], b_ref[...], (dims, ((), ())), preferred_element_type=F32)

        def finish(total):
            if has_add:
                total = total + add_ref[...]
            o_ref[...] = total.astype(o_ref.dtype)

        if nk == 1:
            finish(part)
        else:
            acc = rest[-1]
            step = pl.program_id(2)

            @pl.when(step == 0)
            def _():
                acc[...] = part

            @pl.when(step > 0)
            def _():
                acc[...] += part

            @pl.when(step == nk - 1)
            def _():
                finish(acc[...])

    operands = (a, b, add) if has_add else (a, b)
    return pl.pallas_call(
        body, name=name, out_shape=jax.ShapeDtypeStruct((m, n), out_dtype), grid=(m // tm, n // tn, nk),
        in_specs=[a_spec, b_spec] + ([o_spec] if has_add else []), out_specs=o_spec,
        scratch_shapes=[pltpu.VMEM((tm, tn), F32)] if nk > 1 else [],
        compiler_params=_params("parallel", "parallel", "arbitrary"))(*operands)


def _rms_fwd(x, win, gain, out_dtype, name, res=None):
    width, cidx = win
    s = x.shape[0]
    ts = min(s, 512)
    has_res = res is not None

    def body(x_ref, g_ref, *rest):
        o_ref = rest[-1]
        xv = x_ref[...]
        r = lax.rsqrt(jnp.mean(xv * xv, axis=-1, keepdims=True) + EPS)
        y = (xv * r) * g_ref[...]
        if has_res:
            y = rest[0][...] + y
        o_ref[...] = y.astype(o_ref.dtype)

    ops = (x, gain.reshape(1, width)) + ((res,) if has_res else ())
    return pl.pallas_call(
        body, name=name, out_shape=jax.ShapeDtypeStruct((s, width), out_dtype), grid=(s // ts,),
        in_specs=[_rows(ts, width, cidx), _fixed((1, width))] + ([_rows(ts, width)] if has_res else []),
        out_specs=_rows(ts, width), compiler_params=_params("parallel"))(*ops)


def _rms_bwd(x, win, gain, dy, out_dtype, name, add=None):
    width, cidx = win
    s = x.shape[0]
    ts = min(s, 512)
    has_add = add is not None

    def body(x_ref, g_ref, dy_ref, *rest):
        dx_ref, dg_ref = rest[-2], rest[-1]
        xv = x_ref[...]
        r = lax.rsqrt(jnp.mean(xv * xv, axis=-1, keepdims=True) + EPS)
        xh = xv * r
        dyv = dy_ref[...].astype(F32)
        dyg = dyv * g_ref[...]
        dx = r * (dyg - xh * jnp.mean(dyg * xh, axis=-1, keepdims=True))
        if has_add:
            dx = dx + rest[0][...]
        dx_ref[...] = dx.astype(dx_ref.dtype)

        @pl.when(pl.program_id(0) == 0)
        def _():
            dg_ref[...] = jnp.zeros_like(dg_ref)

        dg_ref[...] += jnp.sum(dyv * xh, axis=0, keepdims=True)

    ops = (x, gain.reshape(1, width), dy) + ((add,) if has_add else ())
    dx, dg = pl.pallas_call(
        body, name=name,
        out_shape=(jax.ShapeDtypeStruct((s, width), out_dtype), jax.ShapeDtypeStruct((1, width), F32)), grid=(s // ts,),
        in_specs=[_rows(ts, width, cidx), _fixed((1, width)), _rows(ts, width)] + ([_rows(ts, width)] if has_add else []),
        out_specs=(_rows(ts, width), _fixed((1, width))), compiler_params=_params("arbitrary"))(*ops)
    return dx, dg.reshape(width)


def _rope(x, c, s1, s2):
    return x * c + pltpu.roll(x, 16, 1) * s1 + pltpu.roll(x, LANES - 16, 1) * s2


def _rope_t(g, c, s1, s2):
    return g * c + pltpu.roll(g * s1, LANES - 16, 1) + pltpu.roll(g * s2, 16, 1)


def _rope_tables(positions):
    inv_freq = ROPE_THETA ** (-jnp.arange(0, QK_ROPE, 2, dtype=F32) / QK_ROPE)
    ang = positions.astype(F32)[:, None] * inv_freq
    cos, sin = jnp.cos(ang), jnp.sin(ang)
    n = positions.shape[0]
    one, zero = jnp.ones((n, 1), F32), jnp.zeros((n, 1), F32)
    c = jnp.concatenate([jnp.tile(one, (1, QK_NOPE)), cos, cos, jnp.tile(one, (1, 32))], axis=1)
    s1 = jnp.concatenate([jnp.tile(zero, (1, QK_NOPE + 16)), sin, jnp.tile(zero, (1, 32))], axis=1)
    s2 = jnp.concatenate([jnp.tile(zero, (1, QK_NOPE)), -sin, jnp.tile(zero, (1, 48))], axis=1)
    return c, s1, s2


def _rope_qk_fwd(qf, kf, z, tables, name):
    s = qf.shape[0]
    ts = min(s, 256)
    hw = N_HEADS * HEAD_PAD

    def body(qf_ref, kf_ref, kr_ref, c_ref, s1_ref, s2_ref, q_ref, k_ref):
        c, s1, s2 = c_ref[...], s1_ref[...], s2_ref[...]
        kr = _rope(kr_ref[...], c, s1, s2)
        for h in range(N_HEADS):
            sl = slice(h * HEAD_PAD, (h + 1) * HEAD_PAD)
            q_ref[:, sl] = _rope(qf_ref[:, sl], c, s1, s2).astype(BF16)
            k_ref[:, sl] = (kf_ref[:, sl] + kr).astype(BF16)

    tab = _rows(ts, LANES)
    return pl.pallas_call(
        body, name=name, out_shape=(jax.ShapeDtypeStruct((s, hw), BF16),) * 2, grid=(s // ts,),
        in_specs=[_rows(ts, hw), _rows(ts, hw), _rows(ts, *ZC_KR), tab, tab, tab],
        out_specs=(_rows(ts, hw), _rows(ts, hw)), compiler_params=_params("parallel"))(qf, kf, z, *tables)


def _rope_qk_bwd(dq, dk, tables, name):
    s = dq.shape[0]
    ts = min(s, 256)
    hw = N_HEADS * HEAD_PAD

    def body(dq_ref, dk_ref, c_ref, s1_ref, s2_ref, dqf_ref, dkf_ref, dkr_ref):
        c, s1, s2 = c_ref[...], s1_ref[...], s2_ref[...]
        ksum = jnp.zeros((ts, HEAD_PAD), F32)
        for h in range(N_HEADS):
            sl = slice(h * HEAD_PAD, (h + 1) * HEAD_PAD)
            dqf_ref[:, sl] = _rope_t(dq_ref[:, sl], c, s1, s2).astype(BF16)
            dkh = dk_ref[:, sl]
            dkf_ref[:, sl] = dkh.astype(BF16)
            ksum = ksum + dkh
        lane = lax.broadcasted_iota(jnp.int32, (ts, HEAD_PAD), 1)
        in_rope = (lane >= QK_NOPE) & (lane < QK_NOPE + QK_ROPE)
        dkr_ref[...] = jnp.where(in_rope, _rope_t(ksum, c, s1, s2), 0.0).astype(BF16)

    tab = _rows(ts, LANES)
    return pl.pallas_call(
        body, name=name,
        out_shape=(jax.ShapeDtypeStruct((s, hw), BF16), jax.ShapeDtypeStruct((s, hw), BF16),
                   jax.ShapeDtypeStruct((s, LANES), BF16)), grid=(s // ts,),
        in_specs=[_rows(ts, hw), _rows(ts, hw), tab, tab, tab],
        out_specs=(_rows(ts, hw), _rows(ts, hw), _rows(ts, LANES)), compiler_params=_params("parallel"))(dq, dk, *tables)


def _attn_tile(s):
    return min(s, 512)


def _raw_scores(q, k, masked):
    sc = lax.dot_general(q, k, (((1,), (1,)), ((), ())), preferred_element_type=F32)
    if masked:
        rows = lax.broadcasted_iota(jnp.int32, sc.shape, 0)
        cols = lax.broadcasted_iota(jnp.int32, sc.shape, 1)
        sc = jnp.where(cols <= rows, sc, -jnp.inf)
    return sc


def _flash_fwd(q, k, v, name):
    s = q.shape[0]
    t = _attn_tile(s)
    c2 = ATTN_SCALE * LOG2E

    def body(q_ref, k_ref, v_ref, o_ref, lse_ref):
        i = pl.program_id(1)
        qv = q_ref[...]

        def chunk(j, carry, masked):
            m_old, l_old, acc = carry
            at = pl.ds(pl.multiple_of(j * t, t), t)
            sc = _raw_scores(qv, k_ref[at, :], masked)
            m_new = jnp.maximum(m_old, jnp.max(sc, axis=-1, keepdims=True))
            p = jnp.exp2((sc - m_new) * c2)
            alpha = jnp.exp2((m_old - m_new) * c2)
            l_new = alpha * l_old + jnp.sum(p, axis=-1, keepdims=True)
            acc = alpha * acc + jnp.dot(p.astype(BF16), v_ref[at, :], preferred_element_type=F32)
            return m_new, l_new, acc

        init = (jnp.full((t, 1), -jnp.inf, F32), jnp.zeros((t, 1), F32), jnp.zeros((t, HEAD_PAD), F32))
        carry = lax.fori_loop(0, i, lambda j, cr: chunk(j, cr, False), init)
        m_fin, l_fin, acc = chunk(i, carry, True)
        o_ref[...] = (acc / l_fin).astype(o_ref.dtype)
        lse_ref[0] = m_fin * ATTN_SCALE + jnp.log(l_fin)

    qo = pl.BlockSpec((t, HEAD_PAD), lambda h, i: (i, h))
    whole = pl.BlockSpec((s, HEAD_PAD), lambda h, i: (0, h))
    return pl.pallas_call(
        body, name=name,
        out_shape=(jax.ShapeDtypeStruct(q.shape, BF16), jax.ShapeDtypeStruct((N_HEADS, s, 1), F32)),
        grid=(N_HEADS, s // t), in_specs=[qo, whole, whole],
        out_specs=(qo, pl.BlockSpec((1, t, 1), lambda h, i: (h, i, 0))),
        compiler_params=_params("parallel", "parallel"))(q, k, v)


def _attn_delta(do, o, name):
    s = o.shape[0]
    t = _attn_tile(s)

    def body(do_ref, o_ref, delta_ref, dob_ref):
        dov = do_ref[...]
        delta_ref[0] = jnp.sum(dov * o_ref[...].astype(F32), axis=-1, keepdims=True)
        dob_ref[...] = dov.astype(BF16)

    blk = pl.BlockSpec((t, HEAD_PAD), lambda i, h: (i, h))
    return pl.pallas_call(
        body, name=name,
        out_shape=(jax.ShapeDtypeStruct((N_HEADS, s, 1), F32), jax.ShapeDtypeStruct(o.shape, BF16)),
        grid=(s // t, N_HEADS), in_specs=[blk, blk],
        out_specs=(pl.BlockSpec((1, t, 1), lambda i, h: (h, i, 0)), blk),
        compiler_params=_params("parallel", "parallel"))(do, o)


def _flash_bwd(q, k, v, do, lse, delta, name):
    s = q.shape[0]
    t = _attn_tile(s)
    nt = s // t
    c2 = ATTN_SCALE * LOG2E

    def body(q_ref, k_ref, v_ref, do_ref, lse_ref, delta_ref, dq_ref, dk_ref, dv_ref):
        j = pl.program_id(1)
        kv, vv = k_ref[...], v_ref[...]

        @pl.when(j == 0)
        def _():
            dq_ref[...] = jnp.zeros_like(dq_ref)

        def chunk(i, carry, masked):
            dk_acc, dv_acc = carry
            at = pl.ds(pl.multiple_of(i * t, t), t)
            qi, doi = q_ref[at, :], do_ref[at, :]
            sc = _raw_scores(qi, kv, masked)
            p = jnp.exp2(sc * c2 - lse_ref[0, at, :] * LOG2E)
            dp = lax.dot_general(doi, vv, (((1,), (1,)), ((), ())), preferred_element_type=F32)
            ds = (p * (dp - delta_ref[0, at, :])).astype(BF16)
            dv_acc = dv_acc + lax.dot_general(p.astype(BF16), doi, (((0,), (0,)), ((), ())), preferred_element_type=F32)
            dk_acc = dk_acc + lax.dot_general(ds, qi, (((0,), (0,)), ((), ())), preferred_element_type=F32)
            dq_ref[at, :] += jnp.dot(ds, kv, preferred_element_type=F32) * ATTN_SCALE
            return dk_acc, dv_acc

        zero = jnp.zeros((t, HEAD_PAD), F32)
        carry = chunk(j, (zero, zero), True)
        dk_acc, dv_acc = lax.fori_loop(j + 1, nt, lambda i, cr: chunk(i, cr, False), carry)
        dk_ref[...] = dk_acc * ATTN_SCALE
        dv_ref[...] = dv_acc.astype(BF16)

    blk = pl.BlockSpec((t, HEAD_PAD), lambda h, j: (j, h))
    whole = pl.BlockSpec((s, HEAD_PAD), lambda h, j: (0, h))
    stat = pl.BlockSpec((1, s, 1), lambda h, j: (h, 0, 0))
    return pl.pallas_call(
        body, name=name,
        out_shape=(jax.ShapeDtypeStruct(q.shape, F32), jax.ShapeDtypeStruct(q.shape, F32),
                   jax.ShapeDtypeStruct(q.shape, BF16)),
        grid=(N_HEADS, nt), in_specs=[whole, blk, blk, whole, stat, stat], out_specs=(whole, blk, blk),
        compiler_params=_params("parallel", "arbitrary"))(q, k, v, do, lse, delta)


def _conv_tile(s):
    return min(s, 256)


def _halo_before(t, width, cidx):
    per = t // CONV_HALO
    return pl.BlockSpec((CONV_HALO, width), lambda i: (jnp.maximum(i * per - 1, 0), cidx))


def _halo_after(t, width, cidx, n_tiles):
    per = t // CONV_HALO
    last = n_tiles * per - 1
    return pl.BlockSpec((CONV_HALO, width), lambda i: (jnp.minimum((i + 1) * per, last), cidx))


def _fill_glu(hbuf, ap_ref, gp_ref, a_ref, g_ref, t):
    first = pl.program_id(0) == 0
    hbuf[pl.ds(0, CONV_HALO), :] = jnp.where(first, 0.0, ap_ref[...] * _sigmoid(gp_ref[...]))
    hbuf[pl.ds(CONV_HALO, t), :] = a_ref[...] * _sigmoid(g_ref[...])


def _layer_norm_parts(co):
    mu = jnp.mean(co, axis=-1, keepdims=True)
    xc = co - mu
    rstd = lax.rsqrt(jnp.mean(xc * xc, axis=-1, keepdims=True) + EPS)
    return xc * rstd, rstd


def _conv_fwd(z, conv_w, conv_b, ln_g, ln_b, name):
    s = z.shape[0]
    t = _conv_tile(s)
    off = CONV_HALO - (CONV_W - 1)

    def body(ap_ref, gp_ref, a_ref, g_ref, w_ref, b_ref, lg_ref, lb_ref, hc_ref, co_ref, hbuf):
        _fill_glu(hbuf, ap_ref, gp_ref, a_ref, g_ref, t)
        acc = jnp.zeros((t, CONV_C), F32) + b_ref[...]
        for j in range(CONV_W):
            acc = acc + hbuf[pl.ds(off + j, t), :] * w_ref[pl.ds(j, 1), :]
        co_ref[...] = acc
        xh, _ = _layer_norm_parts(acc)
        y = xh * lg_ref[...] + lb_ref[...]
        hc_ref[...] = (y * _sigmoid(y)).astype(BF16)

    vec = _fixed((1, CONV_C))
    return pl.pallas_call(
        body, name=name, out_shape=(jax.ShapeDtypeStruct((s, CONV_C), BF16), jax.ShapeDtypeStruct((s, CONV_C), F32)),
        grid=(s // t,),
        in_specs=[_halo_before(t, *ZC_CONV_A), _halo_before(t, *ZC_CONV_G), _rows(t, *ZC_CONV_A), _rows(t, *ZC_CONV_G),
                  _fixed((CONV_HALO, CONV_C)), vec, vec, vec],
        out_specs=(_rows(t, CONV_C), _rows(t, CONV_C)), scratch_shapes=[pltpu.VMEM((t + CONV_HALO, CONV_C), F32)],
        compiler_params=_params("parallel"))(z, z, z, z, conv_w, conv_b.reshape(1, -1), ln_g.reshape(1, -1),
                                             ln_b.reshape(1, -1))


def _conv_bwd_norm(dhc, co, ln_g, ln_b, name):
    s = co.shape[0]
    t = min(s, 512)

    def body(dhc_ref, co_ref, lg_ref, lb_ref, dco_ref, dg_ref, db_ref, dcb_ref):
        xh, rstd = _layer_norm_parts(co_ref[...])
        y = xh * lg_ref[...] + lb_ref[...]
        sg = _sigmoid(y)
        dy = dhc_ref[...] * (sg * (1.0 + y * (1.0 - sg)))
        dxh = dy * lg_ref[...]
        dco = rstd * (dxh - jnp.mean(dxh, axis=-1, keepdims=True) - xh * jnp.mean(dxh * xh, axis=-1, keepdims=True))
        dco_ref[...] = dco

        @pl.when(pl.program_id(0) == 0)
        def _():
            dg_ref[...] = jnp.zeros_like(dg_ref)
            db_ref[...] = jnp.zeros_like(db_ref)
            dcb_ref[...] = jnp.zeros_like(dcb_ref)

        dg_ref[...] += jnp.sum(dy * xh, axis=0, keepdims=True)
        db_ref[...] += jnp.sum(dy, axis=0, keepdims=True)
        dcb_ref[...] += jnp.sum(dco, axis=0, keepdims=True)

    vec = _fixed((1, CONV_C))
    one = jax.ShapeDtypeStruct((1, CONV_C), F32)
    dco, dg, db, dcb = pl.pallas_call(
        body, name=name, out_shape=(jax.ShapeDtypeStruct((s, CONV_C), F32), one, one, one), grid=(s // t,),
        in_specs=[_rows(t, CONV_C), _rows(t, CONV_C), vec, vec], out_specs=(_rows(t, CONV_C), vec, vec, vec),
        compiler_params=_params("arbitrary"))(dhc, co, ln_g.reshape(1, -1), ln_b.reshape(1, -1))
    return dco, dg.reshape(-1), db.reshape(-1), dcb.reshape(-1)


def _conv_bwd_taps(dco, z, conv_w, name):
    s = z.shape[0]
    t = _conv_tile(s)
    nt = s // t
    off = CONV_HALO - (CONV_W - 1)

    def body(ap_ref, gp_ref, a_ref, g_ref, d_ref, dn_ref, w_ref, du_ref, dw_ref, hbuf, dbuf):
        i = pl.program_id(0)
        _fill_glu(hbuf, ap_ref, gp_ref, a_ref, g_ref, t)
        dbuf[pl.ds(0, t), :] = d_ref[...]
        dbuf[pl.ds(t, CONV_HALO), :] = jnp.where(i == nt - 1, 0.0, dn_ref[...])

        @pl.when(i == 0)
        def _():
            dw_ref[...] = jnp.zeros_like(dw_ref)

        dcur = d_ref[...]
        dh = jnp.zeros((t, CONV_C), F32)
        for j in range(CONV_W):
            dh = dh + dbuf[pl.ds(CONV_W - 1 - j, t), :] * w_ref[pl.ds(j, 1), :]
            dw_ref[pl.ds(j, 1), :] += jnp.sum(dcur * hbuf[pl.ds(off + j, t), :], axis=0, keepdims=True)
        a, sg = a_ref[...], _sigmoid(g_ref[...])
        du_ref[:, pl.ds(0, CONV_C)] = (dh * sg).astype(BF16)
        du_ref[:, pl.ds(CONV_C, CONV_C)] = (dh * a * sg * (1.0 - sg)).astype(BF16)

    return pl.pallas_call(
        body, name=name,
        out_shape=(jax.ShapeDtypeStruct((s, 2 * CONV_C), BF16), jax.ShapeDtypeStruct((CONV_HALO, CONV_C), F32)),
        grid=(nt,),
        in_specs=[_halo_before(t, *ZC_CONV_A), _halo_before(t, *ZC_CONV_G), _rows(t, *ZC_CONV_A), _rows(t, *ZC_CONV_G),
                  _rows(t, CONV_C), _halo_after(t, CONV_C, 0, nt), _fixed((CONV_HALO, CONV_C))],
        out_specs=(_rows(t, 2 * CONV_C), _fixed((CONV_HALO, CONV_C))),
        scratch_shapes=[pltpu.VMEM((t + CONV_HALO, CONV_C), F32), pltpu.VMEM((t + CONV_HALO, CONV_C), F32)],
        compiler_params=_params("arbitrary"))(z, z, z, z, dco, dco, conv_w)


def _pool_tile(s):
    return min(s, 512)


def _pool_counts(row0, n, window):
    rows = row0 + lax.broadcasted_iota(jnp.int32, (n, POOL_GD), 0)
    return jnp.minimum(rows + 1, window).astype(F32)


def _pool_diff(ubuf, gi, window, row0, t):
    lanes = pl.ds(gi * POOL_GD, POOL_GD)
    tot = ubuf[pl.ds(CONV_HALO, t), lanes]
    cur = tot
    for back in range(1, window):
        tot = tot + ubuf[pl.ds(CONV_HALO - back, t), lanes]
    return tot / _pool_counts(row0, t, window) - cur


def _pool_fwd(z, pool_w, pool_scale, name):
    s = z.shape[0]
    t = _pool_tile(s)

    def body(up_ref, u_ref, w_ref, sc_ref, m_ref, ubuf):
        i = pl.program_id(0)
        ubuf[pl.ds(0, CONV_HALO), :] = jnp.where(i == 0, 0.0, up_ref[...])
        ubuf[pl.ds(CONV_HALO, t), :] = u_ref[...]
        for gi, window in enumerate(POOL_WINDOWS):
            d = _pool_diff(ubuf, gi, window, i * t, t)
            mm = jnp.dot(d.astype(BF16), w_ref[gi].astype(BF16), preferred_element_type=F32)
            lanes = pl.ds(gi * POOL_GD, POOL_GD)
            m_ref[:, lanes] = (mm * sc_ref[:, lanes]).astype(BF16)

    return pl.pallas_call(
        body, name=name, out_shape=jax.ShapeDtypeStruct((s, POOL_C), BF16), grid=(s // t,),
        in_specs=[_halo_before(t, *ZC_POOL), _rows(t, *ZC_POOL), _fixed((POOL_G, POOL_GD, POOL_GD)), _fixed((1, POOL_C))],
        out_specs=_rows(t, POOL_C), scratch_shapes=[pltpu.VMEM((t + CONV_HALO, POOL_C), F32)],
        compiler_params=_params("parallel"))(z, z, pool_w, pool_scale.reshape(1, -1))


def _pool_bwd(dm, z, pool_w, pool_scale, name):
    s = z.shape[0]
    t = _pool_tile(s)
    nt = s // t

    def body(up_ref, u_ref, dm_ref, dmn_ref, w_ref, sc_ref, du_ref, dw_ref, dsc_ref, ubuf, ebuf):
        i = pl.program_id(0)
        ubuf[pl.ds(0, CONV_HALO), :] = jnp.where(i == 0, 0.0, up_ref[...])
        ubuf[pl.ds(CONV_HALO, t), :] = u_ref[...]

        @pl.when(i == 0)
        def _():
            dw_ref[...] = jnp.zeros_like(dw_ref)
            dsc_ref[...] = jnp.zeros_like(dsc_ref)

        dm_next = jnp.where(i == nt - 1, 0.0, dmn_ref[...])
        for gi, window in enumerate(POOL_WINDOWS):
            lanes = pl.ds(gi * POOL_GD, POOL_GD)
            wb = w_ref[gi].astype(BF16)
            scale = sc_ref[:, lanes]
            d = _pool_diff(ubuf, gi, window, i * t, t).astype(BF16)
            mm = jnp.dot(d, wb, preferred_element_type=F32)
            dmv = dm_ref[:, lanes]
            dsc_ref[:, lanes] += jnp.sum(dmv * mm, axis=0, keepdims=True)
            dmm = (dmv * scale).astype(BF16)
            dw_ref[gi] += lax.dot_general(d, dmm, (((0,), (0,)), ((), ())), preferred_element_type=F32)
            dd = lax.dot_general(dmm, wb, (((1,), (1,)), ((), ())), preferred_element_type=F32)
            dd_next = lax.dot_general((dm_next[:, gi * POOL_GD:(gi + 1) * POOL_GD] * scale).astype(BF16), wb,
                                      (((1,), (1,)), ((), ())), preferred_element_type=F32)
            ebuf[pl.ds(0, t), lanes] = dd / _pool_counts(i * t, t, window)
            ebuf[pl.ds(t, CONV_HALO), lanes] = dd_next / _pool_counts((i + 1) * t, CONV_HALO, window)
            du = -dd
            for ahead in range(window):
                du = du + ebuf[pl.ds(ahead, t), lanes]
            du_ref[:, lanes] = du.astype(BF16)

    du, dw, dsc = pl.pallas_call(
        body, name=name,
        out_shape=(jax.ShapeDtypeStruct((s, POOL_C), BF16), jax.ShapeDtypeStruct((POOL_G, POOL_GD, POOL_GD), F32),
                   jax.ShapeDtypeStruct((1, POOL_C), F32)), grid=(nt,),
        in_specs=[_halo_before(t, *ZC_POOL), _rows(t, *ZC_POOL), _rows(t, POOL_C), _halo_after(t, POOL_C, 0, nt),
                  _fixed((POOL_G, POOL_GD, POOL_GD)), _fixed((1, POOL_C))],
        out_specs=(_rows(t, POOL_C), _fixed((POOL_G, POOL_GD, POOL_GD)), _fixed((1, POOL_C))),
        scratch_shapes=[pltpu.VMEM((t + CONV_HALO, POOL_C), F32), pltpu.VMEM((t + CONV_HALO, POOL_C), F32)],
        compiler_params=_params("arbitrary"))(z, z, dm, dm, pool_w, pool_scale.reshape(1, -1))
    return du, dw, dsc.reshape(-1)


def _gate_specs(ts):
    width, first = ZC_GATE
    return [_rows(ts, width, first + b) for b in range(3)]


def _merge_fwd(z, ys, name):
    s = z.shape[0]
    ts = min(s, 256)

    def body(g0, g1, g2, y0, y1, y2, o_ref):
        o_ref[...] = (_sigmoid(g0[...]) * y0[...] + _sigmoid(g1[...]) * y1[...]
                      + _sigmoid(g2[...]) * y2[...]).astype(BF16)

    return pl.pallas_call(
        body, name=name, out_shape=jax.ShapeDtypeStruct((s, D_MODEL), BF16), grid=(s // ts,),
        in_specs=_gate_specs(ts) + [_rows(ts, D_MODEL)] * 3, out_specs=_rows(ts, D_MODEL),
        compiler_params=_params("parallel"))(z, z, z, *ys)


def _merge_bwd(z, ys, dmerged, name):
    s = z.shape[0]
    ts = min(s, 256)

    def body(g0, g1, g2, y0, y1, y2, dm_ref, dy0, dy1, dy2, dg0, dg1, dg2):
        dmv = dm_ref[...]
        for g_ref, y_ref, dy_ref, dg_ref in ((g0, y0, dy0, dg0), (g1, y1, dy1, dg1), (g2, y2, dy2, dg2)):
            sg = _sigmoid(g_ref[...])
            dy_ref[...] = (dmv * sg).astype(BF16)
            dg_ref[...] = (dmv * y_ref[...] * sg * (1.0 - sg)).astype(BF16)

    out = jax.ShapeDtypeStruct((s, D_MODEL), BF16)
    return pl.pallas_call(
        body, name=name, out_shape=(out,) * 6, grid=(s // ts,),
        in_specs=_gate_specs(ts) + [_rows(ts, D_MODEL)] * 4, out_specs=(_rows(ts, D_MODEL),) * 6,
        compiler_params=_params("parallel"))(z, z, z, *ys, dmerged)


def _swiglu_fwd(hg, hu, name):
    s, f = hg.shape
    ts, tc = min(s, 512), _tile(f, 1024)
    blk = pl.BlockSpec((ts, tc), lambda i, j: (i, j))

    def body(g_ref, u_ref, o_ref):
        g = g_ref[...]
        o_ref[...] = (g * _sigmoid(g) * u_ref[...]).astype(BF16)

    return pl.pallas_call(
        body, name=name, out_shape=jax.ShapeDtypeStruct((s, f), BF16), grid=(s // ts, f // tc),
        in_specs=[blk, blk], out_specs=blk, compiler_params=_params("parallel", "parallel"))(hg, hu)


def _swiglu_bwd(hg, hu, dact, name):
    s, f = hg.shape
    ts, tc = min(s, 512), _tile(f, 1024)
    blk = pl.BlockSpec((ts, tc), lambda i, j: (i, j))

    def body(g_ref, u_ref, d_ref, dg_ref, du_ref):
        g, d = g_ref[...], d_ref[...]
        sg = _sigmoid(g)
        dg_ref[...] = (d * u_ref[...] * (sg * (1.0 + g * (1.0 - sg)))).astype(BF16)
        du_ref[...] = (d * g * sg).astype(BF16)

    out = jax.ShapeDtypeStruct((s, f), BF16)
    return pl.pallas_call(
        body, name=name, out_shape=(out, out), grid=(s // ts, f // tc), in_specs=[blk, blk, blk], out_specs=(blk, blk),
        compiler_params=_params("parallel", "parallel"))(hg, hu, dact)


def _loss_grad(y, target, name):
    s, d = y.shape
    ts = min(s, 512)

    def body(y_ref, t_ref, dy_ref, sq_ref):
        e = y_ref[...] - t_ref[...]
        dy_ref[...] = e / d

        @pl.when(pl.program_id(0) == 0)
        def _():
            sq_ref[...] = jnp.zeros_like(sq_ref)

        sq_ref[...] += jnp.sum(e * e, axis=0, keepdims=True)

    return pl.pallas_call(
        body, name=name, out_shape=(jax.ShapeDtypeStruct((s, d), F32), jax.ShapeDtypeStruct((1, d), F32)),
        grid=(s // ts,), in_specs=[_rows(ts, d), _rows(ts, d)], out_specs=(_rows(ts, d), _fixed((1, d))),
        compiler_params=_params("arbitrary"))(y, target)


def _adamw(w, g, m, v, name):
    shape = w.shape
    cols = shape[-1]
    rows = math.prod(shape[:-1])
    tr = _row_tile(rows, cols * 4)

    def body(w_ref, g_ref, m_ref, v_ref, d_ref, mo_ref, vo_ref):
        gv = g_ref[...]
        mn = B1 * m_ref[...] + (1.0 - B1) * gv
        vn = B2 * v_ref[...] + (1.0 - B2) * (gv * gv)
        m_hat = mn / (1.0 - B1 ** STEP)
        v_hat = vn / (1.0 - B2 ** STEP)
        d_ref[...] = -LR * (m_hat / (jnp.sqrt(v_hat) + ADAM_EPS) + WD * w_ref[...])
        mo_ref[...] = mn
        vo_ref[...] = vn

    spec = _rows(tr, cols)
    out = jax.ShapeDtypeStruct((rows, cols), F32)
    res = pl.pallas_call(
        body, name=name, out_shape=(out,) * 3, grid=(rows // tr,), in_specs=[spec] * 4, out_specs=(spec,) * 3,
        compiler_params=_params("parallel"))(*[t.reshape(rows, cols) for t in (w, g, m, v)])
    return tuple(r.reshape(shape) for r in res)


ANY = pl.BlockSpec(memory_space=pl.ANY)


def _all_gather(arrays, name):
    n = len(arrays)

    def body(*refs):
        ins, outs = refs[:n], refs[n:2 * n]
        send_sems, recv_sems, local_sems = refs[2 * n:]
        x, y, c = lax.axis_index("x"), lax.axis_index("y"), lax.axis_index("c")
        me, sibling = (x, y, c), (x, y, 1 - c)
        chips = [(1 - x, y), (x, 1 - y), (1 - x, 1 - y)]

        def slot(a, px, py, pc):
            return outs[a].at[4 * px + 2 * py + pc]

        def copy(a, k, block, to, src=None):
            return pltpu.make_async_remote_copy(
                src_ref=slot(a, *block) if src is None else src, dst_ref=slot(a, *block), send_sem=send_sems.at[a, k],
                recv_sem=recv_sems.at[a, k], device_id=to, device_id_type=MESH)

        mine = [pltpu.make_async_copy(ins[a], slot(a, *me), local_sems.at[a]) for a in range(n)]
        first = []
        for a in range(n):
            mine[a].start()
            first.append(copy(a, 0, me, sibling, src=ins[a]))
            first += [copy(a, 1 + j, me, (*chip, c), src=ins[a]) for j, chip in enumerate(chips)]
        for cp in first:
            cp.start()
        passed = []
        for j, chip in enumerate(chips):
            for a in range(n):
                copy(a, 1 + j, (*chip, c), me).wait_recv()
                passed.append(copy(a, 4 + j, (*chip, c), sibling))
                passed[-1].start()
        for a in range(n):
            copy(a, 0, sibling, me).wait_recv()
            for j, chip in enumerate(chips):
                copy(a, 4 + j, (*chip, 1 - c), me).wait_recv()
        for cp in first + passed:
            cp.wait_send()
        for cp in mine:
            cp.wait()

    return pl.pallas_call(
        body, name=name, out_shape=[jax.ShapeDtypeStruct((N_DEV,) + a.shape, a.dtype) for a in arrays],
        in_specs=[ANY] * n, out_specs=[ANY] * n,
        scratch_shapes=[pltpu.SemaphoreType.DMA((n, 7)), pltpu.SemaphoreType.DMA((n, 7)), pltpu.SemaphoreType.DMA((n,))],
    )(*arrays)


def _swap_with_sibling(arrays, name):
    n = len(arrays)

    def body(*refs):
        ins, outs = refs[:n], refs[n:2 * n]
        send_sems, recv_sems = refs[2 * n:]
        x, y, c = lax.axis_index("x"), lax.axis_index("y"), lax.axis_index("c")
        copies = [pltpu.make_async_remote_copy(
            src_ref=ins[a].at[1 - c], dst_ref=outs[a], send_sem=send_sems.at[a], recv_sem=recv_sems.at[a],
            device_id=(x, y, 1 - c), device_id_type=MESH) for a in range(n)]
        for cp in copies:
            cp.start()
        for cp in copies:
            cp.wait()

    return pl.pallas_call(
        body, name=name, out_shape=[jax.ShapeDtypeStruct(a.shape[1:], a.dtype) for a in arrays],
        in_specs=[ANY] * n, out_specs=[ANY] * n,
        scratch_shapes=[pltpu.SemaphoreType.DMA((n,)), pltpu.SemaphoreType.DMA((n,))])(*arrays)


def _exchange_chips(arrays, name):
    n = len(arrays)

    def body(*refs):
        ins, outs = refs[:n], refs[n:2 * n]
        send_sems, recv_sems, local_sems = refs[2 * n:]
        x, y, c = lax.axis_index("x"), lax.axis_index("y"), lax.axis_index("c")
        partners = [(x, 1 - y), (1 - x, y), (1 - x, 1 - y)]
        mine = [pltpu.make_async_copy(ins[a].at[2 * x + y], outs[a].at[0], local_sems.at[a]) for a in range(n)]
        copies = [pltpu.make_async_remote_copy(
            src_ref=ins[a].at[2 * px + py], dst_ref=outs[a].at[1 + k], send_sem=send_sems.at[a, k],
            recv_sem=recv_sems.at[a, k], device_id=(px, py, c), device_id_type=MESH)
            for a in range(n) for k, (px, py) in enumerate(partners)]
        for cp in mine + copies:
            cp.start()
        for cp in copies + mine:
            cp.wait()

    return pl.pallas_call(
        body, name=name, out_shape=[jax.ShapeDtypeStruct(a.shape, a.dtype) for a in arrays],
        in_specs=[ANY] * n, out_specs=[ANY] * n,
        scratch_shapes=[pltpu.SemaphoreType.DMA((n, 3)), pltpu.SemaphoreType.DMA((n, 3)), pltpu.SemaphoreType.DMA((n,))],
    )(*arrays)


def _as_rows(a, lead):
    return a.reshape(a.shape[:lead] + (math.prod(a.shape[lead:-1]), a.shape[-1]))


def _add_pairs(a, b, name):
    a2, b2 = _as_rows(a, 0), _as_rows(b, 0)
    rows, cols = a2.shape
    tr = _row_tile(rows, cols * 4)

    def body(a_ref, b_ref, o_ref):
        o_ref[...] = (a_ref[...].astype(F32) + b_ref[...].astype(F32)).astype(o_ref.dtype)

    spec = _rows(tr, cols)
    out = pl.pallas_call(body, name=name, out_shape=jax.ShapeDtypeStruct(a2.shape, a.dtype), grid=(rows // tr,),
                         in_specs=[spec, spec], out_specs=spec, compiler_params=_params("parallel"))(a2, b2)
    return out.reshape(a.shape)


def _sum_blocks(a, name):
    a3 = _as_rows(a, 1)
    n, rows, cols = a3.shape
    tr = _row_tile(rows, n * cols * 4)

    def body(a_ref, o_ref):
        tot = a_ref[0].astype(F32)
        for k in range(1, n):
            tot = tot + a_ref[k].astype(F32)
        o_ref[...] = tot

    out = pl.pallas_call(body, name=name, out_shape=jax.ShapeDtypeStruct((rows, cols), F32), grid=(rows // tr,),
                         in_specs=[pl.BlockSpec((n, tr, cols), lambda j: (0, j, 0))], out_specs=_rows(tr, cols),
                         compiler_params=_params("parallel"))(a3)
    return out.reshape(a.shape[1:])


def _pad_axis(a, axis, size):
    pad = [(0, 0)] * a.ndim
    pad[axis] = (0, size - a.shape[axis])
    return jnp.pad(a, pad)


def _local_groups(sh):
    return {
        "w_in": sh["w_in"].astype(BF16),
        "w_uq": _pad_axis(sh["w_uq"], -1, HEAD_PAD).astype(BF16),
        "w_ukv": jnp.stack([_pad_axis(sh["w_uk"], -1, HEAD_PAD), _pad_axis(sh["w_uv"], -1, HEAD_PAD)]).astype(BF16),
        "w_o3": jnp.stack([sh["w_attn_o"], sh["w_conv_o"], sh["w_pool_o"]]).astype(BF16),
        "w_mix_o": sh["w_mix_o"].astype(BF16),
        "w_gu": jnp.stack([_pad_axis(sh["w_gate"], -1, FF_SHARD_PAD), _pad_axis(sh["w_up"], -1, FF_SHARD_PAD)]).astype(BF16),
        "w_down": _pad_axis(sh["w_down"], 1, FF_SHARD_PAD).astype(BF16),
    }


def _cols(blocks):
    return blocks.transpose(1, 0, 2).reshape(blocks.shape[1], -1)


def _arrange_w_in(blocks):
    parts, pos = [], 0
    for ref_lo, ref_hi, at in sorted(W_IN_PIECES, key=lambda p: p[2]):
        if at > pos:
            parts.append(jnp.zeros((blocks.shape[1], at - pos), blocks.dtype))
        for d in range(N_DEV):
            lo, hi = max(ref_lo, d * W_IN_SHARD), min(ref_hi, (d + 1) * W_IN_SHARD)
            if lo < hi:
                parts.append(blocks[d][:, lo - d * W_IN_SHARD:hi - d * W_IN_SHARD])
        pos = at + ref_hi - ref_lo
    if pos < Z_W:
        parts.append(jnp.zeros((blocks.shape[1], Z_W - pos), blocks.dtype))
    return jnp.concatenate(parts, axis=1)


def _w_in_shard(g, d):
    parts = []
    for ref_lo, ref_hi, at in W_IN_PIECES:
        lo, hi = max(ref_lo, d * W_IN_SHARD), min(ref_hi, (d + 1) * W_IN_SHARD)
        if lo < hi:
            parts.append(g[:, at + lo - ref_lo:at + hi - ref_lo])
    return jnp.concatenate(parts, axis=1)


def _kernel_weights(gat, l):
    attn_o = _cols(gat["w_o3"][:, 0, l]).reshape(N_HEADS, V_HEAD, D_MODEL)
    return {
        "w_in": _arrange_w_in(gat["w_in"][:, l]),
        "w_uq": _cols(gat["w_uq"][:, l]),
        "w_uk": _cols(gat["w_ukv"][:, 0, l]),
        "w_uv": _cols(gat["w_ukv"][:, 1, l]),
        "w_attn_o": _pad_axis(attn_o, 1, HEAD_PAD).reshape(N_HEADS * HEAD_PAD, D_MODEL),
        "w_conv_o": _cols(gat["w_o3"][:, 1, l]),
        "w_pool_o": _cols(gat["w_o3"][:, 2, l]),
        "w_mix_o": gat["w_mix_o"][:, l].reshape(D_MODEL, D_MODEL),
        "w_gate": _cols(gat["w_gu"][:, 0, l]),
        "w_up": _cols(gat["w_gu"][:, 1, l]),
        "w_down": gat["w_down"][:, l].reshape(D_FF_PAD, D_MODEL),
    }


def _by_cols(g, nb):
    return g.reshape(g.shape[0], 4, 2, nb).transpose(2, 1, 0, 3)


def _by_rows(g, rb):
    return g.reshape(4, 2, rb, g.shape[1]).transpose(1, 0, 2, 3)


def _grad_groups(big):
    b = [{k: v.astype(BF16) for k, v in layer.items()} for layer in big]
    layers = range(DEPTH)
    per_layer = lambda f: jnp.stack([f(l) for l in layers], axis=2)
    attn_o = lambda l: b[l]["w_attn_o"].reshape(N_HEADS, HEAD_PAD, D_MODEL)[:, :V_HEAD].reshape(N_HEADS * V_HEAD, D_MODEL)
    w_in = jnp.stack([jnp.stack([jnp.stack([_w_in_shard(b[l]["w_in"], 2 * chip + core) for l in layers])
                                 for chip in range(4)]) for core in range(2)])
    return {
        "w_in": w_in,
        "w_uq": per_layer(lambda l: _by_cols(b[l]["w_uq"], HEAD_PAD)),
        "w_ukv": jnp.stack([per_layer(lambda l: _by_cols(b[l]["w_uk"], HEAD_PAD)),
                            per_layer(lambda l: _by_cols(b[l]["w_uv"], HEAD_PAD))], axis=2),
        "w_o3": jnp.stack([per_layer(lambda l: _by_cols(attn_o(l), LANES)),
                           per_layer(lambda l: _by_cols(b[l]["w_conv_o"], LANES)),
                           per_layer(lambda l: _by_cols(b[l]["w_pool_o"], LANES))], axis=2),
        "w_mix_o": per_layer(lambda l: _by_rows(b[l]["w_mix_o"], D_MODEL // N_DEV)),
        "w_gu": jnp.stack([per_layer(lambda l: _by_cols(b[l]["w_gate"], FF_SHARD_PAD)),
                           per_layer(lambda l: _by_cols(b[l]["w_up"], FF_SHARD_PAD))], axis=2),
        "w_down": per_layer(lambda l: _by_rows(b[l]["w_down"], FF_SHARD_PAD)),
    }


def _grads_from_groups(tot):
    qk = QK_NOPE + QK_ROPE
    return {
        "w_in": tot["w_in"], "w_uq": tot["w_uq"][..., :qk],
        "w_uk": tot["w_ukv"][0][..., :QK_NOPE], "w_uv": tot["w_ukv"][1][..., :V_HEAD],
        "w_attn_o": tot["w_o3"][0], "w_conv_o": tot["w_o3"][1], "w_pool_o": tot["w_o3"][2],
        "w_mix_o": tot["w_mix_o"],
        "w_gate": tot["w_gu"][0][..., :FF_SHARD], "w_up": tot["w_gu"][1][..., :FF_SHARD],
        "w_down": tot["w_down"][:, :FF_SHARD],
    }


SMALL_GROUPS = (
    (D_MODEL, ("mix_norm_pre", "mix_norm_post", "ffn_norm_pre", "ffn_norm_post")),
    (CONV_C, ("conv_w", "conv_b", "conv_ln_g", "conv_ln_b", "pool_scale")),
    (Q_RANK, ("q_norm",)), (KV_RANK, ("kv_norm",)), (POOL_GD, ("pool_w",)),
)


def _small_rows(name):
    return {"conv_w": CONV_HALO, "pool_w": POOL_G * POOL_GD}.get(name, SUBLANES)


def _small_groups(small):
    out = []
    for width, names in SMALL_GROUPS:
        parts = []
        for l in range(DEPTH):
            for n in names:
                part = small[l][n].reshape(-1, width)
                parts.append(_pad_axis(part, 0, _small_rows(n)))
        out.append(jnp.concatenate(parts, axis=0))
    return out


def _small_from_groups(groups):
    shapes = {"conv_w": (CONV_W, CONV_C), "pool_w": (POOL_G, POOL_GD, POOL_GD)}
    out = {}
    for (width, names), g in zip(SMALL_GROUPS, groups):
        row = 0
        for l in range(DEPTH):
            for n in names:
                rows = _small_rows(n)
                real = {"conv_w": CONV_W, "pool_w": POOL_G * POOL_GD}.get(n, 1)
                out.setdefault(n, []).append(g[row:row + real].reshape(shapes.get(n, (width,))))
                row += rows
    return {n: jnp.stack(v) for n, v in out.items()}


def _layer_fwd(x, tables, w, sm, tag):
    nm = lambda n: f"{n}_{tag}"
    h = _rms_fwd(x, (D_MODEL, 0), sm["mix_norm_pre"], BF16, nm("mix_pre_norm"))
    z = _matmul(h, w["w_in"], "nn", F32, nm("in_proj"))
    cq = _rms_fwd(z, ZC_Q, sm["q_norm"], BF16, nm("q_norm"))
    ckv = _rms_fwd(z, ZC_KV, sm["kv_norm"], BF16, nm("kv_norm"))
    qf = _matmul(cq, w["w_uq"], "nn", F32, nm("q_up"))
    kf = _matmul(ckv, w["w_uk"], "nn", F32, nm("k_up"))
    v = _matmul(ckv, w["w_uv"], "nn", BF16, nm("v_up"))
    q, k = _rope_qk_fwd(qf, kf, z, tables, nm("rope_qk"))
    o, lse = _flash_fwd(q, k, v, nm("flash_fwd"))
    y_attn = _matmul(o, w["w_attn_o"], "nn", F32, nm("attn_out"))
    hc, co = _conv_fwd(z, sm["conv_w"], sm["conv_b"], sm["conv_ln_g"], sm["conv_ln_b"], nm("conv_fwd"))
    y_conv = _matmul(hc, w["w_conv_o"], "nn", F32, nm("conv_out"))
    pm = _pool_fwd(z, sm["pool_w"], sm["pool_scale"], nm("pool_fwd"))
    y_pool = _matmul(pm, w["w_pool_o"], "nn", F32, nm("pool_out"))
    ys = (y_attn, y_conv, y_pool)
    merged = _merge_fwd(z, ys, nm("merge_fwd"))
    mo = _matmul(merged, w["w_mix_o"], "nn", F32, nm("mix_out"))
    x_mid = _rms_fwd(mo, (D_MODEL, 0), sm["mix_norm_post"], F32, nm("mix_post_norm"), res=x)
    h2 = _rms_fwd(x_mid, (D_MODEL, 0), sm["ffn_norm_pre"], BF16, nm("ffn_pre_norm"))
    hg = _matmul(h2, w["w_gate"], "nn", F32, nm("ffn_gate"))
    hu = _matmul(h2, w["w_up"], "nn", F32, nm("ffn_up"))
    act = _swiglu_fwd(hg, hu, nm("swiglu_fwd"))
    fo = _matmul(act, w["w_down"], "nn", F32, nm("ffn_down"))
    out = _rms_fwd(fo, (D_MODEL, 0), sm["ffn_norm_post"], F32, nm("ffn_post_norm"), res=x_mid)
    saved = dict(x=x, h=h, z=z, cq=cq, ckv=ckv, q=q, k=k, v=v, o=o, lse=lse, hc=hc, co=co, pm=pm, ys=ys, merged=merged,
                 mo=mo, x_mid=x_mid, h2=h2, hg=hg, hu=hu, act=act, fo=fo)
    return out, saved


def _layer_bwd(dout, sv, tables, w, sm, tag):
    nm = lambda n: f"{n}_{tag}"
    gb, gs = {}, {}
    dfo, gs["ffn_norm_post"] = _rms_bwd(sv["fo"], (D_MODEL, 0), sm["ffn_norm_post"], dout, BF16, nm("ffn_post_norm_bwd"))
    dact = _matmul(dfo, w["w_down"], "nt", F32, nm("ffn_down_dx"))
    gb["w_down"] = _matmul(sv["act"], dfo, "tn", F32, nm("ffn_down_dw"))
    dhg, dhu = _swiglu_bwd(sv["hg"], sv["hu"], dact, nm("swiglu_bwd"))
    dh2_g = _matmul(dhg, w["w_gate"], "nt", F32, nm("ffn_gate_dx"))
    dh2 = _matmul(dhu, w["w_up"], "nt", F32, nm("ffn_up_dx"), add=dh2_g)
    gb["w_gate"] = _matmul(sv["h2"], dhg, "tn", F32, nm("ffn_gate_dw"))
    gb["w_up"] = _matmul(sv["h2"], dhu, "tn", F32, nm("ffn_up_dw"))
    dmid, gs["ffn_norm_pre"] = _rms_bwd(sv["x_mid"], (D_MODEL, 0), sm["ffn_norm_pre"], dh2, F32, nm("ffn_pre_norm_bwd"),
                                        add=dout)
    dmo, gs["mix_norm_post"] = _rms_bwd(sv["mo"], (D_MODEL, 0), sm["mix_norm_post"], dmid, BF16, nm("mix_post_norm_bwd"))
    dmerged = _matmul(dmo, w["w_mix_o"], "nt", F32, nm("mix_out_dx"))
    gb["w_mix_o"] = _matmul(sv["merged"], dmo, "tn", F32, nm("mix_out_dw"))
    dya, dyc, dyp, dg0, dg1, dg2 = _merge_bwd(sv["z"], sv["ys"], dmerged, nm("merge_bwd"))
    dpm = _matmul(dyp, w["w_pool_o"], "nt", F32, nm("pool_out_dx"))
    gb["w_pool_o"] = _matmul(sv["pm"], dyp, "tn", F32, nm("pool_out_dw"))
    du_pool, gs["pool_w"], gs["pool_scale"] = _pool_bwd(dpm, sv["z"], sm["pool_w"], sm["pool_scale"], nm("pool_bwd"))
    dhc = _matmul(dyc, w["w_conv_o"], "nt", F32, nm("conv_out_dx"))
    gb["w_conv_o"] = _matmul(sv["hc"], dyc, "tn", F32, nm("conv_out_dw"))
    dco, gs["conv_ln_g"], gs["conv_ln_b"], gs["conv_b"] = _conv_bwd_norm(dhc, sv["co"], sm["conv_ln_g"], sm["conv_ln_b"],
                                                                        nm("conv_bwd_norm"))
    du_conv, gs["conv_w"] = _conv_bwd_taps(dco, sv["z"], sm["conv_w"], nm("conv_bwd_taps"))
    do = _matmul(dya, w["w_attn_o"], "nt", F32, nm("attn_out_dx"))
    gb["w_attn_o"] = _matmul(sv["o"], dya, "tn", F32, nm("attn_out_dw"))
    delta, dob = _attn_delta(do, sv["o"], nm("attn_delta"))
    dq, dk, dv = _flash_bwd(sv["q"], sv["k"], sv["v"], dob, sv["lse"], delta, nm("flash_bwd"))
    dqf, dkf, dkr = _rope_qk_bwd(dq, dk, tables, nm("rope_qk_bwd"))
    dcq_n = _matmul(dqf, w["w_uq"], "nt", F32, nm("q_up_dx"))
    gb["w_uq"] = _matmul(sv["cq"], dqf, "tn", F32, nm("q_up_dw"))
    dckv_k = _matmul(dkf, w["w_uk"], "nt", F32, nm("k_up_dx"))
    dckv_n = _matmul(dv, w["w_uv"], "nt", F32, nm("v_up_dx"), add=dckv_k)
    gb["w_uk"] = _matmul(sv["ckv"], dkf, "tn", F32, nm("k_up_dw"))
    gb["w_uv"] = _matmul(sv["ckv"], dv, "tn", F32, nm("v_up_dw"))
    dcq, gs["q_norm"] = _rms_bwd(sv["z"], ZC_Q, sm["q_norm"], dcq_n, BF16, nm("q_norm_bwd"))
    dckv, gs["kv_norm"] = _rms_bwd(sv["z"], ZC_KV, sm["kv_norm"], dckv_n, BF16, nm("kv_norm_bwd"))
    dz = jnp.concatenate([dcq, dkr, du_pool, du_conv, dg0, dg1, dg2, dckv], axis=1)
    dh = _matmul(dz, w["w_in"], "nt", F32, nm("in_proj_dx"))
    gb["w_in"] = _matmul(sv["h"], dz, "tn", F32, nm("in_proj_dw"))
    dx, gs["mix_norm_pre"] = _rms_bwd(sv["x"], (D_MODEL, 0), sm["mix_norm_pre"], dh, F32, nm("mix_pre_norm_bwd"), add=dmid)
    return dx, gb, gs


def _local_step(x, positions, target, weights, smalls):
    tables = _rope_tables(positions)
    saved = []
    h = x
    for l in range(DEPTH):
        h, sv = _layer_fwd(h, tables, weights[l], smalls[l], f"l{l}")
        saved.append(sv)
    dy, sq = _loss_grad(h, target, "loss_grad")
    big, small = [None] * DEPTH, [None] * DEPTH
    for l in reversed(range(DEPTH)):
        dy, big[l], small[l] = _layer_bwd(dy, saved[l], tables, weights[l], smalls[l], f"l{l}")
    return sq, dy, big, small


def kernel(x, positions, mix_norm_pre, w_in, q_norm, w_uq, kv_norm, w_uk, w_uv, w_attn_o, conv_w, conv_b, conv_ln_g, conv_ln_b, w_conv_o, pool_w, pool_scale, w_pool_o, w_mix_o, mix_norm_post, ffn_norm_pre, w_gate, w_up, w_down, ffn_norm_post, loss_target, m_mix_norm_pre, m_w_in, m_q_norm, m_w_uq, m_kv_norm, m_w_uk, m_w_uv, m_w_attn_o, m_conv_w, m_conv_b, m_conv_ln_g, m_conv_ln_b, m_w_conv_o, m_pool_w, m_pool_scale, m_w_pool_o, m_w_mix_o, m_mix_norm_post, m_ffn_norm_pre, m_w_gate, m_w_up, m_w_down, m_ffn_norm_post, v_mix_norm_pre, v_w_in, v_q_norm, v_w_uq, v_kv_norm, v_w_uk, v_w_uv, v_w_attn_o, v_conv_w, v_conv_b, v_conv_ln_g, v_conv_ln_b, v_w_conv_o, v_pool_w, v_pool_scale, v_w_pool_o, v_w_mix_o, v_mix_norm_post, v_ffn_norm_pre, v_w_gate, v_w_up, v_w_down, v_ffn_norm_post):
    given = dict(locals())
    dev = 4 * lax.axis_index("x") + 2 * lax.axis_index("y") + lax.axis_index("c")

    local = _local_groups({n: given[n] for n in BIG})
    names = tuple(local)
    gathered = _all_gather([local[n] for n in names] + [conv_w], "gather_weights")
    gat = dict(zip(names, gathered[:-1]))
    weights = [_kernel_weights(gat, l) for l in range(DEPTH)]
    cw = CONV_C // N_DEV
    conv_w_full = gathered[-1].transpose(1, 2, 0, 3).reshape(DEPTH, CONV_W, CONV_C)
    smalls = []
    for l in range(DEPTH):
        sm = {n: given[n][l] for n in SMALL if n != "conv_w"}
        sm["conv_w"] = _pad_axis(conv_w_full[l], 0, CONV_HALO)
        smalls.append(sm)

    sq, grad_x, big, small = _local_step(x[0], positions[0], loss_target[0], weights, smalls)
    loss = lax.psum(0.5 / D_MODEL * jnp.sum(sq), ("x", "y", "c"))

    send = _grad_groups(big)
    core = lax.axis_index("c")
    own = [lax.dynamic_index_in_dim(send[n], core, axis=0, keepdims=False) for n in names]
    got = _swap_with_sibling([send[n] for n in names], "reduce_d2d")
    pairs = [_add_pairs(a, b, f"reduce_pair_add_{n}") for n, a, b in zip(names, own, got)]
    chips = _exchange_chips(pairs, "reduce_ici")
    grads = _grads_from_groups({n: _sum_blocks(a, f"reduce_chip_add_{n}") for n, a in zip(names, chips)})

    small_groups = _all_gather(_small_groups(small), "gather_small_grads")
    small_sum = _small_from_groups([_sum_blocks(g, f"sum_small_grads_{i}") for i, g in enumerate(small_groups)])
    for n in SMALL:
        grads[n] = small_sum[n]
    grads["conv_w"] = lax.dynamic_slice_in_dim(small_sum["conv_w"], dev * cw, cw, axis=2)

    delta, new_m, new_v = {}, {}, {}
    for n in WEIGHTS:
        delta[n], new_m[n], new_v[n] = _adamw(given[n], grads[n], given["m_" + n], given["v_" + n], f"adamw_{n}")
    return (loss, grad_x[None], *[grads[n] for n in WEIGHTS], *[delta[n] for n in WEIGHTS],
            *[new_m[n] for n in WEIGHTS], *[new_v[n] for n in WEIGHTS])
```

```python
import functools
import math

import jax
import jax.numpy as jnp
from jax import lax
from jax.experimental import pallas as pl
from jax.experimental.pallas import tpu as pltpu

F32, BF16 = jnp.float32, jnp.bfloat16
MESH = pl.DeviceIdType.MESH

LANES = 128
SUBLANES = 8
VMEM_LIMIT_BYTES = 56 * 1024 * 1024

N_DEV = 8
D_MODEL = 1024
DEPTH = 2
N_HEADS = 8
QK_NOPE, QK_ROPE, V_HEAD = 64, 32, 64
HEAD_PAD = LANES
Q_RANK, KV_RANK = 384, 256
ROPE_THETA = 10000.0
CONV_C, CONV_W = 512, 31
CONV_HALO = 32
POOL_WINDOWS = (2, 4, 8, 16)
POOL_C, POOL_G = 512, 4
POOL_GD = POOL_C // POOL_G
D_FF = 2816
FF_SHARD = D_FF // N_DEV
FF_SHARD_PAD = 3 * LANES
D_FF_PAD = N_DEV * FF_SHARD_PAD
W_IN_SHARD = 660
EPS = 1e-6
ATTN_SCALE = 1.0 / math.sqrt(QK_NOPE + QK_ROPE)
LOG2E = 1.4426950408889634
LR, B1, B2, ADAM_EPS, WD, STEP = 0.001, 0.9, 0.999, 1e-08, 0.01, 10

Z_W = 5376
ZC_Q = (384, 0)
ZC_KR = (128, 3)
ZC_POOL = (512, 1)
ZC_CONV_A = (512, 2)
ZC_CONV_G = (512, 3)
ZC_GATE = (1024, 2)
ZC_KV = (256, 20)
W_IN_PIECES = ((0, 384, 0), (384, 640, 5120), (640, 672, 448), (672, 1696, 1024), (1696, 2208, 512), (2208, 5280, 2048))

BIG = ("w_in", "w_uq", "w_uk", "w_uv", "w_attn_o", "w_conv_o", "w_pool_o", "w_mix_o", "w_gate", "w_up", "w_down")
SMALL = ("mix_norm_pre", "q_norm", "kv_norm", "conv_w", "conv_b", "conv_ln_g", "conv_ln_b", "pool_w", "pool_scale",
         "mix_norm_post", "ffn_norm_pre", "ffn_norm_post")
WEIGHTS = ("mix_norm_pre", "w_in", "q_norm", "w_uq", "kv_norm", "w_uk", "w_uv", "w_attn_o", "conv_w", "conv_b",
           "conv_ln_g", "conv_ln_b", "w_conv_o", "pool_w", "pool_scale", "w_pool_o", "w_mix_o", "mix_norm_post",
           "ffn_norm_pre", "w_gate", "w_up", "w_down", "ffn_norm_post")


def _params(*semantics):
    return pltpu.CompilerParams(dimension_semantics=semantics, vmem_limit_bytes=VMEM_LIMIT_BYTES)


def _tile(dim, cap):
    if dim <= cap:
        return dim
    for t in range(cap - cap % LANES, 0, -LANES):
        if dim % t == 0:
            return t
    raise ValueError(f"no tile for {dim} under {cap}")


def _row_tile(rows, row_bytes, budget=1 << 20):
    if rows * row_bytes <= budget:
        return rows
    cap = max(16, budget // row_bytes)
    for t in range(cap - cap % 16, 0, -16):
        if rows % t == 0:
            return t
    return rows


def _rows(ts, width, cidx=0):
    return pl.BlockSpec((ts, width), lambda i: (i, cidx))


def _fixed(shape):
    return pl.BlockSpec(shape, lambda *_: (0,) * len(shape))


def _sigmoid(x):
    return 1.0 / (1.0 + jnp.exp(-x))


def _matmul(a, b, mode, out_dtype, name, add=None):
    if mode == "nn":
        (m, k), n = a.shape, b.shape[1]
    elif mode == "nt":
        (m, k), n = a.shape, b.shape[0]
    else:
        (k, m), n = a.shape, b.shape[1]
    tm, tn, tk = _tile(m, 1024), _tile(n, 1408), _tile(k, 1408 if mode != "tn" else 1024)
    nk = k // tk
    dims = {"nn": ((1,), (0,)), "nt": ((1,), (1,)), "tn": ((0,), (0,))}[mode]
    a_spec = {"nn": pl.BlockSpec((tm, tk), lambda i, j, s: (i, s)), "nt": pl.BlockSpec((tm, tk), lambda i, j, s: (i, s)),
              "tn": pl.BlockSpec((tk, tm), lambda i, j, s: (s, i))}[mode]
    b_spec = {"nn": pl.BlockSpec((tk, tn), lambda i, j, s: (s, j)), "nt": pl.BlockSpec((tn, tk), lambda i, j, s: (j, s)),
              "tn": pl.BlockSpec((tk, tn), lambda i, j, s: (s, j))}[mode]
    o_spec = pl.BlockSpec((tm, tn), lambda i, j, s: (i, j))
    has_add = add is not None

    def body(a_ref, b_ref, *rest):
        add_ref = rest[0] if has_add else None
        o_ref = rest[1] if has_add else rest[0]
        part = lax.dot_general(a_ref[...], b_ref[...], (dims, ((), ())), preferred_element_type=F32)

        def finish(total):
            if has_add:
                total = total + add_ref[...]
            o_ref[...] = total.astype(o_ref.dtype)

        if nk == 1:
            finish(part)
        else:
            acc = rest[-1]
            step = pl.program_id(2)

            @pl.when(step == 0)
            def _():
                acc[...] = part

            @pl.when(step > 0)
            def _():
                acc[...] += part

            @pl.when(step == nk - 1)
            def _():
                finish(acc[...])

    operands = (a, b, add) if has_add else (a, b)
    return pl.pallas_call(
        body, name=name, out_shape=jax.ShapeDtypeStruct((m, n), out_dtype), grid=(m // tm, n // tn, nk),
        in_specs=[a_spec, b_spec] + ([o_spec] if has_add else []), out_specs=o_spec,
        scratch_shapes=[pltpu.VMEM((tm, tn), F32)] if nk > 1 else [],
        compiler_params=_params("parallel", "parallel", "arbitrary"))(*operands)


def _rms_fwd(x, win, gain, out_dtype, name, res=None):
    width, cidx = win
    s = x.shape[0]
    ts = min(s, 512)
    has_res = res is not None

    def body(x_ref, g_ref, *rest):
        o_ref = rest[-1]
        xv = x_ref[...]
        r = lax.rsqrt(jnp.mean(xv * xv, axis=-1, keepdims=True) + EPS)
        y = (xv * r) * g_ref[...]
        if has_res:
            y = rest[0][...] + y
        o_ref[...] = y.astype(o_ref.dtype)

    ops = (x, gain.reshape(1, width)) + ((res,) if has_res else ())
    return pl.pallas_call(
        body, name=name, out_shape=jax.ShapeDtypeStruct((s, width), out_dtype), grid=(s // ts,),
        in_specs=[_rows(ts, width, cidx), _fixed((1, width))] + ([_rows(ts, width)] if has_res else []),
        out_specs=_rows(ts, width), compiler_params=_params("parallel"))(*ops)


def _rms_bwd(x, win, gain, dy, out_dtype, name, add=None):
    width, cidx = win
    s = x.shape[0]
    ts = min(s, 512)
    has_add = add is not None

    def body(x_ref, g_ref, dy_ref, *rest):
        dx_ref, dg_ref = rest[-2], rest[-1]
        xv = x_ref[...]
        r = lax.rsqrt(jnp.mean(xv * xv, axis=-1, keepdims=True) + EPS)
        xh = xv * r
        dyv = dy_ref[...].astype(F32)
        dyg = dyv * g_ref[...]
        dx = r * (dyg - xh * jnp.mean(dyg * xh, axis=-1, keepdims=True))
        if has_add:
            dx = dx + rest[0][...]
        dx_ref[...] = dx.astype(dx_ref.dtype)

        @pl.when(pl.program_id(0) == 0)
        def _():
            dg_ref[...] = jnp.zeros_like(dg_ref)

        dg_ref[...] += jnp.sum(dyv * xh, axis=0, keepdims=True)

    ops = (x, gain.reshape(1, width), dy) + ((add,) if has_add else ())
    dx, dg = pl.pallas_call(
        body, name=name,
        out_shape=(jax.ShapeDtypeStruct((s, width), out_dtype), jax.ShapeDtypeStruct((1, width), F32)), grid=(s // ts,),
        in_specs=[_rows(ts, width, cidx), _fixed((1, width)), _rows(ts, width)] + ([_rows(ts, width)] if has_add else []),
        out_specs=(_rows(ts, width), _fixed((1, width))), compiler_params=_params("arbitrary"))(*ops)
    return dx, dg.reshape(width)


def _rope(x, c, s1, s2):
    return x * c + pltpu.roll(x, 16, 1) * s1 + pltpu.roll(x, LANES - 16, 1) * s2


def _rope_t(g, c, s1, s2):
    return g * c + pltpu.roll(g * s1, LANES - 16, 1) + pltpu.roll(g * s2, 16, 1)


def _rope_tables(positions):
    inv_freq = ROPE_THETA ** (-jnp.arange(0, QK_ROPE, 2, dtype=F32) / QK_ROPE)
    ang = positions.astype(F32)[:, None] * inv_freq
    cos, sin = jnp.cos(ang), jnp.sin(ang)
    n = positions.shape[0]
    one, zero = jnp.ones((n, 1), F32), jnp.zeros((n, 1), F32)
    c = jnp.concatenate([jnp.tile(one, (1, QK_NOPE)), cos, cos, jnp.tile(one, (1, 32))], axis=1)
    s1 = jnp.concatenate([jnp.tile(zero, (1, QK_NOPE + 16)), sin, jnp.tile(zero, (1, 32))], axis=1)
    s2 = jnp.concatenate([jnp.tile(zero, (1, QK_NOPE)), -sin, jnp.tile(zero, (1, 48))], axis=1)
    return c, s1, s2


def _rope_qk_fwd(qf, kf, z, tables, name):
    s = qf.shape[0]
    ts = min(s, 256)
    hw = N_HEADS * HEAD_PAD

    def body(qf_ref, kf_ref, kr_ref, c_ref, s1_ref, s2_ref, q_ref, k_ref):
        c, s1, s2 = c_ref[...], s1_ref[...], s2_ref[...]
        kr = _rope(kr_ref[...], c, s1, s2)
        for h in range(N_HEADS):
            sl = slice(h * HEAD_PAD, (h + 1) * HEAD_PAD)
            q_ref[:, sl] = _rope(qf_ref[:, sl], c, s1, s2).astype(BF16)
            k_ref[:, sl] = (kf_ref[:, sl] + kr).astype(BF16)

    tab = _rows(ts, LANES)
    return pl.pallas_call(
        body, name=name, out_shape=(jax.ShapeDtypeStruct((s, hw), BF16),) * 2, grid=(s // ts,),
        in_specs=[_rows(ts, hw), _rows(ts, hw), _rows(ts, *ZC_KR), tab, tab, tab],
        out_specs=(_rows(ts, hw), _rows(ts, hw)), compiler_params=_params("parallel"))(qf, kf, z, *tables)


def _rope_qk_bwd(dq, dk, tables, name):
    s = dq.shape[0]
    ts = min(s, 256)
    hw = N_HEADS * HEAD_PAD

    def body(dq_ref, dk_ref, c_ref, s1_ref, s2_ref, dqf_ref, dkf_ref, dkr_ref):
        c, s1, s2 = c_ref[...], s1_ref[...], s2_ref[...]
        ksum = jnp.zeros((ts, HEAD_PAD), F32)
        for h in range(N_HEADS):
            sl = slice(h * HEAD_PAD, (h + 1) * HEAD_PAD)
            dqf_ref[:, sl] = _rope_t(dq_ref[:, sl], c, s1, s2).astype(BF16)
            dkh = dk_ref[:, sl]
            dkf_ref[:, sl] = dkh.astype(BF16)
            ksum = ksum + dkh
        lane = lax.broadcasted_iota(jnp.int32, (ts, HEAD_PAD), 1)
        in_rope = (lane >= QK_NOPE) & (lane < QK_NOPE + QK_ROPE)
        dkr_ref[...] = jnp.where(in_rope, _rope_t(ksum, c, s1, s2), 0.0).astype(BF16)

    tab = _rows(ts, LANES)
    return pl.pallas_call(
        body, name=name,
        out_shape=(jax.ShapeDtypeStruct((s, hw), BF16), jax.ShapeDtypeStruct((s, hw), BF16),
                   jax.ShapeDtypeStruct((s, LANES), BF16)), grid=(s // ts,),
        in_specs=[_rows(ts, hw), _rows(ts, hw), tab, tab, tab],
        out_specs=(_rows(ts, hw), _rows(ts, hw), _rows(ts, LANES)), compiler_params=_params("parallel"))(dq, dk, *tables)


def _attn_tile(s):
    return min(s, 512)


def _raw_scores(q, k, masked):
    sc = lax.dot_general(q, k, (((1,), (1,)), ((), ())), preferred_element_type=F32)
    if masked:
        rows = lax.broadcasted_iota(jnp.int32, sc.shape, 0)
        cols = lax.broadcasted_iota(jnp.int32, sc.shape, 1)
        sc = jnp.where(cols <= rows, sc, -jnp.inf)
    return sc


def _ride_hooks(ride, refs, n_in, n_out, grid):
    if ride is None:
        return refs, lambda: None, lambda: None
    n = len(ride.arrays)
    own = refs[:n_in] + refs[n_in + n:n_in + n + n_out]
    ins, outs, sems = refs[n_in:n_in + n], refs[n_in + n + n_out:n_in + 2 * n + n_out], refs[n_in + 2 * n + n_out:]
    at_first = functools.reduce(lambda a, b: a & b, [pl.program_id(ax) == 0 for ax in range(len(grid))])
    at_last = functools.reduce(lambda a, b: a & b, [pl.program_id(ax) == g - 1 for ax, g in enumerate(grid)])
    return own, lambda: pl.when(at_first)(lambda: ride.start(ins, outs, sems)), \
        lambda: pl.when(at_last)(lambda: ride.finish(ins, outs, sems))


def _ride_call(ride, body, name, out_shape, grid, in_specs, out_specs, semantics, operands):
    n = 0 if ride is None else len(ride.arrays)
    res = pl.pallas_call(
        body, name=name, out_shape=tuple(out_shape) + (tuple(ride.out_shape) if n else ()), grid=grid,
        in_specs=list(in_specs) + [ANY] * n, out_specs=tuple(out_specs) + (ANY,) * n,
        scratch_shapes=list(ride.scratch) if n else [],
        compiler_params=_params(*(("arbitrary",) * len(grid) if n else semantics)))(*operands, *(ride.arrays if n else ()))
    return res[:len(out_shape)], list(res[len(out_shape):])


def _flash_fwd(q, k, v, name, ride=None):
    s = q.shape[0]
    t = _attn_tile(s)
    c2 = ATTN_SCALE * LOG2E
    grid = (N_HEADS, s // t)

    def body(*refs):
        (q_ref, k_ref, v_ref, o_ref, lse_ref), start, finish = _ride_hooks(ride, refs, 3, 2, grid)
        start()
        i = pl.program_id(1)
        qv = q_ref[...]

        def chunk(j, carry, masked):
            m_old, l_old, acc = carry
            at = pl.ds(pl.multiple_of(j * t, t), t)
            sc = _raw_scores(qv, k_ref[at, :], masked)
            m_new = jnp.maximum(m_old, jnp.max(sc, axis=-1, keepdims=True))
            p = jnp.exp2((sc - m_new) * c2)
            alpha = jnp.exp2((m_old - m_new) * c2)
            l_new = alpha * l_old + jnp.sum(p, axis=-1, keepdims=True)
            acc = alpha * acc + jnp.dot(p.astype(BF16), v_ref[at, :], preferred_element_type=F32)
            return m_new, l_new, acc

        init = (jnp.full((t, 1), -jnp.inf, F32), jnp.zeros((t, 1), F32), jnp.zeros((t, HEAD_PAD), F32))
        carry = lax.fori_loop(0, i, lambda j, cr: chunk(j, cr, False), init)
        m_fin, l_fin, acc = chunk(i, carry, True)
        o_ref[...] = (acc / l_fin).astype(o_ref.dtype)
        lse_ref[0] = m_fin * ATTN_SCALE + jnp.log(l_fin)
        finish()

    qo = pl.BlockSpec((t, HEAD_PAD), lambda h, i: (i, h))
    whole = pl.BlockSpec((s, HEAD_PAD), lambda h, i: (0, h))
    return _ride_call(
        ride, body, name, (jax.ShapeDtypeStruct(q.shape, BF16), jax.ShapeDtypeStruct((N_HEADS, s, 1), F32)), grid,
        [qo, whole, whole], (qo, pl.BlockSpec((1, t, 1), lambda h, i: (h, i, 0))), ("parallel", "parallel"), (q, k, v))


def _attn_delta(do, o, name):
    s = o.shape[0]
    t = _attn_tile(s)

    def body(do_ref, o_ref, delta_ref, dob_ref):
        dov = do_ref[...]
        delta_ref[0] = jnp.sum(dov * o_ref[...].astype(F32), axis=-1, keepdims=True)
        dob_ref[...] = dov.astype(BF16)

    blk = pl.BlockSpec((t, HEAD_PAD), lambda i, h: (i, h))
    return pl.pallas_call(
        body, name=name,
        out_shape=(jax.ShapeDtypeStruct((N_HEADS, s, 1), F32), jax.ShapeDtypeStruct(o.shape, BF16)),
        grid=(s // t, N_HEADS), in_specs=[blk, blk],
        out_specs=(pl.BlockSpec((1, t, 1), lambda i, h: (h, i, 0)), blk),
        compiler_params=_params("parallel", "parallel"))(do, o)


def _flash_bwd(q, k, v, do, lse, delta, name, ride=None):
    s = q.shape[0]
    t = _attn_tile(s)
    nt = s // t
    c2 = ATTN_SCALE * LOG2E
    grid = (N_HEADS, nt)

    def body(*refs):
        (q_ref, k_ref, v_ref, do_ref, lse_ref, delta_ref, dq_ref, dk_ref, dv_ref), start, finish = _ride_hooks(
            ride, refs, 6, 3, grid)
        start()
        j = pl.program_id(1)
        kv, vv = k_ref[...], v_ref[...]

        @pl.when(j == 0)
        def _():
            dq_ref[...] = jnp.zeros_like(dq_ref)

        def chunk(i, carry, masked):
            dk_acc, dv_acc = carry
            at = pl.ds(pl.multiple_of(i * t, t), t)
            qi, doi = q_ref[at, :], do_ref[at, :]
            sc = _raw_scores(qi, kv, masked)
            p = jnp.exp2(sc * c2 - lse_ref[0, at, :] * LOG2E)
            dp = lax.dot_general(doi, vv, (((1,), (1,)), ((), ())), preferred_element_type=F32)
            ds = (p * (dp - delta_ref[0, at, :])).astype(BF16)
            dv_acc = dv_acc + lax.dot_general(p.astype(BF16), doi, (((0,), (0,)), ((), ())), preferred_element_type=F32)
            dk_acc = dk_acc + lax.dot_general(ds, qi, (((0,), (0,)), ((), ())), preferred_element_type=F32)
            dq_ref[at, :] += jnp.dot(ds, kv, preferred_element_type=F32) * ATTN_SCALE
            return dk_acc, dv_acc

        zero = jnp.zeros((t, HEAD_PAD), F32)
        carry = chunk(j, (zero, zero), True)
        dk_acc, dv_acc = lax.fori_loop(j + 1, nt, lambda i, cr: chunk(i, cr, False), carry)
        dk_ref[...] = dk_acc * ATTN_SCALE
        dv_ref[...] = dv_acc.astype(BF16)
        finish()

    blk = pl.BlockSpec((t, HEAD_PAD), lambda h, j: (j, h))
    whole = pl.BlockSpec((s, HEAD_PAD), lambda h, j: (0, h))
    stat = pl.BlockSpec((1, s, 1), lambda h, j: (h, 0, 0))
    return _ride_call(
        ride, body, name, (jax.ShapeDtypeStruct(q.shape, F32), jax.ShapeDtypeStruct(q.shape, F32),
                           jax.ShapeDtypeStruct(q.shape, BF16)), grid,
        [whole, blk, blk, whole, stat, stat], (whole, blk, blk), ("parallel", "arbitrary"), (q, k, v, do, lse, delta))


def _conv_tile(s):
    return min(s, 256)


def _halo_before(t, width, cidx):
    per = t // CONV_HALO
    return pl.BlockSpec((CONV_HALO, width), lambda i: (jnp.maximum(i * per - 1, 0), cidx))


def _halo_after(t, width, cidx, n_tiles):
    per = t // CONV_HALO
    last = n_tiles * per - 1
    return pl.BlockSpec((CONV_HALO, width), lambda i: (jnp.minimum((i + 1) * per, last), cidx))


def _fill_glu(hbuf, ap_ref, gp_ref, a_ref, g_ref, t):
    first = pl.program_id(0) == 0
    hbuf[pl.ds(0, CONV_HALO), :] = jnp.where(first, 0.0, ap_ref[...] * _sigmoid(gp_ref[...]))
    hbuf[pl.ds(CONV_HALO, t), :] = a_ref[...] * _sigmoid(g_ref[...])


def _layer_norm_parts(co):
    mu = jnp.mean(co, axis=-1, keepdims=True)
    xc = co - mu
    rstd = lax.rsqrt(jnp.mean(xc * xc, axis=-1, keepdims=True) + EPS)
    return xc * rstd, rstd


def _conv_fwd(z, conv_w, conv_b, ln_g, ln_b, name):
    s = z.shape[0]
    t = _conv_tile(s)
    off = CONV_HALO - (CONV_W - 1)

    def body(ap_ref, gp_ref, a_ref, g_ref, w_ref, b_ref, lg_ref, lb_ref, hc_ref, co_ref, hbuf):
        _fill_glu(hbuf, ap_ref, gp_ref, a_ref, g_ref, t)
        acc = jnp.zeros((t, CONV_C), F32) + b_ref[...]
        for j in range(CONV_W):
            acc = acc + hbuf[pl.ds(off + j, t), :] * w_ref[pl.ds(j, 1), :]
        co_ref[...] = acc
        xh, _ = _layer_norm_parts(acc)
        y = xh * lg_ref[...] + lb_ref[...]
        hc_ref[...] = (y * _sigmoid(y)).astype(BF16)

    vec = _fixed((1, CONV_C))
    return pl.pallas_call(
        body, name=name, out_shape=(jax.ShapeDtypeStruct((s, CONV_C), BF16), jax.ShapeDtypeStruct((s, CONV_C), F32)),
        grid=(s // t,),
        in_specs=[_halo_before(t, *ZC_CONV_A), _halo_before(t, *ZC_CONV_G), _rows(t, *ZC_CONV_A), _rows(t, *ZC_CONV_G),
                  _fixed((CONV_HALO, CONV_C)), vec, vec, vec],
        out_specs=(_rows(t, CONV_C), _rows(t, CONV_C)), scratch_shapes=[pltpu.VMEM((t + CONV_HALO, CONV_C), F32)],
        compiler_params=_params("parallel"))(z, z, z, z, conv_w, conv_b.reshape(1, -1), ln_g.reshape(1, -1),
                                             ln_b.reshape(1, -1))


def _conv_bwd_norm(dhc, co, ln_g, ln_b, name):
    s = co.shape[0]
    t = min(s, 512)

    def body(dhc_ref, co_ref, lg_ref, lb_ref, dco_ref, dg_ref, db_ref, dcb_ref):
        xh, rstd = _layer_norm_parts(co_ref[...])
        y = xh * lg_ref[...] + lb_ref[...]
        sg = _sigmoid(y)
        dy = dhc_ref[...] * (sg * (1.0 + y * (1.0 - sg)))
        dxh = dy * lg_ref[...]
        dco = rstd * (dxh - jnp.mean(dxh, axis=-1, keepdims=True) - xh * jnp.mean(dxh * xh, axis=-1, keepdims=True))
        dco_ref[...] = dco

        @pl.when(pl.program_id(0) == 0)
        def _():
            dg_ref[...] = jnp.zeros_like(dg_ref)
            db_ref[...] = jnp.zeros_like(db_ref)
            dcb_ref[...] = jnp.zeros_like(dcb_ref)

        dg_ref[...] += jnp.sum(dy * xh, axis=0, keepdims=True)
        db_ref[...] += jnp.sum(dy, axis=0, keepdims=True)
        dcb_ref[...] += jnp.sum(dco, axis=0, keepdims=True)

    vec = _fixed((1, CONV_C))
    one = jax.ShapeDtypeStruct((1, CONV_C), F32)
    dco, dg, db, dcb = pl.pallas_call(
        body, name=name, out_shape=(jax.ShapeDtypeStruct((s, CONV_C), F32), one, one, one), grid=(s // t,),
        in_specs=[_rows(t, CONV_C), _rows(t, CONV_C), vec, vec], out_specs=(_rows(t, CONV_C), vec, vec, vec),
        compiler_params=_params("arbitrary"))(dhc, co, ln_g.reshape(1, -1), ln_b.reshape(1, -1))
    return dco, dg.reshape(-1), db.reshape(-1), dcb.reshape(-1)


def _conv_bwd_taps(dco, z, conv_w, name):
    s = z.shape[0]
    t = _conv_tile(s)
    nt = s // t
    off = CONV_HALO - (CONV_W - 1)

    def body(ap_ref, gp_ref, a_ref, g_ref, d_ref, dn_ref, w_ref, du_ref, dw_ref, hbuf, dbuf):
        i = pl.program_id(0)
        _fill_glu(hbuf, ap_ref, gp_ref, a_ref, g_ref, t)
        dbuf[pl.ds(0, t), :] = d_ref[...]
        dbuf[pl.ds(t, CONV_HALO), :] = jnp.where(i == nt - 1, 0.0, dn_ref[...])

        @pl.when(i == 0)
        def _():
            dw_ref[...] = jnp.zeros_like(dw_ref)

        dcur = d_ref[...]
        dh = jnp.zeros((t, CONV_C), F32)
        for j in range(CONV_W):
            dh = dh + dbuf[pl.ds(CONV_W - 1 - j, t), :] * w_ref[pl.ds(j, 1), :]
            dw_ref[pl.ds(j, 1), :] += jnp.sum(dcur * hbuf[pl.ds(off + j, t), :], axis=0, keepdims=True)
        a, sg = a_ref[...], _sigmoid(g_ref[...])
        du_ref[:, pl.ds(0, CONV_C)] = (dh * sg).astype(BF16)
        du_ref[:, pl.ds(CONV_C, CONV_C)] = (dh * a * sg * (1.0 - sg)).astype(BF16)

    return pl.pallas_call(
        body, name=name,
        out_shape=(jax.ShapeDtypeStruct((s, 2 * CONV_C), BF16), jax.ShapeDtypeStruct((CONV_HALO, CONV_C), F32)),
        grid=(nt,),
        in_specs=[_halo_before(t, *ZC_CONV_A), _halo_before(t, *ZC_CONV_G), _rows(t, *ZC_CONV_A), _rows(t, *ZC_CONV_G),
                  _rows(t, CONV_C), _halo_after(t, CONV_C, 0, nt), _fixed((CONV_HALO, CONV_C))],
        out_specs=(_rows(t, 2 * CONV_C), _fixed((CONV_HALO, CONV_C))),
        scratch_shapes=[pltpu.VMEM((t + CONV_HALO, CONV_C), F32), pltpu.VMEM((t + CONV_HALO, CONV_C), F32)],
        compiler_params=_params("arbitrary"))(z, z, z, z, dco, dco, conv_w)


def _pool_tile(s):
    return min(s, 512)


def _pool_counts(row0, n, window):
    rows = row0 + lax.broadcasted_iota(jnp.int32, (n, POOL_GD), 0)
    return jnp.minimum(rows + 1, window).astype(F32)


def _pool_diff(ubuf, gi, window, row0, t):
    lanes = pl.ds(gi * POOL_GD, POOL_GD)
    tot = ubuf[pl.ds(CONV_HALO, t), lanes]
    cur = tot
    for back in range(1, window):
        tot = tot + ubuf[pl.ds(CONV_HALO - back, t), lanes]
    return tot / _pool_counts(row0, t, window) - cur


def _pool_fwd(z, pool_w, pool_scale, name):
    s = z.shape[0]
    t = _pool_tile(s)

    def body(up_ref, u_ref, w_ref, sc_ref, m_ref, ubuf):
        i = pl.program_id(0)
        ubuf[pl.ds(0, CONV_HALO), :] = jnp.where(i == 0, 0.0, up_ref[...])
        ubuf[pl.ds(CONV_HALO, t), :] = u_ref[...]
        for gi, window in enumerate(POOL_WINDOWS):
            d = _pool_diff(ubuf, gi, window, i * t, t)
            mm = jnp.dot(d.astype(BF16), w_ref[gi].astype(BF16), preferred_element_type=F32)
            lanes = pl.ds(gi * POOL_GD, POOL_GD)
            m_ref[:, lanes] = (mm * sc_ref[:, lanes]).astype(BF16)

    return pl.pallas_call(
        body, name=name, out_shape=jax.ShapeDtypeStruct((s, POOL_C), BF16), grid=(s // t,),
        in_specs=[_halo_before(t, *ZC_POOL), _rows(t, *ZC_POOL), _fixed((POOL_G, POOL_GD, POOL_GD)), _fixed((1, POOL_C))],
        out_specs=_rows(t, POOL_C), scratch_shapes=[pltpu.VMEM((t + CONV_HALO, POOL_C), F32)],
        compiler_params=_params("parallel"))(z, z, pool_w, pool_scale.reshape(1, -1))


def _pool_bwd(dm, z, pool_w, pool_scale, name):
    s = z.shape[0]
    t = _pool_tile(s)
    nt = s // t

    def body(up_ref, u_ref, dm_ref, dmn_ref, w_ref, sc_ref, du_ref, dw_ref, dsc_ref, ubuf, ebuf):
        i = pl.program_id(0)
        ubuf[pl.ds(0, CONV_HALO), :] = jnp.where(i == 0, 0.0, up_ref[...])
        ubuf[pl.ds(CONV_HALO, t), :] = u_ref[...]

        @pl.when(i == 0)
        def _():
            dw_ref[...] = jnp.zeros_like(dw_ref)
            dsc_ref[...] = jnp.zeros_like(dsc_ref)

        dm_next = jnp.where(i == nt - 1, 0.0, dmn_ref[...])
        for gi, window in enumerate(POOL_WINDOWS):
            lanes = pl.ds(gi * POOL_GD, POOL_GD)
            wb = w_ref[gi].astype(BF16)
            scale = sc_ref[:, lanes]
            d = _pool_diff(ubuf, gi, window, i * t, t).astype(BF16)
            mm = jnp.dot(d, wb, preferred_element_type=F32)
            dmv = dm_ref[:, lanes]
            dsc_ref[:, lanes] += jnp.sum(dmv * mm, axis=0, keepdims=True)
            dmm = (dmv * scale).astype(BF16)
            dw_ref[gi] += lax.dot_general(d, dmm, (((0,), (0,)), ((), ())), preferred_element_type=F32)
            dd = lax.dot_general(dmm, wb, (((1,), (1,)), ((), ())), preferred_element_type=F32)
            dd_next = lax.dot_general((dm_next[:, gi * POOL_GD:(gi + 1) * POOL_GD] * scale).astype(BF16), wb,
                                      (((1,), (1,)), ((), ())), preferred_element_type=F32)
            ebuf[pl.ds(0, t), lanes] = dd / _pool_counts(i * t, t, window)
            ebuf[pl.ds(t, CONV_HALO), lanes] = dd_next / _pool_counts((i + 1) * t, CONV_HALO, window)
            du = -dd
            for ahead in range(window):
                du = du + ebuf[pl.ds(ahead, t), lanes]
            du_ref[:, lanes] = du.astype(BF16)

    du, dw, dsc = pl.pallas_call(
        body, name=name,
        out_shape=(jax.ShapeDtypeStruct((s, POOL_C), BF16), jax.ShapeDtypeStruct((POOL_G, POOL_GD, POOL_GD), F32),
                   jax.ShapeDtypeStruct((1, POOL_C), F32)), grid=(nt,),
        in_specs=[_halo_before(t, *ZC_POOL), _rows(t, *ZC_POOL), _rows(t, POOL_C), _halo_after(t, POOL_C, 0, nt),
                  _fixed((POOL_G, POOL_GD, POOL_GD)), _fixed((1, POOL_C))],
        out_specs=(_rows(t, POOL_C), _fixed((POOL_G, POOL_GD, POOL_GD)), _fixed((1, POOL_C))),
        scratch_shapes=[pltpu.VMEM((t + CONV_HALO, POOL_C), F32), pltpu.VMEM((t + CONV_HALO, POOL_C), F32)],
        compiler_params=_params("arbitrary"))(z, z, dm, dm, pool_w, pool_scale.reshape(1, -1))
    return du, dw, dsc.reshape(-1)


def _gate_specs(ts):
    width, first = ZC_GATE
    return [_rows(ts, width, first + b) for b in range(3)]


def _merge_fwd(z, ys, name):
    s = z.shape[0]
    ts = min(s, 256)

    def body(g0, g1, g2, y0, y1, y2, o_ref):
        o_ref[...] = (_sigmoid(g0[...]) * y0[...] + _sigmoid(g1[...]) * y1[...]
                      + _sigmoid(g2[...]) * y2[...]).astype(BF16)

    return pl.pallas_call(
        body, name=name, out_shape=jax.ShapeDtypeStruct((s, D_MODEL), BF16), grid=(s // ts,),
        in_specs=_gate_specs(ts) + [_rows(ts, D_MODEL)] * 3, out_specs=_rows(ts, D_MODEL),
        compiler_params=_params("parallel"))(z, z, z, *ys)


def _merge_bwd(z, ys, dmerged, name):
    s = z.shape[0]
    ts = min(s, 256)

    def body(g0, g1, g2, y0, y1, y2, dm_ref, dy0, dy1, dy2, dg0, dg1, dg2):
        dmv = dm_ref[...]
        for g_ref, y_ref, dy_ref, dg_ref in ((g0, y0, dy0, dg0), (g1, y1, dy1, dg1), (g2, y2, dy2, dg2)):
            sg = _sigmoid(g_ref[...])
            dy_ref[...] = (dmv * sg).astype(BF16)
            dg_ref[...] = (dmv * y_ref[...] * sg * (1.0 - sg)).astype(BF16)

    out = jax.ShapeDtypeStruct((s, D_MODEL), BF16)
    return pl.pallas_call(
        body, name=name, out_shape=(out,) * 6, grid=(s // ts,),
        in_specs=_gate_specs(ts) + [_rows(ts, D_MODEL)] * 4, out_specs=(_rows(ts, D_MODEL),) * 6,
        compiler_params=_params("parallel"))(z, z, z, *ys, dmerged)


def _swiglu_fwd(hg, hu, name):
    s, f = hg.shape
    ts, tc = min(s, 512), _tile(f, 1024)
    blk = pl.BlockSpec((ts, tc), lambda i, j: (i, j))

    def body(g_ref, u_ref, o_ref):
        g = g_ref[...]
        o_ref[...] = (g * _sigmoid(g) * u_ref[...]).astype(BF16)

    return pl.pallas_call(
        body, name=name, out_shape=jax.ShapeDtypeStruct((s, f), BF16), grid=(s // ts, f // tc),
        in_specs=[blk, blk], out_specs=blk, compiler_params=_params("parallel", "parallel"))(hg, hu)


def _swiglu_bwd(hg, hu, dact, name):
    s, f = hg.shape
    ts, tc = min(s, 512), _tile(f, 1024)
    blk = pl.BlockSpec((ts, tc), lambda i, j: (i, j))

    def body(g_ref, u_ref, d_ref, dg_ref, du_ref):
        g, d = g_ref[...], d_ref[...]
        sg = _sigmoid(g)
        dg_ref[...] = (d * u_ref[...] * (sg * (1.0 + g * (1.0 - sg)))).astype(BF16)
        du_ref[...] = (d * g * sg).astype(BF16)

    out = jax.ShapeDtypeStruct((s, f), BF16)
    return pl.pallas_call(
        body, name=name, out_shape=(out, out), grid=(s // ts, f // tc), in_specs=[blk, blk, blk], out_specs=(blk, blk),
        compiler_params=_params("parallel", "parallel"))(hg, hu, dact)


def _loss_grad(y, target, name):
    s, d = y.shape
    ts = min(s, 512)

    def body(y_ref, t_ref, dy_ref, sq_ref):
        e = y_ref[...] - t_ref[...]
        dy_ref[...] = e / d

        @pl.when(pl.program_id(0) == 0)
        def _():
            sq_ref[...] = jnp.zeros_like(sq_ref)

        sq_ref[...] += jnp.sum(e * e, axis=0, keepdims=True)

    return pl.pallas_call(
        body, name=name, out_shape=(jax.ShapeDtypeStruct((s, d), F32), jax.ShapeDtypeStruct((1, d), F32)),
        grid=(s // ts,), in_specs=[_rows(ts, d), _rows(ts, d)], out_specs=(_rows(ts, d), _fixed((1, d))),
        compiler_params=_params("arbitrary"))(y, target)


def _adamw(w, g, m, v, name):
    shape = w.shape
    cols = shape[-1]
    rows = math.prod(shape[:-1])
    tr = _row_tile(rows, cols * 4)

    def body(w_ref, g_ref, m_ref, v_ref, d_ref, mo_ref, vo_ref):
        gv = g_ref[...]
        mn = B1 * m_ref[...] + (1.0 - B1) * gv
        vn = B2 * v_ref[...] + (1.0 - B2) * (gv * gv)
        m_hat = mn / (1.0 - B1 ** STEP)
        v_hat = vn / (1.0 - B2 ** STEP)
        d_ref[...] = -LR * (m_hat / (jnp.sqrt(v_hat) + ADAM_EPS) + WD * w_ref[...])
        mo_ref[...] = mn
        vo_ref[...] = vn

    spec = _rows(tr, cols)
    out = jax.ShapeDtypeStruct((rows, cols), F32)
    res = pl.pallas_call(
        body, name=name, out_shape=(out,) * 3, grid=(rows // tr,), in_specs=[spec] * 4, out_specs=(spec,) * 3,
        compiler_params=_params("parallel"))(*[t.reshape(rows, cols) for t in (w, g, m, v)])
    return tuple(r.reshape(shape) for r in res)


ANY = pl.BlockSpec(memory_space=pl.ANY)


class _GatherRide:
    def __init__(self, arrays):
        n = len(arrays)
        self.arrays = list(arrays)
        self.out_shape = [jax.ShapeDtypeStruct((N_DEV,) + a.shape, a.dtype) for a in arrays]
        self.scratch = [pltpu.SemaphoreType.DMA((n, 7)), pltpu.SemaphoreType.DMA((n, 7)), pltpu.SemaphoreType.DMA((n,))]

    def _copies(self, ins, outs, sems):
        send_sems, recv_sems, local_sems = sems
        n = len(self.arrays)
        x, y, c = lax.axis_index("x"), lax.axis_index("y"), lax.axis_index("c")
        me, sibling = (x, y, c), (x, y, 1 - c)
        chips = [(1 - x, y), (x, 1 - y), (1 - x, 1 - y)]

        def slot(a, px, py, pc):
            return outs[a].at[4 * px + 2 * py + pc]

        def copy(a, k, block, to, src=None):
            return pltpu.make_async_remote_copy(
                src_ref=slot(a, *block) if src is None else src, dst_ref=slot(a, *block), send_sem=send_sems.at[a, k],
                recv_sem=recv_sems.at[a, k], device_id=to, device_id_type=MESH)

        mine = [pltpu.make_async_copy(ins[a], slot(a, *me), local_sems.at[a]) for a in range(n)]
        first = []
        for a in range(n):
            first.append(copy(a, 0, me, sibling, src=ins[a]))
            first += [copy(a, 1 + j, me, (*chip, c), src=ins[a]) for j, chip in enumerate(chips)]
        return n, me, sibling, chips, c, copy, mine, first

    def start(self, ins, outs, sems):
        _, _, _, _, _, _, mine, first = self._copies(ins, outs, sems)
        for cp in mine + first:
            cp.start()

    def finish(self, ins, outs, sems):
        n, me, sibling, chips, c, copy, mine, first = self._copies(ins, outs, sems)
        passed = []
        for j, chip in enumerate(chips):
            for a in range(n):
                copy(a, 1 + j, (*chip, c), me).wait_recv()
                passed.append(copy(a, 4 + j, (*chip, c), sibling))
                passed[-1].start()
        for a in range(n):
            copy(a, 0, sibling, me).wait_recv()
            for j, chip in enumerate(chips):
                copy(a, 4 + j, (*chip, 1 - c), me).wait_recv()
        for cp in first + passed:
            cp.wait_send()
        for cp in mine:
            cp.wait()


class _ReduceRide:
    def __init__(self, arrays):
        n = len(arrays)
        self.arrays = list(arrays)
        self.out_shape = [jax.ShapeDtypeStruct(a.shape, a.dtype) for a in arrays]
        self.scratch = [pltpu.SemaphoreType.DMA((n, 7)), pltpu.SemaphoreType.DMA((n, 7)), pltpu.SemaphoreType.DMA((n,))]

    def _copies(self, ins, outs, sems):
        send_sems, recv_sems, local_sems = sems
        n = len(self.arrays)
        x, y, c = lax.axis_index("x"), lax.axis_index("y"), lax.axis_index("c")
        mine = [pltpu.make_async_copy(ins[a].at[4 * x + 2 * y + c], outs[a].at[0], local_sems.at[a]) for a in range(n)]
        copies = []
        for a in range(n):
            for k in range(1, N_DEV):
                px = 1 - x if k & 4 else x
                py = 1 - y if k & 2 else y
                pc = 1 - c if k & 1 else c
                copies.append(pltpu.make_async_remote_copy(
                    src_ref=ins[a].at[4 * px + 2 * py + pc], dst_ref=outs[a].at[k], send_sem=send_sems.at[a, k - 1],
                    recv_sem=recv_sems.at[a, k - 1], device_id=(px, py, pc), device_id_type=MESH))
        return mine, copies

    def start(self, ins, outs, sems):
        mine, copies = self._copies(ins, outs, sems)
        for cp in mine + copies:
            cp.start()

    def finish(self, ins, outs, sems):
        mine, copies = self._copies(ins, outs, sems)
        for cp in copies + mine:
            cp.wait()


def _run_ride(ride, name):
    n = len(ride.arrays)

    def body(*refs):
        ins, outs, sems = refs[:n], refs[n:2 * n], refs[2 * n:]
        ride.start(ins, outs, sems)
        ride.finish(ins, outs, sems)

    return pl.pallas_call(body, name=name, out_shape=ride.out_shape, in_specs=[ANY] * n, out_specs=[ANY] * n,
                          scratch_shapes=ride.scratch)(*ride.arrays)


def _all_gather(arrays, name):
    return _run_ride(_GatherRide(arrays), name)


def _swap_with_sibling(arrays, name):
    n = len(arrays)

    def body(*refs):
        ins, outs = refs[:n], refs[n:2 * n]
        send_sems, recv_sems = refs[2 * n:]
        x, y, c = lax.axis_index("x"), lax.axis_index("y"), lax.axis_index("c")
        copies = [pltpu.make_async_remote_copy(
            src_ref=ins[a].at[1 - c], dst_ref=outs[a], send_sem=send_sems.at[a], recv_sem=recv_sems.at[a],
            device_id=(x, y, 1 - c), device_id_type=MESH) for a in range(n)]
        for cp in copies:
            cp.start()
        for cp in copies:
            cp.wait()

    return pl.pallas_call(
        body, name=name, out_shape=[jax.ShapeDtypeStruct(a.shape[1:], a.dtype) for a in arrays],
        in_specs=[ANY] * n, out_specs=[ANY] * n,
        scratch_shapes=[pltpu.SemaphoreType.DMA((n,)), pltpu.SemaphoreType.DMA((n,))])(*arrays)


def _exchange_chips(arrays, name):
    n = len(arrays)

    def body(*refs):
        ins, outs = refs[:n], refs[n:2 * n]
        send_sems, recv_sems, local_sems = refs[2 * n:]
        x, y, c = lax.axis_index("x"), lax.axis_index("y"), lax.axis_index("c")
        partners = [(x, 1 - y), (1 - x, y), (1 - x, 1 - y)]
        mine = [pltpu.make_async_copy(ins[a].at[2 * x + y], outs[a].at[0], local_sems.at[a]) for a in range(n)]
        copies = [pltpu.make_async_remote_copy(
            src_ref=ins[a].at[2 * px + py], dst_ref=outs[a].at[1 + k], send_sem=send_sems.at[a, k],
            recv_sem=recv_sems.at[a, k], device_id=(px, py, c), device_id_type=MESH)
            for a in range(n) for k, (px, py) in enumerate(partners)]
        for cp in mine + copies:
            cp.start()
        for cp in copies + mine:
            cp.wait()

    return pl.pallas_call(
        body, name=name, out_shape=[jax.ShapeDtypeStruct(a.shape, a.dtype) for a in arrays],
        in_specs=[ANY] * n, out_specs=[ANY] * n,
        scratch_shapes=[pltpu.SemaphoreType.DMA((n, 3)), pltpu.SemaphoreType.DMA((n, 3)), pltpu.SemaphoreType.DMA((n,))],
    )(*arrays)


def _as_rows(a, lead):
    return a.reshape(a.shape[:lead] + (math.prod(a.shape[lead:-1]), a.shape[-1]))


def _add_pairs(a, b, name):
    a2, b2 = _as_rows(a, 0), _as_rows(b, 0)
    rows, cols = a2.shape
    tr = _row_tile(rows, cols * 4)

    def body(a_ref, b_ref, o_ref):
        o_ref[...] = (a_ref[...].astype(F32) + b_ref[...].astype(F32)).astype(o_ref.dtype)

    spec = _rows(tr, cols)
    out = pl.pallas_call(body, name=name, out_shape=jax.ShapeDtypeStruct(a2.shape, a.dtype), grid=(rows // tr,),
                         in_specs=[spec, spec], out_specs=spec, compiler_params=_params("parallel"))(a2, b2)
    return out.reshape(a.shape)


def _sum_blocks(a, name):
    a3 = _as_rows(a, 1)
    n, rows, cols = a3.shape
    tr = _row_tile(rows, n * cols * 4)

    def body(a_ref, o_ref):
        tot = a_ref[0].astype(F32)
        for k in range(1, n):
            tot = tot + a_ref[k].astype(F32)
        o_ref[...] = tot

    out = pl.pallas_call(body, name=name, out_shape=jax.ShapeDtypeStruct((rows, cols), F32), grid=(rows // tr,),
                         in_specs=[pl.BlockSpec((n, tr, cols), lambda j: (0, j, 0))], out_specs=_rows(tr, cols),
                         compiler_params=_params("parallel"))(a3)
    return out.reshape(a.shape[1:])


MIX_GROUPS = ("w_in", "w_uq", "w_ukv", "w_o3", "w_mix_o")
FFN_GROUPS = ("w_gu", "w_down")


def _pad_axis(a, axis, size):
    pad = [(0, 0)] * a.ndim
    pad[axis] = (0, size - a.shape[axis])
    return jnp.pad(a, pad)


def _local_groups(sh, l):
    return {
        "w_in": sh["w_in"][l].astype(BF16),
        "w_uq": _pad_axis(sh["w_uq"][l], -1, HEAD_PAD).astype(BF16),
        "w_ukv": jnp.stack([_pad_axis(sh["w_uk"][l], -1, HEAD_PAD), _pad_axis(sh["w_uv"][l], -1, HEAD_PAD)]).astype(BF16),
        "w_o3": jnp.stack([sh["w_attn_o"][l], sh["w_conv_o"][l], sh["w_pool_o"][l]]).astype(BF16),
        "w_mix_o": sh["w_mix_o"][l].astype(BF16),
        "w_gu": jnp.stack([_pad_axis(sh["w_gate"][l], -1, FF_SHARD_PAD),
                           _pad_axis(sh["w_up"][l], -1, FF_SHARD_PAD)]).astype(BF16),
        "w_down": _pad_axis(sh["w_down"][l], 0, FF_SHARD_PAD).astype(BF16),
    }


def _cols(blocks):
    return blocks.transpose(1, 0, 2).reshape(blocks.shape[1], -1)


def _arrange_w_in(blocks):
    parts, pos = [], 0
    for ref_lo, ref_hi, at in sorted(W_IN_PIECES, key=lambda p: p[2]):
        if at > pos:
            parts.append(jnp.zeros((blocks.shape[1], at - pos), blocks.dtype))
        for d in range(N_DEV):
            lo, hi = max(ref_lo, d * W_IN_SHARD), min(ref_hi, (d + 1) * W_IN_SHARD)
            if lo < hi:
                parts.append(blocks[d][:, lo - d * W_IN_SHARD:hi - d * W_IN_SHARD])
        pos = at + ref_hi - ref_lo
    if pos < Z_W:
        parts.append(jnp.zeros((blocks.shape[1], Z_W - pos), blocks.dtype))
    return jnp.concatenate(parts, axis=1)


def _w_in_shard(g, d):
    parts = []
    for ref_lo, ref_hi, at in W_IN_PIECES:
        lo, hi = max(ref_lo, d * W_IN_SHARD), min(ref_hi, (d + 1) * W_IN_SHARD)
        if lo < hi:
            parts.append(g[:, at + lo - ref_lo:at + hi - ref_lo])
    return jnp.concatenate(parts, axis=1)


def _mixer_weights(gat):
    attn_o = _cols(gat["w_o3"][:, 0]).reshape(N_HEADS, V_HEAD, D_MODEL)
    return {
        "w_in": _arrange_w_in(gat["w_in"]),
        "w_uq": _cols(gat["w_uq"]),
        "w_uk": _cols(gat["w_ukv"][:, 0]),
        "w_uv": _cols(gat["w_ukv"][:, 1]),
        "w_attn_o": _pad_axis(attn_o, 1, HEAD_PAD).reshape(N_HEADS * HEAD_PAD, D_MODEL),
        "w_conv_o": _cols(gat["w_o3"][:, 1]),
        "w_pool_o": _cols(gat["w_o3"][:, 2]),
        "w_mix_o": gat["w_mix_o"].reshape(D_MODEL, D_MODEL),
    }


def _ffn_weights(gat):
    return {"w_gate": _cols(gat["w_gu"][:, 0]), "w_up": _cols(gat["w_gu"][:, 1]),
            "w_down": gat["w_down"].reshape(D_FF_PAD, D_MODEL)}


def _by_cols(g, nb):
    return g.reshape(g.shape[0], N_DEV, nb).transpose(1, 0, 2)


def _mixer_grad_groups(gb):
    b = {k: v.astype(BF16) for k, v in gb.items()}
    attn_o = b["w_attn_o"].reshape(N_HEADS, HEAD_PAD, D_MODEL)[:, :V_HEAD].reshape(N_HEADS * V_HEAD, D_MODEL)
    return {
        "w_in": jnp.stack([_w_in_shard(b["w_in"], d) for d in range(N_DEV)]),
        "w_uq": _by_cols(b["w_uq"], HEAD_PAD),
        "w_ukv": jnp.stack([_by_cols(b["w_uk"], HEAD_PAD), _by_cols(b["w_uv"], HEAD_PAD)], axis=1),
        "w_o3": jnp.stack([_by_cols(attn_o, LANES), _by_cols(b["w_conv_o"], LANES), _by_cols(b["w_pool_o"], LANES)], axis=1),
        "w_mix_o": b["w_mix_o"].reshape(N_DEV, D_MODEL // N_DEV, D_MODEL),
    }


def _ffn_grad_groups(gb):
    b = {k: v.astype(BF16) for k, v in gb.items()}
    return {"w_gu": jnp.stack([_by_cols(b["w_gate"], FF_SHARD_PAD), _by_cols(b["w_up"], FF_SHARD_PAD)], axis=1),
            "w_down": b["w_down"].reshape(N_DEV, FF_SHARD_PAD, D_MODEL)}


def _grads_from_groups(tot):
    qk = QK_NOPE + QK_ROPE
    return {
        "w_in": tot["w_in"], "w_uq": tot["w_uq"][..., :qk],
        "w_uk": tot["w_ukv"][0][..., :QK_NOPE], "w_uv": tot["w_ukv"][1][..., :V_HEAD],
        "w_attn_o": tot["w_o3"][0], "w_conv_o": tot["w_o3"][1], "w_pool_o": tot["w_o3"][2],
        "w_mix_o": tot["w_mix_o"],
        "w_gate": tot["w_gu"][0][..., :FF_SHARD], "w_up": tot["w_gu"][1][..., :FF_SHARD],
        "w_down": tot["w_down"][:FF_SHARD],
    }


SMALL_GROUPS = (
    (D_MODEL, ("mix_norm_pre", "mix_norm_post", "ffn_norm_pre", "ffn_norm_post")),
    (CONV_C, ("conv_w", "conv_b", "conv_ln_g", "conv_ln_b", "pool_scale")),
    (Q_RANK, ("q_norm",)), (KV_RANK, ("kv_norm",)), (POOL_GD, ("pool_w",)),
)


def _small_rows(name):
    return {"conv_w": CONV_HALO, "pool_w": POOL_G * POOL_GD}.get(name, SUBLANES)


def _small_groups(small):
    out = []
    for width, names in SMALL_GROUPS:
        parts = []
        for l in range(DEPTH):
            for n in names:
                part = small[l][n].reshape(-1, width)
                parts.append(_pad_axis(part, 0, _small_rows(n)))
        out.append(jnp.concatenate(parts, axis=0))
    return out


def _small_from_groups(groups):
    shapes = {"conv_w": (CONV_W, CONV_C), "pool_w": (POOL_G, POOL_GD, POOL_GD)}
    out = {}
    for (width, names), g in zip(SMALL_GROUPS, groups):
        row = 0
        for l in range(DEPTH):
            for n in names:
                rows = _small_rows(n)
                real = {"conv_w": CONV_W, "pool_w": POOL_G * POOL_GD}.get(n, 1)
                out.setdefault(n, []).append(g[row:row + real].reshape(shapes.get(n, (width,))))
                row += rows
    return {n: jnp.stack(v) for n, v in out.items()}


def _mixer_fwd(x, tables, w, sm, tag, ride):
    nm = lambda n: f"{n}_{tag}"
    h = _rms_fwd(x, (D_MODEL, 0), sm["mix_norm_pre"], BF16, nm("mix_pre_norm"))
    z = _matmul(h, w["w_in"], "nn", F32, nm("in_proj"))
    cq = _rms_fwd(z, ZC_Q, sm["q_norm"], BF16, nm("q_norm"))
    ckv = _rms_fwd(z, ZC_KV, sm["kv_norm"], BF16, nm("kv_norm"))
    qf = _matmul(cq, w["w_uq"], "nn", F32, nm("q_up"))
    kf = _matmul(ckv, w["w_uk"], "nn", F32, nm("k_up"))
    v = _matmul(ckv, w["w_uv"], "nn", BF16, nm("v_up"))
    q, k = _rope_qk_fwd(qf, kf, z, tables, nm("rope_qk"))
    (o, lse), rode = _flash_fwd(q, k, v, nm("flash_fwd"), ride)
    y_attn = _matmul(o, w["w_attn_o"], "nn", F32, nm("attn_out"))
    hc, co = _conv_fwd(z, sm["conv_w"], sm["conv_b"], sm["conv_ln_g"], sm["conv_ln_b"], nm("conv_fwd"))
    y_conv = _matmul(hc, w["w_conv_o"], "nn", F32, nm("conv_out"))
    pm = _pool_fwd(z, sm["pool_w"], sm["pool_scale"], nm("pool_fwd"))
    y_pool = _matmul(pm, w["w_pool_o"], "nn", F32, nm("pool_out"))
    ys = (y_attn, y_conv, y_pool)
    merged = _merge_fwd(z, ys, nm("merge_fwd"))
    mo = _matmul(merged, w["w_mix_o"], "nn", F32, nm("mix_out"))
    x_mid = _rms_fwd(mo, (D_MODEL, 0), sm["mix_norm_post"], F32, nm("mix_post_norm"), res=x)
    saved = dict(x=x, h=h, z=z, cq=cq, ckv=ckv, q=q, k=k, v=v, o=o, lse=lse, hc=hc, co=co, pm=pm, ys=ys, merged=merged,
                 mo=mo)
    return x_mid, saved, rode


def _ffn_fwd(x_mid, w, sm, tag):
    nm = lambda n: f"{n}_{tag}"
    h2 = _rms_fwd(x_mid, (D_MODEL, 0), sm["ffn_norm_pre"], BF16, nm("ffn_pre_norm"))
    hg = _matmul(h2, w["w_gate"], "nn", F32, nm("ffn_gate"))
    hu = _matmul(h2, w["w_up"], "nn", F32, nm("ffn_up"))
    act = _swiglu_fwd(hg, hu, nm("swiglu_fwd"))
    fo = _matmul(act, w["w_down"], "nn", F32, nm("ffn_down"))
    out = _rms_fwd(fo, (D_MODEL, 0), sm["ffn_norm_post"], F32, nm("ffn_post_norm"), res=x_mid)
    saved = dict(x_mid=x_mid, h2=h2, hg=hg, hu=hu, act=act, fo=fo)
    return out, saved


def _ffn_bwd(dout, sv, w, sm, tag):
    nm = lambda n: f"{n}_{tag}"
    gb, gs = {}, {}
    dfo, gs["ffn_norm_post"] = _rms_bwd(sv["fo"], (D_MODEL, 0), sm["ffn_norm_post"], dout, BF16, nm("ffn_post_norm_bwd"))
    dact = _matmul(dfo, w["w_down"], "nt", F32, nm("ffn_down_dx"))
    gb["w_down"] = _matmul(sv["act"], dfo, "tn", F32, nm("ffn_down_dw"))
    dhg, dhu = _swiglu_bwd(sv["hg"], sv["hu"], dact, nm("swiglu_bwd"))
    dh2_g = _matmul(dhg, w["w_gate"], "nt", F32, nm("ffn_gate_dx"))
    dh2 = _matmul(dhu, w["w_up"], "nt", F32, nm("ffn_up_dx"), add=dh2_g)
    gb["w_gate"] = _matmul(sv["h2"], dhg, "tn", F32, nm("ffn_gate_dw"))
    gb["w_up"] = _matmul(sv["h2"], dhu, "tn", F32, nm("ffn_up_dw"))
    dmid, gs["ffn_norm_pre"] = _rms_bwd(sv["x_mid"], (D_MODEL, 0), sm["ffn_norm_pre"], dh2, F32, nm("ffn_pre_norm_bwd"),
                                        add=dout)
    return dmid, gb, gs


def _mixer_bwd(dmid, sv, tables, w, sm, tag, make_ride):
    nm = lambda n: f"{n}_{tag}"
    gb, gs = {}, {}
    dmo, gs["mix_norm_post"] = _rms_bwd(sv["mo"], (D_MODEL, 0), sm["mix_norm_post"], dmid, BF16, nm("mix_post_norm_bwd"))
    dmerged = _matmul(dmo, w["w_mix_o"], "nt", F32, nm("mix_out_dx"))
    gb["w_mix_o"] = _matmul(sv["merged"], dmo, "tn", F32, nm("mix_out_dw"))
    dya, dyc, dyp, dg0, dg1, dg2 = _merge_bwd(sv["z"], sv["ys"], dmerged, nm("merge_bwd"))
    dpm = _matmul(dyp, w["w_pool_o"], "nt", F32, nm("pool_out_dx"))
    gb["w_pool_o"] = _matmul(sv["pm"], dyp, "tn", F32, nm("pool_out_dw"))
    du_pool, gs["pool_w"], gs["pool_scale"] = _pool_bwd(dpm, sv["z"], sm["pool_w"], sm["pool_scale"], nm("pool_bwd"))
    dhc = _matmul(dyc, w["w_conv_o"], "nt", F32, nm("conv_out_dx"))
    gb["w_conv_o"] = _matmul(sv["hc"], dyc, "tn", F32, nm("conv_out_dw"))
    dco, gs["conv_ln_g"], gs["conv_ln_b"], gs["conv_b"] = _conv_bwd_norm(dhc, sv["co"], sm["conv_ln_g"], sm["conv_ln_b"],
                                                                        nm("conv_bwd_norm"))
    du_conv, gs["conv_w"] = _conv_bwd_taps(dco, sv["z"], sm["conv_w"], nm("conv_bwd_taps"))
    do = _matmul(dya, w["w_attn_o"], "nt", F32, nm("attn_out_dx"))
    gb["w_attn_o"] = _matmul(sv["o"], dya, "tn", F32, nm("attn_out_dw"))
    delta, dob = _attn_delta(do, sv["o"], nm("attn_delta"))
    (dq, dk, dv), rode = _flash_bwd(sv["q"], sv["k"], sv["v"], dob, sv["lse"], delta, nm("flash_bwd"), make_ride(gb))
    dqf, dkf, dkr = _rope_qk_bwd(dq, dk, tables, nm("rope_qk_bwd"))
    dcq_n = _matmul(dqf, w["w_uq"], "nt", F32, nm("q_up_dx"))
    gb["w_uq"] = _matmul(sv["cq"], dqf, "tn", F32, nm("q_up_dw"))
    dckv_k = _matmul(dkf, w["w_uk"], "nt", F32, nm("k_up_dx"))
    dckv_n = _matmul(dv, w["w_uv"], "nt", F32, nm("v_up_dx"), add=dckv_k)
    gb["w_uk"] = _matmul(sv["ckv"], dkf, "tn", F32, nm("k_up_dw"))
    gb["w_uv"] = _matmul(sv["ckv"], dv, "tn", F32, nm("v_up_dw"))
    dcq, gs["q_norm"] = _rms_bwd(sv["z"], ZC_Q, sm["q_norm"], dcq_n, BF16, nm("q_norm_bwd"))
    dckv, gs["kv_norm"] = _rms_bwd(sv["z"], ZC_KV, sm["kv_norm"], dckv_n, BF16, nm("kv_norm_bwd"))
    dz = jnp.concatenate([dcq, dkr, du_pool, du_conv, dg0, dg1, dg2, dckv], axis=1)
    dh = _matmul(dz, w["w_in"], "nt", F32, nm("in_proj_dx"))
    gb["w_in"] = _matmul(sv["h"], dz, "tn", F32, nm("in_proj_dw"))
    dx, gs["mix_norm_pre"] = _rms_bwd(sv["x"], (D_MODEL, 0), sm["mix_norm_pre"], dh, F32, nm("mix_pre_norm_bwd"), add=dmid)
    return dx, gb, gs, rode


def _part_groups(part):
    return MIX_GROUPS if part == "mix" else FFN_GROUPS


class _Plan:
    def __init__(self, shards, conv_w):
        self.local = [_local_groups(shards, l) for l in range(DEPTH)]
        self.conv_w = conv_w
        self.gat, self.send, self.recv = {}, {}, {}

    @staticmethod
    def _riders(l):
        return [(l, "ffn")] + ([(l + 1, "mix")] if l + 1 < DEPTH else [])

    def gather_first(self):
        out = _all_gather([self.local[0][g] for g in MIX_GROUPS] + [self.conv_w], "gather_mixer_l0")
        self.gat[(0, "mix")] = dict(zip(MIX_GROUPS, out[:-1]))
        return out[-1]

    def fwd_ride(self, l):
        return _GatherRide([self.local[ll][g] for ll, part in self._riders(l) for g in _part_groups(part)])

    def fwd_done(self, l, outs):
        outs = list(outs)
        for ll, part in self._riders(l):
            self.gat[(ll, part)] = {g: outs.pop(0) for g in _part_groups(part)}

    def mixer_weights(self, l):
        return _mixer_weights(self.gat[(l, "mix")])

    def ffn_weights(self, l):
        return _ffn_weights(self.gat[(l, "ffn")])

    def add_grads(self, l, part, gb):
        self.send[(l, part)] = _mixer_grad_groups(gb) if part == "mix" else _ffn_grad_groups(gb)

    def bwd_ride(self, l):
        return _ReduceRide([self.send[(ll, part)][g] for ll, part in self._riders(l) for g in _part_groups(part)])

    def bwd_done(self, l, outs):
        outs = list(outs)
        for ll, part in self._riders(l):
            self.recv[(ll, part)] = {g: outs.pop(0) for g in _part_groups(part)}

    def finish(self):
        send = [self.send[(0, "mix")][g] for g in MIX_GROUPS]
        by_core = [a.reshape((4, 2) + a.shape[1:]).transpose((1, 0) + tuple(range(2, a.ndim + 1))) for a in send]
        core = lax.axis_index("c")
        own = [lax.dynamic_index_in_dim(a, core, axis=0, keepdims=False) for a in by_core]
        got = _swap_with_sibling(by_core, "reduce_d2d")
        pairs = [_add_pairs(a, b, f"reduce_pair_add_{g}") for g, a, b in zip(MIX_GROUPS, own, got)]
        self.recv[(0, "mix")] = dict(zip(MIX_GROUPS, _exchange_chips(pairs, "reduce_ici")))
        layers = []
        for l in range(DEPTH):
            tot = {g: _sum_blocks(a, f"reduce_sum_{g}_l{l}") for part in ("mix", "ffn")
                   for g, a in self.recv[(l, part)].items()}
            layers.append(_grads_from_groups(tot))
        return {n: jnp.stack([layers[l][n] for l in range(DEPTH)]) for n in BIG}


def _local_step(x, positions, target, smalls, plan):
    tables = _rope_tables(positions)
    saved = []
    h = x
    for l in range(DEPTH):
        wm = plan.mixer_weights(l)
        h, svm, rode = _mixer_fwd(h, tables, wm, smalls[l], f"l{l}", plan.fwd_ride(l))
        plan.fwd_done(l, rode)
        wf = plan.ffn_weights(l)
        h, svf = _ffn_fwd(h, wf, smalls[l], f"l{l}")
        saved.append((svm, svf, wm, wf))
    dy, sq = _loss_grad(h, target, "loss_grad")
    small = [None] * DEPTH
    for l in reversed(range(DEPTH)):
        svm, svf, wm, wf = saved[l]
        dmid, gbf, gsf = _ffn_bwd(dy, svf, wf, smalls[l], f"l{l}")
        plan.add_grads(l, "ffn", gbf)
        dy, gbm, gsm, rode = _mixer_bwd(dmid, svm, tables, wm, smalls[l], f"l{l}", lambda gb, l=l: plan.bwd_ride(l))
        plan.bwd_done(l, rode)
        plan.add_grads(l, "mix", gbm)
        small[l] = {**gsf, **gsm}
    return sq, dy, small


def kernel(x, positions, mix_norm_pre, w_in, q_norm, w_uq, kv_norm, w_uk, w_uv, w_attn_o, conv_w, conv_b, conv_ln_g, conv_ln_b, w_conv_o, pool_w, pool_scale, w_pool_o, w_mix_o, mix_norm_post, ffn_norm_pre, w_gate, w_up, w_down, ffn_norm_post, loss_target, m_mix_norm_pre, m_w_in, m_q_norm, m_w_uq, m_kv_norm, m_w_uk, m_w_uv, m_w_attn_o, m_conv_w, m_conv_b, m_conv_ln_g, m_conv_ln_b, m_w_conv_o, m_pool_w, m_pool_scale, m_w_pool_o, m_w_mix_o, m_mix_norm_post, m_ffn_norm_pre, m_w_gate, m_w_up, m_w_down, m_ffn_norm_post, v_mix_norm_pre, v_w_in, v_q_norm, v_w_uq, v_kv_norm, v_w_uk, v_w_uv, v_w_attn_o, v_conv_w, v_conv_b, v_conv_ln_g, v_conv_ln_b, v_w_conv_o, v_pool_w, v_pool_scale, v_w_pool_o, v_w_mix_o, v_mix_norm_post, v_ffn_norm_pre, v_w_gate, v_w_up, v_w_down, v_ffn_norm_post):
    given = dict(locals())
    dev = 4 * lax.axis_index("x") + 2 * lax.axis_index("y") + lax.axis_index("c")

    plan = _Plan({n: given[n] for n in BIG}, conv_w)
    cw = CONV_C // N_DEV
    conv_w_full = plan.gather_first().transpose(1, 2, 0, 3).reshape(DEPTH, CONV_W, CONV_C)
    smalls = []
    for l in range(DEPTH):
        sm = {n: given[n][l] for n in SMALL if n != "conv_w"}
        sm["conv_w"] = _pad_axis(conv_w_full[l], 0, CONV_HALO)
        smalls.append(sm)

    sq, grad_x, small = _local_step(x[0], positions[0], loss_target[0], smalls, plan)
    loss = lax.psum(0.5 / D_MODEL * jnp.sum(sq), ("x", "y", "c"))
    grads = plan.finish()

    small_groups = _all_gather(_small_groups(small), "gather_small_grads")
    small_sum = _small_from_groups([_sum_blocks(g, f"sum_small_grads_{i}") for i, g in enumerate(small_groups)])
    for n in SMALL:
        grads[n] = small_sum[n]
    grads["conv_w"] = lax.dynamic_slice_in_dim(small_sum["conv_w"], dev * cw, cw, axis=2)

    delta, new_m, new_v = {}, {}, {}
    for n in WEIGHTS:
        delta[n], new_m[n], new_v[n] = _adamw(given[n], grads[n], given["m_" + n], given["v_" + n], f"adamw_{n}")
    return (loss, grad_x[None], *[grads[n] for n in WEIGHTS], *[delta[n] for n in WEIGHTS],
            *[new_m[n] for n in WEIGHTS], *[new_v[n] for n in WEIGHTS])
```

```python
import functools
import math

import jax
import jax.numpy as jnp
from jax import lax
from jax.experimental import pallas as pl
from jax.experimental.pallas import tpu as pltpu

F32, BF16 = jnp.float32, jnp.bfloat16
MESH = pl.DeviceIdType.MESH

LANES = 128
SUBLANES = 8
VMEM_LIMIT_BYTES = 56 * 1024 * 1024

N_DEV = 8
D_MODEL = 1024
DEPTH = 2
N_HEADS = 8
QK_NOPE, QK_ROPE, V_HEAD = 64, 32, 64
HEAD_PAD = LANES
Q_RANK, KV_RANK = 384, 256
ROPE_THETA = 10000.0
CONV_C, CONV_W = 512, 31
CONV_HALO = 32
POOL_WINDOWS = (2, 4, 8, 16)
POOL_C, POOL_G = 512, 4
POOL_GD = POOL_C // POOL_G
D_FF = 2816
FF_SHARD = D_FF // N_DEV
FF_SHARD_PAD = 3 * LANES
D_FF_PAD = N_DEV * FF_SHARD_PAD
W_IN_SHARD = 660
EPS = 1e-6
ATTN_SCALE = 1.0 / math.sqrt(QK_NOPE + QK_ROPE)
LOG2E = 1.4426950408889634
LR, B1, B2, ADAM_EPS, WD, STEP = 0.001, 0.9, 0.999, 1e-08, 0.01, 10

Z_W = 5376
ZC_Q = (384, 0)
ZC_KR = (128, 3)
ZC_POOL = (512, 1)
ZC_CONV_A = (512, 2)
ZC_CONV_G = (512, 3)
ZC_GATE = (1024, 2)
ZC_KV = (256, 20)
W_IN_PIECES = ((0, 384, 0), (384, 640, 5120), (640, 672, 448), (672, 1696, 1024), (1696, 2208, 512), (2208, 5280, 2048))

BIG = ("w_in", "w_uq", "w_uk", "w_uv", "w_attn_o", "w_conv_o", "w_pool_o", "w_mix_o", "w_gate", "w_up", "w_down")
SMALL = ("mix_norm_pre", "q_norm", "kv_norm", "conv_w", "conv_b", "conv_ln_g", "conv_ln_b", "pool_w", "pool_scale",
         "mix_norm_post", "ffn_norm_pre", "ffn_norm_post")
WEIGHTS = ("mix_norm_pre", "w_in", "q_norm", "w_uq", "kv_norm", "w_uk", "w_uv", "w_attn_o", "conv_w", "conv_b",
           "conv_ln_g", "conv_ln_b", "w_conv_o", "pool_w", "pool_scale", "w_pool_o", "w_mix_o", "mix_norm_post",
           "ffn_norm_pre", "w_gate", "w_up", "w_down", "ffn_norm_post")


def _params(*semantics):
    return pltpu.CompilerParams(dimension_semantics=semantics, vmem_limit_bytes=VMEM_LIMIT_BYTES)


def _tile(dim, cap):
    if dim <= cap:
        return dim
    for t in range(cap - cap % LANES, 0, -LANES):
        if dim % t == 0:
            return t
    raise ValueError(f"no tile for {dim} under {cap}")


def _row_tile(rows, row_bytes, budget=1 << 20):
    if rows * row_bytes <= budget:
        return rows
    cap = max(16, budget // row_bytes)
    for t in range(cap - cap % 16, 0, -16):
        if rows % t == 0:
            return t
    return rows


def _rows(ts, width, cidx=0):
    return pl.BlockSpec((ts, width), lambda i: (i, cidx))


def _fixed(shape):
    return pl.BlockSpec(shape, lambda *_: (0,) * len(shape))


def _sigmoid(x):
    return 1.0 / (1.0 + jnp.exp(-x))


def _block_count(n_blocks, width, cap):
    return max(c for c in range(1, n_blocks + 1) if n_blocks % c == 0 and c * width <= max(cap, width))


def _matmul(a, b, mode, out_dtype, name, add=None, blocked=False):
    nb = n_blk = 0
    blocked = blocked or b.ndim == 3
    if mode == "nn":
        (m, k) = a.shape
        n = b.shape[0] * b.shape[2] if blocked else b.shape[1]
    elif mode == "nt":
        (m, k) = a.shape
        n = b.shape[1] if blocked else b.shape[0]
    else:
        (k, m), n = a.shape, b.shape[1]
    tm, tn, tk = _tile(m, 1024), _tile(n, 1408), _tile(k, 1408 if mode != "tn" else 1024)
    if blocked:
        nb = b.shape[2] if mode != "tn" else n // N_DEV
        n_blk = _block_count(N_DEV, nb, 1536 if mode == "nt" else 1408)
        if mode == "nt":
            tk = n_blk * nb
        else:
            tn = n_blk * nb
    nk = k // tk
    dims = {"nn": ((1,), (0,)), "nt": ((1,), (1,)), "tn": ((0,), (0,))}[mode]
    a_spec = {"nn": pl.BlockSpec((tm, tk), lambda i, j, s: (i, s)), "nt": pl.BlockSpec((tm, tk), lambda i, j, s: (i, s)),
              "tn": pl.BlockSpec((tk, tm), lambda i, j, s: (s, i))}[mode]
    b_spec = {"nn": pl.BlockSpec((tk, tn), lambda i, j, s: (s, j)), "nt": pl.BlockSpec((tn, tk), lambda i, j, s: (j, s)),
              "tn": pl.BlockSpec((tk, tn), lambda i, j, s: (s, j))}[mode]
    o_spec = pl.BlockSpec((tm, tn), lambda i, j, s: (i, j))
    out_shape = jax.ShapeDtypeStruct((m, n), out_dtype)
    if blocked and mode == "nn":
        b_spec = pl.BlockSpec((n_blk, tk, nb), lambda i, j, s: (j, s, 0))
    elif blocked and mode == "nt":
        b_spec = pl.BlockSpec((n_blk, tn, nb), lambda i, j, s: (s, j, 0))
    elif blocked:
        o_spec = pl.BlockSpec((n_blk, tm, nb), lambda i, j, s: (j, i, 0))
        out_shape = jax.ShapeDtypeStruct((N_DEV, m, nb), out_dtype)
    has_add = add is not None

    def body(a_ref, b_ref, *rest):
        add_ref = rest[0] if has_add else None
        o_ref = rest[1] if has_add else rest[0]
        if blocked and mode != "tn":
            bv = jnp.concatenate([b_ref[c] for c in range(n_blk)], axis=1) if n_blk > 1 else b_ref[0]
        else:
            bv = b_ref[...]
        part = lax.dot_general(a_ref[...], bv, (dims, ((), ())), preferred_element_type=F32)

        def finish(total):
            if has_add:
                total = total + add_ref[...]
            if blocked and mode == "tn":
                for c in range(n_blk):
                    o_ref[c] = total[:, c * nb:(c + 1) * nb].astype(o_ref.dtype)
            else:
                o_ref[...] = total.astype(o_ref.dtype)

        if nk == 1:
            finish(part)
        else:
            acc = rest[-1]
            step = pl.program_id(2)

            @pl.when(step == 0)
            def _():
                acc[...] = part

            @pl.when(step > 0)
            def _():
                acc[...] += part

            @pl.when(step == nk - 1)
            def _():
                finish(acc[...])

    operands = (a, b, add) if has_add else (a, b)
    return pl.pallas_call(
        body, name=name, out_shape=out_shape, grid=(m // tm, n // tn, nk),
        in_specs=[a_spec, b_spec] + ([o_spec] if has_add else []), out_specs=o_spec,
        scratch_shapes=[pltpu.VMEM((tm, tn), F32)] if nk > 1 else [],
        compiler_params=_params("parallel", "parallel", "arbitrary"))(*operands)


def _rms_fwd(x, win, gain, out_dtype, name, res=None):
    width, cidx = win
    s = x.shape[0]
    ts = min(s, 512)
    has_res = res is not None

    def body(x_ref, g_ref, *rest):
        o_ref = rest[-1]
        xv = x_ref[...]
        r = lax.rsqrt(jnp.mean(xv * xv, axis=-1, keepdims=True) + EPS)
        y = (xv * r) * g_ref[...]
        if has_res:
            y = rest[0][...] + y
        o_ref[...] = y.astype(o_ref.dtype)

    ops = (x, gain.reshape(1, width)) + ((res,) if has_res else ())
    return pl.pallas_call(
        body, name=name, out_shape=jax.ShapeDtypeStruct((s, width), out_dtype), grid=(s // ts,),
        in_specs=[_rows(ts, width, cidx), _fixed((1, width))] + ([_rows(ts, width)] if has_res else []),
        out_specs=_rows(ts, width), compiler_params=_params("parallel"))(*ops)


def _rms_bwd(x, win, gain, dy, out_dtype, name, add=None):
    width, cidx = win
    s = x.shape[0]
    ts = min(s, 512)
    has_add = add is not None

    def body(x_ref, g_ref, dy_ref, *rest):
        dx_ref, dg_ref = rest[-2], rest[-1]
        xv = x_ref[...]
        r = lax.rsqrt(jnp.mean(xv * xv, axis=-1, keepdims=True) + EPS)
        xh = xv * r
        dyv = dy_ref[...].astype(F32)
        dyg = dyv * g_ref[...]
        dx = r * (dyg - xh * jnp.mean(dyg * xh, axis=-1, keepdims=True))
        if has_add:
            dx = dx + rest[0][...]
        dx_ref[...] = dx.astype(dx_ref.dtype)

        @pl.when(pl.program_id(0) == 0)
        def _():
            dg_ref[...] = jnp.zeros_like(dg_ref)

        dg_ref[...] += jnp.sum(dyv * xh, axis=0, keepdims=True)

    ops = (x, gain.reshape(1, width), dy) + ((add,) if has_add else ())
    dx, dg = pl.pallas_call(
        body, name=name,
        out_shape=(jax.ShapeDtypeStruct((s, width), out_dtype), jax.ShapeDtypeStruct((1, width), F32)), grid=(s // ts,),
        in_specs=[_rows(ts, width, cidx), _fixed((1, width)), _rows(ts, width)] + ([_rows(ts, width)] if has_add else []),
        out_specs=(_rows(ts, width), _fixed((1, width))), compiler_params=_params("arbitrary"))(*ops)
    return dx, dg.reshape(width)


def _rope(x, c, s1, s2):
    return x * c + pltpu.roll(x, 16, 1) * s1 + pltpu.roll(x, LANES - 16, 1) * s2


def _rope_t(g, c, s1, s2):
    return g * c + pltpu.roll(g * s1, LANES - 16, 1) + pltpu.roll(g * s2, 16, 1)


def _rope_tables(positions):
    inv_freq = ROPE_THETA ** (-jnp.arange(0, QK_ROPE, 2, dtype=F32) / QK_ROPE)
    ang = positions.astype(F32)[:, None] * inv_freq
    cos, sin = jnp.cos(ang), jnp.sin(ang)
    n = positions.shape[0]
    one, zero = jnp.ones((n, 1), F32), jnp.zeros((n, 1), F32)
    c = jnp.concatenate([jnp.tile(one, (1, QK_NOPE)), cos, cos, jnp.tile(one, (1, 32))], axis=1)
    s1 = jnp.concatenate([jnp.tile(zero, (1, QK_NOPE + 16)), sin, jnp.tile(zero, (1, 32))], axis=1)
    s2 = jnp.concatenate([jnp.tile(zero, (1, QK_NOPE)), -sin, jnp.tile(zero, (1, 48))], axis=1)
    return c, s1, s2


def _rope_qk_fwd(qf, kf, z, tables, name):
    s = qf.shape[0]
    ts = min(s, 256)
    hw = N_HEADS * HEAD_PAD

    def body(qf_ref, kf_ref, kr_ref, c_ref, s1_ref, s2_ref, q_ref, k_ref):
        c, s1, s2 = c_ref[...], s1_ref[...], s2_ref[...]
        kr = _rope(kr_ref[...], c, s1, s2)
        for h in range(N_HEADS):
            sl = slice(h * HEAD_PAD, (h + 1) * HEAD_PAD)
            q_ref[:, sl] = _rope(qf_ref[:, sl], c, s1, s2).astype(BF16)
            k_ref[:, sl] = (kf_ref[:, sl] + kr).astype(BF16)

    tab = _rows(ts, LANES)
    return pl.pallas_call(
        body, name=name, out_shape=(jax.ShapeDtypeStruct((s, hw), BF16),) * 2, grid=(s // ts,),
        in_specs=[_rows(ts, hw), _rows(ts, hw), _rows(ts, *ZC_KR), tab, tab, tab],
        out_specs=(_rows(ts, hw), _rows(ts, hw)), compiler_params=_params("parallel"))(qf, kf, z, *tables)


def _rope_qk_bwd(dq, dk, tables, name):
    s = dq.shape[0]
    ts = min(s, 256)
    hw = N_HEADS * HEAD_PAD

    def body(dq_ref, dk_ref, c_ref, s1_ref, s2_ref, dqf_ref, dkf_ref, dkr_ref):
        c, s1, s2 = c_ref[...], s1_ref[...], s2_ref[...]
        ksum = jnp.zeros((ts, HEAD_PAD), F32)
        for h in range(N_HEADS):
            sl = slice(h * HEAD_PAD, (h + 1) * HEAD_PAD)
            dqf_ref[:, sl] = _rope_t(dq_ref[:, sl], c, s1, s2).astype(BF16)
            dkh = dk_ref[:, sl]
            dkf_ref[:, sl] = dkh.astype(BF16)
            ksum = ksum + dkh
        lane = lax.broadcasted_iota(jnp.int32, (ts, HEAD_PAD), 1)
        in_rope = (lane >= QK_NOPE) & (lane < QK_NOPE + QK_ROPE)
        dkr_ref[...] = jnp.where(in_rope, _rope_t(ksum, c, s1, s2), 0.0).astype(BF16)

    tab = _rows(ts, LANES)
    return pl.pallas_call(
        body, name=name,
        out_shape=(jax.ShapeDtypeStruct((s, hw), BF16), jax.ShapeDtypeStruct((s, hw), BF16),
                   jax.ShapeDtypeStruct((s, LANES), BF16)), grid=(s // ts,),
        in_specs=[_rows(ts, hw), _rows(ts, hw), tab, tab, tab],
        out_specs=(_rows(ts, hw), _rows(ts, hw), _rows(ts, LANES)), compiler_params=_params("parallel"))(dq, dk, *tables)


def _attn_tile(s):
    return min(s, 512)


def _raw_scores(q, k, masked, row0=0):
    sc = lax.dot_general(q, k, (((1,), (1,)), ((), ())), preferred_element_type=F32)
    if masked:
        rows = row0 + lax.broadcasted_iota(jnp.int32, sc.shape, 0)
        cols = lax.broadcasted_iota(jnp.int32, sc.shape, 1)
        sc = jnp.where(cols <= rows, sc, -jnp.inf)
    return sc


def _ride_hooks(ride, refs, n_in, n_out, grid):
    if ride is None:
        return refs, lambda: None, lambda: None
    n = len(ride.arrays)
    own = refs[:n_in] + refs[n_in + n:n_in + n + n_out]
    ins, outs, sems = refs[n_in:n_in + n], refs[n_in + n + n_out:n_in + 2 * n + n_out], refs[n_in + 2 * n + n_out:]
    at_first = functools.reduce(lambda a, b: a & b, [pl.program_id(ax) == 0 for ax in range(len(grid))])
    at_last = functools.reduce(lambda a, b: a & b, [pl.program_id(ax) == g - 1 for ax, g in enumerate(grid)])
    return own, lambda: pl.when(at_first)(lambda: ride.start(ins, outs, sems)), \
        lambda: pl.when(at_last)(lambda: ride.finish(ins, outs, sems))


def _ride_call(ride, body, name, out_shape, grid, in_specs, out_specs, semantics, operands):
    n = 0 if ride is None else len(ride.arrays)
    res = pl.pallas_call(
        body, name=name, out_shape=tuple(out_shape) + (tuple(ride.out_shape) if n else ()), grid=grid,
        in_specs=list(in_specs) + [ANY] * n, out_specs=tuple(out_specs) + (ANY,) * n,
        scratch_shapes=list(ride.scratch) if n else [],
        compiler_params=_params(*(("arbitrary",) * len(grid) if n else semantics)))(*operands, *(ride.arrays if n else ()))
    return res[:len(out_shape)], list(res[len(out_shape):])


def _flash_fwd(q, k, v, name, ride=None):
    s = q.shape[0]
    t = _attn_tile(s)
    c2 = ATTN_SCALE * LOG2E
    grid = (N_HEADS, s // t)

    def body(*refs):
        (q_ref, k_ref, v_ref, o_ref, lse_ref), start, finish = _ride_hooks(ride, refs, 3, 2, grid)
        start()
        i = pl.program_id(1)
        qv = q_ref[...]

        def chunk(j, carry, masked):
            m_old, l_old, acc = carry
            at = pl.ds(pl.multiple_of(j * t, t), t)
            sc = _raw_scores(qv, k_ref[at, :], masked)
            m_new = jnp.maximum(m_old, jnp.max(sc, axis=-1, keepdims=True))
            p = jnp.exp2((sc - m_new) * c2)
            alpha = jnp.exp2((m_old - m_new) * c2)
            l_new = alpha * l_old + jnp.sum(p, axis=-1, keepdims=True)
            acc = alpha * acc + jnp.dot(p.astype(BF16), v_ref[at, :], preferred_element_type=F32)
            return m_new, l_new, acc

        init = (jnp.full((t, 1), -jnp.inf, F32), jnp.zeros((t, 1), F32), jnp.zeros((t, HEAD_PAD), F32))
        carry = lax.fori_loop(0, i, lambda j, cr: chunk(j, cr, False), init)
        m_fin, l_fin, acc = chunk(i, carry, True)
        o_ref[...] = (acc / l_fin).astype(o_ref.dtype)
        lse_ref[0] = m_fin * ATTN_SCALE + jnp.log(l_fin)
        finish()

    qo = pl.BlockSpec((t, HEAD_PAD), lambda h, i: (i, h))
    whole = pl.BlockSpec((s, HEAD_PAD), lambda h, i: (0, h))
    return _ride_call(
        ride, body, name, (jax.ShapeDtypeStruct(q.shape, BF16), jax.ShapeDtypeStruct((N_HEADS, s, 1), F32)), grid,
        [qo, whole, whole], (qo, pl.BlockSpec((1, t, 1), lambda h, i: (h, i, 0))), ("parallel", "parallel"), (q, k, v))


def _attn_delta(do, o, name):
    s = o.shape[0]
    t = _attn_tile(s)

    def body(do_ref, o_ref, delta_ref, dob_ref):
        dov = do_ref[...]
        delta_ref[0] = jnp.sum(dov * o_ref[...].astype(F32), axis=-1, keepdims=True)
        dob_ref[...] = dov.astype(BF16)

    blk = pl.BlockSpec((t, HEAD_PAD), lambda i, h: (i, h))
    return pl.pallas_call(
        body, name=name,
        out_shape=(jax.ShapeDtypeStruct((N_HEADS, s, 1), F32), jax.ShapeDtypeStruct(o.shape, BF16)),
        grid=(s // t, N_HEADS), in_specs=[blk, blk],
        out_specs=(pl.BlockSpec((1, t, 1), lambda i, h: (h, i, 0)), blk),
        compiler_params=_params("parallel", "parallel"))(do, o)


def _flash_bwd(q, k, v, do, lse, delta, name, ride=None):
    s = q.shape[0]
    t = _attn_tile(s)
    nt = s // t
    c2 = ATTN_SCALE * LOG2E
    grid = (N_HEADS, nt)

    def body(*refs):
        (q_ref, k_ref, v_ref, do_ref, lse_ref, delta_ref, dq_ref, dk_ref, dv_ref), start, finish = _ride_hooks(
            ride, refs, 6, 3, grid)
        start()
        j = pl.program_id(1)
        kv, vv = k_ref[...], v_ref[...]

        @pl.when(j == 0)
        def _():
            dq_ref[...] = jnp.zeros_like(dq_ref)

        def chunk(i, carry, masked):
            dk_acc, dv_acc = carry
            at = pl.ds(pl.multiple_of(i * t, t), t)
            qi, doi = q_ref[at, :], do_ref[at, :]
            sc = _raw_scores(qi, kv, masked)
            p = jnp.exp2(sc * c2 - lse_ref[0, at, :] * LOG2E)
            dp = lax.dot_general(doi, vv, (((1,), (1,)), ((), ())), preferred_element_type=F32)
            ds = (p * (dp - delta_ref[0, at, :])).astype(BF16)
            dv_acc = dv_acc + lax.dot_general(p.astype(BF16), doi, (((0,), (0,)), ((), ())), preferred_element_type=F32)
            dk_acc = dk_acc + lax.dot_general(ds, qi, (((0,), (0,)), ((), ())), preferred_element_type=F32)
            dq_ref[at, :] += jnp.dot(ds, kv, preferred_element_type=F32) * ATTN_SCALE
            return dk_acc, dv_acc

        zero = jnp.zeros((t, HEAD_PAD), F32)
        carry = chunk(j, (zero, zero), True)
        dk_acc, dv_acc = lax.fori_loop(j + 1, nt, lambda i, cr: chunk(i, cr, False), carry)
        dk_ref[...] = dk_acc * ATTN_SCALE
        dv_ref[...] = dv_acc.astype(BF16)
        finish()

    blk = pl.BlockSpec((t, HEAD_PAD), lambda h, j: (j, h))
    whole = pl.BlockSpec((s, HEAD_PAD), lambda h, j: (0, h))
    stat = pl.BlockSpec((1, s, 1), lambda h, j: (h, 0, 0))
    return _ride_call(
        ride, body, name, (jax.ShapeDtypeStruct(q.shape, F32), jax.ShapeDtypeStruct(q.shape, F32),
                           jax.ShapeDtypeStruct(q.shape, BF16)), grid,
        [whole, blk, blk, whole, stat, stat], (whole, blk, blk), ("parallel", "arbitrary"), (q, k, v, do, lse, delta))


def _conv_tile(s):
    return min(s, 256)


def _halo_before(t, width, cidx):
    per = t // CONV_HALO
    return pl.BlockSpec((CONV_HALO, width), lambda i: (jnp.maximum(i * per - 1, 0), cidx))


def _halo_after(t, width, cidx, n_tiles):
    per = t // CONV_HALO
    last = n_tiles * per - 1
    return pl.BlockSpec((CONV_HALO, width), lambda i: (jnp.minimum((i + 1) * per, last), cidx))


def _fill_glu(hbuf, ap_ref, gp_ref, a_ref, g_ref, t):
    first = pl.program_id(0) == 0
    hbuf[pl.ds(0, CONV_HALO), :] = jnp.where(first, 0.0, ap_ref[...] * _sigmoid(gp_ref[...]))
    hbuf[pl.ds(CONV_HALO, t), :] = a_ref[...] * _sigmoid(g_ref[...])


def _phase_copies(dst, src, t):
    n = t + CONV_HALO - SUBLANES
    for s in range(1, SUBLANES):
        dst[s, pl.ds(0, n), :] = src[pl.ds(s, n), :]


def _window(phases, src, k, t):
    if k % SUBLANES == 0:
        return src[pl.ds(k, t), :]
    return phases[k % SUBLANES, pl.ds(k - k % SUBLANES, t), :]


def _layer_norm_parts(co):
    mu = jnp.mean(co, axis=-1, keepdims=True)
    xc = co - mu
    rstd = lax.rsqrt(jnp.mean(xc * xc, axis=-1, keepdims=True) + EPS)
    return xc * rstd, rstd


def _conv_fwd(z, conv_w, conv_b, ln_g, ln_b, name):
    s = z.shape[0]
    t = _conv_tile(s)
    off = CONV_HALO - (CONV_W - 1)

    def body(ap_ref, gp_ref, a_ref, g_ref, w_ref, b_ref, lg_ref, lb_ref, hc_ref, co_ref, hbuf, hph):
        _fill_glu(hbuf, ap_ref, gp_ref, a_ref, g_ref, t)
        _phase_copies(hph, hbuf, t)
        acc = jnp.zeros((t, CONV_C), F32) + b_ref[...]
        for j in range(CONV_W):
            acc = acc + _window(hph, hbuf, off + j, t) * w_ref[pl.ds(j, 1), :]
        co_ref[...] = acc
        xh, _ = _layer_norm_parts(acc)
        y = xh * lg_ref[...] + lb_ref[...]
        hc_ref[...] = (y * _sigmoid(y)).astype(BF16)

    vec = _fixed((1, CONV_C))
    return pl.pallas_call(
        body, name=name, out_shape=(jax.ShapeDtypeStruct((s, CONV_C), BF16), jax.ShapeDtypeStruct((s, CONV_C), F32)),
        grid=(s // t,),
        in_specs=[_halo_before(t, *ZC_CONV_A), _halo_before(t, *ZC_CONV_G), _rows(t, *ZC_CONV_A), _rows(t, *ZC_CONV_G),
                  _fixed((CONV_HALO, CONV_C)), vec, vec, vec],
        out_specs=(_rows(t, CONV_C), _rows(t, CONV_C)),
        scratch_shapes=[pltpu.VMEM((t + CONV_HALO, CONV_C), F32), pltpu.VMEM((SUBLANES, t + CONV_HALO, CONV_C), F32)],
        compiler_params=_params("parallel"))(z, z, z, z, conv_w, conv_b.reshape(1, -1), ln_g.reshape(1, -1),
                                             ln_b.reshape(1, -1))


def _conv_bwd_norm(dhc, co, ln_g, ln_b, name):
    s = co.shape[0]
    t = min(s, 512)

    def body(dhc_ref, co_ref, lg_ref, lb_ref, dco_ref, dg_ref, db_ref, dcb_ref):
        xh, rstd = _layer_norm_parts(co_ref[...])
        y = xh * lg_ref[...] + lb_ref[...]
        sg = _sigmoid(y)
        dy = dhc_ref[...] * (sg * (1.0 + y * (1.0 - sg)))
        dxh = dy * lg_ref[...]
        dco = rstd * (dxh - jnp.mean(dxh, axis=-1, keepdims=True) - xh * jnp.mean(dxh * xh, axis=-1, keepdims=True))
        dco_ref[...] = dco

        @pl.when(pl.program_id(0) == 0)
        def _():
            dg_ref[...] = jnp.zeros_like(dg_ref)
            db_ref[...] = jnp.zeros_like(db_ref)
            dcb_ref[...] = jnp.zeros_like(dcb_ref)

        dg_ref[...] += jnp.sum(dy * xh, axis=0, keepdims=True)
        db_ref[...] += jnp.sum(dy, axis=0, keepdims=True)
        dcb_ref[...] += jnp.sum(dco, axis=0, keepdims=True)

    vec = _fixed((1, CONV_C))
    one = jax.ShapeDtypeStruct((1, CONV_C), F32)
    dco, dg, db, dcb = pl.pallas_call(
        body, name=name, out_shape=(jax.ShapeDtypeStruct((s, CONV_C), F32), one, one, one), grid=(s // t,),
        in_specs=[_rows(t, CONV_C), _rows(t, CONV_C), vec, vec], out_specs=(_rows(t, CONV_C), vec, vec, vec),
        compiler_params=_params("arbitrary"))(dhc, co, ln_g.reshape(1, -1), ln_b.reshape(1, -1))
    return dco, dg.reshape(-1), db.reshape(-1), dcb.reshape(-1)


def _conv_bwd_taps(dco, z, conv_w, name):
    s = z.shape[0]
    t = _conv_tile(s)
    nt = s // t
    off = CONV_HALO - (CONV_W - 1)

    def body(ap_ref, gp_ref, a_ref, g_ref, d_ref, dn_ref, w_ref, du_ref, dw_ref, hbuf, dbuf, hph, dph):
        i = pl.program_id(0)
        _fill_glu(hbuf, ap_ref, gp_ref, a_ref, g_ref, t)
        dbuf[pl.ds(0, t), :] = d_ref[...]
        dbuf[pl.ds(t, CONV_HALO), :] = jnp.where(i == nt - 1, 0.0, dn_ref[...])
        _phase_copies(hph, hbuf, t)
        _phase_copies(dph, dbuf, t)

        @pl.when(i == 0)
        def _():
            dw_ref[...] = jnp.zeros_like(dw_ref)

        dcur = d_ref[...]
        dh = jnp.zeros((t, CONV_C), F32)
        for j in range(CONV_W):
            dh = dh + _window(dph, dbuf, CONV_W - 1 - j, t) * w_ref[pl.ds(j, 1), :]
            dw_ref[pl.ds(j, 1), :] += jnp.sum(dcur * _window(hph, hbuf, off + j, t), axis=0, keepdims=True)
        a, sg = a_ref[...], _sigmoid(g_ref[...])
        du_ref[:, pl.ds(0, CONV_C)] = (dh * sg).astype(BF16)
        du_ref[:, pl.ds(CONV_C, CONV_C)] = (dh * a * sg * (1.0 - sg)).astype(BF16)

    return pl.pallas_call(
        body, name=name,
        out_shape=(jax.ShapeDtypeStruct((s, 2 * CONV_C), BF16), jax.ShapeDtypeStruct((CONV_HALO, CONV_C), F32)),
        grid=(nt,),
        in_specs=[_halo_before(t, *ZC_CONV_A), _halo_before(t, *ZC_CONV_G), _rows(t, *ZC_CONV_A), _rows(t, *ZC_CONV_G),
                  _rows(t, CONV_C), _halo_after(t, CONV_C, 0, nt), _fixed((CONV_HALO, CONV_C))],
        out_specs=(_rows(t, 2 * CONV_C), _fixed((CONV_HALO, CONV_C))),
        scratch_shapes=[pltpu.VMEM((t + CONV_HALO, CONV_C), F32), pltpu.VMEM((t + CONV_HALO, CONV_C), F32),
                        pltpu.VMEM((SUBLANES, t + CONV_HALO, CONV_C), F32),
                        pltpu.VMEM((SUBLANES, t + CONV_HALO, CONV_C), F32)],
        compiler_params=_params("arbitrary"))(z, z, z, z, dco, dco, conv_w)


def _pool_tile(s):
    return min(s, 512)


def _pool_counts(row0, n, window):
    rows = row0 + lax.broadcasted_iota(jnp.int32, (n, POOL_GD), 0)
    return jnp.minimum(rows + 1, window).astype(F32)


def _pool_diff(ubuf, gi, window, row0, t):
    lanes = pl.ds(gi * POOL_GD, POOL_GD)
    tot = ubuf[pl.ds(CONV_HALO, t), lanes]
    cur = tot
    for back in range(1, window):
        tot = tot + ubuf[pl.ds(CONV_HALO - back, t), lanes]
    return tot / _pool_counts(row0, t, window) - cur


def _pool_fwd(z, pool_w, pool_scale, name):
    s = z.shape[0]
    t = _pool_tile(s)

    def body(up_ref, u_ref, w_ref, sc_ref, m_ref, ubuf):
        i = pl.program_id(0)
        ubuf[pl.ds(0, CONV_HALO), :] = jnp.where(i == 0, 0.0, up_ref[...])
        ubuf[pl.ds(CONV_HALO, t), :] = u_ref[...]
        for gi, window in enumerate(POOL_WINDOWS):
            d = _pool_diff(ubuf, gi, window, i * t, t)
            mm = jnp.dot(d.astype(BF16), w_ref[gi].astype(BF16), preferred_element_type=F32)
            lanes = pl.ds(gi * POOL_GD, POOL_GD)
            m_ref[:, lanes] = (mm * sc_ref[:, lanes]).astype(BF16)

    return pl.pallas_call(
        body, name=name, out_shape=jax.ShapeDtypeStruct((s, POOL_C), BF16), grid=(s // t,),
        in_specs=[_halo_before(t, *ZC_POOL), _rows(t, *ZC_POOL), _fixed((POOL_G, POOL_GD, POOL_GD)), _fixed((1, POOL_C))],
        out_specs=_rows(t, POOL_C), scratch_shapes=[pltpu.VMEM((t + CONV_HALO, POOL_C), F32)],
        compiler_params=_params("parallel"))(z, z, pool_w, pool_scale.reshape(1, -1))


def _pool_bwd(dm, z, pool_w, pool_scale, name):
    s = z.shape[0]
    t = _pool_tile(s)
    nt = s // t

    def body(up_ref, u_ref, dm_ref, dmn_ref, w_ref, sc_ref, du_ref, dw_ref, dsc_ref, ubuf, ebuf):
        i = pl.program_id(0)
        ubuf[pl.ds(0, CONV_HALO), :] = jnp.where(i == 0, 0.0, up_ref[...])
        ubuf[pl.ds(CONV_HALO, t), :] = u_ref[...]

        @pl.when(i == 0)
        def _():
            dw_ref[...] = jnp.zeros_like(dw_ref)
            dsc_ref[...] = jnp.zeros_like(dsc_ref)

        dm_next = jnp.where(i == nt - 1, 0.0, dmn_ref[...])
        for gi, window in enumerate(POOL_WINDOWS):
            lanes = pl.ds(gi * POOL_GD, POOL_GD)
            wb = w_ref[gi].astype(BF16)
            scale = sc_ref[:, lanes]
            d = _pool_diff(ubuf, gi, window, i * t, t).astype(BF16)
            mm = jnp.dot(d, wb, preferred_element_type=F32)
            dmv = dm_ref[:, lanes]
            dsc_ref[:, lanes] += jnp.sum(dmv * mm, axis=0, keepdims=True)
            dmm = (dmv * scale).astype(BF16)
            dw_ref[gi] += lax.dot_general(d, dmm, (((0,), (0,)), ((), ())), preferred_element_type=F32)
            dd = lax.dot_general(dmm, wb, (((1,), (1,)), ((), ())), preferred_element_type=F32)
            dd_next = lax.dot_general((dm_next[:, gi * POOL_GD:(gi + 1) * POOL_GD] * scale).astype(BF16), wb,
                                      (((1,), (1,)), ((), ())), preferred_element_type=F32)
            ebuf[pl.ds(0, t), lanes] = dd / _pool_counts(i * t, t, window)
            ebuf[pl.ds(t, CONV_HALO), lanes] = dd_next / _pool_counts((i + 1) * t, CONV_HALO, window)
            du = -dd
            for ahead in range(window):
                du = du + ebuf[pl.ds(ahead, t), lanes]
            du_ref[:, lanes] = du.astype(BF16)

    du, dw, dsc = pl.pallas_call(
        body, name=name,
        out_shape=(jax.ShapeDtypeStruct((s, POOL_C), BF16), jax.ShapeDtypeStruct((POOL_G, POOL_GD, POOL_GD), F32),
                   jax.ShapeDtypeStruct((1, POOL_C), F32)), grid=(nt,),
        in_specs=[_halo_before(t, *ZC_POOL), _rows(t, *ZC_POOL), _rows(t, POOL_C), _halo_after(t, POOL_C, 0, nt),
                  _fixed((POOL_G, POOL_GD, POOL_GD)), _fixed((1, POOL_C))],
        out_specs=(_rows(t, POOL_C), _fixed((POOL_G, POOL_GD, POOL_GD)), _fixed((1, POOL_C))),
        scratch_shapes=[pltpu.VMEM((t + CONV_HALO, POOL_C), F32), pltpu.VMEM((t + CONV_HALO, POOL_C), F32)],
        compiler_params=_params("arbitrary"))(z, z, dm, dm, pool_w, pool_scale.reshape(1, -1))
    return du, dw, dsc.reshape(-1)


def _gate_specs(ts):
    width, first = ZC_GATE
    return [_rows(ts, width, first + b) for b in range(3)]


def _merge_fwd(z, ys, name):
    s = z.shape[0]
    ts = min(s, 256)

    def body(g0, g1, g2, y0, y1, y2, o_ref):
        o_ref[...] = (_sigmoid(g0[...]) * y0[...] + _sigmoid(g1[...]) * y1[...]
                      + _sigmoid(g2[...]) * y2[...]).astype(BF16)

    return pl.pallas_call(
        body, name=name, out_shape=jax.ShapeDtypeStruct((s, D_MODEL), BF16), grid=(s // ts,),
        in_specs=_gate_specs(ts) + [_rows(ts, D_MODEL)] * 3, out_specs=_rows(ts, D_MODEL),
        compiler_params=_params("parallel"))(z, z, z, *ys)


def _merge_bwd(z, ys, dmerged, name):
    s = z.shape[0]
    ts = min(s, 256)

    def body(g0, g1, g2, y0, y1, y2, dm_ref, dy0, dy1, dy2, dg0, dg1, dg2):
        dmv = dm_ref[...]
        for g_ref, y_ref, dy_ref, dg_ref in ((g0, y0, dy0, dg0), (g1, y1, dy1, dg1), (g2, y2, dy2, dg2)):
            sg = _sigmoid(g_ref[...])
            dy_ref[...] = (dmv * sg).astype(BF16)
            dg_ref[...] = (dmv * y_ref[...] * sg * (1.0 - sg)).astype(BF16)

    out = jax.ShapeDtypeStruct((s, D_MODEL), BF16)
    return pl.pallas_call(
        body, name=name, out_shape=(out,) * 6, grid=(s // ts,),
        in_specs=_gate_specs(ts) + [_rows(ts, D_MODEL)] * 4, out_specs=(_rows(ts, D_MODEL),) * 6,
        compiler_params=_params("parallel"))(z, z, z, *ys, dmerged)


def _swiglu_fwd(hg, hu, name):
    s, f = hg.shape
    ts, tc = min(s, 512), _tile(f, 1024)
    blk = pl.BlockSpec((ts, tc), lambda i, j: (i, j))

    def body(g_ref, u_ref, o_ref):
        g = g_ref[...]
        o_ref[...] = (g * _sigmoid(g) * u_ref[...]).astype(BF16)

    return pl.pallas_call(
        body, name=name, out_shape=jax.ShapeDtypeStruct((s, f), BF16), grid=(s // ts, f // tc),
        in_specs=[blk, blk], out_specs=blk, compiler_params=_params("parallel", "parallel"))(hg, hu)


def _swiglu_bwd(hg, hu, dact, name):
    s, f = hg.shape
    ts, tc = min(s, 512), _tile(f, 1024)
    blk = pl.BlockSpec((ts, tc), lambda i, j: (i, j))

    def body(g_ref, u_ref, d_ref, dg_ref, du_ref):
        g, d = g_ref[...], d_ref[...]
        sg = _sigmoid(g)
        dg_ref[...] = (d * u_ref[...] * (sg * (1.0 + g * (1.0 - sg)))).astype(BF16)
        du_ref[...] = (d * g * sg).astype(BF16)

    out = jax.ShapeDtypeStruct((s, f), BF16)
    return pl.pallas_call(
        body, name=name, out_shape=(out, out), grid=(s // ts, f // tc), in_specs=[blk, blk, blk], out_specs=(blk, blk),
        compiler_params=_params("parallel", "parallel"))(hg, hu, dact)


def _loss_grad(y, target, name):
    s, d = y.shape
    ts = min(s, 512)

    def body(y_ref, t_ref, dy_ref, sq_ref):
        e = y_ref[...] - t_ref[...]
        dy_ref[...] = e / d

        @pl.when(pl.program_id(0) == 0)
        def _():
            sq_ref[...] = jnp.zeros_like(sq_ref)

        sq_ref[...] += jnp.sum(e * e, axis=0, keepdims=True)

    return pl.pallas_call(
        body, name=name, out_shape=(jax.ShapeDtypeStruct((s, d), F32), jax.ShapeDtypeStruct((1, d), F32)),
        grid=(s // ts,), in_specs=[_rows(ts, d), _rows(ts, d)], out_specs=(_rows(ts, d), _fixed((1, d))),
        compiler_params=_params("arbitrary"))(y, target)


def _adamw(w, g, m, v, name):
    shape = w.shape
    cols = shape[-1]
    rows = math.prod(shape[:-1])
    tr = _row_tile(rows, cols * 4)

    def body(w_ref, g_ref, m_ref, v_ref, d_ref, mo_ref, vo_ref):
        gv = g_ref[...]
        mn = B1 * m_ref[...] + (1.0 - B1) * gv
        vn = B2 * v_ref[...] + (1.0 - B2) * (gv * gv)
        m_hat = mn / (1.0 - B1 ** STEP)
        v_hat = vn / (1.0 - B2 ** STEP)
        d_ref[...] = -LR * (m_hat / (jnp.sqrt(v_hat) + ADAM_EPS) + WD * w_ref[...])
        mo_ref[...] = mn
        vo_ref[...] = vn

    spec = _rows(tr, cols)
    out = jax.ShapeDtypeStruct((rows, cols), F32)
    res = pl.pallas_call(
        body, name=name, out_shape=(out,) * 3, grid=(rows // tr,), in_specs=[spec] * 4, out_specs=(spec,) * 3,
        compiler_params=_params("parallel"))(*[t.reshape(rows, cols) for t in (w, g, m, v)])
    return tuple(r.reshape(shape) for r in res)


ANY = pl.BlockSpec(memory_space=pl.ANY)


class _GatherRide:
    def __init__(self, arrays):
        n = len(arrays)
        self.arrays = list(arrays)
        self.out_shape = [jax.ShapeDtypeStruct((N_DEV,) + a.shape, a.dtype) for a in arrays]
        self.scratch = [pltpu.SemaphoreType.DMA((n, 7)), pltpu.SemaphoreType.DMA((n, 7)), pltpu.SemaphoreType.DMA((n,))]

    def _copies(self, ins, outs, sems):
        send_sems, recv_sems, local_sems = sems
        n = len(self.arrays)
        x, y, c = lax.axis_index("x"), lax.axis_index("y"), lax.axis_index("c")
        me, sibling = (x, y, c), (x, y, 1 - c)
        chips = [(1 - x, y), (x, 1 - y), (1 - x, 1 - y)]

        def slot(a, px, py, pc):
            return outs[a].at[4 * px + 2 * py + pc]

        def copy(a, k, block, to, src=None):
            return pltpu.make_async_remote_copy(
                src_ref=slot(a, *block) if src is None else src, dst_ref=slot(a, *block), send_sem=send_sems.at[a, k],
                recv_sem=recv_sems.at[a, k], device_id=to, device_id_type=MESH)

        mine = [pltpu.make_async_copy(ins[a], slot(a, *me), local_sems.at[a]) for a in range(n)]
        first = []
        for a in range(n):
            first.append(copy(a, 0, me, sibling, src=ins[a]))
            first += [copy(a, 1 + j, me, (*chip, c), src=ins[a]) for j, chip in enumerate(chips)]
        return n, me, sibling, chips, c, copy, mine, first

    def start(self, ins, outs, sems):
        _, _, _, _, _, _, mine, first = self._copies(ins, outs, sems)
        for cp in mine + first:
            cp.start()

    def finish(self, ins, outs, sems):
        n, me, sibling, chips, c, copy, mine, first = self._copies(ins, outs, sems)
        passed = []
        for j, chip in enumerate(chips):
            for a in range(n):
                copy(a, 1 + j, (*chip, c), me).wait_recv()
                passed.append(copy(a, 4 + j, (*chip, c), sibling))
                passed[-1].start()
        for a in range(n):
            copy(a, 0, sibling, me).wait_recv()
            for j, chip in enumerate(chips):
                copy(a, 4 + j, (*chip, 1 - c), me).wait_recv()
        for cp in first + passed:
            cp.wait_send()
        for cp in mine:
            cp.wait()


class _ReduceRide:
    def __init__(self, arrays):
        n = len(arrays)
        self.arrays = list(arrays)
        self.out_shape = [jax.ShapeDtypeStruct(a.shape, a.dtype) for a in arrays]
        self.scratch = [pltpu.SemaphoreType.DMA((n, 7)), pltpu.SemaphoreType.DMA((n, 7)), pltpu.SemaphoreType.DMA((n,))]

    def _copies(self, ins, outs, sems):
        send_sems, recv_sems, local_sems = sems
        n = len(self.arrays)
        x, y, c = lax.axis_index("x"), lax.axis_index("y"), lax.axis_index("c")
        mine = [pltpu.make_async_copy(ins[a].at[4 * x + 2 * y + c], outs[a].at[0], local_sems.at[a]) for a in range(n)]
        copies = []
        for a in range(n):
            for k in range(1, N_DEV):
                px = 1 - x if k & 4 else x
                py = 1 - y if k & 2 else y
                pc = 1 - c if k & 1 else c
                copies.append(pltpu.make_async_remote_copy(
                    src_ref=ins[a].at[4 * px + 2 * py + pc], dst_ref=outs[a].at[k], send_sem=send_sems.at[a, k - 1],
                    recv_sem=recv_sems.at[a, k - 1], device_id=(px, py, pc), device_id_type=MESH))
        return mine, copies

    def start(self, ins, outs, sems):
        mine, copies = self._copies(ins, outs, sems)
        for cp in mine + copies:
            cp.start()

    def finish(self, ins, outs, sems):
        mine, copies = self._copies(ins, outs, sems)
        for cp in copies + mine:
            cp.wait()


def _run_ride(ride, name):
    n = len(ride.arrays)

    def body(*refs):
        ins, outs, sems = refs[:n], refs[n:2 * n], refs[2 * n:]
        ride.start(ins, outs, sems)
        ride.finish(ins, outs, sems)

    return pl.pallas_call(body, name=name, out_shape=ride.out_shape, in_specs=[ANY] * n, out_specs=[ANY] * n,
                          scratch_shapes=ride.scratch)(*ride.arrays)


def _all_gather(arrays, name):
    return _run_ride(_GatherRide(arrays), name)


def _swap_with_sibling(arrays, name):
    n = len(arrays)

    def body(*refs):
        ins, outs = refs[:n], refs[n:2 * n]
        send_sems, recv_sems = refs[2 * n:]
        x, y, c = lax.axis_index("x"), lax.axis_index("y"), lax.axis_index("c")
        copies = [pltpu.make_async_remote_copy(
            src_ref=ins[a].at[1 - c], dst_ref=outs[a], send_sem=send_sems.at[a], recv_sem=recv_sems.at[a],
            device_id=(x, y, 1 - c), device_id_type=MESH) for a in range(n)]
        for cp in copies:
            cp.start()
        for cp in copies:
            cp.wait()

    return pl.pallas_call(
        body, name=name, out_shape=[jax.ShapeDtypeStruct(a.shape[1:], a.dtype) for a in arrays],
        in_specs=[ANY] * n, out_specs=[ANY] * n,
        scratch_shapes=[pltpu.SemaphoreType.DMA((n,)), pltpu.SemaphoreType.DMA((n,))])(*arrays)


def _exchange_chips(arrays, name):
    n = len(arrays)

    def body(*refs):
        ins, outs = refs[:n], refs[n:2 * n]
        send_sems, recv_sems, local_sems = refs[2 * n:]
        x, y, c = lax.axis_index("x"), lax.axis_index("y"), lax.axis_index("c")
        partners = [(x, 1 - y), (1 - x, y), (1 - x, 1 - y)]
        mine = [pltpu.make_async_copy(ins[a].at[2 * x + y], outs[a].at[0], local_sems.at[a]) for a in range(n)]
        copies = [pltpu.make_async_remote_copy(
            src_ref=ins[a].at[2 * px + py], dst_ref=outs[a].at[1 + k], send_sem=send_sems.at[a, k],
            recv_sem=recv_sems.at[a, k], device_id=(px, py, c), device_id_type=MESH)
            for a in range(n) for k, (px, py) in enumerate(partners)]
        for cp in mine + copies:
            cp.start()
        for cp in copies + mine:
            cp.wait()

    return pl.pallas_call(
        body, name=name, out_shape=[jax.ShapeDtypeStruct(a.shape, a.dtype) for a in arrays],
        in_specs=[ANY] * n, out_specs=[ANY] * n,
        scratch_shapes=[pltpu.SemaphoreType.DMA((n, 3)), pltpu.SemaphoreType.DMA((n, 3)), pltpu.SemaphoreType.DMA((n,))],
    )(*arrays)


def _as_rows(a, lead):
    return a.reshape(a.shape[:lead] + (math.prod(a.shape[lead:-1]), a.shape[-1]))


def _add_pairs(a, b, name):
    a2, b2 = _as_rows(a, 0), _as_rows(b, 0)
    rows, cols = a2.shape
    tr = _row_tile(rows, cols * 4)

    def body(a_ref, b_ref, o_ref):
        o_ref[...] = (a_ref[...].astype(F32) + b_ref[...].astype(F32)).astype(o_ref.dtype)

    spec = _rows(tr, cols)
    out = pl.pallas_call(body, name=name, out_shape=jax.ShapeDtypeStruct(a2.shape, a.dtype), grid=(rows // tr,),
                         in_specs=[spec, spec], out_specs=spec, compiler_params=_params("parallel"))(a2, b2)
    return out.reshape(a.shape)


def _sum_blocks(a, name):
    a3 = _as_rows(a, 1)
    n, rows, cols = a3.shape
    tr = _row_tile(rows, n * cols * 4)

    def body(a_ref, o_ref):
        tot = a_ref[0].astype(F32)
        for k in range(1, n):
            tot = tot + a_ref[k].astype(F32)
        o_ref[...] = tot

    out = pl.pallas_call(body, name=name, out_shape=jax.ShapeDtypeStruct((rows, cols), F32), grid=(rows // tr,),
                         in_specs=[pl.BlockSpec((n, tr, cols), lambda j: (0, j, 0))], out_specs=_rows(tr, cols),
                         compiler_params=_params("parallel"))(a3)
    return out.reshape(a.shape[1:])


MIX_GROUPS = ("w_in", "w_uq", "w_uk", "w_uv", "w_attn_o", "w_conv_o", "w_pool_o", "w_mix_o")
FFN_GROUPS = ("w_gate", "w_up", "w_down")


def _pad_axis(a, axis, size):
    pad = [(0, 0)] * a.ndim
    pad[axis] = (0, size - a.shape[axis])
    return jnp.pad(a, pad)


def _local_groups(sh, l):
    out = {n: sh[n][l] for n in BIG}
    for n in ("w_uq", "w_uk", "w_uv"):
        out[n] = _pad_axis(out[n], -1, HEAD_PAD)
    for n in ("w_gate", "w_up"):
        out[n] = _pad_axis(out[n], -1, FF_SHARD_PAD)
    out["w_down"] = _pad_axis(out["w_down"], 0, FF_SHARD_PAD)
    return {n: v.astype(BF16) for n, v in out.items()}


def _arrange_w_in(blocks):
    parts, pos = [], 0
    for ref_lo, ref_hi, at in sorted(W_IN_PIECES, key=lambda p: p[2]):
        if at > pos:
            parts.append(jnp.zeros((blocks.shape[1], at - pos), blocks.dtype))
        for d in range(N_DEV):
            lo, hi = max(ref_lo, d * W_IN_SHARD), min(ref_hi, (d + 1) * W_IN_SHARD)
            if lo < hi:
                parts.append(blocks[d][:, lo - d * W_IN_SHARD:hi - d * W_IN_SHARD])
        pos = at + ref_hi - ref_lo
    if pos < Z_W:
        parts.append(jnp.zeros((blocks.shape[1], Z_W - pos), blocks.dtype))
    return jnp.concatenate(parts, axis=1)


def _w_in_shard(g, d):
    parts = []
    for ref_lo, ref_hi, at in W_IN_PIECES:
        lo, hi = max(ref_lo, d * W_IN_SHARD), min(ref_hi, (d + 1) * W_IN_SHARD)
        if lo < hi:
            parts.append(g[:, at + lo - ref_lo:at + hi - ref_lo])
    return jnp.concatenate(parts, axis=1)


def _mixer_weights(gat):
    w = dict(gat)
    w["w_in"] = _arrange_w_in(gat["w_in"])
    attn_o = gat["w_attn_o"].reshape(N_DEV, N_HEADS, V_HEAD, LANES)
    w["w_attn_o"] = _pad_axis(attn_o, 2, HEAD_PAD).reshape(N_DEV, N_HEADS * HEAD_PAD, LANES)
    w["w_mix_o"] = gat["w_mix_o"].reshape(D_MODEL, D_MODEL)
    return w


def _ffn_weights(gat):
    return {"w_gate": gat["w_gate"], "w_up": gat["w_up"], "w_down": gat["w_down"].reshape(D_FF_PAD, D_MODEL)}


def _mixer_grad_groups(gb):
    g = dict(gb)
    g["w_in"] = jnp.stack([_w_in_shard(gb["w_in"], d) for d in range(N_DEV)])
    attn_o = gb["w_attn_o"].reshape(N_DEV, N_HEADS, HEAD_PAD, LANES)[:, :, :V_HEAD]
    g["w_attn_o"] = attn_o.reshape(N_DEV, N_HEADS * V_HEAD, LANES)
    g["w_mix_o"] = gb["w_mix_o"].reshape(N_DEV, D_MODEL // N_DEV, D_MODEL)
    return g


def _ffn_grad_groups(gb):
    return {"w_gate": gb["w_gate"], "w_up": gb["w_up"], "w_down": gb["w_down"].reshape(N_DEV, FF_SHARD_PAD, D_MODEL)}


def _grads_from_groups(tot):
    g = dict(tot)
    g["w_uq"] = tot["w_uq"][:, :QK_NOPE + QK_ROPE]
    g["w_uk"], g["w_uv"] = tot["w_uk"][:, :QK_NOPE], tot["w_uv"][:, :V_HEAD]
    g["w_gate"], g["w_up"] = tot["w_gate"][:, :FF_SHARD], tot["w_up"][:, :FF_SHARD]
    g["w_down"] = tot["w_down"][:FF_SHARD]
    return g


SMALL_GROUPS = (
    (D_MODEL, ("mix_norm_pre", "mix_norm_post", "ffn_norm_pre", "ffn_norm_post")),
    (CONV_C, ("conv_w", "conv_b", "conv_ln_g", "conv_ln_b", "pool_scale")),
    (Q_RANK, ("q_norm",)), (KV_RANK, ("kv_norm",)), (POOL_GD, ("pool_w",)),
)


def _small_rows(name):
    return {"conv_w": CONV_HALO, "pool_w": POOL_G * POOL_GD}.get(name, SUBLANES)


def _small_groups(small):
    out = []
    for width, names in SMALL_GROUPS:
        parts = []
        for l in range(DEPTH):
            for n in names:
                part = small[l][n].reshape(-1, width)
                parts.append(_pad_axis(part, 0, _small_rows(n)))
        out.append(jnp.concatenate(parts, axis=0))
    return out


def _small_from_groups(groups):
    shapes = {"conv_w": (CONV_W, CONV_C), "pool_w": (POOL_G, POOL_GD, POOL_GD)}
    out = {}
    for (width, names), g in zip(SMALL_GROUPS, groups):
        row = 0
        for l in range(DEPTH):
            for n in names:
                rows = _small_rows(n)
                real = {"conv_w": CONV_W, "pool_w": POOL_G * POOL_GD}.get(n, 1)
                out.setdefault(n, []).append(g[row:row + real].reshape(shapes.get(n, (width,))))
                row += rows
    return {n: jnp.stack(v) for n, v in out.items()}


def _mixer_fwd(x, tables, w, sm, tag, ride):
    nm = lambda n: f"{n}_{tag}"
    h = _rms_fwd(x, (D_MODEL, 0), sm["mix_norm_pre"], BF16, nm("mix_pre_norm"))
    z = _matmul(h, w["w_in"], "nn", F32, nm("in_proj"))
    cq = _rms_fwd(z, ZC_Q, sm["q_norm"], BF16, nm("q_norm"))
    ckv = _rms_fwd(z, ZC_KV, sm["kv_norm"], BF16, nm("kv_norm"))
    qf = _matmul(cq, w["w_uq"], "nn", F32, nm("q_up"))
    kf = _matmul(ckv, w["w_uk"], "nn", F32, nm("k_up"))
    v = _matmul(ckv, w["w_uv"], "nn", BF16, nm("v_up"))
    q, k = _rope_qk_fwd(qf, kf, z, tables, nm("rope_qk"))
    (o, lse), rode = _flash_fwd(q, k, v, nm("flash_fwd"), ride)
    y_attn = _matmul(o, w["w_attn_o"], "nn", F32, nm("attn_out"))
    hc, co = _conv_fwd(z, sm["conv_w"], sm["conv_b"], sm["conv_ln_g"], sm["conv_ln_b"], nm("conv_fwd"))
    y_conv = _matmul(hc, w["w_conv_o"], "nn", F32, nm("conv_out"))
    pm = _pool_fwd(z, sm["pool_w"], sm["pool_scale"], nm("pool_fwd"))
    y_pool = _matmul(pm, w["w_pool_o"], "nn", F32, nm("pool_out"))
    ys = (y_attn, y_conv, y_pool)
    merged = _merge_fwd(z, ys, nm("merge_fwd"))
    mo = _matmul(merged, w["w_mix_o"], "nn", F32, nm("mix_out"))
    x_mid = _rms_fwd(mo, (D_MODEL, 0), sm["mix_norm_post"], F32, nm("mix_post_norm"), res=x)
    saved = dict(x=x, h=h, z=z, cq=cq, ckv=ckv, q=q, k=k, v=v, o=o, lse=lse, hc=hc, co=co, pm=pm, ys=ys, merged=merged,
                 mo=mo)
    return x_mid, saved, rode


def _ffn_fwd(x_mid, w, sm, tag):
    nm = lambda n: f"{n}_{tag}"
    h2 = _rms_fwd(x_mid, (D_MODEL, 0), sm["ffn_norm_pre"], BF16, nm("ffn_pre_norm"))
    hg = _matmul(h2, w["w_gate"], "nn", F32, nm("ffn_gate"))
    hu = _matmul(h2, w["w_up"], "nn", F32, nm("ffn_up"))
    act = _swiglu_fwd(hg, hu, nm("swiglu_fwd"))
    fo = _matmul(act, w["w_down"], "nn", F32, nm("ffn_down"))
    out = _rms_fwd(fo, (D_MODEL, 0), sm["ffn_norm_post"], F32, nm("ffn_post_norm"), res=x_mid)
    saved = dict(x_mid=x_mid, h2=h2, hg=hg, hu=hu, act=act, fo=fo)
    return out, saved


def _ffn_bwd(dout, sv, w, sm, tag):
    nm = lambda n: f"{n}_{tag}"
    gb, gs = {}, {}
    dfo, gs["ffn_norm_post"] = _rms_bwd(sv["fo"], (D_MODEL, 0), sm["ffn_norm_post"], dout, BF16, nm("ffn_post_norm_bwd"))
    dact = _matmul(dfo, w["w_down"], "nt", F32, nm("ffn_down_dx"))
    gb["w_down"] = _matmul(sv["act"], dfo, "tn", BF16, nm("ffn_down_dw"))
    dhg, dhu = _swiglu_bwd(sv["hg"], sv["hu"], dact, nm("swiglu_bwd"))
    dh2_g = _matmul(dhg, w["w_gate"], "nt", F32, nm("ffn_gate_dx"))
    dh2 = _matmul(dhu, w["w_up"], "nt", F32, nm("ffn_up_dx"), add=dh2_g)
    gb["w_gate"] = _matmul(sv["h2"], dhg, "tn", BF16, nm("ffn_gate_dw"), blocked=True)
    gb["w_up"] = _matmul(sv["h2"], dhu, "tn", BF16, nm("ffn_up_dw"), blocked=True)
    dmid, gs["ffn_norm_pre"] = _rms_bwd(sv["x_mid"], (D_MODEL, 0), sm["ffn_norm_pre"], dh2, F32, nm("ffn_pre_norm_bwd"),
                                        add=dout)
    return dmid, gb, gs


def _mixer_bwd(dmid, sv, tables, w, sm, tag, make_ride):
    nm = lambda n: f"{n}_{tag}"
    gb, gs = {}, {}
    dmo, gs["mix_norm_post"] = _rms_bwd(sv["mo"], (D_MODEL, 0), sm["mix_norm_post"], dmid, BF16, nm("mix_post_norm_bwd"))
    dmerged = _matmul(dmo, w["w_mix_o"], "nt", F32, nm("mix_out_dx"))
    gb["w_mix_o"] = _matmul(sv["merged"], dmo, "tn", BF16, nm("mix_out_dw"))
    dya, dyc, dyp, dg0, dg1, dg2 = _merge_bwd(sv["z"], sv["ys"], dmerged, nm("merge_bwd"))
    dpm = _matmul(dyp, w["w_pool_o"], "nt", F32, nm("pool_out_dx"))
    gb["w_pool_o"] = _matmul(sv["pm"], dyp, "tn", BF16, nm("pool_out_dw"), blocked=True)
    du_pool, gs["pool_w"], gs["pool_scale"] = _pool_bwd(dpm, sv["z"], sm["pool_w"], sm["pool_scale"], nm("pool_bwd"))
    dhc = _matmul(dyc, w["w_conv_o"], "nt", F32, nm("conv_out_dx"))
    gb["w_conv_o"] = _matmul(sv["hc"], dyc, "tn", BF16, nm("conv_out_dw"), blocked=True)
    dco, gs["conv_ln_g"], gs["conv_ln_b"], gs["conv_b"] = _conv_bwd_norm(dhc, sv["co"], sm["conv_ln_g"], sm["conv_ln_b"],
                                                                        nm("conv_bwd_norm"))
    du_conv, gs["conv_w"] = _conv_bwd_taps(dco, sv["z"], sm["conv_w"], nm("conv_bwd_taps"))
    do = _matmul(dya, w["w_attn_o"], "nt", F32, nm("attn_out_dx"))
    gb["w_attn_o"] = _matmul(sv["o"], dya, "tn", BF16, nm("attn_out_dw"), blocked=True)
    delta, dob = _attn_delta(do, sv["o"], nm("attn_delta"))
    (dq, dk, dv), rode = _flash_bwd(sv["q"], sv["k"], sv["v"], dob, sv["lse"], delta, nm("flash_bwd"), make_ride(gb))
    dqf, dkf, dkr = _rope_qk_bwd(dq, dk, tables, nm("rope_qk_bwd"))
    dcq_n = _matmul(dqf, w["w_uq"], "nt", F32, nm("q_up_dx"))
    gb["w_uq"] = _matmul(sv["cq"], dqf, "tn", BF16, nm("q_up_dw"), blocked=True)
    dckv_k = _matmul(dkf, w["w_uk"], "nt", F32, nm("k_up_dx"))
    dckv_n = _matmul(dv, w["w_uv"], "nt", F32, nm("v_up_dx"), add=dckv_k)
    gb["w_uk"] = _matmul(sv["ckv"], dkf, "tn", BF16, nm("k_up_dw"), blocked=True)
    gb["w_uv"] = _matmul(sv["ckv"], dv, "tn", BF16, nm("v_up_dw"), blocked=True)
    dcq, gs["q_norm"] = _rms_bwd(sv["z"], ZC_Q, sm["q_norm"], dcq_n, BF16, nm("q_norm_bwd"))
    dckv, gs["kv_norm"] = _rms_bwd(sv["z"], ZC_KV, sm["kv_norm"], dckv_n, BF16, nm("kv_norm_bwd"))
    dz = jnp.concatenate([dcq, dkr, du_pool, du_conv, dg0, dg1, dg2, dckv], axis=1)
    dh = _matmul(dz, w["w_in"], "nt", F32, nm("in_proj_dx"))
    gb["w_in"] = _matmul(sv["h"], dz, "tn", BF16, nm("in_proj_dw"))
    dx, gs["mix_norm_pre"] = _rms_bwd(sv["x"], (D_MODEL, 0), sm["mix_norm_pre"], dh, F32, nm("mix_pre_norm_bwd"), add=dmid)
    return dx, gb, gs, rode


def _part_groups(part):
    return MIX_GROUPS if part == "mix" else FFN_GROUPS


class _Plan:
    def __init__(self, shards, conv_w):
        self.local = [_local_groups(shards, l) for l in range(DEPTH)]
        self.conv_w = conv_w
        self.gat, self.send, self.recv = {}, {}, {}

    @staticmethod
    def _riders(l):
        return [(l, "ffn")] + ([(l + 1, "mix")] if l + 1 < DEPTH else [])

    def gather_first(self):
        out = _all_gather([self.local[0][g] for g in MIX_GROUPS] + [self.conv_w], "gather_mixer_l0")
        self.gat[(0, "mix")] = dict(zip(MIX_GROUPS, out[:-1]))
        return out[-1]

    def fwd_ride(self, l):
        return _GatherRide([self.local[ll][g] for ll, part in self._riders(l) for g in _part_groups(part)])

    def fwd_done(self, l, outs):
        outs = list(outs)
        for ll, part in self._riders(l):
            self.gat[(ll, part)] = {g: outs.pop(0) for g in _part_groups(part)}

    def mixer_weights(self, l):
        return _mixer_weights(self.gat[(l, "mix")])

    def ffn_weights(self, l):
        return _ffn_weights(self.gat[(l, "ffn")])

    def add_grads(self, l, part, gb):
        self.send[(l, part)] = _mixer_grad_groups(gb) if part == "mix" else _ffn_grad_groups(gb)

    def bwd_ride(self, l):
        return _ReduceRide([self.send[(ll, part)][g] for ll, part in self._riders(l) for g in _part_groups(part)])

    def bwd_done(self, l, outs):
        outs = list(outs)
        for ll, part in self._riders(l):
            self.recv[(ll, part)] = {g: outs.pop(0) for g in _part_groups(part)}

    def finish(self):
        send = [self.send[(0, "mix")][g] for g in MIX_GROUPS]
        by_core = [a.reshape((4, 2) + a.shape[1:]).transpose((1, 0) + tuple(range(2, a.ndim + 1))) for a in send]
        core = lax.axis_index("c")
        own = [lax.dynamic_index_in_dim(a, core, axis=0, keepdims=False) for a in by_core]
        got = _swap_with_sibling(by_core, "reduce_d2d")
        pairs = [_add_pairs(a, b, f"reduce_pair_add_{g}") for g, a, b in zip(MIX_GROUPS, own, got)]
        self.recv[(0, "mix")] = dict(zip(MIX_GROUPS, _exchange_chips(pairs, "reduce_ici")))
        layers = []
        for l in range(DEPTH):
            tot = {g: _sum_blocks(a, f"reduce_sum_{g}_l{l}") for part in ("mix", "ffn")
                   for g, a in self.recv[(l, part)].items()}
            layers.append(_grads_from_groups(tot))
        return {n: jnp.stack([layers[l][n] for l in range(DEPTH)]) for n in BIG}


def _local_step(x, positions, target, smalls, plan):
    tables = _rope_tables(positions)
    saved = []
    h = x
    for l in range(DEPTH):
        wm = plan.mixer_weights(l)
        h, svm, rode = _mixer_fwd(h, tables, wm, smalls[l], f"l{l}", plan.fwd_ride(l))
        plan.fwd_done(l, rode)
        wf = plan.ffn_weights(l)
        h, svf = _ffn_fwd(h, wf, smalls[l], f"l{l}")
        saved.append((svm, svf, wm, wf))
    dy, sq = _loss_grad(h, target, "loss_grad")
    small = [None] * DEPTH
    for l in reversed(range(DEPTH)):
        svm, svf, wm, wf = saved[l]
        dmid, gbf, gsf = _ffn_bwd(dy, svf, wf, smalls[l], f"l{l}")
        plan.add_grads(l, "ffn", gbf)
        dy, gbm, gsm, rode = _mixer_bwd(dmid, svm, tables, wm, smalls[l], f"l{l}", lambda gb, l=l: plan.bwd_ride(l))
        plan.bwd_done(l, rode)
        plan.add_grads(l, "mix", gbm)
        small[l] = {**gsf, **gsm}
    return sq, dy, small


def kernel(x, positions, mix_norm_pre, w_in, q_norm, w_uq, kv_norm, w_uk, w_uv, w_attn_o, conv_w, conv_b, conv_ln_g, conv_ln_b, w_conv_o, pool_w, pool_scale, w_pool_o, w_mix_o, mix_norm_post, ffn_norm_pre, w_gate, w_up, w_down, ffn_norm_post, loss_target, m_mix_norm_pre, m_w_in, m_q_norm, m_w_uq, m_kv_norm, m_w_uk, m_w_uv, m_w_attn_o, m_conv_w, m_conv_b, m_conv_ln_g, m_conv_ln_b, m_w_conv_o, m_pool_w, m_pool_scale, m_w_pool_o, m_w_mix_o, m_mix_norm_post, m_ffn_norm_pre, m_w_gate, m_w_up, m_w_down, m_ffn_norm_post, v_mix_norm_pre, v_w_in, v_q_norm, v_w_uq, v_kv_norm, v_w_uk, v_w_uv, v_w_attn_o, v_conv_w, v_conv_b, v_conv_ln_g, v_conv_ln_b, v_w_conv_o, v_pool_w, v_pool_scale, v_w_pool_o, v_w_mix_o, v_mix_norm_post, v_ffn_norm_pre, v_w_gate, v_w_up, v_w_down, v_ffn_norm_post):
    given = dict(locals())
    dev = 4 * lax.axis_index("x") + 2 * lax.axis_index("y") + lax.axis_index("c")

    plan = _Plan({n: given[n] for n in BIG}, conv_w)
    cw = CONV_C // N_DEV
    conv_w_full = plan.gather_first().transpose(1, 2, 0, 3).reshape(DEPTH, CONV_W, CONV_C)
    smalls = []
    for l in range(DEPTH):
        sm = {n: given[n][l] for n in SMALL if n != "conv_w"}
        sm["conv_w"] = _pad_axis(conv_w_full[l], 0, CONV_HALO)
        smalls.append(sm)

    sq, grad_x, small = _local_step(x[0], positions[0], loss_target[0], smalls, plan)
    loss = lax.psum(0.5 / D_MODEL * jnp.sum(sq), ("x", "y", "c"))
    grads = plan.finish()

    small_groups = _all_gather(_small_groups(small), "gather_small_grads")
    small_sum = _small_from_groups([_sum_blocks(g, f"sum_small_grads_{i}") for i, g in enumerate(small_groups)])
    for n in SMALL:
        grads[n] = small_sum[n]
    grads["conv_w"] = lax.dynamic_slice_in_dim(small_sum["conv_w"], dev * cw, cw, axis=2)

    delta, new_m, new_v = {}, {}, {}
    for n in WEIGHTS:
        delta[n], new_m[n], new_v[n] = _adamw(given[n], grads[n], given["m_" + n], given["v_" + n], f"adamw_{n}")
    return (loss, grad_x[None], *[grads[n] for n in WEIGHTS], *[delta[n] for n in WEIGHTS],
            *[new_m[n] for n in WEIGHTS], *[new_v[n] for n in WEIGHTS])
```

```python
import functools
import math

import jax
import jax.numpy as jnp
from jax import lax
from jax.experimental import pallas as pl
from jax.experimental.pallas import tpu as pltpu

F32, BF16 = jnp.float32, jnp.bfloat16
MESH = pl.DeviceIdType.MESH

LANES = 128
SUBLANES = 8
VMEM_LIMIT_BYTES = 56 * 1024 * 1024

N_DEV = 8
D_MODEL = 1024
DEPTH = 2
N_HEADS = 8
QK_NOPE, QK_ROPE, V_HEAD = 64, 32, 64
HEAD_PAD = LANES
Q_RANK, KV_RANK = 384, 256
ROPE_THETA = 10000.0
CONV_C, CONV_W = 512, 31
CONV_HALO = 32
POOL_WINDOWS = (2, 4, 8, 16)
POOL_C, POOL_G = 512, 4
POOL_GD = POOL_C // POOL_G
D_FF = 2816
FF_SHARD = D_FF // N_DEV
FF_SHARD_PAD = 3 * LANES
D_FF_PAD = N_DEV * FF_SHARD_PAD
W_IN_SHARD = 660
EPS = 1e-6
ATTN_SCALE = 1.0 / math.sqrt(QK_NOPE + QK_ROPE)
LOG2E = 1.4426950408889634
LR, B1, B2, ADAM_EPS, WD, STEP = 0.001, 0.9, 0.999, 1e-08, 0.01, 10

Z_W = 5376
ZC_GATE = (1024, 0)
ZC_GATES = (3072, 0)
ZC_CONV_A = (512, 6)
ZC_CONV_G = (512, 7)
ZC_CONV = (1024, 3)
ZC_POOL = (512, 8)
ZC_Q = (384, 12)
ZC_KR = (128, 39)
ZC_KV = (256, 20)
W_IN_PIECES = ((0, 384, 4608), (384, 640, 5120), (640, 672, 5056), (672, 1696, 3072), (1696, 2208, 4096),
               (2208, 5280, 0))

BIG = ("w_in", "w_uq", "w_uk", "w_uv", "w_attn_o", "w_conv_o", "w_pool_o", "w_mix_o", "w_gate", "w_up", "w_down")
SMALL = ("mix_norm_pre", "q_norm", "kv_norm", "conv_w", "conv_b", "conv_ln_g", "conv_ln_b", "pool_w", "pool_scale",
         "mix_norm_post", "ffn_norm_pre", "ffn_norm_post")
WEIGHTS = ("mix_norm_pre", "w_in", "q_norm", "w_uq", "kv_norm", "w_uk", "w_uv", "w_attn_o", "conv_w", "conv_b",
           "conv_ln_g", "conv_ln_b", "w_conv_o", "pool_w", "pool_scale", "w_pool_o", "w_mix_o", "mix_norm_post",
           "ffn_norm_pre", "w_gate", "w_up", "w_down", "ffn_norm_post")


def _params(*semantics):
    return pltpu.CompilerParams(dimension_semantics=semantics, vmem_limit_bytes=VMEM_LIMIT_BYTES)


def _tile(dim, cap):
    if dim <= cap:
        return dim
    for t in range(cap - cap % LANES, 0, -LANES):
        if dim % t == 0:
            return t
    raise ValueError(f"no tile for {dim} under {cap}")


def _row_tile(rows, row_bytes, budget=1 << 20):
    if rows * row_bytes <= budget:
        return rows
    cap = max(16, budget // row_bytes)
    for t in range(cap - cap % 16, 0, -16):
        if rows % t == 0:
            return t
    return rows


def _rows(ts, width, cidx=0):
    return pl.BlockSpec((ts, width), lambda i: (i, cidx))


def _fixed(shape):
    return pl.BlockSpec(shape, lambda *_: (0,) * len(shape))


def _sigmoid(x):
    return 1.0 / (1.0 + jnp.exp(-x))


def _block_count(n_blocks, width, cap):
    return max(c for c in range(1, n_blocks + 1) if n_blocks % c == 0 and c * width <= max(cap, width))


def _matmul(a, b, mode, out_dtype, name, add=None, blocked=False):
    nb = n_blk = 0
    blocked = blocked or b.ndim == 3
    if mode == "nn":
        (m, k) = a.shape
        n = b.shape[0] * b.shape[2] if blocked else b.shape[1]
    elif mode == "nt":
        (m, k) = a.shape
        n = b.shape[1] if blocked else b.shape[0]
    else:
        (k, m), n = a.shape, b.shape[1]
    tm, tn, tk = _tile(m, 1024), _tile(n, 1408), _tile(k, 1408 if mode != "tn" else 1024)
    if blocked:
        nb = b.shape[2] if mode != "tn" else n // N_DEV
        n_blk = _block_count(N_DEV, nb, 1536 if mode == "nt" else 1408)
        if mode == "nt":
            tk = n_blk * nb
        else:
            tn = n_blk * nb
    nk = k // tk
    dims = {"nn": ((1,), (0,)), "nt": ((1,), (1,)), "tn": ((0,), (0,))}[mode]
    a_spec = {"nn": pl.BlockSpec((tm, tk), lambda i, j, s: (i, s)), "nt": pl.BlockSpec((tm, tk), lambda i, j, s: (i, s)),
              "tn": pl.BlockSpec((tk, tm), lambda i, j, s: (s, i))}[mode]
    b_spec = {"nn": pl.BlockSpec((tk, tn), lambda i, j, s: (s, j)), "nt": pl.BlockSpec((tn, tk), lambda i, j, s: (j, s)),
              "tn": pl.BlockSpec((tk, tn), lambda i, j, s: (s, j))}[mode]
    o_spec = pl.BlockSpec((tm, tn), lambda i, j, s: (i, j))
    out_shape = jax.ShapeDtypeStruct((m, n), out_dtype)
    if blocked and mode == "nn":
        b_spec = pl.BlockSpec((n_blk, tk, nb), lambda i, j, s: (j, s, 0))
    elif blocked and mode == "nt":
        b_spec = pl.BlockSpec((n_blk, tn, nb), lambda i, j, s: (s, j, 0))
    elif blocked:
        o_spec = pl.BlockSpec((n_blk, tm, nb), lambda i, j, s: (j, i, 0))
        out_shape = jax.ShapeDtypeStruct((N_DEV, m, nb), out_dtype)
    has_add = add is not None

    def body(a_ref, b_ref, *rest):
        add_ref = rest[0] if has_add else None
        o_ref = rest[1] if has_add else rest[0]
        if blocked and mode != "tn":
            bv = jnp.concatenate([b_ref[c] for c in range(n_blk)], axis=1) if n_blk > 1 else b_ref[0]
        else:
            bv = b_ref[...]
        part = lax.dot_general(a_ref[...], bv, (dims, ((), ())), preferred_element_type=F32)

        def finish(total):
            if has_add:
                total = total + add_ref[...]
            if blocked and mode == "tn":
                for c in range(n_blk):
                    o_ref[c] = total[:, c * nb:(c + 1) * nb].astype(o_ref.dtype)
            else:
                o_ref[...] = total.astype(o_ref.dtype)

        if nk == 1:
            finish(part)
        else:
            acc = rest[-1]
            step = pl.program_id(2)

            @pl.when(step == 0)
            def _():
                acc[...] = part

            @pl.when(step > 0)
            def _():
                acc[...] += part

            @pl.when(step == nk - 1)
            def _():
                finish(acc[...])

    operands = (a, b, add) if has_add else (a, b)
    return pl.pallas_call(
        body, name=name, out_shape=out_shape, grid=(m // tm, n // tn, nk),
        in_specs=[a_spec, b_spec] + ([o_spec] if has_add else []), out_specs=o_spec,
        scratch_shapes=[pltpu.VMEM((tm, tn), F32)] if nk > 1 else [],
        compiler_params=_params("parallel", "parallel", "arbitrary"))(*operands)


def _rms_fwd(x, win, gain, out_dtype, name, res=None):
    width, cidx = win
    s = x.shape[0]
    ts = min(s, 512)
    has_res = res is not None

    def body(x_ref, g_ref, *rest):
        o_ref = rest[-1]
        xv = x_ref[...]
        r = lax.rsqrt(jnp.mean(xv * xv, axis=-1, keepdims=True) + EPS)
        y = (xv * r) * g_ref[...]
        if has_res:
            y = rest[0][...] + y
        o_ref[...] = y.astype(o_ref.dtype)

    ops = (x, gain.reshape(1, width)) + ((res,) if has_res else ())
    return pl.pallas_call(
        body, name=name, out_shape=jax.ShapeDtypeStruct((s, width), out_dtype), grid=(s // ts,),
        in_specs=[_rows(ts, width, cidx), _fixed((1, width))] + ([_rows(ts, width)] if has_res else []),
        out_specs=_rows(ts, width), compiler_params=_params("parallel"))(*ops)


def _into(dz, n_inputs, out_index):
    return dict(in_specs=[ANY], operands=(dz,), input_output_aliases={n_inputs: out_index},
                out_shape=jax.ShapeDtypeStruct(dz.shape, dz.dtype))


def _rms_bwd(x, win, gain, dy, out_dtype, name, add=None, dz=None):
    width, cidx = win
    s = x.shape[0]
    ts = min(s, 512)
    has_add = add is not None

    def body(x_ref, g_ref, dy_ref, *rest):
        dx_ref, dg_ref = rest[-2], rest[-1]
        xv = x_ref[...]
        r = lax.rsqrt(jnp.mean(xv * xv, axis=-1, keepdims=True) + EPS)
        xh = xv * r
        dyv = dy_ref[...].astype(F32)
        dyg = dyv * g_ref[...]
        dx = r * (dyg - xh * jnp.mean(dyg * xh, axis=-1, keepdims=True))
        if has_add:
            dx = dx + rest[0][...]
        dx_ref[...] = dx.astype(dx_ref.dtype)

        @pl.when(pl.program_id(0) == 0)
        def _():
            dg_ref[...] = jnp.zeros_like(dg_ref)

        dg_ref[...] += jnp.sum(dyv * xh, axis=0, keepdims=True)

    ops = (x, gain.reshape(1, width), dy) + ((add,) if has_add else ())
    in_specs = [_rows(ts, width, cidx), _fixed((1, width)), _rows(ts, width)] + ([_rows(ts, width)] if has_add else [])
    dx_shape, dx_spec, alias = jax.ShapeDtypeStruct((s, width), out_dtype), _rows(ts, width), {}
    if dz is not None:
        into = _into(dz, len(ops), 0)
        ops, in_specs, alias = ops + into["operands"], in_specs + into["in_specs"], into["input_output_aliases"]
        dx_shape, dx_spec = into["out_shape"], _rows(ts, width, cidx)
    dx, dg = pl.pallas_call(
        body, name=name, out_shape=(dx_shape, jax.ShapeDtypeStruct((1, width), F32)), grid=(s // ts,),
        in_specs=in_specs, out_specs=(dx_spec, _fixed((1, width))), input_output_aliases=alias,
        compiler_params=_params("arbitrary"))(*ops)
    return dx, dg.reshape(width)


def _rope(x, c, s1, s2):
    return x * c + pltpu.roll(x, 16, 1) * s1 + pltpu.roll(x, LANES - 16, 1) * s2


def _rope_t(g, c, s1, s2):
    return g * c + pltpu.roll(g * s1, LANES - 16, 1) + pltpu.roll(g * s2, 16, 1)


def _rope_tables(positions):
    inv_freq = ROPE_THETA ** (-jnp.arange(0, QK_ROPE, 2, dtype=F32) / QK_ROPE)
    ang = positions.astype(F32)[:, None] * inv_freq
    cos, sin = jnp.cos(ang), jnp.sin(ang)
    n = positions.shape[0]
    one, zero = jnp.ones((n, 1), F32), jnp.zeros((n, 1), F32)
    c = jnp.concatenate([jnp.tile(one, (1, QK_NOPE)), cos, cos, jnp.tile(one, (1, 32))], axis=1)
    s1 = jnp.concatenate([jnp.tile(zero, (1, QK_NOPE + 16)), sin, jnp.tile(zero, (1, 32))], axis=1)
    s2 = jnp.concatenate([jnp.tile(zero, (1, QK_NOPE)), -sin, jnp.tile(zero, (1, 48))], axis=1)
    return c, s1, s2


def _rope_qk_fwd(qf, kf, z, tables, name):
    s = qf.shape[0]
    ts = min(s, 256)
    hw = N_HEADS * HEAD_PAD

    def body(qf_ref, kf_ref, kr_ref, c_ref, s1_ref, s2_ref, q_ref, k_ref):
        c, s1, s2 = c_ref[...], s1_ref[...], s2_ref[...]
        kr = _rope(kr_ref[...], c, s1, s2)
        for h in range(N_HEADS):
            sl = slice(h * HEAD_PAD, (h + 1) * HEAD_PAD)
            q_ref[:, sl] = _rope(qf_ref[:, sl], c, s1, s2).astype(BF16)
            k_ref[:, sl] = (kf_ref[:, sl] + kr).astype(BF16)

    tab = _rows(ts, LANES)
    return pl.pallas_call(
        body, name=name, out_shape=(jax.ShapeDtypeStruct((s, hw), BF16),) * 2, grid=(s // ts,),
        in_specs=[_rows(ts, hw), _rows(ts, hw), _rows(ts, *ZC_KR), tab, tab, tab],
        out_specs=(_rows(ts, hw), _rows(ts, hw)), compiler_params=_params("parallel"))(qf, kf, z, *tables)


def _rope_qk_bwd(dq, dk, tables, dz, name):
    s = dq.shape[0]
    ts = min(s, 256)
    hw = N_HEADS * HEAD_PAD

    def body(dq_ref, dk_ref, c_ref, s1_ref, s2_ref, _, dqf_ref, dkf_ref, dkr_ref):
        c, s1, s2 = c_ref[...], s1_ref[...], s2_ref[...]
        ksum = jnp.zeros((ts, HEAD_PAD), F32)
        for h in range(N_HEADS):
            sl = slice(h * HEAD_PAD, (h + 1) * HEAD_PAD)
            dqf_ref[:, sl] = _rope_t(dq_ref[:, sl], c, s1, s2).astype(BF16)
            dkh = dk_ref[:, sl]
            dkf_ref[:, sl] = dkh.astype(BF16)
            ksum = ksum + dkh
        lane = lax.broadcasted_iota(jnp.int32, (ts, HEAD_PAD), 1)
        in_rope = (lane >= QK_NOPE) & (lane < QK_NOPE + QK_ROPE)
        dkr_ref[...] = jnp.where(in_rope, _rope_t(ksum, c, s1, s2), 0.0).astype(BF16)

    tab = _rows(ts, LANES)
    into = _into(dz, 5, 2)
    return pl.pallas_call(
        body, name=name,
        out_shape=(jax.ShapeDtypeStruct((s, hw), BF16), jax.ShapeDtypeStruct((s, hw), BF16), into["out_shape"]),
        grid=(s // ts,), in_specs=[_rows(ts, hw), _rows(ts, hw), tab, tab, tab] + into["in_specs"],
        out_specs=(_rows(ts, hw), _rows(ts, hw), _rows(ts, *ZC_KR)), input_output_aliases=into["input_output_aliases"],
        compiler_params=_params("parallel"))(dq, dk, *tables, dz)


def _attn_tile(s):
    return min(s, 512)


def _raw_scores(q, k, masked, row0=0):
    sc = lax.dot_general(q, k, (((1,), (1,)), ((), ())), preferred_element_type=F32)
    if masked:
        rows = row0 + lax.broadcasted_iota(jnp.int32, sc.shape, 0)
        cols = lax.broadcasted_iota(jnp.int32, sc.shape, 1)
        sc = jnp.where(cols <= rows, sc, -jnp.inf)
    return sc


def _ride_hooks(ride, refs, n_in, n_out, grid):
    if ride is None:
        return refs, lambda: None, lambda: None
    n = len(ride.arrays)
    own = refs[:n_in] + refs[n_in + n:n_in + n + n_out]
    ins, outs, sems = refs[n_in:n_in + n], refs[n_in + n + n_out:n_in + 2 * n + n_out], refs[n_in + 2 * n + n_out:]
    at_first = functools.reduce(lambda a, b: a & b, [pl.program_id(ax) == 0 for ax in range(len(grid))])
    at_last = functools.reduce(lambda a, b: a & b, [pl.program_id(ax) == g - 1 for ax, g in enumerate(grid)])
    return own, lambda: pl.when(at_first)(lambda: ride.start(ins, outs, sems)), \
        lambda: pl.when(at_last)(lambda: ride.finish(ins, outs, sems))


def _ride_call(ride, body, name, out_shape, grid, in_specs, out_specs, semantics, operands):
    n = 0 if ride is None else len(ride.arrays)
    res = pl.pallas_call(
        body, name=name, out_shape=tuple(out_shape) + (tuple(ride.out_shape) if n else ()), grid=grid,
        in_specs=list(in_specs) + [ANY] * n, out_specs=tuple(out_specs) + (ANY,) * n,
        scratch_shapes=list(ride.scratch) if n else [],
        compiler_params=_params(*(("arbitrary",) * len(grid) if n else semantics)))(*operands, *(ride.arrays if n else ()))
    return res[:len(out_shape)], list(res[len(out_shape):])


def _flash_fwd(q, k, v, name, ride=None):
    s = q.shape[0]
    t = _attn_tile(s)
    c2 = ATTN_SCALE * LOG2E
    grid = (N_HEADS, s // t)

    def body(*refs):
        (q_ref, k_ref, v_ref, o_ref, lse_ref), start, finish = _ride_hooks(ride, refs, 3, 2, grid)
        start()
        i = pl.program_id(1)
        qv = q_ref[...]

        def chunk(j, carry, masked):
            m_old, l_old, acc = carry
            at = pl.ds(pl.multiple_of(j * t, t), t)
            sc = _raw_scores(qv, k_ref[at, :], masked)
            m_new = jnp.maximum(m_old, jnp.max(sc, axis=-1, keepdims=True))
            p = jnp.exp2((sc - m_new) * c2)
            alpha = jnp.exp2((m_old - m_new) * c2)
            l_new = alpha * l_old + jnp.sum(p, axis=-1, keepdims=True)
            acc = alpha * acc + jnp.dot(p.astype(BF16), v_ref[at, :], preferred_element_type=F32)
            return m_new, l_new, acc

        init = (jnp.full((t, 1), -jnp.inf, F32), jnp.zeros((t, 1), F32), jnp.zeros((t, HEAD_PAD), F32))
        carry = lax.fori_loop(0, i, lambda j, cr: chunk(j, cr, False), init)
        m_fin, l_fin, acc = chunk(i, carry, True)
        o_ref[...] = (acc / l_fin).astype(o_ref.dtype)
        lse_ref[0] = m_fin * ATTN_SCALE + jnp.log(l_fin)
        finish()

    qo = pl.BlockSpec((t, HEAD_PAD), lambda h, i: (i, h))
    whole = pl.BlockSpec((s, HEAD_PAD), lambda h, i: (0, h))
    return _ride_call(
        ride, body, name, (jax.ShapeDtypeStruct(q.shape, BF16), jax.ShapeDtypeStruct((N_HEADS, s, 1), F32)), grid,
        [qo, whole, whole], (qo, pl.BlockSpec((1, t, 1), lambda h, i: (h, i, 0))), ("parallel", "parallel"), (q, k, v))


def _attn_delta(do, o, name):
    s = o.shape[0]
    t = _attn_tile(s)

    def body(do_ref, o_ref, delta_ref, dob_ref):
        dov = do_ref[...]
        delta_ref[0] = jnp.sum(dov * o_ref[...].astype(F32), axis=-1, keepdims=True)
        dob_ref[...] = dov.astype(BF16)

    blk = pl.BlockSpec((t, HEAD_PAD), lambda i, h: (i, h))
    return pl.pallas_call(
        body, name=name,
        out_shape=(jax.ShapeDtypeStruct((N_HEADS, s, 1), F32), jax.ShapeDtypeStruct(o.shape, BF16)),
        grid=(s // t, N_HEADS), in_specs=[blk, blk],
        out_specs=(pl.BlockSpec((1, t, 1), lambda i, h: (h, i, 0)), blk),
        compiler_params=_params("parallel", "parallel"))(do, o)


def _flash_bwd(q, k, v, do, lse, delta, name, ride=None):
    s = q.shape[0]
    t = _attn_tile(s)
    nt = s // t
    c2 = ATTN_SCALE * LOG2E
    grid = (N_HEADS, nt)

    def body(*refs):
        (q_ref, k_ref, v_ref, do_ref, lse_ref, delta_ref, dq_ref, dk_ref, dv_ref), start, finish = _ride_hooks(
            ride, refs, 6, 3, grid)
        start()
        j = pl.program_id(1)
        kv, vv = k_ref[...], v_ref[...]

        @pl.when(j == 0)
        def _():
            dq_ref[...] = jnp.zeros_like(dq_ref)

        def chunk(i, carry, masked):
            dk_acc, dv_acc = carry
            at = pl.ds(pl.multiple_of(i * t, t), t)
            qi, doi = q_ref[at, :], do_ref[at, :]
            sc = _raw_scores(qi, kv, masked)
            p = jnp.exp2(sc * c2 - lse_ref[0, at, :] * LOG2E)
            dp = lax.dot_general(doi, vv, (((1,), (1,)), ((), ())), preferred_element_type=F32)
            ds = (p * (dp - delta_ref[0, at, :])).astype(BF16)
            dv_acc = dv_acc + lax.dot_general(p.astype(BF16), doi, (((0,), (0,)), ((), ())), preferred_element_type=F32)
            dk_acc = dk_acc + lax.dot_general(ds, qi, (((0,), (0,)), ((), ())), preferred_element_type=F32)
            dq_ref[at, :] += jnp.dot(ds, kv, preferred_element_type=F32) * ATTN_SCALE
            return dk_acc, dv_acc

        zero = jnp.zeros((t, HEAD_PAD), F32)
        carry = chunk(j, (zero, zero), True)
        dk_acc, dv_acc = lax.fori_loop(j + 1, nt, lambda i, cr: chunk(i, cr, False), carry)
        dk_ref[...] = dk_acc * ATTN_SCALE
        dv_ref[...] = dv_acc.astype(BF16)
        finish()

    blk = pl.BlockSpec((t, HEAD_PAD), lambda h, j: (j, h))
    whole = pl.BlockSpec((s, HEAD_PAD), lambda h, j: (0, h))
    stat = pl.BlockSpec((1, s, 1), lambda h, j: (h, 0, 0))
    return _ride_call(
        ride, body, name, (jax.ShapeDtypeStruct(q.shape, F32), jax.ShapeDtypeStruct(q.shape, F32),
                           jax.ShapeDtypeStruct(q.shape, BF16)), grid,
        [whole, blk, blk, whole, stat, stat], (whole, blk, blk), ("parallel", "arbitrary"), (q, k, v, do, lse, delta))


def _conv_tile(s):
    return min(s, 256)


def _halo_before(t, width, cidx):
    per = t // CONV_HALO
    return pl.BlockSpec((CONV_HALO, width), lambda i: (jnp.maximum(i * per - 1, 0), cidx))


def _halo_after(t, width, cidx, n_tiles):
    per = t // CONV_HALO
    last = n_tiles * per - 1
    return pl.BlockSpec((CONV_HALO, width), lambda i: (jnp.minimum((i + 1) * per, last), cidx))


def _fill_glu(hbuf, ap_ref, gp_ref, a_ref, g_ref, t):
    first = pl.program_id(0) == 0
    hbuf[pl.ds(0, CONV_HALO), :] = jnp.where(first, 0.0, ap_ref[...] * _sigmoid(gp_ref[...]))
    hbuf[pl.ds(CONV_HALO, t), :] = a_ref[...] * _sigmoid(g_ref[...])


def _phase_copies(dst, src, t):
    n = t + CONV_HALO - SUBLANES
    for s in range(1, SUBLANES):
        dst[s, pl.ds(0, n), :] = src[pl.ds(s, n), :]


def _window(phases, src, k, t):
    if k % SUBLANES == 0:
        return src[pl.ds(k, t), :]
    return phases[k % SUBLANES, pl.ds(k - k % SUBLANES, t), :]


def _layer_norm_parts(co):
    mu = jnp.mean(co, axis=-1, keepdims=True)
    xc = co - mu
    rstd = lax.rsqrt(jnp.mean(xc * xc, axis=-1, keepdims=True) + EPS)
    return xc * rstd, rstd


def _conv_fwd(z, conv_w, conv_b, ln_g, ln_b, name):
    s = z.shape[0]
    t = _conv_tile(s)
    off = CONV_HALO - (CONV_W - 1)

    def body(ap_ref, gp_ref, a_ref, g_ref, w_ref, b_ref, lg_ref, lb_ref, hc_ref, co_ref, hbuf, hph):
        _fill_glu(hbuf, ap_ref, gp_ref, a_ref, g_ref, t)
        _phase_copies(hph, hbuf, t)
        acc = jnp.zeros((t, CONV_C), F32) + b_ref[...]
        for j in range(CONV_W):
            acc = acc + _window(hph, hbuf, off + j, t) * w_ref[pl.ds(j, 1), :]
        co_ref[...] = acc
        xh, _ = _layer_norm_parts(acc)
        y = xh * lg_ref[...] + lb_ref[...]
        hc_ref[...] = (y * _sigmoid(y)).astype(BF16)

    vec = _fixed((1, CONV_C))
    return pl.pallas_call(
        body, name=name, out_shape=(jax.ShapeDtypeStruct((s, CONV_C), BF16), jax.ShapeDtypeStruct((s, CONV_C), F32)),
        grid=(s // t,),
        in_specs=[_halo_before(t, *ZC_CONV_A), _halo_before(t, *ZC_CONV_G), _rows(t, *ZC_CONV_A), _rows(t, *ZC_CONV_G),
                  _fixed((CONV_HALO, CONV_C)), vec, vec, vec],
        out_specs=(_rows(t, CONV_C), _rows(t, CONV_C)),
        scratch_shapes=[pltpu.VMEM((t + CONV_HALO, CONV_C), F32), pltpu.VMEM((SUBLANES, t + CONV_HALO, CONV_C), F32)],
        compiler_params=_params("parallel"))(z, z, z, z, conv_w, conv_b.reshape(1, -1), ln_g.reshape(1, -1),
                                             ln_b.reshape(1, -1))


def _conv_bwd_norm(dhc, co, ln_g, ln_b, name):
    s = co.shape[0]
    t = min(s, 512)

    def body(dhc_ref, co_ref, lg_ref, lb_ref, dco_ref, dg_ref, db_ref, dcb_ref):
        xh, rstd = _layer_norm_parts(co_ref[...])
        y = xh * lg_ref[...] + lb_ref[...]
        sg = _sigmoid(y)
        dy = dhc_ref[...] * (sg * (1.0 + y * (1.0 - sg)))
        dxh = dy * lg_ref[...]
        dco = rstd * (dxh - jnp.mean(dxh, axis=-1, keepdims=True) - xh * jnp.mean(dxh * xh, axis=-1, keepdims=True))
        dco_ref[...] = dco

        @pl.when(pl.program_id(0) == 0)
        def _():
            dg_ref[...] = jnp.zeros_like(dg_ref)
            db_ref[...] = jnp.zeros_like(db_ref)
            dcb_ref[...] = jnp.zeros_like(dcb_ref)

        dg_ref[...] += jnp.sum(dy * xh, axis=0, keepdims=True)
        db_ref[...] += jnp.sum(dy, axis=0, keepdims=True)
        dcb_ref[...] += jnp.sum(dco, axis=0, keepdims=True)

    vec = _fixed((1, CONV_C))
    one = jax.ShapeDtypeStruct((1, CONV_C), F32)
    dco, dg, db, dcb = pl.pallas_call(
        body, name=name, out_shape=(jax.ShapeDtypeStruct((s, CONV_C), F32), one, one, one), grid=(s // t,),
        in_specs=[_rows(t, CONV_C), _rows(t, CONV_C), vec, vec], out_specs=(_rows(t, CONV_C), vec, vec, vec),
        compiler_params=_params("arbitrary"))(dhc, co, ln_g.reshape(1, -1), ln_b.reshape(1, -1))
    return dco, dg.reshape(-1), db.reshape(-1), dcb.reshape(-1)


def _conv_bwd_taps(dco, z, conv_w, dz, name):
    s = z.shape[0]
    t = _conv_tile(s)
    nt = s // t
    off = CONV_HALO - (CONV_W - 1)

    def body(ap_ref, gp_ref, a_ref, g_ref, d_ref, dn_ref, w_ref, _, du_ref, dw_ref, hbuf, dbuf, hph, dph):
        i = pl.program_id(0)
        _fill_glu(hbuf, ap_ref, gp_ref, a_ref, g_ref, t)
        dbuf[pl.ds(0, t), :] = d_ref[...]
        dbuf[pl.ds(t, CONV_HALO), :] = jnp.where(i == nt - 1, 0.0, dn_ref[...])
        _phase_copies(hph, hbuf, t)
        _phase_copies(dph, dbuf, t)

        @pl.when(i == 0)
        def _():
            dw_ref[...] = jnp.zeros_like(dw_ref)

        dcur = d_ref[...]
        dh = jnp.zeros((t, CONV_C), F32)
        for j in range(CONV_W):
            dh = dh + _window(dph, dbuf, CONV_W - 1 - j, t) * w_ref[pl.ds(j, 1), :]
            dw_ref[pl.ds(j, 1), :] += jnp.sum(dcur * _window(hph, hbuf, off + j, t), axis=0, keepdims=True)
        a, sg = a_ref[...], _sigmoid(g_ref[...])
        du_ref[:, pl.ds(0, CONV_C)] = (dh * sg).astype(BF16)
        du_ref[:, pl.ds(CONV_C, CONV_C)] = (dh * a * sg * (1.0 - sg)).astype(BF16)

    into = _into(dz, 7, 0)
    return pl.pallas_call(
        body, name=name, out_shape=(into["out_shape"], jax.ShapeDtypeStruct((CONV_HALO, CONV_C), F32)), grid=(nt,),
        in_specs=[_halo_before(t, *ZC_CONV_A), _halo_before(t, *ZC_CONV_G), _rows(t, *ZC_CONV_A), _rows(t, *ZC_CONV_G),
                  _rows(t, CONV_C), _halo_after(t, CONV_C, 0, nt), _fixed((CONV_HALO, CONV_C))] + into["in_specs"],
        out_specs=(_rows(t, *ZC_CONV), _fixed((CONV_HALO, CONV_C))), input_output_aliases=into["input_output_aliases"],
        scratch_shapes=[pltpu.VMEM((t + CONV_HALO, CONV_C), F32), pltpu.VMEM((t + CONV_HALO, CONV_C), F32),
                        pltpu.VMEM((SUBLANES, t + CONV_HALO, CONV_C), F32),
                        pltpu.VMEM((SUBLANES, t + CONV_HALO, CONV_C), F32)],
        compiler_params=_params("arbitrary"))(z, z, z, z, dco, dco, conv_w, dz)


def _pool_tile(s):
    return min(s, 512)


def _pool_counts(row0, n, window):
    rows = row0 + lax.broadcasted_iota(jnp.int32, (n, POOL_GD), 0)
    return jnp.minimum(rows + 1, window).astype(F32)


def _pool_diff(ubuf, gi, window, row0, t):
    lanes = pl.ds(gi * POOL_GD, POOL_GD)
    tot = ubuf[pl.ds(CONV_HALO, t), lanes]
    cur = tot
    for back in range(1, window):
        tot = tot + ubuf[pl.ds(CONV_HALO - back, t), lanes]
    return tot / _pool_counts(row0, t, window) - cur


def _pool_fwd(z, pool_w, pool_scale, name):
    s = z.shape[0]
    t = _pool_tile(s)

    def body(up_ref, u_ref, w_ref, sc_ref, m_ref, ubuf):
        i = pl.program_id(0)
        ubuf[pl.ds(0, CONV_HALO), :] = jnp.where(i == 0, 0.0, up_ref[...])
        ubuf[pl.ds(CONV_HALO, t), :] = u_ref[...]
        for gi, window in enumerate(POOL_WINDOWS):
            d = _pool_diff(ubuf, gi, window, i * t, t)
            mm = jnp.dot(d.astype(BF16), w_ref[gi].astype(BF16), preferred_element_type=F32)
            lanes = pl.ds(gi * POOL_GD, POOL_GD)
            m_ref[:, lanes] = (mm * sc_ref[:, lanes]).astype(BF16)

    return pl.pallas_call(
        body, name=name, out_shape=jax.ShapeDtypeStruct((s, POOL_C), BF16), grid=(s // t,),
        in_specs=[_halo_before(t, *ZC_POOL), _rows(t, *ZC_POOL), _fixed((POOL_G, POOL_GD, POOL_GD)), _fixed((1, POOL_C))],
        out_specs=_rows(t, POOL_C), scratch_shapes=[pltpu.VMEM((t + CONV_HALO, POOL_C), F32)],
        compiler_params=_params("parallel"))(z, z, pool_w, pool_scale.reshape(1, -1))


def _pool_bwd(dm, z, pool_w, pool_scale, dz, name):
    s = z.shape[0]
    t = _pool_tile(s)
    nt = s // t

    def body(up_ref, u_ref, dm_ref, dmn_ref, w_ref, sc_ref, _, du_ref, dw_ref, dsc_ref, ubuf, ebuf):
        i = pl.program_id(0)
        ubuf[pl.ds(0, CONV_HALO), :] = jnp.where(i == 0, 0.0, up_ref[...])
        ubuf[pl.ds(CONV_HALO, t), :] = u_ref[...]

        @pl.when(i == 0)
        def _():
            dw_ref[...] = jnp.zeros_like(dw_ref)
            dsc_ref[...] = jnp.zeros_like(dsc_ref)

        dm_next = jnp.where(i == nt - 1, 0.0, dmn_ref[...])
        for gi, window in enumerate(POOL_WINDOWS):
            lanes = pl.ds(gi * POOL_GD, POOL_GD)
            wb = w_ref[gi].astype(BF16)
            scale = sc_ref[:, lanes]
            d = _pool_diff(ubuf, gi, window, i * t, t).astype(BF16)
            mm = jnp.dot(d, wb, preferred_element_type=F32)
            dmv = dm_ref[:, lanes]
            dsc_ref[:, lanes] += jnp.sum(dmv * mm, axis=0, keepdims=True)
            dmm = (dmv * scale).astype(BF16)
            dw_ref[gi] += lax.dot_general(d, dmm, (((0,), (0,)), ((), ())), preferred_element_type=F32)
            dd = lax.dot_general(dmm, wb, (((1,), (1,)), ((), ())), preferred_element_type=F32)
            dd_next = lax.dot_general((dm_next[:, gi * POOL_GD:(gi + 1) * POOL_GD] * scale).astype(BF16), wb,
                                      (((1,), (1,)), ((), ())), preferred_element_type=F32)
            ebuf[pl.ds(0, t), lanes] = dd / _pool_counts(i * t, t, window)
            ebuf[pl.ds(t, CONV_HALO), lanes] = dd_next / _pool_counts((i + 1) * t, CONV_HALO, window)
            du = -dd
            for ahead in range(window):
                du = du + ebuf[pl.ds(ahead, t), lanes]
            du_ref[:, lanes] = du.astype(BF16)

    into = _into(dz, 6, 0)
    du, dw, dsc = pl.pallas_call(
        body, name=name,
        out_shape=(into["out_shape"], jax.ShapeDtypeStruct((POOL_G, POOL_GD, POOL_GD), F32),
                   jax.ShapeDtypeStruct((1, POOL_C), F32)), grid=(nt,),
        in_specs=[_halo_before(t, *ZC_POOL), _rows(t, *ZC_POOL), _rows(t, POOL_C), _halo_after(t, POOL_C, 0, nt),
                  _fixed((POOL_G, POOL_GD, POOL_GD)), _fixed((1, POOL_C))] + into["in_specs"],
        out_specs=(_rows(t, *ZC_POOL), _fixed((POOL_G, POOL_GD, POOL_GD)), _fixed((1, POOL_C))),
        input_output_aliases=into["input_output_aliases"],
        scratch_shapes=[pltpu.VMEM((t + CONV_HALO, POOL_C), F32), pltpu.VMEM((t + CONV_HALO, POOL_C), F32)],
        compiler_params=_params("arbitrary"))(z, z, dm, dm, pool_w, pool_scale.reshape(1, -1), dz)
    return du, dw, dsc.reshape(-1)


def _gate_specs(ts):
    width, first = ZC_GATE
    return [_rows(ts, width, first + b) for b in range(3)]


def _merge_fwd(z, ys, name):
    s = z.shape[0]
    ts = min(s, 256)

    def body(g0, g1, g2, y0, y1, y2, o_ref):
        o_ref[...] = (_sigmoid(g0[...]) * y0[...] + _sigmoid(g1[...]) * y1[...]
                      + _sigmoid(g2[...]) * y2[...]).astype(BF16)

    return pl.pallas_call(
        body, name=name, out_shape=jax.ShapeDtypeStruct((s, D_MODEL), BF16), grid=(s // ts,),
        in_specs=_gate_specs(ts) + [_rows(ts, D_MODEL)] * 3, out_specs=_rows(ts, D_MODEL),
        compiler_params=_params("parallel"))(z, z, z, *ys)


def _merge_bwd(z, ys, dmerged, name):
    s = z.shape[0]
    ts = min(s, 256)

    def body(g0, g1, g2, y0, y1, y2, dm_ref, dy0, dy1, dy2, dz_ref):
        dmv = dm_ref[...]
        for b, (g_ref, y_ref, dy_ref) in enumerate(((g0, y0, dy0), (g1, y1, dy1), (g2, y2, dy2))):
            sg = _sigmoid(g_ref[...])
            dy_ref[...] = (dmv * sg).astype(BF16)
            dz_ref[:, pl.ds(b * D_MODEL, D_MODEL)] = (dmv * y_ref[...] * sg * (1.0 - sg)).astype(BF16)

    out = jax.ShapeDtypeStruct((s, D_MODEL), BF16)
    return pl.pallas_call(
        body, name=name, out_shape=(out,) * 3 + (jax.ShapeDtypeStruct((s, Z_W), BF16),), grid=(s // ts,),
        in_specs=_gate_specs(ts) + [_rows(ts, D_MODEL)] * 4,
        out_specs=(_rows(ts, D_MODEL),) * 3 + (_rows(ts, *ZC_GATES),),
        compiler_params=_params("parallel"))(z, z, z, *ys, dmerged)


def _swiglu_fwd(hg, hu, name):
    s, f = hg.shape
    ts, tc = min(s, 512), _tile(f, 1024)
    blk = pl.BlockSpec((ts, tc), lambda i, j: (i, j))

    def body(g_ref, u_ref, o_ref):
        g = g_ref[...]
        o_ref[...] = (g * _sigmoid(g) * u_ref[...]).astype(BF16)

    return pl.pallas_call(
        body, name=name, out_shape=jax.ShapeDtypeStruct((s, f), BF16), grid=(s // ts, f // tc),
        in_specs=[blk, blk], out_specs=blk, compiler_params=_params("parallel", "parallel"))(hg, hu)


def _swiglu_bwd(hg, hu, dact, name):
    s, f = hg.shape
    ts, tc = min(s, 512), _tile(f, 1024)
    blk = pl.BlockSpec((ts, tc), lambda i, j: (i, j))

    def body(g_ref, u_ref, d_ref, dg_ref, du_ref):
        g, d = g_ref[...], d_ref[...]
        sg = _sigmoid(g)
        dg_ref[...] = (d * u_ref[...] * (sg * (1.0 + g * (1.0 - sg)))).astype(BF16)
        du_ref[...] = (d * g * sg).astype(BF16)

    out = jax.ShapeDtypeStruct((s, f), BF16)
    return pl.pallas_call(
        body, name=name, out_shape=(out, out), grid=(s // ts, f // tc), in_specs=[blk, blk, blk], out_specs=(blk, blk),
        compiler_params=_params("parallel", "parallel"))(hg, hu, dact)


def _loss_grad(y, target, name):
    s, d = y.shape
    ts = min(s, 512)

    def body(y_ref, t_ref, dy_ref, sq_ref):
        e = y_ref[...] - t_ref[...]
        dy_ref[...] = e / d

        @pl.when(pl.program_id(0) == 0)
        def _():
            sq_ref[...] = jnp.zeros_like(sq_ref)

        sq_ref[...] += jnp.sum(e * e, axis=0, keepdims=True)

    return pl.pallas_call(
        body, name=name, out_shape=(jax.ShapeDtypeStruct((s, d), F32), jax.ShapeDtypeStruct((1, d), F32)),
        grid=(s // ts,), in_specs=[_rows(ts, d), _rows(ts, d)], out_specs=(_rows(ts, d), _fixed((1, d))),
        compiler_params=_params("arbitrary"))(y, target)


def _adamw(w, g, m, v, name):
    shape = w.shape
    cols = shape[-1]
    keep3 = w.ndim == 3 and shape[1] < SUBLANES
    view = shape if keep3 else (math.prod(shape[:-1]), cols)
    rows = view[0]
    if keep3:
        cap = max(1, (1 << 20) // (SUBLANES * cols * 4))
        tr = max(t for t in range(1, cap + 1) if rows % t == 0)
    else:
        tr = _row_tile(rows, cols * 4)

    def body(w_ref, g_ref, m_ref, v_ref, d_ref, mo_ref, vo_ref):
        gv = g_ref[...]
        mn = B1 * m_ref[...] + (1.0 - B1) * gv
        vn = B2 * v_ref[...] + (1.0 - B2) * (gv * gv)
        m_hat = mn / (1.0 - B1 ** STEP)
        v_hat = vn / (1.0 - B2 ** STEP)
        d_ref[...] = -LR * (m_hat / (jnp.sqrt(v_hat) + ADAM_EPS) + WD * w_ref[...])
        mo_ref[...] = mn
        vo_ref[...] = vn

    spec = pl.BlockSpec((tr,) + view[1:], lambda i: (i,) + (0,) * (len(view) - 1))
    out = jax.ShapeDtypeStruct(view, F32)
    res = pl.pallas_call(
        body, name=name, out_shape=(out,) * 3, grid=(rows // tr,), in_specs=[spec] * 4, out_specs=(spec,) * 3,
        compiler_params=_params("parallel"))(*[t.reshape(view) for t in (w, g, m, v)])
    return tuple(r.reshape(shape) for r in res)


LANE_MAJOR = ("w_uq", "w_uk", "w_uv", "w_gate", "w_up")


def _lane_major(name, a):
    if name == "w_in":
        return a.transpose(2, 0, 1)
    if name in LANE_MAJOR:
        return a.transpose(0, 2, 1)
    return a


def _from_lane_major(name, a):
    if name == "w_in":
        return a.transpose(1, 2, 0)
    return _lane_major(name, a)


ANY = pl.BlockSpec(memory_space=pl.ANY)


class _GatherRide:
    def __init__(self, arrays):
        n = len(arrays)
        self.arrays = list(arrays)
        self.out_shape = [jax.ShapeDtypeStruct((N_DEV,) + a.shape, a.dtype) for a in arrays]
        self.scratch = [pltpu.SemaphoreType.DMA((n, 7)), pltpu.SemaphoreType.DMA((n, 7)), pltpu.SemaphoreType.DMA((n,))]

    def _copies(self, ins, outs, sems):
        send_sems, recv_sems, local_sems = sems
        n = len(self.arrays)
        x, y, c = lax.axis_index("x"), lax.axis_index("y"), lax.axis_index("c")
        me, sibling = (x, y, c), (x, y, 1 - c)
        chips = [(1 - x, y), (x, 1 - y), (1 - x, 1 - y)]

        def slot(a, px, py, pc):
            return outs[a].at[4 * px + 2 * py + pc]

        def copy(a, k, block, to, src=None):
            return pltpu.make_async_remote_copy(
                src_ref=slot(a, *block) if src is None else src, dst_ref=slot(a, *block), send_sem=send_sems.at[a, k],
                recv_sem=recv_sems.at[a, k], device_id=to, device_id_type=MESH)

        mine = [pltpu.make_async_copy(ins[a], slot(a, *me), local_sems.at[a]) for a in range(n)]
        first = []
        for a in range(n):
            first.append(copy(a, 0, me, sibling, src=ins[a]))
            first += [copy(a, 1 + j, me, (*chip, c), src=ins[a]) for j, chip in enumerate(chips)]
        return n, me, sibling, chips, c, copy, mine, first

    def start(self, ins, outs, sems):
        _, _, _, _, _, _, mine, first = self._copies(ins, outs, sems)
        for cp in mine + first:
            cp.start()

    def finish(self, ins, outs, sems):
        n, me, sibling, chips, c, copy, mine, first = self._copies(ins, outs, sems)
        passed = []
        for j, chip in enumerate(chips):
            for a in range(n):
                copy(a, 1 + j, (*chip, c), me).wait_recv()
                passed.append(copy(a, 4 + j, (*chip, c), sibling))
                passed[-1].start()
        for a in range(n):
            copy(a, 0, sibling, me).wait_recv()
            for j, chip in enumerate(chips):
                copy(a, 4 + j, (*chip, 1 - c), me).wait_recv()
        for cp in first + passed:
            cp.wait_send()
        for cp in mine:
            cp.wait()


class _ReduceRide:
    def __init__(self, arrays):
        n = len(arrays)
        self.arrays = list(arrays)
        self.out_shape = [jax.ShapeDtypeStruct(a.shape, a.dtype) for a in arrays]
        self.scratch = [pltpu.SemaphoreType.DMA((n, 7)), pltpu.SemaphoreType.DMA((n, 7)), pltpu.SemaphoreType.DMA((n,))]

    def _copies(self, ins, outs, sems):
        send_sems, recv_sems, local_sems = sems
        n = len(self.arrays)
        x, y, c = lax.axis_index("x"), lax.axis_index("y"), lax.axis_index("c")
        mine = [pltpu.make_async_copy(ins[a].at[4 * x + 2 * y + c], outs[a].at[0], local_sems.at[a]) for a in range(n)]
        copies = []
        for a in range(n):
            for k in range(1, N_DEV):
                px = 1 - x if k & 4 else x
                py = 1 - y if k & 2 else y
                pc = 1 - c if k & 1 else c
                copies.append(pltpu.make_async_remote_copy(
                    src_ref=ins[a].at[4 * px + 2 * py + pc], dst_ref=outs[a].at[k], send_sem=send_sems.at[a, k - 1],
                    recv_sem=recv_sems.at[a, k - 1], device_id=(px, py, pc), device_id_type=MESH))
        return mine, copies

    def start(self, ins, outs, sems):
        mine, copies = self._copies(ins, outs, sems)
        for cp in mine + copies:
            cp.start()

    def finish(self, ins, outs, sems):
        mine, copies = self._copies(ins, outs, sems)
        for cp in copies + mine:
            cp.wait()


def _run_ride(ride, name):
    n = len(ride.arrays)

    def body(*refs):
        ins, outs, sems = refs[:n], refs[n:2 * n], refs[2 * n:]
        ride.start(ins, outs, sems)
        ride.finish(ins, outs, sems)

    return pl.pallas_call(body, name=name, out_shape=ride.out_shape, in_specs=[ANY] * n, out_specs=[ANY] * n,
                          scratch_shapes=ride.scratch)(*ride.arrays)


def _all_gather(arrays, name):
    return _run_ride(_GatherRide(arrays), name)


def _swap_with_sibling(arrays, name):
    n = len(arrays)

    def body(*refs):
        ins, outs = refs[:n], refs[n:2 * n]
        send_sems, recv_sems = refs[2 * n:]
        x, y, c = lax.axis_index("x"), lax.axis_index("y"), lax.axis_index("c")
        copies = [pltpu.make_async_remote_copy(
            src_ref=ins[a].at[1 - c], dst_ref=outs[a], send_sem=send_sems.at[a], recv_sem=recv_sems.at[a],
            device_id=(x, y, 1 - c), device_id_type=MESH) for a in range(n)]
        for cp in copies:
            cp.start()
        for cp in copies:
            cp.wait()

    return pl.pallas_call(
        body, name=name, out_shape=[jax.ShapeDtypeStruct(a.shape[1:], a.dtype) for a in arrays],
        in_specs=[ANY] * n, out_specs=[ANY] * n,
        scratch_shapes=[pltpu.SemaphoreType.DMA((n,)), pltpu.SemaphoreType.DMA((n,))])(*arrays)


def _exchange_chips(arrays, name):
    n = len(arrays)

    def body(*refs):
        ins, outs = refs[:n], refs[n:2 * n]
        send_sems, recv_sems, local_sems = refs[2 * n:]
        x, y, c = lax.axis_index("x"), lax.axis_index("y"), lax.axis_index("c")
        partners = [(x, 1 - y), (1 - x, y), (1 - x, 1 - y)]
        mine = [pltpu.make_async_copy(ins[a].at[2 * x + y], outs[a].at[0], local_sems.at[a]) for a in range(n)]
        copies = [pltpu.make_async_remote_copy(
            src_ref=ins[a].at[2 * px + py], dst_ref=outs[a].at[1 + k], send_sem=send_sems.at[a, k],
            recv_sem=recv_sems.at[a, k], device_id=(px, py, c), device_id_type=MESH)
            for a in range(n) for k, (px, py) in enumerate(partners)]
        for cp in mine + copies:
            cp.start()
        for cp in copies + mine:
            cp.wait()

    return pl.pallas_call(
        body, name=name, out_shape=[jax.ShapeDtypeStruct(a.shape, a.dtype) for a in arrays],
        in_specs=[ANY] * n, out_specs=[ANY] * n,
        scratch_shapes=[pltpu.SemaphoreType.DMA((n, 3)), pltpu.SemaphoreType.DMA((n, 3)), pltpu.SemaphoreType.DMA((n,))],
    )(*arrays)


def _as_rows(a, lead):
    return a.reshape(a.shape[:lead] + (math.prod(a.shape[lead:-1]), a.shape[-1]))


def _add_pairs(a, b, name):
    a2, b2 = _as_rows(a, 0), _as_rows(b, 0)
    rows, cols = a2.shape
    tr = _row_tile(rows, cols * 4)

    def body(a_ref, b_ref, o_ref):
        o_ref[...] = (a_ref[...].astype(F32) + b_ref[...].astype(F32)).astype(o_ref.dtype)

    spec = _rows(tr, cols)
    out = pl.pallas_call(body, name=name, out_shape=jax.ShapeDtypeStruct(a2.shape, a.dtype), grid=(rows // tr,),
                         in_specs=[spec, spec], out_specs=spec, compiler_params=_params("parallel"))(a2, b2)
    return out.reshape(a.shape)


def _sum_blocks(a, name):
    a3 = _as_rows(a, 1)
    n, rows, cols = a3.shape
    tr = _row_tile(rows, n * cols * 4)

    def body(a_ref, o_ref):
        tot = a_ref[0].astype(F32)
        for k in range(1, n):
            tot = tot + a_ref[k].astype(F32)
        o_ref[...] = tot

    out = pl.pallas_call(body, name=name, out_shape=jax.ShapeDtypeStruct((rows, cols), F32), grid=(rows // tr,),
                         in_specs=[pl.BlockSpec((n, tr, cols), lambda j: (0, j, 0))], out_specs=_rows(tr, cols),
                         compiler_params=_params("parallel"))(a3)
    return out.reshape(a.shape[1:])


MIX_GROUPS = ("w_in", "w_uq", "w_uk", "w_uv", "w_attn_o", "w_conv_o", "w_pool_o", "w_mix_o")
FFN_GROUPS = ("w_gate", "w_up", "w_down")


def _pad_axis(a, axis, size):
    pad = [(0, 0)] * a.ndim
    pad[axis] = (0, size - a.shape[axis])
    return jnp.pad(a, pad)


def _local_groups(sh, l):
    out = {n: sh[n][l] for n in BIG}
    for n in ("w_uq", "w_uk", "w_uv"):
        out[n] = _pad_axis(out[n], -1, HEAD_PAD)
    for n in ("w_gate", "w_up"):
        out[n] = _pad_axis(out[n], -1, FF_SHARD_PAD)
    out["w_down"] = _pad_axis(out["w_down"], 0, FF_SHARD_PAD)
    return {n: v.astype(BF16) for n, v in out.items()}


def _arrange_w_in(blocks):
    parts, pos = [], 0
    for ref_lo, ref_hi, at in sorted(W_IN_PIECES, key=lambda p: p[2]):
        if at > pos:
            parts.append(jnp.zeros((blocks.shape[1], at - pos), blocks.dtype))
        for d in range(N_DEV):
            lo, hi = max(ref_lo, d * W_IN_SHARD), min(ref_hi, (d + 1) * W_IN_SHARD)
            if lo < hi:
                parts.append(blocks[d][:, lo - d * W_IN_SHARD:hi - d * W_IN_SHARD])
        pos = at + ref_hi - ref_lo
    if pos < Z_W:
        parts.append(jnp.zeros((blocks.shape[1], Z_W - pos), blocks.dtype))
    return jnp.concatenate(parts, axis=1)


def _w_in_shard(g, d):
    parts = []
    for ref_lo, ref_hi, at in W_IN_PIECES:
        lo, hi = max(ref_lo, d * W_IN_SHARD), min(ref_hi, (d + 1) * W_IN_SHARD)
        if lo < hi:
            parts.append(g[:, at + lo - ref_lo:at + hi - ref_lo])
    return jnp.concatenate(parts, axis=1)


def _mixer_weights(gat):
    w = dict(gat)
    w["w_in"] = _arrange_w_in(gat["w_in"])
    attn_o = gat["w_attn_o"].reshape(N_DEV, N_HEADS, V_HEAD, LANES)
    w["w_attn_o"] = _pad_axis(attn_o, 2, HEAD_PAD).reshape(N_DEV, N_HEADS * HEAD_PAD, LANES)
    w["w_mix_o"] = gat["w_mix_o"].reshape(D_MODEL, D_MODEL)
    return w


def _ffn_weights(gat):
    return {"w_gate": gat["w_gate"], "w_up": gat["w_up"], "w_down": gat["w_down"].reshape(D_FF_PAD, D_MODEL)}


def _mixer_grad_groups(gb):
    g = dict(gb)
    g["w_in"] = jnp.stack([_w_in_shard(gb["w_in"], d) for d in range(N_DEV)])
    attn_o = gb["w_attn_o"].reshape(N_DEV, N_HEADS, HEAD_PAD, LANES)[:, :, :V_HEAD]
    g["w_attn_o"] = attn_o.reshape(N_DEV, N_HEADS * V_HEAD, LANES)
    g["w_mix_o"] = gb["w_mix_o"].reshape(N_DEV, D_MODEL // N_DEV, D_MODEL)
    return g


def _ffn_grad_groups(gb):
    return {"w_gate": gb["w_gate"], "w_up": gb["w_up"], "w_down": gb["w_down"].reshape(N_DEV, FF_SHARD_PAD, D_MODEL)}


def _grads_from_groups(tot):
    g = dict(tot)
    g["w_uq"] = tot["w_uq"][:, :QK_NOPE + QK_ROPE]
    g["w_uk"], g["w_uv"] = tot["w_uk"][:, :QK_NOPE], tot["w_uv"][:, :V_HEAD]
    g["w_gate"], g["w_up"] = tot["w_gate"][:, :FF_SHARD], tot["w_up"][:, :FF_SHARD]
    g["w_down"] = tot["w_down"][:FF_SHARD]
    return g


SMALL_GROUPS = (
    (D_MODEL, ("mix_norm_pre", "mix_norm_post", "ffn_norm_pre", "ffn_norm_post")),
    (CONV_C, ("conv_w", "conv_b", "conv_ln_g", "conv_ln_b", "pool_scale")),
    (Q_RANK, ("q_norm",)), (KV_RANK, ("kv_norm",)), (POOL_GD, ("pool_w",)),
)


def _small_rows(name):
    return {"conv_w": CONV_HALO, "pool_w": POOL_G * POOL_GD}.get(name, SUBLANES)


def _small_groups(small):
    out = []
    for width, names in SMALL_GROUPS:
        parts = []
        for l in range(DEPTH):
            for n in names:
                part = small[l][n].reshape(-1, width)
                parts.append(_pad_axis(part, 0, _small_rows(n)))
        out.append(jnp.concatenate(parts, axis=0))
    return out


def _small_from_groups(groups):
    shapes = {"conv_w": (CONV_W, CONV_C), "pool_w": (POOL_G, POOL_GD, POOL_GD)}
    out = {}
    for (width, names), g in zip(SMALL_GROUPS, groups):
        row = 0
        for l in range(DEPTH):
            for n in names:
                rows = _small_rows(n)
                real = {"conv_w": CONV_W, "pool_w": POOL_G * POOL_GD}.get(n, 1)
                out.setdefault(n, []).append(g[row:row + real].reshape(shapes.get(n, (width,))))
                row += rows
    return {n: jnp.stack(v) for n, v in out.items()}


def _mixer_fwd(x, tables, w, sm, tag, ride):
    nm = lambda n: f"{n}_{tag}"
    h = _rms_fwd(x, (D_MODEL, 0), sm["mix_norm_pre"], BF16, nm("mix_pre_norm"))
    z = _matmul(h, w["w_in"], "nn", F32, nm("in_proj"))
    cq = _rms_fwd(z, ZC_Q, sm["q_norm"], BF16, nm("q_norm"))
    ckv = _rms_fwd(z, ZC_KV, sm["kv_norm"], BF16, nm("kv_norm"))
    qf = _matmul(cq, w["w_uq"], "nn", F32, nm("q_up"))
    kf = _matmul(ckv, w["w_uk"], "nn", F32, nm("k_up"))
    v = _matmul(ckv, w["w_uv"], "nn", BF16, nm("v_up"))
    q, k = _rope_qk_fwd(qf, kf, z, tables, nm("rope_qk"))
    (o, lse), rode = _flash_fwd(q, k, v, nm("flash_fwd"), ride)
    y_attn = _matmul(o, w["w_attn_o"], "nn", F32, nm("attn_out"))
    hc, co = _conv_fwd(z, sm["conv_w"], sm["conv_b"], sm["conv_ln_g"], sm["conv_ln_b"], nm("conv_fwd"))
    y_conv = _matmul(hc, w["w_conv_o"], "nn", F32, nm("conv_out"))
    pm = _pool_fwd(z, sm["pool_w"], sm["pool_scale"], nm("pool_fwd"))
    y_pool = _matmul(pm, w["w_pool_o"], "nn", F32, nm("pool_out"))
    ys = (y_attn, y_conv, y_pool)
    merged = _merge_fwd(z, ys, nm("merge_fwd"))
    mo = _matmul(merged, w["w_mix_o"], "nn", F32, nm("mix_out"))
    x_mid = _rms_fwd(mo, (D_MODEL, 0), sm["mix_norm_post"], F32, nm("mix_post_norm"), res=x)
    saved = dict(x=x, h=h, z=z, cq=cq, ckv=ckv, q=q, k=k, v=v, o=o, lse=lse, hc=hc, co=co, pm=pm, ys=ys, merged=merged,
                 mo=mo)
    return x_mid, saved, rode


def _ffn_fwd(x_mid, w, sm, tag):
    nm = lambda n: f"{n}_{tag}"
    h2 = _rms_fwd(x_mid, (D_MODEL, 0), sm["ffn_norm_pre"], BF16, nm("ffn_pre_norm"))
    hg = _matmul(h2, w["w_gate"], "nn", F32, nm("ffn_gate"))
    hu = _matmul(h2, w["w_up"], "nn", F32, nm("ffn_up"))
    act = _swiglu_fwd(hg, hu, nm("swiglu_fwd"))
    fo = _matmul(act, w["w_down"], "nn", F32, nm("ffn_down"))
    out = _rms_fwd(fo, (D_MODEL, 0), sm["ffn_norm_post"], F32, nm("ffn_post_norm"), res=x_mid)
    saved = dict(x_mid=x_mid, h2=h2, hg=hg, hu=hu, act=act, fo=fo)
    return out, saved


def _ffn_bwd(dout, sv, w, sm, tag):
    nm = lambda n: f"{n}_{tag}"
    gb, gs = {}, {}
    dfo, gs["ffn_norm_post"] = _rms_bwd(sv["fo"], (D_MODEL, 0), sm["ffn_norm_post"], dout, BF16, nm("ffn_post_norm_bwd"))
    dact = _matmul(dfo, w["w_down"], "nt", F32, nm("ffn_down_dx"))
    gb["w_down"] = _matmul(sv["act"], dfo, "tn", BF16, nm("ffn_down_dw"))
    dhg, dhu = _swiglu_bwd(sv["hg"], sv["hu"], dact, nm("swiglu_bwd"))
    dh2_g = _matmul(dhg, w["w_gate"], "nt", F32, nm("ffn_gate_dx"))
    dh2 = _matmul(dhu, w["w_up"], "nt", F32, nm("ffn_up_dx"), add=dh2_g)
    gb["w_gate"] = _matmul(sv["h2"], dhg, "tn", BF16, nm("ffn_gate_dw"), blocked=True)
    gb["w_up"] = _matmul(sv["h2"], dhu, "tn", BF16, nm("ffn_up_dw"), blocked=True)
    dmid, gs["ffn_norm_pre"] = _rms_bwd(sv["x_mid"], (D_MODEL, 0), sm["ffn_norm_pre"], dh2, F32, nm("ffn_pre_norm_bwd"),
                                        add=dout)
    return dmid, gb, gs


def _mixer_bwd(dmid, sv, tables, w, sm, tag, make_ride):
    nm = lambda n: f"{n}_{tag}"
    gb, gs = {}, {}
    dmo, gs["mix_norm_post"] = _rms_bwd(sv["mo"], (D_MODEL, 0), sm["mix_norm_post"], dmid, BF16, nm("mix_post_norm_bwd"))
    dmerged = _matmul(dmo, w["w_mix_o"], "nt", F32, nm("mix_out_dx"))
    gb["w_mix_o"] = _matmul(sv["merged"], dmo, "tn", BF16, nm("mix_out_dw"))
    dya, dyc, dyp, dz = _merge_bwd(sv["z"], sv["ys"], dmerged, nm("merge_bwd"))
    dpm = _matmul(dyp, w["w_pool_o"], "nt", F32, nm("pool_out_dx"))
    gb["w_pool_o"] = _matmul(sv["pm"], dyp, "tn", BF16, nm("pool_out_dw"), blocked=True)
    dz, gs["pool_w"], gs["pool_scale"] = _pool_bwd(dpm, sv["z"], sm["pool_w"], sm["pool_scale"], dz, nm("pool_bwd"))
    dhc = _matmul(dyc, w["w_conv_o"], "nt", F32, nm("conv_out_dx"))
    gb["w_conv_o"] = _matmul(sv["hc"], dyc, "tn", BF16, nm("conv_out_dw"), blocked=True)
    dco, gs["conv_ln_g"], gs["conv_ln_b"], gs["conv_b"] = _conv_bwd_norm(dhc, sv["co"], sm["conv_ln_g"], sm["conv_ln_b"],
                                                                        nm("conv_bwd_norm"))
    dz, gs["conv_w"] = _conv_bwd_taps(dco, sv["z"], sm["conv_w"], dz, nm("conv_bwd_taps"))
    do = _matmul(dya, w["w_attn_o"], "nt", F32, nm("attn_out_dx"))
    gb["w_attn_o"] = _matmul(sv["o"], dya, "tn", BF16, nm("attn_out_dw"), blocked=True)
    delta, dob = _attn_delta(do, sv["o"], nm("attn_delta"))
    (dq, dk, dv), rode = _flash_bwd(sv["q"], sv["k"], sv["v"], dob, sv["lse"], delta, nm("flash_bwd"), make_ride(gb))
    dqf, dkf, dz = _rope_qk_bwd(dq, dk, tables, dz, nm("rope_qk_bwd"))
    dcq_n = _matmul(dqf, w["w_uq"], "nt", F32, nm("q_up_dx"))
    gb["w_uq"] = _matmul(sv["cq"], dqf, "tn", BF16, nm("q_up_dw"), blocked=True)
    dckv_k = _matmul(dkf, w["w_uk"], "nt", F32, nm("k_up_dx"))
    dckv_n = _matmul(dv, w["w_uv"], "nt", F32, nm("v_up_dx"), add=dckv_k)
    gb["w_uk"] = _matmul(sv["ckv"], dkf, "tn", BF16, nm("k_up_dw"), blocked=True)
    gb["w_uv"] = _matmul(sv["ckv"], dv, "tn", BF16, nm("v_up_dw"), blocked=True)
    dz, gs["q_norm"] = _rms_bwd(sv["z"], ZC_Q, sm["q_norm"], dcq_n, BF16, nm("q_norm_bwd"), dz=dz)
    dz, gs["kv_norm"] = _rms_bwd(sv["z"], ZC_KV, sm["kv_norm"], dckv_n, BF16, nm("kv_norm_bwd"), dz=dz)
    dh =_matmul(dz, w["w_in"], "nt", F32, nm("in_proj_dx"))
    gb["w_in"] = _matmul(sv["h"], dz, "tn", BF16, nm("in_proj_dw"))
    dx, gs["mix_norm_pre"] = _rms_bwd(sv["x"], (D_MODEL, 0), sm["mix_norm_pre"], dh, F32, nm("mix_pre_norm_bwd"), add=dmid)
    return dx, gb, gs, rode


def _part_groups(part):
    return MIX_GROUPS if part == "mix" else FFN_GROUPS


class _Plan:
    def __init__(self, shards, conv_w):
        self.local = [_local_groups(shards, l) for l in range(DEPTH)]
        self.conv_w = conv_w
        self.gat, self.send, self.recv = {}, {}, {}

    @staticmethod
    def _riders(l):
        return [(l, "ffn")] + ([(l + 1, "mix")] if l + 1 < DEPTH else [])

    def gather_first(self):
        out = _all_gather([self.local[0][g] for g in MIX_GROUPS] + [self.conv_w], "gather_mixer_l0")
        self.gat[(0, "mix")] = dict(zip(MIX_GROUPS, out[:-1]))
        return out[-1]

    def fwd_ride(self, l):
        return _GatherRide([self.local[ll][g] for ll, part in self._riders(l) for g in _part_groups(part)])

    def fwd_done(self, l, outs):
        outs = list(outs)
        for ll, part in self._riders(l):
            self.gat[(ll, part)] = {g: outs.pop(0) for g in _part_groups(part)}

    def mixer_weights(self, l):
        return _mixer_weights(self.gat[(l, "mix")])

    def ffn_weights(self, l):
        return _ffn_weights(self.gat[(l, "ffn")])

    def add_grads(self, l, part, gb):
        self.send[(l, part)] = _mixer_grad_groups(gb) if part == "mix" else _ffn_grad_groups(gb)

    def bwd_ride(self, l):
        return _ReduceRide([self.send[(ll, part)][g] for ll, part in self._riders(l) for g in _part_groups(part)])

    def bwd_done(self, l, outs):
        outs = list(outs)
        for ll, part in self._riders(l):
            self.recv[(ll, part)] = {g: outs.pop(0) for g in _part_groups(part)}

    def finish(self):
        send = [self.send[(0, "mix")][g] for g in MIX_GROUPS]
        by_core = [a.reshape((4, 2) + a.shape[1:]).transpose((1, 0) + tuple(range(2, a.ndim + 1))) for a in send]
        core = lax.axis_index("c")
        own = [lax.dynamic_index_in_dim(a, core, axis=0, keepdims=False) for a in by_core]
        got = _swap_with_sibling(by_core, "reduce_d2d")
        pairs = [_add_pairs(a, b, f"reduce_pair_add_{g}") for g, a, b in zip(MIX_GROUPS, own, got)]
        self.recv[(0, "mix")] = dict(zip(MIX_GROUPS, _exchange_chips(pairs, "reduce_ici")))
        layers = []
        for l in range(DEPTH):
            tot = {g: _sum_blocks(a, f"reduce_sum_{g}_l{l}") for part in ("mix", "ffn")
                   for g, a in self.recv[(l, part)].items()}
            layers.append(_grads_from_groups(tot))
        return layers


def _local_step(x, positions, target, smalls, plan):
    tables = _rope_tables(positions)
    saved = []
    h = x
    for l in range(DEPTH):
        wm = plan.mixer_weights(l)
        h, svm, rode = _mixer_fwd(h, tables, wm, smalls[l], f"l{l}", plan.fwd_ride(l))
        plan.fwd_done(l, rode)
        wf = plan.ffn_weights(l)
        h, svf = _ffn_fwd(h, wf, smalls[l], f"l{l}")
        saved.append((svm, svf, wm, wf))
    dy, sq = _loss_grad(h, target, "loss_grad")
    small = [None] * DEPTH
    for l in reversed(range(DEPTH)):
        svm, svf, wm, wf = saved[l]
        dmid, gbf, gsf = _ffn_bwd(dy, svf, wf, smalls[l], f"l{l}")
        plan.add_grads(l, "ffn", gbf)
        dy, gbm, gsm, rode = _mixer_bwd(dmid, svm, tables, wm, smalls[l], f"l{l}", lambda gb, l=l: plan.bwd_ride(l))
        plan.bwd_done(l, rode)
        plan.add_grads(l, "mix", gbm)
        small[l] = {**gsf, **gsm}
    return sq, dy, small


def kernel(x, positions, mix_norm_pre, w_in, q_norm, w_uq, kv_norm, w_uk, w_uv, w_attn_o, conv_w, conv_b, conv_ln_g, conv_ln_b, w_conv_o, pool_w, pool_scale, w_pool_o, w_mix_o, mix_norm_post, ffn_norm_pre, w_gate, w_up, w_down, ffn_norm_post, loss_target, m_mix_norm_pre, m_w_in, m_q_norm, m_w_uq, m_kv_norm, m_w_uk, m_w_uv, m_w_attn_o, m_conv_w, m_conv_b, m_conv_ln_g, m_conv_ln_b, m_w_conv_o, m_pool_w, m_pool_scale, m_w_pool_o, m_w_mix_o, m_mix_norm_post, m_ffn_norm_pre, m_w_gate, m_w_up, m_w_down, m_ffn_norm_post, v_mix_norm_pre, v_w_in, v_q_norm, v_w_uq, v_kv_norm, v_w_uk, v_w_uv, v_w_attn_o, v_conv_w, v_conv_b, v_conv_ln_g, v_conv_ln_b, v_w_conv_o, v_pool_w, v_pool_scale, v_w_pool_o, v_w_mix_o, v_mix_norm_post, v_ffn_norm_pre, v_w_gate, v_w_up, v_w_down, v_ffn_norm_post):
    given = dict(locals())
    dev = 4 * lax.axis_index("x") + 2 * lax.axis_index("y") + lax.axis_index("c")

    plan = _Plan({n: given[n] for n in BIG}, conv_w)
    cw = CONV_C // N_DEV
    conv_w_full = plan.gather_first().transpose(1, 2, 0, 3).reshape(DEPTH, CONV_W, CONV_C)
    smalls = []
    for l in range(DEPTH):
        sm = {n: given[n][l] for n in SMALL if n != "conv_w"}
        sm["conv_w"] = _pad_axis(conv_w_full[l], 0, CONV_HALO)
        smalls.append(sm)

    sq, grad_x, small = _local_step(x[0], positions[0], loss_target[0], smalls, plan)
    loss = lax.psum(0.5 / D_MODEL * jnp.sum(sq), ("x", "y", "c"))
    per_layer = plan.finish()
    views = {}
    for n in BIG:
        if n == "w_in":
            views[n] = jnp.stack([per_layer[l][n].T for l in range(DEPTH)], axis=1)
        elif n in LANE_MAJOR:
            views[n] = jnp.stack([per_layer[l][n].T for l in range(DEPTH)])
        else:
            views[n] = jnp.stack([per_layer[l][n] for l in range(DEPTH)])
    grads = {n: _from_lane_major(n, views[n]) for n in BIG}

    small_groups = _all_gather(_small_groups(small), "gather_small_grads")
    small_sum = _small_from_groups([_sum_blocks(g, f"sum_small_grads_{i}") for i, g in enumerate(small_groups)])
    for n in SMALL:
        grads[n] = small_sum[n]
    grads["conv_w"] = lax.dynamic_slice_in_dim(small_sum["conv_w"], dev * cw, cw, axis=2)

    delta, new_m, new_v = {}, {}, {}
    for n in WEIGHTS:
        g_view = views[n] if n in views else grads[n]
        w_view, m_view, v_view = [_lane_major(n, given[k]) for k in (n, "m_" + n, "v_" + n)]
        res = _adamw(w_view, g_view, m_view, v_view, f"adamw_{n}")
        delta[n], new_m[n], new_v[n] = [_from_lane_major(n, r) for r in res]
    return (loss, grad_x[None], *[grads[n] for n in WEIGHTS], *[delta[n] for n in WEIGHTS],
            *[new_m[n] for n in WEIGHTS], *[new_v[n] for n in WEIGHTS])
```

```python
import functools
import math

import jax
import jax.numpy as jnp
from jax import lax
from jax.experimental import pallas as pl
from jax.experimental.pallas import tpu as pltpu

F32, BF16 = jnp.float32, jnp.bfloat16
MESH = pl.DeviceIdType.MESH

LANES = 128
SUBLANES = 8
VMEM_LIMIT_BYTES = 56 * 1024 * 1024
MATMUL_VMEM_BYTES = 40 * 1024 * 1024

N_DEV = 8
D_MODEL = 1024
DEPTH = 2
N_HEADS = 8
QK_NOPE, QK_ROPE, V_HEAD = 64, 32, 64
HEAD_PAD = LANES
Q_RANK, KV_RANK = 384, 256
ROPE_THETA = 10000.0
CONV_C, CONV_W = 512, 31
CONV_HALO = 32
POOL_WINDOWS = (2, 4, 8, 16)
POOL_C, POOL_G = 512, 4
POOL_GD = POOL_C // POOL_G
D_FF = 2816
FF_SHARD = D_FF // N_DEV
FF_SHARD_PAD = 3 * LANES
D_FF_PAD = N_DEV * FF_SHARD_PAD
W_IN_SHARD = 660
EPS = 1e-6
ATTN_SCALE = 1.0 / math.sqrt(QK_NOPE + QK_ROPE)
LOG2E = 1.4426950408889634
LR, B1, B2, ADAM_EPS, WD, STEP = 0.001, 0.9, 0.999, 1e-08, 0.01, 10

Z_W = 5376
ZC_GATE = (1024, 0)
ZC_GATES = (3072, 0)
ZC_CONV_A = (512, 6)
ZC_CONV_G = (512, 7)
ZC_CONV = (1024, 3)
ZC_POOL = (512, 8)
ZC_Q = (384, 12)
ZC_KR = (128, 39)
ZC_KV = (256, 20)
W_IN_PIECES = ((0, 384, 4608), (384, 640, 5120), (640, 672, 5056), (672, 1696, 3072), (1696, 2208, 4096),
               (2208, 5280, 0))

BIG = ("w_in", "w_uq", "w_uk", "w_uv", "w_attn_o", "w_conv_o", "w_pool_o", "w_mix_o", "w_gate", "w_up", "w_down")
SMALL = ("mix_norm_pre", "q_norm", "kv_norm", "conv_w", "conv_b", "conv_ln_g", "conv_ln_b", "pool_w", "pool_scale",
         "mix_norm_post", "ffn_norm_pre", "ffn_norm_post")
WEIGHTS = ("mix_norm_pre", "w_in", "q_norm", "w_uq", "kv_norm", "w_uk", "w_uv", "w_attn_o", "conv_w", "conv_b",
           "conv_ln_g", "conv_ln_b", "w_conv_o", "pool_w", "pool_scale", "w_pool_o", "w_mix_o", "mix_norm_post",
           "ffn_norm_pre", "w_gate", "w_up", "w_down", "ffn_norm_post")


def _params(*semantics):
    return pltpu.CompilerParams(dimension_semantics=semantics, vmem_limit_bytes=VMEM_LIMIT_BYTES)


def _tile(dim, cap):
    if dim <= cap:
        return dim
    for t in range(cap - cap % LANES, 0, -LANES):
        if dim % t == 0:
            return t
    raise ValueError(f"no tile for {dim} under {cap}")


def _row_tile(rows, row_bytes, budget=1 << 20):
    if rows * row_bytes <= budget:
        return rows
    cap = max(16, budget // row_bytes)
    for t in range(cap - cap % 16, 0, -16):
        if rows % t == 0:
            return t
    return rows


def _rows(ts, width, cidx=0):
    return pl.BlockSpec((ts, width), lambda i: (i, cidx))


def _fixed(shape):
    return pl.BlockSpec(shape, lambda *_: (0,) * len(shape))


def _sigmoid(x):
    return 1.0 / (1.0 + jnp.exp(-x))


def _matmul(a, b, mode, out_dtype, name, add=None, blocked=False):
    nb = n_blk = 0
    blocked = blocked or b.ndim == 3
    if mode == "nn":
        (m, k) = a.shape
        n = b.shape[0] * b.shape[2] if blocked else b.shape[1]
    elif mode == "nt":
        (m, k) = a.shape
        n = b.shape[1] if blocked else b.shape[0]
    else:
        (k, m), n = a.shape, b.shape[1]
    if blocked:
        nb = b.shape[2] if mode != "tn" else n // N_DEV
    unit = nb if blocked and mode != "nt" else LANES
    out_bytes = jnp.dtype(out_dtype).itemsize + (4 if add is not None else 0)
    best = None
    for tn_c in range(unit, min(n, 1536) + 1, unit):
        for tm_c in sorted({256, 512, 1024, 2048, min(m, 2048)}):
            if n % tn_c or m % tm_c or (blocked and mode != "nt" and N_DEV % (tn_c // nb)):
                continue
            vmem = 2 * (tm_c * k * 2 + tn_c * k * 2 + tm_c * tn_c * out_bytes) + tm_c * tn_c * 4 + tn_c * k * 2
            if vmem <= MATMUL_VMEM_BYTES and (best is None or tm_c * tn_c / (tm_c + tn_c) > best[0]):
                best = (tm_c * tn_c / (tm_c + tn_c), tm_c, tn_c)
    if best is None:
        raise ValueError(f"{name}: no tiles for {m}x{n}x{k}")
    _, tm, tn = best
    if blocked:
        n_blk = N_DEV if mode == "nt" else tn // nb
    dims = {"nn": ((1,), (0,)), "nt": ((1,), (1,)), "tn": ((0,), (0,))}[mode]
    a_spec = pl.BlockSpec((k, tm), lambda i, j: (0, i)) if mode == "tn" else pl.BlockSpec((tm, k), lambda i, j: (i, 0))
    b_spec = pl.BlockSpec((tn, k), lambda i, j: (j, 0)) if mode == "nt" else pl.BlockSpec((k, tn), lambda i, j: (0, j))
    o_spec = pl.BlockSpec((tm, tn), lambda i, j: (i, j))
    out_shape = jax.ShapeDtypeStruct((m, n), out_dtype)
    if blocked and mode == "nn":
        b_spec = pl.BlockSpec((n_blk, k, nb), lambda i, j: (j, 0, 0))
    elif blocked and mode == "nt":
        b_spec = pl.BlockSpec((n_blk, tn, nb), lambda i, j: (0, j, 0))
    elif blocked:
        o_spec = pl.BlockSpec((n_blk, tm, nb), lambda i, j: (j, i, 0))
        out_shape = jax.ShapeDtypeStruct((N_DEV, m, nb), out_dtype)
    has_add = add is not None

    def body(a_ref, b_ref, *rest):
        o_ref = rest[-1]
        if blocked and mode != "tn":
            bv = jnp.concatenate([b_ref[c] for c in range(n_blk)], axis=1) if n_blk > 1 else b_ref[0]
        else:
            bv = b_ref[...]
        total = lax.dot_general(a_ref[...], bv, (dims, ((), ())), preferred_element_type=F32)
        if has_add:
            total = total + rest[0][...]
        if blocked and mode == "tn":
            for c in range(n_blk):
                o_ref[c] = total[:, c * nb:(c + 1) * nb].astype(o_ref.dtype)
        else:
            o_ref[...] = total.astype(o_ref.dtype)

    operands = (a, b, add) if has_add else (a, b)
    return pl.pallas_call(
        body, name=name, out_shape=out_shape, grid=(m // tm, n // tn),
        in_specs=[a_spec, b_spec] + ([o_spec] if has_add else []), out_specs=o_spec,
        compiler_params=_params("parallel", "parallel"))(*operands)


def _rms_fwd(x, win, gain, out_dtype, name, res=None):
    width, cidx = win
    s = x.shape[0]
    ts = min(s, 512)
    has_res = res is not None

    def body(x_ref, g_ref, *rest):
        o_ref = rest[-1]
        xv = x_ref[...]
        r = lax.rsqrt(jnp.mean(xv * xv, axis=-1, keepdims=True) + EPS)
        y = (xv * r) * g_ref[...]
        if has_res:
            y = rest[0][...] + y
        o_ref[...] = y.astype(o_ref.dtype)

    ops = (x, gain.reshape(1, width)) + ((res,) if has_res else ())
    return pl.pallas_call(
        body, name=name, out_shape=jax.ShapeDtypeStruct((s, width), out_dtype), grid=(s // ts,),
        in_specs=[_rows(ts, width, cidx), _fixed((1, width))] + ([_rows(ts, width)] if has_res else []),
        out_specs=_rows(ts, width), compiler_params=_params("parallel"))(*ops)


def _into(dz, n_inputs, out_index):
    return dict(in_specs=[ANY], operands=(dz,), input_output_aliases={n_inputs: out_index},
                out_shape=jax.ShapeDtypeStruct(dz.shape, dz.dtype))


def _rms_bwd(x, win, gain, dy, out_dtype, name, add=None, dz=None):
    width, cidx = win
    s = x.shape[0]
    ts = min(s, 512)
    has_add = add is not None

    def body(x_ref, g_ref, dy_ref, *rest):
        dx_ref, dg_ref = rest[-2], rest[-1]
        xv = x_ref[...]
        r = lax.rsqrt(jnp.mean(xv * xv, axis=-1, keepdims=True) + EPS)
        xh = xv * r
        dyv = dy_ref[...].astype(F32)
        dyg = dyv * g_ref[...]
        dx = r * (dyg - xh * jnp.mean(dyg * xh, axis=-1, keepdims=True))
        if has_add:
            dx = dx + rest[0][...]
        dx_ref[...] = dx.astype(dx_ref.dtype)

        @pl.when(pl.program_id(0) == 0)
        def _():
            dg_ref[...] = jnp.zeros_like(dg_ref)

        dg_ref[...] += jnp.sum(dyv * xh, axis=0, keepdims=True)

    ops = (x, gain.reshape(1, width), dy) + ((add,) if has_add else ())
    in_specs = [_rows(ts, width, cidx), _fixed((1, width)), _rows(ts, width)] + ([_rows(ts, width)] if has_add else [])
    dx_shape, dx_spec, alias = jax.ShapeDtypeStruct((s, width), out_dtype), _rows(ts, width), {}
    if dz is not None:
        into = _into(dz, len(ops), 0)
        ops, in_specs, alias = ops + into["operands"], in_specs + into["in_specs"], into["input_output_aliases"]
        dx_shape, dx_spec = into["out_shape"], _rows(ts, width, cidx)
    dx, dg = pl.pallas_call(
        body, name=name, out_shape=(dx_shape, jax.ShapeDtypeStruct((1, width), F32)), grid=(s // ts,),
        in_specs=in_specs, out_specs=(dx_spec, _fixed((1, width))), input_output_aliases=alias,
        compiler_params=_params("arbitrary"))(*ops)
    return dx, dg.reshape(width)


def _rope(x, c, s1, s2):
    return x * c + pltpu.roll(x, 16, 1) * s1 + pltpu.roll(x, LANES - 16, 1) * s2


def _rope_t(g, c, s1, s2):
    return g * c + pltpu.roll(g * s1, LANES - 16, 1) + pltpu.roll(g * s2, 16, 1)


def _rope_tables(positions):
    inv_freq = ROPE_THETA ** (-jnp.arange(0, QK_ROPE, 2, dtype=F32) / QK_ROPE)
    ang = positions.astype(F32)[:, None] * inv_freq
    cos, sin = jnp.cos(ang), jnp.sin(ang)
    n = positions.shape[0]
    one, zero = jnp.ones((n, 1), F32), jnp.zeros((n, 1), F32)
    c = jnp.concatenate([jnp.tile(one, (1, QK_NOPE)), cos, cos, jnp.tile(one, (1, 32))], axis=1)
    s1 = jnp.concatenate([jnp.tile(zero, (1, QK_NOPE + 16)), sin, jnp.tile(zero, (1, 32))], axis=1)
    s2 = jnp.concatenate([jnp.tile(zero, (1, QK_NOPE)), -sin, jnp.tile(zero, (1, 48))], axis=1)
    return c, s1, s2


def _rope_qk_fwd(qf, kf, z, tables, name):
    s = qf.shape[0]
    ts = min(s, 256)
    hw = N_HEADS * HEAD_PAD

    def body(qf_ref, kf_ref, kr_ref, c_ref, s1_ref, s2_ref, q_ref, k_ref):
        c, s1, s2 = c_ref[...], s1_ref[...], s2_ref[...]
        kr = _rope(kr_ref[...], c, s1, s2)
        for h in range(N_HEADS):
            sl = slice(h * HEAD_PAD, (h + 1) * HEAD_PAD)
            q_ref[:, sl] = _rope(qf_ref[:, sl], c, s1, s2).astype(BF16)
            k_ref[:, sl] = (kf_ref[:, sl] + kr).astype(BF16)

    tab = _rows(ts, LANES)
    return pl.pallas_call(
        body, name=name, out_shape=(jax.ShapeDtypeStruct((s, hw), BF16),) * 2, grid=(s // ts,),
        in_specs=[_rows(ts, hw), _rows(ts, hw), _rows(ts, *ZC_KR), tab, tab, tab],
        out_specs=(_rows(ts, hw), _rows(ts, hw)), compiler_params=_params("parallel"))(qf, kf, z, *tables)


def _rope_qk_bwd(dq, dk, tables, dz, name):
    s = dq.shape[0]
    ts = min(s, 256)
    hw = N_HEADS * HEAD_PAD

    def body(dq_ref, dk_ref, c_ref, s1_ref, s2_ref, _, dqf_ref, dkf_ref, dkr_ref):
        c, s1, s2 = c_ref[...], s1_ref[...], s2_ref[...]
        ksum = jnp.zeros((ts, HEAD_PAD), F32)
        for h in range(N_HEADS):
            sl = slice(h * HEAD_PAD, (h + 1) * HEAD_PAD)
            dqf_ref[:, sl] = _rope_t(dq_ref[:, sl], c, s1, s2).astype(BF16)
            dkh = dk_ref[:, sl]
            dkf_ref[:, sl] = dkh.astype(BF16)
            ksum = ksum + dkh
        lane = lax.broadcasted_iota(jnp.int32, (ts, HEAD_PAD), 1)
        in_rope = (lane >= QK_NOPE) & (lane < QK_NOPE + QK_ROPE)
        dkr_ref[...] = jnp.where(in_rope, _rope_t(ksum, c, s1, s2), 0.0).astype(BF16)

    tab = _rows(ts, LANES)
    into = _into(dz, 5, 2)
    return pl.pallas_call(
        body, name=name,
        out_shape=(jax.ShapeDtypeStruct((s, hw), BF16), jax.ShapeDtypeStruct((s, hw), BF16), into["out_shape"]),
        grid=(s // ts,), in_specs=[_rows(ts, hw), _rows(ts, hw), tab, tab, tab] + into["in_specs"],
        out_specs=(_rows(ts, hw), _rows(ts, hw), _rows(ts, *ZC_KR)), input_output_aliases=into["input_output_aliases"],
        compiler_params=_params("parallel"))(dq, dk, *tables, dz)


def _attn_tile(s):
    return min(s, 512)


def _raw_scores(q, k, masked, row0=0):
    sc = lax.dot_general(q, k, (((1,), (1,)), ((), ())), preferred_element_type=F32)
    if masked:
        rows = row0 + lax.broadcasted_iota(jnp.int32, sc.shape, 0)
        cols = lax.broadcasted_iota(jnp.int32, sc.shape, 1)
        sc = jnp.where(cols <= rows, sc, -jnp.inf)
    return sc


def _ride_hooks(ride, refs, n_in, n_out, grid):
    if ride is None:
        return refs, lambda: None, lambda: None
    n = len(ride.arrays)
    own = refs[:n_in] + refs[n_in + n:n_in + n + n_out]
    ins, outs, sems = refs[n_in:n_in + n], refs[n_in + n + n_out:n_in + 2 * n + n_out], refs[n_in + 2 * n + n_out:]
    at_first = functools.reduce(lambda a, b: a & b, [pl.program_id(ax) == 0 for ax in range(len(grid))])
    at_last = functools.reduce(lambda a, b: a & b, [pl.program_id(ax) == g - 1 for ax, g in enumerate(grid)])
    return own, lambda: pl.when(at_first)(lambda: ride.start(ins, outs, sems)), \
        lambda: pl.when(at_last)(lambda: ride.finish(ins, outs, sems))


def _ride_call(ride, body, name, out_shape, grid, in_specs, out_specs, semantics, operands):
    n = 0 if ride is None else len(ride.arrays)
    res = pl.pallas_call(
        body, name=name, out_shape=tuple(out_shape) + (tuple(ride.out_shape) if n else ()), grid=grid,
        in_specs=list(in_specs) + [ANY] * n, out_specs=tuple(out_specs) + (ANY,) * n,
        scratch_shapes=list(ride.scratch) if n else [],
        compiler_params=_params(*(("arbitrary",) * len(grid) if n else semantics)))(*operands, *(ride.arrays if n else ()))
    return res[:len(out_shape)], list(res[len(out_shape):])


def _flash_fwd(q, k, v, name, ride=None):
    s = q.shape[0]
    t = _attn_tile(s)
    c2 = ATTN_SCALE * LOG2E
    grid = (N_HEADS, s // t)

    def body(*refs):
        (q_ref, k_ref, v_ref, o_ref, lse_ref), start, finish = _ride_hooks(ride, refs, 3, 2, grid)
        start()
        i = pl.program_id(1)
        qv = q_ref[...]

        def chunk(j, carry, masked):
            m_old, l_old, acc = carry
            at = pl.ds(pl.multiple_of(j * t, t), t)
            sc = _raw_scores(qv, k_ref[at, :], masked)
            m_new = jnp.maximum(m_old, jnp.max(sc, axis=-1, keepdims=True))
            p = jnp.exp2((sc - m_new) * c2)
            alpha = jnp.exp2((m_old - m_new) * c2)
            l_new = alpha * l_old + jnp.sum(p, axis=-1, keepdims=True)
            acc = alpha * acc + jnp.dot(p.astype(BF16), v_ref[at, :], preferred_element_type=F32)
            return m_new, l_new, acc

        init = (jnp.full((t, 1), -jnp.inf, F32), jnp.zeros((t, 1), F32), jnp.zeros((t, HEAD_PAD), F32))
        carry = lax.fori_loop(0, i, lambda j, cr: chunk(j, cr, False), init)
        m_fin, l_fin, acc = chunk(i, carry, True)
        o_ref[...] = (acc / l_fin).astype(o_ref.dtype)
        lse_ref[...] = jnp.broadcast_to(m_fin * ATTN_SCALE + jnp.log(l_fin), (t, HEAD_PAD))
        finish()

    qo = pl.BlockSpec((t, HEAD_PAD), lambda h, i: (i, h))
    whole = pl.BlockSpec((s, HEAD_PAD), lambda h, i: (0, h))
    return _ride_call(
        ride, body, name, (jax.ShapeDtypeStruct(q.shape, BF16), jax.ShapeDtypeStruct(q.shape, F32)), grid,
        [qo, whole, whole], (qo, qo), ("parallel", "parallel"), (q, k, v))


def _attn_delta(do, o, name):
    s = o.shape[0]
    t = _attn_tile(s)

    def body(do_ref, o_ref, delta_ref, dob_ref):
        dov = do_ref[...]
        delta_ref[...] = jnp.broadcast_to(jnp.sum(dov * o_ref[...].astype(F32), axis=-1, keepdims=True), (t, HEAD_PAD))
        dob_ref[...] = dov.astype(BF16)

    blk = pl.BlockSpec((t, HEAD_PAD), lambda i, h: (i, h))
    return pl.pallas_call(
        body, name=name, out_shape=(jax.ShapeDtypeStruct(o.shape, F32), jax.ShapeDtypeStruct(o.shape, BF16)),
        grid=(s // t, N_HEADS), in_specs=[blk, blk], out_specs=(blk, blk),
        compiler_params=_params("parallel", "parallel"))(do, o)


def _flash_bwd(q, k, v, do, lse, delta, name, ride=None):
    s = q.shape[0]
    t = _attn_tile(s)
    nt = s // t
    c2 = ATTN_SCALE * LOG2E
    grid = (N_HEADS, nt)

    def body(*refs):
        (q_ref, k_ref, v_ref, do_ref, lse_ref, delta_ref, dq_ref, dk_ref, dv_ref), start, finish = _ride_hooks(
            ride, refs, 6, 3, grid)
        start()
        j = pl.program_id(1)
        kv, vv = k_ref[...], v_ref[...]

        @pl.when(j == 0)
        def _():
            dq_ref[...] = jnp.zeros_like(dq_ref)

        def chunk(i, carry, masked):
            dk_acc, dv_acc = carry
            at = pl.ds(pl.multiple_of(i * t, t), t)
            qi, doi = q_ref[at, :], do_ref[at, :]
            sc = _raw_scores(qi, kv, masked)
            p = jnp.exp2(sc * c2 - lse_ref[at, pl.ds(0, 1)] * LOG2E)
            dp = lax.dot_general(doi, vv, (((1,), (1,)), ((), ())), preferred_element_type=F32)
            ds = (p * (dp - delta_ref[at, pl.ds(0, 1)])).astype(BF16)
            dv_acc = dv_acc + lax.dot_general(p.astype(BF16), doi, (((0,), (0,)), ((), ())), preferred_element_type=F32)
            dk_acc = dk_acc + lax.dot_general(ds, qi, (((0,), (0,)), ((), ())), preferred_element_type=F32)
            dq_ref[at, :] += jnp.dot(ds, kv, preferred_element_type=F32) * ATTN_SCALE
            return dk_acc, dv_acc

        zero = jnp.zeros((t, HEAD_PAD), F32)
        carry = chunk(j, (zero, zero), True)
        dk_acc, dv_acc = lax.fori_loop(j + 1, nt, lambda i, cr: chunk(i, cr, False), carry)
        dk_ref[...] = dk_acc * ATTN_SCALE
        dv_ref[...] = dv_acc.astype(BF16)
        finish()

    blk = pl.BlockSpec((t, HEAD_PAD), lambda h, j: (j, h))
    whole = pl.BlockSpec((s, HEAD_PAD), lambda h, j: (0, h))
    return _ride_call(
        ride, body, name, (jax.ShapeDtypeStruct(q.shape, F32), jax.ShapeDtypeStruct(q.shape, F32),
                           jax.ShapeDtypeStruct(q.shape, BF16)), grid,
        [whole, blk, blk, whole, whole, whole], (whole, blk, blk), ("parallel", "arbitrary"), (q, k, v, do, lse, delta))


def _conv_tile(s):
    return min(s, 256)


def _halo_before(t, width, cidx):
    per = t // CONV_HALO
    return pl.BlockSpec((CONV_HALO, width), lambda i: (jnp.maximum(i * per - 1, 0), cidx))


def _halo_after(t, width, cidx, n_tiles):
    per = t // CONV_HALO
    last = n_tiles * per - 1
    return pl.BlockSpec((CONV_HALO, width), lambda i: (jnp.minimum((i + 1) * per, last), cidx))


def _fill_glu(hbuf, ap_ref, gp_ref, a_ref, g_ref, t):
    first = pl.program_id(0) == 0
    hbuf[pl.ds(0, CONV_HALO), :] = jnp.where(first, 0.0, ap_ref[...] * _sigmoid(gp_ref[...]))
    hbuf[pl.ds(CONV_HALO, t), :] = a_ref[...] * _sigmoid(g_ref[...])


def _phase_copies(dst, src, t):
    n = t + CONV_HALO - SUBLANES
    for s in range(1, SUBLANES):
        dst[s, pl.ds(0, n), :] = src[pl.ds(s, n), :]


def _window(phases, src, k, t):
    if k % SUBLANES == 0:
        return src[pl.ds(k, t), :]
    return phases[k % SUBLANES, pl.ds(k - k % SUBLANES, t), :]


def _layer_norm_parts(co):
    mu = jnp.mean(co, axis=-1, keepdims=True)
    xc = co - mu
    rstd = lax.rsqrt(jnp.mean(xc * xc, axis=-1, keepdims=True) + EPS)
    return xc * rstd, rstd


def _conv_fwd(z, conv_w, conv_b, ln_g, ln_b, name):
    s = z.shape[0]
    t = _conv_tile(s)
    off = CONV_HALO - (CONV_W - 1)

    def body(ap_ref, gp_ref, a_ref, g_ref, w_ref, b_ref, lg_ref, lb_ref, hc_ref, co_ref, hbuf, hph):
        _fill_glu(hbuf, ap_ref, gp_ref, a_ref, g_ref, t)
        _phase_copies(hph, hbuf, t)
        acc = jnp.zeros((t, CONV_C), F32) + b_ref[...]
        for j in range(CONV_W):
            acc = acc + _window(hph, hbuf, off + j, t) * w_ref[pl.ds(j, 1), :]
        co_ref[...] = acc
        xh, _ = _layer_norm_parts(acc)
        y = xh * lg_ref[...] + lb_ref[...]
        hc_ref[...] = (y * _sigmoid(y)).astype(BF16)

    vec = _fixed((1, CONV_C))
    return pl.pallas_call(
        body, name=name, out_shape=(jax.ShapeDtypeStruct((s, CONV_C), BF16), jax.ShapeDtypeStruct((s, CONV_C), F32)),
        grid=(s // t,),
        in_specs=[_halo_before(t, *ZC_CONV_A), _halo_before(t, *ZC_CONV_G), _rows(t, *ZC_CONV_A), _rows(t, *ZC_CONV_G),
                  _fixed((CONV_HALO, CONV_C)), vec, vec, vec],
        out_specs=(_rows(t, CONV_C), _rows(t, CONV_C)),
        scratch_shapes=[pltpu.VMEM((t + CONV_HALO, CONV_C), F32), pltpu.VMEM((SUBLANES, t + CONV_HALO, CONV_C), F32)],
        compiler_params=_params("parallel"))(z, z, z, z, conv_w, conv_b.reshape(1, -1), ln_g.reshape(1, -1),
                                             ln_b.reshape(1, -1))


def _conv_bwd_norm(dhc, co, ln_g, ln_b, name):
    s = co.shape[0]
    t = min(s, 512)

    def body(dhc_ref, co_ref, lg_ref, lb_ref, dco_ref, dg_ref, db_ref, dcb_ref):
        xh, rstd = _layer_norm_parts(co_ref[...])
        y = xh * lg_ref[...] + lb_ref[...]
        sg = _sigmoid(y)
        dy = dhc_ref[...] * (sg * (1.0 + y * (1.0 - sg)))
        dxh = dy * lg_ref[...]
        dco = rstd * (dxh - jnp.mean(dxh, axis=-1, keepdims=True) - xh * jnp.mean(dxh * xh, axis=-1, keepdims=True))
        dco_ref[...] = dco

        @pl.when(pl.program_id(0) == 0)
        def _():
            dg_ref[...] = jnp.zeros_like(dg_ref)
            db_ref[...] = jnp.zeros_like(db_ref)
            dcb_ref[...] = jnp.zeros_like(dcb_ref)

        dg_ref[...] += jnp.sum(dy * xh, axis=0, keepdims=True)
        db_ref[...] += jnp.sum(dy, axis=0, keepdims=True)
        dcb_ref[...] += jnp.sum(dco, axis=0, keepdims=True)

    vec = _fixed((1, CONV_C))
    one = jax.ShapeDtypeStruct((1, CONV_C), F32)
    dco, dg, db, dcb = pl.pallas_call(
        body, name=name, out_shape=(jax.ShapeDtypeStruct((s, CONV_C), F32), one, one, one), grid=(s // t,),
        in_specs=[_rows(t, CONV_C), _rows(t, CONV_C), vec, vec], out_specs=(_rows(t, CONV_C), vec, vec, vec),
        compiler_params=_params("arbitrary"))(dhc, co, ln_g.reshape(1, -1), ln_b.reshape(1, -1))
    return dco, dg.reshape(-1), db.reshape(-1), dcb.reshape(-1)


def _conv_bwd_taps(dco, z, conv_w, dz, name):
    s = z.shape[0]
    t = _conv_tile(s)
    nt = s // t
    off = CONV_HALO - (CONV_W - 1)

    def body(ap_ref, gp_ref, a_ref, g_ref, d_ref, dn_ref, w_ref, _, du_ref, dw_ref, hbuf, dbuf, hph, dph):
        i = pl.program_id(0)
        _fill_glu(hbuf, ap_ref, gp_ref, a_ref, g_ref, t)
        dbuf[pl.ds(0, t), :] = d_ref[...]
        dbuf[pl.ds(t, CONV_HALO), :] = jnp.where(i == nt - 1, 0.0, dn_ref[...])
        _phase_copies(hph, hbuf, t)
        _phase_copies(dph, dbuf, t)

        @pl.when(i == 0)
        def _():
            dw_ref[...] = jnp.zeros_like(dw_ref)

        dcur = d_ref[...]
        dh = jnp.zeros((t, CONV_C), F32)
        for j in range(CONV_W):
            dh = dh + _window(dph, dbuf, CONV_W - 1 - j, t) * w_ref[pl.ds(j, 1), :]
            dw_ref[pl.ds(j, 1), :] += jnp.sum(dcur * _window(hph, hbuf, off + j, t), axis=0, keepdims=True)
        a, sg = a_ref[...], _sigmoid(g_ref[...])
        du_ref[:, pl.ds(0, CONV_C)] = (dh * sg).astype(BF16)
        du_ref[:, pl.ds(CONV_C, CONV_C)] = (dh * a * sg * (1.0 - sg)).astype(BF16)

    into = _into(dz, 7, 0)
    return pl.pallas_call(
        body, name=name, out_shape=(into["out_shape"], jax.ShapeDtypeStruct((CONV_HALO, CONV_C), F32)), grid=(nt,),
        in_specs=[_halo_before(t, *ZC_CONV_A), _halo_before(t, *ZC_CONV_G), _rows(t, *ZC_CONV_A), _rows(t, *ZC_CONV_G),
                  _rows(t, CONV_C), _halo_after(t, CONV_C, 0, nt), _fixed((CONV_HALO, CONV_C))] + into["in_specs"],
        out_specs=(_rows(t, *ZC_CONV), _fixed((CONV_HALO, CONV_C))), input_output_aliases=into["input_output_aliases"],
        scratch_shapes=[pltpu.VMEM((t + CONV_HALO, CONV_C), F32), pltpu.VMEM((t + CONV_HALO, CONV_C), F32),
                        pltpu.VMEM((SUBLANES, t + CONV_HALO, CONV_C), F32),
                        pltpu.VMEM((SUBLANES, t + CONV_HALO, CONV_C), F32)],
        compiler_params=_params("arbitrary"))(z, z, z, z, dco, dco, conv_w, dz)


def _pool_tile(s):
    return min(s, 512)


def _pool_counts(row0, n, window):
    rows = row0 + lax.broadcasted_iota(jnp.int32, (n, POOL_GD), 0)
    return jnp.minimum(rows + 1, window).astype(F32)


def _pool_diff(ubuf, gi, window, row0, t):
    lanes = pl.ds(gi * POOL_GD, POOL_GD)
    tot = ubuf[pl.ds(CONV_HALO, t), lanes]
    cur = tot
    for back in range(1, window):
        tot = tot + ubuf[pl.ds(CONV_HALO - back, t), lanes]
    return tot / _pool_counts(row0, t, window) - cur


def _pool_fwd(z, pool_w, pool_scale, name):
    s = z.shape[0]
    t = _pool_tile(s)

    def body(up_ref, u_ref, w_ref, sc_ref, m_ref, ubuf):
        i = pl.program_id(0)
        ubuf[pl.ds(0, CONV_HALO), :] = jnp.where(i == 0, 0.0, up_ref[...])
        ubuf[pl.ds(CONV_HALO, t), :] = u_ref[...]
        for gi, window in enumerate(POOL_WINDOWS):
            d = _pool_diff(ubuf, gi, window, i * t, t)
            mm = jnp.dot(d.astype(BF16), w_ref[gi].astype(BF16), preferred_element_type=F32)
            lanes = pl.ds(gi * POOL_GD, POOL_GD)
            m_ref[:, lanes] = (mm * sc_ref[:, lanes]).astype(BF16)

    return pl.pallas_call(
        body, name=name, out_shape=jax.ShapeDtypeStruct((s, POOL_C), BF16), grid=(s // t,),
        in_specs=[_halo_before(t, *ZC_POOL), _rows(t, *ZC_POOL), _fixed((POOL_G, POOL_GD, POOL_GD)), _fixed((1, POOL_C))],
        out_specs=_rows(t, POOL_C), scratch_shapes=[pltpu.VMEM((t + CONV_HALO, POOL_C), F32)],
        compiler_params=_params("parallel"))(z, z, pool_w, pool_scale.reshape(1, -1))


def _pool_bwd(dm, z, pool_w, pool_scale, dz, name):
    s = z.shape[0]
    t = _pool_tile(s)
    nt = s // t

    def body(up_ref, u_ref, dm_ref, dmn_ref, w_ref, sc_ref, _, du_ref, dw_ref, dsc_ref, ubuf, ebuf):
        i = pl.program_id(0)
        ubuf[pl.ds(0, CONV_HALO), :] = jnp.where(i == 0, 0.0, up_ref[...])
        ubuf[pl.ds(CONV_HALO, t), :] = u_ref[...]

        @pl.when(i == 0)
        def _():
            dw_ref[...] = jnp.zeros_like(dw_ref)
            dsc_ref[...] = jnp.zeros_like(dsc_ref)

        dm_next = jnp.where(i == nt - 1, 0.0, dmn_ref[...])
        for gi, window in enumerate(POOL_WINDOWS):
            lanes = pl.ds(gi * POOL_GD, POOL_GD)
            wb = w_ref[gi].astype(BF16)
            scale = sc_ref[:, lanes]
            d = _pool_diff(ubuf, gi, window, i * t, t).astype(BF16)
            mm = jnp.dot(d, wb, preferred_element_type=F32)
            dmv = dm_ref[:, lanes]
            dsc_ref[:, lanes] += jnp.sum(dmv * mm, axis=0, keepdims=True)
            dmm = (dmv * scale).astype(BF16)
            dw_ref[gi] += lax.dot_general(d, dmm, (((0,), (0,)), ((), ())), preferred_element_type=F32)
            dd = lax.dot_general(dmm, wb, (((1,), (1,)), ((), ())), preferred_element_type=F32)
            dd_next = lax.dot_general((dm_next[:, gi * POOL_GD:(gi + 1) * POOL_GD] * scale).astype(BF16), wb,
                                      (((1,), (1,)), ((), ())), preferred_element_type=F32)
            ebuf[pl.ds(0, t), lanes] = dd / _pool_counts(i * t, t, window)
            ebuf[pl.ds(t, CONV_HALO), lanes] = dd_next / _pool_counts((i + 1) * t, CONV_HALO, window)
            du = -dd
            for ahead in range(window):
                du = du + ebuf[pl.ds(ahead, t), lanes]
            du_ref[:, lanes] = du.astype(BF16)

    into = _into(dz, 6, 0)
    du, dw, dsc = pl.pallas_call(
        body, name=name,
        out_shape=(into["out_shape"], jax.ShapeDtypeStruct((POOL_G, POOL_GD, POOL_GD), F32),
                   jax.ShapeDtypeStruct((1, POOL_C), F32)), grid=(nt,),
        in_specs=[_halo_before(t, *ZC_POOL), _rows(t, *ZC_POOL), _rows(t, POOL_C), _halo_after(t, POOL_C, 0, nt),
                  _fixed((POOL_G, POOL_GD, POOL_GD)), _fixed((1, POOL_C))] + into["in_specs"],
        out_specs=(_rows(t, *ZC_POOL), _fixed((POOL_G, POOL_GD, POOL_GD)), _fixed((1, POOL_C))),
        input_output_aliases=into["input_output_aliases"],
        scratch_shapes=[pltpu.VMEM((t + CONV_HALO, POOL_C), F32), pltpu.VMEM((t + CONV_HALO, POOL_C), F32)],
        compiler_params=_params("arbitrary"))(z, z, dm, dm, pool_w, pool_scale.reshape(1, -1), dz)
    return du, dw, dsc.reshape(-1)


def _gate_specs(ts):
    width, first = ZC_GATE
    return [_rows(ts, width, first + b) for b in range(3)]


def _merge_fwd(z, ys, name):
    s = z.shape[0]
    ts = min(s, 256)

    def body(g0, g1, g2, y0, y1, y2, o_ref):
        o_ref[...] = (_sigmoid(g0[...]) * y0[...] + _sigmoid(g1[...]) * y1[...]
                      + _sigmoid(g2[...]) * y2[...]).astype(BF16)

    return pl.pallas_call(
        body, name=name, out_shape=jax.ShapeDtypeStruct((s, D_MODEL), BF16), grid=(s // ts,),
        in_specs=_gate_specs(ts) + [_rows(ts, D_MODEL)] * 3, out_specs=_rows(ts, D_MODEL),
        compiler_params=_params("parallel"))(z, z, z, *ys)


def _merge_bwd(z, ys, dmerged, name):
    s = z.shape[0]
    ts = min(s, 256)

    def body(g0, g1, g2, y0, y1, y2, dm_ref, dy0, dy1, dy2, dz_ref):
        dmv = dm_ref[...]
        for b, (g_ref, y_ref, dy_ref) in enumerate(((g0, y0, dy0), (g1, y1, dy1), (g2, y2, dy2))):
            sg = _sigmoid(g_ref[...])
            dy_ref[...] = (dmv * sg).astype(BF16)
            dz_ref[:, pl.ds(b * D_MODEL, D_MODEL)] = (dmv * y_ref[...] * sg * (1.0 - sg)).astype(BF16)

    out = jax.ShapeDtypeStruct((s, D_MODEL), BF16)
    return pl.pallas_call(
        body, name=name, out_shape=(out,) * 3 + (jax.ShapeDtypeStruct((s, Z_W), BF16),), grid=(s // ts,),
        in_specs=_gate_specs(ts) + [_rows(ts, D_MODEL)] * 4,
        out_specs=(_rows(ts, D_MODEL),) * 3 + (_rows(ts, *ZC_GATES),),
        compiler_params=_params("parallel"))(z, z, z, *ys, dmerged)


def _swiglu_fwd(hg, hu, name):
    s, f = hg.shape
    ts, tc = min(s, 512), _tile(f, 1024)
    blk = pl.BlockSpec((ts, tc), lambda i, j: (i, j))

    def body(g_ref, u_ref, o_ref):
        g = g_ref[...]
        o_ref[...] = (g * _sigmoid(g) * u_ref[...]).astype(BF16)

    return pl.pallas_call(
        body, name=name, out_shape=jax.ShapeDtypeStruct((s, f), BF16), grid=(s // ts, f // tc),
        in_specs=[blk, blk], out_specs=blk, compiler_params=_params("parallel", "parallel"))(hg, hu)


def _swiglu_bwd(hg, hu, dact, name):
    s, f = hg.shape
    ts, tc = min(s, 512), _tile(f, 1024)
    blk = pl.BlockSpec((ts, tc), lambda i, j: (i, j))

    def body(g_ref, u_ref, d_ref, dg_ref, du_ref):
        g, d = g_ref[...], d_ref[...]
        sg = _sigmoid(g)
        dg_ref[...] = (d * u_ref[...] * (sg * (1.0 + g * (1.0 - sg)))).astype(BF16)
        du_ref[...] = (d * g * sg).astype(BF16)

    out = jax.ShapeDtypeStruct((s, f), BF16)
    return pl.pallas_call(
        body, name=name, out_shape=(out, out), grid=(s // ts, f // tc), in_specs=[blk, blk, blk], out_specs=(blk, blk),
        compiler_params=_params("parallel", "parallel"))(hg, hu, dact)


def _loss_grad(y, target, name):
    s, d = y.shape
    ts = min(s, 512)

    def body(y_ref, t_ref, dy_ref, sq_ref):
        e = y_ref[...] - t_ref[...]
        dy_ref[...] = e / d

        @pl.when(pl.program_id(0) == 0)
        def _():
            sq_ref[...] = jnp.zeros_like(sq_ref)

        sq_ref[...] += jnp.sum(e * e, axis=0, keepdims=True)

    return pl.pallas_call(
        body, name=name, out_shape=(jax.ShapeDtypeStruct((s, d), F32), jax.ShapeDtypeStruct((1, d), F32)),
        grid=(s // ts,), in_specs=[_rows(ts, d), _rows(ts, d)], out_specs=(_rows(ts, d), _fixed((1, d))),
        compiler_params=_params("arbitrary"))(y, target)


def _adamw(w, g, m, v, name):
    shape = w.shape
    cols = shape[-1]
    keep3 = w.ndim == 3 and shape[1] < SUBLANES
    view = shape if keep3 else (math.prod(shape[:-1]), cols)
    rows = view[0]
    if keep3:
        cap = max(1, (1 << 20) // (SUBLANES * cols * 4))
        tr = max(t for t in range(1, cap + 1) if rows % t == 0)
    else:
        tr = _row_tile(rows, cols * 4)

    def body(w_ref, g_ref, m_ref, v_ref, d_ref, mo_ref, vo_ref):
        gv = g_ref[...]
        mn = B1 * m_ref[...] + (1.0 - B1) * gv
        vn = B2 * v_ref[...] + (1.0 - B2) * (gv * gv)
        m_hat = mn / (1.0 - B1 ** STEP)
        v_hat = vn / (1.0 - B2 ** STEP)
        d_ref[...] = -LR * (m_hat / (jnp.sqrt(v_hat) + ADAM_EPS) + WD * w_ref[...])
        mo_ref[...] = mn
        vo_ref[...] = vn

    spec = pl.BlockSpec((tr,) + view[1:], lambda i: (i,) + (0,) * (len(view) - 1))
    out = jax.ShapeDtypeStruct(view, F32)
    res = pl.pallas_call(
        body, name=name, out_shape=(out,) * 3, grid=(rows // tr,), in_specs=[spec] * 4, out_specs=(spec,) * 3,
        compiler_params=_params("parallel"))(*[t.reshape(view) for t in (w, g, m, v)])
    return tuple(r.reshape(shape) for r in res)


LANE_MAJOR = ("w_uq", "w_uk", "w_uv", "w_gate", "w_up")


def _lane_major(name, a):
    if name == "w_in":
        return a.transpose(2, 0, 1)
    if name in LANE_MAJOR:
        return a.transpose(0, 2, 1)
    return a


def _from_lane_major(name, a):
    if name == "w_in":
        return a.transpose(1, 2, 0)
    return _lane_major(name, a)


ANY = pl.BlockSpec(memory_space=pl.ANY)


class _GatherRide:
    def __init__(self, arrays):
        n = len(arrays)
        self.arrays = list(arrays)
        self.out_shape = [jax.ShapeDtypeStruct((N_DEV,) + a.shape, a.dtype) for a in arrays]
        self.scratch = [pltpu.SemaphoreType.DMA((n, 7)), pltpu.SemaphoreType.DMA((n, 7)), pltpu.SemaphoreType.DMA((n,))]

    def _copies(self, ins, outs, sems):
        send_sems, recv_sems, local_sems = sems
        n = len(self.arrays)
        x, y, c = lax.axis_index("x"), lax.axis_index("y"), lax.axis_index("c")
        me, sibling = (x, y, c), (x, y, 1 - c)
        chips = [(1 - x, y), (x, 1 - y), (1 - x, 1 - y)]

        def slot(a, px, py, pc):
            return outs[a].at[4 * px + 2 * py + pc]

        def copy(a, k, block, to, src=None):
            return pltpu.make_async_remote_copy(
                src_ref=slot(a, *block) if src is None else src, dst_ref=slot(a, *block), send_sem=send_sems.at[a, k],
                recv_sem=recv_sems.at[a, k], device_id=to, device_id_type=MESH)

        mine = [pltpu.make_async_copy(ins[a], slot(a, *me), local_sems.at[a]) for a in range(n)]
        first = []
        for a in range(n):
            first.append(copy(a, 0, me, sibling, src=ins[a]))
            first += [copy(a, 1 + j, me, (*chip, c), src=ins[a]) for j, chip in enumerate(chips)]
        return n, me, sibling, chips, c, copy, mine, first

    def start(self, ins, outs, sems):
        _, _, _, _, _, _, mine, first = self._copies(ins, outs, sems)
        for cp in mine + first:
            cp.start()

    def finish(self, ins, outs, sems):
        n, me, sibling, chips, c, copy, mine, first = self._copies(ins, outs, sems)
        passed = []
        for j, chip in enumerate(chips):
            for a in range(n):
                copy(a, 1 + j, (*chip, c), me).wait_recv()
                passed.append(copy(a, 4 + j, (*chip, c), sibling))
                passed[-1].start()
        for a in range(n):
            copy(a, 0, sibling, me).wait_recv()
            for j, chip in enumerate(chips):
                copy(a, 4 + j, (*chip, 1 - c), me).wait_recv()
        for cp in first + passed:
            cp.wait_send()
        for cp in mine:
            cp.wait()


class _ReduceRide:
    def __init__(self, arrays):
        n = len(arrays)
        self.arrays = list(arrays)
        self.out_shape = [jax.ShapeDtypeStruct(a.shape, a.dtype) for a in arrays]
        self.scratch = [pltpu.SemaphoreType.DMA((n, 7)), pltpu.SemaphoreType.DMA((n, 7)), pltpu.SemaphoreType.DMA((n,))]

    def _copies(self, ins, outs, sems):
        send_sems, recv_sems, local_sems = sems
        n = len(self.arrays)
        x, y, c = lax.axis_index("x"), lax.axis_index("y"), lax.axis_index("c")
        mine = [pltpu.make_async_copy(ins[a].at[4 * x + 2 * y + c], outs[a].at[0], local_sems.at[a]) for a in range(n)]
        copies = []
        for a in range(n):
            for k in range(1, N_DEV):
                px = 1 - x if k & 4 else x
                py = 1 - y if k & 2 else y
                pc = 1 - c if k & 1 else c
                copies.append(pltpu.make_async_remote_copy(
                    src_ref=ins[a].at[4 * px + 2 * py + pc], dst_ref=outs[a].at[k], send_sem=send_sems.at[a, k - 1],
                    recv_sem=recv_sems.at[a, k - 1], device_id=(px, py, pc), device_id_type=MESH))
        return mine, copies

    def start(self, ins, outs, sems):
        mine, copies = self._copies(ins, outs, sems)
        for cp in mine + copies:
            cp.start()

    def finish(self, ins, outs, sems):
        mine, copies = self._copies(ins, outs, sems)
        for cp in copies + mine:
            cp.wait()


def _run_ride(ride, name):
    n = len(ride.arrays)

    def body(*refs):
        ins, outs, sems = refs[:n], refs[n:2 * n], refs[2 * n:]
        ride.start(ins, outs, sems)
        ride.finish(ins, outs, sems)

    return pl.pallas_call(body, name=name, out_shape=ride.out_shape, in_specs=[ANY] * n, out_specs=[ANY] * n,
                          scratch_shapes=ride.scratch)(*ride.arrays)


def _all_gather(arrays, name):
    return _run_ride(_GatherRide(arrays), name)


def _swap_with_sibling(arrays, name):
    n = len(arrays)

    def body(*refs):
        ins, outs = refs[:n], refs[n:2 * n]
        send_sems, recv_sems = refs[2 * n:]
        x, y, c = lax.axis_index("x"), lax.axis_index("y"), lax.axis_index("c")
        copies = [pltpu.make_async_remote_copy(
            src_ref=ins[a].at[1 - c], dst_ref=outs[a], send_sem=send_sems.at[a], recv_sem=recv_sems.at[a],
            device_id=(x, y, 1 - c), device_id_type=MESH) for a in range(n)]
        for cp in copies:
            cp.start()
        for cp in copies:
            cp.wait()

    return pl.pallas_call(
        body, name=name, out_shape=[jax.ShapeDtypeStruct(a.shape[1:], a.dtype) for a in arrays],
        in_specs=[ANY] * n, out_specs=[ANY] * n,
        scratch_shapes=[pltpu.SemaphoreType.DMA((n,)), pltpu.SemaphoreType.DMA((n,))])(*arrays)


def _exchange_chips(arrays, name):
    n = len(arrays)

    def body(*refs):
        ins, outs = refs[:n], refs[n:2 * n]
        send_sems, recv_sems, local_sems = refs[2 * n:]
        x, y, c = lax.axis_index("x"), lax.axis_index("y"), lax.axis_index("c")
        partners = [(x, 1 - y), (1 - x, y), (1 - x, 1 - y)]
        mine = [pltpu.make_async_copy(ins[a].at[2 * x + y], outs[a].at[0], local_sems.at[a]) for a in range(n)]
        copies = [pltpu.make_async_remote_copy(
            src_ref=ins[a].at[2 * px + py], dst_ref=outs[a].at[1 + k], send_sem=send_sems.at[a, k],
            recv_sem=recv_sems.at[a, k], device_id=(px, py, c), device_id_type=MESH)
            for a in range(n) for k, (px, py) in enumerate(partners)]
        for cp in mine + copies:
            cp.start()
        for cp in copies + mine:
            cp.wait()

    return pl.pallas_call(
        body, name=name, out_shape=[jax.ShapeDtypeStruct(a.shape, a.dtype) for a in arrays],
        in_specs=[ANY] * n, out_specs=[ANY] * n,
        scratch_shapes=[pltpu.SemaphoreType.DMA((n, 3)), pltpu.SemaphoreType.DMA((n, 3)), pltpu.SemaphoreType.DMA((n,))],
    )(*arrays)


def _as_rows(a, lead):
    return a.reshape(a.shape[:lead] + (math.prod(a.shape[lead:-1]), a.shape[-1]))


def _add_pairs(a, b, name):
    a2, b2 = _as_rows(a, 0), _as_rows(b, 0)
    rows, cols = a2.shape
    tr = _row_tile(rows, cols * 4)

    def body(a_ref, b_ref, o_ref):
        o_ref[...] = (a_ref[...].astype(F32) + b_ref[...].astype(F32)).astype(o_ref.dtype)

    spec = _rows(tr, cols)
    out = pl.pallas_call(body, name=name, out_shape=jax.ShapeDtypeStruct(a2.shape, a.dtype), grid=(rows // tr,),
                         in_specs=[spec, spec], out_specs=spec, compiler_params=_params("parallel"))(a2, b2)
    return out.reshape(a.shape)


def _sum_blocks(a, name):
    a3 = _as_rows(a, 1)
    n, rows, cols = a3.shape
    tr = _row_tile(rows, n * cols * 4)

    def body(a_ref, o_ref):
        tot = a_ref[0].astype(F32)
        for k in range(1, n):
            tot = tot + a_ref[k].astype(F32)
        o_ref[...] = tot

    out = pl.pallas_call(body, name=name, out_shape=jax.ShapeDtypeStruct((rows, cols), F32), grid=(rows // tr,),
                         in_specs=[pl.BlockSpec((n, tr, cols), lambda j: (0, j, 0))], out_specs=_rows(tr, cols),
                         compiler_params=_params("parallel"))(a3)
    return out.reshape(a.shape[1:])


MIX_GROUPS = ("w_in", "w_uq", "w_uk", "w_uv", "w_attn_o", "w_conv_o", "w_pool_o", "w_mix_o")
FFN_GROUPS = ("w_gate", "w_up", "w_down")


def _pad_axis(a, axis, size):
    pad = [(0, 0)] * a.ndim
    pad[axis] = (0, size - a.shape[axis])
    return jnp.pad(a, pad)


def _local_groups(sh, l):
    out = {n: sh[n][l] for n in BIG}
    for n in ("w_uq", "w_uk", "w_uv"):
        out[n] = _pad_axis(out[n], -1, HEAD_PAD)
    for n in ("w_gate", "w_up"):
        out[n] = _pad_axis(out[n], -1, FF_SHARD_PAD)
    out["w_down"] = _pad_axis(out["w_down"], 0, FF_SHARD_PAD)
    return {n: v.astype(BF16) for n, v in out.items()}


def _arrange_w_in(blocks):
    parts, pos = [], 0
    for ref_lo, ref_hi, at in sorted(W_IN_PIECES, key=lambda p: p[2]):
        if at > pos:
            parts.append(jnp.zeros((blocks.shape[1], at - pos), blocks.dtype))
        for d in range(N_DEV):
            lo, hi = max(ref_lo, d * W_IN_SHARD), min(ref_hi, (d + 1) * W_IN_SHARD)
            if lo < hi:
                parts.append(blocks[d][:, lo - d * W_IN_SHARD:hi - d * W_IN_SHARD])
        pos = at + ref_hi - ref_lo
    if pos < Z_W:
        parts.append(jnp.zeros((blocks.shape[1], Z_W - pos), blocks.dtype))
    return jnp.concatenate(parts, axis=1)


def _w_in_shard(g, d):
    parts = []
    for ref_lo, ref_hi, at in W_IN_PIECES:
        lo, hi = max(ref_lo, d * W_IN_SHARD), min(ref_hi, (d + 1) * W_IN_SHARD)
        if lo < hi:
            parts.append(g[:, at + lo - ref_lo:at + hi - ref_lo])
    return jnp.concatenate(parts, axis=1)


def _mixer_weights(gat):
    w = dict(gat)
    w["w_in"] = _arrange_w_in(gat["w_in"])
    attn_o = gat["w_attn_o"].reshape(N_DEV, N_HEADS, V_HEAD, LANES)
    w["w_attn_o"] = _pad_axis(attn_o, 2, HEAD_PAD).reshape(N_DEV, N_HEADS * HEAD_PAD, LANES)
    w["w_mix_o"] = gat["w_mix_o"].reshape(D_MODEL, D_MODEL)
    return w


def _ffn_weights(gat):
    return {"w_gate": gat["w_gate"], "w_up": gat["w_up"], "w_down": gat["w_down"].reshape(D_FF_PAD, D_MODEL)}


def _mixer_grad_groups(gb):
    g = dict(gb)
    g["w_in"] = jnp.stack([_w_in_shard(gb["w_in"], d) for d in range(N_DEV)])
    attn_o = gb["w_attn_o"].reshape(N_DEV, N_HEADS, HEAD_PAD, LANES)[:, :, :V_HEAD]
    g["w_attn_o"] = attn_o.reshape(N_DEV, N_HEADS * V_HEAD, LANES)
    g["w_mix_o"] = gb["w_mix_o"].reshape(N_DEV, D_MODEL // N_DEV, D_MODEL)
    return g


def _ffn_grad_groups(gb):
    return {"w_gate": gb["w_gate"], "w_up": gb["w_up"], "w_down": gb["w_down"].reshape(N_DEV, FF_SHARD_PAD, D_MODEL)}


def _grads_from_groups(tot):
    g = dict(tot)
    g["w_uq"] = tot["w_uq"][:, :QK_NOPE + QK_ROPE]
    g["w_uk"], g["w_uv"] = tot["w_uk"][:, :QK_NOPE], tot["w_uv"][:, :V_HEAD]
    g["w_gate"], g["w_up"] = tot["w_gate"][:, :FF_SHARD], tot["w_up"][:, :FF_SHARD]
    g["w_down"] = tot["w_down"][:FF_SHARD]
    return g


SMALL_GROUPS = (
    (D_MODEL, ("mix_norm_pre", "mix_norm_post", "ffn_norm_pre", "ffn_norm_post")),
    (CONV_C, ("conv_w", "conv_b", "conv_ln_g", "conv_ln_b", "pool_scale")),
    (Q_RANK, ("q_norm",)), (KV_RANK, ("kv_norm",)), (POOL_GD, ("pool_w",)),
)


def _small_rows(name):
    return {"conv_w": CONV_HALO, "pool_w": POOL_G * POOL_GD}.get(name, SUBLANES)


def _small_groups(small):
    out = []
    for width, names in SMALL_GROUPS:
        parts = []
        for l in range(DEPTH):
            for n in names:
                part = small[l][n].reshape(-1, width)
                parts.append(_pad_axis(part, 0, _small_rows(n)))
        out.append(jnp.concatenate(parts, axis=0))
    return out


def _small_from_groups(groups):
    shapes = {"conv_w": (CONV_W, CONV_C), "pool_w": (POOL_G, POOL_GD, POOL_GD)}
    out = {}
    for (width, names), g in zip(SMALL_GROUPS, groups):
        row = 0
        for l in range(DEPTH):
            for n in names:
                rows = _small_rows(n)
                real = {"conv_w": CONV_W, "pool_w": POOL_G * POOL_GD}.get(n, 1)
                out.setdefault(n, []).append(g[row:row + real].reshape(shapes.get(n, (width,))))
                row += rows
    return {n: jnp.stack(v) for n, v in out.items()}


def _mixer_fwd(x, tables, w, sm, tag, ride):
    nm = lambda n: f"{n}_{tag}"
    h = _rms_fwd(x, (D_MODEL, 0), sm["mix_norm_pre"], BF16, nm("mix_pre_norm"))
    z = _matmul(h, w["w_in"], "nn", F32, nm("in_proj"))
    cq = _rms_fwd(z, ZC_Q, sm["q_norm"], BF16, nm("q_norm"))
    ckv = _rms_fwd(z, ZC_KV, sm["kv_norm"], BF16, nm("kv_norm"))
    qf = _matmul(cq, w["w_uq"], "nn", F32, nm("q_up"))
    kf = _matmul(ckv, w["w_uk"], "nn", F32, nm("k_up"))
    v = _matmul(ckv, w["w_uv"], "nn", BF16, nm("v_up"))
    q, k = _rope_qk_fwd(qf, kf, z, tables, nm("rope_qk"))
    (o, lse), rode = _flash_fwd(q, k, v, nm("flash_fwd"), ride)
    y_attn = _matmul(o, w["w_attn_o"], "nn", F32, nm("attn_out"))
    hc, co = _conv_fwd(z, sm["conv_w"], sm["conv_b"], sm["conv_ln_g"], sm["conv_ln_b"], nm("conv_fwd"))
    y_conv = _matmul(hc, w["w_conv_o"], "nn", F32, nm("conv_out"))
    pm = _pool_fwd(z, sm["pool_w"], sm["pool_scale"], nm("pool_fwd"))
    y_pool = _matmul(pm, w["w_pool_o"], "nn", F32, nm("pool_out"))
    ys = (y_attn, y_conv, y_pool)
    merged = _merge_fwd(z, ys, nm("merge_fwd"))
    mo = _matmul(merged, w["w_mix_o"], "nn", F32, nm("mix_out"))
    x_mid = _rms_fwd(mo, (D_MODEL, 0), sm["mix_norm_post"], F32, nm("mix_post_norm"), res=x)
    saved = dict(x=x, h=h, z=z, cq=cq, ckv=ckv, q=q, k=k, v=v, o=o, lse=lse, hc=hc, co=co, pm=pm, ys=ys, merged=merged,
                 mo=mo)
    return x_mid, saved, rode


def _ffn_fwd(x_mid, w, sm, tag):
    nm = lambda n: f"{n}_{tag}"
    h2 = _rms_fwd(x_mid, (D_MODEL, 0), sm["ffn_norm_pre"], BF16, nm("ffn_pre_norm"))
    hg = _matmul(h2, w["w_gate"], "nn", F32, nm("ffn_gate"))
    hu = _matmul(h2, w["w_up"], "nn", F32, nm("ffn_up"))
    act = _swiglu_fwd(hg, hu, nm("swiglu_fwd"))
    fo = _matmul(act, w["w_down"], "nn", F32, nm("ffn_down"))
    out = _rms_fwd(fo, (D_MODEL, 0), sm["ffn_norm_post"], F32, nm("ffn_post_norm"), res=x_mid)
    saved = dict(x_mid=x_mid, h2=h2, hg=hg, hu=hu, act=act, fo=fo)
    return out, saved


def _ffn_bwd(dout, sv, w, sm, tag):
    nm = lambda n: f"{n}_{tag}"
    gb, gs = {}, {}
    dfo, gs["ffn_norm_post"] = _rms_bwd(sv["fo"], (D_MODEL, 0), sm["ffn_norm_post"], dout, BF16, nm("ffn_post_norm_bwd"))
    dact = _matmul(dfo, w["w_down"], "nt", F32, nm("ffn_down_dx"))
    gb["w_down"] = _matmul(sv["act"], dfo, "tn", BF16, nm("ffn_down_dw"))
    dhg, dhu = _swiglu_bwd(sv["hg"], sv["hu"], dact, nm("swiglu_bwd"))
    dh2_g = _matmul(dhg, w["w_gate"], "nt", F32, nm("ffn_gate_dx"))
    dh2 = _matmul(dhu, w["w_up"], "nt", F32, nm("ffn_up_dx"), add=dh2_g)
    gb["w_gate"] = _matmul(sv["h2"], dhg, "tn", BF16, nm("ffn_gate_dw"), blocked=True)
    gb["w_up"] = _matmul(sv["h2"], dhu, "tn", BF16, nm("ffn_up_dw"), blocked=True)
    dmid, gs["ffn_norm_pre"] = _rms_bwd(sv["x_mid"], (D_MODEL, 0), sm["ffn_norm_pre"], dh2, F32, nm("ffn_pre_norm_bwd"),
                                        add=dout)
    return dmid, gb, gs


def _mixer_bwd(dmid, sv, tables, w, sm, tag, make_ride):
    nm = lambda n: f"{n}_{tag}"
    gb, gs = {}, {}
    dmo, gs["mix_norm_post"] = _rms_bwd(sv["mo"], (D_MODEL, 0), sm["mix_norm_post"], dmid, BF16, nm("mix_post_norm_bwd"))
    dmerged = _matmul(dmo, w["w_mix_o"], "nt", F32, nm("mix_out_dx"))
    gb["w_mix_o"] = _matmul(sv["merged"], dmo, "tn", BF16, nm("mix_out_dw"))
    dya, dyc, dyp, dz = _merge_bwd(sv["z"], sv["ys"], dmerged, nm("merge_bwd"))
    dpm = _matmul(dyp, w["w_pool_o"], "nt", F32, nm("pool_out_dx"))
    gb["w_pool_o"] = _matmul(sv["pm"], dyp, "tn", BF16, nm("pool_out_dw"), blocked=True)
    dz, gs["pool_w"], gs["pool_scale"] = _pool_bwd(dpm, sv["z"], sm["pool_w"], sm["pool_scale"], dz, nm("pool_bwd"))
    dhc = _matmul(dyc, w["w_conv_o"], "nt", F32, nm("conv_out_dx"))
    gb["w_conv_o"] = _matmul(sv["hc"], dyc, "tn", BF16, nm("conv_out_dw"), blocked=True)
    dco, gs["conv_ln_g"], gs["conv_ln_b"], gs["conv_b"] = _conv_bwd_norm(dhc, sv["co"], sm["conv_ln_g"], sm["conv_ln_b"],
                                                                        nm("conv_bwd_norm"))
    dz, gs["conv_w"] = _conv_bwd_taps(dco, sv["z"], sm["conv_w"], dz, nm("conv_bwd_taps"))
    do = _matmul(dya, w["w_attn_o"], "nt", F32, nm("attn_out_dx"))
    gb["w_attn_o"] = _matmul(sv["o"], dya, "tn", BF16, nm("attn_out_dw"), blocked=True)
    delta, dob = _attn_delta(do, sv["o"], nm("attn_delta"))
    (dq, dk, dv), rode = _flash_bwd(sv["q"], sv["k"], sv["v"], dob, sv["lse"], delta, nm("flash_bwd"), make_ride(gb))
    dqf, dkf, dz = _rope_qk_bwd(dq, dk, tables, dz, nm("rope_qk_bwd"))
    dcq_n = _matmul(dqf, w["w_uq"], "nt", F32, nm("q_up_dx"))
    gb["w_uq"] = _matmul(sv["cq"], dqf, "tn", BF16, nm("q_up_dw"), blocked=True)
    dckv_k = _matmul(dkf, w["w_uk"], "nt", F32, nm("k_up_dx"))
    dckv_n = _matmul(dv, w["w_uv"], "nt", F32, nm("v_up_dx"), add=dckv_k)
    gb["w_uk"] = _matmul(sv["ckv"], dkf, "tn", BF16, nm("k_up_dw"), blocked=True)
    gb["w_uv"] = _matmul(sv["ckv"], dv, "tn", BF16, nm("v_up_dw"), blocked=True)
    dz, gs["q_norm"] = _rms_bwd(sv["z"], ZC_Q, sm["q_norm"], dcq_n, BF16, nm("q_norm_bwd"), dz=dz)
    dz, gs["kv_norm"] = _rms_bwd(sv["z"], ZC_KV, sm["kv_norm"], dckv_n, BF16, nm("kv_norm_bwd"), dz=dz)
    dh =_matmul(dz, w["w_in"], "nt", F32, nm("in_proj_dx"))
    gb["w_in"] = _matmul(sv["h"], dz, "tn", BF16, nm("in_proj_dw"))
    dx, gs["mix_norm_pre"] = _rms_bwd(sv["x"], (D_MODEL, 0), sm["mix_norm_pre"], dh, F32, nm("mix_pre_norm_bwd"), add=dmid)
    return dx, gb, gs, rode


def _part_groups(part):
    return MIX_GROUPS if part == "mix" else FFN_GROUPS


class _Plan:
    def __init__(self, shards, conv_w):
        self.local = [_local_groups(shards, l) for l in range(DEPTH)]
        self.conv_w = conv_w
        self.gat, self.send, self.recv = {}, {}, {}

    @staticmethod
    def _riders(l):
        return [(l, "ffn")] + ([(l + 1, "mix")] if l + 1 < DEPTH else [])

    def gather_first(self):
        out = _all_gather([self.local[0][g] for g in MIX_GROUPS] + [self.conv_w], "gather_mixer_l0")
        self.gat[(0, "mix")] = dict(zip(MIX_GROUPS, out[:-1]))
        return out[-1]

    def fwd_ride(self, l):
        return _GatherRide([self.local[ll][g] for ll, part in self._riders(l) for g in _part_groups(part)])

    def fwd_done(self, l, outs):
        outs = list(outs)
        for ll, part in self._riders(l):
            self.gat[(ll, part)] = {g: outs.pop(0) for g in _part_groups(part)}

    def mixer_weights(self, l):
        return _mixer_weights(self.gat[(l, "mix")])

    def ffn_weights(self, l):
        return _ffn_weights(self.gat[(l, "ffn")])

    def add_grads(self, l, part, gb):
        self.send[(l, part)] = _mixer_grad_groups(gb) if part == "mix" else _ffn_grad_groups(gb)

    def bwd_ride(self, l):
        return _ReduceRide([self.send[(ll, part)][g] for ll, part in self._riders(l) for g in _part_groups(part)])

    def bwd_done(self, l, outs):
        outs = list(outs)
        for ll, part in self._riders(l):
            self.recv[(ll, part)] = {g: outs.pop(0) for g in _part_groups(part)}

    def finish(self):
        send = [self.send[(0, "mix")][g] for g in MIX_GROUPS]
        by_core = [a.reshape((4, 2) + a.shape[1:]).transpose((1, 0) + tuple(range(2, a.ndim + 1))) for a in send]
        core = lax.axis_index("c")
        own = [lax.dynamic_index_in_dim(a, core, axis=0, keepdims=False) for a in by_core]
        got = _swap_with_sibling(by_core, "reduce_d2d")
        pairs = [_add_pairs(a, b, f"reduce_pair_add_{g}") for g, a, b in zip(MIX_GROUPS, own, got)]
        self.recv[(0, "mix")] = dict(zip(MIX_GROUPS, _exchange_chips(pairs, "reduce_ici")))
        layers = []
        for l in range(DEPTH):
            tot = {g: _sum_blocks(a, f"reduce_sum_{g}_l{l}") for part in ("mix", "ffn")
                   for g, a in self.recv[(l, part)].items()}
            layers.append(_grads_from_groups(tot))
        return layers


def _local_step(x, positions, target, smalls, plan):
    tables = _rope_tables(positions)
    saved = []
    h = x
    for l in range(DEPTH):
        wm = plan.mixer_weights(l)
        h, svm, rode = _mixer_fwd(h, tables, wm, smalls[l], f"l{l}", plan.fwd_ride(l))
        plan.fwd_done(l, rode)
        wf = plan.ffn_weights(l)
        h, svf = _ffn_fwd(h, wf, smalls[l], f"l{l}")
        saved.append((svm, svf, wm, wf))
    dy, sq = _loss_grad(h, target, "loss_grad")
    small = [None] * DEPTH
    for l in reversed(range(DEPTH)):
        svm, svf, wm, wf = saved[l]
        dmid, gbf, gsf = _ffn_bwd(dy, svf, wf, smalls[l], f"l{l}")
        plan.add_grads(l, "ffn", gbf)
        dy, gbm, gsm, rode = _mixer_bwd(dmid, svm, tables, wm, smalls[l], f"l{l}", lambda gb, l=l: plan.bwd_ride(l))
        plan.bwd_done(l, rode)
        plan.add_grads(l, "mix", gbm)
        small[l] = {**gsf, **gsm}
    return sq, dy, small


def kernel(x, positions, mix_norm_pre, w_in, q_norm, w_uq, kv_norm, w_uk, w_uv, w_attn_o, conv_w, conv_b, conv_ln_g, conv_ln_b, w_conv_o, pool_w, pool_scale, w_pool_o, w_mix_o, mix_norm_post, ffn_norm_pre, w_gate, w_up, w_down, ffn_norm_post, loss_target, m_mix_norm_pre, m_w_in, m_q_norm, m_w_uq, m_kv_norm, m_w_uk, m_w_uv, m_w_attn_o, m_conv_w, m_conv_b, m_conv_ln_g, m_conv_ln_b, m_w_conv_o, m_pool_w, m_pool_scale, m_w_pool_o, m_w_mix_o, m_mix_norm_post, m_ffn_norm_pre, m_w_gate, m_w_up, m_w_down, m_ffn_norm_post, v_mix_norm_pre, v_w_in, v_q_norm, v_w_uq, v_kv_norm, v_w_uk, v_w_uv, v_w_attn_o, v_conv_w, v_conv_b, v_conv_ln_g, v_conv_ln_b, v_w_conv_o, v_pool_w, v_pool_scale, v_w_pool_o, v_w_mix_o, v_mix_norm_post, v_ffn_norm_pre, v_w_gate, v_w_up, v_w_down, v_ffn_norm_post):
    given = dict(locals())
    dev = 4 * lax.axis_index("x") + 2 * lax.axis_index("y") + lax.axis_index("c")

    plan = _Plan({n: given[n] for n in BIG}, conv_w)
    cw = CONV_C // N_DEV
    conv_w_full = plan.gather_first().transpose(1, 2, 0, 3).reshape(DEPTH, CONV_W, CONV_C)
    smalls = []
    for l in range(DEPTH):
        sm = {n: given[n][l] for n in SMALL if n != "conv_w"}
        sm["conv_w"] = _pad_axis(conv_w_full[l], 0, CONV_HALO)
        smalls.append(sm)

    sq, grad_x, small = _local_step(x[0], positions[0], loss_target[0], smalls, plan)
    loss = lax.psum(0.5 / D_MODEL * jnp.sum(sq), ("x", "y", "c"))
    per_layer = plan.finish()
    views = {}
    for n in BIG:
        if n == "w_in":
            views[n] = jnp.stack([per_layer[l][n].T for l in range(DEPTH)], axis=1)
        elif n in LANE_MAJOR:
            views[n] = jnp.stack([per_layer[l][n].T for l in range(DEPTH)])
        else:
            views[n] = jnp.stack([per_layer[l][n] for l in range(DEPTH)])
    grads = {n: _from_lane_major(n, views[n]) for n in BIG}

    small_groups = _all_gather(_small_groups(small), "gather_small_grads")
    small_sum = _small_from_groups([_sum_blocks(g, f"sum_small_grads_{i}") for i, g in enumerate(small_groups)])
    for n in SMALL:
        grads[n] = small_sum[n]
    grads["conv_w"] = lax.dynamic_slice_in_dim(small_sum["conv_w"], dev * cw, cw, axis=2)

    delta, new_m, new_v = {}, {}, {}
    for n in WEIGHTS:
        g_view = views[n] if n in views else grads[n]
        w_view, m_view, v_view = [_lane_major(n, given[k]) for k in (n, "m_" + n, "v_" + n)]
        res = _adamw(w_view, g_view, m_view, v_view, f"adamw_{n}")
        delta[n], new_m[n], new_v[n] = [_from_lane_major(n, r) for r in res]
    return (loss, grad_x[None], *[grads[n] for n in WEIGHTS], *[delta[n] for n in WEIGHTS],
            *[new_m[n] for n in WEIGHTS], *[new_v[n] for n in WEIGHTS])
```

```python
import functools
import math

import jax
import jax.numpy as jnp
from jax import lax
from jax.experimental import pallas as pl
from jax.experimental.pallas import tpu as pltpu

F32, BF16 = jnp.float32, jnp.bfloat16
MESH = pl.DeviceIdType.MESH

LANES = 128
SUBLANES = 8
VMEM_LIMIT_BYTES = 56 * 1024 * 1024
MATMUL_VMEM_BYTES = 40 * 1024 * 1024

N_DEV = 8
D_MODEL = 1024
DEPTH = 2
N_HEADS = 8
QK_NOPE, QK_ROPE, V_HEAD = 64, 32, 64
HEAD_PAD = LANES
Q_RANK, KV_RANK = 384, 256
ROPE_THETA = 10000.0
CONV_C, CONV_W = 512, 31
CONV_HALO = 32
POOL_WINDOWS = (2, 4, 8, 16)
POOL_C, POOL_G = 512, 4
POOL_GD = POOL_C // POOL_G
D_FF = 2816
FF_SHARD = D_FF // N_DEV
FF_SHARD_PAD = 3 * LANES
D_FF_PAD = N_DEV * FF_SHARD_PAD
W_IN_SHARD = 660
EPS = 1e-6
ATTN_SCALE = 1.0 / math.sqrt(QK_NOPE + QK_ROPE)
LOG2E = 1.4426950408889634
LR, B1, B2, ADAM_EPS, WD, STEP = 0.001, 0.9, 0.999, 1e-08, 0.01, 10

Z_W = 5376
ZC_GATE = (1024, 0)
ZC_GATES = (3072, 0)
ZC_CONV_A = (512, 6)
ZC_CONV_G = (512, 7)
ZC_CONV = (1024, 3)
ZC_POOL = (512, 8)
ZC_Q = (384, 12)
ZC_KR = (128, 39)
ZC_KV = (256, 20)
W_IN_PIECES = ((0, 384, 4608), (384, 640, 5120), (640, 672, 5056), (672, 1696, 3072), (1696, 2208, 4096),
               (2208, 5280, 0))

BIG = ("w_in", "w_uq", "w_uk", "w_uv", "w_attn_o", "w_conv_o", "w_pool_o", "w_mix_o", "w_gate", "w_up", "w_down")
SMALL = ("mix_norm_pre", "q_norm", "kv_norm", "conv_w", "conv_b", "conv_ln_g", "conv_ln_b", "pool_w", "pool_scale",
         "mix_norm_post", "ffn_norm_pre", "ffn_norm_post")
WEIGHTS = ("mix_norm_pre", "w_in", "q_norm", "w_uq", "kv_norm", "w_uk", "w_uv", "w_attn_o", "conv_w", "conv_b",
           "conv_ln_g", "conv_ln_b", "w_conv_o", "pool_w", "pool_scale", "w_pool_o", "w_mix_o", "mix_norm_post",
           "ffn_norm_pre", "w_gate", "w_up", "w_down", "ffn_norm_post")


def _params(*semantics):
    return pltpu.CompilerParams(dimension_semantics=semantics, vmem_limit_bytes=VMEM_LIMIT_BYTES)


def _tile(dim, cap):
    if dim <= cap:
        return dim
    for t in range(cap - cap % LANES, 0, -LANES):
        if dim % t == 0:
            return t
    raise ValueError(f"no tile for {dim} under {cap}")


def _row_tile(rows, row_bytes, budget=1 << 20):
    if rows * row_bytes <= budget:
        return rows
    cap = max(16, budget // row_bytes)
    for t in range(cap - cap % 16, 0, -16):
        if rows % t == 0:
            return t
    return rows


def _rows(ts, width, cidx=0):
    return pl.BlockSpec((ts, width), lambda i: (i, cidx))


def _fixed(shape):
    return pl.BlockSpec(shape, lambda *_: (0,) * len(shape))


def _sigmoid(x):
    return 1.0 / (1.0 + jnp.exp(-x))


def _matmul(a, b, mode, out_dtype, name, add=None, blocked=False):
    nb = n_blk = 0
    blocked = blocked or b.ndim == 3
    if mode == "nn":
        (m, k) = a.shape
        n = b.shape[0] * b.shape[2] if blocked else b.shape[1]
    elif mode == "nt":
        (m, k) = a.shape
        n = b.shape[1] if blocked else b.shape[0]
    else:
        (k, m), n = a.shape, b.shape[1]
    if blocked:
        nb = b.shape[2] if mode != "tn" else n // N_DEV
    unit = nb if blocked and mode != "nt" else LANES
    out_bytes = jnp.dtype(out_dtype).itemsize + (4 if add is not None else 0)
    best = None
    for tn_c in range(unit, min(n, 1536) + 1, unit):
        for tm_c in sorted({256, 512, 1024, 2048, min(m, 2048)}):
            if n % tn_c or m % tm_c or (blocked and mode != "nt" and N_DEV % (tn_c // nb)):
                continue
            vmem = 2 * (tm_c * k * 2 + tn_c * k * 2 + tm_c * tn_c * out_bytes) + tm_c * tn_c * 4 + tn_c * k * 2
            if vmem <= MATMUL_VMEM_BYTES and (best is None or tm_c * tn_c / (tm_c + tn_c) > best[0]):
                best = (tm_c * tn_c / (tm_c + tn_c), tm_c, tn_c)
    if best is None:
        raise ValueError(f"{name}: no tiles for {m}x{n}x{k}")
    _, tm, tn = best
    if blocked:
        n_blk = N_DEV if mode == "nt" else tn // nb
    dims = {"nn": ((1,), (0,)), "nt": ((1,), (1,)), "tn": ((0,), (0,))}[mode]
    a_spec = pl.BlockSpec((k, tm), lambda i, j: (0, i)) if mode == "tn" else pl.BlockSpec((tm, k), lambda i, j: (i, 0))
    b_spec = pl.BlockSpec((tn, k), lambda i, j: (j, 0)) if mode == "nt" else pl.BlockSpec((k, tn), lambda i, j: (0, j))
    o_spec = pl.BlockSpec((tm, tn), lambda i, j: (i, j))
    out_shape = jax.ShapeDtypeStruct((m, n), out_dtype)
    if blocked and mode == "nn":
        b_spec = pl.BlockSpec((n_blk, k, nb), lambda i, j: (j, 0, 0))
    elif blocked and mode == "nt":
        b_spec = pl.BlockSpec((n_blk, tn, nb), lambda i, j: (0, j, 0))
    elif blocked:
        o_spec = pl.BlockSpec((n_blk, tm, nb), lambda i, j: (j, i, 0))
        out_shape = jax.ShapeDtypeStruct((N_DEV, m, nb), out_dtype)
    has_add = add is not None

    def body(a_ref, b_ref, *rest):
        o_ref = rest[-1]
        if blocked and mode != "tn":
            bv = jnp.concatenate([b_ref[c] for c in range(n_blk)], axis=1) if n_blk > 1 else b_ref[0]
        else:
            bv = b_ref[...]
        total = lax.dot_general(a_ref[...], bv, (dims, ((), ())), preferred_element_type=F32)
        if has_add:
            total = total + rest[0][...]
        if blocked and mode == "tn":
            for c in range(n_blk):
                o_ref[c] = total[:, c * nb:(c + 1) * nb].astype(o_ref.dtype)
        else:
            o_ref[...] = total.astype(o_ref.dtype)

    operands = (a, b, add) if has_add else (a, b)
    return pl.pallas_call(
        body, name=name, out_shape=out_shape, grid=(m // tm, n // tn),
        in_specs=[a_spec, b_spec] + ([o_spec] if has_add else []), out_specs=o_spec,
        compiler_params=_params("parallel", "parallel"))(*operands)


def _rms_fwd(x, win, gain, out_dtype, name, res=None):
    width, cidx = win
    s = x.shape[0]
    ts = min(s, 512)
    has_res = res is not None

    def body(x_ref, g_ref, *rest):
        o_ref = rest[-1]
        xv = x_ref[...]
        r = lax.rsqrt(jnp.mean(xv * xv, axis=-1, keepdims=True) + EPS)
        y = (xv * r) * g_ref[...]
        if has_res:
            y = rest[0][...] + y
        o_ref[...] = y.astype(o_ref.dtype)

    ops = (x, gain.reshape(1, width)) + ((res,) if has_res else ())
    return pl.pallas_call(
        body, name=name, out_shape=jax.ShapeDtypeStruct((s, width), out_dtype), grid=(s // ts,),
        in_specs=[_rows(ts, width, cidx), _fixed((1, width))] + ([_rows(ts, width)] if has_res else []),
        out_specs=_rows(ts, width), compiler_params=_params("parallel"))(*ops)


def _into(dz, n_inputs, out_index):
    return dict(in_specs=[ANY], operands=(dz,), input_output_aliases={n_inputs: out_index},
                out_shape=jax.ShapeDtypeStruct(dz.shape, dz.dtype))


def _rms_bwd(x, win, gain, dy, out_dtype, name, add=None, dz=None):
    width, cidx = win
    s = x.shape[0]
    ts = min(s, 512)
    has_add = add is not None

    def body(x_ref, g_ref, dy_ref, *rest):
        dx_ref, dg_ref = rest[-2], rest[-1]
        xv = x_ref[...]
        r = lax.rsqrt(jnp.mean(xv * xv, axis=-1, keepdims=True) + EPS)
        xh = xv * r
        dyv = dy_ref[...].astype(F32)
        dyg = dyv * g_ref[...]
        dx = r * (dyg - xh * jnp.mean(dyg * xh, axis=-1, keepdims=True))
        if has_add:
            dx = dx + rest[0][...]
        dx_ref[...] = dx.astype(dx_ref.dtype)

        @pl.when(pl.program_id(0) == 0)
        def _():
            dg_ref[...] = jnp.zeros_like(dg_ref)

        dg_ref[...] += jnp.sum(dyv * xh, axis=0, keepdims=True)

    ops = (x, gain.reshape(1, width), dy) + ((add,) if has_add else ())
    in_specs = [_rows(ts, width, cidx), _fixed((1, width)), _rows(ts, width)] + ([_rows(ts, width)] if has_add else [])
    dx_shape, dx_spec, alias = jax.ShapeDtypeStruct((s, width), out_dtype), _rows(ts, width), {}
    if dz is not None:
        into = _into(dz, len(ops), 0)
        ops, in_specs, alias = ops + into["operands"], in_specs + into["in_specs"], into["input_output_aliases"]
        dx_shape, dx_spec = into["out_shape"], _rows(ts, width, cidx)
    dx, dg = pl.pallas_call(
        body, name=name, out_shape=(dx_shape, jax.ShapeDtypeStruct((1, width), F32)), grid=(s // ts,),
        in_specs=in_specs, out_specs=(dx_spec, _fixed((1, width))), input_output_aliases=alias,
        compiler_params=_params("arbitrary"))(*ops)
    return dx, dg.reshape(width)


def _rope(x, c, s1, s2):
    return x * c + pltpu.roll(x, 16, 1) * s1 + pltpu.roll(x, LANES - 16, 1) * s2


def _rope_t(g, c, s1, s2):
    return g * c + pltpu.roll(g * s1, LANES - 16, 1) + pltpu.roll(g * s2, 16, 1)


def _rope_tables(positions):
    inv_freq = ROPE_THETA ** (-jnp.arange(0, QK_ROPE, 2, dtype=F32) / QK_ROPE)
    ang = positions.astype(F32)[:, None] * inv_freq
    cos, sin = jnp.cos(ang), jnp.sin(ang)
    n = positions.shape[0]
    one, zero = jnp.ones((n, 1), F32), jnp.zeros((n, 1), F32)
    c = jnp.concatenate([jnp.tile(one, (1, QK_NOPE)), cos, cos, jnp.tile(one, (1, 32))], axis=1)
    s1 = jnp.concatenate([jnp.tile(zero, (1, QK_NOPE + 16)), sin, jnp.tile(zero, (1, 32))], axis=1)
    s2 = jnp.concatenate([jnp.tile(zero, (1, QK_NOPE)), -sin, jnp.tile(zero, (1, 48))], axis=1)
    return c, s1, s2


def _rope_qk_fwd(qf, kf, z, tables, name):
    s = qf.shape[0]
    ts = min(s, 256)
    hw = N_HEADS * HEAD_PAD

    def body(qf_ref, kf_ref, kr_ref, c_ref, s1_ref, s2_ref, q_ref, k_ref):
        c, s1, s2 = c_ref[...], s1_ref[...], s2_ref[...]
        kr = _rope(kr_ref[...], c, s1, s2)
        for h in range(N_HEADS):
            sl = slice(h * HEAD_PAD, (h + 1) * HEAD_PAD)
            q_ref[:, sl] = _rope(qf_ref[:, sl], c, s1, s2).astype(BF16)
            k_ref[:, sl] = (kf_ref[:, sl] + kr).astype(BF16)

    tab = _rows(ts, LANES)
    return pl.pallas_call(
        body, name=name, out_shape=(jax.ShapeDtypeStruct((s, hw), BF16),) * 2, grid=(s // ts,),
        in_specs=[_rows(ts, hw), _rows(ts, hw), _rows(ts, *ZC_KR), tab, tab, tab],
        out_specs=(_rows(ts, hw), _rows(ts, hw)), compiler_params=_params("parallel"))(qf, kf, z, *tables)


def _rope_qk_bwd(dq, dk, tables, dz, name):
    s = dq.shape[0]
    ts = min(s, 256)
    hw = N_HEADS * HEAD_PAD

    def body(dq_ref, dk_ref, c_ref, s1_ref, s2_ref, _, dqf_ref, dkf_ref, dkr_ref):
        c, s1, s2 = c_ref[...], s1_ref[...], s2_ref[...]
        ksum = jnp.zeros((ts, HEAD_PAD), F32)
        for h in range(N_HEADS):
            sl = slice(h * HEAD_PAD, (h + 1) * HEAD_PAD)
            dqf_ref[:, sl] = _rope_t(dq_ref[:, sl], c, s1, s2).astype(BF16)
            dkh = dk_ref[:, sl]
            dkf_ref[:, sl] = dkh.astype(BF16)
            ksum = ksum + dkh
        lane = lax.broadcasted_iota(jnp.int32, (ts, HEAD_PAD), 1)
        in_rope = (lane >= QK_NOPE) & (lane < QK_NOPE + QK_ROPE)
        dkr_ref[...] = jnp.where(in_rope, _rope_t(ksum, c, s1, s2), 0.0).astype(BF16)

    tab = _rows(ts, LANES)
    into = _into(dz, 5, 2)
    return pl.pallas_call(
        body, name=name,
        out_shape=(jax.ShapeDtypeStruct((s, hw), BF16), jax.ShapeDtypeStruct((s, hw), BF16), into["out_shape"]),
        grid=(s // ts,), in_specs=[_rows(ts, hw), _rows(ts, hw), tab, tab, tab] + into["in_specs"],
        out_specs=(_rows(ts, hw), _rows(ts, hw), _rows(ts, *ZC_KR)), input_output_aliases=into["input_output_aliases"],
        compiler_params=_params("parallel"))(dq, dk, *tables, dz)


def _attn_tile(s):
    return min(s, 512)


def _raw_scores(q, k, masked, row0=0):
    sc = lax.dot_general(q, k, (((1,), (1,)), ((), ())), preferred_element_type=F32)
    if masked:
        rows = row0 + lax.broadcasted_iota(jnp.int32, sc.shape, 0)
        cols = lax.broadcasted_iota(jnp.int32, sc.shape, 1)
        sc = jnp.where(cols <= rows, sc, -jnp.inf)
    return sc


def _ride_hooks(ride, refs, n_in, n_out, grid):
    if ride is None:
        return refs, lambda: None, lambda: None
    n = len(ride.arrays)
    own = refs[:n_in] + refs[n_in + n:n_in + n + n_out]
    ins, outs, sems = refs[n_in:n_in + n], refs[n_in + n + n_out:n_in + 2 * n + n_out], refs[n_in + 2 * n + n_out:]
    at_first = functools.reduce(lambda a, b: a & b, [pl.program_id(ax) == 0 for ax in range(len(grid))])
    at_last = functools.reduce(lambda a, b: a & b, [pl.program_id(ax) == g - 1 for ax, g in enumerate(grid)])
    return own, lambda: pl.when(at_first)(lambda: ride.start(ins, outs, sems)), \
        lambda: pl.when(at_last)(lambda: ride.finish(ins, outs, sems))


def _ride_call(ride, body, name, out_shape, grid, in_specs, out_specs, semantics, operands):
    n = 0 if ride is None else len(ride.arrays)
    res = pl.pallas_call(
        body, name=name, out_shape=tuple(out_shape) + (tuple(ride.out_shape) if n else ()), grid=grid,
        in_specs=list(in_specs) + [ANY] * n, out_specs=tuple(out_specs) + (ANY,) * n,
        scratch_shapes=list(ride.scratch) if n else [],
        compiler_params=_params(*(("arbitrary",) * len(grid) if n else semantics)))(*operands, *(ride.arrays if n else ()))
    return res[:len(out_shape)], list(res[len(out_shape):])


def _flash_fwd(q, k, v, name, ride=None):
    s = q.shape[0]
    t = _attn_tile(s)
    c2 = ATTN_SCALE * LOG2E
    grid = (N_HEADS, s // t)

    def body(*refs):
        (q_ref, k_ref, v_ref, o_ref, lse_ref), start, finish = _ride_hooks(ride, refs, 3, 2, grid)
        start()
        i = pl.program_id(1)
        qv = q_ref[...]

        def chunk(j, carry, masked):
            m_old, l_old, acc = carry
            at = pl.ds(pl.multiple_of(j * t, t), t)
            sc = _raw_scores(qv, k_ref[at, :], masked)
            m_new = jnp.maximum(m_old, jnp.max(sc, axis=-1, keepdims=True))
            p = jnp.exp2((sc - m_new) * c2)
            alpha = jnp.exp2((m_old - m_new) * c2)
            l_new = alpha * l_old + jnp.sum(p, axis=-1, keepdims=True)
            acc = alpha * acc + jnp.dot(p.astype(BF16), v_ref[at, :], preferred_element_type=F32)
            return m_new, l_new, acc

        init = (jnp.full((t, 1), -jnp.inf, F32), jnp.zeros((t, 1), F32), jnp.zeros((t, HEAD_PAD), F32))
        carry = lax.fori_loop(0, i, lambda j, cr: chunk(j, cr, False), init)
        m_fin, l_fin, acc = chunk(i, carry, True)
        o_ref[...] = (acc / l_fin).astype(o_ref.dtype)
        lse_ref[...] = jnp.broadcast_to(m_fin * ATTN_SCALE + jnp.log(l_fin), (t, HEAD_PAD))
        finish()

    qo = pl.BlockSpec((t, HEAD_PAD), lambda h, i: (i, h))
    whole = pl.BlockSpec((s, HEAD_PAD), lambda h, i: (0, h))
    return _ride_call(
        ride, body, name, (jax.ShapeDtypeStruct(q.shape, BF16), jax.ShapeDtypeStruct(q.shape, F32)), grid,
        [qo, whole, whole], (qo, qo), ("parallel", "parallel"), (q, k, v))


def _attn_delta(do, o, name):
    s = o.shape[0]
    t = _attn_tile(s)

    def body(do_ref, o_ref, delta_ref, dob_ref):
        for h in range(N_HEADS):
            sl = slice(h * HEAD_PAD, (h + 1) * HEAD_PAD)
            dov = do_ref[:, sl]
            delta_ref[:, sl] = jnp.broadcast_to(jnp.sum(dov * o_ref[:, sl].astype(F32), axis=-1, keepdims=True),
                                                (t, HEAD_PAD))
            dob_ref[:, sl] = dov.astype(BF16)

    blk = _rows(t, N_HEADS * HEAD_PAD)
    return pl.pallas_call(
        body, name=name, out_shape=(jax.ShapeDtypeStruct(o.shape, F32), jax.ShapeDtypeStruct(o.shape, BF16)),
        grid=(s // t,), in_specs=[blk, blk], out_specs=(blk, blk), compiler_params=_params("parallel"))(do, o)


def _flash_bwd(q, k, v, do, lse, delta, name, ride=None):
    s = q.shape[0]
    t = _attn_tile(s)
    nt = s // t
    c2 = ATTN_SCALE * LOG2E
    grid = (N_HEADS, nt)

    def body(*refs):
        (q_ref, k_ref, v_ref, do_ref, lse_ref, delta_ref, dq_ref, dk_ref, dv_ref), start, finish = _ride_hooks(
            ride, refs, 6, 3, grid)
        start()
        j = pl.program_id(1)
        kv, vv = k_ref[...], v_ref[...]

        @pl.when(j == 0)
        def _():
            dq_ref[...] = jnp.zeros_like(dq_ref)

        def chunk(i, carry, masked):
            dk_acc, dv_acc = carry
            at = pl.ds(pl.multiple_of(i * t, t), t)
            qi, doi = q_ref[at, :], do_ref[at, :]
            sc = _raw_scores(qi, kv, masked)
            p = jnp.exp2(sc * c2 - lse_ref[at, pl.ds(0, 1)] * LOG2E)
            dp = lax.dot_general(doi, vv, (((1,), (1,)), ((), ())), preferred_element_type=F32)
            ds = (p * (dp - delta_ref[at, pl.ds(0, 1)])).astype(BF16)
            dv_acc = dv_acc + lax.dot_general(p.astype(BF16), doi, (((0,), (0,)), ((), ())), preferred_element_type=F32)
            dk_acc = dk_acc + lax.dot_general(ds, qi, (((0,), (0,)), ((), ())), preferred_element_type=F32)
            dq_ref[at, :] += jnp.dot(ds, kv, preferred_element_type=F32) * ATTN_SCALE
            return dk_acc, dv_acc

        zero = jnp.zeros((t, HEAD_PAD), F32)
        carry = chunk(j, (zero, zero), True)
        dk_acc, dv_acc = lax.fori_loop(j + 1, nt, lambda i, cr: chunk(i, cr, False), carry)
        dk_ref[...] = dk_acc * ATTN_SCALE
        dv_ref[...] = dv_acc.astype(BF16)
        finish()

    blk = pl.BlockSpec((t, HEAD_PAD), lambda h, j: (j, h))
    whole = pl.BlockSpec((s, HEAD_PAD), lambda h, j: (0, h))
    return _ride_call(
        ride, body, name, (jax.ShapeDtypeStruct(q.shape, F32), jax.ShapeDtypeStruct(q.shape, F32),
                           jax.ShapeDtypeStruct(q.shape, BF16)), grid,
        [whole, blk, blk, whole, whole, whole], (whole, blk, blk), ("parallel", "arbitrary"), (q, k, v, do, lse, delta))


def _conv_tile(s):
    return min(s, 256)


def _halo_before(t, width, cidx):
    per = t // CONV_HALO
    return pl.BlockSpec((CONV_HALO, width), lambda i: (jnp.maximum(i * per - 1, 0), cidx))


def _halo_after(t, width, cidx, n_tiles):
    per = t // CONV_HALO
    last = n_tiles * per - 1
    return pl.BlockSpec((CONV_HALO, width), lambda i: (jnp.minimum((i + 1) * per, last), cidx))


def _fill_glu(hbuf, ap_ref, gp_ref, a_ref, g_ref, t):
    first = pl.program_id(0) == 0
    hbuf[pl.ds(0, CONV_HALO), :] = jnp.where(first, 0.0, ap_ref[...] * _sigmoid(gp_ref[...]))
    hbuf[pl.ds(CONV_HALO, t), :] = a_ref[...] * _sigmoid(g_ref[...])


def _phase_copies(dst, src, t):
    n = t + CONV_HALO - SUBLANES
    for s in range(1, SUBLANES):
        dst[s, pl.ds(0, n), :] = src[pl.ds(s, n), :]


def _window(phases, src, k, t):
    if k % SUBLANES == 0:
        return src[pl.ds(k, t), :]
    return phases[k % SUBLANES, pl.ds(k - k % SUBLANES, t), :]


def _layer_norm_parts(co):
    mu = jnp.mean(co, axis=-1, keepdims=True)
    xc = co - mu
    rstd = lax.rsqrt(jnp.mean(xc * xc, axis=-1, keepdims=True) + EPS)
    return xc * rstd, rstd


def _conv_fwd(z, conv_w, conv_b, ln_g, ln_b, name):
    s = z.shape[0]
    t = _conv_tile(s)
    off = CONV_HALO - (CONV_W - 1)

    def body(ap_ref, gp_ref, a_ref, g_ref, w_ref, b_ref, lg_ref, lb_ref, hc_ref, co_ref, hbuf, hph):
        _fill_glu(hbuf, ap_ref, gp_ref, a_ref, g_ref, t)
        _phase_copies(hph, hbuf, t)
        acc = jnp.zeros((t, CONV_C), F32) + b_ref[...]
        for j in range(CONV_W):
            acc = acc + _window(hph, hbuf, off + j, t) * w_ref[pl.ds(j, 1), :]
        co_ref[...] = acc
        xh, _ = _layer_norm_parts(acc)
        y = xh * lg_ref[...] + lb_ref[...]
        hc_ref[...] = (y * _sigmoid(y)).astype(BF16)

    vec = _fixed((1, CONV_C))
    return pl.pallas_call(
        body, name=name, out_shape=(jax.ShapeDtypeStruct((s, CONV_C), BF16), jax.ShapeDtypeStruct((s, CONV_C), F32)),
        grid=(s // t,),
        in_specs=[_halo_before(t, *ZC_CONV_A), _halo_before(t, *ZC_CONV_G), _rows(t, *ZC_CONV_A), _rows(t, *ZC_CONV_G),
                  _fixed((CONV_HALO, CONV_C)), vec, vec, vec],
        out_specs=(_rows(t, CONV_C), _rows(t, CONV_C)),
        scratch_shapes=[pltpu.VMEM((t + CONV_HALO, CONV_C), F32), pltpu.VMEM((SUBLANES, t + CONV_HALO, CONV_C), F32)],
        compiler_params=_params("parallel"))(z, z, z, z, conv_w, conv_b.reshape(1, -1), ln_g.reshape(1, -1),
                                             ln_b.reshape(1, -1))


def _conv_bwd_norm(dhc, co, ln_g, ln_b, name):
    s = co.shape[0]
    t = min(s, 512)

    def body(dhc_ref, co_ref, lg_ref, lb_ref, dco_ref, dg_ref, db_ref, dcb_ref):
        xh, rstd = _layer_norm_parts(co_ref[...])
        y = xh * lg_ref[...] + lb_ref[...]
        sg = _sigmoid(y)
        dy = dhc_ref[...] * (sg * (1.0 + y * (1.0 - sg)))
        dxh = dy * lg_ref[...]
        dco = rstd * (dxh - jnp.mean(dxh, axis=-1, keepdims=True) - xh * jnp.mean(dxh * xh, axis=-1, keepdims=True))
        dco_ref[...] = dco

        @pl.when(pl.program_id(0) == 0)
        def _():
            dg_ref[...] = jnp.zeros_like(dg_ref)
            db_ref[...] = jnp.zeros_like(db_ref)
            dcb_ref[...] = jnp.zeros_like(dcb_ref)

        dg_ref[...] += jnp.sum(dy * xh, axis=0, keepdims=True)
        db_ref[...] += jnp.sum(dy, axis=0, keepdims=True)
        dcb_ref[...] += jnp.sum(dco, axis=0, keepdims=True)

    vec = _fixed((1, CONV_C))
    one = jax.ShapeDtypeStruct((1, CONV_C), F32)
    dco, dg, db, dcb = pl.pallas_call(
        body, name=name, out_shape=(jax.ShapeDtypeStruct((s, CONV_C), F32), one, one, one), grid=(s // t,),
        in_specs=[_rows(t, CONV_C), _rows(t, CONV_C), vec, vec], out_specs=(_rows(t, CONV_C), vec, vec, vec),
        compiler_params=_params("arbitrary"))(dhc, co, ln_g.reshape(1, -1), ln_b.reshape(1, -1))
    return dco, dg.reshape(-1), db.reshape(-1), dcb.reshape(-1)


def _conv_bwd_taps(dco, z, conv_w, dz, name):
    s = z.shape[0]
    t = _conv_tile(s)
    nt = s // t
    off = CONV_HALO - (CONV_W - 1)

    def body(ap_ref, gp_ref, a_ref, g_ref, d_ref, dn_ref, w_ref, _, du_ref, dw_ref, hbuf, dbuf, hph, dph):
        i = pl.program_id(0)
        _fill_glu(hbuf, ap_ref, gp_ref, a_ref, g_ref, t)
        dbuf[pl.ds(0, t), :] = d_ref[...]
        dbuf[pl.ds(t, CONV_HALO), :] = jnp.where(i == nt - 1, 0.0, dn_ref[...])
        _phase_copies(hph, hbuf, t)
        _phase_copies(dph, dbuf, t)

        @pl.when(i == 0)
        def _():
            dw_ref[...] = jnp.zeros_like(dw_ref)

        dcur = d_ref[...]
        dh = jnp.zeros((t, CONV_C), F32)
        for j in range(CONV_W):
            dh = dh + _window(dph, dbuf, CONV_W - 1 - j, t) * w_ref[pl.ds(j, 1), :]
            dw_ref[pl.ds(j, 1), :] += jnp.sum(dcur * _window(hph, hbuf, off + j, t), axis=0, keepdims=True)
        a, sg = a_ref[...], _sigmoid(g_ref[...])
        du_ref[:, pl.ds(0, CONV_C)] = (dh * sg).astype(BF16)
        du_ref[:, pl.ds(CONV_C, CONV_C)] = (dh * a * sg * (1.0 - sg)).astype(BF16)

    into = _into(dz, 7, 0)
    return pl.pallas_call(
        body, name=name, out_shape=(into["out_shape"], jax.ShapeDtypeStruct((CONV_HALO, CONV_C), F32)), grid=(nt,),
        in_specs=[_halo_before(t, *ZC_CONV_A), _halo_before(t, *ZC_CONV_G), _rows(t, *ZC_CONV_A), _rows(t, *ZC_CONV_G),
                  _rows(t, CONV_C), _halo_after(t, CONV_C, 0, nt), _fixed((CONV_HALO, CONV_C))] + into["in_specs"],
        out_specs=(_rows(t, *ZC_CONV), _fixed((CONV_HALO, CONV_C))), input_output_aliases=into["input_output_aliases"],
        scratch_shapes=[pltpu.VMEM((t + CONV_HALO, CONV_C), F32), pltpu.VMEM((t + CONV_HALO, CONV_C), F32),
                        pltpu.VMEM((SUBLANES, t + CONV_HALO, CONV_C), F32),
                        pltpu.VMEM((SUBLANES, t + CONV_HALO, CONV_C), F32)],
        compiler_params=_params("arbitrary"))(z, z, z, z, dco, dco, conv_w, dz)


def _pool_tile(s):
    return min(s, 512)


def _pool_counts(row0, n, window):
    rows = row0 + lax.broadcasted_iota(jnp.int32, (n, POOL_GD), 0)
    return jnp.minimum(rows + 1, window).astype(F32)


def _pool_diff(ubuf, gi, window, row0, t):
    lanes = pl.ds(gi * POOL_GD, POOL_GD)
    tot = ubuf[pl.ds(CONV_HALO, t), lanes]
    cur = tot
    for back in range(1, window):
        tot = tot + ubuf[pl.ds(CONV_HALO - back, t), lanes]
    return tot / _pool_counts(row0, t, window) - cur


def _pool_fwd(z, pool_w, pool_scale, name):
    s = z.shape[0]
    t = _pool_tile(s)

    def body(up_ref, u_ref, w_ref, sc_ref, m_ref, ubuf):
        i = pl.program_id(0)
        ubuf[pl.ds(0, CONV_HALO), :] = jnp.where(i == 0, 0.0, up_ref[...])
        ubuf[pl.ds(CONV_HALO, t), :] = u_ref[...]
        for gi, window in enumerate(POOL_WINDOWS):
            d = _pool_diff(ubuf, gi, window, i * t, t)
            mm = jnp.dot(d.astype(BF16), w_ref[gi].astype(BF16), preferred_element_type=F32)
            lanes = pl.ds(gi * POOL_GD, POOL_GD)
            m_ref[:, lanes] = (mm * sc_ref[:, lanes]).astype(BF16)

    return pl.pallas_call(
        body, name=name, out_shape=jax.ShapeDtypeStruct((s, POOL_C), BF16), grid=(s // t,),
        in_specs=[_halo_before(t, *ZC_POOL), _rows(t, *ZC_POOL), _fixed((POOL_G, POOL_GD, POOL_GD)), _fixed((1, POOL_C))],
        out_specs=_rows(t, POOL_C), scratch_shapes=[pltpu.VMEM((t + CONV_HALO, POOL_C), F32)],
        compiler_params=_params("parallel"))(z, z, pool_w, pool_scale.reshape(1, -1))


def _pool_bwd(dm, z, pool_w, pool_scale, dz, name):
    s = z.shape[0]
    t = _pool_tile(s)
    nt = s // t

    def body(up_ref, u_ref, dm_ref, dmn_ref, w_ref, sc_ref, _, du_ref, dw_ref, dsc_ref, ubuf, ebuf):
        i = pl.program_id(0)
        ubuf[pl.ds(0, CONV_HALO), :] = jnp.where(i == 0, 0.0, up_ref[...])
        ubuf[pl.ds(CONV_HALO, t), :] = u_ref[...]

        @pl.when(i == 0)
        def _():
            dw_ref[...] = jnp.zeros_like(dw_ref)
            dsc_ref[...] = jnp.zeros_like(dsc_ref)

        dm_next = jnp.where(i == nt - 1, 0.0, dmn_ref[...])
        for gi, window in enumerate(POOL_WINDOWS):
            lanes = pl.ds(gi * POOL_GD, POOL_GD)
            wb = w_ref[gi].astype(BF16)
            scale = sc_ref[:, lanes]
            d = _pool_diff(ubuf, gi, window, i * t, t).astype(BF16)
            mm = jnp.dot(d, wb, preferred_element_type=F32)
            dmv = dm_ref[:, lanes]
            dsc_ref[:, lanes] += jnp.sum(dmv * mm, axis=0, keepdims=True)
            dmm = (dmv * scale).astype(BF16)
            dw_ref[gi] += lax.dot_general(d, dmm, (((0,), (0,)), ((), ())), preferred_element_type=F32)
            dd = lax.dot_general(dmm, wb, (((1,), (1,)), ((), ())), preferred_element_type=F32)
            dd_next = lax.dot_general((dm_next[:, gi * POOL_GD:(gi + 1) * POOL_GD] * scale).astype(BF16), wb,
                                      (((1,), (1,)), ((), ())), preferred_element_type=F32)
            ebuf[pl.ds(0, t), lanes] = dd / _pool_counts(i * t, t, window)
            ebuf[pl.ds(t, CONV_HALO), lanes] = dd_next / _pool_counts((i + 1) * t, CONV_HALO, window)
            du = -dd
            for ahead in range(window):
                du = du + ebuf[pl.ds(ahead, t), lanes]
            du_ref[:, lanes] = du.astype(BF16)

    into = _into(dz, 6, 0)
    du, dw, dsc = pl.pallas_call(
        body, name=name,
        out_shape=(into["out_shape"], jax.ShapeDtypeStruct((POOL_G, POOL_GD, POOL_GD), F32),
                   jax.ShapeDtypeStruct((1, POOL_C), F32)), grid=(nt,),
        in_specs=[_halo_before(t, *ZC_POOL), _rows(t, *ZC_POOL), _rows(t, POOL_C), _halo_after(t, POOL_C, 0, nt),
                  _fixed((POOL_G, POOL_GD, POOL_GD)), _fixed((1, POOL_C))] + into["in_specs"],
        out_specs=(_rows(t, *ZC_POOL), _fixed((POOL_G, POOL_GD, POOL_GD)), _fixed((1, POOL_C))),
        input_output_aliases=into["input_output_aliases"],
        scratch_shapes=[pltpu.VMEM((t + CONV_HALO, POOL_C), F32), pltpu.VMEM((t + CONV_HALO, POOL_C), F32)],
        compiler_params=_params("arbitrary"))(z, z, dm, dm, pool_w, pool_scale.reshape(1, -1), dz)
    return du, dw, dsc.reshape(-1)


def _gate_specs(ts):
    width, first = ZC_GATE
    return [_rows(ts, width, first + b) for b in range(3)]


def _merge_fwd(z, ys, name):
    s = z.shape[0]
    ts = min(s, 256)

    def body(g0, g1, g2, y0, y1, y2, o_ref):
        o_ref[...] = (_sigmoid(g0[...]) * y0[...] + _sigmoid(g1[...]) * y1[...]
                      + _sigmoid(g2[...]) * y2[...]).astype(BF16)

    return pl.pallas_call(
        body, name=name, out_shape=jax.ShapeDtypeStruct((s, D_MODEL), BF16), grid=(s // ts,),
        in_specs=_gate_specs(ts) + [_rows(ts, D_MODEL)] * 3, out_specs=_rows(ts, D_MODEL),
        compiler_params=_params("parallel"))(z, z, z, *ys)


def _merge_bwd(z, ys, dmerged, name):
    s = z.shape[0]
    ts = min(s, 256)

    def body(g0, g1, g2, y0, y1, y2, dm_ref, dy0, dy1, dy2, dz_ref):
        dmv = dm_ref[...]
        for b, (g_ref, y_ref, dy_ref) in enumerate(((g0, y0, dy0), (g1, y1, dy1), (g2, y2, dy2))):
            sg = _sigmoid(g_ref[...])
            dy_ref[...] = (dmv * sg).astype(BF16)
            dz_ref[:, pl.ds(b * D_MODEL, D_MODEL)] = (dmv * y_ref[...] * sg * (1.0 - sg)).astype(BF16)

    out = jax.ShapeDtypeStruct((s, D_MODEL), BF16)
    return pl.pallas_call(
        body, name=name, out_shape=(out,) * 3 + (jax.ShapeDtypeStruct((s, Z_W), BF16),), grid=(s // ts,),
        in_specs=_gate_specs(ts) + [_rows(ts, D_MODEL)] * 4,
        out_specs=(_rows(ts, D_MODEL),) * 3 + (_rows(ts, *ZC_GATES),),
        compiler_params=_params("parallel"))(z, z, z, *ys, dmerged)


def _ffn_up_fwd(h, w_gate, w_up, name):
    s, d = h.shape
    nb = w_gate.shape[2]
    f = N_DEV * nb
    tm, n_blk = min(s, 1024), 2
    tn = n_blk * nb
    blk = pl.BlockSpec((tm, tn), lambda i, j: (i, j))
    wspec = pl.BlockSpec((n_blk, d, nb), lambda i, j: (j, 0, 0))

    def body(h_ref, wg_ref, wu_ref, hg_ref, hu_ref, act_ref):
        hv = h_ref[...]
        g = jnp.dot(hv, jnp.concatenate([wg_ref[c] for c in range(n_blk)], axis=1), preferred_element_type=F32)
        u = jnp.dot(hv, jnp.concatenate([wu_ref[c] for c in range(n_blk)], axis=1), preferred_element_type=F32)
        hg_ref[...] = g
        hu_ref[...] = u
        act_ref[...] = (g * _sigmoid(g) * u).astype(BF16)

    return pl.pallas_call(
        body, name=name,
        out_shape=(jax.ShapeDtypeStruct((s, f), F32), jax.ShapeDtypeStruct((s, f), F32), jax.ShapeDtypeStruct((s, f), BF16)),
        grid=(s // tm, f // tn), in_specs=[pl.BlockSpec((tm, d), lambda i, j: (i, 0)), wspec, wspec],
        out_specs=(blk, blk, blk), compiler_params=_params("parallel", "parallel"))(h, w_gate, w_up)


def _ffn_down_bwd(dfo, w_down, hg, hu, name):
    s, d = dfo.shape
    f = w_down.shape[0]
    tm, tn = min(s, 1024), _tile(f, 1024)
    blk = pl.BlockSpec((tm, tn), lambda i, j: (i, j))

    def body(d_ref, w_ref, g_ref, u_ref, dg_ref, du_ref):
        dact = lax.dot_general(d_ref[...], w_ref[...], (((1,), (1,)), ((), ())), preferred_element_type=F32)
        g = g_ref[...]
        sg = _sigmoid(g)
        dg_ref[...] = (dact * u_ref[...] * (sg * (1.0 + g * (1.0 - sg)))).astype(BF16)
        du_ref[...] = (dact * g * sg).astype(BF16)

    out = jax.ShapeDtypeStruct((s, f), BF16)
    return pl.pallas_call(
        body, name=name, out_shape=(out, out), grid=(s // tm, f // tn),
        in_specs=[pl.BlockSpec((tm, d), lambda i, j: (i, 0)), pl.BlockSpec((tn, d), lambda i, j: (j, 0)), blk, blk],
        out_specs=(blk, blk), compiler_params=_params("parallel", "parallel"))(dfo, w_down, hg, hu)


def _loss_grad(y, target, name):
    s, d = y.shape
    ts = min(s, 512)

    def body(y_ref, t_ref, dy_ref, sq_ref):
        e = y_ref[...] - t_ref[...]
        dy_ref[...] = e / d

        @pl.when(pl.program_id(0) == 0)
        def _():
            sq_ref[...] = jnp.zeros_like(sq_ref)

        sq_ref[...] += jnp.sum(e * e, axis=0, keepdims=True)

    return pl.pallas_call(
        body, name=name, out_shape=(jax.ShapeDtypeStruct((s, d), F32), jax.ShapeDtypeStruct((1, d), F32)),
        grid=(s // ts,), in_specs=[_rows(ts, d), _rows(ts, d)], out_specs=(_rows(ts, d), _fixed((1, d))),
        compiler_params=_params("arbitrary"))(y, target)


def _adamw(w, g, m, v, name):
    shape = w.shape
    cols = shape[-1]
    keep3 = w.ndim == 3 and shape[1] < SUBLANES
    view = shape if keep3 else (math.prod(shape[:-1]), cols)
    rows = view[0]
    if keep3:
        cap = max(1, (1 << 20) // (SUBLANES * cols * 4))
        tr = max(t for t in range(1, cap + 1) if rows % t == 0)
    else:
        tr = _row_tile(rows, cols * 4)

    def body(w_ref, g_ref, m_ref, v_ref, d_ref, mo_ref, vo_ref):
        gv = g_ref[...]
        mn = B1 * m_ref[...] + (1.0 - B1) * gv
        vn = B2 * v_ref[...] + (1.0 - B2) * (gv * gv)
        m_hat = mn / (1.0 - B1 ** STEP)
        v_hat = vn / (1.0 - B2 ** STEP)
        d_ref[...] = -LR * (m_hat / (jnp.sqrt(v_hat) + ADAM_EPS) + WD * w_ref[...])
        mo_ref[...] = mn
        vo_ref[...] = vn

    spec = pl.BlockSpec((tr,) + view[1:], lambda i: (i,) + (0,) * (len(view) - 1))
    out = jax.ShapeDtypeStruct(view, F32)
    res = pl.pallas_call(
        body, name=name, out_shape=(out,) * 3, grid=(rows // tr,), in_specs=[spec] * 4, out_specs=(spec,) * 3,
        compiler_params=_params("parallel"))(*[t.reshape(view) for t in (w, g, m, v)])
    return tuple(r.reshape(shape) for r in res)


LANE_MAJOR = ("w_uq", "w_uk", "w_uv", "w_gate", "w_up")


def _lane_major(name, a):
    if name == "w_in":
        return a.transpose(2, 0, 1)
    if name in LANE_MAJOR:
        return a.transpose(0, 2, 1)
    return a


def _from_lane_major(name, a):
    if name == "w_in":
        return a.transpose(1, 2, 0)
    return _lane_major(name, a)


ANY = pl.BlockSpec(memory_space=pl.ANY)


class _GatherRide:
    def __init__(self, arrays):
        n = len(arrays)
        self.arrays = list(arrays)
        self.out_shape = [jax.ShapeDtypeStruct((N_DEV,) + a.shape, a.dtype) for a in arrays]
        self.scratch = [pltpu.SemaphoreType.DMA((n, 7)), pltpu.SemaphoreType.DMA((n, 7)), pltpu.SemaphoreType.DMA((n,))]

    def _copies(self, ins, outs, sems):
        send_sems, recv_sems, local_sems = sems
        n = len(self.arrays)
        x, y, c = lax.axis_index("x"), lax.axis_index("y"), lax.axis_index("c")
        me, sibling = (x, y, c), (x, y, 1 - c)
        chips = [(1 - x, y), (x, 1 - y), (1 - x, 1 - y)]

        def slot(a, px, py, pc):
            return outs[a].at[4 * px + 2 * py + pc]

        def copy(a, k, block, to, src=None):
            return pltpu.make_async_remote_copy(
                src_ref=slot(a, *block) if src is None else src, dst_ref=slot(a, *block), send_sem=send_sems.at[a, k],
                recv_sem=recv_sems.at[a, k], device_id=to, device_id_type=MESH)

        mine = [pltpu.make_async_copy(ins[a], slot(a, *me), local_sems.at[a]) for a in range(n)]
        first = []
        for a in range(n):
            first.append(copy(a, 0, me, sibling, src=ins[a]))
            first += [copy(a, 1 + j, me, (*chip, c), src=ins[a]) for j, chip in enumerate(chips)]
        return n, me, sibling, chips, c, copy, mine, first

    def start(self, ins, outs, sems):
        _, _, _, _, _, _, mine, first = self._copies(ins, outs, sems)
        for cp in mine + first:
            cp.start()

    def finish(self, ins, outs, sems):
        n, me, sibling, chips, c, copy, mine, first = self._copies(ins, outs, sems)
        passed = []
        for j, chip in enumerate(chips):
            for a in range(n):
                copy(a, 1 + j, (*chip, c), me).wait_recv()
                passed.append(copy(a, 4 + j, (*chip, c), sibling))
                passed[-1].start()
        for a in range(n):
            copy(a, 0, sibling, me).wait_recv()
            for j, chip in enumerate(chips):
                copy(a, 4 + j, (*chip, 1 - c), me).wait_recv()
        for cp in first + passed:
            cp.wait_send()
        for cp in mine:
            cp.wait()


class _ReduceRide:
    def __init__(self, arrays):
        n = len(arrays)
        self.arrays = list(arrays)
        self.out_shape = [jax.ShapeDtypeStruct(a.shape, a.dtype) for a in arrays]
        self.scratch = [pltpu.SemaphoreType.DMA((n, 7)), pltpu.SemaphoreType.DMA((n, 7)), pltpu.SemaphoreType.DMA((n,))]

    def _copies(self, ins, outs, sems):
        send_sems, recv_sems, local_sems = sems
        n = len(self.arrays)
        x, y, c = lax.axis_index("x"), lax.axis_index("y"), lax.axis_index("c")
        mine = [pltpu.make_async_copy(ins[a].at[4 * x + 2 * y + c], outs[a].at[0], local_sems.at[a]) for a in range(n)]
        copies = []
        for a in range(n):
            for k in range(1, N_DEV):
                px = 1 - x if k & 4 else x
                py = 1 - y if k & 2 else y
                pc = 1 - c if k & 1 else c
                copies.append(pltpu.make_async_remote_copy(
                    src_ref=ins[a].at[4 * px + 2 * py + pc], dst_ref=outs[a].at[k], send_sem=send_sems.at[a, k - 1],
                    recv_sem=recv_sems.at[a, k - 1], device_id=(px, py, pc), device_id_type=MESH))
        return mine, copies

    def start(self, ins, outs, sems):
        mine, copies = self._copies(ins, outs, sems)
        for cp in mine + copies:
            cp.start()

    def finish(self, ins, outs, sems):
        mine, copies = self._copies(ins, outs, sems)
        for cp in copies + mine:
            cp.wait()


def _run_ride(ride, name):
    n = len(ride.arrays)

    def body(*refs):
        ins, outs, sems = refs[:n], refs[n:2 * n], refs[2 * n:]
        ride.start(ins, outs, sems)
        ride.finish(ins, outs, sems)

    return pl.pallas_call(body, name=name, out_shape=ride.out_shape, in_specs=[ANY] * n, out_specs=[ANY] * n,
                          scratch_shapes=ride.scratch)(*ride.arrays)


def _all_gather(arrays, name):
    return _run_ride(_GatherRide(arrays), name)


def _swap_with_sibling(arrays, name):
    n = len(arrays)

    def body(*refs):
        ins, outs = refs[:n], refs[n:2 * n]
        send_sems, recv_sems = refs[2 * n:]
        x, y, c = lax.axis_index("x"), lax.axis_index("y"), lax.axis_index("c")
        copies = [pltpu.make_async_remote_copy(
            src_ref=ins[a].at[1 - c], dst_ref=outs[a], send_sem=send_sems.at[a], recv_sem=recv_sems.at[a],
            device_id=(x, y, 1 - c), device_id_type=MESH) for a in range(n)]
        for cp in copies:
            cp.start()
        for cp in copies:
            cp.wait()

    return pl.pallas_call(
        body, name=name, out_shape=[jax.ShapeDtypeStruct(a.shape[1:], a.dtype) for a in arrays],
        in_specs=[ANY] * n, out_specs=[ANY] * n,
        scratch_shapes=[pltpu.SemaphoreType.DMA((n,)), pltpu.SemaphoreType.DMA((n,))])(*arrays)


class _ChipExchangeRide:
    def __init__(self, arrays):
        n = len(arrays)
        self.arrays = list(arrays)
        self.out_shape = [jax.ShapeDtypeStruct(a.shape, a.dtype) for a in arrays]
        self.scratch = [pltpu.SemaphoreType.DMA((n, 3)), pltpu.SemaphoreType.DMA((n, 3)), pltpu.SemaphoreType.DMA((n,))]

    def _copies(self, ins, outs, sems):
        send_sems, recv_sems, local_sems = sems
        n = len(self.arrays)
        x, y, c = lax.axis_index("x"), lax.axis_index("y"), lax.axis_index("c")
        partners = [(x, 1 - y), (1 - x, y), (1 - x, 1 - y)]
        mine = [pltpu.make_async_copy(ins[a].at[2 * x + y], outs[a].at[0], local_sems.at[a]) for a in range(n)]
        copies = [pltpu.make_async_remote_copy(
            src_ref=ins[a].at[2 * px + py], dst_ref=outs[a].at[1 + k], send_sem=send_sems.at[a, k],
            recv_sem=recv_sems.at[a, k], device_id=(px, py, c), device_id_type=MESH)
            for a in range(n) for k, (px, py) in enumerate(partners)]
        return mine, copies

    def start(self, ins, outs, sems):
        mine, copies = self._copies(ins, outs, sems)
        for cp in mine + copies:
            cp.start()

    def finish(self, ins, outs, sems):
        mine, copies = self._copies(ins, outs, sems)
        for cp in copies + mine:
            cp.wait()


class _Combo:
    def __init__(self, rides):
        self.rides = rides
        self.arrays = [a for r in rides for a in r.arrays]
        self.out_shape = [o for r in rides for o in r.out_shape]
        self.scratch = [sc for r in rides for sc in r.scratch]

    def _parts(self, ins, outs, sems):
        at_a = at_s = 0
        for r in self.rides:
            na, ns = len(r.arrays), len(r.scratch)
            yield r, ins[at_a:at_a + na], outs[at_a:at_a + na], sems[at_s:at_s + ns]
            at_a, at_s = at_a + na, at_s + ns

    def start(self, ins, outs, sems):
        for r, i, o, sm in self._parts(ins, outs, sems):
            r.start(i, o, sm)

    def finish(self, ins, outs, sems):
        for r, i, o, sm in self._parts(ins, outs, sems):
            r.finish(i, o, sm)


def _as_rows(a, lead):
    return a.reshape(a.shape[:lead] + (math.prod(a.shape[lead:-1]), a.shape[-1]))


def _add_pairs(a, b, name):
    a2, b2 = _as_rows(a, 0), _as_rows(b, 0)
    rows, cols = a2.shape
    tr = _row_tile(rows, cols * 4)

    def body(a_ref, b_ref, o_ref):
        o_ref[...] = (a_ref[...].astype(F32) + b_ref[...].astype(F32)).astype(o_ref.dtype)

    spec = _rows(tr, cols)
    out = pl.pallas_call(body, name=name, out_shape=jax.ShapeDtypeStruct(a2.shape, a.dtype), grid=(rows // tr,),
                         in_specs=[spec, spec], out_specs=spec, compiler_params=_params("parallel"))(a2, b2)
    return out.reshape(a.shape)


def _sum_blocks(a, name):
    a3 = _as_rows(a, 1)
    n, rows, cols = a3.shape
    tr = _row_tile(rows, n * cols * 4)

    def body(a_ref, o_ref):
        tot = a_ref[0].astype(F32)
        for k in range(1, n):
            tot = tot + a_ref[k].astype(F32)
        o_ref[...] = tot

    out = pl.pallas_call(body, name=name, out_shape=jax.ShapeDtypeStruct((rows, cols), F32), grid=(rows // tr,),
                         in_specs=[pl.BlockSpec((n, tr, cols), lambda j: (0, j, 0))], out_specs=_rows(tr, cols),
                         compiler_params=_params("parallel"))(a3)
    return out.reshape(a.shape[1:])


MIX_GROUPS = ("w_in", "w_uq", "w_uk", "w_uv", "w_attn_o", "w_conv_o", "w_pool_o", "w_mix_o")
FFN_GROUPS = ("w_gate", "w_up", "w_down")
MIX_EARLY = ("w_attn_o", "w_conv_o", "w_pool_o", "w_mix_o")
MIX_LATE = ("w_in", "w_uq", "w_uk", "w_uv")


def _pad_axis(a, axis, size):
    pad = [(0, 0)] * a.ndim
    pad[axis] = (0, size - a.shape[axis])
    return jnp.pad(a, pad)


def _local_groups(sh, l):
    out = {n: sh[n][l] for n in BIG}
    for n in ("w_uq", "w_uk", "w_uv"):
        out[n] = _pad_axis(out[n], -1, HEAD_PAD)
    for n in ("w_gate", "w_up"):
        out[n] = _pad_axis(out[n], -1, FF_SHARD_PAD)
    out["w_down"] = _pad_axis(out["w_down"], 0, FF_SHARD_PAD)
    return {n: v.astype(BF16) for n, v in out.items()}


def _arrange_w_in(blocks):
    parts, pos = [], 0
    for ref_lo, ref_hi, at in sorted(W_IN_PIECES, key=lambda p: p[2]):
        if at > pos:
            parts.append(jnp.zeros((blocks.shape[1], at - pos), blocks.dtype))
        for d in range(N_DEV):
            lo, hi = max(ref_lo, d * W_IN_SHARD), min(ref_hi, (d + 1) * W_IN_SHARD)
            if lo < hi:
                parts.append(blocks[d][:, lo - d * W_IN_SHARD:hi - d * W_IN_SHARD])
        pos = at + ref_hi - ref_lo
    if pos < Z_W:
        parts.append(jnp.zeros((blocks.shape[1], Z_W - pos), blocks.dtype))
    return jnp.concatenate(parts, axis=1)


def _w_in_shard(g, d):
    parts = []
    for ref_lo, ref_hi, at in W_IN_PIECES:
        lo, hi = max(ref_lo, d * W_IN_SHARD), min(ref_hi, (d + 1) * W_IN_SHARD)
        if lo < hi:
            parts.append(g[:, at + lo - ref_lo:at + hi - ref_lo])
    return jnp.concatenate(parts, axis=1)


def _mixer_weights(gat):
    w = dict(gat)
    w["w_in"] = _arrange_w_in(gat["w_in"])
    attn_o = gat["w_attn_o"].reshape(N_DEV, N_HEADS, V_HEAD, LANES)
    w["w_attn_o"] = _pad_axis(attn_o, 2, HEAD_PAD).reshape(N_DEV, N_HEADS * HEAD_PAD, LANES)
    w["w_mix_o"] = gat["w_mix_o"].reshape(D_MODEL, D_MODEL)
    return w


def _ffn_weights(gat):
    return {"w_gate": gat["w_gate"], "w_up": gat["w_up"], "w_down": gat["w_down"].reshape(D_FF_PAD, D_MODEL)}


def _mixer_grad_groups(gb):
    g = dict(gb)
    if "w_in" in gb:
        g["w_in"] = jnp.stack([_w_in_shard(gb["w_in"], d) for d in range(N_DEV)])
    if "w_attn_o" in gb:
        attn_o = gb["w_attn_o"].reshape(N_DEV, N_HEADS, HEAD_PAD, LANES)[:, :, :V_HEAD]
        g["w_attn_o"] = attn_o.reshape(N_DEV, N_HEADS * V_HEAD, LANES)
    if "w_mix_o" in gb:
        g["w_mix_o"] = gb["w_mix_o"].reshape(N_DEV, D_MODEL // N_DEV, D_MODEL)
    return g


def _ffn_grad_groups(gb):
    return {"w_gate": gb["w_gate"], "w_up": gb["w_up"], "w_down": gb["w_down"].reshape(N_DEV, FF_SHARD_PAD, D_MODEL)}


def _grads_from_groups(tot):
    g = dict(tot)
    g["w_uq"] = tot["w_uq"][:, :QK_NOPE + QK_ROPE]
    g["w_uk"], g["w_uv"] = tot["w_uk"][:, :QK_NOPE], tot["w_uv"][:, :V_HEAD]
    g["w_gate"], g["w_up"] = tot["w_gate"][:, :FF_SHARD], tot["w_up"][:, :FF_SHARD]
    g["w_down"] = tot["w_down"][:FF_SHARD]
    return g


SMALL_GROUPS = (
    (D_MODEL, ("mix_norm_pre", "mix_norm_post", "ffn_norm_pre", "ffn_norm_post")),
    (CONV_C, ("conv_w", "conv_b", "conv_ln_g", "conv_ln_b", "pool_scale")),
    (Q_RANK, ("q_norm",)), (KV_RANK, ("kv_norm",)), (POOL_GD, ("pool_w",)),
)


def _small_rows(name):
    return {"conv_w": CONV_HALO, "pool_w": POOL_G * POOL_GD}.get(name, SUBLANES)


def _small_groups(small):
    out = []
    for width, names in SMALL_GROUPS:
        parts = []
        for l in range(DEPTH):
            for n in names:
                part = small[l][n].reshape(-1, width)
                parts.append(_pad_axis(part, 0, _small_rows(n)))
        out.append(jnp.concatenate(parts, axis=0))
    return out


def _small_from_groups(groups):
    shapes = {"conv_w": (CONV_W, CONV_C), "pool_w": (POOL_G, POOL_GD, POOL_GD)}
    out = {}
    for (width, names), g in zip(SMALL_GROUPS, groups):
        row = 0
        for l in range(DEPTH):
            for n in names:
                rows = _small_rows(n)
                real = {"conv_w": CONV_W, "pool_w": POOL_G * POOL_GD}.get(n, 1)
                out.setdefault(n, []).append(g[row:row + real].reshape(shapes.get(n, (width,))))
                row += rows
    return {n: jnp.stack(v) for n, v in out.items()}


def _mixer_fwd(x, tables, w, sm, tag, ride):
    nm = lambda n: f"{n}_{tag}"
    h = _rms_fwd(x, (D_MODEL, 0), sm["mix_norm_pre"], BF16, nm("mix_pre_norm"))
    z = _matmul(h, w["w_in"], "nn", F32, nm("in_proj"))
    cq = _rms_fwd(z, ZC_Q, sm["q_norm"], BF16, nm("q_norm"))
    ckv = _rms_fwd(z, ZC_KV, sm["kv_norm"], BF16, nm("kv_norm"))
    qf = _matmul(cq, w["w_uq"], "nn", F32, nm("q_up"))
    kf = _matmul(ckv, w["w_uk"], "nn", F32, nm("k_up"))
    v = _matmul(ckv, w["w_uv"], "nn", BF16, nm("v_up"))
    q, k = _rope_qk_fwd(qf, kf, z, tables, nm("rope_qk"))
    (o, lse), rode = _flash_fwd(q, k, v, nm("flash_fwd"), ride)
    y_attn = _matmul(o, w["w_attn_o"], "nn", F32, nm("attn_out"))
    hc, co = _conv_fwd(z, sm["conv_w"], sm["conv_b"], sm["conv_ln_g"], sm["conv_ln_b"], nm("conv_fwd"))
    y_conv = _matmul(hc, w["w_conv_o"], "nn", F32, nm("conv_out"))
    pm = _pool_fwd(z, sm["pool_w"], sm["pool_scale"], nm("pool_fwd"))
    y_pool = _matmul(pm, w["w_pool_o"], "nn", F32, nm("pool_out"))
    ys = (y_attn, y_conv, y_pool)
    merged = _merge_fwd(z, ys, nm("merge_fwd"))
    mo = _matmul(merged, w["w_mix_o"], "nn", F32, nm("mix_out"))
    x_mid = _rms_fwd(mo, (D_MODEL, 0), sm["mix_norm_post"], F32, nm("mix_post_norm"), res=x)
    saved = dict(x=x, h=h, z=z, cq=cq, ckv=ckv, q=q, k=k, v=v, o=o, lse=lse, hc=hc, co=co, pm=pm, ys=ys, merged=merged,
                 mo=mo)
    return x_mid, saved, rode


def _ffn_fwd(x_mid, w, sm, tag):
    nm = lambda n: f"{n}_{tag}"
    h2 = _rms_fwd(x_mid, (D_MODEL, 0), sm["ffn_norm_pre"], BF16, nm("ffn_pre_norm"))
    hg, hu, act = _ffn_up_fwd(h2, w["w_gate"], w["w_up"], nm("ffn_up_fwd"))
    fo = _matmul(act, w["w_down"], "nn", F32, nm("ffn_down"))
    out = _rms_fwd(fo, (D_MODEL, 0), sm["ffn_norm_post"], F32, nm("ffn_post_norm"), res=x_mid)
    saved = dict(x_mid=x_mid, h2=h2, hg=hg, hu=hu, act=act, fo=fo)
    return out, saved


def _ffn_bwd(dout, sv, w, sm, tag):
    nm = lambda n: f"{n}_{tag}"
    gb, gs = {}, {}
    dfo, gs["ffn_norm_post"] = _rms_bwd(sv["fo"], (D_MODEL, 0), sm["ffn_norm_post"], dout, BF16, nm("ffn_post_norm_bwd"))
    gb["w_down"] = _matmul(sv["act"], dfo, "tn", BF16, nm("ffn_down_dw"))
    dhg, dhu = _ffn_down_bwd(dfo, w["w_down"], sv["hg"], sv["hu"], nm("ffn_down_bwd"))
    dh2_g = _matmul(dhg, w["w_gate"], "nt", F32, nm("ffn_gate_dx"))
    dh2 = _matmul(dhu, w["w_up"], "nt", F32, nm("ffn_up_dx"), add=dh2_g)
    gb["w_gate"] = _matmul(sv["h2"], dhg, "tn", BF16, nm("ffn_gate_dw"), blocked=True)
    gb["w_up"] = _matmul(sv["h2"], dhu, "tn", BF16, nm("ffn_up_dw"), blocked=True)
    dmid, gs["ffn_norm_pre"] = _rms_bwd(sv["x_mid"], (D_MODEL, 0), sm["ffn_norm_pre"], dh2, F32, nm("ffn_pre_norm_bwd"),
                                        add=dout)
    return dmid, gb, gs


def _mixer_bwd(dmid, sv, tables, w, sm, tag, make_ride):
    nm = lambda n: f"{n}_{tag}"
    gb, gs = {}, {}
    dmo, gs["mix_norm_post"] = _rms_bwd(sv["mo"], (D_MODEL, 0), sm["mix_norm_post"], dmid, BF16, nm("mix_post_norm_bwd"))
    dmerged = _matmul(dmo, w["w_mix_o"], "nt", F32, nm("mix_out_dx"))
    gb["w_mix_o"] = _matmul(sv["merged"], dmo, "tn", BF16, nm("mix_out_dw"))
    dya, dyc, dyp, dz = _merge_bwd(sv["z"], sv["ys"], dmerged, nm("merge_bwd"))
    dpm = _matmul(dyp, w["w_pool_o"], "nt", F32, nm("pool_out_dx"))
    gb["w_pool_o"] = _matmul(sv["pm"], dyp, "tn", BF16, nm("pool_out_dw"), blocked=True)
    dz, gs["pool_w"], gs["pool_scale"] = _pool_bwd(dpm, sv["z"], sm["pool_w"], sm["pool_scale"], dz, nm("pool_bwd"))
    dhc = _matmul(dyc, w["w_conv_o"], "nt", F32, nm("conv_out_dx"))
    gb["w_conv_o"] = _matmul(sv["hc"], dyc, "tn", BF16, nm("conv_out_dw"), blocked=True)
    dco, gs["conv_ln_g"], gs["conv_ln_b"], gs["conv_b"] = _conv_bwd_norm(dhc, sv["co"], sm["conv_ln_g"], sm["conv_ln_b"],
                                                                        nm("conv_bwd_norm"))
    dz, gs["conv_w"] = _conv_bwd_taps(dco, sv["z"], sm["conv_w"], dz, nm("conv_bwd_taps"))
    do = _matmul(dya, w["w_attn_o"], "nt", F32, nm("attn_out_dx"))
    gb["w_attn_o"] = _matmul(sv["o"], dya, "tn", BF16, nm("attn_out_dw"), blocked=True)
    delta, dob = _attn_delta(do, sv["o"], nm("attn_delta"))
    (dq, dk, dv), rode = _flash_bwd(sv["q"], sv["k"], sv["v"], dob, sv["lse"], delta, nm("flash_bwd"), make_ride(gb))
    dqf, dkf, dz = _rope_qk_bwd(dq, dk, tables, dz, nm("rope_qk_bwd"))
    dcq_n = _matmul(dqf, w["w_uq"], "nt", F32, nm("q_up_dx"))
    gb["w_uq"] = _matmul(sv["cq"], dqf, "tn", BF16, nm("q_up_dw"), blocked=True)
    dckv_k = _matmul(dkf, w["w_uk"], "nt", F32, nm("k_up_dx"))
    dckv_n = _matmul(dv, w["w_uv"], "nt", F32, nm("v_up_dx"), add=dckv_k)
    gb["w_uk"] = _matmul(sv["ckv"], dkf, "tn", BF16, nm("k_up_dw"), blocked=True)
    gb["w_uv"] = _matmul(sv["ckv"], dv, "tn", BF16, nm("v_up_dw"), blocked=True)
    dz, gs["q_norm"] = _rms_bwd(sv["z"], ZC_Q, sm["q_norm"], dcq_n, BF16, nm("q_norm_bwd"), dz=dz)
    dz, gs["kv_norm"] = _rms_bwd(sv["z"], ZC_KV, sm["kv_norm"], dckv_n, BF16, nm("kv_norm_bwd"), dz=dz)
    dh =_matmul(dz, w["w_in"], "nt", F32, nm("in_proj_dx"))
    gb["w_in"] = _matmul(sv["h"], dz, "tn", BF16, nm("in_proj_dw"))
    dx, gs["mix_norm_pre"] = _rms_bwd(sv["x"], (D_MODEL, 0), sm["mix_norm_pre"], dh, F32, nm("mix_pre_norm_bwd"), add=dmid)
    return dx, gb, gs, rode


def _part_groups(part):
    return {"mix": MIX_GROUPS, "ffn": FFN_GROUPS, "early": MIX_EARLY, "late": MIX_LATE}[part]


class _Plan:
    def __init__(self, shards, conv_w):
        self.local = [_local_groups(shards, l) for l in range(DEPTH)]
        self.conv_w = conv_w
        self.gat, self.send, self.recv = {}, {}, {}

    @staticmethod
    def _riders(l):
        return [(l, "ffn")] + ([(l + 1, "mix")] if l + 1 < DEPTH else [])

    @staticmethod
    def _grad_riders(l):
        return [(l, "ffn"), (l, "early")] + ([(l + 1, "late")] if l + 1 < DEPTH else [])

    def gather_first(self):
        out = _all_gather([self.local[0][g] for g in MIX_GROUPS] + [self.conv_w], "gather_mixer_l0")
        self.gat[(0, "mix")] = dict(zip(MIX_GROUPS, out[:-1]))
        return out[-1]

    def fwd_ride(self, l):
        return _GatherRide([self.local[ll][g] for ll, part in self._riders(l) for g in _part_groups(part)])

    def fwd_done(self, l, outs):
        outs = list(outs)
        for ll, part in self._riders(l):
            self.gat[(ll, part)] = {g: outs.pop(0) for g in _part_groups(part)}

    def mixer_weights(self, l):
        return _mixer_weights(self.gat[(l, "mix")])

    def ffn_weights(self, l):
        return _ffn_weights(self.gat[(l, "ffn")])

    def add_grads(self, l, part, gb):
        if part == "ffn":
            self.send[(l, "ffn")] = _ffn_grad_groups(gb)
        else:
            self.send[(l, "late")] = _mixer_grad_groups({g: gb[g] for g in MIX_LATE})

    def bwd_ride(self, l, gb_early):
        self.send[(l, "early")] = _mixer_grad_groups({g: gb_early[g] for g in MIX_EARLY})
        return _ReduceRide([self.send[(ll, part)][g] for ll, part in self._grad_riders(l) for g in _part_groups(part)])

    def bwd_done(self, l, outs):
        outs = list(outs)
        for ll, part in self._grad_riders(l):
            self.recv[(ll, part)] = {g: outs.pop(0) for g in _part_groups(part)}

    def finish(self, small_groups):
        send = [self.send[(0, "late")][g] for g in MIX_LATE]
        by_core = [a.reshape((4, 2) + a.shape[1:]).transpose((1, 0) + tuple(range(2, a.ndim + 1))) for a in send]
        core = lax.axis_index("c")
        own = [lax.dynamic_index_in_dim(a, core, axis=0, keepdims=False) for a in by_core]
        got = _swap_with_sibling(by_core, "reduce_d2d")
        pairs = [_add_pairs(a, b, f"reduce_pair_add_{g}") for g, a, b in zip(MIX_LATE, own, got)]
        outs = _run_ride(_Combo([_ChipExchangeRide(pairs), _GatherRide(small_groups)]), "reduce_ici_gather_small")
        self.recv[(0, "late")] = dict(zip(MIX_LATE, outs[:len(pairs)]))
        layers = []
        for l in range(DEPTH):
            tot = {g: _sum_blocks(a, f"reduce_sum_{g}_l{l}") for part in ("early", "late", "ffn")
                   for g, a in self.recv[(l, part)].items()}
            layers.append(_grads_from_groups(tot))
        return layers, outs[len(pairs):]


def _local_step(x, positions, target, smalls, plan):
    tables = _rope_tables(positions)
    saved = []
    h = x
    for l in range(DEPTH):
        wm = plan.mixer_weights(l)
        h, svm, rode = _mixer_fwd(h, tables, wm, smalls[l], f"l{l}", plan.fwd_ride(l))
        plan.fwd_done(l, rode)
        wf = plan.ffn_weights(l)
        h, svf = _ffn_fwd(h, wf, smalls[l], f"l{l}")
        saved.append((svm, svf, wm, wf))
    dy, sq = _loss_grad(h, target, "loss_grad")
    small = [None] * DEPTH
    for l in reversed(range(DEPTH)):
        svm, svf, wm, wf = saved[l]
        dmid, gbf, gsf = _ffn_bwd(dy, svf, wf, smalls[l], f"l{l}")
        plan.add_grads(l, "ffn", gbf)
        dy, gbm, gsm, rode = _mixer_bwd(dmid, svm, tables, wm, smalls[l], f"l{l}", lambda gb, l=l: plan.bwd_ride(l, gb))
        plan.bwd_done(l, rode)
        plan.add_grads(l, "mix", gbm)
        small[l] = {**gsf, **gsm}
    return sq, dy, small


def kernel(x, positions, mix_norm_pre, w_in, q_norm, w_uq, kv_norm, w_uk, w_uv, w_attn_o, conv_w, conv_b, conv_ln_g, conv_ln_b, w_conv_o, pool_w, pool_scale, w_pool_o, w_mix_o, mix_norm_post, ffn_norm_pre, w_gate, w_up, w_down, ffn_norm_post, loss_target, m_mix_norm_pre, m_w_in, m_q_norm, m_w_uq, m_kv_norm, m_w_uk, m_w_uv, m_w_attn_o, m_conv_w, m_conv_b, m_conv_ln_g, m_conv_ln_b, m_w_conv_o, m_pool_w, m_pool_scale, m_w_pool_o, m_w_mix_o, m_mix_norm_post, m_ffn_norm_pre, m_w_gate, m_w_up, m_w_down, m_ffn_norm_post, v_mix_norm_pre, v_w_in, v_q_norm, v_w_uq, v_kv_norm, v_w_uk, v_w_uv, v_w_attn_o, v_conv_w, v_conv_b, v_conv_ln_g, v_conv_ln_b, v_w_conv_o, v_pool_w, v_pool_scale, v_w_pool_o, v_w_mix_o, v_mix_norm_post, v_ffn_norm_pre, v_w_gate, v_w_up, v_w_down, v_ffn_norm_post):
    given = dict(locals())
    dev = 4 * lax.axis_index("x") + 2 * lax.axis_index("y") + lax.axis_index("c")

    plan = _Plan({n: given[n] for n in BIG}, conv_w)
    cw = CONV_C // N_DEV
    conv_w_full = plan.gather_first().transpose(1, 2, 0, 3).reshape(DEPTH, CONV_W, CONV_C)
    smalls = []
    for l in range(DEPTH):
        sm = {n: given[n][l] for n in SMALL if n != "conv_w"}
        sm["conv_w"] = _pad_axis(conv_w_full[l], 0, CONV_HALO)
        smalls.append(sm)

    sq, grad_x, small = _local_step(x[0], positions[0], loss_target[0], smalls, plan)
    loss = lax.psum(0.5 / D_MODEL * jnp.sum(sq), ("x", "y", "c"))
    per_layer, small_groups = plan.finish(_small_groups(small))
    views = {}
    for n in BIG:
        if n == "w_in":
            views[n] = jnp.stack([per_layer[l][n].T for l in range(DEPTH)], axis=1)
        elif n in LANE_MAJOR:
            views[n] = jnp.stack([per_layer[l][n].T for l in range(DEPTH)])
        else:
            views[n] = jnp.stack([per_layer[l][n] for l in range(DEPTH)])
    grads = {n: _from_lane_major(n, views[n]) for n in BIG}

    small_sum = _small_from_groups([_sum_blocks(g, f"sum_small_grads_{i}") for i, g in enumerate(small_groups)])
    for n in SMALL:
        grads[n] = small_sum[n]
    grads["conv_w"] = lax.dynamic_slice_in_dim(small_sum["conv_w"], dev * cw, cw, axis=2)

    delta, new_m, new_v = {}, {}, {}
    for n in WEIGHTS:
        g_view = views[n] if n in views else grads[n]
        w_view, m_view, v_view = [_lane_major(n, given[k]) for k in (n, "m_" + n, "v_" + n)]
        res = _adamw(w_view, g_view, m_view, v_view, f"adamw_{n}")
        delta[n], new_m[n], new_v[n] = [_from_lane_major(n, r) for r in res]
    return (loss, grad_x[None], *[grads[n] for n in WEIGHTS], *[delta[n] for n in WEIGHTS],
            *[new_m[n] for n in WEIGHTS], *[new_v[n] for n in WEIGHTS])
```

```python
import functools
import math

import jax
import jax.numpy as jnp
from jax import lax
from jax.experimental import pallas as pl
from jax.experimental.pallas import tpu as pltpu

F32, BF16 = jnp.float32, jnp.bfloat16
MESH = pl.DeviceIdType.MESH

LANES = 128
SUBLANES = 8
VMEM_LIMIT_BYTES = 56 * 1024 * 1024
MATMUL_VMEM_BYTES = 40 * 1024 * 1024

N_DEV = 8
D_MODEL = 1024
DEPTH = 2
N_HEADS = 8
QK_NOPE, QK_ROPE, V_HEAD = 64, 32, 64
HEAD_PAD = LANES
Q_RANK, KV_RANK = 384, 256
ROPE_THETA = 10000.0
CONV_C, CONV_W = 512, 31
CONV_HALO = 32
POOL_WINDOWS = (2, 4, 8, 16)
POOL_C, POOL_G = 512, 4
POOL_GD = POOL_C // POOL_G
D_FF = 2816
FF_SHARD = D_FF // N_DEV
FF_SHARD_PAD = 3 * LANES
D_FF_PAD = N_DEV * FF_SHARD_PAD
W_IN_SHARD = 660
EPS = 1e-6
ATTN_SCALE = 1.0 / math.sqrt(QK_NOPE + QK_ROPE)
LOG2E = 1.4426950408889634
LR, B1, B2, ADAM_EPS, WD, STEP = 0.001, 0.9, 0.999, 1e-08, 0.01, 10

Z_W = 5376
ZC_GATE = (1024, 0)
ZC_GATES = (3072, 0)
ZC_CONV_A = (512, 6)
ZC_CONV_G = (512, 7)
ZC_CONV = (1024, 3)
ZC_POOL = (512, 8)
ZC_Q = (384, 12)
ZC_KR = (128, 39)
ZC_KV = (256, 20)
W_IN_PIECES = ((0, 384, 4608), (384, 640, 5120), (640, 672, 5056), (672, 1696, 3072), (1696, 2208, 4096),
               (2208, 5280, 0))

BIG = ("w_in", "w_uq", "w_uk", "w_uv", "w_attn_o", "w_conv_o", "w_pool_o", "w_mix_o", "w_gate", "w_up", "w_down")
SMALL = ("mix_norm_pre", "q_norm", "kv_norm", "conv_w", "conv_b", "conv_ln_g", "conv_ln_b", "pool_w", "pool_scale",
         "mix_norm_post", "ffn_norm_pre", "ffn_norm_post")
WEIGHTS = ("mix_norm_pre", "w_in", "q_norm", "w_uq", "kv_norm", "w_uk", "w_uv", "w_attn_o", "conv_w", "conv_b",
           "conv_ln_g", "conv_ln_b", "w_conv_o", "pool_w", "pool_scale", "w_pool_o", "w_mix_o", "mix_norm_post",
           "ffn_norm_pre", "w_gate", "w_up", "w_down", "ffn_norm_post")


def _params(*semantics):
    return pltpu.CompilerParams(dimension_semantics=semantics, vmem_limit_bytes=VMEM_LIMIT_BYTES)


def _tile(dim, cap):
    if dim <= cap:
        return dim
    for t in range(cap - cap % LANES, 0, -LANES):
        if dim % t == 0:
            return t
    raise ValueError(f"no tile for {dim} under {cap}")


def _row_tile(rows, row_bytes, budget=1 << 20):
    if rows * row_bytes <= budget:
        return rows
    cap = max(16, budget // row_bytes)
    for t in range(cap - cap % 16, 0, -16):
        if rows % t == 0:
            return t
    return rows


def _rows(ts, width, cidx=0):
    return pl.BlockSpec((ts, width), lambda i: (i, cidx))


def _fixed(shape):
    return pl.BlockSpec(shape, lambda *_: (0,) * len(shape))


def _sigmoid(x):
    return 1.0 / (1.0 + jnp.exp(-x))


def _matmul(a, b, mode, out_dtype, name, add=None, blocked=False):
    nb = n_blk = 0
    blocked = blocked or b.ndim == 3
    if mode == "nn":
        (m, k) = a.shape
        n = b.shape[0] * b.shape[2] if blocked else b.shape[1]
    elif mode == "nt":
        (m, k) = a.shape
        n = b.shape[1] if blocked else b.shape[0]
    else:
        (k, m), n = a.shape, b.shape[1]
    if blocked:
        nb = b.shape[2] if mode != "tn" else n // N_DEV
    unit = nb if blocked and mode != "nt" else LANES
    out_bytes = jnp.dtype(out_dtype).itemsize + (4 if add is not None else 0)
    best = None
    for tn_c in range(unit, min(n, 1536) + 1, unit):
        for tm_c in sorted({256, 512, 1024, 2048, min(m, 2048)}):
            if n % tn_c or m % tm_c or (blocked and mode != "nt" and N_DEV % (tn_c // nb)):
                continue
            vmem = 2 * (tm_c * k * 2 + tn_c * k * 2 + tm_c * tn_c * out_bytes) + tm_c * tn_c * 4 + tn_c * k * 2
            if vmem <= MATMUL_VMEM_BYTES and (best is None or tm_c * tn_c / (tm_c + tn_c) > best[0]):
                best = (tm_c * tn_c / (tm_c + tn_c), tm_c, tn_c)
    if best is None:
        raise ValueError(f"{name}: no tiles for {m}x{n}x{k}")
    _, tm, tn = best
    if blocked:
        n_blk = N_DEV if mode == "nt" else tn // nb
    dims = {"nn": ((1,), (0,)), "nt": ((1,), (1,)), "tn": ((0,), (0,))}[mode]
    a_spec = pl.BlockSpec((k, tm), lambda i, j: (0, i)) if mode == "tn" else pl.BlockSpec((tm, k), lambda i, j: (i, 0))
    b_spec = pl.BlockSpec((tn, k), lambda i, j: (j, 0)) if mode == "nt" else pl.BlockSpec((k, tn), lambda i, j: (0, j))
    o_spec = pl.BlockSpec((tm, tn), lambda i, j: (i, j))
    out_shape = jax.ShapeDtypeStruct((m, n), out_dtype)
    if blocked and mode == "nn":
        b_spec = pl.BlockSpec((n_blk, k, nb), lambda i, j: (j, 0, 0))
    elif blocked and mode == "nt":
        b_spec = pl.BlockSpec((n_blk, tn, nb), lambda i, j: (0, j, 0))
    elif blocked:
        o_spec = pl.BlockSpec((n_blk, tm, nb), lambda i, j: (j, i, 0))
        out_shape = jax.ShapeDtypeStruct((N_DEV, m, nb), out_dtype)
    has_add = add is not None

    def body(a_ref, b_ref, *rest):
        o_ref = rest[-1]
        if blocked and mode != "tn":
            bv = jnp.concatenate([b_ref[c] for c in range(n_blk)], axis=1) if n_blk > 1 else b_ref[0]
        else:
            bv = b_ref[...]
        total = lax.dot_general(a_ref[...], bv, (dims, ((), ())), preferred_element_type=F32)
        if has_add:
            total = total + rest[0][...]
        if blocked and mode == "tn":
            for c in range(n_blk):
                o_ref[c] = total[:, c * nb:(c + 1) * nb].astype(o_ref.dtype)
        else:
            o_ref[...] = total.astype(o_ref.dtype)

    operands = (a, b, add) if has_add else (a, b)
    return pl.pallas_call(
        body, name=name, out_shape=out_shape, grid=(m // tm, n // tn),
        in_specs=[a_spec, b_spec] + ([o_spec] if has_add else []), out_specs=o_spec,
        compiler_params=_params("parallel", "parallel"))(*operands)


def _rms_fwd(x, win, gain, out_dtype, name, res=None):
    width, cidx = win
    s = x.shape[0]
    ts = min(s, 512)
    has_res = res is not None

    def body(x_ref, g_ref, *rest):
        o_ref = rest[-1]
        xv = x_ref[...].astype(F32)
        r = lax.rsqrt(jnp.mean(xv * xv, axis=-1, keepdims=True) + EPS)
        y = (xv * r) * g_ref[...]
        if has_res:
            y = rest[0][...] + y
        o_ref[...] = y.astype(o_ref.dtype)

    ops = (x, gain.reshape(1, width)) + ((res,) if has_res else ())
    return pl.pallas_call(
        body, name=name, out_shape=jax.ShapeDtypeStruct((s, width), out_dtype), grid=(s // ts,),
        in_specs=[_rows(ts, width, cidx), _fixed((1, width))] + ([_rows(ts, width)] if has_res else []),
        out_specs=_rows(ts, width), compiler_params=_params("parallel"))(*ops)


def _into(dz, n_inputs, out_index):
    return dict(in_specs=[ANY], operands=(dz,), input_output_aliases={n_inputs: out_index},
                out_shape=jax.ShapeDtypeStruct(dz.shape, dz.dtype))


def _rms_bwd(x, win, gain, dy, out_dtype, name, add=None, dz=None):
    width, cidx = win
    s = x.shape[0]
    ts = min(s, 512)
    has_add = add is not None

    def body(x_ref, g_ref, dy_ref, *rest):
        dx_ref, dg_ref = rest[-2], rest[-1]
        xv = x_ref[...].astype(F32)
        r = lax.rsqrt(jnp.mean(xv * xv, axis=-1, keepdims=True) + EPS)
        xh = xv * r
        dyv = dy_ref[...].astype(F32)
        dyg = dyv * g_ref[...]
        dx = r * (dyg - xh * jnp.mean(dyg * xh, axis=-1, keepdims=True))
        if has_add:
            dx = dx + rest[0][...]
        dx_ref[...] = dx.astype(dx_ref.dtype)

        @pl.when(pl.program_id(0) == 0)
        def _():
            dg_ref[...] = jnp.zeros_like(dg_ref)

        dg_ref[...] += jnp.sum(dyv * xh, axis=0, keepdims=True)

    ops = (x, gain.reshape(1, width), dy) + ((add,) if has_add else ())
    in_specs = [_rows(ts, width, cidx), _fixed((1, width)), _rows(ts, width)] + ([_rows(ts, width)] if has_add else [])
    dx_shape, dx_spec, alias = jax.ShapeDtypeStruct((s, width), out_dtype), _rows(ts, width), {}
    if dz is not None:
        into = _into(dz, len(ops), 0)
        ops, in_specs, alias = ops + into["operands"], in_specs + into["in_specs"], into["input_output_aliases"]
        dx_shape, dx_spec = into["out_shape"], _rows(ts, width, cidx)
    dx, dg = pl.pallas_call(
        body, name=name, out_shape=(dx_shape, jax.ShapeDtypeStruct((1, width), F32)), grid=(s // ts,),
        in_specs=in_specs, out_specs=(dx_spec, _fixed((1, width))), input_output_aliases=alias,
        compiler_params=_params("arbitrary"))(*ops)
    return dx, dg.reshape(width)


def _rope(x, c, s1, s2):
    return x * c + pltpu.roll(x, 16, 1) * s1 + pltpu.roll(x, LANES - 16, 1) * s2


def _rope_t(g, c, s1, s2):
    return g * c + pltpu.roll(g * s1, LANES - 16, 1) + pltpu.roll(g * s2, 16, 1)


def _rope_tables(positions):
    inv_freq = ROPE_THETA ** (-jnp.arange(0, QK_ROPE, 2, dtype=F32) / QK_ROPE)
    ang = positions.astype(F32)[:, None] * inv_freq
    cos, sin = jnp.cos(ang), jnp.sin(ang)
    n = positions.shape[0]
    one, zero = jnp.ones((n, 1), F32), jnp.zeros((n, 1), F32)
    c = jnp.concatenate([jnp.tile(one, (1, QK_NOPE)), cos, cos, jnp.tile(one, (1, 32))], axis=1)
    s1 = jnp.concatenate([jnp.tile(zero, (1, QK_NOPE + 16)), sin, jnp.tile(zero, (1, 32))], axis=1)
    s2 = jnp.concatenate([jnp.tile(zero, (1, QK_NOPE)), -sin, jnp.tile(zero, (1, 48))], axis=1)
    return c, s1, s2


def _rope_qk_fwd(qf, kf, z, tables, name):
    s = qf.shape[0]
    ts = min(s, 256)
    hw = N_HEADS * HEAD_PAD

    def body(qf_ref, kf_ref, kr_ref, c_ref, s1_ref, s2_ref, q_ref, k_ref):
        c, s1, s2 = c_ref[...], s1_ref[...], s2_ref[...]
        kr = _rope(kr_ref[...].astype(F32), c, s1, s2)
        for h in range(N_HEADS):
            sl = slice(h * HEAD_PAD, (h + 1) * HEAD_PAD)
            q_ref[:, sl] = _rope(qf_ref[:, sl], c, s1, s2).astype(BF16)
            k_ref[:, sl] = (kf_ref[:, sl] + kr).astype(BF16)

    tab = _rows(ts, LANES)
    return pl.pallas_call(
        body, name=name, out_shape=(jax.ShapeDtypeStruct((s, hw), BF16),) * 2, grid=(s // ts,),
        in_specs=[_rows(ts, hw), _rows(ts, hw), _rows(ts, *ZC_KR), tab, tab, tab],
        out_specs=(_rows(ts, hw), _rows(ts, hw)), compiler_params=_params("parallel"))(qf, kf, z, *tables)


def _rope_qk_bwd(dq, dk, tables, dz, name):
    s = dq.shape[0]
    ts = min(s, 256)
    hw = N_HEADS * HEAD_PAD

    def body(dq_ref, dk_ref, c_ref, s1_ref, s2_ref, _, dqf_ref, dkf_ref, dkr_ref):
        c, s1, s2 = c_ref[...], s1_ref[...], s2_ref[...]
        ksum = jnp.zeros((ts, HEAD_PAD), F32)
        for h in range(N_HEADS):
            sl = slice(h * HEAD_PAD, (h + 1) * HEAD_PAD)
            dqf_ref[:, sl] = _rope_t(dq_ref[:, sl], c, s1, s2).astype(BF16)
            dkh = dk_ref[:, sl]
            dkf_ref[:, sl] = dkh.astype(BF16)
            ksum = ksum + dkh
        lane = lax.broadcasted_iota(jnp.int32, (ts, HEAD_PAD), 1)
        in_rope = (lane >= QK_NOPE) & (lane < QK_NOPE + QK_ROPE)
        dkr_ref[...] = jnp.where(in_rope, _rope_t(ksum, c, s1, s2), 0.0).astype(BF16)

    tab = _rows(ts, LANES)
    into = _into(dz, 5, 2)
    return pl.pallas_call(
        body, name=name,
        out_shape=(jax.ShapeDtypeStruct((s, hw), BF16), jax.ShapeDtypeStruct((s, hw), BF16), into["out_shape"]),
        grid=(s // ts,), in_specs=[_rows(ts, hw), _rows(ts, hw), tab, tab, tab] + into["in_specs"],
        out_specs=(_rows(ts, hw), _rows(ts, hw), _rows(ts, *ZC_KR)), input_output_aliases=into["input_output_aliases"],
        compiler_params=_params("parallel"))(dq, dk, *tables, dz)


def _attn_tile(s):
    return min(s, 512)


def _raw_scores(q, k, masked, row0=0):
    sc = lax.dot_general(q, k, (((1,), (1,)), ((), ())), preferred_element_type=F32)
    if masked:
        rows = row0 + lax.broadcasted_iota(jnp.int32, sc.shape, 0)
        cols = lax.broadcasted_iota(jnp.int32, sc.shape, 1)
        sc = jnp.where(cols <= rows, sc, -jnp.inf)
    return sc


def _ride_hooks(ride, refs, n_in, n_out, grid):
    if ride is None:
        return refs, lambda: None, lambda: None
    n = len(ride.arrays)
    own = refs[:n_in] + refs[n_in + n:n_in + n + n_out]
    ins, outs, sems = refs[n_in:n_in + n], refs[n_in + n + n_out:n_in + 2 * n + n_out], refs[n_in + 2 * n + n_out:]
    at_first = functools.reduce(lambda a, b: a & b, [pl.program_id(ax) == 0 for ax in range(len(grid))])
    at_last = functools.reduce(lambda a, b: a & b, [pl.program_id(ax) == g - 1 for ax, g in enumerate(grid)])
    return own, lambda: pl.when(at_first)(lambda: ride.start(ins, outs, sems)), \
        lambda: pl.when(at_last)(lambda: ride.finish(ins, outs, sems))


def _ride_call(ride, body, name, out_shape, grid, in_specs, out_specs, semantics, operands):
    n = 0 if ride is None else len(ride.arrays)
    res = pl.pallas_call(
        body, name=name, out_shape=tuple(out_shape) + (tuple(ride.out_shape) if n else ()), grid=grid,
        in_specs=list(in_specs) + [ANY] * n, out_specs=tuple(out_specs) + (ANY,) * n,
        scratch_shapes=list(ride.scratch) if n else [],
        compiler_params=_params(*(("arbitrary",) * len(grid) if n else semantics)))(*operands, *(ride.arrays if n else ()))
    return res[:len(out_shape)], list(res[len(out_shape):])


def _flash_fwd(q, k, v, name, ride=None):
    s = q.shape[0]
    t = _attn_tile(s)
    c2 = ATTN_SCALE * LOG2E
    grid = (N_HEADS, s // t)

    def body(*refs):
        (q_ref, k_ref, v_ref, o_ref, lse_ref), start, finish = _ride_hooks(ride, refs, 3, 2, grid)
        start()
        i = pl.program_id(1)
        qv = q_ref[...]

        def chunk(j, carry, masked):
            m_old, l_old, acc = carry
            at = pl.ds(pl.multiple_of(j * t, t), t)
            sc = _raw_scores(qv, k_ref[at, :], masked)
            m_new = jnp.maximum(m_old, jnp.max(sc, axis=-1, keepdims=True))
            p = jnp.exp2((sc - m_new) * c2)
            alpha = jnp.exp2((m_old - m_new) * c2)
            l_new = alpha * l_old + jnp.sum(p, axis=-1, keepdims=True)
            acc = alpha * acc + jnp.dot(p.astype(BF16), v_ref[at, :], preferred_element_type=F32)
            return m_new, l_new, acc

        init = (jnp.full((t, 1), -jnp.inf, F32), jnp.zeros((t, 1), F32), jnp.zeros((t, HEAD_PAD), F32))
        carry = lax.fori_loop(0, i, lambda j, cr: chunk(j, cr, False), init)
        m_fin, l_fin, acc = chunk(i, carry, True)
        o_ref[...] = (acc / l_fin).astype(o_ref.dtype)
        lse_ref[...] = jnp.broadcast_to(m_fin * ATTN_SCALE + jnp.log(l_fin), (t, HEAD_PAD))
        finish()

    qo = pl.BlockSpec((t, HEAD_PAD), lambda h, i: (i, h))
    whole = pl.BlockSpec((s, HEAD_PAD), lambda h, i: (0, h))
    return _ride_call(
        ride, body, name, (jax.ShapeDtypeStruct(q.shape, BF16), jax.ShapeDtypeStruct(q.shape, F32)), grid,
        [qo, whole, whole], (qo, qo), ("parallel", "parallel"), (q, k, v))


def _attn_delta(do, o, name):
    s = o.shape[0]
    t = _attn_tile(s)

    def body(do_ref, o_ref, delta_ref, dob_ref):
        for h in range(N_HEADS):
            sl = slice(h * HEAD_PAD, (h + 1) * HEAD_PAD)
            dov = do_ref[:, sl]
            delta_ref[:, sl] = jnp.broadcast_to(jnp.sum(dov * o_ref[:, sl].astype(F32), axis=-1, keepdims=True),
                                                (t, HEAD_PAD))
            dob_ref[:, sl] = dov.astype(BF16)

    blk = _rows(t, N_HEADS * HEAD_PAD)
    return pl.pallas_call(
        body, name=name, out_shape=(jax.ShapeDtypeStruct(o.shape, F32), jax.ShapeDtypeStruct(o.shape, BF16)),
        grid=(s // t,), in_specs=[blk, blk], out_specs=(blk, blk), compiler_params=_params("parallel"))(do, o)


def _flash_bwd(q, k, v, do, lse, delta, name, ride=None):
    s = q.shape[0]
    t = _attn_tile(s)
    nt = s // t
    c2 = ATTN_SCALE * LOG2E
    grid = (N_HEADS, nt)

    def body(*refs):
        (q_ref, k_ref, v_ref, do_ref, lse_ref, delta_ref, dq_ref, dk_ref, dv_ref), start, finish = _ride_hooks(
            ride, refs, 6, 3, grid)
        start()
        j = pl.program_id(1)
        kv, vv = k_ref[...], v_ref[...]

        @pl.when(j == 0)
        def _():
            dq_ref[...] = jnp.zeros_like(dq_ref)

        def chunk(i, carry, masked):
            dk_acc, dv_acc = carry
            at = pl.ds(pl.multiple_of(i * t, t), t)
            qi, doi = q_ref[at, :], do_ref[at, :]
            sc = _raw_scores(qi, kv, masked)
            p = jnp.exp2(sc * c2 - lse_ref[at, pl.ds(0, 1)] * LOG2E)
            dp = lax.dot_general(doi, vv, (((1,), (1,)), ((), ())), preferred_element_type=F32)
            ds = (p * (dp - delta_ref[at, pl.ds(0, 1)])).astype(BF16)
            dv_acc = dv_acc + lax.dot_general(p.astype(BF16), doi, (((0,), (0,)), ((), ())), preferred_element_type=F32)
            dk_acc = dk_acc + lax.dot_general(ds, qi, (((0,), (0,)), ((), ())), preferred_element_type=F32)
            dq_ref[at, :] += jnp.dot(ds, kv, preferred_element_type=F32) * ATTN_SCALE
            return dk_acc, dv_acc

        zero = jnp.zeros((t, HEAD_PAD), F32)
        carry = chunk(j, (zero, zero), True)
        dk_acc, dv_acc = lax.fori_loop(j + 1, nt, lambda i, cr: chunk(i, cr, False), carry)
        dk_ref[...] = dk_acc * ATTN_SCALE
        dv_ref[...] = dv_acc.astype(BF16)
        finish()

    blk = pl.BlockSpec((t, HEAD_PAD), lambda h, j: (j, h))
    whole = pl.BlockSpec((s, HEAD_PAD), lambda h, j: (0, h))
    return _ride_call(
        ride, body, name, (jax.ShapeDtypeStruct(q.shape, F32), jax.ShapeDtypeStruct(q.shape, F32),
                           jax.ShapeDtypeStruct(q.shape, BF16)), grid,
        [whole, blk, blk, whole, whole, whole], (whole, blk, blk), ("parallel", "arbitrary"), (q, k, v, do, lse, delta))


def _conv_tile(s):
    return min(s, 256)


def _halo_before(t, width, cidx):
    per = t // CONV_HALO
    return pl.BlockSpec((CONV_HALO, width), lambda i: (jnp.maximum(i * per - 1, 0), cidx))


def _halo_after(t, width, cidx, n_tiles):
    per = t // CONV_HALO
    last = n_tiles * per - 1
    return pl.BlockSpec((CONV_HALO, width), lambda i: (jnp.minimum((i + 1) * per, last), cidx))


def _fill_glu(hbuf, ap_ref, gp_ref, a_ref, g_ref, t):
    first = pl.program_id(0) == 0
    hbuf[pl.ds(0, CONV_HALO), :] = jnp.where(first, 0.0, ap_ref[...].astype(F32) * _sigmoid(gp_ref[...].astype(F32)))
    hbuf[pl.ds(CONV_HALO, t), :] = a_ref[...].astype(F32) * _sigmoid(g_ref[...].astype(F32))


def _phase_copies(dst, src, t):
    n = t + CONV_HALO - SUBLANES
    for s in range(1, SUBLANES):
        dst[s, pl.ds(0, n), :] = src[pl.ds(s, n), :]


def _window(phases, src, k, t):
    if k % SUBLANES == 0:
        return src[pl.ds(k, t), :]
    return phases[k % SUBLANES, pl.ds(k - k % SUBLANES, t), :]


def _layer_norm_parts(co):
    mu = jnp.mean(co, axis=-1, keepdims=True)
    xc = co - mu
    rstd = lax.rsqrt(jnp.mean(xc * xc, axis=-1, keepdims=True) + EPS)
    return xc * rstd, rstd


def _conv_fwd(z, conv_w, conv_b, ln_g, ln_b, name):
    s = z.shape[0]
    t = _conv_tile(s)
    off = CONV_HALO - (CONV_W - 1)

    def body(ap_ref, gp_ref, a_ref, g_ref, w_ref, b_ref, lg_ref, lb_ref, hc_ref, co_ref, hbuf, hph):
        _fill_glu(hbuf, ap_ref, gp_ref, a_ref, g_ref, t)
        _phase_copies(hph, hbuf, t)
        acc = jnp.zeros((t, CONV_C), F32) + b_ref[...]
        for j in range(CONV_W):
            acc = acc + _window(hph, hbuf, off + j, t) * w_ref[pl.ds(j, 1), :]
        co_ref[...] = acc
        xh, _ = _layer_norm_parts(acc)
        y = xh * lg_ref[...] + lb_ref[...]
        hc_ref[...] = (y * _sigmoid(y)).astype(BF16)

    vec = _fixed((1, CONV_C))
    return pl.pallas_call(
        body, name=name, out_shape=(jax.ShapeDtypeStruct((s, CONV_C), BF16), jax.ShapeDtypeStruct((s, CONV_C), F32)),
        grid=(s // t,),
        in_specs=[_halo_before(t, *ZC_CONV_A), _halo_before(t, *ZC_CONV_G), _rows(t, *ZC_CONV_A), _rows(t, *ZC_CONV_G),
                  _fixed((CONV_HALO, CONV_C)), vec, vec, vec],
        out_specs=(_rows(t, CONV_C), _rows(t, CONV_C)),
        scratch_shapes=[pltpu.VMEM((t + CONV_HALO, CONV_C), F32), pltpu.VMEM((SUBLANES, t + CONV_HALO, CONV_C), F32)],
        compiler_params=_params("parallel"))(z, z, z, z, conv_w, conv_b.reshape(1, -1), ln_g.reshape(1, -1),
                                             ln_b.reshape(1, -1))


def _conv_bwd_norm(dhc, co, ln_g, ln_b, name):
    s = co.shape[0]
    t = min(s, 512)

    def body(dhc_ref, co_ref, lg_ref, lb_ref, dco_ref, dg_ref, db_ref, dcb_ref):
        xh, rstd = _layer_norm_parts(co_ref[...])
        y = xh * lg_ref[...] + lb_ref[...]
        sg = _sigmoid(y)
        dy = dhc_ref[...] * (sg * (1.0 + y * (1.0 - sg)))
        dxh = dy * lg_ref[...]
        dco = rstd * (dxh - jnp.mean(dxh, axis=-1, keepdims=True) - xh * jnp.mean(dxh * xh, axis=-1, keepdims=True))
        dco_ref[...] = dco

        @pl.when(pl.program_id(0) == 0)
        def _():
            dg_ref[...] = jnp.zeros_like(dg_ref)
            db_ref[...] = jnp.zeros_like(db_ref)
            dcb_ref[...] = jnp.zeros_like(dcb_ref)

        dg_ref[...] += jnp.sum(dy * xh, axis=0, keepdims=True)
        db_ref[...] += jnp.sum(dy, axis=0, keepdims=True)
        dcb_ref[...] += jnp.sum(dco, axis=0, keepdims=True)

    vec = _fixed((1, CONV_C))
    one = jax.ShapeDtypeStruct((1, CONV_C), F32)
    dco, dg, db, dcb = pl.pallas_call(
        body, name=name, out_shape=(jax.ShapeDtypeStruct((s, CONV_C), F32), one, one, one), grid=(s // t,),
        in_specs=[_rows(t, CONV_C), _rows(t, CONV_C), vec, vec], out_specs=(_rows(t, CONV_C), vec, vec, vec),
        compiler_params=_params("arbitrary"))(dhc, co, ln_g.reshape(1, -1), ln_b.reshape(1, -1))
    return dco, dg.reshape(-1), db.reshape(-1), dcb.reshape(-1)


def _conv_bwd_taps(dco, z, conv_w, dz, name):
    s = z.shape[0]
    t = _conv_tile(s)
    nt = s // t
    off = CONV_HALO - (CONV_W - 1)

    def body(ap_ref, gp_ref, a_ref, g_ref, d_ref, dn_ref, w_ref, _, du_ref, dw_ref, hbuf, dbuf, hph, dph):
        i = pl.program_id(0)
        _fill_glu(hbuf, ap_ref, gp_ref, a_ref, g_ref, t)
        dbuf[pl.ds(0, t), :] = d_ref[...]
        dbuf[pl.ds(t, CONV_HALO), :] = jnp.where(i == nt - 1, 0.0, dn_ref[...])
        _phase_copies(hph, hbuf, t)
        _phase_copies(dph, dbuf, t)

        @pl.when(i == 0)
        def _():
            dw_ref[...] = jnp.zeros_like(dw_ref)

        dcur = d_ref[...]
        dh = jnp.zeros((t, CONV_C), F32)
        for j in range(CONV_W):
            dh = dh + _window(dph, dbuf, CONV_W - 1 - j, t) * w_ref[pl.ds(j, 1), :]
            dw_ref[pl.ds(j, 1), :] += jnp.sum(dcur * _window(hph, hbuf, off + j, t), axis=0, keepdims=True)
        a, sg = a_ref[...].astype(F32), _sigmoid(g_ref[...].astype(F32))
        du_ref[:, pl.ds(0, CONV_C)] = (dh * sg).astype(BF16)
        du_ref[:, pl.ds(CONV_C, CONV_C)] = (dh * a * sg * (1.0 - sg)).astype(BF16)

    into = _into(dz, 7, 0)
    return pl.pallas_call(
        body, name=name, out_shape=(into["out_shape"], jax.ShapeDtypeStruct((CONV_HALO, CONV_C), F32)), grid=(nt,),
        in_specs=[_halo_before(t, *ZC_CONV_A), _halo_before(t, *ZC_CONV_G), _rows(t, *ZC_CONV_A), _rows(t, *ZC_CONV_G),
                  _rows(t, CONV_C), _halo_after(t, CONV_C, 0, nt), _fixed((CONV_HALO, CONV_C))] + into["in_specs"],
        out_specs=(_rows(t, *ZC_CONV), _fixed((CONV_HALO, CONV_C))), input_output_aliases=into["input_output_aliases"],
        scratch_shapes=[pltpu.VMEM((t + CONV_HALO, CONV_C), F32), pltpu.VMEM((t + CONV_HALO, CONV_C), F32),
                        pltpu.VMEM((SUBLANES, t + CONV_HALO, CONV_C), F32),
                        pltpu.VMEM((SUBLANES, t + CONV_HALO, CONV_C), F32)],
        compiler_params=_params("arbitrary"))(z, z, z, z, dco, dco, conv_w, dz)


def _pool_tile(s):
    return min(s, 512)


def _pool_counts(row0, n, window):
    rows = row0 + lax.broadcasted_iota(jnp.int32, (n, POOL_GD), 0)
    return jnp.minimum(rows + 1, window).astype(F32)


def _pool_diff(ubuf, gi, window, row0, t):
    lanes = pl.ds(gi * POOL_GD, POOL_GD)
    tot = ubuf[pl.ds(CONV_HALO, t), lanes]
    cur = tot
    for back in range(1, window):
        tot = tot + ubuf[pl.ds(CONV_HALO - back, t), lanes]
    return tot / _pool_counts(row0, t, window) - cur


def _pool_fwd(z, pool_w, pool_scale, name):
    s = z.shape[0]
    t = _pool_tile(s)

    def body(up_ref, u_ref, w_ref, sc_ref, m_ref, ubuf):
        i = pl.program_id(0)
        ubuf[pl.ds(0, CONV_HALO), :] = jnp.where(i == 0, 0.0, up_ref[...].astype(F32))
        ubuf[pl.ds(CONV_HALO, t), :] = u_ref[...].astype(F32)
        for gi, window in enumerate(POOL_WINDOWS):
            d = _pool_diff(ubuf, gi, window, i * t, t)
            mm = jnp.dot(d.astype(BF16), w_ref[gi].astype(BF16), preferred_element_type=F32)
            lanes = pl.ds(gi * POOL_GD, POOL_GD)
            m_ref[:, lanes] = (mm * sc_ref[:, lanes]).astype(BF16)

    return pl.pallas_call(
        body, name=name, out_shape=jax.ShapeDtypeStruct((s, POOL_C), BF16), grid=(s // t,),
        in_specs=[_halo_before(t, *ZC_POOL), _rows(t, *ZC_POOL), _fixed((POOL_G, POOL_GD, POOL_GD)), _fixed((1, POOL_C))],
        out_specs=_rows(t, POOL_C), scratch_shapes=[pltpu.VMEM((t + CONV_HALO, POOL_C), F32)],
        compiler_params=_params("parallel"))(z, z, pool_w, pool_scale.reshape(1, -1))


def _pool_bwd(dm, z, pool_w, pool_scale, dz, name):
    s = z.shape[0]
    t = _pool_tile(s)
    nt = s // t

    def body(up_ref, u_ref, dm_ref, dmn_ref, w_ref, sc_ref, _, du_ref, dw_ref, dsc_ref, ubuf, ebuf):
        i = pl.program_id(0)
        ubuf[pl.ds(0, CONV_HALO), :] = jnp.where(i == 0, 0.0, up_ref[...].astype(F32))
        ubuf[pl.ds(CONV_HALO, t), :] = u_ref[...].astype(F32)

        @pl.when(i == 0)
        def _():
            dw_ref[...] = jnp.zeros_like(dw_ref)
            dsc_ref[...] = jnp.zeros_like(dsc_ref)

        dm_next = jnp.where(i == nt - 1, 0.0, dmn_ref[...])
        for gi, window in enumerate(POOL_WINDOWS):
            lanes = pl.ds(gi * POOL_GD, POOL_GD)
            wb = w_ref[gi].astype(BF16)
            scale = sc_ref[:, lanes]
            d = _pool_diff(ubuf, gi, window, i * t, t).astype(BF16)
            mm = jnp.dot(d, wb, preferred_element_type=F32)
            dmv = dm_ref[:, lanes]
            dsc_ref[:, lanes] += jnp.sum(dmv * mm, axis=0, keepdims=True)
            dmm = (dmv * scale).astype(BF16)
            dw_ref[gi] += lax.dot_general(d, dmm, (((0,), (0,)), ((), ())), preferred_element_type=F32)
            dd = lax.dot_general(dmm, wb, (((1,), (1,)), ((), ())), preferred_element_type=F32)
            dd_next = lax.dot_general((dm_next[:, gi * POOL_GD:(gi + 1) * POOL_GD] * scale).astype(BF16), wb,
                                      (((1,), (1,)), ((), ())), preferred_element_type=F32)
            ebuf[pl.ds(0, t), lanes] = dd / _pool_counts(i * t, t, window)
            ebuf[pl.ds(t, CONV_HALO), lanes] = dd_next / _pool_counts((i + 1) * t, CONV_HALO, window)
            du = -dd
            for ahead in range(window):
                du = du + ebuf[pl.ds(ahead, t), lanes]
            du_ref[:, lanes] = du.astype(BF16)

    into = _into(dz, 6, 0)
    du, dw, dsc = pl.pallas_call(
        body, name=name,
        out_shape=(into["out_shape"], jax.ShapeDtypeStruct((POOL_G, POOL_GD, POOL_GD), F32),
                   jax.ShapeDtypeStruct((1, POOL_C), F32)), grid=(nt,),
        in_specs=[_halo_before(t, *ZC_POOL), _rows(t, *ZC_POOL), _rows(t, POOL_C), _halo_after(t, POOL_C, 0, nt),
                  _fixed((POOL_G, POOL_GD, POOL_GD)), _fixed((1, POOL_C))] + into["in_specs"],
        out_specs=(_rows(t, *ZC_POOL), _fixed((POOL_G, POOL_GD, POOL_GD)), _fixed((1, POOL_C))),
        input_output_aliases=into["input_output_aliases"],
        scratch_shapes=[pltpu.VMEM((t + CONV_HALO, POOL_C), F32), pltpu.VMEM((t + CONV_HALO, POOL_C), F32)],
        compiler_params=_params("arbitrary"))(z, z, dm, dm, pool_w, pool_scale.reshape(1, -1), dz)
    return du, dw, dsc.reshape(-1)


def _gate_specs(ts):
    width, first = ZC_GATE
    return [_rows(ts, width, first + b) for b in range(3)]


def _merge_fwd(z, ys, name):
    s = z.shape[0]
    ts = min(s, 256)

    def body(g0, g1, g2, y0, y1, y2, o_ref):
        o_ref[...] = sum(_sigmoid(g[...].astype(F32)) * y[...].astype(F32)
                         for g, y in ((g0, y0), (g1, y1), (g2, y2))).astype(BF16)

    return pl.pallas_call(
        body, name=name, out_shape=jax.ShapeDtypeStruct((s, D_MODEL), BF16), grid=(s // ts,),
        in_specs=_gate_specs(ts) + [_rows(ts, D_MODEL)] * 3, out_specs=_rows(ts, D_MODEL),
        compiler_params=_params("parallel"))(z, z, z, *ys)


def _merge_bwd(z, ys, dmerged, name):
    s = z.shape[0]
    ts = min(s, 256)

    def body(g0, g1, g2, y0, y1, y2, dm_ref, dy0, dy1, dy2, dz_ref):
        dmv = dm_ref[...]
        for b, (g_ref, y_ref, dy_ref) in enumerate(((g0, y0, dy0), (g1, y1, dy1), (g2, y2, dy2))):
            sg = _sigmoid(g_ref[...].astype(F32))
            dy_ref[...] = (dmv * sg).astype(BF16)
            dz_ref[:, pl.ds(b * D_MODEL, D_MODEL)] = (dmv * y_ref[...].astype(F32) * sg * (1.0 - sg)).astype(BF16)

    out = jax.ShapeDtypeStruct((s, D_MODEL), BF16)
    return pl.pallas_call(
        body, name=name, out_shape=(out,) * 3 + (jax.ShapeDtypeStruct((s, Z_W), BF16),), grid=(s // ts,),
        in_specs=_gate_specs(ts) + [_rows(ts, D_MODEL)] * 4,
        out_specs=(_rows(ts, D_MODEL),) * 3 + (_rows(ts, *ZC_GATES),),
        compiler_params=_params("parallel"))(z, z, z, *ys, dmerged)


def _ffn_up_fwd(h, w_gate, w_up, name):
    s, d = h.shape
    nb = w_gate.shape[2]
    f = N_DEV * nb
    tm, n_blk = min(s, 1024), 2
    tn = n_blk * nb
    blk = pl.BlockSpec((tm, tn), lambda i, j: (i, j))
    wspec = pl.BlockSpec((n_blk, d, nb), lambda i, j: (j, 0, 0))

    def body(h_ref, wg_ref, wu_ref, hg_ref, hu_ref, act_ref):
        hv = h_ref[...]
        g = jnp.dot(hv, jnp.concatenate([wg_ref[c] for c in range(n_blk)], axis=1), preferred_element_type=F32)
        u = jnp.dot(hv, jnp.concatenate([wu_ref[c] for c in range(n_blk)], axis=1), preferred_element_type=F32)
        hg_ref[...] = g.astype(hg_ref.dtype)
        hu_ref[...] = u.astype(hu_ref.dtype)
        act_ref[...] = (g * _sigmoid(g) * u).astype(BF16)

    return pl.pallas_call(
        body, name=name,
        out_shape=(jax.ShapeDtypeStruct((s, f), BF16),) * 3,
        grid=(s // tm, f // tn), in_specs=[pl.BlockSpec((tm, d), lambda i, j: (i, 0)), wspec, wspec],
        out_specs=(blk, blk, blk), compiler_params=_params("parallel", "parallel"))(h, w_gate, w_up)


def _ffn_down_bwd(dfo, w_down, hg, hu, name):
    s, d = dfo.shape
    f = w_down.shape[0]
    tm, tn = min(s, 1024), _tile(f, 1024)
    blk = pl.BlockSpec((tm, tn), lambda i, j: (i, j))

    def body(d_ref, w_ref, g_ref, u_ref, dg_ref, du_ref):
        dact = lax.dot_general(d_ref[...], w_ref[...], (((1,), (1,)), ((), ())), preferred_element_type=F32)
        g = g_ref[...].astype(F32)
        sg = _sigmoid(g)
        dg_ref[...] = (dact * u_ref[...].astype(F32) * (sg * (1.0 + g * (1.0 - sg)))).astype(BF16)
        du_ref[...] = (dact * g * sg).astype(BF16)

    out = jax.ShapeDtypeStruct((s, f), BF16)
    return pl.pallas_call(
        body, name=name, out_shape=(out, out), grid=(s // tm, f // tn),
        in_specs=[pl.BlockSpec((tm, d), lambda i, j: (i, 0)), pl.BlockSpec((tn, d), lambda i, j: (j, 0)), blk, blk],
        out_specs=(blk, blk), compiler_params=_params("parallel", "parallel"))(dfo, w_down, hg, hu)


def _loss_grad(y, target, name):
    s, d = y.shape
    ts = min(s, 512)

    def body(y_ref, t_ref, dy_ref, sq_ref):
        e = y_ref[...] - t_ref[...]
        dy_ref[...] = e / d

        @pl.when(pl.program_id(0) == 0)
        def _():
            sq_ref[...] = jnp.zeros_like(sq_ref)

        sq_ref[...] += jnp.sum(e * e, axis=0, keepdims=True)

    return pl.pallas_call(
        body, name=name, out_shape=(jax.ShapeDtypeStruct((s, d), F32), jax.ShapeDtypeStruct((1, d), F32)),
        grid=(s // ts,), in_specs=[_rows(ts, d), _rows(ts, d)], out_specs=(_rows(ts, d), _fixed((1, d))),
        compiler_params=_params("arbitrary"))(y, target)


def _adamw(w, g, m, v, name):
    shape = w.shape
    cols = shape[-1]
    keep3 = w.ndim == 3 and shape[1] < SUBLANES
    view = shape if keep3 else (math.prod(shape[:-1]), cols)
    rows = view[0]
    if keep3:
        cap = max(1, (1 << 20) // (SUBLANES * cols * 4))
        tr = max(t for t in range(1, cap + 1) if rows % t == 0)
    else:
        tr = _row_tile(rows, cols * 4)

    def body(w_ref, g_ref, m_ref, v_ref, d_ref, mo_ref, vo_ref):
        gv = g_ref[...]
        mn = B1 * m_ref[...] + (1.0 - B1) * gv
        vn = B2 * v_ref[...] + (1.0 - B2) * (gv * gv)
        m_hat = mn / (1.0 - B1 ** STEP)
        v_hat = vn / (1.0 - B2 ** STEP)
        d_ref[...] = -LR * (m_hat / (jnp.sqrt(v_hat) + ADAM_EPS) + WD * w_ref[...])
        mo_ref[...] = mn
        vo_ref[...] = vn

    spec = pl.BlockSpec((tr,) + view[1:], lambda i: (i,) + (0,) * (len(view) - 1))
    out = jax.ShapeDtypeStruct(view, F32)
    res = pl.pallas_call(
        body, name=name, out_shape=(out,) * 3, grid=(rows // tr,), in_specs=[spec] * 4, out_specs=(spec,) * 3,
        compiler_params=_params("parallel"))(*[t.reshape(view) for t in (w, g, m, v)])
    return tuple(r.reshape(shape) for r in res)


LANE_MAJOR = ("w_uq", "w_uk", "w_uv", "w_gate", "w_up")


def _lane_major(name, a):
    if name == "w_in":
        return a.transpose(2, 0, 1)
    if name in LANE_MAJOR:
        return a.transpose(0, 2, 1)
    return a


def _from_lane_major(name, a):
    if name == "w_in":
        return a.transpose(1, 2, 0)
    return _lane_major(name, a)


ANY = pl.BlockSpec(memory_space=pl.ANY)


class _GatherRide:
    def __init__(self, arrays):
        n = len(arrays)
        self.arrays = list(arrays)
        self.out_shape = [jax.ShapeDtypeStruct((N_DEV,) + a.shape, a.dtype) for a in arrays]
        self.scratch = [pltpu.SemaphoreType.DMA((n, 7)), pltpu.SemaphoreType.DMA((n, 7)), pltpu.SemaphoreType.DMA((n,))]

    def _copies(self, ins, outs, sems):
        send_sems, recv_sems, local_sems = sems
        n = len(self.arrays)
        x, y, c = lax.axis_index("x"), lax.axis_index("y"), lax.axis_index("c")
        me, sibling = (x, y, c), (x, y, 1 - c)
        chips = [(1 - x, y), (x, 1 - y), (1 - x, 1 - y)]

        def slot(a, px, py, pc):
            return outs[a].at[4 * px + 2 * py + pc]

        def copy(a, k, block, to, src=None):
            return pltpu.make_async_remote_copy(
                src_ref=slot(a, *block) if src is None else src, dst_ref=slot(a, *block), send_sem=send_sems.at[a, k],
                recv_sem=recv_sems.at[a, k], device_id=to, device_id_type=MESH)

        mine = [pltpu.make_async_copy(ins[a], slot(a, *me), local_sems.at[a]) for a in range(n)]
        first = []
        for a in range(n):
            first.append(copy(a, 0, me, sibling, src=ins[a]))
            first += [copy(a, 1 + j, me, (*chip, c), src=ins[a]) for j, chip in enumerate(chips)]
        return n, me, sibling, chips, c, copy, mine, first

    def start(self, ins, outs, sems):
        _, _, _, _, _, _, mine, first = self._copies(ins, outs, sems)
        for cp in mine + first:
            cp.start()

    def finish(self, ins, outs, sems):
        n, me, sibling, chips, c, copy, mine, first = self._copies(ins, outs, sems)
        passed = []
        for j, chip in enumerate(chips):
            for a in range(n):
                copy(a, 1 + j, (*chip, c), me).wait_recv()
                passed.append(copy(a, 4 + j, (*chip, c), sibling))
                passed[-1].start()
        for a in range(n):
            copy(a, 0, sibling, me).wait_recv()
            for j, chip in enumerate(chips):
                copy(a, 4 + j, (*chip, 1 - c), me).wait_recv()
        for cp in first + passed:
            cp.wait_send()
        for cp in mine:
            cp.wait()


class _ReduceRide:
    def __init__(self, arrays):
        n = len(arrays)
        self.arrays = list(arrays)
        self.out_shape = [jax.ShapeDtypeStruct(a.shape, a.dtype) for a in arrays]
        self.scratch = [pltpu.SemaphoreType.DMA((n, 7)), pltpu.SemaphoreType.DMA((n, 7)), pltpu.SemaphoreType.DMA((n,))]

    def _copies(self, ins, outs, sems):
        send_sems, recv_sems, local_sems = sems
        n = len(self.arrays)
        x, y, c = lax.axis_index("x"), lax.axis_index("y"), lax.axis_index("c")
        mine = [pltpu.make_async_copy(ins[a].at[4 * x + 2 * y + c], outs[a].at[0], local_sems.at[a]) for a in range(n)]
        copies = []
        for a in range(n):
            for k in range(1, N_DEV):
                px = 1 - x if k & 4 else x
                py = 1 - y if k & 2 else y
                pc = 1 - c if k & 1 else c
                copies.append(pltpu.make_async_remote_copy(
                    src_ref=ins[a].at[4 * px + 2 * py + pc], dst_ref=outs[a].at[k], send_sem=send_sems.at[a, k - 1],
                    recv_sem=recv_sems.at[a, k - 1], device_id=(px, py, pc), device_id_type=MESH))
        return mine, copies

    def start(self, ins, outs, sems):
        mine, copies = self._copies(ins, outs, sems)
        for cp in mine + copies:
            cp.start()

    def finish(self, ins, outs, sems):
        mine, copies = self._copies(ins, outs, sems)
        for cp in copies + mine:
            cp.wait()


def _run_ride(ride, name):
    n = len(ride.arrays)

    def body(*refs):
        ins, outs, sems = refs[:n], refs[n:2 * n], refs[2 * n:]
        ride.start(ins, outs, sems)
        ride.finish(ins, outs, sems)

    return pl.pallas_call(body, name=name, out_shape=ride.out_shape, in_specs=[ANY] * n, out_specs=[ANY] * n,
                          scratch_shapes=ride.scratch)(*ride.arrays)


def _all_gather(arrays, name):
    return _run_ride(_GatherRide(arrays), name)


def _swap_with_sibling(arrays, name):
    n = len(arrays)

    def body(*refs):
        ins, outs = refs[:n], refs[n:2 * n]
        send_sems, recv_sems = refs[2 * n:]
        x, y, c = lax.axis_index("x"), lax.axis_index("y"), lax.axis_index("c")
        copies = [pltpu.make_async_remote_copy(
            src_ref=ins[a].at[1 - c], dst_ref=outs[a], send_sem=send_sems.at[a], recv_sem=recv_sems.at[a],
            device_id=(x, y, 1 - c), device_id_type=MESH) for a in range(n)]
        for cp in copies:
            cp.start()
        for cp in copies:
            cp.wait()

    return pl.pallas_call(
        body, name=name, out_shape=[jax.ShapeDtypeStruct(a.shape[1:], a.dtype) for a in arrays],
        in_specs=[ANY] * n, out_specs=[ANY] * n,
        scratch_shapes=[pltpu.SemaphoreType.DMA((n,)), pltpu.SemaphoreType.DMA((n,))])(*arrays)


class _ChipExchangeRide:
    def __init__(self, arrays):
        n = len(arrays)
        self.arrays = list(arrays)
        self.out_shape = [jax.ShapeDtypeStruct(a.shape, a.dtype) for a in arrays]
        self.scratch = [pltpu.SemaphoreType.DMA((n, 3)), pltpu.SemaphoreType.DMA((n, 3)), pltpu.SemaphoreType.DMA((n,))]

    def _copies(self, ins, outs, sems):
        send_sems, recv_sems, local_sems = sems
        n = len(self.arrays)
        x, y, c = lax.axis_index("x"), lax.axis_index("y"), lax.axis_index("c")
        partners = [(x, 1 - y), (1 - x, y), (1 - x, 1 - y)]
        mine = [pltpu.make_async_copy(ins[a].at[2 * x + y], outs[a].at[0], local_sems.at[a]) for a in range(n)]
        copies = [pltpu.make_async_remote_copy(
            src_ref=ins[a].at[2 * px + py], dst_ref=outs[a].at[1 + k], send_sem=send_sems.at[a, k],
            recv_sem=recv_sems.at[a, k], device_id=(px, py, c), device_id_type=MESH)
            for a in range(n) for k, (px, py) in enumerate(partners)]
        return mine, copies

    def start(self, ins, outs, sems):
        mine, copies = self._copies(ins, outs, sems)
        for cp in mine + copies:
            cp.start()

    def finish(self, ins, outs, sems):
        mine, copies = self._copies(ins, outs, sems)
        for cp in copies + mine:
            cp.wait()


class _Combo:
    def __init__(self, rides):
        self.rides = rides
        self.arrays = [a for r in rides for a in r.arrays]
        self.out_shape = [o for r in rides for o in r.out_shape]
        self.scratch = [sc for r in rides for sc in r.scratch]

    def _parts(self, ins, outs, sems):
        at_a = at_s = 0
        for r in self.rides:
            na, ns = len(r.arrays), len(r.scratch)
            yield r, ins[at_a:at_a + na], outs[at_a:at_a + na], sems[at_s:at_s + ns]
            at_a, at_s = at_a + na, at_s + ns

    def start(self, ins, outs, sems):
        for r, i, o, sm in self._parts(ins, outs, sems):
            r.start(i, o, sm)

    def finish(self, ins, outs, sems):
        for r, i, o, sm in self._parts(ins, outs, sems):
            r.finish(i, o, sm)


def _as_rows(a, lead):
    return a.reshape(a.shape[:lead] + (math.prod(a.shape[lead:-1]), a.shape[-1]))


def _add_pairs(a, b, name):
    a2, b2 = _as_rows(a, 0), _as_rows(b, 0)
    rows, cols = a2.shape
    tr = _row_tile(rows, cols * 4)

    def body(a_ref, b_ref, o_ref):
        o_ref[...] = (a_ref[...].astype(F32) + b_ref[...].astype(F32)).astype(o_ref.dtype)

    spec = _rows(tr, cols)
    out = pl.pallas_call(body, name=name, out_shape=jax.ShapeDtypeStruct(a2.shape, a.dtype), grid=(rows // tr,),
                         in_specs=[spec, spec], out_specs=spec, compiler_params=_params("parallel"))(a2, b2)
    return out.reshape(a.shape)


def _sum_blocks(a, name):
    a3 = _as_rows(a, 1)
    n, rows, cols = a3.shape
    tr = _row_tile(rows, n * cols * 4)

    def body(a_ref, o_ref):
        tot = a_ref[0].astype(F32)
        for k in range(1, n):
            tot = tot + a_ref[k].astype(F32)
        o_ref[...] = tot

    out = pl.pallas_call(body, name=name, out_shape=jax.ShapeDtypeStruct((rows, cols), F32), grid=(rows // tr,),
                         in_specs=[pl.BlockSpec((n, tr, cols), lambda j: (0, j, 0))], out_specs=_rows(tr, cols),
                         compiler_params=_params("parallel"))(a3)
    return out.reshape(a.shape[1:])


MIX_GROUPS = ("w_in", "w_uq", "w_uk", "w_uv", "w_attn_o", "w_conv_o", "w_pool_o", "w_mix_o")
FFN_GROUPS = ("w_gate", "w_up", "w_down")
MIX_EARLY = ("w_attn_o", "w_conv_o", "w_pool_o", "w_mix_o")
MIX_LATE = ("w_in", "w_uq", "w_uk", "w_uv")


def _pad_axis(a, axis, size):
    pad = [(0, 0)] * a.ndim
    pad[axis] = (0, size - a.shape[axis])
    return jnp.pad(a, pad)


def _local_groups(sh, l):
    out = {n: sh[n][l] for n in BIG}
    for n in ("w_uq", "w_uk", "w_uv"):
        out[n] = _pad_axis(out[n], -1, HEAD_PAD)
    for n in ("w_gate", "w_up"):
        out[n] = _pad_axis(out[n], -1, FF_SHARD_PAD)
    out["w_down"] = _pad_axis(out["w_down"], 0, FF_SHARD_PAD)
    return {n: v.astype(BF16) for n, v in out.items()}


def _arrange_w_in(blocks):
    parts, pos = [], 0
    for ref_lo, ref_hi, at in sorted(W_IN_PIECES, key=lambda p: p[2]):
        if at > pos:
            parts.append(jnp.zeros((blocks.shape[1], at - pos), blocks.dtype))
        for d in range(N_DEV):
            lo, hi = max(ref_lo, d * W_IN_SHARD), min(ref_hi, (d + 1) * W_IN_SHARD)
            if lo < hi:
                parts.append(blocks[d][:, lo - d * W_IN_SHARD:hi - d * W_IN_SHARD])
        pos = at + ref_hi - ref_lo
    if pos < Z_W:
        parts.append(jnp.zeros((blocks.shape[1], Z_W - pos), blocks.dtype))
    return jnp.concatenate(parts, axis=1)


def _w_in_shard(g, d):
    parts = []
    for ref_lo, ref_hi, at in W_IN_PIECES:
        lo, hi = max(ref_lo, d * W_IN_SHARD), min(ref_hi, (d + 1) * W_IN_SHARD)
        if lo < hi:
            parts.append(g[:, at + lo - ref_lo:at + hi - ref_lo])
    return jnp.concatenate(parts, axis=1)


def _mixer_weights(gat):
    w = dict(gat)
    w["w_in"] = _arrange_w_in(gat["w_in"])
    attn_o = gat["w_attn_o"].reshape(N_DEV, N_HEADS, V_HEAD, LANES)
    w["w_attn_o"] = _pad_axis(attn_o, 2, HEAD_PAD).reshape(N_DEV, N_HEADS * HEAD_PAD, LANES)
    w["w_mix_o"] = gat["w_mix_o"].reshape(D_MODEL, D_MODEL)
    return w


def _ffn_weights(gat):
    return {"w_gate": gat["w_gate"], "w_up": gat["w_up"], "w_down": gat["w_down"].reshape(D_FF_PAD, D_MODEL)}


def _mixer_grad_groups(gb):
    g = dict(gb)
    if "w_in" in gb:
        g["w_in"] = jnp.stack([_w_in_shard(gb["w_in"], d) for d in range(N_DEV)])
    if "w_attn_o" in gb:
        attn_o = gb["w_attn_o"].reshape(N_DEV, N_HEADS, HEAD_PAD, LANES)[:, :, :V_HEAD]
        g["w_attn_o"] = attn_o.reshape(N_DEV, N_HEADS * V_HEAD, LANES)
    if "w_mix_o" in gb:
        g["w_mix_o"] = gb["w_mix_o"].reshape(N_DEV, D_MODEL // N_DEV, D_MODEL)
    return g


def _ffn_grad_groups(gb):
    return {"w_gate": gb["w_gate"], "w_up": gb["w_up"], "w_down": gb["w_down"].reshape(N_DEV, FF_SHARD_PAD, D_MODEL)}


def _grads_from_groups(tot):
    g = dict(tot)
    g["w_uq"] = tot["w_uq"][:, :QK_NOPE + QK_ROPE]
    g["w_uk"], g["w_uv"] = tot["w_uk"][:, :QK_NOPE], tot["w_uv"][:, :V_HEAD]
    g["w_gate"], g["w_up"] = tot["w_gate"][:, :FF_SHARD], tot["w_up"][:, :FF_SHARD]
    g["w_down"] = tot["w_down"][:FF_SHARD]
    return g


SMALL_GROUPS = (
    (D_MODEL, ("mix_norm_pre", "mix_norm_post", "ffn_norm_pre", "ffn_norm_post")),
    (CONV_C, ("conv_w", "conv_b", "conv_ln_g", "conv_ln_b", "pool_scale")),
    (Q_RANK, ("q_norm",)), (KV_RANK, ("kv_norm",)), (POOL_GD, ("pool_w",)),
)


def _small_rows(name):
    return {"conv_w": CONV_HALO, "pool_w": POOL_G * POOL_GD}.get(name, SUBLANES)


def _small_groups(small):
    out = []
    for width, names in SMALL_GROUPS:
        parts = []
        for l in range(DEPTH):
            for n in names:
                part = small[l][n].reshape(-1, width)
                parts.append(_pad_axis(part, 0, _small_rows(n)))
        out.append(jnp.concatenate(parts, axis=0))
    return out


def _small_from_groups(groups):
    shapes = {"conv_w": (CONV_W, CONV_C), "pool_w": (POOL_G, POOL_GD, POOL_GD)}
    out = {}
    for (width, names), g in zip(SMALL_GROUPS, groups):
        row = 0
        for l in range(DEPTH):
            for n in names:
                rows = _small_rows(n)
                real = {"conv_w": CONV_W, "pool_w": POOL_G * POOL_GD}.get(n, 1)
                out.setdefault(n, []).append(g[row:row + real].reshape(shapes.get(n, (width,))))
                row += rows
    return {n: jnp.stack(v) for n, v in out.items()}


def _mixer_fwd(x, tables, w, sm, tag, ride):
    nm = lambda n: f"{n}_{tag}"
    h = _rms_fwd(x, (D_MODEL, 0), sm["mix_norm_pre"], BF16, nm("mix_pre_norm"))
    z = _matmul(h, w["w_in"], "nn", BF16, nm("in_proj"))
    cq = _rms_fwd(z, ZC_Q, sm["q_norm"], BF16, nm("q_norm"))
    ckv = _rms_fwd(z, ZC_KV, sm["kv_norm"], BF16, nm("kv_norm"))
    qf = _matmul(cq, w["w_uq"], "nn", F32, nm("q_up"))
    kf = _matmul(ckv, w["w_uk"], "nn", F32, nm("k_up"))
    v = _matmul(ckv, w["w_uv"], "nn", BF16, nm("v_up"))
    q, k = _rope_qk_fwd(qf, kf, z, tables, nm("rope_qk"))
    (o, lse), rode = _flash_fwd(q, k, v, nm("flash_fwd"), ride)
    y_attn = _matmul(o, w["w_attn_o"], "nn", BF16, nm("attn_out"))
    hc, co = _conv_fwd(z, sm["conv_w"], sm["conv_b"], sm["conv_ln_g"], sm["conv_ln_b"], nm("conv_fwd"))
    y_conv = _matmul(hc, w["w_conv_o"], "nn", BF16, nm("conv_out"))
    pm = _pool_fwd(z, sm["pool_w"], sm["pool_scale"], nm("pool_fwd"))
    y_pool = _matmul(pm, w["w_pool_o"], "nn", BF16, nm("pool_out"))
    ys = (y_attn, y_conv, y_pool)
    merged = _merge_fwd(z, ys, nm("merge_fwd"))
    mo = _matmul(merged, w["w_mix_o"], "nn", F32, nm("mix_out"))
    x_mid = _rms_fwd(mo, (D_MODEL, 0), sm["mix_norm_post"], F32, nm("mix_post_norm"), res=x)
    saved = dict(x=x, h=h, z=z, cq=cq, ckv=ckv, q=q, k=k, v=v, o=o, lse=lse, hc=hc, co=co, pm=pm, ys=ys, merged=merged,
                 mo=mo)
    return x_mid, saved, rode


def _ffn_fwd(x_mid, w, sm, tag):
    nm = lambda n: f"{n}_{tag}"
    h2 = _rms_fwd(x_mid, (D_MODEL, 0), sm["ffn_norm_pre"], BF16, nm("ffn_pre_norm"))
    hg, hu, act = _ffn_up_fwd(h2, w["w_gate"], w["w_up"], nm("ffn_up_fwd"))
    fo = _matmul(act, w["w_down"], "nn", F32, nm("ffn_down"))
    out = _rms_fwd(fo, (D_MODEL, 0), sm["ffn_norm_post"], F32, nm("ffn_post_norm"), res=x_mid)
    saved = dict(x_mid=x_mid, h2=h2, hg=hg, hu=hu, act=act, fo=fo)
    return out, saved


def _ffn_bwd(dout, sv, w, sm, tag):
    nm = lambda n: f"{n}_{tag}"
    gb, gs = {}, {}
    dfo, gs["ffn_norm_post"] = _rms_bwd(sv["fo"], (D_MODEL, 0), sm["ffn_norm_post"], dout, BF16, nm("ffn_post_norm_bwd"))
    gb["w_down"] = _matmul(sv["act"], dfo, "tn", BF16, nm("ffn_down_dw"))
    dhg, dhu = _ffn_down_bwd(dfo, w["w_down"], sv["hg"], sv["hu"], nm("ffn_down_bwd"))
    dh2_g = _matmul(dhg, w["w_gate"], "nt", F32, nm("ffn_gate_dx"))
    dh2 = _matmul(dhu, w["w_up"], "nt", F32, nm("ffn_up_dx"), add=dh2_g)
    gb["w_gate"] = _matmul(sv["h2"], dhg, "tn", BF16, nm("ffn_gate_dw"), blocked=True)
    gb["w_up"] = _matmul(sv["h2"], dhu, "tn", BF16, nm("ffn_up_dw"), blocked=True)
    dmid, gs["ffn_norm_pre"] = _rms_bwd(sv["x_mid"], (D_MODEL, 0), sm["ffn_norm_pre"], dh2, F32, nm("ffn_pre_norm_bwd"),
                                        add=dout)
    return dmid, gb, gs


def _mixer_bwd(dmid, sv, tables, w, sm, tag, make_ride):
    nm = lambda n: f"{n}_{tag}"
    gb, gs = {}, {}
    dmo, gs["mix_norm_post"] = _rms_bwd(sv["mo"], (D_MODEL, 0), sm["mix_norm_post"], dmid, BF16, nm("mix_post_norm_bwd"))
    dmerged = _matmul(dmo, w["w_mix_o"], "nt", F32, nm("mix_out_dx"))
    gb["w_mix_o"] = _matmul(sv["merged"], dmo, "tn", BF16, nm("mix_out_dw"))
    dya, dyc, dyp, dz = _merge_bwd(sv["z"], sv["ys"], dmerged, nm("merge_bwd"))
    dpm = _matmul(dyp, w["w_pool_o"], "nt", F32, nm("pool_out_dx"))
    gb["w_pool_o"] = _matmul(sv["pm"], dyp, "tn", BF16, nm("pool_out_dw"), blocked=True)
    dz, gs["pool_w"], gs["pool_scale"] = _pool_bwd(dpm, sv["z"], sm["pool_w"], sm["pool_scale"], dz, nm("pool_bwd"))
    dhc = _matmul(dyc, w["w_conv_o"], "nt", F32, nm("conv_out_dx"))
    gb["w_conv_o"] = _matmul(sv["hc"], dyc, "tn", BF16, nm("conv_out_dw"), blocked=True)
    dco, gs["conv_ln_g"], gs["conv_ln_b"], gs["conv_b"] = _conv_bwd_norm(dhc, sv["co"], sm["conv_ln_g"], sm["conv_ln_b"],
                                                                        nm("conv_bwd_norm"))
    dz, gs["conv_w"] = _conv_bwd_taps(dco, sv["z"], sm["conv_w"], dz, nm("conv_bwd_taps"))
    do = _matmul(dya, w["w_attn_o"], "nt", F32, nm("attn_out_dx"))
    gb["w_attn_o"] = _matmul(sv["o"], dya, "tn", BF16, nm("attn_out_dw"), blocked=True)
    delta, dob = _attn_delta(do, sv["o"], nm("attn_delta"))
    (dq, dk, dv), rode = _flash_bwd(sv["q"], sv["k"], sv["v"], dob, sv["lse"], delta, nm("flash_bwd"), make_ride(gb))
    dqf, dkf, dz = _rope_qk_bwd(dq, dk, tables, dz, nm("rope_qk_bwd"))
    dcq_n = _matmul(dqf, w["w_uq"], "nt", F32, nm("q_up_dx"))
    gb["w_uq"] = _matmul(sv["cq"], dqf, "tn", BF16, nm("q_up_dw"), blocked=True)
    dckv_k = _matmul(dkf, w["w_uk"], "nt", F32, nm("k_up_dx"))
    dckv_n = _matmul(dv, w["w_uv"], "nt", F32, nm("v_up_dx"), add=dckv_k)
    gb["w_uk"] = _matmul(sv["ckv"], dkf, "tn", BF16, nm("k_up_dw"), blocked=True)
    gb["w_uv"] = _matmul(sv["ckv"], dv, "tn", BF16, nm("v_up_dw"), blocked=True)
    dz, gs["q_norm"] = _rms_bwd(sv["z"], ZC_Q, sm["q_norm"], dcq_n, BF16, nm("q_norm_bwd"), dz=dz)
    dz, gs["kv_norm"] = _rms_bwd(sv["z"], ZC_KV, sm["kv_norm"], dckv_n, BF16, nm("kv_norm_bwd"), dz=dz)
    dh =_matmul(dz, w["w_in"], "nt", F32, nm("in_proj_dx"))
    gb["w_in"] = _matmul(sv["h"], dz, "tn", BF16, nm("in_proj_dw"))
    dx, gs["mix_norm_pre"] = _rms_bwd(sv["x"], (D_MODEL, 0), sm["mix_norm_pre"], dh, F32, nm("mix_pre_norm_bwd"), add=dmid)
    return dx, gb, gs, rode


def _part_groups(part):
    return {"mix": MIX_GROUPS, "ffn": FFN_GROUPS, "early": MIX_EARLY, "late": MIX_LATE}[part]


class _Plan:
    def __init__(self, shards, conv_w):
        self.local = [_local_groups(shards, l) for l in range(DEPTH)]
        self.conv_w = conv_w
        self.gat, self.send, self.recv = {}, {}, {}

    @staticmethod
    def _riders(l):
        return [(l, "ffn")] + ([(l + 1, "mix")] if l + 1 < DEPTH else [])

    @staticmethod
    def _grad_riders(l):
        return [(l, "ffn"), (l, "early")] + ([(l + 1, "late")] if l + 1 < DEPTH else [])

    def gather_first(self):
        out = _all_gather([self.local[0][g] for g in MIX_GROUPS] + [self.conv_w], "gather_mixer_l0")
        self.gat[(0, "mix")] = dict(zip(MIX_GROUPS, out[:-1]))
        return out[-1]

    def fwd_ride(self, l):
        return _GatherRide([self.local[ll][g] for ll, part in self._riders(l) for g in _part_groups(part)])

    def fwd_done(self, l, outs):
        outs = list(outs)
        for ll, part in self._riders(l):
            self.gat[(ll, part)] = {g: outs.pop(0) for g in _part_groups(part)}

    def mixer_weights(self, l):
        return _mixer_weights(self.gat[(l, "mix")])

    def ffn_weights(self, l):
        return _ffn_weights(self.gat[(l, "ffn")])

    def add_grads(self, l, part, gb):
        if part == "ffn":
            self.send[(l, "ffn")] = _ffn_grad_groups(gb)
        else:
            self.send[(l, "late")] = _mixer_grad_groups({g: gb[g] for g in MIX_LATE})

    def bwd_ride(self, l, gb_early):
        self.send[(l, "early")] = _mixer_grad_groups({g: gb_early[g] for g in MIX_EARLY})
        return _ReduceRide([self.send[(ll, part)][g] for ll, part in self._grad_riders(l) for g in _part_groups(part)])

    def bwd_done(self, l, outs):
        outs = list(outs)
        for ll, part in self._grad_riders(l):
            self.recv[(ll, part)] = {g: outs.pop(0) for g in _part_groups(part)}

    def finish(self, small_groups):
        send = [self.send[(0, "late")][g] for g in MIX_LATE]
        by_core = [a.reshape((4, 2) + a.shape[1:]).transpose((1, 0) + tuple(range(2, a.ndim + 1))) for a in send]
        core = lax.axis_index("c")
        own = [lax.dynamic_index_in_dim(a, core, axis=0, keepdims=False) for a in by_core]
        got = _swap_with_sibling(by_core, "reduce_d2d")
        pairs = [_add_pairs(a, b, f"reduce_pair_add_{g}") for g, a, b in zip(MIX_LATE, own, got)]
        outs = _run_ride(_Combo([_ChipExchangeRide(pairs), _GatherRide(small_groups)]), "reduce_ici_gather_small")
        self.recv[(0, "late")] = dict(zip(MIX_LATE, outs[:len(pairs)]))
        layers = []
        for l in range(DEPTH):
            tot = {g: _sum_blocks(a, f"reduce_sum_{g}_l{l}") for part in ("early", "late", "ffn")
                   for g, a in self.recv[(l, part)].items()}
            layers.append(_grads_from_groups(tot))
        return layers, outs[len(pairs):]


def _local_step(x, positions, target, smalls, plan):
    tables = _rope_tables(positions)
    saved = []
    h = x
    for l in range(DEPTH):
        wm = plan.mixer_weights(l)
        h, svm, rode = _mixer_fwd(h, tables, wm, smalls[l], f"l{l}", plan.fwd_ride(l))
        plan.fwd_done(l, rode)
        wf = plan.ffn_weights(l)
        h, svf = _ffn_fwd(h, wf, smalls[l], f"l{l}")
        saved.append((svm, svf, wm, wf))
    dy, sq = _loss_grad(h, target, "loss_grad")
    small = [None] * DEPTH
    for l in reversed(range(DEPTH)):
        svm, svf, wm, wf = saved[l]
        dmid, gbf, gsf = _ffn_bwd(dy, svf, wf, smalls[l], f"l{l}")
        plan.add_grads(l, "ffn", gbf)
        dy, gbm, gsm, rode = _mixer_bwd(dmid, svm, tables, wm, smalls[l], f"l{l}", lambda gb, l=l: plan.bwd_ride(l, gb))
        plan.bwd_done(l, rode)
        plan.add_grads(l, "mix", gbm)
        small[l] = {**gsf, **gsm}
    return sq, dy, small


def kernel(x, positions, mix_norm_pre, w_in, q_norm, w_uq, kv_norm, w_uk, w_uv, w_attn_o, conv_w, conv_b, conv_ln_g, conv_ln_b, w_conv_o, pool_w, pool_scale, w_pool_o, w_mix_o, mix_norm_post, ffn_norm_pre, w_gate, w_up, w_down, ffn_norm_post, loss_target, m_mix_norm_pre, m_w_in, m_q_norm, m_w_uq, m_kv_norm, m_w_uk, m_w_uv, m_w_attn_o, m_conv_w, m_conv_b, m_conv_ln_g, m_conv_ln_b, m_w_conv_o, m_pool_w, m_pool_scale, m_w_pool_o, m_w_mix_o, m_mix_norm_post, m_ffn_norm_pre, m_w_gate, m_w_up, m_w_down, m_ffn_norm_post, v_mix_norm_pre, v_w_in, v_q_norm, v_w_uq, v_kv_norm, v_w_uk, v_w_uv, v_w_attn_o, v_conv_w, v_conv_b, v_conv_ln_g, v_conv_ln_b, v_w_conv_o, v_pool_w, v_pool_scale, v_w_pool_o, v_w_mix_o, v_mix_norm_post, v_ffn_norm_pre, v_w_gate, v_w_up, v_w_down, v_ffn_norm_post):
    given = dict(locals())
    dev = 4 * lax.axis_index("x") + 2 * lax.axis_index("y") + lax.axis_index("c")

    plan = _Plan({n: given[n] for n in BIG}, conv_w)
    cw = CONV_C // N_DEV
    conv_w_full = plan.gather_first().transpose(1, 2, 0, 3).reshape(DEPTH, CONV_W, CONV_C)
    smalls = []
    for l in range(DEPTH):
        sm = {n: given[n][l] for n in SMALL if n != "conv_w"}
        sm["conv_w"] = _pad_axis(conv_w_full[l], 0, CONV_HALO)
        smalls.append(sm)

    sq, grad_x, small = _local_step(x[0], positions[0], loss_target[0], smalls, plan)
    loss = lax.psum(0.5 / D_MODEL * jnp.sum(sq), ("x", "y", "c"))
    per_layer, small_groups = plan.finish(_small_groups(small))
    views = {}
    for n in BIG:
        if n == "w_in":
            views[n] = jnp.stack([per_layer[l][n].T for l in range(DEPTH)], axis=1)
        elif n in LANE_MAJOR:
            views[n] = jnp.stack([per_layer[l][n].T for l in range(DEPTH)])
        else:
            views[n] = jnp.stack([per_layer[l][n] for l in range(DEPTH)])
    grads = {n: _from_lane_major(n, views[n]) for n in BIG}

    small_sum = _small_from_groups([_sum_blocks(g, f"sum_small_grads_{i}") for i, g in enumerate(small_groups)])
    for n in SMALL:
        grads[n] = small_sum[n]
    grads["conv_w"] = lax.dynamic_slice_in_dim(small_sum["conv_w"], dev * cw, cw, axis=2)

    delta, new_m, new_v = {}, {}, {}
    for n in WEIGHTS:
        g_view = views[n] if n in views else grads[n]
        w_view, m_view, v_view = [_lane_major(n, given[k]) for k in (n, "m_" + n, "v_" + n)]
        res = _adamw(w_view, g_view, m_view, v_view, f"adamw_{n}")
        delta[n], new_m[n], new_v[n] = [_from_lane_major(n, r) for r in res]
    return (loss, grad_x[None], *[grads[n] for n in WEIGHTS], *[delta[n] for n in WEIGHTS],
            *[new_m[n] for n in WEIGHTS], *[new_v[n] for n in WEIGHTS])
```

```python
import functools
import math

import jax
import jax.numpy as jnp
from jax import lax
from jax.experimental import pallas as pl
from jax.experimental.pallas import tpu as pltpu

F32, BF16 = jnp.float32, jnp.bfloat16
MESH = pl.DeviceIdType.MESH

LANES = 128
SUBLANES = 8
VMEM_LIMIT_BYTES = 56 * 1024 * 1024
MATMUL_VMEM_BYTES = 40 * 1024 * 1024

N_DEV = 8
D_MODEL = 1024
DEPTH = 2
N_HEADS = 8
QK_NOPE, QK_ROPE, V_HEAD = 64, 32, 64
HEAD_PAD = LANES
Q_RANK, KV_RANK = 384, 256
ROPE_THETA = 10000.0
CONV_C, CONV_W = 512, 31
CONV_HALO = 32
POOL_WINDOWS = (2, 4, 8, 16)
POOL_C, POOL_G = 512, 4
POOL_GD = POOL_C // POOL_G
D_FF = 2816
FF_SHARD = D_FF // N_DEV
FF_SHARD_PAD = 3 * LANES
D_FF_PAD = N_DEV * FF_SHARD_PAD
W_IN_SHARD = 660
EPS = 1e-6
ATTN_SCALE = 1.0 / math.sqrt(QK_NOPE + QK_ROPE)
LOG2E = 1.4426950408889634
LR, B1, B2, ADAM_EPS, WD, STEP = 0.001, 0.9, 0.999, 1e-08, 0.01, 10

Z_W = 5376
ZC_GATE = (1024, 0)
ZC_GATES = (3072, 0)
ZC_CONV_A = (512, 6)
ZC_CONV_G = (512, 7)
ZC_CONV = (1024, 3)
ZC_POOL = (512, 8)
ZC_Q = (384, 12)
ZC_KR = (128, 39)
ZC_KV = (256, 20)
W_IN_PIECES = ((0, 384, 4608), (384, 640, 5120), (640, 672, 5056), (672, 1696, 3072), (1696, 2208, 4096),
               (2208, 5280, 0))

BIG = ("w_in", "w_uq", "w_uk", "w_uv", "w_attn_o", "w_conv_o", "w_pool_o", "w_mix_o", "w_gate", "w_up", "w_down")
SMALL = ("mix_norm_pre", "q_norm", "kv_norm", "conv_w", "conv_b", "conv_ln_g", "conv_ln_b", "pool_w", "pool_scale",
         "mix_norm_post", "ffn_norm_pre", "ffn_norm_post")
WEIGHTS = ("mix_norm_pre", "w_in", "q_norm", "w_uq", "kv_norm", "w_uk", "w_uv", "w_attn_o", "conv_w", "conv_b",
           "conv_ln_g", "conv_ln_b", "w_conv_o", "pool_w", "pool_scale", "w_pool_o", "w_mix_o", "mix_norm_post",
           "ffn_norm_pre", "w_gate", "w_up", "w_down", "ffn_norm_post")


def _params(*semantics):
    return pltpu.CompilerParams(dimension_semantics=semantics, vmem_limit_bytes=VMEM_LIMIT_BYTES)


def _tile(dim, cap):
    if dim <= cap:
        return dim
    for t in range(cap - cap % LANES, 0, -LANES):
        if dim % t == 0:
            return t
    raise ValueError(f"no tile for {dim} under {cap}")


def _row_tile(rows, row_bytes, budget=1 << 20):
    if rows * row_bytes <= budget:
        return rows
    cap = max(16, budget // row_bytes)
    for t in range(cap - cap % 16, 0, -16):
        if rows % t == 0:
            return t
    return rows


def _rows(ts, width, cidx=0):
    return pl.BlockSpec((ts, width), lambda i: (i, cidx))


def _fixed(shape):
    return pl.BlockSpec(shape, lambda *_: (0,) * len(shape))


def _sigmoid(x):
    return 1.0 / (1.0 + jnp.exp(-x))


def _matmul(a, b, mode, out_dtype, name, add=None, blocked=False, ride=None):
    nb = n_blk = 0
    blocked = blocked or b.ndim == 3
    if mode == "nn":
        (m, k) = a.shape
        n = b.shape[0] * b.shape[2] if blocked else b.shape[1]
    elif mode == "nt":
        (m, k) = a.shape
        n = b.shape[1] if blocked else b.shape[0]
    else:
        (k, m), n = a.shape, b.shape[1]
    if blocked:
        nb = b.shape[2] if mode != "tn" else n // N_DEV
    unit = nb if blocked and mode != "nt" else LANES
    out_bytes = jnp.dtype(out_dtype).itemsize + (4 if add is not None else 0)
    best = None
    for tn_c in range(unit, min(n, 1536) + 1, unit):
        for tm_c in sorted({256, 512, 1024, 2048, min(m, 2048)}):
            if n % tn_c or m % tm_c or (blocked and mode != "nt" and N_DEV % (tn_c // nb)):
                continue
            vmem = 2 * (tm_c * k * 2 + tn_c * k * 2 + tm_c * tn_c * out_bytes) + tm_c * tn_c * 4 + tn_c * k * 2
            if vmem <= MATMUL_VMEM_BYTES and (best is None or tm_c * tn_c / (tm_c + tn_c) > best[0]):
                best = (tm_c * tn_c / (tm_c + tn_c), tm_c, tn_c)
    if best is None:
        raise ValueError(f"{name}: no tiles for {m}x{n}x{k}")
    _, tm, tn = best
    if blocked:
        n_blk = N_DEV if mode == "nt" else tn // nb
    dims = {"nn": ((1,), (0,)), "nt": ((1,), (1,)), "tn": ((0,), (0,))}[mode]
    a_spec = pl.BlockSpec((k, tm), lambda i, j: (0, i)) if mode == "tn" else pl.BlockSpec((tm, k), lambda i, j: (i, 0))
    b_spec = pl.BlockSpec((tn, k), lambda i, j: (j, 0)) if mode == "nt" else pl.BlockSpec((k, tn), lambda i, j: (0, j))
    o_spec = pl.BlockSpec((tm, tn), lambda i, j: (i, j))
    out_shape = jax.ShapeDtypeStruct((m, n), out_dtype)
    if blocked and mode == "nn":
        b_spec = pl.BlockSpec((n_blk, k, nb), lambda i, j: (j, 0, 0))
    elif blocked and mode == "nt":
        b_spec = pl.BlockSpec((n_blk, tn, nb), lambda i, j: (0, j, 0))
    elif blocked:
        o_spec = pl.BlockSpec((n_blk, tm, nb), lambda i, j: (j, i, 0))
        out_shape = jax.ShapeDtypeStruct((N_DEV, m, nb), out_dtype)
    has_add = add is not None
    grid = (m // tm, n // tn)

    def body(*refs):
        (a_ref, b_ref, *rest), start, finish = _ride_hooks(ride, refs, 3 if has_add else 2, 1, grid)
        start()
        o_ref = rest[-1]
        if blocked and mode != "tn":
            bv = jnp.concatenate([b_ref[c] for c in range(n_blk)], axis=1) if n_blk > 1 else b_ref[0]
        else:
            bv = b_ref[...]
        total = lax.dot_general(a_ref[...], bv, (dims, ((), ())), preferred_element_type=F32)
        if has_add:
            total = total + rest[0][...]
        if blocked and mode == "tn":
            for c in range(n_blk):
                o_ref[c] = total[:, c * nb:(c + 1) * nb].astype(o_ref.dtype)
        else:
            o_ref[...] = total.astype(o_ref.dtype)
        finish()

    operands = (a, b, add) if has_add else (a, b)
    (out,), rode = _ride_call(ride, body, name, (out_shape,), grid, [a_spec, b_spec] + ([o_spec] if has_add else []),
                              (o_spec,), ("parallel", "parallel"), operands)
    return out if ride is None else (out, rode)


def _rms_fwd(x, win, gain, out_dtype, name, res=None):
    width, cidx = win
    s = x.shape[0]
    ts = min(s, 512)
    has_res = res is not None

    def body(x_ref, g_ref, *rest):
        o_ref = rest[-1]
        xv = x_ref[...].astype(F32)
        r = lax.rsqrt(jnp.mean(xv * xv, axis=-1, keepdims=True) + EPS)
        y = (xv * r) * g_ref[...]
        if has_res:
            y = rest[0][...] + y
        o_ref[...] = y.astype(o_ref.dtype)

    ops = (x, gain.reshape(1, width)) + ((res,) if has_res else ())
    return pl.pallas_call(
        body, name=name, out_shape=jax.ShapeDtypeStruct((s, width), out_dtype), grid=(s // ts,),
        in_specs=[_rows(ts, width, cidx), _fixed((1, width))] + ([_rows(ts, width)] if has_res else []),
        out_specs=_rows(ts, width), compiler_params=_params("parallel"))(*ops)


def _into(dz, n_inputs, out_index):
    return dict(in_specs=[ANY], operands=(dz,), input_output_aliases={n_inputs: out_index},
                out_shape=jax.ShapeDtypeStruct(dz.shape, dz.dtype))


def _rms_bwd(x, win, gain, dy, out_dtype, name, add=None, dz=None):
    width, cidx = win
    s = x.shape[0]
    ts = min(s, 512)
    has_add = add is not None

    def body(x_ref, g_ref, dy_ref, *rest):
        dx_ref, dg_ref = rest[-2], rest[-1]
        xv = x_ref[...].astype(F32)
        r = lax.rsqrt(jnp.mean(xv * xv, axis=-1, keepdims=True) + EPS)
        xh = xv * r
        dyv = dy_ref[...].astype(F32)
        dyg = dyv * g_ref[...]
        dx = r * (dyg - xh * jnp.mean(dyg * xh, axis=-1, keepdims=True))
        if has_add:
            dx = dx + rest[0][...]
        dx_ref[...] = dx.astype(dx_ref.dtype)

        @pl.when(pl.program_id(0) == 0)
        def _():
            dg_ref[...] = jnp.zeros_like(dg_ref)

        dg_ref[...] += jnp.sum(dyv * xh, axis=0, keepdims=True)

    ops = (x, gain.reshape(1, width), dy) + ((add,) if has_add else ())
    in_specs = [_rows(ts, width, cidx), _fixed((1, width)), _rows(ts, width)] + ([_rows(ts, width)] if has_add else [])
    dx_shape, dx_spec, alias = jax.ShapeDtypeStruct((s, width), out_dtype), _rows(ts, width), {}
    if dz is not None:
        into = _into(dz, len(ops), 0)
        ops, in_specs, alias = ops + into["operands"], in_specs + into["in_specs"], into["input_output_aliases"]
        dx_shape, dx_spec = into["out_shape"], _rows(ts, width, cidx)
    dx, dg = pl.pallas_call(
        body, name=name, out_shape=(dx_shape, jax.ShapeDtypeStruct((1, width), F32)), grid=(s // ts,),
        in_specs=in_specs, out_specs=(dx_spec, _fixed((1, width))), input_output_aliases=alias,
        compiler_params=_params("arbitrary"))(*ops)
    return dx, dg.reshape(width)


def _rope(x, c, s1, s2):
    return x * c + pltpu.roll(x, 16, 1) * s1 + pltpu.roll(x, LANES - 16, 1) * s2


def _rope_t(g, c, s1, s2):
    return g * c + pltpu.roll(g * s1, LANES - 16, 1) + pltpu.roll(g * s2, 16, 1)


def _rope_tables(positions):
    inv_freq = ROPE_THETA ** (-jnp.arange(0, QK_ROPE, 2, dtype=F32) / QK_ROPE)
    ang = positions.astype(F32)[:, None] * inv_freq
    cos, sin = jnp.cos(ang), jnp.sin(ang)
    n = positions.shape[0]
    one, zero = jnp.ones((n, 1), F32), jnp.zeros((n, 1), F32)
    c = jnp.concatenate([jnp.tile(one, (1, QK_NOPE)), cos, cos, jnp.tile(one, (1, 32))], axis=1)
    s1 = jnp.concatenate([jnp.tile(zero, (1, QK_NOPE + 16)), sin, jnp.tile(zero, (1, 32))], axis=1)
    s2 = jnp.concatenate([jnp.tile(zero, (1, QK_NOPE)), -sin, jnp.tile(zero, (1, 48))], axis=1)
    return c, s1, s2


def _rope_qk_fwd(qf, kf, z, tables, name):
    s = qf.shape[0]
    ts = min(s, 256)
    hw = N_HEADS * HEAD_PAD

    def body(qf_ref, kf_ref, kr_ref, c_ref, s1_ref, s2_ref, q_ref, k_ref):
        c, s1, s2 = c_ref[...], s1_ref[...], s2_ref[...]
        kr = _rope(kr_ref[...].astype(F32), c, s1, s2)
        for h in range(N_HEADS):
            sl = slice(h * HEAD_PAD, (h + 1) * HEAD_PAD)
            q_ref[:, sl] = _rope(qf_ref[:, sl], c, s1, s2).astype(BF16)
            k_ref[:, sl] = (kf_ref[:, sl] + kr).astype(BF16)

    tab = _rows(ts, LANES)
    return pl.pallas_call(
        body, name=name, out_shape=(jax.ShapeDtypeStruct((s, hw), BF16),) * 2, grid=(s // ts,),
        in_specs=[_rows(ts, hw), _rows(ts, hw), _rows(ts, *ZC_KR), tab, tab, tab],
        out_specs=(_rows(ts, hw), _rows(ts, hw)), compiler_params=_params("parallel"))(qf, kf, z, *tables)


def _rope_qk_bwd(dq, dk, tables, dz, name):
    s = dq.shape[0]
    ts = min(s, 256)
    hw = N_HEADS * HEAD_PAD

    def body(dq_ref, dk_ref, c_ref, s1_ref, s2_ref, _, dqf_ref, dkf_ref, dkr_ref):
        c, s1, s2 = c_ref[...], s1_ref[...], s2_ref[...]
        ksum = jnp.zeros((ts, HEAD_PAD), F32)
        for h in range(N_HEADS):
            sl = slice(h * HEAD_PAD, (h + 1) * HEAD_PAD)
            dqf_ref[:, sl] = _rope_t(dq_ref[:, sl], c, s1, s2).astype(BF16)
            dkh = dk_ref[:, sl]
            dkf_ref[:, sl] = dkh.astype(BF16)
            ksum = ksum + dkh
        lane = lax.broadcasted_iota(jnp.int32, (ts, HEAD_PAD), 1)
        in_rope = (lane >= QK_NOPE) & (lane < QK_NOPE + QK_ROPE)
        dkr_ref[...] = jnp.where(in_rope, _rope_t(ksum, c, s1, s2), 0.0).astype(BF16)

    tab = _rows(ts, LANES)
    into = _into(dz, 5, 2)
    return pl.pallas_call(
        body, name=name,
        out_shape=(jax.ShapeDtypeStruct((s, hw), BF16), jax.ShapeDtypeStruct((s, hw), BF16), into["out_shape"]),
        grid=(s // ts,), in_specs=[_rows(ts, hw), _rows(ts, hw), tab, tab, tab] + into["in_specs"],
        out_specs=(_rows(ts, hw), _rows(ts, hw), _rows(ts, *ZC_KR)), input_output_aliases=into["input_output_aliases"],
        compiler_params=_params("parallel"))(dq, dk, *tables, dz)


def _attn_tile(s):
    return min(s, 512)


def _raw_scores(q, k, masked, row0=0):
    sc = lax.dot_general(q, k, (((1,), (1,)), ((), ())), preferred_element_type=F32)
    if masked:
        rows = row0 + lax.broadcasted_iota(jnp.int32, sc.shape, 0)
        cols = lax.broadcasted_iota(jnp.int32, sc.shape, 1)
        sc = jnp.where(cols <= rows, sc, -jnp.inf)
    return sc


def _ride_hooks(ride, refs, n_in, n_out, grid):
    if ride is None:
        return refs, lambda: None, lambda: None
    n = len(ride.arrays)
    own = refs[:n_in] + refs[n_in + n:n_in + n + n_out]
    ins, outs, sems = refs[n_in:n_in + n], refs[n_in + n + n_out:n_in + 2 * n + n_out], refs[n_in + 2 * n + n_out:]
    at_first = functools.reduce(lambda a, b: a & b, [pl.program_id(ax) == 0 for ax in range(len(grid))])
    at_last = functools.reduce(lambda a, b: a & b, [pl.program_id(ax) == g - 1 for ax, g in enumerate(grid)])
    return own, lambda: pl.when(at_first)(lambda: ride.start(ins, outs, sems)), \
        lambda: pl.when(at_last)(lambda: ride.finish(ins, outs, sems))


def _ride_call(ride, body, name, out_shape, grid, in_specs, out_specs, semantics, operands):
    n = 0 if ride is None else len(ride.arrays)
    res = pl.pallas_call(
        body, name=name, out_shape=tuple(out_shape) + (tuple(ride.out_shape) if n else ()), grid=grid,
        in_specs=list(in_specs) + [ANY] * n, out_specs=tuple(out_specs) + (ANY,) * n,
        scratch_shapes=list(ride.scratch) if n else [],
        compiler_params=_params(*(("arbitrary",) * len(grid) if n else semantics)))(*operands, *(ride.arrays if n else ()))
    return res[:len(out_shape)], list(res[len(out_shape):])


def _flash_fwd(q, k, v, name, ride=None):
    s = q.shape[0]
    t = _attn_tile(s)
    c2 = ATTN_SCALE * LOG2E
    grid = (N_HEADS, s // t)

    def body(*refs):
        (q_ref, k_ref, v_ref, o_ref, lse_ref), start, finish = _ride_hooks(ride, refs, 3, 2, grid)
        start()
        i = pl.program_id(1)
        qv = q_ref[...]

        def chunk(j, carry, masked):
            m_old, l_old, acc = carry
            at = pl.ds(pl.multiple_of(j * t, t), t)
            sc = _raw_scores(qv, k_ref[at, :], masked)
            m_new = jnp.maximum(m_old, jnp.max(sc, axis=-1, keepdims=True))
            p = jnp.exp2((sc - m_new) * c2)
            alpha = jnp.exp2((m_old - m_new) * c2)
            l_new = alpha * l_old + jnp.sum(p, axis=-1, keepdims=True)
            acc = alpha * acc + jnp.dot(p.astype(BF16), v_ref[at, :], preferred_element_type=F32)
            return m_new, l_new, acc

        init = (jnp.full((t, 1), -jnp.inf, F32), jnp.zeros((t, 1), F32), jnp.zeros((t, HEAD_PAD), F32))
        carry = lax.fori_loop(0, i, lambda j, cr: chunk(j, cr, False), init)
        m_fin, l_fin, acc = chunk(i, carry, True)
        o_ref[...] = (acc / l_fin).astype(o_ref.dtype)
        lse_ref[...] = jnp.broadcast_to(m_fin * ATTN_SCALE + jnp.log(l_fin), (t, HEAD_PAD))
        finish()

    qo = pl.BlockSpec((t, HEAD_PAD), lambda h, i: (i, h))
    whole = pl.BlockSpec((s, HEAD_PAD), lambda h, i: (0, h))
    return _ride_call(
        ride, body, name, (jax.ShapeDtypeStruct(q.shape, BF16), jax.ShapeDtypeStruct(q.shape, F32)), grid,
        [qo, whole, whole], (qo, qo), ("parallel", "parallel"), (q, k, v))


def _attn_delta(do, o, name):
    s = o.shape[0]
    t = _attn_tile(s)

    def body(do_ref, o_ref, delta_ref, dob_ref):
        for h in range(N_HEADS):
            sl = slice(h * HEAD_PAD, (h + 1) * HEAD_PAD)
            dov = do_ref[:, sl]
            delta_ref[:, sl] = jnp.broadcast_to(jnp.sum(dov * o_ref[:, sl].astype(F32), axis=-1, keepdims=True),
                                                (t, HEAD_PAD))
            dob_ref[:, sl] = dov.astype(BF16)

    blk = _rows(t, N_HEADS * HEAD_PAD)
    return pl.pallas_call(
        body, name=name, out_shape=(jax.ShapeDtypeStruct(o.shape, F32), jax.ShapeDtypeStruct(o.shape, BF16)),
        grid=(s // t,), in_specs=[blk, blk], out_specs=(blk, blk), compiler_params=_params("parallel"))(do, o)


def _flash_bwd(q, k, v, do, lse, delta, name, ride=None):
    s = q.shape[0]
    t = _attn_tile(s)
    nt = s // t
    c2 = ATTN_SCALE * LOG2E
    grid = (N_HEADS, nt)

    def body(*refs):
        (q_ref, k_ref, v_ref, do_ref, lse_ref, delta_ref, dq_ref, dk_ref, dv_ref), start, finish = _ride_hooks(
            ride, refs, 6, 3, grid)
        start()
        j = pl.program_id(1)
        kv, vv = k_ref[...], v_ref[...]

        @pl.when(j == 0)
        def _():
            dq_ref[...] = jnp.zeros_like(dq_ref)

        def chunk(i, carry, masked):
            dk_acc, dv_acc = carry
            at = pl.ds(pl.multiple_of(i * t, t), t)
            qi, doi = q_ref[at, :], do_ref[at, :]
            sc = _raw_scores(qi, kv, masked)
            p = jnp.exp2(sc * c2 - lse_ref[at, pl.ds(0, 1)] * LOG2E)
            dp = lax.dot_general(doi, vv, (((1,), (1,)), ((), ())), preferred_element_type=F32)
            ds = (p * (dp - delta_ref[at, pl.ds(0, 1)])).astype(BF16)
            dv_acc = dv_acc + lax.dot_general(p.astype(BF16), doi, (((0,), (0,)), ((), ())), preferred_element_type=F32)
            dk_acc = dk_acc + lax.dot_general(ds, qi, (((0,), (0,)), ((), ())), preferred_element_type=F32)
            dq_ref[at, :] += jnp.dot(ds, kv, preferred_element_type=F32) * ATTN_SCALE
            return dk_acc, dv_acc

        zero = jnp.zeros((t, HEAD_PAD), F32)
        carry = chunk(j, (zero, zero), True)
        dk_acc, dv_acc = lax.fori_loop(j + 1, nt, lambda i, cr: chunk(i, cr, False), carry)
        dk_ref[...] = dk_acc * ATTN_SCALE
        dv_ref[...] = dv_acc.astype(BF16)
        finish()

    blk = pl.BlockSpec((t, HEAD_PAD), lambda h, j: (j, h))
    whole = pl.BlockSpec((s, HEAD_PAD), lambda h, j: (0, h))
    return _ride_call(
        ride, body, name, (jax.ShapeDtypeStruct(q.shape, F32), jax.ShapeDtypeStruct(q.shape, F32),
                           jax.ShapeDtypeStruct(q.shape, BF16)), grid,
        [whole, blk, blk, whole, whole, whole], (whole, blk, blk), ("parallel", "arbitrary"), (q, k, v, do, lse, delta))


def _conv_tile(s):
    return min(s, 256)


def _halo_before(t, width, cidx):
    per = t // CONV_HALO
    return pl.BlockSpec((CONV_HALO, width), lambda i: (jnp.maximum(i * per - 1, 0), cidx))


def _halo_after(t, width, cidx, n_tiles):
    per = t // CONV_HALO
    last = n_tiles * per - 1
    return pl.BlockSpec((CONV_HALO, width), lambda i: (jnp.minimum((i + 1) * per, last), cidx))


def _fill_glu(hbuf, ap_ref, gp_ref, a_ref, g_ref, t):
    first = pl.program_id(0) == 0
    hbuf[pl.ds(0, CONV_HALO), :] = jnp.where(first, 0.0, ap_ref[...].astype(F32) * _sigmoid(gp_ref[...].astype(F32)))
    hbuf[pl.ds(CONV_HALO, t), :] = a_ref[...].astype(F32) * _sigmoid(g_ref[...].astype(F32))


def _phase_copies(dst, src, t):
    n = t + CONV_HALO - SUBLANES
    for s in range(1, SUBLANES):
        dst[s, pl.ds(0, n), :] = src[pl.ds(s, n), :]


def _window(phases, src, k, t):
    if k % SUBLANES == 0:
        return src[pl.ds(k, t), :]
    return phases[k % SUBLANES, pl.ds(k - k % SUBLANES, t), :]


def _layer_norm_parts(co):
    mu = jnp.mean(co, axis=-1, keepdims=True)
    xc = co - mu
    rstd = lax.rsqrt(jnp.mean(xc * xc, axis=-1, keepdims=True) + EPS)
    return xc * rstd, rstd


def _conv_fwd(z, conv_w, conv_b, ln_g, ln_b, name):
    s = z.shape[0]
    t = _conv_tile(s)
    off = CONV_HALO - (CONV_W - 1)

    def body(ap_ref, gp_ref, a_ref, g_ref, w_ref, b_ref, lg_ref, lb_ref, hc_ref, co_ref, hbuf, hph):
        _fill_glu(hbuf, ap_ref, gp_ref, a_ref, g_ref, t)
        _phase_copies(hph, hbuf, t)
        acc = jnp.zeros((t, CONV_C), F32) + b_ref[...]
        for j in range(CONV_W):
            acc = acc + _window(hph, hbuf, off + j, t) * w_ref[pl.ds(j, 1), :]
        co_ref[...] = acc
        xh, _ = _layer_norm_parts(acc)
        y = xh * lg_ref[...] + lb_ref[...]
        hc_ref[...] = (y * _sigmoid(y)).astype(BF16)

    vec = _fixed((1, CONV_C))
    return pl.pallas_call(
        body, name=name, out_shape=(jax.ShapeDtypeStruct((s, CONV_C), BF16), jax.ShapeDtypeStruct((s, CONV_C), F32)),
        grid=(s // t,),
        in_specs=[_halo_before(t, *ZC_CONV_A), _halo_before(t, *ZC_CONV_G), _rows(t, *ZC_CONV_A), _rows(t, *ZC_CONV_G),
                  _fixed((CONV_HALO, CONV_C)), vec, vec, vec],
        out_specs=(_rows(t, CONV_C), _rows(t, CONV_C)),
        scratch_shapes=[pltpu.VMEM((t + CONV_HALO, CONV_C), F32), pltpu.VMEM((SUBLANES, t + CONV_HALO, CONV_C), F32)],
        compiler_params=_params("parallel"))(z, z, z, z, conv_w, conv_b.reshape(1, -1), ln_g.reshape(1, -1),
                                             ln_b.reshape(1, -1))


def _conv_bwd_norm(dhc, co, ln_g, ln_b, name):
    s = co.shape[0]
    t = min(s, 512)

    def body(dhc_ref, co_ref, lg_ref, lb_ref, dco_ref, dg_ref, db_ref, dcb_ref):
        xh, rstd = _layer_norm_parts(co_ref[...])
        y = xh * lg_ref[...] + lb_ref[...]
        sg = _sigmoid(y)
        dy = dhc_ref[...] * (sg * (1.0 + y * (1.0 - sg)))
        dxh = dy * lg_ref[...]
        dco = rstd * (dxh - jnp.mean(dxh, axis=-1, keepdims=True) - xh * jnp.mean(dxh * xh, axis=-1, keepdims=True))
        dco_ref[...] = dco

        @pl.when(pl.program_id(0) == 0)
        def _():
            dg_ref[...] = jnp.zeros_like(dg_ref)
            db_ref[...] = jnp.zeros_like(db_ref)
            dcb_ref[...] = jnp.zeros_like(dcb_ref)

        dg_ref[...] += jnp.sum(dy * xh, axis=0, keepdims=True)
        db_ref[...] += jnp.sum(dy, axis=0, keepdims=True)
        dcb_ref[...] += jnp.sum(dco, axis=0, keepdims=True)

    vec = _fixed((1, CONV_C))
    one = jax.ShapeDtypeStruct((1, CONV_C), F32)
    dco, dg, db, dcb = pl.pallas_call(
        body, name=name, out_shape=(jax.ShapeDtypeStruct((s, CONV_C), F32), one, one, one), grid=(s // t,),
        in_specs=[_rows(t, CONV_C), _rows(t, CONV_C), vec, vec], out_specs=(_rows(t, CONV_C), vec, vec, vec),
        compiler_params=_params("arbitrary"))(dhc, co, ln_g.reshape(1, -1), ln_b.reshape(1, -1))
    return dco, dg.reshape(-1), db.reshape(-1), dcb.reshape(-1)


def _conv_bwd_taps(dco, z, conv_w, dz, name):
    s = z.shape[0]
    t = _conv_tile(s)
    nt = s // t
    off = CONV_HALO - (CONV_W - 1)

    def body(ap_ref, gp_ref, a_ref, g_ref, d_ref, dn_ref, w_ref, _, du_ref, dw_ref, hbuf, dbuf, hph, dph):
        i = pl.program_id(0)
        _fill_glu(hbuf, ap_ref, gp_ref, a_ref, g_ref, t)
        dbuf[pl.ds(0, t), :] = d_ref[...]
        dbuf[pl.ds(t, CONV_HALO), :] = jnp.where(i == nt - 1, 0.0, dn_ref[...])
        _phase_copies(hph, hbuf, t)
        _phase_copies(dph, dbuf, t)

        @pl.when(i == 0)
        def _():
            dw_ref[...] = jnp.zeros_like(dw_ref)

        dcur = d_ref[...]
        dh = jnp.zeros((t, CONV_C), F32)
        for j in range(CONV_W):
            dh = dh + _window(dph, dbuf, CONV_W - 1 - j, t) * w_ref[pl.ds(j, 1), :]
            dw_ref[pl.ds(j, 1), :] += jnp.sum(dcur * _window(hph, hbuf, off + j, t), axis=0, keepdims=True)
        a, sg = a_ref[...].astype(F32), _sigmoid(g_ref[...].astype(F32))
        du_ref[:, pl.ds(0, CONV_C)] = (dh * sg).astype(BF16)
        du_ref[:, pl.ds(CONV_C, CONV_C)] = (dh * a * sg * (1.0 - sg)).astype(BF16)

    into = _into(dz, 7, 0)
    return pl.pallas_call(
        body, name=name, out_shape=(into["out_shape"], jax.ShapeDtypeStruct((CONV_HALO, CONV_C), F32)), grid=(nt,),
        in_specs=[_halo_before(t, *ZC_CONV_A), _halo_before(t, *ZC_CONV_G), _rows(t, *ZC_CONV_A), _rows(t, *ZC_CONV_G),
                  _rows(t, CONV_C), _halo_after(t, CONV_C, 0, nt), _fixed((CONV_HALO, CONV_C))] + into["in_specs"],
        out_specs=(_rows(t, *ZC_CONV), _fixed((CONV_HALO, CONV_C))), input_output_aliases=into["input_output_aliases"],
        scratch_shapes=[pltpu.VMEM((t + CONV_HALO, CONV_C), F32), pltpu.VMEM((t + CONV_HALO, CONV_C), F32),
                        pltpu.VMEM((SUBLANES, t + CONV_HALO, CONV_C), F32),
                        pltpu.VMEM((SUBLANES, t + CONV_HALO, CONV_C), F32)],
        compiler_params=_params("arbitrary"))(z, z, z, z, dco, dco, conv_w, dz)


def _pool_tile(s):
    return min(s, 512)


def _pool_counts(row0, n, window):
    rows = row0 + lax.broadcasted_iota(jnp.int32, (n, POOL_GD), 0)
    return jnp.minimum(rows + 1, window).astype(F32)


def _pool_diff(ubuf, gi, window, row0, t):
    lanes = pl.ds(gi * POOL_GD, POOL_GD)
    tot = ubuf[pl.ds(CONV_HALO, t), lanes]
    cur = tot
    for back in range(1, window):
        tot = tot + ubuf[pl.ds(CONV_HALO - back, t), lanes]
    return tot / _pool_counts(row0, t, window) - cur


def _pool_fwd(z, pool_w, pool_scale, name):
    s = z.shape[0]
    t = _pool_tile(s)

    def body(up_ref, u_ref, w_ref, sc_ref, m_ref, ubuf):
        i = pl.program_id(0)
        ubuf[pl.ds(0, CONV_HALO), :] = jnp.where(i == 0, 0.0, up_ref[...].astype(F32))
        ubuf[pl.ds(CONV_HALO, t), :] = u_ref[...].astype(F32)
        for gi, window in enumerate(POOL_WINDOWS):
            d = _pool_diff(ubuf, gi, window, i * t, t)
            mm = jnp.dot(d.astype(BF16), w_ref[gi].astype(BF16), preferred_element_type=F32)
            lanes = pl.ds(gi * POOL_GD, POOL_GD)
            m_ref[:, lanes] = (mm * sc_ref[:, lanes]).astype(BF16)

    return pl.pallas_call(
        body, name=name, out_shape=jax.ShapeDtypeStruct((s, POOL_C), BF16), grid=(s // t,),
        in_specs=[_halo_before(t, *ZC_POOL), _rows(t, *ZC_POOL), _fixed((POOL_G, POOL_GD, POOL_GD)), _fixed((1, POOL_C))],
        out_specs=_rows(t, POOL_C), scratch_shapes=[pltpu.VMEM((t + CONV_HALO, POOL_C), F32)],
        compiler_params=_params("parallel"))(z, z, pool_w, pool_scale.reshape(1, -1))


def _pool_bwd(dm, z, pool_w, pool_scale, dz, name):
    s = z.shape[0]
    t = _pool_tile(s)
    nt = s // t

    def body(up_ref, u_ref, dm_ref, dmn_ref, w_ref, sc_ref, _, du_ref, dw_ref, dsc_ref, ubuf, ebuf):
        i = pl.program_id(0)
        ubuf[pl.ds(0, CONV_HALO), :] = jnp.where(i == 0, 0.0, up_ref[...].astype(F32))
        ubuf[pl.ds(CONV_HALO, t), :] = u_ref[...].astype(F32)

        @pl.when(i == 0)
        def _():
            dw_ref[...] = jnp.zeros_like(dw_ref)
            dsc_ref[...] = jnp.zeros_like(dsc_ref)

        dm_next = jnp.where(i == nt - 1, 0.0, dmn_ref[...])
        for gi, window in enumerate(POOL_WINDOWS):
            lanes = pl.ds(gi * POOL_GD, POOL_GD)
            wb = w_ref[gi].astype(BF16)
            scale = sc_ref[:, lanes]
            d = _pool_diff(ubuf, gi, window, i * t, t).astype(BF16)
            mm = jnp.dot(d, wb, preferred_element_type=F32)
            dmv = dm_ref[:, lanes]
            dsc_ref[:, lanes] += jnp.sum(dmv * mm, axis=0, keepdims=True)
            dmm = (dmv * scale).astype(BF16)
            dw_ref[gi] += lax.dot_general(d, dmm, (((0,), (0,)), ((), ())), preferred_element_type=F32)
            dd = lax.dot_general(dmm, wb, (((1,), (1,)), ((), ())), preferred_element_type=F32)
            dd_next = lax.dot_general((dm_next[:, gi * POOL_GD:(gi + 1) * POOL_GD] * scale).astype(BF16), wb,
                                      (((1,), (1,)), ((), ())), preferred_element_type=F32)
            ebuf[pl.ds(0, t), lanes] = dd / _pool_counts(i * t, t, window)
            ebuf[pl.ds(t, CONV_HALO), lanes] = dd_next / _pool_counts((i + 1) * t, CONV_HALO, window)
            du = -dd
            for ahead in range(window):
                du = du + ebuf[pl.ds(ahead, t), lanes]
            du_ref[:, lanes] = du.astype(BF16)

    into = _into(dz, 6, 0)
    du, dw, dsc = pl.pallas_call(
        body, name=name,
        out_shape=(into["out_shape"], jax.ShapeDtypeStruct((POOL_G, POOL_GD, POOL_GD), F32),
                   jax.ShapeDtypeStruct((1, POOL_C), F32)), grid=(nt,),
        in_specs=[_halo_before(t, *ZC_POOL), _rows(t, *ZC_POOL), _rows(t, POOL_C), _halo_after(t, POOL_C, 0, nt),
                  _fixed((POOL_G, POOL_GD, POOL_GD)), _fixed((1, POOL_C))] + into["in_specs"],
        out_specs=(_rows(t, *ZC_POOL), _fixed((POOL_G, POOL_GD, POOL_GD)), _fixed((1, POOL_C))),
        input_output_aliases=into["input_output_aliases"],
        scratch_shapes=[pltpu.VMEM((t + CONV_HALO, POOL_C), F32), pltpu.VMEM((t + CONV_HALO, POOL_C), F32)],
        compiler_params=_params("arbitrary"))(z, z, dm, dm, pool_w, pool_scale.reshape(1, -1), dz)
    return du, dw, dsc.reshape(-1)


def _gate_specs(ts):
    width, first = ZC_GATE
    return [_rows(ts, width, first + b) for b in range(3)]


def _merge_fwd(z, ys, name):
    s = z.shape[0]
    ts = min(s, 256)

    def body(g0, g1, g2, y0, y1, y2, o_ref):
        o_ref[...] = sum(_sigmoid(g[...].astype(F32)) * y[...].astype(F32)
                         for g, y in ((g0, y0), (g1, y1), (g2, y2))).astype(BF16)

    return pl.pallas_call(
        body, name=name, out_shape=jax.ShapeDtypeStruct((s, D_MODEL), BF16), grid=(s // ts,),
        in_specs=_gate_specs(ts) + [_rows(ts, D_MODEL)] * 3, out_specs=_rows(ts, D_MODEL),
        compiler_params=_params("parallel"))(z, z, z, *ys)


def _merge_bwd(z, ys, dmerged, name):
    s = z.shape[0]
    ts = min(s, 256)

    def body(g0, g1, g2, y0, y1, y2, dm_ref, dy0, dy1, dy2, dz_ref):
        dmv = dm_ref[...]
        for b, (g_ref, y_ref, dy_ref) in enumerate(((g0, y0, dy0), (g1, y1, dy1), (g2, y2, dy2))):
            sg = _sigmoid(g_ref[...].astype(F32))
            dy_ref[...] = (dmv * sg).astype(BF16)
            dz_ref[:, pl.ds(b * D_MODEL, D_MODEL)] = (dmv * y_ref[...].astype(F32) * sg * (1.0 - sg)).astype(BF16)

    out = jax.ShapeDtypeStruct((s, D_MODEL), BF16)
    return pl.pallas_call(
        body, name=name, out_shape=(out,) * 3 + (jax.ShapeDtypeStruct((s, Z_W), BF16),), grid=(s // ts,),
        in_specs=_gate_specs(ts) + [_rows(ts, D_MODEL)] * 4,
        out_specs=(_rows(ts, D_MODEL),) * 3 + (_rows(ts, *ZC_GATES),),
        compiler_params=_params("parallel"))(z, z, z, *ys, dmerged)


def _ffn_up_fwd(h, w_gate, w_up, name):
    s, d = h.shape
    nb = w_gate.shape[2]
    f = N_DEV * nb
    tm, n_blk = min(s, 1024), 2
    tn = n_blk * nb
    blk = pl.BlockSpec((tm, tn), lambda i, j: (i, j))
    wspec = pl.BlockSpec((n_blk, d, nb), lambda i, j: (j, 0, 0))

    def body(h_ref, wg_ref, wu_ref, hg_ref, hu_ref, act_ref):
        hv = h_ref[...]
        g = jnp.dot(hv, jnp.concatenate([wg_ref[c] for c in range(n_blk)], axis=1), preferred_element_type=F32)
        u = jnp.dot(hv, jnp.concatenate([wu_ref[c] for c in range(n_blk)], axis=1), preferred_element_type=F32)
        hg_ref[...] = g.astype(hg_ref.dtype)
        hu_ref[...] = u.astype(hu_ref.dtype)
        act_ref[...] = (g * _sigmoid(g) * u).astype(BF16)

    return pl.pallas_call(
        body, name=name,
        out_shape=(jax.ShapeDtypeStruct((s, f), BF16),) * 3,
        grid=(s // tm, f // tn), in_specs=[pl.BlockSpec((tm, d), lambda i, j: (i, 0)), wspec, wspec],
        out_specs=(blk, blk, blk), compiler_params=_params("parallel", "parallel"))(h, w_gate, w_up)


def _ffn_down_bwd(dfo, w_down, hg, hu, name):
    s, d = dfo.shape
    f = w_down.shape[0]
    tm, tn = min(s, 1024), _tile(f, 1024)
    blk = pl.BlockSpec((tm, tn), lambda i, j: (i, j))

    def body(d_ref, w_ref, g_ref, u_ref, dg_ref, du_ref):
        dact = lax.dot_general(d_ref[...], w_ref[...], (((1,), (1,)), ((), ())), preferred_element_type=F32)
        g = g_ref[...].astype(F32)
        sg = _sigmoid(g)
        dg_ref[...] = (dact * u_ref[...].astype(F32) * (sg * (1.0 + g * (1.0 - sg)))).astype(BF16)
        du_ref[...] = (dact * g * sg).astype(BF16)

    out = jax.ShapeDtypeStruct((s, f), BF16)
    return pl.pallas_call(
        body, name=name, out_shape=(out, out), grid=(s // tm, f // tn),
        in_specs=[pl.BlockSpec((tm, d), lambda i, j: (i, 0)), pl.BlockSpec((tn, d), lambda i, j: (j, 0)), blk, blk],
        out_specs=(blk, blk), compiler_params=_params("parallel", "parallel"))(dfo, w_down, hg, hu)


def _loss_grad(y, target, name):
    s, d = y.shape
    ts = min(s, 512)

    def body(y_ref, t_ref, dy_ref, sq_ref):
        e = y_ref[...] - t_ref[...]
        dy_ref[...] = e / d

        @pl.when(pl.program_id(0) == 0)
        def _():
            sq_ref[...] = jnp.zeros_like(sq_ref)

        sq_ref[...] += jnp.sum(e * e, axis=0, keepdims=True)

    return pl.pallas_call(
        body, name=name, out_shape=(jax.ShapeDtypeStruct((s, d), F32), jax.ShapeDtypeStruct((1, d), F32)),
        grid=(s // ts,), in_specs=[_rows(ts, d), _rows(ts, d)], out_specs=(_rows(ts, d), _fixed((1, d))),
        compiler_params=_params("arbitrary"))(y, target)


def _adamw(w, g, m, v, name):
    shape = w.shape
    cols = shape[-1]
    keep3 = w.ndim == 3 and shape[1] < SUBLANES
    view = shape if keep3 else (math.prod(shape[:-1]), cols)
    rows = view[0]
    if keep3:
        cap = max(1, (1 << 20) // (SUBLANES * cols * 4))
        tr = max(t for t in range(1, cap + 1) if rows % t == 0)
    else:
        tr = _row_tile(rows, cols * 4)

    def body(w_ref, g_ref, m_ref, v_ref, d_ref, mo_ref, vo_ref):
        gv = g_ref[...]
        mn = B1 * m_ref[...] + (1.0 - B1) * gv
        vn = B2 * v_ref[...] + (1.0 - B2) * (gv * gv)
        m_hat = mn / (1.0 - B1 ** STEP)
        v_hat = vn / (1.0 - B2 ** STEP)
        d_ref[...] = -LR * (m_hat / (jnp.sqrt(v_hat) + ADAM_EPS) + WD * w_ref[...])
        mo_ref[...] = mn
        vo_ref[...] = vn

    spec = pl.BlockSpec((tr,) + view[1:], lambda i: (i,) + (0,) * (len(view) - 1))
    out = jax.ShapeDtypeStruct(view, F32)
    res = pl.pallas_call(
        body, name=name, out_shape=(out,) * 3, grid=(rows // tr,), in_specs=[spec] * 4, out_specs=(spec,) * 3,
        compiler_params=_params("parallel"))(*[t.reshape(view) for t in (w, g, m, v)])
    return tuple(r.reshape(shape) for r in res)


LANE_MAJOR = ("w_uq", "w_uk", "w_uv", "w_gate", "w_up")


def _lane_major(name, a):
    if name == "w_in":
        return a.transpose(2, 0, 1)
    if name in LANE_MAJOR:
        return a.transpose(0, 2, 1)
    return a


def _from_lane_major(name, a):
    if name == "w_in":
        return a.transpose(1, 2, 0)
    return _lane_major(name, a)


ANY = pl.BlockSpec(memory_space=pl.ANY)


class _GatherRide:
    def __init__(self, arrays):
        n = len(arrays)
        self.arrays = list(arrays)
        self.out_shape = [jax.ShapeDtypeStruct((N_DEV,) + a.shape, a.dtype) for a in arrays]
        self.scratch = [pltpu.SemaphoreType.DMA((n, 7)), pltpu.SemaphoreType.DMA((n, 7)), pltpu.SemaphoreType.DMA((n,))]

    def _copies(self, ins, outs, sems):
        send_sems, recv_sems, local_sems = sems
        n = len(self.arrays)
        x, y, c = lax.axis_index("x"), lax.axis_index("y"), lax.axis_index("c")
        me, sibling = (x, y, c), (x, y, 1 - c)
        chips = [(1 - x, y), (x, 1 - y), (1 - x, 1 - y)]

        def slot(a, px, py, pc):
            return outs[a].at[4 * px + 2 * py + pc]

        def copy(a, k, block, to, src=None):
            return pltpu.make_async_remote_copy(
                src_ref=slot(a, *block) if src is None else src, dst_ref=slot(a, *block), send_sem=send_sems.at[a, k],
                recv_sem=recv_sems.at[a, k], device_id=to, device_id_type=MESH)

        mine = [pltpu.make_async_copy(ins[a], slot(a, *me), local_sems.at[a]) for a in range(n)]
        first = []
        for a in range(n):
            first.append(copy(a, 0, me, sibling, src=ins[a]))
            first += [copy(a, 1 + j, me, (*chip, c), src=ins[a]) for j, chip in enumerate(chips)]
        return n, me, sibling, chips, c, copy, mine, first

    def start(self, ins, outs, sems):
        _, _, _, _, _, _, mine, first = self._copies(ins, outs, sems)
        for cp in mine + first:
            cp.start()

    def finish(self, ins, outs, sems):
        n, me, sibling, chips, c, copy, mine, first = self._copies(ins, outs, sems)
        passed = []
        for j, chip in enumerate(chips):
            for a in range(n):
                copy(a, 1 + j, (*chip, c), me).wait_recv()
                passed.append(copy(a, 4 + j, (*chip, c), sibling))
                passed[-1].start()
        for a in range(n):
            copy(a, 0, sibling, me).wait_recv()
            for j, chip in enumerate(chips):
                copy(a, 4 + j, (*chip, 1 - c), me).wait_recv()
        for cp in first + passed:
            cp.wait_send()
        for cp in mine:
            cp.wait()


class _ReduceRide:
    def __init__(self, arrays):
        n = len(arrays)
        self.arrays = list(arrays)
        self.out_shape = [jax.ShapeDtypeStruct(a.shape, a.dtype) for a in arrays]
        self.scratch = [pltpu.SemaphoreType.DMA((n, 7)), pltpu.SemaphoreType.DMA((n, 7)), pltpu.SemaphoreType.DMA((n,))]

    def _copies(self, ins, outs, sems):
        send_sems, recv_sems, local_sems = sems
        n = len(self.arrays)
        x, y, c = lax.axis_index("x"), lax.axis_index("y"), lax.axis_index("c")
        mine = [pltpu.make_async_copy(ins[a].at[4 * x + 2 * y + c], outs[a].at[0], local_sems.at[a]) for a in range(n)]
        copies = []
        for a in range(n):
            for k in range(1, N_DEV):
                px = 1 - x if k & 4 else x
                py = 1 - y if k & 2 else y
                pc = 1 - c if k & 1 else c
                copies.append(pltpu.make_async_remote_copy(
                    src_ref=ins[a].at[4 * px + 2 * py + pc], dst_ref=outs[a].at[k], send_sem=send_sems.at[a, k - 1],
                    recv_sem=recv_sems.at[a, k - 1], device_id=(px, py, pc), device_id_type=MESH))
        return mine, copies

    def start(self, ins, outs, sems):
        mine, copies = self._copies(ins, outs, sems)
        for cp in mine + copies:
            cp.start()

    def finish(self, ins, outs, sems):
        mine, copies = self._copies(ins, outs, sems)
        for cp in copies + mine:
            cp.wait()


def _run_ride(ride, name):
    n = len(ride.arrays)

    def body(*refs):
        ins, outs, sems = refs[:n], refs[n:2 * n], refs[2 * n:]
        ride.start(ins, outs, sems)
        ride.finish(ins, outs, sems)

    return pl.pallas_call(body, name=name, out_shape=ride.out_shape, in_specs=[ANY] * n, out_specs=[ANY] * n,
                          scratch_shapes=ride.scratch)(*ride.arrays)


def _all_gather(arrays, name):
    return _run_ride(_GatherRide(arrays), name)


def _swap_with_sibling(arrays, name):
    n = len(arrays)

    def body(*refs):
        ins, outs = refs[:n], refs[n:2 * n]
        send_sems, recv_sems = refs[2 * n:]
        x, y, c = lax.axis_index("x"), lax.axis_index("y"), lax.axis_index("c")
        copies = [pltpu.make_async_remote_copy(
            src_ref=ins[a].at[1 - c], dst_ref=outs[a], send_sem=send_sems.at[a], recv_sem=recv_sems.at[a],
            device_id=(x, y, 1 - c), device_id_type=MESH) for a in range(n)]
        for cp in copies:
            cp.start()
        for cp in copies:
            cp.wait()

    return pl.pallas_call(
        body, name=name, out_shape=[jax.ShapeDtypeStruct(a.shape[1:], a.dtype) for a in arrays],
        in_specs=[ANY] * n, out_specs=[ANY] * n,
        scratch_shapes=[pltpu.SemaphoreType.DMA((n,)), pltpu.SemaphoreType.DMA((n,))])(*arrays)


class _ChipExchangeRide:
    def __init__(self, arrays):
        n = len(arrays)
        self.arrays = list(arrays)
        self.out_shape = [jax.ShapeDtypeStruct(a.shape, a.dtype) for a in arrays]
        self.scratch = [pltpu.SemaphoreType.DMA((n, 3)), pltpu.SemaphoreType.DMA((n, 3)), pltpu.SemaphoreType.DMA((n,))]

    def _copies(self, ins, outs, sems):
        send_sems, recv_sems, local_sems = sems
        n = len(self.arrays)
        x, y, c = lax.axis_index("x"), lax.axis_index("y"), lax.axis_index("c")
        partners = [(x, 1 - y), (1 - x, y), (1 - x, 1 - y)]
        mine = [pltpu.make_async_copy(ins[a].at[2 * x + y], outs[a].at[0], local_sems.at[a]) for a in range(n)]
        copies = [pltpu.make_async_remote_copy(
            src_ref=ins[a].at[2 * px + py], dst_ref=outs[a].at[1 + k], send_sem=send_sems.at[a, k],
            recv_sem=recv_sems.at[a, k], device_id=(px, py, c), device_id_type=MESH)
            for a in range(n) for k, (px, py) in enumerate(partners)]
        return mine, copies

    def start(self, ins, outs, sems):
        mine, copies = self._copies(ins, outs, sems)
        for cp in mine + copies:
            cp.start()

    def finish(self, ins, outs, sems):
        mine, copies = self._copies(ins, outs, sems)
        for cp in copies + mine:
            cp.wait()


class _Combo:
    def __init__(self, rides):
        self.rides = rides
        self.arrays = [a for r in rides for a in r.arrays]
        self.out_shape = [o for r in rides for o in r.out_shape]
        self.scratch = [sc for r in rides for sc in r.scratch]

    def _parts(self, ins, outs, sems):
        at_a = at_s = 0
        for r in self.rides:
            na, ns = len(r.arrays), len(r.scratch)
            yield r, ins[at_a:at_a + na], outs[at_a:at_a + na], sems[at_s:at_s + ns]
            at_a, at_s = at_a + na, at_s + ns

    def start(self, ins, outs, sems):
        for r, i, o, sm in self._parts(ins, outs, sems):
            r.start(i, o, sm)

    def finish(self, ins, outs, sems):
        for r, i, o, sm in self._parts(ins, outs, sems):
            r.finish(i, o, sm)


def _as_rows(a, lead):
    return a.reshape(a.shape[:lead] + (math.prod(a.shape[lead:-1]), a.shape[-1]))


def _add_pairs(a, b, name):
    a2, b2 = _as_rows(a, 0), _as_rows(b, 0)
    rows, cols = a2.shape
    tr = _row_tile(rows, cols * 4)

    def body(a_ref, b_ref, o_ref):
        o_ref[...] = (a_ref[...].astype(F32) + b_ref[...].astype(F32)).astype(o_ref.dtype)

    spec = _rows(tr, cols)
    out = pl.pallas_call(body, name=name, out_shape=jax.ShapeDtypeStruct(a2.shape, a.dtype), grid=(rows // tr,),
                         in_specs=[spec, spec], out_specs=spec, compiler_params=_params("parallel"))(a2, b2)
    return out.reshape(a.shape)


def _sum_blocks(a, name):
    a3 = _as_rows(a, 1)
    n, rows, cols = a3.shape
    tr = _row_tile(rows, n * cols * 4)

    def body(a_ref, o_ref):
        tot = a_ref[0].astype(F32)
        for k in range(1, n):
            tot = tot + a_ref[k].astype(F32)
        o_ref[...] = tot

    out = pl.pallas_call(body, name=name, out_shape=jax.ShapeDtypeStruct((rows, cols), F32), grid=(rows // tr,),
                         in_specs=[pl.BlockSpec((n, tr, cols), lambda j: (0, j, 0))], out_specs=_rows(tr, cols),
                         compiler_params=_params("parallel"))(a3)
    return out.reshape(a.shape[1:])


MIX_GROUPS = ("w_in", "w_uq", "w_uk", "w_uv", "w_attn_o", "w_conv_o", "w_pool_o", "w_mix_o")
FFN_GROUPS = ("w_gate", "w_up", "w_down")
MIX_EARLY = ("w_attn_o", "w_conv_o", "w_pool_o", "w_mix_o")
MIX_LATE = ("w_in", "w_uq", "w_uk", "w_uv")


def _pad_axis(a, axis, size):
    pad = [(0, 0)] * a.ndim
    pad[axis] = (0, size - a.shape[axis])
    return jnp.pad(a, pad)


def _local_groups(sh, l):
    out = {n: sh[n][l] for n in BIG}
    for n in ("w_uq", "w_uk", "w_uv"):
        out[n] = _pad_axis(out[n], -1, HEAD_PAD)
    for n in ("w_gate", "w_up"):
        out[n] = _pad_axis(out[n], -1, FF_SHARD_PAD)
    out["w_down"] = _pad_axis(out["w_down"], 0, FF_SHARD_PAD)
    return {n: v.astype(BF16) for n, v in out.items()}


def _arrange_w_in(blocks):
    parts, pos = [], 0
    for ref_lo, ref_hi, at in sorted(W_IN_PIECES, key=lambda p: p[2]):
        if at > pos:
            parts.append(jnp.zeros((blocks.shape[1], at - pos), blocks.dtype))
        for d in range(N_DEV):
            lo, hi = max(ref_lo, d * W_IN_SHARD), min(ref_hi, (d + 1) * W_IN_SHARD)
            if lo < hi:
                parts.append(blocks[d][:, lo - d * W_IN_SHARD:hi - d * W_IN_SHARD])
        pos = at + ref_hi - ref_lo
    if pos < Z_W:
        parts.append(jnp.zeros((blocks.shape[1], Z_W - pos), blocks.dtype))
    return jnp.concatenate(parts, axis=1)


def _w_in_shard(g, d):
    parts = []
    for ref_lo, ref_hi, at in W_IN_PIECES:
        lo, hi = max(ref_lo, d * W_IN_SHARD), min(ref_hi, (d + 1) * W_IN_SHARD)
        if lo < hi:
            parts.append(g[:, at + lo - ref_lo:at + hi - ref_lo])
    return jnp.concatenate(parts, axis=1)


def _mixer_weights(gat):
    w = {n: v for n, v in gat.items() if n != "w_in"}
    attn_o = gat["w_attn_o"].reshape(N_DEV, N_HEADS, V_HEAD, LANES)
    w["w_attn_o"] = _pad_axis(attn_o, 2, HEAD_PAD).reshape(N_DEV, N_HEADS * HEAD_PAD, LANES)
    w["w_mix_o"] = gat["w_mix_o"].reshape(D_MODEL, D_MODEL)
    return w


def _ffn_weights(gat):
    return {"w_gate": gat["w_gate"], "w_up": gat["w_up"], "w_down": gat["w_down"].reshape(D_FF_PAD, D_MODEL)}


def _mixer_grad_groups(gb):
    g = dict(gb)
    if "w_in" in gb:
        g["w_in"] = jnp.stack([_w_in_shard(gb["w_in"], d) for d in range(N_DEV)])
    if "w_attn_o" in gb:
        attn_o = gb["w_attn_o"].reshape(N_DEV, N_HEADS, HEAD_PAD, LANES)[:, :, :V_HEAD]
        g["w_attn_o"] = attn_o.reshape(N_DEV, N_HEADS * V_HEAD, LANES)
    if "w_mix_o" in gb:
        g["w_mix_o"] = gb["w_mix_o"].reshape(N_DEV, D_MODEL // N_DEV, D_MODEL)
    return g


def _ffn_grad_groups(gb):
    return {"w_gate": gb["w_gate"], "w_up": gb["w_up"], "w_down": gb["w_down"].reshape(N_DEV, FF_SHARD_PAD, D_MODEL)}


def _grads_from_groups(tot):
    g = dict(tot)
    g["w_uq"] = tot["w_uq"][:, :QK_NOPE + QK_ROPE]
    g["w_uk"], g["w_uv"] = tot["w_uk"][:, :QK_NOPE], tot["w_uv"][:, :V_HEAD]
    g["w_gate"], g["w_up"] = tot["w_gate"][:, :FF_SHARD], tot["w_up"][:, :FF_SHARD]
    g["w_down"] = tot["w_down"][:FF_SHARD]
    return g


SMALL_GROUPS = (
    (D_MODEL, ("mix_norm_pre", "mix_norm_post", "ffn_norm_pre", "ffn_norm_post")),
    (CONV_C, ("conv_w", "conv_b", "conv_ln_g", "conv_ln_b", "pool_scale")),
    (Q_RANK, ("q_norm",)), (KV_RANK, ("kv_norm",)), (POOL_GD, ("pool_w",)),
)


def _small_rows(name):
    return {"conv_w": CONV_HALO, "pool_w": POOL_G * POOL_GD}.get(name, SUBLANES)


def _small_groups(small):
    out = []
    for width, names in SMALL_GROUPS:
        parts = []
        for l in range(DEPTH):
            for n in names:
                part = small[l][n].reshape(-1, width)
                parts.append(_pad_axis(part, 0, _small_rows(n)))
        out.append(jnp.concatenate(parts, axis=0))
    return out


def _small_from_groups(groups):
    shapes = {"conv_w": (CONV_W, CONV_C), "pool_w": (POOL_G, POOL_GD, POOL_GD)}
    out = {}
    for (width, names), g in zip(SMALL_GROUPS, groups):
        row = 0
        for l in range(DEPTH):
            for n in names:
                rows = _small_rows(n)
                real = {"conv_w": CONV_W, "pool_w": POOL_G * POOL_GD}.get(n, 1)
                out.setdefault(n, []).append(g[row:row + real].reshape(shapes.get(n, (width,))))
                row += rows
    return {n: jnp.stack(v) for n, v in out.items()}


def _mixer_fwd(x, tables, sm, plan, l):
    nm = lambda n: f"{n}_l{l}"
    h = _rms_fwd(x, (D_MODEL, 0), sm["mix_norm_pre"], BF16, nm("mix_pre_norm"))
    w_in, ride = plan.w_in(l), plan.in_proj_ride(l)
    if ride is None:
        z = _matmul(h, w_in, "nn", BF16, nm("in_proj"))
    else:
        z, rode = _matmul(h, w_in, "nn", BF16, nm("in_proj"), ride=ride)
        plan.in_proj_done(l, rode)
    w = dict(plan.mixer_weights(l), w_in=w_in)
    cq = _rms_fwd(z, ZC_Q, sm["q_norm"], BF16, nm("q_norm"))
    ckv = _rms_fwd(z, ZC_KV, sm["kv_norm"], BF16, nm("kv_norm"))
    qf = _matmul(cq, w["w_uq"], "nn", F32, nm("q_up"))
    kf = _matmul(ckv, w["w_uk"], "nn", F32, nm("k_up"))
    v = _matmul(ckv, w["w_uv"], "nn", BF16, nm("v_up"))
    q, k = _rope_qk_fwd(qf, kf, z, tables, nm("rope_qk"))
    (o, lse), rode = _flash_fwd(q, k, v, nm("flash_fwd"), plan.fwd_ride(l))
    plan.fwd_done(l, rode)
    y_attn = _matmul(o, w["w_attn_o"], "nn", BF16, nm("attn_out"))
    hc, co = _conv_fwd(z, sm["conv_w"], sm["conv_b"], sm["conv_ln_g"], sm["conv_ln_b"], nm("conv_fwd"))
    y_conv = _matmul(hc, w["w_conv_o"], "nn", BF16, nm("conv_out"))
    pm = _pool_fwd(z, sm["pool_w"], sm["pool_scale"], nm("pool_fwd"))
    y_pool = _matmul(pm, w["w_pool_o"], "nn", BF16, nm("pool_out"))
    ys = (y_attn, y_conv, y_pool)
    merged = _merge_fwd(z, ys, nm("merge_fwd"))
    mo = _matmul(merged, w["w_mix_o"], "nn", F32, nm("mix_out"))
    x_mid = _rms_fwd(mo, (D_MODEL, 0), sm["mix_norm_post"], F32, nm("mix_post_norm"), res=x)
    saved = dict(x=x, h=h, z=z, cq=cq, ckv=ckv, q=q, k=k, v=v, o=o, lse=lse, hc=hc, co=co, pm=pm, ys=ys, merged=merged,
                 mo=mo)
    return x_mid, saved, w


def _ffn_fwd(x_mid, w, sm, tag):
    nm = lambda n: f"{n}_{tag}"
    h2 = _rms_fwd(x_mid, (D_MODEL, 0), sm["ffn_norm_pre"], BF16, nm("ffn_pre_norm"))
    hg, hu, act = _ffn_up_fwd(h2, w["w_gate"], w["w_up"], nm("ffn_up_fwd"))
    fo = _matmul(act, w["w_down"], "nn", F32, nm("ffn_down"))
    out = _rms_fwd(fo, (D_MODEL, 0), sm["ffn_norm_post"], F32, nm("ffn_post_norm"), res=x_mid)
    saved = dict(x_mid=x_mid, h2=h2, hg=hg, hu=hu, act=act, fo=fo)
    return out, saved


def _ffn_bwd(dout, sv, w, sm, tag):
    nm = lambda n: f"{n}_{tag}"
    gb, gs = {}, {}
    dfo, gs["ffn_norm_post"] = _rms_bwd(sv["fo"], (D_MODEL, 0), sm["ffn_norm_post"], dout, BF16, nm("ffn_post_norm_bwd"))
    gb["w_down"] = _matmul(sv["act"], dfo, "tn", BF16, nm("ffn_down_dw"))
    dhg, dhu = _ffn_down_bwd(dfo, w["w_down"], sv["hg"], sv["hu"], nm("ffn_down_bwd"))
    dh2_g = _matmul(dhg, w["w_gate"], "nt", F32, nm("ffn_gate_dx"))
    dh2 = _matmul(dhu, w["w_up"], "nt", F32, nm("ffn_up_dx"), add=dh2_g)
    gb["w_gate"] = _matmul(sv["h2"], dhg, "tn", BF16, nm("ffn_gate_dw"), blocked=True)
    gb["w_up"] = _matmul(sv["h2"], dhu, "tn", BF16, nm("ffn_up_dw"), blocked=True)
    dmid, gs["ffn_norm_pre"] = _rms_bwd(sv["x_mid"], (D_MODEL, 0), sm["ffn_norm_pre"], dh2, F32, nm("ffn_pre_norm_bwd"),
                                        add=dout)
    return dmid, gb, gs


def _mixer_bwd(dmid, sv, tables, w, sm, plan, l):
    nm = lambda n: f"{n}_l{l}"
    gb, gs = {}, {}
    dmo, gs["mix_norm_post"] = _rms_bwd(sv["mo"], (D_MODEL, 0), sm["mix_norm_post"], dmid, BF16, nm("mix_post_norm_bwd"))
    dmerged = _matmul(dmo, w["w_mix_o"], "nt", F32, nm("mix_out_dx"))
    gb["w_mix_o"] = _matmul(sv["merged"], dmo, "tn", BF16, nm("mix_out_dw"))
    dya, dyc, dyp, dz = _merge_bwd(sv["z"], sv["ys"], dmerged, nm("merge_bwd"))
    dpm = _matmul(dyp, w["w_pool_o"], "nt", F32, nm("pool_out_dx"))
    gb["w_pool_o"] = _matmul(sv["pm"], dyp, "tn", BF16, nm("pool_out_dw"), blocked=True)
    dz, gs["pool_w"], gs["pool_scale"] = _pool_bwd(dpm, sv["z"], sm["pool_w"], sm["pool_scale"], dz, nm("pool_bwd"))
    dhc = _matmul(dyc, w["w_conv_o"], "nt", F32, nm("conv_out_dx"))
    gb["w_conv_o"] = _matmul(sv["hc"], dyc, "tn", BF16, nm("conv_out_dw"), blocked=True)
    dco, gs["conv_ln_g"], gs["conv_ln_b"], gs["conv_b"] = _conv_bwd_norm(dhc, sv["co"], sm["conv_ln_g"], sm["conv_ln_b"],
                                                                        nm("conv_bwd_norm"))
    dz, gs["conv_w"] = _conv_bwd_taps(dco, sv["z"], sm["conv_w"], dz, nm("conv_bwd_taps"))
    do = _matmul(dya, w["w_attn_o"], "nt", F32, nm("attn_out_dx"))
    gb["w_attn_o"] = _matmul(sv["o"], dya, "tn", BF16, nm("attn_out_dw"), blocked=True)
    delta, dob = _attn_delta(do, sv["o"], nm("attn_delta"))
    (dq, dk, dv), rode = _flash_bwd(sv["q"], sv["k"], sv["v"], dob, sv["lse"], delta, nm("flash_bwd"),
                                  plan.bwd_ride(l, gb))
    plan.bwd_done(l, rode)
    dqf, dkf, dz = _rope_qk_bwd(dq, dk, tables, dz, nm("rope_qk_bwd"))
    dcq_n = _matmul(dqf, w["w_uq"], "nt", F32, nm("q_up_dx"))
    gb["w_uq"] = _matmul(sv["cq"], dqf, "tn", BF16, nm("q_up_dw"), blocked=True)
    dckv_k = _matmul(dkf, w["w_uk"], "nt", F32, nm("k_up_dx"))
    dckv_n = _matmul(dv, w["w_uv"], "nt", F32, nm("v_up_dx"), add=dckv_k)
    gb["w_uk"] = _matmul(sv["ckv"], dkf, "tn", BF16, nm("k_up_dw"), blocked=True)
    gb["w_uv"] = _matmul(sv["ckv"], dv, "tn", BF16, nm("v_up_dw"), blocked=True)
    dz, gs["q_norm"] = _rms_bwd(sv["z"], ZC_Q, sm["q_norm"], dcq_n, BF16, nm("q_norm_bwd"), dz=dz)
    dz, gs["kv_norm"] = _rms_bwd(sv["z"], ZC_KV, sm["kv_norm"], dckv_n, BF16, nm("kv_norm_bwd"), dz=dz)
    gb["w_in"] = _matmul(sv["h"], dz, "tn", BF16, nm("in_proj_dw"))
    plan.add_grads(l, "mix", gb)
    ride = plan.tail_ride(l)
    if ride is None:
        dh = _matmul(dz, w["w_in"], "nt", F32, nm("in_proj_dx"))
    else:
        dh, rode = _matmul(dz, w["w_in"], "nt", F32, nm("in_proj_dx"), ride=ride)
        plan.tail_done(l, rode)
    dx, gs["mix_norm_pre"] = _rms_bwd(sv["x"], (D_MODEL, 0), sm["mix_norm_pre"], dh, F32, nm("mix_pre_norm_bwd"), add=dmid)
    return dx, gs


def _part_groups(part):
    return {"mix": MIX_GROUPS, "ffn": FFN_GROUPS, "early": MIX_EARLY, "late": MIX_LATE}[part]


class _Plan:
    def __init__(self, shards, conv_w):
        self.local = [_local_groups(shards, l) for l in range(DEPTH)]
        self.conv_w = conv_w
        self.gat, self.send, self.recv = {}, {}, {}

    @staticmethod
    def _riders(l):
        return [(l, "ffn")] + ([(l + 1, "mix")] if l + 1 < DEPTH else [])

    @staticmethod
    def _grad_riders(l):
        return [(l, "ffn"), (l, "early")] + ([(l + 1, "late")] if l + 1 < DEPTH else [])

    def gather_first(self):
        w_in, conv_w = _all_gather([self.local[0]["w_in"], self.conv_w], "gather_w_in_l0")
        self.gat[(0, "mix")] = {"w_in": w_in}
        return conv_w

    def w_in(self, l):
        return _arrange_w_in(self.gat[(l, "mix")]["w_in"])

    def in_proj_ride(self, l):
        return _GatherRide([self.local[0][g] for g in MIX_GROUPS[1:]]) if l == 0 else None

    def in_proj_done(self, l, outs):
        self.gat[(l, "mix")].update(zip(MIX_GROUPS[1:], outs))

    def fwd_ride(self, l):
        return _GatherRide([self.local[ll][g] for ll, part in self._riders(l) for g in _part_groups(part)])

    def fwd_done(self, l, outs):
        outs = list(outs)
        for ll, part in self._riders(l):
            self.gat[(ll, part)] = {g: outs.pop(0) for g in _part_groups(part)}

    def mixer_weights(self, l):
        return _mixer_weights(self.gat[(l, "mix")])

    def ffn_weights(self, l):
        return _ffn_weights(self.gat[(l, "ffn")])

    def add_grads(self, l, part, gb):
        if part == "ffn":
            self.send[(l, "ffn")] = _ffn_grad_groups(gb)
        else:
            self.send[(l, "late")] = _mixer_grad_groups({g: gb[g] for g in MIX_LATE})

    def bwd_ride(self, l, gb_early):
        self.send[(l, "early")] = _mixer_grad_groups({g: gb_early[g] for g in MIX_EARLY})
        return _ReduceRide([self.send[(ll, part)][g] for ll, part in self._grad_riders(l) for g in _part_groups(part)])

    def bwd_done(self, l, outs):
        outs = list(outs)
        for ll, part in self._grad_riders(l):
            self.recv[(ll, part)] = {g: outs.pop(0) for g in _part_groups(part)}

    def tail_ride(self, l):
        return _ReduceRide([self.send[(0, "late")][g] for g in MIX_LATE]) if l == 0 else None

    def tail_done(self, l, outs):
        self.recv[(l, "late")] = dict(zip(MIX_LATE, outs))

    def finish(self):
        layers = []
        for l in range(DEPTH):
            tot = {g: _sum_blocks(a, f"reduce_sum_{g}_l{l}") for part in ("early", "late", "ffn")
                   for g, a in self.recv[(l, part)].items()}
            layers.append(_grads_from_groups(tot))
        return layers


def _local_step(x, positions, target, smalls, plan):
    tables = _rope_tables(positions)
    saved = []
    h = x
    for l in range(DEPTH):
        h, svm, wm = _mixer_fwd(h, tables, smalls[l], plan, l)
        wf = plan.ffn_weights(l)
        h, svf = _ffn_fwd(h, wf, smalls[l], f"l{l}")
        saved.append((svm, svf, wm, wf))
    dy, sq = _loss_grad(h, target, "loss_grad")
    small = [None] * DEPTH
    for l in reversed(range(DEPTH)):
        svm, svf, wm, wf = saved[l]
        dmid, gbf, gsf = _ffn_bwd(dy, svf, wf, smalls[l], f"l{l}")
        plan.add_grads(l, "ffn", gbf)
        dy, gsm = _mixer_bwd(dmid, svm, tables, wm, smalls[l], plan, l)
        small[l] = {**gsf, **gsm}
    return sq, dy, small


def kernel(x, positions, mix_norm_pre, w_in, q_norm, w_uq, kv_norm, w_uk, w_uv, w_attn_o, conv_w, conv_b, conv_ln_g, conv_ln_b, w_conv_o, pool_w, pool_scale, w_pool_o, w_mix_o, mix_norm_post, ffn_norm_pre, w_gate, w_up, w_down, ffn_norm_post, loss_target, m_mix_norm_pre, m_w_in, m_q_norm, m_w_uq, m_kv_norm, m_w_uk, m_w_uv, m_w_attn_o, m_conv_w, m_conv_b, m_conv_ln_g, m_conv_ln_b, m_w_conv_o, m_pool_w, m_pool_scale, m_w_pool_o, m_w_mix_o, m_mix_norm_post, m_ffn_norm_pre, m_w_gate, m_w_up, m_w_down, m_ffn_norm_post, v_mix_norm_pre, v_w_in, v_q_norm, v_w_uq, v_kv_norm, v_w_uk, v_w_uv, v_w_attn_o, v_conv_w, v_conv_b, v_conv_ln_g, v_conv_ln_b, v_w_conv_o, v_pool_w, v_pool_scale, v_w_pool_o, v_w_mix_o, v_mix_norm_post, v_ffn_norm_pre, v_w_gate, v_w_up, v_w_down, v_ffn_norm_post):
    given = dict(locals())
    dev = 4 * lax.axis_index("x") + 2 * lax.axis_index("y") + lax.axis_index("c")

    plan = _Plan({n: given[n] for n in BIG}, conv_w)
    cw = CONV_C // N_DEV
    conv_w_full = plan.gather_first().transpose(1, 2, 0, 3).reshape(DEPTH, CONV_W, CONV_C)
    smalls = []
    for l in range(DEPTH):
        sm = {n: given[n][l] for n in SMALL if n != "conv_w"}
        sm["conv_w"] = _pad_axis(conv_w_full[l], 0, CONV_HALO)
        smalls.append(sm)

    sq, grad_x, small = _local_step(x[0], positions[0], loss_target[0], smalls, plan)
    loss = lax.psum(0.5 / D_MODEL * jnp.sum(sq), ("x", "y", "c"))
    per_layer = plan.finish()
    small_groups = _all_gather(_small_groups(small), "gather_small_grads")
    views = {}
    for n in BIG:
        if n == "w_in":
            views[n] = jnp.stack([per_layer[l][n].T for l in range(DEPTH)], axis=1)
        elif n in LANE_MAJOR:
            views[n] = jnp.stack([per_layer[l][n].T for l in range(DEPTH)])
        else:
            views[n] = jnp.stack([per_layer[l][n] for l in range(DEPTH)])
    grads = {n: _from_lane_major(n, views[n]) for n in BIG}

    small_sum = _small_from_groups([_sum_blocks(g, f"sum_small_grads_{i}") for i, g in enumerate(small_groups)])
    for n in SMALL:
        grads[n] = small_sum[n]
    grads["conv_w"] = lax.dynamic_slice_in_dim(small_sum["conv_w"], dev * cw, cw, axis=2)

    delta, new_m, new_v = {}, {}, {}
    for n in WEIGHTS:
        g_view = views[n] if n in views else grads[n]
        w_view, m_view, v_view = [_lane_major(n, given[k]) for k in (n, "m_" + n, "v_" + n)]
        res = _adamw(w_view, g_view, m_view, v_view, f"adamw_{n}")
        delta[n], new_m[n], new_v[n] = [_from_lane_major(n, r) for r in res]
    return (loss, grad_x[None], *[grads[n] for n in WEIGHTS], *[delta[n] for n in WEIGHTS],
            *[new_m[n] for n in WEIGHTS], *[new_v[n] for n in WEIGHTS])
```

```python
import functools
import math

import jax
import jax.numpy as jnp
from jax import lax
from jax.experimental import pallas as pl
from jax.experimental.pallas import tpu as pltpu

F32, BF16 = jnp.float32, jnp.bfloat16
MESH = pl.DeviceIdType.MESH

LANES = 128
SUBLANES = 8
VMEM_LIMIT_BYTES = 56 * 1024 * 1024
MATMUL_VMEM_BYTES = 40 * 1024 * 1024

N_DEV = 8
D_MODEL = 1024
DEPTH = 2
N_HEADS = 8
QK_NOPE, QK_ROPE, V_HEAD = 64, 32, 64
HEAD_PAD = LANES
Q_RANK, KV_RANK = 384, 256
ROPE_THETA = 10000.0
CONV_C, CONV_W = 512, 31
CONV_HALO = 32
POOL_WINDOWS = (2, 4, 8, 16)
POOL_C, POOL_G = 512, 4
POOL_GD = POOL_C // POOL_G
D_FF = 2816
FF_SHARD = D_FF // N_DEV
FF_SHARD_PAD = 3 * LANES
D_FF_PAD = N_DEV * FF_SHARD_PAD
W_IN_SHARD = 660
EPS = 1e-6
ATTN_SCALE = 1.0 / math.sqrt(QK_NOPE + QK_ROPE)
LOG2E = 1.4426950408889634
LR, B1, B2, ADAM_EPS, WD, STEP = 0.001, 0.9, 0.999, 1e-08, 0.01, 10

Z_W = 5376
ZC_GATE = (1024, 0)
ZC_GATES = (3072, 0)
ZC_CONV_A = (512, 6)
ZC_CONV_G = (512, 7)
ZC_CONV = (1024, 3)
ZC_POOL = (512, 8)
ZC_Q = (384, 12)
ZC_KR = (128, 39)
ZC_KV = (256, 20)
W_IN_PIECES = ((0, 384, 4608), (384, 640, 5120), (640, 672, 5056), (672, 1696, 3072), (1696, 2208, 4096),
               (2208, 5280, 0))

BIG = ("w_in", "w_uq", "w_uk", "w_uv", "w_attn_o", "w_conv_o", "w_pool_o", "w_mix_o", "w_gate", "w_up", "w_down")
SMALL = ("mix_norm_pre", "q_norm", "kv_norm", "conv_w", "conv_b", "conv_ln_g", "conv_ln_b", "pool_w", "pool_scale",
         "mix_norm_post", "ffn_norm_pre", "ffn_norm_post")
WEIGHTS = ("mix_norm_pre", "w_in", "q_norm", "w_uq", "kv_norm", "w_uk", "w_uv", "w_attn_o", "conv_w", "conv_b",
           "conv_ln_g", "conv_ln_b", "w_conv_o", "pool_w", "pool_scale", "w_pool_o", "w_mix_o", "mix_norm_post",
           "ffn_norm_pre", "w_gate", "w_up", "w_down", "ffn_norm_post")


def _params(*semantics):
    return pltpu.CompilerParams(dimension_semantics=semantics, vmem_limit_bytes=VMEM_LIMIT_BYTES)


def _tile(dim, cap):
    if dim <= cap:
        return dim
    for t in range(cap - cap % LANES, 0, -LANES):
        if dim % t == 0:
            return t
    raise ValueError(f"no tile for {dim} under {cap}")


def _row_tile(rows, row_bytes, budget=1 << 20):
    if rows * row_bytes <= budget:
        return rows
    cap = max(16, budget // row_bytes)
    for t in range(cap - cap % 16, 0, -16):
        if rows % t == 0:
            return t
    return rows


def _rows(ts, width, cidx=0):
    return pl.BlockSpec((ts, width), lambda i: (i, cidx))


def _fixed(shape):
    return pl.BlockSpec(shape, lambda *_: (0,) * len(shape))


def _sigmoid(x):
    return 1.0 / (1.0 + jnp.exp(-x))


def _matmul(a, b, mode, out_dtype, name, add=None, blocked=False, ride=None):
    nb = n_blk = 0
    blocked = blocked or b.ndim == 3
    if mode == "nn":
        (m, k) = a.shape
        n = b.shape[0] * b.shape[2] if blocked else b.shape[1]
    elif mode == "nt":
        (m, k) = a.shape
        n = b.shape[1] if blocked else b.shape[0]
    else:
        (k, m), n = a.shape, b.shape[1]
    if blocked:
        nb = b.shape[2] if mode != "tn" else n // N_DEV
    unit = nb if blocked and mode != "nt" else LANES
    out_bytes = jnp.dtype(out_dtype).itemsize + (4 if add is not None else 0)
    best = None
    for tn_c in range(unit, min(n, 1536) + 1, unit):
        for tm_c in sorted({256, 512, 1024, 2048, min(m, 2048)}):
            if n % tn_c or m % tm_c or (blocked and mode != "nt" and N_DEV % (tn_c // nb)):
                continue
            vmem = 2 * (tm_c * k * 2 + tn_c * k * 2 + tm_c * tn_c * out_bytes) + tm_c * tn_c * 4 + tn_c * k * 2
            if vmem <= MATMUL_VMEM_BYTES and (best is None or tm_c * tn_c / (tm_c + tn_c) > best[0]):
                best = (tm_c * tn_c / (tm_c + tn_c), tm_c, tn_c)
    if best is None:
        raise ValueError(f"{name}: no tiles for {m}x{n}x{k}")
    _, tm, tn = best
    if blocked:
        n_blk = N_DEV if mode == "nt" else tn // nb
    dims = {"nn": ((1,), (0,)), "nt": ((1,), (1,)), "tn": ((0,), (0,))}[mode]
    a_spec = pl.BlockSpec((k, tm), lambda i, j: (0, i)) if mode == "tn" else pl.BlockSpec((tm, k), lambda i, j: (i, 0))
    b_spec = pl.BlockSpec((tn, k), lambda i, j: (j, 0)) if mode == "nt" else pl.BlockSpec((k, tn), lambda i, j: (0, j))
    o_spec = pl.BlockSpec((tm, tn), lambda i, j: (i, j))
    out_shape = jax.ShapeDtypeStruct((m, n), out_dtype)
    if blocked and mode == "nn":
        b_spec = pl.BlockSpec((n_blk, k, nb), lambda i, j: (j, 0, 0))
    elif blocked and mode == "nt":
        b_spec = pl.BlockSpec((n_blk, tn, nb), lambda i, j: (0, j, 0))
    elif blocked:
        o_spec = pl.BlockSpec((n_blk, tm, nb), lambda i, j: (j, i, 0))
        out_shape = jax.ShapeDtypeStruct((N_DEV, m, nb), out_dtype)
    has_add = add is not None
    grid = (m // tm, n // tn)

    def body(*refs):
        (a_ref, b_ref, *rest), start, finish = _ride_hooks(ride, refs, 3 if has_add else 2, 1, grid)
        start()
        o_ref = rest[-1]
        if blocked and mode != "tn":
            bv = jnp.concatenate([b_ref[c] for c in range(n_blk)], axis=1) if n_blk > 1 else b_ref[0]
        else:
            bv = b_ref[...]
        total = lax.dot_general(a_ref[...], bv, (dims, ((), ())), preferred_element_type=F32)
        if has_add:
            total = total + rest[0][...]
        if blocked and mode == "tn":
            for c in range(n_blk):
                o_ref[c] = total[:, c * nb:(c + 1) * nb].astype(o_ref.dtype)
        else:
            o_ref[...] = total.astype(o_ref.dtype)
        finish()

    operands = (a, b, add) if has_add else (a, b)
    (out,), rode = _ride_call(ride, body, name, (out_shape,), grid, [a_spec, b_spec] + ([o_spec] if has_add else []),
                              (o_spec,), ("parallel", "parallel"), operands)
    return out if ride is None else (out, rode)


def _rms_fwd(x, win, gain, out_dtype, name, res=None):
    width, cidx = win
    s = x.shape[0]
    ts = min(s, 512)
    has_res = res is not None

    def body(x_ref, g_ref, *rest):
        o_ref = rest[-1]
        xv = x_ref[...].astype(F32)
        r = lax.rsqrt(jnp.mean(xv * xv, axis=-1, keepdims=True) + EPS)
        y = (xv * r) * g_ref[...]
        if has_res:
            y = rest[0][...] + y
        o_ref[...] = y.astype(o_ref.dtype)

    ops = (x, gain.reshape(1, width)) + ((res,) if has_res else ())
    return pl.pallas_call(
        body, name=name, out_shape=jax.ShapeDtypeStruct((s, width), out_dtype), grid=(s // ts,),
        in_specs=[_rows(ts, width, cidx), _fixed((1, width))] + ([_rows(ts, width)] if has_res else []),
        out_specs=_rows(ts, width), compiler_params=_params("parallel"))(*ops)


def _into(dz, n_inputs, out_index):
    return dict(in_specs=[ANY], operands=(dz,), input_output_aliases={n_inputs: out_index},
                out_shape=jax.ShapeDtypeStruct(dz.shape, dz.dtype))


def _rms_bwd(x, win, gain, dy, out_dtype, name, add=None, dz=None):
    width, cidx = win
    s = x.shape[0]
    ts = min(s, 512)
    has_add = add is not None

    def body(x_ref, g_ref, dy_ref, *rest):
        dx_ref, dg_ref = rest[-2], rest[-1]
        xv = x_ref[...].astype(F32)
        r = lax.rsqrt(jnp.mean(xv * xv, axis=-1, keepdims=True) + EPS)
        xh = xv * r
        dyv = dy_ref[...].astype(F32)
        dyg = dyv * g_ref[...]
        dx = r * (dyg - xh * jnp.mean(dyg * xh, axis=-1, keepdims=True))
        if has_add:
            dx = dx + rest[0][...]
        dx_ref[...] = dx.astype(dx_ref.dtype)

        @pl.when(pl.program_id(0) == 0)
        def _():
            dg_ref[...] = jnp.zeros_like(dg_ref)

        dg_ref[...] += jnp.sum(dyv * xh, axis=0, keepdims=True)

    ops = (x, gain.reshape(1, width), dy) + ((add,) if has_add else ())
    in_specs = [_rows(ts, width, cidx), _fixed((1, width)), _rows(ts, width)] + ([_rows(ts, width)] if has_add else [])
    dx_shape, dx_spec, alias = jax.ShapeDtypeStruct((s, width), out_dtype), _rows(ts, width), {}
    if dz is not None:
        into = _into(dz, len(ops), 0)
        ops, in_specs, alias = ops + into["operands"], in_specs + into["in_specs"], into["input_output_aliases"]
        dx_shape, dx_spec = into["out_shape"], _rows(ts, width, cidx)
    dx, dg = pl.pallas_call(
        body, name=name, out_shape=(dx_shape, jax.ShapeDtypeStruct((1, width), F32)), grid=(s // ts,),
        in_specs=in_specs, out_specs=(dx_spec, _fixed((1, width))), input_output_aliases=alias,
        compiler_params=_params("arbitrary"))(*ops)
    return dx, dg.reshape(width)


def _rope(x, c, s1, s2):
    return x * c + pltpu.roll(x, 16, 1) * s1 + pltpu.roll(x, LANES - 16, 1) * s2


def _rope_t(g, c, s1, s2):
    return g * c + pltpu.roll(g * s1, LANES - 16, 1) + pltpu.roll(g * s2, 16, 1)


def _rope_tables(positions):
    inv_freq = ROPE_THETA ** (-jnp.arange(0, QK_ROPE, 2, dtype=F32) / QK_ROPE)
    ang = positions.astype(F32)[:, None] * inv_freq
    cos, sin = jnp.cos(ang), jnp.sin(ang)
    n = positions.shape[0]
    one, zero = jnp.ones((n, 1), F32), jnp.zeros((n, 1), F32)
    c = jnp.concatenate([jnp.tile(one, (1, QK_NOPE)), cos, cos, jnp.tile(one, (1, 32))], axis=1)
    s1 = jnp.concatenate([jnp.tile(zero, (1, QK_NOPE + 16)), sin, jnp.tile(zero, (1, 32))], axis=1)
    s2 = jnp.concatenate([jnp.tile(zero, (1, QK_NOPE)), -sin, jnp.tile(zero, (1, 48))], axis=1)
    return c, s1, s2


def _rope_qk_fwd(qf, kf, z, tables, name):
    s = qf.shape[0]
    ts = min(s, 256)
    hw = N_HEADS * HEAD_PAD

    def body(qf_ref, kf_ref, kr_ref, c_ref, s1_ref, s2_ref, q_ref, k_ref):
        c, s1, s2 = c_ref[...], s1_ref[...], s2_ref[...]
        kr = _rope(kr_ref[...].astype(F32), c, s1, s2)
        for h in range(N_HEADS):
            sl = slice(h * HEAD_PAD, (h + 1) * HEAD_PAD)
            q_ref[:, sl] = _rope(qf_ref[:, sl], c, s1, s2).astype(BF16)
            k_ref[:, sl] = (kf_ref[:, sl] + kr).astype(BF16)

    tab = _rows(ts, LANES)
    return pl.pallas_call(
        body, name=name, out_shape=(jax.ShapeDtypeStruct((s, hw), BF16),) * 2, grid=(s // ts,),
        in_specs=[_rows(ts, hw), _rows(ts, hw), _rows(ts, *ZC_KR), tab, tab, tab],
        out_specs=(_rows(ts, hw), _rows(ts, hw)), compiler_params=_params("parallel"))(qf, kf, z, *tables)


def _rope_qk_bwd(dq, dk, tables, dz, name):
    s = dq.shape[0]
    ts = min(s, 256)
    hw = N_HEADS * HEAD_PAD

    def body(dq_ref, dk_ref, c_ref, s1_ref, s2_ref, _, dqf_ref, dkf_ref, dkr_ref):
        c, s1, s2 = c_ref[...], s1_ref[...], s2_ref[...]
        ksum = jnp.zeros((ts, HEAD_PAD), F32)
        for h in range(N_HEADS):
            sl = slice(h * HEAD_PAD, (h + 1) * HEAD_PAD)
            dqf_ref[:, sl] = _rope_t(dq_ref[:, sl], c, s1, s2).astype(BF16)
            dkh = dk_ref[:, sl]
            dkf_ref[:, sl] = dkh.astype(BF16)
            ksum = ksum + dkh
        lane = lax.broadcasted_iota(jnp.int32, (ts, HEAD_PAD), 1)
        in_rope = (lane >= QK_NOPE) & (lane < QK_NOPE + QK_ROPE)
        dkr_ref[...] = jnp.where(in_rope, _rope_t(ksum, c, s1, s2), 0.0).astype(BF16)

    tab = _rows(ts, LANES)
    into = _into(dz, 5, 2)
    return pl.pallas_call(
        body, name=name,
        out_shape=(jax.ShapeDtypeStruct((s, hw), BF16), jax.ShapeDtypeStruct((s, hw), BF16), into["out_shape"]),
        grid=(s // ts,), in_specs=[_rows(ts, hw), _rows(ts, hw), tab, tab, tab] + into["in_specs"],
        out_specs=(_rows(ts, hw), _rows(ts, hw), _rows(ts, *ZC_KR)), input_output_aliases=into["input_output_aliases"],
        compiler_params=_params("parallel"))(dq, dk, *tables, dz)


def _attn_tile(s):
    return min(s, 512)


def _raw_scores(q, k, masked, row0=0):
    sc = lax.dot_general(q, k, (((1,), (1,)), ((), ())), preferred_element_type=F32)
    if masked:
        rows = row0 + lax.broadcasted_iota(jnp.int32, sc.shape, 0)
        cols = lax.broadcasted_iota(jnp.int32, sc.shape, 1)
        sc = jnp.where(cols <= rows, sc, -jnp.inf)
    return sc


def _ride_hooks(ride, refs, n_in, n_out, grid):
    if ride is None:
        return refs, lambda: None, lambda: None
    n = len(ride.arrays)
    own = refs[:n_in] + refs[n_in + n:n_in + n + n_out]
    ins, outs, sems = refs[n_in:n_in + n], refs[n_in + n + n_out:n_in + 2 * n + n_out], refs[n_in + 2 * n + n_out:]
    at_first = functools.reduce(lambda a, b: a & b, [pl.program_id(ax) == 0 for ax in range(len(grid))])
    at_last = functools.reduce(lambda a, b: a & b, [pl.program_id(ax) == g - 1 for ax, g in enumerate(grid)])
    return own, lambda: pl.when(at_first)(lambda: ride.start(ins, outs, sems)), \
        lambda: pl.when(at_last)(lambda: ride.finish(ins, outs, sems))


def _ride_call(ride, body, name, out_shape, grid, in_specs, out_specs, semantics, operands):
    n = 0 if ride is None else len(ride.arrays)
    res = pl.pallas_call(
        body, name=name, out_shape=tuple(out_shape) + (tuple(ride.out_shape) if n else ()), grid=grid,
        in_specs=list(in_specs) + [ANY] * n, out_specs=tuple(out_specs) + (ANY,) * n,
        scratch_shapes=list(ride.scratch) if n else [],
        compiler_params=_params(*(("arbitrary",) * len(grid) if n else semantics)))(*operands, *(ride.arrays if n else ()))
    return res[:len(out_shape)], list(res[len(out_shape):])


def _flash_fwd(q, k, v, name, ride=None):
    s = q.shape[0]
    t = _attn_tile(s)
    c2 = ATTN_SCALE * LOG2E
    grid = (N_HEADS, s // t)

    def body(*refs):
        (q_ref, k_ref, v_ref, o_ref, lse_ref), start, finish = _ride_hooks(ride, refs, 3, 2, grid)
        start()
        i = pl.program_id(1)
        qv = q_ref[...]

        def chunk(j, carry, masked):
            m_old, l_old, acc = carry
            at = pl.ds(pl.multiple_of(j * t, t), t)
            sc = _raw_scores(qv, k_ref[at, :], masked)
            m_new = jnp.maximum(m_old, jnp.max(sc, axis=-1, keepdims=True))
            p = jnp.exp2((sc - m_new) * c2)
            alpha = jnp.exp2((m_old - m_new) * c2)
            l_new = alpha * l_old + jnp.sum(p, axis=-1, keepdims=True)
            acc = alpha * acc + jnp.dot(p.astype(BF16), v_ref[at, :], preferred_element_type=F32)
            return m_new, l_new, acc

        init = (jnp.full((t, 1), -jnp.inf, F32), jnp.zeros((t, 1), F32), jnp.zeros((t, HEAD_PAD), F32))
        carry = lax.fori_loop(0, i, lambda j, cr: chunk(j, cr, False), init)
        m_fin, l_fin, acc = chunk(i, carry, True)
        o_ref[...] = (acc / l_fin).astype(o_ref.dtype)
        lse_ref[...] = jnp.broadcast_to(m_fin * ATTN_SCALE + jnp.log(l_fin), (t, HEAD_PAD))
        finish()

    qo = pl.BlockSpec((t, HEAD_PAD), lambda h, i: (i, h))
    whole = pl.BlockSpec((s, HEAD_PAD), lambda h, i: (0, h))
    return _ride_call(
        ride, body, name, (jax.ShapeDtypeStruct(q.shape, BF16), jax.ShapeDtypeStruct(q.shape, F32)), grid,
        [qo, whole, whole], (qo, qo), ("parallel", "parallel"), (q, k, v))


def _attn_delta(do, o, name):
    s = o.shape[0]
    t = _attn_tile(s)

    def body(do_ref, o_ref, delta_ref, dob_ref):
        for h in range(N_HEADS):
            sl = slice(h * HEAD_PAD, (h + 1) * HEAD_PAD)
            dov = do_ref[:, sl]
            delta_ref[:, sl] = jnp.broadcast_to(jnp.sum(dov * o_ref[:, sl].astype(F32), axis=-1, keepdims=True),
                                                (t, HEAD_PAD))
            dob_ref[:, sl] = dov.astype(BF16)

    blk = _rows(t, N_HEADS * HEAD_PAD)
    return pl.pallas_call(
        body, name=name, out_shape=(jax.ShapeDtypeStruct(o.shape, F32), jax.ShapeDtypeStruct(o.shape, BF16)),
        grid=(s // t,), in_specs=[blk, blk], out_specs=(blk, blk), compiler_params=_params("parallel"))(do, o)


def _flash_bwd(q, k, v, do, lse, delta, name, ride=None):
    s = q.shape[0]
    t = _attn_tile(s)
    nt = s // t
    c2 = ATTN_SCALE * LOG2E
    grid = (N_HEADS, nt)

    def body(*refs):
        (q_ref, k_ref, v_ref, do_ref, lse_ref, delta_ref, dq_ref, dk_ref, dv_ref), start, finish = _ride_hooks(
            ride, refs, 6, 3, grid)
        start()
        j = pl.program_id(1)
        kv, vv = k_ref[...], v_ref[...]

        @pl.when(j == 0)
        def _():
            dq_ref[...] = jnp.zeros_like(dq_ref)

        def chunk(i, carry, masked):
            dk_acc, dv_acc = carry
            at = pl.ds(pl.multiple_of(i * t, t), t)
            qi, doi = q_ref[at, :], do_ref[at, :]
            sc = _raw_scores(qi, kv, masked)
            p = jnp.exp2(sc * c2 - lse_ref[at, pl.ds(0, 1)] * LOG2E)
            dp = lax.dot_general(doi, vv, (((1,), (1,)), ((), ())), preferred_element_type=F32)
            ds = (p * (dp - delta_ref[at, pl.ds(0, 1)])).astype(BF16)
            dv_acc = dv_acc + lax.dot_general(p.astype(BF16), doi, (((0,), (0,)), ((), ())), preferred_element_type=F32)
            dk_acc = dk_acc + lax.dot_general(ds, qi, (((0,), (0,)), ((), ())), preferred_element_type=F32)
            dq_ref[at, :] += jnp.dot(ds, kv, preferred_element_type=F32) * ATTN_SCALE
            return dk_acc, dv_acc

        zero = jnp.zeros((t, HEAD_PAD), F32)
        carry = chunk(j, (zero, zero), True)
        dk_acc, dv_acc = lax.fori_loop(j + 1, nt, lambda i, cr: chunk(i, cr, False), carry)
        dk_ref[...] = dk_acc * ATTN_SCALE
        dv_ref[...] = dv_acc.astype(BF16)
        finish()

    blk = pl.BlockSpec((t, HEAD_PAD), lambda h, j: (j, h))
    whole = pl.BlockSpec((s, HEAD_PAD), lambda h, j: (0, h))
    return _ride_call(
        ride, body, name, (jax.ShapeDtypeStruct(q.shape, F32), jax.ShapeDtypeStruct(q.shape, F32),
                           jax.ShapeDtypeStruct(q.shape, BF16)), grid,
        [whole, blk, blk, whole, whole, whole], (whole, blk, blk), ("parallel", "arbitrary"), (q, k, v, do, lse, delta))


def _conv_tile(s):
    return min(s, 256)


def _halo_before(t, width, cidx):
    per = t // CONV_HALO
    return pl.BlockSpec((CONV_HALO, width), lambda i: (jnp.maximum(i * per - 1, 0), cidx))


def _halo_after(t, width, cidx, n_tiles):
    per = t // CONV_HALO
    last = n_tiles * per - 1
    return pl.BlockSpec((CONV_HALO, width), lambda i: (jnp.minimum((i + 1) * per, last), cidx))


def _fill_glu(hbuf, ap_ref, gp_ref, a_ref, g_ref, t):
    first = pl.program_id(0) == 0
    hbuf[pl.ds(0, CONV_HALO), :] = jnp.where(first, 0.0, ap_ref[...].astype(F32) * _sigmoid(gp_ref[...].astype(F32)))
    hbuf[pl.ds(CONV_HALO, t), :] = a_ref[...].astype(F32) * _sigmoid(g_ref[...].astype(F32))


def _phase_copies(dst, src, t):
    n = t + CONV_HALO - SUBLANES
    for s in range(1, SUBLANES):
        dst[s, pl.ds(0, n), :] = src[pl.ds(s, n), :]


def _window(phases, src, k, t):
    if k % SUBLANES == 0:
        return src[pl.ds(k, t), :]
    return phases[k % SUBLANES, pl.ds(k - k % SUBLANES, t), :]


def _layer_norm_parts(co):
    mu = jnp.mean(co, axis=-1, keepdims=True)
    xc = co - mu
    rstd = lax.rsqrt(jnp.mean(xc * xc, axis=-1, keepdims=True) + EPS)
    return xc * rstd, rstd


def _conv_fwd(z, conv_w, conv_b, ln_g, ln_b, name):
    s = z.shape[0]
    t = _conv_tile(s)
    off = CONV_HALO - (CONV_W - 1)

    def body(ap_ref, gp_ref, a_ref, g_ref, w_ref, b_ref, lg_ref, lb_ref, hc_ref, co_ref, hbuf, hph):
        _fill_glu(hbuf, ap_ref, gp_ref, a_ref, g_ref, t)
        _phase_copies(hph, hbuf, t)
        acc = jnp.zeros((t, CONV_C), F32) + b_ref[...]
        for j in range(CONV_W):
            acc = acc + _window(hph, hbuf, off + j, t) * w_ref[pl.ds(j, 1), :]
        co_ref[...] = acc
        xh, _ = _layer_norm_parts(acc)
        y = xh * lg_ref[...] + lb_ref[...]
        hc_ref[...] = (y * _sigmoid(y)).astype(BF16)

    vec = _fixed((1, CONV_C))
    return pl.pallas_call(
        body, name=name, out_shape=(jax.ShapeDtypeStruct((s, CONV_C), BF16), jax.ShapeDtypeStruct((s, CONV_C), F32)),
        grid=(s // t,),
        in_specs=[_halo_before(t, *ZC_CONV_A), _halo_before(t, *ZC_CONV_G), _rows(t, *ZC_CONV_A), _rows(t, *ZC_CONV_G),
                  _fixed((CONV_HALO, CONV_C)), vec, vec, vec],
        out_specs=(_rows(t, CONV_C), _rows(t, CONV_C)),
        scratch_shapes=[pltpu.VMEM((t + CONV_HALO, CONV_C), F32), pltpu.VMEM((SUBLANES, t + CONV_HALO, CONV_C), F32)],
        compiler_params=_params("parallel"))(z, z, z, z, conv_w, conv_b.reshape(1, -1), ln_g.reshape(1, -1),
                                             ln_b.reshape(1, -1))


def _conv_bwd_norm(dhc, co, ln_g, ln_b, name):
    s = co.shape[0]
    t = min(s, 512)

    def body(dhc_ref, co_ref, lg_ref, lb_ref, dco_ref, dg_ref, db_ref, dcb_ref):
        xh, rstd = _layer_norm_parts(co_ref[...])
        y = xh * lg_ref[...] + lb_ref[...]
        sg = _sigmoid(y)
        dy = dhc_ref[...] * (sg * (1.0 + y * (1.0 - sg)))
        dxh = dy * lg_ref[...]
        dco = rstd * (dxh - jnp.mean(dxh, axis=-1, keepdims=True) - xh * jnp.mean(dxh * xh, axis=-1, keepdims=True))
        dco_ref[...] = dco

        @pl.when(pl.program_id(0) == 0)
        def _():
            dg_ref[...] = jnp.zeros_like(dg_ref)
            db_ref[...] = jnp.zeros_like(db_ref)
            dcb_ref[...] = jnp.zeros_like(dcb_ref)

        dg_ref[...] += jnp.sum(dy * xh, axis=0, keepdims=True)
        db_ref[...] += jnp.sum(dy, axis=0, keepdims=True)
        dcb_ref[...] += jnp.sum(dco, axis=0, keepdims=True)

    vec = _fixed((1, CONV_C))
    one = jax.ShapeDtypeStruct((1, CONV_C), F32)
    dco, dg, db, dcb = pl.pallas_call(
        body, name=name, out_shape=(jax.ShapeDtypeStruct((s, CONV_C), F32), one, one, one), grid=(s // t,),
        in_specs=[_rows(t, CONV_C), _rows(t, CONV_C), vec, vec], out_specs=(_rows(t, CONV_C), vec, vec, vec),
        compiler_params=_params("arbitrary"))(dhc, co, ln_g.reshape(1, -1), ln_b.reshape(1, -1))
    return dco, dg.reshape(-1), db.reshape(-1), dcb.reshape(-1)


def _conv_bwd_taps(dco, z, conv_w, dz, name):
    s = z.shape[0]
    t = _conv_tile(s)
    nt = s // t
    off = CONV_HALO - (CONV_W - 1)

    def body(ap_ref, gp_ref, a_ref, g_ref, d_ref, dn_ref, w_ref, _, du_ref, dw_ref, hbuf, dbuf, hph, dph):
        i = pl.program_id(0)
        _fill_glu(hbuf, ap_ref, gp_ref, a_ref, g_ref, t)
        dbuf[pl.ds(0, t), :] = d_ref[...]
        dbuf[pl.ds(t, CONV_HALO), :] = jnp.where(i == nt - 1, 0.0, dn_ref[...])
        _phase_copies(hph, hbuf, t)
        _phase_copies(dph, dbuf, t)

        @pl.when(i == 0)
        def _():
            dw_ref[...] = jnp.zeros_like(dw_ref)

        dcur = d_ref[...]
        dh = jnp.zeros((t, CONV_C), F32)
        for j in range(CONV_W):
            dh = dh + _window(dph, dbuf, CONV_W - 1 - j, t) * w_ref[pl.ds(j, 1), :]
            dw_ref[pl.ds(j, 1), :] += jnp.sum(dcur * _window(hph, hbuf, off + j, t), axis=0, keepdims=True)
        a, sg = a_ref[...].astype(F32), _sigmoid(g_ref[...].astype(F32))
        du_ref[:, pl.ds(0, CONV_C)] = (dh * sg).astype(BF16)
        du_ref[:, pl.ds(CONV_C, CONV_C)] = (dh * a * sg * (1.0 - sg)).astype(BF16)

    into = _into(dz, 7, 0)
    return pl.pallas_call(
        body, name=name, out_shape=(into["out_shape"], jax.ShapeDtypeStruct((CONV_HALO, CONV_C), F32)), grid=(nt,),
        in_specs=[_halo_before(t, *ZC_CONV_A), _halo_before(t, *ZC_CONV_G), _rows(t, *ZC_CONV_A), _rows(t, *ZC_CONV_G),
                  _rows(t, CONV_C), _halo_after(t, CONV_C, 0, nt), _fixed((CONV_HALO, CONV_C))] + into["in_specs"],
        out_specs=(_rows(t, *ZC_CONV), _fixed((CONV_HALO, CONV_C))), input_output_aliases=into["input_output_aliases"],
        scratch_shapes=[pltpu.VMEM((t + CONV_HALO, CONV_C), F32), pltpu.VMEM((t + CONV_HALO, CONV_C), F32),
                        pltpu.VMEM((SUBLANES, t + CONV_HALO, CONV_C), F32),
                        pltpu.VMEM((SUBLANES, t + CONV_HALO, CONV_C), F32)],
        compiler_params=_params("arbitrary"))(z, z, z, z, dco, dco, conv_w, dz)


def _pool_tile(s):
    return min(s, 512)


def _pool_counts(row0, n, window):
    rows = row0 + lax.broadcasted_iota(jnp.int32, (n, POOL_GD), 0)
    return jnp.minimum(rows + 1, window).astype(F32)


def _pool_diff(ubuf, gi, window, row0, t):
    lanes = pl.ds(gi * POOL_GD, POOL_GD)
    tot = ubuf[pl.ds(CONV_HALO, t), lanes]
    cur = tot
    for back in range(1, window):
        tot = tot + ubuf[pl.ds(CONV_HALO - back, t), lanes]
    return tot / _pool_counts(row0, t, window) - cur


def _pool_fwd(z, pool_w, pool_scale, name):
    s = z.shape[0]
    t = _pool_tile(s)

    def body(up_ref, u_ref, w_ref, sc_ref, m_ref, ubuf):
        i = pl.program_id(0)
        ubuf[pl.ds(0, CONV_HALO), :] = jnp.where(i == 0, 0.0, up_ref[...].astype(F32))
        ubuf[pl.ds(CONV_HALO, t), :] = u_ref[...].astype(F32)
        for gi, window in enumerate(POOL_WINDOWS):
            d = _pool_diff(ubuf, gi, window, i * t, t)
            mm = jnp.dot(d.astype(BF16), w_ref[gi].astype(BF16), preferred_element_type=F32)
            lanes = pl.ds(gi * POOL_GD, POOL_GD)
            m_ref[:, lanes] = (mm * sc_ref[:, lanes]).astype(BF16)

    return pl.pallas_call(
        body, name=name, out_shape=jax.ShapeDtypeStruct((s, POOL_C), BF16), grid=(s // t,),
        in_specs=[_halo_before(t, *ZC_POOL), _rows(t, *ZC_POOL), _fixed((POOL_G, POOL_GD, POOL_GD)), _fixed((1, POOL_C))],
        out_specs=_rows(t, POOL_C), scratch_shapes=[pltpu.VMEM((t + CONV_HALO, POOL_C), F32)],
        compiler_params=_params("parallel"))(z, z, pool_w, pool_scale.reshape(1, -1))


def _pool_bwd(dm, z, pool_w, pool_scale, dz, name):
    s = z.shape[0]
    t = _pool_tile(s)
    nt = s // t

    def body(up_ref, u_ref, dm_ref, dmn_ref, w_ref, sc_ref, _, du_ref, dw_ref, dsc_ref, ubuf, ebuf):
        i = pl.program_id(0)
        ubuf[pl.ds(0, CONV_HALO), :] = jnp.where(i == 0, 0.0, up_ref[...].astype(F32))
        ubuf[pl.ds(CONV_HALO, t), :] = u_ref[...].astype(F32)

        @pl.when(i == 0)
        def _():
            dw_ref[...] = jnp.zeros_like(dw_ref)
            dsc_ref[...] = jnp.zeros_like(dsc_ref)

        dm_next = jnp.where(i == nt - 1, 0.0, dmn_ref[...])
        for gi, window in enumerate(POOL_WINDOWS):
            lanes = pl.ds(gi * POOL_GD, POOL_GD)
            wb = w_ref[gi].astype(BF16)
            scale = sc_ref[:, lanes]
            d = _pool_diff(ubuf, gi, window, i * t, t).astype(BF16)
            mm = jnp.dot(d, wb, preferred_element_type=F32)
            dmv = dm_ref[:, lanes]
            dsc_ref[:, lanes] += jnp.sum(dmv * mm, axis=0, keepdims=True)
            dmm = (dmv * scale).astype(BF16)
            dw_ref[gi] += lax.dot_general(d, dmm, (((0,), (0,)), ((), ())), preferred_element_type=F32)
            dd = lax.dot_general(dmm, wb, (((1,), (1,)), ((), ())), preferred_element_type=F32)
            dd_next = lax.dot_general((dm_next[:, gi * POOL_GD:(gi + 1) * POOL_GD] * scale).astype(BF16), wb,
                                      (((1,), (1,)), ((), ())), preferred_element_type=F32)
            ebuf[pl.ds(0, t), lanes] = dd / _pool_counts(i * t, t, window)
            ebuf[pl.ds(t, CONV_HALO), lanes] = dd_next / _pool_counts((i + 1) * t, CONV_HALO, window)
            du = -dd
            for ahead in range(window):
                du = du + ebuf[pl.ds(ahead, t), lanes]
            du_ref[:, lanes] = du.astype(BF16)

    into = _into(dz, 6, 0)
    du, dw, dsc = pl.pallas_call(
        body, name=name,
        out_shape=(into["out_shape"], jax.ShapeDtypeStruct((POOL_G, POOL_GD, POOL_GD), F32),
                   jax.ShapeDtypeStruct((1, POOL_C), F32)), grid=(nt,),
        in_specs=[_halo_before(t, *ZC_POOL), _rows(t, *ZC_POOL), _rows(t, POOL_C), _halo_after(t, POOL_C, 0, nt),
                  _fixed((POOL_G, POOL_GD, POOL_GD)), _fixed((1, POOL_C))] + into["in_specs"],
        out_specs=(_rows(t, *ZC_POOL), _fixed((POOL_G, POOL_GD, POOL_GD)), _fixed((1, POOL_C))),
        input_output_aliases=into["input_output_aliases"],
        scratch_shapes=[pltpu.VMEM((t + CONV_HALO, POOL_C), F32), pltpu.VMEM((t + CONV_HALO, POOL_C), F32)],
        compiler_params=_params("arbitrary"))(z, z, dm, dm, pool_w, pool_scale.reshape(1, -1), dz)
    return du, dw, dsc.reshape(-1)


def _gate_specs(ts):
    width, first = ZC_GATE
    return [_rows(ts, width, first + b) for b in range(3)]


def _merge_fwd(z, ys, name):
    s = z.shape[0]
    ts = min(s, 256)

    def body(g0, g1, g2, y0, y1, y2, o_ref):
        o_ref[...] = sum(_sigmoid(g[...].astype(F32)) * y[...].astype(F32)
                         for g, y in ((g0, y0), (g1, y1), (g2, y2))).astype(BF16)

    return pl.pallas_call(
        body, name=name, out_shape=jax.ShapeDtypeStruct((s, D_MODEL), BF16), grid=(s // ts,),
        in_specs=_gate_specs(ts) + [_rows(ts, D_MODEL)] * 3, out_specs=_rows(ts, D_MODEL),
        compiler_params=_params("parallel"))(z, z, z, *ys)


def _merge_bwd(z, ys, dmerged, name):
    s = z.shape[0]
    ts = min(s, 256)

    def body(g0, g1, g2, y0, y1, y2, dm_ref, dy0, dy1, dy2, dz_ref):
        dmv = dm_ref[...]
        for b, (g_ref, y_ref, dy_ref) in enumerate(((g0, y0, dy0), (g1, y1, dy1), (g2, y2, dy2))):
            sg = _sigmoid(g_ref[...].astype(F32))
            dy_ref[...] = (dmv * sg).astype(BF16)
            dz_ref[:, pl.ds(b * D_MODEL, D_MODEL)] = (dmv * y_ref[...].astype(F32) * sg * (1.0 - sg)).astype(BF16)

    out = jax.ShapeDtypeStruct((s, D_MODEL), BF16)
    return pl.pallas_call(
        body, name=name, out_shape=(out,) * 3 + (jax.ShapeDtypeStruct((s, Z_W), BF16),), grid=(s // ts,),
        in_specs=_gate_specs(ts) + [_rows(ts, D_MODEL)] * 4,
        out_specs=(_rows(ts, D_MODEL),) * 3 + (_rows(ts, *ZC_GATES),),
        compiler_params=_params("parallel"))(z, z, z, *ys, dmerged)


def _ffn_up_fwd(h, w_gate, w_up, name):
    s, d = h.shape
    nb = w_gate.shape[2]
    f = N_DEV * nb
    tm, n_blk = min(s, 1024), 2
    tn = n_blk * nb
    blk = pl.BlockSpec((tm, tn), lambda i, j: (i, j))
    wspec = pl.BlockSpec((n_blk, d, nb), lambda i, j: (j, 0, 0))

    def body(h_ref, wg_ref, wu_ref, hg_ref, hu_ref, act_ref):
        hv = h_ref[...]
        g = jnp.dot(hv, jnp.concatenate([wg_ref[c] for c in range(n_blk)], axis=1), preferred_element_type=F32)
        u = jnp.dot(hv, jnp.concatenate([wu_ref[c] for c in range(n_blk)], axis=1), preferred_element_type=F32)
        hg_ref[...] = g.astype(hg_ref.dtype)
        hu_ref[...] = u.astype(hu_ref.dtype)
        act_ref[...] = (g * _sigmoid(g) * u).astype(BF16)

    return pl.pallas_call(
        body, name=name,
        out_shape=(jax.ShapeDtypeStruct((s, f), BF16),) * 3,
        grid=(s // tm, f // tn), in_specs=[pl.BlockSpec((tm, d), lambda i, j: (i, 0)), wspec, wspec],
        out_specs=(blk, blk, blk), compiler_params=_params("parallel", "parallel"))(h, w_gate, w_up)


def _ffn_down_bwd(dfo, w_down, hg, hu, name):
    s, d = dfo.shape
    f = w_down.shape[0]
    tm, tn = min(s, 1024), _tile(f, 1024)
    blk = pl.BlockSpec((tm, tn), lambda i, j: (i, j))

    def body(d_ref, w_ref, g_ref, u_ref, dg_ref, du_ref):
        dact = lax.dot_general(d_ref[...], w_ref[...], (((1,), (1,)), ((), ())), preferred_element_type=F32)
        g = g_ref[...].astype(F32)
        sg = _sigmoid(g)
        dg_ref[...] = (dact * u_ref[...].astype(F32) * (sg * (1.0 + g * (1.0 - sg)))).astype(BF16)
        du_ref[...] = (dact * g * sg).astype(BF16)

    out = jax.ShapeDtypeStruct((s, f), BF16)
    return pl.pallas_call(
        body, name=name, out_shape=(out, out), grid=(s // tm, f // tn),
        in_specs=[pl.BlockSpec((tm, d), lambda i, j: (i, 0)), pl.BlockSpec((tn, d), lambda i, j: (j, 0)), blk, blk],
        out_specs=(blk, blk), compiler_params=_params("parallel", "parallel"))(dfo, w_down, hg, hu)


def _loss_grad(y, target, name):
    s, d = y.shape
    ts = min(s, 512)

    def body(y_ref, t_ref, dy_ref, sq_ref):
        e = y_ref[...] - t_ref[...]
        dy_ref[...] = e / d

        @pl.when(pl.program_id(0) == 0)
        def _():
            sq_ref[...] = jnp.zeros_like(sq_ref)

        sq_ref[...] += jnp.sum(e * e, axis=0, keepdims=True)

    return pl.pallas_call(
        body, name=name, out_shape=(jax.ShapeDtypeStruct((s, d), F32), jax.ShapeDtypeStruct((1, d), F32)),
        grid=(s // ts,), in_specs=[_rows(ts, d), _rows(ts, d)], out_specs=(_rows(ts, d), _fixed((1, d))),
        compiler_params=_params("arbitrary"))(y, target)


def _adamw(w, g, m, v, name):
    shape = w.shape
    cols = shape[-1]
    keep3 = w.ndim == 3 and shape[1] < SUBLANES
    view = shape if keep3 else (math.prod(shape[:-1]), cols)
    rows = view[0]
    if keep3:
        cap = max(1, (1 << 20) // (SUBLANES * cols * 4))
        tr = max(t for t in range(1, cap + 1) if rows % t == 0)
    else:
        tr = _row_tile(rows, cols * 4)

    def body(w_ref, g_ref, m_ref, v_ref, d_ref, mo_ref, vo_ref):
        gv = g_ref[...]
        mn = B1 * m_ref[...] + (1.0 - B1) * gv
        vn = B2 * v_ref[...] + (1.0 - B2) * (gv * gv)
        m_hat = mn / (1.0 - B1 ** STEP)
        v_hat = vn / (1.0 - B2 ** STEP)
        d_ref[...] = -LR * (m_hat / (jnp.sqrt(v_hat) + ADAM_EPS) + WD * w_ref[...])
        mo_ref[...] = mn
        vo_ref[...] = vn

    spec = pl.BlockSpec((tr,) + view[1:], lambda i: (i,) + (0,) * (len(view) - 1))
    out = jax.ShapeDtypeStruct(view, F32)
    res = pl.pallas_call(
        body, name=name, out_shape=(out,) * 3, grid=(rows // tr,), in_specs=[spec] * 4, out_specs=(spec,) * 3,
        compiler_params=_params("parallel"))(*[t.reshape(view) for t in (w, g, m, v)])
    return tuple(r.reshape(shape) for r in res)


LANE_MAJOR = ("w_uq", "w_uk", "w_uv", "w_gate", "w_up")


def _lane_major(name, a):
    if name == "w_in":
        return a.transpose(2, 0, 1)
    if name in LANE_MAJOR:
        return a.transpose(0, 2, 1)
    return a


def _from_lane_major(name, a):
    if name == "w_in":
        return a.transpose(1, 2, 0)
    return _lane_major(name, a)


ANY = pl.BlockSpec(memory_space=pl.ANY)


class _GatherRide:
    def __init__(self, arrays):
        n = len(arrays)
        self.arrays = list(arrays)
        self.out_shape = [jax.ShapeDtypeStruct((N_DEV,) + a.shape, a.dtype) for a in arrays]
        self.scratch = [pltpu.SemaphoreType.DMA((n, 7)), pltpu.SemaphoreType.DMA((n, 7)), pltpu.SemaphoreType.DMA((n,))]

    def _copies(self, ins, outs, sems):
        send_sems, recv_sems, local_sems = sems
        n = len(self.arrays)
        x, y, c = lax.axis_index("x"), lax.axis_index("y"), lax.axis_index("c")
        me, sibling = (x, y, c), (x, y, 1 - c)
        chips = [(1 - x, y), (x, 1 - y), (1 - x, 1 - y)]

        def slot(a, px, py, pc):
            return outs[a].at[4 * px + 2 * py + pc]

        def copy(a, k, block, to, src=None):
            return pltpu.make_async_remote_copy(
                src_ref=slot(a, *block) if src is None else src, dst_ref=slot(a, *block), send_sem=send_sems.at[a, k],
                recv_sem=recv_sems.at[a, k], device_id=to, device_id_type=MESH)

        mine = [pltpu.make_async_copy(ins[a], slot(a, *me), local_sems.at[a]) for a in range(n)]
        first = []
        for a in range(n):
            first.append(copy(a, 0, me, sibling, src=ins[a]))
            first += [copy(a, 1 + j, me, (*chip, c), src=ins[a]) for j, chip in enumerate(chips)]
        return n, me, sibling, chips, c, copy, mine, first

    def start(self, ins, outs, sems):
        _, _, _, _, _, _, mine, first = self._copies(ins, outs, sems)
        for cp in mine + first:
            cp.start()

    def finish(self, ins, outs, sems):
        n, me, sibling, chips, c, copy, mine, first = self._copies(ins, outs, sems)
        passed = []
        for j, chip in enumerate(chips):
            for a in range(n):
                copy(a, 1 + j, (*chip, c), me).wait_recv()
                passed.append(copy(a, 4 + j, (*chip, c), sibling))
                passed[-1].start()
        for a in range(n):
            copy(a, 0, sibling, me).wait_recv()
            for j, chip in enumerate(chips):
                copy(a, 4 + j, (*chip, 1 - c), me).wait_recv()
        for cp in first + passed:
            cp.wait_send()
        for cp in mine:
            cp.wait()


class _ReduceRide:
    def __init__(self, arrays):
        n = len(arrays)
        self.arrays = list(arrays)
        self.out_shape = [jax.ShapeDtypeStruct(a.shape, a.dtype) for a in arrays]
        self.scratch = [pltpu.SemaphoreType.DMA((n, 7)), pltpu.SemaphoreType.DMA((n, 7)), pltpu.SemaphoreType.DMA((n,))]

    def _copies(self, ins, outs, sems):
        send_sems, recv_sems, local_sems = sems
        n = len(self.arrays)
        x, y, c = lax.axis_index("x"), lax.axis_index("y"), lax.axis_index("c")
        mine = [pltpu.make_async_copy(ins[a].at[4 * x + 2 * y + c], outs[a].at[0], local_sems.at[a]) for a in range(n)]
        copies = []
        for a in range(n):
            for k in range(1, N_DEV):
                px = 1 - x if k & 4 else x
                py = 1 - y if k & 2 else y
                pc = 1 - c if k & 1 else c
                copies.append(pltpu.make_async_remote_copy(
                    src_ref=ins[a].at[4 * px + 2 * py + pc], dst_ref=outs[a].at[k], send_sem=send_sems.at[a, k - 1],
                    recv_sem=recv_sems.at[a, k - 1], device_id=(px, py, pc), device_id_type=MESH))
        return mine, copies

    def start(self, ins, outs, sems):
        mine, copies = self._copies(ins, outs, sems)
        for cp in mine + copies:
            cp.start()

    def finish(self, ins, outs, sems):
        mine, copies = self._copies(ins, outs, sems)
        for cp in copies + mine:
            cp.wait()


def _run_ride(ride, name):
    n = len(ride.arrays)

    def body(*refs):
        ins, outs, sems = refs[:n], refs[n:2 * n], refs[2 * n:]
        ride.start(ins, outs, sems)
        ride.finish(ins, outs, sems)

    return pl.pallas_call(body, name=name, out_shape=ride.out_shape, in_specs=[ANY] * n, out_specs=[ANY] * n,
                          scratch_shapes=ride.scratch)(*ride.arrays)


def _all_gather(arrays, name):
    return _run_ride(_GatherRide(arrays), name)


def _swap_with_sibling(arrays, name):
    n = len(arrays)

    def body(*refs):
        ins, outs = refs[:n], refs[n:2 * n]
        send_sems, recv_sems = refs[2 * n:]
        x, y, c = lax.axis_index("x"), lax.axis_index("y"), lax.axis_index("c")
        copies = [pltpu.make_async_remote_copy(
            src_ref=ins[a].at[1 - c], dst_ref=outs[a], send_sem=send_sems.at[a], recv_sem=recv_sems.at[a],
            device_id=(x, y, 1 - c), device_id_type=MESH) for a in range(n)]
        for cp in copies:
            cp.start()
        for cp in copies:
            cp.wait()

    return pl.pallas_call(
        body, name=name, out_shape=[jax.ShapeDtypeStruct(a.shape[1:], a.dtype) for a in arrays],
        in_specs=[ANY] * n, out_specs=[ANY] * n,
        scratch_shapes=[pltpu.SemaphoreType.DMA((n,)), pltpu.SemaphoreType.DMA((n,))])(*arrays)


class _ChipExchangeRide:
    def __init__(self, arrays):
        n = len(arrays)
        self.arrays = list(arrays)
        self.out_shape = [jax.ShapeDtypeStruct(a.shape, a.dtype) for a in arrays]
        self.scratch = [pltpu.SemaphoreType.DMA((n, 3)), pltpu.SemaphoreType.DMA((n, 3)), pltpu.SemaphoreType.DMA((n,))]

    def _copies(self, ins, outs, sems):
        send_sems, recv_sems, local_sems = sems
        n = len(self.arrays)
        x, y, c = lax.axis_index("x"), lax.axis_index("y"), lax.axis_index("c")
        partners = [(x, 1 - y), (1 - x, y), (1 - x, 1 - y)]
        mine = [pltpu.make_async_copy(ins[a].at[2 * x + y], outs[a].at[0], local_sems.at[a]) for a in range(n)]
        copies = [pltpu.make_async_remote_copy(
            src_ref=ins[a].at[2 * px + py], dst_ref=outs[a].at[1 + k], send_sem=send_sems.at[a, k],
            recv_sem=recv_sems.at[a, k], device_id=(px, py, c), device_id_type=MESH)
            for a in range(n) for k, (px, py) in enumerate(partners)]
        return mine, copies

    def start(self, ins, outs, sems):
        mine, copies = self._copies(ins, outs, sems)
        for cp in mine + copies:
            cp.start()

    def finish(self, ins, outs, sems):
        mine, copies = self._copies(ins, outs, sems)
        for cp in copies + mine:
            cp.wait()


class _Combo:
    def __init__(self, rides):
        self.rides = rides
        self.arrays = [a for r in rides for a in r.arrays]
        self.out_shape = [o for r in rides for o in r.out_shape]
        self.scratch = [sc for r in rides for sc in r.scratch]

    def _parts(self, ins, outs, sems):
        at_a = at_s = 0
        for r in self.rides:
            na, ns = len(r.arrays), len(r.scratch)
            yield r, ins[at_a:at_a + na], outs[at_a:at_a + na], sems[at_s:at_s + ns]
            at_a, at_s = at_a + na, at_s + ns

    def start(self, ins, outs, sems):
        for r, i, o, sm in self._parts(ins, outs, sems):
            r.start(i, o, sm)

    def finish(self, ins, outs, sems):
        for r, i, o, sm in self._parts(ins, outs, sems):
            r.finish(i, o, sm)


def _as_rows(a, lead):
    return a.reshape(a.shape[:lead] + (math.prod(a.shape[lead:-1]), a.shape[-1]))


def _add_pairs(a, b, name):
    a2, b2 = _as_rows(a, 0), _as_rows(b, 0)
    rows, cols = a2.shape
    tr = _row_tile(rows, cols * 4)

    def body(a_ref, b_ref, o_ref):
        o_ref[...] = (a_ref[...].astype(F32) + b_ref[...].astype(F32)).astype(o_ref.dtype)

    spec = _rows(tr, cols)
    out = pl.pallas_call(body, name=name, out_shape=jax.ShapeDtypeStruct(a2.shape, a.dtype), grid=(rows // tr,),
                         in_specs=[spec, spec], out_specs=spec, compiler_params=_params("parallel"))(a2, b2)
    return out.reshape(a.shape)


def _sum_blocks(a, name):
    a3 = _as_rows(a, 1)
    n, rows, cols = a3.shape
    tr = _row_tile(rows, n * cols * 4)

    def body(a_ref, o_ref):
        tot = a_ref[0].astype(F32)
        for k in range(1, n):
            tot = tot + a_ref[k].astype(F32)
        o_ref[...] = tot

    out = pl.pallas_call(body, name=name, out_shape=jax.ShapeDtypeStruct((rows, cols), F32), grid=(rows // tr,),
                         in_specs=[pl.BlockSpec((n, tr, cols), lambda j: (0, j, 0))], out_specs=_rows(tr, cols),
                         compiler_params=_params("parallel"))(a3)
    return out.reshape(a.shape[1:])


MIX_GROUPS = ("w_in", "w_uq", "w_uk", "w_uv", "w_attn_o", "w_conv_o", "w_pool_o", "w_mix_o")
FFN_GROUPS = ("w_gate", "w_up", "w_down")
MIX_EARLY = ("w_attn_o", "w_conv_o", "w_pool_o", "w_mix_o")
MIX_LATE = ("w_in_a", "w_uq", "w_uk", "w_uv", "w_in_b")
MIX_LATE_A = MIX_LATE[:4]


def _pad_axis(a, axis, size):
    pad = [(0, 0)] * a.ndim
    pad[axis] = (0, size - a.shape[axis])
    return jnp.pad(a, pad)


def _local_groups(sh, l):
    out = {n: sh[n][l] for n in BIG}
    for n in ("w_uq", "w_uk", "w_uv"):
        out[n] = _pad_axis(out[n], -1, HEAD_PAD)
    for n in ("w_gate", "w_up"):
        out[n] = _pad_axis(out[n], -1, FF_SHARD_PAD)
    out["w_down"] = _pad_axis(out["w_down"], 0, FF_SHARD_PAD)
    return {n: v.astype(BF16) for n, v in out.items()}


def _arrange_w_in(blocks):
    parts, pos = [], 0
    for ref_lo, ref_hi, at in sorted(W_IN_PIECES, key=lambda p: p[2]):
        if at > pos:
            parts.append(jnp.zeros((blocks.shape[1], at - pos), blocks.dtype))
        for d in range(N_DEV):
            lo, hi = max(ref_lo, d * W_IN_SHARD), min(ref_hi, (d + 1) * W_IN_SHARD)
            if lo < hi:
                parts.append(blocks[d][:, lo - d * W_IN_SHARD:hi - d * W_IN_SHARD])
        pos = at + ref_hi - ref_lo
    if pos < Z_W:
        parts.append(jnp.zeros((blocks.shape[1], Z_W - pos), blocks.dtype))
    return jnp.concatenate(parts, axis=1)


def _w_in_shard(g, d):
    parts = []
    for ref_lo, ref_hi, at in W_IN_PIECES:
        lo, hi = max(ref_lo, d * W_IN_SHARD), min(ref_hi, (d + 1) * W_IN_SHARD)
        if lo < hi:
            parts.append(g[:, at + lo - ref_lo:at + hi - ref_lo])
    return jnp.concatenate(parts, axis=1)


def _mixer_weights(gat):
    w = {n: v for n, v in gat.items() if n != "w_in"}
    attn_o = gat["w_attn_o"].reshape(N_DEV, N_HEADS, V_HEAD, LANES)
    w["w_attn_o"] = _pad_axis(attn_o, 2, HEAD_PAD).reshape(N_DEV, N_HEADS * HEAD_PAD, LANES)
    w["w_mix_o"] = gat["w_mix_o"].reshape(D_MODEL, D_MODEL)
    return w


def _ffn_weights(gat):
    return {"w_gate": gat["w_gate"], "w_up": gat["w_up"], "w_down": gat["w_down"].reshape(D_FF_PAD, D_MODEL)}


def _mixer_grad_groups(gb):
    g = dict(gb)
    for half in ("w_in_a", "w_in_b"):
        if half in gb:
            g[half] = jnp.stack([_w_in_shard(gb[half], d) for d in range(N_DEV)])
    if "w_attn_o" in gb:
        attn_o = gb["w_attn_o"].reshape(N_DEV, N_HEADS, HEAD_PAD, LANES)[:, :, :V_HEAD]
        g["w_attn_o"] = attn_o.reshape(N_DEV, N_HEADS * V_HEAD, LANES)
    if "w_mix_o" in gb:
        g["w_mix_o"] = gb["w_mix_o"].reshape(N_DEV, D_MODEL // N_DEV, D_MODEL)
    return g


def _ffn_grad_groups(gb):
    return {"w_gate": gb["w_gate"], "w_up": gb["w_up"], "w_down": gb["w_down"].reshape(N_DEV, FF_SHARD_PAD, D_MODEL)}


def _grads_from_groups(tot):
    g = {n: v for n, v in tot.items() if n not in ("w_in_a", "w_in_b")}
    g["w_in"] = jnp.concatenate([tot["w_in_a"], tot["w_in_b"]], axis=0)
    g["w_uq"] = tot["w_uq"][:, :QK_NOPE + QK_ROPE]
    g["w_uk"], g["w_uv"] = tot["w_uk"][:, :QK_NOPE], tot["w_uv"][:, :V_HEAD]
    g["w_gate"], g["w_up"] = tot["w_gate"][:, :FF_SHARD], tot["w_up"][:, :FF_SHARD]
    g["w_down"] = tot["w_down"][:FF_SHARD]
    return g


SMALL_GROUPS = (
    (D_MODEL, ("mix_norm_pre", "mix_norm_post", "ffn_norm_pre", "ffn_norm_post")),
    (CONV_C, ("conv_w", "conv_b", "conv_ln_g", "conv_ln_b", "pool_scale")),
    (Q_RANK, ("q_norm",)), (KV_RANK, ("kv_norm",)), (POOL_GD, ("pool_w",)),
)


def _small_rows(name):
    return {"conv_w": CONV_HALO, "pool_w": POOL_G * POOL_GD}.get(name, SUBLANES)


def _small_groups(small):
    out = []
    for width, names in SMALL_GROUPS:
        parts = []
        for l in range(DEPTH):
            for n in names:
                part = small[l][n].reshape(-1, width)
                parts.append(_pad_axis(part, 0, _small_rows(n)))
        out.append(jnp.concatenate(parts, axis=0))
    return out


def _small_from_groups(groups):
    shapes = {"conv_w": (CONV_W, CONV_C), "pool_w": (POOL_G, POOL_GD, POOL_GD)}
    out = {}
    for (width, names), g in zip(SMALL_GROUPS, groups):
        row = 0
        for l in range(DEPTH):
            for n in names:
                rows = _small_rows(n)
                real = {"conv_w": CONV_W, "pool_w": POOL_G * POOL_GD}.get(n, 1)
                out.setdefault(n, []).append(g[row:row + real].reshape(shapes.get(n, (width,))))
                row += rows
    return {n: jnp.stack(v) for n, v in out.items()}


def _mixer_fwd(x, tables, sm, plan, l):
    nm = lambda n: f"{n}_l{l}"
    h = _rms_fwd(x, (D_MODEL, 0), sm["mix_norm_pre"], BF16, nm("mix_pre_norm"))
    w_in, ride = plan.w_in(l), plan.in_proj_ride(l)
    if ride is None:
        z = _matmul(h, w_in, "nn", BF16, nm("in_proj"))
    else:
        z, rode = _matmul(h, w_in, "nn", BF16, nm("in_proj"), ride=ride)
        plan.in_proj_done(l, rode)
    w = dict(plan.mixer_weights(l), w_in=w_in)
    cq = _rms_fwd(z, ZC_Q, sm["q_norm"], BF16, nm("q_norm"))
    ckv = _rms_fwd(z, ZC_KV, sm["kv_norm"], BF16, nm("kv_norm"))
    qf = _matmul(cq, w["w_uq"], "nn", F32, nm("q_up"))
    kf = _matmul(ckv, w["w_uk"], "nn", F32, nm("k_up"))
    v = _matmul(ckv, w["w_uv"], "nn", BF16, nm("v_up"))
    q, k = _rope_qk_fwd(qf, kf, z, tables, nm("rope_qk"))
    (o, lse), rode = _flash_fwd(q, k, v, nm("flash_fwd"), plan.fwd_ride(l))
    plan.fwd_done(l, rode)
    y_attn = _matmul(o, w["w_attn_o"], "nn", BF16, nm("attn_out"))
    hc, co = _conv_fwd(z, sm["conv_w"], sm["conv_b"], sm["conv_ln_g"], sm["conv_ln_b"], nm("conv_fwd"))
    y_conv = _matmul(hc, w["w_conv_o"], "nn", BF16, nm("conv_out"))
    pm = _pool_fwd(z, sm["pool_w"], sm["pool_scale"], nm("pool_fwd"))
    y_pool = _matmul(pm, w["w_pool_o"], "nn", BF16, nm("pool_out"))
    ys = (y_attn, y_conv, y_pool)
    merged = _merge_fwd(z, ys, nm("merge_fwd"))
    mo = _matmul(merged, w["w_mix_o"], "nn", F32, nm("mix_out"))
    x_mid = _rms_fwd(mo, (D_MODEL, 0), sm["mix_norm_post"], F32, nm("mix_post_norm"), res=x)
    saved = dict(x=x, h=h, z=z, cq=cq, ckv=ckv, q=q, k=k, v=v, o=o, lse=lse, hc=hc, co=co, pm=pm, ys=ys, merged=merged,
                 mo=mo)
    return x_mid, saved, w


def _ffn_fwd(x_mid, w, sm, tag):
    nm = lambda n: f"{n}_{tag}"
    h2 = _rms_fwd(x_mid, (D_MODEL, 0), sm["ffn_norm_pre"], BF16, nm("ffn_pre_norm"))
    hg, hu, act = _ffn_up_fwd(h2, w["w_gate"], w["w_up"], nm("ffn_up_fwd"))
    fo = _matmul(act, w["w_down"], "nn", F32, nm("ffn_down"))
    out = _rms_fwd(fo, (D_MODEL, 0), sm["ffn_norm_post"], F32, nm("ffn_post_norm"), res=x_mid)
    saved = dict(x_mid=x_mid, h2=h2, hg=hg, hu=hu, act=act, fo=fo)
    return out, saved


def _ffn_bwd(dout, sv, w, sm, tag):
    nm = lambda n: f"{n}_{tag}"
    gb, gs = {}, {}
    dfo, gs["ffn_norm_post"] = _rms_bwd(sv["fo"], (D_MODEL, 0), sm["ffn_norm_post"], dout, BF16, nm("ffn_post_norm_bwd"))
    gb["w_down"] = _matmul(sv["act"], dfo, "tn", BF16, nm("ffn_down_dw"))
    dhg, dhu = _ffn_down_bwd(dfo, w["w_down"], sv["hg"], sv["hu"], nm("ffn_down_bwd"))
    dh2_g = _matmul(dhg, w["w_gate"], "nt", F32, nm("ffn_gate_dx"))
    dh2 = _matmul(dhu, w["w_up"], "nt", F32, nm("ffn_up_dx"), add=dh2_g)
    gb["w_gate"] = _matmul(sv["h2"], dhg, "tn", BF16, nm("ffn_gate_dw"), blocked=True)
    gb["w_up"] = _matmul(sv["h2"], dhu, "tn", BF16, nm("ffn_up_dw"), blocked=True)
    dmid, gs["ffn_norm_pre"] = _rms_bwd(sv["x_mid"], (D_MODEL, 0), sm["ffn_norm_pre"], dh2, F32, nm("ffn_pre_norm_bwd"),
                                        add=dout)
    return dmid, gb, gs


def _mixer_bwd(dmid, sv, tables, w, sm, plan, l, pack_small):
    nm = lambda n: f"{n}_l{l}"
    gb, gs = {}, {}
    dmo, gs["mix_norm_post"] = _rms_bwd(sv["mo"], (D_MODEL, 0), sm["mix_norm_post"], dmid, BF16, nm("mix_post_norm_bwd"))
    dmerged = _matmul(dmo, w["w_mix_o"], "nt", F32, nm("mix_out_dx"))
    gb["w_mix_o"] = _matmul(sv["merged"], dmo, "tn", BF16, nm("mix_out_dw"))
    dya, dyc, dyp, dz = _merge_bwd(sv["z"], sv["ys"], dmerged, nm("merge_bwd"))
    dpm = _matmul(dyp, w["w_pool_o"], "nt", F32, nm("pool_out_dx"))
    gb["w_pool_o"] = _matmul(sv["pm"], dyp, "tn", BF16, nm("pool_out_dw"), blocked=True)
    dz, gs["pool_w"], gs["pool_scale"] = _pool_bwd(dpm, sv["z"], sm["pool_w"], sm["pool_scale"], dz, nm("pool_bwd"))
    dhc = _matmul(dyc, w["w_conv_o"], "nt", F32, nm("conv_out_dx"))
    gb["w_conv_o"] = _matmul(sv["hc"], dyc, "tn", BF16, nm("conv_out_dw"), blocked=True)
    dco, gs["conv_ln_g"], gs["conv_ln_b"], gs["conv_b"] = _conv_bwd_norm(dhc, sv["co"], sm["conv_ln_g"], sm["conv_ln_b"],
                                                                        nm("conv_bwd_norm"))
    dz, gs["conv_w"] = _conv_bwd_taps(dco, sv["z"], sm["conv_w"], dz, nm("conv_bwd_taps"))
    do = _matmul(dya, w["w_attn_o"], "nt", F32, nm("attn_out_dx"))
    gb["w_attn_o"] = _matmul(sv["o"], dya, "tn", BF16, nm("attn_out_dw"), blocked=True)
    delta, dob = _attn_delta(do, sv["o"], nm("attn_delta"))
    (dq, dk, dv), rode = _flash_bwd(sv["q"], sv["k"], sv["v"], dob, sv["lse"], delta, nm("flash_bwd"),
                                  plan.bwd_ride(l, gb))
    plan.bwd_done(l, rode)
    dqf, dkf, dz = _rope_qk_bwd(dq, dk, tables, dz, nm("rope_qk_bwd"))
    dcq_n = _matmul(dqf, w["w_uq"], "nt", F32, nm("q_up_dx"))
    gb["w_uq"] = _matmul(sv["cq"], dqf, "tn", BF16, nm("q_up_dw"), blocked=True)
    dckv_k = _matmul(dkf, w["w_uk"], "nt", F32, nm("k_up_dx"))
    dckv_n = _matmul(dv, w["w_uv"], "nt", F32, nm("v_up_dx"), add=dckv_k)
    gb["w_uk"] = _matmul(sv["ckv"], dkf, "tn", BF16, nm("k_up_dw"), blocked=True)
    gb["w_uv"] = _matmul(sv["ckv"], dv, "tn", BF16, nm("v_up_dw"), blocked=True)
    dz, gs["q_norm"] = _rms_bwd(sv["z"], ZC_Q, sm["q_norm"], dcq_n, BF16, nm("q_norm_bwd"), dz=dz)
    dz, gs["kv_norm"] = _rms_bwd(sv["z"], ZC_KV, sm["kv_norm"], dckv_n, BF16, nm("kv_norm_bwd"), dz=dz)
    half = D_MODEL // 2
    gb["w_in_a"] = _matmul(sv["h"][:, :half], dz, "tn", BF16, nm("in_proj_dw_a"))
    plan.add_grads(l, "mix", gb)
    ride = plan.tail_ride(l, MIX_LATE_A)
    if ride is None:
        gb["w_in_b"] = _matmul(sv["h"][:, half:], dz, "tn", BF16, nm("in_proj_dw_b"))
    else:
        gb["w_in_b"], rode = _matmul(sv["h"][:, half:], dz, "tn", BF16, nm("in_proj_dw_b"), ride=ride)
        plan.tail_done(l, MIX_LATE_A, rode)
    plan.add_grads(l, "mix", {"w_in_b": gb["w_in_b"]})
    ride, small_gathered = plan.tail_ride(l, ("w_in_b",), pack_small(gs)), []
    if ride is None:
        dh = _matmul(dz, w["w_in"], "nt", F32, nm("in_proj_dx"))
    else:
        dh, rode = _matmul(dz, w["w_in"], "nt", F32, nm("in_proj_dx"), ride=ride)
        small_gathered = plan.tail_done(l, ("w_in_b",), rode)
    dx, gs["mix_norm_pre"] = _rms_bwd(sv["x"], (D_MODEL, 0), sm["mix_norm_pre"], dh, F32, nm("mix_pre_norm_bwd"), add=dmid)
    return dx, gs, small_gathered


def _part_groups(part):
    return {"mix": MIX_GROUPS, "ffn": FFN_GROUPS, "early": MIX_EARLY, "late": MIX_LATE}[part]


class _Plan:
    def __init__(self, shards, conv_w):
        self.local = [_local_groups(shards, l) for l in range(DEPTH)]
        self.conv_w = conv_w
        self.gat, self.send, self.recv = {}, {}, {}

    @staticmethod
    def _riders(l):
        return [(l, "ffn")] + ([(l + 1, "mix")] if l + 1 < DEPTH else [])

    @staticmethod
    def _grad_riders(l):
        return [(l, "ffn"), (l, "early")] + ([(l + 1, "late")] if l + 1 < DEPTH else [])

    def gather_first(self):
        w_in, conv_w = _all_gather([self.local[0]["w_in"], self.conv_w], "gather_w_in_l0")
        self.gat[(0, "mix")] = {"w_in": w_in}
        return conv_w

    def w_in(self, l):
        return _arrange_w_in(self.gat[(l, "mix")]["w_in"])

    def in_proj_ride(self, l):
        return _GatherRide([self.local[0][g] for g in MIX_GROUPS[1:]]) if l == 0 else None

    def in_proj_done(self, l, outs):
        self.gat[(l, "mix")].update(zip(MIX_GROUPS[1:], outs))

    def fwd_ride(self, l):
        return _GatherRide([self.local[ll][g] for ll, part in self._riders(l) for g in _part_groups(part)])

    def fwd_done(self, l, outs):
        outs = list(outs)
        for ll, part in self._riders(l):
            self.gat[(ll, part)] = {g: outs.pop(0) for g in _part_groups(part)}

    def mixer_weights(self, l):
        return _mixer_weights(self.gat[(l, "mix")])

    def ffn_weights(self, l):
        return _ffn_weights(self.gat[(l, "ffn")])

    def add_grads(self, l, part, gb):
        if part == "ffn":
            self.send[(l, "ffn")] = _ffn_grad_groups(gb)
        else:
            self.send.setdefault((l, "late"), {}).update(_mixer_grad_groups({g: gb[g] for g in MIX_LATE if g in gb}))

    def bwd_ride(self, l, gb_early):
        self.send[(l, "early")] = _mixer_grad_groups({g: gb_early[g] for g in MIX_EARLY})
        return _ReduceRide([self.send[(ll, part)][g] for ll, part in self._grad_riders(l) for g in _part_groups(part)])

    def bwd_done(self, l, outs):
        outs = list(outs)
        for ll, part in self._grad_riders(l):
            self.recv[(ll, part)] = {g: outs.pop(0) for g in _part_groups(part)}

    def tail_ride(self, l, groups, small_groups=None):
        if l > 0:
            return None
        rides = [_ReduceRide([self.send[(0, "late")][g] for g in groups])]
        if small_groups is not None:
            rides.append(_GatherRide(small_groups))
        return _Combo(rides)

    def tail_done(self, l, groups, outs):
        self.recv.setdefault((l, "late"), {}).update(zip(groups, outs[:len(groups)]))
        return outs[len(groups):]

    def finish(self):
        layers = []
        for l in range(DEPTH):
            tot = {g: _sum_blocks(a, f"reduce_sum_{g}_l{l}") for part in ("early", "late", "ffn")
                   for g, a in self.recv[(l, part)].items()}
            layers.append(_grads_from_groups(tot))
        return layers


def _local_step(x, positions, target, smalls, plan):
    tables = _rope_tables(positions)
    saved = []
    h = x
    for l in range(DEPTH):
        h, svm, wm = _mixer_fwd(h, tables, smalls[l], plan, l)
        wf = plan.ffn_weights(l)
        h, svf = _ffn_fwd(h, wf, smalls[l], f"l{l}")
        saved.append((svm, svf, wm, wf))
    dy, sq = _loss_grad(h, target, "loss_grad")
    small = [None] * DEPTH
    for l in reversed(range(DEPTH)):
        svm, svf, wm, wf = saved[l]
        dmid, gbf, gsf = _ffn_bwd(dy, svf, wf, smalls[l], f"l{l}")
        plan.add_grads(l, "ffn", gbf)

        def pack_small(gs, l=l, gsf=gsf):
            if l > 0:
                return None
            return _small_groups([{**gsf, **gs, "mix_norm_pre": jnp.zeros((D_MODEL,), F32)}] + small[1:])

        dy, gsm, small_gathered = _mixer_bwd(dmid, svm, tables, wm, smalls[l], plan, l, pack_small)
        small[l] = {**gsf, **gsm}
    return sq, dy, small, small_gathered


def kernel(x, positions, mix_norm_pre, w_in, q_norm, w_uq, kv_norm, w_uk, w_uv, w_attn_o, conv_w, conv_b, conv_ln_g, conv_ln_b, w_conv_o, pool_w, pool_scale, w_pool_o, w_mix_o, mix_norm_post, ffn_norm_pre, w_gate, w_up, w_down, ffn_norm_post, loss_target, m_mix_norm_pre, m_w_in, m_q_norm, m_w_uq, m_kv_norm, m_w_uk, m_w_uv, m_w_attn_o, m_conv_w, m_conv_b, m_conv_ln_g, m_conv_ln_b, m_w_conv_o, m_pool_w, m_pool_scale, m_w_pool_o, m_w_mix_o, m_mix_norm_post, m_ffn_norm_pre, m_w_gate, m_w_up, m_w_down, m_ffn_norm_post, v_mix_norm_pre, v_w_in, v_q_norm, v_w_uq, v_kv_norm, v_w_uk, v_w_uv, v_w_attn_o, v_conv_w, v_conv_b, v_conv_ln_g, v_conv_ln_b, v_w_conv_o, v_pool_w, v_pool_scale, v_w_pool_o, v_w_mix_o, v_mix_norm_post, v_ffn_norm_pre, v_w_gate, v_w_up, v_w_down, v_ffn_norm_post):
    given = dict(locals())
    dev = 4 * lax.axis_index("x") + 2 * lax.axis_index("y") + lax.axis_index("c")

    plan = _Plan({n: given[n] for n in BIG}, conv_w)
    cw = CONV_C // N_DEV
    conv_w_full = plan.gather_first().transpose(1, 2, 0, 3).reshape(DEPTH, CONV_W, CONV_C)
    smalls = []
    for l in range(DEPTH):
        sm = {n: given[n][l] for n in SMALL if n != "conv_w"}
        sm["conv_w"] = _pad_axis(conv_w_full[l], 0, CONV_HALO)
        smalls.append(sm)

    sq, grad_x, small, small_groups = _local_step(x[0], positions[0], loss_target[0], smalls, plan)
    loss = lax.psum(0.5 / D_MODEL * jnp.sum(sq), ("x", "y", "c"))
    per_layer = plan.finish()
    views = {}
    for n in BIG:
        if n == "w_in":
            views[n] = jnp.stack([per_layer[l][n].T for l in range(DEPTH)], axis=1)
        elif n in LANE_MAJOR:
            views[n] = jnp.stack([per_layer[l][n].T for l in range(DEPTH)])
        else:
            views[n] = jnp.stack([per_layer[l][n] for l in range(DEPTH)])
    grads = {n: _from_lane_major(n, views[n]) for n in BIG}

    small_sum = _small_from_groups([_sum_blocks(g, f"sum_small_grads_{i}") for i, g in enumerate(small_groups)])
    last = _pad_axis(small[0]["mix_norm_pre"].reshape(1, D_MODEL), 0, SUBLANES)
    last_sum = _sum_blocks(_all_gather([last], "gather_last_norm_grad")[0], "sum_last_norm_grad")[0]
    small_sum["mix_norm_pre"] = small_sum["mix_norm_pre"].at[0].set(last_sum)
    for n in SMALL:
        grads[n] = small_sum[n]
    grads["conv_w"] = lax.dynamic_slice_in_dim(small_sum["conv_w"], dev * cw, cw, axis=2)

    delta, new_m, new_v = {}, {}, {}
    for n in WEIGHTS:
        g_view = views[n] if n in views else grads[n]
        w_view, m_view, v_view = [_lane_major(n, given[k]) for k in (n, "m_" + n, "v_" + n)]
        res = _adamw(w_view, g_view, m_view, v_view, f"adamw_{n}")
        delta[n], new_m[n], new_v[n] = [_from_lane_major(n, r) for r in res]
    return (loss, grad_x[None], *[grads[n] for n in WEIGHTS], *[delta[n] for n in WEIGHTS],
            *[new_m[n] for n in WEIGHTS], *[new_v[n] for n in WEIGHTS])
```

```python
import functools
import math

import jax
import jax.numpy as jnp
from jax import lax
from jax.experimental import pallas as pl
from jax.experimental.pallas import tpu as pltpu

F32, BF16 = jnp.float32, jnp.bfloat16
MESH = pl.DeviceIdType.MESH

LANES = 128
SUBLANES = 8
VMEM_LIMIT_BYTES = 56 * 1024 * 1024
MATMUL_VMEM_BYTES = 40 * 1024 * 1024

N_DEV = 8
D_MODEL = 1024
DEPTH = 2
N_HEADS = 8
QK_NOPE, QK_ROPE, V_HEAD = 64, 32, 64
HEAD_PAD = LANES
Q_RANK, KV_RANK = 384, 256
ROPE_THETA = 10000.0
CONV_C, CONV_W = 512, 31
CONV_HALO = 32
POOL_WINDOWS = (2, 4, 8, 16)
POOL_C, POOL_G = 512, 4
POOL_GD = POOL_C // POOL_G
D_FF = 2816
FF_SHARD = D_FF // N_DEV
FF_SHARD_PAD = 3 * LANES
D_FF_PAD = N_DEV * FF_SHARD_PAD
W_IN_SHARD = 660
EPS = 1e-6
ATTN_SCALE = 1.0 / math.sqrt(QK_NOPE + QK_ROPE)
LOG2E = 1.4426950408889634
LR, B1, B2, ADAM_EPS, WD, STEP = 0.001, 0.9, 0.999, 1e-08, 0.01, 10

Z_W = 5376
ZC_GATE = (1024, 0)
ZC_GATES = (3072, 0)
ZC_CONV_A = (512, 6)
ZC_CONV_G = (512, 7)
ZC_CONV = (1024, 3)
ZC_POOL = (512, 8)
ZC_Q = (384, 12)
ZC_KR = (128, 39)
ZC_KV = (256, 20)
W_IN_PIECES = ((0, 384, 4608), (384, 640, 5120), (640, 672, 5056), (672, 1696, 3072), (1696, 2208, 4096),
               (2208, 5280, 0))

BIG = ("w_in", "w_uq", "w_uk", "w_uv", "w_attn_o", "w_conv_o", "w_pool_o", "w_mix_o", "w_gate", "w_up", "w_down")
SMALL = ("mix_norm_pre", "q_norm", "kv_norm", "conv_w", "conv_b", "conv_ln_g", "conv_ln_b", "pool_w", "pool_scale",
         "mix_norm_post", "ffn_norm_pre", "ffn_norm_post")
WEIGHTS = ("mix_norm_pre", "w_in", "q_norm", "w_uq", "kv_norm", "w_uk", "w_uv", "w_attn_o", "conv_w", "conv_b",
           "conv_ln_g", "conv_ln_b", "w_conv_o", "pool_w", "pool_scale", "w_pool_o", "w_mix_o", "mix_norm_post",
           "ffn_norm_pre", "w_gate", "w_up", "w_down", "ffn_norm_post")


def _params(*semantics):
    return pltpu.CompilerParams(dimension_semantics=semantics, vmem_limit_bytes=VMEM_LIMIT_BYTES)


def _tile(dim, cap):
    if dim <= cap:
        return dim
    for t in range(cap - cap % LANES, 0, -LANES):
        if dim % t == 0:
            return t
    raise ValueError(f"no tile for {dim} under {cap}")


def _row_tile(rows, row_bytes, budget=1 << 20):
    if rows * row_bytes <= budget:
        return rows
    cap = max(16, budget // row_bytes)
    for t in range(cap - cap % 16, 0, -16):
        if rows % t == 0:
            return t
    return rows


def _rows(ts, width, cidx=0):
    return pl.BlockSpec((ts, width), lambda i: (i, cidx))


def _fixed(shape):
    return pl.BlockSpec(shape, lambda *_: (0,) * len(shape))


def _sigmoid(x):
    return 1.0 / (1.0 + jnp.exp(-x))


def _matmul(a, b, mode, out_dtype, name, add=None, blocked=False, ride=None):
    nb = n_blk = 0
    blocked = blocked or b.ndim == 3
    if mode == "nn":
        (m, k) = a.shape
        n = b.shape[0] * b.shape[2] if blocked else b.shape[1]
    elif mode == "nt":
        (m, k) = a.shape
        n = b.shape[1] if blocked else b.shape[0]
    else:
        (k, m), n = a.shape, b.shape[1]
    if blocked:
        nb = b.shape[2] if mode != "tn" else n // N_DEV
    unit = nb if blocked and mode != "nt" else LANES
    out_bytes = jnp.dtype(out_dtype).itemsize + (4 if add is not None else 0)
    best = None
    for tn_c in range(unit, min(n, 1536) + 1, unit):
        for tm_c in sorted({256, 512, 1024, 2048, min(m, 2048)}):
            if n % tn_c or m % tm_c or (blocked and mode != "nt" and N_DEV % (tn_c // nb)):
                continue
            vmem = 2 * (tm_c * k * 2 + tn_c * k * 2 + tm_c * tn_c * out_bytes) + tm_c * tn_c * 4 + tn_c * k * 2
            if vmem <= MATMUL_VMEM_BYTES and (best is None or tm_c * tn_c / (tm_c + tn_c) > best[0]):
                best = (tm_c * tn_c / (tm_c + tn_c), tm_c, tn_c)
    if best is None:
        raise ValueError(f"{name}: no tiles for {m}x{n}x{k}")
    _, tm, tn = best
    if blocked:
        n_blk = N_DEV if mode == "nt" else tn // nb
    dims = {"nn": ((1,), (0,)), "nt": ((1,), (1,)), "tn": ((0,), (0,))}[mode]
    a_spec = pl.BlockSpec((k, tm), lambda i, j: (0, i)) if mode == "tn" else pl.BlockSpec((tm, k), lambda i, j: (i, 0))
    b_spec = pl.BlockSpec((tn, k), lambda i, j: (j, 0)) if mode == "nt" else pl.BlockSpec((k, tn), lambda i, j: (0, j))
    o_spec = pl.BlockSpec((tm, tn), lambda i, j: (i, j))
    out_shape = jax.ShapeDtypeStruct((m, n), out_dtype)
    if blocked and mode == "nn":
        b_spec = pl.BlockSpec((n_blk, k, nb), lambda i, j: (j, 0, 0))
    elif blocked and mode == "nt":
        b_spec = pl.BlockSpec((n_blk, tn, nb), lambda i, j: (0, j, 0))
    elif blocked:
        o_spec = pl.BlockSpec((n_blk, tm, nb), lambda i, j: (j, i, 0))
        out_shape = jax.ShapeDtypeStruct((N_DEV, m, nb), out_dtype)
    has_add = add is not None
    grid = (m // tm, n // tn)

    def body(*refs):
        (a_ref, b_ref, *rest), start, finish = _ride_hooks(ride, refs, 3 if has_add else 2, 1, grid)
        start()
        o_ref = rest[-1]
        if blocked and mode != "tn":
            bv = jnp.concatenate([b_ref[c] for c in range(n_blk)], axis=1) if n_blk > 1 else b_ref[0]
        else:
            bv = b_ref[...]
        total = lax.dot_general(a_ref[...], bv, (dims, ((), ())), preferred_element_type=F32)
        if has_add:
            total = total + rest[0][...]
        if blocked and mode == "tn":
            for c in range(n_blk):
                o_ref[c] = total[:, c * nb:(c + 1) * nb].astype(o_ref.dtype)
        else:
            o_ref[...] = total.astype(o_ref.dtype)
        finish()

    operands = (a, b, add) if has_add else (a, b)
    (out,), rode = _ride_call(ride, body, name, (out_shape,), grid, [a_spec, b_spec] + ([o_spec] if has_add else []),
                              (o_spec,), ("parallel", "parallel"), operands)
    return out if ride is None else (out, rode)


def _rms_fwd(x, win, gain, out_dtype, name, res=None):
    width, cidx = win
    s = x.shape[0]
    ts = min(s, 512)
    has_res = res is not None

    def body(x_ref, g_ref, *rest):
        o_ref = rest[-1]
        xv = x_ref[...].astype(F32)
        r = lax.rsqrt(jnp.mean(xv * xv, axis=-1, keepdims=True) + EPS)
        y = (xv * r) * g_ref[...]
        if has_res:
            y = rest[0][...] + y
        o_ref[...] = y.astype(o_ref.dtype)

    ops = (x, gain.reshape(1, width)) + ((res,) if has_res else ())
    return pl.pallas_call(
        body, name=name, out_shape=jax.ShapeDtypeStruct((s, width), out_dtype), grid=(s // ts,),
        in_specs=[_rows(ts, width, cidx), _fixed((1, width))] + ([_rows(ts, width)] if has_res else []),
        out_specs=_rows(ts, width), compiler_params=_params("parallel"))(*ops)


def _into(dz, n_inputs, out_index):
    return dict(in_specs=[ANY], operands=(dz,), input_output_aliases={n_inputs: out_index},
                out_shape=jax.ShapeDtypeStruct(dz.shape, dz.dtype))


def _rms_bwd(x, win, gain, dy, out_dtype, name, add=None, dz=None):
    width, cidx = win
    s = x.shape[0]
    ts = min(s, 512)
    has_add = add is not None

    def body(x_ref, g_ref, dy_ref, *rest):
        dx_ref, dg_ref = rest[-2], rest[-1]
        xv = x_ref[...].astype(F32)
        r = lax.rsqrt(jnp.mean(xv * xv, axis=-1, keepdims=True) + EPS)
        xh = xv * r
        dyv = dy_ref[...].astype(F32)
        dyg = dyv * g_ref[...]
        dx = r * (dyg - xh * jnp.mean(dyg * xh, axis=-1, keepdims=True))
        if has_add:
            dx = dx + rest[0][...]
        dx_ref[...] = dx.astype(dx_ref.dtype)

        @pl.when(pl.program_id(0) == 0)
        def _():
            dg_ref[...] = jnp.zeros_like(dg_ref)

        dg_ref[...] += jnp.sum(dyv * xh, axis=0, keepdims=True)

    ops = (x, gain.reshape(1, width), dy) + ((add,) if has_add else ())
    in_specs = [_rows(ts, width, cidx), _fixed((1, width)), _rows(ts, width)] + ([_rows(ts, width)] if has_add else [])
    dx_shape, dx_spec, alias = jax.ShapeDtypeStruct((s, width), out_dtype), _rows(ts, width), {}
    if dz is not None:
        into = _into(dz, len(ops), 0)
        ops, in_specs, alias = ops + into["operands"], in_specs + into["in_specs"], into["input_output_aliases"]
        dx_shape, dx_spec = into["out_shape"], _rows(ts, width, cidx)
    dx, dg = pl.pallas_call(
        body, name=name, out_shape=(dx_shape, jax.ShapeDtypeStruct((1, width), F32)), grid=(s // ts,),
        in_specs=in_specs, out_specs=(dx_spec, _fixed((1, width))), input_output_aliases=alias,
        compiler_params=_params("arbitrary"))(*ops)
    return dx, dg.reshape(width)


def _rope(x, c, s1, s2):
    return x * c + pltpu.roll(x, 16, 1) * s1 + pltpu.roll(x, LANES - 16, 1) * s2


def _rope_t(g, c, s1, s2):
    return g * c + pltpu.roll(g * s1, LANES - 16, 1) + pltpu.roll(g * s2, 16, 1)


def _rope_tables(positions):
    inv_freq = ROPE_THETA ** (-jnp.arange(0, QK_ROPE, 2, dtype=F32) / QK_ROPE)
    ang = positions.astype(F32)[:, None] * inv_freq
    cos, sin = jnp.cos(ang), jnp.sin(ang)
    n = positions.shape[0]
    one, zero = jnp.ones((n, 1), F32), jnp.zeros((n, 1), F32)
    c = jnp.concatenate([jnp.tile(one, (1, QK_NOPE)), cos, cos, jnp.tile(one, (1, 32))], axis=1)
    s1 = jnp.concatenate([jnp.tile(zero, (1, QK_NOPE + 16)), sin, jnp.tile(zero, (1, 32))], axis=1)
    s2 = jnp.concatenate([jnp.tile(zero, (1, QK_NOPE)), -sin, jnp.tile(zero, (1, 48))], axis=1)
    return c, s1, s2


def _rope_qk_fwd(qf, kf, z, tables, name):
    s = qf.shape[0]
    ts = min(s, 256)
    hw = N_HEADS * HEAD_PAD

    def body(qf_ref, kf_ref, kr_ref, c_ref, s1_ref, s2_ref, q_ref, k_ref):
        c, s1, s2 = c_ref[...], s1_ref[...], s2_ref[...]
        kr = _rope(kr_ref[...].astype(F32), c, s1, s2)
        for h in range(N_HEADS):
            sl = slice(h * HEAD_PAD, (h + 1) * HEAD_PAD)
            q_ref[:, sl] = _rope(qf_ref[:, sl], c, s1, s2).astype(BF16)
            k_ref[:, sl] = (kf_ref[:, sl] + kr).astype(BF16)

    tab = _rows(ts, LANES)
    return pl.pallas_call(
        body, name=name, out_shape=(jax.ShapeDtypeStruct((s, hw), BF16),) * 2, grid=(s // ts,),
        in_specs=[_rows(ts, hw), _rows(ts, hw), _rows(ts, *ZC_KR), tab, tab, tab],
        out_specs=(_rows(ts, hw), _rows(ts, hw)), compiler_params=_params("parallel"))(qf, kf, z, *tables)


def _rope_qk_bwd(dq, dk, tables, dz, name):
    s = dq.shape[0]
    ts = min(s, 256)
    hw = N_HEADS * HEAD_PAD

    def body(dq_ref, dk_ref, c_ref, s1_ref, s2_ref, _, dqf_ref, dkf_ref, dkr_ref):
        c, s1, s2 = c_ref[...], s1_ref[...], s2_ref[...]
        ksum = jnp.zeros((ts, HEAD_PAD), F32)
        for h in range(N_HEADS):
            sl = slice(h * HEAD_PAD, (h + 1) * HEAD_PAD)
            dqf_ref[:, sl] = _rope_t(dq_ref[:, sl], c, s1, s2).astype(BF16)
            dkh = dk_ref[:, sl]
            dkf_ref[:, sl] = dkh.astype(BF16)
            ksum = ksum + dkh
        lane = lax.broadcasted_iota(jnp.int32, (ts, HEAD_PAD), 1)
        in_rope = (lane >= QK_NOPE) & (lane < QK_NOPE + QK_ROPE)
        dkr_ref[...] = jnp.where(in_rope, _rope_t(ksum, c, s1, s2), 0.0).astype(BF16)

    tab = _rows(ts, LANES)
    into = _into(dz, 5, 2)
    return pl.pallas_call(
        body, name=name,
        out_shape=(jax.ShapeDtypeStruct((s, hw), BF16), jax.ShapeDtypeStruct((s, hw), BF16), into["out_shape"]),
        grid=(s // ts,), in_specs=[_rows(ts, hw), _rows(ts, hw), tab, tab, tab] + into["in_specs"],
        out_specs=(_rows(ts, hw), _rows(ts, hw), _rows(ts, *ZC_KR)), input_output_aliases=into["input_output_aliases"],
        compiler_params=_params("parallel"))(dq, dk, *tables, dz)


def _attn_tile(s):
    return min(s, 512)


def _raw_scores(q, k, masked, row0=0):
    sc = lax.dot_general(q, k, (((1,), (1,)), ((), ())), preferred_element_type=F32)
    if masked:
        rows = row0 + lax.broadcasted_iota(jnp.int32, sc.shape, 0)
        cols = lax.broadcasted_iota(jnp.int32, sc.shape, 1)
        sc = jnp.where(cols <= rows, sc, -jnp.inf)
    return sc


def _ride_hooks(ride, refs, n_in, n_out, grid):
    if ride is None:
        return refs, lambda: None, lambda: None
    n = len(ride.arrays)
    own = refs[:n_in] + refs[n_in + n:n_in + n + n_out]
    ins, outs, sems = refs[n_in:n_in + n], refs[n_in + n + n_out:n_in + 2 * n + n_out], refs[n_in + 2 * n + n_out:]
    at_first = functools.reduce(lambda a, b: a & b, [pl.program_id(ax) == 0 for ax in range(len(grid))])
    at_last = functools.reduce(lambda a, b: a & b, [pl.program_id(ax) == g - 1 for ax, g in enumerate(grid)])
    return own, lambda: pl.when(at_first)(lambda: ride.start(ins, outs, sems)), \
        lambda: pl.when(at_last)(lambda: ride.finish(ins, outs, sems))


def _ride_call(ride, body, name, out_shape, grid, in_specs, out_specs, semantics, operands):
    n = 0 if ride is None else len(ride.arrays)
    res = pl.pallas_call(
        body, name=name, out_shape=tuple(out_shape) + (tuple(ride.out_shape) if n else ()), grid=grid,
        in_specs=list(in_specs) + [ANY] * n, out_specs=tuple(out_specs) + (ANY,) * n,
        scratch_shapes=list(ride.scratch) if n else [],
        compiler_params=_params(*(("arbitrary",) * len(grid) if n else semantics)))(*operands, *(ride.arrays if n else ()))
    return res[:len(out_shape)], list(res[len(out_shape):])


def _flash_fwd(q, k, v, name, ride=None):
    s = q.shape[0]
    t = _attn_tile(s)
    c2 = ATTN_SCALE * LOG2E
    grid = (N_HEADS, s // t)

    def body(*refs):
        (q_ref, k_ref, v_ref, o_ref, lse_ref), start, finish = _ride_hooks(ride, refs, 3, 2, grid)
        start()
        i = pl.program_id(1)
        qv = q_ref[...]

        def chunk(j, carry, masked):
            m_old, l_old, acc = carry
            at = pl.ds(pl.multiple_of(j * t, t), t)
            sc = _raw_scores(qv, k_ref[at, :], masked)
            m_new = jnp.maximum(m_old, jnp.max(sc, axis=-1, keepdims=True))
            p = jnp.exp2((sc - m_new) * c2)
            alpha = jnp.exp2((m_old - m_new) * c2)
            l_new = alpha * l_old + jnp.sum(p, axis=-1, keepdims=True)
            acc = alpha * acc + jnp.dot(p.astype(BF16), v_ref[at, :], preferred_element_type=F32)
            return m_new, l_new, acc

        init = (jnp.full((t, 1), -jnp.inf, F32), jnp.zeros((t, 1), F32), jnp.zeros((t, HEAD_PAD), F32))
        carry = lax.fori_loop(0, i, lambda j, cr: chunk(j, cr, False), init)
        m_fin, l_fin, acc = chunk(i, carry, True)
        o_ref[...] = (acc / l_fin).astype(o_ref.dtype)
        lse_ref[...] = jnp.broadcast_to(m_fin * ATTN_SCALE + jnp.log(l_fin), (t, HEAD_PAD))
        finish()

    qo = pl.BlockSpec((t, HEAD_PAD), lambda h, i: (i, h))
    whole = pl.BlockSpec((s, HEAD_PAD), lambda h, i: (0, h))
    return _ride_call(
        ride, body, name, (jax.ShapeDtypeStruct(q.shape, BF16), jax.ShapeDtypeStruct(q.shape, F32)), grid,
        [qo, whole, whole], (qo, qo), ("parallel", "parallel"), (q, k, v))


def _attn_delta(do, o, name):
    s = o.shape[0]
    t = _attn_tile(s)

    def body(do_ref, o_ref, delta_ref, dob_ref):
        for h in range(N_HEADS):
            sl = slice(h * HEAD_PAD, (h + 1) * HEAD_PAD)
            dov = do_ref[:, sl]
            delta_ref[:, sl] = jnp.broadcast_to(jnp.sum(dov * o_ref[:, sl].astype(F32), axis=-1, keepdims=True),
                                                (t, HEAD_PAD))
            dob_ref[:, sl] = dov.astype(BF16)

    blk = _rows(t, N_HEADS * HEAD_PAD)
    return pl.pallas_call(
        body, name=name, out_shape=(jax.ShapeDtypeStruct(o.shape, F32), jax.ShapeDtypeStruct(o.shape, BF16)),
        grid=(s // t,), in_specs=[blk, blk], out_specs=(blk, blk), compiler_params=_params("parallel"))(do, o)


def _flash_bwd(q, k, v, do, lse, delta, name, ride=None):
    s = q.shape[0]
    t = _attn_tile(s)
    nt = s // t
    c2 = ATTN_SCALE * LOG2E
    grid = (N_HEADS, nt)

    def body(*refs):
        (q_ref, k_ref, v_ref, do_ref, lse_ref, delta_ref, dq_ref, dk_ref, dv_ref), start, finish = _ride_hooks(
            ride, refs, 6, 3, grid)
        start()
        j = pl.program_id(1)
        kv, vv = k_ref[...], v_ref[...]

        @pl.when(j == 0)
        def _():
            dq_ref[...] = jnp.zeros_like(dq_ref)

        def chunk(i, carry, masked):
            dk_acc, dv_acc = carry
            at = pl.ds(pl.multiple_of(i * t, t), t)
            qi, doi = q_ref[at, :], do_ref[at, :]
            sc = _raw_scores(qi, kv, masked)
            p = jnp.exp2(sc * c2 - lse_ref[at, pl.ds(0, 1)] * LOG2E)
            dp = lax.dot_general(doi, vv, (((1,), (1,)), ((), ())), preferred_element_type=F32)
            ds = (p * (dp - delta_ref[at, pl.ds(0, 1)])).astype(BF16)
            dv_acc = dv_acc + lax.dot_general(p.astype(BF16), doi, (((0,), (0,)), ((), ())), preferred_element_type=F32)
            dk_acc = dk_acc + lax.dot_general(ds, qi, (((0,), (0,)), ((), ())), preferred_element_type=F32)
            dq_ref[at, :] += jnp.dot(ds, kv, preferred_element_type=F32) * ATTN_SCALE
            return dk_acc, dv_acc

        zero = jnp.zeros((t, HEAD_PAD), F32)
        carry = chunk(j, (zero, zero), True)
        dk_acc, dv_acc = lax.fori_loop(j + 1, nt, lambda i, cr: chunk(i, cr, False), carry)
        dk_ref[...] = dk_acc * ATTN_SCALE
        dv_ref[...] = dv_acc.astype(BF16)
        finish()

    blk = pl.BlockSpec((t, HEAD_PAD), lambda h, j: (j, h))
    whole = pl.BlockSpec((s, HEAD_PAD), lambda h, j: (0, h))
    return _ride_call(
        ride, body, name, (jax.ShapeDtypeStruct(q.shape, F32), jax.ShapeDtypeStruct(q.shape, F32),
                           jax.ShapeDtypeStruct(q.shape, BF16)), grid,
        [whole, blk, blk, whole, whole, whole], (whole, blk, blk), ("parallel", "arbitrary"), (q, k, v, do, lse, delta))


def _conv_tile(s):
    return min(s, 256)


def _halo_before(t, width, cidx):
    per = t // CONV_HALO
    return pl.BlockSpec((CONV_HALO, width), lambda i: (jnp.maximum(i * per - 1, 0), cidx))


def _halo_after(t, width, cidx, n_tiles):
    per = t // CONV_HALO
    last = n_tiles * per - 1
    return pl.BlockSpec((CONV_HALO, width), lambda i: (jnp.minimum((i + 1) * per, last), cidx))


def _fill_glu(hbuf, ap_ref, gp_ref, a_ref, g_ref, t):
    first = pl.program_id(0) == 0
    hbuf[pl.ds(0, CONV_HALO), :] = jnp.where(first, 0.0, ap_ref[...].astype(F32) * _sigmoid(gp_ref[...].astype(F32)))
    hbuf[pl.ds(CONV_HALO, t), :] = a_ref[...].astype(F32) * _sigmoid(g_ref[...].astype(F32))


def _phase_copies(dst, src, t):
    n = t + CONV_HALO - SUBLANES
    for s in range(1, SUBLANES):
        dst[s, pl.ds(0, n), :] = src[pl.ds(s, n), :]


def _window(phases, src, k, t):
    if k % SUBLANES == 0:
        return src[pl.ds(k, t), :]
    return phases[k % SUBLANES, pl.ds(k - k % SUBLANES, t), :]


def _layer_norm_parts(co):
    mu = jnp.mean(co, axis=-1, keepdims=True)
    xc = co - mu
    rstd = lax.rsqrt(jnp.mean(xc * xc, axis=-1, keepdims=True) + EPS)
    return xc * rstd, rstd


def _conv_fwd(z, conv_w, conv_b, ln_g, ln_b, name):
    s = z.shape[0]
    t = _conv_tile(s)
    off = CONV_HALO - (CONV_W - 1)

    def body(ap_ref, gp_ref, a_ref, g_ref, w_ref, b_ref, lg_ref, lb_ref, hc_ref, co_ref, hbuf, hph):
        _fill_glu(hbuf, ap_ref, gp_ref, a_ref, g_ref, t)
        _phase_copies(hph, hbuf, t)
        acc = jnp.zeros((t, CONV_C), F32) + b_ref[...]
        for j in range(CONV_W):
            acc = acc + _window(hph, hbuf, off + j, t) * w_ref[pl.ds(j, 1), :]
        co_ref[...] = acc
        xh, _ = _layer_norm_parts(acc)
        y = xh * lg_ref[...] + lb_ref[...]
        hc_ref[...] = (y * _sigmoid(y)).astype(BF16)

    vec = _fixed((1, CONV_C))
    return pl.pallas_call(
        body, name=name, out_shape=(jax.ShapeDtypeStruct((s, CONV_C), BF16), jax.ShapeDtypeStruct((s, CONV_C), F32)),
        grid=(s // t,),
        in_specs=[_halo_before(t, *ZC_CONV_A), _halo_before(t, *ZC_CONV_G), _rows(t, *ZC_CONV_A), _rows(t, *ZC_CONV_G),
                  _fixed((CONV_HALO, CONV_C)), vec, vec, vec],
        out_specs=(_rows(t, CONV_C), _rows(t, CONV_C)),
        scratch_shapes=[pltpu.VMEM((t + CONV_HALO, CONV_C), F32), pltpu.VMEM((SUBLANES, t + CONV_HALO, CONV_C), F32)],
        compiler_params=_params("parallel"))(z, z, z, z, conv_w, conv_b.reshape(1, -1), ln_g.reshape(1, -1),
                                             ln_b.reshape(1, -1))


def _conv_bwd_norm(dhc, co, ln_g, ln_b, name):
    s = co.shape[0]
    t = min(s, 512)

    def body(dhc_ref, co_ref, lg_ref, lb_ref, dco_ref, dg_ref, db_ref, dcb_ref):
        xh, rstd = _layer_norm_parts(co_ref[...])
        y = xh * lg_ref[...] + lb_ref[...]
        sg = _sigmoid(y)
        dy = dhc_ref[...] * (sg * (1.0 + y * (1.0 - sg)))
        dxh = dy * lg_ref[...]
        dco = rstd * (dxh - jnp.mean(dxh, axis=-1, keepdims=True) - xh * jnp.mean(dxh * xh, axis=-1, keepdims=True))
        dco_ref[...] = dco

        @pl.when(pl.program_id(0) == 0)
        def _():
            dg_ref[...] = jnp.zeros_like(dg_ref)
            db_ref[...] = jnp.zeros_like(db_ref)
            dcb_ref[...] = jnp.zeros_like(dcb_ref)

        dg_ref[...] += jnp.sum(dy * xh, axis=0, keepdims=True)
        db_ref[...] += jnp.sum(dy, axis=0, keepdims=True)
        dcb_ref[...] += jnp.sum(dco, axis=0, keepdims=True)

    vec = _fixed((1, CONV_C))
    one = jax.ShapeDtypeStruct((1, CONV_C), F32)
    dco, dg, db, dcb = pl.pallas_call(
        body, name=name, out_shape=(jax.ShapeDtypeStruct((s, CONV_C), F32), one, one, one), grid=(s // t,),
        in_specs=[_rows(t, CONV_C), _rows(t, CONV_C), vec, vec], out_specs=(_rows(t, CONV_C), vec, vec, vec),
        compiler_params=_params("arbitrary"))(dhc, co, ln_g.reshape(1, -1), ln_b.reshape(1, -1))
    return dco, dg.reshape(-1), db.reshape(-1), dcb.reshape(-1)


def _conv_bwd_taps(dco, z, conv_w, dz, name):
    s = z.shape[0]
    t = _conv_tile(s)
    nt = s // t
    off = CONV_HALO - (CONV_W - 1)

    def body(ap_ref, gp_ref, a_ref, g_ref, d_ref, dn_ref, w_ref, _, du_ref, dw_ref, hbuf, dbuf, hph, dph):
        i = pl.program_id(0)
        _fill_glu(hbuf, ap_ref, gp_ref, a_ref, g_ref, t)
        dbuf[pl.ds(0, t), :] = d_ref[...]
        dbuf[pl.ds(t, CONV_HALO), :] = jnp.where(i == nt - 1, 0.0, dn_ref[...])
        _phase_copies(hph, hbuf, t)
        _phase_copies(dph, dbuf, t)

        @pl.when(i == 0)
        def _():
            dw_ref[...] = jnp.zeros_like(dw_ref)

        dcur = d_ref[...]
        dh = jnp.zeros((t, CONV_C), F32)
        for j in range(CONV_W):
            dh = dh + _window(dph, dbuf, CONV_W - 1 - j, t) * w_ref[pl.ds(j, 1), :]
            dw_ref[pl.ds(j, 1), :] += jnp.sum(dcur * _window(hph, hbuf, off + j, t), axis=0, keepdims=True)
        a, sg = a_ref[...].astype(F32), _sigmoid(g_ref[...].astype(F32))
        du_ref[:, pl.ds(0, CONV_C)] = (dh * sg).astype(BF16)
        du_ref[:, pl.ds(CONV_C, CONV_C)] = (dh * a * sg * (1.0 - sg)).astype(BF16)

    into = _into(dz, 7, 0)
    return pl.pallas_call(
        body, name=name, out_shape=(into["out_shape"], jax.ShapeDtypeStruct((CONV_HALO, CONV_C), F32)), grid=(nt,),
        in_specs=[_halo_before(t, *ZC_CONV_A), _halo_before(t, *ZC_CONV_G), _rows(t, *ZC_CONV_A), _rows(t, *ZC_CONV_G),
                  _rows(t, CONV_C), _halo_after(t, CONV_C, 0, nt), _fixed((CONV_HALO, CONV_C))] + into["in_specs"],
        out_specs=(_rows(t, *ZC_CONV), _fixed((CONV_HALO, CONV_C))), input_output_aliases=into["input_output_aliases"],
        scratch_shapes=[pltpu.VMEM((t + CONV_HALO, CONV_C), F32), pltpu.VMEM((t + CONV_HALO, CONV_C), F32),
                        pltpu.VMEM((SUBLANES, t + CONV_HALO, CONV_C), F32),
                        pltpu.VMEM((SUBLANES, t + CONV_HALO, CONV_C), F32)],
        compiler_params=_params("arbitrary"))(z, z, z, z, dco, dco, conv_w, dz)


def _pool_tile(s):
    return min(s, 512)


def _pool_counts(row0, n, window):
    rows = row0 + lax.broadcasted_iota(jnp.int32, (n, POOL_GD), 0)
    return jnp.minimum(rows + 1, window).astype(F32)


def _pool_diff(ubuf, gi, window, row0, t):
    lanes = pl.ds(gi * POOL_GD, POOL_GD)
    tot = ubuf[pl.ds(CONV_HALO, t), lanes]
    cur = tot
    for back in range(1, window):
        tot = tot + ubuf[pl.ds(CONV_HALO - back, t), lanes]
    return tot / _pool_counts(row0, t, window) - cur


def _pool_fwd(z, pool_w, pool_scale, name):
    s = z.shape[0]
    t = _pool_tile(s)

    def body(up_ref, u_ref, w_ref, sc_ref, m_ref, ubuf):
        i = pl.program_id(0)
        ubuf[pl.ds(0, CONV_HALO), :] = jnp.where(i == 0, 0.0, up_ref[...].astype(F32))
        ubuf[pl.ds(CONV_HALO, t), :] = u_ref[...].astype(F32)
        for gi, window in enumerate(POOL_WINDOWS):
            d = _pool_diff(ubuf, gi, window, i * t, t)
            mm = jnp.dot(d.astype(BF16), w_ref[gi].astype(BF16), preferred_element_type=F32)
            lanes = pl.ds(gi * POOL_GD, POOL_GD)
            m_ref[:, lanes] = (mm * sc_ref[:, lanes]).astype(BF16)

    return pl.pallas_call(
        body, name=name, out_shape=jax.ShapeDtypeStruct((s, POOL_C), BF16), grid=(s // t,),
        in_specs=[_halo_before(t, *ZC_POOL), _rows(t, *ZC_POOL), _fixed((POOL_G, POOL_GD, POOL_GD)), _fixed((1, POOL_C))],
        out_specs=_rows(t, POOL_C), scratch_shapes=[pltpu.VMEM((t + CONV_HALO, POOL_C), F32)],
        compiler_params=_params("parallel"))(z, z, pool_w, pool_scale.reshape(1, -1))


def _pool_bwd(dm, z, pool_w, pool_scale, dz, name):
    s = z.shape[0]
    t = _pool_tile(s)
    nt = s // t

    def body(up_ref, u_ref, dm_ref, dmn_ref, w_ref, sc_ref, _, du_ref, dw_ref, dsc_ref, ubuf, ebuf):
        i = pl.program_id(0)
        ubuf[pl.ds(0, CONV_HALO), :] = jnp.where(i == 0, 0.0, up_ref[...].astype(F32))
        ubuf[pl.ds(CONV_HALO, t), :] = u_ref[...].astype(F32)

        @pl.when(i == 0)
        def _():
            dw_ref[...] = jnp.zeros_like(dw_ref)
            dsc_ref[...] = jnp.zeros_like(dsc_ref)

        dm_next = jnp.where(i == nt - 1, 0.0, dmn_ref[...])
        for gi, window in enumerate(POOL_WINDOWS):
            lanes = pl.ds(gi * POOL_GD, POOL_GD)
            wb = w_ref[gi].astype(BF16)
            scale = sc_ref[:, lanes]
            d = _pool_diff(ubuf, gi, window, i * t, t).astype(BF16)
            mm = jnp.dot(d, wb, preferred_element_type=F32)
            dmv = dm_ref[:, lanes]
            dsc_ref[:, lanes] += jnp.sum(dmv * mm, axis=0, keepdims=True)
            dmm = (dmv * scale).astype(BF16)
            dw_ref[gi] += lax.dot_general(d, dmm, (((0,), (0,)), ((), ())), preferred_element_type=F32)
            dd = lax.dot_general(dmm, wb, (((1,), (1,)), ((), ())), preferred_element_type=F32)
            dd_next = lax.dot_general((dm_next[:, gi * POOL_GD:(gi + 1) * POOL_GD] * scale).astype(BF16), wb,
                                      (((1,), (1,)), ((), ())), preferred_element_type=F32)
            ebuf[pl.ds(0, t), lanes] = dd / _pool_counts(i * t, t, window)
            ebuf[pl.ds(t, CONV_HALO), lanes] = dd_next / _pool_counts((i + 1) * t, CONV_HALO, window)
            du = -dd
            for ahead in range(window):
                du = du + ebuf[pl.ds(ahead, t), lanes]
            du_ref[:, lanes] = du.astype(BF16)

    into = _into(dz, 6, 0)
    du, dw, dsc = pl.pallas_call(
        body, name=name,
        out_shape=(into["out_shape"], jax.ShapeDtypeStruct((POOL_G, POOL_GD, POOL_GD), F32),
                   jax.ShapeDtypeStruct((1, POOL_C), F32)), grid=(nt,),
        in_specs=[_halo_before(t, *ZC_POOL), _rows(t, *ZC_POOL), _rows(t, POOL_C), _halo_after(t, POOL_C, 0, nt),
                  _fixed((POOL_G, POOL_GD, POOL_GD)), _fixed((1, POOL_C))] + into["in_specs"],
        out_specs=(_rows(t, *ZC_POOL), _fixed((POOL_G, POOL_GD, POOL_GD)), _fixed((1, POOL_C))),
        input_output_aliases=into["input_output_aliases"],
        scratch_shapes=[pltpu.VMEM((t + CONV_HALO, POOL_C), F32), pltpu.VMEM((t + CONV_HALO, POOL_C), F32)],
        compiler_params=_params("arbitrary"))(z, z, dm, dm, pool_w, pool_scale.reshape(1, -1), dz)
    return du, dw, dsc.reshape(-1)


def _gate_specs(ts):
    width, first = ZC_GATE
    return [_rows(ts, width, first + b) for b in range(3)]


def _merge_fwd(z, ys, name):
    s = z.shape[0]
    ts = min(s, 256)

    def body(g0, g1, g2, y0, y1, y2, o_ref):
        o_ref[...] = sum(_sigmoid(g[...].astype(F32)) * y[...].astype(F32)
                         for g, y in ((g0, y0), (g1, y1), (g2, y2))).astype(BF16)

    return pl.pallas_call(
        body, name=name, out_shape=jax.ShapeDtypeStruct((s, D_MODEL), BF16), grid=(s // ts,),
        in_specs=_gate_specs(ts) + [_rows(ts, D_MODEL)] * 3, out_specs=_rows(ts, D_MODEL),
        compiler_params=_params("parallel"))(z, z, z, *ys)


def _merge_bwd(z, ys, dmerged, name):
    s = z.shape[0]
    ts = min(s, 256)

    def body(g0, g1, g2, y0, y1, y2, dm_ref, dy0, dy1, dy2, dz_ref):
        dmv = dm_ref[...]
        for b, (g_ref, y_ref, dy_ref) in enumerate(((g0, y0, dy0), (g1, y1, dy1), (g2, y2, dy2))):
            sg = _sigmoid(g_ref[...].astype(F32))
            dy_ref[...] = (dmv * sg).astype(BF16)
            dz_ref[:, pl.ds(b * D_MODEL, D_MODEL)] = (dmv * y_ref[...].astype(F32) * sg * (1.0 - sg)).astype(BF16)

    out = jax.ShapeDtypeStruct((s, D_MODEL), BF16)
    return pl.pallas_call(
        body, name=name, out_shape=(out,) * 3 + (jax.ShapeDtypeStruct((s, Z_W), BF16),), grid=(s // ts,),
        in_specs=_gate_specs(ts) + [_rows(ts, D_MODEL)] * 4,
        out_specs=(_rows(ts, D_MODEL),) * 3 + (_rows(ts, *ZC_GATES),),
        compiler_params=_params("parallel"))(z, z, z, *ys, dmerged)


def _ffn_up_fwd(h, w_gate, w_up, name):
    s, d = h.shape
    nb = w_gate.shape[2]
    f = N_DEV * nb
    tm, n_blk = min(s, 1024), 2
    tn = n_blk * nb
    blk = pl.BlockSpec((tm, tn), lambda i, j: (i, j))
    wspec = pl.BlockSpec((n_blk, d, nb), lambda i, j: (j, 0, 0))

    def body(h_ref, wg_ref, wu_ref, hg_ref, hu_ref, act_ref):
        hv = h_ref[...]
        g = jnp.dot(hv, jnp.concatenate([wg_ref[c] for c in range(n_blk)], axis=1), preferred_element_type=F32)
        u = jnp.dot(hv, jnp.concatenate([wu_ref[c] for c in range(n_blk)], axis=1), preferred_element_type=F32)
        hg_ref[...] = g.astype(hg_ref.dtype)
        hu_ref[...] = u.astype(hu_ref.dtype)
        act_ref[...] = (g * _sigmoid(g) * u).astype(BF16)

    return pl.pallas_call(
        body, name=name,
        out_shape=(jax.ShapeDtypeStruct((s, f), BF16),) * 3,
        grid=(s // tm, f // tn), in_specs=[pl.BlockSpec((tm, d), lambda i, j: (i, 0)), wspec, wspec],
        out_specs=(blk, blk, blk), compiler_params=_params("parallel", "parallel"))(h, w_gate, w_up)


def _ffn_down_bwd(dfo, w_down, hg, hu, name):
    s, d = dfo.shape
    f = w_down.shape[0]
    tm, tn = min(s, 1024), _tile(f, 1024)
    blk = pl.BlockSpec((tm, tn), lambda i, j: (i, j))

    def body(d_ref, w_ref, g_ref, u_ref, dg_ref, du_ref):
        dact = lax.dot_general(d_ref[...], w_ref[...], (((1,), (1,)), ((), ())), preferred_element_type=F32)
        g = g_ref[...].astype(F32)
        sg = _sigmoid(g)
        dg_ref[...] = (dact * u_ref[...].astype(F32) * (sg * (1.0 + g * (1.0 - sg)))).astype(BF16)
        du_ref[...] = (dact * g * sg).astype(BF16)

    out = jax.ShapeDtypeStruct((s, f), BF16)
    return pl.pallas_call(
        body, name=name, out_shape=(out, out), grid=(s // tm, f // tn),
        in_specs=[pl.BlockSpec((tm, d), lambda i, j: (i, 0)), pl.BlockSpec((tn, d), lambda i, j: (j, 0)), blk, blk],
        out_specs=(blk, blk), compiler_params=_params("parallel", "parallel"))(dfo, w_down, hg, hu)


def _loss_grad(y, target, name):
    s, d = y.shape
    ts = min(s, 512)

    def body(y_ref, t_ref, dy_ref, sq_ref):
        e = y_ref[...] - t_ref[...]
        dy_ref[...] = e / d

        @pl.when(pl.program_id(0) == 0)
        def _():
            sq_ref[...] = jnp.zeros_like(sq_ref)

        sq_ref[...] += jnp.sum(e * e, axis=0, keepdims=True)

    return pl.pallas_call(
        body, name=name, out_shape=(jax.ShapeDtypeStruct((s, d), F32), jax.ShapeDtypeStruct((1, d), F32)),
        grid=(s // ts,), in_specs=[_rows(ts, d), _rows(ts, d)], out_specs=(_rows(ts, d), _fixed((1, d))),
        compiler_params=_params("arbitrary"))(y, target)


def _adamw(w, g, m, v, name):
    shape = w.shape
    cols = shape[-1]
    keep3 = w.ndim == 3 and shape[1] < SUBLANES
    view = shape if keep3 else (math.prod(shape[:-1]), cols)
    rows = view[0]
    if keep3:
        cap = max(1, (1 << 20) // (SUBLANES * cols * 4))
        tr = max(t for t in range(1, cap + 1) if rows % t == 0)
    else:
        tr = _row_tile(rows, cols * 4)

    def body(w_ref, g_ref, m_ref, v_ref, d_ref, mo_ref, vo_ref):
        gv = g_ref[...]
        mn = B1 * m_ref[...] + (1.0 - B1) * gv
        vn = B2 * v_ref[...] + (1.0 - B2) * (gv * gv)
        m_hat = mn / (1.0 - B1 ** STEP)
        v_hat = vn / (1.0 - B2 ** STEP)
        d_ref[...] = -LR * (m_hat / (jnp.sqrt(v_hat) + ADAM_EPS) + WD * w_ref[...])
        mo_ref[...] = mn
        vo_ref[...] = vn

    spec = pl.BlockSpec((tr,) + view[1:], lambda i: (i,) + (0,) * (len(view) - 1))
    out = jax.ShapeDtypeStruct(view, F32)
    res = pl.pallas_call(
        body, name=name, out_shape=(out,) * 3, grid=(rows // tr,), in_specs=[spec] * 4, out_specs=(spec,) * 3,
        compiler_params=_params("parallel"))(*[t.reshape(view) for t in (w, g, m, v)])
    return tuple(r.reshape(shape) for r in res)


LANE_MAJOR = ("w_uq", "w_uk", "w_uv", "w_gate", "w_up")


def _lane_major(name, a):
    if name == "w_in":
        return a.transpose(2, 0, 1)
    if name in LANE_MAJOR:
        return a.transpose(0, 2, 1)
    return a


def _from_lane_major(name, a):
    if name == "w_in":
        return a.transpose(1, 2, 0)
    return _lane_major(name, a)


ANY = pl.BlockSpec(memory_space=pl.ANY)


class _GatherRide:
    def __init__(self, arrays):
        n = len(arrays)
        self.arrays = list(arrays)
        self.out_shape = [jax.ShapeDtypeStruct((N_DEV,) + a.shape, a.dtype) for a in arrays]
        self.scratch = [pltpu.SemaphoreType.DMA((n, 7)), pltpu.SemaphoreType.DMA((n, 7)), pltpu.SemaphoreType.DMA((n,))]

    def _copies(self, ins, outs, sems):
        send_sems, recv_sems, local_sems = sems
        n = len(self.arrays)
        x, y, c = lax.axis_index("x"), lax.axis_index("y"), lax.axis_index("c")
        me, sibling = (x, y, c), (x, y, 1 - c)
        chips = [(1 - x, y), (x, 1 - y), (1 - x, 1 - y)]

        def slot(a, px, py, pc):
            return outs[a].at[4 * px + 2 * py + pc]

        def copy(a, k, block, to, src=None):
            return pltpu.make_async_remote_copy(
                src_ref=slot(a, *block) if src is None else src, dst_ref=slot(a, *block), send_sem=send_sems.at[a, k],
                recv_sem=recv_sems.at[a, k], device_id=to, device_id_type=MESH)

        mine = [pltpu.make_async_copy(ins[a], slot(a, *me), local_sems.at[a]) for a in range(n)]
        first = []
        for a in range(n):
            first.append(copy(a, 0, me, sibling, src=ins[a]))
            first += [copy(a, 1 + j, me, (*chip, c), src=ins[a]) for j, chip in enumerate(chips)]
        return n, me, sibling, chips, c, copy, mine, first

    def start(self, ins, outs, sems):
        _, _, _, _, _, _, mine, first = self._copies(ins, outs, sems)
        for cp in mine + first:
            cp.start()

    def finish(self, ins, outs, sems):
        n, me, sibling, chips, c, copy, mine, first = self._copies(ins, outs, sems)
        passed = []
        for j, chip in enumerate(chips):
            for a in range(n):
                copy(a, 1 + j, (*chip, c), me).wait_recv()
                passed.append(copy(a, 4 + j, (*chip, c), sibling))
                passed[-1].start()
        for a in range(n):
            copy(a, 0, sibling, me).wait_recv()
            for j, chip in enumerate(chips):
                copy(a, 4 + j, (*chip, 1 - c), me).wait_recv()
        for cp in first + passed:
            cp.wait_send()
        for cp in mine:
            cp.wait()


class _ReduceRide:
    def __init__(self, arrays):
        n = len(arrays)
        self.arrays = list(arrays)
        self.out_shape = [jax.ShapeDtypeStruct(a.shape, a.dtype) for a in arrays]
        self.scratch = [pltpu.SemaphoreType.DMA((n, 7)), pltpu.SemaphoreType.DMA((n, 7)), pltpu.SemaphoreType.DMA((n,))]

    def _copies(self, ins, outs, sems):
        send_sems, recv_sems, local_sems = sems
        n = len(self.arrays)
        x, y, c = lax.axis_index("x"), lax.axis_index("y"), lax.axis_index("c")
        mine = [pltpu.make_async_copy(ins[a].at[4 * x + 2 * y + c], outs[a].at[0], local_sems.at[a]) for a in range(n)]
        copies = []
        for a in range(n):
            for k in range(1, N_DEV):
                px = 1 - x if k & 4 else x
                py = 1 - y if k & 2 else y
                pc = 1 - c if k & 1 else c
                copies.append(pltpu.make_async_remote_copy(
                    src_ref=ins[a].at[4 * px + 2 * py + pc], dst_ref=outs[a].at[k], send_sem=send_sems.at[a, k - 1],
                    recv_sem=recv_sems.at[a, k - 1], device_id=(px, py, pc), device_id_type=MESH))
        return mine, copies

    def start(self, ins, outs, sems):
        mine, copies = self._copies(ins, outs, sems)
        for cp in mine + copies:
            cp.start()

    def finish(self, ins, outs, sems):
        mine, copies = self._copies(ins, outs, sems)
        for cp in copies + mine:
            cp.wait()


def _run_ride(ride, name):
    n = len(ride.arrays)

    def body(*refs):
        ins, outs, sems = refs[:n], refs[n:2 * n], refs[2 * n:]
        ride.start(ins, outs, sems)
        ride.finish(ins, outs, sems)

    return pl.pallas_call(body, name=name, out_shape=ride.out_shape, in_specs=[ANY] * n, out_specs=[ANY] * n,
                          scratch_shapes=ride.scratch)(*ride.arrays)


def _all_gather(arrays, name):
    return _run_ride(_GatherRide(arrays), name)


def _swap_with_sibling(arrays, name):
    n = len(arrays)

    def body(*refs):
        ins, outs = refs[:n], refs[n:2 * n]
        send_sems, recv_sems = refs[2 * n:]
        x, y, c = lax.axis_index("x"), lax.axis_index("y"), lax.axis_index("c")
        copies = [pltpu.make_async_remote_copy(
            src_ref=ins[a].at[1 - c], dst_ref=outs[a], send_sem=send_sems.at[a], recv_sem=recv_sems.at[a],
            device_id=(x, y, 1 - c), device_id_type=MESH) for a in range(n)]
        for cp in copies:
            cp.start()
        for cp in copies:
            cp.wait()

    return pl.pallas_call(
        body, name=name, out_shape=[jax.ShapeDtypeStruct(a.shape[1:], a.dtype) for a in arrays],
        in_specs=[ANY] * n, out_specs=[ANY] * n,
        scratch_shapes=[pltpu.SemaphoreType.DMA((n,)), pltpu.SemaphoreType.DMA((n,))])(*arrays)


class _ChipExchangeRide:
    def __init__(self, arrays):
        n = len(arrays)
        self.arrays = list(arrays)
        self.out_shape = [jax.ShapeDtypeStruct(a.shape, a.dtype) for a in arrays]
        self.scratch = [pltpu.SemaphoreType.DMA((n, 3)), pltpu.SemaphoreType.DMA((n, 3)), pltpu.SemaphoreType.DMA((n,))]

    def _copies(self, ins, outs, sems):
        send_sems, recv_sems, local_sems = sems
        n = len(self.arrays)
        x, y, c = lax.axis_index("x"), lax.axis_index("y"), lax.axis_index("c")
        partners = [(x, 1 - y), (1 - x, y), (1 - x, 1 - y)]
        mine = [pltpu.make_async_copy(ins[a].at[2 * x + y], outs[a].at[0], local_sems.at[a]) for a in range(n)]
        copies = [pltpu.make_async_remote_copy(
            src_ref=ins[a].at[2 * px + py], dst_ref=outs[a].at[1 + k], send_sem=send_sems.at[a, k],
            recv_sem=recv_sems.at[a, k], device_id=(px, py, c), device_id_type=MESH)
            for a in range(n) for k, (px, py) in enumerate(partners)]
        return mine, copies

    def start(self, ins, outs, sems):
        mine, copies = self._copies(ins, outs, sems)
        for cp in mine + copies:
            cp.start()

    def finish(self, ins, outs, sems):
        mine, copies = self._copies(ins, outs, sems)
        for cp in copies + mine:
            cp.wait()


class _Combo:
    def __init__(self, rides):
        self.rides = rides
        self.arrays = [a for r in rides for a in r.arrays]
        self.out_shape = [o for r in rides for o in r.out_shape]
        self.scratch = [sc for r in rides for sc in r.scratch]

    def _parts(self, ins, outs, sems):
        at_a = at_s = 0
        for r in self.rides:
            na, ns = len(r.arrays), len(r.scratch)
            yield r, ins[at_a:at_a + na], outs[at_a:at_a + na], sems[at_s:at_s + ns]
            at_a, at_s = at_a + na, at_s + ns

    def start(self, ins, outs, sems):
        for r, i, o, sm in self._parts(ins, outs, sems):
            r.start(i, o, sm)

    def finish(self, ins, outs, sems):
        for r, i, o, sm in self._parts(ins, outs, sems):
            r.finish(i, o, sm)


def _as_rows(a, lead):
    return a.reshape(a.shape[:lead] + (math.prod(a.shape[lead:-1]), a.shape[-1]))


def _add_pairs(a, b, name):
    a2, b2 = _as_rows(a, 0), _as_rows(b, 0)
    rows, cols = a2.shape
    tr = _row_tile(rows, cols * 4)

    def body(a_ref, b_ref, o_ref):
        o_ref[...] = (a_ref[...].astype(F32) + b_ref[...].astype(F32)).astype(o_ref.dtype)

    spec = _rows(tr, cols)
    out = pl.pallas_call(body, name=name, out_shape=jax.ShapeDtypeStruct(a2.shape, a.dtype), grid=(rows // tr,),
                         in_specs=[spec, spec], out_specs=spec, compiler_params=_params("parallel"))(a2, b2)
    return out.reshape(a.shape)


def _sum_blocks(a, name):
    a3 = _as_rows(a, 1)
    n, rows, cols = a3.shape
    tr = _row_tile(rows, n * cols * 4)

    def body(a_ref, o_ref):
        tot = a_ref[0].astype(F32)
        for k in range(1, n):
            tot = tot + a_ref[k].astype(F32)
        o_ref[...] = tot

    out = pl.pallas_call(body, name=name, out_shape=jax.ShapeDtypeStruct((rows, cols), F32), grid=(rows // tr,),
                         in_specs=[pl.BlockSpec((n, tr, cols), lambda j: (0, j, 0))], out_specs=_rows(tr, cols),
                         compiler_params=_params("parallel"))(a3)
    return out.reshape(a.shape[1:])


MIX_GROUPS = ("w_in", "w_uq", "w_uk", "w_uv", "w_attn_o", "w_conv_o", "w_pool_o", "w_mix_o")
FFN_GROUPS = ("w_gate", "w_up", "w_down")
MIX_EARLY = ("w_attn_o", "w_conv_o", "w_pool_o", "w_mix_o")
MIX_LATE = ("w_in", "w_uq", "w_uk", "w_uv")


def _pad_axis(a, axis, size):
    pad = [(0, 0)] * a.ndim
    pad[axis] = (0, size - a.shape[axis])
    return jnp.pad(a, pad)


def _local_groups(sh, l):
    out = {n: sh[n][l] for n in BIG}
    for n in ("w_uq", "w_uk", "w_uv"):
        out[n] = _pad_axis(out[n], -1, HEAD_PAD)
    for n in ("w_gate", "w_up"):
        out[n] = _pad_axis(out[n], -1, FF_SHARD_PAD)
    out["w_down"] = _pad_axis(out["w_down"], 0, FF_SHARD_PAD)
    return {n: v.astype(BF16) for n, v in out.items()}


def _arrange_w_in(blocks):
    parts, pos = [], 0
    for ref_lo, ref_hi, at in sorted(W_IN_PIECES, key=lambda p: p[2]):
        if at > pos:
            parts.append(jnp.zeros((blocks.shape[1], at - pos), blocks.dtype))
        for d in range(N_DEV):
            lo, hi = max(ref_lo, d * W_IN_SHARD), min(ref_hi, (d + 1) * W_IN_SHARD)
            if lo < hi:
                parts.append(blocks[d][:, lo - d * W_IN_SHARD:hi - d * W_IN_SHARD])
        pos = at + ref_hi - ref_lo
    if pos < Z_W:
        parts.append(jnp.zeros((blocks.shape[1], Z_W - pos), blocks.dtype))
    return jnp.concatenate(parts, axis=1)


def _w_in_shard(g, d):
    parts = []
    for ref_lo, ref_hi, at in W_IN_PIECES:
        lo, hi = max(ref_lo, d * W_IN_SHARD), min(ref_hi, (d + 1) * W_IN_SHARD)
        if lo < hi:
            parts.append(g[:, at + lo - ref_lo:at + hi - ref_lo])
    return jnp.concatenate(parts, axis=1)


def _mixer_weights(gat):
    w = {n: v for n, v in gat.items() if n != "w_in"}
    attn_o = gat["w_attn_o"].reshape(N_DEV, N_HEADS, V_HEAD, LANES)
    w["w_attn_o"] = _pad_axis(attn_o, 2, HEAD_PAD).reshape(N_DEV, N_HEADS * HEAD_PAD, LANES)
    w["w_mix_o"] = gat["w_mix_o"].reshape(D_MODEL, D_MODEL)
    return w


def _ffn_weights(gat):
    return {"w_gate": gat["w_gate"], "w_up": gat["w_up"], "w_down": gat["w_down"].reshape(D_FF_PAD, D_MODEL)}


def _mixer_grad_groups(gb):
    g = dict(gb)
    if "w_in" in gb:
        g["w_in"] = jnp.stack([_w_in_shard(gb["w_in"], d) for d in range(N_DEV)])
    if "w_attn_o" in gb:
        attn_o = gb["w_attn_o"].reshape(N_DEV, N_HEADS, HEAD_PAD, LANES)[:, :, :V_HEAD]
        g["w_attn_o"] = attn_o.reshape(N_DEV, N_HEADS * V_HEAD, LANES)
    if "w_mix_o" in gb:
        g["w_mix_o"] = gb["w_mix_o"].reshape(N_DEV, D_MODEL // N_DEV, D_MODEL)
    return g


def _ffn_grad_groups(gb):
    return {"w_gate": gb["w_gate"], "w_up": gb["w_up"], "w_down": gb["w_down"].reshape(N_DEV, FF_SHARD_PAD, D_MODEL)}


def _grads_from_groups(tot):
    g = dict(tot)
    g["w_uq"] = tot["w_uq"][:, :QK_NOPE + QK_ROPE]
    g["w_uk"], g["w_uv"] = tot["w_uk"][:, :QK_NOPE], tot["w_uv"][:, :V_HEAD]
    g["w_gate"], g["w_up"] = tot["w_gate"][:, :FF_SHARD], tot["w_up"][:, :FF_SHARD]
    g["w_down"] = tot["w_down"][:FF_SHARD]
    return g


SMALL_GROUPS = (
    (D_MODEL, ("mix_norm_pre", "mix_norm_post", "ffn_norm_pre", "ffn_norm_post")),
    (CONV_C, ("conv_w", "conv_b", "conv_ln_g", "conv_ln_b", "pool_scale")),
    (Q_RANK, ("q_norm",)), (KV_RANK, ("kv_norm",)), (POOL_GD, ("pool_w",)),
)


def _small_rows(name):
    return {"conv_w": CONV_HALO, "pool_w": POOL_G * POOL_GD}.get(name, SUBLANES)


def _small_groups(small):
    out = []
    for width, names in SMALL_GROUPS:
        parts = []
        for l in range(DEPTH):
            for n in names:
                part = small[l][n].reshape(-1, width)
                parts.append(_pad_axis(part, 0, _small_rows(n)))
        out.append(jnp.concatenate(parts, axis=0))
    return out


def _small_from_groups(groups):
    shapes = {"conv_w": (CONV_W, CONV_C), "pool_w": (POOL_G, POOL_GD, POOL_GD)}
    out = {}
    for (width, names), g in zip(SMALL_GROUPS, groups):
        row = 0
        for l in range(DEPTH):
            for n in names:
                rows = _small_rows(n)
                real = {"conv_w": CONV_W, "pool_w": POOL_G * POOL_GD}.get(n, 1)
                out.setdefault(n, []).append(g[row:row + real].reshape(shapes.get(n, (width,))))
                row += rows
    return {n: jnp.stack(v) for n, v in out.items()}


def _mixer_fwd(x, tables, sm, plan, l):
    nm = lambda n: f"{n}_l{l}"
    h = _rms_fwd(x, (D_MODEL, 0), sm["mix_norm_pre"], BF16, nm("mix_pre_norm"))
    w_in, ride = plan.w_in(l), plan.in_proj_ride(l)
    if ride is None:
        z = _matmul(h, w_in, "nn", BF16, nm("in_proj"))
    else:
        z, rode = _matmul(h, w_in, "nn", BF16, nm("in_proj"), ride=ride)
        plan.in_proj_done(l, rode)
    w = dict(plan.mixer_weights(l), w_in=w_in)
    cq = _rms_fwd(z, ZC_Q, sm["q_norm"], BF16, nm("q_norm"))
    ckv = _rms_fwd(z, ZC_KV, sm["kv_norm"], BF16, nm("kv_norm"))
    qf = _matmul(cq, w["w_uq"], "nn", F32, nm("q_up"))
    kf = _matmul(ckv, w["w_uk"], "nn", F32, nm("k_up"))
    v = _matmul(ckv, w["w_uv"], "nn", BF16, nm("v_up"))
    q, k = _rope_qk_fwd(qf, kf, z, tables, nm("rope_qk"))
    (o, lse), rode = _flash_fwd(q, k, v, nm("flash_fwd"), plan.fwd_ride(l))
    plan.fwd_done(l, rode)
    y_attn = _matmul(o, w["w_attn_o"], "nn", BF16, nm("attn_out"))
    hc, co = _conv_fwd(z, sm["conv_w"], sm["conv_b"], sm["conv_ln_g"], sm["conv_ln_b"], nm("conv_fwd"))
    y_conv = _matmul(hc, w["w_conv_o"], "nn", BF16, nm("conv_out"))
    pm = _pool_fwd(z, sm["pool_w"], sm["pool_scale"], nm("pool_fwd"))
    y_pool = _matmul(pm, w["w_pool_o"], "nn", BF16, nm("pool_out"))
    ys = (y_attn, y_conv, y_pool)
    merged = _merge_fwd(z, ys, nm("merge_fwd"))
    mo = _matmul(merged, w["w_mix_o"], "nn", F32, nm("mix_out"))
    x_mid = _rms_fwd(mo, (D_MODEL, 0), sm["mix_norm_post"], F32, nm("mix_post_norm"), res=x)
    saved = dict(x=x, h=h, z=z, cq=cq, ckv=ckv, q=q, k=k, v=v, o=o, lse=lse, hc=hc, co=co, pm=pm, ys=ys, merged=merged,
                 mo=mo)
    return x_mid, saved, w


def _ffn_fwd(x_mid, w, sm, tag):
    nm = lambda n: f"{n}_{tag}"
    h2 = _rms_fwd(x_mid, (D_MODEL, 0), sm["ffn_norm_pre"], BF16, nm("ffn_pre_norm"))
    hg, hu, act = _ffn_up_fwd(h2, w["w_gate"], w["w_up"], nm("ffn_up_fwd"))
    fo = _matmul(act, w["w_down"], "nn", F32, nm("ffn_down"))
    out = _rms_fwd(fo, (D_MODEL, 0), sm["ffn_norm_post"], F32, nm("ffn_post_norm"), res=x_mid)
    saved = dict(x_mid=x_mid, h2=h2, hg=hg, hu=hu, act=act, fo=fo)
    return out, saved


def _ffn_bwd(dout, sv, w, sm, tag):
    nm = lambda n: f"{n}_{tag}"
    gb, gs = {}, {}
    dfo, gs["ffn_norm_post"] = _rms_bwd(sv["fo"], (D_MODEL, 0), sm["ffn_norm_post"], dout, BF16, nm("ffn_post_norm_bwd"))
    gb["w_down"] = _matmul(sv["act"], dfo, "tn", BF16, nm("ffn_down_dw"))
    dhg, dhu = _ffn_down_bwd(dfo, w["w_down"], sv["hg"], sv["hu"], nm("ffn_down_bwd"))
    dh2_g = _matmul(dhg, w["w_gate"], "nt", F32, nm("ffn_gate_dx"))
    dh2 = _matmul(dhu, w["w_up"], "nt", F32, nm("ffn_up_dx"), add=dh2_g)
    gb["w_gate"] = _matmul(sv["h2"], dhg, "tn", BF16, nm("ffn_gate_dw"), blocked=True)
    gb["w_up"] = _matmul(sv["h2"], dhu, "tn", BF16, nm("ffn_up_dw"), blocked=True)
    dmid, gs["ffn_norm_pre"] = _rms_bwd(sv["x_mid"], (D_MODEL, 0), sm["ffn_norm_pre"], dh2, F32, nm("ffn_pre_norm_bwd"),
                                        add=dout)
    return dmid, gb, gs


def _mixer_bwd(dmid, sv, tables, w, sm, plan, l, pack_small):
    nm = lambda n: f"{n}_l{l}"
    gb, gs = {}, {}
    dmo, gs["mix_norm_post"] = _rms_bwd(sv["mo"], (D_MODEL, 0), sm["mix_norm_post"], dmid, BF16, nm("mix_post_norm_bwd"))
    dmerged = _matmul(dmo, w["w_mix_o"], "nt", F32, nm("mix_out_dx"))
    gb["w_mix_o"] = _matmul(sv["merged"], dmo, "tn", BF16, nm("mix_out_dw"))
    dya, dyc, dyp, dz = _merge_bwd(sv["z"], sv["ys"], dmerged, nm("merge_bwd"))
    dpm = _matmul(dyp, w["w_pool_o"], "nt", F32, nm("pool_out_dx"))
    gb["w_pool_o"] = _matmul(sv["pm"], dyp, "tn", BF16, nm("pool_out_dw"), blocked=True)
    dz, gs["pool_w"], gs["pool_scale"] = _pool_bwd(dpm, sv["z"], sm["pool_w"], sm["pool_scale"], dz, nm("pool_bwd"))
    dhc = _matmul(dyc, w["w_conv_o"], "nt", F32, nm("conv_out_dx"))
    gb["w_conv_o"] = _matmul(sv["hc"], dyc, "tn", BF16, nm("conv_out_dw"), blocked=True)
    dco, gs["conv_ln_g"], gs["conv_ln_b"], gs["conv_b"] = _conv_bwd_norm(dhc, sv["co"], sm["conv_ln_g"], sm["conv_ln_b"],
                                                                        nm("conv_bwd_norm"))
    dz, gs["conv_w"] = _conv_bwd_taps(dco, sv["z"], sm["conv_w"], dz, nm("conv_bwd_taps"))
    do = _matmul(dya, w["w_attn_o"], "nt", F32, nm("attn_out_dx"))
    gb["w_attn_o"] = _matmul(sv["o"], dya, "tn", BF16, nm("attn_out_dw"), blocked=True)
    delta, dob = _attn_delta(do, sv["o"], nm("attn_delta"))
    (dq, dk, dv), rode = _flash_bwd(sv["q"], sv["k"], sv["v"], dob, sv["lse"], delta, nm("flash_bwd"),
                                  plan.bwd_ride(l, gb))
    plan.bwd_done(l, rode)
    dqf, dkf, dz = _rope_qk_bwd(dq, dk, tables, dz, nm("rope_qk_bwd"))
    dcq_n = _matmul(dqf, w["w_uq"], "nt", F32, nm("q_up_dx"))
    gb["w_uq"] = _matmul(sv["cq"], dqf, "tn", BF16, nm("q_up_dw"), blocked=True)
    dckv_k = _matmul(dkf, w["w_uk"], "nt", F32, nm("k_up_dx"))
    dckv_n = _matmul(dv, w["w_uv"], "nt", F32, nm("v_up_dx"), add=dckv_k)
    gb["w_uk"] = _matmul(sv["ckv"], dkf, "tn", BF16, nm("k_up_dw"), blocked=True)
    gb["w_uv"] = _matmul(sv["ckv"], dv, "tn", BF16, nm("v_up_dw"), blocked=True)
    dz, gs["q_norm"] = _rms_bwd(sv["z"], ZC_Q, sm["q_norm"], dcq_n, BF16, nm("q_norm_bwd"), dz=dz)
    dz, gs["kv_norm"] = _rms_bwd(sv["z"], ZC_KV, sm["kv_norm"], dckv_n, BF16, nm("kv_norm_bwd"), dz=dz)
    gb["w_in"] = _matmul(sv["h"], dz, "tn", BF16, nm("in_proj_dw"))
    plan.add_grads(l, "mix", gb)
    ride, small_gathered = plan.tail_ride(l, pack_small(gs)), []
    if ride is None:
        dh = _matmul(dz, w["w_in"], "nt", F32, nm("in_proj_dx"))
    else:
        dh, rode = _matmul(dz, w["w_in"], "nt", F32, nm("in_proj_dx"), ride=ride)
        small_gathered = plan.tail_done(l, rode)
    dx, gs["mix_norm_pre"] = _rms_bwd(sv["x"], (D_MODEL, 0), sm["mix_norm_pre"], dh, F32, nm("mix_pre_norm_bwd"), add=dmid)
    return dx, gs, small_gathered


def _part_groups(part):
    return {"mix": MIX_GROUPS, "ffn": FFN_GROUPS, "early": MIX_EARLY, "late": MIX_LATE}[part]


class _Plan:
    def __init__(self, shards, conv_w):
        self.local = [_local_groups(shards, l) for l in range(DEPTH)]
        self.conv_w = conv_w
        self.gat, self.send, self.recv = {}, {}, {}

    @staticmethod
    def _riders(l):
        return [(l, "ffn")] + ([(l + 1, "mix")] if l + 1 < DEPTH else [])

    @staticmethod
    def _grad_riders(l):
        return [(l, "ffn"), (l, "early")] + ([(l + 1, "late")] if l + 1 < DEPTH else [])

    def gather_first(self):
        w_in, conv_w = _all_gather([self.local[0]["w_in"], self.conv_w], "gather_w_in_l0")
        self.gat[(0, "mix")] = {"w_in": w_in}
        return conv_w

    def w_in(self, l):
        return _arrange_w_in(self.gat[(l, "mix")]["w_in"])

    def in_proj_ride(self, l):
        return _GatherRide([self.local[0][g] for g in MIX_GROUPS[1:]]) if l == 0 else None

    def in_proj_done(self, l, outs):
        self.gat[(l, "mix")].update(zip(MIX_GROUPS[1:], outs))

    def fwd_ride(self, l):
        return _GatherRide([self.local[ll][g] for ll, part in self._riders(l) for g in _part_groups(part)])

    def fwd_done(self, l, outs):
        outs = list(outs)
        for ll, part in self._riders(l):
            self.gat[(ll, part)] = {g: outs.pop(0) for g in _part_groups(part)}

    def mixer_weights(self, l):
        return _mixer_weights(self.gat[(l, "mix")])

    def ffn_weights(self, l):
        return _ffn_weights(self.gat[(l, "ffn")])

    def add_grads(self, l, part, gb):
        if part == "ffn":
            self.send[(l, "ffn")] = _ffn_grad_groups(gb)
        else:
            self.send.setdefault((l, "late"), {}).update(_mixer_grad_groups({g: gb[g] for g in MIX_LATE if g in gb}))

    def bwd_ride(self, l, gb_early):
        self.send[(l, "early")] = _mixer_grad_groups({g: gb_early[g] for g in MIX_EARLY})
        return _ReduceRide([self.send[(ll, part)][g] for ll, part in self._grad_riders(l) for g in _part_groups(part)])

    def bwd_done(self, l, outs):
        outs = list(outs)
        for ll, part in self._grad_riders(l):
            self.recv[(ll, part)] = {g: outs.pop(0) for g in _part_groups(part)}

    def tail_ride(self, l, small_groups):
        if l > 0:
            return None
        send = [self.send[(0, "late")][g] for g in MIX_LATE]
        by_core = [a.reshape((4, 2) + a.shape[1:]).transpose((1, 0) + tuple(range(2, a.ndim + 1))) for a in send]
        core = lax.axis_index("c")
        own = [lax.dynamic_index_in_dim(a, core, axis=0, keepdims=False) for a in by_core]
        got = _swap_with_sibling(by_core, "reduce_d2d")
        pairs = [_add_pairs(a, b, f"reduce_pair_add_{g}") for g, a, b in zip(MIX_LATE, own, got)]
        return _Combo([_ChipExchangeRide(pairs), _GatherRide(small_groups)])

    def tail_done(self, l, outs):
        self.recv[(l, "late")] = dict(zip(MIX_LATE, outs[:len(MIX_LATE)]))
        return outs[len(MIX_LATE):]

    def finish(self):
        layers = []
        for l in range(DEPTH):
            tot = {g: _sum_blocks(a, f"reduce_sum_{g}_l{l}") for part in ("early", "late", "ffn")
                   for g, a in self.recv[(l, part)].items()}
            layers.append(_grads_from_groups(tot))
        return layers


def _local_step(x, positions, target, smalls, plan):
    tables = _rope_tables(positions)
    saved = []
    h = x
    for l in range(DEPTH):
        h, svm, wm = _mixer_fwd(h, tables, smalls[l], plan, l)
        wf = plan.ffn_weights(l)
        h, svf = _ffn_fwd(h, wf, smalls[l], f"l{l}")
        saved.append((svm, svf, wm, wf))
    dy, sq = _loss_grad(h, target, "loss_grad")
    small = [None] * DEPTH
    for l in reversed(range(DEPTH)):
        svm, svf, wm, wf = saved[l]
        dmid, gbf, gsf = _ffn_bwd(dy, svf, wf, smalls[l], f"l{l}")
        plan.add_grads(l, "ffn", gbf)

        def pack_small(gs, l=l, gsf=gsf):
            if l > 0:
                return None
            return _small_groups([{**gsf, **gs, "mix_norm_pre": jnp.zeros((D_MODEL,), F32)}] + small[1:])

        dy, gsm, small_gathered = _mixer_bwd(dmid, svm, tables, wm, smalls[l], plan, l, pack_small)
        small[l] = {**gsf, **gsm}
    return sq, dy, small, small_gathered


def kernel(x, positions, mix_norm_pre, w_in, q_norm, w_uq, kv_norm, w_uk, w_uv, w_attn_o, conv_w, conv_b, conv_ln_g, conv_ln_b, w_conv_o, pool_w, pool_scale, w_pool_o, w_mix_o, mix_norm_post, ffn_norm_pre, w_gate, w_up, w_down, ffn_norm_post, loss_target, m_mix_norm_pre, m_w_in, m_q_norm, m_w_uq, m_kv_norm, m_w_uk, m_w_uv, m_w_attn_o, m_conv_w, m_conv_b, m_conv_ln_g, m_conv_ln_b, m_w_conv_o, m_pool_w, m_pool_scale, m_w_pool_o, m_w_mix_o, m_mix_norm_post, m_ffn_norm_pre, m_w_gate, m_w_up, m_w_down, m_ffn_norm_post, v_mix_norm_pre, v_w_in, v_q_norm, v_w_uq, v_kv_norm, v_w_uk, v_w_uv, v_w_attn_o, v_conv_w, v_conv_b, v_conv_ln_g, v_conv_ln_b, v_w_conv_o, v_pool_w, v_pool_scale, v_w_pool_o, v_w_mix_o, v_mix_norm_post, v_ffn_norm_pre, v_w_gate, v_w_up, v_w_down, v_ffn_norm_post):
    given = dict(locals())
    dev = 4 * lax.axis_index("x") + 2 * lax.axis_index("y") + lax.axis_index("c")

    plan = _Plan({n: given[n] for n in BIG}, conv_w)
    cw = CONV_C // N_DEV
    conv_w_full = plan.gather_first().transpose(1, 2, 0, 3).reshape(DEPTH, CONV_W, CONV_C)
    smalls = []
    for l in range(DEPTH):
        sm = {n: given[n][l] for n in SMALL if n != "conv_w"}
        sm["conv_w"] = _pad_axis(conv_w_full[l], 0, CONV_HALO)
        smalls.append(sm)

    sq, grad_x, small, small_groups = _local_step(x[0], positions[0], loss_target[0], smalls, plan)
    loss = lax.psum(0.5 / D_MODEL * jnp.sum(sq), ("x", "y", "c"))
    per_layer = plan.finish()
    views = {}
    for n in BIG:
        if n == "w_in":
            views[n] = jnp.stack([per_layer[l][n].T for l in range(DEPTH)], axis=1)
        elif n in LANE_MAJOR:
            views[n] = jnp.stack([per_layer[l][n].T for l in range(DEPTH)])
        else:
            views[n] = jnp.stack([per_layer[l][n] for l in range(DEPTH)])
    grads = {n: _from_lane_major(n, views[n]) for n in BIG}

    small_sum = _small_from_groups([_sum_blocks(g, f"sum_small_grads_{i}") for i, g in enumerate(small_groups)])
    last = _pad_axis(small[0]["mix_norm_pre"].reshape(1, D_MODEL), 0, SUBLANES)
    last_sum = _sum_blocks(_all_gather([last], "gather_last_norm_grad")[0], "sum_last_norm_grad")[0]
    small_sum["mix_norm_pre"] = small_sum["mix_norm_pre"].at[0].set(last_sum)
    for n in SMALL:
        grads[n] = small_sum[n]
    grads["conv_w"] = lax.dynamic_slice_in_dim(small_sum["conv_w"], dev * cw, cw, axis=2)

    delta, new_m, new_v = {}, {}, {}
    for n in WEIGHTS:
        g_view = views[n] if n in views else grads[n]
        w_view, m_view, v_view = [_lane_major(n, given[k]) for k in (n, "m_" + n, "v_" + n)]
        res = _adamw(w_view, g_view, m_view, v_view, f"adamw_{n}")
        delta[n], new_m[n], new_v[n] = [_from_lane_major(n, r) for r in res]
    return (loss, grad_x[None], *[grads[n] for n in WEIGHTS], *[delta[n] for n in WEIGHTS],
            *[new_m[n] for n in WEIGHTS], *[new_v[n] for n in WEIGHTS])
```

```python
import functools
import math

import jax
import jax.numpy as jnp
from jax import lax
from jax.experimental import pallas as pl
from jax.experimental.pallas import tpu as pltpu

F32, BF16 = jnp.float32, jnp.bfloat16
MESH = pl.DeviceIdType.MESH

LANES = 128
SUBLANES = 8
VMEM_LIMIT_BYTES = 56 * 1024 * 1024
MATMUL_VMEM_BYTES = 40 * 1024 * 1024

N_DEV = 8
D_MODEL = 1024
DEPTH = 2
N_HEADS = 8
QK_NOPE, QK_ROPE, V_HEAD = 64, 32, 64
HEAD_PAD = LANES
Q_RANK, KV_RANK = 384, 256
ROPE_THETA = 10000.0
CONV_C, CONV_W = 512, 31
CONV_HALO = 32
POOL_WINDOWS = (2, 4, 8, 16)
POOL_C, POOL_G = 512, 4
POOL_GD = POOL_C // POOL_G
D_FF = 2816
FF_SHARD = D_FF // N_DEV
FF_SHARD_PAD = 3 * LANES
D_FF_PAD = N_DEV * FF_SHARD_PAD
W_IN_SHARD = 660
EPS = 1e-6
ATTN_SCALE = 1.0 / math.sqrt(QK_NOPE + QK_ROPE)
LOG2E = 1.4426950408889634
LR, B1, B2, ADAM_EPS, WD, STEP = 0.001, 0.9, 0.999, 1e-08, 0.01, 10

Z_W = 5376
ZC_GATE = (1024, 0)
ZC_GATES = (3072, 0)
ZC_CONV_A = (512, 6)
ZC_CONV_G = (512, 7)
ZC_CONV = (1024, 3)
ZC_POOL = (512, 8)
ZC_Q = (384, 12)
ZC_KR = (128, 39)
ZC_KV = (256, 20)
W_IN_PIECES = ((0, 384, 4608), (384, 640, 5120), (640, 672, 5056), (672, 1696, 3072), (1696, 2208, 4096),
               (2208, 5280, 0))

BIG = ("w_in", "w_uq", "w_uk", "w_uv", "w_attn_o", "w_conv_o", "w_pool_o", "w_mix_o", "w_gate", "w_up", "w_down")
SMALL = ("mix_norm_pre", "q_norm", "kv_norm", "conv_w", "conv_b", "conv_ln_g", "conv_ln_b", "pool_w", "pool_scale",
         "mix_norm_post", "ffn_norm_pre", "ffn_norm_post")
WEIGHTS = ("mix_norm_pre", "w_in", "q_norm", "w_uq", "kv_norm", "w_uk", "w_uv", "w_attn_o", "conv_w", "conv_b",
           "conv_ln_g", "conv_ln_b", "w_conv_o", "pool_w", "pool_scale", "w_pool_o", "w_mix_o", "mix_norm_post",
           "ffn_norm_pre", "w_gate", "w_up", "w_down", "ffn_norm_post")


def _params(*semantics):
    return pltpu.CompilerParams(dimension_semantics=semantics, vmem_limit_bytes=VMEM_LIMIT_BYTES)


def _tile(dim, cap):
    if dim <= cap:
        return dim
    for t in range(cap - cap % LANES, 0, -LANES):
        if dim % t == 0:
            return t
    raise ValueError(f"no tile for {dim} under {cap}")


def _row_tile(rows, row_bytes, budget=1 << 20):
    if rows * row_bytes <= budget:
        return rows
    cap = max(16, budget // row_bytes)
    for t in range(cap - cap % 16, 0, -16):
        if rows % t == 0:
            return t
    return rows


def _rows(ts, width, cidx=0):
    return pl.BlockSpec((ts, width), lambda i: (i, cidx))


def _fixed(shape):
    return pl.BlockSpec(shape, lambda *_: (0,) * len(shape))


def _sigmoid(x):
    return 1.0 / (1.0 + jnp.exp(-x))


def _matmul(a, b, mode, out_dtype, name, add=None, blocked=False, ride=None):
    nb = n_blk = 0
    blocked = blocked or b.ndim == 3
    if mode == "nn":
        (m, k) = a.shape
        n = b.shape[0] * b.shape[2] if blocked else b.shape[1]
    elif mode == "nt":
        (m, k) = a.shape
        n = b.shape[1] if blocked else b.shape[0]
    else:
        (k, m), n = a.shape, b.shape[1]
    if blocked:
        nb = b.shape[2] if mode != "tn" else n // N_DEV
    unit = nb if blocked and mode != "nt" else LANES
    out_bytes = jnp.dtype(out_dtype).itemsize + (4 if add is not None else 0)
    best = None
    for tn_c in range(unit, min(n, 1536) + 1, unit):
        for tm_c in sorted({256, 512, 1024, 2048, min(m, 2048)}):
            if n % tn_c or m % tm_c or (blocked and mode != "nt" and N_DEV % (tn_c // nb)):
                continue
            vmem = 2 * (tm_c * k * 2 + tn_c * k * 2 + tm_c * tn_c * out_bytes) + tm_c * tn_c * 4 + tn_c * k * 2
            if vmem <= MATMUL_VMEM_BYTES and (best is None or tm_c * tn_c / (tm_c + tn_c) > best[0]):
                best = (tm_c * tn_c / (tm_c + tn_c), tm_c, tn_c)
    if best is None:
        raise ValueError(f"{name}: no tiles for {m}x{n}x{k}")
    _, tm, tn = best
    if blocked:
        n_blk = N_DEV if mode == "nt" else tn // nb
    dims = {"nn": ((1,), (0,)), "nt": ((1,), (1,)), "tn": ((0,), (0,))}[mode]
    a_spec = pl.BlockSpec((k, tm), lambda i, j: (0, i)) if mode == "tn" else pl.BlockSpec((tm, k), lambda i, j: (i, 0))
    b_spec = pl.BlockSpec((tn, k), lambda i, j: (j, 0)) if mode == "nt" else pl.BlockSpec((k, tn), lambda i, j: (0, j))
    o_spec = pl.BlockSpec((tm, tn), lambda i, j: (i, j))
    out_shape = jax.ShapeDtypeStruct((m, n), out_dtype)
    if blocked and mode == "nn":
        b_spec = pl.BlockSpec((n_blk, k, nb), lambda i, j: (j, 0, 0))
    elif blocked and mode == "nt":
        b_spec = pl.BlockSpec((n_blk, tn, nb), lambda i, j: (0, j, 0))
    elif blocked:
        o_spec = pl.BlockSpec((n_blk, tm, nb), lambda i, j: (j, i, 0))
        out_shape = jax.ShapeDtypeStruct((N_DEV, m, nb), out_dtype)
    has_add = add is not None
    grid = (m // tm, n // tn)

    def body(*refs):
        (a_ref, b_ref, *rest), start, finish = _ride_hooks(ride, refs, 3 if has_add else 2, 1, grid)
        start()
        o_ref = rest[-1]
        if blocked and mode != "tn":
            bv = jnp.concatenate([b_ref[c] for c in range(n_blk)], axis=1) if n_blk > 1 else b_ref[0]
        else:
            bv = b_ref[...]
        total = lax.dot_general(a_ref[...], bv, (dims, ((), ())), preferred_element_type=F32)
        if has_add:
            total = total + rest[0][...]
        if blocked and mode == "tn":
            for c in range(n_blk):
                o_ref[c] = total[:, c * nb:(c + 1) * nb].astype(o_ref.dtype)
        else:
            o_ref[...] = total.astype(o_ref.dtype)
        finish()

    operands = (a, b, add) if has_add else (a, b)
    (out,), rode = _ride_call(ride, body, name, (out_shape,), grid, [a_spec, b_spec] + ([o_spec] if has_add else []),
                              (o_spec,), ("parallel", "parallel"), operands)
    return out if ride is None else (out, rode)


def _rms_fwd(x, win, gain, out_dtype, name, res=None, then=None):
    width, cidx = win
    s = x.shape[0]
    ts = min(s, 512)
    has_res, has_then = res is not None, then is not None

    def norm(v, g_ref):
        return (v * lax.rsqrt(jnp.mean(v * v, axis=-1, keepdims=True) + EPS)) * g_ref[...]

    def body(x_ref, g_ref, *rest):
        y = norm(x_ref[...].astype(F32), g_ref)
        if has_res:
            y = rest[0][...] + y
        o_ref = rest[-2] if has_then else rest[-1]
        o_ref[...] = y.astype(o_ref.dtype)
        if has_then:
            rest[-1][...] = norm(y, rest[-3]).astype(BF16)

    ops = (x, gain.reshape(1, width)) + ((res,) if has_res else ()) + ((then.reshape(1, width),) if has_then else ())
    out_shape = (jax.ShapeDtypeStruct((s, width), out_dtype),) + ((jax.ShapeDtypeStruct((s, width), BF16),) * has_then)
    out = pl.pallas_call(
        body, name=name, out_shape=out_shape, grid=(s // ts,),
        in_specs=([_rows(ts, width, cidx), _fixed((1, width))] + ([_rows(ts, width)] if has_res else [])
                  + ([_fixed((1, width))] if has_then else [])),
        out_specs=(_rows(ts, width),) * len(out_shape), compiler_params=_params("parallel"))(*ops)
    return out if has_then else out[0]


def _into(dz, n_inputs, out_index):
    return dict(in_specs=[ANY], operands=(dz,), input_output_aliases={n_inputs: out_index},
                out_shape=jax.ShapeDtypeStruct(dz.shape, dz.dtype))


def _rms_bwd(x, win, gain, dy, out_dtype, name, add=None, dz=None):
    width, cidx = win
    s = x.shape[0]
    ts = min(s, 512)
    has_add = add is not None

    def body(x_ref, g_ref, dy_ref, *rest):
        dx_ref, dg_ref = rest[-2], rest[-1]
        xv = x_ref[...].astype(F32)
        r = lax.rsqrt(jnp.mean(xv * xv, axis=-1, keepdims=True) + EPS)
        xh = xv * r
        dyv = dy_ref[...].astype(F32)
        dyg = dyv * g_ref[...]
        dx = r * (dyg - xh * jnp.mean(dyg * xh, axis=-1, keepdims=True))
        if has_add:
            dx = dx + rest[0][...]
        dx_ref[...] = dx.astype(dx_ref.dtype)

        @pl.when(pl.program_id(0) == 0)
        def _():
            dg_ref[...] = jnp.zeros_like(dg_ref)

        dg_ref[...] += jnp.sum(dyv * xh, axis=0, keepdims=True)

    ops = (x, gain.reshape(1, width), dy) + ((add,) if has_add else ())
    in_specs = [_rows(ts, width, cidx), _fixed((1, width)), _rows(ts, width)] + ([_rows(ts, width)] if has_add else [])
    dx_shape, dx_spec, alias = jax.ShapeDtypeStruct((s, width), out_dtype), _rows(ts, width), {}
    if dz is not None:
        into = _into(dz, len(ops), 0)
        ops, in_specs, alias = ops + into["operands"], in_specs + into["in_specs"], into["input_output_aliases"]
        dx_shape, dx_spec = into["out_shape"], _rows(ts, width, cidx)
    dx, dg = pl.pallas_call(
        body, name=name, out_shape=(dx_shape, jax.ShapeDtypeStruct((1, width), F32)), grid=(s // ts,),
        in_specs=in_specs, out_specs=(dx_spec, _fixed((1, width))), input_output_aliases=alias,
        compiler_params=_params("arbitrary"))(*ops)
    return dx, dg.reshape(width)


def _rope(x, c, s1, s2):
    return x * c + pltpu.roll(x, 16, 1) * s1 + pltpu.roll(x, LANES - 16, 1) * s2


def _rope_t(g, c, s1, s2):
    return g * c + pltpu.roll(g * s1, LANES - 16, 1) + pltpu.roll(g * s2, 16, 1)


def _rope_tables(positions):
    inv_freq = ROPE_THETA ** (-jnp.arange(0, QK_ROPE, 2, dtype=F32) / QK_ROPE)
    ang = positions.astype(F32)[:, None] * inv_freq
    cos, sin = jnp.cos(ang), jnp.sin(ang)
    n = positions.shape[0]
    one, zero = jnp.ones((n, 1), F32), jnp.zeros((n, 1), F32)
    c = jnp.concatenate([jnp.tile(one, (1, QK_NOPE)), cos, cos, jnp.tile(one, (1, 32))], axis=1)
    s1 = jnp.concatenate([jnp.tile(zero, (1, QK_NOPE + 16)), sin, jnp.tile(zero, (1, 32))], axis=1)
    s2 = jnp.concatenate([jnp.tile(zero, (1, QK_NOPE)), -sin, jnp.tile(zero, (1, 48))], axis=1)
    return c, s1, s2


def _rope_qk_fwd(qf, kf, z, tables, name):
    s = qf.shape[0]
    ts = min(s, 256)
    hw = N_HEADS * HEAD_PAD

    def body(qf_ref, kf_ref, kr_ref, c_ref, s1_ref, s2_ref, q_ref, k_ref):
        c, s1, s2 = c_ref[...], s1_ref[...], s2_ref[...]
        kr = _rope(kr_ref[...].astype(F32), c, s1, s2)
        for h in range(N_HEADS):
            sl = slice(h * HEAD_PAD, (h + 1) * HEAD_PAD)
            q_ref[:, sl] = _rope(qf_ref[:, sl], c, s1, s2).astype(BF16)
            k_ref[:, sl] = (kf_ref[:, sl] + kr).astype(BF16)

    tab = _rows(ts, LANES)
    return pl.pallas_call(
        body, name=name, out_shape=(jax.ShapeDtypeStruct((s, hw), BF16),) * 2, grid=(s // ts,),
        in_specs=[_rows(ts, hw), _rows(ts, hw), _rows(ts, *ZC_KR), tab, tab, tab],
        out_specs=(_rows(ts, hw), _rows(ts, hw)), compiler_params=_params("parallel"))(qf, kf, z, *tables)


def _rope_qk_bwd(dq, dk, tables, dz, name):
    s = dq.shape[0]
    ts = min(s, 256)
    hw = N_HEADS * HEAD_PAD

    def body(dq_ref, dk_ref, c_ref, s1_ref, s2_ref, _, dqf_ref, dkf_ref, dkr_ref):
        c, s1, s2 = c_ref[...], s1_ref[...], s2_ref[...]
        ksum = jnp.zeros((ts, HEAD_PAD), F32)
        for h in range(N_HEADS):
            sl = slice(h * HEAD_PAD, (h + 1) * HEAD_PAD)
            dqf_ref[:, sl] = _rope_t(dq_ref[:, sl], c, s1, s2).astype(BF16)
            dkh = dk_ref[:, sl]
            dkf_ref[:, sl] = dkh.astype(BF16)
            ksum = ksum + dkh
        lane = lax.broadcasted_iota(jnp.int32, (ts, HEAD_PAD), 1)
        in_rope = (lane >= QK_NOPE) & (lane < QK_NOPE + QK_ROPE)
        dkr_ref[...] = jnp.where(in_rope, _rope_t(ksum, c, s1, s2), 0.0).astype(BF16)

    tab = _rows(ts, LANES)
    into = _into(dz, 5, 2)
    return pl.pallas_call(
        body, name=name,
        out_shape=(jax.ShapeDtypeStruct((s, hw), BF16), jax.ShapeDtypeStruct((s, hw), BF16), into["out_shape"]),
        grid=(s // ts,), in_specs=[_rows(ts, hw), _rows(ts, hw), tab, tab, tab] + into["in_specs"],
        out_specs=(_rows(ts, hw), _rows(ts, hw), _rows(ts, *ZC_KR)), input_output_aliases=into["input_output_aliases"],
        compiler_params=_params("parallel"))(dq, dk, *tables, dz)


def _attn_tile(s):
    return min(s, 512)


def _raw_scores(q, k, masked, row0=0):
    sc = lax.dot_general(q, k, (((1,), (1,)), ((), ())), preferred_element_type=F32)
    if masked:
        rows = row0 + lax.broadcasted_iota(jnp.int32, sc.shape, 0)
        cols = lax.broadcasted_iota(jnp.int32, sc.shape, 1)
        sc = jnp.where(cols <= rows, sc, -jnp.inf)
    return sc


def _ride_hooks(ride, refs, n_in, n_out, grid):
    if ride is None:
        return refs, lambda: None, lambda: None
    n = len(ride.arrays)
    own = refs[:n_in] + refs[n_in + n:n_in + n + n_out]
    ins, outs, sems = refs[n_in:n_in + n], refs[n_in + n + n_out:n_in + 2 * n + n_out], refs[n_in + 2 * n + n_out:]
    at_first = functools.reduce(lambda a, b: a & b, [pl.program_id(ax) == 0 for ax in range(len(grid))])
    at_last = functools.reduce(lambda a, b: a & b, [pl.program_id(ax) == g - 1 for ax, g in enumerate(grid)])
    return own, lambda: pl.when(at_first)(lambda: ride.start(ins, outs, sems)), \
        lambda: pl.when(at_last)(lambda: ride.finish(ins, outs, sems))


def _ride_call(ride, body, name, out_shape, grid, in_specs, out_specs, semantics, operands):
    n = 0 if ride is None else len(ride.arrays)
    res = pl.pallas_call(
        body, name=name, out_shape=tuple(out_shape) + (tuple(ride.out_shape) if n else ()), grid=grid,
        in_specs=list(in_specs) + [ANY] * n, out_specs=tuple(out_specs) + (ANY,) * n,
        scratch_shapes=list(ride.scratch) if n else [],
        compiler_params=_params(*(("arbitrary",) * len(grid) if n else semantics)))(*operands, *(ride.arrays if n else ()))
    return res[:len(out_shape)], list(res[len(out_shape):])


def _flash_fwd(q, k, v, name, ride=None):
    s = q.shape[0]
    t = _attn_tile(s)
    c2 = ATTN_SCALE * LOG2E
    grid = (N_HEADS, s // t)

    def body(*refs):
        (q_ref, k_ref, v_ref, o_ref, lse_ref), start, finish = _ride_hooks(ride, refs, 3, 2, grid)
        start()
        i = pl.program_id(1)
        qv = q_ref[...]

        def chunk(j, carry, masked):
            m_old, l_old, acc = carry
            at = pl.ds(pl.multiple_of(j * t, t), t)
            sc = _raw_scores(qv, k_ref[at, :], masked)
            m_new = jnp.maximum(m_old, jnp.max(sc, axis=-1, keepdims=True))
            p = jnp.exp2((sc - m_new) * c2)
            alpha = jnp.exp2((m_old - m_new) * c2)
            l_new = alpha * l_old + jnp.sum(p, axis=-1, keepdims=True)
            acc = alpha * acc + jnp.dot(p.astype(BF16), v_ref[at, :], preferred_element_type=F32)
            return m_new, l_new, acc

        init = (jnp.full((t, 1), -jnp.inf, F32), jnp.zeros((t, 1), F32), jnp.zeros((t, HEAD_PAD), F32))
        carry = lax.fori_loop(0, i, lambda j, cr: chunk(j, cr, False), init)
        m_fin, l_fin, acc = chunk(i, carry, True)
        o_ref[...] = (acc / l_fin).astype(o_ref.dtype)
        lse_ref[...] = jnp.broadcast_to(m_fin * ATTN_SCALE + jnp.log(l_fin), (t, HEAD_PAD))
        finish()

    qo = pl.BlockSpec((t, HEAD_PAD), lambda h, i: (i, h))
    whole = pl.BlockSpec((s, HEAD_PAD), lambda h, i: (0, h))
    return _ride_call(
        ride, body, name, (jax.ShapeDtypeStruct(q.shape, BF16), jax.ShapeDtypeStruct(q.shape, F32)), grid,
        [qo, whole, whole], (qo, qo), ("parallel", "parallel"), (q, k, v))


def _attn_out_bwd(dya, w_attn_o, o, name):
    s, d = dya.shape
    hw = N_HEADS * HEAD_PAD
    t = _attn_tile(s)

    def body(d_ref, w_ref, o_ref, delta_ref, dob_ref):
        wv = jnp.concatenate([w_ref[c] for c in range(N_DEV)], axis=1)
        do = lax.dot_general(d_ref[...], wv, (((1,), (1,)), ((), ())), preferred_element_type=F32)
        for h in range(N_HEADS):
            sl = slice(h * HEAD_PAD, (h + 1) * HEAD_PAD)
            dov = do[:, sl]
            delta_ref[:, sl] = jnp.broadcast_to(jnp.sum(dov * o_ref[:, sl].astype(F32), axis=-1, keepdims=True),
                                                (t, HEAD_PAD))
            dob_ref[:, sl] = dov.astype(BF16)

    blk = _rows(t, hw)
    return pl.pallas_call(
        body, name=name, out_shape=(jax.ShapeDtypeStruct(o.shape, F32), jax.ShapeDtypeStruct(o.shape, BF16)),
        grid=(s // t,), in_specs=[_rows(t, d), _fixed(w_attn_o.shape), blk], out_specs=(blk, blk),
        compiler_params=_params("parallel"))(dya, w_attn_o, o)


def _flash_bwd(q, k, v, do, lse, delta, name, ride=None):
    s = q.shape[0]
    t = _attn_tile(s)
    nt = s // t
    c2 = ATTN_SCALE * LOG2E
    grid = (N_HEADS, nt)

    def body(*refs):
        (q_ref, k_ref, v_ref, do_ref, lse_ref, delta_ref, dq_ref, dk_ref, dv_ref), start, finish = _ride_hooks(
            ride, refs, 6, 3, grid)
        start()
        j = pl.program_id(1)
        kv, vv = k_ref[...], v_ref[...]

        @pl.when(j == 0)
        def _():
            dq_ref[...] = jnp.zeros_like(dq_ref)

        def chunk(i, carry, masked):
            dk_acc, dv_acc = carry
            at = pl.ds(pl.multiple_of(i * t, t), t)
            qi, doi = q_ref[at, :], do_ref[at, :]
            sc = _raw_scores(qi, kv, masked)
            p = jnp.exp2(sc * c2 - lse_ref[at, pl.ds(0, 1)] * LOG2E)
            dp = lax.dot_general(doi, vv, (((1,), (1,)), ((), ())), preferred_element_type=F32)
            ds = (p * (dp - delta_ref[at, pl.ds(0, 1)])).astype(BF16)
            dv_acc = dv_acc + lax.dot_general(p.astype(BF16), doi, (((0,), (0,)), ((), ())), preferred_element_type=F32)
            dk_acc = dk_acc + lax.dot_general(ds, qi, (((0,), (0,)), ((), ())), preferred_element_type=F32)
            dq_ref[at, :] += jnp.dot(ds, kv, preferred_element_type=F32) * ATTN_SCALE
            return dk_acc, dv_acc

        zero = jnp.zeros((t, HEAD_PAD), F32)
        carry = chunk(j, (zero, zero), True)
        dk_acc, dv_acc = lax.fori_loop(j + 1, nt, lambda i, cr: chunk(i, cr, False), carry)
        dk_ref[...] = dk_acc * ATTN_SCALE
        dv_ref[...] = dv_acc.astype(BF16)
        finish()

    blk = pl.BlockSpec((t, HEAD_PAD), lambda h, j: (j, h))
    whole = pl.BlockSpec((s, HEAD_PAD), lambda h, j: (0, h))
    return _ride_call(
        ride, body, name, (jax.ShapeDtypeStruct(q.shape, F32), jax.ShapeDtypeStruct(q.shape, F32),
                           jax.ShapeDtypeStruct(q.shape, BF16)), grid,
        [whole, blk, blk, whole, whole, whole], (whole, blk, blk), ("parallel", "arbitrary"), (q, k, v, do, lse, delta))


def _conv_tile(s):
    return min(s, 256)


def _halo_before(t, width, cidx):
    per = t // CONV_HALO
    return pl.BlockSpec((CONV_HALO, width), lambda i: (jnp.maximum(i * per - 1, 0), cidx))


def _halo_after(t, width, cidx, n_tiles):
    per = t // CONV_HALO
    last = n_tiles * per - 1
    return pl.BlockSpec((CONV_HALO, width), lambda i: (jnp.minimum((i + 1) * per, last), cidx))


def _fill_glu(hbuf, ap_ref, gp_ref, a_ref, g_ref, t):
    first = pl.program_id(0) == 0
    hbuf[pl.ds(0, CONV_HALO), :] = jnp.where(first, 0.0, ap_ref[...].astype(F32) * _sigmoid(gp_ref[...].astype(F32)))
    hbuf[pl.ds(CONV_HALO, t), :] = a_ref[...].astype(F32) * _sigmoid(g_ref[...].astype(F32))


def _phase_copies(dst, src, t):
    n = t + CONV_HALO - SUBLANES
    for s in range(1, SUBLANES):
        dst[s, pl.ds(0, n), :] = src[pl.ds(s, n), :]


def _window(phases, src, k, t):
    if k % SUBLANES == 0:
        return src[pl.ds(k, t), :]
    return phases[k % SUBLANES, pl.ds(k - k % SUBLANES, t), :]


def _layer_norm_parts(co):
    mu = jnp.mean(co, axis=-1, keepdims=True)
    xc = co - mu
    rstd = lax.rsqrt(jnp.mean(xc * xc, axis=-1, keepdims=True) + EPS)
    return xc * rstd, rstd


def _conv_fwd(z, conv_w, conv_b, ln_g, ln_b, name):
    s = z.shape[0]
    t = _conv_tile(s)
    off = CONV_HALO - (CONV_W - 1)

    def body(ap_ref, gp_ref, a_ref, g_ref, w_ref, b_ref, lg_ref, lb_ref, hc_ref, co_ref, hbuf, hph):
        _fill_glu(hbuf, ap_ref, gp_ref, a_ref, g_ref, t)
        _phase_copies(hph, hbuf, t)
        acc = jnp.zeros((t, CONV_C), F32) + b_ref[...]
        for j in range(CONV_W):
            acc = acc + _window(hph, hbuf, off + j, t) * w_ref[pl.ds(j, 1), :]
        co_ref[...] = acc
        xh, _ = _layer_norm_parts(acc)
        y = xh * lg_ref[...] + lb_ref[...]
        hc_ref[...] = (y * _sigmoid(y)).astype(BF16)

    vec = _fixed((1, CONV_C))
    return pl.pallas_call(
        body, name=name, out_shape=(jax.ShapeDtypeStruct((s, CONV_C), BF16), jax.ShapeDtypeStruct((s, CONV_C), F32)),
        grid=(s // t,),
        in_specs=[_halo_before(t, *ZC_CONV_A), _halo_before(t, *ZC_CONV_G), _rows(t, *ZC_CONV_A), _rows(t, *ZC_CONV_G),
                  _fixed((CONV_HALO, CONV_C)), vec, vec, vec],
        out_specs=(_rows(t, CONV_C), _rows(t, CONV_C)),
        scratch_shapes=[pltpu.VMEM((t + CONV_HALO, CONV_C), F32), pltpu.VMEM((SUBLANES, t + CONV_HALO, CONV_C), F32)],
        compiler_params=_params("parallel"))(z, z, z, z, conv_w, conv_b.reshape(1, -1), ln_g.reshape(1, -1),
                                             ln_b.reshape(1, -1))


def _conv_bwd_norm(dhc, co, ln_g, ln_b, name):
    s = co.shape[0]
    t = min(s, 512)

    def body(dhc_ref, co_ref, lg_ref, lb_ref, dco_ref, dg_ref, db_ref, dcb_ref):
        xh, rstd = _layer_norm_parts(co_ref[...])
        y = xh * lg_ref[...] + lb_ref[...]
        sg = _sigmoid(y)
        dy = dhc_ref[...] * (sg * (1.0 + y * (1.0 - sg)))
        dxh = dy * lg_ref[...]
        dco = rstd * (dxh - jnp.mean(dxh, axis=-1, keepdims=True) - xh * jnp.mean(dxh * xh, axis=-1, keepdims=True))
        dco_ref[...] = dco

        @pl.when(pl.program_id(0) == 0)
        def _():
            dg_ref[...] = jnp.zeros_like(dg_ref)
            db_ref[...] = jnp.zeros_like(db_ref)
            dcb_ref[...] = jnp.zeros_like(dcb_ref)

        dg_ref[...] += jnp.sum(dy * xh, axis=0, keepdims=True)
        db_ref[...] += jnp.sum(dy, axis=0, keepdims=True)
        dcb_ref[...] += jnp.sum(dco, axis=0, keepdims=True)

    vec = _fixed((1, CONV_C))
    one = jax.ShapeDtypeStruct((1, CONV_C), F32)
    dco, dg, db, dcb = pl.pallas_call(
        body, name=name, out_shape=(jax.ShapeDtypeStruct((s, CONV_C), F32), one, one, one), grid=(s // t,),
        in_specs=[_rows(t, CONV_C), _rows(t, CONV_C), vec, vec], out_specs=(_rows(t, CONV_C), vec, vec, vec),
        compiler_params=_params("arbitrary"))(dhc, co, ln_g.reshape(1, -1), ln_b.reshape(1, -1))
    return dco, dg.reshape(-1), db.reshape(-1), dcb.reshape(-1)


def _conv_bwd_taps(dco, z, conv_w, dz, name):
    s = z.shape[0]
    t = _conv_tile(s)
    nt = s // t
    off = CONV_HALO - (CONV_W - 1)

    def body(ap_ref, gp_ref, a_ref, g_ref, d_ref, dn_ref, w_ref, _, du_ref, dw_ref, hbuf, dbuf, hph, dph):
        i = pl.program_id(0)
        _fill_glu(hbuf, ap_ref, gp_ref, a_ref, g_ref, t)
        dbuf[pl.ds(0, t), :] = d_ref[...]
        dbuf[pl.ds(t, CONV_HALO), :] = jnp.where(i == nt - 1, 0.0, dn_ref[...])
        _phase_copies(hph, hbuf, t)
        _phase_copies(dph, dbuf, t)

        @pl.when(i == 0)
        def _():
            dw_ref[...] = jnp.zeros_like(dw_ref)

        dcur = d_ref[...]
        dh = jnp.zeros((t, CONV_C), F32)
        for j in range(CONV_W):
            dh = dh + _window(dph, dbuf, CONV_W - 1 - j, t) * w_ref[pl.ds(j, 1), :]
            dw_ref[pl.ds(j, 1), :] += jnp.sum(dcur * _window(hph, hbuf, off + j, t), axis=0, keepdims=True)
        a, sg = a_ref[...].astype(F32), _sigmoid(g_ref[...].astype(F32))
        du_ref[:, pl.ds(0, CONV_C)] = (dh * sg).astype(BF16)
        du_ref[:, pl.ds(CONV_C, CONV_C)] = (dh * a * sg * (1.0 - sg)).astype(BF16)

    into = _into(dz, 7, 0)
    return pl.pallas_call(
        body, name=name, out_shape=(into["out_shape"], jax.ShapeDtypeStruct((CONV_HALO, CONV_C), F32)), grid=(nt,),
        in_specs=[_halo_before(t, *ZC_CONV_A), _halo_before(t, *ZC_CONV_G), _rows(t, *ZC_CONV_A), _rows(t, *ZC_CONV_G),
                  _rows(t, CONV_C), _halo_after(t, CONV_C, 0, nt), _fixed((CONV_HALO, CONV_C))] + into["in_specs"],
        out_specs=(_rows(t, *ZC_CONV), _fixed((CONV_HALO, CONV_C))), input_output_aliases=into["input_output_aliases"],
        scratch_shapes=[pltpu.VMEM((t + CONV_HALO, CONV_C), F32), pltpu.VMEM((t + CONV_HALO, CONV_C), F32),
                        pltpu.VMEM((SUBLANES, t + CONV_HALO, CONV_C), F32),
                        pltpu.VMEM((SUBLANES, t + CONV_HALO, CONV_C), F32)],
        compiler_params=_params("arbitrary"))(z, z, z, z, dco, dco, conv_w, dz)


def _pool_tile(s):
    return min(s, 512)


def _pool_counts(row0, n, window):
    rows = row0 + lax.broadcasted_iota(jnp.int32, (n, POOL_GD), 0)
    return jnp.minimum(rows + 1, window).astype(F32)


def _pool_diff(ubuf, gi, window, row0, t):
    lanes = pl.ds(gi * POOL_GD, POOL_GD)
    tot = ubuf[pl.ds(CONV_HALO, t), lanes]
    cur = tot
    for back in range(1, window):
        tot = tot + ubuf[pl.ds(CONV_HALO - back, t), lanes]
    return tot / _pool_counts(row0, t, window) - cur


def _pool_fwd(z, pool_w, pool_scale, name):
    s = z.shape[0]
    t = _pool_tile(s)

    def body(up_ref, u_ref, w_ref, sc_ref, m_ref, ubuf):
        i = pl.program_id(0)
        ubuf[pl.ds(0, CONV_HALO), :] = jnp.where(i == 0, 0.0, up_ref[...].astype(F32))
        ubuf[pl.ds(CONV_HALO, t), :] = u_ref[...].astype(F32)
        for gi, window in enumerate(POOL_WINDOWS):
            d = _pool_diff(ubuf, gi, window, i * t, t)
            mm = jnp.dot(d.astype(BF16), w_ref[gi].astype(BF16), preferred_element_type=F32)
            lanes = pl.ds(gi * POOL_GD, POOL_GD)
            m_ref[:, lanes] = (mm * sc_ref[:, lanes]).astype(BF16)

    return pl.pallas_call(
        body, name=name, out_shape=jax.ShapeDtypeStruct((s, POOL_C), BF16), grid=(s // t,),
        in_specs=[_halo_before(t, *ZC_POOL), _rows(t, *ZC_POOL), _fixed((POOL_G, POOL_GD, POOL_GD)), _fixed((1, POOL_C))],
        out_specs=_rows(t, POOL_C), scratch_shapes=[pltpu.VMEM((t + CONV_HALO, POOL_C), F32)],
        compiler_params=_params("parallel"))(z, z, pool_w, pool_scale.reshape(1, -1))


def _pool_bwd(dm, z, pool_w, pool_scale, dz, name):
    s = z.shape[0]
    t = _pool_tile(s)
    nt = s // t

    def body(up_ref, u_ref, dm_ref, dmn_ref, w_ref, sc_ref, _, du_ref, dw_ref, dsc_ref, ubuf, ebuf):
        i = pl.program_id(0)
        ubuf[pl.ds(0, CONV_HALO), :] = jnp.where(i == 0, 0.0, up_ref[...].astype(F32))
        ubuf[pl.ds(CONV_HALO, t), :] = u_ref[...].astype(F32)

        @pl.when(i == 0)
        def _():
            dw_ref[...] = jnp.zeros_like(dw_ref)
            dsc_ref[...] = jnp.zeros_like(dsc_ref)

        dm_next = jnp.where(i == nt - 1, 0.0, dmn_ref[...])
        for gi, window in enumerate(POOL_WINDOWS):
            lanes = pl.ds(gi * POOL_GD, POOL_GD)
            wb = w_ref[gi].astype(BF16)
            scale = sc_ref[:, lanes]
            d = _pool_diff(ubuf, gi, window, i * t, t).astype(BF16)
            mm = jnp.dot(d, wb, preferred_element_type=F32)
            dmv = dm_ref[:, lanes]
            dsc_ref[:, lanes] += jnp.sum(dmv * mm, axis=0, keepdims=True)
            dmm = (dmv * scale).astype(BF16)
            dw_ref[gi] += lax.dot_general(d, dmm, (((0,), (0,)), ((), ())), preferred_element_type=F32)
            dd = lax.dot_general(dmm, wb, (((1,), (1,)), ((), ())), preferred_element_type=F32)
            dd_next = lax.dot_general((dm_next[:, gi * POOL_GD:(gi + 1) * POOL_GD] * scale).astype(BF16), wb,
                                      (((1,), (1,)), ((), ())), preferred_element_type=F32)
            ebuf[pl.ds(0, t), lanes] = dd / _pool_counts(i * t, t, window)
            ebuf[pl.ds(t, CONV_HALO), lanes] = dd_next / _pool_counts((i + 1) * t, CONV_HALO, window)
            du = -dd
            for ahead in range(window):
                du = du + ebuf[pl.ds(ahead, t), lanes]
            du_ref[:, lanes] = du.astype(BF16)

    into = _into(dz, 6, 0)
    du, dw, dsc = pl.pallas_call(
        body, name=name,
        out_shape=(into["out_shape"], jax.ShapeDtypeStruct((POOL_G, POOL_GD, POOL_GD), F32),
                   jax.ShapeDtypeStruct((1, POOL_C), F32)), grid=(nt,),
        in_specs=[_halo_before(t, *ZC_POOL), _rows(t, *ZC_POOL), _rows(t, POOL_C), _halo_after(t, POOL_C, 0, nt),
                  _fixed((POOL_G, POOL_GD, POOL_GD)), _fixed((1, POOL_C))] + into["in_specs"],
        out_specs=(_rows(t, *ZC_POOL), _fixed((POOL_G, POOL_GD, POOL_GD)), _fixed((1, POOL_C))),
        input_output_aliases=into["input_output_aliases"],
        scratch_shapes=[pltpu.VMEM((t + CONV_HALO, POOL_C), F32), pltpu.VMEM((t + CONV_HALO, POOL_C), F32)],
        compiler_params=_params("arbitrary"))(z, z, dm, dm, pool_w, pool_scale.reshape(1, -1), dz)
    return du, dw, dsc.reshape(-1)


def _gate_specs(ts):
    width, first = ZC_GATE
    return [_rows(ts, width, first + b) for b in range(3)]


def _merge_fwd(z, ys, name):
    s = z.shape[0]
    ts = min(s, 256)

    def body(g0, g1, g2, y0, y1, y2, o_ref):
        o_ref[...] = sum(_sigmoid(g[...].astype(F32)) * y[...].astype(F32)
                         for g, y in ((g0, y0), (g1, y1), (g2, y2))).astype(BF16)

    return pl.pallas_call(
        body, name=name, out_shape=jax.ShapeDtypeStruct((s, D_MODEL), BF16), grid=(s // ts,),
        in_specs=_gate_specs(ts) + [_rows(ts, D_MODEL)] * 3, out_specs=_rows(ts, D_MODEL),
        compiler_params=_params("parallel"))(z, z, z, *ys)


def _merge_bwd(z, ys, dmerged, name):
    s = z.shape[0]
    ts = min(s, 256)

    def body(g0, g1, g2, y0, y1, y2, dm_ref, dy0, dy1, dy2, dz_ref):
        dmv = dm_ref[...]
        for b, (g_ref, y_ref, dy_ref) in enumerate(((g0, y0, dy0), (g1, y1, dy1), (g2, y2, dy2))):
            sg = _sigmoid(g_ref[...].astype(F32))
            dy_ref[...] = (dmv * sg).astype(BF16)
            dz_ref[:, pl.ds(b * D_MODEL, D_MODEL)] = (dmv * y_ref[...].astype(F32) * sg * (1.0 - sg)).astype(BF16)

    out = jax.ShapeDtypeStruct((s, D_MODEL), BF16)
    return pl.pallas_call(
        body, name=name, out_shape=(out,) * 3 + (jax.ShapeDtypeStruct((s, Z_W), BF16),), grid=(s // ts,),
        in_specs=_gate_specs(ts) + [_rows(ts, D_MODEL)] * 4,
        out_specs=(_rows(ts, D_MODEL),) * 3 + (_rows(ts, *ZC_GATES),),
        compiler_params=_params("parallel"))(z, z, z, *ys, dmerged)


def _ffn_up_fwd(h, w_gate, w_up, name):
    s, d = h.shape
    nb = w_gate.shape[2]
    f = N_DEV * nb
    tm, n_blk = min(s, 1024), 2
    tn = n_blk * nb
    blk = pl.BlockSpec((tm, tn), lambda i, j: (i, j))
    wspec = pl.BlockSpec((n_blk, d, nb), lambda i, j: (j, 0, 0))

    def body(h_ref, wg_ref, wu_ref, hg_ref, hu_ref, act_ref):
        hv = h_ref[...]
        g = jnp.dot(hv, jnp.concatenate([wg_ref[c] for c in range(n_blk)], axis=1), preferred_element_type=F32)
        u = jnp.dot(hv, jnp.concatenate([wu_ref[c] for c in range(n_blk)], axis=1), preferred_element_type=F32)
        hg_ref[...] = g.astype(hg_ref.dtype)
        hu_ref[...] = u.astype(hu_ref.dtype)
        act_ref[...] = (g * _sigmoid(g) * u).astype(BF16)

    return pl.pallas_call(
        body, name=name,
        out_shape=(jax.ShapeDtypeStruct((s, f), BF16),) * 3,
        grid=(s // tm, f // tn), in_specs=[pl.BlockSpec((tm, d), lambda i, j: (i, 0)), wspec, wspec],
        out_specs=(blk, blk, blk), compiler_params=_params("parallel", "parallel"))(h, w_gate, w_up)


def _ffn_down_bwd(dfo, w_down, hg, hu, name):
    s, d = dfo.shape
    f = w_down.shape[0]
    tm, tn = min(s, 1024), _tile(f, 1024)
    blk = pl.BlockSpec((tm, tn), lambda i, j: (i, j))

    def body(d_ref, w_ref, g_ref, u_ref, dg_ref, du_ref):
        dact = lax.dot_general(d_ref[...], w_ref[...], (((1,), (1,)), ((), ())), preferred_element_type=F32)
        g = g_ref[...].astype(F32)
        sg = _sigmoid(g)
        dg_ref[...] = (dact * u_ref[...].astype(F32) * (sg * (1.0 + g * (1.0 - sg)))).astype(BF16)
        du_ref[...] = (dact * g * sg).astype(BF16)

    out = jax.ShapeDtypeStruct((s, f), BF16)
    return pl.pallas_call(
        body, name=name, out_shape=(out, out), grid=(s // tm, f // tn),
        in_specs=[pl.BlockSpec((tm, d), lambda i, j: (i, 0)), pl.BlockSpec((tn, d), lambda i, j: (j, 0)), blk, blk],
        out_specs=(blk, blk), compiler_params=_params("parallel", "parallel"))(dfo, w_down, hg, hu)


def _loss_grad(y, target, name):
    s, d = y.shape
    ts = min(s, 512)

    def body(y_ref, t_ref, dy_ref, sq_ref):
        e = y_ref[...] - t_ref[...]
        dy_ref[...] = e / d

        @pl.when(pl.program_id(0) == 0)
        def _():
            sq_ref[...] = jnp.zeros_like(sq_ref)

        sq_ref[...] += jnp.sum(e * e, axis=0, keepdims=True)

    return pl.pallas_call(
        body, name=name, out_shape=(jax.ShapeDtypeStruct((s, d), F32), jax.ShapeDtypeStruct((1, d), F32)),
        grid=(s // ts,), in_specs=[_rows(ts, d), _rows(ts, d)], out_specs=(_rows(ts, d), _fixed((1, d))),
        compiler_params=_params("arbitrary"))(y, target)


def _adamw(w, g, m, v, name):
    shape = w.shape
    cols = shape[-1]
    keep3 = w.ndim == 3 and shape[1] < SUBLANES
    view = shape if keep3 else (math.prod(shape[:-1]), cols)
    rows = view[0]
    if keep3:
        cap = max(1, (1 << 20) // (SUBLANES * cols * 4))
        tr = max(t for t in range(1, cap + 1) if rows % t == 0)
    else:
        tr = _row_tile(rows, cols * 4)

    def body(w_ref, g_ref, m_ref, v_ref, d_ref, mo_ref, vo_ref):
        gv = g_ref[...]
        mn = B1 * m_ref[...] + (1.0 - B1) * gv
        vn = B2 * v_ref[...] + (1.0 - B2) * (gv * gv)
        m_hat = mn / (1.0 - B1 ** STEP)
        v_hat = vn / (1.0 - B2 ** STEP)
        d_ref[...] = -LR * (m_hat / (jnp.sqrt(v_hat) + ADAM_EPS) + WD * w_ref[...])
        mo_ref[...] = mn
        vo_ref[...] = vn

    spec = pl.BlockSpec((tr,) + view[1:], lambda i: (i,) + (0,) * (len(view) - 1))
    out = jax.ShapeDtypeStruct(view, F32)
    res = pl.pallas_call(
        body, name=name, out_shape=(out,) * 3, grid=(rows // tr,), in_specs=[spec] * 4, out_specs=(spec,) * 3,
        compiler_params=_params("parallel"))(*[t.reshape(view) for t in (w, g, m, v)])
    return tuple(r.reshape(shape) for r in res)


LANE_MAJOR = ("w_uq", "w_uk", "w_uv", "w_gate", "w_up")


def _lane_major(name, a):
    if name == "w_in":
        return a.transpose(2, 0, 1)
    if name in LANE_MAJOR:
        return a.transpose(0, 2, 1)
    return a


def _from_lane_major(name, a):
    if name == "w_in":
        return a.transpose(1, 2, 0)
    return _lane_major(name, a)


ANY = pl.BlockSpec(memory_space=pl.ANY)


class _GatherRide:
    def __init__(self, arrays):
        n = len(arrays)
        self.arrays = list(arrays)
        self.out_shape = [jax.ShapeDtypeStruct((N_DEV,) + a.shape, a.dtype) for a in arrays]
        self.scratch = [pltpu.SemaphoreType.DMA((n, 7)), pltpu.SemaphoreType.DMA((n, 7)), pltpu.SemaphoreType.DMA((n,))]

    def _copies(self, ins, outs, sems):
        send_sems, recv_sems, local_sems = sems
        n = len(self.arrays)
        x, y, c = lax.axis_index("x"), lax.axis_index("y"), lax.axis_index("c")
        me, sibling = (x, y, c), (x, y, 1 - c)
        chips = [(1 - x, y), (x, 1 - y), (1 - x, 1 - y)]

        def slot(a, px, py, pc):
            return outs[a].at[4 * px + 2 * py + pc]

        def copy(a, k, block, to, src=None):
            return pltpu.make_async_remote_copy(
                src_ref=slot(a, *block) if src is None else src, dst_ref=slot(a, *block), send_sem=send_sems.at[a, k],
                recv_sem=recv_sems.at[a, k], device_id=to, device_id_type=MESH)

        mine = [pltpu.make_async_copy(ins[a], slot(a, *me), local_sems.at[a]) for a in range(n)]
        first = []
        for a in range(n):
            first.append(copy(a, 0, me, sibling, src=ins[a]))
            first += [copy(a, 1 + j, me, (*chip, c), src=ins[a]) for j, chip in enumerate(chips)]
        return n, me, sibling, chips, c, copy, mine, first

    def start(self, ins, outs, sems):
        _, _, _, _, _, _, mine, first = self._copies(ins, outs, sems)
        for cp in mine + first:
            cp.start()

    def finish(self, ins, outs, sems):
        n, me, sibling, chips, c, copy, mine, first = self._copies(ins, outs, sems)
        passed = []
        for j, chip in enumerate(chips):
            for a in range(n):
                copy(a, 1 + j, (*chip, c), me).wait_recv()
                passed.append(copy(a, 4 + j, (*chip, c), sibling))
                passed[-1].start()
        for a in range(n):
            copy(a, 0, sibling, me).wait_recv()
            for j, chip in enumerate(chips):
                copy(a, 4 + j, (*chip, 1 - c), me).wait_recv()
        for cp in first + passed:
            cp.wait_send()
        for cp in mine:
            cp.wait()


class _ReduceRide:
    def __init__(self, arrays):
        n = len(arrays)
        self.arrays = list(arrays)
        self.out_shape = [jax.ShapeDtypeStruct(a.shape, a.dtype) for a in arrays]
        self.scratch = [pltpu.SemaphoreType.DMA((n, 7)), pltpu.SemaphoreType.DMA((n, 7)), pltpu.SemaphoreType.DMA((n,))]

    def _copies(self, ins, outs, sems):
        send_sems, recv_sems, local_sems = sems
        n = len(self.arrays)
        x, y, c = lax.axis_index("x"), lax.axis_index("y"), lax.axis_index("c")
        mine = [pltpu.make_async_copy(ins[a].at[4 * x + 2 * y + c], outs[a].at[0], local_sems.at[a]) for a in range(n)]
        copies = []
        for a in range(n):
            for k in range(1, N_DEV):
                px = 1 - x if k & 4 else x
                py = 1 - y if k & 2 else y
                pc = 1 - c if k & 1 else c
                copies.append(pltpu.make_async_remote_copy(
                    src_ref=ins[a].at[4 * px + 2 * py + pc], dst_ref=outs[a].at[k], send_sem=send_sems.at[a, k - 1],
                    recv_sem=recv_sems.at[a, k - 1], device_id=(px, py, pc), device_id_type=MESH))
        return mine, copies

    def start(self, ins, outs, sems):
        mine, copies = self._copies(ins, outs, sems)
        for cp in mine + copies:
            cp.start()

    def finish(self, ins, outs, sems):
        mine, copies = self._copies(ins, outs, sems)
        for cp in copies + mine:
            cp.wait()


def _run_ride(ride, name):
    n = len(ride.arrays)

    def body(*refs):
        ins, outs, sems = refs[:n], refs[n:2 * n], refs[2 * n:]
        ride.start(ins, outs, sems)
        ride.finish(ins, outs, sems)

    return pl.pallas_call(body, name=name, out_shape=ride.out_shape, in_specs=[ANY] * n, out_specs=[ANY] * n,
                          scratch_shapes=ride.scratch)(*ride.arrays)


def _all_gather(arrays, name):
    return _run_ride(_GatherRide(arrays), name)


def _swap_with_sibling(arrays, name):
    n = len(arrays)

    def body(*refs):
        ins, outs = refs[:n], refs[n:2 * n]
        send_sems, recv_sems = refs[2 * n:]
        x, y, c = lax.axis_index("x"), lax.axis_index("y"), lax.axis_index("c")
        copies = [pltpu.make_async_remote_copy(
            src_ref=ins[a].at[1 - c], dst_ref=outs[a], send_sem=send_sems.at[a], recv_sem=recv_sems.at[a],
            device_id=(x, y, 1 - c), device_id_type=MESH) for a in range(n)]
        for cp in copies:
            cp.start()
        for cp in copies:
            cp.wait()

    return pl.pallas_call(
        body, name=name, out_shape=[jax.ShapeDtypeStruct(a.shape[1:], a.dtype) for a in arrays],
        in_specs=[ANY] * n, out_specs=[ANY] * n,
        scratch_shapes=[pltpu.SemaphoreType.DMA((n,)), pltpu.SemaphoreType.DMA((n,))])(*arrays)


class _ChipExchangeRide:
    def __init__(self, arrays):
        n = len(arrays)
        self.arrays = list(arrays)
        self.out_shape = [jax.ShapeDtypeStruct(a.shape, a.dtype) for a in arrays]
        self.scratch = [pltpu.SemaphoreType.DMA((n, 3)), pltpu.SemaphoreType.DMA((n, 3)), pltpu.SemaphoreType.DMA((n,))]

    def _copies(self, ins, outs, sems):
        send_sems, recv_sems, local_sems = sems
        n = len(self.arrays)
        x, y, c = lax.axis_index("x"), lax.axis_index("y"), lax.axis_index("c")
        partners = [(x, 1 - y), (1 - x, y), (1 - x, 1 - y)]
        mine = [pltpu.make_async_copy(ins[a].at[2 * x + y], outs[a].at[0], local_sems.at[a]) for a in range(n)]
        copies = [pltpu.make_async_remote_copy(
            src_ref=ins[a].at[2 * px + py], dst_ref=outs[a].at[1 + k], send_sem=send_sems.at[a, k],
            recv_sem=recv_sems.at[a, k], device_id=(px, py, c), device_id_type=MESH)
            for a in range(n) for k, (px, py) in enumerate(partners)]
        return mine, copies

    def start(self, ins, outs, sems):
        mine, copies = self._copies(ins, outs, sems)
        for cp in mine + copies:
            cp.start()

    def finish(self, ins, outs, sems):
        mine, copies = self._copies(ins, outs, sems)
        for cp in copies + mine:
            cp.wait()


class _Combo:
    def __init__(self, rides):
        self.rides = rides
        self.arrays = [a for r in rides for a in r.arrays]
        self.out_shape = [o for r in rides for o in r.out_shape]
        self.scratch = [sc for r in rides for sc in r.scratch]

    def _parts(self, ins, outs, sems):
        at_a = at_s = 0
        for r in self.rides:
            na, ns = len(r.arrays), len(r.scratch)
            yield r, ins[at_a:at_a + na], outs[at_a:at_a + na], sems[at_s:at_s + ns]
            at_a, at_s = at_a + na, at_s + ns

    def start(self, ins, outs, sems):
        for r, i, o, sm in self._parts(ins, outs, sems):
            r.start(i, o, sm)

    def finish(self, ins, outs, sems):
        for r, i, o, sm in self._parts(ins, outs, sems):
            r.finish(i, o, sm)


def _as_rows(a, lead):
    return a.reshape(a.shape[:lead] + (math.prod(a.shape[lead:-1]), a.shape[-1]))


def _add_pairs(a, b, name):
    a2, b2 = _as_rows(a, 0), _as_rows(b, 0)
    rows, cols = a2.shape
    tr = _row_tile(rows, cols * 4)

    def body(a_ref, b_ref, o_ref):
        o_ref[...] = (a_ref[...].astype(F32) + b_ref[...].astype(F32)).astype(o_ref.dtype)

    spec = _rows(tr, cols)
    out = pl.pallas_call(body, name=name, out_shape=jax.ShapeDtypeStruct(a2.shape, a.dtype), grid=(rows // tr,),
                         in_specs=[spec, spec], out_specs=spec, compiler_params=_params("parallel"))(a2, b2)
    return out.reshape(a.shape)


def _sum_blocks(a, name):
    a3 = _as_rows(a, 1)
    n, rows, cols = a3.shape
    tr = _row_tile(rows, n * cols * 4)

    def body(a_ref, o_ref):
        tot = a_ref[0].astype(F32)
        for k in range(1, n):
            tot = tot + a_ref[k].astype(F32)
        o_ref[...] = tot

    out = pl.pallas_call(body, name=name, out_shape=jax.ShapeDtypeStruct((rows, cols), F32), grid=(rows // tr,),
                         in_specs=[pl.BlockSpec((n, tr, cols), lambda j: (0, j, 0))], out_specs=_rows(tr, cols),
                         compiler_params=_params("parallel"))(a3)
    return out.reshape(a.shape[1:])


MIX_GROUPS = ("w_in", "w_uq", "w_uk", "w_uv", "w_attn_o", "w_conv_o", "w_pool_o", "w_mix_o")
FFN_GROUPS = ("w_gate", "w_up", "w_down")
MIX_EARLY = ("w_attn_o", "w_conv_o", "w_pool_o", "w_mix_o")
MIX_LATE = ("w_in", "w_uq", "w_uk", "w_uv")


def _pad_axis(a, axis, size):
    pad = [(0, 0)] * a.ndim
    pad[axis] = (0, size - a.shape[axis])
    return jnp.pad(a, pad)


def _local_groups(sh, l):
    out = {n: sh[n][l] for n in BIG}
    for n in ("w_uq", "w_uk", "w_uv"):
        out[n] = _pad_axis(out[n], -1, HEAD_PAD)
    for n in ("w_gate", "w_up"):
        out[n] = _pad_axis(out[n], -1, FF_SHARD_PAD)
    out["w_down"] = _pad_axis(out["w_down"], 0, FF_SHARD_PAD)
    return {n: v.astype(BF16) for n, v in out.items()}


def _arrange_w_in(blocks):
    parts, pos = [], 0
    for ref_lo, ref_hi, at in sorted(W_IN_PIECES, key=lambda p: p[2]):
        if at > pos:
            parts.append(jnp.zeros((blocks.shape[1], at - pos), blocks.dtype))
        for d in range(N_DEV):
            lo, hi = max(ref_lo, d * W_IN_SHARD), min(ref_hi, (d + 1) * W_IN_SHARD)
            if lo < hi:
                parts.append(blocks[d][:, lo - d * W_IN_SHARD:hi - d * W_IN_SHARD])
        pos = at + ref_hi - ref_lo
    if pos < Z_W:
        parts.append(jnp.zeros((blocks.shape[1], Z_W - pos), blocks.dtype))
    return jnp.concatenate(parts, axis=1)


def _w_in_shard(g, d):
    parts = []
    for ref_lo, ref_hi, at in W_IN_PIECES:
        lo, hi = max(ref_lo, d * W_IN_SHARD), min(ref_hi, (d + 1) * W_IN_SHARD)
        if lo < hi:
            parts.append(g[:, at + lo - ref_lo:at + hi - ref_lo])
    return jnp.concatenate(parts, axis=1)


def _mixer_weights(gat):
    w = {n: v for n, v in gat.items() if n != "w_in"}
    attn_o = gat["w_attn_o"].reshape(N_DEV, N_HEADS, V_HEAD, LANES)
    w["w_attn_o"] = _pad_axis(attn_o, 2, HEAD_PAD).reshape(N_DEV, N_HEADS * HEAD_PAD, LANES)
    w["w_mix_o"] = gat["w_mix_o"].reshape(D_MODEL, D_MODEL)
    return w


def _ffn_weights(gat):
    return {"w_gate": gat["w_gate"], "w_up": gat["w_up"], "w_down": gat["w_down"].reshape(D_FF_PAD, D_MODEL)}


def _mixer_grad_groups(gb):
    g = dict(gb)
    if "w_in" in gb:
        g["w_in"] = jnp.stack([_w_in_shard(gb["w_in"], d) for d in range(N_DEV)])
    if "w_attn_o" in gb:
        attn_o = gb["w_attn_o"].reshape(N_DEV, N_HEADS, HEAD_PAD, LANES)[:, :, :V_HEAD]
        g["w_attn_o"] = attn_o.reshape(N_DEV, N_HEADS * V_HEAD, LANES)
    if "w_mix_o" in gb:
        g["w_mix_o"] = gb["w_mix_o"].reshape(N_DEV, D_MODEL // N_DEV, D_MODEL)
    return g


def _ffn_grad_groups(gb):
    return {"w_gate": gb["w_gate"], "w_up": gb["w_up"], "w_down": gb["w_down"].reshape(N_DEV, FF_SHARD_PAD, D_MODEL)}


def _grads_from_groups(tot):
    g = dict(tot)
    g["w_uq"] = tot["w_uq"][:, :QK_NOPE + QK_ROPE]
    g["w_uk"], g["w_uv"] = tot["w_uk"][:, :QK_NOPE], tot["w_uv"][:, :V_HEAD]
    g["w_gate"], g["w_up"] = tot["w_gate"][:, :FF_SHARD], tot["w_up"][:, :FF_SHARD]
    g["w_down"] = tot["w_down"][:FF_SHARD]
    return g


SMALL_GROUPS = (
    (D_MODEL, ("mix_norm_pre", "mix_norm_post", "ffn_norm_pre", "ffn_norm_post")),
    (CONV_C, ("conv_w", "conv_b", "conv_ln_g", "conv_ln_b", "pool_scale")),
    (Q_RANK, ("q_norm",)), (KV_RANK, ("kv_norm",)), (POOL_GD, ("pool_w",)),
)


def _small_rows(name):
    return {"conv_w": CONV_HALO, "pool_w": POOL_G * POOL_GD}.get(name, SUBLANES)


def _small_groups(small):
    out = []
    for width, names in SMALL_GROUPS:
        parts = []
        for l in range(DEPTH):
            for n in names:
                part = small[l][n].reshape(-1, width)
                parts.append(_pad_axis(part, 0, _small_rows(n)))
        out.append(jnp.concatenate(parts, axis=0))
    return out


def _small_from_groups(groups):
    shapes = {"conv_w": (CONV_W, CONV_C), "pool_w": (POOL_G, POOL_GD, POOL_GD)}
    out = {}
    for (width, names), g in zip(SMALL_GROUPS, groups):
        row = 0
        for l in range(DEPTH):
            for n in names:
                rows = _small_rows(n)
                real = {"conv_w": CONV_W, "pool_w": POOL_G * POOL_GD}.get(n, 1)
                out.setdefault(n, []).append(g[row:row + real].reshape(shapes.get(n, (width,))))
                row += rows
    return {n: jnp.stack(v) for n, v in out.items()}


def _mixer_fwd(x, h, tables, sm, plan, l):
    nm = lambda n: f"{n}_l{l}"
    if h is None:
        h = _rms_fwd(x, (D_MODEL, 0), sm["mix_norm_pre"], BF16, nm("mix_pre_norm"))
    w_in, ride = plan.w_in(l), plan.in_proj_ride(l)
    if ride is None:
        z = _matmul(h, w_in, "nn", BF16, nm("in_proj"))
    else:
        z, rode = _matmul(h, w_in, "nn", BF16, nm("in_proj"), ride=ride)
        plan.in_proj_done(l, rode)
    w = dict(plan.mixer_weights(l), w_in=w_in)
    cq = _rms_fwd(z, ZC_Q, sm["q_norm"], BF16, nm("q_norm"))
    ckv = _rms_fwd(z, ZC_KV, sm["kv_norm"], BF16, nm("kv_norm"))
    qf = _matmul(cq, w["w_uq"], "nn", F32, nm("q_up"))
    kf = _matmul(ckv, w["w_uk"], "nn", F32, nm("k_up"))
    v = _matmul(ckv, w["w_uv"], "nn", BF16, nm("v_up"))
    q, k = _rope_qk_fwd(qf, kf, z, tables, nm("rope_qk"))
    (o, lse), rode = _flash_fwd(q, k, v, nm("flash_fwd"), plan.fwd_ride(l))
    plan.fwd_done(l, rode)
    y_attn = _matmul(o, w["w_attn_o"], "nn", BF16, nm("attn_out"))
    hc, co = _conv_fwd(z, sm["conv_w"], sm["conv_b"], sm["conv_ln_g"], sm["conv_ln_b"], nm("conv_fwd"))
    y_conv = _matmul(hc, w["w_conv_o"], "nn", BF16, nm("conv_out"))
    pm = _pool_fwd(z, sm["pool_w"], sm["pool_scale"], nm("pool_fwd"))
    y_pool = _matmul(pm, w["w_pool_o"], "nn", BF16, nm("pool_out"))
    ys = (y_attn, y_conv, y_pool)
    merged = _merge_fwd(z, ys, nm("merge_fwd"))
    mo = _matmul(merged, w["w_mix_o"], "nn", F32, nm("mix_out"))
    x_mid, h2 = _rms_fwd(mo, (D_MODEL, 0), sm["mix_norm_post"], F32, nm("mix_post_norm"), res=x, then=sm["ffn_norm_pre"])
    saved = dict(x=x, h=h, z=z, cq=cq, ckv=ckv, q=q, k=k, v=v, o=o, lse=lse, hc=hc, co=co, pm=pm, ys=ys, merged=merged,
                 mo=mo)
    return x_mid, h2, saved, w


def _ffn_fwd(x_mid, h2, w, sm, tag, next_gain):
    nm = lambda n: f"{n}_{tag}"
    hg, hu, act = _ffn_up_fwd(h2, w["w_gate"], w["w_up"], nm("ffn_up_fwd"))
    fo = _matmul(act, w["w_down"], "nn", F32, nm("ffn_down"))
    out = _rms_fwd(fo, (D_MODEL, 0), sm["ffn_norm_post"], F32, nm("ffn_post_norm"), res=x_mid, then=next_gain)
    out, h_next = out if next_gain is not None else (out, None)
    saved = dict(x_mid=x_mid, h2=h2, hg=hg, hu=hu, act=act, fo=fo)
    return out, h_next, saved


def _ffn_bwd(dout, sv, w, sm, tag):
    nm = lambda n: f"{n}_{tag}"
    gb, gs = {}, {}
    dfo, gs["ffn_norm_post"] = _rms_bwd(sv["fo"], (D_MODEL, 0), sm["ffn_norm_post"], dout, BF16, nm("ffn_post_norm_bwd"))
    gb["w_down"] = _matmul(sv["act"], dfo, "tn", BF16, nm("ffn_down_dw"))
    dhg, dhu = _ffn_down_bwd(dfo, w["w_down"], sv["hg"], sv["hu"], nm("ffn_down_bwd"))
    dh2_g = _matmul(dhg, w["w_gate"], "nt", F32, nm("ffn_gate_dx"))
    dh2 = _matmul(dhu, w["w_up"], "nt", F32, nm("ffn_up_dx"), add=dh2_g)
    gb["w_gate"] = _matmul(sv["h2"], dhg, "tn", BF16, nm("ffn_gate_dw"), blocked=True)
    gb["w_up"] = _matmul(sv["h2"], dhu, "tn", BF16, nm("ffn_up_dw"), blocked=True)
    dmid, gs["ffn_norm_pre"] = _rms_bwd(sv["x_mid"], (D_MODEL, 0), sm["ffn_norm_pre"], dh2, F32, nm("ffn_pre_norm_bwd"),
                                        add=dout)
    return dmid, gb, gs


def _mixer_bwd(dmid, sv, tables, w, sm, plan, l, pack_small):
    nm = lambda n: f"{n}_l{l}"
    gb, gs = {}, {}
    dmo, gs["mix_norm_post"] = _rms_bwd(sv["mo"], (D_MODEL, 0), sm["mix_norm_post"], dmid, BF16, nm("mix_post_norm_bwd"))
    dmerged = _matmul(dmo, w["w_mix_o"], "nt", F32, nm("mix_out_dx"))
    gb["w_mix_o"] = _matmul(sv["merged"], dmo, "tn", BF16, nm("mix_out_dw"))
    dya, dyc, dyp, dz = _merge_bwd(sv["z"], sv["ys"], dmerged, nm("merge_bwd"))
    dpm = _matmul(dyp, w["w_pool_o"], "nt", F32, nm("pool_out_dx"))
    gb["w_pool_o"] = _matmul(sv["pm"], dyp, "tn", BF16, nm("pool_out_dw"), blocked=True)
    dz, gs["pool_w"], gs["pool_scale"] = _pool_bwd(dpm, sv["z"], sm["pool_w"], sm["pool_scale"], dz, nm("pool_bwd"))
    dhc = _matmul(dyc, w["w_conv_o"], "nt", F32, nm("conv_out_dx"))
    gb["w_conv_o"] = _matmul(sv["hc"], dyc, "tn", BF16, nm("conv_out_dw"), blocked=True)
    dco, gs["conv_ln_g"], gs["conv_ln_b"], gs["conv_b"] = _conv_bwd_norm(dhc, sv["co"], sm["conv_ln_g"], sm["conv_ln_b"],
                                                                        nm("conv_bwd_norm"))
    dz, gs["conv_w"] = _conv_bwd_taps(dco, sv["z"], sm["conv_w"], dz, nm("conv_bwd_taps"))
    gb["w_attn_o"] = _matmul(sv["o"], dya, "tn", BF16, nm("attn_out_dw"), blocked=True)
    delta, dob = _attn_out_bwd(dya, w["w_attn_o"], sv["o"], nm("attn_out_bwd"))
    (dq, dk, dv), rode = _flash_bwd(sv["q"], sv["k"], sv["v"], dob, sv["lse"], delta, nm("flash_bwd"),
                                  plan.bwd_ride(l, gb))
    plan.bwd_done(l, rode)
    dqf, dkf, dz = _rope_qk_bwd(dq, dk, tables, dz, nm("rope_qk_bwd"))
    dcq_n = _matmul(dqf, w["w_uq"], "nt", F32, nm("q_up_dx"))
    gb["w_uq"] = _matmul(sv["cq"], dqf, "tn", BF16, nm("q_up_dw"), blocked=True)
    dckv_k = _matmul(dkf, w["w_uk"], "nt", F32, nm("k_up_dx"))
    dckv_n = _matmul(dv, w["w_uv"], "nt", F32, nm("v_up_dx"), add=dckv_k)
    gb["w_uk"] = _matmul(sv["ckv"], dkf, "tn", BF16, nm("k_up_dw"), blocked=True)
    gb["w_uv"] = _matmul(sv["ckv"], dv, "tn", BF16, nm("v_up_dw"), blocked=True)
    dz, gs["q_norm"] = _rms_bwd(sv["z"], ZC_Q, sm["q_norm"], dcq_n, BF16, nm("q_norm_bwd"), dz=dz)
    dz, gs["kv_norm"] = _rms_bwd(sv["z"], ZC_KV, sm["kv_norm"], dckv_n, BF16, nm("kv_norm_bwd"), dz=dz)
    gb["w_in"] = _matmul(sv["h"], dz, "tn", BF16, nm("in_proj_dw"))
    plan.add_grads(l, "mix", gb)
    ride, small_gathered = plan.tail_ride(l, pack_small(gs)), []
    if ride is None:
        dh = _matmul(dz, w["w_in"], "nt", F32, nm("in_proj_dx"))
    else:
        dh, rode = _matmul(dz, w["w_in"], "nt", F32, nm("in_proj_dx"), ride=ride)
        small_gathered = plan.tail_done(l, rode)
    dx, gs["mix_norm_pre"] = _rms_bwd(sv["x"], (D_MODEL, 0), sm["mix_norm_pre"], dh, F32, nm("mix_pre_norm_bwd"), add=dmid)
    return dx, gs, small_gathered


def _part_groups(part):
    return {"mix": MIX_GROUPS, "ffn": FFN_GROUPS, "early": MIX_EARLY, "late": MIX_LATE}[part]


class _Plan:
    def __init__(self, shards, conv_w):
        self.local = [_local_groups(shards, l) for l in range(DEPTH)]
        self.conv_w = conv_w
        self.gat, self.send, self.recv = {}, {}, {}

    @staticmethod
    def _riders(l):
        return [(l, "ffn")] + ([(l + 1, "mix")] if l + 1 < DEPTH else [])

    @staticmethod
    def _grad_riders(l):
        return [(l, "ffn"), (l, "early")] + ([(l + 1, "late")] if l + 1 < DEPTH else [])

    def gather_first(self):
        w_in, conv_w = _all_gather([self.local[0]["w_in"], self.conv_w], "gather_w_in_l0")
        self.gat[(0, "mix")] = {"w_in": w_in}
        return conv_w

    def w_in(self, l):
        return _arrange_w_in(self.gat[(l, "mix")]["w_in"])

    def in_proj_ride(self, l):
        return _GatherRide([self.local[0][g] for g in MIX_GROUPS[1:]]) if l == 0 else None

    def in_proj_done(self, l, outs):
        self.gat[(l, "mix")].update(zip(MIX_GROUPS[1:], outs))

    def fwd_ride(self, l):
        return _GatherRide([self.local[ll][g] for ll, part in self._riders(l) for g in _part_groups(part)])

    def fwd_done(self, l, outs):
        outs = list(outs)
        for ll, part in self._riders(l):
            self.gat[(ll, part)] = {g: outs.pop(0) for g in _part_groups(part)}

    def mixer_weights(self, l):
        return _mixer_weights(self.gat[(l, "mix")])

    def ffn_weights(self, l):
        return _ffn_weights(self.gat[(l, "ffn")])

    def add_grads(self, l, part, gb):
        if part == "ffn":
            self.send[(l, "ffn")] = _ffn_grad_groups(gb)
        else:
            self.send.setdefault((l, "late"), {}).update(_mixer_grad_groups({g: gb[g] for g in MIX_LATE if g in gb}))

    def bwd_ride(self, l, gb_early):
        self.send[(l, "early")] = _mixer_grad_groups({g: gb_early[g] for g in MIX_EARLY})
        return _ReduceRide([self.send[(ll, part)][g] for ll, part in self._grad_riders(l) for g in _part_groups(part)])

    def bwd_done(self, l, outs):
        outs = list(outs)
        for ll, part in self._grad_riders(l):
            self.recv[(ll, part)] = {g: outs.pop(0) for g in _part_groups(part)}

    def tail_ride(self, l, small_groups):
        if l > 0:
            return None
        send = [self.send[(0, "late")][g] for g in MIX_LATE]
        by_core = [a.reshape((4, 2) + a.shape[1:]).transpose((1, 0) + tuple(range(2, a.ndim + 1))) for a in send]
        core = lax.axis_index("c")
        own = [lax.dynamic_index_in_dim(a, core, axis=0, keepdims=False) for a in by_core]
        got = _swap_with_sibling(by_core, "reduce_d2d")
        pairs = [_add_pairs(a, b, f"reduce_pair_add_{g}") for g, a, b in zip(MIX_LATE, own, got)]
        return _Combo([_ChipExchangeRide(pairs), _GatherRide(small_groups)])

    def tail_done(self, l, outs):
        self.recv[(l, "late")] = dict(zip(MIX_LATE, outs[:len(MIX_LATE)]))
        return outs[len(MIX_LATE):]

    def finish(self):
        layers = []
        for l in range(DEPTH):
            tot = {g: _sum_blocks(a, f"reduce_sum_{g}_l{l}") for part in ("early", "late", "ffn")
                   for g, a in self.recv[(l, part)].items()}
            layers.append(_grads_from_groups(tot))
        return layers


def _local_step(x, positions, target, smalls, plan):
    tables = _rope_tables(positions)
    saved = []
    h, h_norm = x, None
    for l in range(DEPTH):
        h, h2, svm, wm = _mixer_fwd(h, h_norm, tables, smalls[l], plan, l)
        wf = plan.ffn_weights(l)
        next_gain = smalls[l + 1]["mix_norm_pre"] if l + 1 < DEPTH else None
        h, h_norm, svf = _ffn_fwd(h, h2, wf, smalls[l], f"l{l}", next_gain)
        saved.append((svm, svf, wm, wf))
    dy, sq = _loss_grad(h, target, "loss_grad")
    small = [None] * DEPTH
    for l in reversed(range(DEPTH)):
        svm, svf, wm, wf = saved[l]
        dmid, gbf, gsf = _ffn_bwd(dy, svf, wf, smalls[l], f"l{l}")
        plan.add_grads(l, "ffn", gbf)

        def pack_small(gs, l=l, gsf=gsf):
            if l > 0:
                return None
            return _small_groups([{**gsf, **gs, "mix_norm_pre": jnp.zeros((D_MODEL,), F32)}] + small[1:])

        dy, gsm, small_gathered = _mixer_bwd(dmid, svm, tables, wm, smalls[l], plan, l, pack_small)
        small[l] = {**gsf, **gsm}
    return sq, dy, small, small_gathered


def kernel(x, positions, mix_norm_pre, w_in, q_norm, w_uq, kv_norm, w_uk, w_uv, w_attn_o, conv_w, conv_b, conv_ln_g, conv_ln_b, w_conv_o, pool_w, pool_scale, w_pool_o, w_mix_o, mix_norm_post, ffn_norm_pre, w_gate, w_up, w_down, ffn_norm_post, loss_target, m_mix_norm_pre, m_w_in, m_q_norm, m_w_uq, m_kv_norm, m_w_uk, m_w_uv, m_w_attn_o, m_conv_w, m_conv_b, m_conv_ln_g, m_conv_ln_b, m_w_conv_o, m_pool_w, m_pool_scale, m_w_pool_o, m_w_mix_o, m_mix_norm_post, m_ffn_norm_pre, m_w_gate, m_w_up, m_w_down, m_ffn_norm_post, v_mix_norm_pre, v_w_in, v_q_norm, v_w_uq, v_kv_norm, v_w_uk, v_w_uv, v_w_attn_o, v_conv_w, v_conv_b, v_conv_ln_g, v_conv_ln_b, v_w_conv_o, v_pool_w, v_pool_scale, v_w_pool_o, v_w_mix_o, v_mix_norm_post, v_ffn_norm_pre, v_w_gate, v_w_up, v_w_down, v_ffn_norm_post):
    given = dict(locals())
    dev = 4 * lax.axis_index("x") + 2 * lax.axis_index("y") + lax.axis_index("c")

    plan = _Plan({n: given[n] for n in BIG}, conv_w)
    cw = CONV_C // N_DEV
    conv_w_full = plan.gather_first().transpose(1, 2, 0, 3).reshape(DEPTH, CONV_W, CONV_C)
    smalls = []
    for l in range(DEPTH):
        sm = {n: given[n][l] for n in SMALL if n != "conv_w"}
        sm["conv_w"] = _pad_axis(conv_w_full[l], 0, CONV_HALO)
        smalls.append(sm)

    sq, grad_x, small, small_groups = _local_step(x[0], positions[0], loss_target[0], smalls, plan)
    loss = lax.psum(0.5 / D_MODEL * jnp.sum(sq), ("x", "y", "c"))
    per_layer = plan.finish()
    views = {}
    for n in BIG:
        if n == "w_in":
            views[n] = jnp.stack([per_layer[l][n].T for l in range(DEPTH)], axis=1)
        elif n in LANE_MAJOR:
            views[n] = jnp.stack([per_layer[l][n].T for l in range(DEPTH)])
        else:
            views[n] = jnp.stack([per_layer[l][n] for l in range(DEPTH)])
    grads = {n: _from_lane_major(n, views[n]) for n in BIG}

    small_sum = _small_from_groups([_sum_blocks(g, f"sum_small_grads_{i}") for i, g in enumerate(small_groups)])
    last = _pad_axis(small[0]["mix_norm_pre"].reshape(1, D_MODEL), 0, SUBLANES)
    last_sum = _sum_blocks(_all_gather([last], "gather_last_norm_grad")[0], "sum_last_norm_grad")[0]
    small_sum["mix_norm_pre"] = small_sum["mix_norm_pre"].at[0].set(last_sum)
    for n in SMALL:
        grads[n] = small_sum[n]
    grads["conv_w"] = lax.dynamic_slice_in_dim(small_sum["conv_w"], dev * cw, cw, axis=2)

    delta, new_m, new_v = {}, {}, {}
    for n in WEIGHTS:
        g_view = views[n] if n in views else grads[n]
        w_view, m_view, v_view = [_lane_major(n, given[k]) for k in (n, "m_" + n, "v_" + n)]
        res = _adamw(w_view, g_view, m_view, v_view, f"adamw_{n}")
        delta[n], new_m[n], new_v[n] = [_from_lane_major(n, r) for r in res]
    return (loss, grad_x[None], *[grads[n] for n in WEIGHTS], *[delta[n] for n in WEIGHTS],
            *[new_m[n] for n in WEIGHTS], *[new_v[n] for n in WEIGHTS])
```

```python
import functools
import math

import jax
import jax.numpy as jnp
from jax import lax
from jax.experimental import pallas as pl
from jax.experimental.pallas import tpu as pltpu

F32, BF16 = jnp.float32, jnp.bfloat16
MESH = pl.DeviceIdType.MESH

LANES = 128
SUBLANES = 8
VMEM_LIMIT_BYTES = 56 * 1024 * 1024
MATMUL_VMEM_BYTES = 40 * 1024 * 1024

N_DEV = 8
D_MODEL = 1024
DEPTH = 2
N_HEADS = 8
QK_NOPE, QK_ROPE, V_HEAD = 64, 32, 64
HEAD_PAD = LANES
Q_RANK, KV_RANK = 384, 256
ROPE_THETA = 10000.0
CONV_C, CONV_W = 512, 31
CONV_HALO = 32
POOL_WINDOWS = (2, 4, 8, 16)
POOL_C, POOL_G = 512, 4
POOL_GD = POOL_C // POOL_G
D_FF = 2816
FF_SHARD = D_FF // N_DEV
FF_SHARD_PAD = 3 * LANES
D_FF_PAD = N_DEV * FF_SHARD_PAD
W_IN_SHARD = 660
EPS = 1e-6
ATTN_SCALE = 1.0 / math.sqrt(QK_NOPE + QK_ROPE)
LOG2E = 1.4426950408889634
LR, B1, B2, ADAM_EPS, WD, STEP = 0.001, 0.9, 0.999, 1e-08, 0.01, 10

Z_W = 5376
ZC_GATE = (1024, 0)
ZC_GATES = (3072, 0)
ZC_CONV_A = (512, 6)
ZC_CONV_G = (512, 7)
ZC_CONV = (1024, 3)
ZC_POOL = (512, 8)
ZC_Q = (384, 12)
ZC_KR = (128, 39)
ZC_KV = (256, 20)
W_IN_PIECES = ((0, 384, 4608), (384, 640, 5120), (640, 672, 5056), (672, 1696, 3072), (1696, 2208, 4096),
               (2208, 5280, 0))

BIG = ("w_in", "w_uq", "w_uk", "w_uv", "w_attn_o", "w_conv_o", "w_pool_o", "w_mix_o", "w_gate", "w_up", "w_down")
SMALL = ("mix_norm_pre", "q_norm", "kv_norm", "conv_w", "conv_b", "conv_ln_g", "conv_ln_b", "pool_w", "pool_scale",
         "mix_norm_post", "ffn_norm_pre", "ffn_norm_post")
WEIGHTS = ("mix_norm_pre", "w_in", "q_norm", "w_uq", "kv_norm", "w_uk", "w_uv", "w_attn_o", "conv_w", "conv_b",
           "conv_ln_g", "conv_ln_b", "w_conv_o", "pool_w", "pool_scale", "w_pool_o", "w_mix_o", "mix_norm_post",
           "ffn_norm_pre", "w_gate", "w_up", "w_down", "ffn_norm_post")


def _params(*semantics):
    return pltpu.CompilerParams(dimension_semantics=semantics, vmem_limit_bytes=VMEM_LIMIT_BYTES)


def _tile(dim, cap):
    if dim <= cap:
        return dim
    for t in range(cap - cap % LANES, 0, -LANES):
        if dim % t == 0:
            return t
    raise ValueError(f"no tile for {dim} under {cap}")


def _row_tile(rows, row_bytes, budget=1 << 20):
    if rows * row_bytes <= budget:
        return rows
    cap = max(16, budget // row_bytes)
    for t in range(cap - cap % 16, 0, -16):
        if rows % t == 0:
            return t
    return rows


def _rows(ts, width, cidx=0):
    return pl.BlockSpec((ts, width), lambda i: (i, cidx))


def _fixed(shape):
    return pl.BlockSpec(shape, lambda *_: (0,) * len(shape))


def _sigmoid(x):
    return 1.0 / (1.0 + jnp.exp(-x))


def _matmul(a, b, mode, out_dtype, name, add=None, blocked=False, ride=None):
    nb = n_blk = 0
    blocked = blocked or b.ndim == 3
    if mode == "nn":
        (m, k) = a.shape
        n = b.shape[0] * b.shape[2] if blocked else b.shape[1]
    elif mode == "nt":
        (m, k) = a.shape
        n = b.shape[1] if blocked else b.shape[0]
    else:
        (k, m), n = a.shape, b.shape[1]
    if blocked:
        nb = b.shape[2] if mode != "tn" else n // N_DEV
    unit = nb if blocked and mode != "nt" else LANES
    out_bytes = jnp.dtype(out_dtype).itemsize + (4 if add is not None else 0)
    best = None
    for tn_c in range(unit, min(n, 1536) + 1, unit):
        for tm_c in sorted({256, 512, 1024, 2048, min(m, 2048)}):
            if n % tn_c or m % tm_c or (blocked and mode != "nt" and N_DEV % (tn_c // nb)):
                continue
            vmem = 2 * (tm_c * k * 2 + tn_c * k * 2 + tm_c * tn_c * out_bytes) + tm_c * tn_c * 4 + tn_c * k * 2
            if vmem <= MATMUL_VMEM_BYTES and (best is None or tm_c * tn_c / (tm_c + tn_c) > best[0]):
                best = (tm_c * tn_c / (tm_c + tn_c), tm_c, tn_c)
    if best is None:
        raise ValueError(f"{name}: no tiles for {m}x{n}x{k}")
    _, tm, tn = best
    if blocked:
        n_blk = N_DEV if mode == "nt" else tn // nb
    dims = {"nn": ((1,), (0,)), "nt": ((1,), (1,)), "tn": ((0,), (0,))}[mode]
    a_spec = pl.BlockSpec((k, tm), lambda i, j: (0, i)) if mode == "tn" else pl.BlockSpec((tm, k), lambda i, j: (i, 0))
    b_spec = pl.BlockSpec((tn, k), lambda i, j: (j, 0)) if mode == "nt" else pl.BlockSpec((k, tn), lambda i, j: (0, j))
    o_spec = pl.BlockSpec((tm, tn), lambda i, j: (i, j))
    out_shape = jax.ShapeDtypeStruct((m, n), out_dtype)
    if blocked and mode == "nn":
        b_spec = pl.BlockSpec((n_blk, k, nb), lambda i, j: (j, 0, 0))
    elif blocked and mode == "nt":
        b_spec = pl.BlockSpec((n_blk, tn, nb), lambda i, j: (0, j, 0))
    elif blocked:
        o_spec = pl.BlockSpec((n_blk, tm, nb), lambda i, j: (j, i, 0))
        out_shape = jax.ShapeDtypeStruct((N_DEV, m, nb), out_dtype)
    has_add = add is not None
    grid = (m // tm, n // tn)

    def body(*refs):
        (a_ref, b_ref, *rest), start, finish = _ride_hooks(ride, refs, 3 if has_add else 2, 1, grid)
        start()
        o_ref = rest[-1]
        if blocked and mode != "tn":
            bv = jnp.concatenate([b_ref[c] for c in range(n_blk)], axis=1) if n_blk > 1 else b_ref[0]
        else:
            bv = b_ref[...]
        total = lax.dot_general(a_ref[...], bv, (dims, ((), ())), preferred_element_type=F32)
        if has_add:
            total = total + rest[0][...]
        if blocked and mode == "tn":
            for c in range(n_blk):
                o_ref[c] = total[:, c * nb:(c + 1) * nb].astype(o_ref.dtype)
        else:
            o_ref[...] = total.astype(o_ref.dtype)
        finish()

    operands = (a, b, add) if has_add else (a, b)
    (out,), rode = _ride_call(ride, body, name, (out_shape,), grid, [a_spec, b_spec] + ([o_spec] if has_add else []),
                              (o_spec,), ("parallel", "parallel"), operands)
    return out if ride is None else (out, rode)


def _rms_fwd(x, win, gain, out_dtype, name, res=None, then=None):
    width, cidx = win
    s = x.shape[0]
    ts = min(s, 512)
    has_res, has_then = res is not None, then is not None

    def norm(v, g_ref):
        return (v * lax.rsqrt(jnp.mean(v * v, axis=-1, keepdims=True) + EPS)) * g_ref[...]

    def body(x_ref, g_ref, *rest):
        y = norm(x_ref[...].astype(F32), g_ref)
        if has_res:
            y = rest[0][...] + y
        o_ref = rest[-2] if has_then else rest[-1]
        o_ref[...] = y.astype(o_ref.dtype)
        if has_then:
            rest[-1][...] = norm(y, rest[-3]).astype(BF16)

    ops = (x, gain.reshape(1, width)) + ((res,) if has_res else ()) + ((then.reshape(1, width),) if has_then else ())
    out_shape = (jax.ShapeDtypeStruct((s, width), out_dtype),) + ((jax.ShapeDtypeStruct((s, width), BF16),) * has_then)
    out = pl.pallas_call(
        body, name=name, out_shape=out_shape, grid=(s // ts,),
        in_specs=([_rows(ts, width, cidx), _fixed((1, width))] + ([_rows(ts, width)] if has_res else [])
                  + ([_fixed((1, width))] if has_then else [])),
        out_specs=(_rows(ts, width),) * len(out_shape), compiler_params=_params("parallel"))(*ops)
    return out if has_then else out[0]


def _into(dz, n_inputs, out_index):
    return dict(in_specs=[ANY], operands=(dz,), input_output_aliases={n_inputs: out_index},
                out_shape=jax.ShapeDtypeStruct(dz.shape, dz.dtype))


def _rms_bwd(x, win, gain, dy, out_dtype, name, add=None, dz=None):
    width, cidx = win
    s = x.shape[0]
    ts = min(s, 512)
    has_add = add is not None

    def body(x_ref, g_ref, dy_ref, *rest):
        dx_ref, dg_ref = rest[-2], rest[-1]
        xv = x_ref[...].astype(F32)
        r = lax.rsqrt(jnp.mean(xv * xv, axis=-1, keepdims=True) + EPS)
        xh = xv * r
        dyv = dy_ref[...].astype(F32)
        dyg = dyv * g_ref[...]
        dx = r * (dyg - xh * jnp.mean(dyg * xh, axis=-1, keepdims=True))
        if has_add:
            dx = dx + rest[0][...]
        dx_ref[...] = dx.astype(dx_ref.dtype)

        @pl.when(pl.program_id(0) == 0)
        def _():
            dg_ref[...] = jnp.zeros_like(dg_ref)

        dg_ref[...] += jnp.sum(dyv * xh, axis=0, keepdims=True)

    ops = (x, gain.reshape(1, width), dy) + ((add,) if has_add else ())
    in_specs = [_rows(ts, width, cidx), _fixed((1, width)), _rows(ts, width)] + ([_rows(ts, width)] if has_add else [])
    dx_shape, dx_spec, alias = jax.ShapeDtypeStruct((s, width), out_dtype), _rows(ts, width), {}
    if dz is not None:
        into = _into(dz, len(ops), 0)
        ops, in_specs, alias = ops + into["operands"], in_specs + into["in_specs"], into["input_output_aliases"]
        dx_shape, dx_spec = into["out_shape"], _rows(ts, width, cidx)
    dx, dg = pl.pallas_call(
        body, name=name, out_shape=(dx_shape, jax.ShapeDtypeStruct((1, width), F32)), grid=(s // ts,),
        in_specs=in_specs, out_specs=(dx_spec, _fixed((1, width))), input_output_aliases=alias,
        compiler_params=_params("arbitrary"))(*ops)
    return dx, dg.reshape(width)


def _rope(x, c, s1, s2):
    return x * c + pltpu.roll(x, 16, 1) * s1 + pltpu.roll(x, LANES - 16, 1) * s2


def _rope_t(g, c, s1, s2):
    return g * c + pltpu.roll(g * s1, LANES - 16, 1) + pltpu.roll(g * s2, 16, 1)


def _rope_tables(positions):
    inv_freq = ROPE_THETA ** (-jnp.arange(0, QK_ROPE, 2, dtype=F32) / QK_ROPE)
    ang = positions.astype(F32)[:, None] * inv_freq
    cos, sin = jnp.cos(ang), jnp.sin(ang)
    n = positions.shape[0]
    one, zero = jnp.ones((n, 1), F32), jnp.zeros((n, 1), F32)
    c = jnp.concatenate([jnp.tile(one, (1, QK_NOPE)), cos, cos, jnp.tile(one, (1, 32))], axis=1)
    s1 = jnp.concatenate([jnp.tile(zero, (1, QK_NOPE + 16)), sin, jnp.tile(zero, (1, 32))], axis=1)
    s2 = jnp.concatenate([jnp.tile(zero, (1, QK_NOPE)), -sin, jnp.tile(zero, (1, 48))], axis=1)
    return c, s1, s2


def _qkv_up_fwd(cq, ckv, z, w_uq, w_uk, w_uv, tables, name):
    s = cq.shape[0]
    ts = min(s, 512)
    hw = N_HEADS * HEAD_PAD

    def body(cq_ref, ckv_ref, kr_ref, wq_ref, wk_ref, wv_ref, c_ref, s1_ref, s2_ref, q_ref, k_ref, v_ref):
        c, s1, s2 = c_ref[...], s1_ref[...], s2_ref[...]
        cqv, ckvv = cq_ref[...], ckv_ref[...]
        kr = _rope(kr_ref[...].astype(F32), c, s1, s2)
        for h in range(N_HEADS):
            sl = slice(h * HEAD_PAD, (h + 1) * HEAD_PAD)
            q_ref[:, sl] = _rope(jnp.dot(cqv, wq_ref[h], preferred_element_type=F32), c, s1, s2).astype(BF16)
            k_ref[:, sl] = (jnp.dot(ckvv, wk_ref[h], preferred_element_type=F32) + kr).astype(BF16)
            v_ref[:, sl] = jnp.dot(ckvv, wv_ref[h], preferred_element_type=F32).astype(BF16)

    tab = _rows(ts, LANES)
    return pl.pallas_call(
        body, name=name, out_shape=(jax.ShapeDtypeStruct((s, hw), BF16),) * 3, grid=(s // ts,),
        in_specs=[_rows(ts, Q_RANK), _rows(ts, KV_RANK), _rows(ts, *ZC_KR), _fixed(w_uq.shape), _fixed(w_uk.shape),
                  _fixed(w_uv.shape), tab, tab, tab],
        out_specs=(_rows(ts, hw),) * 3, compiler_params=_params("parallel"))(cq, ckv, z, w_uq, w_uk, w_uv, *tables)


def _rope_qk_bwd(dq, dk, tables, dz, name):
    s = dq.shape[0]
    ts = min(s, 256)
    hw = N_HEADS * HEAD_PAD

    def body(dq_ref, dk_ref, c_ref, s1_ref, s2_ref, _, dqf_ref, dkf_ref, dkr_ref):
        c, s1, s2 = c_ref[...], s1_ref[...], s2_ref[...]
        ksum = jnp.zeros((ts, HEAD_PAD), F32)
        for h in range(N_HEADS):
            sl = slice(h * HEAD_PAD, (h + 1) * HEAD_PAD)
            dqf_ref[:, sl] = _rope_t(dq_ref[:, sl], c, s1, s2).astype(BF16)
            dkh = dk_ref[:, sl]
            dkf_ref[:, sl] = dkh.astype(BF16)
            ksum = ksum + dkh
        lane = lax.broadcasted_iota(jnp.int32, (ts, HEAD_PAD), 1)
        in_rope = (lane >= QK_NOPE) & (lane < QK_NOPE + QK_ROPE)
        dkr_ref[...] = jnp.where(in_rope, _rope_t(ksum, c, s1, s2), 0.0).astype(BF16)

    tab = _rows(ts, LANES)
    into = _into(dz, 5, 2)
    return pl.pallas_call(
        body, name=name,
        out_shape=(jax.ShapeDtypeStruct((s, hw), BF16), jax.ShapeDtypeStruct((s, hw), BF16), into["out_shape"]),
        grid=(s // ts,), in_specs=[_rows(ts, hw), _rows(ts, hw), tab, tab, tab] + into["in_specs"],
        out_specs=(_rows(ts, hw), _rows(ts, hw), _rows(ts, *ZC_KR)), input_output_aliases=into["input_output_aliases"],
        compiler_params=_params("parallel"))(dq, dk, *tables, dz)


def _attn_tile(s):
    return min(s, 512)


def _raw_scores(q, k, masked, row0=0):
    sc = lax.dot_general(q, k, (((1,), (1,)), ((), ())), preferred_element_type=F32)
    if masked:
        rows = row0 + lax.broadcasted_iota(jnp.int32, sc.shape, 0)
        cols = lax.broadcasted_iota(jnp.int32, sc.shape, 1)
        sc = jnp.where(cols <= rows, sc, -jnp.inf)
    return sc


def _ride_hooks(ride, refs, n_in, n_out, grid):
    if ride is None:
        return refs, lambda: None, lambda: None
    n = len(ride.arrays)
    own = refs[:n_in] + refs[n_in + n:n_in + n + n_out]
    ins, outs, sems = refs[n_in:n_in + n], refs[n_in + n + n_out:n_in + 2 * n + n_out], refs[n_in + 2 * n + n_out:]
    at_first = functools.reduce(lambda a, b: a & b, [pl.program_id(ax) == 0 for ax in range(len(grid))])
    at_last = functools.reduce(lambda a, b: a & b, [pl.program_id(ax) == g - 1 for ax, g in enumerate(grid)])
    return own, lambda: pl.when(at_first)(lambda: ride.start(ins, outs, sems)), \
        lambda: pl.when(at_last)(lambda: ride.finish(ins, outs, sems))


def _ride_call(ride, body, name, out_shape, grid, in_specs, out_specs, semantics, operands):
    n = 0 if ride is None else len(ride.arrays)
    res = pl.pallas_call(
        body, name=name, out_shape=tuple(out_shape) + (tuple(ride.out_shape) if n else ()), grid=grid,
        in_specs=list(in_specs) + [ANY] * n, out_specs=tuple(out_specs) + (ANY,) * n,
        scratch_shapes=list(ride.scratch) if n else [],
        compiler_params=_params(*(("arbitrary",) * len(grid) if n else semantics)))(*operands, *(ride.arrays if n else ()))
    return res[:len(out_shape)], list(res[len(out_shape):])


def _flash_fwd(q, k, v, name, ride=None):
    s = q.shape[0]
    t = _attn_tile(s)
    c2 = ATTN_SCALE * LOG2E
    grid = (N_HEADS, s // t)

    def body(*refs):
        (q_ref, k_ref, v_ref, o_ref, lse_ref), start, finish = _ride_hooks(ride, refs, 3, 2, grid)
        start()
        i = pl.program_id(1)
        qv = q_ref[...]

        def chunk(j, carry, masked):
            m_old, l_old, acc = carry
            at = pl.ds(pl.multiple_of(j * t, t), t)
            sc = _raw_scores(qv, k_ref[at, :], masked)
            m_new = jnp.maximum(m_old, jnp.max(sc, axis=-1, keepdims=True))
            p = jnp.exp2((sc - m_new) * c2)
            alpha = jnp.exp2((m_old - m_new) * c2)
            l_new = alpha * l_old + jnp.sum(p, axis=-1, keepdims=True)
            acc = alpha * acc + jnp.dot(p.astype(BF16), v_ref[at, :], preferred_element_type=F32)
            return m_new, l_new, acc

        init = (jnp.full((t, 1), -jnp.inf, F32), jnp.zeros((t, 1), F32), jnp.zeros((t, HEAD_PAD), F32))
        carry = lax.fori_loop(0, i, lambda j, cr: chunk(j, cr, False), init)
        m_fin, l_fin, acc = chunk(i, carry, True)
        o_ref[...] = (acc / l_fin).astype(o_ref.dtype)
        lse_ref[...] = jnp.broadcast_to(m_fin * ATTN_SCALE + jnp.log(l_fin), (t, HEAD_PAD))
        finish()

    qo = pl.BlockSpec((t, HEAD_PAD), lambda h, i: (i, h))
    whole = pl.BlockSpec((s, HEAD_PAD), lambda h, i: (0, h))
    return _ride_call(
        ride, body, name, (jax.ShapeDtypeStruct(q.shape, BF16), jax.ShapeDtypeStruct(q.shape, F32)), grid,
        [qo, whole, whole], (qo, qo), ("parallel", "parallel"), (q, k, v))


def _attn_out_bwd(dya, w_attn_o, o, name):
    s, d = dya.shape
    hw = N_HEADS * HEAD_PAD
    t = _attn_tile(s)

    def body(d_ref, w_ref, o_ref, delta_ref, dob_ref):
        wv = jnp.concatenate([w_ref[c] for c in range(N_DEV)], axis=1)
        do = lax.dot_general(d_ref[...], wv, (((1,), (1,)), ((), ())), preferred_element_type=F32)
        for h in range(N_HEADS):
            sl = slice(h * HEAD_PAD, (h + 1) * HEAD_PAD)
            dov = do[:, sl]
            delta_ref[:, sl] = jnp.broadcast_to(jnp.sum(dov * o_ref[:, sl].astype(F32), axis=-1, keepdims=True),
                                                (t, HEAD_PAD))
            dob_ref[:, sl] = dov.astype(BF16)

    blk = _rows(t, hw)
    return pl.pallas_call(
        body, name=name, out_shape=(jax.ShapeDtypeStruct(o.shape, F32), jax.ShapeDtypeStruct(o.shape, BF16)),
        grid=(s // t,), in_specs=[_rows(t, d), _fixed(w_attn_o.shape), blk], out_specs=(blk, blk),
        compiler_params=_params("parallel"))(dya, w_attn_o, o)


def _flash_bwd(q, k, v, do, lse, delta, name, ride=None):
    s = q.shape[0]
    t = _attn_tile(s)
    nt = s // t
    c2 = ATTN_SCALE * LOG2E
    grid = (N_HEADS, nt)

    def body(*refs):
        (q_ref, k_ref, v_ref, do_ref, lse_ref, delta_ref, dq_ref, dk_ref, dv_ref), start, finish = _ride_hooks(
            ride, refs, 6, 3, grid)
        start()
        j = pl.program_id(1)
        kv, vv = k_ref[...], v_ref[...]

        @pl.when(j == 0)
        def _():
            dq_ref[...] = jnp.zeros_like(dq_ref)

        def chunk(i, carry, masked):
            dk_acc, dv_acc = carry
            at = pl.ds(pl.multiple_of(i * t, t), t)
            qi, doi = q_ref[at, :], do_ref[at, :]
            sc = _raw_scores(qi, kv, masked)
            p = jnp.exp2(sc * c2 - lse_ref[at, pl.ds(0, 1)] * LOG2E)
            dp = lax.dot_general(doi, vv, (((1,), (1,)), ((), ())), preferred_element_type=F32)
            ds = (p * (dp - delta_ref[at, pl.ds(0, 1)])).astype(BF16)
            dv_acc = dv_acc + lax.dot_general(p.astype(BF16), doi, (((0,), (0,)), ((), ())), preferred_element_type=F32)
            dk_acc = dk_acc + lax.dot_general(ds, qi, (((0,), (0,)), ((), ())), preferred_element_type=F32)
            dq_ref[at, :] += jnp.dot(ds, kv, preferred_element_type=F32) * ATTN_SCALE
            return dk_acc, dv_acc

        zero = jnp.zeros((t, HEAD_PAD), F32)
        carry = chunk(j, (zero, zero), True)
        dk_acc, dv_acc = lax.fori_loop(j + 1, nt, lambda i, cr: chunk(i, cr, False), carry)
        dk_ref[...] = dk_acc * ATTN_SCALE
        dv_ref[...] = dv_acc.astype(BF16)
        finish()

    blk = pl.BlockSpec((t, HEAD_PAD), lambda h, j: (j, h))
    whole = pl.BlockSpec((s, HEAD_PAD), lambda h, j: (0, h))
    return _ride_call(
        ride, body, name, (jax.ShapeDtypeStruct(q.shape, F32), jax.ShapeDtypeStruct(q.shape, F32),
                           jax.ShapeDtypeStruct(q.shape, BF16)), grid,
        [whole, blk, blk, whole, whole, whole], (whole, blk, blk), ("parallel", "arbitrary"), (q, k, v, do, lse, delta))


def _conv_tile(s):
    return min(s, 256)


def _halo_before(t, width, cidx):
    per = t // CONV_HALO
    return pl.BlockSpec((CONV_HALO, width), lambda i: (jnp.maximum(i * per - 1, 0), cidx))


def _halo_after(t, width, cidx, n_tiles):
    per = t // CONV_HALO
    last = n_tiles * per - 1
    return pl.BlockSpec((CONV_HALO, width), lambda i: (jnp.minimum((i + 1) * per, last), cidx))


def _fill_glu(hbuf, ap_ref, gp_ref, a_ref, g_ref, t):
    first = pl.program_id(0) == 0
    hbuf[pl.ds(0, CONV_HALO), :] = jnp.where(first, 0.0, ap_ref[...].astype(F32) * _sigmoid(gp_ref[...].astype(F32)))
    hbuf[pl.ds(CONV_HALO, t), :] = a_ref[...].astype(F32) * _sigmoid(g_ref[...].astype(F32))


def _phase_copies(dst, src, t):
    n = t + CONV_HALO - SUBLANES
    for s in range(1, SUBLANES):
        dst[s, pl.ds(0, n), :] = src[pl.ds(s, n), :]


def _window(phases, src, k, t):
    if k % SUBLANES == 0:
        return src[pl.ds(k, t), :]
    return phases[k % SUBLANES, pl.ds(k - k % SUBLANES, t), :]


def _layer_norm_parts(co):
    mu = jnp.mean(co, axis=-1, keepdims=True)
    xc = co - mu
    rstd = lax.rsqrt(jnp.mean(xc * xc, axis=-1, keepdims=True) + EPS)
    return xc * rstd, rstd


def _conv_fwd(z, conv_w, conv_b, ln_g, ln_b, name):
    s = z.shape[0]
    t = _conv_tile(s)
    off = CONV_HALO - (CONV_W - 1)

    def body(ap_ref, gp_ref, a_ref, g_ref, w_ref, b_ref, lg_ref, lb_ref, hc_ref, co_ref, hbuf, hph):
        _fill_glu(hbuf, ap_ref, gp_ref, a_ref, g_ref, t)
        _phase_copies(hph, hbuf, t)
        acc = jnp.zeros((t, CONV_C), F32) + b_ref[...]
        for j in range(CONV_W):
            acc = acc + _window(hph, hbuf, off + j, t) * w_ref[pl.ds(j, 1), :]
        co_ref[...] = acc
        xh, _ = _layer_norm_parts(acc)
        y = xh * lg_ref[...] + lb_ref[...]
        hc_ref[...] = (y * _sigmoid(y)).astype(BF16)

    vec = _fixed((1, CONV_C))
    return pl.pallas_call(
        body, name=name, out_shape=(jax.ShapeDtypeStruct((s, CONV_C), BF16), jax.ShapeDtypeStruct((s, CONV_C), F32)),
        grid=(s // t,),
        in_specs=[_halo_before(t, *ZC_CONV_A), _halo_before(t, *ZC_CONV_G), _rows(t, *ZC_CONV_A), _rows(t, *ZC_CONV_G),
                  _fixed((CONV_HALO, CONV_C)), vec, vec, vec],
        out_specs=(_rows(t, CONV_C), _rows(t, CONV_C)),
        scratch_shapes=[pltpu.VMEM((t + CONV_HALO, CONV_C), F32), pltpu.VMEM((SUBLANES, t + CONV_HALO, CONV_C), F32)],
        compiler_params=_params("parallel"))(z, z, z, z, conv_w, conv_b.reshape(1, -1), ln_g.reshape(1, -1),
                                             ln_b.reshape(1, -1))


def _conv_bwd_norm(dhc, co, ln_g, ln_b, name):
    s = co.shape[0]
    t = min(s, 512)

    def body(dhc_ref, co_ref, lg_ref, lb_ref, dco_ref, dg_ref, db_ref, dcb_ref):
        xh, rstd = _layer_norm_parts(co_ref[...])
        y = xh * lg_ref[...] + lb_ref[...]
        sg = _sigmoid(y)
        dy = dhc_ref[...] * (sg * (1.0 + y * (1.0 - sg)))
        dxh = dy * lg_ref[...]
        dco = rstd * (dxh - jnp.mean(dxh, axis=-1, keepdims=True) - xh * jnp.mean(dxh * xh, axis=-1, keepdims=True))
        dco_ref[...] = dco

        @pl.when(pl.program_id(0) == 0)
        def _():
            dg_ref[...] = jnp.zeros_like(dg_ref)
            db_ref[...] = jnp.zeros_like(db_ref)
            dcb_ref[...] = jnp.zeros_like(dcb_ref)

        dg_ref[...] += jnp.sum(dy * xh, axis=0, keepdims=True)
        db_ref[...] += jnp.sum(dy, axis=0, keepdims=True)
        dcb_ref[...] += jnp.sum(dco, axis=0, keepdims=True)

    vec = _fixed((1, CONV_C))
    one = jax.ShapeDtypeStruct((1, CONV_C), F32)
    dco, dg, db, dcb = pl.pallas_call(
        body, name=name, out_shape=(jax.ShapeDtypeStruct((s, CONV_C), F32), one, one, one), grid=(s // t,),
        in_specs=[_rows(t, CONV_C), _rows(t, CONV_C), vec, vec], out_specs=(_rows(t, CONV_C), vec, vec, vec),
        compiler_params=_params("arbitrary"))(dhc, co, ln_g.reshape(1, -1), ln_b.reshape(1, -1))
    return dco, dg.reshape(-1), db.reshape(-1), dcb.reshape(-1)


def _conv_bwd_taps(dco, z, conv_w, dz, name):
    s = z.shape[0]
    t = _conv_tile(s)
    nt = s // t
    off = CONV_HALO - (CONV_W - 1)

    def body(ap_ref, gp_ref, a_ref, g_ref, d_ref, dn_ref, w_ref, _, du_ref, dw_ref, hbuf, dbuf, hph, dph):
        i = pl.program_id(0)
        _fill_glu(hbuf, ap_ref, gp_ref, a_ref, g_ref, t)
        dbuf[pl.ds(0, t), :] = d_ref[...]
        dbuf[pl.ds(t, CONV_HALO), :] = jnp.where(i == nt - 1, 0.0, dn_ref[...])
        _phase_copies(hph, hbuf, t)
        _phase_copies(dph, dbuf, t)

        @pl.when(i == 0)
        def _():
            dw_ref[...] = jnp.zeros_like(dw_ref)

        dcur = d_ref[...]
        dh = jnp.zeros((t, CONV_C), F32)
        for j in range(CONV_W):
            dh = dh + _window(dph, dbuf, CONV_W - 1 - j, t) * w_ref[pl.ds(j, 1), :]
            dw_ref[pl.ds(j, 1), :] += jnp.sum(dcur * _window(hph, hbuf, off + j, t), axis=0, keepdims=True)
        a, sg = a_ref[...].astype(F32), _sigmoid(g_ref[...].astype(F32))
        du_ref[:, pl.ds(0, CONV_C)] = (dh * sg).astype(BF16)
        du_ref[:, pl.ds(CONV_C, CONV_C)] = (dh * a * sg * (1.0 - sg)).astype(BF16)

    into = _into(dz, 7, 0)
    return pl.pallas_call(
        body, name=name, out_shape=(into["out_shape"], jax.ShapeDtypeStruct((CONV_HALO, CONV_C), F32)), grid=(nt,),
        in_specs=[_halo_before(t, *ZC_CONV_A), _halo_before(t, *ZC_CONV_G), _rows(t, *ZC_CONV_A), _rows(t, *ZC_CONV_G),
                  _rows(t, CONV_C), _halo_after(t, CONV_C, 0, nt), _fixed((CONV_HALO, CONV_C))] + into["in_specs"],
        out_specs=(_rows(t, *ZC_CONV), _fixed((CONV_HALO, CONV_C))), input_output_aliases=into["input_output_aliases"],
        scratch_shapes=[pltpu.VMEM((t + CONV_HALO, CONV_C), F32), pltpu.VMEM((t + CONV_HALO, CONV_C), F32),
                        pltpu.VMEM((SUBLANES, t + CONV_HALO, CONV_C), F32),
                        pltpu.VMEM((SUBLANES, t + CONV_HALO, CONV_C), F32)],
        compiler_params=_params("arbitrary"))(z, z, z, z, dco, dco, conv_w, dz)


def _pool_tile(s):
    return min(s, 512)


def _pool_counts(row0, n, window):
    rows = row0 + lax.broadcasted_iota(jnp.int32, (n, POOL_GD), 0)
    return jnp.minimum(rows + 1, window).astype(F32)


def _pool_diff(ubuf, gi, window, row0, t):
    lanes = pl.ds(gi * POOL_GD, POOL_GD)
    tot = ubuf[pl.ds(CONV_HALO, t), lanes]
    cur = tot
    for back in range(1, window):
        tot = tot + ubuf[pl.ds(CONV_HALO - back, t), lanes]
    return tot / _pool_counts(row0, t, window) - cur


def _pool_fwd(z, pool_w, pool_scale, name):
    s = z.shape[0]
    t = _pool_tile(s)

    def body(up_ref, u_ref, w_ref, sc_ref, m_ref, ubuf):
        i = pl.program_id(0)
        ubuf[pl.ds(0, CONV_HALO), :] = jnp.where(i == 0, 0.0, up_ref[...].astype(F32))
        ubuf[pl.ds(CONV_HALO, t), :] = u_ref[...].astype(F32)
        for gi, window in enumerate(POOL_WINDOWS):
            d = _pool_diff(ubuf, gi, window, i * t, t)
            mm = jnp.dot(d.astype(BF16), w_ref[gi].astype(BF16), preferred_element_type=F32)
            lanes = pl.ds(gi * POOL_GD, POOL_GD)
            m_ref[:, lanes] = (mm * sc_ref[:, lanes]).astype(BF16)

    return pl.pallas_call(
        body, name=name, out_shape=jax.ShapeDtypeStruct((s, POOL_C), BF16), grid=(s // t,),
        in_specs=[_halo_before(t, *ZC_POOL), _rows(t, *ZC_POOL), _fixed((POOL_G, POOL_GD, POOL_GD)), _fixed((1, POOL_C))],
        out_specs=_rows(t, POOL_C), scratch_shapes=[pltpu.VMEM((t + CONV_HALO, POOL_C), F32)],
        compiler_params=_params("parallel"))(z, z, pool_w, pool_scale.reshape(1, -1))


def _pool_bwd(dm, z, pool_w, pool_scale, dz, name):
    s = z.shape[0]
    t = _pool_tile(s)
    nt = s // t

    def body(up_ref, u_ref, dm_ref, dmn_ref, w_ref, sc_ref, _, du_ref, dw_ref, dsc_ref, ubuf, ebuf):
        i = pl.program_id(0)
        ubuf[pl.ds(0, CONV_HALO), :] = jnp.where(i == 0, 0.0, up_ref[...].astype(F32))
        ubuf[pl.ds(CONV_HALO, t), :] = u_ref[...].astype(F32)

        @pl.when(i == 0)
        def _():
            dw_ref[...] = jnp.zeros_like(dw_ref)
            dsc_ref[...] = jnp.zeros_like(dsc_ref)

        dm_next = jnp.where(i == nt - 1, 0.0, dmn_ref[...])
        for gi, window in enumerate(POOL_WINDOWS):
            lanes = pl.ds(gi * POOL_GD, POOL_GD)
            wb = w_ref[gi].astype(BF16)
            scale = sc_ref[:, lanes]
            d = _pool_diff(ubuf, gi, window, i * t, t).astype(BF16)
            mm = jnp.dot(d, wb, preferred_element_type=F32)
            dmv = dm_ref[:, lanes]
            dsc_ref[:, lanes] += jnp.sum(dmv * mm, axis=0, keepdims=True)
            dmm = (dmv * scale).astype(BF16)
            dw_ref[gi] += lax.dot_general(d, dmm, (((0,), (0,)), ((), ())), preferred_element_type=F32)
            dd = lax.dot_general(dmm, wb, (((1,), (1,)), ((), ())), preferred_element_type=F32)
            dd_next = lax.dot_general((dm_next[:, gi * POOL_GD:(gi + 1) * POOL_GD] * scale).astype(BF16), wb,
                                      (((1,), (1,)), ((), ())), preferred_element_type=F32)
            ebuf[pl.ds(0, t), lanes] = dd / _pool_counts(i * t, t, window)
            ebuf[pl.ds(t, CONV_HALO), lanes] = dd_next / _pool_counts((i + 1) * t, CONV_HALO, window)
            du = -dd
            for ahead in range(window):
                du = du + ebuf[pl.ds(ahead, t), lanes]
            du_ref[:, lanes] = du.astype(BF16)

    into = _into(dz, 6, 0)
    du, dw, dsc = pl.pallas_call(
        body, name=name,
        out_shape=(into["out_shape"], jax.ShapeDtypeStruct((POOL_G, POOL_GD, POOL_GD), F32),
                   jax.ShapeDtypeStruct((1, POOL_C), F32)), grid=(nt,),
        in_specs=[_halo_before(t, *ZC_POOL), _rows(t, *ZC_POOL), _rows(t, POOL_C), _halo_after(t, POOL_C, 0, nt),
                  _fixed((POOL_G, POOL_GD, POOL_GD)), _fixed((1, POOL_C))] + into["in_specs"],
        out_specs=(_rows(t, *ZC_POOL), _fixed((POOL_G, POOL_GD, POOL_GD)), _fixed((1, POOL_C))),
        input_output_aliases=into["input_output_aliases"],
        scratch_shapes=[pltpu.VMEM((t + CONV_HALO, POOL_C), F32), pltpu.VMEM((t + CONV_HALO, POOL_C), F32)],
        compiler_params=_params("arbitrary"))(z, z, dm, dm, pool_w, pool_scale.reshape(1, -1), dz)
    return du, dw, dsc.reshape(-1)


def _gate_specs(ts):
    width, first = ZC_GATE
    return [_rows(ts, width, first + b) for b in range(3)]


def _merge_fwd(z, ys, name):
    s = z.shape[0]
    ts = min(s, 256)

    def body(g0, g1, g2, y0, y1, y2, o_ref):
        o_ref[...] = sum(_sigmoid(g[...].astype(F32)) * y[...].astype(F32)
                         for g, y in ((g0, y0), (g1, y1), (g2, y2))).astype(BF16)

    return pl.pallas_call(
        body, name=name, out_shape=jax.ShapeDtypeStruct((s, D_MODEL), BF16), grid=(s // ts,),
        in_specs=_gate_specs(ts) + [_rows(ts, D_MODEL)] * 3, out_specs=_rows(ts, D_MODEL),
        compiler_params=_params("parallel"))(z, z, z, *ys)


def _merge_bwd(z, ys, dmerged, name):
    s = z.shape[0]
    ts = min(s, 256)

    def body(g0, g1, g2, y0, y1, y2, dm_ref, dy0, dy1, dy2, dz_ref):
        dmv = dm_ref[...]
        for b, (g_ref, y_ref, dy_ref) in enumerate(((g0, y0, dy0), (g1, y1, dy1), (g2, y2, dy2))):
            sg = _sigmoid(g_ref[...].astype(F32))
            dy_ref[...] = (dmv * sg).astype(BF16)
            dz_ref[:, pl.ds(b * D_MODEL, D_MODEL)] = (dmv * y_ref[...].astype(F32) * sg * (1.0 - sg)).astype(BF16)

    out = jax.ShapeDtypeStruct((s, D_MODEL), BF16)
    return pl.pallas_call(
        body, name=name, out_shape=(out,) * 3 + (jax.ShapeDtypeStruct((s, Z_W), BF16),), grid=(s // ts,),
        in_specs=_gate_specs(ts) + [_rows(ts, D_MODEL)] * 4,
        out_specs=(_rows(ts, D_MODEL),) * 3 + (_rows(ts, *ZC_GATES),),
        compiler_params=_params("parallel"))(z, z, z, *ys, dmerged)


def _ffn_up_fwd(h, w_gate, w_up, name):
    s, d = h.shape
    nb = w_gate.shape[2]
    f = N_DEV * nb
    tm, n_blk = min(s, 1024), 2
    tn = n_blk * nb
    blk = pl.BlockSpec((tm, tn), lambda i, j: (i, j))
    wspec = pl.BlockSpec((n_blk, d, nb), lambda i, j: (j, 0, 0))

    def body(h_ref, wg_ref, wu_ref, hg_ref, hu_ref, act_ref):
        hv = h_ref[...]
        g = jnp.dot(hv, jnp.concatenate([wg_ref[c] for c in range(n_blk)], axis=1), preferred_element_type=F32)
        u = jnp.dot(hv, jnp.concatenate([wu_ref[c] for c in range(n_blk)], axis=1), preferred_element_type=F32)
        hg_ref[...] = g.astype(hg_ref.dtype)
        hu_ref[...] = u.astype(hu_ref.dtype)
        act_ref[...] = (g * _sigmoid(g) * u).astype(BF16)

    return pl.pallas_call(
        body, name=name,
        out_shape=(jax.ShapeDtypeStruct((s, f), BF16),) * 3,
        grid=(s // tm, f // tn), in_specs=[pl.BlockSpec((tm, d), lambda i, j: (i, 0)), wspec, wspec],
        out_specs=(blk, blk, blk), compiler_params=_params("parallel", "parallel"))(h, w_gate, w_up)


def _ffn_down_bwd(dfo, w_down, hg, hu, name):
    s, d = dfo.shape
    f = w_down.shape[0]
    tm, tn = min(s, 1024), _tile(f, 1024)
    blk = pl.BlockSpec((tm, tn), lambda i, j: (i, j))

    def body(d_ref, w_ref, g_ref, u_ref, dg_ref, du_ref):
        dact = lax.dot_general(d_ref[...], w_ref[...], (((1,), (1,)), ((), ())), preferred_element_type=F32)
        g = g_ref[...].astype(F32)
        sg = _sigmoid(g)
        dg_ref[...] = (dact * u_ref[...].astype(F32) * (sg * (1.0 + g * (1.0 - sg)))).astype(BF16)
        du_ref[...] = (dact * g * sg).astype(BF16)

    out = jax.ShapeDtypeStruct((s, f), BF16)
    return pl.pallas_call(
        body, name=name, out_shape=(out, out), grid=(s // tm, f // tn),
        in_specs=[pl.BlockSpec((tm, d), lambda i, j: (i, 0)), pl.BlockSpec((tn, d), lambda i, j: (j, 0)), blk, blk],
        out_specs=(blk, blk), compiler_params=_params("parallel", "parallel"))(dfo, w_down, hg, hu)


def _loss_grad(y, target, name):
    s, d = y.shape
    ts = min(s, 512)

    def body(y_ref, t_ref, dy_ref, sq_ref):
        e = y_ref[...] - t_ref[...]
        dy_ref[...] = e / d

        @pl.when(pl.program_id(0) == 0)
        def _():
            sq_ref[...] = jnp.zeros_like(sq_ref)

        sq_ref[...] += jnp.sum(e * e, axis=0, keepdims=True)

    return pl.pallas_call(
        body, name=name, out_shape=(jax.ShapeDtypeStruct((s, d), F32), jax.ShapeDtypeStruct((1, d), F32)),
        grid=(s // ts,), in_specs=[_rows(ts, d), _rows(ts, d)], out_specs=(_rows(ts, d), _fixed((1, d))),
        compiler_params=_params("arbitrary"))(y, target)


def _adamw(w, g, m, v, name):
    shape = w.shape
    cols = shape[-1]
    keep3 = w.ndim == 3 and shape[1] < SUBLANES
    view = shape if keep3 else (math.prod(shape[:-1]), cols)
    rows = view[0]
    if keep3:
        cap = max(1, (1 << 20) // (SUBLANES * cols * 4))
        tr = max(t for t in range(1, cap + 1) if rows % t == 0)
    else:
        tr = _row_tile(rows, cols * 4)

    def body(w_ref, g_ref, m_ref, v_ref, d_ref, mo_ref, vo_ref):
        gv = g_ref[...]
        mn = B1 * m_ref[...] + (1.0 - B1) * gv
        vn = B2 * v_ref[...] + (1.0 - B2) * (gv * gv)
        m_hat = mn / (1.0 - B1 ** STEP)
        v_hat = vn / (1.0 - B2 ** STEP)
        d_ref[...] = -LR * (m_hat / (jnp.sqrt(v_hat) + ADAM_EPS) + WD * w_ref[...])
        mo_ref[...] = mn
        vo_ref[...] = vn

    spec = pl.BlockSpec((tr,) + view[1:], lambda i: (i,) + (0,) * (len(view) - 1))
    out = jax.ShapeDtypeStruct(view, F32)
    res = pl.pallas_call(
        body, name=name, out_shape=(out,) * 3, grid=(rows // tr,), in_specs=[spec] * 4, out_specs=(spec,) * 3,
        compiler_params=_params("parallel"))(*[t.reshape(view) for t in (w, g, m, v)])
    return tuple(r.reshape(shape) for r in res)


LANE_MAJOR = ("w_uq", "w_uk", "w_uv", "w_gate", "w_up")


def _lane_major(name, a):
    if name == "w_in":
        return a.transpose(2, 0, 1)
    if name in LANE_MAJOR:
        return a.transpose(0, 2, 1)
    return a


def _from_lane_major(name, a):
    if name == "w_in":
        return a.transpose(1, 2, 0)
    return _lane_major(name, a)


ANY = pl.BlockSpec(memory_space=pl.ANY)


class _GatherRide:
    def __init__(self, arrays):
        n = len(arrays)
        self.arrays = list(arrays)
        self.out_shape = [jax.ShapeDtypeStruct((N_DEV,) + a.shape, a.dtype) for a in arrays]
        self.scratch = [pltpu.SemaphoreType.DMA((n, 7)), pltpu.SemaphoreType.DMA((n, 7)), pltpu.SemaphoreType.DMA((n,))]

    def _copies(self, ins, outs, sems):
        send_sems, recv_sems, local_sems = sems
        n = len(self.arrays)
        x, y, c = lax.axis_index("x"), lax.axis_index("y"), lax.axis_index("c")
        me, sibling = (x, y, c), (x, y, 1 - c)
        chips = [(1 - x, y), (x, 1 - y), (1 - x, 1 - y)]

        def slot(a, px, py, pc):
            return outs[a].at[4 * px + 2 * py + pc]

        def copy(a, k, block, to, src=None):
            return pltpu.make_async_remote_copy(
                src_ref=slot(a, *block) if src is None else src, dst_ref=slot(a, *block), send_sem=send_sems.at[a, k],
                recv_sem=recv_sems.at[a, k], device_id=to, device_id_type=MESH)

        mine = [pltpu.make_async_copy(ins[a], slot(a, *me), local_sems.at[a]) for a in range(n)]
        first = []
        for a in range(n):
            first.append(copy(a, 0, me, sibling, src=ins[a]))
            first += [copy(a, 1 + j, me, (*chip, c), src=ins[a]) for j, chip in enumerate(chips)]
        return n, me, sibling, chips, c, copy, mine, first

    def start(self, ins, outs, sems):
        _, _, _, _, _, _, mine, first = self._copies(ins, outs, sems)
        for cp in mine + first:
            cp.start()

    def finish(self, ins, outs, sems):
        n, me, sibling, chips, c, copy, mine, first = self._copies(ins, outs, sems)
        passed = []
        for j, chip in enumerate(chips):
            for a in range(n):
                copy(a, 1 + j, (*chip, c), me).wait_recv()
                passed.append(copy(a, 4 + j, (*chip, c), sibling))
                passed[-1].start()
        for a in range(n):
            copy(a, 0, sibling, me).wait_recv()
            for j, chip in enumerate(chips):
                copy(a, 4 + j, (*chip, 1 - c), me).wait_recv()
        for cp in first + passed:
            cp.wait_send()
        for cp in mine:
            cp.wait()


class _ReduceRide:
    def __init__(self, arrays):
        n = len(arrays)
        self.arrays = list(arrays)
        self.out_shape = [jax.ShapeDtypeStruct(a.shape, a.dtype) for a in arrays]
        self.scratch = [pltpu.SemaphoreType.DMA((n, 7)), pltpu.SemaphoreType.DMA((n, 7)), pltpu.SemaphoreType.DMA((n,))]

    def _copies(self, ins, outs, sems):
        send_sems, recv_sems, local_sems = sems
        n = len(self.arrays)
        x, y, c = lax.axis_index("x"), lax.axis_index("y"), lax.axis_index("c")
        mine = [pltpu.make_async_copy(ins[a].at[4 * x + 2 * y + c], outs[a].at[0], local_sems.at[a]) for a in range(n)]
        copies = []
        for a in range(n):
            for k in range(1, N_DEV):
                px = 1 - x if k & 4 else x
                py = 1 - y if k & 2 else y
                pc = 1 - c if k & 1 else c
                copies.append(pltpu.make_async_remote_copy(
                    src_ref=ins[a].at[4 * px + 2 * py + pc], dst_ref=outs[a].at[k], send_sem=send_sems.at[a, k - 1],
                    recv_sem=recv_sems.at[a, k - 1], device_id=(px, py, pc), device_id_type=MESH))
        return mine, copies

    def start(self, ins, outs, sems):
        mine, copies = self._copies(ins, outs, sems)
        for cp in mine + copies:
            cp.start()

    def finish(self, ins, outs, sems):
        mine, copies = self._copies(ins, outs, sems)
        for cp in copies + mine:
            cp.wait()


def _run_ride(ride, name):
    n = len(ride.arrays)

    def body(*refs):
        ins, outs, sems = refs[:n], refs[n:2 * n], refs[2 * n:]
        ride.start(ins, outs, sems)
        ride.finish(ins, outs, sems)

    return pl.pallas_call(body, name=name, out_shape=ride.out_shape, in_specs=[ANY] * n, out_specs=[ANY] * n,
                          scratch_shapes=ride.scratch)(*ride.arrays)


def _all_gather(arrays, name):
    return _run_ride(_GatherRide(arrays), name)


def _swap_with_sibling(arrays, name):
    n = len(arrays)

    def body(*refs):
        ins, outs = refs[:n], refs[n:2 * n]
        send_sems, recv_sems = refs[2 * n:]
        x, y, c = lax.axis_index("x"), lax.axis_index("y"), lax.axis_index("c")
        copies = [pltpu.make_async_remote_copy(
            src_ref=ins[a].at[1 - c], dst_ref=outs[a], send_sem=send_sems.at[a], recv_sem=recv_sems.at[a],
            device_id=(x, y, 1 - c), device_id_type=MESH) for a in range(n)]
        for cp in copies:
            cp.start()
        for cp in copies:
            cp.wait()

    return pl.pallas_call(
        body, name=name, out_shape=[jax.ShapeDtypeStruct(a.shape[1:], a.dtype) for a in arrays],
        in_specs=[ANY] * n, out_specs=[ANY] * n,
        scratch_shapes=[pltpu.SemaphoreType.DMA((n,)), pltpu.SemaphoreType.DMA((n,))])(*arrays)


class _ChipExchangeRide:
    def __init__(self, arrays):
        n = len(arrays)
        self.arrays = list(arrays)
        self.out_shape = [jax.ShapeDtypeStruct(a.shape, a.dtype) for a in arrays]
        self.scratch = [pltpu.SemaphoreType.DMA((n, 3)), pltpu.SemaphoreType.DMA((n, 3)), pltpu.SemaphoreType.DMA((n,))]

    def _copies(self, ins, outs, sems):
        send_sems, recv_sems, local_sems = sems
        n = len(self.arrays)
        x, y, c = lax.axis_index("x"), lax.axis_index("y"), lax.axis_index("c")
        partners = [(x, 1 - y), (1 - x, y), (1 - x, 1 - y)]
        mine = [pltpu.make_async_copy(ins[a].at[2 * x + y], outs[a].at[0], local_sems.at[a]) for a in range(n)]
        copies = [pltpu.make_async_remote_copy(
            src_ref=ins[a].at[2 * px + py], dst_ref=outs[a].at[1 + k], send_sem=send_sems.at[a, k],
            recv_sem=recv_sems.at[a, k], device_id=(px, py, c), device_id_type=MESH)
            for a in range(n) for k, (px, py) in enumerate(partners)]
        return mine, copies

    def start(self, ins, outs, sems):
        mine, copies = self._copies(ins, outs, sems)
        for cp in mine + copies:
            cp.start()

    def finish(self, ins, outs, sems):
        mine, copies = self._copies(ins, outs, sems)
        for cp in copies + mine:
            cp.wait()


class _Combo:
    def __init__(self, rides):
        self.rides = rides
        self.arrays = [a for r in rides for a in r.arrays]
        self.out_shape = [o for r in rides for o in r.out_shape]
        self.scratch = [sc for r in rides for sc in r.scratch]

    def _parts(self, ins, outs, sems):
        at_a = at_s = 0
        for r in self.rides:
            na, ns = len(r.arrays), len(r.scratch)
            yield r, ins[at_a:at_a + na], outs[at_a:at_a + na], sems[at_s:at_s + ns]
            at_a, at_s = at_a + na, at_s + ns

    def start(self, ins, outs, sems):
        for r, i, o, sm in self._parts(ins, outs, sems):
            r.start(i, o, sm)

    def finish(self, ins, outs, sems):
        for r, i, o, sm in self._parts(ins, outs, sems):
            r.finish(i, o, sm)


def _as_rows(a, lead):
    return a.reshape(a.shape[:lead] + (math.prod(a.shape[lead:-1]), a.shape[-1]))


def _add_pairs(a, b, name):
    a2, b2 = _as_rows(a, 0), _as_rows(b, 0)
    rows, cols = a2.shape
    tr = _row_tile(rows, cols * 4)

    def body(a_ref, b_ref, o_ref):
        o_ref[...] = (a_ref[...].astype(F32) + b_ref[...].astype(F32)).astype(o_ref.dtype)

    spec = _rows(tr, cols)
    out = pl.pallas_call(body, name=name, out_shape=jax.ShapeDtypeStruct(a2.shape, a.dtype), grid=(rows // tr,),
                         in_specs=[spec, spec], out_specs=spec, compiler_params=_params("parallel"))(a2, b2)
    return out.reshape(a.shape)


def _sum_blocks(a, name):
    a3 = _as_rows(a, 1)
    n, rows, cols = a3.shape
    tr = _row_tile(rows, n * cols * 4)

    def body(a_ref, o_ref):
        tot = a_ref[0].astype(F32)
        for k in range(1, n):
            tot = tot + a_ref[k].astype(F32)
        o_ref[...] = tot

    out = pl.pallas_call(body, name=name, out_shape=jax.ShapeDtypeStruct((rows, cols), F32), grid=(rows // tr,),
                         in_specs=[pl.BlockSpec((n, tr, cols), lambda j: (0, j, 0))], out_specs=_rows(tr, cols),
                         compiler_params=_params("parallel"))(a3)
    return out.reshape(a.shape[1:])


MIX_GROUPS = ("w_in", "w_uq", "w_uk", "w_uv", "w_attn_o", "w_conv_o", "w_pool_o", "w_mix_o")
FFN_GROUPS = ("w_gate", "w_up", "w_down")
MIX_EARLY = ("w_attn_o", "w_conv_o", "w_pool_o", "w_mix_o")
MIX_LATE = ("w_in", "w_uq", "w_uk", "w_uv")


def _pad_axis(a, axis, size):
    pad = [(0, 0)] * a.ndim
    pad[axis] = (0, size - a.shape[axis])
    return jnp.pad(a, pad)


def _local_groups(sh, l):
    out = {n: sh[n][l] for n in BIG}
    for n in ("w_uq", "w_uk", "w_uv"):
        out[n] = _pad_axis(out[n], -1, HEAD_PAD)
    for n in ("w_gate", "w_up"):
        out[n] = _pad_axis(out[n], -1, FF_SHARD_PAD)
    out["w_down"] = _pad_axis(out["w_down"], 0, FF_SHARD_PAD)
    return {n: v.astype(BF16) for n, v in out.items()}


def _arrange_w_in(blocks):
    parts, pos = [], 0
    for ref_lo, ref_hi, at in sorted(W_IN_PIECES, key=lambda p: p[2]):
        if at > pos:
            parts.append(jnp.zeros((blocks.shape[1], at - pos), blocks.dtype))
        for d in range(N_DEV):
            lo, hi = max(ref_lo, d * W_IN_SHARD), min(ref_hi, (d + 1) * W_IN_SHARD)
            if lo < hi:
                parts.append(blocks[d][:, lo - d * W_IN_SHARD:hi - d * W_IN_SHARD])
        pos = at + ref_hi - ref_lo
    if pos < Z_W:
        parts.append(jnp.zeros((blocks.shape[1], Z_W - pos), blocks.dtype))
    return jnp.concatenate(parts, axis=1)


def _w_in_shard(g, d):
    parts = []
    for ref_lo, ref_hi, at in W_IN_PIECES:
        lo, hi = max(ref_lo, d * W_IN_SHARD), min(ref_hi, (d + 1) * W_IN_SHARD)
        if lo < hi:
            parts.append(g[:, at + lo - ref_lo:at + hi - ref_lo])
    return jnp.concatenate(parts, axis=1)


def _mixer_weights(gat):
    w = {n: v for n, v in gat.items() if n != "w_in"}
    attn_o = gat["w_attn_o"].reshape(N_DEV, N_HEADS, V_HEAD, LANES)
    w["w_attn_o"] = _pad_axis(attn_o, 2, HEAD_PAD).reshape(N_DEV, N_HEADS * HEAD_PAD, LANES)
    w["w_mix_o"] = gat["w_mix_o"].reshape(D_MODEL, D_MODEL)
    return w


def _ffn_weights(gat):
    return {"w_gate": gat["w_gate"], "w_up": gat["w_up"], "w_down": gat["w_down"].reshape(D_FF_PAD, D_MODEL)}


def _mixer_grad_groups(gb):
    g = dict(gb)
    if "w_in" in gb:
        g["w_in"] = jnp.stack([_w_in_shard(gb["w_in"], d) for d in range(N_DEV)])
    if "w_attn_o" in gb:
        attn_o = gb["w_attn_o"].reshape(N_DEV, N_HEADS, HEAD_PAD, LANES)[:, :, :V_HEAD]
        g["w_attn_o"] = attn_o.reshape(N_DEV, N_HEADS * V_HEAD, LANES)
    if "w_mix_o" in gb:
        g["w_mix_o"] = gb["w_mix_o"].reshape(N_DEV, D_MODEL // N_DEV, D_MODEL)
    return g


def _ffn_grad_groups(gb):
    return {"w_gate": gb["w_gate"], "w_up": gb["w_up"], "w_down": gb["w_down"].reshape(N_DEV, FF_SHARD_PAD, D_MODEL)}


def _grads_from_groups(tot):
    g = dict(tot)
    g["w_uq"] = tot["w_uq"][:, :QK_NOPE + QK_ROPE]
    g["w_uk"], g["w_uv"] = tot["w_uk"][:, :QK_NOPE], tot["w_uv"][:, :V_HEAD]
    g["w_gate"], g["w_up"] = tot["w_gate"][:, :FF_SHARD], tot["w_up"][:, :FF_SHARD]
    g["w_down"] = tot["w_down"][:FF_SHARD]
    return g


SMALL_GROUPS = (
    (D_MODEL, ("mix_norm_pre", "mix_norm_post", "ffn_norm_pre", "ffn_norm_post")),
    (CONV_C, ("conv_w", "conv_b", "conv_ln_g", "conv_ln_b", "pool_scale")),
    (Q_RANK, ("q_norm",)), (KV_RANK, ("kv_norm",)), (POOL_GD, ("pool_w",)),
)


def _small_rows(name):
    return {"conv_w": CONV_HALO, "pool_w": POOL_G * POOL_GD}.get(name, SUBLANES)


def _small_groups(small):
    out = []
    for width, names in SMALL_GROUPS:
        parts = []
        for l in range(DEPTH):
            for n in names:
                part = small[l][n].reshape(-1, width)
                parts.append(_pad_axis(part, 0, _small_rows(n)))
        out.append(jnp.concatenate(parts, axis=0))
    return out


def _small_from_groups(groups):
    shapes = {"conv_w": (CONV_W, CONV_C), "pool_w": (POOL_G, POOL_GD, POOL_GD)}
    out = {}
    for (width, names), g in zip(SMALL_GROUPS, groups):
        row = 0
        for l in range(DEPTH):
            for n in names:
                rows = _small_rows(n)
                real = {"conv_w": CONV_W, "pool_w": POOL_G * POOL_GD}.get(n, 1)
                out.setdefault(n, []).append(g[row:row + real].reshape(shapes.get(n, (width,))))
                row += rows
    return {n: jnp.stack(v) for n, v in out.items()}


def _mixer_fwd(x, h, tables, sm, plan, l):
    nm = lambda n: f"{n}_l{l}"
    if h is None:
        h = _rms_fwd(x, (D_MODEL, 0), sm["mix_norm_pre"], BF16, nm("mix_pre_norm"))
    w_in, ride = plan.w_in(l), plan.in_proj_ride(l)
    if ride is None:
        z = _matmul(h, w_in, "nn", BF16, nm("in_proj"))
    else:
        z, rode = _matmul(h, w_in, "nn", BF16, nm("in_proj"), ride=ride)
        plan.in_proj_done(l, rode)
    w = dict(plan.mixer_weights(l), w_in=w_in)
    cq = _rms_fwd(z, ZC_Q, sm["q_norm"], BF16, nm("q_norm"))
    ckv = _rms_fwd(z, ZC_KV, sm["kv_norm"], BF16, nm("kv_norm"))
    q, k, v = _qkv_up_fwd(cq, ckv, z, w["w_uq"], w["w_uk"], w["w_uv"], tables, nm("qkv_up"))
    (o, lse), rode = _flash_fwd(q, k, v, nm("flash_fwd"), plan.fwd_ride(l))
    plan.fwd_done(l, rode)
    y_attn = _matmul(o, w["w_attn_o"], "nn", BF16, nm("attn_out"))
    hc, co = _conv_fwd(z, sm["conv_w"], sm["conv_b"], sm["conv_ln_g"], sm["conv_ln_b"], nm("conv_fwd"))
    y_conv = _matmul(hc, w["w_conv_o"], "nn", BF16, nm("conv_out"))
    pm = _pool_fwd(z, sm["pool_w"], sm["pool_scale"], nm("pool_fwd"))
    y_pool = _matmul(pm, w["w_pool_o"], "nn", BF16, nm("pool_out"))
    ys = (y_attn, y_conv, y_pool)
    merged = _merge_fwd(z, ys, nm("merge_fwd"))
    mo = _matmul(merged, w["w_mix_o"], "nn", F32, nm("mix_out"))
    x_mid, h2 = _rms_fwd(mo, (D_MODEL, 0), sm["mix_norm_post"], F32, nm("mix_post_norm"), res=x, then=sm["ffn_norm_pre"])
    saved = dict(x=x, h=h, z=z, cq=cq, ckv=ckv, q=q, k=k, v=v, o=o, lse=lse, hc=hc, co=co, pm=pm, ys=ys, merged=merged,
                 mo=mo)
    return x_mid, h2, saved, w


def _ffn_fwd(x_mid, h2, w, sm, tag, next_gain):
    nm = lambda n: f"{n}_{tag}"
    hg, hu, act = _ffn_up_fwd(h2, w["w_gate"], w["w_up"], nm("ffn_up_fwd"))
    fo = _matmul(act, w["w_down"], "nn", F32, nm("ffn_down"))
    out = _rms_fwd(fo, (D_MODEL, 0), sm["ffn_norm_post"], F32, nm("ffn_post_norm"), res=x_mid, then=next_gain)
    out, h_next = out if next_gain is not None else (out, None)
    saved = dict(x_mid=x_mid, h2=h2, hg=hg, hu=hu, act=act, fo=fo)
    return out, h_next, saved


def _ffn_bwd(dout, sv, w, sm, tag):
    nm = lambda n: f"{n}_{tag}"
    gb, gs = {}, {}
    dfo, gs["ffn_norm_post"] = _rms_bwd(sv["fo"], (D_MODEL, 0), sm["ffn_norm_post"], dout, BF16, nm("ffn_post_norm_bwd"))
    gb["w_down"] = _matmul(sv["act"], dfo, "tn", BF16, nm("ffn_down_dw"))
    dhg, dhu = _ffn_down_bwd(dfo, w["w_down"], sv["hg"], sv["hu"], nm("ffn_down_bwd"))
    dh2_g = _matmul(dhg, w["w_gate"], "nt", F32, nm("ffn_gate_dx"))
    dh2 = _matmul(dhu, w["w_up"], "nt", F32, nm("ffn_up_dx"), add=dh2_g)
    gb["w_gate"] = _matmul(sv["h2"], dhg, "tn", BF16, nm("ffn_gate_dw"), blocked=True)
    gb["w_up"] = _matmul(sv["h2"], dhu, "tn", BF16, nm("ffn_up_dw"), blocked=True)
    dmid, gs["ffn_norm_pre"] = _rms_bwd(sv["x_mid"], (D_MODEL, 0), sm["ffn_norm_pre"], dh2, F32, nm("ffn_pre_norm_bwd"),
                                        add=dout)
    return dmid, gb, gs


def _mixer_bwd(dmid, sv, tables, w, sm, plan, l, pack_small):
    nm = lambda n: f"{n}_l{l}"
    gb, gs = {}, {}
    dmo, gs["mix_norm_post"] = _rms_bwd(sv["mo"], (D_MODEL, 0), sm["mix_norm_post"], dmid, BF16, nm("mix_post_norm_bwd"))
    dmerged = _matmul(dmo, w["w_mix_o"], "nt", F32, nm("mix_out_dx"))
    gb["w_mix_o"] = _matmul(sv["merged"], dmo, "tn", BF16, nm("mix_out_dw"))
    dya, dyc, dyp, dz = _merge_bwd(sv["z"], sv["ys"], dmerged, nm("merge_bwd"))
    dpm = _matmul(dyp, w["w_pool_o"], "nt", F32, nm("pool_out_dx"))
    gb["w_pool_o"] = _matmul(sv["pm"], dyp, "tn", BF16, nm("pool_out_dw"), blocked=True)
    dz, gs["pool_w"], gs["pool_scale"] = _pool_bwd(dpm, sv["z"], sm["pool_w"], sm["pool_scale"], dz, nm("pool_bwd"))
    dhc = _matmul(dyc, w["w_conv_o"], "nt", F32, nm("conv_out_dx"))
    gb["w_conv_o"] = _matmul(sv["hc"], dyc, "tn", BF16, nm("conv_out_dw"), blocked=True)
    dco, gs["conv_ln_g"], gs["conv_ln_b"], gs["conv_b"] = _conv_bwd_norm(dhc, sv["co"], sm["conv_ln_g"], sm["conv_ln_b"],
                                                                        nm("conv_bwd_norm"))
    dz, gs["conv_w"] = _conv_bwd_taps(dco, sv["z"], sm["conv_w"], dz, nm("conv_bwd_taps"))
    gb["w_attn_o"] = _matmul(sv["o"], dya, "tn", BF16, nm("attn_out_dw"), blocked=True)
    delta, dob = _attn_out_bwd(dya, w["w_attn_o"], sv["o"], nm("attn_out_bwd"))
    (dq, dk, dv), rode = _flash_bwd(sv["q"], sv["k"], sv["v"], dob, sv["lse"], delta, nm("flash_bwd"),
                                  plan.bwd_ride(l, gb))
    plan.bwd_done(l, rode)
    dqf, dkf, dz = _rope_qk_bwd(dq, dk, tables, dz, nm("rope_qk_bwd"))
    dcq_n = _matmul(dqf, w["w_uq"], "nt", F32, nm("q_up_dx"))
    gb["w_uq"] = _matmul(sv["cq"], dqf, "tn", BF16, nm("q_up_dw"), blocked=True)
    dckv_k = _matmul(dkf, w["w_uk"], "nt", F32, nm("k_up_dx"))
    dckv_n = _matmul(dv, w["w_uv"], "nt", F32, nm("v_up_dx"), add=dckv_k)
    gb["w_uk"] = _matmul(sv["ckv"], dkf, "tn", BF16, nm("k_up_dw"), blocked=True)
    gb["w_uv"] = _matmul(sv["ckv"], dv, "tn", BF16, nm("v_up_dw"), blocked=True)
    dz, gs["q_norm"] = _rms_bwd(sv["z"], ZC_Q, sm["q_norm"], dcq_n, BF16, nm("q_norm_bwd"), dz=dz)
    dz, gs["kv_norm"] = _rms_bwd(sv["z"], ZC_KV, sm["kv_norm"], dckv_n, BF16, nm("kv_norm_bwd"), dz=dz)
    gb["w_in"] = _matmul(sv["h"], dz, "tn", BF16, nm("in_proj_dw"))
    plan.add_grads(l, "mix", gb)
    ride, small_gathered = plan.tail_ride(l, pack_small(gs)), []
    if ride is None:
        dh = _matmul(dz, w["w_in"], "nt", F32, nm("in_proj_dx"))
    else:
        dh, rode = _matmul(dz, w["w_in"], "nt", F32, nm("in_proj_dx"), ride=ride)
        small_gathered = plan.tail_done(l, rode)
    dx, gs["mix_norm_pre"] = _rms_bwd(sv["x"], (D_MODEL, 0), sm["mix_norm_pre"], dh, F32, nm("mix_pre_norm_bwd"), add=dmid)
    return dx, gs, small_gathered


def _part_groups(part):
    return {"mix": MIX_GROUPS, "ffn": FFN_GROUPS, "early": MIX_EARLY, "late": MIX_LATE}[part]


class _Plan:
    def __init__(self, shards, conv_w):
        self.local = [_local_groups(shards, l) for l in range(DEPTH)]
        self.conv_w = conv_w
        self.gat, self.send, self.recv = {}, {}, {}

    @staticmethod
    def _riders(l):
        return [(l, "ffn")] + ([(l + 1, "mix")] if l + 1 < DEPTH else [])

    @staticmethod
    def _grad_riders(l):
        return [(l, "ffn"), (l, "early")] + ([(l + 1, "late")] if l + 1 < DEPTH else [])

    def gather_first(self):
        w_in, conv_w = _all_gather([self.local[0]["w_in"], self.conv_w], "gather_w_in_l0")
        self.gat[(0, "mix")] = {"w_in": w_in}
        return conv_w

    def w_in(self, l):
        return _arrange_w_in(self.gat[(l, "mix")]["w_in"])

    def in_proj_ride(self, l):
        return _GatherRide([self.local[0][g] for g in MIX_GROUPS[1:]]) if l == 0 else None

    def in_proj_done(self, l, outs):
        self.gat[(l, "mix")].update(zip(MIX_GROUPS[1:], outs))

    def fwd_ride(self, l):
        return _GatherRide([self.local[ll][g] for ll, part in self._riders(l) for g in _part_groups(part)])

    def fwd_done(self, l, outs):
        outs = list(outs)
        for ll, part in self._riders(l):
            self.gat[(ll, part)] = {g: outs.pop(0) for g in _part_groups(part)}

    def mixer_weights(self, l):
        return _mixer_weights(self.gat[(l, "mix")])

    def ffn_weights(self, l):
        return _ffn_weights(self.gat[(l, "ffn")])

    def add_grads(self, l, part, gb):
        if part == "ffn":
            self.send[(l, "ffn")] = _ffn_grad_groups(gb)
        else:
            self.send.setdefault((l, "late"), {}).update(_mixer_grad_groups({g: gb[g] for g in MIX_LATE if g in gb}))

    def bwd_ride(self, l, gb_early):
        self.send[(l, "early")] = _mixer_grad_groups({g: gb_early[g] for g in MIX_EARLY})
        return _ReduceRide([self.send[(ll, part)][g] for ll, part in self._grad_riders(l) for g in _part_groups(part)])

    def bwd_done(self, l, outs):
        outs = list(outs)
        for ll, part in self._grad_riders(l):
            self.recv[(ll, part)] = {g: outs.pop(0) for g in _part_groups(part)}

    def tail_ride(self, l, small_groups):
        if l > 0:
            return None
        send = [self.send[(0, "late")][g] for g in MIX_LATE]
        by_core = [a.reshape((4, 2) + a.shape[1:]).transpose((1, 0) + tuple(range(2, a.ndim + 1))) for a in send]
        core = lax.axis_index("c")
        own = [lax.dynamic_index_in_dim(a, core, axis=0, keepdims=False) for a in by_core]
        got = _swap_with_sibling(by_core, "reduce_d2d")
        pairs = [_add_pairs(a, b, f"reduce_pair_add_{g}") for g, a, b in zip(MIX_LATE, own, got)]
        return _Combo([_ChipExchangeRide(pairs), _GatherRide(small_groups)])

    def tail_done(self, l, outs):
        self.recv[(l, "late")] = dict(zip(MIX_LATE, outs[:len(MIX_LATE)]))
        return outs[len(MIX_LATE):]

    def finish(self):
        layers = []
        for l in range(DEPTH):
            tot = {g: _sum_blocks(a, f"reduce_sum_{g}_l{l}") for part in ("early", "late", "ffn")
                   for g, a in self.recv[(l, part)].items()}
            layers.append(_grads_from_groups(tot))
        return layers


def _local_step(x, positions, target, smalls, plan):
    tables = _rope_tables(positions)
    saved = []
    h, h_norm = x, None
    for l in range(DEPTH):
        h, h2, svm, wm = _mixer_fwd(h, h_norm, tables, smalls[l], plan, l)
        wf = plan.ffn_weights(l)
        next_gain = smalls[l + 1]["mix_norm_pre"] if l + 1 < DEPTH else None
        h, h_norm, svf = _ffn_fwd(h, h2, wf, smalls[l], f"l{l}", next_gain)
        saved.append((svm, svf, wm, wf))
    dy, sq = _loss_grad(h, target, "loss_grad")
    small = [None] * DEPTH
    for l in reversed(range(DEPTH)):
        svm, svf, wm, wf = saved[l]
        dmid, gbf, gsf = _ffn_bwd(dy, svf, wf, smalls[l], f"l{l}")
        plan.add_grads(l, "ffn", gbf)

        def pack_small(gs, l=l, gsf=gsf):
            if l > 0:
                return None
            return _small_groups([{**gsf, **gs, "mix_norm_pre": jnp.zeros((D_MODEL,), F32)}] + small[1:])

        dy, gsm, small_gathered = _mixer_bwd(dmid, svm, tables, wm, smalls[l], plan, l, pack_small)
        small[l] = {**gsf, **gsm}
    return sq, dy, small, small_gathered


def kernel(x, positions, mix_norm_pre, w_in, q_norm, w_uq, kv_norm, w_uk, w_uv, w_attn_o, conv_w, conv_b, conv_ln_g, conv_ln_b, w_conv_o, pool_w, pool_scale, w_pool_o, w_mix_o, mix_norm_post, ffn_norm_pre, w_gate, w_up, w_down, ffn_norm_post, loss_target, m_mix_norm_pre, m_w_in, m_q_norm, m_w_uq, m_kv_norm, m_w_uk, m_w_uv, m_w_attn_o, m_conv_w, m_conv_b, m_conv_ln_g, m_conv_ln_b, m_w_conv_o, m_pool_w, m_pool_scale, m_w_pool_o, m_w_mix_o, m_mix_norm_post, m_ffn_norm_pre, m_w_gate, m_w_up, m_w_down, m_ffn_norm_post, v_mix_norm_pre, v_w_in, v_q_norm, v_w_uq, v_kv_norm, v_w_uk, v_w_uv, v_w_attn_o, v_conv_w, v_conv_b, v_conv_ln_g, v_conv_ln_b, v_w_conv_o, v_pool_w, v_pool_scale, v_w_pool_o, v_w_mix_o, v_mix_norm_post, v_ffn_norm_pre, v_w_gate, v_w_up, v_w_down, v_ffn_norm_post):
    given = dict(locals())
    dev = 4 * lax.axis_index("x") + 2 * lax.axis_index("y") + lax.axis_index("c")

    plan = _Plan({n: given[n] for n in BIG}, conv_w)
    cw = CONV_C // N_DEV
    conv_w_full = plan.gather_first().transpose(1, 2, 0, 3).reshape(DEPTH, CONV_W, CONV_C)
    smalls = []
    for l in range(DEPTH):
        sm = {n: given[n][l] for n in SMALL if n != "conv_w"}
        sm["conv_w"] = _pad_axis(conv_w_full[l], 0, CONV_HALO)
        smalls.append(sm)

    sq, grad_x, small, small_groups = _local_step(x[0], positions[0], loss_target[0], smalls, plan)
    loss = lax.psum(0.5 / D_MODEL * jnp.sum(sq), ("x", "y", "c"))
    per_layer = plan.finish()
    views = {}
    for n in BIG:
        if n == "w_in":
            views[n] = jnp.stack([per_layer[l][n].T for l in range(DEPTH)], axis=1)
        elif n in LANE_MAJOR:
            views[n] = jnp.stack([per_layer[l][n].T for l in range(DEPTH)])
        else:
            views[n] = jnp.stack([per_layer[l][n] for l in range(DEPTH)])
    grads = {n: _from_lane_major(n, views[n]) for n in BIG}

    small_sum = _small_from_groups([_sum_blocks(g, f"sum_small_grads_{i}") for i, g in enumerate(small_groups)])
    last = _pad_axis(small[0]["mix_norm_pre"].reshape(1, D_MODEL), 0, SUBLANES)
    last_sum = _sum_blocks(_all_gather([last], "gather_last_norm_grad")[0], "sum_last_norm_grad")[0]
    small_sum["mix_norm_pre"] = small_sum["mix_norm_pre"].at[0].set(last_sum)
    for n in SMALL:
        grads[n] = small_sum[n]
    grads["conv_w"] = lax.dynamic_slice_in_dim(small_sum["conv_w"], dev * cw, cw, axis=2)

    delta, new_m, new_v = {}, {}, {}
    for n in WEIGHTS:
        g_view = views[n] if n in views else grads[n]
        w_view, m_view, v_view = [_lane_major(n, given[k]) for k in (n, "m_" + n, "v_" + n)]
        res = _adamw(w_view, g_view, m_view, v_view, f"adamw_{n}")
        delta[n], new_m[n], new_v[n] = [_from_lane_major(n, r) for r in res]
    return (loss, grad_x[None], *[grads[n] for n in WEIGHTS], *[delta[n] for n in WEIGHTS],
            *[new_m[n] for n in WEIGHTS], *[new_v[n] for n in WEIGHTS])
```

```python
import functools
import math

import jax
import jax.numpy as jnp
from jax import lax
from jax.experimental import pallas as pl
from jax.experimental.pallas import tpu as pltpu

F32, BF16 = jnp.float32, jnp.bfloat16
MESH = pl.DeviceIdType.MESH

LANES = 128
SUBLANES = 8
VMEM_LIMIT_BYTES = 56 * 1024 * 1024
MATMUL_VMEM_BYTES = 40 * 1024 * 1024

N_DEV = 8
D_MODEL = 1024
DEPTH = 2
N_HEADS = 8
QK_NOPE, QK_ROPE, V_HEAD = 64, 32, 64
HEAD_PAD = LANES
Q_RANK, KV_RANK = 384, 256
ROPE_THETA = 10000.0
CONV_C, CONV_W = 512, 31
CONV_HALO = 32
POOL_WINDOWS = (2, 4, 8, 16)
POOL_C, POOL_G = 512, 4
POOL_GD = POOL_C // POOL_G
D_FF = 2816
FF_SHARD = D_FF // N_DEV
FF_SHARD_PAD = 3 * LANES
D_FF_PAD = N_DEV * FF_SHARD_PAD
W_IN_SHARD = 660
EPS = 1e-6
ATTN_SCALE = 1.0 / math.sqrt(QK_NOPE + QK_ROPE)
LOG2E = 1.4426950408889634
LR, B1, B2, ADAM_EPS, WD, STEP = 0.001, 0.9, 0.999, 1e-08, 0.01, 10

Z_W = 5376
ZC_GATE = (1024, 0)
ZC_GATES = (3072, 0)
ZC_CONV_A = (512, 6)
ZC_CONV_G = (512, 7)
ZC_CONV = (1024, 3)
ZC_POOL = (512, 8)
ZC_Q = (384, 12)
ZC_KR = (128, 39)
ZC_KV = (256, 20)
ZC_QKR = (768, 6)
W_IN_PIECES = ((0, 384, 4608), (384, 640, 5120), (640, 672, 5056), (672, 1696, 3072), (1696, 2208, 4096),
               (2208, 5280, 0))

BIG = ("w_in", "w_uq", "w_uk", "w_uv", "w_attn_o", "w_conv_o", "w_pool_o", "w_mix_o", "w_gate", "w_up", "w_down")
SMALL = ("mix_norm_pre", "q_norm", "kv_norm", "conv_w", "conv_b", "conv_ln_g", "conv_ln_b", "pool_w", "pool_scale",
         "mix_norm_post", "ffn_norm_pre", "ffn_norm_post")
WEIGHTS = ("mix_norm_pre", "w_in", "q_norm", "w_uq", "kv_norm", "w_uk", "w_uv", "w_attn_o", "conv_w", "conv_b",
           "conv_ln_g", "conv_ln_b", "w_conv_o", "pool_w", "pool_scale", "w_pool_o", "w_mix_o", "mix_norm_post",
           "ffn_norm_pre", "w_gate", "w_up", "w_down", "ffn_norm_post")


def _params(*semantics):
    return pltpu.CompilerParams(dimension_semantics=semantics, vmem_limit_bytes=VMEM_LIMIT_BYTES)


def _tile(dim, cap):
    if dim <= cap:
        return dim
    for t in range(cap - cap % LANES, 0, -LANES):
        if dim % t == 0:
            return t
    raise ValueError(f"no tile for {dim} under {cap}")


def _row_tile(rows, row_bytes, budget=1 << 20):
    if rows * row_bytes <= budget:
        return rows
    cap = max(16, budget // row_bytes)
    for t in range(cap - cap % 16, 0, -16):
        if rows % t == 0:
            return t
    return rows


def _rows(ts, width, cidx=0):
    return pl.BlockSpec((ts, width), lambda i: (i, cidx))


def _fixed(shape):
    return pl.BlockSpec(shape, lambda *_: (0,) * len(shape))


def _sigmoid(x):
    return 1.0 / (1.0 + jnp.exp(-x))


def _matmul(a, b, mode, out_dtype, name, add=None, blocked=False, ride=None):
    nb = n_blk = 0
    blocked = blocked or b.ndim == 3
    if mode == "nn":
        (m, k) = a.shape
        n = b.shape[0] * b.shape[2] if blocked else b.shape[1]
    elif mode == "nt":
        (m, k) = a.shape
        n = b.shape[1] if blocked else b.shape[0]
    else:
        (k, m), n = a.shape, b.shape[1]
    if blocked:
        nb = b.shape[2] if mode != "tn" else n // N_DEV
    unit = nb if blocked and mode != "nt" else LANES
    out_bytes = jnp.dtype(out_dtype).itemsize + (4 if add is not None else 0)
    best = None
    for tn_c in range(unit, min(n, 1536) + 1, unit):
        for tm_c in sorted({256, 512, 1024, 2048, min(m, 2048)}):
            if n % tn_c or m % tm_c or (blocked and mode != "nt" and N_DEV % (tn_c // nb)):
                continue
            vmem = 2 * (tm_c * k * 2 + tn_c * k * 2 + tm_c * tn_c * out_bytes) + tm_c * tn_c * 4 + tn_c * k * 2
            if vmem <= MATMUL_VMEM_BYTES and (best is None or tm_c * tn_c / (tm_c + tn_c) > best[0]):
                best = (tm_c * tn_c / (tm_c + tn_c), tm_c, tn_c)
    if best is None:
        raise ValueError(f"{name}: no tiles for {m}x{n}x{k}")
    _, tm, tn = best
    if blocked:
        n_blk = N_DEV if mode == "nt" else tn // nb
    dims = {"nn": ((1,), (0,)), "nt": ((1,), (1,)), "tn": ((0,), (0,))}[mode]
    a_spec = pl.BlockSpec((k, tm), lambda i, j: (0, i)) if mode == "tn" else pl.BlockSpec((tm, k), lambda i, j: (i, 0))
    b_spec = pl.BlockSpec((tn, k), lambda i, j: (j, 0)) if mode == "nt" else pl.BlockSpec((k, tn), lambda i, j: (0, j))
    o_spec = pl.BlockSpec((tm, tn), lambda i, j: (i, j))
    out_shape = jax.ShapeDtypeStruct((m, n), out_dtype)
    if blocked and mode == "nn":
        b_spec = pl.BlockSpec((n_blk, k, nb), lambda i, j: (j, 0, 0))
    elif blocked and mode == "nt":
        b_spec = pl.BlockSpec((n_blk, tn, nb), lambda i, j: (0, j, 0))
    elif blocked:
        o_spec = pl.BlockSpec((n_blk, tm, nb), lambda i, j: (j, i, 0))
        out_shape = jax.ShapeDtypeStruct((N_DEV, m, nb), out_dtype)
    has_add = add is not None
    grid = (m // tm, n // tn)

    def body(*refs):
        (a_ref, b_ref, *rest), start, finish = _ride_hooks(ride, refs, 3 if has_add else 2, 1, grid)
        start()
        o_ref = rest[-1]
        if blocked and mode != "tn":
            bv = jnp.concatenate([b_ref[c] for c in range(n_blk)], axis=1) if n_blk > 1 else b_ref[0]
        else:
            bv = b_ref[...]
        total = lax.dot_general(a_ref[...], bv, (dims, ((), ())), preferred_element_type=F32)
        if has_add:
            total = total + rest[0][...]
        if blocked and mode == "tn":
            for c in range(n_blk):
                o_ref[c] = total[:, c * nb:(c + 1) * nb].astype(o_ref.dtype)
        else:
            o_ref[...] = total.astype(o_ref.dtype)
        finish()

    operands = (a, b, add) if has_add else (a, b)
    (out,), rode = _ride_call(ride, body, name, (out_shape,), grid, [a_spec, b_spec] + ([o_spec] if has_add else []),
                              (o_spec,), ("parallel", "parallel"), operands)
    return out if ride is None else (out, rode)


def _rms_fwd(x, win, gain, out_dtype, name, res=None, then=None):
    width, cidx = win
    s = x.shape[0]
    ts = min(s, 512)
    has_res, has_then = res is not None, then is not None

    def norm(v, g_ref):
        return (v * lax.rsqrt(jnp.mean(v * v, axis=-1, keepdims=True) + EPS)) * g_ref[...]

    def body(x_ref, g_ref, *rest):
        y = norm(x_ref[...].astype(F32), g_ref)
        if has_res:
            y = rest[0][...] + y
        o_ref = rest[-2] if has_then else rest[-1]
        o_ref[...] = y.astype(o_ref.dtype)
        if has_then:
            rest[-1][...] = norm(y, rest[-3]).astype(BF16)

    ops = (x, gain.reshape(1, width)) + ((res,) if has_res else ()) + ((then.reshape(1, width),) if has_then else ())
    out_shape = (jax.ShapeDtypeStruct((s, width), out_dtype),) + ((jax.ShapeDtypeStruct((s, width), BF16),) * has_then)
    out = pl.pallas_call(
        body, name=name, out_shape=out_shape, grid=(s // ts,),
        in_specs=([_rows(ts, width, cidx), _fixed((1, width))] + ([_rows(ts, width)] if has_res else [])
                  + ([_fixed((1, width))] if has_then else [])),
        out_specs=(_rows(ts, width),) * len(out_shape), compiler_params=_params("parallel"))(*ops)
    return out if has_then else out[0]


def _into(dz, n_inputs, out_index):
    return dict(in_specs=[ANY], operands=(dz,), input_output_aliases={n_inputs: out_index},
                out_shape=jax.ShapeDtypeStruct(dz.shape, dz.dtype))


def _rms_bwd(x, win, gain, dy, out_dtype, name, add=None, dz=None):
    width, cidx = win
    s = x.shape[0]
    ts = min(s, 512)
    has_add = add is not None

    def body(x_ref, g_ref, dy_ref, *rest):
        dx_ref, dg_ref = rest[-2], rest[-1]
        xv = x_ref[...].astype(F32)
        r = lax.rsqrt(jnp.mean(xv * xv, axis=-1, keepdims=True) + EPS)
        xh = xv * r
        dyv = dy_ref[...].astype(F32)
        dyg = dyv * g_ref[...]
        dx = r * (dyg - xh * jnp.mean(dyg * xh, axis=-1, keepdims=True))
        if has_add:
            dx = dx + rest[0][...]
        dx_ref[...] = dx.astype(dx_ref.dtype)

        @pl.when(pl.program_id(0) == 0)
        def _():
            dg_ref[...] = jnp.zeros_like(dg_ref)

        dg_ref[...] += jnp.sum(dyv * xh, axis=0, keepdims=True)

    ops = (x, gain.reshape(1, width), dy) + ((add,) if has_add else ())
    in_specs = [_rows(ts, width, cidx), _fixed((1, width)), _rows(ts, width)] + ([_rows(ts, width)] if has_add else [])
    dx_shape, dx_spec, alias = jax.ShapeDtypeStruct((s, width), out_dtype), _rows(ts, width), {}
    if dz is not None:
        into = _into(dz, len(ops), 0)
        ops, in_specs, alias = ops + into["operands"], in_specs + into["in_specs"], into["input_output_aliases"]
        dx_shape, dx_spec = into["out_shape"], _rows(ts, width, cidx)
    dx, dg = pl.pallas_call(
        body, name=name, out_shape=(dx_shape, jax.ShapeDtypeStruct((1, width), F32)), grid=(s // ts,),
        in_specs=in_specs, out_specs=(dx_spec, _fixed((1, width))), input_output_aliases=alias,
        compiler_params=_params("arbitrary"))(*ops)
    return dx, dg.reshape(width)


def _rope(x, c, s1, s2):
    return x * c + pltpu.roll(x, 16, 1) * s1 + pltpu.roll(x, LANES - 16, 1) * s2


def _rope_t(g, c, s1, s2):
    return g * c + pltpu.roll(g * s1, LANES - 16, 1) + pltpu.roll(g * s2, 16, 1)


def _rope_tables(positions):
    inv_freq = ROPE_THETA ** (-jnp.arange(0, QK_ROPE, 2, dtype=F32) / QK_ROPE)
    ang = positions.astype(F32)[:, None] * inv_freq
    cos, sin = jnp.cos(ang), jnp.sin(ang)
    n = positions.shape[0]
    one, zero = jnp.ones((n, 1), F32), jnp.zeros((n, 1), F32)
    c = jnp.concatenate([jnp.tile(one, (1, QK_NOPE)), cos, cos, jnp.tile(one, (1, 32))], axis=1)
    s1 = jnp.concatenate([jnp.tile(zero, (1, QK_NOPE + 16)), sin, jnp.tile(zero, (1, 32))], axis=1)
    s2 = jnp.concatenate([jnp.tile(zero, (1, QK_NOPE)), -sin, jnp.tile(zero, (1, 48))], axis=1)
    return c, s1, s2


def _qkv_up_fwd(cq, ckv, z, w_uq, w_uk, w_uv, tables, name):
    s = cq.shape[0]
    ts = min(s, 512)
    hw = N_HEADS * HEAD_PAD

    def body(cq_ref, ckv_ref, kr_ref, wq_ref, wk_ref, wv_ref, c_ref, s1_ref, s2_ref, q_ref, k_ref, v_ref):
        c, s1, s2 = c_ref[...], s1_ref[...], s2_ref[...]
        cqv, ckvv = cq_ref[...], ckv_ref[...]
        kr = _rope(kr_ref[...].astype(F32), c, s1, s2)
        for h in range(N_HEADS):
            sl = slice(h * HEAD_PAD, (h + 1) * HEAD_PAD)
            q_ref[:, sl] = _rope(jnp.dot(cqv, wq_ref[h], preferred_element_type=F32), c, s1, s2).astype(BF16)
            k_ref[:, sl] = (jnp.dot(ckvv, wk_ref[h], preferred_element_type=F32) + kr).astype(BF16)
            v_ref[:, sl] = jnp.dot(ckvv, wv_ref[h], preferred_element_type=F32).astype(BF16)

    tab = _rows(ts, LANES)
    return pl.pallas_call(
        body, name=name, out_shape=(jax.ShapeDtypeStruct((s, hw), BF16),) * 3, grid=(s // ts,),
        in_specs=[_rows(ts, Q_RANK), _rows(ts, KV_RANK), _rows(ts, *ZC_KR), _fixed(w_uq.shape), _fixed(w_uk.shape),
                  _fixed(w_uv.shape), tab, tab, tab],
        out_specs=(_rows(ts, hw),) * 3, compiler_params=_params("parallel"))(cq, ckv, z, w_uq, w_uk, w_uv, *tables)


def _qkv_up_bwd(dq, dk, dv, z, w_uq, w_uk, w_uv, tables, q_gain, kv_gain, dz, name):
    s = dq.shape[0]
    ts = min(s, 512)
    hw = N_HEADS * HEAD_PAD
    zw = ZC_QKR[0]
    kv0 = Q_RANK + LANES
    dims_nt = (((1,), (1,)), ((), ()))

    def norm_bwd(xv, g_ref, dyv):
        r = lax.rsqrt(jnp.mean(xv * xv, axis=-1, keepdims=True) + EPS)
        xh = xv * r
        dyg = dyv * g_ref[...]
        return r * (dyg - xh * jnp.mean(dyg * xh, axis=-1, keepdims=True)), jnp.sum(dyv * xh, axis=0, keepdims=True)

    def body(dq_ref, dk_ref, dv_ref, z_ref, wq_ref, wk_ref, wv_ref, c_ref, s1_ref, s2_ref, gq_ref, gkv_ref, _,
             dqf_ref, dkf_ref, dz_ref, dgq_ref, dgkv_ref):
        c, s1, s2 = c_ref[...], s1_ref[...], s2_ref[...]
        ksum = jnp.zeros((ts, HEAD_PAD), F32)
        dcq = jnp.zeros((ts, Q_RANK), F32)
        dckv = jnp.zeros((ts, KV_RANK), F32)
        for h in range(N_HEADS):
            sl = slice(h * HEAD_PAD, (h + 1) * HEAD_PAD)
            dqh = _rope_t(dq_ref[:, sl], c, s1, s2).astype(BF16)
            dkv = dk_ref[:, sl]
            dkh = dkv.astype(BF16)
            dqf_ref[:, sl] = dqh
            dkf_ref[:, sl] = dkh
            ksum = ksum + dkv
            dcq = dcq + lax.dot_general(dqh, wq_ref[h], dims_nt, preferred_element_type=F32)
            dckv = dckv + (lax.dot_general(dkh, wk_ref[h], dims_nt, preferred_element_type=F32)
                           + lax.dot_general(dv_ref[:, sl], wv_ref[h], dims_nt, preferred_element_type=F32))
        dxq, dgq = norm_bwd(z_ref[:, pl.ds(0, Q_RANK)].astype(F32), gq_ref, dcq)
        dxkv, dgkv = norm_bwd(z_ref[:, pl.ds(kv0, KV_RANK)].astype(F32), gkv_ref, dckv)
        lane = lax.broadcasted_iota(jnp.int32, (ts, HEAD_PAD), 1)
        in_rope = (lane >= QK_NOPE) & (lane < QK_NOPE + QK_ROPE)
        dz_ref[:, pl.ds(0, Q_RANK)] = dxq.astype(BF16)
        dz_ref[:, pl.ds(Q_RANK, LANES)] = jnp.where(in_rope, _rope_t(ksum, c, s1, s2), 0.0).astype(BF16)
        dz_ref[:, pl.ds(kv0, KV_RANK)] = dxkv.astype(BF16)

        @pl.when(pl.program_id(0) == 0)
        def _():
            dgq_ref[...] = jnp.zeros_like(dgq_ref)
            dgkv_ref[...] = jnp.zeros_like(dgkv_ref)

        dgq_ref[...] += dgq
        dgkv_ref[...] += dgkv

    tab = _rows(ts, LANES)
    into = _into(dz, 12, 2)
    dqf, dkf, dz, dgq, dgkv = pl.pallas_call(
        body, name=name,
        out_shape=(jax.ShapeDtypeStruct((s, hw), BF16), jax.ShapeDtypeStruct((s, hw), BF16), into["out_shape"],
                   jax.ShapeDtypeStruct((1, Q_RANK), F32), jax.ShapeDtypeStruct((1, KV_RANK), F32)),
        grid=(s // ts,),
        in_specs=[_rows(ts, hw), _rows(ts, hw), _rows(ts, hw), _rows(ts, *ZC_QKR), _fixed(w_uq.shape), _fixed(w_uk.shape),
                  _fixed(w_uv.shape), tab, tab, tab, _fixed((1, Q_RANK)), _fixed((1, KV_RANK))] + into["in_specs"],
        out_specs=(_rows(ts, hw), _rows(ts, hw), _rows(ts, *ZC_QKR), _fixed((1, Q_RANK)), _fixed((1, KV_RANK))),
        input_output_aliases=into["input_output_aliases"], compiler_params=_params("arbitrary"))(
            dq, dk, dv, z, w_uq, w_uk, w_uv, *tables, q_gain.reshape(1, -1), kv_gain.reshape(1, -1), dz)
    return dqf, dkf, dz, dgq.reshape(-1), dgkv.reshape(-1)


def _attn_tile(s):
    return min(s, 512)


def _raw_scores(q, k, masked, row0=0):
    sc = lax.dot_general(q, k, (((1,), (1,)), ((), ())), preferred_element_type=F32)
    if masked:
        rows = row0 + lax.broadcasted_iota(jnp.int32, sc.shape, 0)
        cols = lax.broadcasted_iota(jnp.int32, sc.shape, 1)
        sc = jnp.where(cols <= rows, sc, -jnp.inf)
    return sc


def _ride_hooks(ride, refs, n_in, n_out, grid):
    if ride is None:
        return refs, lambda: None, lambda: None
    n = len(ride.arrays)
    own = refs[:n_in] + refs[n_in + n:n_in + n + n_out]
    ins, outs, sems = refs[n_in:n_in + n], refs[n_in + n + n_out:n_in + 2 * n + n_out], refs[n_in + 2 * n + n_out:]
    at_first = functools.reduce(lambda a, b: a & b, [pl.program_id(ax) == 0 for ax in range(len(grid))])
    at_last = functools.reduce(lambda a, b: a & b, [pl.program_id(ax) == g - 1 for ax, g in enumerate(grid)])
    return own, lambda: pl.when(at_first)(lambda: ride.start(ins, outs, sems)), \
        lambda: pl.when(at_last)(lambda: ride.finish(ins, outs, sems))


def _ride_call(ride, body, name, out_shape, grid, in_specs, out_specs, semantics, operands):
    n = 0 if ride is None else len(ride.arrays)
    res = pl.pallas_call(
        body, name=name, out_shape=tuple(out_shape) + (tuple(ride.out_shape) if n else ()), grid=grid,
        in_specs=list(in_specs) + [ANY] * n, out_specs=tuple(out_specs) + (ANY,) * n,
        scratch_shapes=list(ride.scratch) if n else [],
        compiler_params=_params(*(("arbitrary",) * len(grid) if n else semantics)))(*operands, *(ride.arrays if n else ()))
    return res[:len(out_shape)], list(res[len(out_shape):])


def _flash_fwd(q, k, v, name, ride=None):
    s = q.shape[0]
    t = _attn_tile(s)
    c2 = ATTN_SCALE * LOG2E
    grid = (N_HEADS, s // t)

    def body(*refs):
        (q_ref, k_ref, v_ref, o_ref, lse_ref), start, finish = _ride_hooks(ride, refs, 3, 2, grid)
        start()
        i = pl.program_id(1)
        qv = q_ref[...]

        def chunk(j, carry, masked):
            m_old, l_old, acc = carry
            at = pl.ds(pl.multiple_of(j * t, t), t)
            sc = _raw_scores(qv, k_ref[at, :], masked)
            m_new = jnp.maximum(m_old, jnp.max(sc, axis=-1, keepdims=True))
            p = jnp.exp2((sc - m_new) * c2)
            alpha = jnp.exp2((m_old - m_new) * c2)
            l_new = alpha * l_old + jnp.sum(p, axis=-1, keepdims=True)
            acc = alpha * acc + jnp.dot(p.astype(BF16), v_ref[at, :], preferred_element_type=F32)
            return m_new, l_new, acc

        init = (jnp.full((t, 1), -jnp.inf, F32), jnp.zeros((t, 1), F32), jnp.zeros((t, HEAD_PAD), F32))
        carry = lax.fori_loop(0, i, lambda j, cr: chunk(j, cr, False), init)
        m_fin, l_fin, acc = chunk(i, carry, True)
        o_ref[...] = (acc / l_fin).astype(o_ref.dtype)
        lse_ref[...] = jnp.broadcast_to(m_fin * ATTN_SCALE + jnp.log(l_fin), (t, HEAD_PAD))
        finish()

    qo = pl.BlockSpec((t, HEAD_PAD), lambda h, i: (i, h))
    whole = pl.BlockSpec((s, HEAD_PAD), lambda h, i: (0, h))
    return _ride_call(
        ride, body, name, (jax.ShapeDtypeStruct(q.shape, BF16), jax.ShapeDtypeStruct(q.shape, F32)), grid,
        [qo, whole, whole], (qo, qo), ("parallel", "parallel"), (q, k, v))


def _attn_out_bwd(dya, w_attn_o, o, name):
    s, d = dya.shape
    hw = N_HEADS * HEAD_PAD
    t = _attn_tile(s)

    def body(d_ref, w_ref, o_ref, delta_ref, dob_ref):
        wv = jnp.concatenate([w_ref[c] for c in range(N_DEV)], axis=1)
        do = lax.dot_general(d_ref[...], wv, (((1,), (1,)), ((), ())), preferred_element_type=F32)
        for h in range(N_HEADS):
            sl = slice(h * HEAD_PAD, (h + 1) * HEAD_PAD)
            dov = do[:, sl]
            delta_ref[:, sl] = jnp.broadcast_to(jnp.sum(dov * o_ref[:, sl].astype(F32), axis=-1, keepdims=True),
                                                (t, HEAD_PAD))
            dob_ref[:, sl] = dov.astype(BF16)

    blk = _rows(t, hw)
    return pl.pallas_call(
        body, name=name, out_shape=(jax.ShapeDtypeStruct(o.shape, F32), jax.ShapeDtypeStruct(o.shape, BF16)),
        grid=(s // t,), in_specs=[_rows(t, d), _fixed(w_attn_o.shape), blk], out_specs=(blk, blk),
        compiler_params=_params("parallel"))(dya, w_attn_o, o)


def _flash_bwd(q, k, v, do, lse, delta, name, ride=None):
    s = q.shape[0]
    t = _attn_tile(s)
    nt = s // t
    c2 = ATTN_SCALE * LOG2E
    grid = (N_HEADS, nt)

    def body(*refs):
        (q_ref, k_ref, v_ref, do_ref, lse_ref, delta_ref, dq_ref, dk_ref, dv_ref), start, finish = _ride_hooks(
            ride, refs, 6, 3, grid)
        start()
        j = pl.program_id(1)
        kv, vv = k_ref[...], v_ref[...]

        @pl.when(j == 0)
        def _():
            dq_ref[...] = jnp.zeros_like(dq_ref)

        def chunk(i, carry, masked):
            dk_acc, dv_acc = carry
            at = pl.ds(pl.multiple_of(i * t, t), t)
            qi, doi = q_ref[at, :], do_ref[at, :]
            sc = _raw_scores(qi, kv, masked)
            p = jnp.exp2(sc * c2 - lse_ref[at, pl.ds(0, 1)] * LOG2E)
            dp = lax.dot_general(doi, vv, (((1,), (1,)), ((), ())), preferred_element_type=F32)
            ds = (p * (dp - delta_ref[at, pl.ds(0, 1)])).astype(BF16)
            dv_acc = dv_acc + lax.dot_general(p.astype(BF16), doi, (((0,), (0,)), ((), ())), preferred_element_type=F32)
            dk_acc = dk_acc + lax.dot_general(ds, qi, (((0,), (0,)), ((), ())), preferred_element_type=F32)
            dq_ref[at, :] += jnp.dot(ds, kv, preferred_element_type=F32) * ATTN_SCALE
            return dk_acc, dv_acc

        zero = jnp.zeros((t, HEAD_PAD), F32)
        carry = chunk(j, (zero, zero), True)
        dk_acc, dv_acc = lax.fori_loop(j + 1, nt, lambda i, cr: chunk(i, cr, False), carry)
        dk_ref[...] = dk_acc * ATTN_SCALE
        dv_ref[...] = dv_acc.astype(BF16)
        finish()

    blk = pl.BlockSpec((t, HEAD_PAD), lambda h, j: (j, h))
    whole = pl.BlockSpec((s, HEAD_PAD), lambda h, j: (0, h))
    return _ride_call(
        ride, body, name, (jax.ShapeDtypeStruct(q.shape, F32), jax.ShapeDtypeStruct(q.shape, F32),
                           jax.ShapeDtypeStruct(q.shape, BF16)), grid,
        [whole, blk, blk, whole, whole, whole], (whole, blk, blk), ("parallel", "arbitrary"), (q, k, v, do, lse, delta))


def _conv_tile(s):
    return min(s, 256)


def _halo_before(t, width, cidx):
    per = t // CONV_HALO
    return pl.BlockSpec((CONV_HALO, width), lambda i: (jnp.maximum(i * per - 1, 0), cidx))


def _halo_after(t, width, cidx, n_tiles):
    per = t // CONV_HALO
    last = n_tiles * per - 1
    return pl.BlockSpec((CONV_HALO, width), lambda i: (jnp.minimum((i + 1) * per, last), cidx))


def _fill_glu(hbuf, ap_ref, gp_ref, a_ref, g_ref, t):
    first = pl.program_id(0) == 0
    hbuf[pl.ds(0, CONV_HALO), :] = jnp.where(first, 0.0, ap_ref[...].astype(F32) * _sigmoid(gp_ref[...].astype(F32)))
    hbuf[pl.ds(CONV_HALO, t), :] = a_ref[...].astype(F32) * _sigmoid(g_ref[...].astype(F32))


def _phase_copies(dst, src, t):
    n = t + CONV_HALO - SUBLANES
    for s in range(1, SUBLANES):
        dst[s, pl.ds(0, n), :] = src[pl.ds(s, n), :]


def _window(phases, src, k, t):
    if k % SUBLANES == 0:
        return src[pl.ds(k, t), :]
    return phases[k % SUBLANES, pl.ds(k - k % SUBLANES, t), :]


def _layer_norm_parts(co):
    mu = jnp.mean(co, axis=-1, keepdims=True)
    xc = co - mu
    rstd = lax.rsqrt(jnp.mean(xc * xc, axis=-1, keepdims=True) + EPS)
    return xc * rstd, rstd


def _conv_fwd(z, conv_w, conv_b, ln_g, ln_b, name):
    s = z.shape[0]
    t = _conv_tile(s)
    off = CONV_HALO - (CONV_W - 1)

    def body(ap_ref, gp_ref, a_ref, g_ref, w_ref, b_ref, lg_ref, lb_ref, hc_ref, co_ref, hbuf, hph):
        _fill_glu(hbuf, ap_ref, gp_ref, a_ref, g_ref, t)
        _phase_copies(hph, hbuf, t)
        acc = jnp.zeros((t, CONV_C), F32) + b_ref[...]
        for j in range(CONV_W):
            acc = acc + _window(hph, hbuf, off + j, t) * w_ref[pl.ds(j, 1), :]
        co_ref[...] = acc
        xh, _ = _layer_norm_parts(acc)
        y = xh * lg_ref[...] + lb_ref[...]
        hc_ref[...] = (y * _sigmoid(y)).astype(BF16)

    vec = _fixed((1, CONV_C))
    return pl.pallas_call(
        body, name=name, out_shape=(jax.ShapeDtypeStruct((s, CONV_C), BF16), jax.ShapeDtypeStruct((s, CONV_C), F32)),
        grid=(s // t,),
        in_specs=[_halo_before(t, *ZC_CONV_A), _halo_before(t, *ZC_CONV_G), _rows(t, *ZC_CONV_A), _rows(t, *ZC_CONV_G),
                  _fixed((CONV_HALO, CONV_C)), vec, vec, vec],
        out_specs=(_rows(t, CONV_C), _rows(t, CONV_C)),
        scratch_shapes=[pltpu.VMEM((t + CONV_HALO, CONV_C), F32), pltpu.VMEM((SUBLANES, t + CONV_HALO, CONV_C), F32)],
        compiler_params=_params("parallel"))(z, z, z, z, conv_w, conv_b.reshape(1, -1), ln_g.reshape(1, -1),
                                             ln_b.reshape(1, -1))


def _conv_bwd_norm(dhc, co, ln_g, ln_b, name):
    s = co.shape[0]
    t = min(s, 512)

    def body(dhc_ref, co_ref, lg_ref, lb_ref, dco_ref, dg_ref, db_ref, dcb_ref):
        xh, rstd = _layer_norm_parts(co_ref[...])
        y = xh * lg_ref[...] + lb_ref[...]
        sg = _sigmoid(y)
        dy = dhc_ref[...] * (sg * (1.0 + y * (1.0 - sg)))
        dxh = dy * lg_ref[...]
        dco = rstd * (dxh - jnp.mean(dxh, axis=-1, keepdims=True) - xh * jnp.mean(dxh * xh, axis=-1, keepdims=True))
        dco_ref[...] = dco

        @pl.when(pl.program_id(0) == 0)
        def _():
            dg_ref[...] = jnp.zeros_like(dg_ref)
            db_ref[...] = jnp.zeros_like(db_ref)
            dcb_ref[...] = jnp.zeros_like(dcb_ref)

        dg_ref[...] += jnp.sum(dy * xh, axis=0, keepdims=True)
        db_ref[...] += jnp.sum(dy, axis=0, keepdims=True)
        dcb_ref[...] += jnp.sum(dco, axis=0, keepdims=True)

    vec = _fixed((1, CONV_C))
    one = jax.ShapeDtypeStruct((1, CONV_C), F32)
    dco, dg, db, dcb = pl.pallas_call(
        body, name=name, out_shape=(jax.ShapeDtypeStruct((s, CONV_C), F32), one, one, one), grid=(s // t,),
        in_specs=[_rows(t, CONV_C), _rows(t, CONV_C), vec, vec], out_specs=(_rows(t, CONV_C), vec, vec, vec),
        compiler_params=_params("arbitrary"))(dhc, co, ln_g.reshape(1, -1), ln_b.reshape(1, -1))
    return dco, dg.reshape(-1), db.reshape(-1), dcb.reshape(-1)


def _conv_bwd_taps(dco, z, conv_w, dz, name):
    s = z.shape[0]
    t = _conv_tile(s)
    nt = s // t
    off = CONV_HALO - (CONV_W - 1)

    def body(ap_ref, gp_ref, a_ref, g_ref, d_ref, dn_ref, w_ref, _, du_ref, dw_ref, hbuf, dbuf, hph, dph):
        i = pl.program_id(0)
        _fill_glu(hbuf, ap_ref, gp_ref, a_ref, g_ref, t)
        dbuf[pl.ds(0, t), :] = d_ref[...]
        dbuf[pl.ds(t, CONV_HALO), :] = jnp.where(i == nt - 1, 0.0, dn_ref[...])
        _phase_copies(hph, hbuf, t)
        _phase_copies(dph, dbuf, t)

        @pl.when(i == 0)
        def _():
            dw_ref[...] = jnp.zeros_like(dw_ref)

        dcur = d_ref[...]
        dh = jnp.zeros((t, CONV_C), F32)
        for j in range(CONV_W):
            dh = dh + _window(dph, dbuf, CONV_W - 1 - j, t) * w_ref[pl.ds(j, 1), :]
            dw_ref[pl.ds(j, 1), :] += jnp.sum(dcur * _window(hph, hbuf, off + j, t), axis=0, keepdims=True)
        a, sg = a_ref[...].astype(F32), _sigmoid(g_ref[...].astype(F32))
        du_ref[:, pl.ds(0, CONV_C)] = (dh * sg).astype(BF16)
        du_ref[:, pl.ds(CONV_C, CONV_C)] = (dh * a * sg * (1.0 - sg)).astype(BF16)

    into = _into(dz, 7, 0)
    return pl.pallas_call(
        body, name=name, out_shape=(into["out_shape"], jax.ShapeDtypeStruct((CONV_HALO, CONV_C), F32)), grid=(nt,),
        in_specs=[_halo_before(t, *ZC_CONV_A), _halo_before(t, *ZC_CONV_G), _rows(t, *ZC_CONV_A), _rows(t, *ZC_CONV_G),
                  _rows(t, CONV_C), _halo_after(t, CONV_C, 0, nt), _fixed((CONV_HALO, CONV_C))] + into["in_specs"],
        out_specs=(_rows(t, *ZC_CONV), _fixed((CONV_HALO, CONV_C))), input_output_aliases=into["input_output_aliases"],
        scratch_shapes=[pltpu.VMEM((t + CONV_HALO, CONV_C), F32), pltpu.VMEM((t + CONV_HALO, CONV_C), F32),
                        pltpu.VMEM((SUBLANES, t + CONV_HALO, CONV_C), F32),
                        pltpu.VMEM((SUBLANES, t + CONV_HALO, CONV_C), F32)],
        compiler_params=_params("arbitrary"))(z, z, z, z, dco, dco, conv_w, dz)


def _pool_tile(s):
    return min(s, 512)


def _pool_counts(row0, n, window):
    rows = row0 + lax.broadcasted_iota(jnp.int32, (n, POOL_GD), 0)
    return jnp.minimum(rows + 1, window).astype(F32)


def _pool_diff(ubuf, gi, window, row0, t):
    lanes = pl.ds(gi * POOL_GD, POOL_GD)
    tot = ubuf[pl.ds(CONV_HALO, t), lanes]
    cur = tot
    for back in range(1, window):
        tot = tot + ubuf[pl.ds(CONV_HALO - back, t), lanes]
    return tot / _pool_counts(row0, t, window) - cur


def _pool_fwd(z, pool_w, pool_scale, name):
    s = z.shape[0]
    t = _pool_tile(s)

    def body(up_ref, u_ref, w_ref, sc_ref, m_ref, ubuf):
        i = pl.program_id(0)
        ubuf[pl.ds(0, CONV_HALO), :] = jnp.where(i == 0, 0.0, up_ref[...].astype(F32))
        ubuf[pl.ds(CONV_HALO, t), :] = u_ref[...].astype(F32)
        for gi, window in enumerate(POOL_WINDOWS):
            d = _pool_diff(ubuf, gi, window, i * t, t)
            mm = jnp.dot(d.astype(BF16), w_ref[gi].astype(BF16), preferred_element_type=F32)
            lanes = pl.ds(gi * POOL_GD, POOL_GD)
            m_ref[:, lanes] = (mm * sc_ref[:, lanes]).astype(BF16)

    return pl.pallas_call(
        body, name=name, out_shape=jax.ShapeDtypeStruct((s, POOL_C), BF16), grid=(s // t,),
        in_specs=[_halo_before(t, *ZC_POOL), _rows(t, *ZC_POOL), _fixed((POOL_G, POOL_GD, POOL_GD)), _fixed((1, POOL_C))],
        out_specs=_rows(t, POOL_C), scratch_shapes=[pltpu.VMEM((t + CONV_HALO, POOL_C), F32)],
        compiler_params=_params("parallel"))(z, z, pool_w, pool_scale.reshape(1, -1))


def _pool_bwd(dm, z, pool_w, pool_scale, dz, name):
    s = z.shape[0]
    t = _pool_tile(s)
    nt = s // t

    def body(up_ref, u_ref, dm_ref, dmn_ref, w_ref, sc_ref, _, du_ref, dw_ref, dsc_ref, ubuf, ebuf):
        i = pl.program_id(0)
        ubuf[pl.ds(0, CONV_HALO), :] = jnp.where(i == 0, 0.0, up_ref[...].astype(F32))
        ubuf[pl.ds(CONV_HALO, t), :] = u_ref[...].astype(F32)

        @pl.when(i == 0)
        def _():
            dw_ref[...] = jnp.zeros_like(dw_ref)
            dsc_ref[...] = jnp.zeros_like(dsc_ref)

        dm_next = jnp.where(i == nt - 1, 0.0, dmn_ref[...])
        for gi, window in enumerate(POOL_WINDOWS):
            lanes = pl.ds(gi * POOL_GD, POOL_GD)
            wb = w_ref[gi].astype(BF16)
            scale = sc_ref[:, lanes]
            d = _pool_diff(ubuf, gi, window, i * t, t).astype(BF16)
            mm = jnp.dot(d, wb, preferred_element_type=F32)
            dmv = dm_ref[:, lanes]
            dsc_ref[:, lanes] += jnp.sum(dmv * mm, axis=0, keepdims=True)
            dmm = (dmv * scale).astype(BF16)
            dw_ref[gi] += lax.dot_general(d, dmm, (((0,), (0,)), ((), ())), preferred_element_type=F32)
            dd = lax.dot_general(dmm, wb, (((1,), (1,)), ((), ())), preferred_element_type=F32)
            dd_next = lax.dot_general((dm_next[:, gi * POOL_GD:(gi + 1) * POOL_GD] * scale).astype(BF16), wb,
                                      (((1,), (1,)), ((), ())), preferred_element_type=F32)
            ebuf[pl.ds(0, t), lanes] = dd / _pool_counts(i * t, t, window)
            ebuf[pl.ds(t, CONV_HALO), lanes] = dd_next / _pool_counts((i + 1) * t, CONV_HALO, window)
            du = -dd
            for ahead in range(window):
                du = du + ebuf[pl.ds(ahead, t), lanes]
            du_ref[:, lanes] = du.astype(BF16)

    into = _into(dz, 6, 0)
    du, dw, dsc = pl.pallas_call(
        body, name=name,
        out_shape=(into["out_shape"], jax.ShapeDtypeStruct((POOL_G, POOL_GD, POOL_GD), F32),
                   jax.ShapeDtypeStruct((1, POOL_C), F32)), grid=(nt,),
        in_specs=[_halo_before(t, *ZC_POOL), _rows(t, *ZC_POOL), _rows(t, POOL_C), _halo_after(t, POOL_C, 0, nt),
                  _fixed((POOL_G, POOL_GD, POOL_GD)), _fixed((1, POOL_C))] + into["in_specs"],
        out_specs=(_rows(t, *ZC_POOL), _fixed((POOL_G, POOL_GD, POOL_GD)), _fixed((1, POOL_C))),
        input_output_aliases=into["input_output_aliases"],
        scratch_shapes=[pltpu.VMEM((t + CONV_HALO, POOL_C), F32), pltpu.VMEM((t + CONV_HALO, POOL_C), F32)],
        compiler_params=_params("arbitrary"))(z, z, dm, dm, pool_w, pool_scale.reshape(1, -1), dz)
    return du, dw, dsc.reshape(-1)


def _gate_specs(ts):
    width, first = ZC_GATE
    return [_rows(ts, width, first + b) for b in range(3)]


def _merge_fwd(z, ys, name):
    s = z.shape[0]
    ts = min(s, 256)

    def body(g0, g1, g2, y0, y1, y2, o_ref):
        o_ref[...] = sum(_sigmoid(g[...].astype(F32)) * y[...].astype(F32)
                         for g, y in ((g0, y0), (g1, y1), (g2, y2))).astype(BF16)

    return pl.pallas_call(
        body, name=name, out_shape=jax.ShapeDtypeStruct((s, D_MODEL), BF16), grid=(s // ts,),
        in_specs=_gate_specs(ts) + [_rows(ts, D_MODEL)] * 3, out_specs=_rows(ts, D_MODEL),
        compiler_params=_params("parallel"))(z, z, z, *ys)


def _merge_bwd(z, ys, dmerged, name):
    s = z.shape[0]
    ts = min(s, 256)

    def body(g0, g1, g2, y0, y1, y2, dm_ref, dy0, dy1, dy2, dz_ref):
        dmv = dm_ref[...]
        for b, (g_ref, y_ref, dy_ref) in enumerate(((g0, y0, dy0), (g1, y1, dy1), (g2, y2, dy2))):
            sg = _sigmoid(g_ref[...].astype(F32))
            dy_ref[...] = (dmv * sg).astype(BF16)
            dz_ref[:, pl.ds(b * D_MODEL, D_MODEL)] = (dmv * y_ref[...].astype(F32) * sg * (1.0 - sg)).astype(BF16)

    out = jax.ShapeDtypeStruct((s, D_MODEL), BF16)
    return pl.pallas_call(
        body, name=name, out_shape=(out,) * 3 + (jax.ShapeDtypeStruct((s, Z_W), BF16),), grid=(s // ts,),
        in_specs=_gate_specs(ts) + [_rows(ts, D_MODEL)] * 4,
        out_specs=(_rows(ts, D_MODEL),) * 3 + (_rows(ts, *ZC_GATES),),
        compiler_params=_params("parallel"))(z, z, z, *ys, dmerged)


def _ffn_up_fwd(h, w_gate, w_up, name):
    s, d = h.shape
    nb = w_gate.shape[2]
    f = N_DEV * nb
    tm, n_blk = min(s, 1024), 2
    tn = n_blk * nb
    blk = pl.BlockSpec((tm, tn), lambda i, j: (i, j))
    wspec = pl.BlockSpec((n_blk, d, nb), lambda i, j: (j, 0, 0))

    def body(h_ref, wg_ref, wu_ref, hg_ref, hu_ref, act_ref):
        hv = h_ref[...]
        g = jnp.dot(hv, jnp.concatenate([wg_ref[c] for c in range(n_blk)], axis=1), preferred_element_type=F32)
        u = jnp.dot(hv, jnp.concatenate([wu_ref[c] for c in range(n_blk)], axis=1), preferred_element_type=F32)
        hg_ref[...] = g.astype(hg_ref.dtype)
        hu_ref[...] = u.astype(hu_ref.dtype)
        act_ref[...] = (g * _sigmoid(g) * u).astype(BF16)

    return pl.pallas_call(
        body, name=name,
        out_shape=(jax.ShapeDtypeStruct((s, f), BF16),) * 3,
        grid=(s // tm, f // tn), in_specs=[pl.BlockSpec((tm, d), lambda i, j: (i, 0)), wspec, wspec],
        out_specs=(blk, blk, blk), compiler_params=_params("parallel", "parallel"))(h, w_gate, w_up)


def _ffn_down_bwd(dfo, w_down, hg, hu, name):
    s, d = dfo.shape
    f = w_down.shape[0]
    tm, tn = min(s, 1024), _tile(f, 1024)
    blk = pl.BlockSpec((tm, tn), lambda i, j: (i, j))

    def body(d_ref, w_ref, g_ref, u_ref, dg_ref, du_ref):
        dact = lax.dot_general(d_ref[...], w_ref[...], (((1,), (1,)), ((), ())), preferred_element_type=F32)
        g = g_ref[...].astype(F32)
        sg = _sigmoid(g)
        dg_ref[...] = (dact * u_ref[...].astype(F32) * (sg * (1.0 + g * (1.0 - sg)))).astype(BF16)
        du_ref[...] = (dact * g * sg).astype(BF16)

    out = jax.ShapeDtypeStruct((s, f), BF16)
    return pl.pallas_call(
        body, name=name, out_shape=(out, out), grid=(s // tm, f // tn),
        in_specs=[pl.BlockSpec((tm, d), lambda i, j: (i, 0)), pl.BlockSpec((tn, d), lambda i, j: (j, 0)), blk, blk],
        out_specs=(blk, blk), compiler_params=_params("parallel", "parallel"))(dfo, w_down, hg, hu)


def _loss_grad(y, target, name):
    s, d = y.shape
    ts = min(s, 512)

    def body(y_ref, t_ref, dy_ref, sq_ref):
        e = y_ref[...] - t_ref[...]
        dy_ref[...] = e / d

        @pl.when(pl.program_id(0) == 0)
        def _():
            sq_ref[...] = jnp.zeros_like(sq_ref)

        sq_ref[...] += jnp.sum(e * e, axis=0, keepdims=True)

    return pl.pallas_call(
        body, name=name, out_shape=(jax.ShapeDtypeStruct((s, d), F32), jax.ShapeDtypeStruct((1, d), F32)),
        grid=(s // ts,), in_specs=[_rows(ts, d), _rows(ts, d)], out_specs=(_rows(ts, d), _fixed((1, d))),
        compiler_params=_params("arbitrary"))(y, target)


def _adamw(w, g, m, v, name):
    shape = w.shape
    cols = shape[-1]
    keep3 = w.ndim == 3 and shape[1] < SUBLANES
    view = shape if keep3 else (math.prod(shape[:-1]), cols)
    rows = view[0]
    if keep3:
        cap = max(1, (1 << 20) // (SUBLANES * cols * 4))
        tr = max(t for t in range(1, cap + 1) if rows % t == 0)
    else:
        tr = _row_tile(rows, cols * 4)

    def body(w_ref, g_ref, m_ref, v_ref, d_ref, mo_ref, vo_ref):
        gv = g_ref[...]
        mn = B1 * m_ref[...] + (1.0 - B1) * gv
        vn = B2 * v_ref[...] + (1.0 - B2) * (gv * gv)
        m_hat = mn / (1.0 - B1 ** STEP)
        v_hat = vn / (1.0 - B2 ** STEP)
        d_ref[...] = -LR * (m_hat / (jnp.sqrt(v_hat) + ADAM_EPS) + WD * w_ref[...])
        mo_ref[...] = mn
        vo_ref[...] = vn

    spec = pl.BlockSpec((tr,) + view[1:], lambda i: (i,) + (0,) * (len(view) - 1))
    out = jax.ShapeDtypeStruct(view, F32)
    res = pl.pallas_call(
        body, name=name, out_shape=(out,) * 3, grid=(rows // tr,), in_specs=[spec] * 4, out_specs=(spec,) * 3,
        compiler_params=_params("parallel"))(*[t.reshape(view) for t in (w, g, m, v)])
    return tuple(r.reshape(shape) for r in res)


LANE_MAJOR = ("w_uq", "w_uk", "w_uv", "w_gate", "w_up")


def _lane_major(name, a):
    if name == "w_in":
        return a.transpose(2, 0, 1)
    if name in LANE_MAJOR:
        return a.transpose(0, 2, 1)
    return a


def _from_lane_major(name, a):
    if name == "w_in":
        return a.transpose(1, 2, 0)
    return _lane_major(name, a)


ANY = pl.BlockSpec(memory_space=pl.ANY)


class _GatherRide:
    def __init__(self, arrays):
        n = len(arrays)
        self.arrays = list(arrays)
        self.out_shape = [jax.ShapeDtypeStruct((N_DEV,) + a.shape, a.dtype) for a in arrays]
        self.scratch = [pltpu.SemaphoreType.DMA((n, 7)), pltpu.SemaphoreType.DMA((n, 7)), pltpu.SemaphoreType.DMA((n,))]

    def _copies(self, ins, outs, sems):
        send_sems, recv_sems, local_sems = sems
        n = len(self.arrays)
        x, y, c = lax.axis_index("x"), lax.axis_index("y"), lax.axis_index("c")
        me, sibling = (x, y, c), (x, y, 1 - c)
        chips = [(1 - x, y), (x, 1 - y), (1 - x, 1 - y)]

        def slot(a, px, py, pc):
            return outs[a].at[4 * px + 2 * py + pc]

        def copy(a, k, block, to, src=None):
            return pltpu.make_async_remote_copy(
                src_ref=slot(a, *block) if src is None else src, dst_ref=slot(a, *block), send_sem=send_sems.at[a, k],
                recv_sem=recv_sems.at[a, k], device_id=to, device_id_type=MESH)

        mine = [pltpu.make_async_copy(ins[a], slot(a, *me), local_sems.at[a]) for a in range(n)]
        first = []
        for a in range(n):
            first.append(copy(a, 0, me, sibling, src=ins[a]))
            first += [copy(a, 1 + j, me, (*chip, c), src=ins[a]) for j, chip in enumerate(chips)]
        return n, me, sibling, chips, c, copy, mine, first

    def start(self, ins, outs, sems):
        _, _, _, _, _, _, mine, first = self._copies(ins, outs, sems)
        for cp in mine + first:
            cp.start()

    def finish(self, ins, outs, sems):
        n, me, sibling, chips, c, copy, mine, first = self._copies(ins, outs, sems)
        passed = []
        for j, chip in enumerate(chips):
            for a in range(n):
                copy(a, 1 + j, (*chip, c), me).wait_recv()
                passed.append(copy(a, 4 + j, (*chip, c), sibling))
                passed[-1].start()
        for a in range(n):
            copy(a, 0, sibling, me).wait_recv()
            for j, chip in enumerate(chips):
                copy(a, 4 + j, (*chip, 1 - c), me).wait_recv()
        for cp in first + passed:
            cp.wait_send()
        for cp in mine:
            cp.wait()


class _ReduceRide:
    def __init__(self, arrays):
        n = len(arrays)
        self.arrays = list(arrays)
        self.out_shape = [jax.ShapeDtypeStruct(a.shape, a.dtype) for a in arrays]
        self.scratch = [pltpu.SemaphoreType.DMA((n, 7)), pltpu.SemaphoreType.DMA((n, 7)), pltpu.SemaphoreType.DMA((n,))]

    def _copies(self, ins, outs, sems):
        send_sems, recv_sems, local_sems = sems
        n = len(self.arrays)
        x, y, c = lax.axis_index("x"), lax.axis_index("y"), lax.axis_index("c")
        mine = [pltpu.make_async_copy(ins[a].at[4 * x + 2 * y + c], outs[a].at[0], local_sems.at[a]) for a in range(n)]
        copies = []
        for a in range(n):
            for k in range(1, N_DEV):
                px = 1 - x if k & 4 else x
                py = 1 - y if k & 2 else y
                pc = 1 - c if k & 1 else c
                copies.append(pltpu.make_async_remote_copy(
                    src_ref=ins[a].at[4 * px + 2 * py + pc], dst_ref=outs[a].at[k], send_sem=send_sems.at[a, k - 1],
                    recv_sem=recv_sems.at[a, k - 1], device_id=(px, py, pc), device_id_type=MESH))
        return mine, copies

    def start(self, ins, outs, sems):
        mine, copies = self._copies(ins, outs, sems)
        for cp in mine + copies:
            cp.start()

    def finish(self, ins, outs, sems):
        mine, copies = self._copies(ins, outs, sems)
        for cp in copies + mine:
            cp.wait()


def _run_ride(ride, name):
    n = len(ride.arrays)

    def body(*refs):
        ins, outs, sems = refs[:n], refs[n:2 * n], refs[2 * n:]
        ride.start(ins, outs, sems)
        ride.finish(ins, outs, sems)

    return pl.pallas_call(body, name=name, out_shape=ride.out_shape, in_specs=[ANY] * n, out_specs=[ANY] * n,
                          scratch_shapes=ride.scratch)(*ride.arrays)


def _all_gather(arrays, name):
    return _run_ride(_GatherRide(arrays), name)


def _swap_with_sibling(arrays, name):
    n = len(arrays)

    def body(*refs):
        ins, outs = refs[:n], refs[n:2 * n]
        send_sems, recv_sems = refs[2 * n:]
        x, y, c = lax.axis_index("x"), lax.axis_index("y"), lax.axis_index("c")
        copies = [pltpu.make_async_remote_copy(
            src_ref=ins[a].at[1 - c], dst_ref=outs[a], send_sem=send_sems.at[a], recv_sem=recv_sems.at[a],
            device_id=(x, y, 1 - c), device_id_type=MESH) for a in range(n)]
        for cp in copies:
            cp.start()
        for cp in copies:
            cp.wait()

    return pl.pallas_call(
        body, name=name, out_shape=[jax.ShapeDtypeStruct(a.shape[1:], a.dtype) for a in arrays],
        in_specs=[ANY] * n, out_specs=[ANY] * n,
        scratch_shapes=[pltpu.SemaphoreType.DMA((n,)), pltpu.SemaphoreType.DMA((n,))])(*arrays)


class _ChipExchangeRide:
    def __init__(self, arrays):
        n = len(arrays)
        self.arrays = list(arrays)
        self.out_shape = [jax.ShapeDtypeStruct(a.shape, a.dtype) for a in arrays]
        self.scratch = [pltpu.SemaphoreType.DMA((n, 3)), pltpu.SemaphoreType.DMA((n, 3)), pltpu.SemaphoreType.DMA((n,))]

    def _copies(self, ins, outs, sems):
        send_sems, recv_sems, local_sems = sems
        n = len(self.arrays)
        x, y, c = lax.axis_index("x"), lax.axis_index("y"), lax.axis_index("c")
        partners = [(x, 1 - y), (1 - x, y), (1 - x, 1 - y)]
        mine = [pltpu.make_async_copy(ins[a].at[2 * x + y], outs[a].at[0], local_sems.at[a]) for a in range(n)]
        copies = [pltpu.make_async_remote_copy(
            src_ref=ins[a].at[2 * px + py], dst_ref=outs[a].at[1 + k], send_sem=send_sems.at[a, k],
            recv_sem=recv_sems.at[a, k], device_id=(px, py, c), device_id_type=MESH)
            for a in range(n) for k, (px, py) in enumerate(partners)]
        return mine, copies

    def start(self, ins, outs, sems):
        mine, copies = self._copies(ins, outs, sems)
        for cp in mine + copies:
            cp.start()

    def finish(self, ins, outs, sems):
        mine, copies = self._copies(ins, outs, sems)
        for cp in copies + mine:
            cp.wait()


class _Combo:
    def __init__(self, rides):
        self.rides = rides
        self.arrays = [a for r in rides for a in r.arrays]
        self.out_shape = [o for r in rides for o in r.out_shape]
        self.scratch = [sc for r in rides for sc in r.scratch]

    def _parts(self, ins, outs, sems):
        at_a = at_s = 0
        for r in self.rides:
            na, ns = len(r.arrays), len(r.scratch)
            yield r, ins[at_a:at_a + na], outs[at_a:at_a + na], sems[at_s:at_s + ns]
            at_a, at_s = at_a + na, at_s + ns

    def start(self, ins, outs, sems):
        for r, i, o, sm in self._parts(ins, outs, sems):
            r.start(i, o, sm)

    def finish(self, ins, outs, sems):
        for r, i, o, sm in self._parts(ins, outs, sems):
            r.finish(i, o, sm)


def _as_rows(a, lead):
    return a.reshape(a.shape[:lead] + (math.prod(a.shape[lead:-1]), a.shape[-1]))


def _add_pairs(a, b, name):
    a2, b2 = _as_rows(a, 0), _as_rows(b, 0)
    rows, cols = a2.shape
    tr = _row_tile(rows, cols * 4)

    def body(a_ref, b_ref, o_ref):
        o_ref[...] = (a_ref[...].astype(F32) + b_ref[...].astype(F32)).astype(o_ref.dtype)

    spec = _rows(tr, cols)
    out = pl.pallas_call(body, name=name, out_shape=jax.ShapeDtypeStruct(a2.shape, a.dtype), grid=(rows // tr,),
                         in_specs=[spec, spec], out_specs=spec, compiler_params=_params("parallel"))(a2, b2)
    return out.reshape(a.shape)


def _sum_blocks(a, name):
    a3 = _as_rows(a, 1)
    n, rows, cols = a3.shape
    tr = _row_tile(rows, n * cols * 4)

    def body(a_ref, o_ref):
        tot = a_ref[0].astype(F32)
        for k in range(1, n):
            tot = tot + a_ref[k].astype(F32)
        o_ref[...] = tot

    out = pl.pallas_call(body, name=name, out_shape=jax.ShapeDtypeStruct((rows, cols), F32), grid=(rows // tr,),
                         in_specs=[pl.BlockSpec((n, tr, cols), lambda j: (0, j, 0))], out_specs=_rows(tr, cols),
                         compiler_params=_params("parallel"))(a3)
    return out.reshape(a.shape[1:])


MIX_GROUPS = ("w_in", "w_uq", "w_uk", "w_uv", "w_attn_o", "w_conv_o", "w_pool_o", "w_mix_o")
FFN_GROUPS = ("w_gate", "w_up", "w_down")
MIX_EARLY = ("w_attn_o", "w_conv_o", "w_pool_o", "w_mix_o")
MIX_LATE = ("w_in", "w_uq", "w_uk", "w_uv")


def _pad_axis(a, axis, size):
    pad = [(0, 0)] * a.ndim
    pad[axis] = (0, size - a.shape[axis])
    return jnp.pad(a, pad)


def _local_groups(sh, l):
    out = {n: sh[n][l] for n in BIG}
    for n in ("w_uq", "w_uk", "w_uv"):
        out[n] = _pad_axis(out[n], -1, HEAD_PAD)
    for n in ("w_gate", "w_up"):
        out[n] = _pad_axis(out[n], -1, FF_SHARD_PAD)
    out["w_down"] = _pad_axis(out["w_down"], 0, FF_SHARD_PAD)
    return {n: v.astype(BF16) for n, v in out.items()}


def _arrange_w_in(blocks):
    parts, pos = [], 0
    for ref_lo, ref_hi, at in sorted(W_IN_PIECES, key=lambda p: p[2]):
        if at > pos:
            parts.append(jnp.zeros((blocks.shape[1], at - pos), blocks.dtype))
        for d in range(N_DEV):
            lo, hi = max(ref_lo, d * W_IN_SHARD), min(ref_hi, (d + 1) * W_IN_SHARD)
            if lo < hi:
                parts.append(blocks[d][:, lo - d * W_IN_SHARD:hi - d * W_IN_SHARD])
        pos = at + ref_hi - ref_lo
    if pos < Z_W:
        parts.append(jnp.zeros((blocks.shape[1], Z_W - pos), blocks.dtype))
    return jnp.concatenate(parts, axis=1)


def _w_in_shard(g, d):
    parts = []
    for ref_lo, ref_hi, at in W_IN_PIECES:
        lo, hi = max(ref_lo, d * W_IN_SHARD), min(ref_hi, (d + 1) * W_IN_SHARD)
        if lo < hi:
            parts.append(g[:, at + lo - ref_lo:at + hi - ref_lo])
    return jnp.concatenate(parts, axis=1)


def _mixer_weights(gat):
    w = {n: v for n, v in gat.items() if n != "w_in"}
    attn_o = gat["w_attn_o"].reshape(N_DEV, N_HEADS, V_HEAD, LANES)
    w["w_attn_o"] = _pad_axis(attn_o, 2, HEAD_PAD).reshape(N_DEV, N_HEADS * HEAD_PAD, LANES)
    w["w_mix_o"] = gat["w_mix_o"].reshape(D_MODEL, D_MODEL)
    return w


def _ffn_weights(gat):
    return {"w_gate": gat["w_gate"], "w_up": gat["w_up"], "w_down": gat["w_down"].reshape(D_FF_PAD, D_MODEL)}


def _mixer_grad_groups(gb):
    g = dict(gb)
    if "w_in" in gb:
        g["w_in"] = jnp.stack([_w_in_shard(gb["w_in"], d) for d in range(N_DEV)])
    if "w_attn_o" in gb:
        attn_o = gb["w_attn_o"].reshape(N_DEV, N_HEADS, HEAD_PAD, LANES)[:, :, :V_HEAD]
        g["w_attn_o"] = attn_o.reshape(N_DEV, N_HEADS * V_HEAD, LANES)
    if "w_mix_o" in gb:
        g["w_mix_o"] = gb["w_mix_o"].reshape(N_DEV, D_MODEL // N_DEV, D_MODEL)
    return g


def _ffn_grad_groups(gb):
    return {"w_gate": gb["w_gate"], "w_up": gb["w_up"], "w_down": gb["w_down"].reshape(N_DEV, FF_SHARD_PAD, D_MODEL)}


def _grads_from_groups(tot):
    g = dict(tot)
    g["w_uq"] = tot["w_uq"][:, :QK_NOPE + QK_ROPE]
    g["w_uk"], g["w_uv"] = tot["w_uk"][:, :QK_NOPE], tot["w_uv"][:, :V_HEAD]
    g["w_gate"], g["w_up"] = tot["w_gate"][:, :FF_SHARD], tot["w_up"][:, :FF_SHARD]
    g["w_down"] = tot["w_down"][:FF_SHARD]
    return g


SMALL_GROUPS = (
    (D_MODEL, ("mix_norm_pre", "mix_norm_post", "ffn_norm_pre", "ffn_norm_post")),
    (CONV_C, ("conv_w", "conv_b", "conv_ln_g", "conv_ln_b", "pool_scale")),
    (Q_RANK, ("q_norm",)), (KV_RANK, ("kv_norm",)), (POOL_GD, ("pool_w",)),
)


def _small_rows(name):
    return {"conv_w": CONV_HALO, "pool_w": POOL_G * POOL_GD}.get(name, SUBLANES)


def _small_groups(small):
    out = []
    for width, names in SMALL_GROUPS:
        parts = []
        for l in range(DEPTH):
            for n in names:
                part = small[l][n].reshape(-1, width)
                parts.append(_pad_axis(part, 0, _small_rows(n)))
        out.append(jnp.concatenate(parts, axis=0))
    return out


def _small_from_groups(groups):
    shapes = {"conv_w": (CONV_W, CONV_C), "pool_w": (POOL_G, POOL_GD, POOL_GD)}
    out = {}
    for (width, names), g in zip(SMALL_GROUPS, groups):
        row = 0
        for l in range(DEPTH):
            for n in names:
                rows = _small_rows(n)
                real = {"conv_w": CONV_W, "pool_w": POOL_G * POOL_GD}.get(n, 1)
                out.setdefault(n, []).append(g[row:row + real].reshape(shapes.get(n, (width,))))
                row += rows
    return {n: jnp.stack(v) for n, v in out.items()}


def _mixer_fwd(x, h, tables, sm, plan, l):
    nm = lambda n: f"{n}_l{l}"
    if h is None:
        h = _rms_fwd(x, (D_MODEL, 0), sm["mix_norm_pre"], BF16, nm("mix_pre_norm"))
    w_in, ride = plan.w_in(l), plan.in_proj_ride(l)
    if ride is None:
        z = _matmul(h, w_in, "nn", BF16, nm("in_proj"))
    else:
        z, rode = _matmul(h, w_in, "nn", BF16, nm("in_proj"), ride=ride)
        plan.in_proj_done(l, rode)
    w = dict(plan.mixer_weights(l), w_in=w_in)
    cq = _rms_fwd(z, ZC_Q, sm["q_norm"], BF16, nm("q_norm"))
    ckv = _rms_fwd(z, ZC_KV, sm["kv_norm"], BF16, nm("kv_norm"))
    q, k, v = _qkv_up_fwd(cq, ckv, z, w["w_uq"], w["w_uk"], w["w_uv"], tables, nm("qkv_up"))
    (o, lse), rode = _flash_fwd(q, k, v, nm("flash_fwd"), plan.fwd_ride(l))
    plan.fwd_done(l, rode)
    y_attn = _matmul(o, w["w_attn_o"], "nn", BF16, nm("attn_out"))
    hc, co = _conv_fwd(z, sm["conv_w"], sm["conv_b"], sm["conv_ln_g"], sm["conv_ln_b"], nm("conv_fwd"))
    y_conv = _matmul(hc, w["w_conv_o"], "nn", BF16, nm("conv_out"))
    pm = _pool_fwd(z, sm["pool_w"], sm["pool_scale"], nm("pool_fwd"))
    y_pool = _matmul(pm, w["w_pool_o"], "nn", BF16, nm("pool_out"))
    ys = (y_attn, y_conv, y_pool)
    merged = _merge_fwd(z, ys, nm("merge_fwd"))
    mo = _matmul(merged, w["w_mix_o"], "nn", F32, nm("mix_out"))
    x_mid, h2 = _rms_fwd(mo, (D_MODEL, 0), sm["mix_norm_post"], F32, nm("mix_post_norm"), res=x, then=sm["ffn_norm_pre"])
    saved = dict(x=x, h=h, z=z, cq=cq, ckv=ckv, q=q, k=k, v=v, o=o, lse=lse, hc=hc, co=co, pm=pm, ys=ys, merged=merged,
                 mo=mo)
    return x_mid, h2, saved, w


def _ffn_fwd(x_mid, h2, w, sm, tag, next_gain):
    nm = lambda n: f"{n}_{tag}"
    hg, hu, act = _ffn_up_fwd(h2, w["w_gate"], w["w_up"], nm("ffn_up_fwd"))
    fo = _matmul(act, w["w_down"], "nn", F32, nm("ffn_down"))
    out = _rms_fwd(fo, (D_MODEL, 0), sm["ffn_norm_post"], F32, nm("ffn_post_norm"), res=x_mid, then=next_gain)
    out, h_next = out if next_gain is not None else (out, None)
    saved = dict(x_mid=x_mid, h2=h2, hg=hg, hu=hu, act=act, fo=fo)
    return out, h_next, saved


def _ffn_bwd(dout, sv, w, sm, tag):
    nm = lambda n: f"{n}_{tag}"
    gb, gs = {}, {}
    dfo, gs["ffn_norm_post"] = _rms_bwd(sv["fo"], (D_MODEL, 0), sm["ffn_norm_post"], dout, BF16, nm("ffn_post_norm_bwd"))
    gb["w_down"] = _matmul(sv["act"], dfo, "tn", BF16, nm("ffn_down_dw"))
    dhg, dhu = _ffn_down_bwd(dfo, w["w_down"], sv["hg"], sv["hu"], nm("ffn_down_bwd"))
    dh2_g = _matmul(dhg, w["w_gate"], "nt", F32, nm("ffn_gate_dx"))
    dh2 = _matmul(dhu, w["w_up"], "nt", F32, nm("ffn_up_dx"), add=dh2_g)
    gb["w_gate"] = _matmul(sv["h2"], dhg, "tn", BF16, nm("ffn_gate_dw"), blocked=True)
    gb["w_up"] = _matmul(sv["h2"], dhu, "tn", BF16, nm("ffn_up_dw"), blocked=True)
    dmid, gs["ffn_norm_pre"] = _rms_bwd(sv["x_mid"], (D_MODEL, 0), sm["ffn_norm_pre"], dh2, F32, nm("ffn_pre_norm_bwd"),
                                        add=dout)
    return dmid, gb, gs


def _mixer_bwd(dmid, sv, tables, w, sm, plan, l, pack_small):
    nm = lambda n: f"{n}_l{l}"
    gb, gs = {}, {}
    dmo, gs["mix_norm_post"] = _rms_bwd(sv["mo"], (D_MODEL, 0), sm["mix_norm_post"], dmid, BF16, nm("mix_post_norm_bwd"))
    dmerged = _matmul(dmo, w["w_mix_o"], "nt", F32, nm("mix_out_dx"))
    gb["w_mix_o"] = _matmul(sv["merged"], dmo, "tn", BF16, nm("mix_out_dw"))
    dya, dyc, dyp, dz = _merge_bwd(sv["z"], sv["ys"], dmerged, nm("merge_bwd"))
    dpm = _matmul(dyp, w["w_pool_o"], "nt", F32, nm("pool_out_dx"))
    gb["w_pool_o"] = _matmul(sv["pm"], dyp, "tn", BF16, nm("pool_out_dw"), blocked=True)
    dz, gs["pool_w"], gs["pool_scale"] = _pool_bwd(dpm, sv["z"], sm["pool_w"], sm["pool_scale"], dz, nm("pool_bwd"))
    dhc = _matmul(dyc, w["w_conv_o"], "nt", F32, nm("conv_out_dx"))
    gb["w_conv_o"] = _matmul(sv["hc"], dyc, "tn", BF16, nm("conv_out_dw"), blocked=True)
    dco, gs["conv_ln_g"], gs["conv_ln_b"], gs["conv_b"] = _conv_bwd_norm(dhc, sv["co"], sm["conv_ln_g"], sm["conv_ln_b"],
                                                                        nm("conv_bwd_norm"))
    dz, gs["conv_w"] = _conv_bwd_taps(dco, sv["z"], sm["conv_w"], dz, nm("conv_bwd_taps"))
    gb["w_attn_o"] = _matmul(sv["o"], dya, "tn", BF16, nm("attn_out_dw"), blocked=True)
    delta, dob = _attn_out_bwd(dya, w["w_attn_o"], sv["o"], nm("attn_out_bwd"))
    (dq, dk, dv), rode = _flash_bwd(sv["q"], sv["k"], sv["v"], dob, sv["lse"], delta, nm("flash_bwd"),
                                  plan.bwd_ride(l, gb))
    plan.bwd_done(l, rode)
    dqf, dkf, dz, gs["q_norm"], gs["kv_norm"] = _qkv_up_bwd(
        dq, dk, dv, sv["z"], w["w_uq"], w["w_uk"], w["w_uv"], tables, sm["q_norm"], sm["kv_norm"], dz, nm("qkv_up_bwd"))
    gb["w_uq"] = _matmul(sv["cq"], dqf, "tn", BF16, nm("q_up_dw"), blocked=True)
    gb["w_uk"] = _matmul(sv["ckv"], dkf, "tn", BF16, nm("k_up_dw"), blocked=True)
    gb["w_uv"] = _matmul(sv["ckv"], dv, "tn", BF16, nm("v_up_dw"), blocked=True)
    gb["w_in"] = _matmul(sv["h"], dz, "tn", BF16, nm("in_proj_dw"))
    plan.add_grads(l, "mix", gb)
    ride, small_gathered = plan.tail_ride(l, pack_small(gs)), []
    if ride is None:
        dh = _matmul(dz, w["w_in"], "nt", F32, nm("in_proj_dx"))
    else:
        dh, rode = _matmul(dz, w["w_in"], "nt", F32, nm("in_proj_dx"), ride=ride)
        small_gathered = plan.tail_done(l, rode)
    dx, gs["mix_norm_pre"] = _rms_bwd(sv["x"], (D_MODEL, 0), sm["mix_norm_pre"], dh, F32, nm("mix_pre_norm_bwd"), add=dmid)
    return dx, gs, small_gathered


def _part_groups(part):
    return {"mix": MIX_GROUPS, "ffn": FFN_GROUPS, "early": MIX_EARLY, "late": MIX_LATE}[part]


class _Plan:
    def __init__(self, shards, conv_w):
        self.local = [_local_groups(shards, l) for l in range(DEPTH)]
        self.conv_w = conv_w
        self.gat, self.send, self.recv = {}, {}, {}

    @staticmethod
    def _riders(l):
        return [(l, "ffn")] + ([(l + 1, "mix")] if l + 1 < DEPTH else [])

    @staticmethod
    def _grad_riders(l):
        return [(l, "ffn"), (l, "early")] + ([(l + 1, "late")] if l + 1 < DEPTH else [])

    def gather_first(self):
        w_in, conv_w = _all_gather([self.local[0]["w_in"], self.conv_w], "gather_w_in_l0")
        self.gat[(0, "mix")] = {"w_in": w_in}
        return conv_w

    def w_in(self, l):
        return _arrange_w_in(self.gat[(l, "mix")]["w_in"])

    def in_proj_ride(self, l):
        return _GatherRide([self.local[0][g] for g in MIX_GROUPS[1:]]) if l == 0 else None

    def in_proj_done(self, l, outs):
        self.gat[(l, "mix")].update(zip(MIX_GROUPS[1:], outs))

    def fwd_ride(self, l):
        return _GatherRide([self.local[ll][g] for ll, part in self._riders(l) for g in _part_groups(part)])

    def fwd_done(self, l, outs):
        outs = list(outs)
        for ll, part in self._riders(l):
            self.gat[(ll, part)] = {g: outs.pop(0) for g in _part_groups(part)}

    def mixer_weights(self, l):
        return _mixer_weights(self.gat[(l, "mix")])

    def ffn_weights(self, l):
        return _ffn_weights(self.gat[(l, "ffn")])

    def add_grads(self, l, part, gb):
        if part == "ffn":
            self.send[(l, "ffn")] = _ffn_grad_groups(gb)
        else:
            self.send.setdefault((l, "late"), {}).update(_mixer_grad_groups({g: gb[g] for g in MIX_LATE if g in gb}))

    def bwd_ride(self, l, gb_early):
        self.send[(l, "early")] = _mixer_grad_groups({g: gb_early[g] for g in MIX_EARLY})
        return _ReduceRide([self.send[(ll, part)][g] for ll, part in self._grad_riders(l) for g in _part_groups(part)])

    def bwd_done(self, l, outs):
        outs = list(outs)
        for ll, part in self._grad_riders(l):
            self.recv[(ll, part)] = {g: outs.pop(0) for g in _part_groups(part)}

    def tail_ride(self, l, small_groups):
        if l > 0:
            return None
        send = [self.send[(0, "late")][g] for g in MIX_LATE]
        by_core = [a.reshape((4, 2) + a.shape[1:]).transpose((1, 0) + tuple(range(2, a.ndim + 1))) for a in send]
        core = lax.axis_index("c")
        own = [lax.dynamic_index_in_dim(a, core, axis=0, keepdims=False) for a in by_core]
        got = _swap_with_sibling(by_core, "reduce_d2d")
        pairs = [_add_pairs(a, b, f"reduce_pair_add_{g}") for g, a, b in zip(MIX_LATE, own, got)]
        return _Combo([_ChipExchangeRide(pairs), _GatherRide(small_groups)])

    def tail_done(self, l, outs):
        self.recv[(l, "late")] = dict(zip(MIX_LATE, outs[:len(MIX_LATE)]))
        return outs[len(MIX_LATE):]

    def finish(self):
        layers = []
        for l in range(DEPTH):
            tot = {g: _sum_blocks(a, f"reduce_sum_{g}_l{l}") for part in ("early", "late", "ffn")
                   for g, a in self.recv[(l, part)].items()}
            layers.append(_grads_from_groups(tot))
        return layers


def _local_step(x, positions, target, smalls, plan):
    tables = _rope_tables(positions)
    saved = []
    h, h_norm = x, None
    for l in range(DEPTH):
        h, h2, svm, wm = _mixer_fwd(h, h_norm, tables, smalls[l], plan, l)
        wf = plan.ffn_weights(l)
        next_gain = smalls[l + 1]["mix_norm_pre"] if l + 1 < DEPTH else None
        h, h_norm, svf = _ffn_fwd(h, h2, wf, smalls[l], f"l{l}", next_gain)
        saved.append((svm, svf, wm, wf))
    dy, sq = _loss_grad(h, target, "loss_grad")
    small = [None] * DEPTH
    for l in reversed(range(DEPTH)):
        svm, svf, wm, wf = saved[l]
        dmid, gbf, gsf = _ffn_bwd(dy, svf, wf, smalls[l], f"l{l}")
        plan.add_grads(l, "ffn", gbf)

        def pack_small(gs, l=l, gsf=gsf):
            if l > 0:
                return None
            return _small_groups([{**gsf, **gs, "mix_norm_pre": jnp.zeros((D_MODEL,), F32)}] + small[1:])

        dy, gsm, small_gathered = _mixer_bwd(dmid, svm, tables, wm, smalls[l], plan, l, pack_small)
        small[l] = {**gsf, **gsm}
    return sq, dy, small, small_gathered


def kernel(x, positions, mix_norm_pre, w_in, q_norm, w_uq, kv_norm, w_uk, w_uv, w_attn_o, conv_w, conv_b, conv_ln_g, conv_ln_b, w_conv_o, pool_w, pool_scale, w_pool_o, w_mix_o, mix_norm_post, ffn_norm_pre, w_gate, w_up, w_down, ffn_norm_post, loss_target, m_mix_norm_pre, m_w_in, m_q_norm, m_w_uq, m_kv_norm, m_w_uk, m_w_uv, m_w_attn_o, m_conv_w, m_conv_b, m_conv_ln_g, m_conv_ln_b, m_w_conv_o, m_pool_w, m_pool_scale, m_w_pool_o, m_w_mix_o, m_mix_norm_post, m_ffn_norm_pre, m_w_gate, m_w_up, m_w_down, m_ffn_norm_post, v_mix_norm_pre, v_w_in, v_q_norm, v_w_uq, v_kv_norm, v_w_uk, v_w_uv, v_w_attn_o, v_conv_w, v_conv_b, v_conv_ln_g, v_conv_ln_b, v_w_conv_o, v_pool_w, v_pool_scale, v_w_pool_o, v_w_mix_o, v_mix_norm_post, v_ffn_norm_pre, v_w_gate, v_w_up, v_w_down, v_ffn_norm_post):
    given = dict(locals())
    dev = 4 * lax.axis_index("x") + 2 * lax.axis_index("y") + lax.axis_index("c")

    plan = _Plan({n: given[n] for n in BIG}, conv_w)
    cw = CONV_C // N_DEV
    conv_w_full = plan.gather_first().transpose(1, 2, 0, 3).reshape(DEPTH, CONV_W, CONV_C)
    smalls = []
    for l in range(DEPTH):
        sm = {n: given[n][l] for n in SMALL if n != "conv_w"}
        sm["conv_w"] = _pad_axis(conv_w_full[l], 0, CONV_HALO)
        smalls.append(sm)

    sq, grad_x, small, small_groups = _local_step(x[0], positions[0], loss_target[0], smalls, plan)
    loss = lax.psum(0.5 / D_MODEL * jnp.sum(sq), ("x", "y", "c"))
    per_layer = plan.finish()
    views = {}
    for n in BIG:
        if n == "w_in":
            views[n] = jnp.stack([per_layer[l][n].T for l in range(DEPTH)], axis=1)
        elif n in LANE_MAJOR:
            views[n] = jnp.stack([per_layer[l][n].T for l in range(DEPTH)])
        else:
            views[n] = jnp.stack([per_layer[l][n] for l in range(DEPTH)])
    grads = {n: _from_lane_major(n, views[n]) for n in BIG}

    small_sum = _small_from_groups([_sum_blocks(g, f"sum_small_grads_{i}") for i, g in enumerate(small_groups)])
    last = _pad_axis(small[0]["mix_norm_pre"].reshape(1, D_MODEL), 0, SUBLANES)
    last_sum = _sum_blocks(_all_gather([last], "gather_last_norm_grad")[0], "sum_last_norm_grad")[0]
    small_sum["mix_norm_pre"] = small_sum["mix_norm_pre"].at[0].set(last_sum)
    for n in SMALL:
        grads[n] = small_sum[n]
    grads["conv_w"] = lax.dynamic_slice_in_dim(small_sum["conv_w"], dev * cw, cw, axis=2)

    delta, new_m, new_v = {}, {}, {}
    for n in WEIGHTS:
        g_view = views[n] if n in views else grads[n]
        w_view, m_view, v_view = [_lane_major(n, given[k]) for k in (n, "m_" + n, "v_" + n)]
        res = _adamw(w_view, g_view, m_view, v_view, f"adamw_{n}")
        delta[n], new_m[n], new_v[n] = [_from_lane_major(n, r) for r in res]
    return (loss, grad_x[None], *[grads[n] for n in WEIGHTS], *[delta[n] for n in WEIGHTS],
            *[new_m[n] for n in WEIGHTS], *[new_v[n] for n in WEIGHTS])
```

```python
import functools
import math

import jax
import jax.numpy as jnp
from jax import lax
from jax.experimental import pallas as pl
from jax.experimental.pallas import tpu as pltpu

F32, BF16 = jnp.float32, jnp.bfloat16
MESH = pl.DeviceIdType.MESH

LANES = 128
SUBLANES = 8
VMEM_LIMIT_BYTES = 56 * 1024 * 1024
MATMUL_VMEM_BYTES = 40 * 1024 * 1024

N_DEV = 8
D_MODEL = 1024
DEPTH = 2
N_HEADS = 8
QK_NOPE, QK_ROPE, V_HEAD = 64, 32, 64
HEAD_PAD = LANES
Q_RANK, KV_RANK = 384, 256
ROPE_THETA = 10000.0
CONV_C, CONV_W = 512, 31
CONV_HALO = 32
POOL_WINDOWS = (2, 4, 8, 16)
POOL_C, POOL_G = 512, 4
POOL_GD = POOL_C // POOL_G
D_FF = 2816
FF_SHARD = D_FF // N_DEV
FF_SHARD_PAD = 3 * LANES
D_FF_PAD = N_DEV * FF_SHARD_PAD
W_IN_SHARD = 660
EPS = 1e-6
ATTN_SCALE = 1.0 / math.sqrt(QK_NOPE + QK_ROPE)
LOG2E = 1.4426950408889634
LR, B1, B2, ADAM_EPS, WD, STEP = 0.001, 0.9, 0.999, 1e-08, 0.01, 10

Z_W = 5376
ZC_GATE = (1024, 0)
ZC_GATES = (3072, 0)
ZC_CONV_A = (512, 6)
ZC_CONV_G = (512, 7)
ZC_CONV = (1024, 3)
ZC_POOL = (512, 8)
ZC_Q = (384, 12)
ZC_KR = (128, 39)
ZC_KV = (256, 20)
ZC_QKR = (768, 6)
W_IN_PIECES = ((0, 384, 4608), (384, 640, 5120), (640, 672, 5056), (672, 1696, 3072), (1696, 2208, 4096),
               (2208, 5280, 0))

BIG = ("w_in", "w_uq", "w_uk", "w_uv", "w_attn_o", "w_conv_o", "w_pool_o", "w_mix_o", "w_gate", "w_up", "w_down")
SMALL = ("mix_norm_pre", "q_norm", "kv_norm", "conv_w", "conv_b", "conv_ln_g", "conv_ln_b", "pool_w", "pool_scale",
         "mix_norm_post", "ffn_norm_pre", "ffn_norm_post")
WEIGHTS = ("mix_norm_pre", "w_in", "q_norm", "w_uq", "kv_norm", "w_uk", "w_uv", "w_attn_o", "conv_w", "conv_b",
           "conv_ln_g", "conv_ln_b", "w_conv_o", "pool_w", "pool_scale", "w_pool_o", "w_mix_o", "mix_norm_post",
           "ffn_norm_pre", "w_gate", "w_up", "w_down", "ffn_norm_post")


def _params(*semantics):
    return pltpu.CompilerParams(dimension_semantics=semantics, vmem_limit_bytes=VMEM_LIMIT_BYTES)


def _tile(dim, cap):
    if dim <= cap:
        return dim
    for t in range(cap - cap % LANES, 0, -LANES):
        if dim % t == 0:
            return t
    raise ValueError(f"no tile for {dim} under {cap}")


def _row_tile(rows, row_bytes, budget=1 << 20):
    if rows * row_bytes <= budget:
        return rows
    cap = max(16, budget // row_bytes)
    for t in range(cap - cap % 16, 0, -16):
        if rows % t == 0:
            return t
    return rows


def _rows(ts, width, cidx=0):
    return pl.BlockSpec((ts, width), lambda i: (i, cidx))


def _fixed(shape):
    return pl.BlockSpec(shape, lambda *_: (0,) * len(shape))


def _sigmoid(x):
    return 1.0 / (1.0 + jnp.exp(-x))


def _matmul(a, b, mode, out_dtype, name, add=None, blocked=False, ride=None):
    nb = n_blk = 0
    blocked = blocked or b.ndim == 3
    if mode == "nn":
        (m, k) = a.shape
        n = b.shape[0] * b.shape[2] if blocked else b.shape[1]
    elif mode == "nt":
        (m, k) = a.shape
        n = b.shape[1] if blocked else b.shape[0]
    else:
        (k, m), n = a.shape, b.shape[1]
    if blocked:
        nb = b.shape[2] if mode != "tn" else n // N_DEV
    unit = nb if blocked and mode != "nt" else LANES
    out_bytes = jnp.dtype(out_dtype).itemsize + (4 if add is not None else 0)
    best = None
    for tn_c in range(unit, min(n, 1536) + 1, unit):
        for tm_c in sorted({256, 512, 1024, 2048, min(m, 2048)}):
            if n % tn_c or m % tm_c or (blocked and mode != "nt" and N_DEV % (tn_c // nb)):
                continue
            vmem = 2 * (tm_c * k * 2 + tn_c * k * 2 + tm_c * tn_c * out_bytes) + tm_c * tn_c * 4 + tn_c * k * 2
            if vmem <= MATMUL_VMEM_BYTES and (best is None or tm_c * tn_c / (tm_c + tn_c) > best[0]):
                best = (tm_c * tn_c / (tm_c + tn_c), tm_c, tn_c)
    if best is None:
        raise ValueError(f"{name}: no tiles for {m}x{n}x{k}")
    _, tm, tn = best
    if blocked:
        n_blk = N_DEV if mode == "nt" else tn // nb
    dims = {"nn": ((1,), (0,)), "nt": ((1,), (1,)), "tn": ((0,), (0,))}[mode]
    a_spec = pl.BlockSpec((k, tm), lambda i, j: (0, i)) if mode == "tn" else pl.BlockSpec((tm, k), lambda i, j: (i, 0))
    b_spec = pl.BlockSpec((tn, k), lambda i, j: (j, 0)) if mode == "nt" else pl.BlockSpec((k, tn), lambda i, j: (0, j))
    o_spec = pl.BlockSpec((tm, tn), lambda i, j: (i, j))
    out_shape = jax.ShapeDtypeStruct((m, n), out_dtype)
    if blocked and mode == "nn":
        b_spec = pl.BlockSpec((n_blk, k, nb), lambda i, j: (j, 0, 0))
    elif blocked and mode == "nt":
        b_spec = pl.BlockSpec((n_blk, tn, nb), lambda i, j: (0, j, 0))
    elif blocked:
        o_spec = pl.BlockSpec((n_blk, tm, nb), lambda i, j: (j, i, 0))
        out_shape = jax.ShapeDtypeStruct((N_DEV, m, nb), out_dtype)
    has_add = add is not None
    grid = (m // tm, n // tn)

    def body(*refs):
        (a_ref, b_ref, *rest), start, finish = _ride_hooks(ride, refs, 3 if has_add else 2, 1, grid)
        start()
        o_ref = rest[-1]
        if blocked and mode != "tn":
            bv = jnp.concatenate([b_ref[c] for c in range(n_blk)], axis=1) if n_blk > 1 else b_ref[0]
        else:
            bv = b_ref[...]
        total = lax.dot_general(a_ref[...], bv, (dims, ((), ())), preferred_element_type=F32)
        if has_add:
            total = total + rest[0][...]
        if blocked and mode == "tn":
            for c in range(n_blk):
                o_ref[c] = total[:, c * nb:(c + 1) * nb].astype(o_ref.dtype)
        else:
            o_ref[...] = total.astype(o_ref.dtype)
        finish()

    operands = (a, b, add) if has_add else (a, b)
    (out,), rode = _ride_call(ride, body, name, (out_shape,), grid, [a_spec, b_spec] + ([o_spec] if has_add else []),
                              (o_spec,), ("parallel", "parallel"), operands)
    return out if ride is None else (out, rode)


def _rms_fwd(x, win, gain, out_dtype, name, res=None, then=None):
    width, cidx = win
    s = x.shape[0]
    ts = min(s, 512)
    has_res, has_then = res is not None, then is not None

    def norm(v, g_ref):
        return (v * lax.rsqrt(jnp.mean(v * v, axis=-1, keepdims=True) + EPS)) * g_ref[...]

    def body(x_ref, g_ref, *rest):
        y = norm(x_ref[...].astype(F32), g_ref)
        if has_res:
            y = rest[0][...] + y
        o_ref = rest[-2] if has_then else rest[-1]
        o_ref[...] = y.astype(o_ref.dtype)
        if has_then:
            rest[-1][...] = norm(y, rest[-3]).astype(BF16)

    ops = (x, gain.reshape(1, width)) + ((res,) if has_res else ()) + ((then.reshape(1, width),) if has_then else ())
    out_shape = (jax.ShapeDtypeStruct((s, width), out_dtype),) + ((jax.ShapeDtypeStruct((s, width), BF16),) * has_then)
    out = pl.pallas_call(
        body, name=name, out_shape=out_shape, grid=(s // ts,),
        in_specs=([_rows(ts, width, cidx), _fixed((1, width))] + ([_rows(ts, width)] if has_res else [])
                  + ([_fixed((1, width))] if has_then else [])),
        out_specs=(_rows(ts, width),) * len(out_shape), compiler_params=_params("parallel"))(*ops)
    return out if has_then else out[0]


def _into(dz, n_inputs, out_index):
    return dict(in_specs=[ANY], operands=(dz,), input_output_aliases={n_inputs: out_index},
                out_shape=jax.ShapeDtypeStruct(dz.shape, dz.dtype))


def _rms_bwd(x, win, gain, dy, out_dtype, name, add=None, dz=None):
    width, cidx = win
    s = x.shape[0]
    ts = min(s, 512)
    has_add = add is not None

    def body(x_ref, g_ref, dy_ref, *rest):
        dx_ref, dg_ref = rest[-2], rest[-1]
        xv = x_ref[...].astype(F32)
        r = lax.rsqrt(jnp.mean(xv * xv, axis=-1, keepdims=True) + EPS)
        xh = xv * r
        dyv = dy_ref[...].astype(F32)
        dyg = dyv * g_ref[...]
        dx = r * (dyg - xh * jnp.mean(dyg * xh, axis=-1, keepdims=True))
        if has_add:
            dx = dx + rest[0][...]
        dx_ref[...] = dx.astype(dx_ref.dtype)

        @pl.when(pl.program_id(0) == 0)
        def _():
            dg_ref[...] = jnp.zeros_like(dg_ref)

        dg_ref[...] += jnp.sum(dyv * xh, axis=0, keepdims=True)

    ops = (x, gain.reshape(1, width), dy) + ((add,) if has_add else ())
    in_specs = [_rows(ts, width, cidx), _fixed((1, width)), _rows(ts, width)] + ([_rows(ts, width)] if has_add else [])
    dx_shape, dx_spec, alias = jax.ShapeDtypeStruct((s, width), out_dtype), _rows(ts, width), {}
    if dz is not None:
        into = _into(dz, len(ops), 0)
        ops, in_specs, alias = ops + into["operands"], in_specs + into["in_specs"], into["input_output_aliases"]
        dx_shape, dx_spec = into["out_shape"], _rows(ts, width, cidx)
    dx, dg = pl.pallas_call(
        body, name=name, out_shape=(dx_shape, jax.ShapeDtypeStruct((1, width), F32)), grid=(s // ts,),
        in_specs=in_specs, out_specs=(dx_spec, _fixed((1, width))), input_output_aliases=alias,
        compiler_params=_params("arbitrary"))(*ops)
    return dx, dg.reshape(width)


def _rope(x, c, s1, s2):
    return x * c + pltpu.roll(x, 16, 1) * s1 + pltpu.roll(x, LANES - 16, 1) * s2


def _rope_t(g, c, s1, s2):
    return g * c + pltpu.roll(g * s1, LANES - 16, 1) + pltpu.roll(g * s2, 16, 1)


def _rope_tables(positions):
    inv_freq = ROPE_THETA ** (-jnp.arange(0, QK_ROPE, 2, dtype=F32) / QK_ROPE)
    ang = positions.astype(F32)[:, None] * inv_freq
    cos, sin = jnp.cos(ang), jnp.sin(ang)
    n = positions.shape[0]
    one, zero = jnp.ones((n, 1), F32), jnp.zeros((n, 1), F32)
    c = jnp.concatenate([jnp.tile(one, (1, QK_NOPE)), cos, cos, jnp.tile(one, (1, 32))], axis=1)
    s1 = jnp.concatenate([jnp.tile(zero, (1, QK_NOPE + 16)), sin, jnp.tile(zero, (1, 32))], axis=1)
    s2 = jnp.concatenate([jnp.tile(zero, (1, QK_NOPE)), -sin, jnp.tile(zero, (1, 48))], axis=1)
    return c, s1, s2


def _qkv_up_fwd(z, q_gain, kv_gain, w_uq, w_uk, w_uv, tables, name):
    s = z.shape[0]
    ts = min(s, 512)
    hw = N_HEADS * HEAD_PAD
    kv0 = Q_RANK + LANES

    def norm(v, g_ref):
        return ((v * lax.rsqrt(jnp.mean(v * v, axis=-1, keepdims=True) + EPS)) * g_ref[...]).astype(BF16)

    def body(z_ref, gq_ref, gkv_ref, wq_ref, wk_ref, wv_ref, c_ref, s1_ref, s2_ref, cq_ref, ckv_ref, q_ref, k_ref, v_ref):
        c, s1, s2 = c_ref[...], s1_ref[...], s2_ref[...]
        cqv = norm(z_ref[:, pl.ds(0, Q_RANK)].astype(F32), gq_ref)
        ckvv = norm(z_ref[:, pl.ds(kv0, KV_RANK)].astype(F32), gkv_ref)
        cq_ref[...] = cqv
        ckv_ref[...] = ckvv
        kr = _rope(z_ref[:, pl.ds(Q_RANK, LANES)].astype(F32), c, s1, s2)
        for h in range(N_HEADS):
            sl = slice(h * HEAD_PAD, (h + 1) * HEAD_PAD)
            q_ref[:, sl] = _rope(jnp.dot(cqv, wq_ref[h], preferred_element_type=F32), c, s1, s2).astype(BF16)
            k_ref[:, sl] = (jnp.dot(ckvv, wk_ref[h], preferred_element_type=F32) + kr).astype(BF16)
            v_ref[:, sl] = jnp.dot(ckvv, wv_ref[h], preferred_element_type=F32).astype(BF16)

    tab = _rows(ts, LANES)
    wide = jax.ShapeDtypeStruct((s, hw), BF16)
    return pl.pallas_call(
        body, name=name,
        out_shape=(jax.ShapeDtypeStruct((s, Q_RANK), BF16), jax.ShapeDtypeStruct((s, KV_RANK), BF16), wide, wide, wide),
        grid=(s // ts,),
        in_specs=[_rows(ts, *ZC_QKR), _fixed((1, Q_RANK)), _fixed((1, KV_RANK)), _fixed(w_uq.shape), _fixed(w_uk.shape),
                  _fixed(w_uv.shape), tab, tab, tab],
        out_specs=(_rows(ts, Q_RANK), _rows(ts, KV_RANK)) + (_rows(ts, hw),) * 3, compiler_params=_params("parallel"))(
            z, q_gain.reshape(1, -1), kv_gain.reshape(1, -1), w_uq, w_uk, w_uv, *tables)


def _qkv_up_bwd(dq, dk, dv, z, w_uq, w_uk, w_uv, tables, q_gain, kv_gain, dz, name):
    s = dq.shape[0]
    ts = min(s, 512)
    hw = N_HEADS * HEAD_PAD
    zw = ZC_QKR[0]
    kv0 = Q_RANK + LANES
    dims_nt = (((1,), (1,)), ((), ()))

    def norm_bwd(xv, g_ref, dyv):
        r = lax.rsqrt(jnp.mean(xv * xv, axis=-1, keepdims=True) + EPS)
        xh = xv * r
        dyg = dyv * g_ref[...]
        return r * (dyg - xh * jnp.mean(dyg * xh, axis=-1, keepdims=True)), jnp.sum(dyv * xh, axis=0, keepdims=True)

    def body(dq_ref, dk_ref, dv_ref, z_ref, wq_ref, wk_ref, wv_ref, c_ref, s1_ref, s2_ref, gq_ref, gkv_ref, _,
             dqf_ref, dkf_ref, dz_ref, dgq_ref, dgkv_ref):
        c, s1, s2 = c_ref[...], s1_ref[...], s2_ref[...]
        ksum = jnp.zeros((ts, HEAD_PAD), F32)
        dcq = jnp.zeros((ts, Q_RANK), F32)
        dckv = jnp.zeros((ts, KV_RANK), F32)
        for h in range(N_HEADS):
            sl = slice(h * HEAD_PAD, (h + 1) * HEAD_PAD)
            dqh = _rope_t(dq_ref[:, sl], c, s1, s2).astype(BF16)
            dkv = dk_ref[:, sl]
            dkh = dkv.astype(BF16)
            dqf_ref[:, sl] = dqh
            dkf_ref[:, sl] = dkh
            ksum = ksum + dkv
            dcq = dcq + lax.dot_general(dqh, wq_ref[h], dims_nt, preferred_element_type=F32)
            dckv = dckv + (lax.dot_general(dkh, wk_ref[h], dims_nt, preferred_element_type=F32)
                           + lax.dot_general(dv_ref[:, sl], wv_ref[h], dims_nt, preferred_element_type=F32))
        dxq, dgq = norm_bwd(z_ref[:, pl.ds(0, Q_RANK)].astype(F32), gq_ref, dcq)
        dxkv, dgkv = norm_bwd(z_ref[:, pl.ds(kv0, KV_RANK)].astype(F32), gkv_ref, dckv)
        lane = lax.broadcasted_iota(jnp.int32, (ts, HEAD_PAD), 1)
        in_rope = (lane >= QK_NOPE) & (lane < QK_NOPE + QK_ROPE)
        dz_ref[:, pl.ds(0, Q_RANK)] = dxq.astype(BF16)
        dz_ref[:, pl.ds(Q_RANK, LANES)] = jnp.where(in_rope, _rope_t(ksum, c, s1, s2), 0.0).astype(BF16)
        dz_ref[:, pl.ds(kv0, KV_RANK)] = dxkv.astype(BF16)

        @pl.when(pl.program_id(0) == 0)
        def _():
            dgq_ref[...] = jnp.zeros_like(dgq_ref)
            dgkv_ref[...] = jnp.zeros_like(dgkv_ref)

        dgq_ref[...] += dgq
        dgkv_ref[...] += dgkv

    tab = _rows(ts, LANES)
    into = _into(dz, 12, 2)
    dqf, dkf, dz, dgq, dgkv = pl.pallas_call(
        body, name=name,
        out_shape=(jax.ShapeDtypeStruct((s, hw), BF16), jax.ShapeDtypeStruct((s, hw), BF16), into["out_shape"],
                   jax.ShapeDtypeStruct((1, Q_RANK), F32), jax.ShapeDtypeStruct((1, KV_RANK), F32)),
        grid=(s // ts,),
        in_specs=[_rows(ts, hw), _rows(ts, hw), _rows(ts, hw), _rows(ts, *ZC_QKR), _fixed(w_uq.shape), _fixed(w_uk.shape),
                  _fixed(w_uv.shape), tab, tab, tab, _fixed((1, Q_RANK)), _fixed((1, KV_RANK))] + into["in_specs"],
        out_specs=(_rows(ts, hw), _rows(ts, hw), _rows(ts, *ZC_QKR), _fixed((1, Q_RANK)), _fixed((1, KV_RANK))),
        input_output_aliases=into["input_output_aliases"], compiler_params=_params("arbitrary"))(
            dq, dk, dv, z, w_uq, w_uk, w_uv, *tables, q_gain.reshape(1, -1), kv_gain.reshape(1, -1), dz)
    return dqf, dkf, dz, dgq.reshape(-1), dgkv.reshape(-1)


def _attn_tile(s):
    return min(s, 512)


def _raw_scores(q, k, masked, row0=0):
    sc = lax.dot_general(q, k, (((1,), (1,)), ((), ())), preferred_element_type=F32)
    if masked:
        rows = row0 + lax.broadcasted_iota(jnp.int32, sc.shape, 0)
        cols = lax.broadcasted_iota(jnp.int32, sc.shape, 1)
        sc = jnp.where(cols <= rows, sc, -jnp.inf)
    return sc


def _ride_hooks(ride, refs, n_in, n_out, grid):
    if ride is None:
        return refs, lambda: None, lambda: None
    n = len(ride.arrays)
    own = refs[:n_in] + refs[n_in + n:n_in + n + n_out]
    ins, outs, sems = refs[n_in:n_in + n], refs[n_in + n + n_out:n_in + 2 * n + n_out], refs[n_in + 2 * n + n_out:]
    at_first = functools.reduce(lambda a, b: a & b, [pl.program_id(ax) == 0 for ax in range(len(grid))])
    at_last = functools.reduce(lambda a, b: a & b, [pl.program_id(ax) == g - 1 for ax, g in enumerate(grid)])
    return own, lambda: pl.when(at_first)(lambda: ride.start(ins, outs, sems)), \
        lambda: pl.when(at_last)(lambda: ride.finish(ins, outs, sems))


def _ride_call(ride, body, name, out_shape, grid, in_specs, out_specs, semantics, operands):
    n = 0 if ride is None else len(ride.arrays)
    res = pl.pallas_call(
        body, name=name, out_shape=tuple(out_shape) + (tuple(ride.out_shape) if n else ()), grid=grid,
        in_specs=list(in_specs) + [ANY] * n, out_specs=tuple(out_specs) + (ANY,) * n,
        scratch_shapes=list(ride.scratch) if n else [],
        compiler_params=_params(*(("arbitrary",) * len(grid) if n else semantics)))(*operands, *(ride.arrays if n else ()))
    return res[:len(out_shape)], list(res[len(out_shape):])


def _flash_fwd(q, k, v, name, ride=None):
    s = q.shape[0]
    t = _attn_tile(s)
    c2 = ATTN_SCALE * LOG2E
    grid = (N_HEADS, s // t)

    def body(*refs):
        (q_ref, k_ref, v_ref, o_ref, lse_ref), start, finish = _ride_hooks(ride, refs, 3, 2, grid)
        start()
        i = pl.program_id(1)
        qv = q_ref[...]

        def chunk(j, carry, masked):
            m_old, l_old, acc = carry
            at = pl.ds(pl.multiple_of(j * t, t), t)
            sc = _raw_scores(qv, k_ref[at, :], masked)
            m_new = jnp.maximum(m_old, jnp.max(sc, axis=-1, keepdims=True))
            p = jnp.exp2((sc - m_new) * c2)
            alpha = jnp.exp2((m_old - m_new) * c2)
            l_new = alpha * l_old + jnp.sum(p, axis=-1, keepdims=True)
            acc = alpha * acc + jnp.dot(p.astype(BF16), v_ref[at, :], preferred_element_type=F32)
            return m_new, l_new, acc

        init = (jnp.full((t, 1), -jnp.inf, F32), jnp.zeros((t, 1), F32), jnp.zeros((t, HEAD_PAD), F32))
        carry = lax.fori_loop(0, i, lambda j, cr: chunk(j, cr, False), init)
        m_fin, l_fin, acc = chunk(i, carry, True)
        o_ref[...] = (acc / l_fin).astype(o_ref.dtype)
        lse_ref[...] = jnp.broadcast_to(m_fin * ATTN_SCALE + jnp.log(l_fin), (t, HEAD_PAD))
        finish()

    qo = pl.BlockSpec((t, HEAD_PAD), lambda h, i: (i, h))
    whole = pl.BlockSpec((s, HEAD_PAD), lambda h, i: (0, h))
    return _ride_call(
        ride, body, name, (jax.ShapeDtypeStruct(q.shape, BF16), jax.ShapeDtypeStruct(q.shape, F32)), grid,
        [qo, whole, whole], (qo, qo), ("parallel", "parallel"), (q, k, v))


def _attn_out_bwd(dya, w_attn_o, o, name):
    s, d = dya.shape
    hw = N_HEADS * HEAD_PAD
    t = _attn_tile(s)

    def body(d_ref, w_ref, o_ref, delta_ref, dob_ref):
        wv = jnp.concatenate([w_ref[c] for c in range(N_DEV)], axis=1)
        do = lax.dot_general(d_ref[...], wv, (((1,), (1,)), ((), ())), preferred_element_type=F32)
        for h in range(N_HEADS):
            sl = slice(h * HEAD_PAD, (h + 1) * HEAD_PAD)
            dov = do[:, sl]
            delta_ref[:, sl] = jnp.broadcast_to(jnp.sum(dov * o_ref[:, sl].astype(F32), axis=-1, keepdims=True),
                                                (t, HEAD_PAD))
            dob_ref[:, sl] = dov.astype(BF16)

    blk = _rows(t, hw)
    return pl.pallas_call(
        body, name=name, out_shape=(jax.ShapeDtypeStruct(o.shape, F32), jax.ShapeDtypeStruct(o.shape, BF16)),
        grid=(s // t,), in_specs=[_rows(t, d), _fixed(w_attn_o.shape), blk], out_specs=(blk, blk),
        compiler_params=_params("parallel"))(dya, w_attn_o, o)


def _flash_bwd(q, k, v, do, lse, delta, name, ride=None):
    s = q.shape[0]
    t = _attn_tile(s)
    nt = s // t
    c2 = ATTN_SCALE * LOG2E
    grid = (N_HEADS, nt)

    def body(*refs):
        (q_ref, k_ref, v_ref, do_ref, lse_ref, delta_ref, dq_ref, dk_ref, dv_ref), start, finish = _ride_hooks(
            ride, refs, 6, 3, grid)
        start()
        j = pl.program_id(1)
        kv, vv = k_ref[...], v_ref[...]

        @pl.when(j == 0)
        def _():
            dq_ref[...] = jnp.zeros_like(dq_ref)

        def chunk(i, carry, masked):
            dk_acc, dv_acc = carry
            at = pl.ds(pl.multiple_of(i * t, t), t)
            qi, doi = q_ref[at, :], do_ref[at, :]
            sc = _raw_scores(qi, kv, masked)
            p = jnp.exp2(sc * c2 - lse_ref[at, pl.ds(0, 1)] * LOG2E)
            dp = lax.dot_general(doi, vv, (((1,), (1,)), ((), ())), preferred_element_type=F32)
            ds = (p * (dp - delta_ref[at, pl.ds(0, 1)])).astype(BF16)
            dv_acc = dv_acc + lax.dot_general(p.astype(BF16), doi, (((0,), (0,)), ((), ())), preferred_element_type=F32)
            dk_acc = dk_acc + lax.dot_general(ds, qi, (((0,), (0,)), ((), ())), preferred_element_type=F32)
            dq_ref[at, :] += jnp.dot(ds, kv, preferred_element_type=F32) * ATTN_SCALE
            return dk_acc, dv_acc

        zero = jnp.zeros((t, HEAD_PAD), F32)
        carry = chunk(j, (zero, zero), True)
        dk_acc, dv_acc = lax.fori_loop(j + 1, nt, lambda i, cr: chunk(i, cr, False), carry)
        dk_ref[...] = dk_acc * ATTN_SCALE
        dv_ref[...] = dv_acc.astype(BF16)
        finish()

    blk = pl.BlockSpec((t, HEAD_PAD), lambda h, j: (j, h))
    whole = pl.BlockSpec((s, HEAD_PAD), lambda h, j: (0, h))
    return _ride_call(
        ride, body, name, (jax.ShapeDtypeStruct(q.shape, F32), jax.ShapeDtypeStruct(q.shape, F32),
                           jax.ShapeDtypeStruct(q.shape, BF16)), grid,
        [whole, blk, blk, whole, whole, whole], (whole, blk, blk), ("parallel", "arbitrary"), (q, k, v, do, lse, delta))


def _conv_tile(s):
    return min(s, 256)


def _halo_before(t, width, cidx):
    per = t // CONV_HALO
    return pl.BlockSpec((CONV_HALO, width), lambda i: (jnp.maximum(i * per - 1, 0), cidx))


def _halo_after(t, width, cidx, n_tiles):
    per = t // CONV_HALO
    last = n_tiles * per - 1
    return pl.BlockSpec((CONV_HALO, width), lambda i: (jnp.minimum((i + 1) * per, last), cidx))


def _fill_glu(hbuf, ap_ref, gp_ref, a_ref, g_ref, t):
    first = pl.program_id(0) == 0
    hbuf[pl.ds(0, CONV_HALO), :] = jnp.where(first, 0.0, ap_ref[...].astype(F32) * _sigmoid(gp_ref[...].astype(F32)))
    hbuf[pl.ds(CONV_HALO, t), :] = a_ref[...].astype(F32) * _sigmoid(g_ref[...].astype(F32))


def _phase_copies(dst, src, t):
    n = t + CONV_HALO - SUBLANES
    for s in range(1, SUBLANES):
        dst[s, pl.ds(0, n), :] = src[pl.ds(s, n), :]


def _window(phases, src, k, t):
    if k % SUBLANES == 0:
        return src[pl.ds(k, t), :]
    return phases[k % SUBLANES, pl.ds(k - k % SUBLANES, t), :]


def _layer_norm_parts(co):
    mu = jnp.mean(co, axis=-1, keepdims=True)
    xc = co - mu
    rstd = lax.rsqrt(jnp.mean(xc * xc, axis=-1, keepdims=True) + EPS)
    return xc * rstd, rstd


def _conv_fwd(z, conv_w, conv_b, ln_g, ln_b, name):
    s = z.shape[0]
    t = _conv_tile(s)
    off = CONV_HALO - (CONV_W - 1)

    def body(ap_ref, gp_ref, a_ref, g_ref, w_ref, b_ref, lg_ref, lb_ref, hc_ref, co_ref, hbuf, hph):
        _fill_glu(hbuf, ap_ref, gp_ref, a_ref, g_ref, t)
        _phase_copies(hph, hbuf, t)
        acc = jnp.zeros((t, CONV_C), F32) + b_ref[...]
        for j in range(CONV_W):
            acc = acc + _window(hph, hbuf, off + j, t) * w_ref[pl.ds(j, 1), :]
        co_ref[...] = acc
        xh, _ = _layer_norm_parts(acc)
        y = xh * lg_ref[...] + lb_ref[...]
        hc_ref[...] = (y * _sigmoid(y)).astype(BF16)

    vec = _fixed((1, CONV_C))
    return pl.pallas_call(
        body, name=name, out_shape=(jax.ShapeDtypeStruct((s, CONV_C), BF16), jax.ShapeDtypeStruct((s, CONV_C), F32)),
        grid=(s // t,),
        in_specs=[_halo_before(t, *ZC_CONV_A), _halo_before(t, *ZC_CONV_G), _rows(t, *ZC_CONV_A), _rows(t, *ZC_CONV_G),
                  _fixed((CONV_HALO, CONV_C)), vec, vec, vec],
        out_specs=(_rows(t, CONV_C), _rows(t, CONV_C)),
        scratch_shapes=[pltpu.VMEM((t + CONV_HALO, CONV_C), F32), pltpu.VMEM((SUBLANES, t + CONV_HALO, CONV_C), F32)],
        compiler_params=_params("parallel"))(z, z, z, z, conv_w, conv_b.reshape(1, -1), ln_g.reshape(1, -1),
                                             ln_b.reshape(1, -1))


def _conv_bwd_norm(dhc, co, ln_g, ln_b, name):
    s = co.shape[0]
    t = min(s, 512)

    def body(dhc_ref, co_ref, lg_ref, lb_ref, dco_ref, dg_ref, db_ref, dcb_ref):
        xh, rstd = _layer_norm_parts(co_ref[...])
        y = xh * lg_ref[...] + lb_ref[...]
        sg = _sigmoid(y)
        dy = dhc_ref[...] * (sg * (1.0 + y * (1.0 - sg)))
        dxh = dy * lg_ref[...]
        dco = rstd * (dxh - jnp.mean(dxh, axis=-1, keepdims=True) - xh * jnp.mean(dxh * xh, axis=-1, keepdims=True))
        dco_ref[...] = dco

        @pl.when(pl.program_id(0) == 0)
        def _():
            dg_ref[...] = jnp.zeros_like(dg_ref)
            db_ref[...] = jnp.zeros_like(db_ref)
            dcb_ref[...] = jnp.zeros_like(dcb_ref)

        dg_ref[...] += jnp.sum(dy * xh, axis=0, keepdims=True)
        db_ref[...] += jnp.sum(dy, axis=0, keepdims=True)
        dcb_ref[...] += jnp.sum(dco, axis=0, keepdims=True)

    vec = _fixed((1, CONV_C))
    one = jax.ShapeDtypeStruct((1, CONV_C), F32)
    dco, dg, db, dcb = pl.pallas_call(
        body, name=name, out_shape=(jax.ShapeDtypeStruct((s, CONV_C), F32), one, one, one), grid=(s // t,),
        in_specs=[_rows(t, CONV_C), _rows(t, CONV_C), vec, vec], out_specs=(_rows(t, CONV_C), vec, vec, vec),
        compiler_params=_params("arbitrary"))(dhc, co, ln_g.reshape(1, -1), ln_b.reshape(1, -1))
    return dco, dg.reshape(-1), db.reshape(-1), dcb.reshape(-1)


def _conv_bwd_taps(dco, z, conv_w, dz, name):
    s = z.shape[0]
    t = _conv_tile(s)
    nt = s // t
    off = CONV_HALO - (CONV_W - 1)

    def body(ap_ref, gp_ref, a_ref, g_ref, d_ref, dn_ref, w_ref, _, du_ref, dw_ref, hbuf, dbuf, hph, dph):
        i = pl.program_id(0)
        _fill_glu(hbuf, ap_ref, gp_ref, a_ref, g_ref, t)
        dbuf[pl.ds(0, t), :] = d_ref[...]
        dbuf[pl.ds(t, CONV_HALO), :] = jnp.where(i == nt - 1, 0.0, dn_ref[...])
        _phase_copies(hph, hbuf, t)
        _phase_copies(dph, dbuf, t)

        @pl.when(i == 0)
        def _():
            dw_ref[...] = jnp.zeros_like(dw_ref)

        dcur = d_ref[...]
        dh = jnp.zeros((t, CONV_C), F32)
        for j in range(CONV_W):
            dh = dh + _window(dph, dbuf, CONV_W - 1 - j, t) * w_ref[pl.ds(j, 1), :]
            dw_ref[pl.ds(j, 1), :] += jnp.sum(dcur * _window(hph, hbuf, off + j, t), axis=0, keepdims=True)
        a, sg = a_ref[...].astype(F32), _sigmoid(g_ref[...].astype(F32))
        du_ref[:, pl.ds(0, CONV_C)] = (dh * sg).astype(BF16)
        du_ref[:, pl.ds(CONV_C, CONV_C)] = (dh * a * sg * (1.0 - sg)).astype(BF16)

    into = _into(dz, 7, 0)
    return pl.pallas_call(
        body, name=name, out_shape=(into["out_shape"], jax.ShapeDtypeStruct((CONV_HALO, CONV_C), F32)), grid=(nt,),
        in_specs=[_halo_before(t, *ZC_CONV_A), _halo_before(t, *ZC_CONV_G), _rows(t, *ZC_CONV_A), _rows(t, *ZC_CONV_G),
                  _rows(t, CONV_C), _halo_after(t, CONV_C, 0, nt), _fixed((CONV_HALO, CONV_C))] + into["in_specs"],
        out_specs=(_rows(t, *ZC_CONV), _fixed((CONV_HALO, CONV_C))), input_output_aliases=into["input_output_aliases"],
        scratch_shapes=[pltpu.VMEM((t + CONV_HALO, CONV_C), F32), pltpu.VMEM((t + CONV_HALO, CONV_C), F32),
                        pltpu.VMEM((SUBLANES, t + CONV_HALO, CONV_C), F32),
                        pltpu.VMEM((SUBLANES, t + CONV_HALO, CONV_C), F32)],
        compiler_params=_params("arbitrary"))(z, z, z, z, dco, dco, conv_w, dz)


def _pool_tile(s):
    return min(s, 512)


def _pool_counts(row0, n, window):
    rows = row0 + lax.broadcasted_iota(jnp.int32, (n, POOL_GD), 0)
    return jnp.minimum(rows + 1, window).astype(F32)


def _pool_diff(ubuf, gi, window, row0, t):
    lanes = pl.ds(gi * POOL_GD, POOL_GD)
    tot = ubuf[pl.ds(CONV_HALO, t), lanes]
    cur = tot
    for back in range(1, window):
        tot = tot + ubuf[pl.ds(CONV_HALO - back, t), lanes]
    return tot / _pool_counts(row0, t, window) - cur


def _pool_fwd(z, pool_w, pool_scale, name):
    s = z.shape[0]
    t = _pool_tile(s)

    def body(up_ref, u_ref, w_ref, sc_ref, m_ref, ubuf):
        i = pl.program_id(0)
        ubuf[pl.ds(0, CONV_HALO), :] = jnp.where(i == 0, 0.0, up_ref[...].astype(F32))
        ubuf[pl.ds(CONV_HALO, t), :] = u_ref[...].astype(F32)
        for gi, window in enumerate(POOL_WINDOWS):
            d = _pool_diff(ubuf, gi, window, i * t, t)
            mm = jnp.dot(d.astype(BF16), w_ref[gi].astype(BF16), preferred_element_type=F32)
            lanes = pl.ds(gi * POOL_GD, POOL_GD)
            m_ref[:, lanes] = (mm * sc_ref[:, lanes]).astype(BF16)

    return pl.pallas_call(
        body, name=name, out_shape=jax.ShapeDtypeStruct((s, POOL_C), BF16), grid=(s // t,),
        in_specs=[_halo_before(t, *ZC_POOL), _rows(t, *ZC_POOL), _fixed((POOL_G, POOL_GD, POOL_GD)), _fixed((1, POOL_C))],
        out_specs=_rows(t, POOL_C), scratch_shapes=[pltpu.VMEM((t + CONV_HALO, POOL_C), F32)],
        compiler_params=_params("parallel"))(z, z, pool_w, pool_scale.reshape(1, -1))


def _pool_bwd(dm, z, pool_w, pool_scale, dz, name):
    s = z.shape[0]
    t = _pool_tile(s)
    nt = s // t

    def body(up_ref, u_ref, dm_ref, dmn_ref, w_ref, sc_ref, _, du_ref, dw_ref, dsc_ref, ubuf, ebuf):
        i = pl.program_id(0)
        ubuf[pl.ds(0, CONV_HALO), :] = jnp.where(i == 0, 0.0, up_ref[...].astype(F32))
        ubuf[pl.ds(CONV_HALO, t), :] = u_ref[...].astype(F32)

        @pl.when(i == 0)
        def _():
            dw_ref[...] = jnp.zeros_like(dw_ref)
            dsc_ref[...] = jnp.zeros_like(dsc_ref)

        dm_next = jnp.where(i == nt - 1, 0.0, dmn_ref[...])
        for gi, window in enumerate(POOL_WINDOWS):
            lanes = pl.ds(gi * POOL_GD, POOL_GD)
            wb = w_ref[gi].astype(BF16)
            scale = sc_ref[:, lanes]
            d = _pool_diff(ubuf, gi, window, i * t, t).astype(BF16)
            mm = jnp.dot(d, wb, preferred_element_type=F32)
            dmv = dm_ref[:, lanes]
            dsc_ref[:, lanes] += jnp.sum(dmv * mm, axis=0, keepdims=True)
            dmm = (dmv * scale).astype(BF16)
            dw_ref[gi] += lax.dot_general(d, dmm, (((0,), (0,)), ((), ())), preferred_element_type=F32)
            dd = lax.dot_general(dmm, wb, (((1,), (1,)), ((), ())), preferred_element_type=F32)
            dd_next = lax.dot_general((dm_next[:, gi * POOL_GD:(gi + 1) * POOL_GD] * scale).astype(BF16), wb,
                                      (((1,), (1,)), ((), ())), preferred_element_type=F32)
            ebuf[pl.ds(0, t), lanes] = dd / _pool_counts(i * t, t, window)
            ebuf[pl.ds(t, CONV_HALO), lanes] = dd_next / _pool_counts((i + 1) * t, CONV_HALO, window)
            du = -dd
            for ahead in range(window):
                du = du + ebuf[pl.ds(ahead, t), lanes]
            du_ref[:, lanes] = du.astype(BF16)

    into = _into(dz, 6, 0)
    du, dw, dsc = pl.pallas_call(
        body, name=name,
        out_shape=(into["out_shape"], jax.ShapeDtypeStruct((POOL_G, POOL_GD, POOL_GD), F32),
                   jax.ShapeDtypeStruct((1, POOL_C), F32)), grid=(nt,),
        in_specs=[_halo_before(t, *ZC_POOL), _rows(t, *ZC_POOL), _rows(t, POOL_C), _halo_after(t, POOL_C, 0, nt),
                  _fixed((POOL_G, POOL_GD, POOL_GD)), _fixed((1, POOL_C))] + into["in_specs"],
        out_specs=(_rows(t, *ZC_POOL), _fixed((POOL_G, POOL_GD, POOL_GD)), _fixed((1, POOL_C))),
        input_output_aliases=into["input_output_aliases"],
        scratch_shapes=[pltpu.VMEM((t + CONV_HALO, POOL_C), F32), pltpu.VMEM((t + CONV_HALO, POOL_C), F32)],
        compiler_params=_params("arbitrary"))(z, z, dm, dm, pool_w, pool_scale.reshape(1, -1), dz)
    return du, dw, dsc.reshape(-1)


def _gate_specs(ts):
    width, first = ZC_GATE
    return [_rows(ts, width, first + b) for b in range(3)]


def _branches_merge_fwd(z, acts, ws, name):
    s = z.shape[0]
    ts = min(s, 512)

    def body(g0, g1, g2, a0, a1, a2, w0, w1, w2, y0, y1, y2, m_ref):
        merged = jnp.zeros((ts, D_MODEL), F32)
        for g_ref, a_ref, w_ref, y_ref in ((g0, a0, w0, y0), (g1, a1, w1, y1), (g2, a2, w2, y2)):
            wv = jnp.concatenate([w_ref[c] for c in range(N_DEV)], axis=1)
            yb = jnp.dot(a_ref[...], wv, preferred_element_type=F32).astype(BF16)
            y_ref[...] = yb
            merged = merged + _sigmoid(g_ref[...].astype(F32)) * yb.astype(F32)
        m_ref[...] = merged.astype(BF16)

    out = jax.ShapeDtypeStruct((s, D_MODEL), BF16)
    res = pl.pallas_call(
        body, name=name, out_shape=(out,) * 4, grid=(s // ts,),
        in_specs=_gate_specs(ts) + [_rows(ts, a.shape[1]) for a in acts] + [_fixed(w.shape) for w in ws],
        out_specs=(_rows(ts, D_MODEL),) * 4, compiler_params=_params("parallel"))(z, z, z, *acts, *ws)
    return tuple(res[:3]), res[3]


def _merge_bwd(z, ys, dmerged, name):
    s = z.shape[0]
    ts = min(s, 256)

    def body(g0, g1, g2, y0, y1, y2, dm_ref, dy0, dy1, dy2, dz_ref):
        dmv = dm_ref[...]
        for b, (g_ref, y_ref, dy_ref) in enumerate(((g0, y0, dy0), (g1, y1, dy1), (g2, y2, dy2))):
            sg = _sigmoid(g_ref[...].astype(F32))
            dy_ref[...] = (dmv * sg).astype(BF16)
            dz_ref[:, pl.ds(b * D_MODEL, D_MODEL)] = (dmv * y_ref[...].astype(F32) * sg * (1.0 - sg)).astype(BF16)

    out = jax.ShapeDtypeStruct((s, D_MODEL), BF16)
    return pl.pallas_call(
        body, name=name, out_shape=(out,) * 3 + (jax.ShapeDtypeStruct((s, Z_W), BF16),), grid=(s // ts,),
        in_specs=_gate_specs(ts) + [_rows(ts, D_MODEL)] * 4,
        out_specs=(_rows(ts, D_MODEL),) * 3 + (_rows(ts, *ZC_GATES),),
        compiler_params=_params("parallel"))(z, z, z, *ys, dmerged)


def _ffn_up_fwd(h, w_gate, w_up, name):
    s, d = h.shape
    nb = w_gate.shape[2]
    f = N_DEV * nb
    tm, n_blk = min(s, 1024), 2
    tn = n_blk * nb
    blk = pl.BlockSpec((tm, tn), lambda i, j: (i, j))
    wspec = pl.BlockSpec((n_blk, d, nb), lambda i, j: (j, 0, 0))

    def body(h_ref, wg_ref, wu_ref, hg_ref, hu_ref, act_ref):
        hv = h_ref[...]
        g = jnp.dot(hv, jnp.concatenate([wg_ref[c] for c in range(n_blk)], axis=1), preferred_element_type=F32)
        u = jnp.dot(hv, jnp.concatenate([wu_ref[c] for c in range(n_blk)], axis=1), preferred_element_type=F32)
        hg_ref[...] = g.astype(hg_ref.dtype)
        hu_ref[...] = u.astype(hu_ref.dtype)
        act_ref[...] = (g * _sigmoid(g) * u).astype(BF16)

    return pl.pallas_call(
        body, name=name,
        out_shape=(jax.ShapeDtypeStruct((s, f), BF16),) * 3,
        grid=(s // tm, f // tn), in_specs=[pl.BlockSpec((tm, d), lambda i, j: (i, 0)), wspec, wspec],
        out_specs=(blk, blk, blk), compiler_params=_params("parallel", "parallel"))(h, w_gate, w_up)


def _ffn_down_bwd(dfo, w_down, hg, hu, name):
    s, d = dfo.shape
    f = w_down.shape[0]
    tm, tn = min(s, 1024), _tile(f, 1024)
    blk = pl.BlockSpec((tm, tn), lambda i, j: (i, j))

    def body(d_ref, w_ref, g_ref, u_ref, dg_ref, du_ref):
        dact = lax.dot_general(d_ref[...], w_ref[...], (((1,), (1,)), ((), ())), preferred_element_type=F32)
        g = g_ref[...].astype(F32)
        sg = _sigmoid(g)
        dg_ref[...] = (dact * u_ref[...].astype(F32) * (sg * (1.0 + g * (1.0 - sg)))).astype(BF16)
        du_ref[...] = (dact * g * sg).astype(BF16)

    out = jax.ShapeDtypeStruct((s, f), BF16)
    return pl.pallas_call(
        body, name=name, out_shape=(out, out), grid=(s // tm, f // tn),
        in_specs=[pl.BlockSpec((tm, d), lambda i, j: (i, 0)), pl.BlockSpec((tn, d), lambda i, j: (j, 0)), blk, blk],
        out_specs=(blk, blk), compiler_params=_params("parallel", "parallel"))(dfo, w_down, hg, hu)


def _loss_grad(y, target, name):
    s, d = y.shape
    ts = min(s, 512)

    def body(y_ref, t_ref, dy_ref, sq_ref):
        e = y_ref[...] - t_ref[...]
        dy_ref[...] = e / d

        @pl.when(pl.program_id(0) == 0)
        def _():
            sq_ref[...] = jnp.zeros_like(sq_ref)

        sq_ref[...] += jnp.sum(e * e, axis=0, keepdims=True)

    return pl.pallas_call(
        body, name=name, out_shape=(jax.ShapeDtypeStruct((s, d), F32), jax.ShapeDtypeStruct((1, d), F32)),
        grid=(s // ts,), in_specs=[_rows(ts, d), _rows(ts, d)], out_specs=(_rows(ts, d), _fixed((1, d))),
        compiler_params=_params("arbitrary"))(y, target)


def _adamw(w, g, m, v, name):
    shape = w.shape
    cols = shape[-1]
    keep3 = w.ndim == 3 and shape[1] < SUBLANES
    view = shape if keep3 else (math.prod(shape[:-1]), cols)
    rows = view[0]
    if keep3:
        cap = max(1, (1 << 20) // (SUBLANES * cols * 4))
        tr = max(t for t in range(1, cap + 1) if rows % t == 0)
    else:
        tr = _row_tile(rows, cols * 4)

    def body(w_ref, g_ref, m_ref, v_ref, d_ref, mo_ref, vo_ref):
        gv = g_ref[...]
        mn = B1 * m_ref[...] + (1.0 - B1) * gv
        vn = B2 * v_ref[...] + (1.0 - B2) * (gv * gv)
        m_hat = mn / (1.0 - B1 ** STEP)
        v_hat = vn / (1.0 - B2 ** STEP)
        d_ref[...] = -LR * (m_hat / (jnp.sqrt(v_hat) + ADAM_EPS) + WD * w_ref[...])
        mo_ref[...] = mn
        vo_ref[...] = vn

    spec = pl.BlockSpec((tr,) + view[1:], lambda i: (i,) + (0,) * (len(view) - 1))
    out = jax.ShapeDtypeStruct(view, F32)
    res = pl.pallas_call(
        body, name=name, out_shape=(out,) * 3, grid=(rows // tr,), in_specs=[spec] * 4, out_specs=(spec,) * 3,
        compiler_params=_params("parallel"))(*[t.reshape(view) for t in (w, g, m, v)])
    return tuple(r.reshape(shape) for r in res)


LANE_MAJOR = ("w_uq", "w_uk", "w_uv", "w_gate", "w_up")


def _lane_major(name, a):
    if name == "w_in":
        return a.transpose(2, 0, 1)
    if name in LANE_MAJOR:
        return a.transpose(0, 2, 1)
    return a


def _from_lane_major(name, a):
    if name == "w_in":
        return a.transpose(1, 2, 0)
    return _lane_major(name, a)


ANY = pl.BlockSpec(memory_space=pl.ANY)


class _GatherRide:
    def __init__(self, arrays):
        n = len(arrays)
        self.arrays = list(arrays)
        self.out_shape = [jax.ShapeDtypeStruct((N_DEV,) + a.shape, a.dtype) for a in arrays]
        self.scratch = [pltpu.SemaphoreType.DMA((n, 7)), pltpu.SemaphoreType.DMA((n, 7)), pltpu.SemaphoreType.DMA((n,))]

    def _copies(self, ins, outs, sems):
        send_sems, recv_sems, local_sems = sems
        n = len(self.arrays)
        x, y, c = lax.axis_index("x"), lax.axis_index("y"), lax.axis_index("c")
        me, sibling = (x, y, c), (x, y, 1 - c)
        chips = [(1 - x, y), (x, 1 - y), (1 - x, 1 - y)]

        def slot(a, px, py, pc):
            return outs[a].at[4 * px + 2 * py + pc]

        def copy(a, k, block, to, src=None):
            return pltpu.make_async_remote_copy(
                src_ref=slot(a, *block) if src is None else src, dst_ref=slot(a, *block), send_sem=send_sems.at[a, k],
                recv_sem=recv_sems.at[a, k], device_id=to, device_id_type=MESH)

        mine = [pltpu.make_async_copy(ins[a], slot(a, *me), local_sems.at[a]) for a in range(n)]
        first = []
        for a in range(n):
            first.append(copy(a, 0, me, sibling, src=ins[a]))
            first += [copy(a, 1 + j, me, (*chip, c), src=ins[a]) for j, chip in enumerate(chips)]
        return n, me, sibling, chips, c, copy, mine, first

    def start(self, ins, outs, sems):
        _, _, _, _, _, _, mine, first = self._copies(ins, outs, sems)
        for cp in mine + first:
            cp.start()

    def finish(self, ins, outs, sems):
        n, me, sibling, chips, c, copy, mine, first = self._copies(ins, outs, sems)
        passed = []
        for j, chip in enumerate(chips):
            for a in range(n):
                copy(a, 1 + j, (*chip, c), me).wait_recv()
                passed.append(copy(a, 4 + j, (*chip, c), sibling))
                passed[-1].start()
        for a in range(n):
            copy(a, 0, sibling, me).wait_recv()
            for j, chip in enumerate(chips):
                copy(a, 4 + j, (*chip, 1 - c), me).wait_recv()
        for cp in first + passed:
            cp.wait_send()
        for cp in mine:
            cp.wait()


class _ReduceRide:
    def __init__(self, arrays):
        n = len(arrays)
        self.arrays = list(arrays)
        self.out_shape = [jax.ShapeDtypeStruct(a.shape, a.dtype) for a in arrays]
        self.scratch = [pltpu.SemaphoreType.DMA((n, 7)), pltpu.SemaphoreType.DMA((n, 7)), pltpu.SemaphoreType.DMA((n,))]

    def _copies(self, ins, outs, sems):
        send_sems, recv_sems, local_sems = sems
        n = len(self.arrays)
        x, y, c = lax.axis_index("x"), lax.axis_index("y"), lax.axis_index("c")
        mine = [pltpu.make_async_copy(ins[a].at[4 * x + 2 * y + c], outs[a].at[0], local_sems.at[a]) for a in range(n)]
        copies = []
        for a in range(n):
            for k in range(1, N_DEV):
                px = 1 - x if k & 4 else x
                py = 1 - y if k & 2 else y
                pc = 1 - c if k & 1 else c
                copies.append(pltpu.make_async_remote_copy(
                    src_ref=ins[a].at[4 * px + 2 * py + pc], dst_ref=outs[a].at[k], send_sem=send_sems.at[a, k - 1],
                    recv_sem=recv_sems.at[a, k - 1], device_id=(px, py, pc), device_id_type=MESH))
        return mine, copies

    def start(self, ins, outs, sems):
        mine, copies = self._copies(ins, outs, sems)
        for cp in mine + copies:
            cp.start()

    def finish(self, ins, outs, sems):
        mine, copies = self._copies(ins, outs, sems)
        for cp in copies + mine:
            cp.wait()


def _run_ride(ride, name):
    n = len(ride.arrays)

    def body(*refs):
        ins, outs, sems = refs[:n], refs[n:2 * n], refs[2 * n:]
        ride.start(ins, outs, sems)
        ride.finish(ins, outs, sems)

    return pl.pallas_call(body, name=name, out_shape=ride.out_shape, in_specs=[ANY] * n, out_specs=[ANY] * n,
                          scratch_shapes=ride.scratch)(*ride.arrays)


def _all_gather(arrays, name):
    return _run_ride(_GatherRide(arrays), name)


def _swap_with_sibling(arrays, name):
    n = len(arrays)

    def body(*refs):
        ins, outs = refs[:n], refs[n:2 * n]
        send_sems, recv_sems = refs[2 * n:]
        x, y, c = lax.axis_index("x"), lax.axis_index("y"), lax.axis_index("c")
        copies = [pltpu.make_async_remote_copy(
            src_ref=ins[a].at[1 - c], dst_ref=outs[a], send_sem=send_sems.at[a], recv_sem=recv_sems.at[a],
            device_id=(x, y, 1 - c), device_id_type=MESH) for a in range(n)]
        for cp in copies:
            cp.start()
        for cp in copies:
            cp.wait()

    return pl.pallas_call(
        body, name=name, out_shape=[jax.ShapeDtypeStruct(a.shape[1:], a.dtype) for a in arrays],
        in_specs=[ANY] * n, out_specs=[ANY] * n,
        scratch_shapes=[pltpu.SemaphoreType.DMA((n,)), pltpu.SemaphoreType.DMA((n,))])(*arrays)


class _ChipExchangeRide:
    def __init__(self, arrays):
        n = len(arrays)
        self.arrays = list(arrays)
        self.out_shape = [jax.ShapeDtypeStruct(a.shape, a.dtype) for a in arrays]
        self.scratch = [pltpu.SemaphoreType.DMA((n, 3)), pltpu.SemaphoreType.DMA((n, 3)), pltpu.SemaphoreType.DMA((n,))]

    def _copies(self, ins, outs, sems):
        send_sems, recv_sems, local_sems = sems
        n = len(self.arrays)
        x, y, c = lax.axis_index("x"), lax.axis_index("y"), lax.axis_index("c")
        partners = [(x, 1 - y), (1 - x, y), (1 - x, 1 - y)]
        mine = [pltpu.make_async_copy(ins[a].at[2 * x + y], outs[a].at[0], local_sems.at[a]) for a in range(n)]
        copies = [pltpu.make_async_remote_copy(
            src_ref=ins[a].at[2 * px + py], dst_ref=outs[a].at[1 + k], send_sem=send_sems.at[a, k],
            recv_sem=recv_sems.at[a, k], device_id=(px, py, c), device_id_type=MESH)
            for a in range(n) for k, (px, py) in enumerate(partners)]
        return mine, copies

    def start(self, ins, outs, sems):
        mine, copies = self._copies(ins, outs, sems)
        for cp in mine + copies:
            cp.start()

    def finish(self, ins, outs, sems):
        mine, copies = self._copies(ins, outs, sems)
        for cp in copies + mine:
            cp.wait()


class _Combo:
    def __init__(self, rides):
        self.rides = rides
        self.arrays = [a for r in rides for a in r.arrays]
        self.out_shape = [o for r in rides for o in r.out_shape]
        self.scratch = [sc for r in rides for sc in r.scratch]

    def _parts(self, ins, outs, sems):
        at_a = at_s = 0
        for r in self.rides:
            na, ns = len(r.arrays), len(r.scratch)
            yield r, ins[at_a:at_a + na], outs[at_a:at_a + na], sems[at_s:at_s + ns]
            at_a, at_s = at_a + na, at_s + ns

    def start(self, ins, outs, sems):
        for r, i, o, sm in self._parts(ins, outs, sems):
            r.start(i, o, sm)

    def finish(self, ins, outs, sems):
        for r, i, o, sm in self._parts(ins, outs, sems):
            r.finish(i, o, sm)


def _as_rows(a, lead):
    return a.reshape(a.shape[:lead] + (math.prod(a.shape[lead:-1]), a.shape[-1]))


def _add_pairs(a, b, name):
    a2, b2 = _as_rows(a, 0), _as_rows(b, 0)
    rows, cols = a2.shape
    tr = _row_tile(rows, cols * 4)

    def body(a_ref, b_ref, o_ref):
        o_ref[...] = (a_ref[...].astype(F32) + b_ref[...].astype(F32)).astype(o_ref.dtype)

    spec = _rows(tr, cols)
    out = pl.pallas_call(body, name=name, out_shape=jax.ShapeDtypeStruct(a2.shape, a.dtype), grid=(rows // tr,),
                         in_specs=[spec, spec], out_specs=spec, compiler_params=_params("parallel"))(a2, b2)
    return out.reshape(a.shape)


def _sum_blocks(a, name):
    a3 = _as_rows(a, 1)
    n, rows, cols = a3.shape
    tr = _row_tile(rows, n * cols * 4)

    def body(a_ref, o_ref):
        tot = a_ref[0].astype(F32)
        for k in range(1, n):
            tot = tot + a_ref[k].astype(F32)
        o_ref[...] = tot

    out = pl.pallas_call(body, name=name, out_shape=jax.ShapeDtypeStruct((rows, cols), F32), grid=(rows // tr,),
                         in_specs=[pl.BlockSpec((n, tr, cols), lambda j: (0, j, 0))], out_specs=_rows(tr, cols),
                         compiler_params=_params("parallel"))(a3)
    return out.reshape(a.shape[1:])


MIX_GROUPS = ("w_in", "w_uq", "w_uk", "w_uv", "w_attn_o", "w_conv_o", "w_pool_o", "w_mix_o")
FFN_GROUPS = ("w_gate", "w_up", "w_down")
MIX_EARLY = ("w_attn_o", "w_conv_o", "w_pool_o", "w_mix_o")
MIX_LATE = ("w_in", "w_uq", "w_uk", "w_uv")


def _pad_axis(a, axis, size):
    pad = [(0, 0)] * a.ndim
    pad[axis] = (0, size - a.shape[axis])
    return jnp.pad(a, pad)


def _local_groups(sh, l):
    out = {n: sh[n][l] for n in BIG}
    for n in ("w_uq", "w_uk", "w_uv"):
        out[n] = _pad_axis(out[n], -1, HEAD_PAD)
    for n in ("w_gate", "w_up"):
        out[n] = _pad_axis(out[n], -1, FF_SHARD_PAD)
    out["w_down"] = _pad_axis(out["w_down"], 0, FF_SHARD_PAD)
    return {n: v.astype(BF16) for n, v in out.items()}


def _arrange_w_in(blocks):
    parts, pos = [], 0
    for ref_lo, ref_hi, at in sorted(W_IN_PIECES, key=lambda p: p[2]):
        if at > pos:
            parts.append(jnp.zeros((blocks.shape[1], at - pos), blocks.dtype))
        for d in range(N_DEV):
            lo, hi = max(ref_lo, d * W_IN_SHARD), min(ref_hi, (d + 1) * W_IN_SHARD)
            if lo < hi:
                parts.append(blocks[d][:, lo - d * W_IN_SHARD:hi - d * W_IN_SHARD])
        pos = at + ref_hi - ref_lo
    if pos < Z_W:
        parts.append(jnp.zeros((blocks.shape[1], Z_W - pos), blocks.dtype))
    return jnp.concatenate(parts, axis=1)


def _w_in_shard(g, d):
    parts = []
    for ref_lo, ref_hi, at in W_IN_PIECES:
        lo, hi = max(ref_lo, d * W_IN_SHARD), min(ref_hi, (d + 1) * W_IN_SHARD)
        if lo < hi:
            parts.append(g[:, at + lo - ref_lo:at + hi - ref_lo])
    return jnp.concatenate(parts, axis=1)


def _mixer_weights(gat):
    w = {n: v for n, v in gat.items() if n != "w_in"}
    attn_o = gat["w_attn_o"].reshape(N_DEV, N_HEADS, V_HEAD, LANES)
    w["w_attn_o"] = _pad_axis(attn_o, 2, HEAD_PAD).reshape(N_DEV, N_HEADS * HEAD_PAD, LANES)
    w["w_mix_o"] = gat["w_mix_o"].reshape(D_MODEL, D_MODEL)
    return w


def _ffn_weights(gat):
    return {"w_gate": gat["w_gate"], "w_up": gat["w_up"], "w_down": gat["w_down"].reshape(D_FF_PAD, D_MODEL)}


def _mixer_grad_groups(gb):
    g = dict(gb)
    if "w_in" in gb:
        g["w_in"] = jnp.stack([_w_in_shard(gb["w_in"], d) for d in range(N_DEV)])
    if "w_attn_o" in gb:
        attn_o = gb["w_attn_o"].reshape(N_DEV, N_HEADS, HEAD_PAD, LANES)[:, :, :V_HEAD]
        g["w_attn_o"] = attn_o.reshape(N_DEV, N_HEADS * V_HEAD, LANES)
    if "w_mix_o" in gb:
        g["w_mix_o"] = gb["w_mix_o"].reshape(N_DEV, D_MODEL // N_DEV, D_MODEL)
    return g


def _ffn_grad_groups(gb):
    return {"w_gate": gb["w_gate"], "w_up": gb["w_up"], "w_down": gb["w_down"].reshape(N_DEV, FF_SHARD_PAD, D_MODEL)}


def _grads_from_groups(tot):
    g = dict(tot)
    g["w_uq"] = tot["w_uq"][:, :QK_NOPE + QK_ROPE]
    g["w_uk"], g["w_uv"] = tot["w_uk"][:, :QK_NOPE], tot["w_uv"][:, :V_HEAD]
    g["w_gate"], g["w_up"] = tot["w_gate"][:, :FF_SHARD], tot["w_up"][:, :FF_SHARD]
    g["w_down"] = tot["w_down"][:FF_SHARD]
    return g


SMALL_GROUPS = (
    (D_MODEL, ("mix_norm_pre", "mix_norm_post", "ffn_norm_pre", "ffn_norm_post")),
    (CONV_C, ("conv_w", "conv_b", "conv_ln_g", "conv_ln_b", "pool_scale")),
    (Q_RANK, ("q_norm",)), (KV_RANK, ("kv_norm",)), (POOL_GD, ("pool_w",)),
)


def _small_rows(name):
    return {"conv_w": CONV_HALO, "pool_w": POOL_G * POOL_GD}.get(name, SUBLANES)


def _small_groups(small):
    out = []
    for width, names in SMALL_GROUPS:
        parts = []
        for l in range(DEPTH):
            for n in names:
                part = small[l][n].reshape(-1, width)
                parts.append(_pad_axis(part, 0, _small_rows(n)))
        out.append(jnp.concatenate(parts, axis=0))
    return out


def _small_from_groups(groups):
    shapes = {"conv_w": (CONV_W, CONV_C), "pool_w": (POOL_G, POOL_GD, POOL_GD)}
    out = {}
    for (width, names), g in zip(SMALL_GROUPS, groups):
        row = 0
        for l in range(DEPTH):
            for n in names:
                rows = _small_rows(n)
                real = {"conv_w": CONV_W, "pool_w": POOL_G * POOL_GD}.get(n, 1)
                out.setdefault(n, []).append(g[row:row + real].reshape(shapes.get(n, (width,))))
                row += rows
    return {n: jnp.stack(v) for n, v in out.items()}


def _mixer_fwd(x, h, tables, sm, plan, l):
    nm = lambda n: f"{n}_l{l}"
    if h is None:
        h = _rms_fwd(x, (D_MODEL, 0), sm["mix_norm_pre"], BF16, nm("mix_pre_norm"))
    w_in, ride = plan.w_in(l), plan.in_proj_ride(l)
    if ride is None:
        z = _matmul(h, w_in, "nn", BF16, nm("in_proj"))
    else:
        z, rode = _matmul(h, w_in, "nn", BF16, nm("in_proj"), ride=ride)
        plan.in_proj_done(l, rode)
    w = dict(plan.mixer_weights(l), w_in=w_in)
    cq, ckv, q, k, v = _qkv_up_fwd(z, sm["q_norm"], sm["kv_norm"], w["w_uq"], w["w_uk"], w["w_uv"], tables, nm("qkv_up"))
    (o, lse), rode = _flash_fwd(q, k, v, nm("flash_fwd"), plan.fwd_ride(l))
    plan.fwd_done(l, rode)
    hc, co = _conv_fwd(z, sm["conv_w"], sm["conv_b"], sm["conv_ln_g"], sm["conv_ln_b"], nm("conv_fwd"))
    pm = _pool_fwd(z, sm["pool_w"], sm["pool_scale"], nm("pool_fwd"))
    ys, merged = _branches_merge_fwd(z, (o, hc, pm), (w["w_attn_o"], w["w_conv_o"], w["w_pool_o"]), nm("branches_merge"))
    mo = _matmul(merged, w["w_mix_o"], "nn", F32, nm("mix_out"))
    x_mid, h2 = _rms_fwd(mo, (D_MODEL, 0), sm["mix_norm_post"], F32, nm("mix_post_norm"), res=x, then=sm["ffn_norm_pre"])
    saved = dict(x=x, h=h, z=z, cq=cq, ckv=ckv, q=q, k=k, v=v, o=o, lse=lse, hc=hc, co=co, pm=pm, ys=ys, merged=merged,
                 mo=mo)
    return x_mid, h2, saved, w


def _ffn_fwd(x_mid, h2, w, sm, tag, next_gain):
    nm = lambda n: f"{n}_{tag}"
    hg, hu, act = _ffn_up_fwd(h2, w["w_gate"], w["w_up"], nm("ffn_up_fwd"))
    fo = _matmul(act, w["w_down"], "nn", F32, nm("ffn_down"))
    out = _rms_fwd(fo, (D_MODEL, 0), sm["ffn_norm_post"], F32, nm("ffn_post_norm"), res=x_mid, then=next_gain)
    out, h_next = out if next_gain is not None else (out, None)
    saved = dict(x_mid=x_mid, h2=h2, hg=hg, hu=hu, act=act, fo=fo)
    return out, h_next, saved


def _ffn_bwd(dout, sv, w, sm, tag):
    nm = lambda n: f"{n}_{tag}"
    gb, gs = {}, {}
    dfo, gs["ffn_norm_post"] = _rms_bwd(sv["fo"], (D_MODEL, 0), sm["ffn_norm_post"], dout, BF16, nm("ffn_post_norm_bwd"))
    gb["w_down"] = _matmul(sv["act"], dfo, "tn", BF16, nm("ffn_down_dw"))
    dhg, dhu = _ffn_down_bwd(dfo, w["w_down"], sv["hg"], sv["hu"], nm("ffn_down_bwd"))
    dh2_g = _matmul(dhg, w["w_gate"], "nt", F32, nm("ffn_gate_dx"))
    dh2 = _matmul(dhu, w["w_up"], "nt", F32, nm("ffn_up_dx"), add=dh2_g)
    gb["w_gate"] = _matmul(sv["h2"], dhg, "tn", BF16, nm("ffn_gate_dw"), blocked=True)
    gb["w_up"] = _matmul(sv["h2"], dhu, "tn", BF16, nm("ffn_up_dw"), blocked=True)
    dmid, gs["ffn_norm_pre"] = _rms_bwd(sv["x_mid"], (D_MODEL, 0), sm["ffn_norm_pre"], dh2, F32, nm("ffn_pre_norm_bwd"),
                                        add=dout)
    return dmid, gb, gs


def _mixer_bwd(dmid, sv, tables, w, sm, plan, l, pack_small):
    nm = lambda n: f"{n}_l{l}"
    gb, gs = {}, {}
    dmo, gs["mix_norm_post"] = _rms_bwd(sv["mo"], (D_MODEL, 0), sm["mix_norm_post"], dmid, BF16, nm("mix_post_norm_bwd"))
    dmerged = _matmul(dmo, w["w_mix_o"], "nt", F32, nm("mix_out_dx"))
    gb["w_mix_o"] = _matmul(sv["merged"], dmo, "tn", BF16, nm("mix_out_dw"))
    dya, dyc, dyp, dz = _merge_bwd(sv["z"], sv["ys"], dmerged, nm("merge_bwd"))
    dpm = _matmul(dyp, w["w_pool_o"], "nt", F32, nm("pool_out_dx"))
    gb["w_pool_o"] = _matmul(sv["pm"], dyp, "tn", BF16, nm("pool_out_dw"), blocked=True)
    dz, gs["pool_w"], gs["pool_scale"] = _pool_bwd(dpm, sv["z"], sm["pool_w"], sm["pool_scale"], dz, nm("pool_bwd"))
    dhc = _matmul(dyc, w["w_conv_o"], "nt", F32, nm("conv_out_dx"))
    gb["w_conv_o"] = _matmul(sv["hc"], dyc, "tn", BF16, nm("conv_out_dw"), blocked=True)
    dco, gs["conv_ln_g"], gs["conv_ln_b"], gs["conv_b"] = _conv_bwd_norm(dhc, sv["co"], sm["conv_ln_g"], sm["conv_ln_b"],
                                                                        nm("conv_bwd_norm"))
    dz, gs["conv_w"] = _conv_bwd_taps(dco, sv["z"], sm["conv_w"], dz, nm("conv_bwd_taps"))
    gb["w_attn_o"] = _matmul(sv["o"], dya, "tn", BF16, nm("attn_out_dw"), blocked=True)
    delta, dob = _attn_out_bwd(dya, w["w_attn_o"], sv["o"], nm("attn_out_bwd"))
    (dq, dk, dv), rode = _flash_bwd(sv["q"], sv["k"], sv["v"], dob, sv["lse"], delta, nm("flash_bwd"),
                                  plan.bwd_ride(l, gb))
    plan.bwd_done(l, rode)
    dqf, dkf, dz, gs["q_norm"], gs["kv_norm"] = _qkv_up_bwd(
        dq, dk, dv, sv["z"], w["w_uq"], w["w_uk"], w["w_uv"], tables, sm["q_norm"], sm["kv_norm"], dz, nm("qkv_up_bwd"))
    gb["w_uq"] = _matmul(sv["cq"], dqf, "tn", BF16, nm("q_up_dw"), blocked=True)
    gb["w_uk"] = _matmul(sv["ckv"], dkf, "tn", BF16, nm("k_up_dw"), blocked=True)
    gb["w_uv"] = _matmul(sv["ckv"], dv, "tn", BF16, nm("v_up_dw"), blocked=True)
    gb["w_in"] = _matmul(sv["h"], dz, "tn", BF16, nm("in_proj_dw"))
    plan.add_grads(l, "mix", gb)
    ride, small_gathered = plan.tail_ride(l, pack_small(gs)), []
    if ride is None:
        dh = _matmul(dz, w["w_in"], "nt", F32, nm("in_proj_dx"))
    else:
        dh, rode = _matmul(dz, w["w_in"], "nt", F32, nm("in_proj_dx"), ride=ride)
        small_gathered = plan.tail_done(l, rode)
    dx, gs["mix_norm_pre"] = _rms_bwd(sv["x"], (D_MODEL, 0), sm["mix_norm_pre"], dh, F32, nm("mix_pre_norm_bwd"), add=dmid)
    return dx, gs, small_gathered


def _part_groups(part):
    return {"mix": MIX_GROUPS, "ffn": FFN_GROUPS, "early": MIX_EARLY, "late": MIX_LATE}[part]


class _Plan:
    def __init__(self, shards, conv_w):
        self.local = [_local_groups(shards, l) for l in range(DEPTH)]
        self.conv_w = conv_w
        self.gat, self.send, self.recv = {}, {}, {}

    @staticmethod
    def _riders(l):
        return [(l, "ffn")] + ([(l + 1, "mix")] if l + 1 < DEPTH else [])

    @staticmethod
    def _grad_riders(l):
        return [(l, "ffn"), (l, "early")] + ([(l + 1, "late")] if l + 1 < DEPTH else [])

    def gather_first(self):
        w_in, conv_w = _all_gather([self.local[0]["w_in"], self.conv_w], "gather_w_in_l0")
        self.gat[(0, "mix")] = {"w_in": w_in}
        return conv_w

    def w_in(self, l):
        return _arrange_w_in(self.gat[(l, "mix")]["w_in"])

    def in_proj_ride(self, l):
        return _GatherRide([self.local[0][g] for g in MIX_GROUPS[1:]]) if l == 0 else None

    def in_proj_done(self, l, outs):
        self.gat[(l, "mix")].update(zip(MIX_GROUPS[1:], outs))

    def fwd_ride(self, l):
        return _GatherRide([self.local[ll][g] for ll, part in self._riders(l) for g in _part_groups(part)])

    def fwd_done(self, l, outs):
        outs = list(outs)
        for ll, part in self._riders(l):
            self.gat[(ll, part)] = {g: outs.pop(0) for g in _part_groups(part)}

    def mixer_weights(self, l):
        return _mixer_weights(self.gat[(l, "mix")])

    def ffn_weights(self, l):
        return _ffn_weights(self.gat[(l, "ffn")])

    def add_grads(self, l, part, gb):
        if part == "ffn":
            self.send[(l, "ffn")] = _ffn_grad_groups(gb)
        else:
            self.send.setdefault((l, "late"), {}).update(_mixer_grad_groups({g: gb[g] for g in MIX_LATE if g in gb}))

    def bwd_ride(self, l, gb_early):
        self.send[(l, "early")] = _mixer_grad_groups({g: gb_early[g] for g in MIX_EARLY})
        return _ReduceRide([self.send[(ll, part)][g] for ll, part in self._grad_riders(l) for g in _part_groups(part)])

    def bwd_done(self, l, outs):
        outs = list(outs)
        for ll, part in self._grad_riders(l):
            self.recv[(ll, part)] = {g: outs.pop(0) for g in _part_groups(part)}

    def tail_ride(self, l, small_groups):
        if l > 0:
            return None
        send = [self.send[(0, "late")][g] for g in MIX_LATE]
        by_core = [a.reshape((4, 2) + a.shape[1:]).transpose((1, 0) + tuple(range(2, a.ndim + 1))) for a in send]
        core = lax.axis_index("c")
        own = [lax.dynamic_index_in_dim(a, core, axis=0, keepdims=False) for a in by_core]
        got = _swap_with_sibling(by_core, "reduce_d2d")
        pairs = [_add_pairs(a, b, f"reduce_pair_add_{g}") for g, a, b in zip(MIX_LATE, own, got)]
        return _Combo([_ChipExchangeRide(pairs), _GatherRide(small_groups)])

    def tail_done(self, l, outs):
        self.recv[(l, "late")] = dict(zip(MIX_LATE, outs[:len(MIX_LATE)]))
        return outs[len(MIX_LATE):]

    def finish(self):
        layers = []
        for l in range(DEPTH):
            tot = {g: _sum_blocks(a, f"reduce_sum_{g}_l{l}") for part in ("early", "late", "ffn")
                   for g, a in self.recv[(l, part)].items()}
            layers.append(_grads_from_groups(tot))
        return layers


def _local_step(x, positions, target, smalls, plan):
    tables = _rope_tables(positions)
    saved = []
    h, h_norm = x, None
    for l in range(DEPTH):
        h, h2, svm, wm = _mixer_fwd(h, h_norm, tables, smalls[l], plan, l)
        wf = plan.ffn_weights(l)
        next_gain = smalls[l + 1]["mix_norm_pre"] if l + 1 < DEPTH else None
        h, h_norm, svf = _ffn_fwd(h, h2, wf, smalls[l], f"l{l}", next_gain)
        saved.append((svm, svf, wm, wf))
    dy, sq = _loss_grad(h, target, "loss_grad")
    small = [None] * DEPTH
    for l in reversed(range(DEPTH)):
        svm, svf, wm, wf = saved[l]
        dmid, gbf, gsf = _ffn_bwd(dy, svf, wf, smalls[l], f"l{l}")
        plan.add_grads(l, "ffn", gbf)

        def pack_small(gs, l=l, gsf=gsf):
            if l > 0:
                return None
            return _small_groups([{**gsf, **gs, "mix_norm_pre": jnp.zeros((D_MODEL,), F32)}] + small[1:])

        dy, gsm, small_gathered = _mixer_bwd(dmid, svm, tables, wm, smalls[l], plan, l, pack_small)
        small[l] = {**gsf, **gsm}
    return sq, dy, small, small_gathered


def kernel(x, positions, mix_norm_pre, w_in, q_norm, w_uq, kv_norm, w_uk, w_uv, w_attn_o, conv_w, conv_b, conv_ln_g, conv_ln_b, w_conv_o, pool_w, pool_scale, w_pool_o, w_mix_o, mix_norm_post, ffn_norm_pre, w_gate, w_up, w_down, ffn_norm_post, loss_target, m_mix_norm_pre, m_w_in, m_q_norm, m_w_uq, m_kv_norm, m_w_uk, m_w_uv, m_w_attn_o, m_conv_w, m_conv_b, m_conv_ln_g, m_conv_ln_b, m_w_conv_o, m_pool_w, m_pool_scale, m_w_pool_o, m_w_mix_o, m_mix_norm_post, m_ffn_norm_pre, m_w_gate, m_w_up, m_w_down, m_ffn_norm_post, v_mix_norm_pre, v_w_in, v_q_norm, v_w_uq, v_kv_norm, v_w_uk, v_w_uv, v_w_attn_o, v_conv_w, v_conv_b, v_conv_ln_g, v_conv_ln_b, v_w_conv_o, v_pool_w, v_pool_scale, v_w_pool_o, v_w_mix_o, v_mix_norm_post, v_ffn_norm_pre, v_w_gate, v_w_up, v_w_down, v_ffn_norm_post):
    given = dict(locals())
    dev = 4 * lax.axis_index("x") + 2 * lax.axis_index("y") + lax.axis_index("c")

    plan = _Plan({n: given[n] for n in BIG}, conv_w)
    cw = CONV_C // N_DEV
    conv_w_full = plan.gather_first().transpose(1, 2, 0, 3).reshape(DEPTH, CONV_W, CONV_C)
    smalls = []
    for l in range(DEPTH):
        sm = {n: given[n][l] for n in SMALL if n != "conv_w"}
        sm["conv_w"] = _pad_axis(conv_w_full[l], 0, CONV_HALO)
        smalls.append(sm)

    sq, grad_x, small, small_groups = _local_step(x[0], positions[0], loss_target[0], smalls, plan)
    loss = lax.psum(0.5 / D_MODEL * jnp.sum(sq), ("x", "y", "c"))
    per_layer = plan.finish()
    views = {}
    for n in BIG:
        if n == "w_in":
            views[n] = jnp.stack([per_layer[l][n].T for l in range(DEPTH)], axis=1)
        elif n in LANE_MAJOR:
            views[n] = jnp.stack([per_layer[l][n].T for l in range(DEPTH)])
        else:
            views[n] = jnp.stack([per_layer[l][n] for l in range(DEPTH)])
    grads = {n: _from_lane_major(n, views[n]) for n in BIG}

    small_sum = _small_from_groups([_sum_blocks(g, f"sum_small_grads_{i}") for i, g in enumerate(small_groups)])
    last = _pad_axis(small[0]["mix_norm_pre"].reshape(1, D_MODEL), 0, SUBLANES)
    last_sum = _sum_blocks(_all_gather([last], "gather_last_norm_grad")[0], "sum_last_norm_grad")[0]
    small_sum["mix_norm_pre"] = small_sum["mix_norm_pre"].at[0].set(last_sum)
    for n in SMALL:
        grads[n] = small_sum[n]
    grads["conv_w"] = lax.dynamic_slice_in_dim(small_sum["conv_w"], dev * cw, cw, axis=2)

    delta, new_m, new_v = {}, {}, {}
    for n in WEIGHTS:
        g_view = views[n] if n in views else grads[n]
        w_view, m_view, v_view = [_lane_major(n, given[k]) for k in (n, "m_" + n, "v_" + n)]
        res = _adamw(w_view, g_view, m_view, v_view, f"adamw_{n}")
        delta[n], new_m[n], new_v[n] = [_from_lane_major(n, r) for r in res]
    return (loss, grad_x[None], *[grads[n] for n in WEIGHTS], *[delta[n] for n in WEIGHTS],
            *[new_m[n] for n in WEIGHTS], *[new_v[n] for n in WEIGHTS])
```

```python
import functools
import math

import jax
import jax.numpy as jnp
from jax import lax
from jax.experimental import pallas as pl
from jax.experimental.pallas import tpu as pltpu

F32, BF16 = jnp.float32, jnp.bfloat16
MESH = pl.DeviceIdType.MESH

LANES = 128
SUBLANES = 8
VMEM_LIMIT_BYTES = 56 * 1024 * 1024
MATMUL_VMEM_BYTES = 40 * 1024 * 1024

N_DEV = 8
D_MODEL = 1024
DEPTH = 2
N_HEADS = 8
QK_NOPE, QK_ROPE, V_HEAD = 64, 32, 64
HEAD_PAD = LANES
Q_RANK, KV_RANK = 384, 256
ROPE_THETA = 10000.0
CONV_C, CONV_W = 512, 31
CONV_HALO = 32
POOL_WINDOWS = (2, 4, 8, 16)
POOL_C, POOL_G = 512, 4
POOL_GD = POOL_C // POOL_G
D_FF = 2816
FF_SHARD = D_FF // N_DEV
FF_SHARD_PAD = 3 * LANES
D_FF_PAD = N_DEV * FF_SHARD_PAD
W_IN_SHARD = 660
EPS = 1e-6
ATTN_SCALE = 1.0 / math.sqrt(QK_NOPE + QK_ROPE)
LOG2E = 1.4426950408889634
LR, B1, B2, ADAM_EPS, WD, STEP = 0.001, 0.9, 0.999, 1e-08, 0.01, 10

Z_W = 5376
ZC_GATE = (1024, 0)
ZC_GATES = (3072, 0)
ZC_CONV_A = (512, 6)
ZC_CONV_G = (512, 7)
ZC_CONV = (1024, 3)
ZC_POOL = (512, 8)
ZC_Q = (384, 12)
ZC_KR = (128, 39)
ZC_KV = (256, 20)
ZC_QKR = (768, 6)
W_IN_PIECES = ((0, 384, 4608), (384, 640, 5120), (640, 672, 5056), (672, 1696, 3072), (1696, 2208, 4096),
               (2208, 5280, 0))

BIG = ("w_in", "w_uq", "w_uk", "w_uv", "w_attn_o", "w_conv_o", "w_pool_o", "w_mix_o", "w_gate", "w_up", "w_down")
SMALL = ("mix_norm_pre", "q_norm", "kv_norm", "conv_w", "conv_b", "conv_ln_g", "conv_ln_b", "pool_w", "pool_scale",
         "mix_norm_post", "ffn_norm_pre", "ffn_norm_post")
WEIGHTS = ("mix_norm_pre", "w_in", "q_norm", "w_uq", "kv_norm", "w_uk", "w_uv", "w_attn_o", "conv_w", "conv_b",
           "conv_ln_g", "conv_ln_b", "w_conv_o", "pool_w", "pool_scale", "w_pool_o", "w_mix_o", "mix_norm_post",
           "ffn_norm_pre", "w_gate", "w_up", "w_down", "ffn_norm_post")


def _params(*semantics):
    return pltpu.CompilerParams(dimension_semantics=semantics, vmem_limit_bytes=VMEM_LIMIT_BYTES)


def _tile(dim, cap):
    if dim <= cap:
        return dim
    for t in range(cap - cap % LANES, 0, -LANES):
        if dim % t == 0:
            return t
    raise ValueError(f"no tile for {dim} under {cap}")


def _row_tile(rows, row_bytes, budget=1 << 20):
    if rows * row_bytes <= budget:
        return rows
    cap = max(16, budget // row_bytes)
    for t in range(cap - cap % 16, 0, -16):
        if rows % t == 0:
            return t
    return rows


def _rows(ts, width, cidx=0):
    return pl.BlockSpec((ts, width), lambda i: (i, cidx))


def _fixed(shape):
    return pl.BlockSpec(shape, lambda *_: (0,) * len(shape))


def _sigmoid(x):
    return 1.0 / (1.0 + jnp.exp(-x))


def _matmul(a, b, mode, out_dtype, name, add=None, blocked=False, ride=None):
    nb = n_blk = 0
    blocked = blocked or b.ndim == 3
    if mode == "nn":
        (m, k) = a.shape
        n = b.shape[0] * b.shape[2] if blocked else b.shape[1]
    elif mode == "nt":
        (m, k) = a.shape
        n = b.shape[1] if blocked else b.shape[0]
    else:
        (k, m), n = a.shape, b.shape[1]
    if blocked:
        nb = b.shape[2] if mode != "tn" else n // N_DEV
    unit = nb if blocked and mode != "nt" else LANES
    out_bytes = jnp.dtype(out_dtype).itemsize + (4 if add is not None else 0)
    best = None
    for tn_c in range(unit, min(n, 1536) + 1, unit):
        for tm_c in sorted({256, 512, 1024, 2048, min(m, 2048)}):
            if n % tn_c or m % tm_c or (blocked and mode != "nt" and N_DEV % (tn_c // nb)):
                continue
            vmem = 2 * (tm_c * k * 2 + tn_c * k * 2 + tm_c * tn_c * out_bytes) + tm_c * tn_c * 4 + tn_c * k * 2
            if vmem <= MATMUL_VMEM_BYTES and (best is None or tm_c * tn_c / (tm_c + tn_c) > best[0]):
                best = (tm_c * tn_c / (tm_c + tn_c), tm_c, tn_c)
    if best is None:
        raise ValueError(f"{name}: no tiles for {m}x{n}x{k}")
    _, tm, tn = best
    if blocked:
        n_blk = N_DEV if mode == "nt" else tn // nb
    dims = {"nn": ((1,), (0,)), "nt": ((1,), (1,)), "tn": ((0,), (0,))}[mode]
    a_spec = pl.BlockSpec((k, tm), lambda i, j: (0, i)) if mode == "tn" else pl.BlockSpec((tm, k), lambda i, j: (i, 0))
    b_spec = pl.BlockSpec((tn, k), lambda i, j: (j, 0)) if mode == "nt" else pl.BlockSpec((k, tn), lambda i, j: (0, j))
    o_spec = pl.BlockSpec((tm, tn), lambda i, j: (i, j))
    out_shape = jax.ShapeDtypeStruct((m, n), out_dtype)
    if blocked and mode == "nn":
        b_spec = pl.BlockSpec((n_blk, k, nb), lambda i, j: (j, 0, 0))
    elif blocked and mode == "nt":
        b_spec = pl.BlockSpec((n_blk, tn, nb), lambda i, j: (0, j, 0))
    elif blocked:
        o_spec = pl.BlockSpec((n_blk, tm, nb), lambda i, j: (j, i, 0))
        out_shape = jax.ShapeDtypeStruct((N_DEV, m, nb), out_dtype)
    has_add = add is not None
    grid = (m // tm, n // tn)

    def body(*refs):
        (a_ref, b_ref, *rest), start, finish = _ride_hooks(ride, refs, 3 if has_add else 2, 1, grid)
        start()
        o_ref = rest[-1]
        if blocked and mode != "tn":
            bv = jnp.concatenate([b_ref[c] for c in range(n_blk)], axis=1) if n_blk > 1 else b_ref[0]
        else:
            bv = b_ref[...]
        total = lax.dot_general(a_ref[...], bv, (dims, ((), ())), preferred_element_type=F32)
        if has_add:
            total = total + rest[0][...]
        if blocked and mode == "tn":
            for c in range(n_blk):
                o_ref[c] = total[:, c * nb:(c + 1) * nb].astype(o_ref.dtype)
        else:
            o_ref[...] = total.astype(o_ref.dtype)
        finish()

    operands = (a, b, add) if has_add else (a, b)
    (out,), rode = _ride_call(ride, body, name, (out_shape,), grid, [a_spec, b_spec] + ([o_spec] if has_add else []),
                              (o_spec,), ("parallel", "parallel"), operands)
    return out if ride is None else (out, rode)


def _rms_fwd(x, win, gain, out_dtype, name, res=None, then=None):
    width, cidx = win
    s = x.shape[0]
    ts = min(s, 512)
    has_res, has_then = res is not None, then is not None

    def norm(v, g_ref):
        return (v * lax.rsqrt(jnp.mean(v * v, axis=-1, keepdims=True) + EPS)) * g_ref[...]

    def body(x_ref, g_ref, *rest):
        y = norm(x_ref[...].astype(F32), g_ref)
        if has_res:
            y = rest[0][...] + y
        o_ref = rest[-2] if has_then else rest[-1]
        o_ref[...] = y.astype(o_ref.dtype)
        if has_then:
            rest[-1][...] = norm(y, rest[-3]).astype(BF16)

    ops = (x, gain.reshape(1, width)) + ((res,) if has_res else ()) + ((then.reshape(1, width),) if has_then else ())
    out_shape = (jax.ShapeDtypeStruct((s, width), out_dtype),) + ((jax.ShapeDtypeStruct((s, width), BF16),) * has_then)
    out = pl.pallas_call(
        body, name=name, out_shape=out_shape, grid=(s // ts,),
        in_specs=([_rows(ts, width, cidx), _fixed((1, width))] + ([_rows(ts, width)] if has_res else [])
                  + ([_fixed((1, width))] if has_then else [])),
        out_specs=(_rows(ts, width),) * len(out_shape), compiler_params=_params("parallel"))(*ops)
    return out if has_then else out[0]


def _into(dz, n_inputs, out_index):
    return dict(in_specs=[ANY], operands=(dz,), input_output_aliases={n_inputs: out_index},
                out_shape=jax.ShapeDtypeStruct(dz.shape, dz.dtype))


def _rms_bwd(x, win, gain, dy, out_dtype, name, add=None, dz=None):
    width, cidx = win
    s = x.shape[0]
    ts = min(s, 512)
    has_add = add is not None

    def body(x_ref, g_ref, dy_ref, *rest):
        dx_ref, dg_ref = rest[-2], rest[-1]
        xv = x_ref[...].astype(F32)
        r = lax.rsqrt(jnp.mean(xv * xv, axis=-1, keepdims=True) + EPS)
        xh = xv * r
        dyv = dy_ref[...].astype(F32)
        dyg = dyv * g_ref[...]
        dx = r * (dyg - xh * jnp.mean(dyg * xh, axis=-1, keepdims=True))
        if has_add:
            dx = dx + rest[0][...]
        dx_ref[...] = dx.astype(dx_ref.dtype)

        @pl.when(pl.program_id(0) == 0)
        def _():
            dg_ref[...] = jnp.zeros_like(dg_ref)

        dg_ref[...] += jnp.sum(dyv * xh, axis=0, keepdims=True)

    ops = (x, gain.reshape(1, width), dy) + ((add,) if has_add else ())
    in_specs = [_rows(ts, width, cidx), _fixed((1, width)), _rows(ts, width)] + ([_rows(ts, width)] if has_add else [])
    dx_shape, dx_spec, alias = jax.ShapeDtypeStruct((s, width), out_dtype), _rows(ts, width), {}
    if dz is not None:
        into = _into(dz, len(ops), 0)
        ops, in_specs, alias = ops + into["operands"], in_specs + into["in_specs"], into["input_output_aliases"]
        dx_shape, dx_spec = into["out_shape"], _rows(ts, width, cidx)
    dx, dg = pl.pallas_call(
        body, name=name, out_shape=(dx_shape, jax.ShapeDtypeStruct((1, width), F32)), grid=(s // ts,),
        in_specs=in_specs, out_specs=(dx_spec, _fixed((1, width))), input_output_aliases=alias,
        compiler_params=_params("arbitrary"))(*ops)
    return dx, dg.reshape(width)


def _rope(x, c, s1, s2):
    return x * c + pltpu.roll(x, 16, 1) * s1 + pltpu.roll(x, LANES - 16, 1) * s2


def _rope_t(g, c, s1, s2):
    return g * c + pltpu.roll(g * s1, LANES - 16, 1) + pltpu.roll(g * s2, 16, 1)


def _rope_tables(positions):
    inv_freq = ROPE_THETA ** (-jnp.arange(0, QK_ROPE, 2, dtype=F32) / QK_ROPE)
    ang = positions.astype(F32)[:, None] * inv_freq
    cos, sin = jnp.cos(ang), jnp.sin(ang)
    n = positions.shape[0]
    one, zero = jnp.ones((n, 1), F32), jnp.zeros((n, 1), F32)
    c = jnp.concatenate([jnp.tile(one, (1, QK_NOPE)), cos, cos, jnp.tile(one, (1, 32))], axis=1)
    s1 = jnp.concatenate([jnp.tile(zero, (1, QK_NOPE + 16)), sin, jnp.tile(zero, (1, 32))], axis=1)
    s2 = jnp.concatenate([jnp.tile(zero, (1, QK_NOPE)), -sin, jnp.tile(zero, (1, 48))], axis=1)
    return c, s1, s2


def _qkv_up_fwd(z, q_gain, kv_gain, w_uq, w_uk, w_uv, tables, name):
    s = z.shape[0]
    ts = min(s, 512)
    hw = N_HEADS * HEAD_PAD
    kv0 = Q_RANK + LANES

    def norm(v, g_ref):
        return ((v * lax.rsqrt(jnp.mean(v * v, axis=-1, keepdims=True) + EPS)) * g_ref[...]).astype(BF16)

    def body(z_ref, gq_ref, gkv_ref, wq_ref, wk_ref, wv_ref, c_ref, s1_ref, s2_ref, cq_ref, ckv_ref, q_ref, k_ref, v_ref):
        c, s1, s2 = c_ref[...], s1_ref[...], s2_ref[...]
        cqv = norm(z_ref[:, pl.ds(0, Q_RANK)].astype(F32), gq_ref)
        ckvv = norm(z_ref[:, pl.ds(kv0, KV_RANK)].astype(F32), gkv_ref)
        cq_ref[...] = cqv
        ckv_ref[...] = ckvv
        kr = _rope(z_ref[:, pl.ds(Q_RANK, LANES)].astype(F32), c, s1, s2)
        for h in range(N_HEADS):
            sl = slice(h * HEAD_PAD, (h + 1) * HEAD_PAD)
            q_ref[:, sl] = _rope(jnp.dot(cqv, wq_ref[h], preferred_element_type=F32), c, s1, s2).astype(BF16)
            k_ref[:, sl] = (jnp.dot(ckvv, wk_ref[h], preferred_element_type=F32) + kr).astype(BF16)
            v_ref[:, sl] = jnp.dot(ckvv, wv_ref[h], preferred_element_type=F32).astype(BF16)

    tab = _rows(ts, LANES)
    wide = jax.ShapeDtypeStruct((s, hw), BF16)
    return pl.pallas_call(
        body, name=name,
        out_shape=(jax.ShapeDtypeStruct((s, Q_RANK), BF16), jax.ShapeDtypeStruct((s, KV_RANK), BF16), wide, wide, wide),
        grid=(s // ts,),
        in_specs=[_rows(ts, *ZC_QKR), _fixed((1, Q_RANK)), _fixed((1, KV_RANK)), _fixed(w_uq.shape), _fixed(w_uk.shape),
                  _fixed(w_uv.shape), tab, tab, tab],
        out_specs=(_rows(ts, Q_RANK), _rows(ts, KV_RANK)) + (_rows(ts, hw),) * 3, compiler_params=_params("parallel"))(
            z, q_gain.reshape(1, -1), kv_gain.reshape(1, -1), w_uq, w_uk, w_uv, *tables)


def _qkv_up_bwd(dq, dk, dv, z, w_uq, w_uk, w_uv, tables, q_gain, kv_gain, dz, name):
    s = dq.shape[0]
    ts = min(s, 512)
    hw = N_HEADS * HEAD_PAD
    zw = ZC_QKR[0]
    kv0 = Q_RANK + LANES
    dims_nt = (((1,), (1,)), ((), ()))

    def norm_bwd(xv, g_ref, dyv):
        r = lax.rsqrt(jnp.mean(xv * xv, axis=-1, keepdims=True) + EPS)
        xh = xv * r
        dyg = dyv * g_ref[...]
        return r * (dyg - xh * jnp.mean(dyg * xh, axis=-1, keepdims=True)), jnp.sum(dyv * xh, axis=0, keepdims=True)

    def body(dq_ref, dk_ref, dv_ref, z_ref, wq_ref, wk_ref, wv_ref, c_ref, s1_ref, s2_ref, gq_ref, gkv_ref, _,
             dqf_ref, dkf_ref, dz_ref, dgq_ref, dgkv_ref):
        c, s1, s2 = c_ref[...], s1_ref[...], s2_ref[...]
        ksum = jnp.zeros((ts, HEAD_PAD), F32)
        dcq = jnp.zeros((ts, Q_RANK), F32)
        dckv = jnp.zeros((ts, KV_RANK), F32)
        for h in range(N_HEADS):
            sl = slice(h * HEAD_PAD, (h + 1) * HEAD_PAD)
            dqh = _rope_t(dq_ref[:, sl], c, s1, s2).astype(BF16)
            dkv = dk_ref[:, sl]
            dkh = dkv.astype(BF16)
            dqf_ref[:, sl] = dqh
            dkf_ref[:, sl] = dkh
            ksum = ksum + dkv
            dcq = dcq + lax.dot_general(dqh, wq_ref[h], dims_nt, preferred_element_type=F32)
            dckv = dckv + (lax.dot_general(dkh, wk_ref[h], dims_nt, preferred_element_type=F32)
                           + lax.dot_general(dv_ref[:, sl], wv_ref[h], dims_nt, preferred_element_type=F32))
        dxq, dgq = norm_bwd(z_ref[:, pl.ds(0, Q_RANK)].astype(F32), gq_ref, dcq)
        dxkv, dgkv = norm_bwd(z_ref[:, pl.ds(kv0, KV_RANK)].astype(F32), gkv_ref, dckv)
        lane = lax.broadcasted_iota(jnp.int32, (ts, HEAD_PAD), 1)
        in_rope = (lane >= QK_NOPE) & (lane < QK_NOPE + QK_ROPE)
        dz_ref[:, pl.ds(0, Q_RANK)] = dxq.astype(BF16)
        dz_ref[:, pl.ds(Q_RANK, LANES)] = jnp.where(in_rope, _rope_t(ksum, c, s1, s2), 0.0).astype(BF16)
        dz_ref[:, pl.ds(kv0, KV_RANK)] = dxkv.astype(BF16)

        @pl.when(pl.program_id(0) == 0)
        def _():
            dgq_ref[...] = jnp.zeros_like(dgq_ref)
            dgkv_ref[...] = jnp.zeros_like(dgkv_ref)

        dgq_ref[...] += dgq
        dgkv_ref[...] += dgkv

    tab = _rows(ts, LANES)
    into = _into(dz, 12, 2)
    dqf, dkf, dz, dgq, dgkv = pl.pallas_call(
        body, name=name,
        out_shape=(jax.ShapeDtypeStruct((s, hw), BF16), jax.ShapeDtypeStruct((s, hw), BF16), into["out_shape"],
                   jax.ShapeDtypeStruct((1, Q_RANK), F32), jax.ShapeDtypeStruct((1, KV_RANK), F32)),
        grid=(s // ts,),
        in_specs=[_rows(ts, hw), _rows(ts, hw), _rows(ts, hw), _rows(ts, *ZC_QKR), _fixed(w_uq.shape), _fixed(w_uk.shape),
                  _fixed(w_uv.shape), tab, tab, tab, _fixed((1, Q_RANK)), _fixed((1, KV_RANK))] + into["in_specs"],
        out_specs=(_rows(ts, hw), _rows(ts, hw), _rows(ts, *ZC_QKR), _fixed((1, Q_RANK)), _fixed((1, KV_RANK))),
        input_output_aliases=into["input_output_aliases"], compiler_params=_params("arbitrary"))(
            dq, dk, dv, z, w_uq, w_uk, w_uv, *tables, q_gain.reshape(1, -1), kv_gain.reshape(1, -1), dz)
    return dqf, dkf, dz, dgq.reshape(-1), dgkv.reshape(-1)


def _attn_tile(s):
    return min(s, 512)


def _raw_scores(q, k, masked, row0=0):
    sc = lax.dot_general(q, k, (((1,), (1,)), ((), ())), preferred_element_type=F32)
    if masked:
        rows = row0 + lax.broadcasted_iota(jnp.int32, sc.shape, 0)
        cols = lax.broadcasted_iota(jnp.int32, sc.shape, 1)
        sc = jnp.where(cols <= rows, sc, -jnp.inf)
    return sc


def _ride_hooks(ride, refs, n_in, n_out, grid):
    if ride is None:
        return refs, lambda: None, lambda: None
    n = len(ride.arrays)
    own = refs[:n_in] + refs[n_in + n:n_in + n + n_out]
    ins, outs, sems = refs[n_in:n_in + n], refs[n_in + n + n_out:n_in + 2 * n + n_out], refs[n_in + 2 * n + n_out:]
    at_first = functools.reduce(lambda a, b: a & b, [pl.program_id(ax) == 0 for ax in range(len(grid))])
    at_last = functools.reduce(lambda a, b: a & b, [pl.program_id(ax) == g - 1 for ax, g in enumerate(grid)])
    return own, lambda: pl.when(at_first)(lambda: ride.start(ins, outs, sems)), \
        lambda: pl.when(at_last)(lambda: ride.finish(ins, outs, sems))


def _ride_call(ride, body, name, out_shape, grid, in_specs, out_specs, semantics, operands):
    n = 0 if ride is None else len(ride.arrays)
    res = pl.pallas_call(
        body, name=name, out_shape=tuple(out_shape) + (tuple(ride.out_shape) if n else ()), grid=grid,
        in_specs=list(in_specs) + [ANY] * n, out_specs=tuple(out_specs) + (ANY,) * n,
        scratch_shapes=list(ride.scratch) if n else [],
        compiler_params=_params(*(("arbitrary",) * len(grid) if n else semantics)))(*operands, *(ride.arrays if n else ()))
    return res[:len(out_shape)], list(res[len(out_shape):])


def _flash_fwd(q, k, v, name, ride=None):
    s = q.shape[0]
    t = _attn_tile(s)
    c2 = ATTN_SCALE * LOG2E
    grid = (N_HEADS, s // t)

    def body(*refs):
        (q_ref, k_ref, v_ref, o_ref, lse_ref), start, finish = _ride_hooks(ride, refs, 3, 2, grid)
        start()
        i = pl.program_id(1)
        qv = q_ref[...]

        def chunk(j, carry, masked):
            m_old, l_old, acc = carry
            at = pl.ds(pl.multiple_of(j * t, t), t)
            sc = _raw_scores(qv, k_ref[at, :], masked)
            m_new = jnp.maximum(m_old, jnp.max(sc, axis=-1, keepdims=True))
            p = jnp.exp2((sc - m_new) * c2)
            alpha = jnp.exp2((m_old - m_new) * c2)
            l_new = alpha * l_old + jnp.sum(p, axis=-1, keepdims=True)
            acc = alpha * acc + jnp.dot(p.astype(BF16), v_ref[at, :], preferred_element_type=F32)
            return m_new, l_new, acc

        init = (jnp.full((t, 1), -jnp.inf, F32), jnp.zeros((t, 1), F32), jnp.zeros((t, HEAD_PAD), F32))
        carry = lax.fori_loop(0, i, lambda j, cr: chunk(j, cr, False), init)
        m_fin, l_fin, acc = chunk(i, carry, True)
        o_ref[...] = (acc / l_fin).astype(o_ref.dtype)
        lse_ref[...] = jnp.broadcast_to(m_fin * ATTN_SCALE + jnp.log(l_fin), (t, HEAD_PAD))
        finish()

    qo = pl.BlockSpec((t, HEAD_PAD), lambda h, i: (i, h))
    whole = pl.BlockSpec((s, HEAD_PAD), lambda h, i: (0, h))
    return _ride_call(
        ride, body, name, (jax.ShapeDtypeStruct(q.shape, BF16), jax.ShapeDtypeStruct(q.shape, F32)), grid,
        [qo, whole, whole], (qo, qo), ("parallel", "parallel"), (q, k, v))


def _attn_out_bwd(dya, w_attn_o, o, name):
    s, d = dya.shape
    hw = N_HEADS * HEAD_PAD
    t = _attn_tile(s)

    def body(d_ref, w_ref, o_ref, delta_ref, dob_ref):
        wv = jnp.concatenate([w_ref[c] for c in range(N_DEV)], axis=1)
        do = lax.dot_general(d_ref[...], wv, (((1,), (1,)), ((), ())), preferred_element_type=F32)
        for h in range(N_HEADS):
            sl = slice(h * HEAD_PAD, (h + 1) * HEAD_PAD)
            dov = do[:, sl]
            delta_ref[:, sl] = jnp.broadcast_to(jnp.sum(dov * o_ref[:, sl].astype(F32), axis=-1, keepdims=True),
                                                (t, HEAD_PAD))
            dob_ref[:, sl] = dov.astype(BF16)

    blk = _rows(t, hw)
    return pl.pallas_call(
        body, name=name, out_shape=(jax.ShapeDtypeStruct(o.shape, F32), jax.ShapeDtypeStruct(o.shape, BF16)),
        grid=(s // t,), in_specs=[_rows(t, d), _fixed(w_attn_o.shape), blk], out_specs=(blk, blk),
        compiler_params=_params("parallel"))(dya, w_attn_o, o)


def _flash_bwd(q, k, v, do, lse, delta, name, ride=None):
    s = q.shape[0]
    t = _attn_tile(s)
    nt = s // t
    c2 = ATTN_SCALE * LOG2E
    grid = (N_HEADS, nt)

    def body(*refs):
        (q_ref, k_ref, v_ref, do_ref, lse_ref, delta_ref, dq_ref, dk_ref, dv_ref), start, finish = _ride_hooks(
            ride, refs, 6, 3, grid)
        start()
        j = pl.program_id(1)
        kv, vv = k_ref[...], v_ref[...]

        @pl.when(j == 0)
        def _():
            dq_ref[...] = jnp.zeros_like(dq_ref)

        def chunk(i, carry, masked):
            dk_acc, dv_acc = carry
            at = pl.ds(pl.multiple_of(i * t, t), t)
            qi, doi = q_ref[at, :], do_ref[at, :]
            sc = _raw_scores(qi, kv, masked)
            p = jnp.exp2(sc * c2 - lse_ref[at, pl.ds(0, 1)] * LOG2E)
            dp = lax.dot_general(doi, vv, (((1,), (1,)), ((), ())), preferred_element_type=F32)
            ds = (p * (dp - delta_ref[at, pl.ds(0, 1)])).astype(BF16)
            dv_acc = dv_acc + lax.dot_general(p.astype(BF16), doi, (((0,), (0,)), ((), ())), preferred_element_type=F32)
            dk_acc = dk_acc + lax.dot_general(ds, qi, (((0,), (0,)), ((), ())), preferred_element_type=F32)
            dq_ref[at, :] += jnp.dot(ds, kv, preferred_element_type=F32) * ATTN_SCALE
            return dk_acc, dv_acc

        zero = jnp.zeros((t, HEAD_PAD), F32)
        carry = chunk(j, (zero, zero), True)
        dk_acc, dv_acc = lax.fori_loop(j + 1, nt, lambda i, cr: chunk(i, cr, False), carry)
        dk_ref[...] = dk_acc * ATTN_SCALE
        dv_ref[...] = dv_acc.astype(BF16)
        finish()

    blk = pl.BlockSpec((t, HEAD_PAD), lambda h, j: (j, h))
    whole = pl.BlockSpec((s, HEAD_PAD), lambda h, j: (0, h))
    return _ride_call(
        ride, body, name, (jax.ShapeDtypeStruct(q.shape, F32), jax.ShapeDtypeStruct(q.shape, F32),
                           jax.ShapeDtypeStruct(q.shape, BF16)), grid,
        [whole, blk, blk, whole, whole, whole], (whole, blk, blk), ("parallel", "arbitrary"), (q, k, v, do, lse, delta))


def _conv_tile(s):
    return min(s, 256)


def _halo_before(t, width, cidx):
    per = t // CONV_HALO
    return pl.BlockSpec((CONV_HALO, width), lambda i: (jnp.maximum(i * per - 1, 0), cidx))


def _halo_after(t, width, cidx, n_tiles):
    per = t // CONV_HALO
    last = n_tiles * per - 1
    return pl.BlockSpec((CONV_HALO, width), lambda i: (jnp.minimum((i + 1) * per, last), cidx))


def _fill_glu(hbuf, ap_ref, gp_ref, a_ref, g_ref, t):
    first = pl.program_id(0) == 0
    hbuf[pl.ds(0, CONV_HALO), :] = jnp.where(first, 0.0, ap_ref[...].astype(F32) * _sigmoid(gp_ref[...].astype(F32)))
    hbuf[pl.ds(CONV_HALO, t), :] = a_ref[...].astype(F32) * _sigmoid(g_ref[...].astype(F32))


def _phase_copies(dst, src, t):
    n = t + CONV_HALO - SUBLANES
    for s in range(1, SUBLANES):
        dst[s, pl.ds(0, n), :] = src[pl.ds(s, n), :]


def _window(phases, src, k, t):
    if k % SUBLANES == 0:
        return src[pl.ds(k, t), :]
    return phases[k % SUBLANES, pl.ds(k - k % SUBLANES, t), :]


def _layer_norm_parts(co):
    mu = jnp.mean(co, axis=-1, keepdims=True)
    xc = co - mu
    rstd = lax.rsqrt(jnp.mean(xc * xc, axis=-1, keepdims=True) + EPS)
    return xc * rstd, rstd


def _conv_fwd(z, conv_w, conv_b, ln_g, ln_b, name):
    s = z.shape[0]
    t = _conv_tile(s)
    off = CONV_HALO - (CONV_W - 1)

    def body(ap_ref, gp_ref, a_ref, g_ref, w_ref, b_ref, lg_ref, lb_ref, hc_ref, co_ref, hbuf, hph):
        _fill_glu(hbuf, ap_ref, gp_ref, a_ref, g_ref, t)
        _phase_copies(hph, hbuf, t)
        acc = jnp.zeros((t, CONV_C), F32) + b_ref[...]
        for j in range(CONV_W):
            acc = acc + _window(hph, hbuf, off + j, t) * w_ref[pl.ds(j, 1), :]
        co_ref[...] = acc
        xh, _ = _layer_norm_parts(acc)
        y = xh * lg_ref[...] + lb_ref[...]
        hc_ref[...] = (y * _sigmoid(y)).astype(BF16)

    vec = _fixed((1, CONV_C))
    return pl.pallas_call(
        body, name=name, out_shape=(jax.ShapeDtypeStruct((s, CONV_C), BF16), jax.ShapeDtypeStruct((s, CONV_C), F32)),
        grid=(s // t,),
        in_specs=[_halo_before(t, *ZC_CONV_A), _halo_before(t, *ZC_CONV_G), _rows(t, *ZC_CONV_A), _rows(t, *ZC_CONV_G),
                  _fixed((CONV_HALO, CONV_C)), vec, vec, vec],
        out_specs=(_rows(t, CONV_C), _rows(t, CONV_C)),
        scratch_shapes=[pltpu.VMEM((t + CONV_HALO, CONV_C), F32), pltpu.VMEM((SUBLANES, t + CONV_HALO, CONV_C), F32)],
        compiler_params=_params("parallel"))(z, z, z, z, conv_w, conv_b.reshape(1, -1), ln_g.reshape(1, -1),
                                             ln_b.reshape(1, -1))


def _conv_bwd_norm(dhc, co, ln_g, ln_b, name):
    s = co.shape[0]
    t = min(s, 512)

    def body(dhc_ref, co_ref, lg_ref, lb_ref, dco_ref, dg_ref, db_ref, dcb_ref):
        xh, rstd = _layer_norm_parts(co_ref[...])
        y = xh * lg_ref[...] + lb_ref[...]
        sg = _sigmoid(y)
        dy = dhc_ref[...] * (sg * (1.0 + y * (1.0 - sg)))
        dxh = dy * lg_ref[...]
        dco = rstd * (dxh - jnp.mean(dxh, axis=-1, keepdims=True) - xh * jnp.mean(dxh * xh, axis=-1, keepdims=True))
        dco_ref[...] = dco

        @pl.when(pl.program_id(0) == 0)
        def _():
            dg_ref[...] = jnp.zeros_like(dg_ref)
            db_ref[...] = jnp.zeros_like(db_ref)
            dcb_ref[...] = jnp.zeros_like(dcb_ref)

        dg_ref[...] += jnp.sum(dy * xh, axis=0, keepdims=True)
        db_ref[...] += jnp.sum(dy, axis=0, keepdims=True)
        dcb_ref[...] += jnp.sum(dco, axis=0, keepdims=True)

    vec = _fixed((1, CONV_C))
    one = jax.ShapeDtypeStruct((1, CONV_C), F32)
    dco, dg, db, dcb = pl.pallas_call(
        body, name=name, out_shape=(jax.ShapeDtypeStruct((s, CONV_C), F32), one, one, one), grid=(s // t,),
        in_specs=[_rows(t, CONV_C), _rows(t, CONV_C), vec, vec], out_specs=(_rows(t, CONV_C), vec, vec, vec),
        compiler_params=_params("arbitrary"))(dhc, co, ln_g.reshape(1, -1), ln_b.reshape(1, -1))
    return dco, dg.reshape(-1), db.reshape(-1), dcb.reshape(-1)


def _conv_bwd_taps(dco, z, conv_w, dz, name):
    s = z.shape[0]
    t = _conv_tile(s)
    nt = s // t
    off = CONV_HALO - (CONV_W - 1)

    def body(ap_ref, gp_ref, a_ref, g_ref, d_ref, dn_ref, w_ref, _, du_ref, dw_ref, hbuf, dbuf, hph, dph):
        i = pl.program_id(0)
        _fill_glu(hbuf, ap_ref, gp_ref, a_ref, g_ref, t)
        dbuf[pl.ds(0, t), :] = d_ref[...]
        dbuf[pl.ds(t, CONV_HALO), :] = jnp.where(i == nt - 1, 0.0, dn_ref[...])
        _phase_copies(hph, hbuf, t)
        _phase_copies(dph, dbuf, t)

        @pl.when(i == 0)
        def _():
            dw_ref[...] = jnp.zeros_like(dw_ref)

        dcur = d_ref[...]
        dh = jnp.zeros((t, CONV_C), F32)
        for j in range(CONV_W):
            dh = dh + _window(dph, dbuf, CONV_W - 1 - j, t) * w_ref[pl.ds(j, 1), :]
            dw_ref[pl.ds(j, 1), :] += jnp.sum(dcur * _window(hph, hbuf, off + j, t), axis=0, keepdims=True)
        a, sg = a_ref[...].astype(F32), _sigmoid(g_ref[...].astype(F32))
        du_ref[:, pl.ds(0, CONV_C)] = (dh * sg).astype(BF16)
        du_ref[:, pl.ds(CONV_C, CONV_C)] = (dh * a * sg * (1.0 - sg)).astype(BF16)

    into = _into(dz, 7, 0)
    return pl.pallas_call(
        body, name=name, out_shape=(into["out_shape"], jax.ShapeDtypeStruct((CONV_HALO, CONV_C), F32)), grid=(nt,),
        in_specs=[_halo_before(t, *ZC_CONV_A), _halo_before(t, *ZC_CONV_G), _rows(t, *ZC_CONV_A), _rows(t, *ZC_CONV_G),
                  _rows(t, CONV_C), _halo_after(t, CONV_C, 0, nt), _fixed((CONV_HALO, CONV_C))] + into["in_specs"],
        out_specs=(_rows(t, *ZC_CONV), _fixed((CONV_HALO, CONV_C))), input_output_aliases=into["input_output_aliases"],
        scratch_shapes=[pltpu.VMEM((t + CONV_HALO, CONV_C), F32), pltpu.VMEM((t + CONV_HALO, CONV_C), F32),
                        pltpu.VMEM((SUBLANES, t + CONV_HALO, CONV_C), F32),
                        pltpu.VMEM((SUBLANES, t + CONV_HALO, CONV_C), F32)],
        compiler_params=_params("arbitrary"))(z, z, z, z, dco, dco, conv_w, dz)


def _pool_tile(s):
    return min(s, 512)


def _pool_counts(row0, n, window):
    rows = row0 + lax.broadcasted_iota(jnp.int32, (n, POOL_GD), 0)
    return jnp.minimum(rows + 1, window).astype(F32)


def _pool_diff(ubuf, gi, window, row0, t):
    lanes = pl.ds(gi * POOL_GD, POOL_GD)
    tot = ubuf[pl.ds(CONV_HALO, t), lanes]
    cur = tot
    for back in range(1, window):
        tot = tot + ubuf[pl.ds(CONV_HALO - back, t), lanes]
    return tot / _pool_counts(row0, t, window) - cur


def _pool_fwd(z, pool_w, pool_scale, name):
    s = z.shape[0]
    t = _pool_tile(s)

    def body(up_ref, u_ref, w_ref, sc_ref, m_ref, ubuf):
        i = pl.program_id(0)
        ubuf[pl.ds(0, CONV_HALO), :] = jnp.where(i == 0, 0.0, up_ref[...].astype(F32))
        ubuf[pl.ds(CONV_HALO, t), :] = u_ref[...].astype(F32)
        for gi, window in enumerate(POOL_WINDOWS):
            d = _pool_diff(ubuf, gi, window, i * t, t)
            mm = jnp.dot(d.astype(BF16), w_ref[gi].astype(BF16), preferred_element_type=F32)
            lanes = pl.ds(gi * POOL_GD, POOL_GD)
            m_ref[:, lanes] = (mm * sc_ref[:, lanes]).astype(BF16)

    return pl.pallas_call(
        body, name=name, out_shape=jax.ShapeDtypeStruct((s, POOL_C), BF16), grid=(s // t,),
        in_specs=[_halo_before(t, *ZC_POOL), _rows(t, *ZC_POOL), _fixed((POOL_G, POOL_GD, POOL_GD)), _fixed((1, POOL_C))],
        out_specs=_rows(t, POOL_C), scratch_shapes=[pltpu.VMEM((t + CONV_HALO, POOL_C), F32)],
        compiler_params=_params("parallel"))(z, z, pool_w, pool_scale.reshape(1, -1))


def _pool_bwd(dm, z, pool_w, pool_scale, dz, name):
    s = z.shape[0]
    t = _pool_tile(s)
    nt = s // t

    def body(up_ref, u_ref, dm_ref, dmn_ref, w_ref, sc_ref, _, du_ref, dw_ref, dsc_ref, ubuf, ebuf):
        i = pl.program_id(0)
        ubuf[pl.ds(0, CONV_HALO), :] = jnp.where(i == 0, 0.0, up_ref[...].astype(F32))
        ubuf[pl.ds(CONV_HALO, t), :] = u_ref[...].astype(F32)

        @pl.when(i == 0)
        def _():
            dw_ref[...] = jnp.zeros_like(dw_ref)
            dsc_ref[...] = jnp.zeros_like(dsc_ref)

        dm_next = jnp.where(i == nt - 1, 0.0, dmn_ref[...])
        for gi, window in enumerate(POOL_WINDOWS):
            lanes = pl.ds(gi * POOL_GD, POOL_GD)
            wb = w_ref[gi].astype(BF16)
            scale = sc_ref[:, lanes]
            d = _pool_diff(ubuf, gi, window, i * t, t).astype(BF16)
            mm = jnp.dot(d, wb, preferred_element_type=F32)
            dmv = dm_ref[:, lanes]
            dsc_ref[:, lanes] += jnp.sum(dmv * mm, axis=0, keepdims=True)
            dmm = (dmv * scale).astype(BF16)
            dw_ref[gi] += lax.dot_general(d, dmm, (((0,), (0,)), ((), ())), preferred_element_type=F32)
            dd = lax.dot_general(dmm, wb, (((1,), (1,)), ((), ())), preferred_element_type=F32)
            dd_next = lax.dot_general((dm_next[:, gi * POOL_GD:(gi + 1) * POOL_GD] * scale).astype(BF16), wb,
                                      (((1,), (1,)), ((), ())), preferred_element_type=F32)
            ebuf[pl.ds(0, t), lanes] = dd / _pool_counts(i * t, t, window)
            ebuf[pl.ds(t, CONV_HALO), lanes] = dd_next / _pool_counts((i + 1) * t, CONV_HALO, window)
            du = -dd
            for ahead in range(window):
                du = du + ebuf[pl.ds(ahead, t), lanes]
            du_ref[:, lanes] = du.astype(BF16)

    into = _into(dz, 6, 0)
    du, dw, dsc = pl.pallas_call(
        body, name=name,
        out_shape=(into["out_shape"], jax.ShapeDtypeStruct((POOL_G, POOL_GD, POOL_GD), F32),
                   jax.ShapeDtypeStruct((1, POOL_C), F32)), grid=(nt,),
        in_specs=[_halo_before(t, *ZC_POOL), _rows(t, *ZC_POOL), _rows(t, POOL_C), _halo_after(t, POOL_C, 0, nt),
                  _fixed((POOL_G, POOL_GD, POOL_GD)), _fixed((1, POOL_C))] + into["in_specs"],
        out_specs=(_rows(t, *ZC_POOL), _fixed((POOL_G, POOL_GD, POOL_GD)), _fixed((1, POOL_C))),
        input_output_aliases=into["input_output_aliases"],
        scratch_shapes=[pltpu.VMEM((t + CONV_HALO, POOL_C), F32), pltpu.VMEM((t + CONV_HALO, POOL_C), F32)],
        compiler_params=_params("arbitrary"))(z, z, dm, dm, pool_w, pool_scale.reshape(1, -1), dz)
    return du, dw, dsc.reshape(-1)


def _gate_specs(ts):
    width, first = ZC_GATE
    return [_rows(ts, width, first + b) for b in range(3)]


def _branches_merge_fwd(z, acts, ws, name):
    s = z.shape[0]
    ts = min(s, 512)

    def body(g0, g1, g2, a0, a1, a2, w0, w1, w2, y0, y1, y2, m_ref):
        merged = jnp.zeros((ts, D_MODEL), F32)
        for g_ref, a_ref, w_ref, y_ref in ((g0, a0, w0, y0), (g1, a1, w1, y1), (g2, a2, w2, y2)):
            wv = jnp.concatenate([w_ref[c] for c in range(N_DEV)], axis=1)
            yb = jnp.dot(a_ref[...], wv, preferred_element_type=F32).astype(BF16)
            y_ref[...] = yb
            merged = merged + _sigmoid(g_ref[...].astype(F32)) * yb.astype(F32)
        m_ref[...] = merged.astype(BF16)

    out = jax.ShapeDtypeStruct((s, D_MODEL), BF16)
    res = pl.pallas_call(
        body, name=name, out_shape=(out,) * 4, grid=(s // ts,),
        in_specs=_gate_specs(ts) + [_rows(ts, a.shape[1]) for a in acts] + [_fixed(w.shape) for w in ws],
        out_specs=(_rows(ts, D_MODEL),) * 4, compiler_params=_params("parallel"))(z, z, z, *acts, *ws)
    return tuple(res[:3]), res[3]


def _merge_bwd(z, ys, dmerged, name):
    s = z.shape[0]
    ts = min(s, 256)

    def body(g0, g1, g2, y0, y1, y2, dm_ref, dy0, dy1, dy2, dz_ref):
        dmv = dm_ref[...]
        for b, (g_ref, y_ref, dy_ref) in enumerate(((g0, y0, dy0), (g1, y1, dy1), (g2, y2, dy2))):
            sg = _sigmoid(g_ref[...].astype(F32))
            dy_ref[...] = (dmv * sg).astype(BF16)
            dz_ref[:, pl.ds(b * D_MODEL, D_MODEL)] = (dmv * y_ref[...].astype(F32) * sg * (1.0 - sg)).astype(BF16)

    out = jax.ShapeDtypeStruct((s, D_MODEL), BF16)
    return pl.pallas_call(
        body, name=name, out_shape=(out,) * 3 + (jax.ShapeDtypeStruct((s, Z_W), BF16),), grid=(s // ts,),
        in_specs=_gate_specs(ts) + [_rows(ts, D_MODEL)] * 4,
        out_specs=(_rows(ts, D_MODEL),) * 3 + (_rows(ts, *ZC_GATES),),
        compiler_params=_params("parallel"))(z, z, z, *ys, dmerged)


def _ffn_up_fwd(h, w_gate, w_up, name):
    s, d = h.shape
    nb = w_gate.shape[2]
    f = N_DEV * nb
    tm, n_blk = min(s, 1024), 2
    tn = n_blk * nb
    blk = pl.BlockSpec((tm, tn), lambda i, j: (i, j))
    wspec = pl.BlockSpec((n_blk, d, nb), lambda i, j: (j, 0, 0))

    def body(h_ref, wg_ref, wu_ref, hg_ref, hu_ref, act_ref):
        hv = h_ref[...]
        g = jnp.dot(hv, jnp.concatenate([wg_ref[c] for c in range(n_blk)], axis=1), preferred_element_type=F32)
        u = jnp.dot(hv, jnp.concatenate([wu_ref[c] for c in range(n_blk)], axis=1), preferred_element_type=F32)
        hg_ref[...] = g.astype(hg_ref.dtype)
        hu_ref[...] = u.astype(hu_ref.dtype)
        act_ref[...] = (g * _sigmoid(g) * u).astype(BF16)

    return pl.pallas_call(
        body, name=name,
        out_shape=(jax.ShapeDtypeStruct((s, f), BF16),) * 3,
        grid=(s // tm, f // tn), in_specs=[pl.BlockSpec((tm, d), lambda i, j: (i, 0)), wspec, wspec],
        out_specs=(blk, blk, blk), compiler_params=_params("parallel", "parallel"))(h, w_gate, w_up)


def _ffn_down_bwd(dfo, w_down, hg, hu, name):
    s, d = dfo.shape
    f = w_down.shape[0]
    tm, tn = min(s, 1024), _tile(f, 1024)
    blk = pl.BlockSpec((tm, tn), lambda i, j: (i, j))

    def body(d_ref, w_ref, g_ref, u_ref, dg_ref, du_ref):
        dact = lax.dot_general(d_ref[...], w_ref[...], (((1,), (1,)), ((), ())), preferred_element_type=F32)
        g = g_ref[...].astype(F32)
        sg = _sigmoid(g)
        dg_ref[...] = (dact * u_ref[...].astype(F32) * (sg * (1.0 + g * (1.0 - sg)))).astype(BF16)
        du_ref[...] = (dact * g * sg).astype(BF16)

    out = jax.ShapeDtypeStruct((s, f), BF16)
    return pl.pallas_call(
        body, name=name, out_shape=(out, out), grid=(s // tm, f // tn),
        in_specs=[pl.BlockSpec((tm, d), lambda i, j: (i, 0)), pl.BlockSpec((tn, d), lambda i, j: (j, 0)), blk, blk],
        out_specs=(blk, blk), compiler_params=_params("parallel", "parallel"))(dfo, w_down, hg, hu)


def _loss_grad(y, target, name):
    s, d = y.shape
    ts = min(s, 512)

    def body(y_ref, t_ref, dy_ref, sq_ref):
        e = y_ref[...] - t_ref[...]
        dy_ref[...] = e / d

        @pl.when(pl.program_id(0) == 0)
        def _():
            sq_ref[...] = jnp.zeros_like(sq_ref)

        sq_ref[...] += jnp.sum(e * e, axis=0, keepdims=True)

    return pl.pallas_call(
        body, name=name, out_shape=(jax.ShapeDtypeStruct((s, d), F32), jax.ShapeDtypeStruct((1, d), F32)),
        grid=(s // ts,), in_specs=[_rows(ts, d), _rows(ts, d)], out_specs=(_rows(ts, d), _fixed((1, d))),
        compiler_params=_params("arbitrary"))(y, target)


def _adamw(w, g, m, v, name):
    shape = w.shape
    cols = shape[-1]
    keep3 = w.ndim == 3 and shape[1] < SUBLANES
    view = shape if keep3 else (math.prod(shape[:-1]), cols)
    rows = view[0]
    if keep3:
        cap = max(1, (1 << 20) // (SUBLANES * cols * 4))
        tr = max(t for t in range(1, cap + 1) if rows % t == 0)
    else:
        tr = _row_tile(rows, cols * 4)

    def body(w_ref, g_ref, m_ref, v_ref, d_ref, mo_ref, vo_ref):
        gv = g_ref[...]
        mn = B1 * m_ref[...] + (1.0 - B1) * gv
        vn = B2 * v_ref[...] + (1.0 - B2) * (gv * gv)
        m_hat = mn / (1.0 - B1 ** STEP)
        v_hat = vn / (1.0 - B2 ** STEP)
        d_ref[...] = -LR * (m_hat / (jnp.sqrt(v_hat) + ADAM_EPS) + WD * w_ref[...])
        mo_ref[...] = mn
        vo_ref[...] = vn

    spec = pl.BlockSpec((tr,) + view[1:], lambda i: (i,) + (0,) * (len(view) - 1))
    out = jax.ShapeDtypeStruct(view, F32)
    res = pl.pallas_call(
        body, name=name, out_shape=(out,) * 3, grid=(rows // tr,), in_specs=[spec] * 4, out_specs=(spec,) * 3,
        compiler_params=_params("parallel"))(*[t.reshape(view) for t in (w, g, m, v)])
    return tuple(r.reshape(shape) for r in res)


LANE_MAJOR = ("w_uq", "w_uk", "w_uv", "w_gate", "w_up")


def _lane_major(name, a):
    if name == "w_in":
        return a.transpose(2, 0, 1)
    if name in LANE_MAJOR:
        return a.transpose(0, 2, 1)
    return a


def _from_lane_major(name, a):
    if name == "w_in":
        return a.transpose(1, 2, 0)
    return _lane_major(name, a)


ANY = pl.BlockSpec(memory_space=pl.ANY)


class _GatherRide:
    def __init__(self, arrays):
        n = len(arrays)
        self.arrays = list(arrays)
        self.out_shape = [jax.ShapeDtypeStruct((N_DEV,) + a.shape, a.dtype) for a in arrays]
        self.scratch = [pltpu.SemaphoreType.DMA((n, 7)), pltpu.SemaphoreType.DMA((n, 7)), pltpu.SemaphoreType.DMA((n,))]

    def _copies(self, ins, outs, sems):
        send_sems, recv_sems, local_sems = sems
        n = len(self.arrays)
        x, y, c = lax.axis_index("x"), lax.axis_index("y"), lax.axis_index("c")
        me, sibling = (x, y, c), (x, y, 1 - c)
        chips = [(1 - x, y), (x, 1 - y), (1 - x, 1 - y)]

        def slot(a, px, py, pc):
            return outs[a].at[4 * px + 2 * py + pc]

        def copy(a, k, block, to, src=None):
            return pltpu.make_async_remote_copy(
                src_ref=slot(a, *block) if src is None else src, dst_ref=slot(a, *block), send_sem=send_sems.at[a, k],
                recv_sem=recv_sems.at[a, k], device_id=to, device_id_type=MESH)

        mine = [pltpu.make_async_copy(ins[a], slot(a, *me), local_sems.at[a]) for a in range(n)]
        first = []
        for a in range(n):
            first.append(copy(a, 0, me, sibling, src=ins[a]))
            first += [copy(a, 1 + j, me, (*chip, c), src=ins[a]) for j, chip in enumerate(chips)]
        return n, me, sibling, chips, c, copy, mine, first

    def start(self, ins, outs, sems):
        _, _, _, _, _, _, mine, first = self._copies(ins, outs, sems)
        for cp in mine + first:
            cp.start()

    def finish(self, ins, outs, sems):
        n, me, sibling, chips, c, copy, mine, first = self._copies(ins, outs, sems)
        passed = []
        for j, chip in enumerate(chips):
            for a in range(n):
                copy(a, 1 + j, (*chip, c), me).wait_recv()
                passed.append(copy(a, 4 + j, (*chip, c), sibling))
                passed[-1].start()
        for a in range(n):
            copy(a, 0, sibling, me).wait_recv()
            for j, chip in enumerate(chips):
                copy(a, 4 + j, (*chip, 1 - c), me).wait_recv()
        for cp in first + passed:
            cp.wait_send()
        for cp in mine:
            cp.wait()


class _ReduceRide:
    def __init__(self, arrays):
        n = len(arrays)
        self.arrays = list(arrays)
        self.out_shape = [jax.ShapeDtypeStruct(a.shape, a.dtype) for a in arrays]
        self.scratch = [pltpu.SemaphoreType.DMA((n, 7)), pltpu.SemaphoreType.DMA((n, 7)), pltpu.SemaphoreType.DMA((n,))]

    def _copies(self, ins, outs, sems):
        send_sems, recv_sems, local_sems = sems
        n = len(self.arrays)
        x, y, c = lax.axis_index("x"), lax.axis_index("y"), lax.axis_index("c")
        mine = [pltpu.make_async_copy(ins[a].at[4 * x + 2 * y + c], outs[a].at[0], local_sems.at[a]) for a in range(n)]
        copies = []
        for a in range(n):
            for k in range(1, N_DEV):
                px = 1 - x if k & 4 else x
                py = 1 - y if k & 2 else y
                pc = 1 - c if k & 1 else c
                copies.append(pltpu.make_async_remote_copy(
                    src_ref=ins[a].at[4 * px + 2 * py + pc], dst_ref=outs[a].at[k], send_sem=send_sems.at[a, k - 1],
                    recv_sem=recv_sems.at[a, k - 1], device_id=(px, py, pc), device_id_type=MESH))
        return mine, copies

    def start(self, ins, outs, sems):
        mine, copies = self._copies(ins, outs, sems)
        for cp in mine + copies:
            cp.start()

    def finish(self, ins, outs, sems):
        mine, copies = self._copies(ins, outs, sems)
        for cp in copies + mine:
            cp.wait()


def _run_ride(ride, name):
    n = len(ride.arrays)

    def body(*refs):
        ins, outs, sems = refs[:n], refs[n:2 * n], refs[2 * n:]
        ride.start(ins, outs, sems)
        ride.finish(ins, outs, sems)

    return pl.pallas_call(body, name=name, out_shape=ride.out_shape, in_specs=[ANY] * n, out_specs=[ANY] * n,
                          scratch_shapes=ride.scratch)(*ride.arrays)


def _all_gather(arrays, name):
    return _run_ride(_GatherRide(arrays), name)


def _swap_with_sibling(arrays, name):
    n = len(arrays)

    def body(*refs):
        ins, outs = refs[:n], refs[n:2 * n]
        send_sems, recv_sems = refs[2 * n:]
        x, y, c = lax.axis_index("x"), lax.axis_index("y"), lax.axis_index("c")
        copies = [pltpu.make_async_remote_copy(
            src_ref=ins[a].at[1 - c], dst_ref=outs[a], send_sem=send_sems.at[a], recv_sem=recv_sems.at[a],
            device_id=(x, y, 1 - c), device_id_type=MESH) for a in range(n)]
        for cp in copies:
            cp.start()
        for cp in copies:
            cp.wait()

    return pl.pallas_call(
        body, name=name, out_shape=[jax.ShapeDtypeStruct(a.shape[1:], a.dtype) for a in arrays],
        in_specs=[ANY] * n, out_specs=[ANY] * n,
        scratch_shapes=[pltpu.SemaphoreType.DMA((n,)), pltpu.SemaphoreType.DMA((n,))])(*arrays)


class _ChipExchangeRide:
    def __init__(self, arrays):
        n = len(arrays)
        self.arrays = list(arrays)
        self.out_shape = [jax.ShapeDtypeStruct(a.shape, a.dtype) for a in arrays]
        self.scratch = [pltpu.SemaphoreType.DMA((n, 3)), pltpu.SemaphoreType.DMA((n, 3)), pltpu.SemaphoreType.DMA((n,))]

    def _copies(self, ins, outs, sems):
        send_sems, recv_sems, local_sems = sems
        n = len(self.arrays)
        x, y, c = lax.axis_index("x"), lax.axis_index("y"), lax.axis_index("c")
        partners = [(x, 1 - y), (1 - x, y), (1 - x, 1 - y)]
        mine = [pltpu.make_async_copy(ins[a].at[2 * x + y], outs[a].at[0], local_sems.at[a]) for a in range(n)]
        copies = [pltpu.make_async_remote_copy(
            src_ref=ins[a].at[2 * px + py], dst_ref=outs[a].at[1 + k], send_sem=send_sems.at[a, k],
            recv_sem=recv_sems.at[a, k], device_id=(px, py, c), device_id_type=MESH)
            for a in range(n) for k, (px, py) in enumerate(partners)]
        return mine, copies

    def start(self, ins, outs, sems):
        mine, copies = self._copies(ins, outs, sems)
        for cp in mine + copies:
            cp.start()

    def finish(self, ins, outs, sems):
        mine, copies = self._copies(ins, outs, sems)
        for cp in copies + mine:
            cp.wait()


class _Combo:
    def __init__(self, rides):
        self.rides = rides
        self.arrays = [a for r in rides for a in r.arrays]
        self.out_shape = [o for r in rides for o in r.out_shape]
        self.scratch = [sc for r in rides for sc in r.scratch]

    def _parts(self, ins, outs, sems):
        at_a = at_s = 0
        for r in self.rides:
            na, ns = len(r.arrays), len(r.scratch)
            yield r, ins[at_a:at_a + na], outs[at_a:at_a + na], sems[at_s:at_s + ns]
            at_a, at_s = at_a + na, at_s + ns

    def start(self, ins, outs, sems):
        for r, i, o, sm in self._parts(ins, outs, sems):
            r.start(i, o, sm)

    def finish(self, ins, outs, sems):
        for r, i, o, sm in self._parts(ins, outs, sems):
            r.finish(i, o, sm)


def _as_rows(a, lead):
    return a.reshape(a.shape[:lead] + (math.prod(a.shape[lead:-1]), a.shape[-1]))


def _add_pairs(a, b, name):
    a2, b2 = _as_rows(a, 0), _as_rows(b, 0)
    rows, cols = a2.shape
    tr = _row_tile(rows, cols * 4)

    def body(a_ref, b_ref, o_ref):
        o_ref[...] = (a_ref[...].astype(F32) + b_ref[...].astype(F32)).astype(o_ref.dtype)

    spec = _rows(tr, cols)
    out = pl.pallas_call(body, name=name, out_shape=jax.ShapeDtypeStruct(a2.shape, a.dtype), grid=(rows // tr,),
                         in_specs=[spec, spec], out_specs=spec, compiler_params=_params("parallel"))(a2, b2)
    return out.reshape(a.shape)


def _sum_blocks(a, name):
    a3 = _as_rows(a, 1)
    n, rows, cols = a3.shape
    tr = _row_tile(rows, n * cols * 4)

    def body(a_ref, o_ref):
        tot = a_ref[0].astype(F32)
        for k in range(1, n):
            tot = tot + a_ref[k].astype(F32)
        o_ref[...] = tot

    out = pl.pallas_call(body, name=name, out_shape=jax.ShapeDtypeStruct((rows, cols), F32), grid=(rows // tr,),
                         in_specs=[pl.BlockSpec((n, tr, cols), lambda j: (0, j, 0))], out_specs=_rows(tr, cols),
                         compiler_params=_params("parallel"))(a3)
    return out.reshape(a.shape[1:])


def _sum_layers(blocks, name):
    arrs = [_as_rows(a, 1) for a in blocks]
    rows, cols = arrs[0].shape[1:]
    tr = _row_tile(rows, max(a.shape[0] for a in arrs) * cols * 4)
    nj = rows // tr

    def body(*refs):
        o_ref = refs[-1]
        for k, a_ref in enumerate(refs[:-1]):
            @pl.when(pl.program_id(0) == k)
            def _(a_ref=a_ref, n=arrs[k].shape[0]):
                tot = a_ref[0].astype(F32)
                for b in range(1, n):
                    tot = tot + a_ref[b].astype(F32)
                o_ref[0] = tot

    in_specs = [pl.BlockSpec((a.shape[0], tr, cols),
                             lambda l, j, k=k: (0, jnp.where(l == k, j, jnp.where(l < k, 0, nj - 1)), 0))
                for k, a in enumerate(arrs)]
    out = pl.pallas_call(body, name=name, out_shape=jax.ShapeDtypeStruct((len(arrs), rows, cols), F32),
                         grid=(len(arrs), nj), in_specs=in_specs,
                         out_specs=pl.BlockSpec((1, tr, cols), lambda l, j: (l, j, 0)),
                         compiler_params=_params("arbitrary", "arbitrary"))(*arrs)
    return out.reshape((len(arrs),) + blocks[0].shape[1:])


MIX_GROUPS = ("w_in", "w_uq", "w_uk", "w_uv", "w_attn_o", "w_conv_o", "w_pool_o", "w_mix_o")
FFN_GROUPS = ("w_gate", "w_up", "w_down")
MIX_EARLY = ("w_attn_o", "w_conv_o", "w_pool_o", "w_mix_o")
MIX_LATE = ("w_in", "w_uq", "w_uk", "w_uv")


def _pad_axis(a, axis, size):
    pad = [(0, 0)] * a.ndim
    pad[axis] = (0, size - a.shape[axis])
    return jnp.pad(a, pad)


def _local_groups(sh, l):
    out = {n: sh[n][l] for n in BIG}
    for n in ("w_uq", "w_uk", "w_uv"):
        out[n] = _pad_axis(out[n], -1, HEAD_PAD)
    for n in ("w_gate", "w_up"):
        out[n] = _pad_axis(out[n], -1, FF_SHARD_PAD)
    out["w_down"] = _pad_axis(out["w_down"], 0, FF_SHARD_PAD)
    return {n: v.astype(BF16) for n, v in out.items()}


def _arrange_w_in(blocks):
    parts, pos = [], 0
    for ref_lo, ref_hi, at in sorted(W_IN_PIECES, key=lambda p: p[2]):
        if at > pos:
            parts.append(jnp.zeros((blocks.shape[1], at - pos), blocks.dtype))
        for d in range(N_DEV):
            lo, hi = max(ref_lo, d * W_IN_SHARD), min(ref_hi, (d + 1) * W_IN_SHARD)
            if lo < hi:
                parts.append(blocks[d][:, lo - d * W_IN_SHARD:hi - d * W_IN_SHARD])
        pos = at + ref_hi - ref_lo
    if pos < Z_W:
        parts.append(jnp.zeros((blocks.shape[1], Z_W - pos), blocks.dtype))
    return jnp.concatenate(parts, axis=1)


def _w_in_shard(g, d):
    parts = []
    for ref_lo, ref_hi, at in W_IN_PIECES:
        lo, hi = max(ref_lo, d * W_IN_SHARD), min(ref_hi, (d + 1) * W_IN_SHARD)
        if lo < hi:
            parts.append(g[:, at + lo - ref_lo:at + hi - ref_lo])
    return jnp.concatenate(parts, axis=1)


def _mixer_weights(gat):
    w = {n: v for n, v in gat.items() if n != "w_in"}
    attn_o = gat["w_attn_o"].reshape(N_DEV, N_HEADS, V_HEAD, LANES)
    w["w_attn_o"] = _pad_axis(attn_o, 2, HEAD_PAD).reshape(N_DEV, N_HEADS * HEAD_PAD, LANES)
    w["w_mix_o"] = gat["w_mix_o"].reshape(D_MODEL, D_MODEL)
    return w


def _ffn_weights(gat):
    return {"w_gate": gat["w_gate"], "w_up": gat["w_up"], "w_down": gat["w_down"].reshape(D_FF_PAD, D_MODEL)}


def _mixer_grad_groups(gb):
    g = dict(gb)
    if "w_in" in gb:
        g["w_in"] = jnp.stack([_w_in_shard(gb["w_in"], d) for d in range(N_DEV)])
    if "w_attn_o" in gb:
        attn_o = gb["w_attn_o"].reshape(N_DEV, N_HEADS, HEAD_PAD, LANES)[:, :, :V_HEAD]
        g["w_attn_o"] = attn_o.reshape(N_DEV, N_HEADS * V_HEAD, LANES)
    if "w_mix_o" in gb:
        g["w_mix_o"] = gb["w_mix_o"].reshape(N_DEV, D_MODEL // N_DEV, D_MODEL)
    return g


def _ffn_grad_groups(gb):
    return {"w_gate": gb["w_gate"], "w_up": gb["w_up"], "w_down": gb["w_down"].reshape(N_DEV, FF_SHARD_PAD, D_MODEL)}


def _grads_from_groups(tot):
    g = dict(tot)
    g["w_uq"] = tot["w_uq"][..., :QK_NOPE + QK_ROPE]
    g["w_uk"], g["w_uv"] = tot["w_uk"][..., :QK_NOPE], tot["w_uv"][..., :V_HEAD]
    g["w_gate"], g["w_up"] = tot["w_gate"][..., :FF_SHARD], tot["w_up"][..., :FF_SHARD]
    g["w_down"] = tot["w_down"][..., :FF_SHARD, :]
    return g


SMALL_GROUPS = (
    (D_MODEL, ("mix_norm_pre", "mix_norm_post", "ffn_norm_pre", "ffn_norm_post")),
    (CONV_C, ("conv_w", "conv_b", "conv_ln_g", "conv_ln_b", "pool_scale")),
    (Q_RANK, ("q_norm",)), (KV_RANK, ("kv_norm",)), (POOL_GD, ("pool_w",)),
)


def _small_rows(name):
    return {"conv_w": CONV_HALO, "pool_w": POOL_G * POOL_GD}.get(name, SUBLANES)


def _small_groups(small):
    out = []
    for width, names in SMALL_GROUPS:
        parts = []
        for l in range(DEPTH):
            for n in names:
                part = small[l][n].reshape(-1, width)
                parts.append(_pad_axis(part, 0, _small_rows(n)))
        out.append(jnp.concatenate(parts, axis=0))
    return out


def _small_from_groups(groups):
    shapes = {"conv_w": (CONV_W, CONV_C), "pool_w": (POOL_G, POOL_GD, POOL_GD)}
    out = {}
    for (width, names), g in zip(SMALL_GROUPS, groups):
        row = 0
        for l in range(DEPTH):
            for n in names:
                rows = _small_rows(n)
                real = {"conv_w": CONV_W, "pool_w": POOL_G * POOL_GD}.get(n, 1)
                out.setdefault(n, []).append(g[row:row + real].reshape(shapes.get(n, (width,))))
                row += rows
    return {n: jnp.stack(v) for n, v in out.items()}


def _mixer_fwd(x, h, tables, sm, plan, l):
    nm = lambda n: f"{n}_l{l}"
    if h is None:
        h = _rms_fwd(x, (D_MODEL, 0), sm["mix_norm_pre"], BF16, nm("mix_pre_norm"))
    w_in, ride = plan.w_in(l), plan.in_proj_ride(l)
    if ride is None:
        z = _matmul(h, w_in, "nn", BF16, nm("in_proj"))
    else:
        z, rode = _matmul(h, w_in, "nn", BF16, nm("in_proj"), ride=ride)
        plan.in_proj_done(l, rode)
    w = dict(plan.mixer_weights(l), w_in=w_in)
    cq, ckv, q, k, v = _qkv_up_fwd(z, sm["q_norm"], sm["kv_norm"], w["w_uq"], w["w_uk"], w["w_uv"], tables, nm("qkv_up"))
    (o, lse), rode = _flash_fwd(q, k, v, nm("flash_fwd"), plan.fwd_ride(l))
    plan.fwd_done(l, rode)
    hc, co = _conv_fwd(z, sm["conv_w"], sm["conv_b"], sm["conv_ln_g"], sm["conv_ln_b"], nm("conv_fwd"))
    pm = _pool_fwd(z, sm["pool_w"], sm["pool_scale"], nm("pool_fwd"))
    ys, merged = _branches_merge_fwd(z, (o, hc, pm), (w["w_attn_o"], w["w_conv_o"], w["w_pool_o"]), nm("branches_merge"))
    mo = _matmul(merged, w["w_mix_o"], "nn", F32, nm("mix_out"))
    x_mid, h2 = _rms_fwd(mo, (D_MODEL, 0), sm["mix_norm_post"], F32, nm("mix_post_norm"), res=x, then=sm["ffn_norm_pre"])
    saved = dict(x=x, h=h, z=z, cq=cq, ckv=ckv, q=q, k=k, v=v, o=o, lse=lse, hc=hc, co=co, pm=pm, ys=ys, merged=merged,
                 mo=mo)
    return x_mid, h2, saved, w


def _ffn_fwd(x_mid, h2, w, sm, tag, next_gain):
    nm = lambda n: f"{n}_{tag}"
    hg, hu, act = _ffn_up_fwd(h2, w["w_gate"], w["w_up"], nm("ffn_up_fwd"))
    fo = _matmul(act, w["w_down"], "nn", F32, nm("ffn_down"))
    out = _rms_fwd(fo, (D_MODEL, 0), sm["ffn_norm_post"], F32, nm("ffn_post_norm"), res=x_mid, then=next_gain)
    out, h_next = out if next_gain is not None else (out, None)
    saved = dict(x_mid=x_mid, h2=h2, hg=hg, hu=hu, act=act, fo=fo)
    return out, h_next, saved


def _ffn_bwd(dout, sv, w, sm, tag):
    nm = lambda n: f"{n}_{tag}"
    gb, gs = {}, {}
    dfo, gs["ffn_norm_post"] = _rms_bwd(sv["fo"], (D_MODEL, 0), sm["ffn_norm_post"], dout, BF16, nm("ffn_post_norm_bwd"))
    gb["w_down"] = _matmul(sv["act"], dfo, "tn", BF16, nm("ffn_down_dw"))
    dhg, dhu = _ffn_down_bwd(dfo, w["w_down"], sv["hg"], sv["hu"], nm("ffn_down_bwd"))
    dh2_g = _matmul(dhg, w["w_gate"], "nt", F32, nm("ffn_gate_dx"))
    dh2 = _matmul(dhu, w["w_up"], "nt", F32, nm("ffn_up_dx"), add=dh2_g)
    gb["w_gate"] = _matmul(sv["h2"], dhg, "tn", BF16, nm("ffn_gate_dw"), blocked=True)
    gb["w_up"] = _matmul(sv["h2"], dhu, "tn", BF16, nm("ffn_up_dw"), blocked=True)
    dmid, gs["ffn_norm_pre"] = _rms_bwd(sv["x_mid"], (D_MODEL, 0), sm["ffn_norm_pre"], dh2, F32, nm("ffn_pre_norm_bwd"),
                                        add=dout)
    return dmid, gb, gs


def _mixer_bwd(dmid, sv, tables, w, sm, plan, l, pack_small):
    nm = lambda n: f"{n}_l{l}"
    gb, gs = {}, {}
    dmo, gs["mix_norm_post"] = _rms_bwd(sv["mo"], (D_MODEL, 0), sm["mix_norm_post"], dmid, BF16, nm("mix_post_norm_bwd"))
    dmerged = _matmul(dmo, w["w_mix_o"], "nt", F32, nm("mix_out_dx"))
    gb["w_mix_o"] = _matmul(sv["merged"], dmo, "tn", BF16, nm("mix_out_dw"))
    dya, dyc, dyp, dz = _merge_bwd(sv["z"], sv["ys"], dmerged, nm("merge_bwd"))
    dpm = _matmul(dyp, w["w_pool_o"], "nt", F32, nm("pool_out_dx"))
    gb["w_pool_o"] = _matmul(sv["pm"], dyp, "tn", BF16, nm("pool_out_dw"), blocked=True)
    dz, gs["pool_w"], gs["pool_scale"] = _pool_bwd(dpm, sv["z"], sm["pool_w"], sm["pool_scale"], dz, nm("pool_bwd"))
    dhc = _matmul(dyc, w["w_conv_o"], "nt", F32, nm("conv_out_dx"))
    gb["w_conv_o"] = _matmul(sv["hc"], dyc, "tn", BF16, nm("conv_out_dw"), blocked=True)
    dco, gs["conv_ln_g"], gs["conv_ln_b"], gs["conv_b"] = _conv_bwd_norm(dhc, sv["co"], sm["conv_ln_g"], sm["conv_ln_b"],
                                                                        nm("conv_bwd_norm"))
    dz, gs["conv_w"] = _conv_bwd_taps(dco, sv["z"], sm["conv_w"], dz, nm("conv_bwd_taps"))
    gb["w_attn_o"] = _matmul(sv["o"], dya, "tn", BF16, nm("attn_out_dw"), blocked=True)
    delta, dob = _attn_out_bwd(dya, w["w_attn_o"], sv["o"], nm("attn_out_bwd"))
    (dq, dk, dv), rode = _flash_bwd(sv["q"], sv["k"], sv["v"], dob, sv["lse"], delta, nm("flash_bwd"),
                                  plan.bwd_ride(l, gb))
    plan.bwd_done(l, rode)
    dqf, dkf, dz, gs["q_norm"], gs["kv_norm"] = _qkv_up_bwd(
        dq, dk, dv, sv["z"], w["w_uq"], w["w_uk"], w["w_uv"], tables, sm["q_norm"], sm["kv_norm"], dz, nm("qkv_up_bwd"))
    gb["w_uq"] = _matmul(sv["cq"], dqf, "tn", BF16, nm("q_up_dw"), blocked=True)
    gb["w_uk"] = _matmul(sv["ckv"], dkf, "tn", BF16, nm("k_up_dw"), blocked=True)
    gb["w_uv"] = _matmul(sv["ckv"], dv, "tn", BF16, nm("v_up_dw"), blocked=True)
    gb["w_in"] = _matmul(sv["h"], dz, "tn", BF16, nm("in_proj_dw"))
    plan.add_grads(l, "mix", gb)
    ride, small_gathered = plan.tail_ride(l, pack_small(gs)), []
    if ride is None:
        dh = _matmul(dz, w["w_in"], "nt", F32, nm("in_proj_dx"))
    else:
        dh, rode = _matmul(dz, w["w_in"], "nt", F32, nm("in_proj_dx"), ride=ride)
        small_gathered = plan.tail_done(l, rode)
    dx, gs["mix_norm_pre"] = _rms_bwd(sv["x"], (D_MODEL, 0), sm["mix_norm_pre"], dh, F32, nm("mix_pre_norm_bwd"), add=dmid)
    return dx, gs, small_gathered


def _part_groups(part):
    return {"mix": MIX_GROUPS, "ffn": FFN_GROUPS, "early": MIX_EARLY, "late": MIX_LATE}[part]


class _Plan:
    def __init__(self, shards, conv_w):
        self.local = [_local_groups(shards, l) for l in range(DEPTH)]
        self.conv_w = conv_w
        self.gat, self.send, self.recv = {}, {}, {}

    @staticmethod
    def _riders(l):
        return [(l, "ffn")] + ([(l + 1, "mix")] if l + 1 < DEPTH else [])

    @staticmethod
    def _grad_riders(l):
        return [(l, "ffn"), (l, "early")] + ([(l + 1, "late")] if l + 1 < DEPTH else [])

    def gather_first(self):
        w_in, conv_w = _all_gather([self.local[0]["w_in"], self.conv_w], "gather_w_in_l0")
        self.gat[(0, "mix")] = {"w_in": w_in}
        return conv_w

    def w_in(self, l):
        return _arrange_w_in(self.gat[(l, "mix")]["w_in"])

    def in_proj_ride(self, l):
        return _GatherRide([self.local[0][g] for g in MIX_GROUPS[1:]]) if l == 0 else None

    def in_proj_done(self, l, outs):
        self.gat[(l, "mix")].update(zip(MIX_GROUPS[1:], outs))

    def fwd_ride(self, l):
        return _GatherRide([self.local[ll][g] for ll, part in self._riders(l) for g in _part_groups(part)])

    def fwd_done(self, l, outs):
        outs = list(outs)
        for ll, part in self._riders(l):
            self.gat[(ll, part)] = {g: outs.pop(0) for g in _part_groups(part)}

    def mixer_weights(self, l):
        return _mixer_weights(self.gat[(l, "mix")])

    def ffn_weights(self, l):
        return _ffn_weights(self.gat[(l, "ffn")])

    def add_grads(self, l, part, gb):
        if part == "ffn":
            self.send[(l, "ffn")] = _ffn_grad_groups(gb)
        else:
            self.send.setdefault((l, "late"), {}).update(_mixer_grad_groups({g: gb[g] for g in MIX_LATE if g in gb}))

    def bwd_ride(self, l, gb_early):
        self.send[(l, "early")] = _mixer_grad_groups({g: gb_early[g] for g in MIX_EARLY})
        return _ReduceRide([self.send[(ll, part)][g] for ll, part in self._grad_riders(l) for g in _part_groups(part)])

    def bwd_done(self, l, outs):
        outs = list(outs)
        for ll, part in self._grad_riders(l):
            self.recv[(ll, part)] = {g: outs.pop(0) for g in _part_groups(part)}

    def tail_ride(self, l, small_groups):
        if l > 0:
            return None
        send = [self.send[(0, "late")][g] for g in MIX_LATE]
        by_core = [a.reshape((4, 2) + a.shape[1:]).transpose((1, 0) + tuple(range(2, a.ndim + 1))) for a in send]
        core = lax.axis_index("c")
        own = [lax.dynamic_index_in_dim(a, core, axis=0, keepdims=False) for a in by_core]
        got = _swap_with_sibling(by_core, "reduce_d2d")
        pairs = [_add_pairs(a, b, f"reduce_pair_add_{g}") for g, a, b in zip(MIX_LATE, own, got)]
        return _Combo([_ChipExchangeRide(pairs), _GatherRide(small_groups)])

    def tail_done(self, l, outs):
        self.recv[(l, "late")] = dict(zip(MIX_LATE, outs[:len(MIX_LATE)]))
        return outs[len(MIX_LATE):]

    def finish(self):
        per_layer = [{g: a for part in ("early", "late", "ffn") for g, a in self.recv[(l, part)].items()}
                     for l in range(DEPTH)]
        return _grads_from_groups({g: _sum_layers([per_layer[l][g] for l in range(DEPTH)], f"reduce_sum_{g}")
                                   for g in BIG})


def _local_step(x, positions, target, smalls, plan):
    tables = _rope_tables(positions)
    saved = []
    h, h_norm = x, None
    for l in range(DEPTH):
        h, h2, svm, wm = _mixer_fwd(h, h_norm, tables, smalls[l], plan, l)
        wf = plan.ffn_weights(l)
        next_gain = smalls[l + 1]["mix_norm_pre"] if l + 1 < DEPTH else None
        h, h_norm, svf = _ffn_fwd(h, h2, wf, smalls[l], f"l{l}", next_gain)
        saved.append((svm, svf, wm, wf))
    dy, sq = _loss_grad(h, target, "loss_grad")
    small = [None] * DEPTH
    for l in reversed(range(DEPTH)):
        svm, svf, wm, wf = saved[l]
        dmid, gbf, gsf = _ffn_bwd(dy, svf, wf, smalls[l], f"l{l}")
        plan.add_grads(l, "ffn", gbf)

        def pack_small(gs, l=l, gsf=gsf):
            if l > 0:
                return None
            return _small_groups([{**gsf, **gs, "mix_norm_pre": jnp.zeros((D_MODEL,), F32)}] + small[1:])

        dy, gsm, small_gathered = _mixer_bwd(dmid, svm, tables, wm, smalls[l], plan, l, pack_small)
        small[l] = {**gsf, **gsm}
    return sq, dy, small, small_gathered


def kernel(x, positions, mix_norm_pre, w_in, q_norm, w_uq, kv_norm, w_uk, w_uv, w_attn_o, conv_w, conv_b, conv_ln_g, conv_ln_b, w_conv_o, pool_w, pool_scale, w_pool_o, w_mix_o, mix_norm_post, ffn_norm_pre, w_gate, w_up, w_down, ffn_norm_post, loss_target, m_mix_norm_pre, m_w_in, m_q_norm, m_w_uq, m_kv_norm, m_w_uk, m_w_uv, m_w_attn_o, m_conv_w, m_conv_b, m_conv_ln_g, m_conv_ln_b, m_w_conv_o, m_pool_w, m_pool_scale, m_w_pool_o, m_w_mix_o, m_mix_norm_post, m_ffn_norm_pre, m_w_gate, m_w_up, m_w_down, m_ffn_norm_post, v_mix_norm_pre, v_w_in, v_q_norm, v_w_uq, v_kv_norm, v_w_uk, v_w_uv, v_w_attn_o, v_conv_w, v_conv_b, v_conv_ln_g, v_conv_ln_b, v_w_conv_o, v_pool_w, v_pool_scale, v_w_pool_o, v_w_mix_o, v_mix_norm_post, v_ffn_norm_pre, v_w_gate, v_w_up, v_w_down, v_ffn_norm_post):
    given = dict(locals())
    dev = 4 * lax.axis_index("x") + 2 * lax.axis_index("y") + lax.axis_index("c")

    plan = _Plan({n: given[n] for n in BIG}, conv_w)
    cw = CONV_C // N_DEV
    conv_w_full = plan.gather_first().transpose(1, 2, 0, 3).reshape(DEPTH, CONV_W, CONV_C)
    smalls = []
    for l in range(DEPTH):
        sm = {n: given[n][l] for n in SMALL if n != "conv_w"}
        sm["conv_w"] = _pad_axis(conv_w_full[l], 0, CONV_HALO)
        smalls.append(sm)

    sq, grad_x, small, small_groups = _local_step(x[0], positions[0], loss_target[0], smalls, plan)
    loss = lax.psum(0.5 / D_MODEL * jnp.sum(sq), ("x", "y", "c"))
    views = {n: lax.optimization_barrier(_lane_major(n, g)) for n, g in plan.finish().items()}
    grads = {n: _from_lane_major(n, views[n]) for n in BIG}

    small_sum = _small_from_groups([_sum_blocks(g, f"sum_small_grads_{i}") for i, g in enumerate(small_groups)])
    last = _pad_axis(small[0]["mix_norm_pre"].reshape(1, D_MODEL), 0, SUBLANES)
    last_sum = _sum_blocks(_all_gather([last], "gather_last_norm_grad")[0], "sum_last_norm_grad")[0]
    small_sum["mix_norm_pre"] = small_sum["mix_norm_pre"].at[0].set(last_sum)
    for n in SMALL:
        grads[n] = small_sum[n]
    grads["conv_w"] = lax.dynamic_slice_in_dim(small_sum["conv_w"], dev * cw, cw, axis=2)

    delta, new_m, new_v = {}, {}, {}
    for n in WEIGHTS:
        g_view = views[n] if n in views else grads[n]
        w_view, m_view, v_view = [_lane_major(n, given[k]) for k in (n, "m_" + n, "v_" + n)]
        res = _adamw(w_view, g_view, m_view, v_view, f"adamw_{n}")
        delta[n], new_m[n], new_v[n] = [_from_lane_major(n, r) for r in res]
    return (loss, grad_x[None], *[grads[n] for n in WEIGHTS], *[delta[n] for n in WEIGHTS],
            *[new_m[n] for n in WEIGHTS], *[new_v[n] for n in WEIGHTS])
```

```python
import functools
import math

import jax
import jax.numpy as jnp
from jax import lax
from jax.experimental import pallas as pl
from jax.experimental.pallas import tpu as pltpu

F32, BF16 = jnp.float32, jnp.bfloat16
MESH = pl.DeviceIdType.MESH

LANES = 128
SUBLANES = 8
VMEM_LIMIT_BYTES = 56 * 1024 * 1024
MATMUL_VMEM_BYTES = 40 * 1024 * 1024

N_DEV = 8
D_MODEL = 1024
DEPTH = 2
N_HEADS = 8
QK_NOPE, QK_ROPE, V_HEAD = 64, 32, 64
HEAD_PAD = LANES
Q_RANK, KV_RANK = 384, 256
ROPE_THETA = 10000.0
CONV_C, CONV_W = 512, 31
CONV_HALO = 32
POOL_WINDOWS = (2, 4, 8, 16)
POOL_C, POOL_G = 512, 4
POOL_GD = POOL_C // POOL_G
D_FF = 2816
FF_SHARD = D_FF // N_DEV
FF_SHARD_PAD = 3 * LANES
D_FF_PAD = N_DEV * FF_SHARD_PAD
W_IN_SHARD = 660
EPS = 1e-6
ATTN_SCALE = 1.0 / math.sqrt(QK_NOPE + QK_ROPE)
LOG2E = 1.4426950408889634
LR, B1, B2, ADAM_EPS, WD, STEP = 0.001, 0.9, 0.999, 1e-08, 0.01, 10

Z_W = 5376
ZC_GATE = (1024, 0)
ZC_GATES = (3072, 0)
ZC_CONV_A = (512, 6)
ZC_CONV_G = (512, 7)
ZC_CONV = (1024, 3)
ZC_POOL = (512, 8)
ZC_Q = (384, 12)
ZC_KR = (128, 39)
ZC_KV = (256, 20)
ZC_QKR = (768, 6)
W_IN_PIECES = ((0, 384, 4608), (384, 640, 5120), (640, 672, 5056), (672, 1696, 3072), (1696, 2208, 4096),
               (2208, 5280, 0))

BIG = ("w_in", "w_uq", "w_uk", "w_uv", "w_attn_o", "w_conv_o", "w_pool_o", "w_mix_o", "w_gate", "w_up", "w_down")
SMALL = ("mix_norm_pre", "q_norm", "kv_norm", "conv_w", "conv_b", "conv_ln_g", "conv_ln_b", "pool_w", "pool_scale",
         "mix_norm_post", "ffn_norm_pre", "ffn_norm_post")
WEIGHTS = ("mix_norm_pre", "w_in", "q_norm", "w_uq", "kv_norm", "w_uk", "w_uv", "w_attn_o", "conv_w", "conv_b",
           "conv_ln_g", "conv_ln_b", "w_conv_o", "pool_w", "pool_scale", "w_pool_o", "w_mix_o", "mix_norm_post",
           "ffn_norm_pre", "w_gate", "w_up", "w_down", "ffn_norm_post")


def _params(*semantics):
    return pltpu.CompilerParams(dimension_semantics=semantics, vmem_limit_bytes=VMEM_LIMIT_BYTES)


def _tile(dim, cap):
    if dim <= cap:
        return dim
    for t in range(cap - cap % LANES, 0, -LANES):
        if dim % t == 0:
            return t
    raise ValueError(f"no tile for {dim} under {cap}")


def _row_tile(rows, row_bytes, budget=1 << 20):
    if rows * row_bytes <= budget:
        return rows
    cap = max(16, budget // row_bytes)
    for t in range(cap - cap % 16, 0, -16):
        if rows % t == 0:
            return t
    return rows


def _rows(ts, width, cidx=0):
    return pl.BlockSpec((ts, width), lambda i: (i, cidx))


def _fixed(shape):
    return pl.BlockSpec(shape, lambda *_: (0,) * len(shape))


def _sigmoid(x):
    return 1.0 / (1.0 + jnp.exp(-x))


def _matmul(a, b, mode, out_dtype, name, add=None, blocked=False, ride=None):
    nb = n_blk = 0
    blocked = blocked or b.ndim == 3
    if mode == "nn":
        (m, k) = a.shape
        n = b.shape[0] * b.shape[2] if blocked else b.shape[1]
    elif mode == "nt":
        (m, k) = a.shape
        n = b.shape[1] if blocked else b.shape[0]
    else:
        (k, m), n = a.shape, b.shape[1]
    if blocked:
        nb = b.shape[2] if mode != "tn" else n // N_DEV
    unit = nb if blocked and mode != "nt" else LANES
    out_bytes = jnp.dtype(out_dtype).itemsize + (4 if add is not None else 0)
    best = None
    for tn_c in range(unit, min(n, 1536) + 1, unit):
        for tm_c in sorted({256, 512, 1024, 2048, min(m, 2048)}):
            if n % tn_c or m % tm_c or (blocked and mode != "nt" and N_DEV % (tn_c // nb)):
                continue
            vmem = 2 * (tm_c * k * 2 + tn_c * k * 2 + tm_c * tn_c * out_bytes) + tm_c * tn_c * 4 + tn_c * k * 2
            if vmem <= MATMUL_VMEM_BYTES and (best is None or tm_c * tn_c / (tm_c + tn_c) > best[0]):
                best = (tm_c * tn_c / (tm_c + tn_c), tm_c, tn_c)
    if best is None:
        raise ValueError(f"{name}: no tiles for {m}x{n}x{k}")
    _, tm, tn = best
    if blocked:
        n_blk = N_DEV if mode == "nt" else tn // nb
    dims = {"nn": ((1,), (0,)), "nt": ((1,), (1,)), "tn": ((0,), (0,))}[mode]
    a_spec = pl.BlockSpec((k, tm), lambda i, j: (0, i)) if mode == "tn" else pl.BlockSpec((tm, k), lambda i, j: (i, 0))
    b_spec = pl.BlockSpec((tn, k), lambda i, j: (j, 0)) if mode == "nt" else pl.BlockSpec((k, tn), lambda i, j: (0, j))
    o_spec = pl.BlockSpec((tm, tn), lambda i, j: (i, j))
    out_shape = jax.ShapeDtypeStruct((m, n), out_dtype)
    if blocked and mode == "nn":
        b_spec = pl.BlockSpec((n_blk, k, nb), lambda i, j: (j, 0, 0))
    elif blocked and mode == "nt":
        b_spec = pl.BlockSpec((n_blk, tn, nb), lambda i, j: (0, j, 0))
    elif blocked:
        o_spec = pl.BlockSpec((n_blk, tm, nb), lambda i, j: (j, i, 0))
        out_shape = jax.ShapeDtypeStruct((N_DEV, m, nb), out_dtype)
    has_add = add is not None
    grid = (m // tm, n // tn)

    def body(*refs):
        (a_ref, b_ref, *rest), start, finish = _ride_hooks(ride, refs, 3 if has_add else 2, 1, grid)
        start()
        o_ref = rest[-1]
        if blocked and mode != "tn":
            bv = jnp.concatenate([b_ref[c] for c in range(n_blk)], axis=1) if n_blk > 1 else b_ref[0]
        else:
            bv = b_ref[...]
        total = lax.dot_general(a_ref[...], bv, (dims, ((), ())), preferred_element_type=F32)
        if has_add:
            total = total + rest[0][...]
        if blocked and mode == "tn":
            for c in range(n_blk):
                o_ref[c] = total[:, c * nb:(c + 1) * nb].astype(o_ref.dtype)
        else:
            o_ref[...] = total.astype(o_ref.dtype)
        finish()

    operands = (a, b, add) if has_add else (a, b)
    (out,), rode = _ride_call(ride, body, name, (out_shape,), grid, [a_spec, b_spec] + ([o_spec] if has_add else []),
                              (o_spec,), ("parallel", "parallel"), operands)
    return out if ride is None else (out, rode)


def _rms_fwd(x, win, gain, out_dtype, name, res=None, then=None):
    width, cidx = win
    s = x.shape[0]
    ts = min(s, 512)
    has_res, has_then = res is not None, then is not None

    def norm(v, g_ref):
        return (v * lax.rsqrt(jnp.mean(v * v, axis=-1, keepdims=True) + EPS)) * g_ref[...]

    def body(x_ref, g_ref, *rest):
        y = norm(x_ref[...].astype(F32), g_ref)
        if has_res:
            y = rest[0][...] + y
        o_ref = rest[-2] if has_then else rest[-1]
        o_ref[...] = y.astype(o_ref.dtype)
        if has_then:
            rest[-1][...] = norm(y, rest[-3]).astype(BF16)

    ops = (x, gain.reshape(1, width)) + ((res,) if has_res else ()) + ((then.reshape(1, width),) if has_then else ())
    out_shape = (jax.ShapeDtypeStruct((s, width), out_dtype),) + ((jax.ShapeDtypeStruct((s, width), BF16),) * has_then)
    out = pl.pallas_call(
        body, name=name, out_shape=out_shape, grid=(s // ts,),
        in_specs=([_rows(ts, width, cidx), _fixed((1, width))] + ([_rows(ts, width)] if has_res else [])
                  + ([_fixed((1, width))] if has_then else [])),
        out_specs=(_rows(ts, width),) * len(out_shape), compiler_params=_params("parallel"))(*ops)
    return out if has_then else out[0]


def _into(dz, n_inputs, out_index):
    return dict(in_specs=[ANY], operands=(dz,), input_output_aliases={n_inputs: out_index},
                out_shape=jax.ShapeDtypeStruct(dz.shape, dz.dtype))


def _rms_bwd(x, win, gain, dy, out_dtype, name, add=None, dz=None):
    width, cidx = win
    s = x.shape[0]
    ts = min(s, 512)
    has_add = add is not None

    def body(x_ref, g_ref, dy_ref, *rest):
        dx_ref, dg_ref = rest[-2], rest[-1]
        xv = x_ref[...].astype(F32)
        r = lax.rsqrt(jnp.mean(xv * xv, axis=-1, keepdims=True) + EPS)
        xh = xv * r
        dyv = dy_ref[...].astype(F32)
        dyg = dyv * g_ref[...]
        dx = r * (dyg - xh * jnp.mean(dyg * xh, axis=-1, keepdims=True))
        if has_add:
            dx = dx + rest[0][...]
        dx_ref[...] = dx.astype(dx_ref.dtype)

        @pl.when(pl.program_id(0) == 0)
        def _():
            dg_ref[...] = jnp.zeros_like(dg_ref)

        dg_ref[...] += jnp.sum(dyv * xh, axis=0, keepdims=True)

    ops = (x, gain.reshape(1, width), dy) + ((add,) if has_add else ())
    in_specs = [_rows(ts, width, cidx), _fixed((1, width)), _rows(ts, width)] + ([_rows(ts, width)] if has_add else [])
    dx_shape, dx_spec, alias = jax.ShapeDtypeStruct((s, width), out_dtype), _rows(ts, width), {}
    if dz is not None:
        into = _into(dz, len(ops), 0)
        ops, in_specs, alias = ops + into["operands"], in_specs + into["in_specs"], into["input_output_aliases"]
        dx_shape, dx_spec = into["out_shape"], _rows(ts, width, cidx)
    dx, dg = pl.pallas_call(
        body, name=name, out_shape=(dx_shape, jax.ShapeDtypeStruct((1, width), F32)), grid=(s // ts,),
        in_specs=in_specs, out_specs=(dx_spec, _fixed((1, width))), input_output_aliases=alias,
        compiler_params=_params("arbitrary"))(*ops)
    return dx, dg.reshape(width)


def _rope(x, c, s1, s2):
    return x * c + pltpu.roll(x, 16, 1) * s1 + pltpu.roll(x, LANES - 16, 1) * s2


def _rope_t(g, c, s1, s2):
    return g * c + pltpu.roll(g * s1, LANES - 16, 1) + pltpu.roll(g * s2, 16, 1)


def _rope_tables(positions):
    inv_freq = ROPE_THETA ** (-jnp.arange(0, QK_ROPE, 2, dtype=F32) / QK_ROPE)
    ang = positions.astype(F32)[:, None] * inv_freq
    cos, sin = jnp.cos(ang), jnp.sin(ang)
    n = positions.shape[0]
    one, zero = jnp.ones((n, 1), F32), jnp.zeros((n, 1), F32)
    c = jnp.concatenate([jnp.tile(one, (1, QK_NOPE)), cos, cos, jnp.tile(one, (1, 32))], axis=1)
    s1 = jnp.concatenate([jnp.tile(zero, (1, QK_NOPE + 16)), sin, jnp.tile(zero, (1, 32))], axis=1)
    s2 = jnp.concatenate([jnp.tile(zero, (1, QK_NOPE)), -sin, jnp.tile(zero, (1, 48))], axis=1)
    return c, s1, s2


def _qkv_up_fwd(z, q_gain, kv_gain, w_uq, w_uk, w_uv, tables, name):
    s = z.shape[0]
    ts = min(s, 512)
    hw = N_HEADS * HEAD_PAD
    kv0 = Q_RANK + LANES

    def norm(v, g_ref):
        return ((v * lax.rsqrt(jnp.mean(v * v, axis=-1, keepdims=True) + EPS)) * g_ref[...]).astype(BF16)

    def body(z_ref, gq_ref, gkv_ref, wq_ref, wk_ref, wv_ref, c_ref, s1_ref, s2_ref, cq_ref, ckv_ref, q_ref, k_ref, v_ref):
        c, s1, s2 = c_ref[...], s1_ref[...], s2_ref[...]
        cqv = norm(z_ref[:, pl.ds(0, Q_RANK)].astype(F32), gq_ref)
        ckvv = norm(z_ref[:, pl.ds(kv0, KV_RANK)].astype(F32), gkv_ref)
        cq_ref[...] = cqv
        ckv_ref[...] = ckvv
        kr = _rope(z_ref[:, pl.ds(Q_RANK, LANES)].astype(F32), c, s1, s2)
        for h in range(N_HEADS):
            sl = slice(h * HEAD_PAD, (h + 1) * HEAD_PAD)
            q_ref[:, sl] = _rope(jnp.dot(cqv, wq_ref[h], preferred_element_type=F32), c, s1, s2).astype(BF16)
            k_ref[:, sl] = (jnp.dot(ckvv, wk_ref[h], preferred_element_type=F32) + kr).astype(BF16)
            v_ref[:, sl] = jnp.dot(ckvv, wv_ref[h], preferred_element_type=F32).astype(BF16)

    tab = _rows(ts, LANES)
    wide = jax.ShapeDtypeStruct((s, hw), BF16)
    return pl.pallas_call(
        body, name=name,
        out_shape=(jax.ShapeDtypeStruct((s, Q_RANK), BF16), jax.ShapeDtypeStruct((s, KV_RANK), BF16), wide, wide, wide),
        grid=(s // ts,),
        in_specs=[_rows(ts, *ZC_QKR), _fixed((1, Q_RANK)), _fixed((1, KV_RANK)), _fixed(w_uq.shape), _fixed(w_uk.shape),
                  _fixed(w_uv.shape), tab, tab, tab],
        out_specs=(_rows(ts, Q_RANK), _rows(ts, KV_RANK)) + (_rows(ts, hw),) * 3, compiler_params=_params("parallel"))(
            z, q_gain.reshape(1, -1), kv_gain.reshape(1, -1), w_uq, w_uk, w_uv, *tables)


def _qkv_up_bwd(dq, dk, dv, z, w_uq, w_uk, w_uv, tables, q_gain, kv_gain, dz, name):
    s = dq.shape[0]
    ts = min(s, 512)
    hw = N_HEADS * HEAD_PAD
    zw = ZC_QKR[0]
    kv0 = Q_RANK + LANES
    dims_nt = (((1,), (1,)), ((), ()))

    def norm_bwd(xv, g_ref, dyv):
        r = lax.rsqrt(jnp.mean(xv * xv, axis=-1, keepdims=True) + EPS)
        xh = xv * r
        dyg = dyv * g_ref[...]
        return r * (dyg - xh * jnp.mean(dyg * xh, axis=-1, keepdims=True)), jnp.sum(dyv * xh, axis=0, keepdims=True)

    def body(dq_ref, dk_ref, dv_ref, z_ref, wq_ref, wk_ref, wv_ref, c_ref, s1_ref, s2_ref, gq_ref, gkv_ref, _,
             dqf_ref, dkf_ref, dz_ref, dgq_ref, dgkv_ref):
        c, s1, s2 = c_ref[...], s1_ref[...], s2_ref[...]
        ksum = jnp.zeros((ts, HEAD_PAD), F32)
        dcq = jnp.zeros((ts, Q_RANK), F32)
        dckv = jnp.zeros((ts, KV_RANK), F32)
        for h in range(N_HEADS):
            sl = slice(h * HEAD_PAD, (h + 1) * HEAD_PAD)
            dqh = _rope_t(dq_ref[:, sl], c, s1, s2).astype(BF16)
            dkv = dk_ref[:, sl]
            dkh = dkv.astype(BF16)
            dqf_ref[:, sl] = dqh
            dkf_ref[:, sl] = dkh
            ksum = ksum + dkv
            dcq = dcq + lax.dot_general(dqh, wq_ref[h], dims_nt, preferred_element_type=F32)
            dckv = dckv + (lax.dot_general(dkh, wk_ref[h], dims_nt, preferred_element_type=F32)
                           + lax.dot_general(dv_ref[:, sl], wv_ref[h], dims_nt, preferred_element_type=F32))
        dxq, dgq = norm_bwd(z_ref[:, pl.ds(0, Q_RANK)].astype(F32), gq_ref, dcq)
        dxkv, dgkv = norm_bwd(z_ref[:, pl.ds(kv0, KV_RANK)].astype(F32), gkv_ref, dckv)
        lane = lax.broadcasted_iota(jnp.int32, (ts, HEAD_PAD), 1)
        in_rope = (lane >= QK_NOPE) & (lane < QK_NOPE + QK_ROPE)
        dz_ref[:, pl.ds(0, Q_RANK)] = dxq.astype(BF16)
        dz_ref[:, pl.ds(Q_RANK, LANES)] = jnp.where(in_rope, _rope_t(ksum, c, s1, s2), 0.0).astype(BF16)
        dz_ref[:, pl.ds(kv0, KV_RANK)] = dxkv.astype(BF16)

        @pl.when(pl.program_id(0) == 0)
        def _():
            dgq_ref[...] = jnp.zeros_like(dgq_ref)
            dgkv_ref[...] = jnp.zeros_like(dgkv_ref)

        dgq_ref[...] += dgq
        dgkv_ref[...] += dgkv

    tab = _rows(ts, LANES)
    into = _into(dz, 12, 2)
    dqf, dkf, dz, dgq, dgkv = pl.pallas_call(
        body, name=name,
        out_shape=(jax.ShapeDtypeStruct((s, hw), BF16), jax.ShapeDtypeStruct((s, hw), BF16), into["out_shape"],
                   jax.ShapeDtypeStruct((1, Q_RANK), F32), jax.ShapeDtypeStruct((1, KV_RANK), F32)),
        grid=(s // ts,),
        in_specs=[_rows(ts, hw), _rows(ts, hw), _rows(ts, hw), _rows(ts, *ZC_QKR), _fixed(w_uq.shape), _fixed(w_uk.shape),
                  _fixed(w_uv.shape), tab, tab, tab, _fixed((1, Q_RANK)), _fixed((1, KV_RANK))] + into["in_specs"],
        out_specs=(_rows(ts, hw), _rows(ts, hw), _rows(ts, *ZC_QKR), _fixed((1, Q_RANK)), _fixed((1, KV_RANK))),
        input_output_aliases=into["input_output_aliases"], compiler_params=_params("arbitrary"))(
            dq, dk, dv, z, w_uq, w_uk, w_uv, *tables, q_gain.reshape(1, -1), kv_gain.reshape(1, -1), dz)
    return dqf, dkf, dz, dgq.reshape(-1), dgkv.reshape(-1)


def _attn_tile(s):
    return min(s, 512)


def _raw_scores(q, k, masked, row0=0):
    sc = lax.dot_general(q, k, (((1,), (1,)), ((), ())), preferred_element_type=F32)
    if masked:
        rows = row0 + lax.broadcasted_iota(jnp.int32, sc.shape, 0)
        cols = lax.broadcasted_iota(jnp.int32, sc.shape, 1)
        sc = jnp.where(cols <= rows, sc, -jnp.inf)
    return sc


def _ride_hooks(ride, refs, n_in, n_out, grid):
    if ride is None:
        return refs, lambda: None, lambda: None
    n = len(ride.arrays)
    own = refs[:n_in] + refs[n_in + n:n_in + n + n_out]
    ins, outs, sems = refs[n_in:n_in + n], refs[n_in + n + n_out:n_in + 2 * n + n_out], refs[n_in + 2 * n + n_out:]
    at_first = functools.reduce(lambda a, b: a & b, [pl.program_id(ax) == 0 for ax in range(len(grid))])
    at_last = functools.reduce(lambda a, b: a & b, [pl.program_id(ax) == g - 1 for ax, g in enumerate(grid)])
    return own, lambda: pl.when(at_first)(lambda: ride.start(ins, outs, sems)), \
        lambda: pl.when(at_last)(lambda: ride.finish(ins, outs, sems))


def _ride_call(ride, body, name, out_shape, grid, in_specs, out_specs, semantics, operands):
    n = 0 if ride is None else len(ride.arrays)
    res = pl.pallas_call(
        body, name=name, out_shape=tuple(out_shape) + (tuple(ride.out_shape) if n else ()), grid=grid,
        in_specs=list(in_specs) + [ANY] * n, out_specs=tuple(out_specs) + (ANY,) * n,
        scratch_shapes=list(ride.scratch) if n else [],
        compiler_params=_params(*(("arbitrary",) * len(grid) if n else semantics)))(*operands, *(ride.arrays if n else ()))
    return res[:len(out_shape)], list(res[len(out_shape):])


def _flash_fwd(q, k, v, name, ride=None):
    s = q.shape[0]
    t = _attn_tile(s)
    c2 = ATTN_SCALE * LOG2E
    grid = (N_HEADS, s // t)

    def body(*refs):
        (q_ref, k_ref, v_ref, o_ref, lse_ref), start, finish = _ride_hooks(ride, refs, 3, 2, grid)
        start()
        i = pl.program_id(1)
        qv = q_ref[...]

        def chunk(j, carry, masked):
            m_old, l_old, acc = carry
            at = pl.ds(pl.multiple_of(j * t, t), t)
            sc = _raw_scores(qv, k_ref[at, :], masked)
            m_new = jnp.maximum(m_old, jnp.max(sc, axis=-1, keepdims=True))
            p = jnp.exp2((sc - m_new) * c2)
            alpha = jnp.exp2((m_old - m_new) * c2)
            l_new = alpha * l_old + jnp.sum(p, axis=-1, keepdims=True)
            acc = alpha * acc + jnp.dot(p.astype(BF16), v_ref[at, :], preferred_element_type=F32)
            return m_new, l_new, acc

        init = (jnp.full((t, 1), -jnp.inf, F32), jnp.zeros((t, 1), F32), jnp.zeros((t, HEAD_PAD), F32))
        carry = lax.fori_loop(0, i, lambda j, cr: chunk(j, cr, False), init)
        m_fin, l_fin, acc = chunk(i, carry, True)
        o_ref[...] = (acc / l_fin).astype(o_ref.dtype)
        lse_ref[...] = jnp.broadcast_to(m_fin * ATTN_SCALE + jnp.log(l_fin), (t, HEAD_PAD))
        finish()

    qo = pl.BlockSpec((t, HEAD_PAD), lambda h, i: (i, h))
    whole = pl.BlockSpec((s, HEAD_PAD), lambda h, i: (0, h))
    return _ride_call(
        ride, body, name, (jax.ShapeDtypeStruct(q.shape, BF16), jax.ShapeDtypeStruct(q.shape, F32)), grid,
        [qo, whole, whole], (qo, qo), ("parallel", "parallel"), (q, k, v))


def _attn_out_bwd(dya, w_attn_o, o, name):
    s, d = dya.shape
    hw = N_HEADS * HEAD_PAD
    t = _attn_tile(s)

    def body(d_ref, w_ref, o_ref, delta_ref, dob_ref):
        wv = jnp.concatenate([w_ref[c] for c in range(N_DEV)], axis=1)
        do = lax.dot_general(d_ref[...], wv, (((1,), (1,)), ((), ())), preferred_element_type=F32)
        for h in range(N_HEADS):
            sl = slice(h * HEAD_PAD, (h + 1) * HEAD_PAD)
            dov = do[:, sl]
            delta_ref[:, sl] = jnp.broadcast_to(jnp.sum(dov * o_ref[:, sl].astype(F32), axis=-1, keepdims=True),
                                                (t, HEAD_PAD))
            dob_ref[:, sl] = dov.astype(BF16)

    blk = _rows(t, hw)
    return pl.pallas_call(
        body, name=name, out_shape=(jax.ShapeDtypeStruct(o.shape, F32), jax.ShapeDtypeStruct(o.shape, BF16)),
        grid=(s // t,), in_specs=[_rows(t, d), _fixed(w_attn_o.shape), blk], out_specs=(blk, blk),
        compiler_params=_params("parallel"))(dya, w_attn_o, o)


def _flash_bwd(q, k, v, do, lse, delta, name, ride=None):
    s = q.shape[0]
    t = _attn_tile(s)
    nt = s // t
    c2 = ATTN_SCALE * LOG2E
    grid = (N_HEADS, nt)

    def body(*refs):
        (q_ref, k_ref, v_ref, do_ref, lse_ref, delta_ref, dq_ref, dk_ref, dv_ref), start, finish = _ride_hooks(
            ride, refs, 6, 3, grid)
        start()
        j = pl.program_id(1)
        kv, vv = k_ref[...], v_ref[...]

        @pl.when(j == 0)
        def _():
            dq_ref[...] = jnp.zeros_like(dq_ref)

        def chunk(i, carry, masked):
            dk_acc, dv_acc = carry
            at = pl.ds(pl.multiple_of(i * t, t), t)
            qi, doi = q_ref[at, :], do_ref[at, :]
            sc = _raw_scores(qi, kv, masked)
            p = jnp.exp2(sc * c2 - lse_ref[at, pl.ds(0, 1)] * LOG2E)
            dp = lax.dot_general(doi, vv, (((1,), (1,)), ((), ())), preferred_element_type=F32)
            ds = (p * (dp - delta_ref[at, pl.ds(0, 1)])).astype(BF16)
            dv_acc = dv_acc + lax.dot_general(p.astype(BF16), doi, (((0,), (0,)), ((), ())), preferred_element_type=F32)
            dk_acc = dk_acc + lax.dot_general(ds, qi, (((0,), (0,)), ((), ())), preferred_element_type=F32)
            dq_ref[at, :] += jnp.dot(ds, kv, preferred_element_type=F32) * ATTN_SCALE
            return dk_acc, dv_acc

        zero = jnp.zeros((t, HEAD_PAD), F32)
        carry = chunk(j, (zero, zero), True)
        dk_acc, dv_acc = lax.fori_loop(j + 1, nt, lambda i, cr: chunk(i, cr, False), carry)
        dk_ref[...] = dk_acc * ATTN_SCALE
        dv_ref[...] = dv_acc.astype(BF16)
        finish()

    blk = pl.BlockSpec((t, HEAD_PAD), lambda h, j: (j, h))
    whole = pl.BlockSpec((s, HEAD_PAD), lambda h, j: (0, h))
    return _ride_call(
        ride, body, name, (jax.ShapeDtypeStruct(q.shape, F32), jax.ShapeDtypeStruct(q.shape, F32),
                           jax.ShapeDtypeStruct(q.shape, BF16)), grid,
        [whole, blk, blk, whole, whole, whole], (whole, blk, blk), ("parallel", "arbitrary"), (q, k, v, do, lse, delta))


def _conv_tile(s):
    return min(s, 256)


def _halo_before(t, width, cidx):
    per = t // CONV_HALO
    return pl.BlockSpec((CONV_HALO, width), lambda i: (jnp.maximum(i * per - 1, 0), cidx))


def _halo_after(t, width, cidx, n_tiles):
    per = t // CONV_HALO
    last = n_tiles * per - 1
    return pl.BlockSpec((CONV_HALO, width), lambda i: (jnp.minimum((i + 1) * per, last), cidx))


def _fill_glu(hbuf, ap_ref, gp_ref, a_ref, g_ref, t):
    first = pl.program_id(0) == 0
    hbuf[pl.ds(0, CONV_HALO), :] = jnp.where(first, 0.0, ap_ref[...].astype(F32) * _sigmoid(gp_ref[...].astype(F32)))
    hbuf[pl.ds(CONV_HALO, t), :] = a_ref[...].astype(F32) * _sigmoid(g_ref[...].astype(F32))


def _phase_copies(dst, src, t):
    n = t + CONV_HALO - SUBLANES
    for s in range(1, SUBLANES):
        dst[s, pl.ds(0, n), :] = src[pl.ds(s, n), :]


def _window(phases, src, k, t):
    if k % SUBLANES == 0:
        return src[pl.ds(k, t), :]
    return phases[k % SUBLANES, pl.ds(k - k % SUBLANES, t), :]


def _layer_norm_parts(co):
    mu = jnp.mean(co, axis=-1, keepdims=True)
    xc = co - mu
    rstd = lax.rsqrt(jnp.mean(xc * xc, axis=-1, keepdims=True) + EPS)
    return xc * rstd, rstd


def _conv_fwd(z, conv_w, conv_b, ln_g, ln_b, name):
    s = z.shape[0]
    t = _conv_tile(s)
    off = CONV_HALO - (CONV_W - 1)

    def body(ap_ref, gp_ref, a_ref, g_ref, w_ref, b_ref, lg_ref, lb_ref, hc_ref, co_ref, hbuf, hph):
        _fill_glu(hbuf, ap_ref, gp_ref, a_ref, g_ref, t)
        _phase_copies(hph, hbuf, t)
        acc = jnp.zeros((t, CONV_C), F32) + b_ref[...]
        for j in range(CONV_W):
            acc = acc + _window(hph, hbuf, off + j, t) * w_ref[pl.ds(j, 1), :]
        co_ref[...] = acc
        xh, _ = _layer_norm_parts(acc)
        y = xh * lg_ref[...] + lb_ref[...]
        hc_ref[...] = (y * _sigmoid(y)).astype(BF16)

    vec = _fixed((1, CONV_C))
    return pl.pallas_call(
        body, name=name, out_shape=(jax.ShapeDtypeStruct((s, CONV_C), BF16), jax.ShapeDtypeStruct((s, CONV_C), F32)),
        grid=(s // t,),
        in_specs=[_halo_before(t, *ZC_CONV_A), _halo_before(t, *ZC_CONV_G), _rows(t, *ZC_CONV_A), _rows(t, *ZC_CONV_G),
                  _fixed((CONV_HALO, CONV_C)), vec, vec, vec],
        out_specs=(_rows(t, CONV_C), _rows(t, CONV_C)),
        scratch_shapes=[pltpu.VMEM((t + CONV_HALO, CONV_C), F32), pltpu.VMEM((SUBLANES, t + CONV_HALO, CONV_C), F32)],
        compiler_params=_params("parallel"))(z, z, z, z, conv_w, conv_b.reshape(1, -1), ln_g.reshape(1, -1),
                                             ln_b.reshape(1, -1))


def _conv_bwd_norm(dhc, co, ln_g, ln_b, name):
    s = co.shape[0]
    t = min(s, 512)

    def body(dhc_ref, co_ref, lg_ref, lb_ref, dco_ref, dg_ref, db_ref, dcb_ref):
        xh, rstd = _layer_norm_parts(co_ref[...])
        y = xh * lg_ref[...] + lb_ref[...]
        sg = _sigmoid(y)
        dy = dhc_ref[...] * (sg * (1.0 + y * (1.0 - sg)))
        dxh = dy * lg_ref[...]
        dco = rstd * (dxh - jnp.mean(dxh, axis=-1, keepdims=True) - xh * jnp.mean(dxh * xh, axis=-1, keepdims=True))
        dco_ref[...] = dco

        @pl.when(pl.program_id(0) == 0)
        def _():
            dg_ref[...] = jnp.zeros_like(dg_ref)
            db_ref[...] = jnp.zeros_like(db_ref)
            dcb_ref[...] = jnp.zeros_like(dcb_ref)

        dg_ref[...] += jnp.sum(dy * xh, axis=0, keepdims=True)
        db_ref[...] += jnp.sum(dy, axis=0, keepdims=True)
        dcb_ref[...] += jnp.sum(dco, axis=0, keepdims=True)

    vec = _fixed((1, CONV_C))
    one = jax.ShapeDtypeStruct((1, CONV_C), F32)
    dco, dg, db, dcb = pl.pallas_call(
        body, name=name, out_shape=(jax.ShapeDtypeStruct((s, CONV_C), F32), one, one, one), grid=(s // t,),
        in_specs=[_rows(t, CONV_C), _rows(t, CONV_C), vec, vec], out_specs=(_rows(t, CONV_C), vec, vec, vec),
        compiler_params=_params("arbitrary"))(dhc, co, ln_g.reshape(1, -1), ln_b.reshape(1, -1))
    return dco, dg.reshape(-1), db.reshape(-1), dcb.reshape(-1)


def _conv_bwd_taps(dco, z, conv_w, dz, name):
    s = z.shape[0]
    t = _conv_tile(s)
    nt = s // t
    off = CONV_HALO - (CONV_W - 1)

    def body(ap_ref, gp_ref, a_ref, g_ref, d_ref, dn_ref, w_ref, _, du_ref, dw_ref, hbuf, dbuf, hph, dph):
        i = pl.program_id(0)
        _fill_glu(hbuf, ap_ref, gp_ref, a_ref, g_ref, t)
        dbuf[pl.ds(0, t), :] = d_ref[...]
        dbuf[pl.ds(t, CONV_HALO), :] = jnp.where(i == nt - 1, 0.0, dn_ref[...])
        _phase_copies(hph, hbuf, t)
        _phase_copies(dph, dbuf, t)

        @pl.when(i == 0)
        def _():
            dw_ref[...] = jnp.zeros_like(dw_ref)

        dcur = d_ref[...]
        dh = jnp.zeros((t, CONV_C), F32)
        for j in range(CONV_W):
            dh = dh + _window(dph, dbuf, CONV_W - 1 - j, t) * w_ref[pl.ds(j, 1), :]
            dw_ref[pl.ds(j, 1), :] += jnp.sum(dcur * _window(hph, hbuf, off + j, t), axis=0, keepdims=True)
        a, sg = a_ref[...].astype(F32), _sigmoid(g_ref[...].astype(F32))
        du_ref[:, pl.ds(0, CONV_C)] = (dh * sg).astype(BF16)
        du_ref[:, pl.ds(CONV_C, CONV_C)] = (dh * a * sg * (1.0 - sg)).astype(BF16)

    into = _into(dz, 7, 0)
    return pl.pallas_call(
        body, name=name, out_shape=(into["out_shape"], jax.ShapeDtypeStruct((CONV_HALO, CONV_C), F32)), grid=(nt,),
        in_specs=[_halo_before(t, *ZC_CONV_A), _halo_before(t, *ZC_CONV_G), _rows(t, *ZC_CONV_A), _rows(t, *ZC_CONV_G),
                  _rows(t, CONV_C), _halo_after(t, CONV_C, 0, nt), _fixed((CONV_HALO, CONV_C))] + into["in_specs"],
        out_specs=(_rows(t, *ZC_CONV), _fixed((CONV_HALO, CONV_C))), input_output_aliases=into["input_output_aliases"],
        scratch_shapes=[pltpu.VMEM((t + CONV_HALO, CONV_C), F32), pltpu.VMEM((t + CONV_HALO, CONV_C), F32),
                        pltpu.VMEM((SUBLANES, t + CONV_HALO, CONV_C), F32),
                        pltpu.VMEM((SUBLANES, t + CONV_HALO, CONV_C), F32)],
        compiler_params=_params("arbitrary"))(z, z, z, z, dco, dco, conv_w, dz)


def _pool_tile(s):
    return min(s, 512)


def _pool_counts(row0, n, window):
    rows = row0 + lax.broadcasted_iota(jnp.int32, (n, POOL_GD), 0)
    return jnp.minimum(rows + 1, window).astype(F32)


def _pool_diff(ubuf, gi, window, row0, t):
    lanes = pl.ds(gi * POOL_GD, POOL_GD)
    tot = ubuf[pl.ds(CONV_HALO, t), lanes]
    cur = tot
    for back in range(1, window):
        tot = tot + ubuf[pl.ds(CONV_HALO - back, t), lanes]
    return tot / _pool_counts(row0, t, window) - cur


def _pool_fwd(z, pool_w, pool_scale, name):
    s = z.shape[0]
    t = _pool_tile(s)

    def body(up_ref, u_ref, w_ref, sc_ref, m_ref, ubuf):
        i = pl.program_id(0)
        ubuf[pl.ds(0, CONV_HALO), :] = jnp.where(i == 0, 0.0, up_ref[...].astype(F32))
        ubuf[pl.ds(CONV_HALO, t), :] = u_ref[...].astype(F32)
        for gi, window in enumerate(POOL_WINDOWS):
            d = _pool_diff(ubuf, gi, window, i * t, t)
            mm = jnp.dot(d.astype(BF16), w_ref[gi].astype(BF16), preferred_element_type=F32)
            lanes = pl.ds(gi * POOL_GD, POOL_GD)
            m_ref[:, lanes] = (mm * sc_ref[:, lanes]).astype(BF16)

    return pl.pallas_call(
        body, name=name, out_shape=jax.ShapeDtypeStruct((s, POOL_C), BF16), grid=(s // t,),
        in_specs=[_halo_before(t, *ZC_POOL), _rows(t, *ZC_POOL), _fixed((POOL_G, POOL_GD, POOL_GD)), _fixed((1, POOL_C))],
        out_specs=_rows(t, POOL_C), scratch_shapes=[pltpu.VMEM((t + CONV_HALO, POOL_C), F32)],
        compiler_params=_params("parallel"))(z, z, pool_w, pool_scale.reshape(1, -1))


def _pool_bwd(dm, z, pool_w, pool_scale, dz, name):
    s = z.shape[0]
    t = _pool_tile(s)
    nt = s // t

    def body(up_ref, u_ref, dm_ref, dmn_ref, w_ref, sc_ref, _, du_ref, dw_ref, dsc_ref, ubuf, ebuf):
        i = pl.program_id(0)
        ubuf[pl.ds(0, CONV_HALO), :] = jnp.where(i == 0, 0.0, up_ref[...].astype(F32))
        ubuf[pl.ds(CONV_HALO, t), :] = u_ref[...].astype(F32)

        @pl.when(i == 0)
        def _():
            dw_ref[...] = jnp.zeros_like(dw_ref)
            dsc_ref[...] = jnp.zeros_like(dsc_ref)

        dm_next = jnp.where(i == nt - 1, 0.0, dmn_ref[...])
        for gi, window in enumerate(POOL_WINDOWS):
            lanes = pl.ds(gi * POOL_GD, POOL_GD)
            wb = w_ref[gi].astype(BF16)
            scale = sc_ref[:, lanes]
            d = _pool_diff(ubuf, gi, window, i * t, t).astype(BF16)
            mm = jnp.dot(d, wb, preferred_element_type=F32)
            dmv = dm_ref[:, lanes]
            dsc_ref[:, lanes] += jnp.sum(dmv * mm, axis=0, keepdims=True)
            dmm = (dmv * scale).astype(BF16)
            dw_ref[gi] += lax.dot_general(d, dmm, (((0,), (0,)), ((), ())), preferred_element_type=F32)
            dd = lax.dot_general(dmm, wb, (((1,), (1,)), ((), ())), preferred_element_type=F32)
            dd_next = lax.dot_general((dm_next[:, gi * POOL_GD:(gi + 1) * POOL_GD] * scale).astype(BF16), wb,
                                      (((1,), (1,)), ((), ())), preferred_element_type=F32)
            ebuf[pl.ds(0, t), lanes] = dd / _pool_counts(i * t, t, window)
            ebuf[pl.ds(t, CONV_HALO), lanes] = dd_next / _pool_counts((i + 1) * t, CONV_HALO, window)
            du = -dd
            for ahead in range(window):
                du = du + ebuf[pl.ds(ahead, t), lanes]
            du_ref[:, lanes] = du.astype(BF16)

    into = _into(dz, 6, 0)
    du, dw, dsc = pl.pallas_call(
        body, name=name,
        out_shape=(into["out_shape"], jax.ShapeDtypeStruct((POOL_G, POOL_GD, POOL_GD), F32),
                   jax.ShapeDtypeStruct((1, POOL_C), F32)), grid=(nt,),
        in_specs=[_halo_before(t, *ZC_POOL), _rows(t, *ZC_POOL), _rows(t, POOL_C), _halo_after(t, POOL_C, 0, nt),
                  _fixed((POOL_G, POOL_GD, POOL_GD)), _fixed((1, POOL_C))] + into["in_specs"],
        out_specs=(_rows(t, *ZC_POOL), _fixed((POOL_G, POOL_GD, POOL_GD)), _fixed((1, POOL_C))),
        input_output_aliases=into["input_output_aliases"],
        scratch_shapes=[pltpu.VMEM((t + CONV_HALO, POOL_C), F32), pltpu.VMEM((t + CONV_HALO, POOL_C), F32)],
        compiler_params=_params("arbitrary"))(z, z, dm, dm, pool_w, pool_scale.reshape(1, -1), dz)
    return du, dw, dsc.reshape(-1)


def _gate_specs(ts):
    width, first = ZC_GATE
    return [_rows(ts, width, first + b) for b in range(3)]


def _branches_merge_fwd(z, acts, ws, name):
    s = z.shape[0]
    ts = min(s, 512)

    def body(g0, g1, g2, a0, a1, a2, w0, w1, w2, y0, y1, y2, m_ref):
        merged = jnp.zeros((ts, D_MODEL), F32)
        for g_ref, a_ref, w_ref, y_ref in ((g0, a0, w0, y0), (g1, a1, w1, y1), (g2, a2, w2, y2)):
            wv = jnp.concatenate([w_ref[c] for c in range(N_DEV)], axis=1)
            yb = jnp.dot(a_ref[...], wv, preferred_element_type=F32).astype(BF16)
            y_ref[...] = yb
            merged = merged + _sigmoid(g_ref[...].astype(F32)) * yb.astype(F32)
        m_ref[...] = merged.astype(BF16)

    out = jax.ShapeDtypeStruct((s, D_MODEL), BF16)
    res = pl.pallas_call(
        body, name=name, out_shape=(out,) * 4, grid=(s // ts,),
        in_specs=_gate_specs(ts) + [_rows(ts, a.shape[1]) for a in acts] + [_fixed(w.shape) for w in ws],
        out_specs=(_rows(ts, D_MODEL),) * 4, compiler_params=_params("parallel"))(z, z, z, *acts, *ws)
    return tuple(res[:3]), res[3]


def _merge_bwd(z, ys, dmerged, name):
    s = z.shape[0]
    ts = min(s, 256)

    def body(g0, g1, g2, y0, y1, y2, dm_ref, dy0, dy1, dy2, dz_ref):
        dmv = dm_ref[...]
        for b, (g_ref, y_ref, dy_ref) in enumerate(((g0, y0, dy0), (g1, y1, dy1), (g2, y2, dy2))):
            sg = _sigmoid(g_ref[...].astype(F32))
            dy_ref[...] = (dmv * sg).astype(BF16)
            dz_ref[:, pl.ds(b * D_MODEL, D_MODEL)] = (dmv * y_ref[...].astype(F32) * sg * (1.0 - sg)).astype(BF16)

    out = jax.ShapeDtypeStruct((s, D_MODEL), BF16)
    return pl.pallas_call(
        body, name=name, out_shape=(out,) * 3 + (jax.ShapeDtypeStruct((s, Z_W), BF16),), grid=(s // ts,),
        in_specs=_gate_specs(ts) + [_rows(ts, D_MODEL)] * 4,
        out_specs=(_rows(ts, D_MODEL),) * 3 + (_rows(ts, *ZC_GATES),),
        compiler_params=_params("parallel"))(z, z, z, *ys, dmerged)


def _ffn_up_fwd(h, w_gate, w_up, name):
    s, d = h.shape
    nb = w_gate.shape[2]
    f = N_DEV * nb
    tm, n_blk = min(s, 1024), 2
    tn = n_blk * nb
    blk = pl.BlockSpec((tm, tn), lambda i, j: (i, j))
    wspec = pl.BlockSpec((n_blk, d, nb), lambda i, j: (j, 0, 0))

    def body(h_ref, wg_ref, wu_ref, hg_ref, hu_ref, act_ref):
        hv = h_ref[...]
        g = jnp.dot(hv, jnp.concatenate([wg_ref[c] for c in range(n_blk)], axis=1), preferred_element_type=F32)
        u = jnp.dot(hv, jnp.concatenate([wu_ref[c] for c in range(n_blk)], axis=1), preferred_element_type=F32)
        hg_ref[...] = g.astype(hg_ref.dtype)
        hu_ref[...] = u.astype(hu_ref.dtype)
        act_ref[...] = (g * _sigmoid(g) * u).astype(BF16)

    return pl.pallas_call(
        body, name=name,
        out_shape=(jax.ShapeDtypeStruct((s, f), BF16),) * 3,
        grid=(s // tm, f // tn), in_specs=[pl.BlockSpec((tm, d), lambda i, j: (i, 0)), wspec, wspec],
        out_specs=(blk, blk, blk), compiler_params=_params("parallel", "parallel"))(h, w_gate, w_up)


def _ffn_down_bwd(dfo, w_down, hg, hu, name):
    s, d = dfo.shape
    f = w_down.shape[0]
    tm, tn = min(s, 1024), _tile(f, 1024)
    blk = pl.BlockSpec((tm, tn), lambda i, j: (i, j))

    def body(d_ref, w_ref, g_ref, u_ref, dg_ref, du_ref):
        dact = lax.dot_general(d_ref[...], w_ref[...], (((1,), (1,)), ((), ())), preferred_element_type=F32)
        g = g_ref[...].astype(F32)
        sg = _sigmoid(g)
        dg_ref[...] = (dact * u_ref[...].astype(F32) * (sg * (1.0 + g * (1.0 - sg)))).astype(BF16)
        du_ref[...] = (dact * g * sg).astype(BF16)

    out = jax.ShapeDtypeStruct((s, f), BF16)
    return pl.pallas_call(
        body, name=name, out_shape=(out, out), grid=(s // tm, f // tn),
        in_specs=[pl.BlockSpec((tm, d), lambda i, j: (i, 0)), pl.BlockSpec((tn, d), lambda i, j: (j, 0)), blk, blk],
        out_specs=(blk, blk), compiler_params=_params("parallel", "parallel"))(dfo, w_down, hg, hu)


def _loss_grad(y, target, name):
    s, d = y.shape
    ts = min(s, 512)

    def body(y_ref, t_ref, dy_ref, sq_ref):
        e = y_ref[...] - t_ref[...]
        dy_ref[...] = e / d

        @pl.when(pl.program_id(0) == 0)
        def _():
            sq_ref[...] = jnp.zeros_like(sq_ref)

        sq_ref[...] += jnp.sum(e * e, axis=0, keepdims=True)

    return pl.pallas_call(
        body, name=name, out_shape=(jax.ShapeDtypeStruct((s, d), F32), jax.ShapeDtypeStruct((1, d), F32)),
        grid=(s // ts,), in_specs=[_rows(ts, d), _rows(ts, d)], out_specs=(_rows(ts, d), _fixed((1, d))),
        compiler_params=_params("arbitrary"))(y, target)


def _adamw(w, g, m, v, name):
    shape = w.shape
    cols = shape[-1]
    keep3 = w.ndim == 3 and shape[1] < SUBLANES
    view = shape if keep3 else (math.prod(shape[:-1]), cols)
    rows = view[0]
    if keep3:
        cap = max(1, (1 << 20) // (SUBLANES * cols * 4))
        tr = max(t for t in range(1, cap + 1) if rows % t == 0)
    else:
        tr = _row_tile(rows, cols * 4)

    def body(w_ref, g_ref, m_ref, v_ref, d_ref, mo_ref, vo_ref):
        gv = g_ref[...]
        mn = B1 * m_ref[...] + (1.0 - B1) * gv
        vn = B2 * v_ref[...] + (1.0 - B2) * (gv * gv)
        m_hat = mn / (1.0 - B1 ** STEP)
        v_hat = vn / (1.0 - B2 ** STEP)
        d_ref[...] = -LR * (m_hat / (jnp.sqrt(v_hat) + ADAM_EPS) + WD * w_ref[...])
        mo_ref[...] = mn
        vo_ref[...] = vn

    spec = pl.BlockSpec((tr,) + view[1:], lambda i: (i,) + (0,) * (len(view) - 1))
    out = jax.ShapeDtypeStruct(view, F32)
    res = pl.pallas_call(
        body, name=name, out_shape=(out,) * 3, grid=(rows // tr,), in_specs=[spec] * 4, out_specs=(spec,) * 3,
        compiler_params=_params("parallel"))(*[t.reshape(view) for t in (w, g, m, v)])
    return tuple(r.reshape(shape) for r in res)


LANE_MAJOR = ("w_uq", "w_uk", "w_uv", "w_gate", "w_up")


def _lane_major(name, a):
    if name == "w_in":
        return a.transpose(2, 0, 1)
    if name in LANE_MAJOR:
        return a.transpose(0, 2, 1)
    return a


def _from_lane_major(name, a):
    if name == "w_in":
        return a.transpose(1, 2, 0)
    return _lane_major(name, a)


ANY = pl.BlockSpec(memory_space=pl.ANY)


class _GatherRide:
    def __init__(self, arrays):
        n = len(arrays)
        self.arrays = list(arrays)
        self.out_shape = [jax.ShapeDtypeStruct((N_DEV,) + a.shape, a.dtype) for a in arrays]
        self.scratch = [pltpu.SemaphoreType.DMA((n, 7)), pltpu.SemaphoreType.DMA((n, 7)), pltpu.SemaphoreType.DMA((n,))]

    def _copies(self, ins, outs, sems):
        send_sems, recv_sems, local_sems = sems
        n = len(self.arrays)
        x, y, c = lax.axis_index("x"), lax.axis_index("y"), lax.axis_index("c")
        me, sibling = (x, y, c), (x, y, 1 - c)
        chips = [(1 - x, y), (x, 1 - y), (1 - x, 1 - y)]

        def slot(a, px, py, pc):
            return outs[a].at[4 * px + 2 * py + pc]

        def copy(a, k, block, to, src=None):
            return pltpu.make_async_remote_copy(
                src_ref=slot(a, *block) if src is None else src, dst_ref=slot(a, *block), send_sem=send_sems.at[a, k],
                recv_sem=recv_sems.at[a, k], device_id=to, device_id_type=MESH)

        mine = [pltpu.make_async_copy(ins[a], slot(a, *me), local_sems.at[a]) for a in range(n)]
        first = []
        for a in range(n):
            first.append(copy(a, 0, me, sibling, src=ins[a]))
            first += [copy(a, 1 + j, me, (*chip, c), src=ins[a]) for j, chip in enumerate(chips)]
        return n, me, sibling, chips, c, copy, mine, first

    def start(self, ins, outs, sems):
        _, _, _, _, _, _, mine, first = self._copies(ins, outs, sems)
        for cp in mine + first:
            cp.start()

    def finish(self, ins, outs, sems):
        n, me, sibling, chips, c, copy, mine, first = self._copies(ins, outs, sems)
        passed = []
        for j, chip in enumerate(chips):
            for a in range(n):
                copy(a, 1 + j, (*chip, c), me).wait_recv()
                passed.append(copy(a, 4 + j, (*chip, c), sibling))
                passed[-1].start()
        for a in range(n):
            copy(a, 0, sibling, me).wait_recv()
            for j, chip in enumerate(chips):
                copy(a, 4 + j, (*chip, 1 - c), me).wait_recv()
        for cp in first + passed:
            cp.wait_send()
        for cp in mine:
            cp.wait()


class _ReduceRide:
    def __init__(self, arrays):
        n = len(arrays)
        self.arrays = list(arrays)
        self.out_shape = [jax.ShapeDtypeStruct(a.shape, a.dtype) for a in arrays]
        self.scratch = [pltpu.SemaphoreType.DMA((n, 7)), pltpu.SemaphoreType.DMA((n, 7)), pltpu.SemaphoreType.DMA((n,))]

    def _copies(self, ins, outs, sems):
        send_sems, recv_sems, local_sems = sems
        n = len(self.arrays)
        x, y, c = lax.axis_index("x"), lax.axis_index("y"), lax.axis_index("c")
        mine = [pltpu.make_async_copy(ins[a].at[4 * x + 2 * y + c], outs[a].at[0], local_sems.at[a]) for a in range(n)]
        copies = []
        for a in range(n):
            for k in range(1, N_DEV):
                px = 1 - x if k & 4 else x
                py = 1 - y if k & 2 else y
                pc = 1 - c if k & 1 else c
                copies.append(pltpu.make_async_remote_copy(
                    src_ref=ins[a].at[4 * px + 2 * py + pc], dst_ref=outs[a].at[k], send_sem=send_sems.at[a, k - 1],
                    recv_sem=recv_sems.at[a, k - 1], device_id=(px, py, pc), device_id_type=MESH))
        return mine, copies

    def start(self, ins, outs, sems):
        mine, copies = self._copies(ins, outs, sems)
        for cp in mine + copies:
            cp.start()

    def finish(self, ins, outs, sems):
        mine, copies = self._copies(ins, outs, sems)
        for cp in copies + mine:
            cp.wait()


def _run_ride(ride, name):
    n = len(ride.arrays)

    def body(*refs):
        ins, outs, sems = refs[:n], refs[n:2 * n], refs[2 * n:]
        ride.start(ins, outs, sems)
        ride.finish(ins, outs, sems)

    return pl.pallas_call(body, name=name, out_shape=ride.out_shape, in_specs=[ANY] * n, out_specs=[ANY] * n,
                          scratch_shapes=ride.scratch)(*ride.arrays)


def _all_gather(arrays, name):
    return _run_ride(_GatherRide(arrays), name)


def _swap_with_sibling(arrays, name):
    n = len(arrays)

    def body(*refs):
        ins, outs = refs[:n], refs[n:2 * n]
        send_sems, recv_sems = refs[2 * n:]
        x, y, c = lax.axis_index("x"), lax.axis_index("y"), lax.axis_index("c")
        copies = [pltpu.make_async_remote_copy(
            src_ref=ins[a].at[1 - c], dst_ref=outs[a], send_sem=send_sems.at[a], recv_sem=recv_sems.at[a],
            device_id=(x, y, 1 - c), device_id_type=MESH) for a in range(n)]
        for cp in copies:
            cp.start()
        for cp in copies:
            cp.wait()

    return pl.pallas_call(
        body, name=name, out_shape=[jax.ShapeDtypeStruct(a.shape[1:], a.dtype) for a in arrays],
        in_specs=[ANY] * n, out_specs=[ANY] * n,
        scratch_shapes=[pltpu.SemaphoreType.DMA((n,)), pltpu.SemaphoreType.DMA((n,))])(*arrays)


class _ChipExchangeRide:
    def __init__(self, arrays):
        n = len(arrays)
        self.arrays = list(arrays)
        self.out_shape = [jax.ShapeDtypeStruct(a.shape, a.dtype) for a in arrays]
        self.scratch = [pltpu.SemaphoreType.DMA((n, 3)), pltpu.SemaphoreType.DMA((n, 3)), pltpu.SemaphoreType.DMA((n,))]

    def _copies(self, ins, outs, sems):
        send_sems, recv_sems, local_sems = sems
        n = len(self.arrays)
        x, y, c = lax.axis_index("x"), lax.axis_index("y"), lax.axis_index("c")
        partners = [(x, 1 - y), (1 - x, y), (1 - x, 1 - y)]
        mine = [pltpu.make_async_copy(ins[a].at[2 * x + y], outs[a].at[0], local_sems.at[a]) for a in range(n)]
        copies = [pltpu.make_async_remote_copy(
            src_ref=ins[a].at[2 * px + py], dst_ref=outs[a].at[1 + k], send_sem=send_sems.at[a, k],
            recv_sem=recv_sems.at[a, k], device_id=(px, py, c), device_id_type=MESH)
            for a in range(n) for k, (px, py) in enumerate(partners)]
        return mine, copies

    def start(self, ins, outs, sems):
        mine, copies = self._copies(ins, outs, sems)
        for cp in mine + copies:
            cp.start()

    def finish(self, ins, outs, sems):
        mine, copies = self._copies(ins, outs, sems)
        for cp in copies + mine:
            cp.wait()


class _Combo:
    def __init__(self, rides):
        self.rides = rides
        self.arrays = [a for r in rides for a in r.arrays]
        self.out_shape = [o for r in rides for o in r.out_shape]
        self.scratch = [sc for r in rides for sc in r.scratch]

    def _parts(self, ins, outs, sems):
        at_a = at_s = 0
        for r in self.rides:
            na, ns = len(r.arrays), len(r.scratch)
            yield r, ins[at_a:at_a + na], outs[at_a:at_a + na], sems[at_s:at_s + ns]
            at_a, at_s = at_a + na, at_s + ns

    def start(self, ins, outs, sems):
        for r, i, o, sm in self._parts(ins, outs, sems):
            r.start(i, o, sm)

    def finish(self, ins, outs, sems):
        for r, i, o, sm in self._parts(ins, outs, sems):
            r.finish(i, o, sm)


def _as_rows(a, lead):
    return a.reshape(a.shape[:lead] + (math.prod(a.shape[lead:-1]), a.shape[-1]))


def _add_pairs(a, b, name):
    a2, b2 = _as_rows(a, 0), _as_rows(b, 0)
    rows, cols = a2.shape
    tr = _row_tile(rows, cols * 4)

    def body(a_ref, b_ref, o_ref):
        o_ref[...] = (a_ref[...].astype(F32) + b_ref[...].astype(F32)).astype(o_ref.dtype)

    spec = _rows(tr, cols)
    out = pl.pallas_call(body, name=name, out_shape=jax.ShapeDtypeStruct(a2.shape, a.dtype), grid=(rows // tr,),
                         in_specs=[spec, spec], out_specs=spec, compiler_params=_params("parallel"))(a2, b2)
    return out.reshape(a.shape)


def _sum_blocks(a, name):
    a3 = _as_rows(a, 1)
    n, rows, cols = a3.shape
    tr = _row_tile(rows, n * cols * 4)

    def body(a_ref, o_ref):
        tot = a_ref[0].astype(F32)
        for k in range(1, n):
            tot = tot + a_ref[k].astype(F32)
        o_ref[...] = tot

    out = pl.pallas_call(body, name=name, out_shape=jax.ShapeDtypeStruct((rows, cols), F32), grid=(rows // tr,),
                         in_specs=[pl.BlockSpec((n, tr, cols), lambda j: (0, j, 0))], out_specs=_rows(tr, cols),
                         compiler_params=_params("parallel"))(a3)
    return out.reshape(a.shape[1:])


def _sum_layers(blocks, name):
    arrs = [_as_rows(a, 1) for a in blocks]
    rows, cols = arrs[0].shape[1:]
    tr = _row_tile(rows, max(a.shape[0] for a in arrs) * cols * 4, budget=8 << 20)
    nj = rows // tr

    def body(*refs):
        o_ref = refs[-1]
        for k, a_ref in enumerate(refs[:-1]):
            @pl.when(pl.program_id(0) == k)
            def _(a_ref=a_ref, n=arrs[k].shape[0]):
                tot = a_ref[0].astype(F32)
                for b in range(1, n):
                    tot = tot + a_ref[b].astype(F32)
                o_ref[0] = tot

    in_specs = [pl.BlockSpec((a.shape[0], tr, cols),
                             lambda l, j, k=k: (0, jnp.where(l == k, j, jnp.where(l < k, 0, nj - 1)), 0))
                for k, a in enumerate(arrs)]
    out = pl.pallas_call(body, name=name, out_shape=jax.ShapeDtypeStruct((len(arrs), rows, cols), F32),
                         grid=(len(arrs), nj), in_specs=in_specs,
                         out_specs=pl.BlockSpec((1, tr, cols), lambda l, j: (l, j, 0)),
                         compiler_params=_params("arbitrary", "arbitrary"))(*arrs)
    return out.reshape((len(arrs),) + blocks[0].shape[1:])


MIX_GROUPS = ("w_in", "w_uq", "w_uk", "w_uv", "w_attn_o", "w_conv_o", "w_pool_o", "w_mix_o")
FFN_GROUPS = ("w_gate", "w_up", "w_down")
MIX_EARLY = ("w_attn_o", "w_conv_o", "w_pool_o", "w_mix_o")
MIX_LATE = ("w_in", "w_uq", "w_uk", "w_uv")


def _pad_axis(a, axis, size):
    pad = [(0, 0)] * a.ndim
    pad[axis] = (0, size - a.shape[axis])
    return jnp.pad(a, pad)


def _local_groups(sh, l):
    out = {n: sh[n][l] for n in BIG}
    for n in ("w_uq", "w_uk", "w_uv"):
        out[n] = _pad_axis(out[n], -1, HEAD_PAD)
    for n in ("w_gate", "w_up"):
        out[n] = _pad_axis(out[n], -1, FF_SHARD_PAD)
    out["w_down"] = _pad_axis(out["w_down"], 0, FF_SHARD_PAD)
    return {n: v.astype(BF16) for n, v in out.items()}


def _arrange_w_in(blocks):
    parts, pos = [], 0
    for ref_lo, ref_hi, at in sorted(W_IN_PIECES, key=lambda p: p[2]):
        if at > pos:
            parts.append(jnp.zeros((blocks.shape[1], at - pos), blocks.dtype))
        for d in range(N_DEV):
            lo, hi = max(ref_lo, d * W_IN_SHARD), min(ref_hi, (d + 1) * W_IN_SHARD)
            if lo < hi:
                parts.append(blocks[d][:, lo - d * W_IN_SHARD:hi - d * W_IN_SHARD])
        pos = at + ref_hi - ref_lo
    if pos < Z_W:
        parts.append(jnp.zeros((blocks.shape[1], Z_W - pos), blocks.dtype))
    return jnp.concatenate(parts, axis=1)


def _w_in_shard(g, d):
    parts = []
    for ref_lo, ref_hi, at in W_IN_PIECES:
        lo, hi = max(ref_lo, d * W_IN_SHARD), min(ref_hi, (d + 1) * W_IN_SHARD)
        if lo < hi:
            parts.append(g[:, at + lo - ref_lo:at + hi - ref_lo])
    return jnp.concatenate(parts, axis=1)


def _mixer_weights(gat):
    w = {n: v for n, v in gat.items() if n != "w_in"}
    attn_o = gat["w_attn_o"].reshape(N_DEV, N_HEADS, V_HEAD, LANES)
    w["w_attn_o"] = _pad_axis(attn_o, 2, HEAD_PAD).reshape(N_DEV, N_HEADS * HEAD_PAD, LANES)
    w["w_mix_o"] = gat["w_mix_o"].reshape(D_MODEL, D_MODEL)
    return w


def _ffn_weights(gat):
    return {"w_gate": gat["w_gate"], "w_up": gat["w_up"], "w_down": gat["w_down"].reshape(D_FF_PAD, D_MODEL)}


def _mixer_grad_groups(gb):
    g = dict(gb)
    if "w_in" in gb:
        g["w_in"] = jnp.stack([_w_in_shard(gb["w_in"], d) for d in range(N_DEV)])
    if "w_attn_o" in gb:
        attn_o = gb["w_attn_o"].reshape(N_DEV, N_HEADS, HEAD_PAD, LANES)[:, :, :V_HEAD]
        g["w_attn_o"] = attn_o.reshape(N_DEV, N_HEADS * V_HEAD, LANES)
    if "w_mix_o" in gb:
        g["w_mix_o"] = gb["w_mix_o"].reshape(N_DEV, D_MODEL // N_DEV, D_MODEL)
    return g


def _ffn_grad_groups(gb):
    return {"w_gate": gb["w_gate"], "w_up": gb["w_up"], "w_down": gb["w_down"].reshape(N_DEV, FF_SHARD_PAD, D_MODEL)}


def _grads_from_groups(tot):
    g = dict(tot)
    g["w_uq"] = tot["w_uq"][..., :QK_NOPE + QK_ROPE]
    g["w_uk"], g["w_uv"] = tot["w_uk"][..., :QK_NOPE], tot["w_uv"][..., :V_HEAD]
    g["w_gate"], g["w_up"] = tot["w_gate"][..., :FF_SHARD], tot["w_up"][..., :FF_SHARD]
    g["w_down"] = tot["w_down"][..., :FF_SHARD, :]
    return g


SMALL_GROUPS = (
    (D_MODEL, ("mix_norm_pre", "mix_norm_post", "ffn_norm_pre", "ffn_norm_post")),
    (CONV_C, ("conv_w", "conv_b", "conv_ln_g", "conv_ln_b", "pool_scale")),
    (Q_RANK, ("q_norm",)), (KV_RANK, ("kv_norm",)), (POOL_GD, ("pool_w",)),
)


def _small_rows(name):
    return {"conv_w": CONV_HALO, "pool_w": POOL_G * POOL_GD}.get(name, SUBLANES)


def _small_groups(small):
    out = []
    for width, names in SMALL_GROUPS:
        parts = []
        for l in range(DEPTH):
            for n in names:
                part = small[l][n].reshape(-1, width)
                parts.append(_pad_axis(part, 0, _small_rows(n)))
        out.append(jnp.concatenate(parts, axis=0))
    return out


def _small_from_groups(groups):
    shapes = {"conv_w": (CONV_W, CONV_C), "pool_w": (POOL_G, POOL_GD, POOL_GD)}
    out = {}
    for (width, names), g in zip(SMALL_GROUPS, groups):
        row = 0
        for l in range(DEPTH):
            for n in names:
                rows = _small_rows(n)
                real = {"conv_w": CONV_W, "pool_w": POOL_G * POOL_GD}.get(n, 1)
                out.setdefault(n, []).append(g[row:row + real].reshape(shapes.get(n, (width,))))
                row += rows
    return {n: jnp.stack(v) for n, v in out.items()}


def _mixer_fwd(x, h, tables, sm, plan, l):
    nm = lambda n: f"{n}_l{l}"
    if h is None:
        h = _rms_fwd(x, (D_MODEL, 0), sm["mix_norm_pre"], BF16, nm("mix_pre_norm"))
    w_in, ride = plan.w_in(l), plan.in_proj_ride(l)
    if ride is None:
        z = _matmul(h, w_in, "nn", BF16, nm("in_proj"))
    else:
        z, rode = _matmul(h, w_in, "nn", BF16, nm("in_proj"), ride=ride)
        plan.in_proj_done(l, rode)
    w = dict(plan.mixer_weights(l), w_in=w_in)
    cq, ckv, q, k, v = _qkv_up_fwd(z, sm["q_norm"], sm["kv_norm"], w["w_uq"], w["w_uk"], w["w_uv"], tables, nm("qkv_up"))
    (o, lse), rode = _flash_fwd(q, k, v, nm("flash_fwd"), plan.fwd_ride(l))
    plan.fwd_done(l, rode)
    hc, co = _conv_fwd(z, sm["conv_w"], sm["conv_b"], sm["conv_ln_g"], sm["conv_ln_b"], nm("conv_fwd"))
    pm = _pool_fwd(z, sm["pool_w"], sm["pool_scale"], nm("pool_fwd"))
    ys, merged = _branches_merge_fwd(z, (o, hc, pm), (w["w_attn_o"], w["w_conv_o"], w["w_pool_o"]), nm("branches_merge"))
    mo = _matmul(merged, w["w_mix_o"], "nn", F32, nm("mix_out"))
    x_mid, h2 = _rms_fwd(mo, (D_MODEL, 0), sm["mix_norm_post"], F32, nm("mix_post_norm"), res=x, then=sm["ffn_norm_pre"])
    saved = dict(x=x, h=h, z=z, cq=cq, ckv=ckv, q=q, k=k, v=v, o=o, lse=lse, hc=hc, co=co, pm=pm, ys=ys, merged=merged,
                 mo=mo)
    return x_mid, h2, saved, w


def _ffn_fwd(x_mid, h2, w, sm, tag, next_gain):
    nm = lambda n: f"{n}_{tag}"
    hg, hu, act = _ffn_up_fwd(h2, w["w_gate"], w["w_up"], nm("ffn_up_fwd"))
    fo = _matmul(act, w["w_down"], "nn", F32, nm("ffn_down"))
    out = _rms_fwd(fo, (D_MODEL, 0), sm["ffn_norm_post"], F32, nm("ffn_post_norm"), res=x_mid, then=next_gain)
    out, h_next = out if next_gain is not None else (out, None)
    saved = dict(x_mid=x_mid, h2=h2, hg=hg, hu=hu, act=act, fo=fo)
    return out, h_next, saved


def _ffn_bwd(dout, sv, w, sm, tag):
    nm = lambda n: f"{n}_{tag}"
    gb, gs = {}, {}
    dfo, gs["ffn_norm_post"] = _rms_bwd(sv["fo"], (D_MODEL, 0), sm["ffn_norm_post"], dout, BF16, nm("ffn_post_norm_bwd"))
    gb["w_down"] = _matmul(sv["act"], dfo, "tn", BF16, nm("ffn_down_dw"))
    dhg, dhu = _ffn_down_bwd(dfo, w["w_down"], sv["hg"], sv["hu"], nm("ffn_down_bwd"))
    dh2_g = _matmul(dhg, w["w_gate"], "nt", F32, nm("ffn_gate_dx"))
    dh2 = _matmul(dhu, w["w_up"], "nt", F32, nm("ffn_up_dx"), add=dh2_g)
    gb["w_gate"] = _matmul(sv["h2"], dhg, "tn", BF16, nm("ffn_gate_dw"), blocked=True)
    gb["w_up"] = _matmul(sv["h2"], dhu, "tn", BF16, nm("ffn_up_dw"), blocked=True)
    dmid, gs["ffn_norm_pre"] = _rms_bwd(sv["x_mid"], (D_MODEL, 0), sm["ffn_norm_pre"], dh2, F32, nm("ffn_pre_norm_bwd"),
                                        add=dout)
    return dmid, gb, gs


def _mixer_bwd(dmid, sv, tables, w, sm, plan, l, pack_small):
    nm = lambda n: f"{n}_l{l}"
    gb, gs = {}, {}
    dmo, gs["mix_norm_post"] = _rms_bwd(sv["mo"], (D_MODEL, 0), sm["mix_norm_post"], dmid, BF16, nm("mix_post_norm_bwd"))
    dmerged = _matmul(dmo, w["w_mix_o"], "nt", F32, nm("mix_out_dx"))
    gb["w_mix_o"] = _matmul(sv["merged"], dmo, "tn", BF16, nm("mix_out_dw"))
    dya, dyc, dyp, dz = _merge_bwd(sv["z"], sv["ys"], dmerged, nm("merge_bwd"))
    dpm = _matmul(dyp, w["w_pool_o"], "nt", F32, nm("pool_out_dx"))
    gb["w_pool_o"] = _matmul(sv["pm"], dyp, "tn", BF16, nm("pool_out_dw"), blocked=True)
    dz, gs["pool_w"], gs["pool_scale"] = _pool_bwd(dpm, sv["z"], sm["pool_w"], sm["pool_scale"], dz, nm("pool_bwd"))
    dhc = _matmul(dyc, w["w_conv_o"], "nt", F32, nm("conv_out_dx"))
    gb["w_conv_o"] = _matmul(sv["hc"], dyc, "tn", BF16, nm("conv_out_dw"), blocked=True)
    dco, gs["conv_ln_g"], gs["conv_ln_b"], gs["conv_b"] = _conv_bwd_norm(dhc, sv["co"], sm["conv_ln_g"], sm["conv_ln_b"],
                                                                        nm("conv_bwd_norm"))
    dz, gs["conv_w"] = _conv_bwd_taps(dco, sv["z"], sm["conv_w"], dz, nm("conv_bwd_taps"))
    gb["w_attn_o"] = _matmul(sv["o"], dya, "tn", BF16, nm("attn_out_dw"), blocked=True)
    delta, dob = _attn_out_bwd(dya, w["w_attn_o"], sv["o"], nm("attn_out_bwd"))
    (dq, dk, dv), rode = _flash_bwd(sv["q"], sv["k"], sv["v"], dob, sv["lse"], delta, nm("flash_bwd"),
                                  plan.bwd_ride(l, gb))
    plan.bwd_done(l, rode)
    dqf, dkf, dz, gs["q_norm"], gs["kv_norm"] = _qkv_up_bwd(
        dq, dk, dv, sv["z"], w["w_uq"], w["w_uk"], w["w_uv"], tables, sm["q_norm"], sm["kv_norm"], dz, nm("qkv_up_bwd"))
    gb["w_uq"] = _matmul(sv["cq"], dqf, "tn", BF16, nm("q_up_dw"), blocked=True)
    gb["w_uk"] = _matmul(sv["ckv"], dkf, "tn", BF16, nm("k_up_dw"), blocked=True)
    gb["w_uv"] = _matmul(sv["ckv"], dv, "tn", BF16, nm("v_up_dw"), blocked=True)
    gb["w_in"] = _matmul(sv["h"], dz, "tn", BF16, nm("in_proj_dw"))
    plan.add_grads(l, "mix", gb)
    ride, small_gathered = plan.tail_ride(l, pack_small(gs)), []
    if ride is None:
        dh = _matmul(dz, w["w_in"], "nt", F32, nm("in_proj_dx"))
    else:
        dh, rode = _matmul(dz, w["w_in"], "nt", F32, nm("in_proj_dx"), ride=ride)
        small_gathered = plan.tail_done(l, rode)
    dx, gs["mix_norm_pre"] = _rms_bwd(sv["x"], (D_MODEL, 0), sm["mix_norm_pre"], dh, F32, nm("mix_pre_norm_bwd"), add=dmid)
    return dx, gs, small_gathered


def _part_groups(part):
    return {"mix": MIX_GROUPS, "ffn": FFN_GROUPS, "early": MIX_EARLY, "late": MIX_LATE}[part]


class _Plan:
    def __init__(self, shards, conv_w):
        self.local = [_local_groups(shards, l) for l in range(DEPTH)]
        self.conv_w = conv_w
        self.gat, self.send, self.recv = {}, {}, {}

    @staticmethod
    def _riders(l):
        return [(l, "ffn")] + ([(l + 1, "mix")] if l + 1 < DEPTH else [])

    @staticmethod
    def _grad_riders(l):
        return [(l, "ffn"), (l, "early")] + ([(l + 1, "late")] if l + 1 < DEPTH else [])

    def gather_first(self):
        w_in, conv_w = _all_gather([self.local[0]["w_in"], self.conv_w], "gather_w_in_l0")
        self.gat[(0, "mix")] = {"w_in": w_in}
        return conv_w

    def w_in(self, l):
        return _arrange_w_in(self.gat[(l, "mix")]["w_in"])

    def in_proj_ride(self, l):
        return _GatherRide([self.local[0][g] for g in MIX_GROUPS[1:]]) if l == 0 else None

    def in_proj_done(self, l, outs):
        self.gat[(l, "mix")].update(zip(MIX_GROUPS[1:], outs))

    def fwd_ride(self, l):
        return _GatherRide([self.local[ll][g] for ll, part in self._riders(l) for g in _part_groups(part)])

    def fwd_done(self, l, outs):
        outs = list(outs)
        for ll, part in self._riders(l):
            self.gat[(ll, part)] = {g: outs.pop(0) for g in _part_groups(part)}

    def mixer_weights(self, l):
        return _mixer_weights(self.gat[(l, "mix")])

    def ffn_weights(self, l):
        return _ffn_weights(self.gat[(l, "ffn")])

    def add_grads(self, l, part, gb):
        if part == "ffn":
            self.send[(l, "ffn")] = _ffn_grad_groups(gb)
        else:
            self.send.setdefault((l, "late"), {}).update(_mixer_grad_groups({g: gb[g] for g in MIX_LATE if g in gb}))

    def bwd_ride(self, l, gb_early):
        self.send[(l, "early")] = _mixer_grad_groups({g: gb_early[g] for g in MIX_EARLY})
        return _ReduceRide([self.send[(ll, part)][g] for ll, part in self._grad_riders(l) for g in _part_groups(part)])

    def bwd_done(self, l, outs):
        outs = list(outs)
        for ll, part in self._grad_riders(l):
            self.recv[(ll, part)] = {g: outs.pop(0) for g in _part_groups(part)}

    def tail_ride(self, l, small_groups):
        if l > 0:
            return None
        send = [self.send[(0, "late")][g] for g in MIX_LATE]
        by_core = [a.reshape((4, 2) + a.shape[1:]).transpose((1, 0) + tuple(range(2, a.ndim + 1))) for a in send]
        core = lax.axis_index("c")
        own = [lax.dynamic_index_in_dim(a, core, axis=0, keepdims=False) for a in by_core]
        got = _swap_with_sibling(by_core, "reduce_d2d")
        pairs = [_add_pairs(a, b, f"reduce_pair_add_{g}") for g, a, b in zip(MIX_LATE, own, got)]
        return _Combo([_ChipExchangeRide(pairs), _GatherRide(small_groups)])

    def tail_done(self, l, outs):
        self.recv[(l, "late")] = dict(zip(MIX_LATE, outs[:len(MIX_LATE)]))
        return outs[len(MIX_LATE):]

    def finish(self):
        per_layer = [{g: a for part in ("early", "late", "ffn") for g, a in self.recv[(l, part)].items()}
                     for l in range(DEPTH)]
        return _grads_from_groups({g: _sum_layers([per_layer[l][g] for l in range(DEPTH)], f"reduce_sum_{g}")
                                   for g in BIG})


def _local_step(x, positions, target, smalls, plan):
    tables = _rope_tables(positions)
    saved = []
    h, h_norm = x, None
    for l in range(DEPTH):
        h, h2, svm, wm = _mixer_fwd(h, h_norm, tables, smalls[l], plan, l)
        wf = plan.ffn_weights(l)
        next_gain = smalls[l + 1]["mix_norm_pre"] if l + 1 < DEPTH else None
        h, h_norm, svf = _ffn_fwd(h, h2, wf, smalls[l], f"l{l}", next_gain)
        saved.append((svm, svf, wm, wf))
    dy, sq = _loss_grad(h, target, "loss_grad")
    small = [None] * DEPTH
    for l in reversed(range(DEPTH)):
        svm, svf, wm, wf = saved[l]
        dmid, gbf, gsf = _ffn_bwd(dy, svf, wf, smalls[l], f"l{l}")
        plan.add_grads(l, "ffn", gbf)

        def pack_small(gs, l=l, gsf=gsf):
            if l > 0:
                return None
            return _small_groups([{**gsf, **gs, "mix_norm_pre": jnp.zeros((D_MODEL,), F32)}] + small[1:])

        dy, gsm, small_gathered = _mixer_bwd(dmid, svm, tables, wm, smalls[l], plan, l, pack_small)
        small[l] = {**gsf, **gsm}
    return sq, dy, small, small_gathered


def kernel(x, positions, mix_norm_pre, w_in, q_norm, w_uq, kv_norm, w_uk, w_uv, w_attn_o, conv_w, conv_b, conv_ln_g, conv_ln_b, w_conv_o, pool_w, pool_scale, w_pool_o, w_mix_o, mix_norm_post, ffn_norm_pre, w_gate, w_up, w_down, ffn_norm_post, loss_target, m_mix_norm_pre, m_w_in, m_q_norm, m_w_uq, m_kv_norm, m_w_uk, m_w_uv, m_w_attn_o, m_conv_w, m_conv_b, m_conv_ln_g, m_conv_ln_b, m_w_conv_o, m_pool_w, m_pool_scale, m_w_pool_o, m_w_mix_o, m_mix_norm_post, m_ffn_norm_pre, m_w_gate, m_w_up, m_w_down, m_ffn_norm_post, v_mix_norm_pre, v_w_in, v_q_norm, v_w_uq, v_kv_norm, v_w_uk, v_w_uv, v_w_attn_o, v_conv_w, v_conv_b, v_conv_ln_g, v_conv_ln_b, v_w_conv_o, v_pool_w, v_pool_scale, v_w_pool_o, v_w_mix_o, v_mix_norm_post, v_ffn_norm_pre, v_w_gate, v_w_up, v_w_down, v_ffn_norm_post):
    given = dict(locals())
    dev = 4 * lax.axis_index("x") + 2 * lax.axis_index("y") + lax.axis_index("c")

    plan = _Plan({n: given[n] for n in BIG}, conv_w)
    cw = CONV_C // N_DEV
    conv_w_full = plan.gather_first().transpose(1, 2, 0, 3).reshape(DEPTH, CONV_W, CONV_C)
    smalls = []
    for l in range(DEPTH):
        sm = {n: given[n][l] for n in SMALL if n != "conv_w"}
        sm["conv_w"] = _pad_axis(conv_w_full[l], 0, CONV_HALO)
        smalls.append(sm)

    sq, grad_x, small, small_groups = _local_step(x[0], positions[0], loss_target[0], smalls, plan)
    loss = lax.psum(0.5 / D_MODEL * jnp.sum(sq), ("x", "y", "c"))
    views = {n: lax.optimization_barrier(_lane_major(n, g)) for n, g in plan.finish().items()}
    grads = {n: _from_lane_major(n, views[n]) for n in BIG}

    small_sum = _small_from_groups([_sum_blocks(g, f"sum_small_grads_{i}") for i, g in enumerate(small_groups)])
    last = _pad_axis(small[0]["mix_norm_pre"].reshape(1, D_MODEL), 0, SUBLANES)
    last_sum = _sum_blocks(_all_gather([last], "gather_last_norm_grad")[0], "sum_last_norm_grad")[0]
    small_sum["mix_norm_pre"] = small_sum["mix_norm_pre"].at[0].set(last_sum)
    for n in SMALL:
        grads[n] = small_sum[n]
    grads["conv_w"] = lax.dynamic_slice_in_dim(small_sum["conv_w"], dev * cw, cw, axis=2)

    delta, new_m, new_v = {}, {}, {}
    for n in WEIGHTS:
        g_view = views[n] if n in views else grads[n]
        w_view, m_view, v_view = [_lane_major(n, given[k]) for k in (n, "m_" + n, "v_" + n)]
        res = _adamw(w_view, g_view, m_view, v_view, f"adamw_{n}")
        delta[n], new_m[n], new_v[n] = [_from_lane_major(n, r) for r in res]
    return (loss, grad_x[None], *[grads[n] for n in WEIGHTS], *[delta[n] for n in WEIGHTS],
            *[new_m[n] for n in WEIGHTS], *[new_v[n] for n in WEIGHTS])
```

```python
import functools
import math

import jax
import jax.numpy as jnp
from jax import lax
from jax.experimental import pallas as pl
from jax.experimental.pallas import tpu as pltpu

F32, BF16 = jnp.float32, jnp.bfloat16
MESH = pl.DeviceIdType.MESH

LANES = 128
SUBLANES = 8
VMEM_LIMIT_BYTES = 56 * 1024 * 1024
MATMUL_VMEM_BYTES = 40 * 1024 * 1024

N_DEV = 8
D_MODEL = 1024
DEPTH = 2
N_HEADS = 8
QK_NOPE, QK_ROPE, V_HEAD = 64, 32, 64
HEAD_PAD = LANES
Q_RANK, KV_RANK = 384, 256
ROPE_THETA = 10000.0
CONV_C, CONV_W = 512, 31
CONV_HALO = 32
POOL_WINDOWS = (2, 4, 8, 16)
POOL_C, POOL_G = 512, 4
POOL_GD = POOL_C // POOL_G
D_FF = 2816
FF_SHARD = D_FF // N_DEV
FF_SHARD_PAD = 3 * LANES
D_FF_PAD = N_DEV * FF_SHARD_PAD
W_IN_SHARD = 660
EPS = 1e-6
ATTN_SCALE = 1.0 / math.sqrt(QK_NOPE + QK_ROPE)
LOG2E = 1.4426950408889634
LR, B1, B2, ADAM_EPS, WD, STEP = 0.001, 0.9, 0.999, 1e-08, 0.01, 10

Z_W = 5376
ZC_GATE = (1024, 0)
ZC_GATES = (3072, 0)
ZC_CONV_A = (512, 6)
ZC_CONV_G = (512, 7)
ZC_CONV = (1024, 3)
ZC_POOL = (512, 8)
ZC_Q = (384, 12)
ZC_KR = (128, 39)
ZC_KV = (256, 20)
ZC_QKR = (768, 6)
W_IN_PIECES = ((0, 384, 4608), (384, 640, 5120), (640, 672, 5056), (672, 1696, 3072), (1696, 2208, 4096),
               (2208, 5280, 0))

BIG = ("w_in", "w_uq", "w_uk", "w_uv", "w_attn_o", "w_conv_o", "w_pool_o", "w_mix_o", "w_gate", "w_up", "w_down")
SMALL = ("mix_norm_pre", "q_norm", "kv_norm", "conv_w", "conv_b", "conv_ln_g", "conv_ln_b", "pool_w", "pool_scale",
         "mix_norm_post", "ffn_norm_pre", "ffn_norm_post")
WEIGHTS = ("mix_norm_pre", "w_in", "q_norm", "w_uq", "kv_norm", "w_uk", "w_uv", "w_attn_o", "conv_w", "conv_b",
           "conv_ln_g", "conv_ln_b", "w_conv_o", "pool_w", "pool_scale", "w_pool_o", "w_mix_o", "mix_norm_post",
           "ffn_norm_pre", "w_gate", "w_up", "w_down", "ffn_norm_post")


def _params(*semantics):
    return pltpu.CompilerParams(dimension_semantics=semantics, vmem_limit_bytes=VMEM_LIMIT_BYTES)


def _tile(dim, cap):
    if dim <= cap:
        return dim
    for t in range(cap - cap % LANES, 0, -LANES):
        if dim % t == 0:
            return t
    raise ValueError(f"no tile for {dim} under {cap}")


def _row_tile(rows, row_bytes, budget=1 << 20):
    if rows * row_bytes <= budget:
        return rows
    cap = max(16, budget // row_bytes)
    for t in range(cap - cap % 16, 0, -16):
        if rows % t == 0:
            return t
    return rows


def _rows(ts, width, cidx=0):
    return pl.BlockSpec((ts, width), lambda i: (i, cidx))


def _fixed(shape):
    return pl.BlockSpec(shape, lambda *_: (0,) * len(shape))


def _sigmoid(x):
    return 1.0 / (1.0 + jnp.exp(-x))


def _matmul(a, b, mode, out_dtype, name, add=None, blocked=False, ride=None):
    nb = n_blk = 0
    blocked = blocked or b.ndim == 3
    if mode == "nn":
        (m, k) = a.shape
        n = b.shape[0] * b.shape[2] if blocked else b.shape[1]
    elif mode == "nt":
        (m, k) = a.shape
        n = b.shape[1] if blocked else b.shape[0]
    else:
        (k, m), n = a.shape, b.shape[1]
    if blocked:
        nb = b.shape[2] if mode != "tn" else n // N_DEV
    unit = nb if blocked and mode != "nt" else LANES
    out_bytes = jnp.dtype(out_dtype).itemsize + (4 if add is not None else 0)
    best = None
    for tn_c in range(unit, min(n, 1536) + 1, unit):
        for tm_c in sorted({256, 512, 1024, 2048, min(m, 2048)}):
            if n % tn_c or m % tm_c or (blocked and mode != "nt" and N_DEV % (tn_c // nb)):
                continue
            vmem = 2 * (tm_c * k * 2 + tn_c * k * 2 + tm_c * tn_c * out_bytes) + tm_c * tn_c * 4 + tn_c * k * 2
            if vmem <= MATMUL_VMEM_BYTES and (best is None or tm_c * tn_c / (tm_c + tn_c) > best[0]):
                best = (tm_c * tn_c / (tm_c + tn_c), tm_c, tn_c)
    if best is None:
        raise ValueError(f"{name}: no tiles for {m}x{n}x{k}")
    _, tm, tn = best
    if blocked:
        n_blk = N_DEV if mode == "nt" else tn // nb
    dims = {"nn": ((1,), (0,)), "nt": ((1,), (1,)), "tn": ((0,), (0,))}[mode]
    a_spec = pl.BlockSpec((k, tm), lambda i, j: (0, i)) if mode == "tn" else pl.BlockSpec((tm, k), lambda i, j: (i, 0))
    b_spec = pl.BlockSpec((tn, k), lambda i, j: (j, 0)) if mode == "nt" else pl.BlockSpec((k, tn), lambda i, j: (0, j))
    o_spec = pl.BlockSpec((tm, tn), lambda i, j: (i, j))
    out_shape = jax.ShapeDtypeStruct((m, n), out_dtype)
    if blocked and mode == "nn":
        b_spec = pl.BlockSpec((n_blk, k, nb), lambda i, j: (j, 0, 0))
    elif blocked and mode == "nt":
        b_spec = pl.BlockSpec((n_blk, tn, nb), lambda i, j: (0, j, 0))
    elif blocked:
        o_spec = pl.BlockSpec((n_blk, tm, nb), lambda i, j: (j, i, 0))
        out_shape = jax.ShapeDtypeStruct((N_DEV, m, nb), out_dtype)
    has_add = add is not None
    grid = (m // tm, n // tn)

    def body(*refs):
        (a_ref, b_ref, *rest), start, finish = _ride_hooks(ride, refs, 3 if has_add else 2, 1, grid)
        start()
        o_ref = rest[-1]
        if blocked and mode != "tn":
            bv = jnp.concatenate([b_ref[c] for c in range(n_blk)], axis=1) if n_blk > 1 else b_ref[0]
        else:
            bv = b_ref[...]
        total = lax.dot_general(a_ref[...], bv, (dims, ((), ())), preferred_element_type=F32)
        if has_add:
            total = total + rest[0][...]
        if blocked and mode == "tn":
            for c in range(n_blk):
                o_ref[c] = total[:, c * nb:(c + 1) * nb].astype(o_ref.dtype)
        else:
            o_ref[...] = total.astype(o_ref.dtype)
        finish()

    operands = (a, b, add) if has_add else (a, b)
    (out,), rode = _ride_call(ride, body, name, (out_shape,), grid, [a_spec, b_spec] + ([o_spec] if has_add else []),
                              (o_spec,), ("parallel", "parallel"), operands)
    return out if ride is None else (out, rode)


def _rms_fwd(x, win, gain, out_dtype, name, res=None, then=None):
    width, cidx = win
    s = x.shape[0]
    ts = min(s, 512)
    has_res, has_then = res is not None, then is not None

    def norm(v, g_ref):
        return (v * lax.rsqrt(jnp.mean(v * v, axis=-1, keepdims=True) + EPS)) * g_ref[...]

    def body(x_ref, g_ref, *rest):
        y = norm(x_ref[...].astype(F32), g_ref)
        if has_res:
            y = rest[0][...] + y
        o_ref = rest[-2] if has_then else rest[-1]
        o_ref[...] = y.astype(o_ref.dtype)
        if has_then:
            rest[-1][...] = norm(y, rest[-3]).astype(BF16)

    ops = (x, gain.reshape(1, width)) + ((res,) if has_res else ()) + ((then.reshape(1, width),) if has_then else ())
    out_shape = (jax.ShapeDtypeStruct((s, width), out_dtype),) + ((jax.ShapeDtypeStruct((s, width), BF16),) * has_then)
    out = pl.pallas_call(
        body, name=name, out_shape=out_shape, grid=(s // ts,),
        in_specs=([_rows(ts, width, cidx), _fixed((1, width))] + ([_rows(ts, width)] if has_res else [])
                  + ([_fixed((1, width))] if has_then else [])),
        out_specs=(_rows(ts, width),) * len(out_shape), compiler_params=_params("parallel"))(*ops)
    return out if has_then else out[0]


def _into(dz, n_inputs, out_index):
    return dict(in_specs=[ANY], operands=(dz,), input_output_aliases={n_inputs: out_index},
                out_shape=jax.ShapeDtypeStruct(dz.shape, dz.dtype))


def _rms_bwd(x, win, gain, dy, out_dtype, name, add=None, dz=None):
    width, cidx = win
    s = x.shape[0]
    ts = min(s, 512)
    has_add = add is not None

    def body(x_ref, g_ref, dy_ref, *rest):
        dx_ref, dg_ref = rest[-2], rest[-1]
        xv = x_ref[...].astype(F32)
        r = lax.rsqrt(jnp.mean(xv * xv, axis=-1, keepdims=True) + EPS)
        xh = xv * r
        dyv = dy_ref[...].astype(F32)
        dyg = dyv * g_ref[...]
        dx = r * (dyg - xh * jnp.mean(dyg * xh, axis=-1, keepdims=True))
        if has_add:
            dx = dx + rest[0][...]
        dx_ref[...] = dx.astype(dx_ref.dtype)

        @pl.when(pl.program_id(0) == 0)
        def _():
            dg_ref[...] = jnp.zeros_like(dg_ref)

        dg_ref[...] += jnp.sum(dyv * xh, axis=0, keepdims=True)

    ops = (x, gain.reshape(1, width), dy) + ((add,) if has_add else ())
    in_specs = [_rows(ts, width, cidx), _fixed((1, width)), _rows(ts, width)] + ([_rows(ts, width)] if has_add else [])
    dx_shape, dx_spec, alias = jax.ShapeDtypeStruct((s, width), out_dtype), _rows(ts, width), {}
    if dz is not None:
        into = _into(dz, len(ops), 0)
        ops, in_specs, alias = ops + into["operands"], in_specs + into["in_specs"], into["input_output_aliases"]
        dx_shape, dx_spec = into["out_shape"], _rows(ts, width, cidx)
    dx, dg = pl.pallas_call(
        body, name=name, out_shape=(dx_shape, jax.ShapeDtypeStruct((1, width), F32)), grid=(s // ts,),
        in_specs=in_specs, out_specs=(dx_spec, _fixed((1, width))), input_output_aliases=alias,
        compiler_params=_params("arbitrary"))(*ops)
    return dx, dg.reshape(width)


def _rope(x, c, s1, s2):
    return x * c + pltpu.roll(x, 16, 1) * s1 + pltpu.roll(x, LANES - 16, 1) * s2


def _rope_t(g, c, s1, s2):
    return g * c + pltpu.roll(g * s1, LANES - 16, 1) + pltpu.roll(g * s2, 16, 1)


def _rope_tables(positions):
    inv_freq = ROPE_THETA ** (-jnp.arange(0, QK_ROPE, 2, dtype=F32) / QK_ROPE)
    ang = positions.astype(F32)[:, None] * inv_freq
    cos, sin = jnp.cos(ang), jnp.sin(ang)
    n = positions.shape[0]
    one, zero = jnp.ones((n, 1), F32), jnp.zeros((n, 1), F32)
    c = jnp.concatenate([jnp.tile(one, (1, QK_NOPE)), cos, cos, jnp.tile(one, (1, 32))], axis=1)
    s1 = jnp.concatenate([jnp.tile(zero, (1, QK_NOPE + 16)), sin, jnp.tile(zero, (1, 32))], axis=1)
    s2 = jnp.concatenate([jnp.tile(zero, (1, QK_NOPE)), -sin, jnp.tile(zero, (1, 48))], axis=1)
    return c, s1, s2


def _qkv_up_fwd(z, q_gain, kv_gain, w_uq, w_uk, w_uv, tables, name):
    s = z.shape[0]
    ts = min(s, 512)
    hw = N_HEADS * HEAD_PAD
    kv0 = Q_RANK + LANES

    def norm(v, g_ref):
        return ((v * lax.rsqrt(jnp.mean(v * v, axis=-1, keepdims=True) + EPS)) * g_ref[...]).astype(BF16)

    def body(z_ref, gq_ref, gkv_ref, wq_ref, wk_ref, wv_ref, c_ref, s1_ref, s2_ref, cq_ref, ckv_ref, q_ref, k_ref, v_ref):
        c, s1, s2 = c_ref[...], s1_ref[...], s2_ref[...]
        cqv = norm(z_ref[:, pl.ds(0, Q_RANK)].astype(F32), gq_ref)
        ckvv = norm(z_ref[:, pl.ds(kv0, KV_RANK)].astype(F32), gkv_ref)
        cq_ref[...] = cqv
        ckv_ref[...] = ckvv
        kr = _rope(z_ref[:, pl.ds(Q_RANK, LANES)].astype(F32), c, s1, s2)
        for h in range(N_HEADS):
            sl = slice(h * HEAD_PAD, (h + 1) * HEAD_PAD)
            q_ref[:, sl] = _rope(jnp.dot(cqv, wq_ref[h], preferred_element_type=F32), c, s1, s2).astype(BF16)
            k_ref[:, sl] = (jnp.dot(ckvv, wk_ref[h], preferred_element_type=F32) + kr).astype(BF16)
            v_ref[:, sl] = jnp.dot(ckvv, wv_ref[h], preferred_element_type=F32).astype(BF16)

    tab = _rows(ts, LANES)
    wide = jax.ShapeDtypeStruct((s, hw), BF16)
    return pl.pallas_call(
        body, name=name,
        out_shape=(jax.ShapeDtypeStruct((s, Q_RANK), BF16), jax.ShapeDtypeStruct((s, KV_RANK), BF16), wide, wide, wide),
        grid=(s // ts,),
        in_specs=[_rows(ts, *ZC_QKR), _fixed((1, Q_RANK)), _fixed((1, KV_RANK)), _fixed(w_uq.shape), _fixed(w_uk.shape),
                  _fixed(w_uv.shape), tab, tab, tab],
        out_specs=(_rows(ts, Q_RANK), _rows(ts, KV_RANK)) + (_rows(ts, hw),) * 3, compiler_params=_params("parallel"))(
            z, q_gain.reshape(1, -1), kv_gain.reshape(1, -1), w_uq, w_uk, w_uv, *tables)


def _qkv_up_bwd(dq, dk, dv, z, w_uq, w_uk, w_uv, tables, q_gain, kv_gain, dz, name):
    s = dq.shape[0]
    ts = min(s, 512)
    hw = N_HEADS * HEAD_PAD
    zw = ZC_QKR[0]
    kv0 = Q_RANK + LANES
    dims_nt = (((1,), (1,)), ((), ()))

    def norm_bwd(xv, g_ref, dyv):
        r = lax.rsqrt(jnp.mean(xv * xv, axis=-1, keepdims=True) + EPS)
        xh = xv * r
        dyg = dyv * g_ref[...]
        return r * (dyg - xh * jnp.mean(dyg * xh, axis=-1, keepdims=True)), jnp.sum(dyv * xh, axis=0, keepdims=True)

    def body(dq_ref, dk_ref, dv_ref, z_ref, wq_ref, wk_ref, wv_ref, c_ref, s1_ref, s2_ref, gq_ref, gkv_ref, _,
             dqf_ref, dkf_ref, dz_ref, dgq_ref, dgkv_ref):
        c, s1, s2 = c_ref[...], s1_ref[...], s2_ref[...]
        ksum = jnp.zeros((ts, HEAD_PAD), F32)
        dcq = jnp.zeros((ts, Q_RANK), F32)
        dckv = jnp.zeros((ts, KV_RANK), F32)
        for h in range(N_HEADS):
            sl = slice(h * HEAD_PAD, (h + 1) * HEAD_PAD)
            dqh = _rope_t(dq_ref[:, sl], c, s1, s2).astype(BF16)
            dkv = dk_ref[:, sl]
            dkh = dkv.astype(BF16)
            dqf_ref[:, sl] = dqh
            dkf_ref[:, sl] = dkh
            ksum = ksum + dkv
            dcq = dcq + lax.dot_general(dqh, wq_ref[h], dims_nt, preferred_element_type=F32)
            dckv = dckv + (lax.dot_general(dkh, wk_ref[h], dims_nt, preferred_element_type=F32)
                           + lax.dot_general(dv_ref[:, sl], wv_ref[h], dims_nt, preferred_element_type=F32))
        dxq, dgq = norm_bwd(z_ref[:, pl.ds(0, Q_RANK)].astype(F32), gq_ref, dcq)
        dxkv, dgkv = norm_bwd(z_ref[:, pl.ds(kv0, KV_RANK)].astype(F32), gkv_ref, dckv)
        lane = lax.broadcasted_iota(jnp.int32, (ts, HEAD_PAD), 1)
        in_rope = (lane >= QK_NOPE) & (lane < QK_NOPE + QK_ROPE)
        dz_ref[:, pl.ds(0, Q_RANK)] = dxq.astype(BF16)
        dz_ref[:, pl.ds(Q_RANK, LANES)] = jnp.where(in_rope, _rope_t(ksum, c, s1, s2), 0.0).astype(BF16)
        dz_ref[:, pl.ds(kv0, KV_RANK)] = dxkv.astype(BF16)

        @pl.when(pl.program_id(0) == 0)
        def _():
            dgq_ref[...] = jnp.zeros_like(dgq_ref)
            dgkv_ref[...] = jnp.zeros_like(dgkv_ref)

        dgq_ref[...] += dgq
        dgkv_ref[...] += dgkv

    tab = _rows(ts, LANES)
    into = _into(dz, 12, 2)
    dqf, dkf, dz, dgq, dgkv = pl.pallas_call(
        body, name=name,
        out_shape=(jax.ShapeDtypeStruct((s, hw), BF16), jax.ShapeDtypeStruct((s, hw), BF16), into["out_shape"],
                   jax.ShapeDtypeStruct((1, Q_RANK), F32), jax.ShapeDtypeStruct((1, KV_RANK), F32)),
        grid=(s // ts,),
        in_specs=[_rows(ts, hw), _rows(ts, hw), _rows(ts, hw), _rows(ts, *ZC_QKR), _fixed(w_uq.shape), _fixed(w_uk.shape),
                  _fixed(w_uv.shape), tab, tab, tab, _fixed((1, Q_RANK)), _fixed((1, KV_RANK))] + into["in_specs"],
        out_specs=(_rows(ts, hw), _rows(ts, hw), _rows(ts, *ZC_QKR), _fixed((1, Q_RANK)), _fixed((1, KV_RANK))),
        input_output_aliases=into["input_output_aliases"], compiler_params=_params("arbitrary"))(
            dq, dk, dv, z, w_uq, w_uk, w_uv, *tables, q_gain.reshape(1, -1), kv_gain.reshape(1, -1), dz)
    return dqf, dkf, dz, dgq.reshape(-1), dgkv.reshape(-1)


def _attn_tile(s):
    return min(s, 512)


def _raw_scores(q, k, masked, row0=0):
    sc = lax.dot_general(q, k, (((1,), (1,)), ((), ())), preferred_element_type=F32)
    if masked:
        rows = row0 + lax.broadcasted_iota(jnp.int32, sc.shape, 0)
        cols = lax.broadcasted_iota(jnp.int32, sc.shape, 1)
        sc = jnp.where(cols <= rows, sc, -jnp.inf)
    return sc


def _ride_hooks(ride, refs, n_in, n_out, grid):
    if ride is None:
        return refs, lambda: None, lambda: None
    n = len(ride.arrays)
    own = refs[:n_in] + refs[n_in + n:n_in + n + n_out]
    ins, outs, sems = refs[n_in:n_in + n], refs[n_in + n + n_out:n_in + 2 * n + n_out], refs[n_in + 2 * n + n_out:]
    at_first = functools.reduce(lambda a, b: a & b, [pl.program_id(ax) == 0 for ax in range(len(grid))])
    at_last = functools.reduce(lambda a, b: a & b, [pl.program_id(ax) == g - 1 for ax, g in enumerate(grid)])
    return own, lambda: pl.when(at_first)(lambda: ride.start(ins, outs, sems)), \
        lambda: pl.when(at_last)(lambda: ride.finish(ins, outs, sems))


def _ride_call(ride, body, name, out_shape, grid, in_specs, out_specs, semantics, operands):
    n = 0 if ride is None else len(ride.arrays)
    res = pl.pallas_call(
        body, name=name, out_shape=tuple(out_shape) + (tuple(ride.out_shape) if n else ()), grid=grid,
        in_specs=list(in_specs) + [ANY] * n, out_specs=tuple(out_specs) + (ANY,) * n,
        scratch_shapes=list(ride.scratch) if n else [],
        compiler_params=_params(*(("arbitrary",) * len(grid) if n else semantics)))(*operands, *(ride.arrays if n else ()))
    return res[:len(out_shape)], list(res[len(out_shape):])


def _flash_fwd(q, k, v, name, ride=None):
    s = q.shape[0]
    t = _attn_tile(s)
    c2 = ATTN_SCALE * LOG2E
    grid = (N_HEADS, s // t)

    def body(*refs):
        (q_ref, k_ref, v_ref, o_ref, lse_ref), start, finish = _ride_hooks(ride, refs, 3, 2, grid)
        start()
        i = pl.program_id(1)
        qv = q_ref[...]

        def chunk(j, carry, masked):
            m_old, l_old, acc = carry
            at = pl.ds(pl.multiple_of(j * t, t), t)
            sc = _raw_scores(qv, k_ref[at, :], masked)
            m_new = jnp.maximum(m_old, jnp.max(sc, axis=-1, keepdims=True))
            p = jnp.exp2((sc - m_new) * c2)
            alpha = jnp.exp2((m_old - m_new) * c2)
            l_new = alpha * l_old + jnp.sum(p, axis=-1, keepdims=True)
            acc = alpha * acc + jnp.dot(p.astype(BF16), v_ref[at, :], preferred_element_type=F32)
            return m_new, l_new, acc

        init = (jnp.full((t, 1), -jnp.inf, F32), jnp.zeros((t, 1), F32), jnp.zeros((t, HEAD_PAD), F32))
        carry = lax.fori_loop(0, i, lambda j, cr: chunk(j, cr, False), init)
        m_fin, l_fin, acc = chunk(i, carry, True)
        o_ref[...] = (acc / l_fin).astype(o_ref.dtype)
        lse_ref[...] = jnp.broadcast_to(m_fin * ATTN_SCALE + jnp.log(l_fin), (t, HEAD_PAD))
        finish()

    qo = pl.BlockSpec((t, HEAD_PAD), lambda h, i: (i, h))
    whole = pl.BlockSpec((s, HEAD_PAD), lambda h, i: (0, h))
    return _ride_call(
        ride, body, name, (jax.ShapeDtypeStruct(q.shape, BF16), jax.ShapeDtypeStruct(q.shape, F32)), grid,
        [qo, whole, whole], (qo, qo), ("parallel", "parallel"), (q, k, v))


def _attn_out_bwd(dya, w_attn_o, o, name):
    s, d = dya.shape
    hw = N_HEADS * HEAD_PAD
    t = _attn_tile(s)

    def body(d_ref, w_ref, o_ref, delta_ref, dob_ref):
        wv = jnp.concatenate([w_ref[c] for c in range(N_DEV)], axis=1)
        do = lax.dot_general(d_ref[...], wv, (((1,), (1,)), ((), ())), preferred_element_type=F32)
        for h in range(N_HEADS):
            sl = slice(h * HEAD_PAD, (h + 1) * HEAD_PAD)
            dov = do[:, sl]
            delta_ref[:, sl] = jnp.broadcast_to(jnp.sum(dov * o_ref[:, sl].astype(F32), axis=-1, keepdims=True),
                                                (t, HEAD_PAD))
            dob_ref[:, sl] = dov.astype(BF16)

    blk = _rows(t, hw)
    return pl.pallas_call(
        body, name=name, out_shape=(jax.ShapeDtypeStruct(o.shape, F32), jax.ShapeDtypeStruct(o.shape, BF16)),
        grid=(s // t,), in_specs=[_rows(t, d), _fixed(w_attn_o.shape), blk], out_specs=(blk, blk),
        compiler_params=_params("parallel"))(dya, w_attn_o, o)


def _flash_bwd(q, k, v, do, lse, delta, name, ride=None):
    s = q.shape[0]
    t = _attn_tile(s)
    nt = s // t
    c2 = ATTN_SCALE * LOG2E
    grid = (N_HEADS, nt)

    def body(*refs):
        (q_ref, k_ref, v_ref, do_ref, lse_ref, delta_ref, dq_ref, dk_ref, dv_ref), start, finish = _ride_hooks(
            ride, refs, 6, 3, grid)
        start()
        j = pl.program_id(1)
        kv, vv = k_ref[...], v_ref[...]

        @pl.when(j == 0)
        def _():
            dq_ref[...] = jnp.zeros_like(dq_ref)

        def chunk(i, carry, masked):
            dk_acc, dv_acc = carry
            at = pl.ds(pl.multiple_of(i * t, t), t)
            qi, doi = q_ref[at, :], do_ref[at, :]
            sc = _raw_scores(qi, kv, masked)
            p = jnp.exp2(sc * c2 - lse_ref[at, pl.ds(0, 1)] * LOG2E)
            dp = lax.dot_general(doi, vv, (((1,), (1,)), ((), ())), preferred_element_type=F32)
            ds = (p * (dp - delta_ref[at, pl.ds(0, 1)])).astype(BF16)
            dv_acc = dv_acc + lax.dot_general(p.astype(BF16), doi, (((0,), (0,)), ((), ())), preferred_element_type=F32)
            dk_acc = dk_acc + lax.dot_general(ds, qi, (((0,), (0,)), ((), ())), preferred_element_type=F32)
            dq_ref[at, :] += jnp.dot(ds, kv, preferred_element_type=F32) * ATTN_SCALE
            return dk_acc, dv_acc

        zero = jnp.zeros((t, HEAD_PAD), F32)
        carry = chunk(j, (zero, zero), True)
        dk_acc, dv_acc = lax.fori_loop(j + 1, nt, lambda i, cr: chunk(i, cr, False), carry)
        dk_ref[...] = dk_acc * ATTN_SCALE
        dv_ref[...] = dv_acc.astype(BF16)
        finish()

    blk = pl.BlockSpec((t, HEAD_PAD), lambda h, j: (j, h))
    whole = pl.BlockSpec((s, HEAD_PAD), lambda h, j: (0, h))
    return _ride_call(
        ride, body, name, (jax.ShapeDtypeStruct(q.shape, F32), jax.ShapeDtypeStruct(q.shape, F32),
                           jax.ShapeDtypeStruct(q.shape, BF16)), grid,
        [whole, blk, blk, whole, whole, whole], (whole, blk, blk), ("parallel", "arbitrary"), (q, k, v, do, lse, delta))


def _conv_tile(s):
    return min(s, 256)


def _halo_before(t, width, cidx):
    per = t // CONV_HALO
    return pl.BlockSpec((CONV_HALO, width), lambda i: (jnp.maximum(i * per - 1, 0), cidx))


def _halo_after(t, width, cidx, n_tiles):
    per = t // CONV_HALO
    last = n_tiles * per - 1
    return pl.BlockSpec((CONV_HALO, width), lambda i: (jnp.minimum((i + 1) * per, last), cidx))


def _fill_glu(hbuf, ap_ref, gp_ref, a_ref, g_ref, t):
    first = pl.program_id(0) == 0
    hbuf[pl.ds(0, CONV_HALO), :] = jnp.where(first, 0.0, ap_ref[...].astype(F32) * _sigmoid(gp_ref[...].astype(F32)))
    hbuf[pl.ds(CONV_HALO, t), :] = a_ref[...].astype(F32) * _sigmoid(g_ref[...].astype(F32))


def _phase_copies(dst, src, t):
    n = t + CONV_HALO - SUBLANES
    for s in range(1, SUBLANES):
        dst[s, pl.ds(0, n), :] = src[pl.ds(s, n), :]


def _window(phases, src, k, t):
    if k % SUBLANES == 0:
        return src[pl.ds(k, t), :]
    return phases[k % SUBLANES, pl.ds(k - k % SUBLANES, t), :]


def _layer_norm_parts(co):
    mu = jnp.mean(co, axis=-1, keepdims=True)
    xc = co - mu
    rstd = lax.rsqrt(jnp.mean(xc * xc, axis=-1, keepdims=True) + EPS)
    return xc * rstd, rstd


def _conv_fwd(z, conv_w, conv_b, ln_g, ln_b, name):
    s = z.shape[0]
    t = _conv_tile(s)
    off = CONV_HALO - (CONV_W - 1)

    def body(ap_ref, gp_ref, a_ref, g_ref, w_ref, b_ref, lg_ref, lb_ref, hc_ref, co_ref, hbuf, hph):
        _fill_glu(hbuf, ap_ref, gp_ref, a_ref, g_ref, t)
        _phase_copies(hph, hbuf, t)
        acc = jnp.zeros((t, CONV_C), F32) + b_ref[...]
        for j in range(CONV_W):
            acc = acc + _window(hph, hbuf, off + j, t) * w_ref[pl.ds(j, 1), :]
        co_ref[...] = acc
        xh, _ = _layer_norm_parts(acc)
        y = xh * lg_ref[...] + lb_ref[...]
        hc_ref[...] = (y * _sigmoid(y)).astype(BF16)

    vec = _fixed((1, CONV_C))
    return pl.pallas_call(
        body, name=name, out_shape=(jax.ShapeDtypeStruct((s, CONV_C), BF16), jax.ShapeDtypeStruct((s, CONV_C), F32)),
        grid=(s // t,),
        in_specs=[_halo_before(t, *ZC_CONV_A), _halo_before(t, *ZC_CONV_G), _rows(t, *ZC_CONV_A), _rows(t, *ZC_CONV_G),
                  _fixed((CONV_HALO, CONV_C)), vec, vec, vec],
        out_specs=(_rows(t, CONV_C), _rows(t, CONV_C)),
        scratch_shapes=[pltpu.VMEM((t + CONV_HALO, CONV_C), F32), pltpu.VMEM((SUBLANES, t + CONV_HALO, CONV_C), F32)],
        compiler_params=_params("parallel"))(z, z, z, z, conv_w, conv_b.reshape(1, -1), ln_g.reshape(1, -1),
                                             ln_b.reshape(1, -1))


def _conv_bwd_norm(dhc, co, ln_g, ln_b, name):
    s = co.shape[0]
    t = min(s, 512)

    def body(dhc_ref, co_ref, lg_ref, lb_ref, dco_ref, dg_ref, db_ref, dcb_ref):
        xh, rstd = _layer_norm_parts(co_ref[...])
        y = xh * lg_ref[...] + lb_ref[...]
        sg = _sigmoid(y)
        dy = dhc_ref[...] * (sg * (1.0 + y * (1.0 - sg)))
        dxh = dy * lg_ref[...]
        dco = rstd * (dxh - jnp.mean(dxh, axis=-1, keepdims=True) - xh * jnp.mean(dxh * xh, axis=-1, keepdims=True))
        dco_ref[...] = dco

        @pl.when(pl.program_id(0) == 0)
        def _():
            dg_ref[...] = jnp.zeros_like(dg_ref)
            db_ref[...] = jnp.zeros_like(db_ref)
            dcb_ref[...] = jnp.zeros_like(dcb_ref)

        dg_ref[...] += jnp.sum(dy * xh, axis=0, keepdims=True)
        db_ref[...] += jnp.sum(dy, axis=0, keepdims=True)
        dcb_ref[...] += jnp.sum(dco, axis=0, keepdims=True)

    vec = _fixed((1, CONV_C))
    one = jax.ShapeDtypeStruct((1, CONV_C), F32)
    dco, dg, db, dcb = pl.pallas_call(
        body, name=name, out_shape=(jax.ShapeDtypeStruct((s, CONV_C), F32), one, one, one), grid=(s // t,),
        in_specs=[_rows(t, CONV_C), _rows(t, CONV_C), vec, vec], out_specs=(_rows(t, CONV_C), vec, vec, vec),
        compiler_params=_params("arbitrary"))(dhc, co, ln_g.reshape(1, -1), ln_b.reshape(1, -1))
    return dco, dg.reshape(-1), db.reshape(-1), dcb.reshape(-1)


def _conv_bwd_taps(dco, z, conv_w, dz, name):
    s = z.shape[0]
    t = _conv_tile(s)
    nt = s // t
    off = CONV_HALO - (CONV_W - 1)

    def body(ap_ref, gp_ref, a_ref, g_ref, d_ref, dn_ref, w_ref, _, du_ref, dw_ref, hbuf, dbuf, hph, dph):
        i = pl.program_id(0)
        _fill_glu(hbuf, ap_ref, gp_ref, a_ref, g_ref, t)
        dbuf[pl.ds(0, t), :] = d_ref[...]
        dbuf[pl.ds(t, CONV_HALO), :] = jnp.where(i == nt - 1, 0.0, dn_ref[...])
        _phase_copies(hph, hbuf, t)
        _phase_copies(dph, dbuf, t)

        @pl.when(i == 0)
        def _():
            dw_ref[...] = jnp.zeros_like(dw_ref)

        dcur = d_ref[...]
        dh = jnp.zeros((t, CONV_C), F32)
        for j in range(CONV_W):
            dh = dh + _window(dph, dbuf, CONV_W - 1 - j, t) * w_ref[pl.ds(j, 1), :]
            dw_ref[pl.ds(j, 1), :] += jnp.sum(dcur * _window(hph, hbuf, off + j, t), axis=0, keepdims=True)
        a, sg = a_ref[...].astype(F32), _sigmoid(g_ref[...].astype(F32))
        du_ref[:, pl.ds(0, CONV_C)] = (dh * sg).astype(BF16)
        du_ref[:, pl.ds(CONV_C, CONV_C)] = (dh * a * sg * (1.0 - sg)).astype(BF16)

    into = _into(dz, 7, 0)
    return pl.pallas_call(
        body, name=name, out_shape=(into["out_shape"], jax.ShapeDtypeStruct((CONV_HALO, CONV_C), F32)), grid=(nt,),
        in_specs=[_halo_before(t, *ZC_CONV_A), _halo_before(t, *ZC_CONV_G), _rows(t, *ZC_CONV_A), _rows(t, *ZC_CONV_G),
                  _rows(t, CONV_C), _halo_after(t, CONV_C, 0, nt), _fixed((CONV_HALO, CONV_C))] + into["in_specs"],
        out_specs=(_rows(t, *ZC_CONV), _fixed((CONV_HALO, CONV_C))), input_output_aliases=into["input_output_aliases"],
        scratch_shapes=[pltpu.VMEM((t + CONV_HALO, CONV_C), F32), pltpu.VMEM((t + CONV_HALO, CONV_C), F32),
                        pltpu.VMEM((SUBLANES, t + CONV_HALO, CONV_C), F32),
                        pltpu.VMEM((SUBLANES, t + CONV_HALO, CONV_C), F32)],
        compiler_params=_params("arbitrary"))(z, z, z, z, dco, dco, conv_w, dz)


def _pool_tile(s):
    return min(s, 512)


def _pool_counts(row0, n, window):
    rows = row0 + lax.broadcasted_iota(jnp.int32, (n, POOL_GD), 0)
    return jnp.minimum(rows + 1, window).astype(F32)


def _pool_diff(ubuf, gi, window, row0, t):
    lanes = pl.ds(gi * POOL_GD, POOL_GD)
    tot = ubuf[pl.ds(CONV_HALO, t), lanes]
    cur = tot
    for back in range(1, window):
        tot = tot + ubuf[pl.ds(CONV_HALO - back, t), lanes]
    return tot / _pool_counts(row0, t, window) - cur


def _pool_fwd(z, pool_w, pool_scale, name):
    s = z.shape[0]
    t = _pool_tile(s)

    def body(up_ref, u_ref, w_ref, sc_ref, m_ref, ubuf):
        i = pl.program_id(0)
        ubuf[pl.ds(0, CONV_HALO), :] = jnp.where(i == 0, 0.0, up_ref[...].astype(F32))
        ubuf[pl.ds(CONV_HALO, t), :] = u_ref[...].astype(F32)
        for gi, window in enumerate(POOL_WINDOWS):
            d = _pool_diff(ubuf, gi, window, i * t, t)
            mm = jnp.dot(d.astype(BF16), w_ref[gi].astype(BF16), preferred_element_type=F32)
            lanes = pl.ds(gi * POOL_GD, POOL_GD)
            m_ref[:, lanes] = (mm * sc_ref[:, lanes]).astype(BF16)

    return pl.pallas_call(
        body, name=name, out_shape=jax.ShapeDtypeStruct((s, POOL_C), BF16), grid=(s // t,),
        in_specs=[_halo_before(t, *ZC_POOL), _rows(t, *ZC_POOL), _fixed((POOL_G, POOL_GD, POOL_GD)), _fixed((1, POOL_C))],
        out_specs=_rows(t, POOL_C), scratch_shapes=[pltpu.VMEM((t + CONV_HALO, POOL_C), F32)],
        compiler_params=_params("parallel"))(z, z, pool_w, pool_scale.reshape(1, -1))


def _pool_bwd(dm, z, pool_w, pool_scale, dz, name):
    s = z.shape[0]
    t = _pool_tile(s)
    nt = s // t

    def body(up_ref, u_ref, dm_ref, dmn_ref, w_ref, sc_ref, _, du_ref, dw_ref, dsc_ref, ubuf, ebuf):
        i = pl.program_id(0)
        ubuf[pl.ds(0, CONV_HALO), :] = jnp.where(i == 0, 0.0, up_ref[...].astype(F32))
        ubuf[pl.ds(CONV_HALO, t), :] = u_ref[...].astype(F32)

        @pl.when(i == 0)
        def _():
            dw_ref[...] = jnp.zeros_like(dw_ref)
            dsc_ref[...] = jnp.zeros_like(dsc_ref)

        dm_next = jnp.where(i == nt - 1, 0.0, dmn_ref[...])
        for gi, window in enumerate(POOL_WINDOWS):
            lanes = pl.ds(gi * POOL_GD, POOL_GD)
            wb = w_ref[gi].astype(BF16)
            scale = sc_ref[:, lanes]
            d = _pool_diff(ubuf, gi, window, i * t, t).astype(BF16)
            mm = jnp.dot(d, wb, preferred_element_type=F32)
            dmv = dm_ref[:, lanes]
            dsc_ref[:, lanes] += jnp.sum(dmv * mm, axis=0, keepdims=True)
            dmm = (dmv * scale).astype(BF16)
            dw_ref[gi] += lax.dot_general(d, dmm, (((0,), (0,)), ((), ())), preferred_element_type=F32)
            dd = lax.dot_general(dmm, wb, (((1,), (1,)), ((), ())), preferred_element_type=F32)
            dd_next = lax.dot_general((dm_next[:, gi * POOL_GD:(gi + 1) * POOL_GD] * scale).astype(BF16), wb,
                                      (((1,), (1,)), ((), ())), preferred_element_type=F32)
            ebuf[pl.ds(0, t), lanes] = dd / _pool_counts(i * t, t, window)
            ebuf[pl.ds(t, CONV_HALO), lanes] = dd_next / _pool_counts((i + 1) * t, CONV_HALO, window)
            du = -dd
            for ahead in range(window):
                du = du + ebuf[pl.ds(ahead, t), lanes]
            du_ref[:, lanes] = du.astype(BF16)

    into = _into(dz, 6, 0)
    du, dw, dsc = pl.pallas_call(
        body, name=name,
        out_shape=(into["out_shape"], jax.ShapeDtypeStruct((POOL_G, POOL_GD, POOL_GD), F32),
                   jax.ShapeDtypeStruct((1, POOL_C), F32)), grid=(nt,),
        in_specs=[_halo_before(t, *ZC_POOL), _rows(t, *ZC_POOL), _rows(t, POOL_C), _halo_after(t, POOL_C, 0, nt),
                  _fixed((POOL_G, POOL_GD, POOL_GD)), _fixed((1, POOL_C))] + into["in_specs"],
        out_specs=(_rows(t, *ZC_POOL), _fixed((POOL_G, POOL_GD, POOL_GD)), _fixed((1, POOL_C))),
        input_output_aliases=into["input_output_aliases"],
        scratch_shapes=[pltpu.VMEM((t + CONV_HALO, POOL_C), F32), pltpu.VMEM((t + CONV_HALO, POOL_C), F32)],
        compiler_params=_params("arbitrary"))(z, z, dm, dm, pool_w, pool_scale.reshape(1, -1), dz)
    return du, dw, dsc.reshape(-1)


def _gate_specs(ts):
    width, first = ZC_GATE
    return [_rows(ts, width, first + b) for b in range(3)]


def _branches_merge_fwd(z, acts, ws, name):
    s = z.shape[0]
    ts = min(s, 512)

    def body(g0, g1, g2, a0, a1, a2, w0, w1, w2, y0, y1, y2, m_ref):
        merged = jnp.zeros((ts, D_MODEL), F32)
        for g_ref, a_ref, w_ref, y_ref in ((g0, a0, w0, y0), (g1, a1, w1, y1), (g2, a2, w2, y2)):
            wv = jnp.concatenate([w_ref[c] for c in range(N_DEV)], axis=1)
            yb = jnp.dot(a_ref[...], wv, preferred_element_type=F32).astype(BF16)
            y_ref[...] = yb
            merged = merged + _sigmoid(g_ref[...].astype(F32)) * yb.astype(F32)
        m_ref[...] = merged.astype(BF16)

    out = jax.ShapeDtypeStruct((s, D_MODEL), BF16)
    res = pl.pallas_call(
        body, name=name, out_shape=(out,) * 4, grid=(s // ts,),
        in_specs=_gate_specs(ts) + [_rows(ts, a.shape[1]) for a in acts] + [_fixed(w.shape) for w in ws],
        out_specs=(_rows(ts, D_MODEL),) * 4, compiler_params=_params("parallel"))(z, z, z, *acts, *ws)
    return tuple(res[:3]), res[3]


def _merge_bwd(z, ys, dmerged, name):
    s = z.shape[0]
    ts = min(s, 256)

    def body(g0, g1, g2, y0, y1, y2, dm_ref, dy0, dy1, dy2, dz_ref):
        dmv = dm_ref[...]
        for b, (g_ref, y_ref, dy_ref) in enumerate(((g0, y0, dy0), (g1, y1, dy1), (g2, y2, dy2))):
            sg = _sigmoid(g_ref[...].astype(F32))
            dy_ref[...] = (dmv * sg).astype(BF16)
            dz_ref[:, pl.ds(b * D_MODEL, D_MODEL)] = (dmv * y_ref[...].astype(F32) * sg * (1.0 - sg)).astype(BF16)

    out = jax.ShapeDtypeStruct((s, D_MODEL), BF16)
    return pl.pallas_call(
        body, name=name, out_shape=(out,) * 3 + (jax.ShapeDtypeStruct((s, Z_W), BF16),), grid=(s // ts,),
        in_specs=_gate_specs(ts) + [_rows(ts, D_MODEL)] * 4,
        out_specs=(_rows(ts, D_MODEL),) * 3 + (_rows(ts, *ZC_GATES),),
        compiler_params=_params("parallel"))(z, z, z, *ys, dmerged)


def _ffn_up_fwd(h, w_gate, w_up, name):
    s, d = h.shape
    nb = w_gate.shape[2]
    f = N_DEV * nb
    tm, n_blk = min(s, 1024), 2
    tn = n_blk * nb
    blk = pl.BlockSpec((tm, tn), lambda i, j: (i, j))
    wspec = pl.BlockSpec((n_blk, d, nb), lambda i, j: (j, 0, 0))

    def body(h_ref, wg_ref, wu_ref, hg_ref, hu_ref, act_ref):
        hv = h_ref[...]
        g = jnp.dot(hv, jnp.concatenate([wg_ref[c] for c in range(n_blk)], axis=1), preferred_element_type=F32)
        u = jnp.dot(hv, jnp.concatenate([wu_ref[c] for c in range(n_blk)], axis=1), preferred_element_type=F32)
        hg_ref[...] = g.astype(hg_ref.dtype)
        hu_ref[...] = u.astype(hu_ref.dtype)
        act_ref[...] = (g * _sigmoid(g) * u).astype(BF16)

    return pl.pallas_call(
        body, name=name,
        out_shape=(jax.ShapeDtypeStruct((s, f), BF16),) * 3,
        grid=(s // tm, f // tn), in_specs=[pl.BlockSpec((tm, d), lambda i, j: (i, 0)), wspec, wspec],
        out_specs=(blk, blk, blk), compiler_params=_params("parallel", "parallel"))(h, w_gate, w_up)


def _ffn_down_bwd(dfo, w_down, hg, hu, name):
    s, d = dfo.shape
    f = w_down.shape[0]
    tm, tn = min(s, 1024), _tile(f, 1024)
    blk = pl.BlockSpec((tm, tn), lambda i, j: (i, j))

    def body(d_ref, w_ref, g_ref, u_ref, dg_ref, du_ref):
        dact = lax.dot_general(d_ref[...], w_ref[...], (((1,), (1,)), ((), ())), preferred_element_type=F32)
        g = g_ref[...].astype(F32)
        sg = _sigmoid(g)
        dg_ref[...] = (dact * u_ref[...].astype(F32) * (sg * (1.0 + g * (1.0 - sg)))).astype(BF16)
        du_ref[...] = (dact * g * sg).astype(BF16)

    out = jax.ShapeDtypeStruct((s, f), BF16)
    return pl.pallas_call(
        body, name=name, out_shape=(out, out), grid=(s // tm, f // tn),
        in_specs=[pl.BlockSpec((tm, d), lambda i, j: (i, 0)), pl.BlockSpec((tn, d), lambda i, j: (j, 0)), blk, blk],
        out_specs=(blk, blk), compiler_params=_params("parallel", "parallel"))(dfo, w_down, hg, hu)


def _loss_grad(y, target, name):
    s, d = y.shape
    ts = min(s, 512)

    def body(y_ref, t_ref, dy_ref, sq_ref):
        e = y_ref[...] - t_ref[...]
        dy_ref[...] = e / d

        @pl.when(pl.program_id(0) == 0)
        def _():
            sq_ref[...] = jnp.zeros_like(sq_ref)

        sq_ref[...] += jnp.sum(e * e, axis=0, keepdims=True)

    return pl.pallas_call(
        body, name=name, out_shape=(jax.ShapeDtypeStruct((s, d), F32), jax.ShapeDtypeStruct((1, d), F32)),
        grid=(s // ts,), in_specs=[_rows(ts, d), _rows(ts, d)], out_specs=(_rows(ts, d), _fixed((1, d))),
        compiler_params=_params("arbitrary"))(y, target)


def _adamw(w, g, m, v, name):
    shape = w.shape
    cols = shape[-1]
    keep3 = w.ndim == 3 and shape[1] < SUBLANES
    view = shape if keep3 else (math.prod(shape[:-1]), cols)
    rows = view[0]
    if keep3:
        cap = max(1, (2 << 20) // (SUBLANES * cols * 4))
        tr = max(t for t in range(1, cap + 1) if rows % t == 0)
    else:
        tr = _row_tile(rows, cols * 4, budget=2 << 20)

    def body(w_ref, g_ref, m_ref, v_ref, d_ref, mo_ref, vo_ref):
        gv = g_ref[...]
        mn = B1 * m_ref[...] + (1.0 - B1) * gv
        vn = B2 * v_ref[...] + (1.0 - B2) * (gv * gv)
        m_hat = mn / (1.0 - B1 ** STEP)
        v_hat = vn / (1.0 - B2 ** STEP)
        d_ref[...] = -LR * (m_hat / (jnp.sqrt(v_hat) + ADAM_EPS) + WD * w_ref[...])
        mo_ref[...] = mn
        vo_ref[...] = vn

    spec = pl.BlockSpec((tr,) + view[1:], lambda i: (i,) + (0,) * (len(view) - 1))
    out = jax.ShapeDtypeStruct(view, F32)
    res = pl.pallas_call(
        body, name=name, out_shape=(out,) * 3, grid=(rows // tr,), in_specs=[spec] * 4, out_specs=(spec,) * 3,
        compiler_params=_params("parallel"))(*[t.reshape(view) for t in (w, g, m, v)])
    return tuple(r.reshape(shape) for r in res)


LANE_MAJOR = ("w_uq", "w_uk", "w_uv", "w_gate", "w_up")


def _lane_major(name, a):
    if name == "w_in":
        return a.transpose(2, 0, 1)
    if name in LANE_MAJOR:
        return a.transpose(0, 2, 1)
    return a


def _from_lane_major(name, a):
    if name == "w_in":
        return a.transpose(1, 2, 0)
    return _lane_major(name, a)


ANY = pl.BlockSpec(memory_space=pl.ANY)


class _GatherRide:
    def __init__(self, arrays):
        n = len(arrays)
        self.arrays = list(arrays)
        self.out_shape = [jax.ShapeDtypeStruct((N_DEV,) + a.shape, a.dtype) for a in arrays]
        self.scratch = [pltpu.SemaphoreType.DMA((n, 7)), pltpu.SemaphoreType.DMA((n, 7)), pltpu.SemaphoreType.DMA((n,))]

    def _copies(self, ins, outs, sems):
        send_sems, recv_sems, local_sems = sems
        n = len(self.arrays)
        x, y, c = lax.axis_index("x"), lax.axis_index("y"), lax.axis_index("c")
        me, sibling = (x, y, c), (x, y, 1 - c)
        chips = [(1 - x, y), (x, 1 - y), (1 - x, 1 - y)]

        def slot(a, px, py, pc):
            return outs[a].at[4 * px + 2 * py + pc]

        def copy(a, k, block, to, src=None):
            return pltpu.make_async_remote_copy(
                src_ref=slot(a, *block) if src is None else src, dst_ref=slot(a, *block), send_sem=send_sems.at[a, k],
                recv_sem=recv_sems.at[a, k], device_id=to, device_id_type=MESH)

        mine = [pltpu.make_async_copy(ins[a], slot(a, *me), local_sems.at[a]) for a in range(n)]
        first = []
        for a in range(n):
            first.append(copy(a, 0, me, sibling, src=ins[a]))
            first += [copy(a, 1 + j, me, (*chip, c), src=ins[a]) for j, chip in enumerate(chips)]
        return n, me, sibling, chips, c, copy, mine, first

    def start(self, ins, outs, sems):
        _, _, _, _, _, _, mine, first = self._copies(ins, outs, sems)
        for cp in mine + first:
            cp.start()

    def finish(self, ins, outs, sems):
        n, me, sibling, chips, c, copy, mine, first = self._copies(ins, outs, sems)
        passed = []
        for j, chip in enumerate(chips):
            for a in range(n):
                copy(a, 1 + j, (*chip, c), me).wait_recv()
                passed.append(copy(a, 4 + j, (*chip, c), sibling))
                passed[-1].start()
        for a in range(n):
            copy(a, 0, sibling, me).wait_recv()
            for j, chip in enumerate(chips):
                copy(a, 4 + j, (*chip, 1 - c), me).wait_recv()
        for cp in first + passed:
            cp.wait_send()
        for cp in mine:
            cp.wait()


class _ReduceRide:
    def __init__(self, arrays):
        n = len(arrays)
        self.arrays = list(arrays)
        self.out_shape = [jax.ShapeDtypeStruct(a.shape, a.dtype) for a in arrays]
        self.scratch = [pltpu.SemaphoreType.DMA((n, 7)), pltpu.SemaphoreType.DMA((n, 7)), pltpu.SemaphoreType.DMA((n,))]

    def _copies(self, ins, outs, sems):
        send_sems, recv_sems, local_sems = sems
        n = len(self.arrays)
        x, y, c = lax.axis_index("x"), lax.axis_index("y"), lax.axis_index("c")
        mine = [pltpu.make_async_copy(ins[a].at[4 * x + 2 * y + c], outs[a].at[0], local_sems.at[a]) for a in range(n)]
        copies = []
        for a in range(n):
            for k in range(1, N_DEV):
                px = 1 - x if k & 4 else x
                py = 1 - y if k & 2 else y
                pc = 1 - c if k & 1 else c
                copies.append(pltpu.make_async_remote_copy(
                    src_ref=ins[a].at[4 * px + 2 * py + pc], dst_ref=outs[a].at[k], send_sem=send_sems.at[a, k - 1],
                    recv_sem=recv_sems.at[a, k - 1], device_id=(px, py, pc), device_id_type=MESH))
        return mine, copies

    def start(self, ins, outs, sems):
        mine, copies = self._copies(ins, outs, sems)
        for cp in mine + copies:
            cp.start()

    def finish(self, ins, outs, sems):
        mine, copies = self._copies(ins, outs, sems)
        for cp in copies + mine:
            cp.wait()


def _run_ride(ride, name):
    n = len(ride.arrays)

    def body(*refs):
        ins, outs, sems = refs[:n], refs[n:2 * n], refs[2 * n:]
        ride.start(ins, outs, sems)
        ride.finish(ins, outs, sems)

    return pl.pallas_call(body, name=name, out_shape=ride.out_shape, in_specs=[ANY] * n, out_specs=[ANY] * n,
                          scratch_shapes=ride.scratch)(*ride.arrays)


def _all_gather(arrays, name):
    return _run_ride(_GatherRide(arrays), name)


def _swap_with_sibling(arrays, name):
    n = len(arrays)

    def body(*refs):
        ins, outs = refs[:n], refs[n:2 * n]
        send_sems, recv_sems = refs[2 * n:]
        x, y, c = lax.axis_index("x"), lax.axis_index("y"), lax.axis_index("c")
        copies = [pltpu.make_async_remote_copy(
            src_ref=ins[a].at[1 - c], dst_ref=outs[a], send_sem=send_sems.at[a], recv_sem=recv_sems.at[a],
            device_id=(x, y, 1 - c), device_id_type=MESH) for a in range(n)]
        for cp in copies:
            cp.start()
        for cp in copies:
            cp.wait()

    return pl.pallas_call(
        body, name=name, out_shape=[jax.ShapeDtypeStruct(a.shape[1:], a.dtype) for a in arrays],
        in_specs=[ANY] * n, out_specs=[ANY] * n,
        scratch_shapes=[pltpu.SemaphoreType.DMA((n,)), pltpu.SemaphoreType.DMA((n,))])(*arrays)


class _ChipExchangeRide:
    def __init__(self, arrays):
        n = len(arrays)
        self.arrays = list(arrays)
        self.out_shape = [jax.ShapeDtypeStruct(a.shape, a.dtype) for a in arrays]
        self.scratch = [pltpu.SemaphoreType.DMA((n, 3)), pltpu.SemaphoreType.DMA((n, 3)), pltpu.SemaphoreType.DMA((n,))]

    def _copies(self, ins, outs, sems):
        send_sems, recv_sems, local_sems = sems
        n = len(self.arrays)
        x, y, c = lax.axis_index("x"), lax.axis_index("y"), lax.axis_index("c")
        partners = [(x, 1 - y), (1 - x, y), (1 - x, 1 - y)]
        mine = [pltpu.make_async_copy(ins[a].at[2 * x + y], outs[a].at[0], local_sems.at[a]) for a in range(n)]
        copies = [pltpu.make_async_remote_copy(
            src_ref=ins[a].at[2 * px + py], dst_ref=outs[a].at[1 + k], send_sem=send_sems.at[a, k],
            recv_sem=recv_sems.at[a, k], device_id=(px, py, c), device_id_type=MESH)
            for a in range(n) for k, (px, py) in enumerate(partners)]
        return mine, copies

    def start(self, ins, outs, sems):
        mine, copies = self._copies(ins, outs, sems)
        for cp in mine + copies:
            cp.start()

    def finish(self, ins, outs, sems):
        mine, copies = self._copies(ins, outs, sems)
        for cp in copies + mine:
            cp.wait()


class _Combo:
    def __init__(self, rides):
        self.rides = rides
        self.arrays = [a for r in rides for a in r.arrays]
        self.out_shape = [o for r in rides for o in r.out_shape]
        self.scratch = [sc for r in rides for sc in r.scratch]

    def _parts(self, ins, outs, sems):
        at_a = at_s = 0
        for r in self.rides:
            na, ns = len(r.arrays), len(r.scratch)
            yield r, ins[at_a:at_a + na], outs[at_a:at_a + na], sems[at_s:at_s + ns]
            at_a, at_s = at_a + na, at_s + ns

    def start(self, ins, outs, sems):
        for r, i, o, sm in self._parts(ins, outs, sems):
            r.start(i, o, sm)

    def finish(self, ins, outs, sems):
        for r, i, o, sm in reversed(list(self._parts(ins, outs, sems))):
            r.finish(i, o, sm)


def _as_rows(a, lead):
    return a.reshape(a.shape[:lead] + (math.prod(a.shape[lead:-1]), a.shape[-1]))


def _add_pairs(a, b, name):
    a2, b2 = _as_rows(a, 0), _as_rows(b, 0)
    rows, cols = a2.shape
    tr = _row_tile(rows, cols * 4)

    def body(a_ref, b_ref, o_ref):
        o_ref[...] = (a_ref[...].astype(F32) + b_ref[...].astype(F32)).astype(o_ref.dtype)

    spec = _rows(tr, cols)
    out = pl.pallas_call(body, name=name, out_shape=jax.ShapeDtypeStruct(a2.shape, a.dtype), grid=(rows // tr,),
                         in_specs=[spec, spec], out_specs=spec, compiler_params=_params("parallel"))(a2, b2)
    return out.reshape(a.shape)


def _sum_blocks(a, name):
    a3 = _as_rows(a, 1)
    n, rows, cols = a3.shape
    tr = _row_tile(rows, n * cols * 4)

    def body(a_ref, o_ref):
        tot = a_ref[0].astype(F32)
        for k in range(1, n):
            tot = tot + a_ref[k].astype(F32)
        o_ref[...] = tot

    out = pl.pallas_call(body, name=name, out_shape=jax.ShapeDtypeStruct((rows, cols), F32), grid=(rows // tr,),
                         in_specs=[pl.BlockSpec((n, tr, cols), lambda j: (0, j, 0))], out_specs=_rows(tr, cols),
                         compiler_params=_params("parallel"))(a3)
    return out.reshape(a.shape[1:])


def _sum_layers(blocks, name):
    arrs = [_as_rows(a, 1) for a in blocks]
    rows, cols = arrs[0].shape[1:]
    tr = _row_tile(rows, max(a.shape[0] for a in arrs) * cols * 4, budget=8 << 20)
    nj = rows // tr

    def body(*refs):
        o_ref = refs[-1]
        for k, a_ref in enumerate(refs[:-1]):
            @pl.when(pl.program_id(0) == k)
            def _(a_ref=a_ref, n=arrs[k].shape[0]):
                tot = a_ref[0].astype(F32)
                for b in range(1, n):
                    tot = tot + a_ref[b].astype(F32)
                o_ref[0] = tot

    in_specs = [pl.BlockSpec((a.shape[0], tr, cols),
                             lambda l, j, k=k: (0, jnp.where(l == k, j, jnp.where(l < k, 0, nj - 1)), 0))
                for k, a in enumerate(arrs)]
    out = pl.pallas_call(body, name=name, out_shape=jax.ShapeDtypeStruct((len(arrs), rows, cols), F32),
                         grid=(len(arrs), nj), in_specs=in_specs,
                         out_specs=pl.BlockSpec((1, tr, cols), lambda l, j: (l, j, 0)),
                         compiler_params=_params("arbitrary", "arbitrary"))(*arrs)
    return out.reshape((len(arrs),) + blocks[0].shape[1:])


MIX_GROUPS = ("w_in", "w_uq", "w_uk", "w_uv", "w_attn_o", "w_conv_o", "w_pool_o", "w_mix_o")
FFN_GROUPS = ("w_gate", "w_up", "w_down")
MIX_EARLY = ("w_attn_o", "w_conv_o", "w_pool_o", "w_mix_o")
MIX_LATE = ("w_in", "w_uq", "w_uk", "w_uv")


def _pad_axis(a, axis, size):
    pad = [(0, 0)] * a.ndim
    pad[axis] = (0, size - a.shape[axis])
    return jnp.pad(a, pad)


def _local_groups(sh, l):
    out = {n: sh[n][l] for n in BIG}
    for n in ("w_uq", "w_uk", "w_uv"):
        out[n] = _pad_axis(out[n], -1, HEAD_PAD)
    for n in ("w_gate", "w_up"):
        out[n] = _pad_axis(out[n], -1, FF_SHARD_PAD)
    out["w_down"] = _pad_axis(out["w_down"], 0, FF_SHARD_PAD)
    return {n: v.astype(BF16) for n, v in out.items()}


def _arrange_w_in(blocks):
    parts, pos = [], 0
    for ref_lo, ref_hi, at in sorted(W_IN_PIECES, key=lambda p: p[2]):
        if at > pos:
            parts.append(jnp.zeros((blocks.shape[1], at - pos), blocks.dtype))
        for d in range(N_DEV):
            lo, hi = max(ref_lo, d * W_IN_SHARD), min(ref_hi, (d + 1) * W_IN_SHARD)
            if lo < hi:
                parts.append(blocks[d][:, lo - d * W_IN_SHARD:hi - d * W_IN_SHARD])
        pos = at + ref_hi - ref_lo
    if pos < Z_W:
        parts.append(jnp.zeros((blocks.shape[1], Z_W - pos), blocks.dtype))
    return jnp.concatenate(parts, axis=1)


def _w_in_shard(g, d):
    parts = []
    for ref_lo, ref_hi, at in W_IN_PIECES:
        lo, hi = max(ref_lo, d * W_IN_SHARD), min(ref_hi, (d + 1) * W_IN_SHARD)
        if lo < hi:
            parts.append(g[:, at + lo - ref_lo:at + hi - ref_lo])
    return jnp.concatenate(parts, axis=1)


def _mixer_weights(gat):
    w = {n: v for n, v in gat.items() if n != "w_in"}
    attn_o = gat["w_attn_o"].reshape(N_DEV, N_HEADS, V_HEAD, LANES)
    w["w_attn_o"] = _pad_axis(attn_o, 2, HEAD_PAD).reshape(N_DEV, N_HEADS * HEAD_PAD, LANES)
    w["w_mix_o"] = gat["w_mix_o"].reshape(D_MODEL, D_MODEL)
    return w


def _ffn_weights(gat):
    return {"w_gate": gat["w_gate"], "w_up": gat["w_up"], "w_down": gat["w_down"].reshape(D_FF_PAD, D_MODEL)}


def _mixer_grad_groups(gb):
    g = dict(gb)
    if "w_in" in gb:
        g["w_in"] = jnp.stack([_w_in_shard(gb["w_in"], d) for d in range(N_DEV)])
    if "w_attn_o" in gb:
        attn_o = gb["w_attn_o"].reshape(N_DEV, N_HEADS, HEAD_PAD, LANES)[:, :, :V_HEAD]
        g["w_attn_o"] = attn_o.reshape(N_DEV, N_HEADS * V_HEAD, LANES)
    if "w_mix_o" in gb:
        g["w_mix_o"] = gb["w_mix_o"].reshape(N_DEV, D_MODEL // N_DEV, D_MODEL)
    return g


def _ffn_grad_groups(gb):
    return {"w_gate": gb["w_gate"], "w_up": gb["w_up"], "w_down": gb["w_down"].reshape(N_DEV, FF_SHARD_PAD, D_MODEL)}


def _grads_from_groups(tot):
    g = dict(tot)
    g["w_uq"] = tot["w_uq"][..., :QK_NOPE + QK_ROPE]
    g["w_uk"], g["w_uv"] = tot["w_uk"][..., :QK_NOPE], tot["w_uv"][..., :V_HEAD]
    g["w_gate"], g["w_up"] = tot["w_gate"][..., :FF_SHARD], tot["w_up"][..., :FF_SHARD]
    g["w_down"] = tot["w_down"][..., :FF_SHARD, :]
    return g


SMALL_GROUPS = (
    (D_MODEL, ("mix_norm_pre", "mix_norm_post", "ffn_norm_pre", "ffn_norm_post")),
    (CONV_C, ("conv_w", "conv_b", "conv_ln_g", "conv_ln_b", "pool_scale")),
    (Q_RANK, ("q_norm",)), (KV_RANK, ("kv_norm",)), (POOL_GD, ("pool_w",)),
)


def _small_rows(name):
    return {"conv_w": CONV_HALO, "pool_w": POOL_G * POOL_GD}.get(name, SUBLANES)


def _small_groups(small):
    out = []
    for width, names in SMALL_GROUPS:
        parts = []
        for l in range(DEPTH):
            for n in names:
                part = small[l][n].reshape(-1, width)
                parts.append(_pad_axis(part, 0, _small_rows(n)))
        out.append(jnp.concatenate(parts, axis=0))
    return out


def _small_from_groups(groups):
    shapes = {"conv_w": (CONV_W, CONV_C), "pool_w": (POOL_G, POOL_GD, POOL_GD)}
    out = {}
    for (width, names), g in zip(SMALL_GROUPS, groups):
        row = 0
        for l in range(DEPTH):
            for n in names:
                rows = _small_rows(n)
                real = {"conv_w": CONV_W, "pool_w": POOL_G * POOL_GD}.get(n, 1)
                out.setdefault(n, []).append(g[row:row + real].reshape(shapes.get(n, (width,))))
                row += rows
    return {n: jnp.stack(v) for n, v in out.items()}


def _mixer_fwd(x, h, tables, sm, plan, l):
    nm = lambda n: f"{n}_l{l}"
    if h is None:
        h = _rms_fwd(x, (D_MODEL, 0), sm["mix_norm_pre"], BF16, nm("mix_pre_norm"))
    w_in, ride = plan.w_in(l), plan.in_proj_ride(l)
    if ride is None:
        z = _matmul(h, w_in, "nn", BF16, nm("in_proj"))
    else:
        z, rode = _matmul(h, w_in, "nn", BF16, nm("in_proj"), ride=ride)
        plan.in_proj_done(l, rode)
    w = dict(plan.mixer_weights(l), w_in=w_in)
    cq, ckv, q, k, v = _qkv_up_fwd(z, sm["q_norm"], sm["kv_norm"], w["w_uq"], w["w_uk"], w["w_uv"], tables, nm("qkv_up"))
    (o, lse), rode = _flash_fwd(q, k, v, nm("flash_fwd"), plan.fwd_ride(l))
    plan.fwd_done(l, rode)
    hc, co = _conv_fwd(z, sm["conv_w"], sm["conv_b"], sm["conv_ln_g"], sm["conv_ln_b"], nm("conv_fwd"))
    pm = _pool_fwd(z, sm["pool_w"], sm["pool_scale"], nm("pool_fwd"))
    ys, merged = _branches_merge_fwd(z, (o, hc, pm), (w["w_attn_o"], w["w_conv_o"], w["w_pool_o"]), nm("branches_merge"))
    mo = _matmul(merged, w["w_mix_o"], "nn", F32, nm("mix_out"))
    x_mid, h2 = _rms_fwd(mo, (D_MODEL, 0), sm["mix_norm_post"], F32, nm("mix_post_norm"), res=x, then=sm["ffn_norm_pre"])
    saved = dict(x=x, h=h, z=z, cq=cq, ckv=ckv, q=q, k=k, v=v, o=o, lse=lse, hc=hc, co=co, pm=pm, ys=ys, merged=merged,
                 mo=mo)
    return x_mid, h2, saved, w


def _ffn_fwd(x_mid, h2, w, sm, tag, next_gain):
    nm = lambda n: f"{n}_{tag}"
    hg, hu, act = _ffn_up_fwd(h2, w["w_gate"], w["w_up"], nm("ffn_up_fwd"))
    fo = _matmul(act, w["w_down"], "nn", F32, nm("ffn_down"))
    out = _rms_fwd(fo, (D_MODEL, 0), sm["ffn_norm_post"], F32, nm("ffn_post_norm"), res=x_mid, then=next_gain)
    out, h_next = out if next_gain is not None else (out, None)
    saved = dict(x_mid=x_mid, h2=h2, hg=hg, hu=hu, act=act, fo=fo)
    return out, h_next, saved


def _ffn_bwd(dout, sv, w, sm, tag):
    nm = lambda n: f"{n}_{tag}"
    gb, gs = {}, {}
    dfo, gs["ffn_norm_post"] = _rms_bwd(sv["fo"], (D_MODEL, 0), sm["ffn_norm_post"], dout, BF16, nm("ffn_post_norm_bwd"))
    gb["w_down"] = _matmul(sv["act"], dfo, "tn", BF16, nm("ffn_down_dw"))
    dhg, dhu = _ffn_down_bwd(dfo, w["w_down"], sv["hg"], sv["hu"], nm("ffn_down_bwd"))
    dh2_g = _matmul(dhg, w["w_gate"], "nt", F32, nm("ffn_gate_dx"))
    dh2 = _matmul(dhu, w["w_up"], "nt", F32, nm("ffn_up_dx"), add=dh2_g)
    gb["w_gate"] = _matmul(sv["h2"], dhg, "tn", BF16, nm("ffn_gate_dw"), blocked=True)
    gb["w_up"] = _matmul(sv["h2"], dhu, "tn", BF16, nm("ffn_up_dw"), blocked=True)
    dmid, gs["ffn_norm_pre"] = _rms_bwd(sv["x_mid"], (D_MODEL, 0), sm["ffn_norm_pre"], dh2, F32, nm("ffn_pre_norm_bwd"),
                                        add=dout)
    return dmid, gb, gs


def _mixer_bwd(dmid, sv, tables, w, sm, plan, l, pack_small):
    nm = lambda n: f"{n}_l{l}"
    gb, gs = {}, {}
    dmo, gs["mix_norm_post"] = _rms_bwd(sv["mo"], (D_MODEL, 0), sm["mix_norm_post"], dmid, BF16, nm("mix_post_norm_bwd"))
    dmerged = _matmul(dmo, w["w_mix_o"], "nt", F32, nm("mix_out_dx"))
    gb["w_mix_o"] = _matmul(sv["merged"], dmo, "tn", BF16, nm("mix_out_dw"))
    dya, dyc, dyp, dz = _merge_bwd(sv["z"], sv["ys"], dmerged, nm("merge_bwd"))
    dpm = _matmul(dyp, w["w_pool_o"], "nt", F32, nm("pool_out_dx"))
    gb["w_pool_o"] = _matmul(sv["pm"], dyp, "tn", BF16, nm("pool_out_dw"), blocked=True)
    dz, gs["pool_w"], gs["pool_scale"] = _pool_bwd(dpm, sv["z"], sm["pool_w"], sm["pool_scale"], dz, nm("pool_bwd"))
    dhc = _matmul(dyc, w["w_conv_o"], "nt", F32, nm("conv_out_dx"))
    gb["w_conv_o"] = _matmul(sv["hc"], dyc, "tn", BF16, nm("conv_out_dw"), blocked=True)
    dco, gs["conv_ln_g"], gs["conv_ln_b"], gs["conv_b"] = _conv_bwd_norm(dhc, sv["co"], sm["conv_ln_g"], sm["conv_ln_b"],
                                                                        nm("conv_bwd_norm"))
    dz, gs["conv_w"] = _conv_bwd_taps(dco, sv["z"], sm["conv_w"], dz, nm("conv_bwd_taps"))
    gb["w_attn_o"] = _matmul(sv["o"], dya, "tn", BF16, nm("attn_out_dw"), blocked=True)
    delta, dob = _attn_out_bwd(dya, w["w_attn_o"], sv["o"], nm("attn_out_bwd"))
    (dq, dk, dv), rode = _flash_bwd(sv["q"], sv["k"], sv["v"], dob, sv["lse"], delta, nm("flash_bwd"),
                                  plan.bwd_ride(l, gb))
    plan.bwd_done(l, rode)
    dqf, dkf, dz, gs["q_norm"], gs["kv_norm"] = _qkv_up_bwd(
        dq, dk, dv, sv["z"], w["w_uq"], w["w_uk"], w["w_uv"], tables, sm["q_norm"], sm["kv_norm"], dz, nm("qkv_up_bwd"))
    gb["w_uq"] = _matmul(sv["cq"], dqf, "tn", BF16, nm("q_up_dw"), blocked=True)
    gb["w_uk"] = _matmul(sv["ckv"], dkf, "tn", BF16, nm("k_up_dw"), blocked=True)
    gb["w_uv"] = _matmul(sv["ckv"], dv, "tn", BF16, nm("v_up_dw"), blocked=True)
    gb["w_in"] = _matmul(sv["h"], dz, "tn", BF16, nm("in_proj_dw"))
    plan.add_grads(l, "mix", gb)
    ride, small_gathered = plan.tail_ride(l, pack_small(gs)), []
    if ride is None:
        dh = _matmul(dz, w["w_in"], "nt", F32, nm("in_proj_dx"))
    else:
        dh, rode = _matmul(dz, w["w_in"], "nt", F32, nm("in_proj_dx"), ride=ride)
        small_gathered = plan.tail_done(l, rode)
    dx, gs["mix_norm_pre"] = _rms_bwd(sv["x"], (D_MODEL, 0), sm["mix_norm_pre"], dh, F32, nm("mix_pre_norm_bwd"), add=dmid)
    return dx, gs, small_gathered


def _part_groups(part):
    return {"mix": MIX_GROUPS, "ffn": FFN_GROUPS, "early": MIX_EARLY, "late": MIX_LATE}[part]


class _Plan:
    def __init__(self, shards, conv_w):
        self.local = [_local_groups(shards, l) for l in range(DEPTH)]
        self.conv_w = conv_w
        self.gat, self.send, self.recv = {}, {}, {}

    @staticmethod
    def _riders(l):
        return [(l, "ffn")] + ([(l + 1, "mix")] if l + 1 < DEPTH else [])

    @staticmethod
    def _grad_riders(l):
        return [(l, "ffn"), (l, "early")] + ([(l + 1, "late")] if l + 1 < DEPTH else [])

    def gather_first(self):
        w_in, conv_w = _all_gather([self.local[0]["w_in"], self.conv_w], "gather_w_in_l0")
        self.gat[(0, "mix")] = {"w_in": w_in}
        return conv_w

    def w_in(self, l):
        return _arrange_w_in(self.gat[(l, "mix")]["w_in"])

    def in_proj_ride(self, l):
        return _GatherRide([self.local[0][g] for g in MIX_GROUPS[1:]]) if l == 0 else None

    def in_proj_done(self, l, outs):
        self.gat[(l, "mix")].update(zip(MIX_GROUPS[1:], outs))

    def fwd_ride(self, l):
        return _GatherRide([self.local[ll][g] for ll, part in self._riders(l) for g in _part_groups(part)])

    def fwd_done(self, l, outs):
        outs = list(outs)
        for ll, part in self._riders(l):
            self.gat[(ll, part)] = {g: outs.pop(0) for g in _part_groups(part)}

    def mixer_weights(self, l):
        return _mixer_weights(self.gat[(l, "mix")])

    def ffn_weights(self, l):
        return _ffn_weights(self.gat[(l, "ffn")])

    def add_grads(self, l, part, gb):
        if part == "ffn":
            self.send[(l, "ffn")] = _ffn_grad_groups(gb)
        else:
            self.send.setdefault((l, "late"), {}).update(_mixer_grad_groups({g: gb[g] for g in MIX_LATE if g in gb}))

    def bwd_ride(self, l, gb_early):
        self.send[(l, "early")] = _mixer_grad_groups({g: gb_early[g] for g in MIX_EARLY})
        return _ReduceRide([self.send[(ll, part)][g] for ll, part in self._grad_riders(l) for g in _part_groups(part)])

    def bwd_done(self, l, outs):
        outs = list(outs)
        for ll, part in self._grad_riders(l):
            self.recv[(ll, part)] = {g: outs.pop(0) for g in _part_groups(part)}

    def tail_ride(self, l, small_groups):
        if l > 0:
            return None
        send = [self.send[(0, "late")][g] for g in MIX_LATE]
        by_core = [a.reshape((4, 2) + a.shape[1:]).transpose((1, 0) + tuple(range(2, a.ndim + 1))) for a in send]
        core = lax.axis_index("c")
        own = [lax.dynamic_index_in_dim(a, core, axis=0, keepdims=False) for a in by_core]
        got = _swap_with_sibling(by_core, "reduce_d2d")
        pairs = [_add_pairs(a, b, f"reduce_pair_add_{g}") for g, a, b in zip(MIX_LATE, own, got)]
        return _Combo([_ChipExchangeRide(pairs), _GatherRide(small_groups)])

    def tail_done(self, l, outs):
        self.recv[(l, "late")] = dict(zip(MIX_LATE, outs[:len(MIX_LATE)]))
        return outs[len(MIX_LATE):]

    def finish(self):
        per_layer = [{g: a for part in ("early", "late", "ffn") for g, a in self.recv[(l, part)].items()}
                     for l in range(DEPTH)]
        return _grads_from_groups({g: _sum_layers([per_layer[l][g] for l in range(DEPTH)], f"reduce_sum_{g}")
                                   for g in BIG})


def _local_step(x, positions, target, smalls, plan):
    tables = _rope_tables(positions)
    saved = []
    h, h_norm = x, None
    for l in range(DEPTH):
        h, h2, svm, wm = _mixer_fwd(h, h_norm, tables, smalls[l], plan, l)
        wf = plan.ffn_weights(l)
        next_gain = smalls[l + 1]["mix_norm_pre"] if l + 1 < DEPTH else None
        h, h_norm, svf = _ffn_fwd(h, h2, wf, smalls[l], f"l{l}", next_gain)
        saved.append((svm, svf, wm, wf))
    dy, sq = _loss_grad(h, target, "loss_grad")
    small = [None] * DEPTH
    for l in reversed(range(DEPTH)):
        svm, svf, wm, wf = saved[l]
        dmid, gbf, gsf = _ffn_bwd(dy, svf, wf, smalls[l], f"l{l}")
        plan.add_grads(l, "ffn", gbf)

        def pack_small(gs, l=l, gsf=gsf):
            if l > 0:
                return None
            return _small_groups([{**gsf, **gs, "mix_norm_pre": jnp.zeros((D_MODEL,), F32)}] + small[1:])

        dy, gsm, small_gathered = _mixer_bwd(dmid, svm, tables, wm, smalls[l], plan, l, pack_small)
        small[l] = {**gsf, **gsm}
    return sq, dy, small, small_gathered


def kernel(x, positions, mix_norm_pre, w_in, q_norm, w_uq, kv_norm, w_uk, w_uv, w_attn_o, conv_w, conv_b, conv_ln_g, conv_ln_b, w_conv_o, pool_w, pool_scale, w_pool_o, w_mix_o, mix_norm_post, ffn_norm_pre, w_gate, w_up, w_down, ffn_norm_post, loss_target, m_mix_norm_pre, m_w_in, m_q_norm, m_w_uq, m_kv_norm, m_w_uk, m_w_uv, m_w_attn_o, m_conv_w, m_conv_b, m_conv_ln_g, m_conv_ln_b, m_w_conv_o, m_pool_w, m_pool_scale, m_w_pool_o, m_w_mix_o, m_mix_norm_post, m_ffn_norm_pre, m_w_gate, m_w_up, m_w_down, m_ffn_norm_post, v_mix_norm_pre, v_w_in, v_q_norm, v_w_uq, v_kv_norm, v_w_uk, v_w_uv, v_w_attn_o, v_conv_w, v_conv_b, v_conv_ln_g, v_conv_ln_b, v_w_conv_o, v_pool_w, v_pool_scale, v_w_pool_o, v_w_mix_o, v_mix_norm_post, v_ffn_norm_pre, v_w_gate, v_w_up, v_w_down, v_ffn_norm_post):
    given = dict(locals())
    dev = 4 * lax.axis_index("x") + 2 * lax.axis_index("y") + lax.axis_index("c")

    plan = _Plan({n: given[n] for n in BIG}, conv_w)
    cw = CONV_C // N_DEV
    conv_w_full = plan.gather_first().transpose(1, 2, 0, 3).reshape(DEPTH, CONV_W, CONV_C)
    smalls = []
    for l in range(DEPTH):
        sm = {n: given[n][l] for n in SMALL if n != "conv_w"}
        sm["conv_w"] = _pad_axis(conv_w_full[l], 0, CONV_HALO)
        smalls.append(sm)

    sq, grad_x, small, small_groups = _local_step(x[0], positions[0], loss_target[0], smalls, plan)
    loss = lax.psum(0.5 / D_MODEL * jnp.sum(sq), ("x", "y", "c"))
    views = {n: lax.optimization_barrier(_lane_major(n, g)) for n, g in plan.finish().items()}
    grads = {n: _from_lane_major(n, views[n]) for n in BIG}

    small_sum = _small_from_groups([_sum_blocks(g, f"sum_small_grads_{i}") for i, g in enumerate(small_groups)])
    last = _pad_axis(small[0]["mix_norm_pre"].reshape(1, D_MODEL), 0, SUBLANES)
    last_sum = _sum_blocks(_all_gather([last], "gather_last_norm_grad")[0], "sum_last_norm_grad")[0]
    small_sum["mix_norm_pre"] = small_sum["mix_norm_pre"].at[0].set(last_sum)
    for n in SMALL:
        grads[n] = small_sum[n]
    grads["conv_w"] = lax.dynamic_slice_in_dim(small_sum["conv_w"], dev * cw, cw, axis=2)

    delta, new_m, new_v = {}, {}, {}
    for n in WEIGHTS:
        g_view = views[n] if n in views else grads[n]
        w_view, m_view, v_view = [_lane_major(n, given[k]) for k in (n, "m_" + n, "v_" + n)]
        res = _adamw(w_view, g_view, m_view, v_view, f"adamw_{n}")
        delta[n], new_m[n], new_v[n] = [_from_lane_major(n, r) for r in res]
    return (loss, grad_x[None], *[grads[n] for n in WEIGHTS], *[delta[n] for n in WEIGHTS],
            *[new_m[n] for n in WEIGHTS], *[new_v[n] for n in WEIGHTS])
```

```python
import functools
import math

import jax
import jax.numpy as jnp
from jax import lax
from jax.experimental import pallas as pl
from jax.experimental.pallas import tpu as pltpu

F32, BF16 = jnp.float32, jnp.bfloat16
MESH = pl.DeviceIdType.MESH

LANES = 128
SUBLANES = 8
VMEM_LIMIT_BYTES = 56 * 1024 * 1024
MATMUL_VMEM_BYTES = 40 * 1024 * 1024

N_DEV = 8
D_MODEL = 1024
DEPTH = 2
N_HEADS = 8
QK_NOPE, QK_ROPE, V_HEAD = 64, 32, 64
HEAD_PAD = LANES
Q_RANK, KV_RANK = 384, 256
ROPE_THETA = 10000.0
CONV_C, CONV_W = 512, 31
CONV_HALO = 32
POOL_WINDOWS = (2, 4, 8, 16)
POOL_C, POOL_G = 512, 4
POOL_GD = POOL_C // POOL_G
D_FF = 2816
FF_SHARD = D_FF // N_DEV
FF_SHARD_PAD = 3 * LANES
D_FF_PAD = N_DEV * FF_SHARD_PAD
W_IN_SHARD = 660
EPS = 1e-6
ATTN_SCALE = 1.0 / math.sqrt(QK_NOPE + QK_ROPE)
LOG2E = 1.4426950408889634
LR, B1, B2, ADAM_EPS, WD, STEP = 0.001, 0.9, 0.999, 1e-08, 0.01, 10

Z_W = 5376
ZC_GATE = (1024, 0)
ZC_GATES = (3072, 0)
ZC_CONV_A = (512, 6)
ZC_CONV_G = (512, 7)
ZC_CONV = (1024, 3)
ZC_POOL = (512, 8)
ZC_Q = (384, 12)
ZC_KR = (128, 39)
ZC_KV = (256, 20)
ZC_QKR = (768, 6)
W_IN_PIECES = ((0, 384, 4608), (384, 640, 5120), (640, 672, 5056), (672, 1696, 3072), (1696, 2208, 4096),
               (2208, 5280, 0))

BIG = ("w_in", "w_uq", "w_uk", "w_uv", "w_attn_o", "w_conv_o", "w_pool_o", "w_mix_o", "w_gate", "w_up", "w_down")
SMALL = ("mix_norm_pre", "q_norm", "kv_norm", "conv_w", "conv_b", "conv_ln_g", "conv_ln_b", "pool_w", "pool_scale",
         "mix_norm_post", "ffn_norm_pre", "ffn_norm_post")
WEIGHTS = ("mix_norm_pre", "w_in", "q_norm", "w_uq", "kv_norm", "w_uk", "w_uv", "w_attn_o", "conv_w", "conv_b",
           "conv_ln_g", "conv_ln_b", "w_conv_o", "pool_w", "pool_scale", "w_pool_o", "w_mix_o", "mix_norm_post",
           "ffn_norm_pre", "w_gate", "w_up", "w_down", "ffn_norm_post")


def _params(*semantics):
    return pltpu.CompilerParams(dimension_semantics=semantics, vmem_limit_bytes=VMEM_LIMIT_BYTES)


def _tile(dim, cap):
    if dim <= cap:
        return dim
    for t in range(cap - cap % LANES, 0, -LANES):
        if dim % t == 0:
            return t
    raise ValueError(f"no tile for {dim} under {cap}")


def _row_tile(rows, row_bytes, budget=1 << 20):
    if rows * row_bytes <= budget:
        return rows
    cap = max(16, budget // row_bytes)
    for t in range(cap - cap % 16, 0, -16):
        if rows % t == 0:
            return t
    return rows


def _rows(ts, width, cidx=0):
    return pl.BlockSpec((ts, width), lambda i: (i, cidx))


def _fixed(shape):
    return pl.BlockSpec(shape, lambda *_: (0,) * len(shape))


def _sigmoid(x):
    return 1.0 / (1.0 + jnp.exp(-x))


def _matmul(a, b, mode, out_dtype, name, add=None, blocked=False, ride=None):
    nb = n_blk = 0
    blocked = blocked or b.ndim == 3
    if mode == "nn":
        (m, k) = a.shape
        n = b.shape[0] * b.shape[2] if blocked else b.shape[1]
    elif mode == "nt":
        (m, k) = a.shape
        n = b.shape[1] if blocked else b.shape[0]
    else:
        (k, m), n = a.shape, b.shape[1]
    if blocked:
        nb = b.shape[2] if mode != "tn" else n // N_DEV
    unit = nb if blocked and mode != "nt" else LANES
    out_bytes = jnp.dtype(out_dtype).itemsize + (4 if add is not None else 0)
    best = None
    for tn_c in range(unit, min(n, 1536) + 1, unit):
        for tm_c in sorted({256, 512, 1024, 2048, min(m, 2048)}):
            if n % tn_c or m % tm_c or (blocked and mode != "nt" and N_DEV % (tn_c // nb)):
                continue
            vmem = 2 * (tm_c * k * 2 + tn_c * k * 2 + tm_c * tn_c * out_bytes) + tm_c * tn_c * 4 + tn_c * k * 2
            if vmem <= MATMUL_VMEM_BYTES and (best is None or tm_c * tn_c / (tm_c + tn_c) > best[0]):
                best = (tm_c * tn_c / (tm_c + tn_c), tm_c, tn_c)
    if best is None:
        raise ValueError(f"{name}: no tiles for {m}x{n}x{k}")
    _, tm, tn = best
    if blocked:
        n_blk = N_DEV if mode == "nt" else tn // nb
    dims = {"nn": ((1,), (0,)), "nt": ((1,), (1,)), "tn": ((0,), (0,))}[mode]
    a_spec = pl.BlockSpec((k, tm), lambda i, j: (0, i)) if mode == "tn" else pl.BlockSpec((tm, k), lambda i, j: (i, 0))
    b_spec = pl.BlockSpec((tn, k), lambda i, j: (j, 0)) if mode == "nt" else pl.BlockSpec((k, tn), lambda i, j: (0, j))
    o_spec = pl.BlockSpec((tm, tn), lambda i, j: (i, j))
    out_shape = jax.ShapeDtypeStruct((m, n), out_dtype)
    if blocked and mode == "nn":
        b_spec = pl.BlockSpec((n_blk, k, nb), lambda i, j: (j, 0, 0))
    elif blocked and mode == "nt":
        b_spec = pl.BlockSpec((n_blk, tn, nb), lambda i, j: (0, j, 0))
    elif blocked:
        o_spec = pl.BlockSpec((n_blk, tm, nb), lambda i, j: (j, i, 0))
        out_shape = jax.ShapeDtypeStruct((N_DEV, m, nb), out_dtype)
    has_add = add is not None
    grid = (m // tm, n // tn)

    def body(*refs):
        (a_ref, b_ref, *rest), start, finish = _ride_hooks(ride, refs, 3 if has_add else 2, 1, grid)
        start()
        o_ref = rest[-1]
        if blocked and mode != "tn":
            bv = jnp.concatenate([b_ref[c] for c in range(n_blk)], axis=1) if n_blk > 1 else b_ref[0]
        else:
            bv = b_ref[...]
        total = lax.dot_general(a_ref[...], bv, (dims, ((), ())), preferred_element_type=F32)
        if has_add:
            total = total + rest[0][...]
        if blocked and mode == "tn":
            for c in range(n_blk):
                o_ref[c] = total[:, c * nb:(c + 1) * nb].astype(o_ref.dtype)
        else:
            o_ref[...] = total.astype(o_ref.dtype)
        finish()

    operands = (a, b, add) if has_add else (a, b)
    (out,), rode = _ride_call(ride, body, name, (out_shape,), grid, [a_spec, b_spec] + ([o_spec] if has_add else []),
                              (o_spec,), ("parallel", "parallel"), operands)
    return out if ride is None else (out, rode)


def _rms_fwd(x, win, gain, out_dtype, name, res=None, then=None):
    width, cidx = win
    s = x.shape[0]
    ts = min(s, 512)
    has_res, has_then = res is not None, then is not None

    def norm(v, g_ref):
        return (v * lax.rsqrt(jnp.mean(v * v, axis=-1, keepdims=True) + EPS)) * g_ref[...]

    def body(x_ref, g_ref, *rest):
        y = norm(x_ref[...].astype(F32), g_ref)
        if has_res:
            y = rest[0][...] + y
        o_ref = rest[-2] if has_then else rest[-1]
        o_ref[...] = y.astype(o_ref.dtype)
        if has_then:
            rest[-1][...] = norm(y, rest[-3]).astype(BF16)

    ops = (x, gain.reshape(1, width)) + ((res,) if has_res else ()) + ((then.reshape(1, width),) if has_then else ())
    out_shape = (jax.ShapeDtypeStruct((s, width), out_dtype),) + ((jax.ShapeDtypeStruct((s, width), BF16),) * has_then)
    out = pl.pallas_call(
        body, name=name, out_shape=out_shape, grid=(s // ts,),
        in_specs=([_rows(ts, width, cidx), _fixed((1, width))] + ([_rows(ts, width)] if has_res else [])
                  + ([_fixed((1, width))] if has_then else [])),
        out_specs=(_rows(ts, width),) * len(out_shape), compiler_params=_params("parallel"))(*ops)
    return out if has_then else out[0]


def _into(dz, n_inputs, out_index):
    return dict(in_specs=[ANY], operands=(dz,), input_output_aliases={n_inputs: out_index},
                out_shape=jax.ShapeDtypeStruct(dz.shape, dz.dtype))


def _rms_bwd(x, win, gain, dy, out_dtype, name, add=None, dz=None):
    width, cidx = win
    s = x.shape[0]
    ts = min(s, 512)
    has_add = add is not None

    def body(x_ref, g_ref, dy_ref, *rest):
        dx_ref, dg_ref = rest[-2], rest[-1]
        xv = x_ref[...].astype(F32)
        r = lax.rsqrt(jnp.mean(xv * xv, axis=-1, keepdims=True) + EPS)
        xh = xv * r
        dyv = dy_ref[...].astype(F32)
        dyg = dyv * g_ref[...]
        dx = r * (dyg - xh * jnp.mean(dyg * xh, axis=-1, keepdims=True))
        if has_add:
            dx = dx + rest[0][...]
        dx_ref[...] = dx.astype(dx_ref.dtype)

        @pl.when(pl.program_id(0) == 0)
        def _():
            dg_ref[...] = jnp.zeros_like(dg_ref)

        dg_ref[...] += jnp.sum(dyv * xh, axis=0, keepdims=True)

    ops = (x, gain.reshape(1, width), dy) + ((add,) if has_add else ())
    in_specs = [_rows(ts, width, cidx), _fixed((1, width)), _rows(ts, width)] + ([_rows(ts, width)] if has_add else [])
    dx_shape, dx_spec, alias = jax.ShapeDtypeStruct((s, width), out_dtype), _rows(ts, width), {}
    if dz is not None:
        into = _into(dz, len(ops), 0)
        ops, in_specs, alias = ops + into["operands"], in_specs + into["in_specs"], into["input_output_aliases"]
        dx_shape, dx_spec = into["out_shape"], _rows(ts, width, cidx)
    dx, dg = pl.pallas_call(
        body, name=name, out_shape=(dx_shape, jax.ShapeDtypeStruct((1, width), F32)), grid=(s // ts,),
        in_specs=in_specs, out_specs=(dx_spec, _fixed((1, width))), input_output_aliases=alias,
        compiler_params=_params("arbitrary"))(*ops)
    return dx, dg.reshape(width)


def _rope(x, c, s1, s2):
    return x * c + pltpu.roll(x, 16, 1) * s1 + pltpu.roll(x, LANES - 16, 1) * s2


def _rope_t(g, c, s1, s2):
    return g * c + pltpu.roll(g * s1, LANES - 16, 1) + pltpu.roll(g * s2, 16, 1)


def _rope_tables(positions):
    inv_freq = ROPE_THETA ** (-jnp.arange(0, QK_ROPE, 2, dtype=F32) / QK_ROPE)
    ang = positions.astype(F32)[:, None] * inv_freq
    cos, sin = jnp.cos(ang), jnp.sin(ang)
    n = positions.shape[0]
    one, zero = jnp.ones((n, 1), F32), jnp.zeros((n, 1), F32)
    c = jnp.concatenate([jnp.tile(one, (1, QK_NOPE)), cos, cos, jnp.tile(one, (1, 32))], axis=1)
    s1 = jnp.concatenate([jnp.tile(zero, (1, QK_NOPE + 16)), sin, jnp.tile(zero, (1, 32))], axis=1)
    s2 = jnp.concatenate([jnp.tile(zero, (1, QK_NOPE)), -sin, jnp.tile(zero, (1, 48))], axis=1)
    return c, s1, s2


def _qkv_up_fwd(z, q_gain, kv_gain, w_uq, w_uk, w_uv, tables, name):
    s = z.shape[0]
    ts = min(s, 512)
    hw = N_HEADS * HEAD_PAD
    kv0 = Q_RANK + LANES

    def norm(v, g_ref):
        return ((v * lax.rsqrt(jnp.mean(v * v, axis=-1, keepdims=True) + EPS)) * g_ref[...]).astype(BF16)

    def body(z_ref, gq_ref, gkv_ref, wq_ref, wk_ref, wv_ref, c_ref, s1_ref, s2_ref, cq_ref, ckv_ref, q_ref, k_ref, v_ref):
        c, s1, s2 = c_ref[...], s1_ref[...], s2_ref[...]
        cqv = norm(z_ref[:, pl.ds(0, Q_RANK)].astype(F32), gq_ref)
        ckvv = norm(z_ref[:, pl.ds(kv0, KV_RANK)].astype(F32), gkv_ref)
        cq_ref[...] = cqv
        ckv_ref[...] = ckvv
        kr = _rope(z_ref[:, pl.ds(Q_RANK, LANES)].astype(F32), c, s1, s2)
        for h in range(N_HEADS):
            sl = slice(h * HEAD_PAD, (h + 1) * HEAD_PAD)
            q_ref[:, sl] = _rope(jnp.dot(cqv, wq_ref[h], preferred_element_type=F32), c, s1, s2).astype(BF16)
            k_ref[:, sl] = (jnp.dot(ckvv, wk_ref[h], preferred_element_type=F32) + kr).astype(BF16)
            v_ref[:, sl] = jnp.dot(ckvv, wv_ref[h], preferred_element_type=F32).astype(BF16)

    tab = _rows(ts, LANES)
    wide = jax.ShapeDtypeStruct((s, hw), BF16)
    return pl.pallas_call(
        body, name=name,
        out_shape=(jax.ShapeDtypeStruct((s, Q_RANK), BF16), jax.ShapeDtypeStruct((s, KV_RANK), BF16), wide, wide, wide),
        grid=(s // ts,),
        in_specs=[_rows(ts, *ZC_QKR), _fixed((1, Q_RANK)), _fixed((1, KV_RANK)), _fixed(w_uq.shape), _fixed(w_uk.shape),
                  _fixed(w_uv.shape), tab, tab, tab],
        out_specs=(_rows(ts, Q_RANK), _rows(ts, KV_RANK)) + (_rows(ts, hw),) * 3, compiler_params=_params("parallel"))(
            z, q_gain.reshape(1, -1), kv_gain.reshape(1, -1), w_uq, w_uk, w_uv, *tables)


def _qkv_up_bwd(dq, dk, dv, z, w_uq, w_uk, w_uv, tables, q_gain, kv_gain, dz, name):
    s = dq.shape[0]
    ts = min(s, 512)
    hw = N_HEADS * HEAD_PAD
    zw = ZC_QKR[0]
    kv0 = Q_RANK + LANES
    dims_nt = (((1,), (1,)), ((), ()))

    def norm_bwd(xv, g_ref, dyv):
        r = lax.rsqrt(jnp.mean(xv * xv, axis=-1, keepdims=True) + EPS)
        xh = xv * r
        dyg = dyv * g_ref[...]
        return r * (dyg - xh * jnp.mean(dyg * xh, axis=-1, keepdims=True)), jnp.sum(dyv * xh, axis=0, keepdims=True)

    def body(dq_ref, dk_ref, dv_ref, z_ref, wq_ref, wk_ref, wv_ref, c_ref, s1_ref, s2_ref, gq_ref, gkv_ref, _,
             dqf_ref, dkf_ref, dz_ref, dgq_ref, dgkv_ref):
        c, s1, s2 = c_ref[...], s1_ref[...], s2_ref[...]
        ksum = jnp.zeros((ts, HEAD_PAD), F32)
        dcq = jnp.zeros((ts, Q_RANK), F32)
        dckv = jnp.zeros((ts, KV_RANK), F32)
        for h in range(N_HEADS):
            sl = slice(h * HEAD_PAD, (h + 1) * HEAD_PAD)
            dqh = _rope_t(dq_ref[:, sl], c, s1, s2).astype(BF16)
            dkv = dk_ref[:, sl]
            dkh = dkv.astype(BF16)
            dqf_ref[:, sl] = dqh
            dkf_ref[:, sl] = dkh
            ksum = ksum + dkv
            dcq = dcq + lax.dot_general(dqh, wq_ref[h], dims_nt, preferred_element_type=F32)
            dckv = dckv + (lax.dot_general(dkh, wk_ref[h], dims_nt, preferred_element_type=F32)
                           + lax.dot_general(dv_ref[:, sl], wv_ref[h], dims_nt, preferred_element_type=F32))
        dxq, dgq = norm_bwd(z_ref[:, pl.ds(0, Q_RANK)].astype(F32), gq_ref, dcq)
        dxkv, dgkv = norm_bwd(z_ref[:, pl.ds(kv0, KV_RANK)].astype(F32), gkv_ref, dckv)
        lane = lax.broadcasted_iota(jnp.int32, (ts, HEAD_PAD), 1)
        in_rope = (lane >= QK_NOPE) & (lane < QK_NOPE + QK_ROPE)
        dz_ref[:, pl.ds(0, Q_RANK)] = dxq.astype(BF16)
        dz_ref[:, pl.ds(Q_RANK, LANES)] = jnp.where(in_rope, _rope_t(ksum, c, s1, s2), 0.0).astype(BF16)
        dz_ref[:, pl.ds(kv0, KV_RANK)] = dxkv.astype(BF16)

        @pl.when(pl.program_id(0) == 0)
        def _():
            dgq_ref[...] = jnp.zeros_like(dgq_ref)
            dgkv_ref[...] = jnp.zeros_like(dgkv_ref)

        dgq_ref[...] += dgq
        dgkv_ref[...] += dgkv

    tab = _rows(ts, LANES)
    into = _into(dz, 12, 2)
    dqf, dkf, dz, dgq, dgkv = pl.pallas_call(
        body, name=name,
        out_shape=(jax.ShapeDtypeStruct((s, hw), BF16), jax.ShapeDtypeStruct((s, hw), BF16), into["out_shape"],
                   jax.ShapeDtypeStruct((1, Q_RANK), F32), jax.ShapeDtypeStruct((1, KV_RANK), F32)),
        grid=(s // ts,),
        in_specs=[_rows(ts, hw), _rows(ts, hw), _rows(ts, hw), _rows(ts, *ZC_QKR), _fixed(w_uq.shape), _fixed(w_uk.shape),
                  _fixed(w_uv.shape), tab, tab, tab, _fixed((1, Q_RANK)), _fixed((1, KV_RANK))] + into["in_specs"],
        out_specs=(_rows(ts, hw), _rows(ts, hw), _rows(ts, *ZC_QKR), _fixed((1, Q_RANK)), _fixed((1, KV_RANK))),
        input_output_aliases=into["input_output_aliases"], compiler_params=_params("arbitrary"))(
            dq, dk, dv, z, w_uq, w_uk, w_uv, *tables, q_gain.reshape(1, -1), kv_gain.reshape(1, -1), dz)
    return dqf, dkf, dz, dgq.reshape(-1), dgkv.reshape(-1)


def _attn_tile(s):
    return min(s, 512)


def _raw_scores(q, k, masked, row0=0):
    sc = lax.dot_general(q, k, (((1,), (1,)), ((), ())), preferred_element_type=F32)
    if masked:
        rows = row0 + lax.broadcasted_iota(jnp.int32, sc.shape, 0)
        cols = lax.broadcasted_iota(jnp.int32, sc.shape, 1)
        sc = jnp.where(cols <= rows, sc, -jnp.inf)
    return sc


def _ride_hooks(ride, refs, n_in, n_out, grid):
    if ride is None:
        return refs, lambda: None, lambda: None
    n = len(ride.arrays)
    own = refs[:n_in] + refs[n_in + n:n_in + n + n_out]
    ins, outs, sems = refs[n_in:n_in + n], refs[n_in + n + n_out:n_in + 2 * n + n_out], refs[n_in + 2 * n + n_out:]
    at_first = functools.reduce(lambda a, b: a & b, [pl.program_id(ax) == 0 for ax in range(len(grid))])
    at_last = functools.reduce(lambda a, b: a & b, [pl.program_id(ax) == g - 1 for ax, g in enumerate(grid)])
    return own, lambda: pl.when(at_first)(lambda: ride.start(ins, outs, sems)), \
        lambda: pl.when(at_last)(lambda: ride.finish(ins, outs, sems))


def _ride_call(ride, body, name, out_shape, grid, in_specs, out_specs, semantics, operands):
    n = 0 if ride is None else len(ride.arrays)
    res = pl.pallas_call(
        body, name=name, out_shape=tuple(out_shape) + (tuple(ride.out_shape) if n else ()), grid=grid,
        in_specs=list(in_specs) + [ANY] * n, out_specs=tuple(out_specs) + (ANY,) * n,
        scratch_shapes=list(ride.scratch) if n else [],
        compiler_params=_params(*(("arbitrary",) * len(grid) if n else semantics)))(*operands, *(ride.arrays if n else ()))
    return res[:len(out_shape)], list(res[len(out_shape):])


def _flash_fwd(q, k, v, name, ride=None):
    s = q.shape[0]
    t = _attn_tile(s)
    c2 = ATTN_SCALE * LOG2E
    grid = (N_HEADS, s // t)

    def body(*refs):
        (q_ref, k_ref, v_ref, o_ref, lse_ref), start, finish = _ride_hooks(ride, refs, 3, 2, grid)
        start()
        i = pl.program_id(1)
        qv = q_ref[...]

        def chunk(j, carry, masked):
            m_old, l_old, acc = carry
            at = pl.ds(pl.multiple_of(j * t, t), t)
            sc = _raw_scores(qv, k_ref[at, :], masked)
            m_new = jnp.maximum(m_old, jnp.max(sc, axis=-1, keepdims=True))
            p = jnp.exp2((sc - m_new) * c2)
            alpha = jnp.exp2((m_old - m_new) * c2)
            l_new = alpha * l_old + jnp.sum(p, axis=-1, keepdims=True)
            acc = alpha * acc + jnp.dot(p.astype(BF16), v_ref[at, :], preferred_element_type=F32)
            return m_new, l_new, acc

        init = (jnp.full((t, 1), -jnp.inf, F32), jnp.zeros((t, 1), F32), jnp.zeros((t, HEAD_PAD), F32))
        carry = lax.fori_loop(0, i, lambda j, cr: chunk(j, cr, False), init)
        m_fin, l_fin, acc = chunk(i, carry, True)
        o_ref[...] = (acc / l_fin).astype(o_ref.dtype)
        lse_ref[...] = jnp.broadcast_to(m_fin * ATTN_SCALE + jnp.log(l_fin), (t, HEAD_PAD))
        finish()

    qo = pl.BlockSpec((t, HEAD_PAD), lambda h, i: (i, h))
    whole = pl.BlockSpec((s, HEAD_PAD), lambda h, i: (0, h))
    return _ride_call(
        ride, body, name, (jax.ShapeDtypeStruct(q.shape, BF16), jax.ShapeDtypeStruct(q.shape, F32)), grid,
        [qo, whole, whole], (qo, qo), ("parallel", "parallel"), (q, k, v))


def _attn_out_bwd(dya, w_attn_o, o, name):
    s, d = dya.shape
    hw = N_HEADS * HEAD_PAD
    t = _attn_tile(s)

    def body(d_ref, w_ref, o_ref, delta_ref, dob_ref):
        wv = jnp.concatenate([w_ref[c] for c in range(N_DEV)], axis=1)
        do = lax.dot_general(d_ref[...], wv, (((1,), (1,)), ((), ())), preferred_element_type=F32)
        for h in range(N_HEADS):
            sl = slice(h * HEAD_PAD, (h + 1) * HEAD_PAD)
            dov = do[:, sl]
            delta_ref[:, sl] = jnp.broadcast_to(jnp.sum(dov * o_ref[:, sl].astype(F32), axis=-1, keepdims=True),
                                                (t, HEAD_PAD))
            dob_ref[:, sl] = dov.astype(BF16)

    blk = _rows(t, hw)
    return pl.pallas_call(
        body, name=name, out_shape=(jax.ShapeDtypeStruct(o.shape, F32), jax.ShapeDtypeStruct(o.shape, BF16)),
        grid=(s // t,), in_specs=[_rows(t, d), _fixed(w_attn_o.shape), blk], out_specs=(blk, blk),
        compiler_params=_params("parallel"))(dya, w_attn_o, o)


def _flash_bwd(q, k, v, do, lse, delta, name, ride=None):
    s = q.shape[0]
    t = _attn_tile(s)
    nt = s // t
    c2 = ATTN_SCALE * LOG2E
    grid = (N_HEADS, nt)

    def body(*refs):
        (q_ref, k_ref, v_ref, do_ref, lse_ref, delta_ref, dq_ref, dk_ref, dv_ref), start, finish = _ride_hooks(
            ride, refs, 6, 3, grid)
        start()
        j = pl.program_id(1)
        kv, vv = k_ref[...], v_ref[...]

        @pl.when(j == 0)
        def _():
            dq_ref[...] = jnp.zeros_like(dq_ref)

        def chunk(i, carry, masked):
            dk_acc, dv_acc = carry
            at = pl.ds(pl.multiple_of(i * t, t), t)
            qi, doi = q_ref[at, :], do_ref[at, :]
            sc = _raw_scores(qi, kv, masked)
            p = jnp.exp2(sc * c2 - lse_ref[at, pl.ds(0, 1)] * LOG2E)
            dp = lax.dot_general(doi, vv, (((1,), (1,)), ((), ())), preferred_element_type=F32)
            ds = (p * (dp - delta_ref[at, pl.ds(0, 1)])).astype(BF16)
            dv_acc = dv_acc + lax.dot_general(p.astype(BF16), doi, (((0,), (0,)), ((), ())), preferred_element_type=F32)
            dk_acc = dk_acc + lax.dot_general(ds, qi, (((0,), (0,)), ((), ())), preferred_element_type=F32)
            dq_ref[at, :] += jnp.dot(ds, kv, preferred_element_type=F32) * ATTN_SCALE
            return dk_acc, dv_acc

        zero = jnp.zeros((t, HEAD_PAD), F32)
        carry = chunk(j, (zero, zero), True)
        dk_acc, dv_acc = lax.fori_loop(j + 1, nt, lambda i, cr: chunk(i, cr, False), carry)
        dk_ref[...] = dk_acc * ATTN_SCALE
        dv_ref[...] = dv_acc.astype(BF16)
        finish()

    blk = pl.BlockSpec((t, HEAD_PAD), lambda h, j: (j, h))
    whole = pl.BlockSpec((s, HEAD_PAD), lambda h, j: (0, h))
    return _ride_call(
        ride, body, name, (jax.ShapeDtypeStruct(q.shape, F32), jax.ShapeDtypeStruct(q.shape, F32),
                           jax.ShapeDtypeStruct(q.shape, BF16)), grid,
        [whole, blk, blk, whole, whole, whole], (whole, blk, blk), ("parallel", "arbitrary"), (q, k, v, do, lse, delta))


def _conv_tile(s):
    return min(s, 256)


def _halo_before(t, width, cidx):
    per = t // CONV_HALO
    return pl.BlockSpec((CONV_HALO, width), lambda i: (jnp.maximum(i * per - 1, 0), cidx))


def _halo_after(t, width, cidx, n_tiles):
    per = t // CONV_HALO
    last = n_tiles * per - 1
    return pl.BlockSpec((CONV_HALO, width), lambda i: (jnp.minimum((i + 1) * per, last), cidx))


def _fill_glu(hbuf, ap_ref, gp_ref, a_ref, g_ref, t):
    first = pl.program_id(0) == 0
    hbuf[pl.ds(0, CONV_HALO), :] = jnp.where(first, 0.0, ap_ref[...].astype(F32) * _sigmoid(gp_ref[...].astype(F32)))
    hbuf[pl.ds(CONV_HALO, t), :] = a_ref[...].astype(F32) * _sigmoid(g_ref[...].astype(F32))


def _phase_copies(dst, src, t):
    n = t + CONV_HALO - SUBLANES
    for s in range(1, SUBLANES):
        dst[s, pl.ds(0, n), :] = src[pl.ds(s, n), :]


def _window(phases, src, k, t):
    if k % SUBLANES == 0:
        return src[pl.ds(k, t), :]
    return phases[k % SUBLANES, pl.ds(k - k % SUBLANES, t), :]


def _layer_norm_parts(co):
    mu = jnp.mean(co, axis=-1, keepdims=True)
    xc = co - mu
    rstd = lax.rsqrt(jnp.mean(xc * xc, axis=-1, keepdims=True) + EPS)
    return xc * rstd, rstd


def _conv_fwd(z, conv_w, conv_b, ln_g, ln_b, name):
    s = z.shape[0]
    t = _conv_tile(s)
    off = CONV_HALO - (CONV_W - 1)

    def body(ap_ref, gp_ref, a_ref, g_ref, w_ref, b_ref, lg_ref, lb_ref, hc_ref, co_ref, hbuf, hph):
        _fill_glu(hbuf, ap_ref, gp_ref, a_ref, g_ref, t)
        _phase_copies(hph, hbuf, t)
        acc = jnp.zeros((t, CONV_C), F32) + b_ref[...]
        for j in range(CONV_W):
            acc = acc + _window(hph, hbuf, off + j, t) * w_ref[pl.ds(j, 1), :]
        co_ref[...] = acc
        xh, _ = _layer_norm_parts(acc)
        y = xh * lg_ref[...] + lb_ref[...]
        hc_ref[...] = (y * _sigmoid(y)).astype(BF16)

    vec = _fixed((1, CONV_C))
    return pl.pallas_call(
        body, name=name, out_shape=(jax.ShapeDtypeStruct((s, CONV_C), BF16), jax.ShapeDtypeStruct((s, CONV_C), F32)),
        grid=(s // t,),
        in_specs=[_halo_before(t, *ZC_CONV_A), _halo_before(t, *ZC_CONV_G), _rows(t, *ZC_CONV_A), _rows(t, *ZC_CONV_G),
                  _fixed((CONV_HALO, CONV_C)), vec, vec, vec],
        out_specs=(_rows(t, CONV_C), _rows(t, CONV_C)),
        scratch_shapes=[pltpu.VMEM((t + CONV_HALO, CONV_C), F32), pltpu.VMEM((SUBLANES, t + CONV_HALO, CONV_C), F32)],
        compiler_params=_params("parallel"))(z, z, z, z, conv_w, conv_b.reshape(1, -1), ln_g.reshape(1, -1),
                                             ln_b.reshape(1, -1))


def _conv_bwd_norm(dhc, co, ln_g, ln_b, name):
    s = co.shape[0]
    t = min(s, 512)

    def body(dhc_ref, co_ref, lg_ref, lb_ref, dco_ref, dg_ref, db_ref, dcb_ref):
        xh, rstd = _layer_norm_parts(co_ref[...])
        y = xh * lg_ref[...] + lb_ref[...]
        sg = _sigmoid(y)
        dy = dhc_ref[...] * (sg * (1.0 + y * (1.0 - sg)))
        dxh = dy * lg_ref[...]
        dco = rstd * (dxh - jnp.mean(dxh, axis=-1, keepdims=True) - xh * jnp.mean(dxh * xh, axis=-1, keepdims=True))
        dco_ref[...] = dco

        @pl.when(pl.program_id(0) == 0)
        def _():
            dg_ref[...] = jnp.zeros_like(dg_ref)
            db_ref[...] = jnp.zeros_like(db_ref)
            dcb_ref[...] = jnp.zeros_like(dcb_ref)

        dg_ref[...] += jnp.sum(dy * xh, axis=0, keepdims=True)
        db_ref[...] += jnp.sum(dy, axis=0, keepdims=True)
        dcb_ref[...] += jnp.sum(dco, axis=0, keepdims=True)

    vec = _fixed((1, CONV_C))
    one = jax.ShapeDtypeStruct((1, CONV_C), F32)
    dco, dg, db, dcb = pl.pallas_call(
        body, name=name, out_shape=(jax.ShapeDtypeStruct((s, CONV_C), F32), one, one, one), grid=(s // t,),
        in_specs=[_rows(t, CONV_C), _rows(t, CONV_C), vec, vec], out_specs=(_rows(t, CONV_C), vec, vec, vec),
        compiler_params=_params("arbitrary"))(dhc, co, ln_g.reshape(1, -1), ln_b.reshape(1, -1))
    return dco, dg.reshape(-1), db.reshape(-1), dcb.reshape(-1)


def _conv_bwd_taps(dco, z, conv_w, dz, name):
    s = z.shape[0]
    t = _conv_tile(s)
    nt = s // t
    off = CONV_HALO - (CONV_W - 1)

    def body(ap_ref, gp_ref, a_ref, g_ref, d_ref, dn_ref, w_ref, _, du_ref, dw_ref, hbuf, dbuf, hph, dph):
        i = pl.program_id(0)
        _fill_glu(hbuf, ap_ref, gp_ref, a_ref, g_ref, t)
        dbuf[pl.ds(0, t), :] = d_ref[...]
        dbuf[pl.ds(t, CONV_HALO), :] = jnp.where(i == nt - 1, 0.0, dn_ref[...])
        _phase_copies(hph, hbuf, t)
        _phase_copies(dph, dbuf, t)

        @pl.when(i == 0)
        def _():
            dw_ref[...] = jnp.zeros_like(dw_ref)

        dcur = d_ref[...]
        dh = jnp.zeros((t, CONV_C), F32)
        for j in range(CONV_W):
            dh = dh + _window(dph, dbuf, CONV_W - 1 - j, t) * w_ref[pl.ds(j, 1), :]
            dw_ref[pl.ds(j, 1), :] += jnp.sum(dcur * _window(hph, hbuf, off + j, t), axis=0, keepdims=True)
        a, sg = a_ref[...].astype(F32), _sigmoid(g_ref[...].astype(F32))
        du_ref[:, pl.ds(0, CONV_C)] = (dh * sg).astype(BF16)
        du_ref[:, pl.ds(CONV_C, CONV_C)] = (dh * a * sg * (1.0 - sg)).astype(BF16)

    into = _into(dz, 7, 0)
    return pl.pallas_call(
        body, name=name, out_shape=(into["out_shape"], jax.ShapeDtypeStruct((CONV_HALO, CONV_C), F32)), grid=(nt,),
        in_specs=[_halo_before(t, *ZC_CONV_A), _halo_before(t, *ZC_CONV_G), _rows(t, *ZC_CONV_A), _rows(t, *ZC_CONV_G),
                  _rows(t, CONV_C), _halo_after(t, CONV_C, 0, nt), _fixed((CONV_HALO, CONV_C))] + into["in_specs"],
        out_specs=(_rows(t, *ZC_CONV), _fixed((CONV_HALO, CONV_C))), input_output_aliases=into["input_output_aliases"],
        scratch_shapes=[pltpu.VMEM((t + CONV_HALO, CONV_C), F32), pltpu.VMEM((t + CONV_HALO, CONV_C), F32),
                        pltpu.VMEM((SUBLANES, t + CONV_HALO, CONV_C), F32),
                        pltpu.VMEM((SUBLANES, t + CONV_HALO, CONV_C), F32)],
        compiler_params=_params("arbitrary"))(z, z, z, z, dco, dco, conv_w, dz)


def _pool_tile(s):
    return min(s, 512)


def _pool_counts(row0, n, window):
    rows = row0 + lax.broadcasted_iota(jnp.int32, (n, POOL_GD), 0)
    return jnp.minimum(rows + 1, window).astype(F32)


def _pool_diff(ubuf, gi, window, row0, t):
    lanes = pl.ds(gi * POOL_GD, POOL_GD)
    tot = ubuf[pl.ds(CONV_HALO, t), lanes]
    cur = tot
    for back in range(1, window):
        tot = tot + ubuf[pl.ds(CONV_HALO - back, t), lanes]
    return tot / _pool_counts(row0, t, window) - cur


def _pool_fwd(z, pool_w, pool_scale, name):
    s = z.shape[0]
    t = _pool_tile(s)

    def body(up_ref, u_ref, w_ref, sc_ref, m_ref, ubuf):
        i = pl.program_id(0)
        ubuf[pl.ds(0, CONV_HALO), :] = jnp.where(i == 0, 0.0, up_ref[...].astype(F32))
        ubuf[pl.ds(CONV_HALO, t), :] = u_ref[...].astype(F32)
        for gi, window in enumerate(POOL_WINDOWS):
            d = _pool_diff(ubuf, gi, window, i * t, t)
            mm = jnp.dot(d.astype(BF16), w_ref[gi].astype(BF16), preferred_element_type=F32)
            lanes = pl.ds(gi * POOL_GD, POOL_GD)
            m_ref[:, lanes] = (mm * sc_ref[:, lanes]).astype(BF16)

    return pl.pallas_call(
        body, name=name, out_shape=jax.ShapeDtypeStruct((s, POOL_C), BF16), grid=(s // t,),
        in_specs=[_halo_before(t, *ZC_POOL), _rows(t, *ZC_POOL), _fixed((POOL_G, POOL_GD, POOL_GD)), _fixed((1, POOL_C))],
        out_specs=_rows(t, POOL_C), scratch_shapes=[pltpu.VMEM((t + CONV_HALO, POOL_C), F32)],
        compiler_params=_params("parallel"))(z, z, pool_w, pool_scale.reshape(1, -1))


def _pool_bwd(dm, z, pool_w, pool_scale, dz, name):
    s = z.shape[0]
    t = _pool_tile(s)
    nt = s // t

    def body(up_ref, u_ref, dm_ref, dmn_ref, w_ref, sc_ref, _, du_ref, dw_ref, dsc_ref, ubuf, ebuf):
        i = pl.program_id(0)
        ubuf[pl.ds(0, CONV_HALO), :] = jnp.where(i == 0, 0.0, up_ref[...].astype(F32))
        ubuf[pl.ds(CONV_HALO, t), :] = u_ref[...].astype(F32)

        @pl.when(i == 0)
        def _():
            dw_ref[...] = jnp.zeros_like(dw_ref)
            dsc_ref[...] = jnp.zeros_like(dsc_ref)

        dm_next = jnp.where(i == nt - 1, 0.0, dmn_ref[...])
        for gi, window in enumerate(POOL_WINDOWS):
            lanes = pl.ds(gi * POOL_GD, POOL_GD)
            wb = w_ref[gi].astype(BF16)
            scale = sc_ref[:, lanes]
            d = _pool_diff(ubuf, gi, window, i * t, t).astype(BF16)
            mm = jnp.dot(d, wb, preferred_element_type=F32)
            dmv = dm_ref[:, lanes]
            dsc_ref[:, lanes] += jnp.sum(dmv * mm, axis=0, keepdims=True)
            dmm = (dmv * scale).astype(BF16)
            dw_ref[gi] += lax.dot_general(d, dmm, (((0,), (0,)), ((), ())), preferred_element_type=F32)
            dd = lax.dot_general(dmm, wb, (((1,), (1,)), ((), ())), preferred_element_type=F32)
            dd_next = lax.dot_general((dm_next[:, gi * POOL_GD:(gi + 1) * POOL_GD] * scale).astype(BF16), wb,
                                      (((1,), (1,)), ((), ())), preferred_element_type=F32)
            ebuf[pl.ds(0, t), lanes] = dd / _pool_counts(i * t, t, window)
            ebuf[pl.ds(t, CONV_HALO), lanes] = dd_next / _pool_counts((i + 1) * t, CONV_HALO, window)
            du = -dd
            for ahead in range(window):
                du = du + ebuf[pl.ds(ahead, t), lanes]
            du_ref[:, lanes] = du.astype(BF16)

    into = _into(dz, 6, 0)
    du, dw, dsc = pl.pallas_call(
        body, name=name,
        out_shape=(into["out_shape"], jax.ShapeDtypeStruct((POOL_G, POOL_GD, POOL_GD), F32),
                   jax.ShapeDtypeStruct((1, POOL_C), F32)), grid=(nt,),
        in_specs=[_halo_before(t, *ZC_POOL), _rows(t, *ZC_POOL), _rows(t, POOL_C), _halo_after(t, POOL_C, 0, nt),
                  _fixed((POOL_G, POOL_GD, POOL_GD)), _fixed((1, POOL_C))] + into["in_specs"],
        out_specs=(_rows(t, *ZC_POOL), _fixed((POOL_G, POOL_GD, POOL_GD)), _fixed((1, POOL_C))),
        input_output_aliases=into["input_output_aliases"],
        scratch_shapes=[pltpu.VMEM((t + CONV_HALO, POOL_C), F32), pltpu.VMEM((t + CONV_HALO, POOL_C), F32)],
        compiler_params=_params("arbitrary"))(z, z, dm, dm, pool_w, pool_scale.reshape(1, -1), dz)
    return du, dw, dsc.reshape(-1)


def _gate_specs(ts):
    width, first = ZC_GATE
    return [_rows(ts, width, first + b) for b in range(3)]


def _branches_merge_fwd(z, acts, ws, name):
    s = z.shape[0]
    ts = min(s, 512)

    def body(g0, g1, g2, a0, a1, a2, w0, w1, w2, y0, y1, y2, m_ref):
        merged = jnp.zeros((ts, D_MODEL), F32)
        for g_ref, a_ref, w_ref, y_ref in ((g0, a0, w0, y0), (g1, a1, w1, y1), (g2, a2, w2, y2)):
            wv = jnp.concatenate([w_ref[c] for c in range(N_DEV)], axis=1)
            yb = jnp.dot(a_ref[...], wv, preferred_element_type=F32).astype(BF16)
            y_ref[...] = yb
            merged = merged + _sigmoid(g_ref[...].astype(F32)) * yb.astype(F32)
        m_ref[...] = merged.astype(BF16)

    out = jax.ShapeDtypeStruct((s, D_MODEL), BF16)
    res = pl.pallas_call(
        body, name=name, out_shape=(out,) * 4, grid=(s // ts,),
        in_specs=_gate_specs(ts) + [_rows(ts, a.shape[1]) for a in acts] + [_fixed(w.shape) for w in ws],
        out_specs=(_rows(ts, D_MODEL),) * 4, compiler_params=_params("parallel"))(z, z, z, *acts, *ws)
    return tuple(res[:3]), res[3]


def _merge_bwd(z, ys, dmerged, name):
    s = z.shape[0]
    ts = min(s, 256)

    def body(g0, g1, g2, y0, y1, y2, dm_ref, dy0, dy1, dy2, dz_ref):
        dmv = dm_ref[...]
        for b, (g_ref, y_ref, dy_ref) in enumerate(((g0, y0, dy0), (g1, y1, dy1), (g2, y2, dy2))):
            sg = _sigmoid(g_ref[...].astype(F32))
            dy_ref[...] = (dmv * sg).astype(BF16)
            dz_ref[:, pl.ds(b * D_MODEL, D_MODEL)] = (dmv * y_ref[...].astype(F32) * sg * (1.0 - sg)).astype(BF16)

    out = jax.ShapeDtypeStruct((s, D_MODEL), BF16)
    return pl.pallas_call(
        body, name=name, out_shape=(out,) * 3 + (jax.ShapeDtypeStruct((s, Z_W), BF16),), grid=(s // ts,),
        in_specs=_gate_specs(ts) + [_rows(ts, D_MODEL)] * 4,
        out_specs=(_rows(ts, D_MODEL),) * 3 + (_rows(ts, *ZC_GATES),),
        compiler_params=_params("parallel"))(z, z, z, *ys, dmerged)


def _ffn_up_fwd(h, w_gate, w_up, name):
    s, d = h.shape
    nb = w_gate.shape[2]
    f = N_DEV * nb
    tm, n_blk = min(s, 1024), 2
    tn = n_blk * nb
    blk = pl.BlockSpec((tm, tn), lambda i, j: (i, j))
    wspec = pl.BlockSpec((n_blk, d, nb), lambda i, j: (j, 0, 0))

    def body(h_ref, wg_ref, wu_ref, hg_ref, hu_ref, act_ref):
        hv = h_ref[...]
        g = jnp.dot(hv, jnp.concatenate([wg_ref[c] for c in range(n_blk)], axis=1), preferred_element_type=F32)
        u = jnp.dot(hv, jnp.concatenate([wu_ref[c] for c in range(n_blk)], axis=1), preferred_element_type=F32)
        hg_ref[...] = g.astype(hg_ref.dtype)
        hu_ref[...] = u.astype(hu_ref.dtype)
        act_ref[...] = (g * _sigmoid(g) * u).astype(BF16)

    return pl.pallas_call(
        body, name=name,
        out_shape=(jax.ShapeDtypeStruct((s, f), BF16),) * 3,
        grid=(s // tm, f // tn), in_specs=[pl.BlockSpec((tm, d), lambda i, j: (i, 0)), wspec, wspec],
        out_specs=(blk, blk, blk), compiler_params=_params("parallel", "parallel"))(h, w_gate, w_up)


def _ffn_down_bwd(dfo, w_down, hg, hu, name):
    s, d = dfo.shape
    f = w_down.shape[0]
    tm, tn = min(s, 1024), _tile(f, 1024)
    blk = pl.BlockSpec((tm, tn), lambda i, j: (i, j))

    def body(d_ref, w_ref, g_ref, u_ref, dg_ref, du_ref):
        dact = lax.dot_general(d_ref[...], w_ref[...], (((1,), (1,)), ((), ())), preferred_element_type=F32)
        g = g_ref[...].astype(F32)
        sg = _sigmoid(g)
        dg_ref[...] = (dact * u_ref[...].astype(F32) * (sg * (1.0 + g * (1.0 - sg)))).astype(BF16)
        du_ref[...] = (dact * g * sg).astype(BF16)

    out = jax.ShapeDtypeStruct((s, f), BF16)
    return pl.pallas_call(
        body, name=name, out_shape=(out, out), grid=(s // tm, f // tn),
        in_specs=[pl.BlockSpec((tm, d), lambda i, j: (i, 0)), pl.BlockSpec((tn, d), lambda i, j: (j, 0)), blk, blk],
        out_specs=(blk, blk), compiler_params=_params("parallel", "parallel"))(dfo, w_down, hg, hu)


def _ffn_up_dx(dhg, dhu, w_gate, w_up, name):
    s, f = dhg.shape
    d = w_gate.shape[1]
    tm, tn = min(s, 1024), min(d, 256)
    dims = (((1,), (1,)), ((), ()))

    def body(g_ref, u_ref, wg_ref, wu_ref, o_ref):
        wg = jnp.concatenate([wg_ref[c] for c in range(N_DEV)], axis=1)
        wu = jnp.concatenate([wu_ref[c] for c in range(N_DEV)], axis=1)
        o_ref[...] = (lax.dot_general(g_ref[...], wg, dims, preferred_element_type=F32)
                      + lax.dot_general(u_ref[...], wu, dims, preferred_element_type=F32))

    a_spec = pl.BlockSpec((tm, f), lambda i, j: (i, 0))
    w_spec = pl.BlockSpec((N_DEV, tn, w_gate.shape[2]), lambda i, j: (0, j, 0))
    return pl.pallas_call(
        body, name=name, out_shape=jax.ShapeDtypeStruct((s, d), F32), grid=(s // tm, d // tn),
        in_specs=[a_spec, a_spec, w_spec, w_spec], out_specs=pl.BlockSpec((tm, tn), lambda i, j: (i, j)),
        compiler_params=_params("parallel", "parallel"))(dhg, dhu, w_gate, w_up)


def _loss_grad(y, target, name):
    s, d = y.shape
    ts = min(s, 512)

    def body(y_ref, t_ref, dy_ref, sq_ref):
        e = y_ref[...] - t_ref[...]
        dy_ref[...] = e / d

        @pl.when(pl.program_id(0) == 0)
        def _():
            sq_ref[...] = jnp.zeros_like(sq_ref)

        sq_ref[...] += jnp.sum(e * e, axis=0, keepdims=True)

    return pl.pallas_call(
        body, name=name, out_shape=(jax.ShapeDtypeStruct((s, d), F32), jax.ShapeDtypeStruct((1, d), F32)),
        grid=(s // ts,), in_specs=[_rows(ts, d), _rows(ts, d)], out_specs=(_rows(ts, d), _fixed((1, d))),
        compiler_params=_params("arbitrary"))(y, target)


def _adamw(w, g, m, v, name):
    shape = w.shape
    cols = shape[-1]
    keep3 = w.ndim == 3 and shape[1] < SUBLANES
    view = shape if keep3 else (math.prod(shape[:-1]), cols)
    rows = view[0]
    if keep3:
        cap = max(1, (2 << 20) // (SUBLANES * cols * 4))
        tr = max(t for t in range(1, cap + 1) if rows % t == 0)
    else:
        tr = _row_tile(rows, cols * 4, budget=2 << 20)

    def body(w_ref, g_ref, m_ref, v_ref, d_ref, mo_ref, vo_ref):
        gv = g_ref[...]
        mn = B1 * m_ref[...] + (1.0 - B1) * gv
        vn = B2 * v_ref[...] + (1.0 - B2) * (gv * gv)
        m_hat = mn / (1.0 - B1 ** STEP)
        v_hat = vn / (1.0 - B2 ** STEP)
        d_ref[...] = -LR * (m_hat / (jnp.sqrt(v_hat) + ADAM_EPS) + WD * w_ref[...])
        mo_ref[...] = mn
        vo_ref[...] = vn

    spec = pl.BlockSpec((tr,) + view[1:], lambda i: (i,) + (0,) * (len(view) - 1))
    out = jax.ShapeDtypeStruct(view, F32)
    res = pl.pallas_call(
        body, name=name, out_shape=(out,) * 3, grid=(rows // tr,), in_specs=[spec] * 4, out_specs=(spec,) * 3,
        compiler_params=_params("parallel"))(*[t.reshape(view) for t in (w, g, m, v)])
    return tuple(r.reshape(shape) for r in res)


LANE_MAJOR = ("w_uq", "w_uk", "w_uv", "w_gate", "w_up")


def _lane_major(name, a):
    if name == "w_in":
        return a.transpose(2, 0, 1)
    if name in LANE_MAJOR:
        return a.transpose(0, 2, 1)
    return a


def _from_lane_major(name, a):
    if name == "w_in":
        return a.transpose(1, 2, 0)
    return _lane_major(name, a)


ANY = pl.BlockSpec(memory_space=pl.ANY)


class _GatherRide:
    def __init__(self, arrays):
        n = len(arrays)
        self.arrays = list(arrays)
        self.out_shape = [jax.ShapeDtypeStruct((N_DEV,) + a.shape, a.dtype) for a in arrays]
        self.scratch = [pltpu.SemaphoreType.DMA((n, 7)), pltpu.SemaphoreType.DMA((n, 7)), pltpu.SemaphoreType.DMA((n,))]

    def _copies(self, ins, outs, sems):
        send_sems, recv_sems, local_sems = sems
        n = len(self.arrays)
        x, y, c = lax.axis_index("x"), lax.axis_index("y"), lax.axis_index("c")
        me, sibling = (x, y, c), (x, y, 1 - c)
        chips = [(1 - x, y), (x, 1 - y), (1 - x, 1 - y)]

        def slot(a, px, py, pc):
            return outs[a].at[4 * px + 2 * py + pc]

        def copy(a, k, block, to, src=None):
            return pltpu.make_async_remote_copy(
                src_ref=slot(a, *block) if src is None else src, dst_ref=slot(a, *block), send_sem=send_sems.at[a, k],
                recv_sem=recv_sems.at[a, k], device_id=to, device_id_type=MESH)

        mine = [pltpu.make_async_copy(ins[a], slot(a, *me), local_sems.at[a]) for a in range(n)]
        first = []
        for a in range(n):
            first.append(copy(a, 0, me, sibling, src=ins[a]))
            first += [copy(a, 1 + j, me, (*chip, c), src=ins[a]) for j, chip in enumerate(chips)]
        return n, me, sibling, chips, c, copy, mine, first

    def start(self, ins, outs, sems):
        _, _, _, _, _, _, mine, first = self._copies(ins, outs, sems)
        for cp in mine + first:
            cp.start()

    def finish(self, ins, outs, sems):
        n, me, sibling, chips, c, copy, mine, first = self._copies(ins, outs, sems)
        passed = []
        for j, chip in enumerate(chips):
            for a in range(n):
                copy(a, 1 + j, (*chip, c), me).wait_recv()
                passed.append(copy(a, 4 + j, (*chip, c), sibling))
                passed[-1].start()
        for a in range(n):
            copy(a, 0, sibling, me).wait_recv()
            for j, chip in enumerate(chips):
                copy(a, 4 + j, (*chip, 1 - c), me).wait_recv()
        for cp in first + passed:
            cp.wait_send()
        for cp in mine:
            cp.wait()


class _ReduceRide:
    def __init__(self, arrays):
        n = len(arrays)
        self.arrays = list(arrays)
        self.out_shape = [jax.ShapeDtypeStruct(a.shape, a.dtype) for a in arrays]
        self.scratch = [pltpu.SemaphoreType.DMA((n, 7)), pltpu.SemaphoreType.DMA((n, 7)), pltpu.SemaphoreType.DMA((n,))]

    def _copies(self, ins, outs, sems):
        send_sems, recv_sems, local_sems = sems
        n = len(self.arrays)
        x, y, c = lax.axis_index("x"), lax.axis_index("y"), lax.axis_index("c")
        mine = [pltpu.make_async_copy(ins[a].at[4 * x + 2 * y + c], outs[a].at[0], local_sems.at[a]) for a in range(n)]
        copies = []
        for a in range(n):
            for k in range(1, N_DEV):
                px = 1 - x if k & 4 else x
                py = 1 - y if k & 2 else y
                pc = 1 - c if k & 1 else c
                copies.append(pltpu.make_async_remote_copy(
                    src_ref=ins[a].at[4 * px + 2 * py + pc], dst_ref=outs[a].at[k], send_sem=send_sems.at[a, k - 1],
                    recv_sem=recv_sems.at[a, k - 1], device_id=(px, py, pc), device_id_type=MESH))
        return mine, copies

    def start(self, ins, outs, sems):
        mine, copies = self._copies(ins, outs, sems)
        for cp in mine + copies:
            cp.start()

    def finish(self, ins, outs, sems):
        mine, copies = self._copies(ins, outs, sems)
        for cp in copies + mine:
            cp.wait()


def _run_ride(ride, name):
    n = len(ride.arrays)

    def body(*refs):
        ins, outs, sems = refs[:n], refs[n:2 * n], refs[2 * n:]
        ride.start(ins, outs, sems)
        ride.finish(ins, outs, sems)

    return pl.pallas_call(body, name=name, out_shape=ride.out_shape, in_specs=[ANY] * n, out_specs=[ANY] * n,
                          scratch_shapes=ride.scratch)(*ride.arrays)


def _all_gather(arrays, name):
    return _run_ride(_GatherRide(arrays), name)


def _swap_with_sibling(arrays, name):
    n = len(arrays)

    def body(*refs):
        ins, outs = refs[:n], refs[n:2 * n]
        send_sems, recv_sems = refs[2 * n:]
        x, y, c = lax.axis_index("x"), lax.axis_index("y"), lax.axis_index("c")
        copies = [pltpu.make_async_remote_copy(
            src_ref=ins[a].at[1 - c], dst_ref=outs[a], send_sem=send_sems.at[a], recv_sem=recv_sems.at[a],
            device_id=(x, y, 1 - c), device_id_type=MESH) for a in range(n)]
        for cp in copies:
            cp.start()
        for cp in copies:
            cp.wait()

    return pl.pallas_call(
        body, name=name, out_shape=[jax.ShapeDtypeStruct(a.shape[1:], a.dtype) for a in arrays],
        in_specs=[ANY] * n, out_specs=[ANY] * n,
        scratch_shapes=[pltpu.SemaphoreType.DMA((n,)), pltpu.SemaphoreType.DMA((n,))])(*arrays)


class _ChipExchangeRide:
    def __init__(self, arrays):
        n = len(arrays)
        self.arrays = list(arrays)
        self.out_shape = [jax.ShapeDtypeStruct(a.shape, a.dtype) for a in arrays]
        self.scratch = [pltpu.SemaphoreType.DMA((n, 3)), pltpu.SemaphoreType.DMA((n, 3)), pltpu.SemaphoreType.DMA((n,))]

    def _copies(self, ins, outs, sems):
        send_sems, recv_sems, local_sems = sems
        n = len(self.arrays)
        x, y, c = lax.axis_index("x"), lax.axis_index("y"), lax.axis_index("c")
        partners = [(x, 1 - y), (1 - x, y), (1 - x, 1 - y)]
        mine = [pltpu.make_async_copy(ins[a].at[2 * x + y], outs[a].at[0], local_sems.at[a]) for a in range(n)]
        copies = [pltpu.make_async_remote_copy(
            src_ref=ins[a].at[2 * px + py], dst_ref=outs[a].at[1 + k], send_sem=send_sems.at[a, k],
            recv_sem=recv_sems.at[a, k], device_id=(px, py, c), device_id_type=MESH)
            for a in range(n) for k, (px, py) in enumerate(partners)]
        return mine, copies

    def start(self, ins, outs, sems):
        mine, copies = self._copies(ins, outs, sems)
        for cp in mine + copies:
            cp.start()

    def finish(self, ins, outs, sems):
        mine, copies = self._copies(ins, outs, sems)
        for cp in copies + mine:
            cp.wait()


class _Combo:
    def __init__(self, rides):
        self.rides = rides
        self.arrays = [a for r in rides for a in r.arrays]
        self.out_shape = [o for r in rides for o in r.out_shape]
        self.scratch = [sc for r in rides for sc in r.scratch]

    def _parts(self, ins, outs, sems):
        at_a = at_s = 0
        for r in self.rides:
            na, ns = len(r.arrays), len(r.scratch)
            yield r, ins[at_a:at_a + na], outs[at_a:at_a + na], sems[at_s:at_s + ns]
            at_a, at_s = at_a + na, at_s + ns

    def start(self, ins, outs, sems):
        for r, i, o, sm in self._parts(ins, outs, sems):
            r.start(i, o, sm)

    def finish(self, ins, outs, sems):
        for r, i, o, sm in reversed(list(self._parts(ins, outs, sems))):
            r.finish(i, o, sm)


def _as_rows(a, lead):
    return a.reshape(a.shape[:lead] + (math.prod(a.shape[lead:-1]), a.shape[-1]))


def _add_pairs(a, b, name):
    a2, b2 = _as_rows(a, 0), _as_rows(b, 0)
    rows, cols = a2.shape
    tr = _row_tile(rows, cols * 4)

    def body(a_ref, b_ref, o_ref):
        o_ref[...] = (a_ref[...].astype(F32) + b_ref[...].astype(F32)).astype(o_ref.dtype)

    spec = _rows(tr, cols)
    out = pl.pallas_call(body, name=name, out_shape=jax.ShapeDtypeStruct(a2.shape, a.dtype), grid=(rows // tr,),
                         in_specs=[spec, spec], out_specs=spec, compiler_params=_params("parallel"))(a2, b2)
    return out.reshape(a.shape)


def _sum_blocks(a, name):
    a3 = _as_rows(a, 1)
    n, rows, cols = a3.shape
    tr = _row_tile(rows, n * cols * 4)

    def body(a_ref, o_ref):
        tot = a_ref[0].astype(F32)
        for k in range(1, n):
            tot = tot + a_ref[k].astype(F32)
        o_ref[...] = tot

    out = pl.pallas_call(body, name=name, out_shape=jax.ShapeDtypeStruct((rows, cols), F32), grid=(rows // tr,),
                         in_specs=[pl.BlockSpec((n, tr, cols), lambda j: (0, j, 0))], out_specs=_rows(tr, cols),
                         compiler_params=_params("parallel"))(a3)
    return out.reshape(a.shape[1:])


def _sum_layers(blocks, name):
    arrs = [_as_rows(a, 1) for a in blocks]
    rows, cols = arrs[0].shape[1:]
    tr = _row_tile(rows, max(a.shape[0] for a in arrs) * cols * 4, budget=8 << 20)
    nj = rows // tr

    def body(*refs):
        o_ref = refs[-1]
        for k, a_ref in enumerate(refs[:-1]):
            @pl.when(pl.program_id(0) == k)
            def _(a_ref=a_ref, n=arrs[k].shape[0]):
                tot = a_ref[0].astype(F32)
                for b in range(1, n):
                    tot = tot + a_ref[b].astype(F32)
                o_ref[0] = tot

    in_specs = [pl.BlockSpec((a.shape[0], tr, cols),
                             lambda l, j, k=k: (0, jnp.where(l == k, j, jnp.where(l < k, 0, nj - 1)), 0))
                for k, a in enumerate(arrs)]
    out = pl.pallas_call(body, name=name, out_shape=jax.ShapeDtypeStruct((len(arrs), rows, cols), F32),
                         grid=(len(arrs), nj), in_specs=in_specs,
                         out_specs=pl.BlockSpec((1, tr, cols), lambda l, j: (l, j, 0)),
                         compiler_params=_params("arbitrary", "arbitrary"))(*arrs)
    return out.reshape((len(arrs),) + blocks[0].shape[1:])


MIX_GROUPS = ("w_in", "w_uq", "w_uk", "w_uv", "w_attn_o", "w_conv_o", "w_pool_o", "w_mix_o")
FFN_GROUPS = ("w_gate", "w_up", "w_down")
MIX_EARLY = ("w_attn_o", "w_conv_o", "w_pool_o", "w_mix_o")
MIX_LATE = ("w_in", "w_uq", "w_uk", "w_uv")


def _pad_axis(a, axis, size):
    pad = [(0, 0)] * a.ndim
    pad[axis] = (0, size - a.shape[axis])
    return jnp.pad(a, pad)


def _local_groups(sh, l):
    out = {n: sh[n][l] for n in BIG}
    for n in ("w_uq", "w_uk", "w_uv"):
        out[n] = _pad_axis(out[n], -1, HEAD_PAD)
    for n in ("w_gate", "w_up"):
        out[n] = _pad_axis(out[n], -1, FF_SHARD_PAD)
    out["w_down"] = _pad_axis(out["w_down"], 0, FF_SHARD_PAD)
    return {n: v.astype(BF16) for n, v in out.items()}


def _arrange_w_in(blocks):
    parts, pos = [], 0
    for ref_lo, ref_hi, at in sorted(W_IN_PIECES, key=lambda p: p[2]):
        if at > pos:
            parts.append(jnp.zeros((blocks.shape[1], at - pos), blocks.dtype))
        for d in range(N_DEV):
            lo, hi = max(ref_lo, d * W_IN_SHARD), min(ref_hi, (d + 1) * W_IN_SHARD)
            if lo < hi:
                parts.append(blocks[d][:, lo - d * W_IN_SHARD:hi - d * W_IN_SHARD])
        pos = at + ref_hi - ref_lo
    if pos < Z_W:
        parts.append(jnp.zeros((blocks.shape[1], Z_W - pos), blocks.dtype))
    return jnp.concatenate(parts, axis=1)


def _w_in_shard(g, d):
    parts = []
    for ref_lo, ref_hi, at in W_IN_PIECES:
        lo, hi = max(ref_lo, d * W_IN_SHARD), min(ref_hi, (d + 1) * W_IN_SHARD)
        if lo < hi:
            parts.append(g[:, at + lo - ref_lo:at + hi - ref_lo])
    return jnp.concatenate(parts, axis=1)


def _mixer_weights(gat):
    w = {n: v for n, v in gat.items() if n != "w_in"}
    attn_o = gat["w_attn_o"].reshape(N_DEV, N_HEADS, V_HEAD, LANES)
    w["w_attn_o"] = _pad_axis(attn_o, 2, HEAD_PAD).reshape(N_DEV, N_HEADS * HEAD_PAD, LANES)
    w["w_mix_o"] = gat["w_mix_o"].reshape(D_MODEL, D_MODEL)
    return w


def _ffn_weights(gat):
    return {"w_gate": gat["w_gate"], "w_up": gat["w_up"], "w_down": gat["w_down"].reshape(D_FF_PAD, D_MODEL)}


def _mixer_grad_groups(gb):
    g = dict(gb)
    if "w_in" in gb:
        g["w_in"] = jnp.stack([_w_in_shard(gb["w_in"], d) for d in range(N_DEV)])
    if "w_attn_o" in gb:
        attn_o = gb["w_attn_o"].reshape(N_DEV, N_HEADS, HEAD_PAD, LANES)[:, :, :V_HEAD]
        g["w_attn_o"] = attn_o.reshape(N_DEV, N_HEADS * V_HEAD, LANES)
    if "w_mix_o" in gb:
        g["w_mix_o"] = gb["w_mix_o"].reshape(N_DEV, D_MODEL // N_DEV, D_MODEL)
    return g


def _ffn_grad_groups(gb):
    return {"w_gate": gb["w_gate"], "w_up": gb["w_up"], "w_down": gb["w_down"].reshape(N_DEV, FF_SHARD_PAD, D_MODEL)}


def _grads_from_groups(tot):
    g = dict(tot)
    g["w_uq"] = tot["w_uq"][..., :QK_NOPE + QK_ROPE]
    g["w_uk"], g["w_uv"] = tot["w_uk"][..., :QK_NOPE], tot["w_uv"][..., :V_HEAD]
    g["w_gate"], g["w_up"] = tot["w_gate"][..., :FF_SHARD], tot["w_up"][..., :FF_SHARD]
    g["w_down"] = tot["w_down"][..., :FF_SHARD, :]
    return g


SMALL_GROUPS = (
    (D_MODEL, ("mix_norm_pre", "mix_norm_post", "ffn_norm_pre", "ffn_norm_post")),
    (CONV_C, ("conv_w", "conv_b", "conv_ln_g", "conv_ln_b", "pool_scale")),
    (Q_RANK, ("q_norm",)), (KV_RANK, ("kv_norm",)), (POOL_GD, ("pool_w",)),
)


def _small_rows(name):
    return {"conv_w": CONV_HALO, "pool_w": POOL_G * POOL_GD}.get(name, SUBLANES)


def _small_groups(small):
    out = []
    for width, names in SMALL_GROUPS:
        parts = []
        for l in range(DEPTH):
            for n in names:
                part = small[l][n].reshape(-1, width)
                parts.append(_pad_axis(part, 0, _small_rows(n)))
        out.append(jnp.concatenate(parts, axis=0))
    return out


def _small_from_groups(groups):
    shapes = {"conv_w": (CONV_W, CONV_C), "pool_w": (POOL_G, POOL_GD, POOL_GD)}
    out = {}
    for (width, names), g in zip(SMALL_GROUPS, groups):
        row = 0
        for l in range(DEPTH):
            for n in names:
                rows = _small_rows(n)
                real = {"conv_w": CONV_W, "pool_w": POOL_G * POOL_GD}.get(n, 1)
                out.setdefault(n, []).append(g[row:row + real].reshape(shapes.get(n, (width,))))
                row += rows
    return {n: jnp.stack(v) for n, v in out.items()}


def _mixer_fwd(x, h, tables, sm, plan, l):
    nm = lambda n: f"{n}_l{l}"
    if h is None:
        h = _rms_fwd(x, (D_MODEL, 0), sm["mix_norm_pre"], BF16, nm("mix_pre_norm"))
    w_in, ride = plan.w_in(l), plan.in_proj_ride(l)
    if ride is None:
        z = _matmul(h, w_in, "nn", BF16, nm("in_proj"))
    else:
        z, rode = _matmul(h, w_in, "nn", BF16, nm("in_proj"), ride=ride)
        plan.in_proj_done(l, rode)
    w = dict(plan.mixer_weights(l), w_in=w_in)
    cq, ckv, q, k, v = _qkv_up_fwd(z, sm["q_norm"], sm["kv_norm"], w["w_uq"], w["w_uk"], w["w_uv"], tables, nm("qkv_up"))
    (o, lse), rode = _flash_fwd(q, k, v, nm("flash_fwd"), plan.fwd_ride(l))
    plan.fwd_done(l, rode)
    hc, co = _conv_fwd(z, sm["conv_w"], sm["conv_b"], sm["conv_ln_g"], sm["conv_ln_b"], nm("conv_fwd"))
    pm = _pool_fwd(z, sm["pool_w"], sm["pool_scale"], nm("pool_fwd"))
    ys, merged = _branches_merge_fwd(z, (o, hc, pm), (w["w_attn_o"], w["w_conv_o"], w["w_pool_o"]), nm("branches_merge"))
    mo = _matmul(merged, w["w_mix_o"], "nn", F32, nm("mix_out"))
    x_mid, h2 = _rms_fwd(mo, (D_MODEL, 0), sm["mix_norm_post"], F32, nm("mix_post_norm"), res=x, then=sm["ffn_norm_pre"])
    saved = dict(x=x, h=h, z=z, cq=cq, ckv=ckv, q=q, k=k, v=v, o=o, lse=lse, hc=hc, co=co, pm=pm, ys=ys, merged=merged,
                 mo=mo)
    return x_mid, h2, saved, w


def _ffn_fwd(x_mid, h2, w, sm, tag, next_gain):
    nm = lambda n: f"{n}_{tag}"
    hg, hu, act = _ffn_up_fwd(h2, w["w_gate"], w["w_up"], nm("ffn_up_fwd"))
    fo = _matmul(act, w["w_down"], "nn", F32, nm("ffn_down"))
    out = _rms_fwd(fo, (D_MODEL, 0), sm["ffn_norm_post"], F32, nm("ffn_post_norm"), res=x_mid, then=next_gain)
    out, h_next = out if next_gain is not None else (out, None)
    saved = dict(x_mid=x_mid, h2=h2, hg=hg, hu=hu, act=act, fo=fo)
    return out, h_next, saved


def _ffn_bwd(dout, sv, w, sm, tag):
    nm = lambda n: f"{n}_{tag}"
    gb, gs = {}, {}
    dfo, gs["ffn_norm_post"] = _rms_bwd(sv["fo"], (D_MODEL, 0), sm["ffn_norm_post"], dout, BF16, nm("ffn_post_norm_bwd"))
    gb["w_down"] = _matmul(sv["act"], dfo, "tn", BF16, nm("ffn_down_dw"))
    dhg, dhu = _ffn_down_bwd(dfo, w["w_down"], sv["hg"], sv["hu"], nm("ffn_down_bwd"))
    dh2 = _ffn_up_dx(dhg, dhu, w["w_gate"], w["w_up"], nm("ffn_up_dx"))
    gb["w_gate"] = _matmul(sv["h2"], dhg, "tn", BF16, nm("ffn_gate_dw"), blocked=True)
    gb["w_up"] = _matmul(sv["h2"], dhu, "tn", BF16, nm("ffn_up_dw"), blocked=True)
    dmid, gs["ffn_norm_pre"] = _rms_bwd(sv["x_mid"], (D_MODEL, 0), sm["ffn_norm_pre"], dh2, F32, nm("ffn_pre_norm_bwd"),
                                        add=dout)
    return dmid, gb, gs


def _mixer_bwd(dmid, sv, tables, w, sm, plan, l, pack_small):
    nm = lambda n: f"{n}_l{l}"
    gb, gs = {}, {}
    dmo, gs["mix_norm_post"] = _rms_bwd(sv["mo"], (D_MODEL, 0), sm["mix_norm_post"], dmid, BF16, nm("mix_post_norm_bwd"))
    dmerged = _matmul(dmo, w["w_mix_o"], "nt", F32, nm("mix_out_dx"))
    gb["w_mix_o"] = _matmul(sv["merged"], dmo, "tn", BF16, nm("mix_out_dw"))
    dya, dyc, dyp, dz = _merge_bwd(sv["z"], sv["ys"], dmerged, nm("merge_bwd"))
    dpm = _matmul(dyp, w["w_pool_o"], "nt", F32, nm("pool_out_dx"))
    gb["w_pool_o"] = _matmul(sv["pm"], dyp, "tn", BF16, nm("pool_out_dw"), blocked=True)
    dz, gs["pool_w"], gs["pool_scale"] = _pool_bwd(dpm, sv["z"], sm["pool_w"], sm["pool_scale"], dz, nm("pool_bwd"))
    dhc = _matmul(dyc, w["w_conv_o"], "nt", F32, nm("conv_out_dx"))
    gb["w_conv_o"] = _matmul(sv["hc"], dyc, "tn", BF16, nm("conv_out_dw"), blocked=True)
    dco, gs["conv_ln_g"], gs["conv_ln_b"], gs["conv_b"] = _conv_bwd_norm(dhc, sv["co"], sm["conv_ln_g"], sm["conv_ln_b"],
                                                                        nm("conv_bwd_norm"))
    dz, gs["conv_w"] = _conv_bwd_taps(dco, sv["z"], sm["conv_w"], dz, nm("conv_bwd_taps"))
    gb["w_attn_o"] = _matmul(sv["o"], dya, "tn", BF16, nm("attn_out_dw"), blocked=True)
    delta, dob = _attn_out_bwd(dya, w["w_attn_o"], sv["o"], nm("attn_out_bwd"))
    (dq, dk, dv), rode = _flash_bwd(sv["q"], sv["k"], sv["v"], dob, sv["lse"], delta, nm("flash_bwd"),
                                  plan.bwd_ride(l, gb))
    plan.bwd_done(l, rode)
    dqf, dkf, dz, gs["q_norm"], gs["kv_norm"] = _qkv_up_bwd(
        dq, dk, dv, sv["z"], w["w_uq"], w["w_uk"], w["w_uv"], tables, sm["q_norm"], sm["kv_norm"], dz, nm("qkv_up_bwd"))
    gb["w_uq"] = _matmul(sv["cq"], dqf, "tn", BF16, nm("q_up_dw"), blocked=True)
    gb["w_uk"] = _matmul(sv["ckv"], dkf, "tn", BF16, nm("k_up_dw"), blocked=True)
    gb["w_uv"] = _matmul(sv["ckv"], dv, "tn", BF16, nm("v_up_dw"), blocked=True)
    gb["w_in"] = _matmul(sv["h"], dz, "tn", BF16, nm("in_proj_dw"))
    plan.add_grads(l, "mix", gb)
    ride, small_gathered = plan.tail_ride(l, pack_small(gs)), []
    if ride is None:
        dh = _matmul(dz, w["w_in"], "nt", F32, nm("in_proj_dx"))
    else:
        dh, rode = _matmul(dz, w["w_in"], "nt", F32, nm("in_proj_dx"), ride=ride)
        small_gathered = plan.tail_done(l, rode)
    dx, gs["mix_norm_pre"] = _rms_bwd(sv["x"], (D_MODEL, 0), sm["mix_norm_pre"], dh, F32, nm("mix_pre_norm_bwd"), add=dmid)
    return dx, gs, small_gathered


def _part_groups(part):
    return {"mix": MIX_GROUPS, "ffn": FFN_GROUPS, "early": MIX_EARLY, "late": MIX_LATE}[part]


class _Plan:
    def __init__(self, shards, conv_w):
        self.local = [_local_groups(shards, l) for l in range(DEPTH)]
        self.conv_w = conv_w
        self.gat, self.send, self.recv = {}, {}, {}

    @staticmethod
    def _riders(l):
        return [(l, "ffn")] + ([(l + 1, "mix")] if l + 1 < DEPTH else [])

    @staticmethod
    def _grad_riders(l):
        return [(l, "ffn"), (l, "early")] + ([(l + 1, "late")] if l + 1 < DEPTH else [])

    def gather_first(self):
        w_in, conv_w = _all_gather([self.local[0]["w_in"], self.conv_w], "gather_w_in_l0")
        self.gat[(0, "mix")] = {"w_in": w_in}
        return conv_w

    def w_in(self, l):
        return _arrange_w_in(self.gat[(l, "mix")]["w_in"])

    def in_proj_ride(self, l):
        return _GatherRide([self.local[0][g] for g in MIX_GROUPS[1:]]) if l == 0 else None

    def in_proj_done(self, l, outs):
        self.gat[(l, "mix")].update(zip(MIX_GROUPS[1:], outs))

    def fwd_ride(self, l):
        return _GatherRide([self.local[ll][g] for ll, part in self._riders(l) for g in _part_groups(part)])

    def fwd_done(self, l, outs):
        outs = list(outs)
        for ll, part in self._riders(l):
            self.gat[(ll, part)] = {g: outs.pop(0) for g in _part_groups(part)}

    def mixer_weights(self, l):
        return _mixer_weights(self.gat[(l, "mix")])

    def ffn_weights(self, l):
        return _ffn_weights(self.gat[(l, "ffn")])

    def add_grads(self, l, part, gb):
        if part == "ffn":
            self.send[(l, "ffn")] = _ffn_grad_groups(gb)
        else:
            self.send.setdefault((l, "late"), {}).update(_mixer_grad_groups({g: gb[g] for g in MIX_LATE if g in gb}))

    def bwd_ride(self, l, gb_early):
        self.send[(l, "early")] = _mixer_grad_groups({g: gb_early[g] for g in MIX_EARLY})
        return _ReduceRide([self.send[(ll, part)][g] for ll, part in self._grad_riders(l) for g in _part_groups(part)])

    def bwd_done(self, l, outs):
        outs = list(outs)
        for ll, part in self._grad_riders(l):
            self.recv[(ll, part)] = {g: outs.pop(0) for g in _part_groups(part)}

    def tail_ride(self, l, small_groups):
        if l > 0:
            return None
        send = [self.send[(0, "late")][g] for g in MIX_LATE]
        by_core = [a.reshape((4, 2) + a.shape[1:]).transpose((1, 0) + tuple(range(2, a.ndim + 1))) for a in send]
        core = lax.axis_index("c")
        own = [lax.dynamic_index_in_dim(a, core, axis=0, keepdims=False) for a in by_core]
        got = _swap_with_sibling(by_core, "reduce_d2d")
        pairs = [_add_pairs(a, b, f"reduce_pair_add_{g}") for g, a, b in zip(MIX_LATE, own, got)]
        return _Combo([_ChipExchangeRide(pairs), _GatherRide(small_groups)])

    def tail_done(self, l, outs):
        self.recv[(l, "late")] = dict(zip(MIX_LATE, outs[:len(MIX_LATE)]))
        return outs[len(MIX_LATE):]

    def finish(self):
        per_layer = [{g: a for part in ("early", "late", "ffn") for g, a in self.recv[(l, part)].items()}
                     for l in range(DEPTH)]
        return _grads_from_groups({g: _sum_layers([per_layer[l][g] for l in range(DEPTH)], f"reduce_sum_{g}")
                                   for g in BIG})


def _local_step(x, positions, target, smalls, plan):
    tables = _rope_tables(positions)
    saved = []
    h, h_norm = x, None
    for l in range(DEPTH):
        h, h2, svm, wm = _mixer_fwd(h, h_norm, tables, smalls[l], plan, l)
        wf = plan.ffn_weights(l)
        next_gain = smalls[l + 1]["mix_norm_pre"] if l + 1 < DEPTH else None
        h, h_norm, svf = _ffn_fwd(h, h2, wf, smalls[l], f"l{l}", next_gain)
        saved.append((svm, svf, wm, wf))
    dy, sq = _loss_grad(h, target, "loss_grad")
    small = [None] * DEPTH
    for l in reversed(range(DEPTH)):
        svm, svf, wm, wf = saved[l]
        dmid, gbf, gsf = _ffn_bwd(dy, svf, wf, smalls[l], f"l{l}")
        plan.add_grads(l, "ffn", gbf)

        def pack_small(gs, l=l, gsf=gsf):
            if l > 0:
                return None
            return _small_groups([{**gsf, **gs, "mix_norm_pre": jnp.zeros((D_MODEL,), F32)}] + small[1:])

        dy, gsm, small_gathered = _mixer_bwd(dmid, svm, tables, wm, smalls[l], plan, l, pack_small)
        small[l] = {**gsf, **gsm}
    return sq, dy, small, small_gathered


def kernel(x, positions, mix_norm_pre, w_in, q_norm, w_uq, kv_norm, w_uk, w_uv, w_attn_o, conv_w, conv_b, conv_ln_g, conv_ln_b, w_conv_o, pool_w, pool_scale, w_pool_o, w_mix_o, mix_norm_post, ffn_norm_pre, w_gate, w_up, w_down, ffn_norm_post, loss_target, m_mix_norm_pre, m_w_in, m_q_norm, m_w_uq, m_kv_norm, m_w_uk, m_w_uv, m_w_attn_o, m_conv_w, m_conv_b, m_conv_ln_g, m_conv_ln_b, m_w_conv_o, m_pool_w, m_pool_scale, m_w_pool_o, m_w_mix_o, m_mix_norm_post, m_ffn_norm_pre, m_w_gate, m_w_up, m_w_down, m_ffn_norm_post, v_mix_norm_pre, v_w_in, v_q_norm, v_w_uq, v_kv_norm, v_w_uk, v_w_uv, v_w_attn_o, v_conv_w, v_conv_b, v_conv_ln_g, v_conv_ln_b, v_w_conv_o, v_pool_w, v_pool_scale, v_w_pool_o, v_w_mix_o, v_mix_norm_post, v_ffn_norm_pre, v_w_gate, v_w_up, v_w_down, v_ffn_norm_post):
    given = dict(locals())
    dev = 4 * lax.axis_index("x") + 2 * lax.axis_index("y") + lax.axis_index("c")

    plan = _Plan({n: given[n] for n in BIG}, conv_w)
    cw = CONV_C // N_DEV
    conv_w_full = plan.gather_first().transpose(1, 2, 0, 3).reshape(DEPTH, CONV_W, CONV_C)
    smalls = []
    for l in range(DEPTH):
        sm = {n: given[n][l] for n in SMALL if n != "conv_w"}
        sm["conv_w"] = _pad_axis(conv_w_full[l], 0, CONV_HALO)
        smalls.append(sm)

    sq, grad_x, small, small_groups = _local_step(x[0], positions[0], loss_target[0], smalls, plan)
    loss = lax.psum(0.5 / D_MODEL * jnp.sum(sq), ("x", "y", "c"))
    views = {n: lax.optimization_barrier(_lane_major(n, g)) for n, g in plan.finish().items()}
    grads = {n: _from_lane_major(n, views[n]) for n in BIG}

    small_sum = _small_from_groups([_sum_blocks(g, f"sum_small_grads_{i}") for i, g in enumerate(small_groups)])
    last = _pad_axis(small[0]["mix_norm_pre"].reshape(1, D_MODEL), 0, SUBLANES)
    last_sum = _sum_blocks(_all_gather([last], "gather_last_norm_grad")[0], "sum_last_norm_grad")[0]
    small_sum["mix_norm_pre"] = small_sum["mix_norm_pre"].at[0].set(last_sum)
    for n in SMALL:
        grads[n] = small_sum[n]
    grads["conv_w"] = lax.dynamic_slice_in_dim(small_sum["conv_w"], dev * cw, cw, axis=2)

    delta, new_m, new_v = {}, {}, {}
    for n in WEIGHTS:
        g_view = views[n] if n in views else grads[n]
        w_view, m_view, v_view = [_lane_major(n, given[k]) for k in (n, "m_" + n, "v_" + n)]
        res = _adamw(w_view, g_view, m_view, v_view, f"adamw_{n}")
        delta[n], new_m[n], new_v[n] = [_from_lane_major(n, r) for r in res]
    return (loss, grad_x[None], *[grads[n] for n in WEIGHTS], *[delta[n] for n in WEIGHTS],
            *[new_m[n] for n in WEIGHTS], *[new_v[n] for n in WEIGHTS])
```

```python
import functools
import math

import jax
import jax.numpy as jnp
from jax import lax
from jax.experimental import pallas as pl
from jax.experimental.pallas import tpu as pltpu

F32, BF16 = jnp.float32, jnp.bfloat16
MESH = pl.DeviceIdType.MESH

LANES = 128
SUBLANES = 8
VMEM_LIMIT_BYTES = 56 * 1024 * 1024
MATMUL_VMEM_BYTES = 40 * 1024 * 1024

N_DEV = 8
D_MODEL = 1024
DEPTH = 2
N_HEADS = 8
QK_NOPE, QK_ROPE, V_HEAD = 64, 32, 64
HEAD_PAD = LANES
Q_RANK, KV_RANK = 384, 256
ROPE_THETA = 10000.0
CONV_C, CONV_W = 512, 31
CONV_HALO = 32
POOL_WINDOWS = (2, 4, 8, 16)
POOL_C, POOL_G = 512, 4
POOL_GD = POOL_C // POOL_G
D_FF = 2816
FF_SHARD = D_FF // N_DEV
FF_SHARD_PAD = 3 * LANES
D_FF_PAD = N_DEV * FF_SHARD_PAD
W_IN_SHARD = 660
EPS = 1e-6
ATTN_SCALE = 1.0 / math.sqrt(QK_NOPE + QK_ROPE)
LOG2E = 1.4426950408889634
LR, B1, B2, ADAM_EPS, WD, STEP = 0.001, 0.9, 0.999, 1e-08, 0.01, 10

Z_W = 5376
ZC_GATE = (1024, 0)
ZC_GATES = (3072, 0)
ZC_CONV_A = (512, 6)
ZC_CONV_G = (512, 7)
ZC_CONV = (1024, 3)
ZC_POOL = (512, 8)
ZC_Q = (384, 12)
ZC_KR = (128, 39)
ZC_KV = (256, 20)
ZC_QKR = (768, 6)
W_IN_PIECES = ((0, 384, 4608), (384, 640, 5120), (640, 672, 5056), (672, 1696, 3072), (1696, 2208, 4096),
               (2208, 5280, 0))

BIG = ("w_in", "w_uq", "w_uk", "w_uv", "w_attn_o", "w_conv_o", "w_pool_o", "w_mix_o", "w_gate", "w_up", "w_down")
SMALL = ("mix_norm_pre", "q_norm", "kv_norm", "conv_w", "conv_b", "conv_ln_g", "conv_ln_b", "pool_w", "pool_scale",
         "mix_norm_post", "ffn_norm_pre", "ffn_norm_post")
WEIGHTS = ("mix_norm_pre", "w_in", "q_norm", "w_uq", "kv_norm", "w_uk", "w_uv", "w_attn_o", "conv_w", "conv_b",
           "conv_ln_g", "conv_ln_b", "w_conv_o", "pool_w", "pool_scale", "w_pool_o", "w_mix_o", "mix_norm_post",
           "ffn_norm_pre", "w_gate", "w_up", "w_down", "ffn_norm_post")


def _params(*semantics):
    return pltpu.CompilerParams(dimension_semantics=semantics, vmem_limit_bytes=VMEM_LIMIT_BYTES)


def _tile(dim, cap):
    if dim <= cap:
        return dim
    for t in range(cap - cap % LANES, 0, -LANES):
        if dim % t == 0:
            return t
    raise ValueError(f"no tile for {dim} under {cap}")


def _row_tile(rows, row_bytes, budget=1 << 20):
    if rows * row_bytes <= budget:
        return rows
    cap = max(16, budget // row_bytes)
    for t in range(cap - cap % 16, 0, -16):
        if rows % t == 0:
            return t
    return rows


def _rows(ts, width, cidx=0):
    return pl.BlockSpec((ts, width), lambda i: (i, cidx))


def _fixed(shape):
    return pl.BlockSpec(shape, lambda *_: (0,) * len(shape))


def _sigmoid(x):
    return 1.0 / (1.0 + jnp.exp(-x))


def _matmul(a, b, mode, out_dtype, name, add=None, blocked=False, ride=None):
    nb = n_blk = 0
    blocked = blocked or b.ndim == 3
    if mode == "nn":
        (m, k) = a.shape
        n = b.shape[0] * b.shape[2] if blocked else b.shape[1]
    elif mode == "nt":
        (m, k) = a.shape
        n = b.shape[1] if blocked else b.shape[0]
    else:
        (k, m), n = a.shape, b.shape[1]
    if blocked:
        nb = b.shape[2] if mode != "tn" else n // N_DEV
    unit = nb if blocked and mode != "nt" else LANES
    out_bytes = jnp.dtype(out_dtype).itemsize + (4 if add is not None else 0)
    best = None
    for tn_c in range(unit, min(n, 1536) + 1, unit):
        for tm_c in sorted({256, 512, 1024, 2048, min(m, 2048)}):
            if n % tn_c or m % tm_c or (blocked and mode != "nt" and N_DEV % (tn_c // nb)):
                continue
            vmem = 2 * (tm_c * k * 2 + tn_c * k * 2 + tm_c * tn_c * out_bytes) + tm_c * tn_c * 4 + tn_c * k * 2
            if vmem <= MATMUL_VMEM_BYTES and (best is None or tm_c * tn_c / (tm_c + tn_c) > best[0]):
                best = (tm_c * tn_c / (tm_c + tn_c), tm_c, tn_c)
    if best is None:
        raise ValueError(f"{name}: no tiles for {m}x{n}x{k}")
    _, tm, tn = best
    if blocked:
        n_blk = N_DEV if mode == "nt" else tn // nb
    dims = {"nn": ((1,), (0,)), "nt": ((1,), (1,)), "tn": ((0,), (0,))}[mode]
    a_spec = pl.BlockSpec((k, tm), lambda i, j: (0, i)) if mode == "tn" else pl.BlockSpec((tm, k), lambda i, j: (i, 0))
    b_spec = pl.BlockSpec((tn, k), lambda i, j: (j, 0)) if mode == "nt" else pl.BlockSpec((k, tn), lambda i, j: (0, j))
    o_spec = pl.BlockSpec((tm, tn), lambda i, j: (i, j))
    out_shape = jax.ShapeDtypeStruct((m, n), out_dtype)
    if blocked and mode == "nn":
        b_spec = pl.BlockSpec((n_blk, k, nb), lambda i, j: (j, 0, 0))
    elif blocked and mode == "nt":
        b_spec = pl.BlockSpec((n_blk, tn, nb), lambda i, j: (0, j, 0))
    elif blocked:
        o_spec = pl.BlockSpec((n_blk, tm, nb), lambda i, j: (j, i, 0))
        out_shape = jax.ShapeDtypeStruct((N_DEV, m, nb), out_dtype)
    has_add = add is not None
    grid = (m // tm, n // tn)

    def body(*refs):
        (a_ref, b_ref, *rest), start, finish = _ride_hooks(ride, refs, 3 if has_add else 2, 1, grid)
        start()
        o_ref = rest[-1]
        if blocked and mode != "tn":
            bv = jnp.concatenate([b_ref[c] for c in range(n_blk)], axis=1) if n_blk > 1 else b_ref[0]
        else:
            bv = b_ref[...]
        total = lax.dot_general(a_ref[...], bv, (dims, ((), ())), preferred_element_type=F32)
        if has_add:
            total = total + rest[0][...]
        if blocked and mode == "tn":
            for c in range(n_blk):
                o_ref[c] = total[:, c * nb:(c + 1) * nb].astype(o_ref.dtype)
        else:
            o_ref[...] = total.astype(o_ref.dtype)
        finish()

    operands = (a, b, add) if has_add else (a, b)
    (out,), rode = _ride_call(ride, body, name, (out_shape,), grid, [a_spec, b_spec] + ([o_spec] if has_add else []),
                              (o_spec,), ("parallel", "parallel"), operands)
    return out if ride is None else (out, rode)


def _rms_fwd(x, win, gain, out_dtype, name, res=None, then=None):
    width, cidx = win
    s = x.shape[0]
    ts = min(s, 512)
    has_res, has_then = res is not None, then is not None

    def norm(v, g_ref):
        return (v * lax.rsqrt(jnp.mean(v * v, axis=-1, keepdims=True) + EPS)) * g_ref[...]

    def body(x_ref, g_ref, *rest):
        y = norm(x_ref[...].astype(F32), g_ref)
        if has_res:
            y = rest[0][...] + y
        o_ref = rest[-2] if has_then else rest[-1]
        o_ref[...] = y.astype(o_ref.dtype)
        if has_then:
            rest[-1][...] = norm(y, rest[-3]).astype(BF16)

    ops = (x, gain.reshape(1, width)) + ((res,) if has_res else ()) + ((then.reshape(1, width),) if has_then else ())
    out_shape = (jax.ShapeDtypeStruct((s, width), out_dtype),) + ((jax.ShapeDtypeStruct((s, width), BF16),) * has_then)
    out = pl.pallas_call(
        body, name=name, out_shape=out_shape, grid=(s // ts,),
        in_specs=([_rows(ts, width, cidx), _fixed((1, width))] + ([_rows(ts, width)] if has_res else [])
                  + ([_fixed((1, width))] if has_then else [])),
        out_specs=(_rows(ts, width),) * len(out_shape), compiler_params=_params("parallel"))(*ops)
    return out if has_then else out[0]


def _into(dz, n_inputs, out_index):
    return dict(in_specs=[ANY], operands=(dz,), input_output_aliases={n_inputs: out_index},
                out_shape=jax.ShapeDtypeStruct(dz.shape, dz.dtype))


def _rms_bwd(x, win, gain, dy, out_dtype, name, add=None, dz=None):
    width, cidx = win
    s = x.shape[0]
    ts = min(s, 512)
    has_add = add is not None

    def body(x_ref, g_ref, dy_ref, *rest):
        dx_ref, dg_ref = rest[-2], rest[-1]
        xv = x_ref[...].astype(F32)
        r = lax.rsqrt(jnp.mean(xv * xv, axis=-1, keepdims=True) + EPS)
        xh = xv * r
        dyv = dy_ref[...].astype(F32)
        dyg = dyv * g_ref[...]
        dx = r * (dyg - xh * jnp.mean(dyg * xh, axis=-1, keepdims=True))
        if has_add:
            dx = dx + rest[0][...]
        dx_ref[...] = dx.astype(dx_ref.dtype)

        @pl.when(pl.program_id(0) == 0)
        def _():
            dg_ref[...] = jnp.zeros_like(dg_ref)

        dg_ref[...] += jnp.sum(dyv * xh, axis=0, keepdims=True)

    ops = (x, gain.reshape(1, width), dy) + ((add,) if has_add else ())
    in_specs = [_rows(ts, width, cidx), _fixed((1, width)), _rows(ts, width)] + ([_rows(ts, width)] if has_add else [])
    dx_shape, dx_spec, alias = jax.ShapeDtypeStruct((s, width), out_dtype), _rows(ts, width), {}
    if dz is not None:
        into = _into(dz, len(ops), 0)
        ops, in_specs, alias = ops + into["operands"], in_specs + into["in_specs"], into["input_output_aliases"]
        dx_shape, dx_spec = into["out_shape"], _rows(ts, width, cidx)
    dx, dg = pl.pallas_call(
        body, name=name, out_shape=(dx_shape, jax.ShapeDtypeStruct((1, width), F32)), grid=(s // ts,),
        in_specs=in_specs, out_specs=(dx_spec, _fixed((1, width))), input_output_aliases=alias,
        compiler_params=_params("arbitrary"))(*ops)
    return dx, dg.reshape(width)


def _rope(x, c, s1, s2):
    return x * c + pltpu.roll(x, 16, 1) * s1 + pltpu.roll(x, LANES - 16, 1) * s2


def _rope_t(g, c, s1, s2):
    return g * c + pltpu.roll(g * s1, LANES - 16, 1) + pltpu.roll(g * s2, 16, 1)


def _rope_tables(positions):
    inv_freq = ROPE_THETA ** (-jnp.arange(0, QK_ROPE, 2, dtype=F32) / QK_ROPE)
    ang = positions.astype(F32)[:, None] * inv_freq
    cos, sin = jnp.cos(ang), jnp.sin(ang)
    n = positions.shape[0]
    one, zero = jnp.ones((n, 1), F32), jnp.zeros((n, 1), F32)
    c = jnp.concatenate([jnp.tile(one, (1, QK_NOPE)), cos, cos, jnp.tile(one, (1, 32))], axis=1)
    s1 = jnp.concatenate([jnp.tile(zero, (1, QK_NOPE + 16)), sin, jnp.tile(zero, (1, 32))], axis=1)
    s2 = jnp.concatenate([jnp.tile(zero, (1, QK_NOPE)), -sin, jnp.tile(zero, (1, 48))], axis=1)
    return c, s1, s2


def _qkv_up_fwd(z, q_gain, kv_gain, w_uq, w_uk, w_uv, tables, name):
    s = z.shape[0]
    ts = min(s, 512)
    hw = N_HEADS * HEAD_PAD
    kv0 = Q_RANK + LANES

    def norm(v, g_ref):
        return ((v * lax.rsqrt(jnp.mean(v * v, axis=-1, keepdims=True) + EPS)) * g_ref[...]).astype(BF16)

    def body(z_ref, gq_ref, gkv_ref, wq_ref, wk_ref, wv_ref, c_ref, s1_ref, s2_ref, cq_ref, ckv_ref, q_ref, k_ref, v_ref):
        c, s1, s2 = c_ref[...], s1_ref[...], s2_ref[...]
        cqv = norm(z_ref[:, pl.ds(0, Q_RANK)].astype(F32), gq_ref)
        ckvv = norm(z_ref[:, pl.ds(kv0, KV_RANK)].astype(F32), gkv_ref)
        cq_ref[...] = cqv
        ckv_ref[...] = ckvv
        kr = _rope(z_ref[:, pl.ds(Q_RANK, LANES)].astype(F32), c, s1, s2)
        for h in range(N_HEADS):
            sl = slice(h * HEAD_PAD, (h + 1) * HEAD_PAD)
            q_ref[:, sl] = _rope(jnp.dot(cqv, wq_ref[h], preferred_element_type=F32), c, s1, s2).astype(BF16)
            k_ref[:, sl] = (jnp.dot(ckvv, wk_ref[h], preferred_element_type=F32) + kr).astype(BF16)
            v_ref[:, sl] = jnp.dot(ckvv, wv_ref[h], preferred_element_type=F32).astype(BF16)

    tab = _rows(ts, LANES)
    wide = jax.ShapeDtypeStruct((s, hw), BF16)
    return pl.pallas_call(
        body, name=name,
        out_shape=(jax.ShapeDtypeStruct((s, Q_RANK), BF16), jax.ShapeDtypeStruct((s, KV_RANK), BF16), wide, wide, wide),
        grid=(s // ts,),
        in_specs=[_rows(ts, *ZC_QKR), _fixed((1, Q_RANK)), _fixed((1, KV_RANK)), _fixed(w_uq.shape), _fixed(w_uk.shape),
                  _fixed(w_uv.shape), tab, tab, tab],
        out_specs=(_rows(ts, Q_RANK), _rows(ts, KV_RANK)) + (_rows(ts, hw),) * 3, compiler_params=_params("parallel"))(
            z, q_gain.reshape(1, -1), kv_gain.reshape(1, -1), w_uq, w_uk, w_uv, *tables)


def _qkv_up_bwd(dq, dk, dv, z, w_uq, w_uk, w_uv, tables, q_gain, kv_gain, dz, name):
    s = dq.shape[0]
    ts = min(s, 512)
    hw = N_HEADS * HEAD_PAD
    zw = ZC_QKR[0]
    kv0 = Q_RANK + LANES
    dims_nt = (((1,), (1,)), ((), ()))

    def norm_bwd(xv, g_ref, dyv):
        r = lax.rsqrt(jnp.mean(xv * xv, axis=-1, keepdims=True) + EPS)
        xh = xv * r
        dyg = dyv * g_ref[...]
        return r * (dyg - xh * jnp.mean(dyg * xh, axis=-1, keepdims=True)), jnp.sum(dyv * xh, axis=0, keepdims=True)

    def body(dq_ref, dk_ref, dv_ref, z_ref, wq_ref, wk_ref, wv_ref, c_ref, s1_ref, s2_ref, gq_ref, gkv_ref, _,
             dqf_ref, dkf_ref, dz_ref, dgq_ref, dgkv_ref):
        c, s1, s2 = c_ref[...], s1_ref[...], s2_ref[...]
        ksum = jnp.zeros((ts, HEAD_PAD), F32)
        dcq = jnp.zeros((ts, Q_RANK), F32)
        dckv = jnp.zeros((ts, KV_RANK), F32)
        for h in range(N_HEADS):
            sl = slice(h * HEAD_PAD, (h + 1) * HEAD_PAD)
            dqh = _rope_t(dq_ref[:, sl], c, s1, s2).astype(BF16)
            dkv = dk_ref[:, sl]
            dkh = dkv.astype(BF16)
            dqf_ref[:, sl] = dqh
            dkf_ref[:, sl] = dkh
            ksum = ksum + dkv
            dcq = dcq + lax.dot_general(dqh, wq_ref[h], dims_nt, preferred_element_type=F32)
            dckv = dckv + (lax.dot_general(dkh, wk_ref[h], dims_nt, preferred_element_type=F32)
                           + lax.dot_general(dv_ref[:, sl], wv_ref[h], dims_nt, preferred_element_type=F32))
        dxq, dgq = norm_bwd(z_ref[:, pl.ds(0, Q_RANK)].astype(F32), gq_ref, dcq)
        dxkv, dgkv = norm_bwd(z_ref[:, pl.ds(kv0, KV_RANK)].astype(F32), gkv_ref, dckv)
        lane = lax.broadcasted_iota(jnp.int32, (ts, HEAD_PAD), 1)
        in_rope = (lane >= QK_NOPE) & (lane < QK_NOPE + QK_ROPE)
        dz_ref[:, pl.ds(0, Q_RANK)] = dxq.astype(BF16)
        dz_ref[:, pl.ds(Q_RANK, LANES)] = jnp.where(in_rope, _rope_t(ksum, c, s1, s2), 0.0).astype(BF16)
        dz_ref[:, pl.ds(kv0, KV_RANK)] = dxkv.astype(BF16)

        @pl.when(pl.program_id(0) == 0)
        def _():
            dgq_ref[...] = jnp.zeros_like(dgq_ref)
            dgkv_ref[...] = jnp.zeros_like(dgkv_ref)

        dgq_ref[...] += dgq
        dgkv_ref[...] += dgkv

    tab = _rows(ts, LANES)
    into = _into(dz, 12, 2)
    dqf, dkf, dz, dgq, dgkv = pl.pallas_call(
        body, name=name,
        out_shape=(jax.ShapeDtypeStruct((s, hw), BF16), jax.ShapeDtypeStruct((s, hw), BF16), into["out_shape"],
                   jax.ShapeDtypeStruct((1, Q_RANK), F32), jax.ShapeDtypeStruct((1, KV_RANK), F32)),
        grid=(s // ts,),
        in_specs=[_rows(ts, hw), _rows(ts, hw), _rows(ts, hw), _rows(ts, *ZC_QKR), _fixed(w_uq.shape), _fixed(w_uk.shape),
                  _fixed(w_uv.shape), tab, tab, tab, _fixed((1, Q_RANK)), _fixed((1, KV_RANK))] + into["in_specs"],
        out_specs=(_rows(ts, hw), _rows(ts, hw), _rows(ts, *ZC_QKR), _fixed((1, Q_RANK)), _fixed((1, KV_RANK))),
        input_output_aliases=into["input_output_aliases"], compiler_params=_params("arbitrary"))(
            dq, dk, dv, z, w_uq, w_uk, w_uv, *tables, q_gain.reshape(1, -1), kv_gain.reshape(1, -1), dz)
    return dqf, dkf, dz, dgq.reshape(-1), dgkv.reshape(-1)


def _attn_tile(s):
    return min(s, 512)


def _raw_scores(q, k, masked, row0=0):
    sc = lax.dot_general(q, k, (((1,), (1,)), ((), ())), preferred_element_type=F32)
    if masked:
        rows = row0 + lax.broadcasted_iota(jnp.int32, sc.shape, 0)
        cols = lax.broadcasted_iota(jnp.int32, sc.shape, 1)
        sc = jnp.where(cols <= rows, sc, -jnp.inf)
    return sc


def _ride_hooks(ride, refs, n_in, n_out, grid):
    if ride is None:
        return refs, lambda: None, lambda: None
    n = len(ride.arrays)
    own = refs[:n_in] + refs[n_in + n:n_in + n + n_out]
    ins, outs, sems = refs[n_in:n_in + n], refs[n_in + n + n_out:n_in + 2 * n + n_out], refs[n_in + 2 * n + n_out:]
    at_first = functools.reduce(lambda a, b: a & b, [pl.program_id(ax) == 0 for ax in range(len(grid))])
    at_last = functools.reduce(lambda a, b: a & b, [pl.program_id(ax) == g - 1 for ax, g in enumerate(grid)])
    return own, lambda: pl.when(at_first)(lambda: ride.start(ins, outs, sems)), \
        lambda: pl.when(at_last)(lambda: ride.finish(ins, outs, sems))


def _ride_call(ride, body, name, out_shape, grid, in_specs, out_specs, semantics, operands):
    n = 0 if ride is None else len(ride.arrays)
    res = pl.pallas_call(
        body, name=name, out_shape=tuple(out_shape) + (tuple(ride.out_shape) if n else ()), grid=grid,
        in_specs=list(in_specs) + [ANY] * n, out_specs=tuple(out_specs) + (ANY,) * n,
        scratch_shapes=list(ride.scratch) if n else [],
        compiler_params=_params(*(("arbitrary",) * len(grid) if n else semantics)))(*operands, *(ride.arrays if n else ()))
    return res[:len(out_shape)], list(res[len(out_shape):])


def _flash_fwd(q, k, v, name, ride=None):
    s = q.shape[0]
    t = _attn_tile(s)
    c2 = ATTN_SCALE * LOG2E
    grid = (N_HEADS, s // t)

    def body(*refs):
        (q_ref, k_ref, v_ref, o_ref, lse_ref), start, finish = _ride_hooks(ride, refs, 3, 2, grid)
        start()
        i = pl.program_id(1)
        qv = q_ref[...]

        def chunk(j, carry, masked):
            m_old, l_old, acc = carry
            at = pl.ds(pl.multiple_of(j * t, t), t)
            sc = _raw_scores(qv, k_ref[at, :], masked)
            m_new = jnp.maximum(m_old, jnp.max(sc, axis=-1, keepdims=True))
            p = jnp.exp2((sc - m_new) * c2)
            alpha = jnp.exp2((m_old - m_new) * c2)
            l_new = alpha * l_old + jnp.sum(p, axis=-1, keepdims=True)
            acc = alpha * acc + jnp.dot(p.astype(BF16), v_ref[at, :], preferred_element_type=F32)
            return m_new, l_new, acc

        init = (jnp.full((t, 1), -jnp.inf, F32), jnp.zeros((t, 1), F32), jnp.zeros((t, HEAD_PAD), F32))
        carry = lax.fori_loop(0, i, lambda j, cr: chunk(j, cr, False), init)
        m_fin, l_fin, acc = chunk(i, carry, True)
        o_ref[...] = (acc / l_fin).astype(o_ref.dtype)
        lse_ref[...] = jnp.broadcast_to(m_fin * ATTN_SCALE + jnp.log(l_fin), (t, HEAD_PAD))
        finish()

    qo = pl.BlockSpec((t, HEAD_PAD), lambda h, i: (i, h))
    whole = pl.BlockSpec((s, HEAD_PAD), lambda h, i: (0, h))
    return _ride_call(
        ride, body, name, (jax.ShapeDtypeStruct(q.shape, BF16), jax.ShapeDtypeStruct(q.shape, F32)), grid,
        [qo, whole, whole], (qo, qo), ("parallel", "parallel"), (q, k, v))


def _attn_out_bwd(dya, w_attn_o, o, name):
    s, d = dya.shape
    hw = N_HEADS * HEAD_PAD
    t = _attn_tile(s)

    def body(d_ref, w_ref, o_ref, delta_ref, dob_ref):
        wv = jnp.concatenate([w_ref[c] for c in range(N_DEV)], axis=1)
        do = lax.dot_general(d_ref[...], wv, (((1,), (1,)), ((), ())), preferred_element_type=F32)
        for h in range(N_HEADS):
            sl = slice(h * HEAD_PAD, (h + 1) * HEAD_PAD)
            dov = do[:, sl]
            delta_ref[:, sl] = jnp.broadcast_to(jnp.sum(dov * o_ref[:, sl].astype(F32), axis=-1, keepdims=True),
                                                (t, HEAD_PAD))
            dob_ref[:, sl] = dov.astype(BF16)

    blk = _rows(t, hw)
    return pl.pallas_call(
        body, name=name, out_shape=(jax.ShapeDtypeStruct(o.shape, F32), jax.ShapeDtypeStruct(o.shape, BF16)),
        grid=(s // t,), in_specs=[_rows(t, d), _fixed(w_attn_o.shape), blk], out_specs=(blk, blk),
        compiler_params=_params("parallel"))(dya, w_attn_o, o)


def _flash_bwd(q, k, v, do, lse, delta, name, ride=None):
    s = q.shape[0]
    t = _attn_tile(s)
    nt = s // t
    c2 = ATTN_SCALE * LOG2E
    grid = (N_HEADS, nt)

    def body(*refs):
        (q_ref, k_ref, v_ref, do_ref, lse_ref, delta_ref, dq_ref, dk_ref, dv_ref), start, finish = _ride_hooks(
            ride, refs, 6, 3, grid)
        start()
        j = pl.program_id(1)
        kv, vv = k_ref[...], v_ref[...]

        @pl.when(j == 0)
        def _():
            dq_ref[...] = jnp.zeros_like(dq_ref)

        def chunk(i, carry, masked):
            dk_acc, dv_acc = carry
            at = pl.ds(pl.multiple_of(i * t, t), t)
            qi, doi = q_ref[at, :], do_ref[at, :]
            sc = _raw_scores(qi, kv, masked)
            p = jnp.exp2(sc * c2 - lse_ref[at, pl.ds(0, 1)] * LOG2E)
            dp = lax.dot_general(doi, vv, (((1,), (1,)), ((), ())), preferred_element_type=F32)
            ds = (p * (dp - delta_ref[at, pl.ds(0, 1)])).astype(BF16)
            dv_acc = dv_acc + lax.dot_general(p.astype(BF16), doi, (((0,), (0,)), ((), ())), preferred_element_type=F32)
            dk_acc = dk_acc + lax.dot_general(ds, qi, (((0,), (0,)), ((), ())), preferred_element_type=F32)
            dq_ref[at, :] += jnp.dot(ds, kv, preferred_element_type=F32) * ATTN_SCALE
            return dk_acc, dv_acc

        zero = jnp.zeros((t, HEAD_PAD), F32)
        carry = chunk(j, (zero, zero), True)
        dk_acc, dv_acc = lax.fori_loop(j + 1, nt, lambda i, cr: chunk(i, cr, False), carry)
        dk_ref[...] = dk_acc * ATTN_SCALE
        dv_ref[...] = dv_acc.astype(BF16)
        finish()

    blk = pl.BlockSpec((t, HEAD_PAD), lambda h, j: (j, h))
    whole = pl.BlockSpec((s, HEAD_PAD), lambda h, j: (0, h))
    return _ride_call(
        ride, body, name, (jax.ShapeDtypeStruct(q.shape, F32), jax.ShapeDtypeStruct(q.shape, F32),
                           jax.ShapeDtypeStruct(q.shape, BF16)), grid,
        [whole, blk, blk, whole, whole, whole], (whole, blk, blk), ("parallel", "arbitrary"), (q, k, v, do, lse, delta))


def _conv_tile(s):
    return min(s, 256)


def _halo_before(t, width, cidx):
    per = t // CONV_HALO
    return pl.BlockSpec((CONV_HALO, width), lambda i: (jnp.maximum(i * per - 1, 0), cidx))


def _halo_after(t, width, cidx, n_tiles):
    per = t // CONV_HALO
    last = n_tiles * per - 1
    return pl.BlockSpec((CONV_HALO, width), lambda i: (jnp.minimum((i + 1) * per, last), cidx))


def _fill_glu(hbuf, ap_ref, gp_ref, a_ref, g_ref, t):
    first = pl.program_id(0) == 0
    hbuf[pl.ds(0, CONV_HALO), :] = jnp.where(first, 0.0, ap_ref[...].astype(F32) * _sigmoid(gp_ref[...].astype(F32)))
    hbuf[pl.ds(CONV_HALO, t), :] = a_ref[...].astype(F32) * _sigmoid(g_ref[...].astype(F32))


def _phase_copies(dst, src, t):
    n = t + CONV_HALO - SUBLANES
    for s in range(1, SUBLANES):
        dst[s, pl.ds(0, n), :] = src[pl.ds(s, n), :]


def _window(phases, src, k, t):
    if k % SUBLANES == 0:
        return src[pl.ds(k, t), :]
    return phases[k % SUBLANES, pl.ds(k - k % SUBLANES, t), :]


def _layer_norm_parts(co):
    mu = jnp.mean(co, axis=-1, keepdims=True)
    xc = co - mu
    rstd = lax.rsqrt(jnp.mean(xc * xc, axis=-1, keepdims=True) + EPS)
    return xc * rstd, rstd


def _conv_fwd(z, conv_w, conv_b, ln_g, ln_b, name):
    s = z.shape[0]
    t = _conv_tile(s)
    off = CONV_HALO - (CONV_W - 1)

    def body(ap_ref, gp_ref, a_ref, g_ref, w_ref, b_ref, lg_ref, lb_ref, hc_ref, co_ref, hbuf, hph):
        _fill_glu(hbuf, ap_ref, gp_ref, a_ref, g_ref, t)
        _phase_copies(hph, hbuf, t)
        acc = jnp.zeros((t, CONV_C), F32) + b_ref[...]
        for j in range(CONV_W):
            acc = acc + _window(hph, hbuf, off + j, t) * w_ref[pl.ds(j, 1), :]
        co_ref[...] = acc
        xh, _ = _layer_norm_parts(acc)
        y = xh * lg_ref[...] + lb_ref[...]
        hc_ref[...] = (y * _sigmoid(y)).astype(BF16)

    vec = _fixed((1, CONV_C))
    return pl.pallas_call(
        body, name=name, out_shape=(jax.ShapeDtypeStruct((s, CONV_C), BF16), jax.ShapeDtypeStruct((s, CONV_C), F32)),
        grid=(s // t,),
        in_specs=[_halo_before(t, *ZC_CONV_A), _halo_before(t, *ZC_CONV_G), _rows(t, *ZC_CONV_A), _rows(t, *ZC_CONV_G),
                  _fixed((CONV_HALO, CONV_C)), vec, vec, vec],
        out_specs=(_rows(t, CONV_C), _rows(t, CONV_C)),
        scratch_shapes=[pltpu.VMEM((t + CONV_HALO, CONV_C), F32), pltpu.VMEM((SUBLANES, t + CONV_HALO, CONV_C), F32)],
        compiler_params=_params("parallel"))(z, z, z, z, conv_w, conv_b.reshape(1, -1), ln_g.reshape(1, -1),
                                             ln_b.reshape(1, -1))


def _conv_bwd_norm(dhc, co, ln_g, ln_b, name):
    s = co.shape[0]
    t = min(s, 512)

    def body(dhc_ref, co_ref, lg_ref, lb_ref, dco_ref, dg_ref, db_ref, dcb_ref):
        xh, rstd = _layer_norm_parts(co_ref[...])
        y = xh * lg_ref[...] + lb_ref[...]
        sg = _sigmoid(y)
        dy = dhc_ref[...] * (sg * (1.0 + y * (1.0 - sg)))
        dxh = dy * lg_ref[...]
        dco = rstd * (dxh - jnp.mean(dxh, axis=-1, keepdims=True) - xh * jnp.mean(dxh * xh, axis=-1, keepdims=True))
        dco_ref[...] = dco

        @pl.when(pl.program_id(0) == 0)
        def _():
            dg_ref[...] = jnp.zeros_like(dg_ref)
            db_ref[...] = jnp.zeros_like(db_ref)
            dcb_ref[...] = jnp.zeros_like(dcb_ref)

        dg_ref[...] += jnp.sum(dy * xh, axis=0, keepdims=True)
        db_ref[...] += jnp.sum(dy, axis=0, keepdims=True)
        dcb_ref[...] += jnp.sum(dco, axis=0, keepdims=True)

    vec = _fixed((1, CONV_C))
    one = jax.ShapeDtypeStruct((1, CONV_C), F32)
    dco, dg, db, dcb = pl.pallas_call(
        body, name=name, out_shape=(jax.ShapeDtypeStruct((s, CONV_C), F32), one, one, one), grid=(s // t,),
        in_specs=[_rows(t, CONV_C), _rows(t, CONV_C), vec, vec], out_specs=(_rows(t, CONV_C), vec, vec, vec),
        compiler_params=_params("arbitrary"))(dhc, co, ln_g.reshape(1, -1), ln_b.reshape(1, -1))
    return dco, dg.reshape(-1), db.reshape(-1), dcb.reshape(-1)


def _conv_bwd_taps(dco, z, conv_w, dz, name):
    s = z.shape[0]
    t = _conv_tile(s)
    nt = s // t
    off = CONV_HALO - (CONV_W - 1)

    def body(ap_ref, gp_ref, a_ref, g_ref, d_ref, dn_ref, w_ref, _, du_ref, dw_ref, hbuf, dbuf, hph, dph):
        i = pl.program_id(0)
        _fill_glu(hbuf, ap_ref, gp_ref, a_ref, g_ref, t)
        dbuf[pl.ds(0, t), :] = d_ref[...]
        dbuf[pl.ds(t, CONV_HALO), :] = jnp.where(i == nt - 1, 0.0, dn_ref[...])
        _phase_copies(hph, hbuf, t)
        _phase_copies(dph, dbuf, t)

        @pl.when(i == 0)
        def _():
            dw_ref[...] = jnp.zeros_like(dw_ref)

        dcur = d_ref[...]
        dh = jnp.zeros((t, CONV_C), F32)
        for j in range(CONV_W):
            dh = dh + _window(dph, dbuf, CONV_W - 1 - j, t) * w_ref[pl.ds(j, 1), :]
            dw_ref[pl.ds(j, 1), :] += jnp.sum(dcur * _window(hph, hbuf, off + j, t), axis=0, keepdims=True)
        a, sg = a_ref[...].astype(F32), _sigmoid(g_ref[...].astype(F32))
        du_ref[:, pl.ds(0, CONV_C)] = (dh * sg).astype(BF16)
        du_ref[:, pl.ds(CONV_C, CONV_C)] = (dh * a * sg * (1.0 - sg)).astype(BF16)

    into = _into(dz, 7, 0)
    return pl.pallas_call(
        body, name=name, out_shape=(into["out_shape"], jax.ShapeDtypeStruct((CONV_HALO, CONV_C), F32)), grid=(nt,),
        in_specs=[_halo_before(t, *ZC_CONV_A), _halo_before(t, *ZC_CONV_G), _rows(t, *ZC_CONV_A), _rows(t, *ZC_CONV_G),
                  _rows(t, CONV_C), _halo_after(t, CONV_C, 0, nt), _fixed((CONV_HALO, CONV_C))] + into["in_specs"],
        out_specs=(_rows(t, *ZC_CONV), _fixed((CONV_HALO, CONV_C))), input_output_aliases=into["input_output_aliases"],
        scratch_shapes=[pltpu.VMEM((t + CONV_HALO, CONV_C), F32), pltpu.VMEM((t + CONV_HALO, CONV_C), F32),
                        pltpu.VMEM((SUBLANES, t + CONV_HALO, CONV_C), F32),
                        pltpu.VMEM((SUBLANES, t + CONV_HALO, CONV_C), F32)],
        compiler_params=_params("arbitrary"))(z, z, z, z, dco, dco, conv_w, dz)


def _pool_tile(s):
    return min(s, 512)


def _pool_counts(row0, n, window):
    rows = row0 + lax.broadcasted_iota(jnp.int32, (n, POOL_GD), 0)
    return jnp.minimum(rows + 1, window).astype(F32)


def _pool_diff(ubuf, gi, window, row0, t):
    lanes = pl.ds(gi * POOL_GD, POOL_GD)
    tot = ubuf[pl.ds(CONV_HALO, t), lanes]
    cur = tot
    for back in range(1, window):
        tot = tot + ubuf[pl.ds(CONV_HALO - back, t), lanes]
    return tot / _pool_counts(row0, t, window) - cur


def _pool_fwd(z, pool_w, pool_scale, name):
    s = z.shape[0]
    t = _pool_tile(s)

    def body(up_ref, u_ref, w_ref, sc_ref, m_ref, ubuf):
        i = pl.program_id(0)
        ubuf[pl.ds(0, CONV_HALO), :] = jnp.where(i == 0, 0.0, up_ref[...].astype(F32))
        ubuf[pl.ds(CONV_HALO, t), :] = u_ref[...].astype(F32)
        for gi, window in enumerate(POOL_WINDOWS):
            d = _pool_diff(ubuf, gi, window, i * t, t)
            mm = jnp.dot(d.astype(BF16), w_ref[gi].astype(BF16), preferred_element_type=F32)
            lanes = pl.ds(gi * POOL_GD, POOL_GD)
            m_ref[:, lanes] = (mm * sc_ref[:, lanes]).astype(BF16)

    return pl.pallas_call(
        body, name=name, out_shape=jax.ShapeDtypeStruct((s, POOL_C), BF16), grid=(s // t,),
        in_specs=[_halo_before(t, *ZC_POOL), _rows(t, *ZC_POOL), _fixed((POOL_G, POOL_GD, POOL_GD)), _fixed((1, POOL_C))],
        out_specs=_rows(t, POOL_C), scratch_shapes=[pltpu.VMEM((t + CONV_HALO, POOL_C), F32)],
        compiler_params=_params("parallel"))(z, z, pool_w, pool_scale.reshape(1, -1))


def _pool_bwd(dm, z, pool_w, pool_scale, dz, name):
    s = z.shape[0]
    t = _pool_tile(s)
    nt = s // t

    def body(up_ref, u_ref, dm_ref, dmn_ref, w_ref, sc_ref, _, du_ref, dw_ref, dsc_ref, ubuf, ebuf):
        i = pl.program_id(0)
        ubuf[pl.ds(0, CONV_HALO), :] = jnp.where(i == 0, 0.0, up_ref[...].astype(F32))
        ubuf[pl.ds(CONV_HALO, t), :] = u_ref[...].astype(F32)

        @pl.when(i == 0)
        def _():
            dw_ref[...] = jnp.zeros_like(dw_ref)
            dsc_ref[...] = jnp.zeros_like(dsc_ref)

        dm_next = jnp.where(i == nt - 1, 0.0, dmn_ref[...])
        for gi, window in enumerate(POOL_WINDOWS):
            lanes = pl.ds(gi * POOL_GD, POOL_GD)
            wb = w_ref[gi].astype(BF16)
            scale = sc_ref[:, lanes]
            d = _pool_diff(ubuf, gi, window, i * t, t).astype(BF16)
            mm = jnp.dot(d, wb, preferred_element_type=F32)
            dmv = dm_ref[:, lanes]
            dsc_ref[:, lanes] += jnp.sum(dmv * mm, axis=0, keepdims=True)
            dmm = (dmv * scale).astype(BF16)
            dw_ref[gi] += lax.dot_general(d, dmm, (((0,), (0,)), ((), ())), preferred_element_type=F32)
            dd = lax.dot_general(dmm, wb, (((1,), (1,)), ((), ())), preferred_element_type=F32)
            dd_next = lax.dot_general((dm_next[:, gi * POOL_GD:(gi + 1) * POOL_GD] * scale).astype(BF16), wb,
                                      (((1,), (1,)), ((), ())), preferred_element_type=F32)
            ebuf[pl.ds(0, t), lanes] = dd / _pool_counts(i * t, t, window)
            ebuf[pl.ds(t, CONV_HALO), lanes] = dd_next / _pool_counts((i + 1) * t, CONV_HALO, window)
            du = -dd
            for ahead in range(window):
                du = du + ebuf[pl.ds(ahead, t), lanes]
            du_ref[:, lanes] = du.astype(BF16)

    into = _into(dz, 6, 0)
    du, dw, dsc = pl.pallas_call(
        body, name=name,
        out_shape=(into["out_shape"], jax.ShapeDtypeStruct((POOL_G, POOL_GD, POOL_GD), F32),
                   jax.ShapeDtypeStruct((1, POOL_C), F32)), grid=(nt,),
        in_specs=[_halo_before(t, *ZC_POOL), _rows(t, *ZC_POOL), _rows(t, POOL_C), _halo_after(t, POOL_C, 0, nt),
                  _fixed((POOL_G, POOL_GD, POOL_GD)), _fixed((1, POOL_C))] + into["in_specs"],
        out_specs=(_rows(t, *ZC_POOL), _fixed((POOL_G, POOL_GD, POOL_GD)), _fixed((1, POOL_C))),
        input_output_aliases=into["input_output_aliases"],
        scratch_shapes=[pltpu.VMEM((t + CONV_HALO, POOL_C), F32), pltpu.VMEM((t + CONV_HALO, POOL_C), F32)],
        compiler_params=_params("arbitrary"))(z, z, dm, dm, pool_w, pool_scale.reshape(1, -1), dz)
    return du, dw, dsc.reshape(-1)


def _gate_specs(ts):
    width, first = ZC_GATE
    return [_rows(ts, width, first + b) for b in range(3)]


def _branches_merge_fwd(z, acts, ws, name):
    s = z.shape[0]
    ts = min(s, 512)

    def body(g0, g1, g2, a0, a1, a2, w0, w1, w2, y0, y1, y2, m_ref):
        merged = jnp.zeros((ts, D_MODEL), F32)
        for g_ref, a_ref, w_ref, y_ref in ((g0, a0, w0, y0), (g1, a1, w1, y1), (g2, a2, w2, y2)):
            wv = jnp.concatenate([w_ref[c] for c in range(N_DEV)], axis=1)
            yb = jnp.dot(a_ref[...], wv, preferred_element_type=F32).astype(BF16)
            y_ref[...] = yb
            merged = merged + _sigmoid(g_ref[...].astype(F32)) * yb.astype(F32)
        m_ref[...] = merged.astype(BF16)

    out = jax.ShapeDtypeStruct((s, D_MODEL), BF16)
    res = pl.pallas_call(
        body, name=name, out_shape=(out,) * 4, grid=(s // ts,),
        in_specs=_gate_specs(ts) + [_rows(ts, a.shape[1]) for a in acts] + [_fixed(w.shape) for w in ws],
        out_specs=(_rows(ts, D_MODEL),) * 4, compiler_params=_params("parallel"))(z, z, z, *acts, *ws)
    return tuple(res[:3]), res[3]


def _merge_bwd(z, ys, dmo, w_mix_o, name):
    s = z.shape[0]
    ts = min(s, 256)

    def body(g0, g1, g2, y0, y1, y2, dm_ref, w_ref, dy0, dy1, dy2, dz_ref):
        dmv = lax.dot_general(dm_ref[...], w_ref[...], (((1,), (1,)), ((), ())), preferred_element_type=F32)
        for b, (g_ref, y_ref, dy_ref) in enumerate(((g0, y0, dy0), (g1, y1, dy1), (g2, y2, dy2))):
            sg = _sigmoid(g_ref[...].astype(F32))
            dy_ref[...] = (dmv * sg).astype(BF16)
            dz_ref[:, pl.ds(b * D_MODEL, D_MODEL)] = (dmv * y_ref[...].astype(F32) * sg * (1.0 - sg)).astype(BF16)

    out = jax.ShapeDtypeStruct((s, D_MODEL), BF16)
    return pl.pallas_call(
        body, name=name, out_shape=(out,) * 3 + (jax.ShapeDtypeStruct((s, Z_W), BF16),), grid=(s // ts,),
        in_specs=_gate_specs(ts) + [_rows(ts, D_MODEL)] * 4 + [_fixed(w_mix_o.shape)],
        out_specs=(_rows(ts, D_MODEL),) * 3 + (_rows(ts, *ZC_GATES),),
        compiler_params=_params("parallel"))(z, z, z, *ys, dmo, w_mix_o)


def _ffn_up_fwd(h, w_gate, w_up, name):
    s, d = h.shape
    nb = w_gate.shape[2]
    f = N_DEV * nb
    tm, n_blk = min(s, 1024), 2
    tn = n_blk * nb
    blk = pl.BlockSpec((tm, tn), lambda i, j: (i, j))
    wspec = pl.BlockSpec((n_blk, d, nb), lambda i, j: (j, 0, 0))

    def body(h_ref, wg_ref, wu_ref, hg_ref, hu_ref, act_ref):
        hv = h_ref[...]
        g = jnp.dot(hv, jnp.concatenate([wg_ref[c] for c in range(n_blk)], axis=1), preferred_element_type=F32)
        u = jnp.dot(hv, jnp.concatenate([wu_ref[c] for c in range(n_blk)], axis=1), preferred_element_type=F32)
        hg_ref[...] = g.astype(hg_ref.dtype)
        hu_ref[...] = u.astype(hu_ref.dtype)
        act_ref[...] = (g * _sigmoid(g) * u).astype(BF16)

    return pl.pallas_call(
        body, name=name,
        out_shape=(jax.ShapeDtypeStruct((s, f), BF16),) * 3,
        grid=(s // tm, f // tn), in_specs=[pl.BlockSpec((tm, d), lambda i, j: (i, 0)), wspec, wspec],
        out_specs=(blk, blk, blk), compiler_params=_params("parallel", "parallel"))(h, w_gate, w_up)


def _ffn_down_bwd(dfo, w_down, hg, hu, name):
    s, d = dfo.shape
    f = w_down.shape[0]
    tm, tn = min(s, 1024), _tile(f, 1024)
    blk = pl.BlockSpec((tm, tn), lambda i, j: (i, j))

    def body(d_ref, w_ref, g_ref, u_ref, dg_ref, du_ref):
        dact = lax.dot_general(d_ref[...], w_ref[...], (((1,), (1,)), ((), ())), preferred_element_type=F32)
        g = g_ref[...].astype(F32)
        sg = _sigmoid(g)
        dg_ref[...] = (dact * u_ref[...].astype(F32) * (sg * (1.0 + g * (1.0 - sg)))).astype(BF16)
        du_ref[...] = (dact * g * sg).astype(BF16)

    out = jax.ShapeDtypeStruct((s, f), BF16)
    return pl.pallas_call(
        body, name=name, out_shape=(out, out), grid=(s // tm, f // tn),
        in_specs=[pl.BlockSpec((tm, d), lambda i, j: (i, 0)), pl.BlockSpec((tn, d), lambda i, j: (j, 0)), blk, blk],
        out_specs=(blk, blk), compiler_params=_params("parallel", "parallel"))(dfo, w_down, hg, hu)


def _ffn_up_dx(dhg, dhu, w_gate, w_up, name):
    s, f = dhg.shape
    d = w_gate.shape[1]
    tm, tn = min(s, 1024), min(d, 256)
    dims = (((1,), (1,)), ((), ()))

    def body(g_ref, u_ref, wg_ref, wu_ref, o_ref):
        wg = jnp.concatenate([wg_ref[c] for c in range(N_DEV)], axis=1)
        wu = jnp.concatenate([wu_ref[c] for c in range(N_DEV)], axis=1)
        o_ref[...] = (lax.dot_general(g_ref[...], wg, dims, preferred_element_type=F32)
                      + lax.dot_general(u_ref[...], wu, dims, preferred_element_type=F32))

    a_spec = pl.BlockSpec((tm, f), lambda i, j: (i, 0))
    w_spec = pl.BlockSpec((N_DEV, tn, w_gate.shape[2]), lambda i, j: (0, j, 0))
    return pl.pallas_call(
        body, name=name, out_shape=jax.ShapeDtypeStruct((s, d), F32), grid=(s // tm, d // tn),
        in_specs=[a_spec, a_spec, w_spec, w_spec], out_specs=pl.BlockSpec((tm, tn), lambda i, j: (i, j)),
        compiler_params=_params("parallel", "parallel"))(dhg, dhu, w_gate, w_up)


def _loss_grad(y, target, name):
    s, d = y.shape
    ts = min(s, 512)

    def body(y_ref, t_ref, dy_ref, sq_ref):
        e = y_ref[...] - t_ref[...]
        dy_ref[...] = e / d

        @pl.when(pl.program_id(0) == 0)
        def _():
            sq_ref[...] = jnp.zeros_like(sq_ref)

        sq_ref[...] += jnp.sum(e * e, axis=0, keepdims=True)

    return pl.pallas_call(
        body, name=name, out_shape=(jax.ShapeDtypeStruct((s, d), F32), jax.ShapeDtypeStruct((1, d), F32)),
        grid=(s // ts,), in_specs=[_rows(ts, d), _rows(ts, d)], out_specs=(_rows(ts, d), _fixed((1, d))),
        compiler_params=_params("arbitrary"))(y, target)


def _adamw(w, g, m, v, name):
    shape = w.shape
    cols = shape[-1]
    keep3 = w.ndim == 3 and shape[1] < SUBLANES
    view = shape if keep3 else (math.prod(shape[:-1]), cols)
    rows = view[0]
    if keep3:
        cap = max(1, (2 << 20) // (SUBLANES * cols * 4))
        tr = max(t for t in range(1, cap + 1) if rows % t == 0)
    else:
        tr = _row_tile(rows, cols * 4, budget=2 << 20)

    def body(w_ref, g_ref, m_ref, v_ref, d_ref, mo_ref, vo_ref):
        gv = g_ref[...]
        mn = B1 * m_ref[...] + (1.0 - B1) * gv
        vn = B2 * v_ref[...] + (1.0 - B2) * (gv * gv)
        m_hat = mn / (1.0 - B1 ** STEP)
        v_hat = vn / (1.0 - B2 ** STEP)
        d_ref[...] = -LR * (m_hat / (jnp.sqrt(v_hat) + ADAM_EPS) + WD * w_ref[...])
        mo_ref[...] = mn
        vo_ref[...] = vn

    spec = pl.BlockSpec((tr,) + view[1:], lambda i: (i,) + (0,) * (len(view) - 1))
    out = jax.ShapeDtypeStruct(view, F32)
    res = pl.pallas_call(
        body, name=name, out_shape=(out,) * 3, grid=(rows // tr,), in_specs=[spec] * 4, out_specs=(spec,) * 3,
        compiler_params=_params("parallel"))(*[t.reshape(view) for t in (w, g, m, v)])
    return tuple(r.reshape(shape) for r in res)


LANE_MAJOR = ("w_uq", "w_uk", "w_uv", "w_gate", "w_up")


def _lane_major(name, a):
    if name == "w_in":
        return a.transpose(2, 0, 1)
    if name in LANE_MAJOR:
        return a.transpose(0, 2, 1)
    return a


def _from_lane_major(name, a):
    if name == "w_in":
        return a.transpose(1, 2, 0)
    return _lane_major(name, a)


ANY = pl.BlockSpec(memory_space=pl.ANY)


class _GatherRide:
    def __init__(self, arrays):
        n = len(arrays)
        self.arrays = list(arrays)
        self.out_shape = [jax.ShapeDtypeStruct((N_DEV,) + a.shape, a.dtype) for a in arrays]
        self.scratch = [pltpu.SemaphoreType.DMA((n, 7)), pltpu.SemaphoreType.DMA((n, 7)), pltpu.SemaphoreType.DMA((n,))]

    def _copies(self, ins, outs, sems):
        send_sems, recv_sems, local_sems = sems
        n = len(self.arrays)
        x, y, c = lax.axis_index("x"), lax.axis_index("y"), lax.axis_index("c")
        me, sibling = (x, y, c), (x, y, 1 - c)
        chips = [(1 - x, y), (x, 1 - y), (1 - x, 1 - y)]

        def slot(a, px, py, pc):
            return outs[a].at[4 * px + 2 * py + pc]

        def copy(a, k, block, to, src=None):
            return pltpu.make_async_remote_copy(
                src_ref=slot(a, *block) if src is None else src, dst_ref=slot(a, *block), send_sem=send_sems.at[a, k],
                recv_sem=recv_sems.at[a, k], device_id=to, device_id_type=MESH)

        mine = [pltpu.make_async_copy(ins[a], slot(a, *me), local_sems.at[a]) for a in range(n)]
        first = []
        for a in range(n):
            first.append(copy(a, 0, me, sibling, src=ins[a]))
            first += [copy(a, 1 + j, me, (*chip, c), src=ins[a]) for j, chip in enumerate(chips)]
        return n, me, sibling, chips, c, copy, mine, first

    def start(self, ins, outs, sems):
        _, _, _, _, _, _, mine, first = self._copies(ins, outs, sems)
        for cp in mine + first:
            cp.start()

    def finish(self, ins, outs, sems):
        n, me, sibling, chips, c, copy, mine, first = self._copies(ins, outs, sems)
        passed = []
        for j, chip in enumerate(chips):
            for a in range(n):
                copy(a, 1 + j, (*chip, c), me).wait_recv()
                passed.append(copy(a, 4 + j, (*chip, c), sibling))
                passed[-1].start()
        for a in range(n):
            copy(a, 0, sibling, me).wait_recv()
            for j, chip in enumerate(chips):
                copy(a, 4 + j, (*chip, 1 - c), me).wait_recv()
        for cp in first + passed:
            cp.wait_send()
        for cp in mine:
            cp.wait()


class _ReduceRide:
    def __init__(self, arrays):
        n = len(arrays)
        self.arrays = list(arrays)
        self.out_shape = [jax.ShapeDtypeStruct(a.shape, a.dtype) for a in arrays]
        self.scratch = [pltpu.SemaphoreType.DMA((n, 7)), pltpu.SemaphoreType.DMA((n, 7)), pltpu.SemaphoreType.DMA((n,))]

    def _copies(self, ins, outs, sems):
        send_sems, recv_sems, local_sems = sems
        n = len(self.arrays)
        x, y, c = lax.axis_index("x"), lax.axis_index("y"), lax.axis_index("c")
        mine = [pltpu.make_async_copy(ins[a].at[4 * x + 2 * y + c], outs[a].at[0], local_sems.at[a]) for a in range(n)]
        copies = []
        for a in range(n):
            for k in range(1, N_DEV):
                px = 1 - x if k & 4 else x
                py = 1 - y if k & 2 else y
                pc = 1 - c if k & 1 else c
                copies.append(pltpu.make_async_remote_copy(
                    src_ref=ins[a].at[4 * px + 2 * py + pc], dst_ref=outs[a].at[k], send_sem=send_sems.at[a, k - 1],
                    recv_sem=recv_sems.at[a, k - 1], device_id=(px, py, pc), device_id_type=MESH))
        return mine, copies

    def start(self, ins, outs, sems):
        mine, copies = self._copies(ins, outs, sems)
        for cp in mine + copies:
            cp.start()

    def finish(self, ins, outs, sems):
        mine, copies = self._copies(ins, outs, sems)
        for cp in copies + mine:
            cp.wait()


def _run_ride(ride, name):
    n = len(ride.arrays)

    def body(*refs):
        ins, outs, sems = refs[:n], refs[n:2 * n], refs[2 * n:]
        ride.start(ins, outs, sems)
        ride.finish(ins, outs, sems)

    return pl.pallas_call(body, name=name, out_shape=ride.out_shape, in_specs=[ANY] * n, out_specs=[ANY] * n,
                          scratch_shapes=ride.scratch)(*ride.arrays)


def _all_gather(arrays, name):
    return _run_ride(_GatherRide(arrays), name)


def _swap_with_sibling(arrays, name):
    n = len(arrays)

    def body(*refs):
        ins, outs = refs[:n], refs[n:2 * n]
        send_sems, recv_sems = refs[2 * n:]
        x, y, c = lax.axis_index("x"), lax.axis_index("y"), lax.axis_index("c")
        copies = [pltpu.make_async_remote_copy(
            src_ref=ins[a].at[1 - c], dst_ref=outs[a], send_sem=send_sems.at[a], recv_sem=recv_sems.at[a],
            device_id=(x, y, 1 - c), device_id_type=MESH) for a in range(n)]
        for cp in copies:
            cp.start()
        for cp in copies:
            cp.wait()

    return pl.pallas_call(
        body, name=name, out_shape=[jax.ShapeDtypeStruct(a.shape[1:], a.dtype) for a in arrays],
        in_specs=[ANY] * n, out_specs=[ANY] * n,
        scratch_shapes=[pltpu.SemaphoreType.DMA((n,)), pltpu.SemaphoreType.DMA((n,))])(*arrays)


class _ChipExchangeRide:
    def __init__(self, arrays):
        n = len(arrays)
        self.arrays = list(arrays)
        self.out_shape = [jax.ShapeDtypeStruct(a.shape, a.dtype) for a in arrays]
        self.scratch = [pltpu.SemaphoreType.DMA((n, 3)), pltpu.SemaphoreType.DMA((n, 3)), pltpu.SemaphoreType.DMA((n,))]

    def _copies(self, ins, outs, sems):
        send_sems, recv_sems, local_sems = sems
        n = len(self.arrays)
        x, y, c = lax.axis_index("x"), lax.axis_index("y"), lax.axis_index("c")
        partners = [(x, 1 - y), (1 - x, y), (1 - x, 1 - y)]
        mine = [pltpu.make_async_copy(ins[a].at[2 * x + y], outs[a].at[0], local_sems.at[a]) for a in range(n)]
        copies = [pltpu.make_async_remote_copy(
            src_ref=ins[a].at[2 * px + py], dst_ref=outs[a].at[1 + k], send_sem=send_sems.at[a, k],
            recv_sem=recv_sems.at[a, k], device_id=(px, py, c), device_id_type=MESH)
            for a in range(n) for k, (px, py) in enumerate(partners)]
        return mine, copies

    def start(self, ins, outs, sems):
        mine, copies = self._copies(ins, outs, sems)
        for cp in mine + copies:
            cp.start()

    def finish(self, ins, outs, sems):
        mine, copies = self._copies(ins, outs, sems)
        for cp in copies + mine:
            cp.wait()


class _Combo:
    def __init__(self, rides):
        self.rides = rides
        self.arrays = [a for r in rides for a in r.arrays]
        self.out_shape = [o for r in rides for o in r.out_shape]
        self.scratch = [sc for r in rides for sc in r.scratch]

    def _parts(self, ins, outs, sems):
        at_a = at_s = 0
        for r in self.rides:
            na, ns = len(r.arrays), len(r.scratch)
            yield r, ins[at_a:at_a + na], outs[at_a:at_a + na], sems[at_s:at_s + ns]
            at_a, at_s = at_a + na, at_s + ns

    def start(self, ins, outs, sems):
        for r, i, o, sm in self._parts(ins, outs, sems):
            r.start(i, o, sm)

    def finish(self, ins, outs, sems):
        for r, i, o, sm in reversed(list(self._parts(ins, outs, sems))):
            r.finish(i, o, sm)


def _as_rows(a, lead):
    return a.reshape(a.shape[:lead] + (math.prod(a.shape[lead:-1]), a.shape[-1]))


def _add_pairs(a, b, name):
    a2, b2 = _as_rows(a, 0), _as_rows(b, 0)
    rows, cols = a2.shape
    tr = _row_tile(rows, cols * 4)

    def body(a_ref, b_ref, o_ref):
        o_ref[...] = (a_ref[...].astype(F32) + b_ref[...].astype(F32)).astype(o_ref.dtype)

    spec = _rows(tr, cols)
    out = pl.pallas_call(body, name=name, out_shape=jax.ShapeDtypeStruct(a2.shape, a.dtype), grid=(rows // tr,),
                         in_specs=[spec, spec], out_specs=spec, compiler_params=_params("parallel"))(a2, b2)
    return out.reshape(a.shape)


def _sum_blocks(a, name):
    a3 = _as_rows(a, 1)
    n, rows, cols = a3.shape
    tr = _row_tile(rows, n * cols * 4)

    def body(a_ref, o_ref):
        tot = a_ref[0].astype(F32)
        for k in range(1, n):
            tot = tot + a_ref[k].astype(F32)
        o_ref[...] = tot

    out = pl.pallas_call(body, name=name, out_shape=jax.ShapeDtypeStruct((rows, cols), F32), grid=(rows // tr,),
                         in_specs=[pl.BlockSpec((n, tr, cols), lambda j: (0, j, 0))], out_specs=_rows(tr, cols),
                         compiler_params=_params("parallel"))(a3)
    return out.reshape(a.shape[1:])


def _sum_layers(blocks, name):
    arrs = [_as_rows(a, 1) for a in blocks]
    rows, cols = arrs[0].shape[1:]
    tr = _row_tile(rows, max(a.shape[0] for a in arrs) * cols * 4, budget=8 << 20)
    nj = rows // tr

    def body(*refs):
        o_ref = refs[-1]
        for k, a_ref in enumerate(refs[:-1]):
            @pl.when(pl.program_id(0) == k)
            def _(a_ref=a_ref, n=arrs[k].shape[0]):
                tot = a_ref[0].astype(F32)
                for b in range(1, n):
                    tot = tot + a_ref[b].astype(F32)
                o_ref[0] = tot

    in_specs = [pl.BlockSpec((a.shape[0], tr, cols),
                             lambda l, j, k=k: (0, jnp.where(l == k, j, jnp.where(l < k, 0, nj - 1)), 0))
                for k, a in enumerate(arrs)]
    out = pl.pallas_call(body, name=name, out_shape=jax.ShapeDtypeStruct((len(arrs), rows, cols), F32),
                         grid=(len(arrs), nj), in_specs=in_specs,
                         out_specs=pl.BlockSpec((1, tr, cols), lambda l, j: (l, j, 0)),
                         compiler_params=_params("arbitrary", "arbitrary"))(*arrs)
    return out.reshape((len(arrs),) + blocks[0].shape[1:])


MIX_GROUPS = ("w_in", "w_uq", "w_uk", "w_uv", "w_attn_o", "w_conv_o", "w_pool_o", "w_mix_o")
FFN_GROUPS = ("w_gate", "w_up", "w_down")
MIX_EARLY = ("w_attn_o", "w_conv_o", "w_pool_o", "w_mix_o")
MIX_LATE = ("w_in", "w_uq", "w_uk", "w_uv")


def _pad_axis(a, axis, size):
    pad = [(0, 0)] * a.ndim
    pad[axis] = (0, size - a.shape[axis])
    return jnp.pad(a, pad)


def _local_groups(sh, l):
    out = {n: sh[n][l] for n in BIG}
    for n in ("w_uq", "w_uk", "w_uv"):
        out[n] = _pad_axis(out[n], -1, HEAD_PAD)
    for n in ("w_gate", "w_up"):
        out[n] = _pad_axis(out[n], -1, FF_SHARD_PAD)
    out["w_down"] = _pad_axis(out["w_down"], 0, FF_SHARD_PAD)
    return {n: v.astype(BF16) for n, v in out.items()}


def _arrange_w_in(blocks):
    parts, pos = [], 0
    for ref_lo, ref_hi, at in sorted(W_IN_PIECES, key=lambda p: p[2]):
        if at > pos:
            parts.append(jnp.zeros((blocks.shape[1], at - pos), blocks.dtype))
        for d in range(N_DEV):
            lo, hi = max(ref_lo, d * W_IN_SHARD), min(ref_hi, (d + 1) * W_IN_SHARD)
            if lo < hi:
                parts.append(blocks[d][:, lo - d * W_IN_SHARD:hi - d * W_IN_SHARD])
        pos = at + ref_hi - ref_lo
    if pos < Z_W:
        parts.append(jnp.zeros((blocks.shape[1], Z_W - pos), blocks.dtype))
    return jnp.concatenate(parts, axis=1)


def _w_in_shard(g, d):
    parts = []
    for ref_lo, ref_hi, at in W_IN_PIECES:
        lo, hi = max(ref_lo, d * W_IN_SHARD), min(ref_hi, (d + 1) * W_IN_SHARD)
        if lo < hi:
            parts.append(g[:, at + lo - ref_lo:at + hi - ref_lo])
    return jnp.concatenate(parts, axis=1)


def _mixer_weights(gat):
    w = {n: v for n, v in gat.items() if n != "w_in"}
    attn_o = gat["w_attn_o"].reshape(N_DEV, N_HEADS, V_HEAD, LANES)
    w["w_attn_o"] = _pad_axis(attn_o, 2, HEAD_PAD).reshape(N_DEV, N_HEADS * HEAD_PAD, LANES)
    w["w_mix_o"] = gat["w_mix_o"].reshape(D_MODEL, D_MODEL)
    return w


def _ffn_weights(gat):
    return {"w_gate": gat["w_gate"], "w_up": gat["w_up"], "w_down": gat["w_down"].reshape(D_FF_PAD, D_MODEL)}


def _mixer_grad_groups(gb):
    g = dict(gb)
    if "w_in" in gb:
        g["w_in"] = jnp.stack([_w_in_shard(gb["w_in"], d) for d in range(N_DEV)])
    if "w_attn_o" in gb:
        attn_o = gb["w_attn_o"].reshape(N_DEV, N_HEADS, HEAD_PAD, LANES)[:, :, :V_HEAD]
        g["w_attn_o"] = attn_o.reshape(N_DEV, N_HEADS * V_HEAD, LANES)
    if "w_mix_o" in gb:
        g["w_mix_o"] = gb["w_mix_o"].reshape(N_DEV, D_MODEL // N_DEV, D_MODEL)
    return g


def _ffn_grad_groups(gb):
    return {"w_gate": gb["w_gate"], "w_up": gb["w_up"], "w_down": gb["w_down"].reshape(N_DEV, FF_SHARD_PAD, D_MODEL)}


def _grads_from_groups(tot):
    g = dict(tot)
    g["w_uq"] = tot["w_uq"][..., :QK_NOPE + QK_ROPE]
    g["w_uk"], g["w_uv"] = tot["w_uk"][..., :QK_NOPE], tot["w_uv"][..., :V_HEAD]
    g["w_gate"], g["w_up"] = tot["w_gate"][..., :FF_SHARD], tot["w_up"][..., :FF_SHARD]
    g["w_down"] = tot["w_down"][..., :FF_SHARD, :]
    return g


SMALL_GROUPS = (
    (D_MODEL, ("mix_norm_pre", "mix_norm_post", "ffn_norm_pre", "ffn_norm_post")),
    (CONV_C, ("conv_w", "conv_b", "conv_ln_g", "conv_ln_b", "pool_scale")),
    (Q_RANK, ("q_norm",)), (KV_RANK, ("kv_norm",)), (POOL_GD, ("pool_w",)),
)


def _small_rows(name):
    return {"conv_w": CONV_HALO, "pool_w": POOL_G * POOL_GD}.get(name, SUBLANES)


def _small_groups(small):
    out = []
    for width, names in SMALL_GROUPS:
        parts = []
        for l in range(DEPTH):
            for n in names:
                part = small[l][n].reshape(-1, width)
                parts.append(_pad_axis(part, 0, _small_rows(n)))
        out.append(jnp.concatenate(parts, axis=0))
    return out


def _small_from_groups(groups):
    shapes = {"conv_w": (CONV_W, CONV_C), "pool_w": (POOL_G, POOL_GD, POOL_GD)}
    out = {}
    for (width, names), g in zip(SMALL_GROUPS, groups):
        row = 0
        for l in range(DEPTH):
            for n in names:
                rows = _small_rows(n)
                real = {"conv_w": CONV_W, "pool_w": POOL_G * POOL_GD}.get(n, 1)
                out.setdefault(n, []).append(g[row:row + real].reshape(shapes.get(n, (width,))))
                row += rows
    return {n: jnp.stack(v) for n, v in out.items()}


def _mixer_fwd(x, h, tables, sm, plan, l):
    nm = lambda n: f"{n}_l{l}"
    if h is None:
        h = _rms_fwd(x, (D_MODEL, 0), sm["mix_norm_pre"], BF16, nm("mix_pre_norm"))
    w_in, ride = plan.w_in(l), plan.in_proj_ride(l)
    if ride is None:
        z = _matmul(h, w_in, "nn", BF16, nm("in_proj"))
    else:
        z, rode = _matmul(h, w_in, "nn", BF16, nm("in_proj"), ride=ride)
        plan.in_proj_done(l, rode)
    w = dict(plan.mixer_weights(l), w_in=w_in)
    cq, ckv, q, k, v = _qkv_up_fwd(z, sm["q_norm"], sm["kv_norm"], w["w_uq"], w["w_uk"], w["w_uv"], tables, nm("qkv_up"))
    (o, lse), rode = _flash_fwd(q, k, v, nm("flash_fwd"), plan.fwd_ride(l))
    plan.fwd_done(l, rode)
    hc, co = _conv_fwd(z, sm["conv_w"], sm["conv_b"], sm["conv_ln_g"], sm["conv_ln_b"], nm("conv_fwd"))
    pm = _pool_fwd(z, sm["pool_w"], sm["pool_scale"], nm("pool_fwd"))
    ys, merged = _branches_merge_fwd(z, (o, hc, pm), (w["w_attn_o"], w["w_conv_o"], w["w_pool_o"]), nm("branches_merge"))
    mo = _matmul(merged, w["w_mix_o"], "nn", F32, nm("mix_out"))
    x_mid, h2 = _rms_fwd(mo, (D_MODEL, 0), sm["mix_norm_post"], F32, nm("mix_post_norm"), res=x, then=sm["ffn_norm_pre"])
    saved = dict(x=x, h=h, z=z, cq=cq, ckv=ckv, q=q, k=k, v=v, o=o, lse=lse, hc=hc, co=co, pm=pm, ys=ys, merged=merged,
                 mo=mo)
    return x_mid, h2, saved, w


def _ffn_fwd(x_mid, h2, w, sm, tag, next_gain):
    nm = lambda n: f"{n}_{tag}"
    hg, hu, act = _ffn_up_fwd(h2, w["w_gate"], w["w_up"], nm("ffn_up_fwd"))
    fo = _matmul(act, w["w_down"], "nn", F32, nm("ffn_down"))
    out = _rms_fwd(fo, (D_MODEL, 0), sm["ffn_norm_post"], F32, nm("ffn_post_norm"), res=x_mid, then=next_gain)
    out, h_next = out if next_gain is not None else (out, None)
    saved = dict(x_mid=x_mid, h2=h2, hg=hg, hu=hu, act=act, fo=fo)
    return out, h_next, saved


def _ffn_bwd(dout, sv, w, sm, tag):
    nm = lambda n: f"{n}_{tag}"
    gb, gs = {}, {}
    dfo, gs["ffn_norm_post"] = _rms_bwd(sv["fo"], (D_MODEL, 0), sm["ffn_norm_post"], dout, BF16, nm("ffn_post_norm_bwd"))
    gb["w_down"] = _matmul(sv["act"], dfo, "tn", BF16, nm("ffn_down_dw"))
    dhg, dhu = _ffn_down_bwd(dfo, w["w_down"], sv["hg"], sv["hu"], nm("ffn_down_bwd"))
    dh2 = _ffn_up_dx(dhg, dhu, w["w_gate"], w["w_up"], nm("ffn_up_dx"))
    gb["w_gate"] = _matmul(sv["h2"], dhg, "tn", BF16, nm("ffn_gate_dw"), blocked=True)
    gb["w_up"] = _matmul(sv["h2"], dhu, "tn", BF16, nm("ffn_up_dw"), blocked=True)
    dmid, gs["ffn_norm_pre"] = _rms_bwd(sv["x_mid"], (D_MODEL, 0), sm["ffn_norm_pre"], dh2, F32, nm("ffn_pre_norm_bwd"),
                                        add=dout)
    return dmid, gb, gs


def _mixer_bwd(dmid, sv, tables, w, sm, plan, l, pack_small):
    nm = lambda n: f"{n}_l{l}"
    gb, gs = {}, {}
    dmo, gs["mix_norm_post"] = _rms_bwd(sv["mo"], (D_MODEL, 0), sm["mix_norm_post"], dmid, BF16, nm("mix_post_norm_bwd"))
    gb["w_mix_o"] = _matmul(sv["merged"], dmo, "tn", BF16, nm("mix_out_dw"))
    dya, dyc, dyp, dz = _merge_bwd(sv["z"], sv["ys"], dmo, w["w_mix_o"], nm("merge_bwd"))
    dpm = _matmul(dyp, w["w_pool_o"], "nt", F32, nm("pool_out_dx"))
    gb["w_pool_o"] = _matmul(sv["pm"], dyp, "tn", BF16, nm("pool_out_dw"), blocked=True)
    dz, gs["pool_w"], gs["pool_scale"] = _pool_bwd(dpm, sv["z"], sm["pool_w"], sm["pool_scale"], dz, nm("pool_bwd"))
    dhc = _matmul(dyc, w["w_conv_o"], "nt", F32, nm("conv_out_dx"))
    gb["w_conv_o"] = _matmul(sv["hc"], dyc, "tn", BF16, nm("conv_out_dw"), blocked=True)
    dco, gs["conv_ln_g"], gs["conv_ln_b"], gs["conv_b"] = _conv_bwd_norm(dhc, sv["co"], sm["conv_ln_g"], sm["conv_ln_b"],
                                                                        nm("conv_bwd_norm"))
    dz, gs["conv_w"] = _conv_bwd_taps(dco, sv["z"], sm["conv_w"], dz, nm("conv_bwd_taps"))
    gb["w_attn_o"] = _matmul(sv["o"], dya, "tn", BF16, nm("attn_out_dw"), blocked=True)
    delta, dob = _attn_out_bwd(dya, w["w_attn_o"], sv["o"], nm("attn_out_bwd"))
    (dq, dk, dv), rode = _flash_bwd(sv["q"], sv["k"], sv["v"], dob, sv["lse"], delta, nm("flash_bwd"),
                                  plan.bwd_ride(l, gb))
    plan.bwd_done(l, rode)
    dqf, dkf, dz, gs["q_norm"], gs["kv_norm"] = _qkv_up_bwd(
        dq, dk, dv, sv["z"], w["w_uq"], w["w_uk"], w["w_uv"], tables, sm["q_norm"], sm["kv_norm"], dz, nm("qkv_up_bwd"))
    gb["w_uq"] = _matmul(sv["cq"], dqf, "tn", BF16, nm("q_up_dw"), blocked=True)
    gb["w_uk"] = _matmul(sv["ckv"], dkf, "tn", BF16, nm("k_up_dw"), blocked=True)
    gb["w_uv"] = _matmul(sv["ckv"], dv, "tn", BF16, nm("v_up_dw"), blocked=True)
    gb["w_in"] = _matmul(sv["h"], dz, "tn", BF16, nm("in_proj_dw"))
    plan.add_grads(l, "mix", gb)
    ride, small_gathered = plan.tail_ride(l, pack_small(gs)), []
    if ride is None:
        dh = _matmul(dz, w["w_in"], "nt", F32, nm("in_proj_dx"))
    else:
        dh, rode = _matmul(dz, w["w_in"], "nt", F32, nm("in_proj_dx"), ride=ride)
        small_gathered = plan.tail_done(l, rode)
    dx, gs["mix_norm_pre"] = _rms_bwd(sv["x"], (D_MODEL, 0), sm["mix_norm_pre"], dh, F32, nm("mix_pre_norm_bwd"), add=dmid)
    return dx, gs, small_gathered


def _part_groups(part):
    return {"mix": MIX_GROUPS, "ffn": FFN_GROUPS, "early": MIX_EARLY, "late": MIX_LATE}[part]


class _Plan:
    def __init__(self, shards, conv_w):
        self.local = [_local_groups(shards, l) for l in range(DEPTH)]
        self.conv_w = conv_w
        self.gat, self.send, self.recv = {}, {}, {}

    @staticmethod
    def _riders(l):
        return [(l, "ffn")] + ([(l + 1, "mix")] if l + 1 < DEPTH else [])

    @staticmethod
    def _grad_riders(l):
        return [(l, "ffn"), (l, "early")] + ([(l + 1, "late")] if l + 1 < DEPTH else [])

    def gather_first(self):
        w_in, conv_w = _all_gather([self.local[0]["w_in"], self.conv_w], "gather_w_in_l0")
        self.gat[(0, "mix")] = {"w_in": w_in}
        return conv_w

    def w_in(self, l):
        return _arrange_w_in(self.gat[(l, "mix")]["w_in"])

    def in_proj_ride(self, l):
        return _GatherRide([self.local[0][g] for g in MIX_GROUPS[1:]]) if l == 0 else None

    def in_proj_done(self, l, outs):
        self.gat[(l, "mix")].update(zip(MIX_GROUPS[1:], outs))

    def fwd_ride(self, l):
        return _GatherRide([self.local[ll][g] for ll, part in self._riders(l) for g in _part_groups(part)])

    def fwd_done(self, l, outs):
        outs = list(outs)
        for ll, part in self._riders(l):
            self.gat[(ll, part)] = {g: outs.pop(0) for g in _part_groups(part)}

    def mixer_weights(self, l):
        return _mixer_weights(self.gat[(l, "mix")])

    def ffn_weights(self, l):
        return _ffn_weights(self.gat[(l, "ffn")])

    def add_grads(self, l, part, gb):
        if part == "ffn":
            self.send[(l, "ffn")] = _ffn_grad_groups(gb)
        else:
            self.send.setdefault((l, "late"), {}).update(_mixer_grad_groups({g: gb[g] for g in MIX_LATE if g in gb}))

    def bwd_ride(self, l, gb_early):
        self.send[(l, "early")] = _mixer_grad_groups({g: gb_early[g] for g in MIX_EARLY})
        return _ReduceRide([self.send[(ll, part)][g] for ll, part in self._grad_riders(l) for g in _part_groups(part)])

    def bwd_done(self, l, outs):
        outs = list(outs)
        for ll, part in self._grad_riders(l):
            self.recv[(ll, part)] = {g: outs.pop(0) for g in _part_groups(part)}

    def tail_ride(self, l, small_groups):
        if l > 0:
            return None
        send = [self.send[(0, "late")][g] for g in MIX_LATE]
        by_core = [a.reshape((4, 2) + a.shape[1:]).transpose((1, 0) + tuple(range(2, a.ndim + 1))) for a in send]
        core = lax.axis_index("c")
        own = [lax.dynamic_index_in_dim(a, core, axis=0, keepdims=False) for a in by_core]
        got = _swap_with_sibling(by_core, "reduce_d2d")
        pairs = [_add_pairs(a, b, f"reduce_pair_add_{g}") for g, a, b in zip(MIX_LATE, own, got)]
        return _Combo([_ChipExchangeRide(pairs), _GatherRide(small_groups)])

    def tail_done(self, l, outs):
        self.recv[(l, "late")] = dict(zip(MIX_LATE, outs[:len(MIX_LATE)]))
        return outs[len(MIX_LATE):]

    def finish(self):
        per_layer = [{g: a for part in ("early", "late", "ffn") for g, a in self.recv[(l, part)].items()}
                     for l in range(DEPTH)]
        return _grads_from_groups({g: _sum_layers([per_layer[l][g] for l in range(DEPTH)], f"reduce_sum_{g}")
                                   for g in BIG})


def _local_step(x, positions, target, smalls, plan):
    tables = _rope_tables(positions)
    saved = []
    h, h_norm = x, None
    for l in range(DEPTH):
        h, h2, svm, wm = _mixer_fwd(h, h_norm, tables, smalls[l], plan, l)
        wf = plan.ffn_weights(l)
        next_gain = smalls[l + 1]["mix_norm_pre"] if l + 1 < DEPTH else None
        h, h_norm, svf = _ffn_fwd(h, h2, wf, smalls[l], f"l{l}", next_gain)
        saved.append((svm, svf, wm, wf))
    dy, sq = _loss_grad(h, target, "loss_grad")
    small = [None] * DEPTH
    for l in reversed(range(DEPTH)):
        svm, svf, wm, wf = saved[l]
        dmid, gbf, gsf = _ffn_bwd(dy, svf, wf, smalls[l], f"l{l}")
        plan.add_grads(l, "ffn", gbf)

        def pack_small(gs, l=l, gsf=gsf):
            if l > 0:
                return None
            return _small_groups([{**gsf, **gs, "mix_norm_pre": jnp.zeros((D_MODEL,), F32)}] + small[1:])

        dy, gsm, small_gathered = _mixer_bwd(dmid, svm, tables, wm, smalls[l], plan, l, pack_small)
        small[l] = {**gsf, **gsm}
    return sq, dy, small, small_gathered


def kernel(x, positions, mix_norm_pre, w_in, q_norm, w_uq, kv_norm, w_uk, w_uv, w_attn_o, conv_w, conv_b, conv_ln_g, conv_ln_b, w_conv_o, pool_w, pool_scale, w_pool_o, w_mix_o, mix_norm_post, ffn_norm_pre, w_gate, w_up, w_down, ffn_norm_post, loss_target, m_mix_norm_pre, m_w_in, m_q_norm, m_w_uq, m_kv_norm, m_w_uk, m_w_uv, m_w_attn_o, m_conv_w, m_conv_b, m_conv_ln_g, m_conv_ln_b, m_w_conv_o, m_pool_w, m_pool_scale, m_w_pool_o, m_w_mix_o, m_mix_norm_post, m_ffn_norm_pre, m_w_gate, m_w_up, m_w_down, m_ffn_norm_post, v_mix_norm_pre, v_w_in, v_q_norm, v_w_uq, v_kv_norm, v_w_uk, v_w_uv, v_w_attn_o, v_conv_w, v_conv_b, v_conv_ln_g, v_conv_ln_b, v_w_conv_o, v_pool_w, v_pool_scale, v_w_pool_o, v_w_mix_o, v_mix_norm_post, v_ffn_norm_pre, v_w_gate, v_w_up, v_w_down, v_ffn_norm_post):
    given = dict(locals())
    dev = 4 * lax.axis_index("x") + 2 * lax.axis_index("y") + lax.axis_index("c")

    plan = _Plan({n: given[n] for n in BIG}, conv_w)
    cw = CONV_C // N_DEV
    conv_w_full = plan.gather_first().transpose(1, 2, 0, 3).reshape(DEPTH, CONV_W, CONV_C)
    smalls = []
    for l in range(DEPTH):
        sm = {n: given[n][l] for n in SMALL if n != "conv_w"}
        sm["conv_w"] = _pad_axis(conv_w_full[l], 0, CONV_HALO)
        smalls.append(sm)

    sq, grad_x, small, small_groups = _local_step(x[0], positions[0], loss_target[0], smalls, plan)
    loss = lax.psum(0.5 / D_MODEL * jnp.sum(sq), ("x", "y", "c"))
    views = {n: lax.optimization_barrier(_lane_major(n, g)) for n, g in plan.finish().items()}
    grads = {n: _from_lane_major(n, views[n]) for n in BIG}

    small_sum = _small_from_groups([_sum_blocks(g, f"sum_small_grads_{i}") for i, g in enumerate(small_groups)])
    last = _pad_axis(small[0]["mix_norm_pre"].reshape(1, D_MODEL), 0, SUBLANES)
    last_sum = _sum_blocks(_all_gather([last], "gather_last_norm_grad")[0], "sum_last_norm_grad")[0]
    small_sum["mix_norm_pre"] = small_sum["mix_norm_pre"].at[0].set(last_sum)
    for n in SMALL:
        grads[n] = small_sum[n]
    grads["conv_w"] = lax.dynamic_slice_in_dim(small_sum["conv_w"], dev * cw, cw, axis=2)

    delta, new_m, new_v = {}, {}, {}
    for n in WEIGHTS:
        g_view = views[n] if n in views else grads[n]
        w_view, m_view, v_view = [_lane_major(n, given[k]) for k in (n, "m_" + n, "v_" + n)]
        res = _adamw(w_view, g_view, m_view, v_view, f"adamw_{n}")
        delta[n], new_m[n], new_v[n] = [_from_lane_major(n, r) for r in res]
    return (loss, grad_x[None], *[grads[n] for n in WEIGHTS], *[delta[n] for n in WEIGHTS],
            *[new_m[n] for n in WEIGHTS], *[new_v[n] for n in WEIGHTS])
```

```python
import functools
import math

import jax
import jax.numpy as jnp
from jax import lax
from jax.experimental import pallas as pl
from jax.experimental.pallas import tpu as pltpu

F32, BF16 = jnp.float32, jnp.bfloat16
MESH = pl.DeviceIdType.MESH

LANES = 128
SUBLANES = 8
VMEM_LIMIT_BYTES = 56 * 1024 * 1024
MATMUL_VMEM_BYTES = 40 * 1024 * 1024

N_DEV = 8
D_MODEL = 1024
DEPTH = 2
N_HEADS = 8
QK_NOPE, QK_ROPE, V_HEAD = 64, 32, 64
HEAD_PAD = LANES
Q_RANK, KV_RANK = 384, 256
ROPE_THETA = 10000.0
CONV_C, CONV_W = 512, 31
CONV_HALO = 32
POOL_WINDOWS = (2, 4, 8, 16)
POOL_C, POOL_G = 512, 4
POOL_GD = POOL_C // POOL_G
D_FF = 2816
FF_SHARD = D_FF // N_DEV
FF_SHARD_PAD = 3 * LANES
D_FF_PAD = N_DEV * FF_SHARD_PAD
W_IN_SHARD = 660
EPS = 1e-6
ATTN_SCALE = 1.0 / math.sqrt(QK_NOPE + QK_ROPE)
LOG2E = 1.4426950408889634
LR, B1, B2, ADAM_EPS, WD, STEP = 0.001, 0.9, 0.999, 1e-08, 0.01, 10

Z_W = 5376
ZC_GATE = (1024, 0)
ZC_GATES = (3072, 0)
ZC_CONV_A = (512, 6)
ZC_CONV_G = (512, 7)
ZC_CONV = (1024, 3)
ZC_POOL = (512, 8)
ZC_Q = (384, 12)
ZC_KR = (128, 39)
ZC_KV = (256, 20)
ZC_QKR = (768, 6)
W_IN_PIECES = ((0, 384, 4608), (384, 640, 5120), (640, 672, 5056), (672, 1696, 3072), (1696, 2208, 4096),
               (2208, 5280, 0))

BIG = ("w_in", "w_uq", "w_uk", "w_uv", "w_attn_o", "w_conv_o", "w_pool_o", "w_mix_o", "w_gate", "w_up", "w_down")
SMALL = ("mix_norm_pre", "q_norm", "kv_norm", "conv_w", "conv_b", "conv_ln_g", "conv_ln_b", "pool_w", "pool_scale",
         "mix_norm_post", "ffn_norm_pre", "ffn_norm_post")
WEIGHTS = ("mix_norm_pre", "w_in", "q_norm", "w_uq", "kv_norm", "w_uk", "w_uv", "w_attn_o", "conv_w", "conv_b",
           "conv_ln_g", "conv_ln_b", "w_conv_o", "pool_w", "pool_scale", "w_pool_o", "w_mix_o", "mix_norm_post",
           "ffn_norm_pre", "w_gate", "w_up", "w_down", "ffn_norm_post")


def _params(*semantics):
    return pltpu.CompilerParams(dimension_semantics=semantics, vmem_limit_bytes=VMEM_LIMIT_BYTES)


def _tile(dim, cap):
    if dim <= cap:
        return dim
    for t in range(cap - cap % LANES, 0, -LANES):
        if dim % t == 0:
            return t
    raise ValueError(f"no tile for {dim} under {cap}")


def _row_tile(rows, row_bytes, budget=1 << 20):
    if rows * row_bytes <= budget:
        return rows
    cap = max(16, budget // row_bytes)
    for t in range(cap - cap % 16, 0, -16):
        if rows % t == 0:
            return t
    return rows


def _rows(ts, width, cidx=0):
    return pl.BlockSpec((ts, width), lambda i: (i, cidx))


def _fixed(shape):
    return pl.BlockSpec(shape, lambda *_: (0,) * len(shape))


def _sigmoid(x):
    return 1.0 / (1.0 + jnp.exp(-x))


def _matmul(a, b, mode, out_dtype, name, add=None, blocked=False, ride=None):
    nb = n_blk = 0
    blocked = blocked or b.ndim == 3
    if mode == "nn":
        (m, k) = a.shape
        n = b.shape[0] * b.shape[2] if blocked else b.shape[1]
    elif mode == "nt":
        (m, k) = a.shape
        n = b.shape[1] if blocked else b.shape[0]
    else:
        (k, m), n = a.shape, b.shape[1]
    if blocked:
        nb = b.shape[2] if mode != "tn" else n // N_DEV
    unit = nb if blocked and mode != "nt" else LANES
    out_bytes = jnp.dtype(out_dtype).itemsize + (4 if add is not None else 0)
    best = None
    for tn_c in range(unit, min(n, 1536) + 1, unit):
        for tm_c in sorted({256, 512, 1024, 2048, min(m, 2048)}):
            if n % tn_c or m % tm_c or (blocked and mode != "nt" and N_DEV % (tn_c // nb)):
                continue
            vmem = 2 * (tm_c * k * 2 + tn_c * k * 2 + tm_c * tn_c * out_bytes) + tm_c * tn_c * 4 + tn_c * k * 2
            if vmem <= MATMUL_VMEM_BYTES and (best is None or tm_c * tn_c / (tm_c + tn_c) > best[0]):
                best = (tm_c * tn_c / (tm_c + tn_c), tm_c, tn_c)
    if best is None:
        raise ValueError(f"{name}: no tiles for {m}x{n}x{k}")
    _, tm, tn = best
    if blocked:
        n_blk = N_DEV if mode == "nt" else tn // nb
    dims = {"nn": ((1,), (0,)), "nt": ((1,), (1,)), "tn": ((0,), (0,))}[mode]
    a_spec = pl.BlockSpec((k, tm), lambda i, j: (0, i)) if mode == "tn" else pl.BlockSpec((tm, k), lambda i, j: (i, 0))
    b_spec = pl.BlockSpec((tn, k), lambda i, j: (j, 0)) if mode == "nt" else pl.BlockSpec((k, tn), lambda i, j: (0, j))
    o_spec = pl.BlockSpec((tm, tn), lambda i, j: (i, j))
    out_shape = jax.ShapeDtypeStruct((m, n), out_dtype)
    if blocked and mode == "nn":
        b_spec = pl.BlockSpec((n_blk, k, nb), lambda i, j: (j, 0, 0))
    elif blocked and mode == "nt":
        b_spec = pl.BlockSpec((n_blk, tn, nb), lambda i, j: (0, j, 0))
    elif blocked:
        o_spec = pl.BlockSpec((n_blk, tm, nb), lambda i, j: (j, i, 0))
        out_shape = jax.ShapeDtypeStruct((N_DEV, m, nb), out_dtype)
    has_add = add is not None
    grid = (m // tm, n // tn)

    def body(*refs):
        (a_ref, b_ref, *rest), start, finish = _ride_hooks(ride, refs, 3 if has_add else 2, 1, grid)
        start()
        o_ref = rest[-1]
        if blocked and mode != "tn":
            bv = jnp.concatenate([b_ref[c] for c in range(n_blk)], axis=1) if n_blk > 1 else b_ref[0]
        else:
            bv = b_ref[...]
        total = lax.dot_general(a_ref[...], bv, (dims, ((), ())), preferred_element_type=F32)
        if has_add:
            total = total + rest[0][...]
        if blocked and mode == "tn":
            for c in range(n_blk):
                o_ref[c] = total[:, c * nb:(c + 1) * nb].astype(o_ref.dtype)
        else:
            o_ref[...] = total.astype(o_ref.dtype)
        finish()

    operands = (a, b, add) if has_add else (a, b)
    (out,), rode = _ride_call(ride, body, name, (out_shape,), grid, [a_spec, b_spec] + ([o_spec] if has_add else []),
                              (o_spec,), ("parallel", "parallel"), operands)
    return out if ride is None else (out, rode)


def _rms_fwd(x, win, gain, out_dtype, name, res=None, then=None):
    width, cidx = win
    s = x.shape[0]
    ts = min(s, 512)
    has_res, has_then = res is not None, then is not None

    def norm(v, g_ref):
        return (v * lax.rsqrt(jnp.mean(v * v, axis=-1, keepdims=True) + EPS)) * g_ref[...]

    def body(x_ref, g_ref, *rest):
        y = norm(x_ref[...].astype(F32), g_ref)
        if has_res:
            y = rest[0][...] + y
        o_ref = rest[-2] if has_then else rest[-1]
        o_ref[...] = y.astype(o_ref.dtype)
        if has_then:
            rest[-1][...] = norm(y, rest[-3]).astype(BF16)

    ops = (x, gain.reshape(1, width)) + ((res,) if has_res else ()) + ((then.reshape(1, width),) if has_then else ())
    out_shape = (jax.ShapeDtypeStruct((s, width), out_dtype),) + ((jax.ShapeDtypeStruct((s, width), BF16),) * has_then)
    out = pl.pallas_call(
        body, name=name, out_shape=out_shape, grid=(s // ts,),
        in_specs=([_rows(ts, width, cidx), _fixed((1, width))] + ([_rows(ts, width)] if has_res else [])
                  + ([_fixed((1, width))] if has_then else [])),
        out_specs=(_rows(ts, width),) * len(out_shape), compiler_params=_params("parallel"))(*ops)
    return out if has_then else out[0]


def _into(dz, n_inputs, out_index):
    return dict(in_specs=[ANY], operands=(dz,), input_output_aliases={n_inputs: out_index},
                out_shape=jax.ShapeDtypeStruct(dz.shape, dz.dtype))


def _rms_bwd(x, win, gain, dy, out_dtype, name, add=None, dz=None):
    width, cidx = win
    s = x.shape[0]
    ts = min(s, 512)
    has_add = add is not None

    def body(x_ref, g_ref, dy_ref, *rest):
        dx_ref, dg_ref = rest[-2], rest[-1]
        xv = x_ref[...].astype(F32)
        r = lax.rsqrt(jnp.mean(xv * xv, axis=-1, keepdims=True) + EPS)
        xh = xv * r
        dyv = dy_ref[...].astype(F32)
        dyg = dyv * g_ref[...]
        dx = r * (dyg - xh * jnp.mean(dyg * xh, axis=-1, keepdims=True))
        if has_add:
            dx = dx + rest[0][...]
        dx_ref[...] = dx.astype(dx_ref.dtype)

        @pl.when(pl.program_id(0) == 0)
        def _():
            dg_ref[...] = jnp.zeros_like(dg_ref)

        dg_ref[...] += jnp.sum(dyv * xh, axis=0, keepdims=True)

    ops = (x, gain.reshape(1, width), dy) + ((add,) if has_add else ())
    in_specs = [_rows(ts, width, cidx), _fixed((1, width)), _rows(ts, width)] + ([_rows(ts, width)] if has_add else [])
    dx_shape, dx_spec, alias = jax.ShapeDtypeStruct((s, width), out_dtype), _rows(ts, width), {}
    if dz is not None:
        into = _into(dz, len(ops), 0)
        ops, in_specs, alias = ops + into["operands"], in_specs + into["in_specs"], into["input_output_aliases"]
        dx_shape, dx_spec = into["out_shape"], _rows(ts, width, cidx)
    dx, dg = pl.pallas_call(
        body, name=name, out_shape=(dx_shape, jax.ShapeDtypeStruct((1, width), F32)), grid=(s // ts,),
        in_specs=in_specs, out_specs=(dx_spec, _fixed((1, width))), input_output_aliases=alias,
        compiler_params=_params("arbitrary"))(*ops)
    return dx, dg.reshape(width)


def _rope(x, c, s1, s2):
    return x * c + pltpu.roll(x, 16, 1) * s1 + pltpu.roll(x, LANES - 16, 1) * s2


def _rope_t(g, c, s1, s2):
    return g * c + pltpu.roll(g * s1, LANES - 16, 1) + pltpu.roll(g * s2, 16, 1)


def _rope_tables(positions):
    inv_freq = ROPE_THETA ** (-jnp.arange(0, QK_ROPE, 2, dtype=F32) / QK_ROPE)
    ang = positions.astype(F32)[:, None] * inv_freq
    cos, sin = jnp.cos(ang), jnp.sin(ang)
    n = positions.shape[0]
    one, zero = jnp.ones((n, 1), F32), jnp.zeros((n, 1), F32)
    c = jnp.concatenate([jnp.tile(one, (1, QK_NOPE)), cos, cos, jnp.tile(one, (1, 32))], axis=1)
    s1 = jnp.concatenate([jnp.tile(zero, (1, QK_NOPE + 16)), sin, jnp.tile(zero, (1, 32))], axis=1)
    s2 = jnp.concatenate([jnp.tile(zero, (1, QK_NOPE)), -sin, jnp.tile(zero, (1, 48))], axis=1)
    return c, s1, s2


def _qkv_up_fwd(z, q_gain, kv_gain, w_uq, w_uk, w_uv, tables, name):
    s = z.shape[0]
    ts = min(s, 512)
    hw = N_HEADS * HEAD_PAD
    kv0 = Q_RANK + LANES

    def norm(v, g_ref):
        return ((v * lax.rsqrt(jnp.mean(v * v, axis=-1, keepdims=True) + EPS)) * g_ref[...]).astype(BF16)

    def body(z_ref, gq_ref, gkv_ref, wq_ref, wk_ref, wv_ref, c_ref, s1_ref, s2_ref, cq_ref, ckv_ref, q_ref, k_ref, v_ref):
        c, s1, s2 = c_ref[...], s1_ref[...], s2_ref[...]
        cqv = norm(z_ref[:, pl.ds(0, Q_RANK)].astype(F32), gq_ref)
        ckvv = norm(z_ref[:, pl.ds(kv0, KV_RANK)].astype(F32), gkv_ref)
        cq_ref[...] = cqv
        ckv_ref[...] = ckvv
        kr = _rope(z_ref[:, pl.ds(Q_RANK, LANES)].astype(F32), c, s1, s2)
        for h in range(N_HEADS):
            sl = slice(h * HEAD_PAD, (h + 1) * HEAD_PAD)
            q_ref[:, sl] = _rope(jnp.dot(cqv, wq_ref[h], preferred_element_type=F32), c, s1, s2).astype(BF16)
            k_ref[:, sl] = (jnp.dot(ckvv, wk_ref[h], preferred_element_type=F32) + kr).astype(BF16)
            v_ref[:, sl] = jnp.dot(ckvv, wv_ref[h], preferred_element_type=F32).astype(BF16)

    tab = _rows(ts, LANES)
    wide = jax.ShapeDtypeStruct((s, hw), BF16)
    return pl.pallas_call(
        body, name=name,
        out_shape=(jax.ShapeDtypeStruct((s, Q_RANK), BF16), jax.ShapeDtypeStruct((s, KV_RANK), BF16), wide, wide, wide),
        grid=(s // ts,),
        in_specs=[_rows(ts, *ZC_QKR), _fixed((1, Q_RANK)), _fixed((1, KV_RANK)), _fixed(w_uq.shape), _fixed(w_uk.shape),
                  _fixed(w_uv.shape), tab, tab, tab],
        out_specs=(_rows(ts, Q_RANK), _rows(ts, KV_RANK)) + (_rows(ts, hw),) * 3, compiler_params=_params("parallel"))(
            z, q_gain.reshape(1, -1), kv_gain.reshape(1, -1), w_uq, w_uk, w_uv, *tables)


def _qkv_up_bwd(dq, dk, dv, z, w_uq, w_uk, w_uv, tables, q_gain, kv_gain, dz, name):
    s = dq.shape[0]
    ts = min(s, 512)
    hw = N_HEADS * HEAD_PAD
    zw = ZC_QKR[0]
    kv0 = Q_RANK + LANES
    dims_nt = (((1,), (1,)), ((), ()))

    def norm_bwd(xv, g_ref, dyv):
        r = lax.rsqrt(jnp.mean(xv * xv, axis=-1, keepdims=True) + EPS)
        xh = xv * r
        dyg = dyv * g_ref[...]
        return r * (dyg - xh * jnp.mean(dyg * xh, axis=-1, keepdims=True)), jnp.sum(dyv * xh, axis=0, keepdims=True)

    def body(dq_ref, dk_ref, dv_ref, z_ref, wq_ref, wk_ref, wv_ref, c_ref, s1_ref, s2_ref, gq_ref, gkv_ref, _,
             dqf_ref, dkf_ref, dz_ref, dgq_ref, dgkv_ref):
        c, s1, s2 = c_ref[...], s1_ref[...], s2_ref[...]
        ksum = jnp.zeros((ts, HEAD_PAD), F32)
        dcq = jnp.zeros((ts, Q_RANK), F32)
        dckv = jnp.zeros((ts, KV_RANK), F32)
        for h in range(N_HEADS):
            sl = slice(h * HEAD_PAD, (h + 1) * HEAD_PAD)
            dqh = _rope_t(dq_ref[:, sl], c, s1, s2).astype(BF16)
            dkv = dk_ref[:, sl]
            dkh = dkv.astype(BF16)
            dqf_ref[:, sl] = dqh
            dkf_ref[:, sl] = dkh
            ksum = ksum + dkv
            dcq = dcq + lax.dot_general(dqh, wq_ref[h], dims_nt, preferred_element_type=F32)
            dckv = dckv + (lax.dot_general(dkh, wk_ref[h], dims_nt, preferred_element_type=F32)
                           + lax.dot_general(dv_ref[:, sl], wv_ref[h], dims_nt, preferred_element_type=F32))
        dxq, dgq = norm_bwd(z_ref[:, pl.ds(0, Q_RANK)].astype(F32), gq_ref, dcq)
        dxkv, dgkv = norm_bwd(z_ref[:, pl.ds(kv0, KV_RANK)].astype(F32), gkv_ref, dckv)
        lane = lax.broadcasted_iota(jnp.int32, (ts, HEAD_PAD), 1)
        in_rope = (lane >= QK_NOPE) & (lane < QK_NOPE + QK_ROPE)
        dz_ref[:, pl.ds(0, Q_RANK)] = dxq.astype(BF16)
        dz_ref[:, pl.ds(Q_RANK, LANES)] = jnp.where(in_rope, _rope_t(ksum, c, s1, s2), 0.0).astype(BF16)
        dz_ref[:, pl.ds(kv0, KV_RANK)] = dxkv.astype(BF16)

        @pl.when(pl.program_id(0) == 0)
        def _():
            dgq_ref[...] = jnp.zeros_like(dgq_ref)
            dgkv_ref[...] = jnp.zeros_like(dgkv_ref)

        dgq_ref[...] += dgq
        dgkv_ref[...] += dgkv

    tab = _rows(ts, LANES)
    into = _into(dz, 12, 2)
    dqf, dkf, dz, dgq, dgkv = pl.pallas_call(
        body, name=name,
        out_shape=(jax.ShapeDtypeStruct((s, hw), BF16), jax.ShapeDtypeStruct((s, hw), BF16), into["out_shape"],
                   jax.ShapeDtypeStruct((1, Q_RANK), F32), jax.ShapeDtypeStruct((1, KV_RANK), F32)),
        grid=(s // ts,),
        in_specs=[_rows(ts, hw), _rows(ts, hw), _rows(ts, hw), _rows(ts, *ZC_QKR), _fixed(w_uq.shape), _fixed(w_uk.shape),
                  _fixed(w_uv.shape), tab, tab, tab, _fixed((1, Q_RANK)), _fixed((1, KV_RANK))] + into["in_specs"],
        out_specs=(_rows(ts, hw), _rows(ts, hw), _rows(ts, *ZC_QKR), _fixed((1, Q_RANK)), _fixed((1, KV_RANK))),
        input_output_aliases=into["input_output_aliases"], compiler_params=_params("arbitrary"))(
            dq, dk, dv, z, w_uq, w_uk, w_uv, *tables, q_gain.reshape(1, -1), kv_gain.reshape(1, -1), dz)
    return dqf, dkf, dz, dgq.reshape(-1), dgkv.reshape(-1)


def _attn_tile(s):
    return min(s, 512)


def _raw_scores(q, k, masked, row0=0):
    sc = lax.dot_general(q, k, (((1,), (1,)), ((), ())), preferred_element_type=F32)
    if masked:
        rows = row0 + lax.broadcasted_iota(jnp.int32, sc.shape, 0)
        cols = lax.broadcasted_iota(jnp.int32, sc.shape, 1)
        sc = jnp.where(cols <= rows, sc, -jnp.inf)
    return sc


def _ride_hooks(ride, refs, n_in, n_out, grid):
    if ride is None:
        return refs, lambda: None, lambda: None
    n = len(ride.arrays)
    own = refs[:n_in] + refs[n_in + n:n_in + n + n_out]
    ins, outs, sems = refs[n_in:n_in + n], refs[n_in + n + n_out:n_in + 2 * n + n_out], refs[n_in + 2 * n + n_out:]
    at_first = functools.reduce(lambda a, b: a & b, [pl.program_id(ax) == 0 for ax in range(len(grid))])
    at_last = functools.reduce(lambda a, b: a & b, [pl.program_id(ax) == g - 1 for ax, g in enumerate(grid)])
    return own, lambda: pl.when(at_first)(lambda: ride.start(ins, outs, sems)), \
        lambda: pl.when(at_last)(lambda: ride.finish(ins, outs, sems))


def _ride_call(ride, body, name, out_shape, grid, in_specs, out_specs, semantics, operands):
    n = 0 if ride is None else len(ride.arrays)
    res = pl.pallas_call(
        body, name=name, out_shape=tuple(out_shape) + (tuple(ride.out_shape) if n else ()), grid=grid,
        in_specs=list(in_specs) + [ANY] * n, out_specs=tuple(out_specs) + (ANY,) * n,
        scratch_shapes=list(ride.scratch) if n else [],
        compiler_params=_params(*(("arbitrary",) * len(grid) if n else semantics)))(*operands, *(ride.arrays if n else ()))
    return res[:len(out_shape)], list(res[len(out_shape):])


def _flash_fwd(q, k, v, name, ride=None):
    s = q.shape[0]
    t = _attn_tile(s)
    c2 = ATTN_SCALE * LOG2E
    grid = (N_HEADS, s // t)

    def body(*refs):
        (q_ref, k_ref, v_ref, o_ref, lse_ref), start, finish = _ride_hooks(ride, refs, 3, 2, grid)
        start()
        i = pl.program_id(1)
        qv = q_ref[...]

        def chunk(j, carry, masked):
            m_old, l_old, acc = carry
            at = pl.ds(pl.multiple_of(j * t, t), t)
            sc = _raw_scores(qv, k_ref[at, :], masked)
            m_new = jnp.maximum(m_old, jnp.max(sc, axis=-1, keepdims=True))
            p = jnp.exp2((sc - m_new) * c2)
            alpha = jnp.exp2((m_old - m_new) * c2)
            l_new = alpha * l_old + jnp.sum(p, axis=-1, keepdims=True)
            acc = alpha * acc + jnp.dot(p.astype(BF16), v_ref[at, :], preferred_element_type=F32)
            return m_new, l_new, acc

        init = (jnp.full((t, 1), -jnp.inf, F32), jnp.zeros((t, 1), F32), jnp.zeros((t, HEAD_PAD), F32))
        carry = lax.fori_loop(0, i, lambda j, cr: chunk(j, cr, False), init)
        m_fin, l_fin, acc = chunk(i, carry, True)
        o_ref[...] = (acc / l_fin).astype(o_ref.dtype)
        lse_ref[...] = jnp.broadcast_to(m_fin * ATTN_SCALE + jnp.log(l_fin), (t, HEAD_PAD))
        finish()

    qo = pl.BlockSpec((t, HEAD_PAD), lambda h, i: (i, h))
    whole = pl.BlockSpec((s, HEAD_PAD), lambda h, i: (0, h))
    return _ride_call(
        ride, body, name, (jax.ShapeDtypeStruct(q.shape, BF16), jax.ShapeDtypeStruct(q.shape, F32)), grid,
        [qo, whole, whole], (qo, qo), ("parallel", "parallel"), (q, k, v))


def _attn_out_bwd(dya, w_attn_o, o, name):
    s, d = dya.shape
    hw = N_HEADS * HEAD_PAD
    t = _attn_tile(s)

    def body(d_ref, w_ref, o_ref, delta_ref, dob_ref):
        wv = jnp.concatenate([w_ref[c] for c in range(N_DEV)], axis=1)
        do = lax.dot_general(d_ref[...], wv, (((1,), (1,)), ((), ())), preferred_element_type=F32)
        for h in range(N_HEADS):
            sl = slice(h * HEAD_PAD, (h + 1) * HEAD_PAD)
            dov = do[:, sl]
            delta_ref[:, sl] = jnp.broadcast_to(jnp.sum(dov * o_ref[:, sl].astype(F32), axis=-1, keepdims=True),
                                                (t, HEAD_PAD))
            dob_ref[:, sl] = dov.astype(BF16)

    blk = _rows(t, hw)
    return pl.pallas_call(
        body, name=name, out_shape=(jax.ShapeDtypeStruct(o.shape, F32), jax.ShapeDtypeStruct(o.shape, BF16)),
        grid=(s // t,), in_specs=[_rows(t, d), _fixed(w_attn_o.shape), blk], out_specs=(blk, blk),
        compiler_params=_params("parallel"))(dya, w_attn_o, o)


def _flash_bwd(q, k, v, do, lse, delta, name, ride=None):
    s = q.shape[0]
    t = _attn_tile(s)
    nt = s // t
    c2 = ATTN_SCALE * LOG2E
    grid = (N_HEADS, nt)

    def body(*refs):
        (q_ref, k_ref, v_ref, do_ref, lse_ref, delta_ref, dq_ref, dk_ref, dv_ref), start, finish = _ride_hooks(
            ride, refs, 6, 3, grid)
        start()
        j = pl.program_id(1)
        kv, vv = k_ref[...], v_ref[...]

        @pl.when(j == 0)
        def _():
            dq_ref[...] = jnp.zeros_like(dq_ref)

        def chunk(i, carry, masked):
            dk_acc, dv_acc = carry
            at = pl.ds(pl.multiple_of(i * t, t), t)
            qi, doi = q_ref[at, :], do_ref[at, :]
            sc = _raw_scores(qi, kv, masked)
            p = jnp.exp2(sc * c2 - lse_ref[at, pl.ds(0, 1)] * LOG2E)
            dp = lax.dot_general(doi, vv, (((1,), (1,)), ((), ())), preferred_element_type=F32)
            ds = (p * (dp - delta_ref[at, pl.ds(0, 1)])).astype(BF16)
            dv_acc = dv_acc + lax.dot_general(p.astype(BF16), doi, (((0,), (0,)), ((), ())), preferred_element_type=F32)
            dk_acc = dk_acc + lax.dot_general(ds, qi, (((0,), (0,)), ((), ())), preferred_element_type=F32)
            dq_ref[at, :] += jnp.dot(ds, kv, preferred_element_type=F32) * ATTN_SCALE
            return dk_acc, dv_acc

        zero = jnp.zeros((t, HEAD_PAD), F32)
        carry = chunk(j, (zero, zero), True)
        dk_acc, dv_acc = lax.fori_loop(j + 1, nt, lambda i, cr: chunk(i, cr, False), carry)
        dk_ref[...] = dk_acc * ATTN_SCALE
        dv_ref[...] = dv_acc.astype(BF16)
        finish()

    blk = pl.BlockSpec((t, HEAD_PAD), lambda h, j: (j, h))
    whole = pl.BlockSpec((s, HEAD_PAD), lambda h, j: (0, h))
    return _ride_call(
        ride, body, name, (jax.ShapeDtypeStruct(q.shape, F32), jax.ShapeDtypeStruct(q.shape, F32),
                           jax.ShapeDtypeStruct(q.shape, BF16)), grid,
        [whole, blk, blk, whole, whole, whole], (whole, blk, blk), ("parallel", "arbitrary"), (q, k, v, do, lse, delta))


def _conv_tile(s):
    return min(s, 256)


def _halo_before(t, width, cidx):
    per = t // CONV_HALO
    return pl.BlockSpec((CONV_HALO, width), lambda i: (jnp.maximum(i * per - 1, 0), cidx))


def _halo_after(t, width, cidx, n_tiles):
    per = t // CONV_HALO
    last = n_tiles * per - 1
    return pl.BlockSpec((CONV_HALO, width), lambda i: (jnp.minimum((i + 1) * per, last), cidx))


def _fill_glu(hbuf, ap_ref, gp_ref, a_ref, g_ref, t):
    first = pl.program_id(0) == 0
    hbuf[pl.ds(0, CONV_HALO), :] = jnp.where(first, 0.0, ap_ref[...].astype(F32) * _sigmoid(gp_ref[...].astype(F32)))
    hbuf[pl.ds(CONV_HALO, t), :] = a_ref[...].astype(F32) * _sigmoid(g_ref[...].astype(F32))


def _phase_copies(dst, src, t):
    n = t + CONV_HALO - SUBLANES
    for s in range(1, SUBLANES):
        dst[s, pl.ds(0, n), :] = src[pl.ds(s, n), :]


def _window(phases, src, k, t):
    if k % SUBLANES == 0:
        return src[pl.ds(k, t), :]
    return phases[k % SUBLANES, pl.ds(k - k % SUBLANES, t), :]


def _layer_norm_parts(co):
    mu = jnp.mean(co, axis=-1, keepdims=True)
    xc = co - mu
    rstd = lax.rsqrt(jnp.mean(xc * xc, axis=-1, keepdims=True) + EPS)
    return xc * rstd, rstd


def _conv_fwd(z, conv_w, conv_b, ln_g, ln_b, name):
    s = z.shape[0]
    t = _conv_tile(s)
    off = CONV_HALO - (CONV_W - 1)

    def body(ap_ref, gp_ref, a_ref, g_ref, w_ref, b_ref, lg_ref, lb_ref, hc_ref, co_ref, hbuf, hph):
        _fill_glu(hbuf, ap_ref, gp_ref, a_ref, g_ref, t)
        _phase_copies(hph, hbuf, t)
        acc = jnp.zeros((t, CONV_C), F32) + b_ref[...]
        for j in range(CONV_W):
            acc = acc + _window(hph, hbuf, off + j, t) * w_ref[pl.ds(j, 1), :]
        co_ref[...] = acc
        xh, _ = _layer_norm_parts(acc)
        y = xh * lg_ref[...] + lb_ref[...]
        hc_ref[...] = (y * _sigmoid(y)).astype(BF16)

    vec = _fixed((1, CONV_C))
    return pl.pallas_call(
        body, name=name, out_shape=(jax.ShapeDtypeStruct((s, CONV_C), BF16), jax.ShapeDtypeStruct((s, CONV_C), F32)),
        grid=(s // t,),
        in_specs=[_halo_before(t, *ZC_CONV_A), _halo_before(t, *ZC_CONV_G), _rows(t, *ZC_CONV_A), _rows(t, *ZC_CONV_G),
                  _fixed((CONV_HALO, CONV_C)), vec, vec, vec],
        out_specs=(_rows(t, CONV_C), _rows(t, CONV_C)),
        scratch_shapes=[pltpu.VMEM((t + CONV_HALO, CONV_C), F32), pltpu.VMEM((SUBLANES, t + CONV_HALO, CONV_C), F32)],
        compiler_params=_params("parallel"))(z, z, z, z, conv_w, conv_b.reshape(1, -1), ln_g.reshape(1, -1),
                                             ln_b.reshape(1, -1))


def _conv_bwd_norm(dyc, w_conv_o, co, ln_g, ln_b, name):
    s = co.shape[0]
    t = min(s, 512)

    def body(dyc_ref, w_ref, co_ref, lg_ref, lb_ref, dco_ref, dg_ref, db_ref, dcb_ref):
        wv = jnp.concatenate([w_ref[c] for c in range(N_DEV)], axis=1)
        dhc = lax.dot_general(dyc_ref[...], wv, (((1,), (1,)), ((), ())), preferred_element_type=F32)
        xh, rstd = _layer_norm_parts(co_ref[...])
        y = xh * lg_ref[...] + lb_ref[...]
        sg = _sigmoid(y)
        dy = dhc * (sg * (1.0 + y * (1.0 - sg)))
        dxh = dy * lg_ref[...]
        dco = rstd * (dxh - jnp.mean(dxh, axis=-1, keepdims=True) - xh * jnp.mean(dxh * xh, axis=-1, keepdims=True))
        dco_ref[...] = dco

        @pl.when(pl.program_id(0) == 0)
        def _():
            dg_ref[...] = jnp.zeros_like(dg_ref)
            db_ref[...] = jnp.zeros_like(db_ref)
            dcb_ref[...] = jnp.zeros_like(dcb_ref)

        dg_ref[...] += jnp.sum(dy * xh, axis=0, keepdims=True)
        db_ref[...] += jnp.sum(dy, axis=0, keepdims=True)
        dcb_ref[...] += jnp.sum(dco, axis=0, keepdims=True)

    vec = _fixed((1, CONV_C))
    one = jax.ShapeDtypeStruct((1, CONV_C), F32)
    dco, dg, db, dcb = pl.pallas_call(
        body, name=name, out_shape=(jax.ShapeDtypeStruct((s, CONV_C), F32), one, one, one), grid=(s // t,),
        in_specs=[_rows(t, D_MODEL), _fixed(w_conv_o.shape), _rows(t, CONV_C), vec, vec],
        out_specs=(_rows(t, CONV_C), vec, vec, vec),
        compiler_params=_params("arbitrary"))(dyc, w_conv_o, co, ln_g.reshape(1, -1), ln_b.reshape(1, -1))
    return dco, dg.reshape(-1), db.reshape(-1), dcb.reshape(-1)


def _conv_bwd_taps(dco, z, conv_w, dz, name):
    s = z.shape[0]
    t = _conv_tile(s)
    nt = s // t
    off = CONV_HALO - (CONV_W - 1)

    def body(ap_ref, gp_ref, a_ref, g_ref, d_ref, dn_ref, w_ref, _, du_ref, dw_ref, hbuf, dbuf, hph, dph):
        i = pl.program_id(0)
        _fill_glu(hbuf, ap_ref, gp_ref, a_ref, g_ref, t)
        dbuf[pl.ds(0, t), :] = d_ref[...]
        dbuf[pl.ds(t, CONV_HALO), :] = jnp.where(i == nt - 1, 0.0, dn_ref[...])
        _phase_copies(hph, hbuf, t)
        _phase_copies(dph, dbuf, t)

        @pl.when(i == 0)
        def _():
            dw_ref[...] = jnp.zeros_like(dw_ref)

        dcur = d_ref[...]
        dh = jnp.zeros((t, CONV_C), F32)
        for j in range(CONV_W):
            dh = dh + _window(dph, dbuf, CONV_W - 1 - j, t) * w_ref[pl.ds(j, 1), :]
            dw_ref[pl.ds(j, 1), :] += jnp.sum(dcur * _window(hph, hbuf, off + j, t), axis=0, keepdims=True)
        a, sg = a_ref[...].astype(F32), _sigmoid(g_ref[...].astype(F32))
        du_ref[:, pl.ds(0, CONV_C)] = (dh * sg).astype(BF16)
        du_ref[:, pl.ds(CONV_C, CONV_C)] = (dh * a * sg * (1.0 - sg)).astype(BF16)

    into = _into(dz, 7, 0)
    return pl.pallas_call(
        body, name=name, out_shape=(into["out_shape"], jax.ShapeDtypeStruct((CONV_HALO, CONV_C), F32)), grid=(nt,),
        in_specs=[_halo_before(t, *ZC_CONV_A), _halo_before(t, *ZC_CONV_G), _rows(t, *ZC_CONV_A), _rows(t, *ZC_CONV_G),
                  _rows(t, CONV_C), _halo_after(t, CONV_C, 0, nt), _fixed((CONV_HALO, CONV_C))] + into["in_specs"],
        out_specs=(_rows(t, *ZC_CONV), _fixed((CONV_HALO, CONV_C))), input_output_aliases=into["input_output_aliases"],
        scratch_shapes=[pltpu.VMEM((t + CONV_HALO, CONV_C), F32), pltpu.VMEM((t + CONV_HALO, CONV_C), F32),
                        pltpu.VMEM((SUBLANES, t + CONV_HALO, CONV_C), F32),
                        pltpu.VMEM((SUBLANES, t + CONV_HALO, CONV_C), F32)],
        compiler_params=_params("arbitrary"))(z, z, z, z, dco, dco, conv_w, dz)


def _pool_tile(s):
    return min(s, 512)


def _pool_counts(row0, n, window):
    rows = row0 + lax.broadcasted_iota(jnp.int32, (n, POOL_GD), 0)
    return jnp.minimum(rows + 1, window).astype(F32)


def _pool_diff(ubuf, gi, window, row0, t):
    lanes = pl.ds(gi * POOL_GD, POOL_GD)
    tot = ubuf[pl.ds(CONV_HALO, t), lanes]
    cur = tot
    for back in range(1, window):
        tot = tot + ubuf[pl.ds(CONV_HALO - back, t), lanes]
    return tot / _pool_counts(row0, t, window) - cur


def _pool_fwd(z, pool_w, pool_scale, name):
    s = z.shape[0]
    t = _pool_tile(s)

    def body(up_ref, u_ref, w_ref, sc_ref, m_ref, ubuf):
        i = pl.program_id(0)
        ubuf[pl.ds(0, CONV_HALO), :] = jnp.where(i == 0, 0.0, up_ref[...].astype(F32))
        ubuf[pl.ds(CONV_HALO, t), :] = u_ref[...].astype(F32)
        for gi, window in enumerate(POOL_WINDOWS):
            d = _pool_diff(ubuf, gi, window, i * t, t)
            mm = jnp.dot(d.astype(BF16), w_ref[gi].astype(BF16), preferred_element_type=F32)
            lanes = pl.ds(gi * POOL_GD, POOL_GD)
            m_ref[:, lanes] = (mm * sc_ref[:, lanes]).astype(BF16)

    return pl.pallas_call(
        body, name=name, out_shape=jax.ShapeDtypeStruct((s, POOL_C), BF16), grid=(s // t,),
        in_specs=[_halo_before(t, *ZC_POOL), _rows(t, *ZC_POOL), _fixed((POOL_G, POOL_GD, POOL_GD)), _fixed((1, POOL_C))],
        out_specs=_rows(t, POOL_C), scratch_shapes=[pltpu.VMEM((t + CONV_HALO, POOL_C), F32)],
        compiler_params=_params("parallel"))(z, z, pool_w, pool_scale.reshape(1, -1))


def _pool_bwd(dm, z, pool_w, pool_scale, dz, name):
    s = z.shape[0]
    t = _pool_tile(s)
    nt = s // t

    def body(up_ref, u_ref, dm_ref, dmn_ref, w_ref, sc_ref, _, du_ref, dw_ref, dsc_ref, ubuf, ebuf):
        i = pl.program_id(0)
        ubuf[pl.ds(0, CONV_HALO), :] = jnp.where(i == 0, 0.0, up_ref[...].astype(F32))
        ubuf[pl.ds(CONV_HALO, t), :] = u_ref[...].astype(F32)

        @pl.when(i == 0)
        def _():
            dw_ref[...] = jnp.zeros_like(dw_ref)
            dsc_ref[...] = jnp.zeros_like(dsc_ref)

        dm_next = jnp.where(i == nt - 1, 0.0, dmn_ref[...])
        for gi, window in enumerate(POOL_WINDOWS):
            lanes = pl.ds(gi * POOL_GD, POOL_GD)
            wb = w_ref[gi].astype(BF16)
            scale = sc_ref[:, lanes]
            d = _pool_diff(ubuf, gi, window, i * t, t).astype(BF16)
            mm = jnp.dot(d, wb, preferred_element_type=F32)
            dmv = dm_ref[:, lanes]
            dsc_ref[:, lanes] += jnp.sum(dmv * mm, axis=0, keepdims=True)
            dmm = (dmv * scale).astype(BF16)
            dw_ref[gi] += lax.dot_general(d, dmm, (((0,), (0,)), ((), ())), preferred_element_type=F32)
            dd = lax.dot_general(dmm, wb, (((1,), (1,)), ((), ())), preferred_element_type=F32)
            dd_next = lax.dot_general((dm_next[:, gi * POOL_GD:(gi + 1) * POOL_GD] * scale).astype(BF16), wb,
                                      (((1,), (1,)), ((), ())), preferred_element_type=F32)
            ebuf[pl.ds(0, t), lanes] = dd / _pool_counts(i * t, t, window)
            ebuf[pl.ds(t, CONV_HALO), lanes] = dd_next / _pool_counts((i + 1) * t, CONV_HALO, window)
            du = -dd
            for ahead in range(window):
                du = du + ebuf[pl.ds(ahead, t), lanes]
            du_ref[:, lanes] = du.astype(BF16)

    into = _into(dz, 6, 0)
    du, dw, dsc = pl.pallas_call(
        body, name=name,
        out_shape=(into["out_shape"], jax.ShapeDtypeStruct((POOL_G, POOL_GD, POOL_GD), F32),
                   jax.ShapeDtypeStruct((1, POOL_C), F32)), grid=(nt,),
        in_specs=[_halo_before(t, *ZC_POOL), _rows(t, *ZC_POOL), _rows(t, POOL_C), _halo_after(t, POOL_C, 0, nt),
                  _fixed((POOL_G, POOL_GD, POOL_GD)), _fixed((1, POOL_C))] + into["in_specs"],
        out_specs=(_rows(t, *ZC_POOL), _fixed((POOL_G, POOL_GD, POOL_GD)), _fixed((1, POOL_C))),
        input_output_aliases=into["input_output_aliases"],
        scratch_shapes=[pltpu.VMEM((t + CONV_HALO, POOL_C), F32), pltpu.VMEM((t + CONV_HALO, POOL_C), F32)],
        compiler_params=_params("arbitrary"))(z, z, dm, dm, pool_w, pool_scale.reshape(1, -1), dz)
    return du, dw, dsc.reshape(-1)


def _gate_specs(ts):
    width, first = ZC_GATE
    return [_rows(ts, width, first + b) for b in range(3)]


def _branches_merge_fwd(z, acts, ws, name):
    s = z.shape[0]
    ts = min(s, 512)

    def body(g0, g1, g2, a0, a1, a2, w0, w1, w2, y0, y1, y2, m_ref):
        merged = jnp.zeros((ts, D_MODEL), F32)
        for g_ref, a_ref, w_ref, y_ref in ((g0, a0, w0, y0), (g1, a1, w1, y1), (g2, a2, w2, y2)):
            wv = jnp.concatenate([w_ref[c] for c in range(N_DEV)], axis=1)
            yb = jnp.dot(a_ref[...], wv, preferred_element_type=F32).astype(BF16)
            y_ref[...] = yb
            merged = merged + _sigmoid(g_ref[...].astype(F32)) * yb.astype(F32)
        m_ref[...] = merged.astype(BF16)

    out = jax.ShapeDtypeStruct((s, D_MODEL), BF16)
    res = pl.pallas_call(
        body, name=name, out_shape=(out,) * 4, grid=(s // ts,),
        in_specs=_gate_specs(ts) + [_rows(ts, a.shape[1]) for a in acts] + [_fixed(w.shape) for w in ws],
        out_specs=(_rows(ts, D_MODEL),) * 4, compiler_params=_params("parallel"))(z, z, z, *acts, *ws)
    return tuple(res[:3]), res[3]


def _merge_bwd(z, ys, dmo, w_mix_o, name):
    s = z.shape[0]
    ts = min(s, 256)

    def body(g0, g1, g2, y0, y1, y2, dm_ref, w_ref, dy0, dy1, dy2, dz_ref):
        dmv = lax.dot_general(dm_ref[...], w_ref[...], (((1,), (1,)), ((), ())), preferred_element_type=F32)
        for b, (g_ref, y_ref, dy_ref) in enumerate(((g0, y0, dy0), (g1, y1, dy1), (g2, y2, dy2))):
            sg = _sigmoid(g_ref[...].astype(F32))
            dy_ref[...] = (dmv * sg).astype(BF16)
            dz_ref[:, pl.ds(b * D_MODEL, D_MODEL)] = (dmv * y_ref[...].astype(F32) * sg * (1.0 - sg)).astype(BF16)

    out = jax.ShapeDtypeStruct((s, D_MODEL), BF16)
    return pl.pallas_call(
        body, name=name, out_shape=(out,) * 3 + (jax.ShapeDtypeStruct((s, Z_W), BF16),), grid=(s // ts,),
        in_specs=_gate_specs(ts) + [_rows(ts, D_MODEL)] * 4 + [_fixed(w_mix_o.shape)],
        out_specs=(_rows(ts, D_MODEL),) * 3 + (_rows(ts, *ZC_GATES),),
        compiler_params=_params("parallel"))(z, z, z, *ys, dmo, w_mix_o)


def _ffn_up_fwd(h, w_gate, w_up, name):
    s, d = h.shape
    nb = w_gate.shape[2]
    f = N_DEV * nb
    tm, n_blk = min(s, 1024), 2
    tn = n_blk * nb
    blk = pl.BlockSpec((tm, tn), lambda i, j: (i, j))
    wspec = pl.BlockSpec((n_blk, d, nb), lambda i, j: (j, 0, 0))

    def body(h_ref, wg_ref, wu_ref, hg_ref, hu_ref, act_ref):
        hv = h_ref[...]
        g = jnp.dot(hv, jnp.concatenate([wg_ref[c] for c in range(n_blk)], axis=1), preferred_element_type=F32)
        u = jnp.dot(hv, jnp.concatenate([wu_ref[c] for c in range(n_blk)], axis=1), preferred_element_type=F32)
        hg_ref[...] = g.astype(hg_ref.dtype)
        hu_ref[...] = u.astype(hu_ref.dtype)
        act_ref[...] = (g * _sigmoid(g) * u).astype(BF16)

    return pl.pallas_call(
        body, name=name,
        out_shape=(jax.ShapeDtypeStruct((s, f), BF16),) * 3,
        grid=(s // tm, f // tn), in_specs=[pl.BlockSpec((tm, d), lambda i, j: (i, 0)), wspec, wspec],
        out_specs=(blk, blk, blk), compiler_params=_params("parallel", "parallel"))(h, w_gate, w_up)


def _ffn_down_bwd(dfo, w_down, hg, hu, name):
    s, d = dfo.shape
    f = w_down.shape[0]
    tm, tn = min(s, 1024), _tile(f, 1024)
    blk = pl.BlockSpec((tm, tn), lambda i, j: (i, j))

    def body(d_ref, w_ref, g_ref, u_ref, dg_ref, du_ref):
        dact = lax.dot_general(d_ref[...], w_ref[...], (((1,), (1,)), ((), ())), preferred_element_type=F32)
        g = g_ref[...].astype(F32)
        sg = _sigmoid(g)
        dg_ref[...] = (dact * u_ref[...].astype(F32) * (sg * (1.0 + g * (1.0 - sg)))).astype(BF16)
        du_ref[...] = (dact * g * sg).astype(BF16)

    out = jax.ShapeDtypeStruct((s, f), BF16)
    return pl.pallas_call(
        body, name=name, out_shape=(out, out), grid=(s // tm, f // tn),
        in_specs=[pl.BlockSpec((tm, d), lambda i, j: (i, 0)), pl.BlockSpec((tn, d), lambda i, j: (j, 0)), blk, blk],
        out_specs=(blk, blk), compiler_params=_params("parallel", "parallel"))(dfo, w_down, hg, hu)


def _ffn_up_dx(dhg, dhu, w_gate, w_up, name):
    s, f = dhg.shape
    d = w_gate.shape[1]
    tm, tn = min(s, 1024), min(d, 256)
    dims = (((1,), (1,)), ((), ()))

    def body(g_ref, u_ref, wg_ref, wu_ref, o_ref):
        wg = jnp.concatenate([wg_ref[c] for c in range(N_DEV)], axis=1)
        wu = jnp.concatenate([wu_ref[c] for c in range(N_DEV)], axis=1)
        o_ref[...] = (lax.dot_general(g_ref[...], wg, dims, preferred_element_type=F32)
                      + lax.dot_general(u_ref[...], wu, dims, preferred_element_type=F32))

    a_spec = pl.BlockSpec((tm, f), lambda i, j: (i, 0))
    w_spec = pl.BlockSpec((N_DEV, tn, w_gate.shape[2]), lambda i, j: (0, j, 0))
    return pl.pallas_call(
        body, name=name, out_shape=jax.ShapeDtypeStruct((s, d), F32), grid=(s // tm, d // tn),
        in_specs=[a_spec, a_spec, w_spec, w_spec], out_specs=pl.BlockSpec((tm, tn), lambda i, j: (i, j)),
        compiler_params=_params("parallel", "parallel"))(dhg, dhu, w_gate, w_up)


def _loss_grad(y, target, name):
    s, d = y.shape
    ts = min(s, 512)

    def body(y_ref, t_ref, dy_ref, sq_ref):
        e = y_ref[...] - t_ref[...]
        dy_ref[...] = e / d

        @pl.when(pl.program_id(0) == 0)
        def _():
            sq_ref[...] = jnp.zeros_like(sq_ref)

        sq_ref[...] += jnp.sum(e * e, axis=0, keepdims=True)

    return pl.pallas_call(
        body, name=name, out_shape=(jax.ShapeDtypeStruct((s, d), F32), jax.ShapeDtypeStruct((1, d), F32)),
        grid=(s // ts,), in_specs=[_rows(ts, d), _rows(ts, d)], out_specs=(_rows(ts, d), _fixed((1, d))),
        compiler_params=_params("arbitrary"))(y, target)


def _adamw(w, g, m, v, name):
    shape = w.shape
    cols = shape[-1]
    keep3 = w.ndim == 3 and shape[1] < SUBLANES
    view = shape if keep3 else (math.prod(shape[:-1]), cols)
    rows = view[0]
    if keep3:
        cap = max(1, (2 << 20) // (SUBLANES * cols * 4))
        tr = max(t for t in range(1, cap + 1) if rows % t == 0)
    else:
        tr = _row_tile(rows, cols * 4, budget=2 << 20)

    def body(w_ref, g_ref, m_ref, v_ref, d_ref, mo_ref, vo_ref):
        gv = g_ref[...]
        mn = B1 * m_ref[...] + (1.0 - B1) * gv
        vn = B2 * v_ref[...] + (1.0 - B2) * (gv * gv)
        m_hat = mn / (1.0 - B1 ** STEP)
        v_hat = vn / (1.0 - B2 ** STEP)
        d_ref[...] = -LR * (m_hat / (jnp.sqrt(v_hat) + ADAM_EPS) + WD * w_ref[...])
        mo_ref[...] = mn
        vo_ref[...] = vn

    spec = pl.BlockSpec((tr,) + view[1:], lambda i: (i,) + (0,) * (len(view) - 1))
    out = jax.ShapeDtypeStruct(view, F32)
    res = pl.pallas_call(
        body, name=name, out_shape=(out,) * 3, grid=(rows // tr,), in_specs=[spec] * 4, out_specs=(spec,) * 3,
        compiler_params=_params("parallel"))(*[t.reshape(view) for t in (w, g, m, v)])
    return tuple(r.reshape(shape) for r in res)


LANE_MAJOR = ("w_uq", "w_uk", "w_uv", "w_gate", "w_up")


def _lane_major(name, a):
    if name == "w_in":
        return a.transpose(2, 0, 1)
    if name in LANE_MAJOR:
        return a.transpose(0, 2, 1)
    return a


def _from_lane_major(name, a):
    if name == "w_in":
        return a.transpose(1, 2, 0)
    return _lane_major(name, a)


ANY = pl.BlockSpec(memory_space=pl.ANY)


class _GatherRide:
    def __init__(self, arrays):
        n = len(arrays)
        self.arrays = list(arrays)
        self.out_shape = [jax.ShapeDtypeStruct((N_DEV,) + a.shape, a.dtype) for a in arrays]
        self.scratch = [pltpu.SemaphoreType.DMA((n, 7)), pltpu.SemaphoreType.DMA((n, 7)), pltpu.SemaphoreType.DMA((n,))]

    def _copies(self, ins, outs, sems):
        send_sems, recv_sems, local_sems = sems
        n = len(self.arrays)
        x, y, c = lax.axis_index("x"), lax.axis_index("y"), lax.axis_index("c")
        me, sibling = (x, y, c), (x, y, 1 - c)
        chips = [(1 - x, y), (x, 1 - y), (1 - x, 1 - y)]

        def slot(a, px, py, pc):
            return outs[a].at[4 * px + 2 * py + pc]

        def copy(a, k, block, to, src=None):
            return pltpu.make_async_remote_copy(
                src_ref=slot(a, *block) if src is None else src, dst_ref=slot(a, *block), send_sem=send_sems.at[a, k],
                recv_sem=recv_sems.at[a, k], device_id=to, device_id_type=MESH)

        mine = [pltpu.make_async_copy(ins[a], slot(a, *me), local_sems.at[a]) for a in range(n)]
        first = []
        for a in range(n):
            first.append(copy(a, 0, me, sibling, src=ins[a]))
            first += [copy(a, 1 + j, me, (*chip, c), src=ins[a]) for j, chip in enumerate(chips)]
        return n, me, sibling, chips, c, copy, mine, first

    def start(self, ins, outs, sems):
        _, _, _, _, _, _, mine, first = self._copies(ins, outs, sems)
        for cp in mine + first:
            cp.start()

    def finish(self, ins, outs, sems):
        n, me, sibling, chips, c, copy, mine, first = self._copies(ins, outs, sems)
        passed = []
        for j, chip in enumerate(chips):
            for a in range(n):
                copy(a, 1 + j, (*chip, c), me).wait_recv()
                passed.append(copy(a, 4 + j, (*chip, c), sibling))
                passed[-1].start()
        for a in range(n):
            copy(a, 0, sibling, me).wait_recv()
            for j, chip in enumerate(chips):
                copy(a, 4 + j, (*chip, 1 - c), me).wait_recv()
        for cp in first + passed:
            cp.wait_send()
        for cp in mine:
            cp.wait()


class _ReduceRide:
    def __init__(self, arrays):
        n = len(arrays)
        self.arrays = list(arrays)
        self.out_shape = [jax.ShapeDtypeStruct(a.shape, a.dtype) for a in arrays]
        self.scratch = [pltpu.SemaphoreType.DMA((n, 7)), pltpu.SemaphoreType.DMA((n, 7)), pltpu.SemaphoreType.DMA((n,))]

    def _copies(self, ins, outs, sems):
        send_sems, recv_sems, local_sems = sems
        n = len(self.arrays)
        x, y, c = lax.axis_index("x"), lax.axis_index("y"), lax.axis_index("c")
        mine = [pltpu.make_async_copy(ins[a].at[4 * x + 2 * y + c], outs[a].at[0], local_sems.at[a]) for a in range(n)]
        copies = []
        for a in range(n):
            for k in range(1, N_DEV):
                px = 1 - x if k & 4 else x
                py = 1 - y if k & 2 else y
                pc = 1 - c if k & 1 else c
                copies.append(pltpu.make_async_remote_copy(
                    src_ref=ins[a].at[4 * px + 2 * py + pc], dst_ref=outs[a].at[k], send_sem=send_sems.at[a, k - 1],
                    recv_sem=recv_sems.at[a, k - 1], device_id=(px, py, pc), device_id_type=MESH))
        return mine, copies

    def start(self, ins, outs, sems):
        mine, copies = self._copies(ins, outs, sems)
        for cp in mine + copies:
            cp.start()

    def finish(self, ins, outs, sems):
        mine, copies = self._copies(ins, outs, sems)
        for cp in copies + mine:
            cp.wait()


def _run_ride(ride, name):
    n = len(ride.arrays)

    def body(*refs):
        ins, outs, sems = refs[:n], refs[n:2 * n], refs[2 * n:]
        ride.start(ins, outs, sems)
        ride.finish(ins, outs, sems)

    return pl.pallas_call(body, name=name, out_shape=ride.out_shape, in_specs=[ANY] * n, out_specs=[ANY] * n,
                          scratch_shapes=ride.scratch)(*ride.arrays)


def _all_gather(arrays, name):
    return _run_ride(_GatherRide(arrays), name)


def _swap_with_sibling(arrays, name):
    n = len(arrays)

    def body(*refs):
        ins, outs = refs[:n], refs[n:2 * n]
        send_sems, recv_sems = refs[2 * n:]
        x, y, c = lax.axis_index("x"), lax.axis_index("y"), lax.axis_index("c")
        copies = [pltpu.make_async_remote_copy(
            src_ref=ins[a].at[1 - c], dst_ref=outs[a], send_sem=send_sems.at[a], recv_sem=recv_sems.at[a],
            device_id=(x, y, 1 - c), device_id_type=MESH) for a in range(n)]
        for cp in copies:
            cp.start()
        for cp in copies:
            cp.wait()

    return pl.pallas_call(
        body, name=name, out_shape=[jax.ShapeDtypeStruct(a.shape[1:], a.dtype) for a in arrays],
        in_specs=[ANY] * n, out_specs=[ANY] * n,
        scratch_shapes=[pltpu.SemaphoreType.DMA((n,)), pltpu.SemaphoreType.DMA((n,))])(*arrays)


class _ChipExchangeRide:
    def __init__(self, arrays):
        n = len(arrays)
        self.arrays = list(arrays)
        self.out_shape = [jax.ShapeDtypeStruct(a.shape, a.dtype) for a in arrays]
        self.scratch = [pltpu.SemaphoreType.DMA((n, 3)), pltpu.SemaphoreType.DMA((n, 3)), pltpu.SemaphoreType.DMA((n,))]

    def _copies(self, ins, outs, sems):
        send_sems, recv_sems, local_sems = sems
        n = len(self.arrays)
        x, y, c = lax.axis_index("x"), lax.axis_index("y"), lax.axis_index("c")
        partners = [(x, 1 - y), (1 - x, y), (1 - x, 1 - y)]
        mine = [pltpu.make_async_copy(ins[a].at[2 * x + y], outs[a].at[0], local_sems.at[a]) for a in range(n)]
        copies = [pltpu.make_async_remote_copy(
            src_ref=ins[a].at[2 * px + py], dst_ref=outs[a].at[1 + k], send_sem=send_sems.at[a, k],
            recv_sem=recv_sems.at[a, k], device_id=(px, py, c), device_id_type=MESH)
            for a in range(n) for k, (px, py) in enumerate(partners)]
        return mine, copies

    def start(self, ins, outs, sems):
        mine, copies = self._copies(ins, outs, sems)
        for cp in mine + copies:
            cp.start()

    def finish(self, ins, outs, sems):
        mine, copies = self._copies(ins, outs, sems)
        for cp in copies + mine:
            cp.wait()


class _Combo:
    def __init__(self, rides):
        self.rides = rides
        self.arrays = [a for r in rides for a in r.arrays]
        self.out_shape = [o for r in rides for o in r.out_shape]
        self.scratch = [sc for r in rides for sc in r.scratch]

    def _parts(self, ins, outs, sems):
        at_a = at_s = 0
        for r in self.rides:
            na, ns = len(r.arrays), len(r.scratch)
            yield r, ins[at_a:at_a + na], outs[at_a:at_a + na], sems[at_s:at_s + ns]
            at_a, at_s = at_a + na, at_s + ns

    def start(self, ins, outs, sems):
        for r, i, o, sm in self._parts(ins, outs, sems):
            r.start(i, o, sm)

    def finish(self, ins, outs, sems):
        for r, i, o, sm in reversed(list(self._parts(ins, outs, sems))):
            r.finish(i, o, sm)


def _as_rows(a, lead):
    return a.reshape(a.shape[:lead] + (math.prod(a.shape[lead:-1]), a.shape[-1]))


def _add_pairs(a, b, name):
    a2, b2 = _as_rows(a, 0), _as_rows(b, 0)
    rows, cols = a2.shape
    tr = _row_tile(rows, cols * 4)

    def body(a_ref, b_ref, o_ref):
        o_ref[...] = (a_ref[...].astype(F32) + b_ref[...].astype(F32)).astype(o_ref.dtype)

    spec = _rows(tr, cols)
    out = pl.pallas_call(body, name=name, out_shape=jax.ShapeDtypeStruct(a2.shape, a.dtype), grid=(rows // tr,),
                         in_specs=[spec, spec], out_specs=spec, compiler_params=_params("parallel"))(a2, b2)
    return out.reshape(a.shape)


def _sum_blocks(a, name):
    a3 = _as_rows(a, 1)
    n, rows, cols = a3.shape
    tr = _row_tile(rows, n * cols * 4)

    def body(a_ref, o_ref):
        tot = a_ref[0].astype(F32)
        for k in range(1, n):
            tot = tot + a_ref[k].astype(F32)
        o_ref[...] = tot

    out = pl.pallas_call(body, name=name, out_shape=jax.ShapeDtypeStruct((rows, cols), F32), grid=(rows // tr,),
                         in_specs=[pl.BlockSpec((n, tr, cols), lambda j: (0, j, 0))], out_specs=_rows(tr, cols),
                         compiler_params=_params("parallel"))(a3)
    return out.reshape(a.shape[1:])


def _sum_layers(blocks, name):
    arrs = [_as_rows(a, 1) for a in blocks]
    rows, cols = arrs[0].shape[1:]
    tr = _row_tile(rows, max(a.shape[0] for a in arrs) * cols * 4, budget=8 << 20)
    nj = rows // tr

    def body(*refs):
        o_ref = refs[-1]
        for k, a_ref in enumerate(refs[:-1]):
            @pl.when(pl.program_id(0) == k)
            def _(a_ref=a_ref, n=arrs[k].shape[0]):
                tot = a_ref[0].astype(F32)
                for b in range(1, n):
                    tot = tot + a_ref[b].astype(F32)
                o_ref[0] = tot

    in_specs = [pl.BlockSpec((a.shape[0], tr, cols),
                             lambda l, j, k=k: (0, jnp.where(l == k, j, jnp.where(l < k, 0, nj - 1)), 0))
                for k, a in enumerate(arrs)]
    out = pl.pallas_call(body, name=name, out_shape=jax.ShapeDtypeStruct((len(arrs), rows, cols), F32),
                         grid=(len(arrs), nj), in_specs=in_specs,
                         out_specs=pl.BlockSpec((1, tr, cols), lambda l, j: (l, j, 0)),
                         compiler_params=_params("arbitrary", "arbitrary"))(*arrs)
    return out.reshape((len(arrs),) + blocks[0].shape[1:])


MIX_GROUPS = ("w_in", "w_uq", "w_uk", "w_uv", "w_attn_o", "w_conv_o", "w_pool_o", "w_mix_o")
FFN_GROUPS = ("w_gate", "w_up", "w_down")
MIX_EARLY = ("w_attn_o", "w_conv_o", "w_pool_o", "w_mix_o")
MIX_LATE = ("w_in", "w_uq", "w_uk", "w_uv")


def _pad_axis(a, axis, size):
    pad = [(0, 0)] * a.ndim
    pad[axis] = (0, size - a.shape[axis])
    return jnp.pad(a, pad)


def _local_groups(sh, l):
    out = {n: sh[n][l] for n in BIG}
    for n in ("w_uq", "w_uk", "w_uv"):
        out[n] = _pad_axis(out[n], -1, HEAD_PAD)
    for n in ("w_gate", "w_up"):
        out[n] = _pad_axis(out[n], -1, FF_SHARD_PAD)
    out["w_down"] = _pad_axis(out["w_down"], 0, FF_SHARD_PAD)
    return {n: v.astype(BF16) for n, v in out.items()}


def _arrange_w_in(blocks):
    parts, pos = [], 0
    for ref_lo, ref_hi, at in sorted(W_IN_PIECES, key=lambda p: p[2]):
        if at > pos:
            parts.append(jnp.zeros((blocks.shape[1], at - pos), blocks.dtype))
        for d in range(N_DEV):
            lo, hi = max(ref_lo, d * W_IN_SHARD), min(ref_hi, (d + 1) * W_IN_SHARD)
            if lo < hi:
                parts.append(blocks[d][:, lo - d * W_IN_SHARD:hi - d * W_IN_SHARD])
        pos = at + ref_hi - ref_lo
    if pos < Z_W:
        parts.append(jnp.zeros((blocks.shape[1], Z_W - pos), blocks.dtype))
    return jnp.concatenate(parts, axis=1)


def _w_in_shard(g, d):
    parts = []
    for ref_lo, ref_hi, at in W_IN_PIECES:
        lo, hi = max(ref_lo, d * W_IN_SHARD), min(ref_hi, (d + 1) * W_IN_SHARD)
        if lo < hi:
            parts.append(g[:, at + lo - ref_lo:at + hi - ref_lo])
    return jnp.concatenate(parts, axis=1)


def _mixer_weights(gat):
    w = {n: v for n, v in gat.items() if n != "w_in"}
    attn_o = gat["w_attn_o"].reshape(N_DEV, N_HEADS, V_HEAD, LANES)
    w["w_attn_o"] = _pad_axis(attn_o, 2, HEAD_PAD).reshape(N_DEV, N_HEADS * HEAD_PAD, LANES)
    w["w_mix_o"] = gat["w_mix_o"].reshape(D_MODEL, D_MODEL)
    return w


def _ffn_weights(gat):
    return {"w_gate": gat["w_gate"], "w_up": gat["w_up"], "w_down": gat["w_down"].reshape(D_FF_PAD, D_MODEL)}


def _mixer_grad_groups(gb):
    g = dict(gb)
    if "w_in" in gb:
        g["w_in"] = jnp.stack([_w_in_shard(gb["w_in"], d) for d in range(N_DEV)])
    if "w_attn_o" in gb:
        attn_o = gb["w_attn_o"].reshape(N_DEV, N_HEADS, HEAD_PAD, LANES)[:, :, :V_HEAD]
        g["w_attn_o"] = attn_o.reshape(N_DEV, N_HEADS * V_HEAD, LANES)
    if "w_mix_o" in gb:
        g["w_mix_o"] = gb["w_mix_o"].reshape(N_DEV, D_MODEL // N_DEV, D_MODEL)
    return g


def _ffn_grad_groups(gb):
    return {"w_gate": gb["w_gate"], "w_up": gb["w_up"], "w_down": gb["w_down"].reshape(N_DEV, FF_SHARD_PAD, D_MODEL)}


def _grads_from_groups(tot):
    g = dict(tot)
    g["w_uq"] = tot["w_uq"][..., :QK_NOPE + QK_ROPE]
    g["w_uk"], g["w_uv"] = tot["w_uk"][..., :QK_NOPE], tot["w_uv"][..., :V_HEAD]
    g["w_gate"], g["w_up"] = tot["w_gate"][..., :FF_SHARD], tot["w_up"][..., :FF_SHARD]
    g["w_down"] = tot["w_down"][..., :FF_SHARD, :]
    return g


SMALL_GROUPS = (
    (D_MODEL, ("mix_norm_pre", "mix_norm_post", "ffn_norm_pre", "ffn_norm_post")),
    (CONV_C, ("conv_w", "conv_b", "conv_ln_g", "conv_ln_b", "pool_scale")),
    (Q_RANK, ("q_norm",)), (KV_RANK, ("kv_norm",)), (POOL_GD, ("pool_w",)),
)


def _small_rows(name):
    return {"conv_w": CONV_HALO, "pool_w": POOL_G * POOL_GD}.get(name, SUBLANES)


def _small_groups(small):
    out = []
    for width, names in SMALL_GROUPS:
        parts = []
        for l in range(DEPTH):
            for n in names:
                part = small[l][n].reshape(-1, width)
                parts.append(_pad_axis(part, 0, _small_rows(n)))
        out.append(jnp.concatenate(parts, axis=0))
    return out


def _small_from_groups(groups):
    shapes = {"conv_w": (CONV_W, CONV_C), "pool_w": (POOL_G, POOL_GD, POOL_GD)}
    out = {}
    for (width, names), g in zip(SMALL_GROUPS, groups):
        row = 0
        for l in range(DEPTH):
            for n in names:
                rows = _small_rows(n)
                real = {"conv_w": CONV_W, "pool_w": POOL_G * POOL_GD}.get(n, 1)
                out.setdefault(n, []).append(g[row:row + real].reshape(shapes.get(n, (width,))))
                row += rows
    return {n: jnp.stack(v) for n, v in out.items()}


def _mixer_fwd(x, h, tables, sm, plan, l):
    nm = lambda n: f"{n}_l{l}"
    if h is None:
        h = _rms_fwd(x, (D_MODEL, 0), sm["mix_norm_pre"], BF16, nm("mix_pre_norm"))
    w_in, ride = plan.w_in(l), plan.in_proj_ride(l)
    if ride is None:
        z = _matmul(h, w_in, "nn", BF16, nm("in_proj"))
    else:
        z, rode = _matmul(h, w_in, "nn", BF16, nm("in_proj"), ride=ride)
        plan.in_proj_done(l, rode)
    w = dict(plan.mixer_weights(l), w_in=w_in)
    cq, ckv, q, k, v = _qkv_up_fwd(z, sm["q_norm"], sm["kv_norm"], w["w_uq"], w["w_uk"], w["w_uv"], tables, nm("qkv_up"))
    (o, lse), rode = _flash_fwd(q, k, v, nm("flash_fwd"), plan.fwd_ride(l))
    plan.fwd_done(l, rode)
    hc, co = _conv_fwd(z, sm["conv_w"], sm["conv_b"], sm["conv_ln_g"], sm["conv_ln_b"], nm("conv_fwd"))
    pm = _pool_fwd(z, sm["pool_w"], sm["pool_scale"], nm("pool_fwd"))
    ys, merged = _branches_merge_fwd(z, (o, hc, pm), (w["w_attn_o"], w["w_conv_o"], w["w_pool_o"]), nm("branches_merge"))
    mo = _matmul(merged, w["w_mix_o"], "nn", F32, nm("mix_out"))
    x_mid, h2 = _rms_fwd(mo, (D_MODEL, 0), sm["mix_norm_post"], F32, nm("mix_post_norm"), res=x, then=sm["ffn_norm_pre"])
    saved = dict(x=x, h=h, z=z, cq=cq, ckv=ckv, q=q, k=k, v=v, o=o, lse=lse, hc=hc, co=co, pm=pm, ys=ys, merged=merged,
                 mo=mo)
    return x_mid, h2, saved, w


def _ffn_fwd(x_mid, h2, w, sm, tag, next_gain):
    nm = lambda n: f"{n}_{tag}"
    hg, hu, act = _ffn_up_fwd(h2, w["w_gate"], w["w_up"], nm("ffn_up_fwd"))
    fo = _matmul(act, w["w_down"], "nn", F32, nm("ffn_down"))
    out = _rms_fwd(fo, (D_MODEL, 0), sm["ffn_norm_post"], F32, nm("ffn_post_norm"), res=x_mid, then=next_gain)
    out, h_next = out if next_gain is not None else (out, None)
    saved = dict(x_mid=x_mid, h2=h2, hg=hg, hu=hu, act=act, fo=fo)
    return out, h_next, saved


def _ffn_bwd(dout, sv, w, sm, tag):
    nm = lambda n: f"{n}_{tag}"
    gb, gs = {}, {}
    dfo, gs["ffn_norm_post"] = _rms_bwd(sv["fo"], (D_MODEL, 0), sm["ffn_norm_post"], dout, BF16, nm("ffn_post_norm_bwd"))
    gb["w_down"] = _matmul(sv["act"], dfo, "tn", BF16, nm("ffn_down_dw"))
    dhg, dhu = _ffn_down_bwd(dfo, w["w_down"], sv["hg"], sv["hu"], nm("ffn_down_bwd"))
    dh2 = _ffn_up_dx(dhg, dhu, w["w_gate"], w["w_up"], nm("ffn_up_dx"))
    gb["w_gate"] = _matmul(sv["h2"], dhg, "tn", BF16, nm("ffn_gate_dw"), blocked=True)
    gb["w_up"] = _matmul(sv["h2"], dhu, "tn", BF16, nm("ffn_up_dw"), blocked=True)
    dmid, gs["ffn_norm_pre"] = _rms_bwd(sv["x_mid"], (D_MODEL, 0), sm["ffn_norm_pre"], dh2, F32, nm("ffn_pre_norm_bwd"),
                                        add=dout)
    return dmid, gb, gs


def _mixer_bwd(dmid, sv, tables, w, sm, plan, l, pack_small):
    nm = lambda n: f"{n}_l{l}"
    gb, gs = {}, {}
    dmo, gs["mix_norm_post"] = _rms_bwd(sv["mo"], (D_MODEL, 0), sm["mix_norm_post"], dmid, BF16, nm("mix_post_norm_bwd"))
    gb["w_mix_o"] = _matmul(sv["merged"], dmo, "tn", BF16, nm("mix_out_dw"))
    dya, dyc, dyp, dz = _merge_bwd(sv["z"], sv["ys"], dmo, w["w_mix_o"], nm("merge_bwd"))
    dpm = _matmul(dyp, w["w_pool_o"], "nt", F32, nm("pool_out_dx"))
    gb["w_pool_o"] = _matmul(sv["pm"], dyp, "tn", BF16, nm("pool_out_dw"), blocked=True)
    dz, gs["pool_w"], gs["pool_scale"] = _pool_bwd(dpm, sv["z"], sm["pool_w"], sm["pool_scale"], dz, nm("pool_bwd"))
    gb["w_conv_o"] = _matmul(sv["hc"], dyc, "tn", BF16, nm("conv_out_dw"), blocked=True)
    dco, gs["conv_ln_g"], gs["conv_ln_b"], gs["conv_b"] = _conv_bwd_norm(dyc, w["w_conv_o"], sv["co"], sm["conv_ln_g"],
                                                                        sm["conv_ln_b"], nm("conv_bwd_norm"))
    dz, gs["conv_w"] = _conv_bwd_taps(dco, sv["z"], sm["conv_w"], dz, nm("conv_bwd_taps"))
    gb["w_attn_o"] = _matmul(sv["o"], dya, "tn", BF16, nm("attn_out_dw"), blocked=True)
    delta, dob = _attn_out_bwd(dya, w["w_attn_o"], sv["o"], nm("attn_out_bwd"))
    (dq, dk, dv), rode = _flash_bwd(sv["q"], sv["k"], sv["v"], dob, sv["lse"], delta, nm("flash_bwd"),
                                  plan.bwd_ride(l, gb))
    plan.bwd_done(l, rode)
    dqf, dkf, dz, gs["q_norm"], gs["kv_norm"] = _qkv_up_bwd(
        dq, dk, dv, sv["z"], w["w_uq"], w["w_uk"], w["w_uv"], tables, sm["q_norm"], sm["kv_norm"], dz, nm("qkv_up_bwd"))
    gb["w_uq"] = _matmul(sv["cq"], dqf, "tn", BF16, nm("q_up_dw"), blocked=True)
    gb["w_uk"] = _matmul(sv["ckv"], dkf, "tn", BF16, nm("k_up_dw"), blocked=True)
    gb["w_uv"] = _matmul(sv["ckv"], dv, "tn", BF16, nm("v_up_dw"), blocked=True)
    gb["w_in"] = _matmul(sv["h"], dz, "tn", BF16, nm("in_proj_dw"))
    plan.add_grads(l, "mix", gb)
    ride, small_gathered = plan.tail_ride(l, pack_small(gs)), []
    if ride is None:
        dh = _matmul(dz, w["w_in"], "nt", F32, nm("in_proj_dx"))
    else:
        dh, rode = _matmul(dz, w["w_in"], "nt", F32, nm("in_proj_dx"), ride=ride)
        small_gathered = plan.tail_done(l, rode)
    dx, gs["mix_norm_pre"] = _rms_bwd(sv["x"], (D_MODEL, 0), sm["mix_norm_pre"], dh, F32, nm("mix_pre_norm_bwd"), add=dmid)
    return dx, gs, small_gathered


def _part_groups(part):
    return {"mix": MIX_GROUPS, "ffn": FFN_GROUPS, "early": MIX_EARLY, "late": MIX_LATE}[part]


class _Plan:
    def __init__(self, shards, conv_w):
        self.local = [_local_groups(shards, l) for l in range(DEPTH)]
        self.conv_w = conv_w
        self.gat, self.send, self.recv = {}, {}, {}

    @staticmethod
    def _riders(l):
        return [(l, "ffn")] + ([(l + 1, "mix")] if l + 1 < DEPTH else [])

    @staticmethod
    def _grad_riders(l):
        return [(l, "ffn"), (l, "early")] + ([(l + 1, "late")] if l + 1 < DEPTH else [])

    def gather_first(self):
        w_in, conv_w = _all_gather([self.local[0]["w_in"], self.conv_w], "gather_w_in_l0")
        self.gat[(0, "mix")] = {"w_in": w_in}
        return conv_w

    def w_in(self, l):
        return _arrange_w_in(self.gat[(l, "mix")]["w_in"])

    def in_proj_ride(self, l):
        return _GatherRide([self.local[0][g] for g in MIX_GROUPS[1:]]) if l == 0 else None

    def in_proj_done(self, l, outs):
        self.gat[(l, "mix")].update(zip(MIX_GROUPS[1:], outs))

    def fwd_ride(self, l):
        return _GatherRide([self.local[ll][g] for ll, part in self._riders(l) for g in _part_groups(part)])

    def fwd_done(self, l, outs):
        outs = list(outs)
        for ll, part in self._riders(l):
            self.gat[(ll, part)] = {g: outs.pop(0) for g in _part_groups(part)}

    def mixer_weights(self, l):
        return _mixer_weights(self.gat[(l, "mix")])

    def ffn_weights(self, l):
        return _ffn_weights(self.gat[(l, "ffn")])

    def add_grads(self, l, part, gb):
        if part == "ffn":
            self.send[(l, "ffn")] = _ffn_grad_groups(gb)
        else:
            self.send.setdefault((l, "late"), {}).update(_mixer_grad_groups({g: gb[g] for g in MIX_LATE if g in gb}))

    def bwd_ride(self, l, gb_early):
        self.send[(l, "early")] = _mixer_grad_groups({g: gb_early[g] for g in MIX_EARLY})
        return _ReduceRide([self.send[(ll, part)][g] for ll, part in self._grad_riders(l) for g in _part_groups(part)])

    def bwd_done(self, l, outs):
        outs = list(outs)
        for ll, part in self._grad_riders(l):
            self.recv[(ll, part)] = {g: outs.pop(0) for g in _part_groups(part)}

    def tail_ride(self, l, small_groups):
        if l > 0:
            return None
        send = [self.send[(0, "late")][g] for g in MIX_LATE]
        by_core = [a.reshape((4, 2) + a.shape[1:]).transpose((1, 0) + tuple(range(2, a.ndim + 1))) for a in send]
        core = lax.axis_index("c")
        own = [lax.dynamic_index_in_dim(a, core, axis=0, keepdims=False) for a in by_core]
        got = _swap_with_sibling(by_core, "reduce_d2d")
        pairs = [_add_pairs(a, b, f"reduce_pair_add_{g}") for g, a, b in zip(MIX_LATE, own, got)]
        return _Combo([_ChipExchangeRide(pairs), _GatherRide(small_groups)])

    def tail_done(self, l, outs):
        self.recv[(l, "late")] = dict(zip(MIX_LATE, outs[:len(MIX_LATE)]))
        return outs[len(MIX_LATE):]

    def finish(self):
        per_layer = [{g: a for part in ("early", "late", "ffn") for g, a in self.recv[(l, part)].items()}
                     for l in range(DEPTH)]
        return _grads_from_groups({g: _sum_layers([per_layer[l][g] for l in range(DEPTH)], f"reduce_sum_{g}")
                                   for g in BIG})


def _local_step(x, positions, target, smalls, plan):
    tables = _rope_tables(positions)
    saved = []
    h, h_norm = x, None
    for l in range(DEPTH):
        h, h2, svm, wm = _mixer_fwd(h, h_norm, tables, smalls[l], plan, l)
        wf = plan.ffn_weights(l)
        next_gain = smalls[l + 1]["mix_norm_pre"] if l + 1 < DEPTH else None
        h, h_norm, svf = _ffn_fwd(h, h2, wf, smalls[l], f"l{l}", next_gain)
        saved.append((svm, svf, wm, wf))
    dy, sq = _loss_grad(h, target, "loss_grad")
    small = [None] * DEPTH
    for l in reversed(range(DEPTH)):
        svm, svf, wm, wf = saved[l]
        dmid, gbf, gsf = _ffn_bwd(dy, svf, wf, smalls[l], f"l{l}")
        plan.add_grads(l, "ffn", gbf)

        def pack_small(gs, l=l, gsf=gsf):
            if l > 0:
                return None
            return _small_groups([{**gsf, **gs, "mix_norm_pre": jnp.zeros((D_MODEL,), F32)}] + small[1:])

        dy, gsm, small_gathered = _mixer_bwd(dmid, svm, tables, wm, smalls[l], plan, l, pack_small)
        small[l] = {**gsf, **gsm}
    return sq, dy, small, small_gathered


def kernel(x, positions, mix_norm_pre, w_in, q_norm, w_uq, kv_norm, w_uk, w_uv, w_attn_o, conv_w, conv_b, conv_ln_g, conv_ln_b, w_conv_o, pool_w, pool_scale, w_pool_o, w_mix_o, mix_norm_post, ffn_norm_pre, w_gate, w_up, w_down, ffn_norm_post, loss_target, m_mix_norm_pre, m_w_in, m_q_norm, m_w_uq, m_kv_norm, m_w_uk, m_w_uv, m_w_attn_o, m_conv_w, m_conv_b, m_conv_ln_g, m_conv_ln_b, m_w_conv_o, m_pool_w, m_pool_scale, m_w_pool_o, m_w_mix_o, m_mix_norm_post, m_ffn_norm_pre, m_w_gate, m_w_up, m_w_down, m_ffn_norm_post, v_mix_norm_pre, v_w_in, v_q_norm, v_w_uq, v_kv_norm, v_w_uk, v_w_uv, v_w_attn_o, v_conv_w, v_conv_b, v_conv_ln_g, v_conv_ln_b, v_w_conv_o, v_pool_w, v_pool_scale, v_w_pool_o, v_w_mix_o, v_mix_norm_post, v_ffn_norm_pre, v_w_gate, v_w_up, v_w_down, v_ffn_norm_post):
    given = dict(locals())
    dev = 4 * lax.axis_index("x") + 2 * lax.axis_index("y") + lax.axis_index("c")

    plan = _Plan({n: given[n] for n in BIG}, conv_w)
    cw = CONV_C // N_DEV
    conv_w_full = plan.gather_first().transpose(1, 2, 0, 3).reshape(DEPTH, CONV_W, CONV_C)
    smalls = []
    for l in range(DEPTH):
        sm = {n: given[n][l] for n in SMALL if n != "conv_w"}
        sm["conv_w"] = _pad_axis(conv_w_full[l], 0, CONV_HALO)
        smalls.append(sm)

    sq, grad_x, small, small_groups = _local_step(x[0], positions[0], loss_target[0], smalls, plan)
    loss = lax.psum(0.5 / D_MODEL * jnp.sum(sq), ("x", "y", "c"))
    views = {n: lax.optimization_barrier(_lane_major(n, g)) for n, g in plan.finish().items()}
    grads = {n: _from_lane_major(n, views[n]) for n in BIG}

    small_sum = _small_from_groups([_sum_blocks(g, f"sum_small_grads_{i}") for i, g in enumerate(small_groups)])
    last = _pad_axis(small[0]["mix_norm_pre"].reshape(1, D_MODEL), 0, SUBLANES)
    last_sum = _sum_blocks(_all_gather([last], "gather_last_norm_grad")[0], "sum_last_norm_grad")[0]
    small_sum["mix_norm_pre"] = small_sum["mix_norm_pre"].at[0].set(last_sum)
    for n in SMALL:
        grads[n] = small_sum[n]
    grads["conv_w"] = lax.dynamic_slice_in_dim(small_sum["conv_w"], dev * cw, cw, axis=2)

    delta, new_m, new_v = {}, {}, {}
    for n in WEIGHTS:
        g_view = views[n] if n in views else grads[n]
        w_view, m_view, v_view = [_lane_major(n, given[k]) for k in (n, "m_" + n, "v_" + n)]
        res = _adamw(w_view, g_view, m_view, v_view, f"adamw_{n}")
        delta[n], new_m[n], new_v[n] = [_from_lane_major(n, r) for r in res]
    return (loss, grad_x[None], *[grads[n] for n in WEIGHTS], *[delta[n] for n in WEIGHTS],
            *[new_m[n] for n in WEIGHTS], *[new_v[n] for n in WEIGHTS])
```
